```python
import math
import jax, jax.numpy as jnp
from jax import lax
import numpy as np

D_MODEL = 1024
BATCH = 8
SEQ = 4096
DEPTH = 2

CONV_WIDTH = 3
A_WIDTH = D_MODEL // 2
A_HEADS = 8
B_WIDTH = D_MODEL // 2
POOL_WINDOWS = (2, 4, 8, 16)
N_POOL_GROUPS = len(POOL_WINDOWS)
POOL_GROUP = B_WIDTH // N_POOL_GROUPS
EVEN_IN = 3 * A_WIDTH + B_WIDTH
HEAD_DIM = 64
N_HEADS = D_MODEL // HEAD_DIM
DILATED_PAIRS = ((128, 1), (512, 4), (2048, 16))
N_REL_BUCKETS = 32
REL_MAX_DISTANCE = 2048
D_FF = 2816
EPS = 1e-6
MASK_VALUE = -1e30
N_EVEN = (DEPTH + 1) // 2
N_ODD = DEPTH // 2

kernel_name = "hybrid_conv_pool_dilated_attn_trunk"


def rmsnorm(x, g):
    x32 = x.astype(jnp.float32)
    y = x32 * lax.rsqrt(jnp.mean(x32 * x32, axis=-1, keepdims=True) + EPS)
    return (y * g.astype(jnp.float32)).astype(x.dtype)


def causal_dwconv3(x, w):
    S = x.shape[1]
    xp = jnp.pad(x, ((0, 0), (CONV_WIDTH - 1, 0), (0, 0)))
    y = xp[:, 0:S] * w[0]
    for i in range(1, CONV_WIDTH):
        y = y + xp[:, i:i + S] * w[i]
    return y


def causal_window_mean(p, k):
    S = p.shape[1]
    cs = jnp.cumsum(p, axis=1)
    cs_prev = jnp.pad(cs, ((0, 0), (k, 0), (0, 0)))[:, :S]
    cnt = jnp.minimum(jnp.arange(1, S + 1), k).astype(jnp.float32)[None, :, None]
    return (cs - cs_prev) / cnt


def t5_causal_bucket(dist):
    max_exact = N_REL_BUCKETS // 2
    d = jnp.maximum(dist, 1).astype(jnp.float32)
    large = max_exact + (jnp.log(d / max_exact) / math.log(REL_MAX_DISTANCE / max_exact)
                         * (N_REL_BUCKETS - max_exact)).astype(jnp.int32)
    large = jnp.minimum(large, N_REL_BUCKETS - 1)
    return jnp.where(dist < max_exact, dist, large)


def conv_pool_mixer(xn, w_in, conv_w, pool_w, pool_scale, w_out):
    B, S, _ = xn.shape
    proj = xn @ w_in
    h, gate_b, gate_c, pin = jnp.split(proj, [A_WIDTH, 2 * A_WIDTH, 3 * A_WIDTH], axis=-1)
    ya = gate_b * causal_dwconv3(gate_c * h, conv_w)
    p32 = pin.astype(jnp.float32).reshape(B, S, N_POOL_GROUPS, POOL_GROUP)
    pooled = jnp.stack([causal_window_mean(p32[:, :, g], k)
                        for g, k in enumerate(POOL_WINDOWS)], axis=2) - p32
    yb = jnp.einsum('bsgc,gcd->bsgd', pooled, pool_w.astype(jnp.float32))
    yb = (yb.reshape(B, S, B_WIDTH) * pool_scale.astype(jnp.float32)).astype(xn.dtype)
    return jnp.concatenate([ya, yb], axis=-1) @ w_out


def dilated_branch(q, k, v, rel_table, window, dil):
    B, S, H, E = q.shape
    n = window // dil
    QB = n
    L = S // dil
    nb = -(-L // QB)
    Lp = nb * QB

    def residues(t):
        return jnp.swapaxes(t.reshape(B, L, dil, H, E), 1, 2)

    qr = jnp.pad(residues(q), ((0, 0), (0, 0), (0, Lp - L), (0, 0), (0, 0)))
    kp = jnp.pad(residues(k), ((0, 0), (0, 0), (n, Lp - L), (0, 0), (0, 0)))
    vp = jnp.pad(residues(v), ((0, 0), (0, 0), (n, Lp - L), (0, 0), (0, 0)))

    def key_blocks(t):
        prev = t[:, :, :Lp].reshape(B, dil, nb, QB, H, E)
        cur = t[:, :, n:].reshape(B, dil, nb, QB, H, E)
        return jnp.concatenate([prev, cur], axis=3)

    qb = qr.reshape(B, dil, nb, QB, H, E)
    kb = key_blocks(kp)
    vb = key_blocks(vp)

    a = jnp.arange(QB)[:, None]
    c = jnp.arange(2 * QB)[None, :]
    rel = a + n - c
    key_pos = jnp.arange(nb)[:, None] * QB - n + jnp.arange(2 * QB)[None, :]
    valid = ((rel >= 0) & (rel <= n))[None] & (key_pos >= 0)[:, None, :]
    bias = rel_table.astype(jnp.float32)[t5_causal_bucket(jnp.clip(rel, 0, n) * dil)]
    bias = jnp.transpose(bias, (2, 0, 1))

    s = jnp.einsum('bdnqhe,bdnkhe->bdnhqk', qb, kb) * (HEAD_DIM ** -0.5) + bias
    s = jnp.where(valid[None, None, :, None], s, MASK_VALUE)
    m = jnp.max(s, axis=-1)
    p = jnp.exp(s - m[..., None])
    den = jnp.sum(p, axis=-1)
    m = jnp.swapaxes(m, 3, 4)
    den = jnp.swapaxes(den, 3, 4)
    o = jnp.einsum('bdnhqk,bdnkhe->bdnqhe', p, vb) / den[..., None]

    def from_blocks(t):
        rest = t.shape[5:]
        t = t.reshape((B, dil, Lp, H) + rest)[:, :, :L]
        return jnp.swapaxes(t, 1, 2).reshape((B, S, H) + rest)

    return from_blocks(o), from_blocks(m), from_blocks(den)


def dilated_attention(xn, w_qkv, g_q, g_k, w_o, rel_table):
    B, S, D = xn.shape
    qkv = (xn @ w_qkv).reshape(B, S, 3, N_HEADS, HEAD_DIM).astype(jnp.float32)
    q = rmsnorm(qkv[:, :, 0], g_q)
    k = rmsnorm(qkv[:, :, 1], g_k)
    v = qkv[:, :, 2]
    outs, maxes, dens = [], [], []
    for window, dil in DILATED_PAIRS:
        o, m, den = dilated_branch(q, k, v, rel_table, window, dil)
        outs.append(o)
        maxes.append(m)
        dens.append(den)
    outs = jnp.stack(outs)
    maxes = jnp.stack(maxes)
    dens = jnp.stack(dens)
    wts = dens * jnp.exp(maxes - jnp.max(maxes, axis=0, keepdims=True))
    o = jnp.sum(wts[..., None] * outs, axis=0) / jnp.sum(wts, axis=0)[..., None]
    return o.reshape(B, S, D).astype(xn.dtype) @ w_o


def conv_glu_ffn(xn, w_up, conv_w, conv_b, w_down):
    u = causal_dwconv3(xn @ w_up, conv_w) + conv_b
    gate, up = jnp.split(u, 2, axis=-1)
    return (jax.nn.silu(gate) * up) @ w_down


def _fwd_setup_inputs(seed: int = 0) -> dict:
    key = jax.random.key(seed)
    ks = jax.random.split(key, 20)
    f32 = jnp.float32

    def nrm(k, shape, scale):
        return jax.random.normal(k, shape, f32) * scale

    return {
        "x": nrm(ks[0], (BATCH, SEQ, D_MODEL), 1.0),
        "rel_bias": nrm(ks[1], (N_REL_BUCKETS, N_HEADS), 0.5),
        "even_norm": 1.0 + nrm(ks[2], (N_EVEN, D_MODEL), 0.02),
        "even_w_in": nrm(ks[3], (N_EVEN, D_MODEL, EVEN_IN), D_MODEL ** -0.5),
        "even_conv_w": nrm(ks[4], (N_EVEN, CONV_WIDTH, A_WIDTH), CONV_WIDTH ** -0.5),
        "even_pool_w": nrm(ks[5], (N_EVEN, N_POOL_GROUPS, POOL_GROUP, POOL_GROUP), POOL_GROUP ** -0.5),
        "even_pool_scale": 1.0 + nrm(ks[6], (N_EVEN, B_WIDTH), 0.02),
        "even_w_out": nrm(ks[7], (N_EVEN, D_MODEL, D_MODEL), D_MODEL ** -0.5),
        "odd_norm": 1.0 + nrm(ks[8], (N_ODD, D_MODEL), 0.02),
        "odd_w_qkv": nrm(ks[9], (N_ODD, D_MODEL, 3 * D_MODEL), D_MODEL ** -0.5),
        "odd_q_norm": 1.0 + nrm(ks[10], (N_ODD, HEAD_DIM), 0.02),
        "odd_k_norm": 1.0 + nrm(ks[11], (N_ODD, HEAD_DIM), 0.02),
        "odd_w_o": nrm(ks[12], (N_ODD, D_MODEL, D_MODEL), D_MODEL ** -0.5),
        "ffn_norm": 1.0 + nrm(ks[13], (DEPTH, D_MODEL), 0.02),
        "ffn_w_up": nrm(ks[14], (DEPTH, D_MODEL, 2 * D_FF), D_MODEL ** -0.5),
        "ffn_conv_w": nrm(ks[15], (DEPTH, CONV_WIDTH, 2 * D_FF), CONV_WIDTH ** -0.5),
        "ffn_conv_b": nrm(ks[16], (DEPTH, 2 * D_FF), 0.02),
        "ffn_w_down": nrm(ks[17], (DEPTH, D_FF, D_MODEL), D_FF ** -0.5),
    }


def _fwd_reference(x, rel_bias, even_norm, even_w_in, even_conv_w, even_pool_w, even_pool_scale,
              even_w_out, odd_norm, odd_w_qkv, odd_q_norm, odd_k_norm, odd_w_o,
              ffn_norm, ffn_w_up, ffn_conv_w, ffn_conv_b, ffn_w_down):
    for layer in range(DEPTH):
        i = layer // 2
        if layer % 2 == 0:
            x = x + conv_pool_mixer(rmsnorm(x, even_norm[i]), even_w_in[i], even_conv_w[i],
                                    even_pool_w[i], even_pool_scale[i], even_w_out[i])
        else:
            x = x + dilated_attention(rmsnorm(x, odd_norm[i]), odd_w_qkv[i], odd_q_norm[i],
                                      odd_k_norm[i], odd_w_o[i], rel_bias)
        x = x + conv_glu_ffn(rmsnorm(x, ffn_norm[layer]), ffn_w_up[layer], ffn_conv_w[layer],
                             ffn_conv_b[layer], ffn_w_down[layer])
    return x


import jax as _jax
import jax.numpy as _jnp

TWIN_FORMAT = 'train_step'
FWD_PARAMS = ['x', 'rel_bias', 'even_norm', 'even_w_in', 'even_conv_w', 'even_pool_w', 'even_pool_scale', 'even_w_out', 'odd_norm', 'odd_w_qkv', 'odd_q_norm', 'odd_k_norm', 'odd_w_o', 'ffn_norm', 'ffn_w_up', 'ffn_conv_w', 'ffn_conv_b', 'ffn_w_down']
TWIN_WEIGHTS = ['rel_bias', 'even_norm', 'even_w_in', 'even_conv_w', 'even_pool_w', 'even_pool_scale', 'even_w_out', 'odd_norm', 'odd_w_qkv', 'odd_q_norm', 'odd_k_norm', 'odd_w_o', 'ffn_norm', 'ffn_w_up', 'ffn_conv_w', 'ffn_conv_b', 'ffn_w_down']
TWIN_DIFF_INPUT = 'x'
TWIN_INPUTS = ['x', 'rel_bias', 'even_norm', 'even_w_in', 'even_conv_w', 'even_pool_w', 'even_pool_scale', 'even_w_out', 'odd_norm', 'odd_w_qkv', 'odd_q_norm', 'odd_k_norm', 'odd_w_o', 'ffn_norm', 'ffn_w_up', 'ffn_conv_w', 'ffn_conv_b', 'ffn_w_down', 'loss_target', 'm_rel_bias', 'm_even_norm', 'm_even_w_in', 'm_even_conv_w', 'm_even_pool_w', 'm_even_pool_scale', 'm_even_w_out', 'm_odd_norm', 'm_odd_w_qkv', 'm_odd_q_norm', 'm_odd_k_norm', 'm_odd_w_o', 'm_ffn_norm', 'm_ffn_w_up', 'm_ffn_conv_w', 'm_ffn_conv_b', 'm_ffn_w_down', 'v_rel_bias', 'v_even_norm', 'v_even_w_in', 'v_even_conv_w', 'v_even_pool_w', 'v_even_pool_scale', 'v_even_w_out', 'v_odd_norm', 'v_odd_w_qkv', 'v_odd_q_norm', 'v_odd_k_norm', 'v_odd_w_o', 'v_ffn_norm', 'v_ffn_w_up', 'v_ffn_conv_w', 'v_ffn_conv_b', 'v_ffn_w_down']
TWIN_OUTPUTS = ['loss', 'grad_x', 'grad_rel_bias', 'grad_even_norm', 'grad_even_w_in', 'grad_even_conv_w', 'grad_even_pool_w', 'grad_even_pool_scale', 'grad_even_w_out', 'grad_odd_norm', 'grad_odd_w_qkv', 'grad_odd_q_norm', 'grad_odd_k_norm', 'grad_odd_w_o', 'grad_ffn_norm', 'grad_ffn_w_up', 'grad_ffn_conv_w', 'grad_ffn_conv_b', 'grad_ffn_w_down', 'delta_rel_bias', 'delta_even_norm', 'delta_even_w_in', 'delta_even_conv_w', 'delta_even_pool_w', 'delta_even_pool_scale', 'delta_even_w_out', 'delta_odd_norm', 'delta_odd_w_qkv', 'delta_odd_q_norm', 'delta_odd_k_norm', 'delta_odd_w_o', 'delta_ffn_norm', 'delta_ffn_w_up', 'delta_ffn_conv_w', 'delta_ffn_conv_b', 'delta_ffn_w_down', 'new_m_rel_bias', 'new_m_even_norm', 'new_m_even_w_in', 'new_m_even_conv_w', 'new_m_even_pool_w', 'new_m_even_pool_scale', 'new_m_even_w_out', 'new_m_odd_norm', 'new_m_odd_w_qkv', 'new_m_odd_q_norm', 'new_m_odd_k_norm', 'new_m_odd_w_o', 'new_m_ffn_norm', 'new_m_ffn_w_up', 'new_m_ffn_conv_w', 'new_m_ffn_conv_b', 'new_m_ffn_w_down', 'new_v_rel_bias', 'new_v_even_norm', 'new_v_even_w_in', 'new_v_even_conv_w', 'new_v_even_pool_w', 'new_v_even_pool_scale', 'new_v_even_w_out', 'new_v_odd_norm', 'new_v_odd_w_qkv', 'new_v_odd_q_norm', 'new_v_odd_k_norm', 'new_v_odd_w_o', 'new_v_ffn_norm', 'new_v_ffn_w_up', 'new_v_ffn_conv_w', 'new_v_ffn_conv_b', 'new_v_ffn_w_down']
TWIN_LEAF_KINDS = {'loss': 'loss', 'grad_x': 'grad_x', 'grad_rel_bias': 'grad_w', 'grad_even_norm': 'grad_w', 'grad_even_w_in': 'grad_w', 'grad_even_conv_w': 'grad_w', 'grad_even_pool_w': 'grad_w', 'grad_even_pool_scale': 'grad_w', 'grad_even_w_out': 'grad_w', 'grad_odd_norm': 'grad_w', 'grad_odd_w_qkv': 'grad_w', 'grad_odd_q_norm': 'grad_w', 'grad_odd_k_norm': 'grad_w', 'grad_odd_w_o': 'grad_w', 'grad_ffn_norm': 'grad_w', 'grad_ffn_w_up': 'grad_w', 'grad_ffn_conv_w': 'grad_w', 'grad_ffn_conv_b': 'grad_w', 'grad_ffn_w_down': 'grad_w', 'delta_rel_bias': 'delta_w', 'delta_even_norm': 'delta_w', 'delta_even_w_in': 'delta_w', 'delta_even_conv_w': 'delta_w', 'delta_even_pool_w': 'delta_w', 'delta_even_pool_scale': 'delta_w', 'delta_even_w_out': 'delta_w', 'delta_odd_norm': 'delta_w', 'delta_odd_w_qkv': 'delta_w', 'delta_odd_q_norm': 'delta_w', 'delta_odd_k_norm': 'delta_w', 'delta_odd_w_o': 'delta_w', 'delta_ffn_norm': 'delta_w', 'delta_ffn_w_up': 'delta_w', 'delta_ffn_conv_w': 'delta_w', 'delta_ffn_conv_b': 'delta_w', 'delta_ffn_w_down': 'delta_w', 'new_m_rel_bias': 'new_m', 'new_m_even_norm': 'new_m', 'new_m_even_w_in': 'new_m', 'new_m_even_conv_w': 'new_m', 'new_m_even_pool_w': 'new_m', 'new_m_even_pool_scale': 'new_m', 'new_m_even_w_out': 'new_m', 'new_m_odd_norm': 'new_m', 'new_m_odd_w_qkv': 'new_m', 'new_m_odd_q_norm': 'new_m', 'new_m_odd_k_norm': 'new_m', 'new_m_odd_w_o': 'new_m', 'new_m_ffn_norm': 'new_m', 'new_m_ffn_w_up': 'new_m', 'new_m_ffn_conv_w': 'new_m', 'new_m_ffn_conv_b': 'new_m', 'new_m_ffn_w_down': 'new_m', 'new_v_rel_bias': 'new_v', 'new_v_even_norm': 'new_v', 'new_v_even_w_in': 'new_v', 'new_v_even_conv_w': 'new_v', 'new_v_even_pool_w': 'new_v', 'new_v_even_pool_scale': 'new_v', 'new_v_even_w_out': 'new_v', 'new_v_odd_norm': 'new_v', 'new_v_odd_w_qkv': 'new_v', 'new_v_odd_q_norm': 'new_v', 'new_v_odd_k_norm': 'new_v', 'new_v_odd_w_o': 'new_v', 'new_v_ffn_norm': 'new_v', 'new_v_ffn_w_up': 'new_v', 'new_v_ffn_conv_w': 'new_v', 'new_v_ffn_conv_b': 'new_v', 'new_v_ffn_w_down': 'new_v'}


def _forward(args):
    return _fwd_reference(*[args[k] for k in FWD_PARAMS])


def _output_shape():
    def fwd():
        inp = _fwd_setup_inputs(0)
        return _fwd_reference(*[inp[k] for k in FWD_PARAMS])
    out = _jax.eval_shape(fwd)
    return out.shape, out.dtype

N_MICROBATCH = 1
ADAM_LR = 0.001
ADAM_B1 = 0.9
ADAM_B2 = 0.999
ADAM_EPS = 1e-08
ADAM_WD = 0.01
ADAM_STEP = 10
PER_EXAMPLE_BATCH_AXIS = {'x': 0, 'loss_target': 0}
SHARED_INPUTS = []
_WEIGHT_DTYPES = {'rel_bias': _jnp.float32, 'even_norm': _jnp.float32, 'even_w_in': _jnp.float32, 'even_conv_w': _jnp.float32, 'even_pool_w': _jnp.float32, 'even_pool_scale': _jnp.float32, 'even_w_out': _jnp.float32, 'odd_norm': _jnp.float32, 'odd_w_qkv': _jnp.float32, 'odd_q_norm': _jnp.float32, 'odd_k_norm': _jnp.float32, 'odd_w_o': _jnp.float32, 'ffn_norm': _jnp.float32, 'ffn_w_up': _jnp.float32, 'ffn_conv_w': _jnp.float32, 'ffn_conv_b': _jnp.float32, 'ffn_w_down': _jnp.float32}
MOMENT_SCALE = {'rel_bias': 6.013839e-01, 'even_norm': 5.928431e+01, 'even_w_in': 1.236976e+00, 'even_conv_w': 1.834260e+01, 'even_pool_w': 2.471067e+00, 'even_pool_scale': 2.489645e+01, 'even_w_out': 1.302509e+00, 'odd_norm': 5.067106e-01, 'odd_w_qkv': 1.086349e-01, 'odd_q_norm': 5.779998e+00, 'odd_k_norm': 5.785474e+00, 'odd_w_o': 1.237511e-01, 'ffn_norm': 2.605587e+01, 'ffn_w_up': 3.161845e-01, 'ffn_conv_w': 3.648740e+00, 'ffn_conv_b': 3.290017e+00, 'ffn_w_down': 4.340523e-01}


def _to_microbatches(a, axis):
    t = _jnp.moveaxis(a, axis, 0)
    t = t.reshape((N_MICROBATCH, t.shape[0] // N_MICROBATCH) + t.shape[1:])
    return _jnp.moveaxis(t, 1, axis + 1)


def setup_inputs(seed: int = 0) -> dict:
    inp = _fwd_setup_inputs(seed)
    key = _jax.random.fold_in(_jax.random.key(seed), 7919)
    shape, _ = _output_shape()
    out = dict(inp)
    out["loss_target"] = _jax.random.normal(_jax.random.fold_in(key, 0), shape, _jnp.float32)
    for i, name in enumerate(TWIN_WEIGHTS):
        w = inp[name].astype(_jnp.float32)
        if MOMENT_SCALE is None:
            s = _jnp.sqrt(_jnp.mean(_jnp.square(w)) + 1e-30)
        else:
            s = MOMENT_SCALE[name]
        km, kv = _jax.random.split(_jax.random.fold_in(key, i + 1))
        out[name] = w
        out["m_" + name] = s * _jax.random.normal(km, w.shape, _jnp.float32)
        out["v_" + name] = (s * s) * _jax.random.uniform(kv, w.shape, _jnp.float32, 0.5, 1.5)
    if N_MICROBATCH > 1:
        for name, axis in PER_EXAMPLE_BATCH_AXIS.items():
            out[name] = _to_microbatches(out[name], axis)
    return {'x': out['x'], 'rel_bias': out['rel_bias'], 'even_norm': out['even_norm'], 'even_w_in': out['even_w_in'], 'even_conv_w': out['even_conv_w'], 'even_pool_w': out['even_pool_w'], 'even_pool_scale': out['even_pool_scale'], 'even_w_out': out['even_w_out'], 'odd_norm': out['odd_norm'], 'odd_w_qkv': out['odd_w_qkv'], 'odd_q_norm': out['odd_q_norm'], 'odd_k_norm': out['odd_k_norm'], 'odd_w_o': out['odd_w_o'], 'ffn_norm': out['ffn_norm'], 'ffn_w_up': out['ffn_w_up'], 'ffn_conv_w': out['ffn_conv_w'], 'ffn_conv_b': out['ffn_conv_b'], 'ffn_w_down': out['ffn_w_down'], 'loss_target': out['loss_target'], 'm_rel_bias': out['m_rel_bias'], 'm_even_norm': out['m_even_norm'], 'm_even_w_in': out['m_even_w_in'], 'm_even_conv_w': out['m_even_conv_w'], 'm_even_pool_w': out['m_even_pool_w'], 'm_even_pool_scale': out['m_even_pool_scale'], 'm_even_w_out': out['m_even_w_out'], 'm_odd_norm': out['m_odd_norm'], 'm_odd_w_qkv': out['m_odd_w_qkv'], 'm_odd_q_norm': out['m_odd_q_norm'], 'm_odd_k_norm': out['m_odd_k_norm'], 'm_odd_w_o': out['m_odd_w_o'], 'm_ffn_norm': out['m_ffn_norm'], 'm_ffn_w_up': out['m_ffn_w_up'], 'm_ffn_conv_w': out['m_ffn_conv_w'], 'm_ffn_conv_b': out['m_ffn_conv_b'], 'm_ffn_w_down': out['m_ffn_w_down'], 'v_rel_bias': out['v_rel_bias'], 'v_even_norm': out['v_even_norm'], 'v_even_w_in': out['v_even_w_in'], 'v_even_conv_w': out['v_even_conv_w'], 'v_even_pool_w': out['v_even_pool_w'], 'v_even_pool_scale': out['v_even_pool_scale'], 'v_even_w_out': out['v_even_w_out'], 'v_odd_norm': out['v_odd_norm'], 'v_odd_w_qkv': out['v_odd_w_qkv'], 'v_odd_q_norm': out['v_odd_q_norm'], 'v_odd_k_norm': out['v_odd_k_norm'], 'v_odd_w_o': out['v_odd_w_o'], 'v_ffn_norm': out['v_ffn_norm'], 'v_ffn_w_up': out['v_ffn_w_up'], 'v_ffn_conv_w': out['v_ffn_conv_w'], 'v_ffn_conv_b': out['v_ffn_conv_b'], 'v_ffn_w_down': out['v_ffn_w_down']}


def _loss(weights, diff, rest, loss_target):
    with _jax.named_scope("forward"):
        args = {**rest, TWIN_DIFF_INPUT: diff, **{k: w.astype(_WEIGHT_DTYPES[k]) for k, w in weights.items()}}
        y = _forward(args)
    with _jax.named_scope("loss_head"):
        err = _jnp.square(y.astype(_jnp.float32) - loss_target)
        return 0.5 * _jnp.sum(_jnp.mean(err, axis=-1)) if err.ndim else 0.5 * err


def _adamw(w, g, m, v):
    m = ADAM_B1 * m + (1.0 - ADAM_B1) * g
    v = ADAM_B2 * v + (1.0 - ADAM_B2) * _jnp.square(g)
    m_hat = m / (1.0 - ADAM_B1 ** ADAM_STEP)
    v_hat = v / (1.0 - ADAM_B2 ** ADAM_STEP)
    delta = -ADAM_LR * (m_hat / (_jnp.sqrt(v_hat) + ADAM_EPS) + ADAM_WD * w)
    return delta, m, v


def reference(x, rel_bias, even_norm, even_w_in, even_conv_w, even_pool_w, even_pool_scale, even_w_out, odd_norm, odd_w_qkv, odd_q_norm, odd_k_norm, odd_w_o, ffn_norm, ffn_w_up, ffn_conv_w, ffn_conv_b, ffn_w_down, loss_target, m_rel_bias, m_even_norm, m_even_w_in, m_even_conv_w, m_even_pool_w, m_even_pool_scale, m_even_w_out, m_odd_norm, m_odd_w_qkv, m_odd_q_norm, m_odd_k_norm, m_odd_w_o, m_ffn_norm, m_ffn_w_up, m_ffn_conv_w, m_ffn_conv_b, m_ffn_w_down, v_rel_bias, v_even_norm, v_even_w_in, v_even_conv_w, v_even_pool_w, v_even_pool_scale, v_even_w_out, v_odd_norm, v_odd_w_qkv, v_odd_q_norm, v_odd_k_norm, v_odd_w_o, v_ffn_norm, v_ffn_w_up, v_ffn_conv_w, v_ffn_conv_b, v_ffn_w_down):
    given = dict(x=x, rel_bias=rel_bias, even_norm=even_norm, even_w_in=even_w_in, even_conv_w=even_conv_w, even_pool_w=even_pool_w, even_pool_scale=even_pool_scale, even_w_out=even_w_out, odd_norm=odd_norm, odd_w_qkv=odd_w_qkv, odd_q_norm=odd_q_norm, odd_k_norm=odd_k_norm, odd_w_o=odd_w_o, ffn_norm=ffn_norm, ffn_w_up=ffn_w_up, ffn_conv_w=ffn_conv_w, ffn_conv_b=ffn_conv_b, ffn_w_down=ffn_w_down, loss_target=loss_target, m_rel_bias=m_rel_bias, m_even_norm=m_even_norm, m_even_w_in=m_even_w_in, m_even_conv_w=m_even_conv_w, m_even_pool_w=m_even_pool_w, m_even_pool_scale=m_even_pool_scale, m_even_w_out=m_even_w_out, m_odd_norm=m_odd_norm, m_odd_w_qkv=m_odd_w_qkv, m_odd_q_norm=m_odd_q_norm, m_odd_k_norm=m_odd_k_norm, m_odd_w_o=m_odd_w_o, m_ffn_norm=m_ffn_norm, m_ffn_w_up=m_ffn_w_up, m_ffn_conv_w=m_ffn_conv_w, m_ffn_conv_b=m_ffn_conv_b, m_ffn_w_down=m_ffn_w_down, v_rel_bias=v_rel_bias, v_even_norm=v_even_norm, v_even_w_in=v_even_w_in, v_even_conv_w=v_even_conv_w, v_even_pool_w=v_even_pool_w, v_even_pool_scale=v_even_pool_scale, v_even_w_out=v_even_w_out, v_odd_norm=v_odd_norm, v_odd_w_qkv=v_odd_w_qkv, v_odd_q_norm=v_odd_q_norm, v_odd_k_norm=v_odd_k_norm, v_odd_w_o=v_odd_w_o, v_ffn_norm=v_ffn_norm, v_ffn_w_up=v_ffn_w_up, v_ffn_conv_w=v_ffn_conv_w, v_ffn_conv_b=v_ffn_conv_b, v_ffn_w_down=v_ffn_w_down)
    weights = {n: given[n] for n in TWIN_WEIGHTS}
    shared = {n: given[n] for n in SHARED_INPUTS}
    per_example = {n: given[n] for n in ['x']}
    grad_fn = _jax.value_and_grad(_loss, argnums=(0, 1))

    def one_microbatch(ex, loss_target):
        ex = dict(ex)
        diff = ex.pop(TWIN_DIFF_INPUT)
        return grad_fn(weights, diff, {**shared, **ex}, loss_target)

    if N_MICROBATCH == 1:
        loss, (grad_w, grad_x) = one_microbatch(per_example, given["loss_target"])
    else:
        def body(carry, xs):
            loss_sum, grad_sum = carry
            l_k, (gw_k, gx_k) = one_microbatch(xs[0], xs[1])
            with _jax.named_scope("update"):
                return (loss_sum + l_k, _jax.tree.map(_jnp.add, grad_sum, gw_k)), gx_k

        init = (_jnp.zeros((), _jnp.float32), _jax.tree.map(_jnp.zeros_like, weights))
        (loss, grad_w), grad_x = _jax.lax.scan(body, init, (per_example, given["loss_target"]))
    with _jax.named_scope("update"):
        delta_w, new_m, new_v = {}, {}, {}
        for n in TWIN_WEIGHTS:
            delta_w[n], new_m[n], new_v[n] = _adamw(weights[n], grad_w[n], given["m_" + n], given["v_" + n])
    return (loss, grad_x, *[grad_w[n] for n in TWIN_WEIGHTS], *[delta_w[n] for n in TWIN_WEIGHTS],
            *[new_m[n] for n in TWIN_WEIGHTS], *[new_v[n] for n in TWIN_WEIGHTS])
```

```python
import functools
import math

import numpy as np
import jax
import jax.numpy as jnp
from jax import lax
from jax.experimental import pallas as pl
from jax.experimental.pallas import tpu as pltpu

F32 = jnp.float32
BF16 = jnp.bfloat16

D_MODEL = 1024
N_HEADS = 16
HEAD_DIM = 64
A_WIDTH = 512
POOL_WINDOWS = (2, 4, 8, 16)
POOL_GROUP = 128
EVEN_IN = 2048
D_FF = 2816
DILATED_PAIRS = ((128, 1), (512, 4), (2048, 16))
ATT_BLOCK = 128
N_REL_BUCKETS = 32
REL_MAX_DISTANCE = 2048
EPS = 1e-6
MASK_VALUE = -1e30
ADAM_LR, ADAM_B1, ADAM_B2, ADAM_EPS, ADAM_WD, ADAM_STEP = 0.001, 0.9, 0.999, 1e-08, 0.01, 10

VMEM_LIMIT_BYTES = 48 * 1024 * 1024
N_CHIPS = 4
N_DEV = 8
MESH = pl.DeviceIdType.MESH


def _params(*sem):
    return pltpu.CompilerParams(dimension_semantics=sem if sem else None, vmem_limit_bytes=VMEM_LIMIT_BYTES)


def _sds(shape, dtype):
    return jax.ShapeDtypeStruct(tuple(shape), dtype)


def cast_bf16(x, name, tr=None):
    lead, (R, C) = x.shape[:-2], x.shape[-2:]
    n = int(np.prod(lead)) if lead else 1
    x3 = x.reshape((n, R, C))
    tr = tr or R

    def body(x_ref, o_ref):
        o_ref[...] = x_ref[...].astype(BF16)

    out = pl.pallas_call(
        body, name=name, grid=(n, R // tr),
        in_specs=[pl.BlockSpec((None, tr, C), lambda i, r: (i, r, 0))],
        out_specs=pl.BlockSpec((None, tr, C), lambda i, r: (i, r, 0)),
        out_shape=_sds((n, R, C), BF16), compiler_params=_params("parallel", "parallel"),
    )(x3)
    return out.reshape(lead + (R, C))


def rmsnorm_fwd(x, g, name, ts=512):
    S, Dm = x.shape

    def body(x_ref, g_ref, o_ref):
        xv = x_ref[...]
        r = lax.rsqrt(jnp.mean(xv * xv, axis=-1, keepdims=True) + EPS)
        o_ref[...] = ((xv * r) * g_ref[...]).astype(BF16)

    return pl.pallas_call(
        body, name=name, grid=(S // ts,),
        in_specs=[pl.BlockSpec((ts, Dm), lambda i: (i, 0)), pl.BlockSpec((1, Dm), lambda i: (0, 0))],
        out_specs=pl.BlockSpec((ts, Dm), lambda i: (i, 0)),
        out_shape=_sds((S, Dm), BF16), compiler_params=_params("parallel"),
    )(x, g)


def rmsnorm_bwd(x, g, dxn, dres, name, ts=512):
    S, Dm = x.shape

    def body(x_ref, g_ref, d_ref, r_ref, dx_ref, dxb_ref, dg_ref):
        xv = x_ref[...]
        dv = d_ref[...].astype(F32)
        r = lax.rsqrt(jnp.mean(xv * xv, axis=-1, keepdims=True) + EPS)
        gx = dv * g_ref[...]
        dot = jnp.sum(gx * xv, axis=-1, keepdims=True)
        dx = r_ref[...] + r * gx - xv * ((r * r * r) * (dot * (1.0 / Dm)))
        dx_ref[...] = dx
        dxb_ref[...] = dx.astype(BF16)
        part = jnp.sum(dv * (xv * r), axis=0, keepdims=True)

        @pl.when(pl.program_id(0) == 0)
        def _():
            dg_ref[...] = part

        @pl.when(pl.program_id(0) > 0)
        def _():
            dg_ref[...] += part

    row = pl.BlockSpec((ts, Dm), lambda i: (i, 0))
    vec = pl.BlockSpec((1, Dm), lambda i: (0, 0))
    return pl.pallas_call(
        body, name=name, grid=(S // ts,),
        in_specs=[row, vec, row, row], out_specs=[row, row, vec],
        out_shape=[_sds((S, Dm), F32), _sds((S, Dm), BF16), _sds((1, Dm), F32)], compiler_params=_params("arbitrary"),
    )(x, g, dxn, dres)


def mm_nn(a, w, name, layer=0, res=None, out_dtype=F32, tm=512):
    M, K = a.shape
    J, _, _, Ns = w.shape

    def body(*refs):
        a_ref, w_ref = refs[0], refs[1]
        o_ref = refs[-1]
        acc = jnp.dot(a_ref[...], w_ref[...], preferred_element_type=F32)
        if res is not None:
            acc = refs[2][...] + acc
        o_ref[...] = acc.astype(o_ref.dtype)

    in_specs = [pl.BlockSpec((tm, K), lambda j, m: (m, 0)),
                pl.BlockSpec((None, None, K, Ns), lambda j, m: (j, layer, 0, 0))]
    args = [a, w]
    if res is not None:
        in_specs.append(pl.BlockSpec((tm, Ns), lambda j, m: (m, j)))
        args.append(res)
    return pl.pallas_call(
        body, name=name, grid=(J, M // tm), in_specs=in_specs,
        out_specs=pl.BlockSpec((tm, Ns), lambda j, m: (m, j)),
        out_shape=_sds((M, J * Ns), out_dtype), compiler_params=_params("parallel", "parallel"),
    )(*args)


def mm_nt(dy, w, name, tr, layer=0, out_dtype=F32, tm=512):
    M = dy.shape[0]
    J, _, R, Ns = w.shape
    dims = (((1,), (1,)), ((), ()))

    def body(dy_ref, w_ref, o_ref, *scratch):
        p = lax.dot_general(dy_ref[...], w_ref[...], dims, preferred_element_type=F32)
        if J == 1:
            o_ref[...] = p.astype(o_ref.dtype)
            return
        acc_ref, = scratch
        j = pl.program_id(2)

        @pl.when(j == 0)
        def _():
            acc_ref[...] = p

        @pl.when(j > 0)
        def _():
            acc_ref[...] += p

        @pl.when(j == J - 1)
        def _():
            o_ref[...] = acc_ref[...].astype(o_ref.dtype)

    return pl.pallas_call(
        body, name=name, grid=(R // tr, M // tm, J),
        in_specs=[pl.BlockSpec((tm, Ns), lambda r, m, j: (m, j)),
                  pl.BlockSpec((None, None, tr, Ns), lambda r, m, j: (j, layer, r, 0))],
        out_specs=pl.BlockSpec((tm, tr), lambda r, m, j: (m, r)),
        out_shape=_sds((M, R), out_dtype),
        scratch_shapes=[] if J == 1 else [pltpu.VMEM((tm, tr), F32)],
        compiler_params=_params("parallel", "parallel", "arbitrary"),
    )(dy, w)


def mm_tn(a, dy, name, J, tk, tm=512):
    M, K = a.shape
    Ns = dy.shape[1] // J
    dims = (((0,), (0,)), ((), ()))

    def body(a_ref, dy_ref, o_ref):
        p = lax.dot_general(a_ref[...], dy_ref[...], dims, preferred_element_type=F32)
        m = pl.program_id(2)

        @pl.when(m == 0)
        def _():
            o_ref[...] = p

        @pl.when(m > 0)
        def _():
            o_ref[...] += p

    return pl.pallas_call(
        body, name=name, grid=(J, K // tk, M // tm),
        in_specs=[pl.BlockSpec((tm, tk), lambda j, k, m: (m, k)), pl.BlockSpec((tm, Ns), lambda j, k, m: (m, j))],
        out_specs=pl.BlockSpec((None, tk, Ns), lambda j, k, m: (j, k, 0)),
        out_shape=_sds((J, K, Ns), F32), compiler_params=_params("parallel", "parallel", "arbitrary"),
    )(a, dy)


HALO = 16


def _shift_down(x, s):
    return pltpu.roll(x, s, 0)


def _shift_up(x, s):
    return pltpu.roll(x, x.shape[0] - s, 0)


def _conv3(z, cw):
    return (_shift_down(z, 2) * cw[0:1] + _shift_down(z, 1) * cw[1:2]) + z * cw[2:3]


def _window_count(first_row, n, k):
    t = first_row + lax.broadcasted_iota(jnp.int32, (n, 1), 0)
    return jnp.clip(t + 1, 1, k).astype(F32)


def mixer_fwd(proj, conv_w, pool_w, pool_scale, name, ts=256):
    S = proj.shape[0]
    n = ts + HALO

    def body(pm_ref, pb_ref, cw_ref, pw_ref, ps_ref, o_ref):
        i = pl.program_id(0)
        before = jnp.where(i > 0, pb_ref[...], 0.0)
        ext = jnp.concatenate([before, pm_ref[...]], axis=0)
        cw = cw_ref[...]
        z = ext[:, 2 * A_WIDTH:3 * A_WIDTH] * ext[:, 0:A_WIDTH]
        cz = _conv3(z, cw)
        ya = pm_ref[:, A_WIDTH:2 * A_WIDTH] * cz[HALO:]
        o_ref[:, 0:A_WIDTH] = ya.astype(BF16)
        for g, k in enumerate(POOL_WINDOWS):
            lo = 3 * A_WIDTH + g * POOL_GROUP
            p = ext[:, lo:lo + POOL_GROUP]
            w = p
            s = 1
            while s < k:
                w = w + _shift_down(w, s)
                s *= 2
            pooled = w / _window_count(i * ts - HALO, n, k) - p
            yb = jnp.dot(pooled[HALO:].astype(BF16), pw_ref[g], preferred_element_type=F32)
            yb = yb * ps_ref[:, g * POOL_GROUP:(g + 1) * POOL_GROUP]
            o_ref[:, A_WIDTH + g * POOL_GROUP:A_WIDTH + (g + 1) * POOL_GROUP] = yb.astype(BF16)

    hb = ts // HALO
    return pl.pallas_call(
        body, name=name, grid=(S // ts,),
        in_specs=[
            pl.BlockSpec((ts, EVEN_IN), lambda i: (i, 0)),
            pl.BlockSpec((HALO, EVEN_IN), lambda i: (jnp.maximum(i * hb - 1, 0), 0)),
            pl.BlockSpec((3, A_WIDTH), lambda i: (0, 0)),
            pl.BlockSpec((4, POOL_GROUP, POOL_GROUP), lambda i: (0, 0, 0)),
            pl.BlockSpec((1, 4 * POOL_GROUP), lambda i: (0, 0)),
        ],
        out_specs=pl.BlockSpec((ts, D_MODEL), lambda i: (i, 0)),
        out_shape=_sds((S, D_MODEL), BF16), compiler_params=_params("parallel"),
    )(proj, proj, conv_w, pool_w, pool_scale)


def mixer_bwd(proj, dmix, conv_w, pool_w, pool_scale, name, ts=256):
    S = proj.shape[0]
    n = ts + 2 * HALO
    nt = S // ts
    tn_dims = (((0,), (0,)), ((), ()))
    nt_dims = (((1,), (1,)), ((), ()))

    def body(pm_ref, pb_ref, pa_ref, dm_ref, da_ref, cw_ref, pw_ref, ps_ref, o_ref, dcw_ref, dpw_ref, dps_ref):
        i = pl.program_id(0)
        last = i == nt - 1
        before = jnp.where(i > 0, pb_ref[...], 0.0)
        after = jnp.where(last, 0.0, pa_ref[...])
        ext = jnp.concatenate([before, pm_ref[...], after], axis=0)
        dafter = jnp.where(last, 0.0, da_ref[...])
        dext = jnp.concatenate([jnp.zeros((HALO, D_MODEL), F32), dm_ref[...], dafter], axis=0)
        cw = cw_ref[...]
        main = slice(HALO, HALO + ts)

        @pl.when(i == 0)
        def _():
            dcw_ref[...] = jnp.zeros_like(dcw_ref)
            dpw_ref[...] = jnp.zeros_like(dpw_ref)
            dps_ref[...] = jnp.zeros_like(dps_ref)

        h, gb, gc = ext[:, 0:A_WIDTH], ext[:, A_WIDTH:2 * A_WIDTH], ext[:, 2 * A_WIDTH:3 * A_WIDTH]
        z = gc * h
        z1, z2 = _shift_down(z, 1), _shift_down(z, 2)
        cz = (z2 * cw[0:1] + z1 * cw[1:2]) + z * cw[2:3]
        dya = dext[:, 0:A_WIDTH]
        dcz = dya * gb
        dz = dcz * cw[2:3] + _shift_up(dcz, 1) * cw[1:2] + _shift_up(dcz, 2) * cw[0:1]
        o_ref[:, 0:A_WIDTH] = (dz * gc)[main].astype(BF16)
        o_ref[:, A_WIDTH:2 * A_WIDTH] = (dya * cz)[main].astype(BF16)
        o_ref[:, 2 * A_WIDTH:3 * A_WIDTH] = (dz * h)[main].astype(BF16)
        dczm = dcz[main]
        dcw_ref[0:1, :] += jnp.sum(dczm * z2[main], axis=0, keepdims=True)
        dcw_ref[1:2, :] += jnp.sum(dczm * z1[main], axis=0, keepdims=True)
        dcw_ref[2:3, :] += jnp.sum(dczm * z[main], axis=0, keepdims=True)

        for g, k in enumerate(POOL_WINDOWS):
            lo = 3 * A_WIDTH + g * POOL_GROUP
            cols = slice(g * POOL_GROUP, (g + 1) * POOL_GROUP)
            p = ext[:, lo:lo + POOL_GROUP]
            w = p
            s = 1
            while s < k:
                w = w + _shift_down(w, s)
                s *= 2
            cnt = _window_count(i * ts - HALO, n, k)
            pooled = (w / cnt - p)[main].astype(BF16)
            dyb = dext[:, A_WIDTH + g * POOL_GROUP:A_WIDTH + (g + 1) * POOL_GROUP]
            e = dyb * ps_ref[:, cols]
            pre = jnp.dot(pooled, pw_ref[g], preferred_element_type=F32)
            dps_ref[:, cols] += jnp.sum(dyb[main] * pre, axis=0, keepdims=True)
            dpw_ref[g] += lax.dot_general(pooled, e[main].astype(BF16), tn_dims, preferred_element_type=F32)
            dpooled = lax.dot_general(e.astype(BF16), pw_ref[g], nt_dims, preferred_element_type=F32)
            q = dpooled / cnt
            a = q
            s = 1
            while s < k:
                a = a + _shift_up(a, s)
                s *= 2
            o_ref[:, lo:lo + POOL_GROUP] = (a - dpooled)[main].astype(BF16)

    hb = ts // HALO
    nh = S // HALO
    before_map = lambda i: (jnp.maximum(i * hb - 1, 0), 0)
    after_map = lambda i: (jnp.minimum((i + 1) * hb, nh - 1), 0)
    full = lambda *shape: pl.BlockSpec(shape, lambda i: (0,) * len(shape))
    return pl.pallas_call(
        body, name=name, grid=(nt,),
        in_specs=[
            pl.BlockSpec((ts, EVEN_IN), lambda i: (i, 0)),
            pl.BlockSpec((HALO, EVEN_IN), before_map),
            pl.BlockSpec((HALO, EVEN_IN), after_map),
            pl.BlockSpec((ts, D_MODEL), lambda i: (i, 0)),
            pl.BlockSpec((HALO, D_MODEL), after_map),
            full(3, A_WIDTH), full(4, POOL_GROUP, POOL_GROUP), full(1, 4 * POOL_GROUP),
        ],
        out_specs=[pl.BlockSpec((ts, EVEN_IN), lambda i: (i, 0)), full(3, A_WIDTH), full(4, POOL_GROUP, POOL_GROUP),
                   full(1, 4 * POOL_GROUP)],
        out_shape=[_sds((S, EVEN_IN), BF16), _sds((3, A_WIDTH), F32), _sds((4, POOL_GROUP, POOL_GROUP), F32),
                   _sds((1, 4 * POOL_GROUP), F32)],
        compiler_params=_params("arbitrary"),
    )(proj, proj, proj, dmix, dmix, conv_w, pool_w, pool_scale)


FFN_HALO = 8
FFN_TC = 1408


def glu_fwd(up, conv_w, conv_b, name, ts=256):
    S = up.shape[0]
    nc = D_FF // FFN_TC

    def body(gm_ref, gb_ref, um_ref, ub_ref, cwg_ref, cwu_ref, cbg_ref, cbu_ref, o_ref):
        i = pl.program_id(0)

        def conv(m_ref, b_ref, cw_ref, cb_ref):
            before = jnp.where(i > 0, b_ref[...], 0.0)
            ext = jnp.concatenate([before, m_ref[...]], axis=0)
            return _conv3(ext, cw_ref[...])[FFN_HALO:] + cb_ref[...]

        gate = conv(gm_ref, gb_ref, cwg_ref, cbg_ref)
        upv = conv(um_ref, ub_ref, cwu_ref, cbu_ref)
        o_ref[...] = ((gate * (1.0 / (1.0 + jnp.exp(-gate)))) * upv).astype(BF16)

    hb = ts // FFN_HALO
    main = lambda off: pl.BlockSpec((ts, FFN_TC), lambda i, c: (i, c + off))
    halo = lambda off: pl.BlockSpec((FFN_HALO, FFN_TC), lambda i, c: (jnp.maximum(i * hb - 1, 0), c + off))
    cw = lambda off: pl.BlockSpec((3, FFN_TC), lambda i, c: (0, c + off))
    cb = lambda off: pl.BlockSpec((1, FFN_TC), lambda i, c: (0, c + off))
    return pl.pallas_call(
        body, name=name, grid=(S // ts, nc),
        in_specs=[main(0), halo(0), main(nc), halo(nc), cw(0), cw(nc), cb(0), cb(nc)],
        out_specs=pl.BlockSpec((ts, FFN_TC), lambda i, c: (i, c)),
        out_shape=_sds((S, D_FF), BF16), compiler_params=_params("parallel", "parallel"),
    )(up, up, up, up, conv_w, conv_w, conv_b, conv_b)


def glu_bwd(up, da, conv_w, conv_b, name, ts=256):
    S = up.shape[0]
    nc = D_FF // FFN_TC
    nt = S // ts
    main = slice(FFN_HALO, FFN_HALO + ts)
    W = 2 * D_FF

    def body(xm_ref, xb_ref, xa_ref, dm_ref, da_ref, cw_ref, cb_ref, dx_ref, dcw_ref, dcb_ref):
        i = pl.program_id(0)
        last = i == nt - 1

        @pl.when(i == 0)
        def _():
            dcw_ref[...] = jnp.zeros_like(dcw_ref)
            dcb_ref[...] = jnp.zeros_like(dcb_ref)

        def ext_of(cols):
            before = jnp.where(i > 0, xb_ref[:, cols], 0.0)
            return jnp.concatenate([before, xm_ref[:, cols], xa_ref[:, cols]], axis=0)

        def back(x, d, cols):
            cw = cw_ref[:, cols]
            dx = d * cw[2:3] + _shift_up(d, 1) * cw[1:2] + _shift_up(d, 2) * cw[0:1]
            dx_ref[:, cols] = dx[main].astype(BF16)
            dmn = d[main]
            dcb_ref[:, cols] += jnp.sum(dmn, axis=0, keepdims=True)
            dcw_ref[0:1, cols] += jnp.sum(dmn * _shift_down(x, 2)[main], axis=0, keepdims=True)
            dcw_ref[1:2, cols] += jnp.sum(dmn * _shift_down(x, 1)[main], axis=0, keepdims=True)
            dcw_ref[2:3, cols] += jnp.sum(dmn * x[main], axis=0, keepdims=True)

        for c in range(nc):
            gcols = slice(c * FFN_TC, (c + 1) * FFN_TC)
            ucols = slice(D_FF + c * FFN_TC, D_FF + (c + 1) * FFN_TC)
            xg, xu = ext_of(gcols), ext_of(ucols)
            ug = _conv3(xg, cw_ref[:, gcols]) + cb_ref[:, gcols]
            uu = _conv3(xu, cw_ref[:, ucols]) + cb_ref[:, ucols]
            dafter = jnp.where(last, 0.0, da_ref[:, gcols].astype(F32))
            dae = jnp.concatenate([jnp.zeros((FFN_HALO, FFN_TC), F32), dm_ref[:, gcols].astype(F32), dafter], axis=0)
            sg = 1.0 / (1.0 + jnp.exp(-ug))
            duu = dae * (ug * sg)
            dug = (dae * uu) * (sg * (1.0 + ug * (1.0 - sg)))
            back(xg, dug, gcols)
            back(xu, duu, ucols)

    hb = ts // FFN_HALO
    nh = S // FFN_HALO
    before_map = lambda i: (jnp.maximum(i * hb - 1, 0), 0)
    after_map = lambda i: (jnp.minimum((i + 1) * hb, nh - 1), 0)
    return pl.pallas_call(
        body, name=name, grid=(nt,),
        in_specs=[pl.BlockSpec((ts, W), lambda i: (i, 0)), pl.BlockSpec((FFN_HALO, W), before_map),
                  pl.BlockSpec((FFN_HALO, W), after_map), pl.BlockSpec((ts, D_FF), lambda i: (i, 0)),
                  pl.BlockSpec((FFN_HALO, D_FF), after_map), pl.BlockSpec((3, W), lambda i: (0, 0)),
                  pl.BlockSpec((1, W), lambda i: (0, 0))],
        out_specs=[pl.BlockSpec((ts, W), lambda i: (i, 0)), pl.BlockSpec((3, W), lambda i: (0, 0)),
                   pl.BlockSpec((1, W), lambda i: (0, 0))],
        out_shape=[_sds((S, W), BF16), _sds((3, W), F32), _sds((1, W), F32)],
        compiler_params=_params("arbitrary"),
    )(up, up, up, da, da, conv_w, conv_b)


def _head_mean_matrix():
    h = np.arange(D_MODEL) // HEAD_DIM
    return jnp.asarray((h[:, None] == h[None, :]).astype(np.float32) / HEAD_DIM, dtype=BF16)


def _head_mean(v, gm):
    hi = v.astype(BF16)
    lo = (v - hi.astype(F32)).astype(BF16)
    return jnp.dot(hi, gm, preferred_element_type=F32) + jnp.dot(lo, gm, preferred_element_type=F32)


def qknorm_fwd(qkv, gqk, name, ts=512):
    S = qkv.shape[0]

    def body(x_ref, g_ref, gm_ref, o_ref):
        part = pl.program_id(0)
        x = x_ref[...]

        @pl.when(part < 2)
        def _():
            r = lax.rsqrt(_head_mean(x * x, gm_ref[...]) + EPS)
            o_ref[...] = ((x * r) * g_ref[...]).astype(BF16)

        @pl.when(part == 2)
        def _():
            o_ref[...] = x.astype(BF16)

    return pl.pallas_call(
        body, name=name, grid=(3, S // ts),
        in_specs=[pl.BlockSpec((ts, D_MODEL), lambda p, i: (i, p)), pl.BlockSpec((None, 1, D_MODEL), lambda p, i: (p, 0, 0)),
                  pl.BlockSpec((D_MODEL, D_MODEL), lambda p, i: (0, 0))],
        out_specs=pl.BlockSpec((ts, D_MODEL), lambda p, i: (i, p)),
        out_shape=_sds((S, 3 * D_MODEL), BF16), compiler_params=_params("parallel", "parallel"),
    )(qkv, gqk, _head_mean_matrix())


def qknorm_bwd(qkv, dq, dk, dv, gqk, name, ts=256):
    S = qkv.shape[0]

    def body(x_ref, dq_ref, dk_ref, dv_ref, g_ref, gm_ref, o_ref, dg_ref):
        @pl.when(pl.program_id(0) == 0)
        def _():
            dg_ref[...] = jnp.zeros_like(dg_ref)

        gm = gm_ref[...]
        for part, d_ref in enumerate((dq_ref, dk_ref)):
            cols = slice(part * D_MODEL, (part + 1) * D_MODEL)
            x = x_ref[:, cols]
            d = d_ref[...]
            r = lax.rsqrt(_head_mean(x * x, gm) + EPS)
            gx = d * g_ref[part]
            o_ref[:, cols] = (r * gx - x * ((r * r * r) * _head_mean(gx * x, gm))).astype(BF16)
            dg_ref[part] += jnp.sum(d * (x * r), axis=0, keepdims=True)
        o_ref[:, 2 * D_MODEL:] = dv_ref[...].astype(BF16)

    row = pl.BlockSpec((ts, D_MODEL), lambda i: (i, 0))
    wide = pl.BlockSpec((ts, 3 * D_MODEL), lambda i: (i, 0))
    gains = pl.BlockSpec((3, 1, D_MODEL), lambda i: (0, 0, 0))
    return pl.pallas_call(
        body, name=name, grid=(S // ts,),
        in_specs=[wide, row, row, row, gains, pl.BlockSpec((D_MODEL, D_MODEL), lambda i: (0, 0))],
        out_specs=[wide, gains],
        out_shape=[_sds((S, 3 * D_MODEL), BF16), _sds((3, 1, D_MODEL), F32)],
        compiler_params=_params("arbitrary"),
    )(qkv, dq, dk, dv, gqk, _head_mean_matrix())


def _bucket_tables():
    n = ATT_BLOCK
    a = np.arange(n)[:, None]
    c = np.arange(2 * n)[None, :]
    rel = a + n - c
    band = (rel >= 0) & (rel <= n)
    max_exact = N_REL_BUCKETS // 2
    buckets, valids = [], []
    for _, dil in DILATED_PAIRS:
        dist = np.clip(rel, 0, n) * dil
        dd = np.maximum(dist, 1).astype(np.float32)
        large = max_exact + (np.log(dd / np.float32(max_exact)) / np.float32(math.log(REL_MAX_DISTANCE / max_exact))
                             * np.float32(N_REL_BUCKETS - max_exact)).astype(np.int32)
        large = np.minimum(large, N_REL_BUCKETS - 1)
        buckets.append(np.where(dist < max_exact, dist, large).reshape(1, -1))
        valids.append(np.stack([(band & (c >= n)).reshape(1, -1), band.reshape(1, -1)]))
    return np.stack(buckets).astype(np.int32), np.stack(valids).astype(np.int32)


BIAS_CHUNK = 8192


def _split3(x):
    a = x.astype(BF16)
    r = x - a.astype(F32)
    b = r.astype(BF16)
    c = (r - b.astype(F32)).astype(BF16)
    return a, b, c


def bias_expand(rel_bias_t, name):
    bucket, valid = _bucket_tables()
    nq = bucket.shape[-1]

    def body(t_ref, b_ref, v_ref, o_ref):
        onehot = (lax.broadcasted_iota(jnp.int32, (N_REL_BUCKETS, BIAS_CHUNK), 0) == b_ref[...]).astype(BF16)
        acc = None
        for term in _split3(t_ref[...]):
            p = jnp.dot(term, onehot, preferred_element_type=F32)
            acc = p if acc is None else acc + p
        o_ref[...] = jnp.where(v_ref[...] > 0, acc, MASK_VALUE)

    return pl.pallas_call(
        body, name=name, grid=(3, 2, nq // BIAS_CHUNK),
        in_specs=[pl.BlockSpec((N_HEADS, N_REL_BUCKETS), lambda b, v, c: (0, 0)),
                  pl.BlockSpec((None, 1, BIAS_CHUNK), lambda b, v, c: (b, 0, c)),
                  pl.BlockSpec((None, None, 1, BIAS_CHUNK), lambda b, v, c: (b, v, 0, c))],
        out_specs=pl.BlockSpec((None, None, N_HEADS, BIAS_CHUNK), lambda b, v, c: (b, v, 0, c)),
        out_shape=_sds((3, 2, N_HEADS, nq), F32), compiler_params=_params("parallel", "parallel", "parallel"),
    )(rel_bias_t, jnp.asarray(bucket), jnp.asarray(valid))


def bias_reduce(dbias, name):
    bucket, _ = _bucket_tables()
    nq = bucket.shape[-1]
    dims = (((1,), (1,)), ((), ()))

    def body(d_ref, b_ref, o_ref):
        onehot = (lax.broadcasted_iota(jnp.int32, (N_REL_BUCKETS, BIAS_CHUNK), 0) == b_ref[...]).astype(BF16)
        acc = None
        for term in _split3(d_ref[...]):
            p = lax.dot_general(term, onehot, dims, preferred_element_type=F32)
            acc = p if acc is None else acc + p

        @pl.when(pl.program_id(1) == 0)
        def _():
            o_ref[...] = acc

        @pl.when(pl.program_id(1) > 0)
        def _():
            o_ref[...] += acc

    return pl.pallas_call(
        body, name=name, grid=(3, nq // BIAS_CHUNK),
        in_specs=[pl.BlockSpec((None, N_HEADS, BIAS_CHUNK), lambda b, c: (b, 0, c)),
                  pl.BlockSpec((None, 1, BIAS_CHUNK), lambda b, c: (b, 0, c))],
        out_specs=pl.BlockSpec((None, N_HEADS, N_REL_BUCKETS), lambda b, c: (b, 0, 0)),
        out_shape=_sds((3, N_HEADS, N_REL_BUCKETS), F32), compiler_params=_params("parallel", "arbitrary"),
    )(dbias, jnp.asarray(bucket))


PAIR = 2 * HEAD_DIM
N_PAIRS = N_HEADS // 2
_NT = (((1,), (1,)), ((), ()))
_TN = (((0,), (0,)), ((), ()))


def _low_lanes(shape):
    return lax.broadcasted_iota(jnp.int32, shape, 1) < HEAD_DIM


def _per_head(x, low):
    del low
    return x[:, 0:1], x[:, HEAD_DIM:HEAD_DIM + 1]


def attn_fwd_branch(qkvn, bias, branch, state, last, name):
    S = qkvn.shape[0]
    dil = DILATED_PAIRS[branch][1]
    L = S // dil
    nb = L // ATT_BLOCK
    view = qkvn.reshape(L, dil * 3 * D_MODEL)
    first = state is None

    def body(*refs):
        q_ref, kp_ref, kc_ref, vp_ref, vc_ref, b_ref = refs[:6]
        ins = refs[6:6 + (0 if first else 3)]
        outs = refs[6 + len(ins):]
        low = _low_lanes((ATT_BLOCK, PAIR))
        for hp in range(N_PAIRS):
            cols = slice(hp * PAIR, (hp + 1) * PAIR)
            q = q_ref[:, cols]
            k = jnp.concatenate([kp_ref[:, cols], kc_ref[:, cols]], axis=0)
            v = jnp.concatenate([vp_ref[:, cols], vc_ref[:, cols]], axis=0)
            pv, mx, den = [], [], []
            for hh in range(2):
                qh = jnp.where(low if hh == 0 else ~low, q, jnp.zeros_like(q))
                s = lax.dot_general(qh, k, _NT, preferred_element_type=F32) * (HEAD_DIM ** -0.5) + b_ref[2 * hp + hh]
                m = jnp.max(s, axis=-1, keepdims=True)
                p = jnp.exp(s - m)
                den.append(jnp.sum(p, axis=-1, keepdims=True))
                mx.append(m)
                pv.append(jnp.dot(p.astype(BF16), v, preferred_element_type=F32))
            acc = jnp.where(low, pv[0], pv[1])
            m = jnp.where(low, mx[0], mx[1])
            l = jnp.where(low, den[0], den[1])
            if not first:
                m_old = ins[1][:, cols]
                m_new = jnp.maximum(m_old, m)
                a_old, a_new = jnp.exp(m_old - m_new), jnp.exp(m - m_new)
                acc = ins[0][:, cols] * a_old + acc * a_new
                l = ins[2][:, cols] * a_old + l * a_new
                m = m_new
            if last:
                outs[0][:, cols] = (acc / l).astype(BF16)
                outs[1][:, cols] = m + jnp.log(l)
            else:
                outs[0][:, cols] = acc
                outs[1][:, cols] = m
                outs[2][:, cols] = l

    blk = lambda part, prev: pl.BlockSpec(
        (ATT_BLOCK, D_MODEL), lambda r, b: (jnp.maximum(b - 1, 0) if prev else b, 3 * r + part))
    st = pl.BlockSpec((ATT_BLOCK, D_MODEL), lambda r, b: (b, r))
    in_specs = [blk(0, False), blk(1, True), blk(1, False), blk(2, True), blk(2, False),
                pl.BlockSpec((None, N_HEADS, ATT_BLOCK, 2 * ATT_BLOCK), lambda r, b: (jnp.minimum(b, 1), 0, 0, 0))]
    args = [view] * 5 + [bias]
    if not first:
        in_specs += [st] * 3
        args += [s_.reshape(L, dil * D_MODEL) for s_ in state]
    wide = (L, dil * D_MODEL)
    if last:
        out_shape = [_sds(wide, BF16), _sds(wide, F32)]
    else:
        out_shape = [_sds(wide, F32)] * 3
    outs = pl.pallas_call(
        body, name=name, grid=(dil, nb), in_specs=in_specs, out_specs=[st] * len(out_shape), out_shape=out_shape,
        compiler_params=_params("parallel", "arbitrary"),
    )(*args)
    return tuple(o.reshape(S, D_MODEL) for o in outs)


def attn_bwd_branch(qkvn, o, do, lse, bias, branch, grads, name):
    S = qkvn.shape[0]
    dil = DILATED_PAIRS[branch][1]
    L = S // dil
    nb = L // ATT_BLOCK
    view = qkvn.reshape(L, dil * 3 * D_MODEL)
    first = grads is None
    n_in = 0 if first else 3

    def body(*refs):
        q_ref, kp_ref, kc_ref, vp_ref, vc_ref, o_ref, do_ref, lse_ref, b_ref = refs[:9]
        ins = refs[9:9 + n_in]
        dq_ref, dk_ref, dv_ref, db_ref, ck_ref, cv_ref = refs[9 + n_in:]
        b = pl.program_id(1)
        low = _low_lanes((ATT_BLOCK, PAIR))

        @pl.when((pl.program_id(0) == 0) & (b == 0))
        def _():
            db_ref[...] = jnp.zeros_like(db_ref)

        @pl.when(b == 0)
        def _():
            ck_ref[...] = jnp.zeros_like(ck_ref)
            cv_ref[...] = jnp.zeros_like(cv_ref)

        @pl.when(b < nb)
        def _():
            for hp in range(N_PAIRS):
                cols = slice(hp * PAIR, (hp + 1) * PAIR)
                q = q_ref[:, cols]
                k = jnp.concatenate([kp_ref[:, cols], kc_ref[:, cols]], axis=0)
                v = jnp.concatenate([vp_ref[:, cols], vc_ref[:, cols]], axis=0)
                dout = do_ref[:, cols]
                prod = dout.astype(F32) * o_ref[:, cols].astype(F32)
                lse_h = _per_head(lse_ref[:, cols], low)
                dq, dk, dv = [], None, None
                for hh in range(2):
                    keep = low if hh == 0 else ~low
                    qh = jnp.where(keep, q, jnp.zeros_like(q))
                    doh = jnp.where(keep, dout, jnp.zeros_like(dout))
                    delta = jnp.sum(jnp.where(keep, prod, 0.0), axis=-1, keepdims=True)
                    s = lax.dot_general(qh, k, _NT, preferred_element_type=F32) * (HEAD_DIM ** -0.5) + b_ref[2 * hp + hh]
                    p = jnp.exp(s - lse_h[hh])
                    dp = lax.dot_general(doh, v, _NT, preferred_element_type=F32)
                    ds = p * (dp - delta)
                    db_ref[2 * hp + hh] += ds
                    dsb = (ds * (HEAD_DIM ** -0.5)).astype(BF16)
                    dq.append(jnp.dot(dsb, k, preferred_element_type=F32))
                    dkh = lax.dot_general(dsb, qh, _TN, preferred_element_type=F32)
                    dvh = lax.dot_general(p.astype(BF16), doh, _TN, preferred_element_type=F32)
                    dk = dkh if dk is None else dk + dkh
                    dv = dvh if dv is None else dv + dvh
                dq_new = jnp.where(low, dq[0], dq[1])
                dk_prev = ck_ref[:, cols] + dk[:ATT_BLOCK]
                dv_prev = cv_ref[:, cols] + dv[:ATT_BLOCK]
                if not first:
                    dq_new = dq_new + ins[0][:, cols]
                    dk_prev = dk_prev + ins[1][:, cols]
                    dv_prev = dv_prev + ins[2][:, cols]
                dq_ref[:, cols] = dq_new
                dk_ref[:, cols] = dk_prev
                dv_ref[:, cols] = dv_prev
                ck_ref[:, cols] = dk[ATT_BLOCK:]
                cv_ref[:, cols] = dv[ATT_BLOCK:]

        @pl.when(b == nb)
        def _():
            dk_last, dv_last = ck_ref[...], cv_ref[...]
            if not first:
                dk_last = dk_last + ins[1][...]
                dv_last = dv_last + ins[2][...]
            dk_ref[...] = dk_last
            dv_ref[...] = dv_last

    cur = lambda b: jnp.minimum(b, nb - 1)
    prev = lambda b: jnp.maximum(b - 1, 0)
    blk = lambda part, use_prev: pl.BlockSpec(
        (ATT_BLOCK, D_MODEL), lambda r, b: (prev(cur(b)) if use_prev else cur(b), 3 * r + part))
    qside = pl.BlockSpec((ATT_BLOCK, D_MODEL), lambda r, b: (cur(b), r))
    kside = pl.BlockSpec((ATT_BLOCK, D_MODEL), lambda r, b: (prev(b), r))
    bias_spec = pl.BlockSpec((None, N_HEADS, ATT_BLOCK, 2 * ATT_BLOCK), lambda r, b: (jnp.minimum(b, 1), 0, 0, 0))
    wide = (L, dil * D_MODEL)
    in_specs = [blk(0, False), blk(1, True), blk(1, False), blk(2, True), blk(2, False), qside, qside, qside, bias_spec]
    args = [view] * 5 + [o.reshape(wide), do.reshape(wide), lse.reshape(wide), bias]
    aliases = {}
    if not first:
        in_specs += [qside, kside, kside]
        args += [g.reshape(wide) for g in grads]
        aliases = {9: 0, 10: 1, 11: 2}
    dq, dk, dv, db = pl.pallas_call(
        body, name=name, grid=(dil, nb + 1), in_specs=in_specs,
        out_specs=[qside, kside, kside, pl.BlockSpec((N_HEADS, ATT_BLOCK, 2 * ATT_BLOCK), lambda r, b: (0, 0, 0))],
        out_shape=[_sds(wide, F32)] * 3 + [_sds((N_HEADS, ATT_BLOCK, 2 * ATT_BLOCK), F32)],
        scratch_shapes=[pltpu.VMEM((ATT_BLOCK, D_MODEL), F32), pltpu.VMEM((ATT_BLOCK, D_MODEL), F32)],
        input_output_aliases=aliases, compiler_params=_params("arbitrary", "arbitrary"),
    )(*args)
    return (dq.reshape(S, D_MODEL), dk.reshape(S, D_MODEL), dv.reshape(S, D_MODEL)), db


def loss_grad(y, target, name, ts=512):
    S, Dm = y.shape

    def body(y_ref, t_ref, d_ref, db_ref, s_ref):
        e = y_ref[...] - t_ref[...]
        d = e * (1.0 / Dm)
        d_ref[...] = d
        db_ref[...] = d.astype(BF16)
        part = jnp.sum(e * e, axis=0, keepdims=True)

        @pl.when(pl.program_id(0) == 0)
        def _():
            s_ref[...] = part

        @pl.when(pl.program_id(0) > 0)
        def _():
            s_ref[...] += part

    row = pl.BlockSpec((ts, Dm), lambda i: (i, 0))
    vec = pl.BlockSpec((1, Dm), lambda i: (0, 0))
    return pl.pallas_call(
        body, name=name, grid=(S // ts,), in_specs=[row, row], out_specs=[row, row, vec],
        out_shape=[_sds((S, Dm), F32), _sds((S, Dm), BF16), _sds((1, Dm), F32)], compiler_params=_params("arbitrary"),
    )(y, target)


def adamw(w, g, m, v, name):
    n, R, C = w.shape

    def body(w_ref, g_ref, m_ref, v_ref, d_ref, nm_ref, nv_ref):
        gv = g_ref[...]
        m2 = ADAM_B1 * m_ref[...] + (1.0 - ADAM_B1) * gv
        v2 = ADAM_B2 * v_ref[...] + (1.0 - ADAM_B2) * (gv * gv)
        m_hat = m2 / (1.0 - ADAM_B1 ** ADAM_STEP)
        v_hat = v2 / (1.0 - ADAM_B2 ** ADAM_STEP)
        d_ref[...] = -ADAM_LR * (m_hat / (jnp.sqrt(v_hat) + ADAM_EPS) + ADAM_WD * w_ref[...])
        nm_ref[...] = m2
        nv_ref[...] = v2

    tr = R
    while tr * C * 4 > (1 << 21) and tr % 16 == 0:
        tr //= 2
    spec = pl.BlockSpec((None, tr, C), lambda i, r: (i, r, 0))
    return pl.pallas_call(
        body, name=name, grid=(n, R // tr), in_specs=[spec] * 4, out_specs=[spec] * 3,
        out_shape=[_sds((n, R, C), F32)] * 3, compiler_params=_params("parallel", "parallel"),
    )(w, g, m, v)


def pair_sum(g, recv, core, name):
    _, _, hR, C = g.shape

    def body(c_ref, g_ref, r_ref, o_ref):
        del c_ref
        o_ref[...] = (g_ref[...] + r_ref[...]).astype(BF16)

    grid_spec = pltpu.PrefetchScalarGridSpec(
        num_scalar_prefetch=1, grid=(N_CHIPS,),
        in_specs=[pl.BlockSpec((None, None, hR, C), lambda j, c: (j, c[0], 0, 0)),
                  pl.BlockSpec((None, hR, C), lambda j, c: (j, 0, 0))],
        out_specs=pl.BlockSpec((None, hR, C), lambda j, c: (j, 0, 0)))
    return pl.pallas_call(body, name=name, grid_spec=grid_spec, out_shape=_sds((N_CHIPS, hR, C), BF16),
                          compiler_params=_params("parallel"))(core, g, recv)


def sum_slots(parts, name, out_dtype=F32):
    n, R, C = parts.shape
    tr = R
    while tr * C * n * parts.dtype.itemsize > (1 << 22) and tr % 32 == 0:
        tr //= 2

    def body(p_ref, o_ref):
        acc = p_ref[0].astype(F32)
        for q in range(1, n):
            acc = acc + p_ref[q].astype(F32)
        o_ref[...] = acc.astype(out_dtype)

    return pl.pallas_call(
        body, name=name, grid=(R // tr,), in_specs=[pl.BlockSpec((n, tr, C), lambda i: (0, i, 0))],
        out_specs=pl.BlockSpec((tr, C), lambda i: (i, 0)), out_shape=_sds((R, C), out_dtype),
        compiler_params=_params("parallel"),
    )(parts)


ANY = pl.BlockSpec(memory_space=pl.ANY)


def _coords():
    return lax.axis_index("x"), lax.axis_index("y"), lax.axis_index("c")


def _other_chips(mx, my):
    return [(1 - mx, my), (mx, 1 - my), (1 - mx, 1 - my)]


def _remote(src, dst, send, recv, dev):
    return pltpu.make_async_remote_copy(src_ref=src, dst_ref=dst, send_sem=send, recv_sem=recv, device_id=dev,
                                        device_id_type=MESH)


def allgather_devices(x, name):
    R, C = x.shape

    def body(x_ref, o_ref, send, recv, local_sem):
        mx, my, mc = _coords()
        me = 4 * mx + 2 * my + mc
        local = pltpu.make_async_copy(x_ref, o_ref.at[me], local_sem)
        local.start()
        peers = []
        for k in range(1, N_DEV):
            px = 1 - mx if k & 4 else mx
            py = 1 - my if k & 2 else my
            pc = 1 - mc if k & 1 else mc
            peers.append((px, py, pc))
        sends = [_remote(x_ref, o_ref.at[me], send.at[k], recv.at[k], p) for k, p in enumerate(peers)]
        for cp in sends:
            cp.start()
        for k, (px, py, pc) in enumerate(peers):
            _remote(x_ref, o_ref.at[4 * px + 2 * py + pc], send.at[k], recv.at[k], (px, py, pc)).wait_recv()
        for cp in sends:
            cp.wait_send()
        local.wait()

    return pl.pallas_call(
        body, name=name, in_specs=[ANY], out_specs=ANY, out_shape=_sds((N_DEV, R, C), x.dtype),
        scratch_shapes=[pltpu.SemaphoreType.DMA((N_DEV - 1,)), pltpu.SemaphoreType.DMA((N_DEV - 1,)),
                        pltpu.SemaphoreType.DMA],
    )(x)


def allgather_chips(shards, name):
    n = len(shards)

    def body(*refs):
        ins, outs = refs[:n], refs[n:2 * n]
        send, recv, local_sems = refs[2 * n:]
        mx, my, mc = _coords()
        j = 2 * mx + my
        sibling = (mx, my, 1 - mc)
        chips = _other_chips(mx, my)
        local = [pltpu.make_async_copy(ins[a], outs[a].at[j], local_sems.at[a]) for a in range(n)]
        for cp in local:
            cp.start()
        first = [_remote(ins[a].at[mc], outs[a].at[j, mc], send.at[a, k], recv.at[a, k], (px, py, mc))
                 for a in range(n) for k, (px, py) in enumerate(chips)]
        for cp in first:
            cp.start()
        passed = []
        for a in range(n):
            for k, (px, py) in enumerate(chips):
                landed = outs[a].at[2 * px + py, mc]
                _remote(ins[a].at[mc], landed, send.at[a, k], recv.at[a, k], (px, py, mc)).wait_recv()
                cp = _remote(landed, landed, send.at[a, 3 + k], recv.at[a, 3 + k], sibling)
                cp.start()
                passed.append(cp)
        for a in range(n):
            for k, (px, py) in enumerate(chips):
                other = outs[a].at[2 * px + py, 1 - mc]
                _remote(other, other, send.at[a, 3 + k], recv.at[a, 3 + k], sibling).wait_recv()
        for cp in first + passed:
            cp.wait_send()
        for cp in local:
            cp.wait()

    return pl.pallas_call(
        body, name=name, in_specs=[ANY] * n, out_specs=[ANY] * n,
        out_shape=[_sds((N_CHIPS,) + s.shape, s.dtype) for s in shards],
        scratch_shapes=[pltpu.SemaphoreType.DMA((n, 6)), pltpu.SemaphoreType.DMA((n, 6)), pltpu.SemaphoreType.DMA((n,))],
    )(*shards)


def exchange_pair(grads, name):
    n = len(grads)

    def body(*refs):
        ins, outs = refs[:n], refs[n:2 * n]
        send, recv = refs[2 * n:]
        mx, my, mc = _coords()
        copies = [_remote(ins[a].at[:, 1 - mc], outs[a], send.at[a], recv.at[a], (mx, my, 1 - mc)) for a in range(n)]
        for cp in copies:
            cp.start()
        for cp in copies:
            cp.wait()

    return pl.pallas_call(
        body, name=name, in_specs=[ANY] * n, out_specs=[ANY] * n,
        out_shape=[_sds((N_CHIPS,) + g.shape[2:], g.dtype) for g in grads],
        scratch_shapes=[pltpu.SemaphoreType.DMA((n,)), pltpu.SemaphoreType.DMA((n,))],
    )(*grads)


def exchange_chips(parts, name):
    n = len(parts)

    def body(*refs):
        ins, outs = refs[:n], refs[n:2 * n]
        send, recv, local_sems = refs[2 * n:]
        mx, my, mc = _coords()
        j = 2 * mx + my
        chips = _other_chips(mx, my)
        local = [pltpu.make_async_copy(ins[a].at[j], outs[a].at[j], local_sems.at[a]) for a in range(n)]
        for cp in local:
            cp.start()
        sends = [_remote(ins[a].at[2 * px + py], outs[a].at[j], send.at[a, k], recv.at[a, k], (px, py, mc))
                 for a in range(n) for k, (px, py) in enumerate(chips)]
        for cp in sends:
            cp.start()
        for a in range(n):
            for k, (px, py) in enumerate(chips):
                slot = outs[a].at[2 * px + py]
                _remote(slot, slot, send.at[a, k], recv.at[a, k], (px, py, mc)).wait_recv()
        for cp in sends:
            cp.wait_send()
        for cp in local:
            cp.wait()

    return pl.pallas_call(
        body, name=name, in_specs=[ANY] * n, out_specs=[ANY] * n,
        out_shape=[_sds(p.shape, p.dtype) for p in parts],
        scratch_shapes=[pltpu.SemaphoreType.DMA((n, 3)), pltpu.SemaphoreType.DMA((n, 3)), pltpu.SemaphoreType.DMA((n,))],
    )(*parts)


def join_halves(halves, places, out_shapes, name):
    n = len(halves)
    n_out = len(out_shapes)

    def body(*refs):
        ins, outs = refs[:n], refs[n:n + n_out]
        send, recv, local_sems = refs[n + n_out:]
        mx, my, mc = _coords()

        def place(a, half):
            o, layer = places[a]
            return outs[o].at[half] if layer is None else outs[o].at[layer, half]

        local = [pltpu.make_async_copy(ins[a], place(a, mc), local_sems.at[a]) for a in range(n)]
        for cp in local:
            cp.start()
        sends = [_remote(ins[a], place(a, mc), send.at[a], recv.at[a], (mx, my, 1 - mc)) for a in range(n)]
        for cp in sends:
            cp.start()
        for a in range(n):
            theirs = place(a, 1 - mc)
            _remote(theirs, theirs, send.at[a], recv.at[a], (mx, my, 1 - mc)).wait_recv()
        for cp in sends:
            cp.wait_send()
        for cp in local:
            cp.wait()

    return pl.pallas_call(
        body, name=name, in_specs=[ANY] * n, out_specs=[ANY] * n_out,
        out_shape=[_sds(s, F32) for s in out_shapes],
        scratch_shapes=[pltpu.SemaphoreType.DMA((n,)), pltpu.SemaphoreType.DMA((n,)), pltpu.SemaphoreType.DMA((n,))],
    )(*halves)


LANES = 128
SUBLANES = 8


def _n_rows(shape):
    rows = -(-int(np.prod(shape)) // LANES)
    return -(-rows // SUBLANES) * SUBLANES


def _as_rows(a):
    flat = a.reshape(-1)
    rows = _n_rows(a.shape)
    return jnp.pad(flat, (0, rows * LANES - flat.shape[0])).reshape(rows, LANES)


def _pack(arrays):
    return jnp.concatenate([_as_rows(a) for a in arrays], axis=0)


def _unpack(rows, shapes):
    out, r0 = [], 0
    for s in shapes:
        n = _n_rows(s)
        out.append(rows[r0:r0 + n].reshape(-1)[:int(np.prod(s))].reshape(s))
        r0 += n
    return out


REPLICATED_SMALL = [("rel_bias", (32, 16)), ("even_norm", (1, 1024)), ("even_pool_w", (1, 4, 128, 128)),
                    ("even_pool_scale", (1, 512)), ("odd_q_norm", (1, 64)), ("odd_k_norm", (1, 64)),
                    ("ffn_norm", (2, 1024)), ("ffn_conv_b", (2, 5632))]
SHARDED_SMALL = [("even_conv_w", (1, 3, 128)), ("odd_norm", (1, 256)), ("ffn_conv_w", (2, 3, 1408))]
BIG = ["even_w_in", "even_w_out", "odd_w_qkv", "odd_w_o", "ffn_w_up", "ffn_w_down"]
WEIGHT_ORDER = ["rel_bias", "even_norm", "even_w_in", "even_conv_w", "even_pool_w", "even_pool_scale", "even_w_out",
                "odd_norm", "odd_w_qkv", "odd_q_norm", "odd_k_norm", "odd_w_o", "ffn_norm", "ffn_w_up", "ffn_conv_w",
                "ffn_conv_b", "ffn_w_down"]


def kernel(x, rel_bias, even_norm, even_w_in, even_conv_w, even_pool_w, even_pool_scale, even_w_out, odd_norm, odd_w_qkv, odd_q_norm, odd_k_norm, odd_w_o, ffn_norm, ffn_w_up, ffn_conv_w, ffn_conv_b, ffn_w_down, loss_target, m_rel_bias, m_even_norm, m_even_w_in, m_even_conv_w, m_even_pool_w, m_even_pool_scale, m_even_w_out, m_odd_norm, m_odd_w_qkv, m_odd_q_norm, m_odd_k_norm, m_odd_w_o, m_ffn_norm, m_ffn_w_up, m_ffn_conv_w, m_ffn_conv_b, m_ffn_w_down, v_rel_bias, v_even_norm, v_even_w_in, v_even_conv_w, v_even_pool_w, v_even_pool_scale, v_even_w_out, v_odd_norm, v_odd_w_qkv, v_odd_q_norm, v_odd_k_norm, v_odd_w_o, v_ffn_norm, v_ffn_w_up, v_ffn_conv_w, v_ffn_conv_b, v_ffn_w_down):
    W = dict(rel_bias=rel_bias, even_norm=even_norm, even_w_in=even_w_in, even_conv_w=even_conv_w, even_pool_w=even_pool_w,
             even_pool_scale=even_pool_scale, even_w_out=even_w_out, odd_norm=odd_norm, odd_w_qkv=odd_w_qkv,
             odd_q_norm=odd_q_norm, odd_k_norm=odd_k_norm, odd_w_o=odd_w_o, ffn_norm=ffn_norm, ffn_w_up=ffn_w_up,
             ffn_conv_w=ffn_conv_w, ffn_conv_b=ffn_conv_b, ffn_w_down=ffn_w_down)
    M1 = dict(rel_bias=m_rel_bias, even_norm=m_even_norm, even_w_in=m_even_w_in, even_conv_w=m_even_conv_w,
              even_pool_w=m_even_pool_w, even_pool_scale=m_even_pool_scale, even_w_out=m_even_w_out, odd_norm=m_odd_norm,
              odd_w_qkv=m_odd_w_qkv, odd_q_norm=m_odd_q_norm, odd_k_norm=m_odd_k_norm, odd_w_o=m_odd_w_o,
              ffn_norm=m_ffn_norm, ffn_w_up=m_ffn_w_up, ffn_conv_w=m_ffn_conv_w, ffn_conv_b=m_ffn_conv_b,
              ffn_w_down=m_ffn_w_down)
    M2 = dict(rel_bias=v_rel_bias, even_norm=v_even_norm, even_w_in=v_even_w_in, even_conv_w=v_even_conv_w,
              even_pool_w=v_even_pool_w, even_pool_scale=v_even_pool_scale, even_w_out=v_even_w_out, odd_norm=v_odd_norm,
              odd_w_qkv=v_odd_w_qkv, odd_q_norm=v_odd_q_norm, odd_k_norm=v_odd_k_norm, odd_w_o=v_odd_w_o,
              ffn_norm=v_ffn_norm, ffn_w_up=v_ffn_w_up, ffn_conv_w=v_ffn_conv_w, ffn_conv_b=v_ffn_conv_b,
              ffn_w_down=v_ffn_w_down)
    mx, my, mc = _coords()
    chip = 2 * mx + my
    core = jnp.reshape(mc, (1,)).astype(jnp.int32)
    xs, target = x[0], loss_target[0]
    S = xs.shape[0]

    def halves(w):
        return w.reshape((2, w.shape[-2] // 2, w.shape[-1]))

    shards = [cast_bf16(halves(even_w_in), "cast_w_in"), cast_bf16(halves(even_w_out), "cast_w_out"),
              cast_bf16(halves(odd_w_qkv), "cast_w_qkv"), cast_bf16(halves(odd_w_o), "cast_w_o"),
              cast_bf16(ffn_w_up, "cast_w_up"), cast_bf16(ffn_w_down, "cast_w_down")]
    gathered = allgather_chips(shards, "allgather_weights")
    w_in = gathered[0].reshape(N_CHIPS, 1, D_MODEL, EVEN_IN // N_CHIPS)
    w_out = gathered[1].reshape(1, 1, D_MODEL, D_MODEL)
    w_qkv = gathered[2].reshape(N_CHIPS, 1, D_MODEL, 3 * D_MODEL // N_CHIPS)
    w_o = gathered[3].reshape(1, 1, D_MODEL, D_MODEL)
    w_up = gathered[4]
    w_down = gathered[5].transpose(1, 0, 2, 3).reshape(1, 2, D_FF, D_MODEL)

    small = allgather_devices(_pack([even_conv_w, odd_norm, ffn_conv_w]), "allgather_small_weights")[0::2]
    conv_w_full = small[:, 0:3].transpose(1, 0, 2).reshape(3, A_WIDTH)
    odd_norm_full = small[:, 8:10].reshape(1, D_MODEL)
    ffn_cw_full = small[:, 16:82].reshape(N_CHIPS, 2, 3, 2 * D_FF // N_CHIPS).transpose(1, 2, 0, 3).reshape(2, 3, 2 * D_FF)
    pool_w = cast_bf16(even_pool_w[0], "cast_pool_w")
    gqk = jnp.stack([jnp.tile(odd_q_norm[0], N_HEADS), jnp.tile(odd_k_norm[0], N_HEADS),
                     jnp.ones((D_MODEL,), F32)])[:, None, :]
    bias = bias_expand(rel_bias.T, "bias_expand").reshape(3, 2, N_HEADS, ATT_BLOCK, 2 * ATT_BLOCK)

    def ffn_fwd(l, xin):
        xn = rmsnorm_fwd(xin, ffn_norm[l:l + 1], f"ffn{l}_norm")
        up = mm_nn(xn, w_up, f"ffn{l}_up", layer=l)
        act = glu_fwd(up, ffn_cw_full[l], ffn_conv_b[l:l + 1], f"ffn{l}_glu")
        return mm_nn(act, w_down, f"ffn{l}_down", layer=l, res=xin), (xin, xn, up, act)

    xn0 = rmsnorm_fwd(xs, even_norm, "even_norm")
    proj = mm_nn(xn0, w_in, "even_in")
    mix = mixer_fwd(proj, conv_w_full, pool_w, even_pool_scale, "even_mixer")
    x1 = mm_nn(mix, w_out, "even_out", res=xs)
    x2, ffn0 = ffn_fwd(0, x1)
    xn2 = rmsnorm_fwd(x2, odd_norm_full, "odd_norm")
    qkv = mm_nn(xn2, w_qkv, "odd_qkv")
    qkvn = qknorm_fwd(qkv, gqk, "odd_qknorm")
    state = None
    for br in range(3):
        state = attn_fwd_branch(qkvn, bias[br], br, state, br == 2, f"attn_fwd{br}")
    att, lse = state
    x3 = mm_nn(att, w_o, "odd_out", res=x2)
    x4, ffn1 = ffn_fwd(1, x3)

    dy, dyb, sq = loss_grad(x4, target, "loss")
    loss = lax.psum(0.5 * jnp.sum(sq) * (1.0 / D_MODEL), ("x", "y", "c"))

    def ffn_bwd(l, dy, dyb, saved):
        xin, xn, up, act = saved
        dw_down = mm_tn(act, dyb, f"ffn{l}_dw_down", J=1, tk=D_FF // 2)
        dact = mm_nt(dyb, w_down, f"ffn{l}_dact", tr=D_FF // 2, layer=l)
        dup, dcw, dcb = glu_bwd(up, dact, ffn_cw_full[l], ffn_conv_b[l:l + 1], f"ffn{l}_glu_bwd")
        dw_up = mm_tn(xn, dup, f"ffn{l}_dw_up", J=N_CHIPS, tk=512)
        dxn = mm_nt(dup, w_up, f"ffn{l}_dxn", tr=D_MODEL, layer=l)
        dx, dxb, dg = rmsnorm_bwd(xin, ffn_norm[l:l + 1], dxn, dy, f"ffn{l}_norm_bwd")
        return dx, dxb, (dw_down, dw_up, dcw, dcb, dg)

    dx3, dx3b, g_ffn1 = ffn_bwd(1, dy, dyb, ffn1)
    dw_o = mm_tn(att, dx3b, "odd_dw_o", J=1, tk=512)
    datt = mm_nt(dx3b, w_o, "odd_datt", tr=D_MODEL, out_dtype=BF16)
    grads, dbias = None, []
    for br in range(3):
        grads, db = attn_bwd_branch(qkvn, att, datt, lse, bias[br], br, grads, f"attn_bwd{br}")
        dbias.append(db.reshape(N_HEADS, 2 * ATT_BLOCK * ATT_BLOCK))
    dqkv, dgqk = qknorm_bwd(qkv, grads[0], grads[1], grads[2], gqk, "odd_qknorm_bwd")
    dw_qkv = mm_tn(xn2, dqkv, "odd_dw_qkv", J=N_CHIPS, tk=512)
    dxn2 = mm_nt(dqkv, w_qkv, "odd_dxn", tr=D_MODEL)
    dx2, dx2b, dg_odd = rmsnorm_bwd(x2, odd_norm_full, dxn2, dx3, "odd_norm_bwd")
    dx1, dx1b, g_ffn0 = ffn_bwd(0, dx2, dx2b, ffn0)
    dw_out = mm_tn(mix, dx1b, "even_dw_out", J=1, tk=512)
    dmix = mm_nt(dx1b, w_out, "even_dmix", tr=D_MODEL)
    dproj, dcw_even, dpw, dps = mixer_bwd(proj, dmix, conv_w_full, pool_w, even_pool_scale, "even_mixer_bwd")
    dw_in = mm_tn(xn0, dproj, "even_dw_in", J=N_CHIPS, tk=512)
    dxn0 = mm_nt(dproj, w_in, "even_dxn", tr=D_MODEL)
    grad_x, _, dg_even = rmsnorm_bwd(xs, even_norm, dxn0, dx1, "even_norm_bwd")
    d_rel = jnp.sum(bias_reduce(jnp.stack(dbias), "bias_reduce"), axis=0).T

    def quarters(g):
        return g.reshape(N_CHIPS, 2, g.shape[0] * g.shape[1] // (2 * N_CHIPS), g.shape[-1])

    big_grads = [quarters(g) for g in (dw_in, dw_out, dw_qkv, dw_o, g_ffn0[1], g_ffn1[1], g_ffn0[0], g_ffn1[0])]
    names = ["w_in", "w_out", "w_qkv", "w_o", "w_up0", "w_up1", "w_down0", "w_down1"]
    theirs = exchange_pair(big_grads, "grads_pair_exchange")
    parts = [pair_sum(g, r, core, "pair_sum_" + nm) for g, r, nm in zip(big_grads, theirs, names)]
    slots = exchange_chips(parts, "grads_chip_exchange")
    reduced_halves = [sum_slots(s, "chip_sum_" + nm) for s, nm in zip(slots, names)]
    places = [(0, None), (1, None), (2, None), (3, None), (4, 0), (4, 1), (5, 0), (5, 1)]
    out_shapes = [(2,) + reduced_halves[a].shape for a in range(4)] + [(2, 2) + reduced_halves[a].shape for a in (4, 6)]
    joined = join_halves(reduced_halves, places, out_shapes, "grads_join_halves")
    G = {nm: g.reshape(W[nm].shape) for nm, g in zip(BIG, joined)}

    dcw_sh = dcw_even.reshape(3, N_CHIPS, A_WIDTH // N_CHIPS).transpose(1, 0, 2)
    don_sh = dg_odd.reshape(N_CHIPS, D_MODEL // N_CHIPS)
    dfcw = jnp.stack([g_ffn0[2], g_ffn1[2]])
    dfcw_sh = dfcw.reshape(2, 3, N_CHIPS, 2 * D_FF // N_CHIPS).transpose(2, 0, 1, 3)
    rep_grads = [d_rel, dg_even, dpw[None], dps, _head_sum(dgqk[0]), _head_sum(dgqk[1]),
                 jnp.concatenate([g_ffn0[4], g_ffn1[4]], axis=0), jnp.concatenate([g_ffn0[3], g_ffn1[3]], axis=0)]
    rep_rows = _pack(rep_grads)
    shard_rows = jnp.concatenate([_pack([dcw_sh[j], don_sh[j], dfcw_sh[j]]) for j in range(N_CHIPS)], axis=0)
    n_rep, n_shard = rep_rows.shape[0], shard_rows.shape[0] // N_CHIPS
    small_sum = sum_slots(allgather_devices(jnp.concatenate([rep_rows, shard_rows], axis=0), "allgather_small_grads"),
                          "small_grads_sum")
    mine = lax.dynamic_slice_in_dim(small_sum, n_rep + chip * n_shard, n_shard, axis=0)
    g_small = jnp.concatenate([small_sum[:n_rep], mine], axis=0)
    small_names = [n for n, _ in REPLICATED_SMALL + SHARDED_SMALL]
    small_shapes = [s for _, s in REPLICATED_SMALL + SHARDED_SMALL]
    G.update(dict(zip(small_names, _unpack(g_small, small_shapes))))

    D_, NM, NV = {}, {}, {}
    for nm in BIG:
        as3 = lambda a: a.reshape((-1,) + a.shape[-2:])
        outs = adamw(as3(W[nm]), as3(G[nm]), as3(M1[nm]), as3(M2[nm]), "adamw_" + nm)
        D_[nm], NM[nm], NV[nm] = [o.reshape(W[nm].shape) for o in outs]
    packs = [_pack([d[n] for n in small_names])[None] for d in (W, M1, M2)]
    outs = adamw(packs[0], g_small[None], packs[1], packs[2], "adamw_small")
    for dst, o in zip((D_, NM, NV), outs):
        dst.update(dict(zip(small_names, _unpack(o[0], small_shapes))))

    return (loss, grad_x[None], *[G[n] for n in WEIGHT_ORDER], *[D_[n] for n in WEIGHT_ORDER],
            *[NM[n] for n in WEIGHT_ORDER], *[NV[n] for n in WEIGHT_ORDER])


def _head_sum(dg):
    return jnp.sum(dg.reshape(N_HEADS, HEAD_DIM), axis=0, keepdims=True)
```

```python
import functools
import math

import numpy as np
import jax
import jax.numpy as jnp
from jax import lax
from jax.experimental import pallas as pl
from jax.experimental.pallas import tpu as pltpu

F32 = jnp.float32
BF16 = jnp.bfloat16

D_MODEL = 1024
N_HEADS = 16
HEAD_DIM = 64
A_WIDTH = 512
POOL_WINDOWS = (2, 4, 8, 16)
POOL_GROUP = 128
EVEN_IN = 2048
D_FF = 2816
DILATED_PAIRS = ((128, 1), (512, 4), (2048, 16))
ATT_BLOCK = 128
N_REL_BUCKETS = 32
REL_MAX_DISTANCE = 2048
EPS = 1e-6
MASK_VALUE = -1e30
ADAM_LR, ADAM_B1, ADAM_B2, ADAM_EPS, ADAM_WD, ADAM_STEP = 0.001, 0.9, 0.999, 1e-08, 0.01, 10

VMEM_LIMIT_BYTES = 48 * 1024 * 1024
N_CHIPS = 4
N_DEV = 8
MESH = pl.DeviceIdType.MESH


def _params(*sem):
    return pltpu.CompilerParams(dimension_semantics=sem if sem else None, vmem_limit_bytes=VMEM_LIMIT_BYTES)


def _sds(shape, dtype):
    return jax.ShapeDtypeStruct(tuple(shape), dtype)


def cast_bf16(x, name, tr=None):
    lead, (R, C) = x.shape[:-2], x.shape[-2:]
    n = int(np.prod(lead)) if lead else 1
    x3 = x.reshape((n, R, C))
    tr = tr or R

    def body(x_ref, o_ref):
        o_ref[...] = x_ref[...].astype(BF16)

    out = pl.pallas_call(
        body, name=name, grid=(n, R // tr),
        in_specs=[pl.BlockSpec((None, tr, C), lambda i, r: (i, r, 0))],
        out_specs=pl.BlockSpec((None, tr, C), lambda i, r: (i, r, 0)),
        out_shape=_sds((n, R, C), BF16), compiler_params=_params("parallel", "parallel"),
    )(x3)
    return out.reshape(lead + (R, C))


def rmsnorm_fwd(x, g, name, ts=512):
    S, Dm = x.shape

    def body(x_ref, g_ref, o_ref):
        xv = x_ref[...]
        r = lax.rsqrt(jnp.mean(xv * xv, axis=-1, keepdims=True) + EPS)
        o_ref[...] = ((xv * r) * g_ref[...]).astype(BF16)

    return pl.pallas_call(
        body, name=name, grid=(S // ts,),
        in_specs=[pl.BlockSpec((ts, Dm), lambda i: (i, 0)), pl.BlockSpec((1, Dm), lambda i: (0, 0))],
        out_specs=pl.BlockSpec((ts, Dm), lambda i: (i, 0)),
        out_shape=_sds((S, Dm), BF16), compiler_params=_params("parallel"),
    )(x, g)


def rmsnorm_bwd(x, g, dxn, dres, name, ts=512):
    S, Dm = x.shape

    def body(x_ref, g_ref, d_ref, r_ref, dx_ref, dxb_ref, dg_ref):
        xv = x_ref[...]
        dv = d_ref[...].astype(F32)
        r = lax.rsqrt(jnp.mean(xv * xv, axis=-1, keepdims=True) + EPS)
        gx = dv * g_ref[...]
        dot = jnp.sum(gx * xv, axis=-1, keepdims=True)
        dx = r_ref[...] + r * gx - xv * ((r * r * r) * (dot * (1.0 / Dm)))
        dx_ref[...] = dx
        dxb_ref[...] = dx.astype(BF16)
        part = jnp.sum(dv * (xv * r), axis=0, keepdims=True)

        @pl.when(pl.program_id(0) == 0)
        def _():
            dg_ref[...] = part

        @pl.when(pl.program_id(0) > 0)
        def _():
            dg_ref[...] += part

    row = pl.BlockSpec((ts, Dm), lambda i: (i, 0))
    vec = pl.BlockSpec((1, Dm), lambda i: (0, 0))
    return pl.pallas_call(
        body, name=name, grid=(S // ts,),
        in_specs=[row, vec, row, row], out_specs=[row, row, vec],
        out_shape=[_sds((S, Dm), F32), _sds((S, Dm), BF16), _sds((1, Dm), F32)], compiler_params=_params("arbitrary"),
    )(x, g, dxn, dres)


def mm_nn(a, w, name, layer=0, res=None, out_dtype=F32, tm=512):
    M, K = a.shape
    J, _, _, Ns = w.shape

    def body(*refs):
        a_ref, w_ref = refs[0], refs[1]
        o_ref = refs[-1]
        acc = jnp.dot(a_ref[...], w_ref[...], preferred_element_type=F32)
        if res is not None:
            acc = refs[2][...] + acc
        o_ref[...] = acc.astype(o_ref.dtype)

    in_specs = [pl.BlockSpec((tm, K), lambda j, m: (m, 0)),
                pl.BlockSpec((None, None, K, Ns), lambda j, m: (j, layer, 0, 0))]
    args = [a, w]
    if res is not None:
        in_specs.append(pl.BlockSpec((tm, Ns), lambda j, m: (m, j)))
        args.append(res)
    return pl.pallas_call(
        body, name=name, grid=(J, M // tm), in_specs=in_specs,
        out_specs=pl.BlockSpec((tm, Ns), lambda j, m: (m, j)),
        out_shape=_sds((M, J * Ns), out_dtype), compiler_params=_params("parallel", "parallel"),
    )(*args)


def mm_nt(dy, w, name, tr, layer=0, out_dtype=F32, tm=512):
    M = dy.shape[0]
    J, _, R, Ns = w.shape
    dims = (((1,), (1,)), ((), ()))

    def body(dy_ref, w_ref, o_ref, *scratch):
        p = lax.dot_general(dy_ref[...], w_ref[...], dims, preferred_element_type=F32)
        if J == 1:
            o_ref[...] = p.astype(o_ref.dtype)
            return
        acc_ref, = scratch
        j = pl.program_id(2)

        @pl.when(j == 0)
        def _():
            acc_ref[...] = p

        @pl.when(j > 0)
        def _():
            acc_ref[...] += p

        @pl.when(j == J - 1)
        def _():
            o_ref[...] = acc_ref[...].astype(o_ref.dtype)

    return pl.pallas_call(
        body, name=name, grid=(R // tr, M // tm, J),
        in_specs=[pl.BlockSpec((tm, Ns), lambda r, m, j: (m, j)),
                  pl.BlockSpec((None, None, tr, Ns), lambda r, m, j: (j, layer, r, 0))],
        out_specs=pl.BlockSpec((tm, tr), lambda r, m, j: (m, r)),
        out_shape=_sds((M, R), out_dtype),
        scratch_shapes=[] if J == 1 else [pltpu.VMEM((tm, tr), F32)],
        compiler_params=_params("parallel", "parallel", "arbitrary"),
    )(dy, w)


def mm_tn(a, dy, name, J, tk, tm=512):
    M, K = a.shape
    Ns = dy.shape[1] // J
    dims = (((0,), (0,)), ((), ()))

    def body(a_ref, dy_ref, o_ref):
        p = lax.dot_general(a_ref[...], dy_ref[...], dims, preferred_element_type=F32)
        m = pl.program_id(2)

        @pl.when(m == 0)
        def _():
            o_ref[...] = p

        @pl.when(m > 0)
        def _():
            o_ref[...] += p

    return pl.pallas_call(
        body, name=name, grid=(J, K // tk, M // tm),
        in_specs=[pl.BlockSpec((tm, tk), lambda j, k, m: (m, k)), pl.BlockSpec((tm, Ns), lambda j, k, m: (m, j))],
        out_specs=pl.BlockSpec((None, tk, Ns), lambda j, k, m: (j, k, 0)),
        out_shape=_sds((J, K, Ns), F32), compiler_params=_params("parallel", "parallel", "arbitrary"),
    )(a, dy)


HALO = 16


def _shift_down(x, s):
    return pltpu.roll(x, s, 0)


def _shift_up(x, s):
    return pltpu.roll(x, x.shape[0] - s, 0)


def _conv3(z, cw):
    return (_shift_down(z, 2) * cw[0:1] + _shift_down(z, 1) * cw[1:2]) + z * cw[2:3]


def _window_count(first_row, n, k):
    t = first_row + lax.broadcasted_iota(jnp.int32, (n, 1), 0)
    return jnp.clip(t + 1, 1, k).astype(F32)


def mixer_fwd(proj, conv_w, pool_w, pool_scale, name, ts=256):
    S = proj.shape[0]
    n = ts + HALO

    def body(pm_ref, pb_ref, cw_ref, pw_ref, ps_ref, o_ref):
        i = pl.program_id(0)
        before = jnp.where(i > 0, pb_ref[...], 0.0)
        ext = jnp.concatenate([before, pm_ref[...]], axis=0)
        cw = cw_ref[...]
        z = ext[:, 2 * A_WIDTH:3 * A_WIDTH] * ext[:, 0:A_WIDTH]
        cz = _conv3(z, cw)
        ya = pm_ref[:, A_WIDTH:2 * A_WIDTH] * cz[HALO:]
        o_ref[:, 0:A_WIDTH] = ya.astype(BF16)
        for g, k in enumerate(POOL_WINDOWS):
            lo = 3 * A_WIDTH + g * POOL_GROUP
            p = ext[:, lo:lo + POOL_GROUP]
            w = p
            s = 1
            while s < k:
                w = w + _shift_down(w, s)
                s *= 2
            pooled = w / _window_count(i * ts - HALO, n, k) - p
            yb = jnp.dot(pooled[HALO:].astype(BF16), pw_ref[g], preferred_element_type=F32)
            yb = yb * ps_ref[:, g * POOL_GROUP:(g + 1) * POOL_GROUP]
            o_ref[:, A_WIDTH + g * POOL_GROUP:A_WIDTH + (g + 1) * POOL_GROUP] = yb.astype(BF16)

    hb = ts // HALO
    return pl.pallas_call(
        body, name=name, grid=(S // ts,),
        in_specs=[
            pl.BlockSpec((ts, EVEN_IN), lambda i: (i, 0)),
            pl.BlockSpec((HALO, EVEN_IN), lambda i: (jnp.maximum(i * hb - 1, 0), 0)),
            pl.BlockSpec((3, A_WIDTH), lambda i: (0, 0)),
            pl.BlockSpec((4, POOL_GROUP, POOL_GROUP), lambda i: (0, 0, 0)),
            pl.BlockSpec((1, 4 * POOL_GROUP), lambda i: (0, 0)),
        ],
        out_specs=pl.BlockSpec((ts, D_MODEL), lambda i: (i, 0)),
        out_shape=_sds((S, D_MODEL), BF16), compiler_params=_params("parallel"),
    )(proj, proj, conv_w, pool_w, pool_scale)


def mixer_bwd(proj, dmix, conv_w, pool_w, pool_scale, name, ts=256):
    S = proj.shape[0]
    n = ts + 2 * HALO
    nt = S // ts
    tn_dims = (((0,), (0,)), ((), ()))
    nt_dims = (((1,), (1,)), ((), ()))

    def body(pm_ref, pb_ref, pa_ref, dm_ref, da_ref, cw_ref, pw_ref, ps_ref, o_ref, dcw_ref, dpw_ref, dps_ref):
        i = pl.program_id(0)
        last = i == nt - 1
        before = jnp.where(i > 0, pb_ref[...], 0.0)
        after = jnp.where(last, 0.0, pa_ref[...])
        ext = jnp.concatenate([before, pm_ref[...], after], axis=0)
        dafter = jnp.where(last, 0.0, da_ref[...])
        dext = jnp.concatenate([jnp.zeros((HALO, D_MODEL), F32), dm_ref[...], dafter], axis=0)
        cw = cw_ref[...]
        main = slice(HALO, HALO + ts)

        @pl.when(i == 0)
        def _():
            dcw_ref[...] = jnp.zeros_like(dcw_ref)
            dpw_ref[...] = jnp.zeros_like(dpw_ref)
            dps_ref[...] = jnp.zeros_like(dps_ref)

        h, gb, gc = ext[:, 0:A_WIDTH], ext[:, A_WIDTH:2 * A_WIDTH], ext[:, 2 * A_WIDTH:3 * A_WIDTH]
        z = gc * h
        z1, z2 = _shift_down(z, 1), _shift_down(z, 2)
        cz = (z2 * cw[0:1] + z1 * cw[1:2]) + z * cw[2:3]
        dya = dext[:, 0:A_WIDTH]
        dcz = dya * gb
        dz = dcz * cw[2:3] + _shift_up(dcz, 1) * cw[1:2] + _shift_up(dcz, 2) * cw[0:1]
        o_ref[:, 0:A_WIDTH] = (dz * gc)[main].astype(BF16)
        o_ref[:, A_WIDTH:2 * A_WIDTH] = (dya * cz)[main].astype(BF16)
        o_ref[:, 2 * A_WIDTH:3 * A_WIDTH] = (dz * h)[main].astype(BF16)
        dczm = dcz[main]
        dcw_ref[0:1, :] += jnp.sum(dczm * z2[main], axis=0, keepdims=True)
        dcw_ref[1:2, :] += jnp.sum(dczm * z1[main], axis=0, keepdims=True)
        dcw_ref[2:3, :] += jnp.sum(dczm * z[main], axis=0, keepdims=True)

        for g, k in enumerate(POOL_WINDOWS):
            lo = 3 * A_WIDTH + g * POOL_GROUP
            cols = slice(g * POOL_GROUP, (g + 1) * POOL_GROUP)
            p = ext[:, lo:lo + POOL_GROUP]
            w = p
            s = 1
            while s < k:
                w = w + _shift_down(w, s)
                s *= 2
            cnt = _window_count(i * ts - HALO, n, k)
            pooled = (w / cnt - p)[main].astype(BF16)
            dyb = dext[:, A_WIDTH + g * POOL_GROUP:A_WIDTH + (g + 1) * POOL_GROUP]
            e = dyb * ps_ref[:, cols]
            pre = jnp.dot(pooled, pw_ref[g], preferred_element_type=F32)
            dps_ref[:, cols] += jnp.sum(dyb[main] * pre, axis=0, keepdims=True)
            dpw_ref[g] += lax.dot_general(pooled, e[main].astype(BF16), tn_dims, preferred_element_type=F32)
            dpooled = lax.dot_general(e.astype(BF16), pw_ref[g], nt_dims, preferred_element_type=F32)
            q = dpooled / cnt
            a = q
            s = 1
            while s < k:
                a = a + _shift_up(a, s)
                s *= 2
            o_ref[:, lo:lo + POOL_GROUP] = (a - dpooled)[main].astype(BF16)

    hb = ts // HALO
    nh = S // HALO
    before_map = lambda i: (jnp.maximum(i * hb - 1, 0), 0)
    after_map = lambda i: (jnp.minimum((i + 1) * hb, nh - 1), 0)
    full = lambda *shape: pl.BlockSpec(shape, lambda i: (0,) * len(shape))
    return pl.pallas_call(
        body, name=name, grid=(nt,),
        in_specs=[
            pl.BlockSpec((ts, EVEN_IN), lambda i: (i, 0)),
            pl.BlockSpec((HALO, EVEN_IN), before_map),
            pl.BlockSpec((HALO, EVEN_IN), after_map),
            pl.BlockSpec((ts, D_MODEL), lambda i: (i, 0)),
            pl.BlockSpec((HALO, D_MODEL), after_map),
            full(3, A_WIDTH), full(4, POOL_GROUP, POOL_GROUP), full(1, 4 * POOL_GROUP),
        ],
        out_specs=[pl.BlockSpec((ts, EVEN_IN), lambda i: (i, 0)), full(3, A_WIDTH), full(4, POOL_GROUP, POOL_GROUP),
                   full(1, 4 * POOL_GROUP)],
        out_shape=[_sds((S, EVEN_IN), BF16), _sds((3, A_WIDTH), F32), _sds((4, POOL_GROUP, POOL_GROUP), F32),
                   _sds((1, 4 * POOL_GROUP), F32)],
        compiler_params=_params("arbitrary"),
    )(proj, proj, proj, dmix, dmix, conv_w, pool_w, pool_scale)


FFN_HALO = 8
FFN_TC = 1408


def glu_fwd(up, conv_w, conv_b, name, ts=256):
    S = up.shape[0]
    nc = D_FF // FFN_TC

    def body(gm_ref, gb_ref, um_ref, ub_ref, cwg_ref, cwu_ref, cbg_ref, cbu_ref, o_ref):
        i = pl.program_id(0)

        def conv(m_ref, b_ref, cw_ref, cb_ref):
            before = jnp.where(i > 0, b_ref[...], 0.0)
            ext = jnp.concatenate([before, m_ref[...]], axis=0)
            return _conv3(ext, cw_ref[...])[FFN_HALO:] + cb_ref[...]

        gate = conv(gm_ref, gb_ref, cwg_ref, cbg_ref)
        upv = conv(um_ref, ub_ref, cwu_ref, cbu_ref)
        o_ref[...] = ((gate * (1.0 / (1.0 + jnp.exp(-gate)))) * upv).astype(BF16)

    hb = ts // FFN_HALO
    main = lambda off: pl.BlockSpec((ts, FFN_TC), lambda i, c: (i, c + off))
    halo = lambda off: pl.BlockSpec((FFN_HALO, FFN_TC), lambda i, c: (jnp.maximum(i * hb - 1, 0), c + off))
    cw = lambda off: pl.BlockSpec((3, FFN_TC), lambda i, c: (0, c + off))
    cb = lambda off: pl.BlockSpec((1, FFN_TC), lambda i, c: (0, c + off))
    return pl.pallas_call(
        body, name=name, grid=(S // ts, nc),
        in_specs=[main(0), halo(0), main(nc), halo(nc), cw(0), cw(nc), cb(0), cb(nc)],
        out_specs=pl.BlockSpec((ts, FFN_TC), lambda i, c: (i, c)),
        out_shape=_sds((S, D_FF), BF16), compiler_params=_params("parallel", "parallel"),
    )(up, up, up, up, conv_w, conv_w, conv_b, conv_b)


def glu_bwd(up, da, conv_w, conv_b, name, ts=256):
    S = up.shape[0]
    nc = D_FF // FFN_TC
    nt = S // ts
    main = slice(FFN_HALO, FFN_HALO + ts)
    W = 2 * D_FF

    def body(xm_ref, xb_ref, xa_ref, dm_ref, da_ref, cw_ref, cb_ref, dx_ref, dcw_ref, dcb_ref):
        i = pl.program_id(0)
        last = i == nt - 1

        @pl.when(i == 0)
        def _():
            dcw_ref[...] = jnp.zeros_like(dcw_ref)
            dcb_ref[...] = jnp.zeros_like(dcb_ref)

        def ext_of(cols):
            before = jnp.where(i > 0, xb_ref[:, cols], 0.0)
            return jnp.concatenate([before, xm_ref[:, cols], xa_ref[:, cols]], axis=0)

        def back(x, d, cols):
            cw = cw_ref[:, cols]
            dx = d * cw[2:3] + _shift_up(d, 1) * cw[1:2] + _shift_up(d, 2) * cw[0:1]
            dx_ref[:, cols] = dx[main].astype(BF16)
            dmn = d[main]
            dcb_ref[:, cols] += jnp.sum(dmn, axis=0, keepdims=True)
            dcw_ref[0:1, cols] += jnp.sum(dmn * _shift_down(x, 2)[main], axis=0, keepdims=True)
            dcw_ref[1:2, cols] += jnp.sum(dmn * _shift_down(x, 1)[main], axis=0, keepdims=True)
            dcw_ref[2:3, cols] += jnp.sum(dmn * x[main], axis=0, keepdims=True)

        for c in range(nc):
            gcols = slice(c * FFN_TC, (c + 1) * FFN_TC)
            ucols = slice(D_FF + c * FFN_TC, D_FF + (c + 1) * FFN_TC)
            xg, xu = ext_of(gcols), ext_of(ucols)
            ug = _conv3(xg, cw_ref[:, gcols]) + cb_ref[:, gcols]
            uu = _conv3(xu, cw_ref[:, ucols]) + cb_ref[:, ucols]
            dafter = jnp.where(last, 0.0, da_ref[:, gcols].astype(F32))
            dae = jnp.concatenate([jnp.zeros((FFN_HALO, FFN_TC), F32), dm_ref[:, gcols].astype(F32), dafter], axis=0)
            sg = 1.0 / (1.0 + jnp.exp(-ug))
            duu = dae * (ug * sg)
            dug = (dae * uu) * (sg * (1.0 + ug * (1.0 - sg)))
            back(xg, dug, gcols)
            back(xu, duu, ucols)

    hb = ts // FFN_HALO
    nh = S // FFN_HALO
    before_map = lambda i: (jnp.maximum(i * hb - 1, 0), 0)
    after_map = lambda i: (jnp.minimum((i + 1) * hb, nh - 1), 0)
    return pl.pallas_call(
        body, name=name, grid=(nt,),
        in_specs=[pl.BlockSpec((ts, W), lambda i: (i, 0)), pl.BlockSpec((FFN_HALO, W), before_map),
                  pl.BlockSpec((FFN_HALO, W), after_map), pl.BlockSpec((ts, D_FF), lambda i: (i, 0)),
                  pl.BlockSpec((FFN_HALO, D_FF), after_map), pl.BlockSpec((3, W), lambda i: (0, 0)),
                  pl.BlockSpec((1, W), lambda i: (0, 0))],
        out_specs=[pl.BlockSpec((ts, W), lambda i: (i, 0)), pl.BlockSpec((3, W), lambda i: (0, 0)),
                   pl.BlockSpec((1, W), lambda i: (0, 0))],
        out_shape=[_sds((S, W), BF16), _sds((3, W), F32), _sds((1, W), F32)],
        compiler_params=_params("arbitrary"),
    )(up, up, up, da, da, conv_w, conv_b)


def _head_mean_matrix():
    h = np.arange(D_MODEL) // HEAD_DIM
    return jnp.asarray((h[:, None] == h[None, :]).astype(np.float32) / HEAD_DIM, dtype=BF16)


def _head_mean(v, gm):
    hi = v.astype(BF16)
    lo = (v - hi.astype(F32)).astype(BF16)
    return jnp.dot(hi, gm, preferred_element_type=F32) + jnp.dot(lo, gm, preferred_element_type=F32)


def qknorm_fwd(qkv, gqk, name, ts=512):
    S = qkv.shape[0]

    def body(x_ref, g_ref, gm_ref, o_ref):
        part = pl.program_id(0)
        x = x_ref[...]

        @pl.when(part < 2)
        def _():
            r = lax.rsqrt(_head_mean(x * x, gm_ref[...]) + EPS)
            o_ref[...] = ((x * r) * g_ref[...]).astype(BF16)

        @pl.when(part == 2)
        def _():
            o_ref[...] = x.astype(BF16)

    return pl.pallas_call(
        body, name=name, grid=(3, S // ts),
        in_specs=[pl.BlockSpec((ts, D_MODEL), lambda p, i: (i, p)), pl.BlockSpec((None, 1, D_MODEL), lambda p, i: (p, 0, 0)),
                  pl.BlockSpec((D_MODEL, D_MODEL), lambda p, i: (0, 0))],
        out_specs=pl.BlockSpec((ts, D_MODEL), lambda p, i: (i, p)),
        out_shape=_sds((S, 3 * D_MODEL), BF16), compiler_params=_params("parallel", "parallel"),
    )(qkv, gqk, _head_mean_matrix())


def qknorm_bwd(qkv, dq, dk, dv, gqk, name, ts=256):
    S = qkv.shape[0]

    def body(x_ref, dq_ref, dk_ref, dv_ref, g_ref, gm_ref, o_ref, dg_ref):
        @pl.when(pl.program_id(0) == 0)
        def _():
            dg_ref[...] = jnp.zeros_like(dg_ref)

        gm = gm_ref[...]
        for part, d_ref in enumerate((dq_ref, dk_ref)):
            cols = slice(part * D_MODEL, (part + 1) * D_MODEL)
            x = x_ref[:, cols]
            d = d_ref[...]
            r = lax.rsqrt(_head_mean(x * x, gm) + EPS)
            gx = d * g_ref[part]
            o_ref[:, cols] = (r * gx - x * ((r * r * r) * _head_mean(gx * x, gm))).astype(BF16)
            dg_ref[part] += jnp.sum(d * (x * r), axis=0, keepdims=True)
        o_ref[:, 2 * D_MODEL:] = dv_ref[...].astype(BF16)

    row = pl.BlockSpec((ts, D_MODEL), lambda i: (i, 0))
    wide = pl.BlockSpec((ts, 3 * D_MODEL), lambda i: (i, 0))
    gains = pl.BlockSpec((3, 1, D_MODEL), lambda i: (0, 0, 0))
    return pl.pallas_call(
        body, name=name, grid=(S // ts,),
        in_specs=[wide, row, row, row, gains, pl.BlockSpec((D_MODEL, D_MODEL), lambda i: (0, 0))],
        out_specs=[wide, gains],
        out_shape=[_sds((S, 3 * D_MODEL), BF16), _sds((3, 1, D_MODEL), F32)],
        compiler_params=_params("arbitrary"),
    )(qkv, dq, dk, dv, gqk, _head_mean_matrix())


def _bucket_tables():
    n = ATT_BLOCK
    a = np.arange(n)[:, None]
    c = np.arange(2 * n)[None, :]
    rel = a + n - c
    band = (rel >= 0) & (rel <= n)
    max_exact = N_REL_BUCKETS // 2
    buckets, valids = [], []
    for _, dil in DILATED_PAIRS:
        dist = np.clip(rel, 0, n) * dil
        dd = np.maximum(dist, 1).astype(np.float32)
        large = max_exact + (np.log(dd / np.float32(max_exact)) / np.float32(math.log(REL_MAX_DISTANCE / max_exact))
                             * np.float32(N_REL_BUCKETS - max_exact)).astype(np.int32)
        large = np.minimum(large, N_REL_BUCKETS - 1)
        buckets.append(np.where(dist < max_exact, dist, large).reshape(1, -1))
        valids.append(np.stack([(band & (c >= n)).reshape(1, -1), band.reshape(1, -1)]))
    return np.stack(buckets).astype(np.int32), np.stack(valids).astype(np.int32)


BIAS_CHUNK = 8192


def _split3(x):
    a = x.astype(BF16)
    r = x - a.astype(F32)
    b = r.astype(BF16)
    c = (r - b.astype(F32)).astype(BF16)
    return a, b, c


def bias_expand(rel_bias_t, name):
    bucket, valid = _bucket_tables()
    nq = bucket.shape[-1]

    def body(t_ref, b_ref, v_ref, o_ref):
        onehot = (lax.broadcasted_iota(jnp.int32, (N_REL_BUCKETS, BIAS_CHUNK), 0) == b_ref[...]).astype(BF16)
        acc = None
        for term in _split3(t_ref[...]):
            p = jnp.dot(term, onehot, preferred_element_type=F32)
            acc = p if acc is None else acc + p
        o_ref[...] = jnp.where(v_ref[...] > 0, acc, MASK_VALUE)

    return pl.pallas_call(
        body, name=name, grid=(3, 2, nq // BIAS_CHUNK),
        in_specs=[pl.BlockSpec((N_HEADS, N_REL_BUCKETS), lambda b, v, c: (0, 0)),
                  pl.BlockSpec((None, 1, BIAS_CHUNK), lambda b, v, c: (b, 0, c)),
                  pl.BlockSpec((None, None, 1, BIAS_CHUNK), lambda b, v, c: (b, v, 0, c))],
        out_specs=pl.BlockSpec((None, None, N_HEADS, BIAS_CHUNK), lambda b, v, c: (b, v, 0, c)),
        out_shape=_sds((3, 2, N_HEADS, nq), F32), compiler_params=_params("parallel", "parallel", "parallel"),
    )(rel_bias_t, jnp.asarray(bucket), jnp.asarray(valid))


def bias_reduce(dbias, name):
    bucket, _ = _bucket_tables()
    nq = bucket.shape[-1]
    dims = (((1,), (1,)), ((), ()))

    def body(d_ref, b_ref, o_ref):
        onehot = (lax.broadcasted_iota(jnp.int32, (N_REL_BUCKETS, BIAS_CHUNK), 0) == b_ref[...]).astype(BF16)
        acc = None
        for term in _split3(d_ref[...]):
            p = lax.dot_general(term, onehot, dims, preferred_element_type=F32)
            acc = p if acc is None else acc + p

        @pl.when(pl.program_id(1) == 0)
        def _():
            o_ref[...] = acc

        @pl.when(pl.program_id(1) > 0)
        def _():
            o_ref[...] += acc

    return pl.pallas_call(
        body, name=name, grid=(3, nq // BIAS_CHUNK),
        in_specs=[pl.BlockSpec((None, N_HEADS, BIAS_CHUNK), lambda b, c: (b, 0, c)),
                  pl.BlockSpec((None, 1, BIAS_CHUNK), lambda b, c: (b, 0, c))],
        out_specs=pl.BlockSpec((None, N_HEADS, N_REL_BUCKETS), lambda b, c: (b, 0, 0)),
        out_shape=_sds((3, N_HEADS, N_REL_BUCKETS), F32), compiler_params=_params("parallel", "arbitrary"),
    )(dbias, jnp.asarray(bucket))


PAIR = 2 * HEAD_DIM
N_PAIRS = N_HEADS // 2
_NT = (((1,), (1,)), ((), ()))
_TN = (((0,), (0,)), ((), ()))


def _low_lanes(shape):
    return lax.broadcasted_iota(jnp.int32, shape, 1) < HEAD_DIM


def _per_head(x, low):
    del low
    return x[:, 0:1], x[:, HEAD_DIM:HEAD_DIM + 1]


def attn_fwd_branch(qkvn, bias, branch, state, last, name):
    S = qkvn.shape[0]
    dil = DILATED_PAIRS[branch][1]
    L = S // dil
    nb = L // ATT_BLOCK
    view = qkvn.reshape(L, dil * 3 * D_MODEL)
    first = state is None

    def body(*refs):
        q_ref, kp_ref, kc_ref, vp_ref, vc_ref, b_ref = refs[:6]
        ins = refs[6:6 + (0 if first else 3)]
        outs = refs[6 + len(ins):]
        low = _low_lanes((ATT_BLOCK, PAIR))
        for hp in range(N_PAIRS):
            cols = slice(hp * PAIR, (hp + 1) * PAIR)
            q = q_ref[:, cols]
            k = jnp.concatenate([kp_ref[:, cols], kc_ref[:, cols]], axis=0)
            v = jnp.concatenate([vp_ref[:, cols], vc_ref[:, cols]], axis=0)
            pv, mx, den = [], [], []
            for hh in range(2):
                qh = jnp.where(low if hh == 0 else ~low, q, jnp.zeros_like(q))
                s = lax.dot_general(qh, k, _NT, preferred_element_type=F32) * (HEAD_DIM ** -0.5) + b_ref[2 * hp + hh]
                m = jnp.max(s, axis=-1, keepdims=True)
                p = jnp.exp(s - m)
                den.append(jnp.sum(p, axis=-1, keepdims=True))
                mx.append(m)
                pv.append(jnp.dot(p.astype(BF16), v, preferred_element_type=F32))
            acc = jnp.where(low, pv[0], pv[1])
            m = jnp.where(low, mx[0], mx[1])
            l = jnp.where(low, den[0], den[1])
            if not first:
                m_old = ins[1][:, cols]
                m_new = jnp.maximum(m_old, m)
                a_old, a_new = jnp.exp(m_old - m_new), jnp.exp(m - m_new)
                acc = ins[0][:, cols] * a_old + acc * a_new
                l = ins[2][:, cols] * a_old + l * a_new
                m = m_new
            if last:
                outs[0][:, cols] = (acc / l).astype(BF16)
                outs[1][:, cols] = m + jnp.log(l)
            else:
                outs[0][:, cols] = acc
                outs[1][:, cols] = m
                outs[2][:, cols] = l

    blk = lambda part, prev: pl.BlockSpec(
        (ATT_BLOCK, D_MODEL), lambda r, b: (jnp.maximum(b - 1, 0) if prev else b, 3 * r + part))
    st = pl.BlockSpec((ATT_BLOCK, D_MODEL), lambda r, b: (b, r))
    in_specs = [blk(0, False), blk(1, True), blk(1, False), blk(2, True), blk(2, False),
                pl.BlockSpec((None, N_HEADS, ATT_BLOCK, 2 * ATT_BLOCK), lambda r, b: (jnp.minimum(b, 1), 0, 0, 0))]
    args = [view] * 5 + [bias]
    if not first:
        in_specs += [st] * 3
        args += [s_.reshape(L, dil * D_MODEL) for s_ in state]
    wide = (L, dil * D_MODEL)
    if last:
        out_shape = [_sds(wide, BF16), _sds(wide, F32)]
    else:
        out_shape = [_sds(wide, F32)] * 3
    outs = pl.pallas_call(
        body, name=name, grid=(dil, nb), in_specs=in_specs, out_specs=[st] * len(out_shape), out_shape=out_shape,
        compiler_params=_params("parallel", "arbitrary"),
    )(*args)
    return tuple(o.reshape(S, D_MODEL) for o in outs)


def attn_bwd_branch(qkvn, o, do, lse, bias, branch, grads, name):
    S = qkvn.shape[0]
    dil = DILATED_PAIRS[branch][1]
    L = S // dil
    nb = L // ATT_BLOCK
    view = qkvn.reshape(L, dil * 3 * D_MODEL)
    first = grads is None
    n_in = 0 if first else 3

    def body(*refs):
        q_ref, kp_ref, kc_ref, vp_ref, vc_ref, o_ref, do_ref, lse_ref, b_ref = refs[:9]
        ins = refs[9:9 + n_in]
        dq_ref, dk_ref, dv_ref, db_ref, ck_ref, cv_ref = refs[9 + n_in:]
        b = pl.program_id(1)
        low = _low_lanes((ATT_BLOCK, PAIR))

        @pl.when((pl.program_id(0) == 0) & (b == 0))
        def _():
            db_ref[...] = jnp.zeros_like(db_ref)

        @pl.when(b == 0)
        def _():
            ck_ref[...] = jnp.zeros_like(ck_ref)
            cv_ref[...] = jnp.zeros_like(cv_ref)

        @pl.when(b < nb)
        def _():
            for hp in range(N_PAIRS):
                cols = slice(hp * PAIR, (hp + 1) * PAIR)
                q = q_ref[:, cols]
                k = jnp.concatenate([kp_ref[:, cols], kc_ref[:, cols]], axis=0)
                v = jnp.concatenate([vp_ref[:, cols], vc_ref[:, cols]], axis=0)
                dout = do_ref[:, cols]
                prod = dout.astype(F32) * o_ref[:, cols].astype(F32)
                lse_h = _per_head(lse_ref[:, cols], low)
                dq, dk, dv = [], None, None
                for hh in range(2):
                    keep = low if hh == 0 else ~low
                    qh = jnp.where(keep, q, jnp.zeros_like(q))
                    doh = jnp.where(keep, dout, jnp.zeros_like(dout))
                    delta = jnp.sum(jnp.where(keep, prod, 0.0), axis=-1, keepdims=True)
                    s = lax.dot_general(qh, k, _NT, preferred_element_type=F32) * (HEAD_DIM ** -0.5) + b_ref[2 * hp + hh]
                    p = jnp.exp(s - lse_h[hh])
                    dp = lax.dot_general(doh, v, _NT, preferred_element_type=F32)
                    ds = p * (dp - delta)
                    db_ref[2 * hp + hh] += ds
                    dsb = (ds * (HEAD_DIM ** -0.5)).astype(BF16)
                    dq.append(jnp.dot(dsb, k, preferred_element_type=F32))
                    dkh = lax.dot_general(dsb, qh, _TN, preferred_element_type=F32)
                    dvh = lax.dot_general(p.astype(BF16), doh, _TN, preferred_element_type=F32)
                    dk = dkh if dk is None else dk + dkh
                    dv = dvh if dv is None else dv + dvh
                dq_new = jnp.where(low, dq[0], dq[1])
                dk_prev = ck_ref[:, cols] + dk[:ATT_BLOCK]
                dv_prev = cv_ref[:, cols] + dv[:ATT_BLOCK]
                if not first:
                    dq_new = dq_new + ins[0][:, cols]
                    dk_prev = dk_prev + ins[1][:, cols]
                    dv_prev = dv_prev + ins[2][:, cols]
                dq_ref[:, cols] = dq_new
                dk_ref[:, cols] = dk_prev
                dv_ref[:, cols] = dv_prev
                ck_ref[:, cols] = dk[ATT_BLOCK:]
                cv_ref[:, cols] = dv[ATT_BLOCK:]

        @pl.when(b == nb)
        def _():
            dk_last, dv_last = ck_ref[...], cv_ref[...]
            if not first:
                dk_last = dk_last + ins[1][...]
                dv_last = dv_last + ins[2][...]
            dk_ref[...] = dk_last
            dv_ref[...] = dv_last

    cur = lambda b: jnp.minimum(b, nb - 1)
    prev = lambda b: jnp.maximum(b - 1, 0)
    blk = lambda part, use_prev: pl.BlockSpec(
        (ATT_BLOCK, D_MODEL), lambda r, b: (prev(cur(b)) if use_prev else cur(b), 3 * r + part))
    qside = pl.BlockSpec((ATT_BLOCK, D_MODEL), lambda r, b: (cur(b), r))
    kside = pl.BlockSpec((ATT_BLOCK, D_MODEL), lambda r, b: (prev(b), r))
    bias_spec = pl.BlockSpec((None, N_HEADS, ATT_BLOCK, 2 * ATT_BLOCK), lambda r, b: (jnp.minimum(b, 1), 0, 0, 0))
    wide = (L, dil * D_MODEL)
    in_specs = [blk(0, False), blk(1, True), blk(1, False), blk(2, True), blk(2, False), qside, qside, qside, bias_spec]
    args = [view] * 5 + [o.reshape(wide), do.reshape(wide), lse.reshape(wide), bias]
    aliases = {}
    if not first:
        in_specs += [qside, kside, kside]
        args += [g.reshape(wide) for g in grads]
        aliases = {9: 0, 10: 1, 11: 2}
    dq, dk, dv, db = pl.pallas_call(
        body, name=name, grid=(dil, nb + 1), in_specs=in_specs,
        out_specs=[qside, kside, kside, pl.BlockSpec((N_HEADS, ATT_BLOCK, 2 * ATT_BLOCK), lambda r, b: (0, 0, 0))],
        out_shape=[_sds(wide, F32)] * 3 + [_sds((N_HEADS, ATT_BLOCK, 2 * ATT_BLOCK), F32)],
        scratch_shapes=[pltpu.VMEM((ATT_BLOCK, D_MODEL), F32), pltpu.VMEM((ATT_BLOCK, D_MODEL), F32)],
        input_output_aliases=aliases, compiler_params=_params("arbitrary", "arbitrary"),
    )(*args)
    return (dq.reshape(S, D_MODEL), dk.reshape(S, D_MODEL), dv.reshape(S, D_MODEL)), db


def loss_grad(y, target, name, ts=512):
    S, Dm = y.shape

    def body(y_ref, t_ref, d_ref, db_ref, s_ref):
        e = y_ref[...] - t_ref[...]
        d = e * (1.0 / Dm)
        d_ref[...] = d
        db_ref[...] = d.astype(BF16)
        part = jnp.sum(e * e, axis=0, keepdims=True)

        @pl.when(pl.program_id(0) == 0)
        def _():
            s_ref[...] = part

        @pl.when(pl.program_id(0) > 0)
        def _():
            s_ref[...] += part

    row = pl.BlockSpec((ts, Dm), lambda i: (i, 0))
    vec = pl.BlockSpec((1, Dm), lambda i: (0, 0))
    return pl.pallas_call(
        body, name=name, grid=(S // ts,), in_specs=[row, row], out_specs=[row, row, vec],
        out_shape=[_sds((S, Dm), F32), _sds((S, Dm), BF16), _sds((1, Dm), F32)], compiler_params=_params("arbitrary"),
    )(y, target)


def adamw(w, g, m, v, name):
    n, R, C = w.shape

    def body(w_ref, g_ref, m_ref, v_ref, d_ref, nm_ref, nv_ref):
        gv = g_ref[...]
        m2 = ADAM_B1 * m_ref[...] + (1.0 - ADAM_B1) * gv
        v2 = ADAM_B2 * v_ref[...] + (1.0 - ADAM_B2) * (gv * gv)
        m_hat = m2 / (1.0 - ADAM_B1 ** ADAM_STEP)
        v_hat = v2 / (1.0 - ADAM_B2 ** ADAM_STEP)
        d_ref[...] = -ADAM_LR * (m_hat / (jnp.sqrt(v_hat) + ADAM_EPS) + ADAM_WD * w_ref[...])
        nm_ref[...] = m2
        nv_ref[...] = v2

    tr = R
    while tr * C * 4 > (1 << 21) and tr % 16 == 0:
        tr //= 2
    spec = pl.BlockSpec((None, tr, C), lambda i, r: (i, r, 0))
    return pl.pallas_call(
        body, name=name, grid=(n, R // tr), in_specs=[spec] * 4, out_specs=[spec] * 3,
        out_shape=[_sds((n, R, C), F32)] * 3, compiler_params=_params("parallel", "parallel"),
    )(w, g, m, v)


def pair_sum(g, recv, core, name):
    _, _, hR, C = g.shape

    def body(c_ref, g_ref, r_ref, o_ref):
        del c_ref
        o_ref[...] = (g_ref[...] + r_ref[...]).astype(BF16)

    grid_spec = pltpu.PrefetchScalarGridSpec(
        num_scalar_prefetch=1, grid=(N_CHIPS,),
        in_specs=[pl.BlockSpec((None, None, hR, C), lambda j, c: (j, c[0], 0, 0)),
                  pl.BlockSpec((None, hR, C), lambda j, c: (j, 0, 0))],
        out_specs=pl.BlockSpec((None, hR, C), lambda j, c: (j, 0, 0)))
    return pl.pallas_call(body, name=name, grid_spec=grid_spec, out_shape=_sds((N_CHIPS, hR, C), BF16),
                          compiler_params=_params("parallel"))(core, g, recv)


def sum_slots(parts, name, out_dtype=F32):
    n, R, C = parts.shape
    tr = R
    while tr * C * n * parts.dtype.itemsize > (1 << 22) and tr % 32 == 0:
        tr //= 2

    def body(p_ref, o_ref):
        acc = p_ref[0].astype(F32)
        for q in range(1, n):
            acc = acc + p_ref[q].astype(F32)
        o_ref[...] = acc.astype(out_dtype)

    return pl.pallas_call(
        body, name=name, grid=(R // tr,), in_specs=[pl.BlockSpec((n, tr, C), lambda i: (0, i, 0))],
        out_specs=pl.BlockSpec((tr, C), lambda i: (i, 0)), out_shape=_sds((R, C), out_dtype),
        compiler_params=_params("parallel"),
    )(parts)


ANY = pl.BlockSpec(memory_space=pl.ANY)


def _coords():
    return lax.axis_index("x"), lax.axis_index("y"), lax.axis_index("c")


def _other_chips(mx, my):
    return [(1 - mx, my), (mx, 1 - my), (1 - mx, 1 - my)]


def _remote(src, dst, send, recv, dev):
    return pltpu.make_async_remote_copy(src_ref=src, dst_ref=dst, send_sem=send, recv_sem=recv, device_id=dev,
                                        device_id_type=MESH)


def allgather_devices(x, name):
    R, C = x.shape

    def body(x_ref, o_ref, send, recv, local_sem):
        mx, my, mc = _coords()
        me = 4 * mx + 2 * my + mc
        local = pltpu.make_async_copy(x_ref, o_ref.at[me], local_sem)
        local.start()
        peers = []
        for k in range(1, N_DEV):
            px = 1 - mx if k & 4 else mx
            py = 1 - my if k & 2 else my
            pc = 1 - mc if k & 1 else mc
            peers.append((px, py, pc))
        sends = [_remote(x_ref, o_ref.at[me], send.at[k], recv.at[k], p) for k, p in enumerate(peers)]
        for cp in sends:
            cp.start()
        for k, (px, py, pc) in enumerate(peers):
            _remote(x_ref, o_ref.at[4 * px + 2 * py + pc], send.at[k], recv.at[k], (px, py, pc)).wait_recv()
        for cp in sends:
            cp.wait_send()
        local.wait()

    return pl.pallas_call(
        body, name=name, in_specs=[ANY], out_specs=ANY, out_shape=_sds((N_DEV, R, C), x.dtype),
        scratch_shapes=[pltpu.SemaphoreType.DMA((N_DEV - 1,)), pltpu.SemaphoreType.DMA((N_DEV - 1,)),
                        pltpu.SemaphoreType.DMA],
    )(x)


HBM = pl.BlockSpec(memory_space=pltpu.HBM)
SEM = pl.BlockSpec(memory_space=pltpu.SEMAPHORE)
_SPLIT_COPY = pltpu.CompilerParams(has_side_effects=pltpu.SideEffectType.DATAFLOW_SIDE_EFFECTING)


def _in_hbm(a):
    return pltpu.with_memory_space_constraint(a, pltpu.HBM)


def cast_into_slot(w, chip_core, name):
    _, hR, C = w.shape

    def body(s_ref, w_ref, o_ref):
        del s_ref
        o_ref[...] = w_ref[...].astype(BF16)

    grid_spec = pltpu.PrefetchScalarGridSpec(
        num_scalar_prefetch=1, grid=(2,),
        in_specs=[pl.BlockSpec((None, hR, C), lambda h, s: (h, 0, 0))],
        out_specs=pl.BlockSpec((None, None, hR, C), lambda h, s: (s[0], h, 0, 0)))
    return pl.pallas_call(body, name=name, grid_spec=grid_spec, out_shape=_sds((N_CHIPS, 2, hR, C), BF16),
                          compiler_params=_params("parallel"))(chip_core, w)


def gather_start(lands, groups, name):
    n = len(lands)
    n_groups = len(groups)

    def body(*refs):
        ins = refs[:n]
        sems = refs[n:n + 2 * n_groups]
        token = refs[-1]
        mx, my, mc = _coords()
        chip = 2 * mx + my
        for g, members in enumerate(groups):
            send, recv = sems[2 * g], sems[2 * g + 1]
            for i, a in enumerate(members):
                mine = ins[a].at[chip, mc]
                for k, (px, py) in enumerate(_other_chips(mx, my)):
                    for pc in range(2):
                        _remote(mine, mine, send.at[6 * i + 2 * k + pc], recv.at[6 * i + 2 * k + mc], (px, py, pc)).start()
        token[...] = jnp.zeros_like(token)

    sem_shapes = []
    for members in groups:
        sem_shapes += [pltpu.SemaphoreType.DMA((6 * len(members),))] * 2
    outs = pl.pallas_call(
        body, name=name, in_specs=[HBM] * n,
        out_specs=[SEM] * (2 * n_groups) + [HBM] * n + [pl.BlockSpec(memory_space=pltpu.VMEM)],
        out_shape=sem_shapes + [pltpu.HBM(a.shape, a.dtype) for a in lands] + [_sds((SUBLANES, LANES), F32)],
        input_output_aliases={a: 2 * n_groups + a for a in range(n)}, compiler_params=_SPLIT_COPY,
    )(*[_in_hbm(a) for a in lands])
    sems = [(outs[2 * g], outs[2 * g + 1]) for g in range(n_groups)]
    return sems, list(outs[2 * n_groups:2 * n_groups + n]), outs[-1]


def gather_wait(lands, sems, after, name):
    n = len(lands)

    def body(*refs):
        ins = refs[:n]
        send, recv = refs[n], refs[n + 1]
        mx, my, _ = _coords()
        for i in range(n):
            for k, (px, py) in enumerate(_other_chips(mx, my)):
                for pc in range(2):
                    slot = ins[i].at[2 * px + py, pc]
                    cp = _remote(slot, slot, send.at[6 * i + 2 * k + pc], recv.at[6 * i + 2 * k + pc], (px, py, pc))
                    cp.wait_send()
                    cp.wait_recv()

    outs = pl.pallas_call(
        body, name=name, in_specs=[HBM] * n + [SEM, SEM, ANY], out_specs=[HBM] * n,
        out_shape=[pltpu.HBM(a.shape, a.dtype) for a in lands],
        input_output_aliases={a: a for a in range(n)}, compiler_params=_SPLIT_COPY,
    )(*lands, sems[0], sems[1], after)
    return list(outs)


def exchange_pair(grads, name):
    n = len(grads)

    def body(*refs):
        ins, outs = refs[:n], refs[n:2 * n]
        send, recv = refs[2 * n:]
        mx, my, mc = _coords()
        copies = [_remote(ins[a].at[:, 1 - mc], outs[a], send.at[a], recv.at[a], (mx, my, 1 - mc)) for a in range(n)]
        for cp in copies:
            cp.start()
        for cp in copies:
            cp.wait()

    return pl.pallas_call(
        body, name=name, in_specs=[ANY] * n, out_specs=[ANY] * n,
        out_shape=[_sds((N_CHIPS,) + g.shape[2:], g.dtype) for g in grads],
        scratch_shapes=[pltpu.SemaphoreType.DMA((n,)), pltpu.SemaphoreType.DMA((n,))],
    )(*grads)


def chips_start(parts, name):
    n = len(parts)

    def body(*refs):
        ins, lands = refs[:n], refs[n:2 * n]
        send, recv = refs[2 * n], refs[2 * n + 1]
        mx, my, mc = _coords()
        chip = 2 * mx + my
        for a in range(n):
            for k, (px, py) in enumerate(_other_chips(mx, my)):
                _remote(ins[a].at[2 * px + py], lands[a].at[chip], send.at[3 * a + k], recv.at[3 * a + k], (px, py, mc)).start()

    outs = pl.pallas_call(
        body, name=name, in_specs=[HBM] * (2 * n), out_specs=[SEM, SEM] + [HBM] * (2 * n),
        out_shape=[pltpu.SemaphoreType.DMA((3 * n,))] * 2 + [pltpu.HBM(p.shape, p.dtype) for p in parts] * 2,
        input_output_aliases={a: 2 + a for a in range(2 * n)}, compiler_params=_SPLIT_COPY,
    )(*[_in_hbm(p) for p in parts], *[_in_hbm(lax.empty(p.shape, p.dtype)) for p in parts])
    return (outs[0], outs[1]), list(outs[2:2 + n]), list(outs[2 + n:])


def chips_wait(parts, lands, sems, after, name):
    n = len(parts)

    def body(*refs):
        ins, zones = refs[:n], refs[n:2 * n]
        send, recv = refs[2 * n], refs[2 * n + 1]
        mx, my, mc = _coords()
        for a in range(n):
            for k, (px, py) in enumerate(_other_chips(mx, my)):
                cp = _remote(ins[a].at[2 * px + py], zones[a].at[2 * px + py], send.at[3 * a + k], recv.at[3 * a + k], (px, py, mc))
                cp.wait_send()
                cp.wait_recv()

    outs = pl.pallas_call(
        body, name=name, in_specs=[HBM] * (2 * n) + [SEM, SEM, ANY], out_specs=[HBM] * (2 * n),
        out_shape=[pltpu.HBM(p.shape, p.dtype) for p in parts] * 2,
        input_output_aliases={a: a for a in range(2 * n)}, compiler_params=_SPLIT_COPY,
    )(*parts, *lands, sems[0], sems[1], after)
    return list(outs[:n]), list(outs[n:])


def chip_sum(lands, parts, chip_core, name, into=None, layer=None):
    _, hR, C = lands.shape

    def body(s_ref, l_ref, p_ref, *rest):
        o_ref = rest[-1]
        chip = s_ref[0]
        own = p_ref[...].astype(F32)
        acc = None
        for q in range(N_CHIPS):
            term = jnp.where(chip == q, own, l_ref[q].astype(F32))
            acc = term if acc is None else acc + term
        o_ref[...] = acc

    in_specs = [pl.BlockSpec((N_CHIPS, hR, C), lambda i, s: (0, 0, 0)),
                pl.BlockSpec((None, hR, C), lambda i, s: (s[0], 0, 0))]
    args = [chip_core, lands, parts]
    aliases = {}
    if layer is None:
        out_spec = pl.BlockSpec((None, hR, C), lambda i, s: (s[1], 0, 0))
        out_shape = _sds((2, hR, C), F32)
    else:
        out_spec = pl.BlockSpec((None, None, hR, C), lambda i, s: (layer, s[1], 0, 0))
        out_shape = _sds((2, 2, hR, C), F32)
        if into is not None:
            in_specs.append(ANY)
            args.append(into)
            aliases = {3: 0}
    grid_spec = pltpu.PrefetchScalarGridSpec(num_scalar_prefetch=1, grid=(1,), in_specs=in_specs, out_specs=out_spec)
    return pl.pallas_call(body, name=name, grid_spec=grid_spec, out_shape=out_shape, input_output_aliases=aliases,
                          compiler_params=_params("arbitrary"))(*args)


def join_halves(arrays, name):
    n = len(arrays)
    pieces = [(a, l) for a, arr in enumerate(arrays) for l in (range(arr.shape[0]) if arr.ndim == 4 else [None])]

    def body(*refs):
        ins = refs[:n]
        send, recv = refs[2 * n:]
        mx, my, mc = _coords()

        def half(a, l, h):
            return ins[a].at[h] if l is None else ins[a].at[l, h]

        sends = [_remote(half(a, l, mc), half(a, l, mc), send.at[i], recv.at[i], (mx, my, 1 - mc))
                 for i, (a, l) in enumerate(pieces)]
        for cp in sends:
            cp.start()
        for i, (a, l) in enumerate(pieces):
            theirs = half(a, l, 1 - mc)
            _remote(theirs, theirs, send.at[i], recv.at[i], (mx, my, 1 - mc)).wait_recv()
        for cp in sends:
            cp.wait_send()

    return pl.pallas_call(
        body, name=name, in_specs=[ANY] * n, out_specs=[ANY] * n, out_shape=[_sds(a.shape, a.dtype) for a in arrays],
        input_output_aliases={a: a for a in range(n)},
        scratch_shapes=[pltpu.SemaphoreType.DMA((len(pieces),)), pltpu.SemaphoreType.DMA((len(pieces),))],
    )(*arrays)


LANES = 128
SUBLANES = 8


def _n_rows(shape):
    rows = -(-int(np.prod(shape)) // LANES)
    return -(-rows // SUBLANES) * SUBLANES


def _as_rows(a):
    flat = a.reshape(-1)
    rows = _n_rows(a.shape)
    return jnp.pad(flat, (0, rows * LANES - flat.shape[0])).reshape(rows, LANES)


def _pack(arrays):
    return jnp.concatenate([_as_rows(a) for a in arrays], axis=0)


def _unpack(rows, shapes):
    out, r0 = [], 0
    for s in shapes:
        n = _n_rows(s)
        out.append(rows[r0:r0 + n].reshape(-1)[:int(np.prod(s))].reshape(s))
        r0 += n
    return out


REPLICATED_SMALL = [("rel_bias", (32, 16)), ("even_norm", (1, 1024)), ("even_pool_w", (1, 4, 128, 128)),
                    ("even_pool_scale", (1, 512)), ("odd_q_norm", (1, 64)), ("odd_k_norm", (1, 64)),
                    ("ffn_norm", (2, 1024)), ("ffn_conv_b", (2, 5632))]
SHARDED_SMALL = [("even_conv_w", (1, 3, 128)), ("odd_norm", (1, 256)), ("ffn_conv_w", (2, 3, 1408))]
BIG = ["even_w_in", "even_w_out", "odd_w_qkv", "odd_w_o", "ffn_w_up", "ffn_w_down"]
WEIGHT_ORDER = ["rel_bias", "even_norm", "even_w_in", "even_conv_w", "even_pool_w", "even_pool_scale", "even_w_out",
                "odd_norm", "odd_w_qkv", "odd_q_norm", "odd_k_norm", "odd_w_o", "ffn_norm", "ffn_w_up", "ffn_conv_w",
                "ffn_conv_b", "ffn_w_down"]


def kernel(x, rel_bias, even_norm, even_w_in, even_conv_w, even_pool_w, even_pool_scale, even_w_out, odd_norm, odd_w_qkv, odd_q_norm, odd_k_norm, odd_w_o, ffn_norm, ffn_w_up, ffn_conv_w, ffn_conv_b, ffn_w_down, loss_target, m_rel_bias, m_even_norm, m_even_w_in, m_even_conv_w, m_even_pool_w, m_even_pool_scale, m_even_w_out, m_odd_norm, m_odd_w_qkv, m_odd_q_norm, m_odd_k_norm, m_odd_w_o, m_ffn_norm, m_ffn_w_up, m_ffn_conv_w, m_ffn_conv_b, m_ffn_w_down, v_rel_bias, v_even_norm, v_even_w_in, v_even_conv_w, v_even_pool_w, v_even_pool_scale, v_even_w_out, v_odd_norm, v_odd_w_qkv, v_odd_q_norm, v_odd_k_norm, v_odd_w_o, v_ffn_norm, v_ffn_w_up, v_ffn_conv_w, v_ffn_conv_b, v_ffn_w_down):
    W = dict(rel_bias=rel_bias, even_norm=even_norm, even_w_in=even_w_in, even_conv_w=even_conv_w, even_pool_w=even_pool_w,
             even_pool_scale=even_pool_scale, even_w_out=even_w_out, odd_norm=odd_norm, odd_w_qkv=odd_w_qkv,
             odd_q_norm=odd_q_norm, odd_k_norm=odd_k_norm, odd_w_o=odd_w_o, ffn_norm=ffn_norm, ffn_w_up=ffn_w_up,
             ffn_conv_w=ffn_conv_w, ffn_conv_b=ffn_conv_b, ffn_w_down=ffn_w_down)
    M1 = dict(rel_bias=m_rel_bias, even_norm=m_even_norm, even_w_in=m_even_w_in, even_conv_w=m_even_conv_w,
              even_pool_w=m_even_pool_w, even_pool_scale=m_even_pool_scale, even_w_out=m_even_w_out, odd_norm=m_odd_norm,
              odd_w_qkv=m_odd_w_qkv, odd_q_norm=m_odd_q_norm, odd_k_norm=m_odd_k_norm, odd_w_o=m_odd_w_o,
              ffn_norm=m_ffn_norm, ffn_w_up=m_ffn_w_up, ffn_conv_w=m_ffn_conv_w, ffn_conv_b=m_ffn_conv_b,
              ffn_w_down=m_ffn_w_down)
    M2 = dict(rel_bias=v_rel_bias, even_norm=v_even_norm, even_w_in=v_even_w_in, even_conv_w=v_even_conv_w,
              even_pool_w=v_even_pool_w, even_pool_scale=v_even_pool_scale, even_w_out=v_even_w_out, odd_norm=v_odd_norm,
              odd_w_qkv=v_odd_w_qkv, odd_q_norm=v_odd_q_norm, odd_k_norm=v_odd_k_norm, odd_w_o=v_odd_w_o,
              ffn_norm=v_ffn_norm, ffn_w_up=v_ffn_w_up, ffn_conv_w=v_ffn_conv_w, ffn_conv_b=v_ffn_conv_b,
              ffn_w_down=v_ffn_w_down)
    mx, my, mc = _coords()
    chip = 2 * mx + my
    core = jnp.reshape(mc, (1,)).astype(jnp.int32)
    xs, target = x[0], loss_target[0]
    S = xs.shape[0]

    def halves(w):
        return w.reshape((2, w.shape[-2] // 2, w.shape[-1]))

    chip_core = jnp.stack([chip, mc]).astype(jnp.int32)
    lands = [cast_into_slot(halves(even_w_in), chip_core, "cast_w_in"), cast_into_slot(halves(even_w_out), chip_core, "cast_w_out"),
             cast_into_slot(ffn_w_up, chip_core, "cast_w_up"), cast_into_slot(ffn_w_down, chip_core, "cast_w_down"),
             cast_into_slot(halves(odd_w_qkv), chip_core, "cast_w_qkv"), cast_into_slot(halves(odd_w_o), chip_core, "cast_w_o")]
    gather_sems, lands, token = gather_start(lands, [[0, 1], [2, 3], [4, 5]], "gather_start")
    even_norm_after_start = even_norm + token[0:1, 0:1]

    small = allgather_devices(_pack([even_conv_w, odd_norm, ffn_conv_w]), "allgather_small_weights")[0::2]
    conv_w_full = small[:, 0:3].transpose(1, 0, 2).reshape(3, A_WIDTH)
    odd_norm_full = small[:, 8:10].reshape(1, D_MODEL)
    ffn_cw_full = small[:, 16:82].reshape(N_CHIPS, 2, 3, 2 * D_FF // N_CHIPS).transpose(1, 2, 0, 3).reshape(2, 3, 2 * D_FF)
    pool_w = cast_bf16(even_pool_w[0], "cast_pool_w")
    gqk = jnp.stack([jnp.tile(odd_q_norm[0], N_HEADS), jnp.tile(odd_k_norm[0], N_HEADS),
                     jnp.ones((D_MODEL,), F32)])[:, None, :]
    bias = bias_expand(rel_bias.T, "bias_expand").reshape(3, 2, N_HEADS, ATT_BLOCK, 2 * ATT_BLOCK)

    def ffn_fwd(l, xin):
        xn = rmsnorm_fwd(xin, ffn_norm[l:l + 1], f"ffn{l}_norm")
        up = mm_nn(xn, w_up, f"ffn{l}_up", layer=l)
        act = glu_fwd(up, ffn_cw_full[l], ffn_conv_b[l:l + 1], f"ffn{l}_glu")
        return mm_nn(act, w_down, f"ffn{l}_down", layer=l, res=xin), (xin, xn, up, act)

    xn0 = rmsnorm_fwd(xs, even_norm_after_start, "even_norm")
    got = gather_wait(lands[0:2], gather_sems[0], bias, "gather_wait_even")
    w_in = got[0].reshape(N_CHIPS, 1, D_MODEL, EVEN_IN // N_CHIPS)
    w_out = got[1].reshape(1, 1, D_MODEL, D_MODEL)
    proj = mm_nn(xn0, w_in, "even_in")
    mix = mixer_fwd(proj, conv_w_full, pool_w, even_pool_scale, "even_mixer")
    x1 = mm_nn(mix, w_out, "even_out", res=xs)
    got = gather_wait(lands[2:4], gather_sems[1], x1, "gather_wait_ffn")
    w_up = got[0]
    w_down = got[1].transpose(1, 0, 2, 3).reshape(1, 2, D_FF, D_MODEL)
    x2, ffn0 = ffn_fwd(0, x1)
    got = gather_wait(lands[4:6], gather_sems[2], x2, "gather_wait_odd")
    w_qkv = got[0].reshape(N_CHIPS, 1, D_MODEL, 3 * D_MODEL // N_CHIPS)
    w_o = got[1].reshape(1, 1, D_MODEL, D_MODEL)
    xn2 = rmsnorm_fwd(x2, odd_norm_full, "odd_norm")
    qkv = mm_nn(xn2, w_qkv, "odd_qkv")
    qkvn = qknorm_fwd(qkv, gqk, "odd_qknorm")
    state = None
    for br in range(3):
        state = attn_fwd_branch(qkvn, bias[br], br, state, br == 2, f"attn_fwd{br}")
    att, lse = state
    x3 = mm_nn(att, w_o, "odd_out", res=x2)
    x4, ffn1 = ffn_fwd(1, x3)

    dy, dyb, sq = loss_grad(x4, target, "loss")
    loss = lax.psum(0.5 * jnp.sum(sq) * (1.0 / D_MODEL), ("x", "y", "c"))

    def ffn_bwd(l, dy, dyb, saved):
        xin, xn, up, act = saved
        dw_down = mm_tn(act, dyb, f"ffn{l}_dw_down", J=1, tk=D_FF // 2)
        dact = mm_nt(dyb, w_down, f"ffn{l}_dact", tr=D_FF // 2, layer=l)
        dup, dcw, dcb = glu_bwd(up, dact, ffn_cw_full[l], ffn_conv_b[l:l + 1], f"ffn{l}_glu_bwd")
        dw_up = mm_tn(xn, dup, f"ffn{l}_dw_up", J=N_CHIPS, tk=512)
        dxn = mm_nt(dup, w_up, f"ffn{l}_dxn", tr=D_MODEL, layer=l)
        dx, dxb, dg = rmsnorm_bwd(xin, ffn_norm[l:l + 1], dxn, dy, f"ffn{l}_norm_bwd")
        return dx, dxb, (dw_down, dw_up, dcw, dcb, dg)

    def quarters(g):
        return g.reshape(N_CHIPS, 2, g.shape[0] * g.shape[1] // (2 * N_CHIPS), g.shape[-1])

    def reduce_start(grads, tag):
        qs = [quarters(g) for g in grads]
        theirs = exchange_pair(qs, "pair_exchange_" + tag)
        parts = [pair_sum(g, r, core, f"pair_sum_{tag}{i}") for i, (g, r) in enumerate(zip(qs, theirs))]
        return chips_start(parts, "chips_start_" + tag)

    dx3, dx3b, g_ffn1 = ffn_bwd(1, dy, dyb, ffn1)
    red_ffn1 = reduce_start([g_ffn1[1], g_ffn1[0]], "ffn1")
    dw_o = mm_tn(att, dx3b, "odd_dw_o", J=1, tk=512)
    datt = mm_nt(dx3b, w_o, "odd_datt", tr=D_MODEL, out_dtype=BF16)
    grads, dbias = None, []
    for br in range(3):
        grads, db = attn_bwd_branch(qkvn, att, datt, lse, bias[br], br, grads, f"attn_bwd{br}")
        dbias.append(db.reshape(N_HEADS, 2 * ATT_BLOCK * ATT_BLOCK))
    dqkv, dgqk = qknorm_bwd(qkv, grads[0], grads[1], grads[2], gqk, "odd_qknorm_bwd")
    dw_qkv = mm_tn(xn2, dqkv, "odd_dw_qkv", J=N_CHIPS, tk=512)
    dxn2 = mm_nt(dqkv, w_qkv, "odd_dxn", tr=D_MODEL)
    red_odd = reduce_start([dw_qkv, dw_o], "odd")
    dx2, dx2b, dg_odd = rmsnorm_bwd(x2, odd_norm_full, dxn2, dx3, "odd_norm_bwd")
    dx1, dx1b, g_ffn0 = ffn_bwd(0, dx2, dx2b, ffn0)
    red_ffn0 = reduce_start([g_ffn0[1], g_ffn0[0]], "ffn0")
    dw_out = mm_tn(mix, dx1b, "even_dw_out", J=1, tk=512)
    dmix = mm_nt(dx1b, w_out, "even_dmix", tr=D_MODEL)
    dproj, dcw_even, dpw, dps = mixer_bwd(proj, dmix, conv_w_full, pool_w, even_pool_scale, "even_mixer_bwd")
    dw_in = mm_tn(xn0, dproj, "even_dw_in", J=N_CHIPS, tk=512)
    dxn0 = mm_nt(dproj, w_in, "even_dxn", tr=D_MODEL)
    grad_x, _, dg_even = rmsnorm_bwd(xs, even_norm, dxn0, dx1, "even_norm_bwd")
    d_rel = jnp.sum(bias_reduce(jnp.stack(dbias), "bias_reduce"), axis=0).T

    red_even = reduce_start([dw_in, dw_out], "even")

    def reduce_wait(red, tag):
        sems, parts, zones = red
        parts, zones = chips_wait(parts, zones, sems, grad_x, "chips_wait_" + tag)
        return zones, parts

    z_ffn1, p_ffn1 = reduce_wait(red_ffn1, "ffn1")
    z_odd, p_odd = reduce_wait(red_odd, "odd")
    z_ffn0, p_ffn0 = reduce_wait(red_ffn0, "ffn0")
    z_even, p_even = reduce_wait(red_even, "even")
    r_up = chip_sum(z_ffn0[0], p_ffn0[0], chip_core, "chip_sum_w_up0", layer=0)
    r_up = chip_sum(z_ffn1[0], p_ffn1[0], chip_core, "chip_sum_w_up1", into=r_up, layer=1)
    r_down = chip_sum(z_ffn0[1], p_ffn0[1], chip_core, "chip_sum_w_down0", layer=0)
    r_down = chip_sum(z_ffn1[1], p_ffn1[1], chip_core, "chip_sum_w_down1", into=r_down, layer=1)
    halves_written = [chip_sum(z_even[0], p_even[0], chip_core, "chip_sum_w_in"),
                      chip_sum(z_even[1], p_even[1], chip_core, "chip_sum_w_out"),
                      chip_sum(z_odd[0], p_odd[0], chip_core, "chip_sum_w_qkv"),
                      chip_sum(z_odd[1], p_odd[1], chip_core, "chip_sum_w_o"), r_up, r_down]
    joined = join_halves(halves_written, "grads_join_halves")
    G = {nm: g.reshape(W[nm].shape) for nm, g in zip(BIG, joined)}

    dcw_sh = dcw_even.reshape(3, N_CHIPS, A_WIDTH // N_CHIPS).transpose(1, 0, 2)
    don_sh = dg_odd.reshape(N_CHIPS, D_MODEL // N_CHIPS)
    dfcw = jnp.stack([g_ffn0[2], g_ffn1[2]])
    dfcw_sh = dfcw.reshape(2, 3, N_CHIPS, 2 * D_FF // N_CHIPS).transpose(2, 0, 1, 3)
    rep_grads = [d_rel, dg_even, dpw[None], dps, _head_sum(dgqk[0]), _head_sum(dgqk[1]),
                 jnp.concatenate([g_ffn0[4], g_ffn1[4]], axis=0), jnp.concatenate([g_ffn0[3], g_ffn1[3]], axis=0)]
    rep_rows = _pack(rep_grads)
    shard_rows = jnp.concatenate([_pack([dcw_sh[j], don_sh[j], dfcw_sh[j]]) for j in range(N_CHIPS)], axis=0)
    n_rep, n_shard = rep_rows.shape[0], shard_rows.shape[0] // N_CHIPS
    small_sum = sum_slots(allgather_devices(jnp.concatenate([rep_rows, shard_rows], axis=0), "allgather_small_grads"),
                          "small_grads_sum")
    mine = lax.dynamic_slice_in_dim(small_sum, n_rep + chip * n_shard, n_shard, axis=0)
    g_small = jnp.concatenate([small_sum[:n_rep], mine], axis=0)
    small_names = [n for n, _ in REPLICATED_SMALL + SHARDED_SMALL]
    small_shapes = [s for _, s in REPLICATED_SMALL + SHARDED_SMALL]
    G.update(dict(zip(small_names, _unpack(g_small, small_shapes))))

    D_, NM, NV = {}, {}, {}
    for nm in BIG:
        as3 = lambda a: a.reshape((-1,) + a.shape[-2:])
        outs = adamw(as3(W[nm]), as3(G[nm]), as3(M1[nm]), as3(M2[nm]), "adamw_" + nm)
        D_[nm], NM[nm], NV[nm] = [o.reshape(W[nm].shape) for o in outs]
    packs = [_pack([d[n] for n in small_names])[None] for d in (W, M1, M2)]
    outs = adamw(packs[0], g_small[None], packs[1], packs[2], "adamw_small")
    for dst, o in zip((D_, NM, NV), outs):
        dst.update(dict(zip(small_names, _unpack(o[0], small_shapes))))

    return (loss, grad_x[None], *[G[n] for n in WEIGHT_ORDER], *[D_[n] for n in WEIGHT_ORDER],
            *[NM[n] for n in WEIGHT_ORDER], *[NV[n] for n in WEIGHT_ORDER])


def _head_sum(dg):
    return jnp.sum(dg.reshape(N_HEADS, HEAD_DIM), axis=0, keepdims=True)
```

```python
import functools
import math

import numpy as np
import jax
import jax.numpy as jnp
from jax import lax
from jax.experimental import pallas as pl
from jax.experimental.pallas import tpu as pltpu

F32 = jnp.float32
BF16 = jnp.bfloat16

D_MODEL = 1024
N_HEADS = 16
HEAD_DIM = 64
A_WIDTH = 512
POOL_WINDOWS = (2, 4, 8, 16)
POOL_GROUP = 128
EVEN_IN = 2048
D_FF = 2816
DILATED_PAIRS = ((128, 1), (512, 4), (2048, 16))
ATT_BLOCK = 128
N_REL_BUCKETS = 32
REL_MAX_DISTANCE = 2048
EPS = 1e-6
MASK_VALUE = -1e30
ADAM_LR, ADAM_B1, ADAM_B2, ADAM_EPS, ADAM_WD, ADAM_STEP = 0.001, 0.9, 0.999, 1e-08, 0.01, 10

VMEM_LIMIT_BYTES = 48 * 1024 * 1024
N_CHIPS = 4
N_DEV = 8
MESH = pl.DeviceIdType.MESH


def _params(*sem):
    return pltpu.CompilerParams(dimension_semantics=sem if sem else None, vmem_limit_bytes=VMEM_LIMIT_BYTES)


def _sds(shape, dtype):
    return jax.ShapeDtypeStruct(tuple(shape), dtype)


def cast_bf16(x, name, tr=None):
    lead, (R, C) = x.shape[:-2], x.shape[-2:]
    n = int(np.prod(lead)) if lead else 1
    x3 = x.reshape((n, R, C))
    tr = tr or R

    def body(x_ref, o_ref):
        o_ref[...] = x_ref[...].astype(BF16)

    out = pl.pallas_call(
        body, name=name, grid=(n, R // tr),
        in_specs=[pl.BlockSpec((None, tr, C), lambda i, r: (i, r, 0))],
        out_specs=pl.BlockSpec((None, tr, C), lambda i, r: (i, r, 0)),
        out_shape=_sds((n, R, C), BF16), compiler_params=_params("parallel", "parallel"),
    )(x3)
    return out.reshape(lead + (R, C))


def rmsnorm_fwd(x, g, name, ts=512):
    S, Dm = x.shape

    def body(x_ref, g_ref, o_ref):
        xv = x_ref[...]
        r = lax.rsqrt(jnp.mean(xv * xv, axis=-1, keepdims=True) + EPS)
        o_ref[...] = ((xv * r) * g_ref[...]).astype(BF16)

    return pl.pallas_call(
        body, name=name, grid=(S // ts,),
        in_specs=[pl.BlockSpec((ts, Dm), lambda i: (i, 0)), pl.BlockSpec((1, Dm), lambda i: (0, 0))],
        out_specs=pl.BlockSpec((ts, Dm), lambda i: (i, 0)),
        out_shape=_sds((S, Dm), BF16), compiler_params=_params("parallel"),
    )(x, g)


def rmsnorm_bwd(x, g, dxn, dres, name, ts=512):
    S, Dm = x.shape

    def body(x_ref, g_ref, d_ref, r_ref, dx_ref, dxb_ref, dg_ref):
        xv = x_ref[...]
        dv = d_ref[...].astype(F32)
        r = lax.rsqrt(jnp.mean(xv * xv, axis=-1, keepdims=True) + EPS)
        gx = dv * g_ref[...]
        dot = jnp.sum(gx * xv, axis=-1, keepdims=True)
        dx = r_ref[...] + r * gx - xv * ((r * r * r) * (dot * (1.0 / Dm)))
        dx_ref[...] = dx
        dxb_ref[...] = dx.astype(BF16)
        part = jnp.sum(dv * (xv * r), axis=0, keepdims=True)

        @pl.when(pl.program_id(0) == 0)
        def _():
            dg_ref[...] = part

        @pl.when(pl.program_id(0) > 0)
        def _():
            dg_ref[...] += part

    row = pl.BlockSpec((ts, Dm), lambda i: (i, 0))
    vec = pl.BlockSpec((1, Dm), lambda i: (0, 0))
    return pl.pallas_call(
        body, name=name, grid=(S // ts,),
        in_specs=[row, vec, row, row], out_specs=[row, row, vec],
        out_shape=[_sds((S, Dm), F32), _sds((S, Dm), BF16), _sds((1, Dm), F32)], compiler_params=_params("arbitrary"),
    )(x, g, dxn, dres)


def mm_nn(a, w, name, layer=0, res=None, out_dtype=F32, tm=512):
    M, K = a.shape
    J, _, _, Ns = w.shape

    def body(*refs):
        a_ref, w_ref = refs[0], refs[1]
        o_ref = refs[-1]
        acc = jnp.dot(a_ref[...], w_ref[...], preferred_element_type=F32)
        if res is not None:
            acc = refs[2][...] + acc
        o_ref[...] = acc.astype(o_ref.dtype)

    in_specs = [pl.BlockSpec((tm, K), lambda j, m: (m, 0)),
                pl.BlockSpec((None, None, K, Ns), lambda j, m: (j, layer, 0, 0))]
    args = [a, w]
    if res is not None:
        in_specs.append(pl.BlockSpec((tm, Ns), lambda j, m: (m, j)))
        args.append(res)
    return pl.pallas_call(
        body, name=name, grid=(J, M // tm), in_specs=in_specs,
        out_specs=pl.BlockSpec((tm, Ns), lambda j, m: (m, j)),
        out_shape=_sds((M, J * Ns), out_dtype), compiler_params=_params("parallel", "parallel"),
    )(*args)


def mm_nt(dy, w, name, tr, layer=0, out_dtype=F32, tm=512):
    M = dy.shape[0]
    J, _, R, Ns = w.shape
    dims = (((1,), (1,)), ((), ()))

    def body(dy_ref, w_ref, o_ref, *scratch):
        p = lax.dot_general(dy_ref[...], w_ref[...], dims, preferred_element_type=F32)
        if J == 1:
            o_ref[...] = p.astype(o_ref.dtype)
            return
        acc_ref, = scratch
        j = pl.program_id(2)

        @pl.when(j == 0)
        def _():
            acc_ref[...] = p

        @pl.when(j > 0)
        def _():
            acc_ref[...] += p

        @pl.when(j == J - 1)
        def _():
            o_ref[...] = acc_ref[...].astype(o_ref.dtype)

    return pl.pallas_call(
        body, name=name, grid=(R // tr, M // tm, J),
        in_specs=[pl.BlockSpec((tm, Ns), lambda r, m, j: (m, j)),
                  pl.BlockSpec((None, None, tr, Ns), lambda r, m, j: (j, layer, r, 0))],
        out_specs=pl.BlockSpec((tm, tr), lambda r, m, j: (m, r)),
        out_shape=_sds((M, R), out_dtype),
        scratch_shapes=[] if J == 1 else [pltpu.VMEM((tm, tr), F32)],
        compiler_params=_params("parallel", "parallel", "arbitrary"),
    )(dy, w)


def mm_tn(a, dy, name, J, tk, tm=512):
    M, K = a.shape
    Ns = dy.shape[1] // J
    dims = (((0,), (0,)), ((), ()))

    def body(a_ref, dy_ref, o_ref):
        p = lax.dot_general(a_ref[...], dy_ref[...], dims, preferred_element_type=F32)
        m = pl.program_id(2)

        @pl.when(m == 0)
        def _():
            o_ref[...] = p

        @pl.when(m > 0)
        def _():
            o_ref[...] += p

    return pl.pallas_call(
        body, name=name, grid=(J, K // tk, M // tm),
        in_specs=[pl.BlockSpec((tm, tk), lambda j, k, m: (m, k)), pl.BlockSpec((tm, Ns), lambda j, k, m: (m, j))],
        out_specs=pl.BlockSpec((None, tk, Ns), lambda j, k, m: (j, k, 0)),
        out_shape=_sds((J, K, Ns), F32), compiler_params=_params("parallel", "parallel", "arbitrary"),
    )(a, dy)


HALO = 16


def _shift_down(x, s):
    return pltpu.roll(x, s, 0)


def _shift_up(x, s):
    return pltpu.roll(x, x.shape[0] - s, 0)


def _conv3(z, cw):
    return (_shift_down(z, 2) * cw[0:1] + _shift_down(z, 1) * cw[1:2]) + z * cw[2:3]


def _window_count(first_row, n, k):
    t = first_row + lax.broadcasted_iota(jnp.int32, (n, 1), 0)
    return jnp.clip(t + 1, 1, k).astype(F32)


def mixer_fwd(proj, conv_w, pool_w, pool_scale, name, ts=256):
    S = proj.shape[0]
    n = ts + HALO

    def body(pm_ref, pb_ref, cw_ref, pw_ref, ps_ref, o_ref):
        i = pl.program_id(0)
        before = jnp.where(i > 0, pb_ref[...], 0.0)
        ext = jnp.concatenate([before, pm_ref[...]], axis=0)
        cw = cw_ref[...]
        z = ext[:, 2 * A_WIDTH:3 * A_WIDTH] * ext[:, 0:A_WIDTH]
        cz = _conv3(z, cw)
        ya = pm_ref[:, A_WIDTH:2 * A_WIDTH] * cz[HALO:]
        o_ref[:, 0:A_WIDTH] = ya.astype(BF16)
        for g, k in enumerate(POOL_WINDOWS):
            lo = 3 * A_WIDTH + g * POOL_GROUP
            p = ext[:, lo:lo + POOL_GROUP]
            w = p
            s = 1
            while s < k:
                w = w + _shift_down(w, s)
                s *= 2
            pooled = w / _window_count(i * ts - HALO, n, k) - p
            yb = jnp.dot(pooled[HALO:].astype(BF16), pw_ref[g], preferred_element_type=F32)
            yb = yb * ps_ref[:, g * POOL_GROUP:(g + 1) * POOL_GROUP]
            o_ref[:, A_WIDTH + g * POOL_GROUP:A_WIDTH + (g + 1) * POOL_GROUP] = yb.astype(BF16)

    hb = ts // HALO
    return pl.pallas_call(
        body, name=name, grid=(S // ts,),
        in_specs=[
            pl.BlockSpec((ts, EVEN_IN), lambda i: (i, 0)),
            pl.BlockSpec((HALO, EVEN_IN), lambda i: (jnp.maximum(i * hb - 1, 0), 0)),
            pl.BlockSpec((3, A_WIDTH), lambda i: (0, 0)),
            pl.BlockSpec((4, POOL_GROUP, POOL_GROUP), lambda i: (0, 0, 0)),
            pl.BlockSpec((1, 4 * POOL_GROUP), lambda i: (0, 0)),
        ],
        out_specs=pl.BlockSpec((ts, D_MODEL), lambda i: (i, 0)),
        out_shape=_sds((S, D_MODEL), BF16), compiler_params=_params("parallel"),
    )(proj, proj, conv_w, pool_w, pool_scale)


def mixer_bwd(proj, dmix, conv_w, pool_w, pool_scale, name, ts=256):
    S = proj.shape[0]
    n = ts + 2 * HALO
    nt = S // ts
    tn_dims = (((0,), (0,)), ((), ()))
    nt_dims = (((1,), (1,)), ((), ()))

    def body(pm_ref, pb_ref, pa_ref, dm_ref, da_ref, cw_ref, pw_ref, ps_ref, o_ref, dcw_ref, dpw_ref, dps_ref):
        i = pl.program_id(0)
        last = i == nt - 1
        before = jnp.where(i > 0, pb_ref[...], 0.0)
        after = jnp.where(last, 0.0, pa_ref[...])
        ext = jnp.concatenate([before, pm_ref[...], after], axis=0)
        dafter = jnp.where(last, 0.0, da_ref[...])
        dext = jnp.concatenate([jnp.zeros((HALO, D_MODEL), F32), dm_ref[...], dafter], axis=0)
        cw = cw_ref[...]
        main = slice(HALO, HALO + ts)

        @pl.when(i == 0)
        def _():
            dcw_ref[...] = jnp.zeros_like(dcw_ref)
            dpw_ref[...] = jnp.zeros_like(dpw_ref)
            dps_ref[...] = jnp.zeros_like(dps_ref)

        h, gb, gc = ext[:, 0:A_WIDTH], ext[:, A_WIDTH:2 * A_WIDTH], ext[:, 2 * A_WIDTH:3 * A_WIDTH]
        z = gc * h
        z1, z2 = _shift_down(z, 1), _shift_down(z, 2)
        cz = (z2 * cw[0:1] + z1 * cw[1:2]) + z * cw[2:3]
        dya = dext[:, 0:A_WIDTH]
        dcz = dya * gb
        dz = dcz * cw[2:3] + _shift_up(dcz, 1) * cw[1:2] + _shift_up(dcz, 2) * cw[0:1]
        o_ref[:, 0:A_WIDTH] = (dz * gc)[main].astype(BF16)
        o_ref[:, A_WIDTH:2 * A_WIDTH] = (dya * cz)[main].astype(BF16)
        o_ref[:, 2 * A_WIDTH:3 * A_WIDTH] = (dz * h)[main].astype(BF16)
        dczm = dcz[main]
        dcw_ref[0:1, :] += jnp.sum(dczm * z2[main], axis=0, keepdims=True)
        dcw_ref[1:2, :] += jnp.sum(dczm * z1[main], axis=0, keepdims=True)
        dcw_ref[2:3, :] += jnp.sum(dczm * z[main], axis=0, keepdims=True)

        for g, k in enumerate(POOL_WINDOWS):
            lo = 3 * A_WIDTH + g * POOL_GROUP
            cols = slice(g * POOL_GROUP, (g + 1) * POOL_GROUP)
            p = ext[:, lo:lo + POOL_GROUP]
            w = p
            s = 1
            while s < k:
                w = w + _shift_down(w, s)
                s *= 2
            cnt = _window_count(i * ts - HALO, n, k)
            pooled = (w / cnt - p)[main].astype(BF16)
            dyb = dext[:, A_WIDTH + g * POOL_GROUP:A_WIDTH + (g + 1) * POOL_GROUP]
            e = dyb * ps_ref[:, cols]
            pre = jnp.dot(pooled, pw_ref[g], preferred_element_type=F32)
            dps_ref[:, cols] += jnp.sum(dyb[main] * pre, axis=0, keepdims=True)
            dpw_ref[g] += lax.dot_general(pooled, e[main].astype(BF16), tn_dims, preferred_element_type=F32)
            dpooled = lax.dot_general(e.astype(BF16), pw_ref[g], nt_dims, preferred_element_type=F32)
            q = dpooled / cnt
            a = q
            s = 1
            while s < k:
                a = a + _shift_up(a, s)
                s *= 2
            o_ref[:, lo:lo + POOL_GROUP] = (a - dpooled)[main].astype(BF16)

    hb = ts // HALO
    nh = S // HALO
    before_map = lambda i: (jnp.maximum(i * hb - 1, 0), 0)
    after_map = lambda i: (jnp.minimum((i + 1) * hb, nh - 1), 0)
    full = lambda *shape: pl.BlockSpec(shape, lambda i: (0,) * len(shape))
    return pl.pallas_call(
        body, name=name, grid=(nt,),
        in_specs=[
            pl.BlockSpec((ts, EVEN_IN), lambda i: (i, 0)),
            pl.BlockSpec((HALO, EVEN_IN), before_map),
            pl.BlockSpec((HALO, EVEN_IN), after_map),
            pl.BlockSpec((ts, D_MODEL), lambda i: (i, 0)),
            pl.BlockSpec((HALO, D_MODEL), after_map),
            full(3, A_WIDTH), full(4, POOL_GROUP, POOL_GROUP), full(1, 4 * POOL_GROUP),
        ],
        out_specs=[pl.BlockSpec((ts, EVEN_IN), lambda i: (i, 0)), full(3, A_WIDTH), full(4, POOL_GROUP, POOL_GROUP),
                   full(1, 4 * POOL_GROUP)],
        out_shape=[_sds((S, EVEN_IN), BF16), _sds((3, A_WIDTH), F32), _sds((4, POOL_GROUP, POOL_GROUP), F32),
                   _sds((1, 4 * POOL_GROUP), F32)],
        compiler_params=_params("arbitrary"),
    )(proj, proj, proj, dmix, dmix, conv_w, pool_w, pool_scale)


FFN_HALO = 8
FFN_TC = 1408


def glu_fwd(up, conv_w, conv_b, name, ts=256):
    S = up.shape[0]
    nc = D_FF // FFN_TC

    def body(gm_ref, gb_ref, um_ref, ub_ref, cwg_ref, cwu_ref, cbg_ref, cbu_ref, o_ref):
        i = pl.program_id(0)

        def conv(m_ref, b_ref, cw_ref, cb_ref):
            before = jnp.where(i > 0, b_ref[...], 0.0)
            ext = jnp.concatenate([before, m_ref[...]], axis=0)
            return _conv3(ext, cw_ref[...])[FFN_HALO:] + cb_ref[...]

        gate = conv(gm_ref, gb_ref, cwg_ref, cbg_ref)
        upv = conv(um_ref, ub_ref, cwu_ref, cbu_ref)
        o_ref[...] = ((gate * (1.0 / (1.0 + jnp.exp(-gate)))) * upv).astype(BF16)

    hb = ts // FFN_HALO
    main = lambda off: pl.BlockSpec((ts, FFN_TC), lambda i, c: (i, c + off))
    halo = lambda off: pl.BlockSpec((FFN_HALO, FFN_TC), lambda i, c: (jnp.maximum(i * hb - 1, 0), c + off))
    cw = lambda off: pl.BlockSpec((3, FFN_TC), lambda i, c: (0, c + off))
    cb = lambda off: pl.BlockSpec((1, FFN_TC), lambda i, c: (0, c + off))
    return pl.pallas_call(
        body, name=name, grid=(S // ts, nc),
        in_specs=[main(0), halo(0), main(nc), halo(nc), cw(0), cw(nc), cb(0), cb(nc)],
        out_specs=pl.BlockSpec((ts, FFN_TC), lambda i, c: (i, c)),
        out_shape=_sds((S, D_FF), BF16), compiler_params=_params("parallel", "parallel"),
    )(up, up, up, up, conv_w, conv_w, conv_b, conv_b)


def glu_bwd(up, da, conv_w, conv_b, name, ts=256):
    S = up.shape[0]
    nc = D_FF // FFN_TC
    nt = S // ts
    main = slice(FFN_HALO, FFN_HALO + ts)
    W = 2 * D_FF

    def body(xm_ref, xb_ref, xa_ref, dm_ref, da_ref, cw_ref, cb_ref, dx_ref, dcw_ref, dcb_ref):
        i = pl.program_id(0)
        last = i == nt - 1

        @pl.when(i == 0)
        def _():
            dcw_ref[...] = jnp.zeros_like(dcw_ref)
            dcb_ref[...] = jnp.zeros_like(dcb_ref)

        def ext_of(cols):
            before = jnp.where(i > 0, xb_ref[:, cols], 0.0)
            return jnp.concatenate([before, xm_ref[:, cols], xa_ref[:, cols]], axis=0)

        def back(x, d, cols):
            cw = cw_ref[:, cols]
            dx = d * cw[2:3] + _shift_up(d, 1) * cw[1:2] + _shift_up(d, 2) * cw[0:1]
            dx_ref[:, cols] = dx[main].astype(BF16)
            dmn = d[main]
            dcb_ref[:, cols] += jnp.sum(dmn, axis=0, keepdims=True)
            dcw_ref[0:1, cols] += jnp.sum(dmn * _shift_down(x, 2)[main], axis=0, keepdims=True)
            dcw_ref[1:2, cols] += jnp.sum(dmn * _shift_down(x, 1)[main], axis=0, keepdims=True)
            dcw_ref[2:3, cols] += jnp.sum(dmn * x[main], axis=0, keepdims=True)

        for c in range(nc):
            gcols = slice(c * FFN_TC, (c + 1) * FFN_TC)
            ucols = slice(D_FF + c * FFN_TC, D_FF + (c + 1) * FFN_TC)
            xg, xu = ext_of(gcols), ext_of(ucols)
            ug = _conv3(xg, cw_ref[:, gcols]) + cb_ref[:, gcols]
            uu = _conv3(xu, cw_ref[:, ucols]) + cb_ref[:, ucols]
            dafter = jnp.where(last, 0.0, da_ref[:, gcols].astype(F32))
            dae = jnp.concatenate([jnp.zeros((FFN_HALO, FFN_TC), F32), dm_ref[:, gcols].astype(F32), dafter], axis=0)
            sg = 1.0 / (1.0 + jnp.exp(-ug))
            duu = dae * (ug * sg)
            dug = (dae * uu) * (sg * (1.0 + ug * (1.0 - sg)))
            back(xg, dug, gcols)
            back(xu, duu, ucols)

    hb = ts // FFN_HALO
    nh = S // FFN_HALO
    before_map = lambda i: (jnp.maximum(i * hb - 1, 0), 0)
    after_map = lambda i: (jnp.minimum((i + 1) * hb, nh - 1), 0)
    return pl.pallas_call(
        body, name=name, grid=(nt,),
        in_specs=[pl.BlockSpec((ts, W), lambda i: (i, 0)), pl.BlockSpec((FFN_HALO, W), before_map),
                  pl.BlockSpec((FFN_HALO, W), after_map), pl.BlockSpec((ts, D_FF), lambda i: (i, 0)),
                  pl.BlockSpec((FFN_HALO, D_FF), after_map), pl.BlockSpec((3, W), lambda i: (0, 0)),
                  pl.BlockSpec((1, W), lambda i: (0, 0))],
        out_specs=[pl.BlockSpec((ts, W), lambda i: (i, 0)), pl.BlockSpec((3, W), lambda i: (0, 0)),
                   pl.BlockSpec((1, W), lambda i: (0, 0))],
        out_shape=[_sds((S, W), BF16), _sds((3, W), F32), _sds((1, W), F32)],
        compiler_params=_params("arbitrary"),
    )(up, up, up, da, da, conv_w, conv_b)


def _head_mean_matrix():
    h = np.arange(D_MODEL) // HEAD_DIM
    return jnp.asarray((h[:, None] == h[None, :]).astype(np.float32) / HEAD_DIM, dtype=BF16)


def _head_mean(v, gm):
    hi = v.astype(BF16)
    lo = (v - hi.astype(F32)).astype(BF16)
    return jnp.dot(hi, gm, preferred_element_type=F32) + jnp.dot(lo, gm, preferred_element_type=F32)


def qknorm_fwd(qkv, gqk, name, ts=512):
    S = qkv.shape[0]

    def body(x_ref, g_ref, gm_ref, o_ref):
        part = pl.program_id(0)
        x = x_ref[...]

        @pl.when(part < 2)
        def _():
            r = lax.rsqrt(_head_mean(x * x, gm_ref[...]) + EPS)
            o_ref[...] = ((x * r) * g_ref[...]).astype(BF16)

        @pl.when(part == 2)
        def _():
            o_ref[...] = x.astype(BF16)

    return pl.pallas_call(
        body, name=name, grid=(3, S // ts),
        in_specs=[pl.BlockSpec((ts, D_MODEL), lambda p, i: (i, p)), pl.BlockSpec((None, 1, D_MODEL), lambda p, i: (p, 0, 0)),
                  pl.BlockSpec((D_MODEL, D_MODEL), lambda p, i: (0, 0))],
        out_specs=pl.BlockSpec((ts, D_MODEL), lambda p, i: (i, p)),
        out_shape=_sds((S, 3 * D_MODEL), BF16), compiler_params=_params("parallel", "parallel"),
    )(qkv, gqk, _head_mean_matrix())


def qknorm_bwd(qkv, dq, dk, dv, gqk, name, ts=256):
    S = qkv.shape[0]

    def body(x_ref, dq_ref, dk_ref, dv_ref, g_ref, gm_ref, o_ref, dg_ref):
        @pl.when(pl.program_id(0) == 0)
        def _():
            dg_ref[...] = jnp.zeros_like(dg_ref)

        gm = gm_ref[...]
        for part, d_ref in enumerate((dq_ref, dk_ref)):
            cols = slice(part * D_MODEL, (part + 1) * D_MODEL)
            x = x_ref[:, cols]
            d = d_ref[...]
            r = lax.rsqrt(_head_mean(x * x, gm) + EPS)
            gx = d * g_ref[part]
            o_ref[:, cols] = (r * gx - x * ((r * r * r) * _head_mean(gx * x, gm))).astype(BF16)
            dg_ref[part] += jnp.sum(d * (x * r), axis=0, keepdims=True)
        o_ref[:, 2 * D_MODEL:] = dv_ref[...].astype(BF16)

    row = pl.BlockSpec((ts, D_MODEL), lambda i: (i, 0))
    wide = pl.BlockSpec((ts, 3 * D_MODEL), lambda i: (i, 0))
    gains = pl.BlockSpec((3, 1, D_MODEL), lambda i: (0, 0, 0))
    return pl.pallas_call(
        body, name=name, grid=(S // ts,),
        in_specs=[wide, row, row, row, gains, pl.BlockSpec((D_MODEL, D_MODEL), lambda i: (0, 0))],
        out_specs=[wide, gains],
        out_shape=[_sds((S, 3 * D_MODEL), BF16), _sds((3, 1, D_MODEL), F32)],
        compiler_params=_params("arbitrary"),
    )(qkv, dq, dk, dv, gqk, _head_mean_matrix())


def _bucket_tables():
    n = ATT_BLOCK
    a = np.arange(n)[:, None]
    c = np.arange(2 * n)[None, :]
    rel = a + n - c
    band = (rel >= 0) & (rel <= n)
    max_exact = N_REL_BUCKETS // 2
    buckets, valids = [], []
    for _, dil in DILATED_PAIRS:
        dist = np.clip(rel, 0, n) * dil
        dd = np.maximum(dist, 1).astype(np.float32)
        large = max_exact + (np.log(dd / np.float32(max_exact)) / np.float32(math.log(REL_MAX_DISTANCE / max_exact))
                             * np.float32(N_REL_BUCKETS - max_exact)).astype(np.int32)
        large = np.minimum(large, N_REL_BUCKETS - 1)
        buckets.append(np.where(dist < max_exact, dist, large).reshape(1, -1))
        valids.append(np.stack([(band & (c >= n)).reshape(1, -1), band.reshape(1, -1)]))
    return np.stack(buckets).astype(np.int32), np.stack(valids).astype(np.int32)


BIAS_CHUNK = 8192


def _split3(x):
    a = x.astype(BF16)
    r = x - a.astype(F32)
    b = r.astype(BF16)
    c = (r - b.astype(F32)).astype(BF16)
    return a, b, c


def bias_expand(rel_bias_t, name):
    bucket, valid = _bucket_tables()
    nq = bucket.shape[-1]

    def body(t_ref, b_ref, v_ref, o_ref):
        onehot = (lax.broadcasted_iota(jnp.int32, (N_REL_BUCKETS, BIAS_CHUNK), 0) == b_ref[...]).astype(BF16)
        acc = None
        for term in _split3(t_ref[...]):
            p = jnp.dot(term, onehot, preferred_element_type=F32)
            acc = p if acc is None else acc + p
        o_ref[...] = jnp.where(v_ref[...] > 0, acc, MASK_VALUE)

    return pl.pallas_call(
        body, name=name, grid=(3, 2, nq // BIAS_CHUNK),
        in_specs=[pl.BlockSpec((N_HEADS, N_REL_BUCKETS), lambda b, v, c: (0, 0)),
                  pl.BlockSpec((None, 1, BIAS_CHUNK), lambda b, v, c: (b, 0, c)),
                  pl.BlockSpec((None, None, 1, BIAS_CHUNK), lambda b, v, c: (b, v, 0, c))],
        out_specs=pl.BlockSpec((None, None, N_HEADS, BIAS_CHUNK), lambda b, v, c: (b, v, 0, c)),
        out_shape=_sds((3, 2, N_HEADS, nq), F32), compiler_params=_params("parallel", "parallel", "parallel"),
    )(rel_bias_t, jnp.asarray(bucket), jnp.asarray(valid))


def bias_reduce(dbias, name):
    bucket, _ = _bucket_tables()
    nq = bucket.shape[-1]
    dims = (((1,), (1,)), ((), ()))

    def body(d_ref, b_ref, o_ref):
        onehot = (lax.broadcasted_iota(jnp.int32, (N_REL_BUCKETS, BIAS_CHUNK), 0) == b_ref[...]).astype(BF16)
        acc = None
        for term in _split3(d_ref[...]):
            p = lax.dot_general(term, onehot, dims, preferred_element_type=F32)
            acc = p if acc is None else acc + p

        @pl.when(pl.program_id(1) == 0)
        def _():
            o_ref[...] = acc

        @pl.when(pl.program_id(1) > 0)
        def _():
            o_ref[...] += acc

    return pl.pallas_call(
        body, name=name, grid=(3, nq // BIAS_CHUNK),
        in_specs=[pl.BlockSpec((None, N_HEADS, BIAS_CHUNK), lambda b, c: (b, 0, c)),
                  pl.BlockSpec((None, 1, BIAS_CHUNK), lambda b, c: (b, 0, c))],
        out_specs=pl.BlockSpec((None, N_HEADS, N_REL_BUCKETS), lambda b, c: (b, 0, 0)),
        out_shape=_sds((3, N_HEADS, N_REL_BUCKETS), F32), compiler_params=_params("parallel", "arbitrary"),
    )(dbias, jnp.asarray(bucket))


PAIR = 2 * HEAD_DIM
N_PAIRS = N_HEADS // 2
_NT = (((1,), (1,)), ((), ()))
_TN = (((0,), (0,)), ((), ()))


def _low_lanes(shape):
    return lax.broadcasted_iota(jnp.int32, shape, 1) < HEAD_DIM


def _per_head(x, low):
    del low
    return x[:, 0:1], x[:, HEAD_DIM:HEAD_DIM + 1]


def attn_fwd_branch(qkvn, bias, branch, state, last, name):
    S = qkvn.shape[0]
    dil = DILATED_PAIRS[branch][1]
    L = S // dil
    nb = L // ATT_BLOCK
    view = qkvn.reshape(L, dil * 3 * D_MODEL)
    first = state is None

    def body(*refs):
        q_ref, kp_ref, kc_ref, vp_ref, vc_ref, b_ref = refs[:6]
        ins = refs[6:6 + (0 if first else 3)]
        outs = refs[6 + len(ins):]
        low = _low_lanes((ATT_BLOCK, PAIR))
        for hp in range(N_PAIRS):
            cols = slice(hp * PAIR, (hp + 1) * PAIR)
            q = q_ref[:, cols]
            k = jnp.concatenate([kp_ref[:, cols], kc_ref[:, cols]], axis=0)
            v = jnp.concatenate([vp_ref[:, cols], vc_ref[:, cols]], axis=0)
            pv, mx, den = [], [], []
            for hh in range(2):
                qh = jnp.where(low if hh == 0 else ~low, q, jnp.zeros_like(q))
                s = lax.dot_general(qh, k, _NT, preferred_element_type=F32) * (HEAD_DIM ** -0.5) + b_ref[2 * hp + hh]
                m = jnp.max(s, axis=-1, keepdims=True)
                p = jnp.exp(s - m)
                den.append(jnp.sum(p, axis=-1, keepdims=True))
                mx.append(m)
                pv.append(jnp.dot(p.astype(BF16), v, preferred_element_type=F32))
            acc = jnp.where(low, pv[0], pv[1])
            m = jnp.where(low, mx[0], mx[1])
            l = jnp.where(low, den[0], den[1])
            if not first:
                m_old = ins[1][:, cols]
                m_new = jnp.maximum(m_old, m)
                a_old, a_new = jnp.exp(m_old - m_new), jnp.exp(m - m_new)
                acc = ins[0][:, cols] * a_old + acc * a_new
                l = ins[2][:, cols] * a_old + l * a_new
                m = m_new
            if last:
                outs[0][:, cols] = (acc / l).astype(BF16)
                outs[1][:, cols] = m + jnp.log(l)
            else:
                outs[0][:, cols] = acc
                outs[1][:, cols] = m
                outs[2][:, cols] = l

    blk = lambda part, prev: pl.BlockSpec(
        (ATT_BLOCK, D_MODEL), lambda r, b: (jnp.maximum(b - 1, 0) if prev else b, 3 * r + part))
    st = pl.BlockSpec((ATT_BLOCK, D_MODEL), lambda r, b: (b, r))
    in_specs = [blk(0, False), blk(1, True), blk(1, False), blk(2, True), blk(2, False),
                pl.BlockSpec((None, N_HEADS, ATT_BLOCK, 2 * ATT_BLOCK), lambda r, b: (jnp.minimum(b, 1), 0, 0, 0))]
    args = [view] * 5 + [bias]
    if not first:
        in_specs += [st] * 3
        args += [s_.reshape(L, dil * D_MODEL) for s_ in state]
    wide = (L, dil * D_MODEL)
    if last:
        out_shape = [_sds(wide, BF16), _sds(wide, F32)]
    else:
        out_shape = [_sds(wide, F32)] * 3
    outs = pl.pallas_call(
        body, name=name, grid=(dil, nb), in_specs=in_specs, out_specs=[st] * len(out_shape), out_shape=out_shape,
        compiler_params=_params("parallel", "arbitrary"),
    )(*args)
    return tuple(o.reshape(S, D_MODEL) for o in outs)


def attn_bwd_branch(qkvn, o, do, lse, bias, branch, grads, name):
    S = qkvn.shape[0]
    dil = DILATED_PAIRS[branch][1]
    L = S // dil
    nb = L // ATT_BLOCK
    view = qkvn.reshape(L, dil * 3 * D_MODEL)
    first = grads is None
    n_in = 0 if first else 3

    def body(*refs):
        q_ref, kp_ref, kc_ref, vp_ref, vc_ref, o_ref, do_ref, lse_ref, b_ref = refs[:9]
        ins = refs[9:9 + n_in]
        dq_ref, dk_ref, dv_ref, db_ref, ck_ref, cv_ref = refs[9 + n_in:]
        b = pl.program_id(1)
        low = _low_lanes((ATT_BLOCK, PAIR))

        @pl.when((pl.program_id(0) == 0) & (b == 0))
        def _():
            db_ref[...] = jnp.zeros_like(db_ref)

        @pl.when(b == 0)
        def _():
            ck_ref[...] = jnp.zeros_like(ck_ref)
            cv_ref[...] = jnp.zeros_like(cv_ref)

        @pl.when(b < nb)
        def _():
            for hp in range(N_PAIRS):
                cols = slice(hp * PAIR, (hp + 1) * PAIR)
                q = q_ref[:, cols]
                k = jnp.concatenate([kp_ref[:, cols], kc_ref[:, cols]], axis=0)
                v = jnp.concatenate([vp_ref[:, cols], vc_ref[:, cols]], axis=0)
                dout = do_ref[:, cols]
                prod = dout.astype(F32) * o_ref[:, cols].astype(F32)
                lse_h = _per_head(lse_ref[:, cols], low)
                dq, dk, dv = [], None, None
                for hh in range(2):
                    keep = low if hh == 0 else ~low
                    qh = jnp.where(keep, q, jnp.zeros_like(q))
                    doh = jnp.where(keep, dout, jnp.zeros_like(dout))
                    delta = jnp.sum(jnp.where(keep, prod, 0.0), axis=-1, keepdims=True)
                    s = lax.dot_general(qh, k, _NT, preferred_element_type=F32) * (HEAD_DIM ** -0.5) + b_ref[2 * hp + hh]
                    p = jnp.exp(s - lse_h[hh])
                    dp = lax.dot_general(doh, v, _NT, preferred_element_type=F32)
                    ds = p * (dp - delta)
                    db_ref[2 * hp + hh] += ds
                    dsb = (ds * (HEAD_DIM ** -0.5)).astype(BF16)
                    dq.append(jnp.dot(dsb, k, preferred_element_type=F32))
                    dkh = lax.dot_general(dsb, qh, _TN, preferred_element_type=F32)
                    dvh = lax.dot_general(p.astype(BF16), doh, _TN, preferred_element_type=F32)
                    dk = dkh if dk is None else dk + dkh
                    dv = dvh if dv is None else dv + dvh
                dq_new = jnp.where(low, dq[0], dq[1])
                dk_prev = ck_ref[:, cols] + dk[:ATT_BLOCK]
                dv_prev = cv_ref[:, cols] + dv[:ATT_BLOCK]
                if not first:
                    dq_new = dq_new + ins[0][:, cols]
                    dk_prev = dk_prev + ins[1][:, cols]
                    dv_prev = dv_prev + ins[2][:, cols]
                dq_ref[:, cols] = dq_new
                dk_ref[:, cols] = dk_prev
                dv_ref[:, cols] = dv_prev
                ck_ref[:, cols] = dk[ATT_BLOCK:]
                cv_ref[:, cols] = dv[ATT_BLOCK:]

        @pl.when(b == nb)
        def _():
            dk_last, dv_last = ck_ref[...], cv_ref[...]
            if not first:
                dk_last = dk_last + ins[1][...]
                dv_last = dv_last + ins[2][...]
            dk_ref[...] = dk_last
            dv_ref[...] = dv_last

    cur = lambda b: jnp.minimum(b, nb - 1)
    prev = lambda b: jnp.maximum(b - 1, 0)
    blk = lambda part, use_prev: pl.BlockSpec(
        (ATT_BLOCK, D_MODEL), lambda r, b: (prev(cur(b)) if use_prev else cur(b), 3 * r + part))
    qside = pl.BlockSpec((ATT_BLOCK, D_MODEL), lambda r, b: (cur(b), r))
    kside = pl.BlockSpec((ATT_BLOCK, D_MODEL), lambda r, b: (prev(b), r))
    bias_spec = pl.BlockSpec((None, N_HEADS, ATT_BLOCK, 2 * ATT_BLOCK), lambda r, b: (jnp.minimum(b, 1), 0, 0, 0))
    wide = (L, dil * D_MODEL)
    in_specs = [blk(0, False), blk(1, True), blk(1, False), blk(2, True), blk(2, False), qside, qside, qside, bias_spec]
    args = [view] * 5 + [o.reshape(wide), do.reshape(wide), lse.reshape(wide), bias]
    aliases = {}
    if not first:
        in_specs += [qside, kside, kside]
        args += [g.reshape(wide) for g in grads]
        aliases = {9: 0, 10: 1, 11: 2}
    dq, dk, dv, db = pl.pallas_call(
        body, name=name, grid=(dil, nb + 1), in_specs=in_specs,
        out_specs=[qside, kside, kside, pl.BlockSpec((N_HEADS, ATT_BLOCK, 2 * ATT_BLOCK), lambda r, b: (0, 0, 0))],
        out_shape=[_sds(wide, F32)] * 3 + [_sds((N_HEADS, ATT_BLOCK, 2 * ATT_BLOCK), F32)],
        scratch_shapes=[pltpu.VMEM((ATT_BLOCK, D_MODEL), F32), pltpu.VMEM((ATT_BLOCK, D_MODEL), F32)],
        input_output_aliases=aliases, compiler_params=_params("arbitrary", "arbitrary"),
    )(*args)
    return (dq.reshape(S, D_MODEL), dk.reshape(S, D_MODEL), dv.reshape(S, D_MODEL)), db


def loss_grad(y, target, name, ts=512):
    S, Dm = y.shape

    def body(y_ref, t_ref, d_ref, db_ref, s_ref):
        e = y_ref[...] - t_ref[...]
        d = e * (1.0 / Dm)
        d_ref[...] = d
        db_ref[...] = d.astype(BF16)
        part = jnp.sum(e * e, axis=0, keepdims=True)

        @pl.when(pl.program_id(0) == 0)
        def _():
            s_ref[...] = part

        @pl.when(pl.program_id(0) > 0)
        def _():
            s_ref[...] += part

    row = pl.BlockSpec((ts, Dm), lambda i: (i, 0))
    vec = pl.BlockSpec((1, Dm), lambda i: (0, 0))
    return pl.pallas_call(
        body, name=name, grid=(S // ts,), in_specs=[row, row], out_specs=[row, row, vec],
        out_shape=[_sds((S, Dm), F32), _sds((S, Dm), BF16), _sds((1, Dm), F32)], compiler_params=_params("arbitrary"),
    )(y, target)


def adamw(w, g, m, v, name):
    n, R, C = w.shape

    def body(w_ref, g_ref, m_ref, v_ref, d_ref, nm_ref, nv_ref):
        gv = g_ref[...]
        m2 = ADAM_B1 * m_ref[...] + (1.0 - ADAM_B1) * gv
        v2 = ADAM_B2 * v_ref[...] + (1.0 - ADAM_B2) * (gv * gv)
        m_hat = m2 / (1.0 - ADAM_B1 ** ADAM_STEP)
        v_hat = v2 / (1.0 - ADAM_B2 ** ADAM_STEP)
        d_ref[...] = -ADAM_LR * (m_hat / (jnp.sqrt(v_hat) + ADAM_EPS) + ADAM_WD * w_ref[...])
        nm_ref[...] = m2
        nv_ref[...] = v2

    tr = R
    while tr * C * 4 > (1 << 21) and tr % 16 == 0:
        tr //= 2
    spec = pl.BlockSpec((None, tr, C), lambda i, r: (i, r, 0))
    return pl.pallas_call(
        body, name=name, grid=(n, R // tr), in_specs=[spec] * 4, out_specs=[spec] * 3,
        out_shape=[_sds((n, R, C), F32)] * 3, compiler_params=_params("parallel", "parallel"),
    )(w, g, m, v)


def pair_sum(g, recv, core, name):
    _, _, hR, C = g.shape

    def body(c_ref, g_ref, r_ref, o_ref):
        del c_ref
        o_ref[...] = (g_ref[...] + r_ref[...]).astype(BF16)

    grid_spec = pltpu.PrefetchScalarGridSpec(
        num_scalar_prefetch=1, grid=(N_CHIPS,),
        in_specs=[pl.BlockSpec((None, None, hR, C), lambda j, c: (j, c[0], 0, 0)),
                  pl.BlockSpec((None, hR, C), lambda j, c: (j, 0, 0))],
        out_specs=pl.BlockSpec((None, hR, C), lambda j, c: (j, 0, 0)))
    return pl.pallas_call(body, name=name, grid_spec=grid_spec, out_shape=_sds((N_CHIPS, hR, C), BF16),
                          compiler_params=_params("parallel"))(core, g, recv)


def sum_slots(parts, name, out_dtype=F32):
    n, R, C = parts.shape
    tr = R
    while tr * C * n * parts.dtype.itemsize > (1 << 22) and tr % 32 == 0:
        tr //= 2

    def body(p_ref, o_ref):
        acc = p_ref[0].astype(F32)
        for q in range(1, n):
            acc = acc + p_ref[q].astype(F32)
        o_ref[...] = acc.astype(out_dtype)

    return pl.pallas_call(
        body, name=name, grid=(R // tr,), in_specs=[pl.BlockSpec((n, tr, C), lambda i: (0, i, 0))],
        out_specs=pl.BlockSpec((tr, C), lambda i: (i, 0)), out_shape=_sds((R, C), out_dtype),
        compiler_params=_params("parallel"),
    )(parts)


ANY = pl.BlockSpec(memory_space=pl.ANY)


def _coords():
    return lax.axis_index("x"), lax.axis_index("y"), lax.axis_index("c")


def _other_chips(mx, my):
    return [(1 - mx, my), (mx, 1 - my), (1 - mx, 1 - my)]


def _remote(src, dst, send, recv, dev):
    return pltpu.make_async_remote_copy(src_ref=src, dst_ref=dst, send_sem=send, recv_sem=recv, device_id=dev,
                                        device_id_type=MESH)


def allgather_devices(x, name):
    R, C = x.shape

    def body(x_ref, o_ref, send, recv, local_sem):
        mx, my, mc = _coords()
        me = 4 * mx + 2 * my + mc
        local = pltpu.make_async_copy(x_ref, o_ref.at[me], local_sem)
        local.start()
        peers = []
        for k in range(1, N_DEV):
            px = 1 - mx if k & 4 else mx
            py = 1 - my if k & 2 else my
            pc = 1 - mc if k & 1 else mc
            peers.append((px, py, pc))
        sends = [_remote(x_ref, o_ref.at[me], send.at[k], recv.at[k], p) for k, p in enumerate(peers)]
        for cp in sends:
            cp.start()
        for k, (px, py, pc) in enumerate(peers):
            _remote(x_ref, o_ref.at[4 * px + 2 * py + pc], send.at[k], recv.at[k], (px, py, pc)).wait_recv()
        for cp in sends:
            cp.wait_send()
        local.wait()

    return pl.pallas_call(
        body, name=name, in_specs=[ANY], out_specs=ANY, out_shape=_sds((N_DEV, R, C), x.dtype),
        scratch_shapes=[pltpu.SemaphoreType.DMA((N_DEV - 1,)), pltpu.SemaphoreType.DMA((N_DEV - 1,)),
                        pltpu.SemaphoreType.DMA],
    )(x)


HBM = pl.BlockSpec(memory_space=pltpu.HBM)
SEM = pl.BlockSpec(memory_space=pltpu.SEMAPHORE)
_SPLIT_COPY = pltpu.CompilerParams(has_side_effects=pltpu.SideEffectType.DATAFLOW_SIDE_EFFECTING)


def _in_hbm(a):
    return pltpu.with_memory_space_constraint(a, pltpu.HBM)


def cast_into_slot(w, chip_core, name):
    _, hR, C = w.shape

    def body(s_ref, w_ref, o_ref):
        del s_ref
        o_ref[...] = w_ref[...].astype(BF16)

    grid_spec = pltpu.PrefetchScalarGridSpec(
        num_scalar_prefetch=1, grid=(2,),
        in_specs=[pl.BlockSpec((None, hR, C), lambda h, s: (h, 0, 0))],
        out_specs=pl.BlockSpec((None, None, hR, C), lambda h, s: (s[0], h, 0, 0)))
    return pl.pallas_call(body, name=name, grid_spec=grid_spec, out_shape=_sds((N_CHIPS, 2, hR, C), BF16),
                          compiler_params=_params("parallel"))(chip_core, w)


def gather_start(lands, groups, name):
    n = len(lands)
    n_groups = len(groups)

    def body(*refs):
        ins = refs[:n]
        sems = refs[n:n + 2 * n_groups]
        token = refs[-1]
        mx, my, mc = _coords()
        chip = 2 * mx + my
        for g, members in enumerate(groups):
            send, recv = sems[2 * g], sems[2 * g + 1]
            for i, a in enumerate(members):
                mine = ins[a].at[chip, mc]
                for k, (px, py) in enumerate(_other_chips(mx, my)):
                    for pc in range(2):
                        _remote(mine, mine, send.at[6 * i + 2 * k + pc], recv.at[6 * i + 2 * k + mc], (px, py, pc)).start()
        token[...] = jnp.zeros_like(token)

    sem_shapes = []
    for members in groups:
        sem_shapes += [pltpu.SemaphoreType.DMA((6 * len(members),))] * 2
    outs = pl.pallas_call(
        body, name=name, in_specs=[HBM] * n,
        out_specs=[SEM] * (2 * n_groups) + [HBM] * n + [pl.BlockSpec(memory_space=pltpu.VMEM)],
        out_shape=sem_shapes + [pltpu.HBM(a.shape, a.dtype) for a in lands] + [_sds((SUBLANES, LANES), F32)],
        input_output_aliases={a: 2 * n_groups + a for a in range(n)}, compiler_params=_SPLIT_COPY,
    )(*[_in_hbm(a) for a in lands])
    sems = [(outs[2 * g], outs[2 * g + 1]) for g in range(n_groups)]
    return sems, list(outs[2 * n_groups:2 * n_groups + n]), outs[-1]


def gather_wait(lands, sems, after, name):
    n = len(lands)

    def body(*refs):
        ins = refs[:n]
        send, recv = refs[n], refs[n + 1]
        mx, my, _ = _coords()
        for i in range(n):
            for k, (px, py) in enumerate(_other_chips(mx, my)):
                for pc in range(2):
                    slot = ins[i].at[2 * px + py, pc]
                    cp = _remote(slot, slot, send.at[6 * i + 2 * k + pc], recv.at[6 * i + 2 * k + pc], (px, py, pc))
                    cp.wait_send()
                    cp.wait_recv()

    outs = pl.pallas_call(
        body, name=name, in_specs=[HBM] * n + [SEM, SEM, ANY], out_specs=[HBM] * n,
        out_shape=[pltpu.HBM(a.shape, a.dtype) for a in lands],
        input_output_aliases={a: a for a in range(n)}, compiler_params=_SPLIT_COPY,
    )(*lands, sems[0], sems[1], after)
    return list(outs)


def exchange_pair(grads, name):
    n = len(grads)

    def body(*refs):
        ins, outs = refs[:n], refs[n:2 * n]
        send, recv = refs[2 * n:]
        mx, my, mc = _coords()
        copies = [_remote(ins[a].at[:, 1 - mc], outs[a], send.at[a], recv.at[a], (mx, my, 1 - mc)) for a in range(n)]
        for cp in copies:
            cp.start()
        for cp in copies:
            cp.wait()

    return pl.pallas_call(
        body, name=name, in_specs=[ANY] * n, out_specs=[ANY] * n,
        out_shape=[_sds((N_CHIPS,) + g.shape[2:], g.dtype) for g in grads],
        scratch_shapes=[pltpu.SemaphoreType.DMA((n,)), pltpu.SemaphoreType.DMA((n,))],
    )(*grads)


def _peers(mx, my, mc):
    return [(1 - mx if k & 4 else mx, 1 - my if k & 2 else my, 1 - mc if k & 1 else mc) for k in range(1, N_DEV)]


def devices_start(x, name):
    def body(x_ref, land_ref, send, recv, x_thru, land_thru):
        mx, my, mc = _coords()
        me = 4 * mx + 2 * my + mc
        for k, peer in enumerate(_peers(mx, my, mc)):
            _remote(x_ref, land_ref.at[me], send.at[k], recv.at[k], peer).start()

    land = lax.empty((N_DEV,) + x.shape, x.dtype)
    outs = pl.pallas_call(
        body, name=name, in_specs=[HBM, HBM], out_specs=[SEM, SEM, HBM, HBM],
        out_shape=[pltpu.SemaphoreType.DMA((N_DEV - 1,))] * 2 + [pltpu.HBM(x.shape, x.dtype), pltpu.HBM(land.shape, x.dtype)],
        input_output_aliases={0: 2, 1: 3}, compiler_params=_SPLIT_COPY,
    )(_in_hbm(x), _in_hbm(land))
    return (outs[0], outs[1]), outs[2], outs[3]


def devices_wait(x, land, sems, after, name):
    def body(x_ref, land_ref, send, recv, after_ref, x_thru, land_thru):
        mx, my, mc = _coords()
        for k, (px, py, pc) in enumerate(_peers(mx, my, mc)):
            cp = _remote(x_ref, land_ref.at[4 * px + 2 * py + pc], send.at[k], recv.at[k], (px, py, pc))
            cp.wait_send()
            cp.wait_recv()

    outs = pl.pallas_call(
        body, name=name, in_specs=[HBM, HBM, SEM, SEM, ANY], out_specs=[HBM, HBM],
        out_shape=[pltpu.HBM(x.shape, x.dtype), pltpu.HBM(land.shape, land.dtype)],
        input_output_aliases={0: 0, 1: 1}, compiler_params=_SPLIT_COPY,
    )(x, land, sems[0], sems[1], after)
    return outs[0], outs[1]


def device_sum(land, own, me, name):
    _, R, C = land.shape

    def body(s_ref, l_ref, o_ref_in, o_ref):
        acc = None
        for q in range(N_DEV):
            term = jnp.where(s_ref[0] == q, o_ref_in[...], l_ref[q])
            acc = term if acc is None else acc + term
        o_ref[...] = acc

    grid_spec = pltpu.PrefetchScalarGridSpec(
        num_scalar_prefetch=1, grid=(1,),
        in_specs=[pl.BlockSpec((N_DEV, R, C), lambda i, s: (0, 0, 0)), pl.BlockSpec((R, C), lambda i, s: (0, 0))],
        out_specs=pl.BlockSpec((R, C), lambda i, s: (0, 0)))
    return pl.pallas_call(body, name=name, grid_spec=grid_spec, out_shape=_sds((R, C), F32),
                          compiler_params=_params("arbitrary"))(me, land, own)


def chips_start(parts, name):
    n = len(parts)

    def body(*refs):
        ins, lands = refs[:n], refs[n:2 * n]
        send, recv = refs[2 * n], refs[2 * n + 1]
        mx, my, mc = _coords()
        chip = 2 * mx + my
        for a in range(n):
            for k, (px, py) in enumerate(_other_chips(mx, my)):
                _remote(ins[a].at[2 * px + py], lands[a].at[chip], send.at[3 * a + k], recv.at[3 * a + k], (px, py, mc)).start()

    outs = pl.pallas_call(
        body, name=name, in_specs=[HBM] * (2 * n), out_specs=[SEM, SEM] + [HBM] * (2 * n),
        out_shape=[pltpu.SemaphoreType.DMA((3 * n,))] * 2 + [pltpu.HBM(p.shape, p.dtype) for p in parts] * 2,
        input_output_aliases={a: 2 + a for a in range(2 * n)}, compiler_params=_SPLIT_COPY,
    )(*[_in_hbm(p) for p in parts], *[_in_hbm(lax.empty(p.shape, p.dtype)) for p in parts])
    return (outs[0], outs[1]), list(outs[2:2 + n]), list(outs[2 + n:])


def chips_wait(parts, lands, sems, after, name):
    n = len(parts)

    def body(*refs):
        ins, zones = refs[:n], refs[n:2 * n]
        send, recv = refs[2 * n], refs[2 * n + 1]
        mx, my, mc = _coords()
        for a in range(n):
            for k, (px, py) in enumerate(_other_chips(mx, my)):
                cp = _remote(ins[a].at[2 * px + py], zones[a].at[2 * px + py], send.at[3 * a + k], recv.at[3 * a + k], (px, py, mc))
                cp.wait_send()
                cp.wait_recv()

    outs = pl.pallas_call(
        body, name=name, in_specs=[HBM] * (2 * n) + [SEM, SEM, ANY], out_specs=[HBM] * (2 * n),
        out_shape=[pltpu.HBM(p.shape, p.dtype) for p in parts] * 2,
        input_output_aliases={a: a for a in range(2 * n)}, compiler_params=_SPLIT_COPY,
    )(*parts, *lands, sems[0], sems[1], after)
    return list(outs[:n]), list(outs[n:])


def chip_sum(lands, parts, chip_core, name, into=None, layer=None):
    _, hR, C = lands.shape

    def body(s_ref, l_ref, p_ref, *rest):
        o_ref = rest[-1]
        chip = s_ref[0]
        own = p_ref[...].astype(F32)
        acc = None
        for q in range(N_CHIPS):
            term = jnp.where(chip == q, own, l_ref[q].astype(F32))
            acc = term if acc is None else acc + term
        o_ref[...] = acc

    in_specs = [pl.BlockSpec((N_CHIPS, hR, C), lambda i, s: (0, 0, 0)),
                pl.BlockSpec((None, hR, C), lambda i, s: (s[0], 0, 0))]
    args = [chip_core, lands, parts]
    aliases = {}
    if layer is None:
        out_spec = pl.BlockSpec((None, hR, C), lambda i, s: (s[1], 0, 0))
        out_shape = _sds((2, hR, C), F32)
    else:
        out_spec = pl.BlockSpec((None, None, hR, C), lambda i, s: (layer, s[1], 0, 0))
        out_shape = _sds((2, 2, hR, C), F32)
        if into is not None:
            in_specs.append(ANY)
            args.append(into)
            aliases = {3: 0}
    grid_spec = pltpu.PrefetchScalarGridSpec(num_scalar_prefetch=1, grid=(1,), in_specs=in_specs, out_specs=out_spec)
    return pl.pallas_call(body, name=name, grid_spec=grid_spec, out_shape=out_shape, input_output_aliases=aliases,
                          compiler_params=_params("arbitrary"))(*args)


def join_halves(arrays, name):
    n = len(arrays)
    pieces = [(a, l) for a, arr in enumerate(arrays) for l in (range(arr.shape[0]) if arr.ndim == 4 else [None])]

    def body(*refs):
        ins = refs[:n]
        send, recv = refs[2 * n:]
        mx, my, mc = _coords()

        def half(a, l, h):
            return ins[a].at[h] if l is None else ins[a].at[l, h]

        sends = [_remote(half(a, l, mc), half(a, l, mc), send.at[i], recv.at[i], (mx, my, 1 - mc))
                 for i, (a, l) in enumerate(pieces)]
        for cp in sends:
            cp.start()
        for i, (a, l) in enumerate(pieces):
            theirs = half(a, l, 1 - mc)
            _remote(theirs, theirs, send.at[i], recv.at[i], (mx, my, 1 - mc)).wait_recv()
        for cp in sends:
            cp.wait_send()

    return pl.pallas_call(
        body, name=name, in_specs=[ANY] * n, out_specs=[ANY] * n, out_shape=[_sds(a.shape, a.dtype) for a in arrays],
        input_output_aliases={a: a for a in range(n)},
        scratch_shapes=[pltpu.SemaphoreType.DMA((len(pieces),)), pltpu.SemaphoreType.DMA((len(pieces),))],
    )(*arrays)


LANES = 128
SUBLANES = 8


def _n_rows(shape):
    rows = -(-int(np.prod(shape)) // LANES)
    return -(-rows // SUBLANES) * SUBLANES


def _as_rows(a):
    flat = a.reshape(-1)
    rows = _n_rows(a.shape)
    return jnp.pad(flat, (0, rows * LANES - flat.shape[0])).reshape(rows, LANES)


def _pack(arrays):
    return jnp.concatenate([_as_rows(a) for a in arrays], axis=0)


def _unpack(rows, shapes):
    out, r0 = [], 0
    for s in shapes:
        n = _n_rows(s)
        out.append(rows[r0:r0 + n].reshape(-1)[:int(np.prod(s))].reshape(s))
        r0 += n
    return out


REPLICATED_SMALL = [("rel_bias", (32, 16)), ("even_norm", (1, 1024)), ("even_pool_w", (1, 4, 128, 128)),
                    ("even_pool_scale", (1, 512)), ("odd_q_norm", (1, 64)), ("odd_k_norm", (1, 64)),
                    ("ffn_norm", (2, 1024)), ("ffn_conv_b", (2, 5632))]
SHARDED_SMALL = [("even_conv_w", (1, 3, 128)), ("odd_norm", (1, 256)), ("ffn_conv_w", (2, 3, 1408))]
BIG = ["even_w_in", "even_w_out", "odd_w_qkv", "odd_w_o", "ffn_w_up", "ffn_w_down"]
WEIGHT_ORDER = ["rel_bias", "even_norm", "even_w_in", "even_conv_w", "even_pool_w", "even_pool_scale", "even_w_out",
                "odd_norm", "odd_w_qkv", "odd_q_norm", "odd_k_norm", "odd_w_o", "ffn_norm", "ffn_w_up", "ffn_conv_w",
                "ffn_conv_b", "ffn_w_down"]


def kernel(x, rel_bias, even_norm, even_w_in, even_conv_w, even_pool_w, even_pool_scale, even_w_out, odd_norm, odd_w_qkv, odd_q_norm, odd_k_norm, odd_w_o, ffn_norm, ffn_w_up, ffn_conv_w, ffn_conv_b, ffn_w_down, loss_target, m_rel_bias, m_even_norm, m_even_w_in, m_even_conv_w, m_even_pool_w, m_even_pool_scale, m_even_w_out, m_odd_norm, m_odd_w_qkv, m_odd_q_norm, m_odd_k_norm, m_odd_w_o, m_ffn_norm, m_ffn_w_up, m_ffn_conv_w, m_ffn_conv_b, m_ffn_w_down, v_rel_bias, v_even_norm, v_even_w_in, v_even_conv_w, v_even_pool_w, v_even_pool_scale, v_even_w_out, v_odd_norm, v_odd_w_qkv, v_odd_q_norm, v_odd_k_norm, v_odd_w_o, v_ffn_norm, v_ffn_w_up, v_ffn_conv_w, v_ffn_conv_b, v_ffn_w_down):
    W = dict(rel_bias=rel_bias, even_norm=even_norm, even_w_in=even_w_in, even_conv_w=even_conv_w, even_pool_w=even_pool_w,
             even_pool_scale=even_pool_scale, even_w_out=even_w_out, odd_norm=odd_norm, odd_w_qkv=odd_w_qkv,
             odd_q_norm=odd_q_norm, odd_k_norm=odd_k_norm, odd_w_o=odd_w_o, ffn_norm=ffn_norm, ffn_w_up=ffn_w_up,
             ffn_conv_w=ffn_conv_w, ffn_conv_b=ffn_conv_b, ffn_w_down=ffn_w_down)
    M1 = dict(rel_bias=m_rel_bias, even_norm=m_even_norm, even_w_in=m_even_w_in, even_conv_w=m_even_conv_w,
              even_pool_w=m_even_pool_w, even_pool_scale=m_even_pool_scale, even_w_out=m_even_w_out, odd_norm=m_odd_norm,
              odd_w_qkv=m_odd_w_qkv, odd_q_norm=m_odd_q_norm, odd_k_norm=m_odd_k_norm, odd_w_o=m_odd_w_o,
              ffn_norm=m_ffn_norm, ffn_w_up=m_ffn_w_up, ffn_conv_w=m_ffn_conv_w, ffn_conv_b=m_ffn_conv_b,
              ffn_w_down=m_ffn_w_down)
    M2 = dict(rel_bias=v_rel_bias, even_norm=v_even_norm, even_w_in=v_even_w_in, even_conv_w=v_even_conv_w,
              even_pool_w=v_even_pool_w, even_pool_scale=v_even_pool_scale, even_w_out=v_even_w_out, odd_norm=v_odd_norm,
              odd_w_qkv=v_odd_w_qkv, odd_q_norm=v_odd_q_norm, odd_k_norm=v_odd_k_norm, odd_w_o=v_odd_w_o,
              ffn_norm=v_ffn_norm, ffn_w_up=v_ffn_w_up, ffn_conv_w=v_ffn_conv_w, ffn_conv_b=v_ffn_conv_b,
              ffn_w_down=v_ffn_w_down)
    mx, my, mc = _coords()
    chip = 2 * mx + my
    core = jnp.reshape(mc, (1,)).astype(jnp.int32)
    xs, target = x[0], loss_target[0]
    S = xs.shape[0]

    def halves(w):
        return w.reshape((2, w.shape[-2] // 2, w.shape[-1]))

    chip_core = jnp.stack([chip, mc]).astype(jnp.int32)
    lands = [cast_into_slot(halves(even_w_in), chip_core, "cast_w_in"), cast_into_slot(halves(even_w_out), chip_core, "cast_w_out"),
             cast_into_slot(ffn_w_up, chip_core, "cast_w_up"), cast_into_slot(ffn_w_down, chip_core, "cast_w_down"),
             cast_into_slot(halves(odd_w_qkv), chip_core, "cast_w_qkv"), cast_into_slot(halves(odd_w_o), chip_core, "cast_w_o")]
    small = allgather_devices(_pack([even_conv_w, odd_norm, ffn_conv_w]), "allgather_small_weights")
    lands[0], small = lax.optimization_barrier((lands[0], small))
    gather_sems, lands, token = gather_start(lands, [[0, 1], [2, 3], [4, 5]], "gather_start")
    even_norm_after_start = even_norm + token[0:1, 0:1]
    small = small[0::2]
    conv_w_full = small[:, 0:3].transpose(1, 0, 2).reshape(3, A_WIDTH)
    odd_norm_full = small[:, 8:10].reshape(1, D_MODEL)
    ffn_cw_full = small[:, 16:82].reshape(N_CHIPS, 2, 3, 2 * D_FF // N_CHIPS).transpose(1, 2, 0, 3).reshape(2, 3, 2 * D_FF)
    pool_w = cast_bf16(even_pool_w[0], "cast_pool_w")
    gqk = jnp.stack([jnp.tile(odd_q_norm[0], N_HEADS), jnp.tile(odd_k_norm[0], N_HEADS),
                     jnp.ones((D_MODEL,), F32)])[:, None, :]
    bias = bias_expand(rel_bias.T, "bias_expand").reshape(3, 2, N_HEADS, ATT_BLOCK, 2 * ATT_BLOCK)

    def ffn_fwd(l, xin):
        xn = rmsnorm_fwd(xin, ffn_norm[l:l + 1], f"ffn{l}_norm")
        up = mm_nn(xn, w_up, f"ffn{l}_up", layer=l)
        act = glu_fwd(up, ffn_cw_full[l], ffn_conv_b[l:l + 1], f"ffn{l}_glu")
        return mm_nn(act, w_down, f"ffn{l}_down", layer=l, res=xin), (xin, xn, up, act)

    xn0 = rmsnorm_fwd(xs, even_norm_after_start, "even_norm")
    got = gather_wait(lands[0:2], gather_sems[0], bias, "gather_wait_even")
    w_in = got[0].reshape(N_CHIPS, 1, D_MODEL, EVEN_IN // N_CHIPS)
    w_out = got[1].reshape(1, 1, D_MODEL, D_MODEL)
    proj = mm_nn(xn0, w_in, "even_in")
    mix = mixer_fwd(proj, conv_w_full, pool_w, even_pool_scale, "even_mixer")
    x1 = mm_nn(mix, w_out, "even_out", res=xs)
    got = gather_wait(lands[2:4], gather_sems[1], x1, "gather_wait_ffn")
    w_up = got[0]
    w_down = got[1].transpose(1, 0, 2, 3).reshape(1, 2, D_FF, D_MODEL)
    x2, ffn0 = ffn_fwd(0, x1)
    got = gather_wait(lands[4:6], gather_sems[2], x2, "gather_wait_odd")
    w_qkv = got[0].reshape(N_CHIPS, 1, D_MODEL, 3 * D_MODEL // N_CHIPS)
    w_o = got[1].reshape(1, 1, D_MODEL, D_MODEL)
    xn2 = rmsnorm_fwd(x2, odd_norm_full, "odd_norm")
    qkv = mm_nn(xn2, w_qkv, "odd_qkv")
    qkvn = qknorm_fwd(qkv, gqk, "odd_qknorm")
    state = None
    for br in range(3):
        state = attn_fwd_branch(qkvn, bias[br], br, state, br == 2, f"attn_fwd{br}")
    att, lse = state
    x3 = mm_nn(att, w_o, "odd_out", res=x2)
    x4, ffn1 = ffn_fwd(1, x3)

    dy, dyb, sq = loss_grad(x4, target, "loss")
    loss = lax.psum(0.5 * jnp.sum(sq) * (1.0 / D_MODEL), ("x", "y", "c"))

    def ffn_bwd(l, dy, dyb, saved):
        xin, xn, up, act = saved
        dw_down = mm_tn(act, dyb, f"ffn{l}_dw_down", J=1, tk=D_FF // 2)
        dact = mm_nt(dyb, w_down, f"ffn{l}_dact", tr=D_FF // 2, layer=l)
        dup, dcw, dcb = glu_bwd(up, dact, ffn_cw_full[l], ffn_conv_b[l:l + 1], f"ffn{l}_glu_bwd")
        dw_up = mm_tn(xn, dup, f"ffn{l}_dw_up", J=N_CHIPS, tk=512)
        dxn = mm_nt(dup, w_up, f"ffn{l}_dxn", tr=D_MODEL, layer=l)
        dx, dxb, dg = rmsnorm_bwd(xin, ffn_norm[l:l + 1], dxn, dy, f"ffn{l}_norm_bwd")
        return dx, dxb, (dw_down, dw_up, dcw, dcb, dg)

    def quarters(g):
        return g.reshape(N_CHIPS, 2, g.shape[0] * g.shape[1] // (2 * N_CHIPS), g.shape[-1])

    def reduce_start(grads, tag, then):
        qs = [quarters(g) for g in grads]
        theirs = exchange_pair(qs, "pair_exchange_" + tag)
        parts = [pair_sum(g, r, core, f"pair_sum_{tag}{i}") for i, (g, r) in enumerate(zip(qs, theirs))]
        sems, parts, zones = chips_start(parts, "chips_start_" + tag)
        then, parts = lax.optimization_barrier((then, parts))
        return (sems, parts, zones), then

    dx3, dx3b, g_ffn1 = ffn_bwd(1, dy, dyb, ffn1)
    red_ffn1, (dx3, dx3b) = reduce_start([g_ffn1[1], g_ffn1[0]], "ffn1", (dx3, dx3b))
    dw_o = mm_tn(att, dx3b, "odd_dw_o", J=1, tk=512)
    datt = mm_nt(dx3b, w_o, "odd_datt", tr=D_MODEL, out_dtype=BF16)
    grads, dbias = None, []
    for br in range(3):
        grads, db = attn_bwd_branch(qkvn, att, datt, lse, bias[br], br, grads, f"attn_bwd{br}")
        dbias.append(db.reshape(N_HEADS, 2 * ATT_BLOCK * ATT_BLOCK))
    dqkv, dgqk = qknorm_bwd(qkv, grads[0], grads[1], grads[2], gqk, "odd_qknorm_bwd")
    dw_qkv = mm_tn(xn2, dqkv, "odd_dw_qkv", J=N_CHIPS, tk=512)
    dxn2 = mm_nt(dqkv, w_qkv, "odd_dxn", tr=D_MODEL)
    red_odd, dxn2 = reduce_start([dw_qkv, dw_o], "odd", dxn2)
    dx2, dx2b, dg_odd = rmsnorm_bwd(x2, odd_norm_full, dxn2, dx3, "odd_norm_bwd")
    dx1, dx1b, g_ffn0 = ffn_bwd(0, dx2, dx2b, ffn0)
    red_ffn0, (dx1, dx1b) = reduce_start([g_ffn0[1], g_ffn0[0]], "ffn0", (dx1, dx1b))
    dw_out = mm_tn(mix, dx1b, "even_dw_out", J=1, tk=512)
    dmix = mm_nt(dx1b, w_out, "even_dmix", tr=D_MODEL)
    dproj, dcw_even, dpw, dps = mixer_bwd(proj, dmix, conv_w_full, pool_w, even_pool_scale, "even_mixer_bwd")
    dw_in = mm_tn(xn0, dproj, "even_dw_in", J=N_CHIPS, tk=512)
    dxn0 = mm_nt(dproj, w_in, "even_dxn", tr=D_MODEL)
    grad_x, _, dg_even = rmsnorm_bwd(xs, even_norm, dxn0, dx1, "even_norm_bwd")
    d_rel = jnp.sum(bias_reduce(jnp.stack(dbias), "bias_reduce"), axis=0).T

    red_even, grad_x = reduce_start([dw_in, dw_out], "even", grad_x)

    dcw_sh = dcw_even.reshape(3, N_CHIPS, A_WIDTH // N_CHIPS).transpose(1, 0, 2)
    don_sh = dg_odd.reshape(N_CHIPS, D_MODEL // N_CHIPS)
    dfcw = jnp.stack([g_ffn0[2], g_ffn1[2]])
    dfcw_sh = dfcw.reshape(2, 3, N_CHIPS, 2 * D_FF // N_CHIPS).transpose(2, 0, 1, 3)
    rep_grads = [d_rel, dg_even, dpw[None], dps, _head_sum(dgqk[0]), _head_sum(dgqk[1]),
                 jnp.concatenate([g_ffn0[4], g_ffn1[4]], axis=0), jnp.concatenate([g_ffn0[3], g_ffn1[3]], axis=0)]
    rep_rows = _pack(rep_grads)
    shard_rows = jnp.concatenate([_pack([dcw_sh[j], don_sh[j], dfcw_sh[j]]) for j in range(N_CHIPS)], axis=0)
    n_rep, n_shard = rep_rows.shape[0], shard_rows.shape[0] // N_CHIPS
    small_sems, small_rows, small_land = devices_start(jnp.concatenate([rep_rows, shard_rows], axis=0), "small_grads_start")
    grad_x, small_rows = lax.optimization_barrier((grad_x, small_rows))

    def reduce_wait(red, tag, after):
        sems, parts, zones = red
        parts, zones = chips_wait(parts, zones, sems, after, "chips_wait_" + tag)
        return zones, parts

    z_ffn1, p_ffn1 = reduce_wait(red_ffn1, "ffn1", grad_x)
    z_odd, p_odd = reduce_wait(red_odd, "odd", grad_x)
    z_ffn0, p_ffn0 = reduce_wait(red_ffn0, "ffn0", grad_x)
    r_up = chip_sum(z_ffn0[0], p_ffn0[0], chip_core, "chip_sum_w_up0", layer=0)
    r_up = chip_sum(z_ffn1[0], p_ffn1[0], chip_core, "chip_sum_w_up1", into=r_up, layer=1)
    r_down = chip_sum(z_ffn0[1], p_ffn0[1], chip_core, "chip_sum_w_down0", layer=0)
    r_down = chip_sum(z_ffn1[1], p_ffn1[1], chip_core, "chip_sum_w_down1", into=r_down, layer=1)
    z_even, p_even = reduce_wait(red_even, "even", r_down)
    halves_written = [chip_sum(z_even[0], p_even[0], chip_core, "chip_sum_w_in"),
                      chip_sum(z_even[1], p_even[1], chip_core, "chip_sum_w_out"),
                      chip_sum(z_odd[0], p_odd[0], chip_core, "chip_sum_w_qkv"),
                      chip_sum(z_odd[1], p_odd[1], chip_core, "chip_sum_w_o"), r_up, r_down]
    joined = join_halves(halves_written, "grads_join_halves")
    G = {nm: g.reshape(W[nm].shape) for nm, g in zip(BIG, joined)}

    D_, NM, NV = {}, {}, {}
    for nm in BIG:
        as3 = lambda a: a.reshape((-1,) + a.shape[-2:])
        outs = adamw(as3(W[nm]), as3(G[nm]), as3(M1[nm]), as3(M2[nm]), "adamw_" + nm)
        D_[nm], NM[nm], NV[nm] = [o.reshape(W[nm].shape) for o in outs]
    small_rows, small_land = devices_wait(small_rows, small_land, small_sems, D_[BIG[-1]], "small_grads_wait")
    me = jnp.reshape(4 * mx + 2 * my + mc, (1,)).astype(jnp.int32)
    small_sum = device_sum(small_land, small_rows, me, "small_grads_sum")
    mine = lax.dynamic_slice_in_dim(small_sum, n_rep + chip * n_shard, n_shard, axis=0)
    g_small = jnp.concatenate([small_sum[:n_rep], mine], axis=0)
    small_names = [n for n, _ in REPLICATED_SMALL + SHARDED_SMALL]
    small_shapes = [s for _, s in REPLICATED_SMALL + SHARDED_SMALL]
    G.update(dict(zip(small_names, _unpack(g_small, small_shapes))))
    packs = [_pack([d[n] for n in small_names])[None] for d in (W, M1, M2)]
    outs = adamw(packs[0], g_small[None], packs[1], packs[2], "adamw_small")
    for dst, o in zip((D_, NM, NV), outs):
        dst.update(dict(zip(small_names, _unpack(o[0], small_shapes))))

    return (loss, grad_x[None], *[G[n] for n in WEIGHT_ORDER], *[D_[n] for n in WEIGHT_ORDER],
            *[NM[n] for n in WEIGHT_ORDER], *[NV[n] for n in WEIGHT_ORDER])


def _head_sum(dg):
    return jnp.sum(dg.reshape(N_HEADS, HEAD_DIM), axis=0, keepdims=True)
```

```python
import functools
import math

import numpy as np
import jax
import jax.numpy as jnp
from jax import lax
from jax.experimental import pallas as pl
from jax.experimental.pallas import tpu as pltpu

F32 = jnp.float32
BF16 = jnp.bfloat16

D_MODEL = 1024
N_HEADS = 16
HEAD_DIM = 64
A_WIDTH = 512
POOL_WINDOWS = (2, 4, 8, 16)
POOL_GROUP = 128
EVEN_IN = 2048
D_FF = 2816
DILATED_PAIRS = ((128, 1), (512, 4), (2048, 16))
ATT_BLOCK = 128
N_REL_BUCKETS = 32
REL_MAX_DISTANCE = 2048
EPS = 1e-6
MASK_VALUE = -1e30
ADAM_LR, ADAM_B1, ADAM_B2, ADAM_EPS, ADAM_WD, ADAM_STEP = 0.001, 0.9, 0.999, 1e-08, 0.01, 10

VMEM_LIMIT_BYTES = 48 * 1024 * 1024
N_CHIPS = 4
N_DEV = 8
MESH = pl.DeviceIdType.MESH


def _params(*sem):
    return pltpu.CompilerParams(dimension_semantics=sem if sem else None, vmem_limit_bytes=VMEM_LIMIT_BYTES)


def _sds(shape, dtype):
    return jax.ShapeDtypeStruct(tuple(shape), dtype)


def cast_bf16(x, name, tr=None):
    lead, (R, C) = x.shape[:-2], x.shape[-2:]
    n = int(np.prod(lead)) if lead else 1
    x3 = x.reshape((n, R, C))
    tr = tr or R

    def body(x_ref, o_ref):
        o_ref[...] = x_ref[...].astype(BF16)

    out = pl.pallas_call(
        body, name=name, grid=(n, R // tr),
        in_specs=[pl.BlockSpec((None, tr, C), lambda i, r: (i, r, 0))],
        out_specs=pl.BlockSpec((None, tr, C), lambda i, r: (i, r, 0)),
        out_shape=_sds((n, R, C), BF16), compiler_params=_params("parallel", "parallel"),
    )(x3)
    return out.reshape(lead + (R, C))


def rmsnorm_fwd(x, g, name, ts=512):
    S, Dm = x.shape

    def body(x_ref, g_ref, o_ref):
        xv = x_ref[...]
        r = lax.rsqrt(jnp.mean(xv * xv, axis=-1, keepdims=True) + EPS)
        o_ref[...] = ((xv * r) * g_ref[...]).astype(BF16)

    return pl.pallas_call(
        body, name=name, grid=(S // ts,),
        in_specs=[pl.BlockSpec((ts, Dm), lambda i: (i, 0)), pl.BlockSpec((1, Dm), lambda i: (0, 0))],
        out_specs=pl.BlockSpec((ts, Dm), lambda i: (i, 0)),
        out_shape=_sds((S, Dm), BF16), compiler_params=_params("parallel"),
    )(x, g)


def rmsnorm_bwd(x, g, dxn, dres, name, ts=512):
    S, Dm = x.shape

    def body(x_ref, g_ref, d_ref, r_ref, dx_ref, dxb_ref, dg_ref):
        xv = x_ref[...]
        dv = d_ref[...].astype(F32)
        r = lax.rsqrt(jnp.mean(xv * xv, axis=-1, keepdims=True) + EPS)
        gx = dv * g_ref[...]
        dot = jnp.sum(gx * xv, axis=-1, keepdims=True)
        dx = r_ref[...] + r * gx - xv * ((r * r * r) * (dot * (1.0 / Dm)))
        dx_ref[...] = dx
        dxb_ref[...] = dx.astype(BF16)
        part = jnp.sum(dv * (xv * r), axis=0, keepdims=True)

        @pl.when(pl.program_id(0) == 0)
        def _():
            dg_ref[...] = part

        @pl.when(pl.program_id(0) > 0)
        def _():
            dg_ref[...] += part

    row = pl.BlockSpec((ts, Dm), lambda i: (i, 0))
    vec = pl.BlockSpec((1, Dm), lambda i: (0, 0))
    return pl.pallas_call(
        body, name=name, grid=(S // ts,),
        in_specs=[row, vec, row, row], out_specs=[row, row, vec],
        out_shape=[_sds((S, Dm), F32), _sds((S, Dm), BF16), _sds((1, Dm), F32)], compiler_params=_params("arbitrary"),
    )(x, g, dxn, dres)


def mm_nn(a, w, name, layer=0, res=None, out_dtype=F32, tm=512):
    M, K = a.shape
    J, _, _, Ns = w.shape

    def body(*refs):
        a_ref, w_ref = refs[0], refs[1]
        o_ref = refs[-1]
        acc = jnp.dot(a_ref[...], w_ref[...], preferred_element_type=F32)
        if res is not None:
            acc = refs[2][...] + acc
        o_ref[...] = acc.astype(o_ref.dtype)

    in_specs = [pl.BlockSpec((tm, K), lambda j, m: (m, 0)),
                pl.BlockSpec((None, None, K, Ns), lambda j, m: (j, layer, 0, 0))]
    args = [a, w]
    if res is not None:
        in_specs.append(pl.BlockSpec((tm, Ns), lambda j, m: (m, j)))
        args.append(res)
    return pl.pallas_call(
        body, name=name, grid=(J, M // tm), in_specs=in_specs,
        out_specs=pl.BlockSpec((tm, Ns), lambda j, m: (m, j)),
        out_shape=_sds((M, J * Ns), out_dtype), compiler_params=_params("parallel", "parallel"),
    )(*args)


def mm_nt(dy, w, name, tr, layer=0, out_dtype=F32, tm=512):
    M = dy.shape[0]
    J, _, R, Ns = w.shape
    dims = (((1,), (1,)), ((), ()))

    def body(dy_ref, w_ref, o_ref, *scratch):
        p = lax.dot_general(dy_ref[...], w_ref[...], dims, preferred_element_type=F32)
        if J == 1:
            o_ref[...] = p.astype(o_ref.dtype)
            return
        acc_ref, = scratch
        j = pl.program_id(2)

        @pl.when(j == 0)
        def _():
            acc_ref[...] = p

        @pl.when(j > 0)
        def _():
            acc_ref[...] += p

        @pl.when(j == J - 1)
        def _():
            o_ref[...] = acc_ref[...].astype(o_ref.dtype)

    return pl.pallas_call(
        body, name=name, grid=(R // tr, M // tm, J),
        in_specs=[pl.BlockSpec((tm, Ns), lambda r, m, j: (m, j)),
                  pl.BlockSpec((None, None, tr, Ns), lambda r, m, j: (j, layer, r, 0))],
        out_specs=pl.BlockSpec((tm, tr), lambda r, m, j: (m, r)),
        out_shape=_sds((M, R), out_dtype),
        scratch_shapes=[] if J == 1 else [pltpu.VMEM((tm, tr), F32)],
        compiler_params=_params("parallel", "parallel", "arbitrary"),
    )(dy, w)


def mm_tn(a, dy, name, J, tk, tm=512):
    M, K = a.shape
    Ns = dy.shape[1] // J
    n_m = M // tm
    dims = (((0,), (0,)), ((), ()))

    def body(a_ref, dy_ref, o_ref, acc_ref):
        p = lax.dot_general(a_ref[...], dy_ref[...], dims, preferred_element_type=F32)
        m = pl.program_id(2)

        @pl.when(m == 0)
        def _():
            acc_ref[...] = p

        @pl.when(m > 0)
        def _():
            acc_ref[...] += p

        @pl.when(m == n_m - 1)
        def _():
            o_ref[...] = acc_ref[...].astype(BF16)

    return pl.pallas_call(
        body, name=name, grid=(J, K // tk, n_m),
        in_specs=[pl.BlockSpec((tm, tk), lambda j, k, m: (m, k)), pl.BlockSpec((tm, Ns), lambda j, k, m: (m, j))],
        out_specs=pl.BlockSpec((None, tk, Ns), lambda j, k, m: (j, k, 0)),
        out_shape=_sds((J, K, Ns), BF16), scratch_shapes=[pltpu.VMEM((tk, Ns), F32)],
        compiler_params=_params("parallel", "parallel", "arbitrary"),
    )(a, dy)


HALO = 16


def _shift_down(x, s):
    return pltpu.roll(x, s, 0)


def _shift_up(x, s):
    return pltpu.roll(x, x.shape[0] - s, 0)


def _conv3(z, cw):
    return (_shift_down(z, 2) * cw[0:1] + _shift_down(z, 1) * cw[1:2]) + z * cw[2:3]


def _window_count(first_row, n, k):
    t = first_row + lax.broadcasted_iota(jnp.int32, (n, 1), 0)
    return jnp.clip(t + 1, 1, k).astype(F32)


def mixer_fwd(proj, conv_w, pool_w, pool_scale, name, ts=256):
    S = proj.shape[0]
    n = ts + HALO

    def body(pm_ref, pb_ref, cw_ref, pw_ref, ps_ref, o_ref):
        i = pl.program_id(0)
        before = jnp.where(i > 0, pb_ref[...], 0.0)
        ext = jnp.concatenate([before, pm_ref[...]], axis=0)
        cw = cw_ref[...]
        z = ext[:, 2 * A_WIDTH:3 * A_WIDTH] * ext[:, 0:A_WIDTH]
        cz = _conv3(z, cw)
        ya = pm_ref[:, A_WIDTH:2 * A_WIDTH] * cz[HALO:]
        o_ref[:, 0:A_WIDTH] = ya.astype(BF16)
        for g, k in enumerate(POOL_WINDOWS):
            lo = 3 * A_WIDTH + g * POOL_GROUP
            p = ext[:, lo:lo + POOL_GROUP]
            w = p
            s = 1
            while s < k:
                w = w + _shift_down(w, s)
                s *= 2
            pooled = w / _window_count(i * ts - HALO, n, k) - p
            yb = jnp.dot(pooled[HALO:].astype(BF16), pw_ref[g], preferred_element_type=F32)
            yb = yb * ps_ref[:, g * POOL_GROUP:(g + 1) * POOL_GROUP]
            o_ref[:, A_WIDTH + g * POOL_GROUP:A_WIDTH + (g + 1) * POOL_GROUP] = yb.astype(BF16)

    hb = ts // HALO
    return pl.pallas_call(
        body, name=name, grid=(S // ts,),
        in_specs=[
            pl.BlockSpec((ts, EVEN_IN), lambda i: (i, 0)),
            pl.BlockSpec((HALO, EVEN_IN), lambda i: (jnp.maximum(i * hb - 1, 0), 0)),
            pl.BlockSpec((3, A_WIDTH), lambda i: (0, 0)),
            pl.BlockSpec((4, POOL_GROUP, POOL_GROUP), lambda i: (0, 0, 0)),
            pl.BlockSpec((1, 4 * POOL_GROUP), lambda i: (0, 0)),
        ],
        out_specs=pl.BlockSpec((ts, D_MODEL), lambda i: (i, 0)),
        out_shape=_sds((S, D_MODEL), BF16), compiler_params=_params("parallel"),
    )(proj, proj, conv_w, pool_w, pool_scale)


def mixer_bwd(proj, dmix, conv_w, pool_w, pool_scale, name, ts=256):
    S = proj.shape[0]
    n = ts + 2 * HALO
    nt = S // ts
    tn_dims = (((0,), (0,)), ((), ()))
    nt_dims = (((1,), (1,)), ((), ()))

    def body(pm_ref, pb_ref, pa_ref, dm_ref, da_ref, cw_ref, pw_ref, ps_ref, o_ref, dcw_ref, dpw_ref, dps_ref):
        i = pl.program_id(0)
        last = i == nt - 1
        before = jnp.where(i > 0, pb_ref[...], 0.0)
        after = jnp.where(last, 0.0, pa_ref[...])
        ext = jnp.concatenate([before, pm_ref[...], after], axis=0)
        dafter = jnp.where(last, 0.0, da_ref[...])
        dext = jnp.concatenate([jnp.zeros((HALO, D_MODEL), F32), dm_ref[...], dafter], axis=0)
        cw = cw_ref[...]
        main = slice(HALO, HALO + ts)

        @pl.when(i == 0)
        def _():
            dcw_ref[...] = jnp.zeros_like(dcw_ref)
            dpw_ref[...] = jnp.zeros_like(dpw_ref)
            dps_ref[...] = jnp.zeros_like(dps_ref)

        h, gb, gc = ext[:, 0:A_WIDTH], ext[:, A_WIDTH:2 * A_WIDTH], ext[:, 2 * A_WIDTH:3 * A_WIDTH]
        z = gc * h
        z1, z2 = _shift_down(z, 1), _shift_down(z, 2)
        cz = (z2 * cw[0:1] + z1 * cw[1:2]) + z * cw[2:3]
        dya = dext[:, 0:A_WIDTH]
        dcz = dya * gb
        dz = dcz * cw[2:3] + _shift_up(dcz, 1) * cw[1:2] + _shift_up(dcz, 2) * cw[0:1]
        o_ref[:, 0:A_WIDTH] = (dz * gc)[main].astype(BF16)
        o_ref[:, A_WIDTH:2 * A_WIDTH] = (dya * cz)[main].astype(BF16)
        o_ref[:, 2 * A_WIDTH:3 * A_WIDTH] = (dz * h)[main].astype(BF16)
        dczm = dcz[main]
        dcw_ref[0:1, :] += jnp.sum(dczm * z2[main], axis=0, keepdims=True)
        dcw_ref[1:2, :] += jnp.sum(dczm * z1[main], axis=0, keepdims=True)
        dcw_ref[2:3, :] += jnp.sum(dczm * z[main], axis=0, keepdims=True)

        for g, k in enumerate(POOL_WINDOWS):
            lo = 3 * A_WIDTH + g * POOL_GROUP
            cols = slice(g * POOL_GROUP, (g + 1) * POOL_GROUP)
            p = ext[:, lo:lo + POOL_GROUP]
            w = p
            s = 1
            while s < k:
                w = w + _shift_down(w, s)
                s *= 2
            cnt = _window_count(i * ts - HALO, n, k)
            pooled = (w / cnt - p)[main].astype(BF16)
            dyb = dext[:, A_WIDTH + g * POOL_GROUP:A_WIDTH + (g + 1) * POOL_GROUP]
            e = dyb * ps_ref[:, cols]
            pre = jnp.dot(pooled, pw_ref[g], preferred_element_type=F32)
            dps_ref[:, cols] += jnp.sum(dyb[main] * pre, axis=0, keepdims=True)
            dpw_ref[g] += lax.dot_general(pooled, e[main].astype(BF16), tn_dims, preferred_element_type=F32)
            dpooled = lax.dot_general(e.astype(BF16), pw_ref[g], nt_dims, preferred_element_type=F32)
            q = dpooled / cnt
            a = q
            s = 1
            while s < k:
                a = a + _shift_up(a, s)
                s *= 2
            o_ref[:, lo:lo + POOL_GROUP] = (a - dpooled)[main].astype(BF16)

    hb = ts // HALO
    nh = S // HALO
    before_map = lambda i: (jnp.maximum(i * hb - 1, 0), 0)
    after_map = lambda i: (jnp.minimum((i + 1) * hb, nh - 1), 0)
    full = lambda *shape: pl.BlockSpec(shape, lambda i: (0,) * len(shape))
    return pl.pallas_call(
        body, name=name, grid=(nt,),
        in_specs=[
            pl.BlockSpec((ts, EVEN_IN), lambda i: (i, 0)),
            pl.BlockSpec((HALO, EVEN_IN), before_map),
            pl.BlockSpec((HALO, EVEN_IN), after_map),
            pl.BlockSpec((ts, D_MODEL), lambda i: (i, 0)),
            pl.BlockSpec((HALO, D_MODEL), after_map),
            full(3, A_WIDTH), full(4, POOL_GROUP, POOL_GROUP), full(1, 4 * POOL_GROUP),
        ],
        out_specs=[pl.BlockSpec((ts, EVEN_IN), lambda i: (i, 0)), full(3, A_WIDTH), full(4, POOL_GROUP, POOL_GROUP),
                   full(1, 4 * POOL_GROUP)],
        out_shape=[_sds((S, EVEN_IN), BF16), _sds((3, A_WIDTH), F32), _sds((4, POOL_GROUP, POOL_GROUP), F32),
                   _sds((1, 4 * POOL_GROUP), F32)],
        compiler_params=_params("arbitrary"),
    )(proj, proj, proj, dmix, dmix, conv_w, pool_w, pool_scale)


FFN_HALO = 8
FFN_TC = 1408


def glu_fwd(up, conv_w, conv_b, name, ts=256):
    S = up.shape[0]
    nc = D_FF // FFN_TC

    def body(gm_ref, gb_ref, um_ref, ub_ref, cwg_ref, cwu_ref, cbg_ref, cbu_ref, o_ref):
        i = pl.program_id(0)

        def conv(m_ref, b_ref, cw_ref, cb_ref):
            before = jnp.where(i > 0, b_ref[...], 0.0)
            ext = jnp.concatenate([before, m_ref[...]], axis=0)
            return _conv3(ext, cw_ref[...])[FFN_HALO:] + cb_ref[...]

        gate = conv(gm_ref, gb_ref, cwg_ref, cbg_ref)
        upv = conv(um_ref, ub_ref, cwu_ref, cbu_ref)
        o_ref[...] = ((gate * (1.0 / (1.0 + jnp.exp(-gate)))) * upv).astype(BF16)

    hb = ts // FFN_HALO
    main = lambda off: pl.BlockSpec((ts, FFN_TC), lambda i, c: (i, c + off))
    halo = lambda off: pl.BlockSpec((FFN_HALO, FFN_TC), lambda i, c: (jnp.maximum(i * hb - 1, 0), c + off))
    cw = lambda off: pl.BlockSpec((3, FFN_TC), lambda i, c: (0, c + off))
    cb = lambda off: pl.BlockSpec((1, FFN_TC), lambda i, c: (0, c + off))
    return pl.pallas_call(
        body, name=name, grid=(S // ts, nc),
        in_specs=[main(0), halo(0), main(nc), halo(nc), cw(0), cw(nc), cb(0), cb(nc)],
        out_specs=pl.BlockSpec((ts, FFN_TC), lambda i, c: (i, c)),
        out_shape=_sds((S, D_FF), BF16), compiler_params=_params("parallel", "parallel"),
    )(up, up, up, up, conv_w, conv_w, conv_b, conv_b)


def glu_bwd(up, da, conv_w, conv_b, name, ts=256):
    S = up.shape[0]
    nc = D_FF // FFN_TC
    nt = S // ts
    main = slice(FFN_HALO, FFN_HALO + ts)
    W = 2 * D_FF

    def body(xm_ref, xb_ref, xa_ref, dm_ref, da_ref, cw_ref, cb_ref, dx_ref, dcw_ref, dcb_ref):
        i = pl.program_id(0)
        last = i == nt - 1

        @pl.when(i == 0)
        def _():
            dcw_ref[...] = jnp.zeros_like(dcw_ref)
            dcb_ref[...] = jnp.zeros_like(dcb_ref)

        def ext_of(cols):
            before = jnp.where(i > 0, xb_ref[:, cols], 0.0)
            return jnp.concatenate([before, xm_ref[:, cols], xa_ref[:, cols]], axis=0)

        def back(x, d, cols):
            cw = cw_ref[:, cols]
            dx = d * cw[2:3] + _shift_up(d, 1) * cw[1:2] + _shift_up(d, 2) * cw[0:1]
            dx_ref[:, cols] = dx[main].astype(BF16)
            dmn = d[main]
            dcb_ref[:, cols] += jnp.sum(dmn, axis=0, keepdims=True)
            dcw_ref[0:1, cols] += jnp.sum(dmn * _shift_down(x, 2)[main], axis=0, keepdims=True)
            dcw_ref[1:2, cols] += jnp.sum(dmn * _shift_down(x, 1)[main], axis=0, keepdims=True)
            dcw_ref[2:3, cols] += jnp.sum(dmn * x[main], axis=0, keepdims=True)

        for c in range(nc):
            gcols = slice(c * FFN_TC, (c + 1) * FFN_TC)
            ucols = slice(D_FF + c * FFN_TC, D_FF + (c + 1) * FFN_TC)
            xg, xu = ext_of(gcols), ext_of(ucols)
            ug = _conv3(xg, cw_ref[:, gcols]) + cb_ref[:, gcols]
            uu = _conv3(xu, cw_ref[:, ucols]) + cb_ref[:, ucols]
            dafter = jnp.where(last, 0.0, da_ref[:, gcols].astype(F32))
            dae = jnp.concatenate([jnp.zeros((FFN_HALO, FFN_TC), F32), dm_ref[:, gcols].astype(F32), dafter], axis=0)
            sg = 1.0 / (1.0 + jnp.exp(-ug))
            duu = dae * (ug * sg)
            dug = (dae * uu) * (sg * (1.0 + ug * (1.0 - sg)))
            back(xg, dug, gcols)
            back(xu, duu, ucols)

    hb = ts // FFN_HALO
    nh = S // FFN_HALO
    before_map = lambda i: (jnp.maximum(i * hb - 1, 0), 0)
    after_map = lambda i: (jnp.minimum((i + 1) * hb, nh - 1), 0)
    return pl.pallas_call(
        body, name=name, grid=(nt,),
        in_specs=[pl.BlockSpec((ts, W), lambda i: (i, 0)), pl.BlockSpec((FFN_HALO, W), before_map),
                  pl.BlockSpec((FFN_HALO, W), after_map), pl.BlockSpec((ts, D_FF), lambda i: (i, 0)),
                  pl.BlockSpec((FFN_HALO, D_FF), after_map), pl.BlockSpec((3, W), lambda i: (0, 0)),
                  pl.BlockSpec((1, W), lambda i: (0, 0))],
        out_specs=[pl.BlockSpec((ts, W), lambda i: (i, 0)), pl.BlockSpec((3, W), lambda i: (0, 0)),
                   pl.BlockSpec((1, W), lambda i: (0, 0))],
        out_shape=[_sds((S, W), BF16), _sds((3, W), F32), _sds((1, W), F32)],
        compiler_params=_params("arbitrary"),
    )(up, up, up, da, da, conv_w, conv_b)


def _head_mean_matrix():
    h = np.arange(D_MODEL) // HEAD_DIM
    return jnp.asarray((h[:, None] == h[None, :]).astype(np.float32) / HEAD_DIM, dtype=BF16)


def _head_mean(v, gm):
    hi = v.astype(BF16)
    lo = (v - hi.astype(F32)).astype(BF16)
    return jnp.dot(hi, gm, preferred_element_type=F32) + jnp.dot(lo, gm, preferred_element_type=F32)


def qknorm_fwd(qkv, gqk, name, ts=512):
    S = qkv.shape[0]

    def body(x_ref, g_ref, gm_ref, o_ref):
        part = pl.program_id(0)
        x = x_ref[...]

        @pl.when(part < 2)
        def _():
            r = lax.rsqrt(_head_mean(x * x, gm_ref[...]) + EPS)
            o_ref[...] = ((x * r) * g_ref[...]).astype(BF16)

        @pl.when(part == 2)
        def _():
            o_ref[...] = x.astype(BF16)

    return pl.pallas_call(
        body, name=name, grid=(3, S // ts),
        in_specs=[pl.BlockSpec((ts, D_MODEL), lambda p, i: (i, p)), pl.BlockSpec((None, 1, D_MODEL), lambda p, i: (p, 0, 0)),
                  pl.BlockSpec((D_MODEL, D_MODEL), lambda p, i: (0, 0))],
        out_specs=pl.BlockSpec((ts, D_MODEL), lambda p, i: (i, p)),
        out_shape=_sds((S, 3 * D_MODEL), BF16), compiler_params=_params("parallel", "parallel"),
    )(qkv, gqk, _head_mean_matrix())


def qknorm_bwd(qkv, dq, dk, dv, gqk, name, ts=256):
    S = qkv.shape[0]

    def body(x_ref, dq_ref, dk_ref, dv_ref, g_ref, gm_ref, o_ref, dg_ref):
        @pl.when(pl.program_id(0) == 0)
        def _():
            dg_ref[...] = jnp.zeros_like(dg_ref)

        gm = gm_ref[...]
        for part, d_ref in enumerate((dq_ref, dk_ref)):
            cols = slice(part * D_MODEL, (part + 1) * D_MODEL)
            x = x_ref[:, cols]
            d = d_ref[...]
            r = lax.rsqrt(_head_mean(x * x, gm) + EPS)
            gx = d * g_ref[part]
            o_ref[:, cols] = (r * gx - x * ((r * r * r) * _head_mean(gx * x, gm))).astype(BF16)
            dg_ref[part] += jnp.sum(d * (x * r), axis=0, keepdims=True)
        o_ref[:, 2 * D_MODEL:] = dv_ref[...].astype(BF16)

    row = pl.BlockSpec((ts, D_MODEL), lambda i: (i, 0))
    wide = pl.BlockSpec((ts, 3 * D_MODEL), lambda i: (i, 0))
    gains = pl.BlockSpec((3, 1, D_MODEL), lambda i: (0, 0, 0))
    return pl.pallas_call(
        body, name=name, grid=(S // ts,),
        in_specs=[wide, row, row, row, gains, pl.BlockSpec((D_MODEL, D_MODEL), lambda i: (0, 0))],
        out_specs=[wide, gains],
        out_shape=[_sds((S, 3 * D_MODEL), BF16), _sds((3, 1, D_MODEL), F32)],
        compiler_params=_params("arbitrary"),
    )(qkv, dq, dk, dv, gqk, _head_mean_matrix())


def _bucket_tables():
    n = ATT_BLOCK
    a = np.arange(n)[:, None]
    c = np.arange(2 * n)[None, :]
    rel = a + n - c
    band = (rel >= 0) & (rel <= n)
    max_exact = N_REL_BUCKETS // 2
    buckets, valids = [], []
    for _, dil in DILATED_PAIRS:
        dist = np.clip(rel, 0, n) * dil
        dd = np.maximum(dist, 1).astype(np.float32)
        large = max_exact + (np.log(dd / np.float32(max_exact)) / np.float32(math.log(REL_MAX_DISTANCE / max_exact))
                             * np.float32(N_REL_BUCKETS - max_exact)).astype(np.int32)
        large = np.minimum(large, N_REL_BUCKETS - 1)
        buckets.append(np.where(dist < max_exact, dist, large).reshape(1, -1))
        valids.append(np.stack([(band & (c >= n)).reshape(1, -1), band.reshape(1, -1)]))
    return np.stack(buckets).astype(np.int32), np.stack(valids).astype(np.int32)


BIAS_CHUNK = 8192


def _split3(x):
    a = x.astype(BF16)
    r = x - a.astype(F32)
    b = r.astype(BF16)
    c = (r - b.astype(F32)).astype(BF16)
    return a, b, c


def bias_expand(rel_bias_t, name):
    bucket, valid = _bucket_tables()
    nq = bucket.shape[-1]

    def body(t_ref, b_ref, v_ref, o_ref):
        onehot = (lax.broadcasted_iota(jnp.int32, (N_REL_BUCKETS, BIAS_CHUNK), 0) == b_ref[...]).astype(BF16)
        acc = None
        for term in _split3(t_ref[...]):
            p = jnp.dot(term, onehot, preferred_element_type=F32)
            acc = p if acc is None else acc + p
        o_ref[...] = jnp.where(v_ref[...] > 0, acc, MASK_VALUE)

    return pl.pallas_call(
        body, name=name, grid=(3, 2, nq // BIAS_CHUNK),
        in_specs=[pl.BlockSpec((N_HEADS, N_REL_BUCKETS), lambda b, v, c: (0, 0)),
                  pl.BlockSpec((None, 1, BIAS_CHUNK), lambda b, v, c: (b, 0, c)),
                  pl.BlockSpec((None, None, 1, BIAS_CHUNK), lambda b, v, c: (b, v, 0, c))],
        out_specs=pl.BlockSpec((None, None, N_HEADS, BIAS_CHUNK), lambda b, v, c: (b, v, 0, c)),
        out_shape=_sds((3, 2, N_HEADS, nq), F32), compiler_params=_params("parallel", "parallel", "parallel"),
    )(rel_bias_t, jnp.asarray(bucket), jnp.asarray(valid))


def bias_reduce(dbias, name):
    bucket, _ = _bucket_tables()
    nq = bucket.shape[-1]
    dims = (((1,), (1,)), ((), ()))

    def body(d_ref, b_ref, o_ref):
        onehot = (lax.broadcasted_iota(jnp.int32, (N_REL_BUCKETS, BIAS_CHUNK), 0) == b_ref[...]).astype(BF16)
        acc = None
        for term in _split3(d_ref[...]):
            p = lax.dot_general(term, onehot, dims, preferred_element_type=F32)
            acc = p if acc is None else acc + p

        @pl.when(pl.program_id(1) == 0)
        def _():
            o_ref[...] = acc

        @pl.when(pl.program_id(1) > 0)
        def _():
            o_ref[...] += acc

    return pl.pallas_call(
        body, name=name, grid=(3, nq // BIAS_CHUNK),
        in_specs=[pl.BlockSpec((None, N_HEADS, BIAS_CHUNK), lambda b, c: (b, 0, c)),
                  pl.BlockSpec((None, 1, BIAS_CHUNK), lambda b, c: (b, 0, c))],
        out_specs=pl.BlockSpec((None, N_HEADS, N_REL_BUCKETS), lambda b, c: (b, 0, 0)),
        out_shape=_sds((3, N_HEADS, N_REL_BUCKETS), F32), compiler_params=_params("parallel", "arbitrary"),
    )(dbias, jnp.asarray(bucket))


PAIR = 2 * HEAD_DIM
N_PAIRS = N_HEADS // 2
_NT = (((1,), (1,)), ((), ()))
_TN = (((0,), (0,)), ((), ()))


def _low_lanes(shape):
    return lax.broadcasted_iota(jnp.int32, shape, 1) < HEAD_DIM


def _per_head(x, low):
    del low
    return x[:, 0:1], x[:, HEAD_DIM:HEAD_DIM + 1]


def attn_fwd_branch(qkvn, bias, branch, state, last, name):
    S = qkvn.shape[0]
    dil = DILATED_PAIRS[branch][1]
    L = S // dil
    nb = L // ATT_BLOCK
    view = qkvn.reshape(L, dil * 3 * D_MODEL)
    first = state is None

    def body(*refs):
        q_ref, kp_ref, kc_ref, vp_ref, vc_ref, b_ref = refs[:6]
        ins = refs[6:6 + (0 if first else 3)]
        outs = refs[6 + len(ins):]
        low = _low_lanes((ATT_BLOCK, PAIR))
        for hp in range(N_PAIRS):
            cols = slice(hp * PAIR, (hp + 1) * PAIR)
            q = q_ref[:, cols]
            k = jnp.concatenate([kp_ref[:, cols], kc_ref[:, cols]], axis=0)
            v = jnp.concatenate([vp_ref[:, cols], vc_ref[:, cols]], axis=0)
            pv, mx, den = [], [], []
            for hh in range(2):
                qh = jnp.where(low if hh == 0 else ~low, q, jnp.zeros_like(q))
                s = lax.dot_general(qh, k, _NT, preferred_element_type=F32) * (HEAD_DIM ** -0.5) + b_ref[2 * hp + hh]
                m = jnp.max(s, axis=-1, keepdims=True)
                p = jnp.exp(s - m)
                den.append(jnp.sum(p, axis=-1, keepdims=True))
                mx.append(m)
                pv.append(jnp.dot(p.astype(BF16), v, preferred_element_type=F32))
            acc = jnp.where(low, pv[0], pv[1])
            m = jnp.where(low, mx[0], mx[1])
            l = jnp.where(low, den[0], den[1])
            if not first:
                m_old = ins[1][:, cols]
                m_new = jnp.maximum(m_old, m)
                a_old, a_new = jnp.exp(m_old - m_new), jnp.exp(m - m_new)
                acc = ins[0][:, cols] * a_old + acc * a_new
                l = ins[2][:, cols] * a_old + l * a_new
                m = m_new
            if last:
                outs[0][:, cols] = (acc / l).astype(BF16)
                outs[1][:, cols] = m + jnp.log(l)
            else:
                outs[0][:, cols] = acc
                outs[1][:, cols] = m
                outs[2][:, cols] = l

    blk = lambda part, prev: pl.BlockSpec(
        (ATT_BLOCK, D_MODEL), lambda r, b: (jnp.maximum(b - 1, 0) if prev else b, 3 * r + part))
    st = pl.BlockSpec((ATT_BLOCK, D_MODEL), lambda r, b: (b, r))
    in_specs = [blk(0, False), blk(1, True), blk(1, False), blk(2, True), blk(2, False),
                pl.BlockSpec((None, N_HEADS, ATT_BLOCK, 2 * ATT_BLOCK), lambda r, b: (jnp.minimum(b, 1), 0, 0, 0))]
    args = [view] * 5 + [bias]
    if not first:
        in_specs += [st] * 3
        args += [s_.reshape(L, dil * D_MODEL) for s_ in state]
    wide = (L, dil * D_MODEL)
    if last:
        out_shape = [_sds(wide, BF16), _sds(wide, F32)]
    else:
        out_shape = [_sds(wide, F32)] * 3
    outs = pl.pallas_call(
        body, name=name, grid=(dil, nb), in_specs=in_specs, out_specs=[st] * len(out_shape), out_shape=out_shape,
        compiler_params=_params("parallel", "arbitrary"),
    )(*args)
    return tuple(o.reshape(S, D_MODEL) for o in outs)


def attn_bwd_branch(qkvn, o, do, lse, bias, branch, grads, name):
    S = qkvn.shape[0]
    dil = DILATED_PAIRS[branch][1]
    L = S // dil
    nb = L // ATT_BLOCK
    view = qkvn.reshape(L, dil * 3 * D_MODEL)
    first = grads is None
    n_in = 0 if first else 3

    def body(*refs):
        q_ref, kp_ref, kc_ref, vp_ref, vc_ref, o_ref, do_ref, lse_ref, b_ref = refs[:9]
        ins = refs[9:9 + n_in]
        dq_ref, dk_ref, dv_ref, db_ref, ck_ref, cv_ref = refs[9 + n_in:]
        b = pl.program_id(1)
        low = _low_lanes((ATT_BLOCK, PAIR))

        @pl.when((pl.program_id(0) == 0) & (b == 0))
        def _():
            db_ref[...] = jnp.zeros_like(db_ref)

        @pl.when(b == 0)
        def _():
            ck_ref[...] = jnp.zeros_like(ck_ref)
            cv_ref[...] = jnp.zeros_like(cv_ref)

        @pl.when(b < nb)
        def _():
            for hp in range(N_PAIRS):
                cols = slice(hp * PAIR, (hp + 1) * PAIR)
                q = q_ref[:, cols]
                k = jnp.concatenate([kp_ref[:, cols], kc_ref[:, cols]], axis=0)
                v = jnp.concatenate([vp_ref[:, cols], vc_ref[:, cols]], axis=0)
                dout = do_ref[:, cols]
                prod = dout.astype(F32) * o_ref[:, cols].astype(F32)
                lse_h = _per_head(lse_ref[:, cols], low)
                dq, dk, dv = [], None, None
                for hh in range(2):
                    keep = low if hh == 0 else ~low
                    qh = jnp.where(keep, q, jnp.zeros_like(q))
                    doh = jnp.where(keep, dout, jnp.zeros_like(dout))
                    delta = jnp.sum(jnp.where(keep, prod, 0.0), axis=-1, keepdims=True)
                    s = lax.dot_general(qh, k, _NT, preferred_element_type=F32) * (HEAD_DIM ** -0.5) + b_ref[2 * hp + hh]
                    p = jnp.exp(s - lse_h[hh])
                    dp = lax.dot_general(doh, v, _NT, preferred_element_type=F32)
                    ds = p * (dp - delta)
                    db_ref[2 * hp + hh] += ds
                    dsb = (ds * (HEAD_DIM ** -0.5)).astype(BF16)
                    dq.append(jnp.dot(dsb, k, preferred_element_type=F32))
                    dkh = lax.dot_general(dsb, qh, _TN, preferred_element_type=F32)
                    dvh = lax.dot_general(p.astype(BF16), doh, _TN, preferred_element_type=F32)
                    dk = dkh if dk is None else dk + dkh
                    dv = dvh if dv is None else dv + dvh
                dq_new = jnp.where(low, dq[0], dq[1])
                dk_prev = ck_ref[:, cols] + dk[:ATT_BLOCK]
                dv_prev = cv_ref[:, cols] + dv[:ATT_BLOCK]
                if not first:
                    dq_new = dq_new + ins[0][:, cols]
                    dk_prev = dk_prev + ins[1][:, cols]
                    dv_prev = dv_prev + ins[2][:, cols]
                dq_ref[:, cols] = dq_new
                dk_ref[:, cols] = dk_prev
                dv_ref[:, cols] = dv_prev
                ck_ref[:, cols] = dk[ATT_BLOCK:]
                cv_ref[:, cols] = dv[ATT_BLOCK:]

        @pl.when(b == nb)
        def _():
            dk_last, dv_last = ck_ref[...], cv_ref[...]
            if not first:
                dk_last = dk_last + ins[1][...]
                dv_last = dv_last + ins[2][...]
            dk_ref[...] = dk_last
            dv_ref[...] = dv_last

    cur = lambda b: jnp.minimum(b, nb - 1)
    prev = lambda b: jnp.maximum(b - 1, 0)
    blk = lambda part, use_prev: pl.BlockSpec(
        (ATT_BLOCK, D_MODEL), lambda r, b: (prev(cur(b)) if use_prev else cur(b), 3 * r + part))
    qside = pl.BlockSpec((ATT_BLOCK, D_MODEL), lambda r, b: (cur(b), r))
    kside = pl.BlockSpec((ATT_BLOCK, D_MODEL), lambda r, b: (prev(b), r))
    bias_spec = pl.BlockSpec((None, N_HEADS, ATT_BLOCK, 2 * ATT_BLOCK), lambda r, b: (jnp.minimum(b, 1), 0, 0, 0))
    wide = (L, dil * D_MODEL)
    in_specs = [blk(0, False), blk(1, True), blk(1, False), blk(2, True), blk(2, False), qside, qside, qside, bias_spec]
    args = [view] * 5 + [o.reshape(wide), do.reshape(wide), lse.reshape(wide), bias]
    aliases = {}
    if not first:
        in_specs += [qside, kside, kside]
        args += [g.reshape(wide) for g in grads]
        aliases = {9: 0, 10: 1, 11: 2}
    dq, dk, dv, db = pl.pallas_call(
        body, name=name, grid=(dil, nb + 1), in_specs=in_specs,
        out_specs=[qside, kside, kside, pl.BlockSpec((N_HEADS, ATT_BLOCK, 2 * ATT_BLOCK), lambda r, b: (0, 0, 0))],
        out_shape=[_sds(wide, F32)] * 3 + [_sds((N_HEADS, ATT_BLOCK, 2 * ATT_BLOCK), F32)],
        scratch_shapes=[pltpu.VMEM((ATT_BLOCK, D_MODEL), F32), pltpu.VMEM((ATT_BLOCK, D_MODEL), F32)],
        input_output_aliases=aliases, compiler_params=_params("arbitrary", "arbitrary"),
    )(*args)
    return (dq.reshape(S, D_MODEL), dk.reshape(S, D_MODEL), dv.reshape(S, D_MODEL)), db


def loss_grad(y, target, name, ts=512):
    S, Dm = y.shape

    def body(y_ref, t_ref, d_ref, db_ref, s_ref):
        e = y_ref[...] - t_ref[...]
        d = e * (1.0 / Dm)
        d_ref[...] = d
        db_ref[...] = d.astype(BF16)
        part = jnp.sum(e * e, axis=0, keepdims=True)

        @pl.when(pl.program_id(0) == 0)
        def _():
            s_ref[...] = part

        @pl.when(pl.program_id(0) > 0)
        def _():
            s_ref[...] += part

    row = pl.BlockSpec((ts, Dm), lambda i: (i, 0))
    vec = pl.BlockSpec((1, Dm), lambda i: (0, 0))
    return pl.pallas_call(
        body, name=name, grid=(S // ts,), in_specs=[row, row], out_specs=[row, row, vec],
        out_shape=[_sds((S, Dm), F32), _sds((S, Dm), BF16), _sds((1, Dm), F32)], compiler_params=_params("arbitrary"),
    )(y, target)


def adamw(w, g, m, v, name):
    n, R, C = w.shape

    def body(w_ref, g_ref, m_ref, v_ref, d_ref, nm_ref, nv_ref):
        gv = g_ref[...]
        m2 = ADAM_B1 * m_ref[...] + (1.0 - ADAM_B1) * gv
        v2 = ADAM_B2 * v_ref[...] + (1.0 - ADAM_B2) * (gv * gv)
        m_hat = m2 / (1.0 - ADAM_B1 ** ADAM_STEP)
        v_hat = v2 / (1.0 - ADAM_B2 ** ADAM_STEP)
        d_ref[...] = -ADAM_LR * (m_hat / (jnp.sqrt(v_hat) + ADAM_EPS) + ADAM_WD * w_ref[...])
        nm_ref[...] = m2
        nv_ref[...] = v2

    tr = R
    while tr * C * 4 > (1 << 21) and tr % 16 == 0:
        tr //= 2
    spec = pl.BlockSpec((None, tr, C), lambda i, r: (i, r, 0))
    return pl.pallas_call(
        body, name=name, grid=(n, R // tr), in_specs=[spec] * 4, out_specs=[spec] * 3,
        out_shape=[_sds((n, R, C), F32)] * 3, compiler_params=_params("parallel", "parallel"),
    )(w, g, m, v)


ANY = pl.BlockSpec(memory_space=pl.ANY)


def _coords():
    return lax.axis_index("x"), lax.axis_index("y"), lax.axis_index("c")


def _other_chips(mx, my):
    return [(1 - mx, my), (mx, 1 - my), (1 - mx, 1 - my)]


def _remote(src, dst, send, recv, dev):
    return pltpu.make_async_remote_copy(src_ref=src, dst_ref=dst, send_sem=send, recv_sem=recv, device_id=dev,
                                        device_id_type=MESH)


def allgather_devices(x, name):
    R, C = x.shape

    def body(x_ref, o_ref, send, recv, local_sem):
        mx, my, mc = _coords()
        me = 4 * mx + 2 * my + mc
        local = pltpu.make_async_copy(x_ref, o_ref.at[me], local_sem)
        local.start()
        peers = []
        for k in range(1, N_DEV):
            px = 1 - mx if k & 4 else mx
            py = 1 - my if k & 2 else my
            pc = 1 - mc if k & 1 else mc
            peers.append((px, py, pc))
        sends = [_remote(x_ref, o_ref.at[me], send.at[k], recv.at[k], p) for k, p in enumerate(peers)]
        for cp in sends:
            cp.start()
        for k, (px, py, pc) in enumerate(peers):
            _remote(x_ref, o_ref.at[4 * px + 2 * py + pc], send.at[k], recv.at[k], (px, py, pc)).wait_recv()
        for cp in sends:
            cp.wait_send()
        local.wait()

    return pl.pallas_call(
        body, name=name, in_specs=[ANY], out_specs=ANY, out_shape=_sds((N_DEV, R, C), x.dtype),
        scratch_shapes=[pltpu.SemaphoreType.DMA((N_DEV - 1,)), pltpu.SemaphoreType.DMA((N_DEV - 1,)),
                        pltpu.SemaphoreType.DMA],
    )(x)


HBM = pl.BlockSpec(memory_space=pltpu.HBM)
SEM = pl.BlockSpec(memory_space=pltpu.SEMAPHORE)
_SPLIT_COPY = pltpu.CompilerParams(has_side_effects=pltpu.SideEffectType.DATAFLOW_SIDE_EFFECTING)


def _in_hbm(a):
    return pltpu.with_memory_space_constraint(a, pltpu.HBM)


def cast_into_slot(w, layer, chip_core, name):
    _, _, hR, C = w.shape

    def body(s_ref, w_ref, o_ref):
        del s_ref
        o_ref[...] = w_ref[...].astype(BF16)

    grid_spec = pltpu.PrefetchScalarGridSpec(
        num_scalar_prefetch=1, grid=(2,),
        in_specs=[pl.BlockSpec((None, None, hR, C), lambda h, s: (layer, h, 0, 0))],
        out_specs=pl.BlockSpec((None, None, hR, C), lambda h, s: (s[0], h, 0, 0)))
    return pl.pallas_call(body, name=name, grid_spec=grid_spec, out_shape=_sds((N_CHIPS, 2, hR, C), BF16),
                          compiler_params=_params("parallel"))(chip_core, w)


def gather_start(lands, groups, name):
    n = len(lands)
    n_groups = len(groups)

    def body(*refs):
        ins = refs[:n]
        sems = refs[n:n + 2 * n_groups]
        token = refs[-1]
        mx, my, mc = _coords()
        chip = 2 * mx + my
        for g, members in enumerate(groups):
            send, recv = sems[2 * g], sems[2 * g + 1]
            for i, a in enumerate(members):
                mine = ins[a].at[chip, mc]
                for k, (px, py) in enumerate(_other_chips(mx, my)):
                    _remote(mine, mine, send.at[3 * i + k], recv.at[3 * i + k], (px, py, mc)).start()
        token[...] = jnp.zeros_like(token)

    sem_shapes = []
    for members in groups:
        sem_shapes += [pltpu.SemaphoreType.DMA((3 * len(members),))] * 2
    outs = pl.pallas_call(
        body, name=name, in_specs=[HBM] * n,
        out_specs=[SEM] * (2 * n_groups) + [HBM] * n + [pl.BlockSpec(memory_space=pltpu.VMEM)],
        out_shape=sem_shapes + [pltpu.HBM(a.shape, a.dtype) for a in lands] + [_sds((SUBLANES, LANES), F32)],
        input_output_aliases={a: 2 * n_groups + a for a in range(n)}, compiler_params=_SPLIT_COPY,
    )(*[_in_hbm(a) for a in lands])
    sems = [(outs[2 * g], outs[2 * g + 1]) for g in range(n_groups)]
    return sems, list(outs[2 * n_groups:2 * n_groups + n]), outs[-1]


def gather_forward(lands, sems, after, name):
    n = len(lands)

    def body(*refs):
        ins = refs[:n]
        send, recv = refs[n], refs[n + 1]
        fsend, frecv = refs[n + 3], refs[n + 4]
        mx, my, mc = _coords()
        for i in range(n):
            for k, (px, py) in enumerate(_other_chips(mx, my)):
                landed = ins[i].at[2 * px + py, mc]
                cp = _remote(landed, landed, send.at[3 * i + k], recv.at[3 * i + k], (px, py, mc))
                cp.wait_send()
                cp.wait_recv()
                _remote(landed, landed, fsend.at[3 * i + k], frecv.at[3 * i + k], (mx, my, 1 - mc)).start()

    outs = pl.pallas_call(
        body, name=name, in_specs=[HBM] * n + [SEM, SEM, ANY], out_specs=[SEM, SEM] + [HBM] * n,
        out_shape=[pltpu.SemaphoreType.DMA((3 * n,))] * 2 + [pltpu.HBM(a.shape, a.dtype) for a in lands],
        input_output_aliases={a: 2 + a for a in range(n)}, compiler_params=_SPLIT_COPY,
    )(*lands, sems[0], sems[1], after)
    return (outs[0], outs[1]), list(outs[2:])


def gather_wait(lands, sems, after, name):
    n = len(lands)

    def body(*refs):
        ins = refs[:n]
        fsend, frecv = refs[n], refs[n + 1]
        mx, my, mc = _coords()
        for i in range(n):
            for k, (px, py) in enumerate(_other_chips(mx, my)):
                theirs = ins[i].at[2 * px + py, 1 - mc]
                cp = _remote(theirs, theirs, fsend.at[3 * i + k], frecv.at[3 * i + k], (mx, my, 1 - mc))
                cp.wait_send()
                cp.wait_recv()

    outs = pl.pallas_call(
        body, name=name, in_specs=[HBM] * n + [SEM, SEM, ANY], out_specs=[HBM] * n,
        out_shape=[pltpu.HBM(a.shape, a.dtype) for a in lands],
        input_output_aliases={a: a for a in range(n)}, compiler_params=_SPLIT_COPY,
    )(*lands, sems[0], sems[1], after)
    return list(outs)


def _peers(mx, my, mc):
    return [(1 - mx if k & 4 else mx, 1 - my if k & 2 else my, 1 - mc if k & 1 else mc) for k in range(1, N_DEV)]


def devices_start(x, name):
    def body(x_ref, land_ref, send, recv, x_thru, land_thru):
        mx, my, mc = _coords()
        me = 4 * mx + 2 * my + mc
        for k, peer in enumerate(_peers(mx, my, mc)):
            _remote(x_ref, land_ref.at[me], send.at[k], recv.at[k], peer).start()

    land = lax.empty((N_DEV,) + x.shape, x.dtype)
    outs = pl.pallas_call(
        body, name=name, in_specs=[HBM, HBM], out_specs=[SEM, SEM, HBM, HBM],
        out_shape=[pltpu.SemaphoreType.DMA((N_DEV - 1,))] * 2 + [pltpu.HBM(x.shape, x.dtype), pltpu.HBM(land.shape, x.dtype)],
        input_output_aliases={0: 2, 1: 3}, compiler_params=_SPLIT_COPY,
    )(_in_hbm(x), _in_hbm(land))
    return (outs[0], outs[1]), outs[2], outs[3]


def devices_wait(x, land, sems, after, name):
    def body(x_ref, land_ref, send, recv, after_ref, x_thru, land_thru):
        mx, my, mc = _coords()
        for k, (px, py, pc) in enumerate(_peers(mx, my, mc)):
            cp = _remote(x_ref, land_ref.at[4 * px + 2 * py + pc], send.at[k], recv.at[k], (px, py, pc))
            cp.wait_send()
            cp.wait_recv()

    outs = pl.pallas_call(
        body, name=name, in_specs=[HBM, HBM, SEM, SEM, ANY], out_specs=[HBM, HBM],
        out_shape=[pltpu.HBM(x.shape, x.dtype), pltpu.HBM(land.shape, land.dtype)],
        input_output_aliases={0: 0, 1: 1}, compiler_params=_SPLIT_COPY,
    )(x, land, sems[0], sems[1], after)
    return outs[0], outs[1]


def device_sum(land, own, me, name):
    _, R, C = land.shape

    def body(s_ref, l_ref, o_ref_in, o_ref):
        acc = None
        for q in range(N_DEV):
            term = jnp.where(s_ref[0] == q, o_ref_in[...], l_ref[q])
            acc = term if acc is None else acc + term
        o_ref[...] = acc

    grid_spec = pltpu.PrefetchScalarGridSpec(
        num_scalar_prefetch=1, grid=(1,),
        in_specs=[pl.BlockSpec((N_DEV, R, C), lambda i, s: (0, 0, 0)), pl.BlockSpec((R, C), lambda i, s: (0, 0))],
        out_specs=pl.BlockSpec((R, C), lambda i, s: (0, 0)))
    return pl.pallas_call(body, name=name, grid_spec=grid_spec, out_shape=_sds((R, C), F32),
                          compiler_params=_params("arbitrary"))(me, land, own)


def reduce_send(grads, name):
    n = len(grads)

    def body(*refs):
        ins, lands = refs[:n], refs[n:2 * n]
        send, recv = refs[2 * n], refs[2 * n + 1]
        mx, my, mc = _coords()
        me = 4 * mx + 2 * my + mc
        for a in range(n):
            for k, (px, py, pc) in enumerate(_peers(mx, my, mc)):
                _remote(ins[a].at[2 * px + py, pc], lands[a].at[me], send.at[7 * a + k], recv.at[7 * a + k], (px, py, pc)).start()

    lands = [lax.empty((N_DEV,) + g.shape[2:], g.dtype) for g in grads]
    outs = pl.pallas_call(
        body, name=name, in_specs=[HBM] * (2 * n), out_specs=[SEM, SEM] + [HBM] * (2 * n),
        out_shape=[pltpu.SemaphoreType.DMA((7 * n,))] * 2 + [pltpu.HBM(a.shape, a.dtype) for a in grads + lands],
        input_output_aliases={a: 2 + a for a in range(2 * n)}, compiler_params=_SPLIT_COPY,
    )(*[_in_hbm(a) for a in grads + lands])
    return (outs[0], outs[1]), list(outs[2:2 + n]), list(outs[2 + n:])


def reduce_wait(grads, lands, sems, after, name):
    n = len(grads)

    def body(*refs):
        ins, zones = refs[:n], refs[n:2 * n]
        send, recv = refs[2 * n], refs[2 * n + 1]
        mx, my, mc = _coords()
        for a in range(n):
            for k, (px, py, pc) in enumerate(_peers(mx, my, mc)):
                cp = _remote(ins[a].at[2 * px + py, pc], zones[a].at[4 * px + 2 * py + pc], send.at[7 * a + k],
                             recv.at[7 * a + k], (px, py, pc))
                cp.wait_send()
                cp.wait_recv()

    outs = pl.pallas_call(
        body, name=name, in_specs=[HBM] * (2 * n) + [SEM, SEM, ANY], out_specs=[HBM] * (2 * n),
        out_shape=[pltpu.HBM(a.shape, a.dtype) for a in grads + lands],
        input_output_aliases={a: a for a in range(2 * n)}, compiler_params=_SPLIT_COPY,
    )(*grads, *lands, sems[0], sems[1], after)
    return list(outs[:n]), list(outs[n:])


def reduce_sum(land, grad, place, name, into=None, layer=None):
    _, hR, C = land.shape
    tr = hR
    while N_DEV * tr * C * 2 > (6 << 20) and tr % 32 == 0:
        tr //= 2

    def body(s_ref, l_ref, g_ref, *rest):
        o_ref = rest[-1]
        own = g_ref[...].astype(F32)
        acc = None
        for q in range(N_DEV):
            term = jnp.where(s_ref[2] == q, own, l_ref[q].astype(F32))
            acc = term if acc is None else acc + term
        o_ref[...] = acc

    in_specs = [pl.BlockSpec((N_DEV, tr, C), lambda i, s: (0, i, 0)),
                pl.BlockSpec((None, None, tr, C), lambda i, s: (s[0], s[1], i, 0))]
    args = [place, land, grad]
    aliases = {}
    if layer is None:
        out_spec = pl.BlockSpec((None, tr, C), lambda i, s: (s[1], i, 0))
        out_shape = _sds((2, hR, C), F32)
    else:
        out_spec = pl.BlockSpec((None, None, tr, C), lambda i, s: (layer, s[1], i, 0))
        out_shape = _sds((2, 2, hR, C), F32)
        if into is not None:
            in_specs.append(ANY)
            args.append(into)
            aliases = {3: 0}
    grid_spec = pltpu.PrefetchScalarGridSpec(num_scalar_prefetch=1, grid=(hR // tr,), in_specs=in_specs, out_specs=out_spec)
    return pl.pallas_call(body, name=name, grid_spec=grid_spec, out_shape=out_shape, input_output_aliases=aliases,
                          compiler_params=_params("arbitrary"))(*args)


def join_halves(arrays, name):
    n = len(arrays)
    pieces = [(a, l) for a, arr in enumerate(arrays) for l in (range(arr.shape[0]) if arr.ndim == 4 else [None])]

    def body(*refs):
        ins = refs[:n]
        send, recv = refs[2 * n:]
        mx, my, mc = _coords()

        def half(a, l, h):
            return ins[a].at[h] if l is None else ins[a].at[l, h]

        sends = [_remote(half(a, l, mc), half(a, l, mc), send.at[i], recv.at[i], (mx, my, 1 - mc))
                 for i, (a, l) in enumerate(pieces)]
        for cp in sends:
            cp.start()
        for i, (a, l) in enumerate(pieces):
            theirs = half(a, l, 1 - mc)
            _remote(theirs, theirs, send.at[i], recv.at[i], (mx, my, 1 - mc)).wait_recv()
        for cp in sends:
            cp.wait_send()

    return pl.pallas_call(
        body, name=name, in_specs=[ANY] * n, out_specs=[ANY] * n, out_shape=[_sds(a.shape, a.dtype) for a in arrays],
        input_output_aliases={a: a for a in range(n)},
        scratch_shapes=[pltpu.SemaphoreType.DMA((len(pieces),)), pltpu.SemaphoreType.DMA((len(pieces),))],
    )(*arrays)


LANES = 128
SUBLANES = 8


def _n_rows(shape):
    rows = -(-int(np.prod(shape)) // LANES)
    return -(-rows // SUBLANES) * SUBLANES


def _as_rows(a):
    flat = a.reshape(-1)
    rows = _n_rows(a.shape)
    return jnp.pad(flat, (0, rows * LANES - flat.shape[0])).reshape(rows, LANES)


def _pack(arrays):
    return jnp.concatenate([_as_rows(a) for a in arrays], axis=0)


def _unpack(rows, shapes):
    out, r0 = [], 0
    for s in shapes:
        n = _n_rows(s)
        out.append(rows[r0:r0 + n].reshape(-1)[:int(np.prod(s))].reshape(s))
        r0 += n
    return out


REPLICATED_SMALL = [("rel_bias", (32, 16)), ("even_norm", (1, 1024)), ("even_pool_w", (1, 4, 128, 128)),
                    ("even_pool_scale", (1, 512)), ("odd_q_norm", (1, 64)), ("odd_k_norm", (1, 64)),
                    ("ffn_norm", (2, 1024)), ("ffn_conv_b", (2, 5632))]
SHARDED_SMALL = [("even_conv_w", (1, 3, 128)), ("odd_norm", (1, 256)), ("ffn_conv_w", (2, 3, 1408))]
BIG = ["even_w_in", "even_w_out", "odd_w_qkv", "odd_w_o", "ffn_w_up", "ffn_w_down"]
WEIGHT_ORDER = ["rel_bias", "even_norm", "even_w_in", "even_conv_w", "even_pool_w", "even_pool_scale", "even_w_out",
                "odd_norm", "odd_w_qkv", "odd_q_norm", "odd_k_norm", "odd_w_o", "ffn_norm", "ffn_w_up", "ffn_conv_w",
                "ffn_conv_b", "ffn_w_down"]


def kernel(x, rel_bias, even_norm, even_w_in, even_conv_w, even_pool_w, even_pool_scale, even_w_out, odd_norm, odd_w_qkv, odd_q_norm, odd_k_norm, odd_w_o, ffn_norm, ffn_w_up, ffn_conv_w, ffn_conv_b, ffn_w_down, loss_target, m_rel_bias, m_even_norm, m_even_w_in, m_even_conv_w, m_even_pool_w, m_even_pool_scale, m_even_w_out, m_odd_norm, m_odd_w_qkv, m_odd_q_norm, m_odd_k_norm, m_odd_w_o, m_ffn_norm, m_ffn_w_up, m_ffn_conv_w, m_ffn_conv_b, m_ffn_w_down, v_rel_bias, v_even_norm, v_even_w_in, v_even_conv_w, v_even_pool_w, v_even_pool_scale, v_even_w_out, v_odd_norm, v_odd_w_qkv, v_odd_q_norm, v_odd_k_norm, v_odd_w_o, v_ffn_norm, v_ffn_w_up, v_ffn_conv_w, v_ffn_conv_b, v_ffn_w_down):
    W = dict(rel_bias=rel_bias, even_norm=even_norm, even_w_in=even_w_in, even_conv_w=even_conv_w, even_pool_w=even_pool_w,
             even_pool_scale=even_pool_scale, even_w_out=even_w_out, odd_norm=odd_norm, odd_w_qkv=odd_w_qkv,
             odd_q_norm=odd_q_norm, odd_k_norm=odd_k_norm, odd_w_o=odd_w_o, ffn_norm=ffn_norm, ffn_w_up=ffn_w_up,
             ffn_conv_w=ffn_conv_w, ffn_conv_b=ffn_conv_b, ffn_w_down=ffn_w_down)
    M1 = dict(rel_bias=m_rel_bias, even_norm=m_even_norm, even_w_in=m_even_w_in, even_conv_w=m_even_conv_w,
              even_pool_w=m_even_pool_w, even_pool_scale=m_even_pool_scale, even_w_out=m_even_w_out, odd_norm=m_odd_norm,
              odd_w_qkv=m_odd_w_qkv, odd_q_norm=m_odd_q_norm, odd_k_norm=m_odd_k_norm, odd_w_o=m_odd_w_o,
              ffn_norm=m_ffn_norm, ffn_w_up=m_ffn_w_up, ffn_conv_w=m_ffn_conv_w, ffn_conv_b=m_ffn_conv_b,
              ffn_w_down=m_ffn_w_down)
    M2 = dict(rel_bias=v_rel_bias, even_norm=v_even_norm, even_w_in=v_even_w_in, even_conv_w=v_even_conv_w,
              even_pool_w=v_even_pool_w, even_pool_scale=v_even_pool_scale, even_w_out=v_even_w_out, odd_norm=v_odd_norm,
              odd_w_qkv=v_odd_w_qkv, odd_q_norm=v_odd_q_norm, odd_k_norm=v_odd_k_norm, odd_w_o=v_odd_w_o,
              ffn_norm=v_ffn_norm, ffn_w_up=v_ffn_w_up, ffn_conv_w=v_ffn_conv_w, ffn_conv_b=v_ffn_conv_b,
              ffn_w_down=v_ffn_w_down)
    mx, my, mc = _coords()
    chip = 2 * mx + my
    me = 4 * mx + 2 * my + mc
    place = jnp.stack([chip, mc, me]).astype(jnp.int32)
    xs, target = x[0], loss_target[0]

    def halves(w):
        return w.reshape((w.shape[0], 2, w.shape[-2] // 2, w.shape[-1]))

    lands = [cast_into_slot(halves(even_w_in), 0, place, "cast_w_in"), cast_into_slot(halves(even_w_out), 0, place, "cast_w_out"),
             cast_into_slot(halves(ffn_w_up), 0, place, "cast_w_up0"), cast_into_slot(halves(ffn_w_down), 0, place, "cast_w_down0"),
             cast_into_slot(halves(odd_w_qkv), 0, place, "cast_w_qkv"), cast_into_slot(halves(odd_w_o), 0, place, "cast_w_o"),
             cast_into_slot(halves(ffn_w_up), 1, place, "cast_w_up1"), cast_into_slot(halves(ffn_w_down), 1, place, "cast_w_down1")]
    small = allgather_devices(_pack([even_conv_w, odd_norm, ffn_conv_w]), "allgather_small_weights")
    lands[0], small = lax.optimization_barrier((lands[0], small))
    gather_sems, lands, token = gather_start(lands, [[0, 1], [2, 3], [4, 5], [6, 7]], "gather_start")
    even_norm_after_start = even_norm + token[0:1, 0:1]
    small = small[0::2]
    conv_w_full = small[:, 0:3].transpose(1, 0, 2).reshape(3, A_WIDTH)
    odd_norm_full = small[:, 8:10].reshape(1, D_MODEL)
    ffn_cw_full = small[:, 16:82].reshape(N_CHIPS, 2, 3, 2 * D_FF // N_CHIPS).transpose(1, 2, 0, 3).reshape(2, 3, 2 * D_FF)
    pool_w = cast_bf16(even_pool_w[0], "cast_pool_w")
    gqk = jnp.stack([jnp.tile(odd_q_norm[0], N_HEADS), jnp.tile(odd_k_norm[0], N_HEADS),
                     jnp.ones((D_MODEL,), F32)])[:, None, :]
    bias = bias_expand(rel_bias.T, "bias_expand").reshape(3, 2, N_HEADS, ATT_BLOCK, 2 * ATT_BLOCK)

    def ffn_fwd(l, xin):
        xn = rmsnorm_fwd(xin, ffn_norm[l:l + 1], f"ffn{l}_norm")
        up = mm_nn(xn, w_up[l], f"ffn{l}_up")
        act = glu_fwd(up, ffn_cw_full[l], ffn_conv_b[l:l + 1], f"ffn{l}_glu")
        return mm_nn(act, w_down[l], f"ffn{l}_down", res=xin), (xin, xn, up, act)

    def gathered(group, tag, after_landing, after_passing):
        sems, arrays = gather_forward(lands[2 * group:2 * group + 2], gather_sems[group], after_landing, "gather_forward_" + tag)
        return gather_wait(arrays, sems, after_passing, "gather_wait_" + tag)

    def ffn_weights(got):
        return got[0].reshape(N_CHIPS, 1, D_MODEL, 2 * D_FF // N_CHIPS), got[1].reshape(1, 1, D_FF, D_MODEL)

    w_up, w_down = [None, None], [None, None]
    xn0 = rmsnorm_fwd(xs, even_norm_after_start, "even_norm")
    got = gathered(0, "even", bias, xn0)
    w_in = got[0].reshape(N_CHIPS, 1, D_MODEL, EVEN_IN // N_CHIPS)
    w_out = got[1].reshape(1, 1, D_MODEL, D_MODEL)
    proj = mm_nn(xn0, w_in, "even_in")
    mix = mixer_fwd(proj, conv_w_full, pool_w, even_pool_scale, "even_mixer")
    x1 = mm_nn(mix, w_out, "even_out", res=xs)
    w_up[0], w_down[0] = ffn_weights(gathered(1, "ffn0", proj, x1))
    x2, ffn0 = ffn_fwd(0, x1)
    got = gathered(2, "odd", x1, x2)
    w_qkv = got[0].reshape(N_CHIPS, 1, D_MODEL, 3 * D_MODEL // N_CHIPS)
    w_o = got[1].reshape(1, 1, D_MODEL, D_MODEL)
    xn2 = rmsnorm_fwd(x2, odd_norm_full, "odd_norm")
    qkv = mm_nn(xn2, w_qkv, "odd_qkv")
    qkvn = qknorm_fwd(qkv, gqk, "odd_qknorm")
    state = None
    for br in range(3):
        state = attn_fwd_branch(qkvn, bias[br], br, state, br == 2, f"attn_fwd{br}")
    att, lse = state
    x3 = mm_nn(att, w_o, "odd_out", res=x2)
    w_up[1], w_down[1] = ffn_weights(gathered(3, "ffn1", x2, x3))
    x4, ffn1 = ffn_fwd(1, x3)

    dy, dyb, sq = loss_grad(x4, target, "loss")
    loss = lax.psum(0.5 * jnp.sum(sq) * (1.0 / D_MODEL), ("x", "y", "c"))

    def ffn_bwd(l, dy, dyb, saved):
        xin, xn, up, act = saved
        dw_down = mm_tn(act, dyb, f"ffn{l}_dw_down", J=1, tk=D_FF // 2)
        dact = mm_nt(dyb, w_down[l], f"ffn{l}_dact", tr=D_FF // 2)
        dup, dcw, dcb = glu_bwd(up, dact, ffn_cw_full[l], ffn_conv_b[l:l + 1], f"ffn{l}_glu_bwd")
        dw_up = mm_tn(xn, dup, f"ffn{l}_dw_up", J=N_CHIPS, tk=512)
        dxn = mm_nt(dup, w_up[l], f"ffn{l}_dxn", tr=D_MODEL)
        dx, dxb, dg = rmsnorm_bwd(xin, ffn_norm[l:l + 1], dxn, dy, f"ffn{l}_norm_bwd")
        return dx, dxb, (dw_down, dw_up, dcw, dcb, dg)

    def quarters(g):
        return g.reshape(N_CHIPS, 2, g.shape[0] * g.shape[1] // (2 * N_CHIPS), g.shape[-1])

    def reduce_start(grads, tag, then):
        sems, parts, zones = reduce_send([quarters(g) for g in grads], "reduce_send_" + tag)
        then, parts = lax.optimization_barrier((then, parts))
        return (sems, parts, zones), then

    dx3, dx3b, g_ffn1 = ffn_bwd(1, dy, dyb, ffn1)
    red_ffn1, (dx3, dx3b) = reduce_start([g_ffn1[1], g_ffn1[0]], "ffn1", (dx3, dx3b))
    dw_o = mm_tn(att, dx3b, "odd_dw_o", J=1, tk=512)
    datt = mm_nt(dx3b, w_o, "odd_datt", tr=D_MODEL, out_dtype=BF16)
    grads, dbias = None, []
    for br in range(3):
        grads, db = attn_bwd_branch(qkvn, att, datt, lse, bias[br], br, grads, f"attn_bwd{br}")
        dbias.append(db.reshape(N_HEADS, 2 * ATT_BLOCK * ATT_BLOCK))
    dqkv, dgqk = qknorm_bwd(qkv, grads[0], grads[1], grads[2], gqk, "odd_qknorm_bwd")
    dw_qkv = mm_tn(xn2, dqkv, "odd_dw_qkv", J=N_CHIPS, tk=512)
    dxn2 = mm_nt(dqkv, w_qkv, "odd_dxn", tr=D_MODEL)
    red_odd, dxn2 = reduce_start([dw_qkv, dw_o], "odd", dxn2)
    dx2, dx2b, dg_odd = rmsnorm_bwd(x2, odd_norm_full, dxn2, dx3, "odd_norm_bwd")
    dx1, dx1b, g_ffn0 = ffn_bwd(0, dx2, dx2b, ffn0)
    red_ffn0, (dx1, dx1b) = reduce_start([g_ffn0[1], g_ffn0[0]], "ffn0", (dx1, dx1b))
    dw_out = mm_tn(mix, dx1b, "even_dw_out", J=1, tk=512)
    dmix = mm_nt(dx1b, w_out, "even_dmix", tr=D_MODEL)
    dproj, dcw_even, dpw, dps = mixer_bwd(proj, dmix, conv_w_full, pool_w, even_pool_scale, "even_mixer_bwd")
    dw_in = mm_tn(xn0, dproj, "even_dw_in", J=N_CHIPS, tk=512)
    dxn0 = mm_nt(dproj, w_in, "even_dxn", tr=D_MODEL)
    grad_x, _, dg_even = rmsnorm_bwd(xs, even_norm, dxn0, dx1, "even_norm_bwd")
    d_rel = jnp.sum(bias_reduce(jnp.stack(dbias), "bias_reduce"), axis=0).T

    red_even, grad_x = reduce_start([dw_in, dw_out], "even", grad_x)

    dcw_sh = dcw_even.reshape(3, N_CHIPS, A_WIDTH // N_CHIPS).transpose(1, 0, 2)
    don_sh = dg_odd.reshape(N_CHIPS, D_MODEL // N_CHIPS)
    dfcw = jnp.stack([g_ffn0[2], g_ffn1[2]])
    dfcw_sh = dfcw.reshape(2, 3, N_CHIPS, 2 * D_FF // N_CHIPS).transpose(2, 0, 1, 3)
    rep_grads = [d_rel, dg_even, dpw[None], dps, _head_sum(dgqk[0]), _head_sum(dgqk[1]),
                 jnp.concatenate([g_ffn0[4], g_ffn1[4]], axis=0), jnp.concatenate([g_ffn0[3], g_ffn1[3]], axis=0)]
    rep_rows = _pack(rep_grads)
    shard_rows = jnp.concatenate([_pack([dcw_sh[j], don_sh[j], dfcw_sh[j]]) for j in range(N_CHIPS)], axis=0)
    n_rep, n_shard = rep_rows.shape[0], shard_rows.shape[0] // N_CHIPS
    small_sems, small_rows, small_land = devices_start(jnp.concatenate([rep_rows, shard_rows], axis=0), "small_grads_start")
    grad_x, small_rows = lax.optimization_barrier((grad_x, small_rows))

    def reduce_end(red, tag, after):
        sems, parts, zones = red
        parts, zones = reduce_wait(parts, zones, sems, after, "reduce_wait_" + tag)
        return zones, parts

    z_ffn1, p_ffn1 = reduce_end(red_ffn1, "ffn1", grad_x)
    z_odd, p_odd = reduce_end(red_odd, "odd", grad_x)
    z_ffn0, p_ffn0 = reduce_end(red_ffn0, "ffn0", grad_x)
    r_up = reduce_sum(z_ffn0[0], p_ffn0[0], place, "reduce_sum_w_up0", layer=0)
    r_up = reduce_sum(z_ffn1[0], p_ffn1[0], place, "reduce_sum_w_up1", into=r_up, layer=1)
    r_down = reduce_sum(z_ffn0[1], p_ffn0[1], place, "reduce_sum_w_down0", layer=0)
    r_down = reduce_sum(z_ffn1[1], p_ffn1[1], place, "reduce_sum_w_down1", into=r_down, layer=1)
    z_even, p_even = reduce_end(red_even, "even", r_down)
    halves_written = [reduce_sum(z_even[0], p_even[0], place, "reduce_sum_w_in"),
                      reduce_sum(z_even[1], p_even[1], place, "reduce_sum_w_out"),
                      reduce_sum(z_odd[0], p_odd[0], place, "reduce_sum_w_qkv"),
                      reduce_sum(z_odd[1], p_odd[1], place, "reduce_sum_w_o"), r_up, r_down]
    joined = join_halves(halves_written, "grads_join_halves")
    G = {nm: g.reshape(W[nm].shape) for nm, g in zip(BIG, joined)}

    D_, NM, NV = {}, {}, {}
    for nm in BIG:
        as3 = lambda a: a.reshape((-1,) + a.shape[-2:])
        outs = adamw(as3(W[nm]), as3(G[nm]), as3(M1[nm]), as3(M2[nm]), "adamw_" + nm)
        D_[nm], NM[nm], NV[nm] = [o.reshape(W[nm].shape) for o in outs]
    small_rows, small_land = devices_wait(small_rows, small_land, small_sems, D_[BIG[-1]], "small_grads_wait")
    small_sum = device_sum(small_land, small_rows, place[2:3], "small_grads_sum")
    mine = lax.dynamic_slice_in_dim(small_sum, n_rep + chip * n_shard, n_shard, axis=0)
    g_small = jnp.concatenate([small_sum[:n_rep], mine], axis=0)
    small_names = [n for n, _ in REPLICATED_SMALL + SHARDED_SMALL]
    small_shapes = [s for _, s in REPLICATED_SMALL + SHARDED_SMALL]
    G.update(dict(zip(small_names, _unpack(g_small, small_shapes))))
    packs = [_pack([d[n] for n in small_names])[None] for d in (W, M1, M2)]
    outs = adamw(packs[0], g_small[None], packs[1], packs[2], "adamw_small")
    for dst, o in zip((D_, NM, NV), outs):
        dst.update(dict(zip(small_names, _unpack(o[0], small_shapes))))

    return (loss, grad_x[None], *[G[n] for n in WEIGHT_ORDER], *[D_[n] for n in WEIGHT_ORDER],
            *[NM[n] for n in WEIGHT_ORDER], *[NV[n] for n in WEIGHT_ORDER])


def _head_sum(dg):
    return jnp.sum(dg.reshape(N_HEADS, HEAD_DIM), axis=0, keepdims=True)
```

```python
import functools
import math

import numpy as np
import jax
import jax.numpy as jnp
from jax import lax
from jax.experimental import pallas as pl
from jax.experimental.pallas import tpu as pltpu

F32 = jnp.float32
BF16 = jnp.bfloat16

D_MODEL = 1024
N_HEADS = 16
HEAD_DIM = 64
A_WIDTH = 512
POOL_WINDOWS = (2, 4, 8, 16)
POOL_GROUP = 128
EVEN_IN = 2048
D_FF = 2816
DILATED_PAIRS = ((128, 1), (512, 4), (2048, 16))
ATT_BLOCK = 128
N_REL_BUCKETS = 32
REL_MAX_DISTANCE = 2048
EPS = 1e-6
MASK_VALUE = -1e30
ADAM_LR, ADAM_B1, ADAM_B2, ADAM_EPS, ADAM_WD, ADAM_STEP = 0.001, 0.9, 0.999, 1e-08, 0.01, 10

VMEM_LIMIT_BYTES = 48 * 1024 * 1024
N_CHIPS = 4
N_DEV = 8
MESH = pl.DeviceIdType.MESH


def _params(*sem):
    return pltpu.CompilerParams(dimension_semantics=sem if sem else None, vmem_limit_bytes=VMEM_LIMIT_BYTES)


def _sds(shape, dtype):
    return jax.ShapeDtypeStruct(tuple(shape), dtype)


def cast_bf16(x, name, tr=None):
    lead, (R, C) = x.shape[:-2], x.shape[-2:]
    n = int(np.prod(lead)) if lead else 1
    x3 = x.reshape((n, R, C))
    tr = tr or R

    def body(x_ref, o_ref):
        o_ref[...] = x_ref[...].astype(BF16)

    out = pl.pallas_call(
        body, name=name, grid=(n, R // tr),
        in_specs=[pl.BlockSpec((None, tr, C), lambda i, r: (i, r, 0))],
        out_specs=pl.BlockSpec((None, tr, C), lambda i, r: (i, r, 0)),
        out_shape=_sds((n, R, C), BF16), compiler_params=_params("parallel", "parallel"),
    )(x3)
    return out.reshape(lead + (R, C))


def rmsnorm_fwd(x, g, name, ts=512):
    S, Dm = x.shape

    def body(x_ref, g_ref, o_ref):
        xv = x_ref[...]
        r = lax.rsqrt(jnp.mean(xv * xv, axis=-1, keepdims=True) + EPS)
        o_ref[...] = ((xv * r) * g_ref[...]).astype(BF16)

    return pl.pallas_call(
        body, name=name, grid=(S // ts,),
        in_specs=[pl.BlockSpec((ts, Dm), lambda i: (i, 0)), pl.BlockSpec((1, Dm), lambda i: (0, 0))],
        out_specs=pl.BlockSpec((ts, Dm), lambda i: (i, 0)),
        out_shape=_sds((S, Dm), BF16), compiler_params=_params("parallel"),
    )(x, g)


def rmsnorm_bwd(x, g, dxn, dres, name, ts=512):
    S, Dm = x.shape

    def body(x_ref, g_ref, d_ref, r_ref, dx_ref, dxb_ref, dg_ref):
        xv = x_ref[...]
        dv = d_ref[...].astype(F32)
        r = lax.rsqrt(jnp.mean(xv * xv, axis=-1, keepdims=True) + EPS)
        gx = dv * g_ref[...]
        dot = jnp.sum(gx * xv, axis=-1, keepdims=True)
        dx = r_ref[...] + r * gx - xv * ((r * r * r) * (dot * (1.0 / Dm)))
        dx_ref[...] = dx
        dxb_ref[...] = dx.astype(BF16)
        part = jnp.sum(dv * (xv * r), axis=0, keepdims=True)

        @pl.when(pl.program_id(0) == 0)
        def _():
            dg_ref[...] = part

        @pl.when(pl.program_id(0) > 0)
        def _():
            dg_ref[...] += part

    row = pl.BlockSpec((ts, Dm), lambda i: (i, 0))
    vec = pl.BlockSpec((1, Dm), lambda i: (0, 0))
    return pl.pallas_call(
        body, name=name, grid=(S // ts,),
        in_specs=[row, vec, row, row], out_specs=[row, row, vec],
        out_shape=[_sds((S, Dm), F32), _sds((S, Dm), BF16), _sds((1, Dm), F32)], compiler_params=_params("arbitrary"),
    )(x, g, dxn, dres)


def mm_nn(a, w, name, layer=0, res=None, out_dtype=F32, tm=512):
    M, K = a.shape
    J, _, _, Ns = w.shape

    def body(*refs):
        a_ref, w_ref = refs[0], refs[1]
        o_ref = refs[-1]
        acc = jnp.dot(a_ref[...], w_ref[...], preferred_element_type=F32)
        if res is not None:
            acc = refs[2][...] + acc
        o_ref[...] = acc.astype(o_ref.dtype)

    in_specs = [pl.BlockSpec((tm, K), lambda j, m: (m, 0)),
                pl.BlockSpec((None, None, K, Ns), lambda j, m: (j, layer, 0, 0))]
    args = [a, w]
    if res is not None:
        in_specs.append(pl.BlockSpec((tm, Ns), lambda j, m: (m, j)))
        args.append(res)
    return pl.pallas_call(
        body, name=name, grid=(J, M // tm), in_specs=in_specs,
        out_specs=pl.BlockSpec((tm, Ns), lambda j, m: (m, j)),
        out_shape=_sds((M, J * Ns), out_dtype), compiler_params=_params("parallel", "parallel"),
    )(*args)


def mm_nt(dy, w, name, tr, layer=0, out_dtype=F32, tm=512):
    M = dy.shape[0]
    J, _, R, Ns = w.shape
    dims = (((1,), (1,)), ((), ()))

    def body(dy_ref, w_ref, o_ref, *scratch):
        p = lax.dot_general(dy_ref[...], w_ref[...], dims, preferred_element_type=F32)
        if J == 1:
            o_ref[...] = p.astype(o_ref.dtype)
            return
        acc_ref, = scratch
        j = pl.program_id(2)

        @pl.when(j == 0)
        def _():
            acc_ref[...] = p

        @pl.when(j > 0)
        def _():
            acc_ref[...] += p

        @pl.when(j == J - 1)
        def _():
            o_ref[...] = acc_ref[...].astype(o_ref.dtype)

    return pl.pallas_call(
        body, name=name, grid=(R // tr, M // tm, J),
        in_specs=[pl.BlockSpec((tm, Ns), lambda r, m, j: (m, j)),
                  pl.BlockSpec((None, None, tr, Ns), lambda r, m, j: (j, layer, r, 0))],
        out_specs=pl.BlockSpec((tm, tr), lambda r, m, j: (m, r)),
        out_shape=_sds((M, R), out_dtype),
        scratch_shapes=[] if J == 1 else [pltpu.VMEM((tm, tr), F32)],
        compiler_params=_params("parallel", "parallel", "arbitrary"),
    )(dy, w)


def mm_tn(a, dy, name, J, tk, tm=512):
    M, K = a.shape
    Ns = dy.shape[1] // J
    n_m = M // tm
    dims = (((0,), (0,)), ((), ()))

    def body(a_ref, dy_ref, o_ref, acc_ref):
        p = lax.dot_general(a_ref[...], dy_ref[...], dims, preferred_element_type=F32)
        m = pl.program_id(2)

        @pl.when(m == 0)
        def _():
            acc_ref[...] = p

        @pl.when(m > 0)
        def _():
            acc_ref[...] += p

        @pl.when(m == n_m - 1)
        def _():
            o_ref[...] = acc_ref[...].astype(BF16)

    return pl.pallas_call(
        body, name=name, grid=(J, K // tk, n_m),
        in_specs=[pl.BlockSpec((tm, tk), lambda j, k, m: (m, k)), pl.BlockSpec((tm, Ns), lambda j, k, m: (m, j))],
        out_specs=pl.BlockSpec((None, tk, Ns), lambda j, k, m: (j, k, 0)),
        out_shape=_sds((J, K, Ns), BF16), scratch_shapes=[pltpu.VMEM((tk, Ns), F32)],
        compiler_params=_params("parallel", "parallel", "arbitrary"),
    )(a, dy)


HALO = 16


def _shift_down(x, s):
    return pltpu.roll(x, s, 0)


def _shift_up(x, s):
    return pltpu.roll(x, x.shape[0] - s, 0)


def _conv3(z, cw):
    return (_shift_down(z, 2) * cw[0:1] + _shift_down(z, 1) * cw[1:2]) + z * cw[2:3]


def _window_count(first_row, n, k):
    t = first_row + lax.broadcasted_iota(jnp.int32, (n, 1), 0)
    return jnp.clip(t + 1, 1, k).astype(F32)


def mixer_fwd(proj, conv_w, pool_w, pool_scale, name, ts=256):
    S = proj.shape[0]
    n = ts + HALO

    def body(pm_ref, pb_ref, cw_ref, pw_ref, ps_ref, o_ref):
        i = pl.program_id(0)
        before = jnp.where(i > 0, pb_ref[...], 0.0)
        ext = jnp.concatenate([before, pm_ref[...]], axis=0)
        cw = cw_ref[...]
        z = ext[:, 2 * A_WIDTH:3 * A_WIDTH] * ext[:, 0:A_WIDTH]
        cz = _conv3(z, cw)
        ya = pm_ref[:, A_WIDTH:2 * A_WIDTH] * cz[HALO:]
        o_ref[:, 0:A_WIDTH] = ya.astype(BF16)
        for g, k in enumerate(POOL_WINDOWS):
            lo = 3 * A_WIDTH + g * POOL_GROUP
            p = ext[:, lo:lo + POOL_GROUP]
            w = p
            s = 1
            while s < k:
                w = w + _shift_down(w, s)
                s *= 2
            pooled = w / _window_count(i * ts - HALO, n, k) - p
            yb = jnp.dot(pooled[HALO:].astype(BF16), pw_ref[g], preferred_element_type=F32)
            yb = yb * ps_ref[:, g * POOL_GROUP:(g + 1) * POOL_GROUP]
            o_ref[:, A_WIDTH + g * POOL_GROUP:A_WIDTH + (g + 1) * POOL_GROUP] = yb.astype(BF16)

    hb = ts // HALO
    return pl.pallas_call(
        body, name=name, grid=(S // ts,),
        in_specs=[
            pl.BlockSpec((ts, EVEN_IN), lambda i: (i, 0)),
            pl.BlockSpec((HALO, EVEN_IN), lambda i: (jnp.maximum(i * hb - 1, 0), 0)),
            pl.BlockSpec((3, A_WIDTH), lambda i: (0, 0)),
            pl.BlockSpec((4, POOL_GROUP, POOL_GROUP), lambda i: (0, 0, 0)),
            pl.BlockSpec((1, 4 * POOL_GROUP), lambda i: (0, 0)),
        ],
        out_specs=pl.BlockSpec((ts, D_MODEL), lambda i: (i, 0)),
        out_shape=_sds((S, D_MODEL), BF16), compiler_params=_params("parallel"),
    )(proj, proj, conv_w, pool_w, pool_scale)


def mixer_bwd(proj, dmix, conv_w, pool_w, pool_scale, name, ts=256):
    S = proj.shape[0]
    n = ts + 2 * HALO
    nt = S // ts
    tn_dims = (((0,), (0,)), ((), ()))
    nt_dims = (((1,), (1,)), ((), ()))

    def body(pm_ref, pb_ref, pa_ref, dm_ref, da_ref, cw_ref, pw_ref, ps_ref, o_ref, dcw_ref, dpw_ref, dps_ref):
        i = pl.program_id(0)
        last = i == nt - 1
        before = jnp.where(i > 0, pb_ref[...], 0.0)
        after = jnp.where(last, 0.0, pa_ref[...])
        ext = jnp.concatenate([before, pm_ref[...], after], axis=0)
        dafter = jnp.where(last, 0.0, da_ref[...])
        dext = jnp.concatenate([jnp.zeros((HALO, D_MODEL), F32), dm_ref[...], dafter], axis=0)
        cw = cw_ref[...]
        main = slice(HALO, HALO + ts)

        @pl.when(i == 0)
        def _():
            dcw_ref[...] = jnp.zeros_like(dcw_ref)
            dpw_ref[...] = jnp.zeros_like(dpw_ref)
            dps_ref[...] = jnp.zeros_like(dps_ref)

        h, gb, gc = ext[:, 0:A_WIDTH], ext[:, A_WIDTH:2 * A_WIDTH], ext[:, 2 * A_WIDTH:3 * A_WIDTH]
        z = gc * h
        z1, z2 = _shift_down(z, 1), _shift_down(z, 2)
        cz = (z2 * cw[0:1] + z1 * cw[1:2]) + z * cw[2:3]
        dya = dext[:, 0:A_WIDTH]
        dcz = dya * gb
        dz = dcz * cw[2:3] + _shift_up(dcz, 1) * cw[1:2] + _shift_up(dcz, 2) * cw[0:1]
        o_ref[:, 0:A_WIDTH] = (dz * gc)[main].astype(BF16)
        o_ref[:, A_WIDTH:2 * A_WIDTH] = (dya * cz)[main].astype(BF16)
        o_ref[:, 2 * A_WIDTH:3 * A_WIDTH] = (dz * h)[main].astype(BF16)
        dczm = dcz[main]
        dcw_ref[0:1, :] += jnp.sum(dczm * z2[main], axis=0, keepdims=True)
        dcw_ref[1:2, :] += jnp.sum(dczm * z1[main], axis=0, keepdims=True)
        dcw_ref[2:3, :] += jnp.sum(dczm * z[main], axis=0, keepdims=True)

        for g, k in enumerate(POOL_WINDOWS):
            lo = 3 * A_WIDTH + g * POOL_GROUP
            cols = slice(g * POOL_GROUP, (g + 1) * POOL_GROUP)
            p = ext[:, lo:lo + POOL_GROUP]
            w = p
            s = 1
            while s < k:
                w = w + _shift_down(w, s)
                s *= 2
            cnt = _window_count(i * ts - HALO, n, k)
            pooled = (w / cnt - p)[main].astype(BF16)
            dyb = dext[:, A_WIDTH + g * POOL_GROUP:A_WIDTH + (g + 1) * POOL_GROUP]
            e = dyb * ps_ref[:, cols]
            pre = jnp.dot(pooled, pw_ref[g], preferred_element_type=F32)
            dps_ref[:, cols] += jnp.sum(dyb[main] * pre, axis=0, keepdims=True)
            dpw_ref[g] += lax.dot_general(pooled, e[main].astype(BF16), tn_dims, preferred_element_type=F32)
            dpooled = lax.dot_general(e.astype(BF16), pw_ref[g], nt_dims, preferred_element_type=F32)
            q = dpooled / cnt
            a = q
            s = 1
            while s < k:
                a = a + _shift_up(a, s)
                s *= 2
            o_ref[:, lo:lo + POOL_GROUP] = (a - dpooled)[main].astype(BF16)

    hb = ts // HALO
    nh = S // HALO
    before_map = lambda i: (jnp.maximum(i * hb - 1, 0), 0)
    after_map = lambda i: (jnp.minimum((i + 1) * hb, nh - 1), 0)
    full = lambda *shape: pl.BlockSpec(shape, lambda i: (0,) * len(shape))
    return pl.pallas_call(
        body, name=name, grid=(nt,),
        in_specs=[
            pl.BlockSpec((ts, EVEN_IN), lambda i: (i, 0)),
            pl.BlockSpec((HALO, EVEN_IN), before_map),
            pl.BlockSpec((HALO, EVEN_IN), after_map),
            pl.BlockSpec((ts, D_MODEL), lambda i: (i, 0)),
            pl.BlockSpec((HALO, D_MODEL), after_map),
            full(3, A_WIDTH), full(4, POOL_GROUP, POOL_GROUP), full(1, 4 * POOL_GROUP),
        ],
        out_specs=[pl.BlockSpec((ts, EVEN_IN), lambda i: (i, 0)), full(3, A_WIDTH), full(4, POOL_GROUP, POOL_GROUP),
                   full(1, 4 * POOL_GROUP)],
        out_shape=[_sds((S, EVEN_IN), BF16), _sds((3, A_WIDTH), F32), _sds((4, POOL_GROUP, POOL_GROUP), F32),
                   _sds((1, 4 * POOL_GROUP), F32)],
        compiler_params=_params("arbitrary"),
    )(proj, proj, proj, dmix, dmix, conv_w, pool_w, pool_scale)


FFN_HALO = 8
FFN_TC = 1408


def glu_fwd(up, conv_w, conv_b, name, ts=256):
    S = up.shape[0]
    nc = D_FF // FFN_TC

    def body(gm_ref, gb_ref, um_ref, ub_ref, cwg_ref, cwu_ref, cbg_ref, cbu_ref, o_ref):
        i = pl.program_id(0)

        def conv(m_ref, b_ref, cw_ref, cb_ref):
            before = jnp.where(i > 0, b_ref[...], 0.0)
            ext = jnp.concatenate([before, m_ref[...]], axis=0)
            return _conv3(ext, cw_ref[...])[FFN_HALO:] + cb_ref[...]

        gate = conv(gm_ref, gb_ref, cwg_ref, cbg_ref)
        upv = conv(um_ref, ub_ref, cwu_ref, cbu_ref)
        o_ref[...] = ((gate * (1.0 / (1.0 + jnp.exp(-gate)))) * upv).astype(BF16)

    hb = ts // FFN_HALO
    main = lambda off: pl.BlockSpec((ts, FFN_TC), lambda i, c: (i, c + off))
    halo = lambda off: pl.BlockSpec((FFN_HALO, FFN_TC), lambda i, c: (jnp.maximum(i * hb - 1, 0), c + off))
    cw = lambda off: pl.BlockSpec((3, FFN_TC), lambda i, c: (0, c + off))
    cb = lambda off: pl.BlockSpec((1, FFN_TC), lambda i, c: (0, c + off))
    return pl.pallas_call(
        body, name=name, grid=(S // ts, nc),
        in_specs=[main(0), halo(0), main(nc), halo(nc), cw(0), cw(nc), cb(0), cb(nc)],
        out_specs=pl.BlockSpec((ts, FFN_TC), lambda i, c: (i, c)),
        out_shape=_sds((S, D_FF), BF16), compiler_params=_params("parallel", "parallel"),
    )(up, up, up, up, conv_w, conv_w, conv_b, conv_b)


def glu_bwd(up, da, conv_w, conv_b, name, ts=256):
    S = up.shape[0]
    nc = D_FF // FFN_TC
    nt = S // ts
    main = slice(FFN_HALO, FFN_HALO + ts)
    W = 2 * D_FF

    def body(xm_ref, xb_ref, xa_ref, dm_ref, da_ref, cw_ref, cb_ref, dx_ref, dcw_ref, dcb_ref):
        i = pl.program_id(0)
        last = i == nt - 1

        @pl.when(i == 0)
        def _():
            dcw_ref[...] = jnp.zeros_like(dcw_ref)
            dcb_ref[...] = jnp.zeros_like(dcb_ref)

        def ext_of(cols):
            before = jnp.where(i > 0, xb_ref[:, cols], 0.0)
            return jnp.concatenate([before, xm_ref[:, cols], xa_ref[:, cols]], axis=0)

        def back(x, d, cols):
            cw = cw_ref[:, cols]
            dx = d * cw[2:3] + _shift_up(d, 1) * cw[1:2] + _shift_up(d, 2) * cw[0:1]
            dx_ref[:, cols] = dx[main].astype(BF16)
            dmn = d[main]
            dcb_ref[:, cols] += jnp.sum(dmn, axis=0, keepdims=True)
            dcw_ref[0:1, cols] += jnp.sum(dmn * _shift_down(x, 2)[main], axis=0, keepdims=True)
            dcw_ref[1:2, cols] += jnp.sum(dmn * _shift_down(x, 1)[main], axis=0, keepdims=True)
            dcw_ref[2:3, cols] += jnp.sum(dmn * x[main], axis=0, keepdims=True)

        for c in range(nc):
            gcols = slice(c * FFN_TC, (c + 1) * FFN_TC)
            ucols = slice(D_FF + c * FFN_TC, D_FF + (c + 1) * FFN_TC)
            xg, xu = ext_of(gcols), ext_of(ucols)
            ug = _conv3(xg, cw_ref[:, gcols]) + cb_ref[:, gcols]
            uu = _conv3(xu, cw_ref[:, ucols]) + cb_ref[:, ucols]
            dafter = jnp.where(last, 0.0, da_ref[:, gcols].astype(F32))
            dae = jnp.concatenate([jnp.zeros((FFN_HALO, FFN_TC), F32), dm_ref[:, gcols].astype(F32), dafter], axis=0)
            sg = 1.0 / (1.0 + jnp.exp(-ug))
            duu = dae * (ug * sg)
            dug = (dae * uu) * (sg * (1.0 + ug * (1.0 - sg)))
            back(xg, dug, gcols)
            back(xu, duu, ucols)

    hb = ts // FFN_HALO
    nh = S // FFN_HALO
    before_map = lambda i: (jnp.maximum(i * hb - 1, 0), 0)
    after_map = lambda i: (jnp.minimum((i + 1) * hb, nh - 1), 0)
    return pl.pallas_call(
        body, name=name, grid=(nt,),
        in_specs=[pl.BlockSpec((ts, W), lambda i: (i, 0)), pl.BlockSpec((FFN_HALO, W), before_map),
                  pl.BlockSpec((FFN_HALO, W), after_map), pl.BlockSpec((ts, D_FF), lambda i: (i, 0)),
                  pl.BlockSpec((FFN_HALO, D_FF), after_map), pl.BlockSpec((3, W), lambda i: (0, 0)),
                  pl.BlockSpec((1, W), lambda i: (0, 0))],
        out_specs=[pl.BlockSpec((ts, W), lambda i: (i, 0)), pl.BlockSpec((3, W), lambda i: (0, 0)),
                   pl.BlockSpec((1, W), lambda i: (0, 0))],
        out_shape=[_sds((S, W), BF16), _sds((3, W), F32), _sds((1, W), F32)],
        compiler_params=_params("arbitrary"),
    )(up, up, up, da, da, conv_w, conv_b)


def _head_mean_matrix():
    h = np.arange(D_MODEL) // HEAD_DIM
    return jnp.asarray((h[:, None] == h[None, :]).astype(np.float32) / HEAD_DIM, dtype=BF16)


def _head_mean(v, gm):
    hi = v.astype(BF16)
    lo = (v - hi.astype(F32)).astype(BF16)
    return jnp.dot(hi, gm, preferred_element_type=F32) + jnp.dot(lo, gm, preferred_element_type=F32)


def qknorm_fwd(qkv, gqk, name, ts=512):
    S = qkv.shape[0]

    def body(x_ref, g_ref, gm_ref, o_ref):
        part = pl.program_id(0)
        x = x_ref[...]

        @pl.when(part < 2)
        def _():
            r = lax.rsqrt(_head_mean(x * x, gm_ref[...]) + EPS)
            o_ref[...] = ((x * r) * g_ref[...]).astype(BF16)

        @pl.when(part == 2)
        def _():
            o_ref[...] = x.astype(BF16)

    return pl.pallas_call(
        body, name=name, grid=(3, S // ts),
        in_specs=[pl.BlockSpec((ts, D_MODEL), lambda p, i: (i, p)), pl.BlockSpec((None, 1, D_MODEL), lambda p, i: (p, 0, 0)),
                  pl.BlockSpec((D_MODEL, D_MODEL), lambda p, i: (0, 0))],
        out_specs=pl.BlockSpec((ts, D_MODEL), lambda p, i: (i, p)),
        out_shape=_sds((S, 3 * D_MODEL), BF16), compiler_params=_params("parallel", "parallel"),
    )(qkv, gqk, _head_mean_matrix())


def qknorm_bwd(qkv, dq, dk, dv, gqk, name, ts=256):
    S = qkv.shape[0]

    def body(x_ref, dq_ref, dk_ref, dv_ref, g_ref, gm_ref, o_ref, dg_ref):
        @pl.when(pl.program_id(0) == 0)
        def _():
            dg_ref[...] = jnp.zeros_like(dg_ref)

        gm = gm_ref[...]
        for part, d_ref in enumerate((dq_ref, dk_ref)):
            cols = slice(part * D_MODEL, (part + 1) * D_MODEL)
            x = x_ref[:, cols]
            d = d_ref[...]
            r = lax.rsqrt(_head_mean(x * x, gm) + EPS)
            gx = d * g_ref[part]
            o_ref[:, cols] = (r * gx - x * ((r * r * r) * _head_mean(gx * x, gm))).astype(BF16)
            dg_ref[part] += jnp.sum(d * (x * r), axis=0, keepdims=True)
        o_ref[:, 2 * D_MODEL:] = dv_ref[...].astype(BF16)

    row = pl.BlockSpec((ts, D_MODEL), lambda i: (i, 0))
    wide = pl.BlockSpec((ts, 3 * D_MODEL), lambda i: (i, 0))
    gains = pl.BlockSpec((3, 1, D_MODEL), lambda i: (0, 0, 0))
    return pl.pallas_call(
        body, name=name, grid=(S // ts,),
        in_specs=[wide, row, row, row, gains, pl.BlockSpec((D_MODEL, D_MODEL), lambda i: (0, 0))],
        out_specs=[wide, gains],
        out_shape=[_sds((S, 3 * D_MODEL), BF16), _sds((3, 1, D_MODEL), F32)],
        compiler_params=_params("arbitrary"),
    )(qkv, dq, dk, dv, gqk, _head_mean_matrix())


RESIDUES = 16


def _block_order(dil):
    runs = RESIDUES // dil
    slot = np.arange(ATT_BLOCK)
    return (slot % (ATT_BLOCK // runs)) * runs + slot // (ATT_BLOCK // runs)


def _bucket_tables():
    n = ATT_BLOCK
    max_exact = N_REL_BUCKETS // 2
    buckets, valids = [], []
    for _, dil in DILATED_PAIRS:
        order = _block_order(dil)
        a = order[:, None]
        c = np.concatenate([order, n + order])[None, :]
        first_half = (np.arange(2 * n) < n)[None, :]
        rel = a + n - c
        band = (rel >= 0) & (rel <= n)
        dist = np.clip(rel, 0, n) * dil
        dd = np.maximum(dist, 1).astype(np.float32)
        large = max_exact + (np.log(dd / np.float32(max_exact)) / np.float32(math.log(REL_MAX_DISTANCE / max_exact))
                             * np.float32(N_REL_BUCKETS - max_exact)).astype(np.int32)
        large = np.minimum(large, N_REL_BUCKETS - 1)
        buckets.append(np.where(dist < max_exact, dist, large).reshape(1, -1))
        valids.append(np.stack([(band & ~first_half).reshape(1, -1), band.reshape(1, -1)]))
    return np.stack(buckets).astype(np.int32), np.stack(valids).astype(np.int32)


BIAS_CHUNK = 8192


def _split3(x):
    a = x.astype(BF16)
    r = x - a.astype(F32)
    b = r.astype(BF16)
    c = (r - b.astype(F32)).astype(BF16)
    return a, b, c


def bias_expand(rel_bias_t, name):
    bucket, valid = _bucket_tables()
    nq = bucket.shape[-1]

    def body(t_ref, b_ref, v_ref, o_ref):
        onehot = (lax.broadcasted_iota(jnp.int32, (N_REL_BUCKETS, BIAS_CHUNK), 0) == b_ref[...]).astype(BF16)
        acc = None
        for term in _split3(t_ref[...]):
            p = jnp.dot(term, onehot, preferred_element_type=F32)
            acc = p if acc is None else acc + p
        o_ref[...] = jnp.where(v_ref[...] > 0, acc, MASK_VALUE)

    return pl.pallas_call(
        body, name=name, grid=(3, 2, nq // BIAS_CHUNK),
        in_specs=[pl.BlockSpec((N_HEADS, N_REL_BUCKETS), lambda b, v, c: (0, 0)),
                  pl.BlockSpec((None, 1, BIAS_CHUNK), lambda b, v, c: (b, 0, c)),
                  pl.BlockSpec((None, None, 1, BIAS_CHUNK), lambda b, v, c: (b, v, 0, c))],
        out_specs=pl.BlockSpec((None, None, N_HEADS, BIAS_CHUNK), lambda b, v, c: (b, v, 0, c)),
        out_shape=_sds((3, 2, N_HEADS, nq), F32), compiler_params=_params("parallel", "parallel", "parallel"),
    )(rel_bias_t, jnp.asarray(bucket), jnp.asarray(valid))


def bias_reduce(dbias, name):
    bucket, _ = _bucket_tables()
    nq = bucket.shape[-1]
    dims = (((1,), (1,)), ((), ()))

    def body(d_ref, b_ref, o_ref):
        onehot = (lax.broadcasted_iota(jnp.int32, (N_REL_BUCKETS, BIAS_CHUNK), 0) == b_ref[...]).astype(BF16)
        acc = None
        for term in _split3(d_ref[...]):
            p = lax.dot_general(term, onehot, dims, preferred_element_type=F32)
            acc = p if acc is None else acc + p

        @pl.when(pl.program_id(1) == 0)
        def _():
            o_ref[...] = acc

        @pl.when(pl.program_id(1) > 0)
        def _():
            o_ref[...] += acc

    return pl.pallas_call(
        body, name=name, grid=(3, nq // BIAS_CHUNK),
        in_specs=[pl.BlockSpec((None, N_HEADS, BIAS_CHUNK), lambda b, c: (b, 0, c)),
                  pl.BlockSpec((None, 1, BIAS_CHUNK), lambda b, c: (b, 0, c))],
        out_specs=pl.BlockSpec((None, N_HEADS, N_REL_BUCKETS), lambda b, c: (b, 0, 0)),
        out_shape=_sds((3, N_HEADS, N_REL_BUCKETS), F32), compiler_params=_params("parallel", "arbitrary"),
    )(dbias, jnp.asarray(bucket))


PAIR = 2 * HEAD_DIM
N_PAIRS = N_HEADS // 2
_NT = (((1,), (1,)), ((), ()))
_TN = (((0,), (0,)), ((), ()))


def _low_lanes(shape):
    return lax.broadcasted_iota(jnp.int32, shape, 1) < HEAD_DIM


ATTN_VMEM_LIMIT_BYTES = 56 * 1024 * 1024
BRANCH_ORDER = (2, 1, 0)


def _regroup(dst, src, L16):
    for r in range(RESIDUES):
        dst[pl.ds(r * L16, L16), :] = src[pl.ds(r, L16, stride=RESIDUES), :]


def _ungroup(dst, src, L16):
    for r in range(RESIDUES):
        dst[pl.ds(r, L16, stride=RESIDUES), :] = src[pl.ds(r * L16, L16), :]


def _branch_geometry(branch, S):
    dil = DILATED_PAIRS[branch][1]
    runs = RESIDUES // dil
    return dil, runs, ATT_BLOCK // runs, S // dil // ATT_BLOCK


def _block_rows(it, branch, S):
    dil, runs, run_len, n_blocks = _branch_geometry(branch, S)
    L16 = S // RESIDUES
    r, b = it // n_blocks, it % n_blocks
    prev = jnp.maximum(b - 1, 0)
    cur_rows = [pl.multiple_of((j * dil + r) * L16 + run_len * b, 8) for j in range(runs)]
    prev_rows = [pl.multiple_of((j * dil + r) * L16 + run_len * prev, 8) for j in range(runs)]
    return cur_rows, prev_rows, jnp.minimum(b, 1)


def _load_block(ref, rows, run_len):
    parts = [ref[pl.ds(o, run_len), :] for o in rows]
    return parts[0] if len(parts) == 1 else jnp.concatenate(parts, axis=0)


def _store_block(ref, rows, run_len, value, add=False):
    for j, o in enumerate(rows):
        part = value[j * run_len:(j + 1) * run_len]
        if add:
            ref[pl.ds(o, run_len), :] += part
        else:
            ref[pl.ds(o, run_len), :] = part


def attn_fwd(qkvn, bias, name):
    S = qkvn.shape[0]
    L16 = S // RESIDUES
    n_iter = S // ATT_BLOCK

    def body(q_ref, k_ref, v_ref, b_ref, o_ref, lse_ref, stage, qp, kp, vp, acc_s, m_s, l_s):
        for src, dst in ((q_ref, qp), (k_ref, kp), (v_ref, vp)):
            stage[...] = src[...].astype(F32)
            _regroup(dst, stage, L16)
        low = _low_lanes((ATT_BLOCK, PAIR))

        for branch in BRANCH_ORDER:
            _, _, run_len, _ = _branch_geometry(branch, S)
            first = branch == BRANCH_ORDER[0]

            def step(it, carry, branch=branch, run_len=run_len, first=first):
                cur, prev, variant = _block_rows(it, branch, S)
                q = _load_block(qp, cur, run_len).astype(BF16)
                k = jnp.concatenate([_load_block(kp, prev, run_len), _load_block(kp, cur, run_len)], axis=0).astype(BF16)
                v = jnp.concatenate([_load_block(vp, prev, run_len), _load_block(vp, cur, run_len)], axis=0).astype(BF16)
                pv, mx, den = [], [], []
                for hh in range(2):
                    qh = jnp.where(low if hh == 0 else ~low, q, jnp.zeros_like(q))
                    s = lax.dot_general(qh, k, _NT, preferred_element_type=F32) * (HEAD_DIM ** -0.5)
                    s = s + b_ref[2 * branch + variant, hh]
                    m = jnp.max(s, axis=-1, keepdims=True)
                    p = jnp.exp(s - m)
                    den.append(jnp.sum(p, axis=-1, keepdims=True))
                    mx.append(m)
                    pv.append(jnp.dot(p.astype(BF16), v, preferred_element_type=F32))
                acc = jnp.where(low, pv[0], pv[1])
                m = jnp.where(low, mx[0], mx[1])
                l = jnp.where(low, den[0], den[1])
                if not first:
                    m_old = _load_block(m_s, cur, run_len)
                    m_new = jnp.maximum(m_old, m)
                    a_old, a_new = jnp.exp(m_old - m_new), jnp.exp(m - m_new)
                    acc = _load_block(acc_s, cur, run_len) * a_old + acc * a_new
                    l = _load_block(l_s, cur, run_len) * a_old + l * a_new
                    m = m_new
                _store_block(acc_s, cur, run_len, acc)
                _store_block(m_s, cur, run_len, m)
                _store_block(l_s, cur, run_len, l)
                return carry

            lax.fori_loop(0, n_iter, step, 0)

        acc_s[...] = acc_s[...] / l_s[...]
        _ungroup(stage, acc_s, L16)
        o_ref[...] = stage[...].astype(BF16)
        m_s[...] = m_s[...] + jnp.log(l_s[...])
        _ungroup(lse_ref, m_s, L16)

    col = lambda part: pl.BlockSpec((S, PAIR), lambda hp: (0, part * N_PAIRS + hp))
    out = pl.BlockSpec((S, PAIR), lambda hp: (0, hp))
    return pl.pallas_call(
        body, name=name, grid=(N_PAIRS,),
        in_specs=[col(0), col(1), col(2), pl.BlockSpec((6, 2, ATT_BLOCK, 2 * ATT_BLOCK), lambda hp: (0, hp, 0, 0))],
        out_specs=[out, out], out_shape=[_sds((S, D_MODEL), BF16), _sds((S, D_MODEL), F32)],
        scratch_shapes=[pltpu.VMEM((S, PAIR), F32)] * 7,
        compiler_params=pltpu.CompilerParams(dimension_semantics=("parallel",), vmem_limit_bytes=ATTN_VMEM_LIMIT_BYTES),
    )(qkvn, qkvn, qkvn, bias)


def attn_bwd(qkvn, att, datt, lse, bias, name):
    S = qkvn.shape[0]
    L16 = S // RESIDUES
    n_iter = S // ATT_BLOCK
    TILE = 512

    def body(q_ref, k_ref, v_ref, o_ref, do_ref, lse_ref, b_ref, dq_ref, dk_ref, dv_ref, db_ref,
             qp, kp, vp, dop, ldp, dqp, dkp, dvp):
        stage = dqp
        for src, dst in ((q_ref, qp), (k_ref, kp), (v_ref, vp), (do_ref, dop)):
            stage[...] = src[...].astype(F32)
            _regroup(dst, stage, L16)

        def pack(i, carry):
            rows = pl.ds(pl.multiple_of(i * TILE, TILE), TILE)
            low = _low_lanes((TILE, PAIR))
            lane = lax.broadcasted_iota(jnp.int32, (TILE, PAIR), 1)
            prod = do_ref[rows, :].astype(F32) * o_ref[rows, :].astype(F32)
            d0 = jnp.sum(jnp.where(low, prod, 0.0), axis=-1, keepdims=True)
            d1 = jnp.sum(jnp.where(low, 0.0, prod), axis=-1, keepdims=True)
            stage[rows, :] = jnp.where((lane & (HEAD_DIM // 2)) == 0, lse_ref[rows, :], jnp.where(low, d0, d1))
            return carry

        lax.fori_loop(0, S // TILE, pack, 0)
        _regroup(ldp, stage, L16)
        dqp[...] = jnp.zeros_like(dqp)
        dkp[...] = jnp.zeros_like(dkp)
        dvp[...] = jnp.zeros_like(dvp)
        db_ref[...] = jnp.zeros_like(db_ref)
        low = _low_lanes((ATT_BLOCK, PAIR))

        for branch in BRANCH_ORDER:
            _, _, run_len, _ = _branch_geometry(branch, S)

            def step(it, carry, branch=branch, run_len=run_len):
                cur, prev, variant = _block_rows(it, branch, S)
                q = _load_block(qp, cur, run_len).astype(BF16)
                dout = _load_block(dop, cur, run_len).astype(BF16)
                ld = _load_block(ldp, cur, run_len)
                k = jnp.concatenate([_load_block(kp, prev, run_len), _load_block(kp, cur, run_len)], axis=0).astype(BF16)
                v = jnp.concatenate([_load_block(vp, prev, run_len), _load_block(vp, cur, run_len)], axis=0).astype(BF16)
                dq, dk, dv = [], None, None
                for hh in range(2):
                    keep = low if hh == 0 else ~low
                    base = hh * HEAD_DIM
                    lse_h = ld[:, base:base + 1]
                    delta = ld[:, base + HEAD_DIM // 2:base + HEAD_DIM // 2 + 1]
                    qh = jnp.where(keep, q, jnp.zeros_like(q))
                    doh = jnp.where(keep, dout, jnp.zeros_like(dout))
                    s = lax.dot_general(qh, k, _NT, preferred_element_type=F32) * (HEAD_DIM ** -0.5)
                    p = jnp.exp(s + b_ref[2 * branch + variant, hh] - lse_h)
                    dp = lax.dot_general(doh, v, _NT, preferred_element_type=F32)
                    ds = p * (dp - delta)
                    db_ref[branch, hh] += ds
                    dsb = (ds * (HEAD_DIM ** -0.5)).astype(BF16)
                    dq.append(jnp.dot(dsb, k, preferred_element_type=F32))
                    dkh = lax.dot_general(dsb, qh, _TN, preferred_element_type=F32)
                    dvh = lax.dot_general(p.astype(BF16), doh, _TN, preferred_element_type=F32)
                    dk = dkh if dk is None else dk + dkh
                    dv = dvh if dv is None else dv + dvh
                _store_block(dqp, cur, run_len, jnp.where(low, dq[0], dq[1]), add=True)
                _store_block(dkp, prev, run_len, dk[:ATT_BLOCK], add=True)
                _store_block(dvp, prev, run_len, dv[:ATT_BLOCK], add=True)
                _store_block(dkp, cur, run_len, dk[ATT_BLOCK:], add=True)
                _store_block(dvp, cur, run_len, dv[ATT_BLOCK:], add=True)
                return carry

            lax.fori_loop(0, n_iter, step, 0)

        _ungroup(dq_ref, dqp, L16)
        _ungroup(dk_ref, dkp, L16)
        _ungroup(dv_ref, dvp, L16)

    col = lambda part: pl.BlockSpec((S, PAIR), lambda hp: (0, part * N_PAIRS + hp))
    one = pl.BlockSpec((S, PAIR), lambda hp: (0, hp))
    return pl.pallas_call(
        body, name=name, grid=(N_PAIRS,),
        in_specs=[col(0), col(1), col(2), one, one, one,
                  pl.BlockSpec((6, 2, ATT_BLOCK, 2 * ATT_BLOCK), lambda hp: (0, hp, 0, 0))],
        out_specs=[one, one, one, pl.BlockSpec((3, 2, ATT_BLOCK, 2 * ATT_BLOCK), lambda hp: (0, hp, 0, 0))],
        out_shape=[_sds((S, D_MODEL), F32)] * 3 + [_sds((3, N_HEADS, ATT_BLOCK, 2 * ATT_BLOCK), F32)],
        scratch_shapes=[pltpu.VMEM((S, PAIR), F32)] * 8,
        compiler_params=pltpu.CompilerParams(dimension_semantics=("parallel",), vmem_limit_bytes=ATTN_VMEM_LIMIT_BYTES),
    )(qkvn, qkvn, qkvn, att, datt, lse, bias)


def loss_grad(y, target, name, ts=512):
    S, Dm = y.shape

    def body(y_ref, t_ref, d_ref, db_ref, s_ref):
        e = y_ref[...] - t_ref[...]
        d = e * (1.0 / Dm)
        d_ref[...] = d
        db_ref[...] = d.astype(BF16)
        part = jnp.sum(e * e, axis=0, keepdims=True)

        @pl.when(pl.program_id(0) == 0)
        def _():
            s_ref[...] = part

        @pl.when(pl.program_id(0) > 0)
        def _():
            s_ref[...] += part

    row = pl.BlockSpec((ts, Dm), lambda i: (i, 0))
    vec = pl.BlockSpec((1, Dm), lambda i: (0, 0))
    return pl.pallas_call(
        body, name=name, grid=(S // ts,), in_specs=[row, row], out_specs=[row, row, vec],
        out_shape=[_sds((S, Dm), F32), _sds((S, Dm), BF16), _sds((1, Dm), F32)], compiler_params=_params("arbitrary"),
    )(y, target)


def adamw(w, g, m, v, name):
    n, R, C = w.shape

    def body(w_ref, g_ref, m_ref, v_ref, d_ref, nm_ref, nv_ref):
        gv = g_ref[...]
        m2 = ADAM_B1 * m_ref[...] + (1.0 - ADAM_B1) * gv
        v2 = ADAM_B2 * v_ref[...] + (1.0 - ADAM_B2) * (gv * gv)
        m_hat = m2 / (1.0 - ADAM_B1 ** ADAM_STEP)
        v_hat = v2 / (1.0 - ADAM_B2 ** ADAM_STEP)
        d_ref[...] = -ADAM_LR * (m_hat / (jnp.sqrt(v_hat) + ADAM_EPS) + ADAM_WD * w_ref[...])
        nm_ref[...] = m2
        nv_ref[...] = v2

    tr = R
    while tr * C * 4 > (1 << 21) and tr % 16 == 0:
        tr //= 2
    spec = pl.BlockSpec((None, tr, C), lambda i, r: (i, r, 0))
    return pl.pallas_call(
        body, name=name, grid=(n, R // tr), in_specs=[spec] * 4, out_specs=[spec] * 3,
        out_shape=[_sds((n, R, C), F32)] * 3, compiler_params=_params("parallel", "parallel"),
    )(w, g, m, v)


ANY = pl.BlockSpec(memory_space=pl.ANY)


def _coords():
    return lax.axis_index("x"), lax.axis_index("y"), lax.axis_index("c")


def _other_chips(mx, my):
    return [(1 - mx, my), (mx, 1 - my), (1 - mx, 1 - my)]


def _remote(src, dst, send, recv, dev):
    return pltpu.make_async_remote_copy(src_ref=src, dst_ref=dst, send_sem=send, recv_sem=recv, device_id=dev,
                                        device_id_type=MESH)


def allgather_devices(x, name):
    R, C = x.shape

    def body(x_ref, o_ref, send, recv, local_sem):
        mx, my, mc = _coords()
        me = 4 * mx + 2 * my + mc
        local = pltpu.make_async_copy(x_ref, o_ref.at[me], local_sem)
        local.start()
        peers = []
        for k in range(1, N_DEV):
            px = 1 - mx if k & 4 else mx
            py = 1 - my if k & 2 else my
            pc = 1 - mc if k & 1 else mc
            peers.append((px, py, pc))
        sends = [_remote(x_ref, o_ref.at[me], send.at[k], recv.at[k], p) for k, p in enumerate(peers)]
        for cp in sends:
            cp.start()
        for k, (px, py, pc) in enumerate(peers):
            _remote(x_ref, o_ref.at[4 * px + 2 * py + pc], send.at[k], recv.at[k], (px, py, pc)).wait_recv()
        for cp in sends:
            cp.wait_send()
        local.wait()

    return pl.pallas_call(
        body, name=name, in_specs=[ANY], out_specs=ANY, out_shape=_sds((N_DEV, R, C), x.dtype),
        scratch_shapes=[pltpu.SemaphoreType.DMA((N_DEV - 1,)), pltpu.SemaphoreType.DMA((N_DEV - 1,)),
                        pltpu.SemaphoreType.DMA],
    )(x)


HBM = pl.BlockSpec(memory_space=pltpu.HBM)
SEM = pl.BlockSpec(memory_space=pltpu.SEMAPHORE)
_SPLIT_COPY = pltpu.CompilerParams(has_side_effects=pltpu.SideEffectType.DATAFLOW_SIDE_EFFECTING)


def _in_hbm(a):
    return pltpu.with_memory_space_constraint(a, pltpu.HBM)


def cast_into_slot(w, layer, chip_core, name):
    _, _, hR, C = w.shape

    def body(s_ref, w_ref, o_ref):
        del s_ref
        o_ref[...] = w_ref[...].astype(BF16)

    grid_spec = pltpu.PrefetchScalarGridSpec(
        num_scalar_prefetch=1, grid=(2,),
        in_specs=[pl.BlockSpec((None, None, hR, C), lambda h, s: (layer, h, 0, 0))],
        out_specs=pl.BlockSpec((None, None, hR, C), lambda h, s: (s[0], h, 0, 0)))
    return pl.pallas_call(body, name=name, grid_spec=grid_spec, out_shape=_sds((N_CHIPS, 2, hR, C), BF16),
                          compiler_params=_params("parallel"))(chip_core, w)


def gather_start(lands, groups, name):
    n = len(lands)
    n_groups = len(groups)

    def body(*refs):
        ins = refs[:n]
        sems = refs[n:n + 2 * n_groups]
        token = refs[-1]
        mx, my, mc = _coords()
        chip = 2 * mx + my
        for g, members in enumerate(groups):
            send, recv = sems[2 * g], sems[2 * g + 1]
            for i, a in enumerate(members):
                mine = ins[a].at[chip, mc]
                for k, (px, py) in enumerate(_other_chips(mx, my)):
                    _remote(mine, mine, send.at[3 * i + k], recv.at[3 * i + k], (px, py, mc)).start()
        token[...] = jnp.zeros_like(token)

    sem_shapes = []
    for members in groups:
        sem_shapes += [pltpu.SemaphoreType.DMA((3 * len(members),))] * 2
    outs = pl.pallas_call(
        body, name=name, in_specs=[HBM] * n,
        out_specs=[SEM] * (2 * n_groups) + [HBM] * n + [pl.BlockSpec(memory_space=pltpu.VMEM)],
        out_shape=sem_shapes + [pltpu.HBM(a.shape, a.dtype) for a in lands] + [_sds((SUBLANES, LANES), F32)],
        input_output_aliases={a: 2 * n_groups + a for a in range(n)}, compiler_params=_SPLIT_COPY,
    )(*[_in_hbm(a) for a in lands])
    sems = [(outs[2 * g], outs[2 * g + 1]) for g in range(n_groups)]
    return sems, list(outs[2 * n_groups:2 * n_groups + n]), outs[-1]


def gather_forward(lands, sems, after, name):
    n = len(lands)

    def body(*refs):
        ins = refs[:n]
        send, recv = refs[n], refs[n + 1]
        fsend, frecv = refs[n + 3], refs[n + 4]
        mx, my, mc = _coords()
        for i in range(n):
            for k, (px, py) in enumerate(_other_chips(mx, my)):
                landed = ins[i].at[2 * px + py, mc]
                cp = _remote(landed, landed, send.at[3 * i + k], recv.at[3 * i + k], (px, py, mc))
                cp.wait_send()
                cp.wait_recv()
                _remote(landed, landed, fsend.at[3 * i + k], frecv.at[3 * i + k], (mx, my, 1 - mc)).start()

    outs = pl.pallas_call(
        body, name=name, in_specs=[HBM] * n + [SEM, SEM, ANY], out_specs=[SEM, SEM] + [HBM] * n,
        out_shape=[pltpu.SemaphoreType.DMA((3 * n,))] * 2 + [pltpu.HBM(a.shape, a.dtype) for a in lands],
        input_output_aliases={a: 2 + a for a in range(n)}, compiler_params=_SPLIT_COPY,
    )(*lands, sems[0], sems[1], after)
    return (outs[0], outs[1]), list(outs[2:])


def gather_wait(lands, sems, after, name):
    n = len(lands)

    def body(*refs):
        ins = refs[:n]
        fsend, frecv = refs[n], refs[n + 1]
        mx, my, mc = _coords()
        for i in range(n):
            for k, (px, py) in enumerate(_other_chips(mx, my)):
                theirs = ins[i].at[2 * px + py, 1 - mc]
                cp = _remote(theirs, theirs, fsend.at[3 * i + k], frecv.at[3 * i + k], (mx, my, 1 - mc))
                cp.wait_send()
                cp.wait_recv()

    outs = pl.pallas_call(
        body, name=name, in_specs=[HBM] * n + [SEM, SEM, ANY], out_specs=[HBM] * n,
        out_shape=[pltpu.HBM(a.shape, a.dtype) for a in lands],
        input_output_aliases={a: a for a in range(n)}, compiler_params=_SPLIT_COPY,
    )(*lands, sems[0], sems[1], after)
    return list(outs)


def _peers(mx, my, mc):
    return [(1 - mx if k & 4 else mx, 1 - my if k & 2 else my, 1 - mc if k & 1 else mc) for k in range(1, N_DEV)]


def devices_start(x, name):
    def body(x_ref, land_ref, send, recv, x_thru, land_thru):
        mx, my, mc = _coords()
        me = 4 * mx + 2 * my + mc
        for k, peer in enumerate(_peers(mx, my, mc)):
            _remote(x_ref, land_ref.at[me], send.at[k], recv.at[k], peer).start()

    land = lax.empty((N_DEV,) + x.shape, x.dtype)
    outs = pl.pallas_call(
        body, name=name, in_specs=[HBM, HBM], out_specs=[SEM, SEM, HBM, HBM],
        out_shape=[pltpu.SemaphoreType.DMA((N_DEV - 1,))] * 2 + [pltpu.HBM(x.shape, x.dtype), pltpu.HBM(land.shape, x.dtype)],
        input_output_aliases={0: 2, 1: 3}, compiler_params=_SPLIT_COPY,
    )(_in_hbm(x), _in_hbm(land))
    return (outs[0], outs[1]), outs[2], outs[3]


def devices_wait(x, land, sems, after, name):
    def body(x_ref, land_ref, send, recv, after_ref, x_thru, land_thru):
        mx, my, mc = _coords()
        for k, (px, py, pc) in enumerate(_peers(mx, my, mc)):
            cp = _remote(x_ref, land_ref.at[4 * px + 2 * py + pc], send.at[k], recv.at[k], (px, py, pc))
            cp.wait_send()
            cp.wait_recv()

    outs = pl.pallas_call(
        body, name=name, in_specs=[HBM, HBM, SEM, SEM, ANY], out_specs=[HBM, HBM],
        out_shape=[pltpu.HBM(x.shape, x.dtype), pltpu.HBM(land.shape, land.dtype)],
        input_output_aliases={0: 0, 1: 1}, compiler_params=_SPLIT_COPY,
    )(x, land, sems[0], sems[1], after)
    return outs[0], outs[1]


def device_sum(land, own, me, name):
    _, R, C = land.shape

    def body(s_ref, l_ref, o_ref_in, o_ref):
        acc = None
        for q in range(N_DEV):
            term = jnp.where(s_ref[0] == q, o_ref_in[...], l_ref[q])
            acc = term if acc is None else acc + term
        o_ref[...] = acc

    grid_spec = pltpu.PrefetchScalarGridSpec(
        num_scalar_prefetch=1, grid=(1,),
        in_specs=[pl.BlockSpec((N_DEV, R, C), lambda i, s: (0, 0, 0)), pl.BlockSpec((R, C), lambda i, s: (0, 0))],
        out_specs=pl.BlockSpec((R, C), lambda i, s: (0, 0)))
    return pl.pallas_call(body, name=name, grid_spec=grid_spec, out_shape=_sds((R, C), F32),
                          compiler_params=_params("arbitrary"))(me, land, own)


def reduce_send(grads, name):
    n = len(grads)

    def body(*refs):
        ins, lands = refs[:n], refs[n:2 * n]
        send, recv = refs[2 * n], refs[2 * n + 1]
        mx, my, mc = _coords()
        me = 4 * mx + 2 * my + mc
        for a in range(n):
            for k, (px, py, pc) in enumerate(_peers(mx, my, mc)):
                _remote(ins[a].at[2 * px + py, pc], lands[a].at[me], send.at[7 * a + k], recv.at[7 * a + k], (px, py, pc)).start()

    lands = [lax.empty((N_DEV,) + g.shape[2:], g.dtype) for g in grads]
    outs = pl.pallas_call(
        body, name=name, in_specs=[HBM] * (2 * n), out_specs=[SEM, SEM] + [HBM] * (2 * n),
        out_shape=[pltpu.SemaphoreType.DMA((7 * n,))] * 2 + [pltpu.HBM(a.shape, a.dtype) for a in grads + lands],
        input_output_aliases={a: 2 + a for a in range(2 * n)}, compiler_params=_SPLIT_COPY,
    )(*[_in_hbm(a) for a in grads + lands])
    return (outs[0], outs[1]), list(outs[2:2 + n]), list(outs[2 + n:])


def reduce_wait(grads, lands, sems, after, name):
    n = len(grads)

    def body(*refs):
        ins, zones = refs[:n], refs[n:2 * n]
        send, recv = refs[2 * n], refs[2 * n + 1]
        mx, my, mc = _coords()
        for a in range(n):
            for k, (px, py, pc) in enumerate(_peers(mx, my, mc)):
                cp = _remote(ins[a].at[2 * px + py, pc], zones[a].at[4 * px + 2 * py + pc], send.at[7 * a + k],
                             recv.at[7 * a + k], (px, py, pc))
                cp.wait_send()
                cp.wait_recv()

    outs = pl.pallas_call(
        body, name=name, in_specs=[HBM] * (2 * n) + [SEM, SEM, ANY], out_specs=[HBM] * (2 * n),
        out_shape=[pltpu.HBM(a.shape, a.dtype) for a in grads + lands],
        input_output_aliases={a: a for a in range(2 * n)}, compiler_params=_SPLIT_COPY,
    )(*grads, *lands, sems[0], sems[1], after)
    return list(outs[:n]), list(outs[n:])


def reduce_sum(land, grad, place, name, into=None, layer=None):
    _, hR, C = land.shape
    tr = hR
    while N_DEV * tr * C * 2 > (6 << 20) and tr % 32 == 0:
        tr //= 2

    def body(s_ref, l_ref, g_ref, *rest):
        o_ref = rest[-1]
        own = g_ref[...].astype(F32)
        acc = None
        for q in range(N_DEV):
            term = jnp.where(s_ref[2] == q, own, l_ref[q].astype(F32))
            acc = term if acc is None else acc + term
        o_ref[...] = acc

    in_specs = [pl.BlockSpec((N_DEV, tr, C), lambda i, s: (0, i, 0)),
                pl.BlockSpec((None, None, tr, C), lambda i, s: (s[0], s[1], i, 0))]
    args = [place, land, grad]
    aliases = {}
    if layer is None:
        out_spec = pl.BlockSpec((None, tr, C), lambda i, s: (s[1], i, 0))
        out_shape = _sds((2, hR, C), F32)
    else:
        out_spec = pl.BlockSpec((None, None, tr, C), lambda i, s: (layer, s[1], i, 0))
        out_shape = _sds((2, 2, hR, C), F32)
        if into is not None:
            in_specs.append(ANY)
            args.append(into)
            aliases = {3: 0}
    grid_spec = pltpu.PrefetchScalarGridSpec(num_scalar_prefetch=1, grid=(hR // tr,), in_specs=in_specs, out_specs=out_spec)
    return pl.pallas_call(body, name=name, grid_spec=grid_spec, out_shape=out_shape, input_output_aliases=aliases,
                          compiler_params=_params("arbitrary"))(*args)


def join_halves(arrays, name):
    n = len(arrays)
    pieces = [(a, l) for a, arr in enumerate(arrays) for l in (range(arr.shape[0]) if arr.ndim == 4 else [None])]

    def body(*refs):
        ins = refs[:n]
        send, recv = refs[2 * n:]
        mx, my, mc = _coords()

        def half(a, l, h):
            return ins[a].at[h] if l is None else ins[a].at[l, h]

        sends = [_remote(half(a, l, mc), half(a, l, mc), send.at[i], recv.at[i], (mx, my, 1 - mc))
                 for i, (a, l) in enumerate(pieces)]
        for cp in sends:
            cp.start()
        for i, (a, l) in enumerate(pieces):
            theirs = half(a, l, 1 - mc)
            _remote(theirs, theirs, send.at[i], recv.at[i], (mx, my, 1 - mc)).wait_recv()
        for cp in sends:
            cp.wait_send()

    return pl.pallas_call(
        body, name=name, in_specs=[ANY] * n, out_specs=[ANY] * n, out_shape=[_sds(a.shape, a.dtype) for a in arrays],
        input_output_aliases={a: a for a in range(n)},
        scratch_shapes=[pltpu.SemaphoreType.DMA((len(pieces),)), pltpu.SemaphoreType.DMA((len(pieces),))],
    )(*arrays)


LANES = 128
SUBLANES = 8


def _n_rows(shape):
    rows = -(-int(np.prod(shape)) // LANES)
    return -(-rows // SUBLANES) * SUBLANES


def _as_rows(a):
    flat = a.reshape(-1)
    rows = _n_rows(a.shape)
    return jnp.pad(flat, (0, rows * LANES - flat.shape[0])).reshape(rows, LANES)


def _pack(arrays):
    return jnp.concatenate([_as_rows(a) for a in arrays], axis=0)


def _unpack(rows, shapes):
    out, r0 = [], 0
    for s in shapes:
        n = _n_rows(s)
        out.append(rows[r0:r0 + n].reshape(-1)[:int(np.prod(s))].reshape(s))
        r0 += n
    return out


REPLICATED_SMALL = [("rel_bias", (32, 16)), ("even_norm", (1, 1024)), ("even_pool_w", (1, 4, 128, 128)),
                    ("even_pool_scale", (1, 512)), ("odd_q_norm", (1, 64)), ("odd_k_norm", (1, 64)),
                    ("ffn_norm", (2, 1024)), ("ffn_conv_b", (2, 5632))]
SHARDED_SMALL = [("even_conv_w", (1, 3, 128)), ("odd_norm", (1, 256)), ("ffn_conv_w", (2, 3, 1408))]
BIG = ["even_w_in", "even_w_out", "odd_w_qkv", "odd_w_o", "ffn_w_up", "ffn_w_down"]
WEIGHT_ORDER = ["rel_bias", "even_norm", "even_w_in", "even_conv_w", "even_pool_w", "even_pool_scale", "even_w_out",
                "odd_norm", "odd_w_qkv", "odd_q_norm", "odd_k_norm", "odd_w_o", "ffn_norm", "ffn_w_up", "ffn_conv_w",
                "ffn_conv_b", "ffn_w_down"]


def kernel(x, rel_bias, even_norm, even_w_in, even_conv_w, even_pool_w, even_pool_scale, even_w_out, odd_norm, odd_w_qkv, odd_q_norm, odd_k_norm, odd_w_o, ffn_norm, ffn_w_up, ffn_conv_w, ffn_conv_b, ffn_w_down, loss_target, m_rel_bias, m_even_norm, m_even_w_in, m_even_conv_w, m_even_pool_w, m_even_pool_scale, m_even_w_out, m_odd_norm, m_odd_w_qkv, m_odd_q_norm, m_odd_k_norm, m_odd_w_o, m_ffn_norm, m_ffn_w_up, m_ffn_conv_w, m_ffn_conv_b, m_ffn_w_down, v_rel_bias, v_even_norm, v_even_w_in, v_even_conv_w, v_even_pool_w, v_even_pool_scale, v_even_w_out, v_odd_norm, v_odd_w_qkv, v_odd_q_norm, v_odd_k_norm, v_odd_w_o, v_ffn_norm, v_ffn_w_up, v_ffn_conv_w, v_ffn_conv_b, v_ffn_w_down):
    W = dict(rel_bias=rel_bias, even_norm=even_norm, even_w_in=even_w_in, even_conv_w=even_conv_w, even_pool_w=even_pool_w,
             even_pool_scale=even_pool_scale, even_w_out=even_w_out, odd_norm=odd_norm, odd_w_qkv=odd_w_qkv,
             odd_q_norm=odd_q_norm, odd_k_norm=odd_k_norm, odd_w_o=odd_w_o, ffn_norm=ffn_norm, ffn_w_up=ffn_w_up,
             ffn_conv_w=ffn_conv_w, ffn_conv_b=ffn_conv_b, ffn_w_down=ffn_w_down)
    M1 = dict(rel_bias=m_rel_bias, even_norm=m_even_norm, even_w_in=m_even_w_in, even_conv_w=m_even_conv_w,
              even_pool_w=m_even_pool_w, even_pool_scale=m_even_pool_scale, even_w_out=m_even_w_out, odd_norm=m_odd_norm,
              odd_w_qkv=m_odd_w_qkv, odd_q_norm=m_odd_q_norm, odd_k_norm=m_odd_k_norm, odd_w_o=m_odd_w_o,
              ffn_norm=m_ffn_norm, ffn_w_up=m_ffn_w_up, ffn_conv_w=m_ffn_conv_w, ffn_conv_b=m_ffn_conv_b,
              ffn_w_down=m_ffn_w_down)
    M2 = dict(rel_bias=v_rel_bias, even_norm=v_even_norm, even_w_in=v_even_w_in, even_conv_w=v_even_conv_w,
              even_pool_w=v_even_pool_w, even_pool_scale=v_even_pool_scale, even_w_out=v_even_w_out, odd_norm=v_odd_norm,
              odd_w_qkv=v_odd_w_qkv, odd_q_norm=v_odd_q_norm, odd_k_norm=v_odd_k_norm, odd_w_o=v_odd_w_o,
              ffn_norm=v_ffn_norm, ffn_w_up=v_ffn_w_up, ffn_conv_w=v_ffn_conv_w, ffn_conv_b=v_ffn_conv_b,
              ffn_w_down=v_ffn_w_down)
    mx, my, mc = _coords()
    chip = 2 * mx + my
    me = 4 * mx + 2 * my + mc
    place = jnp.stack([chip, mc, me]).astype(jnp.int32)
    xs, target = x[0], loss_target[0]

    def halves(w):
        return w.reshape((w.shape[0], 2, w.shape[-2] // 2, w.shape[-1]))

    lands = [cast_into_slot(halves(even_w_in), 0, place, "cast_w_in"), cast_into_slot(halves(even_w_out), 0, place, "cast_w_out"),
             cast_into_slot(halves(ffn_w_up), 0, place, "cast_w_up0"), cast_into_slot(halves(ffn_w_down), 0, place, "cast_w_down0"),
             cast_into_slot(halves(odd_w_qkv), 0, place, "cast_w_qkv"), cast_into_slot(halves(odd_w_o), 0, place, "cast_w_o"),
             cast_into_slot(halves(ffn_w_up), 1, place, "cast_w_up1"), cast_into_slot(halves(ffn_w_down), 1, place, "cast_w_down1")]
    small = allgather_devices(_pack([even_conv_w, odd_norm, ffn_conv_w]), "allgather_small_weights")
    lands[0], small = lax.optimization_barrier((lands[0], small))
    gather_sems, lands, token = gather_start(lands, [[0, 1], [2, 3], [4, 5], [6, 7]], "gather_start")
    even_norm_after_start = even_norm + token[0:1, 0:1]
    small = small[0::2]
    conv_w_full = small[:, 0:3].transpose(1, 0, 2).reshape(3, A_WIDTH)
    odd_norm_full = small[:, 8:10].reshape(1, D_MODEL)
    ffn_cw_full = small[:, 16:82].reshape(N_CHIPS, 2, 3, 2 * D_FF // N_CHIPS).transpose(1, 2, 0, 3).reshape(2, 3, 2 * D_FF)
    pool_w = cast_bf16(even_pool_w[0], "cast_pool_w")
    gqk = jnp.stack([jnp.tile(odd_q_norm[0], N_HEADS), jnp.tile(odd_k_norm[0], N_HEADS),
                     jnp.ones((D_MODEL,), F32)])[:, None, :]
    bias = bias_expand(rel_bias.T, "bias_expand").reshape(6, N_HEADS, ATT_BLOCK, 2 * ATT_BLOCK)

    def ffn_fwd(l, xin):
        xn = rmsnorm_fwd(xin, ffn_norm[l:l + 1], f"ffn{l}_norm")
        up = mm_nn(xn, w_up[l], f"ffn{l}_up")
        act = glu_fwd(up, ffn_cw_full[l], ffn_conv_b[l:l + 1], f"ffn{l}_glu")
        return mm_nn(act, w_down[l], f"ffn{l}_down", res=xin), (xin, xn, up, act)

    def gathered(group, tag, after_landing, after_passing):
        sems, arrays = gather_forward(lands[2 * group:2 * group + 2], gather_sems[group], after_landing, "gather_forward_" + tag)
        return gather_wait(arrays, sems, after_passing, "gather_wait_" + tag)

    def ffn_weights(got):
        return got[0].reshape(N_CHIPS, 1, D_MODEL, 2 * D_FF // N_CHIPS), got[1].reshape(1, 1, D_FF, D_MODEL)

    w_up, w_down = [None, None], [None, None]
    xn0 = rmsnorm_fwd(xs, even_norm_after_start, "even_norm")
    got = gathered(0, "even", bias, xn0)
    w_in = got[0].reshape(N_CHIPS, 1, D_MODEL, EVEN_IN // N_CHIPS)
    w_out = got[1].reshape(1, 1, D_MODEL, D_MODEL)
    proj = mm_nn(xn0, w_in, "even_in")
    mix = mixer_fwd(proj, conv_w_full, pool_w, even_pool_scale, "even_mixer")
    x1 = mm_nn(mix, w_out, "even_out", res=xs)
    w_up[0], w_down[0] = ffn_weights(gathered(1, "ffn0", proj, x1))
    x2, ffn0 = ffn_fwd(0, x1)
    got = gathered(2, "odd", x1, x2)
    w_qkv = got[0].reshape(N_CHIPS, 1, D_MODEL, 3 * D_MODEL // N_CHIPS)
    w_o = got[1].reshape(1, 1, D_MODEL, D_MODEL)
    xn2 = rmsnorm_fwd(x2, odd_norm_full, "odd_norm")
    qkv = mm_nn(xn2, w_qkv, "odd_qkv")
    qkvn = qknorm_fwd(qkv, gqk, "odd_qknorm")
    att, lse = attn_fwd(qkvn, bias, "attn_fwd")
    x3 = mm_nn(att, w_o, "odd_out", res=x2)
    w_up[1], w_down[1] = ffn_weights(gathered(3, "ffn1", x2, x3))
    x4, ffn1 = ffn_fwd(1, x3)

    dy, dyb, sq = loss_grad(x4, target, "loss")
    loss = lax.psum(0.5 * jnp.sum(sq) * (1.0 / D_MODEL), ("x", "y", "c"))

    def ffn_bwd(l, dy, dyb, saved):
        xin, xn, up, act = saved
        dw_down = mm_tn(act, dyb, f"ffn{l}_dw_down", J=1, tk=D_FF // 2)
        dact = mm_nt(dyb, w_down[l], f"ffn{l}_dact", tr=D_FF // 2)
        dup, dcw, dcb = glu_bwd(up, dact, ffn_cw_full[l], ffn_conv_b[l:l + 1], f"ffn{l}_glu_bwd")
        dw_up = mm_tn(xn, dup, f"ffn{l}_dw_up", J=N_CHIPS, tk=512)
        dxn = mm_nt(dup, w_up[l], f"ffn{l}_dxn", tr=D_MODEL)
        dx, dxb, dg = rmsnorm_bwd(xin, ffn_norm[l:l + 1], dxn, dy, f"ffn{l}_norm_bwd")
        return dx, dxb, (dw_down, dw_up, dcw, dcb, dg)

    def quarters(g):
        return g.reshape(N_CHIPS, 2, g.shape[0] * g.shape[1] // (2 * N_CHIPS), g.shape[-1])

    def reduce_start(grads, tag, then):
        sems, parts, zones = reduce_send([quarters(g) for g in grads], "reduce_send_" + tag)
        then, parts = lax.optimization_barrier((then, parts))
        return (sems, parts, zones), then

    dx3, dx3b, g_ffn1 = ffn_bwd(1, dy, dyb, ffn1)
    red_ffn1, (dx3, dx3b) = reduce_start([g_ffn1[1], g_ffn1[0]], "ffn1", (dx3, dx3b))
    dw_o = mm_tn(att, dx3b, "odd_dw_o", J=1, tk=512)
    datt = mm_nt(dx3b, w_o, "odd_datt", tr=D_MODEL, out_dtype=BF16)
    dq, dk, dv, dbias = attn_bwd(qkvn, att, datt, lse, bias, "attn_bwd")
    dqkv, dgqk = qknorm_bwd(qkv, dq, dk, dv, gqk, "odd_qknorm_bwd")
    dw_qkv = mm_tn(xn2, dqkv, "odd_dw_qkv", J=N_CHIPS, tk=512)
    dxn2 = mm_nt(dqkv, w_qkv, "odd_dxn", tr=D_MODEL)
    red_odd, dxn2 = reduce_start([dw_qkv, dw_o], "odd", dxn2)
    dx2, dx2b, dg_odd = rmsnorm_bwd(x2, odd_norm_full, dxn2, dx3, "odd_norm_bwd")
    dx1, dx1b, g_ffn0 = ffn_bwd(0, dx2, dx2b, ffn0)
    red_ffn0, (dx1, dx1b) = reduce_start([g_ffn0[1], g_ffn0[0]], "ffn0", (dx1, dx1b))
    dw_out = mm_tn(mix, dx1b, "even_dw_out", J=1, tk=512)
    dmix = mm_nt(dx1b, w_out, "even_dmix", tr=D_MODEL)
    dproj, dcw_even, dpw, dps = mixer_bwd(proj, dmix, conv_w_full, pool_w, even_pool_scale, "even_mixer_bwd")
    dw_in = mm_tn(xn0, dproj, "even_dw_in", J=N_CHIPS, tk=512)
    dxn0 = mm_nt(dproj, w_in, "even_dxn", tr=D_MODEL)
    grad_x, _, dg_even = rmsnorm_bwd(xs, even_norm, dxn0, dx1, "even_norm_bwd")
    d_rel = jnp.sum(bias_reduce(dbias.reshape(3, N_HEADS, 2 * ATT_BLOCK * ATT_BLOCK), "bias_reduce"), axis=0).T

    red_even, grad_x = reduce_start([dw_in, dw_out], "even", grad_x)

    dcw_sh = dcw_even.reshape(3, N_CHIPS, A_WIDTH // N_CHIPS).transpose(1, 0, 2)
    don_sh = dg_odd.reshape(N_CHIPS, D_MODEL // N_CHIPS)
    dfcw = jnp.stack([g_ffn0[2], g_ffn1[2]])
    dfcw_sh = dfcw.reshape(2, 3, N_CHIPS, 2 * D_FF // N_CHIPS).transpose(2, 0, 1, 3)
    rep_grads = [d_rel, dg_even, dpw[None], dps, _head_sum(dgqk[0]), _head_sum(dgqk[1]),
                 jnp.concatenate([g_ffn0[4], g_ffn1[4]], axis=0), jnp.concatenate([g_ffn0[3], g_ffn1[3]], axis=0)]
    rep_rows = _pack(rep_grads)
    shard_rows = jnp.concatenate([_pack([dcw_sh[j], don_sh[j], dfcw_sh[j]]) for j in range(N_CHIPS)], axis=0)
    n_rep, n_shard = rep_rows.shape[0], shard_rows.shape[0] // N_CHIPS
    small_sems, small_rows, small_land = devices_start(jnp.concatenate([rep_rows, shard_rows], axis=0), "small_grads_start")
    grad_x, small_rows = lax.optimization_barrier((grad_x, small_rows))

    def reduce_end(red, tag, after):
        sems, parts, zones = red
        parts, zones = reduce_wait(parts, zones, sems, after, "reduce_wait_" + tag)
        return zones, parts

    z_ffn1, p_ffn1 = reduce_end(red_ffn1, "ffn1", grad_x)
    z_odd, p_odd = reduce_end(red_odd, "odd", grad_x)
    z_ffn0, p_ffn0 = reduce_end(red_ffn0, "ffn0", grad_x)
    r_up = reduce_sum(z_ffn0[0], p_ffn0[0], place, "reduce_sum_w_up0", layer=0)
    r_up = reduce_sum(z_ffn1[0], p_ffn1[0], place, "reduce_sum_w_up1", into=r_up, layer=1)
    r_down = reduce_sum(z_ffn0[1], p_ffn0[1], place, "reduce_sum_w_down0", layer=0)
    r_down = reduce_sum(z_ffn1[1], p_ffn1[1], place, "reduce_sum_w_down1", into=r_down, layer=1)
    z_even, p_even = reduce_end(red_even, "even", r_down)
    halves_written = [reduce_sum(z_even[0], p_even[0], place, "reduce_sum_w_in"),
                      reduce_sum(z_even[1], p_even[1], place, "reduce_sum_w_out"),
                      reduce_sum(z_odd[0], p_odd[0], place, "reduce_sum_w_qkv"),
                      reduce_sum(z_odd[1], p_odd[1], place, "reduce_sum_w_o"), r_up, r_down]
    joined = join_halves(halves_written, "grads_join_halves")
    G = {nm: g.reshape(W[nm].shape) for nm, g in zip(BIG, joined)}

    D_, NM, NV = {}, {}, {}
    for nm in BIG:
        as3 = lambda a: a.reshape((-1,) + a.shape[-2:])
        outs = adamw(as3(W[nm]), as3(G[nm]), as3(M1[nm]), as3(M2[nm]), "adamw_" + nm)
        D_[nm], NM[nm], NV[nm] = [o.reshape(W[nm].shape) for o in outs]
    small_rows, small_land = devices_wait(small_rows, small_land, small_sems, D_[BIG[-1]], "small_grads_wait")
    small_sum = device_sum(small_land, small_rows, place[2:3], "small_grads_sum")
    mine = lax.dynamic_slice_in_dim(small_sum, n_rep + chip * n_shard, n_shard, axis=0)
    g_small = jnp.concatenate([small_sum[:n_rep], mine], axis=0)
    small_names = [n for n, _ in REPLICATED_SMALL + SHARDED_SMALL]
    small_shapes = [s for _, s in REPLICATED_SMALL + SHARDED_SMALL]
    G.update(dict(zip(small_names, _unpack(g_small, small_shapes))))
    packs = [_pack([d[n] for n in small_names])[None] for d in (W, M1, M2)]
    outs = adamw(packs[0], g_small[None], packs[1], packs[2], "adamw_small")
    for dst, o in zip((D_, NM, NV), outs):
        dst.update(dict(zip(small_names, _unpack(o[0], small_shapes))))

    return (loss, grad_x[None], *[G[n] for n in WEIGHT_ORDER], *[D_[n] for n in WEIGHT_ORDER],
            *[NM[n] for n in WEIGHT_ORDER], *[NV[n] for n in WEIGHT_ORDER])


def _head_sum(dg):
    return jnp.sum(dg.reshape(N_HEADS, HEAD_DIM), axis=0, keepdims=True)
```

```python
import functools
import math

import numpy as np
import jax
import jax.numpy as jnp
from jax import lax
from jax.experimental import pallas as pl
from jax.experimental.pallas import tpu as pltpu

F32 = jnp.float32
BF16 = jnp.bfloat16

D_MODEL = 1024
N_HEADS = 16
HEAD_DIM = 64
A_WIDTH = 512
POOL_WINDOWS = (2, 4, 8, 16)
POOL_GROUP = 128
EVEN_IN = 2048
D_FF = 2816
DILATED_PAIRS = ((128, 1), (512, 4), (2048, 16))
ATT_BLOCK = 128
N_REL_BUCKETS = 32
REL_MAX_DISTANCE = 2048
EPS = 1e-6
MASK_VALUE = -1e30
ADAM_LR, ADAM_B1, ADAM_B2, ADAM_EPS, ADAM_WD, ADAM_STEP = 0.001, 0.9, 0.999, 1e-08, 0.01, 10

VMEM_LIMIT_BYTES = 48 * 1024 * 1024
N_CHIPS = 4
N_DEV = 8
MESH = pl.DeviceIdType.MESH


def _params(*sem):
    return pltpu.CompilerParams(dimension_semantics=sem if sem else None, vmem_limit_bytes=VMEM_LIMIT_BYTES)


def _sds(shape, dtype):
    return jax.ShapeDtypeStruct(tuple(shape), dtype)


def cast_bf16(x, name, tr=None):
    lead, (R, C) = x.shape[:-2], x.shape[-2:]
    n = int(np.prod(lead)) if lead else 1
    x3 = x.reshape((n, R, C))
    tr = tr or R

    def body(x_ref, o_ref):
        o_ref[...] = x_ref[...].astype(BF16)

    out = pl.pallas_call(
        body, name=name, grid=(n, R // tr),
        in_specs=[pl.BlockSpec((None, tr, C), lambda i, r: (i, r, 0))],
        out_specs=pl.BlockSpec((None, tr, C), lambda i, r: (i, r, 0)),
        out_shape=_sds((n, R, C), BF16), compiler_params=_params("parallel", "parallel"),
    )(x3)
    return out.reshape(lead + (R, C))


def rmsnorm_fwd(x, g, name, ts=512):
    S, Dm = x.shape

    def body(x_ref, g_ref, o_ref):
        xv = x_ref[...]
        r = lax.rsqrt(jnp.mean(xv * xv, axis=-1, keepdims=True) + EPS)
        o_ref[...] = ((xv * r) * g_ref[...]).astype(BF16)

    return pl.pallas_call(
        body, name=name, grid=(S // ts,),
        in_specs=[pl.BlockSpec((ts, Dm), lambda i: (i, 0)), pl.BlockSpec((1, Dm), lambda i: (0, 0))],
        out_specs=pl.BlockSpec((ts, Dm), lambda i: (i, 0)),
        out_shape=_sds((S, Dm), BF16), compiler_params=_params("parallel"),
    )(x, g)


def rmsnorm_bwd(x, g, dxn, dres, name, ts=512):
    S, Dm = x.shape

    def body(x_ref, g_ref, d_ref, r_ref, dx_ref, dxb_ref, dg_ref):
        xv = x_ref[...]
        dv = d_ref[...].astype(F32)
        r = lax.rsqrt(jnp.mean(xv * xv, axis=-1, keepdims=True) + EPS)
        gx = dv * g_ref[...]
        dot = jnp.sum(gx * xv, axis=-1, keepdims=True)
        dx = r_ref[...] + r * gx - xv * ((r * r * r) * (dot * (1.0 / Dm)))
        dx_ref[...] = dx
        dxb_ref[...] = dx.astype(BF16)
        part = jnp.sum(dv * (xv * r), axis=0, keepdims=True)

        @pl.when(pl.program_id(0) == 0)
        def _():
            dg_ref[...] = part

        @pl.when(pl.program_id(0) > 0)
        def _():
            dg_ref[...] += part

    row = pl.BlockSpec((ts, Dm), lambda i: (i, 0))
    vec = pl.BlockSpec((1, Dm), lambda i: (0, 0))
    return pl.pallas_call(
        body, name=name, grid=(S // ts,),
        in_specs=[row, vec, row, row], out_specs=[row, row, vec],
        out_shape=[_sds((S, Dm), F32), _sds((S, Dm), BF16), _sds((1, Dm), F32)], compiler_params=_params("arbitrary"),
    )(x, g, dxn, dres)


def mm_nn(a, w, name, layer=0, res=None, out_dtype=F32, tm=512):
    M, K = a.shape
    J, _, _, Ns = w.shape

    def body(*refs):
        a_ref, w_ref = refs[0], refs[1]
        o_ref = refs[-1]
        acc = jnp.dot(a_ref[...], w_ref[...], preferred_element_type=F32)
        if res is not None:
            acc = refs[2][...] + acc
        o_ref[...] = acc.astype(o_ref.dtype)

    in_specs = [pl.BlockSpec((tm, K), lambda j, m: (m, 0)),
                pl.BlockSpec((None, None, K, Ns), lambda j, m: (j, layer, 0, 0))]
    args = [a, w]
    if res is not None:
        in_specs.append(pl.BlockSpec((tm, Ns), lambda j, m: (m, j)))
        args.append(res)
    return pl.pallas_call(
        body, name=name, grid=(J, M // tm), in_specs=in_specs,
        out_specs=pl.BlockSpec((tm, Ns), lambda j, m: (m, j)),
        out_shape=_sds((M, J * Ns), out_dtype), compiler_params=_params("parallel", "parallel"),
    )(*args)


def mm_nt(dy, w, name, tr, layer=0, out_dtype=F32, tm=512):
    M = dy.shape[0]
    J, _, R, Ns = w.shape
    dims = (((1,), (1,)), ((), ()))

    def body(dy_ref, w_ref, o_ref, *scratch):
        p = lax.dot_general(dy_ref[...], w_ref[...], dims, preferred_element_type=F32)
        if J == 1:
            o_ref[...] = p.astype(o_ref.dtype)
            return
        acc_ref, = scratch
        j = pl.program_id(2)

        @pl.when(j == 0)
        def _():
            acc_ref[...] = p

        @pl.when(j > 0)
        def _():
            acc_ref[...] += p

        @pl.when(j == J - 1)
        def _():
            o_ref[...] = acc_ref[...].astype(o_ref.dtype)

    return pl.pallas_call(
        body, name=name, grid=(R // tr, M // tm, J),
        in_specs=[pl.BlockSpec((tm, Ns), lambda r, m, j: (m, j)),
                  pl.BlockSpec((None, None, tr, Ns), lambda r, m, j: (j, layer, r, 0))],
        out_specs=pl.BlockSpec((tm, tr), lambda r, m, j: (m, r)),
        out_shape=_sds((M, R), out_dtype),
        scratch_shapes=[] if J == 1 else [pltpu.VMEM((tm, tr), F32)],
        compiler_params=_params("parallel", "parallel", "arbitrary"),
    )(dy, w)


def mm_tn(a, dy, name, J, tk, tm=512):
    M, K = a.shape
    N = dy.shape[1]
    Ns = N // J
    n_m = M // tm
    dims = (((0,), (0,)), ((), ()))

    def body(a_ref, dy_ref, o_ref, acc_ref):
        p = lax.dot_general(a_ref[...], dy_ref[...], dims, preferred_element_type=F32)
        m = pl.program_id(1)

        @pl.when(m == 0)
        def _():
            acc_ref[...] = p

        @pl.when(m > 0)
        def _():
            acc_ref[...] += p

        @pl.when(m == n_m - 1)
        def _():
            for j in range(J):
                o_ref[j] = acc_ref[:, j * Ns:(j + 1) * Ns].astype(BF16)

    return pl.pallas_call(
        body, name=name, grid=(K // tk, n_m),
        in_specs=[pl.BlockSpec((tm, tk), lambda k, m: (m, k)), pl.BlockSpec((tm, N), lambda k, m: (m, 0))],
        out_specs=pl.BlockSpec((J, tk, Ns), lambda k, m: (0, k, 0)),
        out_shape=_sds((J, K, Ns), BF16), scratch_shapes=[pltpu.VMEM((tk, N), F32)],
        compiler_params=_params("parallel", "arbitrary"),
    )(a, dy)


HALO = 16


def _shift_down(x, s):
    return pltpu.roll(x, s, 0)


def _shift_up(x, s):
    return pltpu.roll(x, x.shape[0] - s, 0)


def _conv3(z, cw):
    return (_shift_down(z, 2) * cw[0:1] + _shift_down(z, 1) * cw[1:2]) + z * cw[2:3]


def _window_count(first_row, n, k):
    t = first_row + lax.broadcasted_iota(jnp.int32, (n, 1), 0)
    return jnp.clip(t + 1, 1, k).astype(F32)


def mixer_fwd(proj, conv_w, pool_w, pool_scale, name, ts=256):
    S = proj.shape[0]
    n = ts + HALO

    def body(pm_ref, pb_ref, cw_ref, pw_ref, ps_ref, o_ref):
        i = pl.program_id(0)
        before = jnp.where(i > 0, pb_ref[...], 0.0)
        ext = jnp.concatenate([before, pm_ref[...]], axis=0)
        cw = cw_ref[...]
        z = ext[:, 2 * A_WIDTH:3 * A_WIDTH] * ext[:, 0:A_WIDTH]
        cz = _conv3(z, cw)
        ya = pm_ref[:, A_WIDTH:2 * A_WIDTH] * cz[HALO:]
        o_ref[:, 0:A_WIDTH] = ya.astype(BF16)
        for g, k in enumerate(POOL_WINDOWS):
            lo = 3 * A_WIDTH + g * POOL_GROUP
            p = ext[:, lo:lo + POOL_GROUP]
            w = p
            s = 1
            while s < k:
                w = w + _shift_down(w, s)
                s *= 2
            pooled = w / _window_count(i * ts - HALO, n, k) - p
            yb = jnp.dot(pooled[HALO:].astype(BF16), pw_ref[g], preferred_element_type=F32)
            yb = yb * ps_ref[:, g * POOL_GROUP:(g + 1) * POOL_GROUP]
            o_ref[:, A_WIDTH + g * POOL_GROUP:A_WIDTH + (g + 1) * POOL_GROUP] = yb.astype(BF16)

    hb = ts // HALO
    return pl.pallas_call(
        body, name=name, grid=(S // ts,),
        in_specs=[
            pl.BlockSpec((ts, EVEN_IN), lambda i: (i, 0)),
            pl.BlockSpec((HALO, EVEN_IN), lambda i: (jnp.maximum(i * hb - 1, 0), 0)),
            pl.BlockSpec((3, A_WIDTH), lambda i: (0, 0)),
            pl.BlockSpec((4, POOL_GROUP, POOL_GROUP), lambda i: (0, 0, 0)),
            pl.BlockSpec((1, 4 * POOL_GROUP), lambda i: (0, 0)),
        ],
        out_specs=pl.BlockSpec((ts, D_MODEL), lambda i: (i, 0)),
        out_shape=_sds((S, D_MODEL), BF16), compiler_params=_params("parallel"),
    )(proj, proj, conv_w, pool_w, pool_scale)


def mixer_bwd(proj, dmix, conv_w, pool_w, pool_scale, name, ts=256):
    S = proj.shape[0]
    n = ts + 2 * HALO
    nt = S // ts
    tn_dims = (((0,), (0,)), ((), ()))
    nt_dims = (((1,), (1,)), ((), ()))

    def body(pm_ref, pb_ref, pa_ref, dm_ref, da_ref, cw_ref, pw_ref, ps_ref, o_ref, dcw_ref, dpw_ref, dps_ref):
        i = pl.program_id(0)
        last = i == nt - 1
        before = jnp.where(i > 0, pb_ref[...], 0.0)
        after = jnp.where(last, 0.0, pa_ref[...])
        ext = jnp.concatenate([before, pm_ref[...], after], axis=0)
        dafter = jnp.where(last, 0.0, da_ref[...])
        dext = jnp.concatenate([jnp.zeros((HALO, D_MODEL), F32), dm_ref[...], dafter], axis=0)
        cw = cw_ref[...]
        main = slice(HALO, HALO + ts)

        @pl.when(i == 0)
        def _():
            dcw_ref[...] = jnp.zeros_like(dcw_ref)
            dpw_ref[...] = jnp.zeros_like(dpw_ref)
            dps_ref[...] = jnp.zeros_like(dps_ref)

        h, gb, gc = ext[:, 0:A_WIDTH], ext[:, A_WIDTH:2 * A_WIDTH], ext[:, 2 * A_WIDTH:3 * A_WIDTH]
        z = gc * h
        z1, z2 = _shift_down(z, 1), _shift_down(z, 2)
        cz = (z2 * cw[0:1] + z1 * cw[1:2]) + z * cw[2:3]
        dya = dext[:, 0:A_WIDTH]
        dcz = dya * gb
        dz = dcz * cw[2:3] + _shift_up(dcz, 1) * cw[1:2] + _shift_up(dcz, 2) * cw[0:1]
        o_ref[:, 0:A_WIDTH] = (dz * gc)[main].astype(BF16)
        o_ref[:, A_WIDTH:2 * A_WIDTH] = (dya * cz)[main].astype(BF16)
        o_ref[:, 2 * A_WIDTH:3 * A_WIDTH] = (dz * h)[main].astype(BF16)
        dczm = dcz[main]
        dcw_ref[0:1, :] += jnp.sum(dczm * z2[main], axis=0, keepdims=True)
        dcw_ref[1:2, :] += jnp.sum(dczm * z1[main], axis=0, keepdims=True)
        dcw_ref[2:3, :] += jnp.sum(dczm * z[main], axis=0, keepdims=True)

        for g, k in enumerate(POOL_WINDOWS):
            lo = 3 * A_WIDTH + g * POOL_GROUP
            cols = slice(g * POOL_GROUP, (g + 1) * POOL_GROUP)
            p = ext[:, lo:lo + POOL_GROUP]
            w = p
            s = 1
            while s < k:
                w = w + _shift_down(w, s)
                s *= 2
            cnt = _window_count(i * ts - HALO, n, k)
            pooled = (w / cnt - p)[main].astype(BF16)
            dyb = dext[:, A_WIDTH + g * POOL_GROUP:A_WIDTH + (g + 1) * POOL_GROUP]
            e = dyb * ps_ref[:, cols]
            pre = jnp.dot(pooled, pw_ref[g], preferred_element_type=F32)
            dps_ref[:, cols] += jnp.sum(dyb[main] * pre, axis=0, keepdims=True)
            dpw_ref[g] += lax.dot_general(pooled, e[main].astype(BF16), tn_dims, preferred_element_type=F32)
            dpooled = lax.dot_general(e.astype(BF16), pw_ref[g], nt_dims, preferred_element_type=F32)
            q = dpooled / cnt
            a = q
            s = 1
            while s < k:
                a = a + _shift_up(a, s)
                s *= 2
            o_ref[:, lo:lo + POOL_GROUP] = (a - dpooled)[main].astype(BF16)

    hb = ts // HALO
    nh = S // HALO
    before_map = lambda i: (jnp.maximum(i * hb - 1, 0), 0)
    after_map = lambda i: (jnp.minimum((i + 1) * hb, nh - 1), 0)
    full = lambda *shape: pl.BlockSpec(shape, lambda i: (0,) * len(shape))
    return pl.pallas_call(
        body, name=name, grid=(nt,),
        in_specs=[
            pl.BlockSpec((ts, EVEN_IN), lambda i: (i, 0)),
            pl.BlockSpec((HALO, EVEN_IN), before_map),
            pl.BlockSpec((HALO, EVEN_IN), after_map),
            pl.BlockSpec((ts, D_MODEL), lambda i: (i, 0)),
            pl.BlockSpec((HALO, D_MODEL), after_map),
            full(3, A_WIDTH), full(4, POOL_GROUP, POOL_GROUP), full(1, 4 * POOL_GROUP),
        ],
        out_specs=[pl.BlockSpec((ts, EVEN_IN), lambda i: (i, 0)), full(3, A_WIDTH), full(4, POOL_GROUP, POOL_GROUP),
                   full(1, 4 * POOL_GROUP)],
        out_shape=[_sds((S, EVEN_IN), BF16), _sds((3, A_WIDTH), F32), _sds((4, POOL_GROUP, POOL_GROUP), F32),
                   _sds((1, 4 * POOL_GROUP), F32)],
        compiler_params=_params("arbitrary"),
    )(proj, proj, proj, dmix, dmix, conv_w, pool_w, pool_scale)


FFN_HALO = 8
FFN_TC = 1408


def glu_fwd(up, conv_w, conv_b, name, ts=256):
    S = up.shape[0]
    nc = D_FF // FFN_TC

    def body(gm_ref, gb_ref, um_ref, ub_ref, cwg_ref, cwu_ref, cbg_ref, cbu_ref, o_ref):
        i = pl.program_id(0)

        def conv(m_ref, b_ref, cw_ref, cb_ref):
            before = jnp.where(i > 0, b_ref[...], 0.0)
            ext = jnp.concatenate([before, m_ref[...]], axis=0)
            return _conv3(ext, cw_ref[...])[FFN_HALO:] + cb_ref[...]

        gate = conv(gm_ref, gb_ref, cwg_ref, cbg_ref)
        upv = conv(um_ref, ub_ref, cwu_ref, cbu_ref)
        o_ref[...] = ((gate * (1.0 / (1.0 + jnp.exp(-gate)))) * upv).astype(BF16)

    hb = ts // FFN_HALO
    main = lambda off: pl.BlockSpec((ts, FFN_TC), lambda i, c: (i, c + off))
    halo = lambda off: pl.BlockSpec((FFN_HALO, FFN_TC), lambda i, c: (jnp.maximum(i * hb - 1, 0), c + off))
    cw = lambda off: pl.BlockSpec((3, FFN_TC), lambda i, c: (0, c + off))
    cb = lambda off: pl.BlockSpec((1, FFN_TC), lambda i, c: (0, c + off))
    return pl.pallas_call(
        body, name=name, grid=(S // ts, nc),
        in_specs=[main(0), halo(0), main(nc), halo(nc), cw(0), cw(nc), cb(0), cb(nc)],
        out_specs=pl.BlockSpec((ts, FFN_TC), lambda i, c: (i, c)),
        out_shape=_sds((S, D_FF), BF16), compiler_params=_params("parallel", "parallel"),
    )(up, up, up, up, conv_w, conv_w, conv_b, conv_b)


def glu_bwd(up, da, conv_w, conv_b, name, ts=256):
    S = up.shape[0]
    nc = D_FF // FFN_TC
    nt = S // ts
    main = slice(FFN_HALO, FFN_HALO + ts)
    W = 2 * D_FF

    def body(xm_ref, xb_ref, xa_ref, dm_ref, da_ref, cw_ref, cb_ref, dx_ref, dcw_ref, dcb_ref):
        i = pl.program_id(0)
        last = i == nt - 1

        @pl.when(i == 0)
        def _():
            dcw_ref[...] = jnp.zeros_like(dcw_ref)
            dcb_ref[...] = jnp.zeros_like(dcb_ref)

        def ext_of(cols):
            before = jnp.where(i > 0, xb_ref[:, cols], 0.0)
            return jnp.concatenate([before, xm_ref[:, cols], xa_ref[:, cols]], axis=0)

        def back(x, d, cols):
            cw = cw_ref[:, cols]
            dx = d * cw[2:3] + _shift_up(d, 1) * cw[1:2] + _shift_up(d, 2) * cw[0:1]
            dx_ref[:, cols] = dx[main].astype(BF16)
            dmn = d[main]
            dcb_ref[:, cols] += jnp.sum(dmn, axis=0, keepdims=True)
            dcw_ref[0:1, cols] += jnp.sum(dmn * _shift_down(x, 2)[main], axis=0, keepdims=True)
            dcw_ref[1:2, cols] += jnp.sum(dmn * _shift_down(x, 1)[main], axis=0, keepdims=True)
            dcw_ref[2:3, cols] += jnp.sum(dmn * x[main], axis=0, keepdims=True)

        for c in range(nc):
            gcols = slice(c * FFN_TC, (c + 1) * FFN_TC)
            ucols = slice(D_FF + c * FFN_TC, D_FF + (c + 1) * FFN_TC)
            xg, xu = ext_of(gcols), ext_of(ucols)
            ug = _conv3(xg, cw_ref[:, gcols]) + cb_ref[:, gcols]
            uu = _conv3(xu, cw_ref[:, ucols]) + cb_ref[:, ucols]
            dafter = jnp.where(last, 0.0, da_ref[:, gcols].astype(F32))
            dae = jnp.concatenate([jnp.zeros((FFN_HALO, FFN_TC), F32), dm_ref[:, gcols].astype(F32), dafter], axis=0)
            sg = 1.0 / (1.0 + jnp.exp(-ug))
            duu = dae * (ug * sg)
            dug = (dae * uu) * (sg * (1.0 + ug * (1.0 - sg)))
            back(xg, dug, gcols)
            back(xu, duu, ucols)

    hb = ts // FFN_HALO
    nh = S // FFN_HALO
    before_map = lambda i: (jnp.maximum(i * hb - 1, 0), 0)
    after_map = lambda i: (jnp.minimum((i + 1) * hb, nh - 1), 0)
    return pl.pallas_call(
        body, name=name, grid=(nt,),
        in_specs=[pl.BlockSpec((ts, W), lambda i: (i, 0)), pl.BlockSpec((FFN_HALO, W), before_map),
                  pl.BlockSpec((FFN_HALO, W), after_map), pl.BlockSpec((ts, D_FF), lambda i: (i, 0)),
                  pl.BlockSpec((FFN_HALO, D_FF), after_map), pl.BlockSpec((3, W), lambda i: (0, 0)),
                  pl.BlockSpec((1, W), lambda i: (0, 0))],
        out_specs=[pl.BlockSpec((ts, W), lambda i: (i, 0)), pl.BlockSpec((3, W), lambda i: (0, 0)),
                   pl.BlockSpec((1, W), lambda i: (0, 0))],
        out_shape=[_sds((S, W), BF16), _sds((3, W), F32), _sds((1, W), F32)],
        compiler_params=_params("arbitrary"),
    )(up, up, up, da, da, conv_w, conv_b)


def _head_mean_matrix():
    h = np.arange(D_MODEL) // HEAD_DIM
    return jnp.asarray((h[:, None] == h[None, :]).astype(np.float32) / HEAD_DIM, dtype=BF16)


def _head_mean(v, gm):
    return jnp.dot(v.astype(BF16), gm, preferred_element_type=F32)


def qknorm_fwd(qkv, gqk, name, ts=512):
    S = qkv.shape[0]

    def body(x_ref, g_ref, gm_ref, o_ref):
        part = pl.program_id(0)
        x = x_ref[...]

        @pl.when(part < 2)
        def _():
            r = lax.rsqrt(_head_mean(x * x, gm_ref[...]) + EPS)
            o_ref[...] = ((x * r) * g_ref[...]).astype(BF16)

        @pl.when(part == 2)
        def _():
            o_ref[...] = x.astype(BF16)

    return pl.pallas_call(
        body, name=name, grid=(3, S // ts),
        in_specs=[pl.BlockSpec((ts, D_MODEL), lambda p, i: (i, p)), pl.BlockSpec((None, 1, D_MODEL), lambda p, i: (p, 0, 0)),
                  pl.BlockSpec((D_MODEL, D_MODEL), lambda p, i: (0, 0))],
        out_specs=pl.BlockSpec((ts, D_MODEL), lambda p, i: (i, p)),
        out_shape=_sds((S, 3 * D_MODEL), BF16), compiler_params=_params("parallel", "parallel"),
    )(qkv, gqk, _head_mean_matrix())


def qknorm_bwd(qkv, dq, dk, dv, gqk, name, ts=256):
    S = qkv.shape[0]

    def body(x_ref, dq_ref, dk_ref, dv_ref, g_ref, gm_ref, o_ref, dg_ref):
        @pl.when(pl.program_id(0) == 0)
        def _():
            dg_ref[...] = jnp.zeros_like(dg_ref)

        gm = gm_ref[...]
        for part, d_ref in enumerate((dq_ref, dk_ref)):
            cols = slice(part * D_MODEL, (part + 1) * D_MODEL)
            x = x_ref[:, cols]
            d = d_ref[...]
            r = lax.rsqrt(_head_mean(x * x, gm) + EPS)
            gx = d * g_ref[part]
            o_ref[:, cols] = (r * gx - x * ((r * r * r) * _head_mean(gx * x, gm))).astype(BF16)
            dg_ref[part] += jnp.sum(d * (x * r), axis=0, keepdims=True)
        o_ref[:, 2 * D_MODEL:] = dv_ref[...].astype(BF16)

    row = pl.BlockSpec((ts, D_MODEL), lambda i: (i, 0))
    wide = pl.BlockSpec((ts, 3 * D_MODEL), lambda i: (i, 0))
    gains = pl.BlockSpec((3, 1, D_MODEL), lambda i: (0, 0, 0))
    return pl.pallas_call(
        body, name=name, grid=(S // ts,),
        in_specs=[wide, row, row, row, gains, pl.BlockSpec((D_MODEL, D_MODEL), lambda i: (0, 0))],
        out_specs=[wide, gains],
        out_shape=[_sds((S, 3 * D_MODEL), BF16), _sds((3, 1, D_MODEL), F32)],
        compiler_params=_params("arbitrary"),
    )(qkv, dq, dk, dv, gqk, _head_mean_matrix())


RESIDUES = 16


def _block_order(dil):
    runs = RESIDUES // dil
    slot = np.arange(ATT_BLOCK)
    return (slot % (ATT_BLOCK // runs)) * runs + slot // (ATT_BLOCK // runs)


def _bucket_tables():
    n = ATT_BLOCK
    max_exact = N_REL_BUCKETS // 2
    buckets, valids = [], []
    for _, dil in DILATED_PAIRS:
        order = _block_order(dil)
        a = order[:, None]
        c = np.concatenate([order, n + order])[None, :]
        first_half = (np.arange(2 * n) < n)[None, :]
        rel = a + n - c
        band = (rel >= 0) & (rel <= n)
        dist = np.clip(rel, 0, n) * dil
        dd = np.maximum(dist, 1).astype(np.float32)
        large = max_exact + (np.log(dd / np.float32(max_exact)) / np.float32(math.log(REL_MAX_DISTANCE / max_exact))
                             * np.float32(N_REL_BUCKETS - max_exact)).astype(np.int32)
        large = np.minimum(large, N_REL_BUCKETS - 1)
        buckets.append(np.where(dist < max_exact, dist, large).reshape(1, -1))
        valids.append(np.stack([(band & ~first_half).reshape(1, -1), band.reshape(1, -1)]))
    return np.stack(buckets).astype(np.int32), np.stack(valids).astype(np.int32)


BIAS_CHUNK = 8192


def _split3(x):
    a = x.astype(BF16)
    r = x - a.astype(F32)
    b = r.astype(BF16)
    c = (r - b.astype(F32)).astype(BF16)
    return a, b, c


def bias_expand(rel_bias_t, name):
    bucket, valid = _bucket_tables()
    nq = bucket.shape[-1]

    def body(t_ref, b_ref, v_ref, o_ref):
        onehot = (lax.broadcasted_iota(jnp.int32, (N_REL_BUCKETS, BIAS_CHUNK), 0) == b_ref[...]).astype(BF16)
        acc = None
        for term in _split3(t_ref[...]):
            p = jnp.dot(term, onehot, preferred_element_type=F32)
            acc = p if acc is None else acc + p
        o_ref[...] = jnp.where(v_ref[...] > 0, acc, MASK_VALUE)

    return pl.pallas_call(
        body, name=name, grid=(3, 2, nq // BIAS_CHUNK),
        in_specs=[pl.BlockSpec((N_HEADS, N_REL_BUCKETS), lambda b, v, c: (0, 0)),
                  pl.BlockSpec((None, 1, BIAS_CHUNK), lambda b, v, c: (b, 0, c)),
                  pl.BlockSpec((None, None, 1, BIAS_CHUNK), lambda b, v, c: (b, v, 0, c))],
        out_specs=pl.BlockSpec((None, None, N_HEADS, BIAS_CHUNK), lambda b, v, c: (b, v, 0, c)),
        out_shape=_sds((3, 2, N_HEADS, nq), F32), compiler_params=_params("parallel", "parallel", "parallel"),
    )(rel_bias_t, jnp.asarray(bucket), jnp.asarray(valid))


def bias_reduce(dbias, name):
    bucket, _ = _bucket_tables()
    nq = bucket.shape[-1]
    dims = (((1,), (1,)), ((), ()))

    def body(d_ref, b_ref, o_ref):
        onehot = (lax.broadcasted_iota(jnp.int32, (N_REL_BUCKETS, BIAS_CHUNK), 0) == b_ref[...]).astype(BF16)
        acc = None
        for term in _split3(d_ref[...]):
            p = lax.dot_general(term, onehot, dims, preferred_element_type=F32)
            acc = p if acc is None else acc + p

        @pl.when(pl.program_id(1) == 0)
        def _():
            o_ref[...] = acc

        @pl.when(pl.program_id(1) > 0)
        def _():
            o_ref[...] += acc

    return pl.pallas_call(
        body, name=name, grid=(3, nq // BIAS_CHUNK),
        in_specs=[pl.BlockSpec((None, N_HEADS, BIAS_CHUNK), lambda b, c: (b, 0, c)),
                  pl.BlockSpec((None, 1, BIAS_CHUNK), lambda b, c: (b, 0, c))],
        out_specs=pl.BlockSpec((None, N_HEADS, N_REL_BUCKETS), lambda b, c: (b, 0, 0)),
        out_shape=_sds((3, N_HEADS, N_REL_BUCKETS), F32), compiler_params=_params("parallel", "arbitrary"),
    )(dbias, jnp.asarray(bucket))


PAIR = 2 * HEAD_DIM
N_PAIRS = N_HEADS // 2
_NT = (((1,), (1,)), ((), ()))
_TN = (((0,), (0,)), ((), ()))


def _low_lanes(shape):
    return lax.broadcasted_iota(jnp.int32, shape, 1) < HEAD_DIM


ATTN_VMEM_LIMIT_BYTES = 56 * 1024 * 1024
BRANCH_ORDER = (2, 1, 0)


def _regroup(dst, src, L16):
    for r in range(RESIDUES):
        dst[pl.ds(r * L16, L16), :] = src[pl.ds(r, L16, stride=RESIDUES), :]


def _ungroup(dst, src, L16):
    for r in range(RESIDUES):
        dst[pl.ds(r, L16, stride=RESIDUES), :] = src[pl.ds(r * L16, L16), :]


def _branch_geometry(branch, S):
    dil = DILATED_PAIRS[branch][1]
    runs = RESIDUES // dil
    return dil, runs, ATT_BLOCK // runs, S // dil // ATT_BLOCK


def _block_rows(it, branch, S):
    dil, runs, run_len, n_blocks = _branch_geometry(branch, S)
    L16 = S // RESIDUES
    r, b = it // n_blocks, it % n_blocks
    prev = jnp.maximum(b - 1, 0)
    cur_rows = [pl.multiple_of((j * dil + r) * L16 + run_len * b, 8) for j in range(runs)]
    prev_rows = [pl.multiple_of((j * dil + r) * L16 + run_len * prev, 8) for j in range(runs)]
    return cur_rows, prev_rows, jnp.minimum(b, 1)


def _load_block(ref, rows, run_len):
    parts = [ref[pl.ds(o, run_len), :] for o in rows]
    return parts[0] if len(parts) == 1 else jnp.concatenate(parts, axis=0)


def _store_block(ref, rows, run_len, value, add=False):
    for j, o in enumerate(rows):
        part = value[j * run_len:(j + 1) * run_len]
        if add:
            ref[pl.ds(o, run_len), :] += part
        else:
            ref[pl.ds(o, run_len), :] = part


ATTN_FWD_UNROLL = 8
ATTN_BWD_UNROLL = 4


def _stack_heads(x, low):
    zero = jnp.zeros_like(x)
    return jnp.concatenate([jnp.where(low, x, zero), jnp.where(low, zero, x)], axis=0)


def _unstack_heads(y, low):
    return jnp.where(low, y[:ATT_BLOCK], y[ATT_BLOCK:])


def attn_fwd(qkvn, bias, name):
    S = qkvn.shape[0]
    L16 = S // RESIDUES
    n_iter = S // ATT_BLOCK

    def body(q_ref, k_ref, v_ref, b_ref, o_ref, lse_ref, stage, qp, kp, vp, acc_s, m_s, l_s):
        for src, dst in ((q_ref, qp), (k_ref, kp), (v_ref, vp)):
            stage[...] = src[...].astype(F32)
            _regroup(dst, stage, L16)
        low = _low_lanes((ATT_BLOCK, PAIR))

        for branch in BRANCH_ORDER:
            _, _, run_len, _ = _branch_geometry(branch, S)
            first = branch == BRANCH_ORDER[0]

            def step(it, carry, branch=branch, run_len=run_len, first=first):
                cur, prev, variant = _block_rows(it, branch, S)
                q = _load_block(qp, cur, run_len).astype(BF16)
                k = jnp.concatenate([_load_block(kp, prev, run_len), _load_block(kp, cur, run_len)], axis=0).astype(BF16)
                v = jnp.concatenate([_load_block(vp, prev, run_len), _load_block(vp, cur, run_len)], axis=0).astype(BF16)
                s = lax.dot_general(_stack_heads(q, low), k, _NT, preferred_element_type=F32) * (HEAD_DIM ** -0.5)
                s = s + b_ref[2 * branch + variant].reshape(2 * ATT_BLOCK, 2 * ATT_BLOCK)
                mx = jnp.max(s, axis=-1, keepdims=True)
                p = jnp.exp(s - mx)
                den = jnp.sum(p, axis=-1, keepdims=True)
                pv = jnp.dot(p.astype(BF16), v, preferred_element_type=F32)
                acc = _unstack_heads(pv, low)
                m = _unstack_heads(mx, low)
                l = _unstack_heads(den, low)
                if not first:
                    m_old = _load_block(m_s, cur, run_len)
                    m_new = jnp.maximum(m_old, m)
                    a_old, a_new = jnp.exp(m_old - m_new), jnp.exp(m - m_new)
                    acc = _load_block(acc_s, cur, run_len) * a_old + acc * a_new
                    l = _load_block(l_s, cur, run_len) * a_old + l * a_new
                    m = m_new
                _store_block(acc_s, cur, run_len, acc)
                _store_block(m_s, cur, run_len, m)
                _store_block(l_s, cur, run_len, l)
                return carry

            lax.fori_loop(0, n_iter, step, 0, unroll=ATTN_FWD_UNROLL)

        acc_s[...] = acc_s[...] / l_s[...]
        _ungroup(stage, acc_s, L16)
        o_ref[...] = stage[...].astype(BF16)
        m_s[...] = m_s[...] + jnp.log(l_s[...])
        _ungroup(lse_ref, m_s, L16)

    col = lambda part: pl.BlockSpec((S, PAIR), lambda hp: (0, part * N_PAIRS + hp))
    out = pl.BlockSpec((S, PAIR), lambda hp: (0, hp))
    return pl.pallas_call(
        body, name=name, grid=(N_PAIRS,),
        in_specs=[col(0), col(1), col(2), pl.BlockSpec((6, 2, ATT_BLOCK, 2 * ATT_BLOCK), lambda hp: (0, hp, 0, 0))],
        out_specs=[out, out], out_shape=[_sds((S, D_MODEL), BF16), _sds((S, D_MODEL), F32)],
        scratch_shapes=[pltpu.VMEM((S, PAIR), F32)] * 7,
        compiler_params=pltpu.CompilerParams(dimension_semantics=("parallel",), vmem_limit_bytes=ATTN_VMEM_LIMIT_BYTES),
    )(qkvn, qkvn, qkvn, bias)


def attn_bwd(qkvn, att, datt, lse, bias, name):
    S = qkvn.shape[0]
    L16 = S // RESIDUES
    n_iter = S // ATT_BLOCK
    TILE = 512

    def body(q_ref, k_ref, v_ref, o_ref, do_ref, lse_ref, b_ref, dq_ref, dk_ref, dv_ref, db_ref,
             qp, kp, vp, dop, ldp, dqp, dkp, dvp):
        stage = dqp
        for src, dst in ((q_ref, qp), (k_ref, kp), (v_ref, vp), (do_ref, dop)):
            stage[...] = src[...].astype(F32)
            _regroup(dst, stage, L16)

        def pack(i, carry):
            rows = pl.ds(pl.multiple_of(i * TILE, TILE), TILE)
            low = _low_lanes((TILE, PAIR))
            lane = lax.broadcasted_iota(jnp.int32, (TILE, PAIR), 1)
            prod = do_ref[rows, :].astype(F32) * o_ref[rows, :].astype(F32)
            d0 = jnp.sum(jnp.where(low, prod, 0.0), axis=-1, keepdims=True)
            d1 = jnp.sum(jnp.where(low, 0.0, prod), axis=-1, keepdims=True)
            stage[rows, :] = jnp.where((lane & (HEAD_DIM // 2)) == 0, lse_ref[rows, :], jnp.where(low, d0, d1))
            return carry

        lax.fori_loop(0, S // TILE, pack, 0)
        _regroup(ldp, stage, L16)
        dqp[...] = jnp.zeros_like(dqp)
        dkp[...] = jnp.zeros_like(dkp)
        dvp[...] = jnp.zeros_like(dvp)
        db_ref[...] = jnp.zeros_like(db_ref)
        low = _low_lanes((ATT_BLOCK, PAIR))

        for branch in BRANCH_ORDER:
            _, _, run_len, _ = _branch_geometry(branch, S)

            def step(it, carry, branch=branch, run_len=run_len):
                cur, prev, variant = _block_rows(it, branch, S)
                q = _load_block(qp, cur, run_len).astype(BF16)
                dout = _load_block(dop, cur, run_len).astype(BF16)
                ld = _load_block(ldp, cur, run_len)
                k = jnp.concatenate([_load_block(kp, prev, run_len), _load_block(kp, cur, run_len)], axis=0).astype(BF16)
                v = jnp.concatenate([_load_block(vp, prev, run_len), _load_block(vp, cur, run_len)], axis=0).astype(BF16)
                half = HEAD_DIM // 2
                lse2 = jnp.concatenate([ld[:, 0:1], ld[:, HEAD_DIM:HEAD_DIM + 1]], axis=0)
                delta2 = jnp.concatenate([ld[:, half:half + 1], ld[:, HEAD_DIM + half:HEAD_DIM + half + 1]], axis=0)
                q2, do2 = _stack_heads(q, low), _stack_heads(dout, low)
                s = lax.dot_general(q2, k, _NT, preferred_element_type=F32) * (HEAD_DIM ** -0.5)
                p = jnp.exp(s + b_ref[2 * branch + variant].reshape(2 * ATT_BLOCK, 2 * ATT_BLOCK) - lse2)
                dp = lax.dot_general(do2, v, _NT, preferred_element_type=F32)
                ds = p * (dp - delta2)
                db_ref[branch] += ds.reshape(2, ATT_BLOCK, 2 * ATT_BLOCK)
                dsb = (ds * (HEAD_DIM ** -0.5)).astype(BF16)
                dq = _unstack_heads(jnp.dot(dsb, k, preferred_element_type=F32), low)
                dk = lax.dot_general(dsb, q2, _TN, preferred_element_type=F32)
                dv = lax.dot_general(p.astype(BF16), do2, _TN, preferred_element_type=F32)
                _store_block(dqp, cur, run_len, dq, add=True)
                _store_block(dkp, prev, run_len, dk[:ATT_BLOCK], add=True)
                _store_block(dvp, prev, run_len, dv[:ATT_BLOCK], add=True)
                _store_block(dkp, cur, run_len, dk[ATT_BLOCK:], add=True)
                _store_block(dvp, cur, run_len, dv[ATT_BLOCK:], add=True)
                return carry

            lax.fori_loop(0, n_iter, step, 0, unroll=ATTN_BWD_UNROLL)

        _ungroup(dq_ref, dqp, L16)
        _ungroup(dk_ref, dkp, L16)
        _ungroup(dv_ref, dvp, L16)

    col = lambda part: pl.BlockSpec((S, PAIR), lambda hp: (0, part * N_PAIRS + hp))
    one = pl.BlockSpec((S, PAIR), lambda hp: (0, hp))
    return pl.pallas_call(
        body, name=name, grid=(N_PAIRS,),
        in_specs=[col(0), col(1), col(2), one, one, one,
                  pl.BlockSpec((6, 2, ATT_BLOCK, 2 * ATT_BLOCK), lambda hp: (0, hp, 0, 0))],
        out_specs=[one, one, one, pl.BlockSpec((3, 2, ATT_BLOCK, 2 * ATT_BLOCK), lambda hp: (0, hp, 0, 0))],
        out_shape=[_sds((S, D_MODEL), F32)] * 3 + [_sds((3, N_HEADS, ATT_BLOCK, 2 * ATT_BLOCK), F32)],
        scratch_shapes=[pltpu.VMEM((S, PAIR), F32)] * 8,
        compiler_params=pltpu.CompilerParams(dimension_semantics=("parallel",), vmem_limit_bytes=ATTN_VMEM_LIMIT_BYTES),
    )(qkvn, qkvn, qkvn, att, datt, lse, bias)


def loss_grad(y, target, name, ts=512):
    S, Dm = y.shape

    def body(y_ref, t_ref, d_ref, db_ref, s_ref):
        e = y_ref[...] - t_ref[...]
        d = e * (1.0 / Dm)
        d_ref[...] = d
        db_ref[...] = d.astype(BF16)
        part = jnp.sum(e * e, axis=0, keepdims=True)

        @pl.when(pl.program_id(0) == 0)
        def _():
            s_ref[...] = part

        @pl.when(pl.program_id(0) > 0)
        def _():
            s_ref[...] += part

    row = pl.BlockSpec((ts, Dm), lambda i: (i, 0))
    vec = pl.BlockSpec((1, Dm), lambda i: (0, 0))
    return pl.pallas_call(
        body, name=name, grid=(S // ts,), in_specs=[row, row], out_specs=[row, row, vec],
        out_shape=[_sds((S, Dm), F32), _sds((S, Dm), BF16), _sds((1, Dm), F32)], compiler_params=_params("arbitrary"),
    )(y, target)


def adamw(w, g, m, v, name):
    n, R, C = w.shape

    def body(w_ref, g_ref, m_ref, v_ref, d_ref, nm_ref, nv_ref):
        gv = g_ref[...]
        m2 = ADAM_B1 * m_ref[...] + (1.0 - ADAM_B1) * gv
        v2 = ADAM_B2 * v_ref[...] + (1.0 - ADAM_B2) * (gv * gv)
        m_hat = m2 / (1.0 - ADAM_B1 ** ADAM_STEP)
        v_hat = v2 / (1.0 - ADAM_B2 ** ADAM_STEP)
        d_ref[...] = -ADAM_LR * (m_hat / (jnp.sqrt(v_hat) + ADAM_EPS) + ADAM_WD * w_ref[...])
        nm_ref[...] = m2
        nv_ref[...] = v2

    tr = R
    while tr * C * 4 > (1 << 21) and tr % 16 == 0:
        tr //= 2
    spec = pl.BlockSpec((None, tr, C), lambda i, r: (i, r, 0))
    return pl.pallas_call(
        body, name=name, grid=(n, R // tr), in_specs=[spec] * 4, out_specs=[spec] * 3,
        out_shape=[_sds((n, R, C), F32)] * 3, compiler_params=_params("parallel", "parallel"),
    )(w, g, m, v)


ANY = pl.BlockSpec(memory_space=pl.ANY)


def _coords():
    return lax.axis_index("x"), lax.axis_index("y"), lax.axis_index("c")


def _other_chips(mx, my):
    return [(1 - mx, my), (mx, 1 - my), (1 - mx, 1 - my)]


def _remote(src, dst, send, recv, dev):
    return pltpu.make_async_remote_copy(src_ref=src, dst_ref=dst, send_sem=send, recv_sem=recv, device_id=dev,
                                        device_id_type=MESH)


def allgather_devices(x, name):
    R, C = x.shape

    def body(x_ref, o_ref, send, recv, local_sem):
        mx, my, mc = _coords()
        me = 4 * mx + 2 * my + mc
        local = pltpu.make_async_copy(x_ref, o_ref.at[me], local_sem)
        local.start()
        peers = []
        for k in range(1, N_DEV):
            px = 1 - mx if k & 4 else mx
            py = 1 - my if k & 2 else my
            pc = 1 - mc if k & 1 else mc
            peers.append((px, py, pc))
        sends = [_remote(x_ref, o_ref.at[me], send.at[k], recv.at[k], p) for k, p in enumerate(peers)]
        for cp in sends:
            cp.start()
        for k, (px, py, pc) in enumerate(peers):
            _remote(x_ref, o_ref.at[4 * px + 2 * py + pc], send.at[k], recv.at[k], (px, py, pc)).wait_recv()
        for cp in sends:
            cp.wait_send()
        local.wait()

    return pl.pallas_call(
        body, name=name, in_specs=[ANY], out_specs=ANY, out_shape=_sds((N_DEV, R, C), x.dtype),
        scratch_shapes=[pltpu.SemaphoreType.DMA((N_DEV - 1,)), pltpu.SemaphoreType.DMA((N_DEV - 1,)),
                        pltpu.SemaphoreType.DMA],
    )(x)


HBM = pl.BlockSpec(memory_space=pltpu.HBM)
SEM = pl.BlockSpec(memory_space=pltpu.SEMAPHORE)
_SPLIT_COPY = pltpu.CompilerParams(has_side_effects=pltpu.SideEffectType.DATAFLOW_SIDE_EFFECTING)


def _in_hbm(a):
    return pltpu.with_memory_space_constraint(a, pltpu.HBM)


def cast_into_slot(w, layer, chip_core, name):
    _, _, hR, C = w.shape

    def body(s_ref, w_ref, o_ref):
        del s_ref
        o_ref[...] = w_ref[...].astype(BF16)

    grid_spec = pltpu.PrefetchScalarGridSpec(
        num_scalar_prefetch=1, grid=(2,),
        in_specs=[pl.BlockSpec((None, None, hR, C), lambda h, s: (layer, h, 0, 0))],
        out_specs=pl.BlockSpec((None, None, hR, C), lambda h, s: (s[0], h, 0, 0)))
    return pl.pallas_call(body, name=name, grid_spec=grid_spec, out_shape=_sds((N_CHIPS, 2, hR, C), BF16),
                          compiler_params=_params("parallel"))(chip_core, w)


def gather_start(lands, groups, name):
    n = len(lands)
    n_groups = len(groups)

    def body(*refs):
        ins = refs[:n]
        sems = refs[n:n + 2 * n_groups]
        token = refs[-1]
        mx, my, mc = _coords()
        chip = 2 * mx + my
        for g, members in enumerate(groups):
            send, recv = sems[2 * g], sems[2 * g + 1]
            for i, a in enumerate(members):
                mine = ins[a].at[chip, mc]
                for k, (px, py) in enumerate(_other_chips(mx, my)):
                    _remote(mine, mine, send.at[3 * i + k], recv.at[3 * i + k], (px, py, mc)).start()
        token[...] = jnp.zeros_like(token)

    sem_shapes = []
    for members in groups:
        sem_shapes += [pltpu.SemaphoreType.DMA((3 * len(members),))] * 2
    outs = pl.pallas_call(
        body, name=name, in_specs=[HBM] * n,
        out_specs=[SEM] * (2 * n_groups) + [HBM] * n + [pl.BlockSpec(memory_space=pltpu.VMEM)],
        out_shape=sem_shapes + [pltpu.HBM(a.shape, a.dtype) for a in lands] + [_sds((SUBLANES, LANES), F32)],
        input_output_aliases={a: 2 * n_groups + a for a in range(n)}, compiler_params=_SPLIT_COPY,
    )(*[_in_hbm(a) for a in lands])
    sems = [(outs[2 * g], outs[2 * g + 1]) for g in range(n_groups)]
    return sems, list(outs[2 * n_groups:2 * n_groups + n]), outs[-1]


def gather_forward(lands, sems, after, name):
    n = len(lands)

    def body(*refs):
        ins = refs[:n]
        send, recv = refs[n], refs[n + 1]
        fsend, frecv = refs[n + 3], refs[n + 4]
        mx, my, mc = _coords()
        for i in range(n):
            for k, (px, py) in enumerate(_other_chips(mx, my)):
                landed = ins[i].at[2 * px + py, mc]
                cp = _remote(landed, landed, send.at[3 * i + k], recv.at[3 * i + k], (px, py, mc))
                cp.wait_send()
                cp.wait_recv()
                _remote(landed, landed, fsend.at[3 * i + k], frecv.at[3 * i + k], (mx, my, 1 - mc)).start()

    outs = pl.pallas_call(
        body, name=name, in_specs=[HBM] * n + [SEM, SEM, ANY], out_specs=[SEM, SEM] + [HBM] * n,
        out_shape=[pltpu.SemaphoreType.DMA((3 * n,))] * 2 + [pltpu.HBM(a.shape, a.dtype) for a in lands],
        input_output_aliases={a: 2 + a for a in range(n)}, compiler_params=_SPLIT_COPY,
    )(*lands, sems[0], sems[1], after)
    return (outs[0], outs[1]), list(outs[2:])


def gather_wait(lands, sems, after, name):
    n = len(lands)

    def body(*refs):
        ins = refs[:n]
        fsend, frecv = refs[n], refs[n + 1]
        mx, my, mc = _coords()
        for i in range(n):
            for k, (px, py) in enumerate(_other_chips(mx, my)):
                theirs = ins[i].at[2 * px + py, 1 - mc]
                cp = _remote(theirs, theirs, fsend.at[3 * i + k], frecv.at[3 * i + k], (mx, my, 1 - mc))
                cp.wait_send()
                cp.wait_recv()

    outs = pl.pallas_call(
        body, name=name, in_specs=[HBM] * n + [SEM, SEM, ANY], out_specs=[HBM] * n,
        out_shape=[pltpu.HBM(a.shape, a.dtype) for a in lands],
        input_output_aliases={a: a for a in range(n)}, compiler_params=_SPLIT_COPY,
    )(*lands, sems[0], sems[1], after)
    return list(outs)


def _peers(mx, my, mc):
    return [(1 - mx if k & 4 else mx, 1 - my if k & 2 else my, 1 - mc if k & 1 else mc) for k in range(1, N_DEV)]


def devices_start(x, name):
    def body(x_ref, land_ref, send, recv, x_thru, land_thru):
        mx, my, mc = _coords()
        me = 4 * mx + 2 * my + mc
        for k, peer in enumerate(_peers(mx, my, mc)):
            _remote(x_ref, land_ref.at[me], send.at[k], recv.at[k], peer).start()

    land = lax.empty((N_DEV,) + x.shape, x.dtype)
    outs = pl.pallas_call(
        body, name=name, in_specs=[HBM, HBM], out_specs=[SEM, SEM, HBM, HBM],
        out_shape=[pltpu.SemaphoreType.DMA((N_DEV - 1,))] * 2 + [pltpu.HBM(x.shape, x.dtype), pltpu.HBM(land.shape, x.dtype)],
        input_output_aliases={0: 2, 1: 3}, compiler_params=_SPLIT_COPY,
    )(_in_hbm(x), _in_hbm(land))
    return (outs[0], outs[1]), outs[2], outs[3]


def devices_wait(x, land, sems, after, name):
    def body(x_ref, land_ref, send, recv, after_ref, x_thru, land_thru):
        mx, my, mc = _coords()
        for k, (px, py, pc) in enumerate(_peers(mx, my, mc)):
            cp = _remote(x_ref, land_ref.at[4 * px + 2 * py + pc], send.at[k], recv.at[k], (px, py, pc))
            cp.wait_send()
            cp.wait_recv()

    outs = pl.pallas_call(
        body, name=name, in_specs=[HBM, HBM, SEM, SEM, ANY], out_specs=[HBM, HBM],
        out_shape=[pltpu.HBM(x.shape, x.dtype), pltpu.HBM(land.shape, land.dtype)],
        input_output_aliases={0: 0, 1: 1}, compiler_params=_SPLIT_COPY,
    )(x, land, sems[0], sems[1], after)
    return outs[0], outs[1]


def device_sum(land, own, me, name):
    _, R, C = land.shape

    def body(s_ref, l_ref, o_ref_in, o_ref):
        acc = None
        for q in range(N_DEV):
            term = jnp.where(s_ref[0] == q, o_ref_in[...], l_ref[q])
            acc = term if acc is None else acc + term
        o_ref[...] = acc

    grid_spec = pltpu.PrefetchScalarGridSpec(
        num_scalar_prefetch=1, grid=(1,),
        in_specs=[pl.BlockSpec((N_DEV, R, C), lambda i, s: (0, 0, 0)), pl.BlockSpec((R, C), lambda i, s: (0, 0))],
        out_specs=pl.BlockSpec((R, C), lambda i, s: (0, 0)))
    return pl.pallas_call(body, name=name, grid_spec=grid_spec, out_shape=_sds((R, C), F32),
                          compiler_params=_params("arbitrary"))(me, land, own)


def reduce_send(grads, name):
    n = len(grads)

    def body(*refs):
        ins, lands = refs[:n], refs[n:2 * n]
        send, recv = refs[2 * n], refs[2 * n + 1]
        mx, my, mc = _coords()
        me = 4 * mx + 2 * my + mc
        for a in range(n):
            for k, (px, py, pc) in enumerate(_peers(mx, my, mc)):
                _remote(ins[a].at[2 * px + py, pc], lands[a].at[me], send.at[7 * a + k], recv.at[7 * a + k], (px, py, pc)).start()

    lands = [lax.empty((N_DEV,) + g.shape[2:], g.dtype) for g in grads]
    outs = pl.pallas_call(
        body, name=name, in_specs=[HBM] * (2 * n), out_specs=[SEM, SEM] + [HBM] * (2 * n),
        out_shape=[pltpu.SemaphoreType.DMA((7 * n,))] * 2 + [pltpu.HBM(a.shape, a.dtype) for a in grads + lands],
        input_output_aliases={a: 2 + a for a in range(2 * n)}, compiler_params=_SPLIT_COPY,
    )(*[_in_hbm(a) for a in grads + lands])
    return (outs[0], outs[1]), list(outs[2:2 + n]), list(outs[2 + n:])


def reduce_wait(grads, lands, sems, after, name):
    n = len(grads)

    def body(*refs):
        ins, zones = refs[:n], refs[n:2 * n]
        send, recv = refs[2 * n], refs[2 * n + 1]
        mx, my, mc = _coords()
        for a in range(n):
            for k, (px, py, pc) in enumerate(_peers(mx, my, mc)):
                cp = _remote(ins[a].at[2 * px + py, pc], zones[a].at[4 * px + 2 * py + pc], send.at[7 * a + k],
                             recv.at[7 * a + k], (px, py, pc))
                cp.wait_send()
                cp.wait_recv()

    outs = pl.pallas_call(
        body, name=name, in_specs=[HBM] * (2 * n) + [SEM, SEM, ANY], out_specs=[HBM] * (2 * n),
        out_shape=[pltpu.HBM(a.shape, a.dtype) for a in grads + lands],
        input_output_aliases={a: a for a in range(2 * n)}, compiler_params=_SPLIT_COPY,
    )(*grads, *lands, sems[0], sems[1], after)
    return list(outs[:n]), list(outs[n:])


def reduce_sum(land, grad, place, name, into=None, layer=None):
    _, hR, C = land.shape
    tr = hR
    while N_DEV * tr * C * 2 > (6 << 20) and tr % 32 == 0:
        tr //= 2

    def body(s_ref, l_ref, g_ref, *rest):
        o_ref = rest[-1]
        own = g_ref[...].astype(F32)
        acc = None
        for q in range(N_DEV):
            term = jnp.where(s_ref[2] == q, own, l_ref[q].astype(F32))
            acc = term if acc is None else acc + term
        o_ref[...] = acc

    in_specs = [pl.BlockSpec((N_DEV, tr, C), lambda i, s: (0, i, 0)),
                pl.BlockSpec((None, None, tr, C), lambda i, s: (s[0], s[1], i, 0))]
    args = [place, land, grad]
    aliases = {}
    if layer is None:
        out_spec = pl.BlockSpec((None, tr, C), lambda i, s: (s[1], i, 0))
        out_shape = _sds((2, hR, C), F32)
    else:
        out_spec = pl.BlockSpec((None, None, tr, C), lambda i, s: (layer, s[1], i, 0))
        out_shape = _sds((2, 2, hR, C), F32)
        if into is not None:
            in_specs.append(ANY)
            args.append(into)
            aliases = {3: 0}
    grid_spec = pltpu.PrefetchScalarGridSpec(num_scalar_prefetch=1, grid=(hR // tr,), in_specs=in_specs, out_specs=out_spec)
    return pl.pallas_call(body, name=name, grid_spec=grid_spec, out_shape=out_shape, input_output_aliases=aliases,
                          compiler_params=_params("arbitrary"))(*args)


def join_halves(arrays, name):
    n = len(arrays)
    pieces = [(a, l) for a, arr in enumerate(arrays) for l in (range(arr.shape[0]) if arr.ndim == 4 else [None])]

    def body(*refs):
        ins = refs[:n]
        send, recv = refs[2 * n:]
        mx, my, mc = _coords()

        def half(a, l, h):
            return ins[a].at[h] if l is None else ins[a].at[l, h]

        sends = [_remote(half(a, l, mc), half(a, l, mc), send.at[i], recv.at[i], (mx, my, 1 - mc))
                 for i, (a, l) in enumerate(pieces)]
        for cp in sends:
            cp.start()
        for i, (a, l) in enumerate(pieces):
            theirs = half(a, l, 1 - mc)
            _remote(theirs, theirs, send.at[i], recv.at[i], (mx, my, 1 - mc)).wait_recv()
        for cp in sends:
            cp.wait_send()

    return pl.pallas_call(
        body, name=name, in_specs=[ANY] * n, out_specs=[ANY] * n, out_shape=[_sds(a.shape, a.dtype) for a in arrays],
        input_output_aliases={a: a for a in range(n)},
        scratch_shapes=[pltpu.SemaphoreType.DMA((len(pieces),)), pltpu.SemaphoreType.DMA((len(pieces),))],
    )(*arrays)


LANES = 128
SUBLANES = 8


def _n_rows(shape):
    rows = -(-int(np.prod(shape)) // LANES)
    return -(-rows // SUBLANES) * SUBLANES


def _as_rows(a):
    flat = a.reshape(-1)
    rows = _n_rows(a.shape)
    return jnp.pad(flat, (0, rows * LANES - flat.shape[0])).reshape(rows, LANES)


def _pack(arrays):
    return jnp.concatenate([_as_rows(a) for a in arrays], axis=0)


def _unpack(rows, shapes):
    out, r0 = [], 0
    for s in shapes:
        n = _n_rows(s)
        out.append(rows[r0:r0 + n].reshape(-1)[:int(np.prod(s))].reshape(s))
        r0 += n
    return out


REPLICATED_SMALL = [("rel_bias", (32, 16)), ("even_norm", (1, 1024)), ("even_pool_w", (1, 4, 128, 128)),
                    ("even_pool_scale", (1, 512)), ("odd_q_norm", (1, 64)), ("odd_k_norm", (1, 64)),
                    ("ffn_norm", (2, 1024)), ("ffn_conv_b", (2, 5632))]
SHARDED_SMALL = [("even_conv_w", (1, 3, 128)), ("odd_norm", (1, 256)), ("ffn_conv_w", (2, 3, 1408))]
BIG = ["even_w_in", "even_w_out", "odd_w_qkv", "odd_w_o", "ffn_w_up", "ffn_w_down"]
WEIGHT_ORDER = ["rel_bias", "even_norm", "even_w_in", "even_conv_w", "even_pool_w", "even_pool_scale", "even_w_out",
                "odd_norm", "odd_w_qkv", "odd_q_norm", "odd_k_norm", "odd_w_o", "ffn_norm", "ffn_w_up", "ffn_conv_w",
                "ffn_conv_b", "ffn_w_down"]


def kernel(x, rel_bias, even_norm, even_w_in, even_conv_w, even_pool_w, even_pool_scale, even_w_out, odd_norm, odd_w_qkv, odd_q_norm, odd_k_norm, odd_w_o, ffn_norm, ffn_w_up, ffn_conv_w, ffn_conv_b, ffn_w_down, loss_target, m_rel_bias, m_even_norm, m_even_w_in, m_even_conv_w, m_even_pool_w, m_even_pool_scale, m_even_w_out, m_odd_norm, m_odd_w_qkv, m_odd_q_norm, m_odd_k_norm, m_odd_w_o, m_ffn_norm, m_ffn_w_up, m_ffn_conv_w, m_ffn_conv_b, m_ffn_w_down, v_rel_bias, v_even_norm, v_even_w_in, v_even_conv_w, v_even_pool_w, v_even_pool_scale, v_even_w_out, v_odd_norm, v_odd_w_qkv, v_odd_q_norm, v_odd_k_norm, v_odd_w_o, v_ffn_norm, v_ffn_w_up, v_ffn_conv_w, v_ffn_conv_b, v_ffn_w_down):
    W = dict(rel_bias=rel_bias, even_norm=even_norm, even_w_in=even_w_in, even_conv_w=even_conv_w, even_pool_w=even_pool_w,
             even_pool_scale=even_pool_scale, even_w_out=even_w_out, odd_norm=odd_norm, odd_w_qkv=odd_w_qkv,
             odd_q_norm=odd_q_norm, odd_k_norm=odd_k_norm, odd_w_o=odd_w_o, ffn_norm=ffn_norm, ffn_w_up=ffn_w_up,
             ffn_conv_w=ffn_conv_w, ffn_conv_b=ffn_conv_b, ffn_w_down=ffn_w_down)
    M1 = dict(rel_bias=m_rel_bias, even_norm=m_even_norm, even_w_in=m_even_w_in, even_conv_w=m_even_conv_w,
              even_pool_w=m_even_pool_w, even_pool_scale=m_even_pool_scale, even_w_out=m_even_w_out, odd_norm=m_odd_norm,
              odd_w_qkv=m_odd_w_qkv, odd_q_norm=m_odd_q_norm, odd_k_norm=m_odd_k_norm, odd_w_o=m_odd_w_o,
              ffn_norm=m_ffn_norm, ffn_w_up=m_ffn_w_up, ffn_conv_w=m_ffn_conv_w, ffn_conv_b=m_ffn_conv_b,
              ffn_w_down=m_ffn_w_down)
    M2 = dict(rel_bias=v_rel_bias, even_norm=v_even_norm, even_w_in=v_even_w_in, even_conv_w=v_even_conv_w,
              even_pool_w=v_even_pool_w, even_pool_scale=v_even_pool_scale, even_w_out=v_even_w_out, odd_norm=v_odd_norm,
              odd_w_qkv=v_odd_w_qkv, odd_q_norm=v_odd_q_norm, odd_k_norm=v_odd_k_norm, odd_w_o=v_odd_w_o,
              ffn_norm=v_ffn_norm, ffn_w_up=v_ffn_w_up, ffn_conv_w=v_ffn_conv_w, ffn_conv_b=v_ffn_conv_b,
              ffn_w_down=v_ffn_w_down)
    mx, my, mc = _coords()
    chip = 2 * mx + my
    me = 4 * mx + 2 * my + mc
    place = jnp.stack([chip, mc, me]).astype(jnp.int32)
    xs, target = x[0], loss_target[0]

    def halves(w):
        return w.reshape((w.shape[0], 2, w.shape[-2] // 2, w.shape[-1]))

    lands = [cast_into_slot(halves(even_w_in), 0, place, "cast_w_in"), cast_into_slot(halves(even_w_out), 0, place, "cast_w_out"),
             cast_into_slot(halves(ffn_w_up), 0, place, "cast_w_up0"), cast_into_slot(halves(ffn_w_down), 0, place, "cast_w_down0"),
             cast_into_slot(halves(odd_w_qkv), 0, place, "cast_w_qkv"), cast_into_slot(halves(odd_w_o), 0, place, "cast_w_o"),
             cast_into_slot(halves(ffn_w_up), 1, place, "cast_w_up1"), cast_into_slot(halves(ffn_w_down), 1, place, "cast_w_down1")]
    small = allgather_devices(_pack([even_conv_w, odd_norm, ffn_conv_w]), "allgather_small_weights")
    lands[0], small = lax.optimization_barrier((lands[0], small))
    gather_sems, lands, token = gather_start(lands, [[0, 1], [2, 3], [4, 5], [6, 7]], "gather_start")
    even_norm_after_start = even_norm + token[0:1, 0:1]
    small = small[0::2]
    conv_w_full = small[:, 0:3].transpose(1, 0, 2).reshape(3, A_WIDTH)
    odd_norm_full = small[:, 8:10].reshape(1, D_MODEL)
    ffn_cw_full = small[:, 16:82].reshape(N_CHIPS, 2, 3, 2 * D_FF // N_CHIPS).transpose(1, 2, 0, 3).reshape(2, 3, 2 * D_FF)
    pool_w = cast_bf16(even_pool_w[0], "cast_pool_w")
    gqk = jnp.stack([jnp.tile(odd_q_norm[0], N_HEADS), jnp.tile(odd_k_norm[0], N_HEADS),
                     jnp.ones((D_MODEL,), F32)])[:, None, :]
    bias = bias_expand(rel_bias.T, "bias_expand").reshape(6, N_HEADS, ATT_BLOCK, 2 * ATT_BLOCK)

    def ffn_fwd(l, xin):
        xn = rmsnorm_fwd(xin, ffn_norm[l:l + 1], f"ffn{l}_norm")
        up = mm_nn(xn, w_up[l], f"ffn{l}_up")
        act = glu_fwd(up, ffn_cw_full[l], ffn_conv_b[l:l + 1], f"ffn{l}_glu")
        return mm_nn(act, w_down[l], f"ffn{l}_down", res=xin), (xin, xn, up, act)

    def gathered(group, tag, after_landing, after_passing):
        sems, arrays = gather_forward(lands[2 * group:2 * group + 2], gather_sems[group], after_landing, "gather_forward_" + tag)
        return gather_wait(arrays, sems, after_passing, "gather_wait_" + tag)

    def ffn_weights(got):
        return got[0].reshape(N_CHIPS, 1, D_MODEL, 2 * D_FF // N_CHIPS), got[1].reshape(1, 1, D_FF, D_MODEL)

    w_up, w_down = [None, None], [None, None]
    xn0 = rmsnorm_fwd(xs, even_norm_after_start, "even_norm")
    got = gathered(0, "even", bias, xn0)
    w_in = got[0].reshape(N_CHIPS, 1, D_MODEL, EVEN_IN // N_CHIPS)
    w_out = got[1].reshape(1, 1, D_MODEL, D_MODEL)
    proj = mm_nn(xn0, w_in, "even_in")
    mix = mixer_fwd(proj, conv_w_full, pool_w, even_pool_scale, "even_mixer")
    x1 = mm_nn(mix, w_out, "even_out", res=xs)
    w_up[0], w_down[0] = ffn_weights(gathered(1, "ffn0", proj, x1))
    x2, ffn0 = ffn_fwd(0, x1)
    got = gathered(2, "odd", x1, x2)
    w_qkv = got[0].reshape(N_CHIPS, 1, D_MODEL, 3 * D_MODEL // N_CHIPS)
    w_o = got[1].reshape(1, 1, D_MODEL, D_MODEL)
    xn2 = rmsnorm_fwd(x2, odd_norm_full, "odd_norm")
    qkv = mm_nn(xn2, w_qkv, "odd_qkv")
    qkvn = qknorm_fwd(qkv, gqk, "odd_qknorm")
    att, lse = attn_fwd(qkvn, bias, "attn_fwd")
    x3 = mm_nn(att, w_o, "odd_out", res=x2)
    w_up[1], w_down[1] = ffn_weights(gathered(3, "ffn1", x2, x3))
    x4, ffn1 = ffn_fwd(1, x3)

    dy, dyb, sq = loss_grad(x4, target, "loss")
    loss = lax.psum(0.5 * jnp.sum(sq) * (1.0 / D_MODEL), ("x", "y", "c"))

    def ffn_bwd(l, dy, dyb, saved):
        xin, xn, up, act = saved
        dw_down = mm_tn(act, dyb, f"ffn{l}_dw_down", J=1, tk=D_FF // 2)
        dact = mm_nt(dyb, w_down[l], f"ffn{l}_dact", tr=D_FF // 2)
        dup, dcw, dcb = glu_bwd(up, dact, ffn_cw_full[l], ffn_conv_b[l:l + 1], f"ffn{l}_glu_bwd")
        dw_up = mm_tn(xn, dup, f"ffn{l}_dw_up", J=N_CHIPS, tk=512)
        dxn = mm_nt(dup, w_up[l], f"ffn{l}_dxn", tr=D_MODEL)
        dx, dxb, dg = rmsnorm_bwd(xin, ffn_norm[l:l + 1], dxn, dy, f"ffn{l}_norm_bwd")
        return dx, dxb, (dw_down, dw_up, dcw, dcb, dg)

    def quarters(g):
        return g.reshape(N_CHIPS, 2, g.shape[0] * g.shape[1] // (2 * N_CHIPS), g.shape[-1])

    def reduce_start(grads, tag, then):
        sems, parts, zones = reduce_send([quarters(g) for g in grads], "reduce_send_" + tag)
        then, parts = lax.optimization_barrier((then, parts))
        return (sems, parts, zones), then

    dx3, dx3b, g_ffn1 = ffn_bwd(1, dy, dyb, ffn1)
    red_ffn1, (dx3, dx3b) = reduce_start([g_ffn1[1], g_ffn1[0]], "ffn1", (dx3, dx3b))
    dw_o = mm_tn(att, dx3b, "odd_dw_o", J=1, tk=512)
    datt = mm_nt(dx3b, w_o, "odd_datt", tr=D_MODEL, out_dtype=BF16)
    dq, dk, dv, dbias = attn_bwd(qkvn, att, datt, lse, bias, "attn_bwd")
    dqkv, dgqk = qknorm_bwd(qkv, dq, dk, dv, gqk, "odd_qknorm_bwd")
    dw_qkv = mm_tn(xn2, dqkv, "odd_dw_qkv", J=N_CHIPS, tk=512)
    dxn2 = mm_nt(dqkv, w_qkv, "odd_dxn", tr=D_MODEL)
    red_odd, dxn2 = reduce_start([dw_qkv, dw_o], "odd", dxn2)
    dx2, dx2b, dg_odd = rmsnorm_bwd(x2, odd_norm_full, dxn2, dx3, "odd_norm_bwd")
    dx1, dx1b, g_ffn0 = ffn_bwd(0, dx2, dx2b, ffn0)
    red_ffn0, (dx1, dx1b) = reduce_start([g_ffn0[1], g_ffn0[0]], "ffn0", (dx1, dx1b))
    dw_out = mm_tn(mix, dx1b, "even_dw_out", J=1, tk=512)
    dmix = mm_nt(dx1b, w_out, "even_dmix", tr=D_MODEL)
    dproj, dcw_even, dpw, dps = mixer_bwd(proj, dmix, conv_w_full, pool_w, even_pool_scale, "even_mixer_bwd")
    dw_in = mm_tn(xn0, dproj, "even_dw_in", J=N_CHIPS, tk=512)
    dxn0 = mm_nt(dproj, w_in, "even_dxn", tr=D_MODEL)
    grad_x, _, dg_even = rmsnorm_bwd(xs, even_norm, dxn0, dx1, "even_norm_bwd")
    d_rel = jnp.sum(bias_reduce(dbias.reshape(3, N_HEADS, 2 * ATT_BLOCK * ATT_BLOCK), "bias_reduce"), axis=0).T

    red_even, grad_x = reduce_start([dw_in, dw_out], "even", grad_x)

    dcw_sh = dcw_even.reshape(3, N_CHIPS, A_WIDTH // N_CHIPS).transpose(1, 0, 2)
    don_sh = dg_odd.reshape(N_CHIPS, D_MODEL // N_CHIPS)
    dfcw = jnp.stack([g_ffn0[2], g_ffn1[2]])
    dfcw_sh = dfcw.reshape(2, 3, N_CHIPS, 2 * D_FF // N_CHIPS).transpose(2, 0, 1, 3)
    rep_grads = [d_rel, dg_even, dpw[None], dps, _head_sum(dgqk[0]), _head_sum(dgqk[1]),
                 jnp.concatenate([g_ffn0[4], g_ffn1[4]], axis=0), jnp.concatenate([g_ffn0[3], g_ffn1[3]], axis=0)]
    rep_rows = _pack(rep_grads)
    shard_rows = jnp.concatenate([_pack([dcw_sh[j], don_sh[j], dfcw_sh[j]]) for j in range(N_CHIPS)], axis=0)
    n_rep, n_shard = rep_rows.shape[0], shard_rows.shape[0] // N_CHIPS
    small_sems, small_rows, small_land = devices_start(jnp.concatenate([rep_rows, shard_rows], axis=0), "small_grads_start")
    grad_x, small_rows = lax.optimization_barrier((grad_x, small_rows))

    def reduce_end(red, tag, after):
        sems, parts, zones = red
        parts, zones = reduce_wait(parts, zones, sems, after, "reduce_wait_" + tag)
        return zones, parts

    z_ffn1, p_ffn1 = reduce_end(red_ffn1, "ffn1", grad_x)
    z_odd, p_odd = reduce_end(red_odd, "odd", grad_x)
    z_ffn0, p_ffn0 = reduce_end(red_ffn0, "ffn0", grad_x)
    r_up = reduce_sum(z_ffn0[0], p_ffn0[0], place, "reduce_sum_w_up0", layer=0)
    r_up = reduce_sum(z_ffn1[0], p_ffn1[0], place, "reduce_sum_w_up1", into=r_up, layer=1)
    r_down = reduce_sum(z_ffn0[1], p_ffn0[1], place, "reduce_sum_w_down0", layer=0)
    r_down = reduce_sum(z_ffn1[1], p_ffn1[1], place, "reduce_sum_w_down1", into=r_down, layer=1)
    z_even, p_even = reduce_end(red_even, "even", r_down)
    halves_written = [reduce_sum(z_even[0], p_even[0], place, "reduce_sum_w_in"),
                      reduce_sum(z_even[1], p_even[1], place, "reduce_sum_w_out"),
                      reduce_sum(z_odd[0], p_odd[0], place, "reduce_sum_w_qkv"),
                      reduce_sum(z_odd[1], p_odd[1], place, "reduce_sum_w_o"), r_up, r_down]
    joined = join_halves(halves_written, "grads_join_halves")
    G = {nm: g.reshape(W[nm].shape) for nm, g in zip(BIG, joined)}

    D_, NM, NV = {}, {}, {}
    for nm in BIG:
        as3 = lambda a: a.reshape((-1,) + a.shape[-2:])
        outs = adamw(as3(W[nm]), as3(G[nm]), as3(M1[nm]), as3(M2[nm]), "adamw_" + nm)
        D_[nm], NM[nm], NV[nm] = [o.reshape(W[nm].shape) for o in outs]
    small_rows, small_land = devices_wait(small_rows, small_land, small_sems, D_[BIG[-1]], "small_grads_wait")
    small_sum = device_sum(small_land, small_rows, place[2:3], "small_grads_sum")
    mine = lax.dynamic_slice_in_dim(small_sum, n_rep + chip * n_shard, n_shard, axis=0)
    g_small = jnp.concatenate([small_sum[:n_rep], mine], axis=0)
    small_names = [n for n, _ in REPLICATED_SMALL + SHARDED_SMALL]
    small_shapes = [s for _, s in REPLICATED_SMALL + SHARDED_SMALL]
    G.update(dict(zip(small_names, _unpack(g_small, small_shapes))))
    packs = [_pack([d[n] for n in small_names])[None] for d in (W, M1, M2)]
    outs = adamw(packs[0], g_small[None], packs[1], packs[2], "adamw_small")
    for dst, o in zip((D_, NM, NV), outs):
        dst.update(dict(zip(small_names, _unpack(o[0], small_shapes))))

    return (loss, grad_x[None], *[G[n] for n in WEIGHT_ORDER], *[D_[n] for n in WEIGHT_ORDER],
            *[NM[n] for n in WEIGHT_ORDER], *[NV[n] for n in WEIGHT_ORDER])


def _head_sum(dg):
    return jnp.sum(dg.reshape(N_HEADS, HEAD_DIM), axis=0, keepdims=True)
```

```python
import functools
import math

import numpy as np
import jax
import jax.numpy as jnp
from jax import lax
from jax.experimental import pallas as pl
from jax.experimental.pallas import tpu as pltpu

F32 = jnp.float32
BF16 = jnp.bfloat16

D_MODEL = 1024
N_HEADS = 16
HEAD_DIM = 64
A_WIDTH = 512
POOL_WINDOWS = (2, 4, 8, 16)
POOL_GROUP = 128
EVEN_IN = 2048
D_FF = 2816
DILATED_PAIRS = ((128, 1), (512, 4), (2048, 16))
ATT_BLOCK = 128
N_REL_BUCKETS = 32
REL_MAX_DISTANCE = 2048
EPS = 1e-6
MASK_VALUE = -1e30
ADAM_LR, ADAM_B1, ADAM_B2, ADAM_EPS, ADAM_WD, ADAM_STEP = 0.001, 0.9, 0.999, 1e-08, 0.01, 10

VMEM_LIMIT_BYTES = 48 * 1024 * 1024
N_CHIPS = 4
N_DEV = 8
MESH = pl.DeviceIdType.MESH


def _params(*sem):
    return pltpu.CompilerParams(dimension_semantics=sem if sem else None, vmem_limit_bytes=VMEM_LIMIT_BYTES)


def _sds(shape, dtype):
    return jax.ShapeDtypeStruct(tuple(shape), dtype)


def cast_bf16(x, name, tr=None):
    lead, (R, C) = x.shape[:-2], x.shape[-2:]
    n = int(np.prod(lead)) if lead else 1
    x3 = x.reshape((n, R, C))
    tr = tr or R

    def body(x_ref, o_ref):
        o_ref[...] = x_ref[...].astype(BF16)

    out = pl.pallas_call(
        body, name=name, grid=(n, R // tr),
        in_specs=[pl.BlockSpec((None, tr, C), lambda i, r: (i, r, 0))],
        out_specs=pl.BlockSpec((None, tr, C), lambda i, r: (i, r, 0)),
        out_shape=_sds((n, R, C), BF16), compiler_params=_params("parallel", "parallel"),
    )(x3)
    return out.reshape(lead + (R, C))


def rmsnorm_fwd(x, g, name, ts=512):
    S, Dm = x.shape

    def body(x_ref, g_ref, o_ref):
        xv = x_ref[...]
        r = lax.rsqrt(jnp.mean(xv * xv, axis=-1, keepdims=True) + EPS)
        o_ref[...] = ((xv * r) * g_ref[...]).astype(BF16)

    return pl.pallas_call(
        body, name=name, grid=(S // ts,),
        in_specs=[pl.BlockSpec((ts, Dm), lambda i: (i, 0)), pl.BlockSpec((1, Dm), lambda i: (0, 0))],
        out_specs=pl.BlockSpec((ts, Dm), lambda i: (i, 0)),
        out_shape=_sds((S, Dm), BF16), compiler_params=_params("parallel"),
    )(x, g)


def rmsnorm_bwd(x, g, dxn, dres, name, ts=512):
    S, Dm = x.shape

    def body(x_ref, g_ref, d_ref, r_ref, dx_ref, dxb_ref, dg_ref):
        xv = x_ref[...]
        dv = d_ref[...].astype(F32)
        r = lax.rsqrt(jnp.mean(xv * xv, axis=-1, keepdims=True) + EPS)
        gx = dv * g_ref[...]
        dot = jnp.sum(gx * xv, axis=-1, keepdims=True)
        dx = r_ref[...] + r * gx - xv * ((r * r * r) * (dot * (1.0 / Dm)))
        dx_ref[...] = dx
        dxb_ref[...] = dx.astype(BF16)
        part = jnp.sum(dv * (xv * r), axis=0, keepdims=True)

        @pl.when(pl.program_id(0) == 0)
        def _():
            dg_ref[...] = part

        @pl.when(pl.program_id(0) > 0)
        def _():
            dg_ref[...] += part

    row = pl.BlockSpec((ts, Dm), lambda i: (i, 0))
    vec = pl.BlockSpec((1, Dm), lambda i: (0, 0))
    return pl.pallas_call(
        body, name=name, grid=(S // ts,),
        in_specs=[row, vec, row, row], out_specs=[row, row, vec],
        out_shape=[_sds((S, Dm), F32), _sds((S, Dm), BF16), _sds((1, Dm), F32)], compiler_params=_params("arbitrary"),
    )(x, g, dxn, dres)


def mm_nn(a, w, name, layer=0, res=None, out_dtype=F32, tm=512):
    M, K = a.shape
    J, _, _, Ns = w.shape

    def body(*refs):
        a_ref, w_ref = refs[0], refs[1]
        o_ref = refs[-1]
        acc = jnp.dot(a_ref[...], w_ref[...], preferred_element_type=F32)
        if res is not None:
            acc = refs[2][...] + acc
        o_ref[...] = acc.astype(o_ref.dtype)

    in_specs = [pl.BlockSpec((tm, K), lambda j, m: (m, 0)),
                pl.BlockSpec((None, None, K, Ns), lambda j, m: (j, layer, 0, 0))]
    args = [a, w]
    if res is not None:
        in_specs.append(pl.BlockSpec((tm, Ns), lambda j, m: (m, j)))
        args.append(res)
    return pl.pallas_call(
        body, name=name, grid=(J, M // tm), in_specs=in_specs,
        out_specs=pl.BlockSpec((tm, Ns), lambda j, m: (m, j)),
        out_shape=_sds((M, J * Ns), out_dtype), compiler_params=_params("parallel", "parallel"),
    )(*args)


def mm_nt(dy, w, name, tr, layer=0, out_dtype=F32, tm=512):
    M = dy.shape[0]
    J, _, R, Ns = w.shape
    dims = (((1,), (1,)), ((), ()))

    def body(dy_ref, w_ref, o_ref):
        acc = None
        for j in range(J):
            p = lax.dot_general(dy_ref[:, j * Ns:(j + 1) * Ns], w_ref[j], dims, preferred_element_type=F32)
            acc = p if acc is None else acc + p
        o_ref[...] = acc.astype(o_ref.dtype)

    return pl.pallas_call(
        body, name=name, grid=(R // tr, M // tm),
        in_specs=[pl.BlockSpec((tm, J * Ns), lambda r, m: (m, 0)),
                  pl.BlockSpec((J, None, tr, Ns), lambda r, m: (0, layer, r, 0))],
        out_specs=pl.BlockSpec((tm, tr), lambda r, m: (m, r)),
        out_shape=_sds((M, R), out_dtype),
        compiler_params=_params("parallel", "parallel"),
    )(dy, w)


def mm_tn(a, dy, name, J, tk, tm=512):
    M, K = a.shape
    N = dy.shape[1]
    Ns = N // J
    n_m = M // tm
    dims = (((0,), (0,)), ((), ()))

    def body(a_ref, dy_ref, o_ref, acc_ref):
        p = lax.dot_general(a_ref[...], dy_ref[...], dims, preferred_element_type=F32)
        m = pl.program_id(1)

        @pl.when(m == 0)
        def _():
            acc_ref[...] = p

        @pl.when(m > 0)
        def _():
            acc_ref[...] += p

        @pl.when(m == n_m - 1)
        def _():
            for j in range(J):
                o_ref[j] = acc_ref[:, j * Ns:(j + 1) * Ns].astype(BF16)

    return pl.pallas_call(
        body, name=name, grid=(K // tk, n_m),
        in_specs=[pl.BlockSpec((tm, tk), lambda k, m: (m, k)), pl.BlockSpec((tm, N), lambda k, m: (m, 0))],
        out_specs=pl.BlockSpec((J, tk, Ns), lambda k, m: (0, k, 0)),
        out_shape=_sds((J, K, Ns), BF16), scratch_shapes=[pltpu.VMEM((tk, N), F32)],
        compiler_params=_params("parallel", "arbitrary"),
    )(a, dy)


HALO = 16


def _shift_down(x, s):
    return pltpu.roll(x, s, 0)


def _shift_up(x, s):
    return pltpu.roll(x, x.shape[0] - s, 0)


def _conv3(z, cw):
    return (_shift_down(z, 2) * cw[0:1] + _shift_down(z, 1) * cw[1:2]) + z * cw[2:3]


def _window_count(first_row, n, k):
    t = first_row + lax.broadcasted_iota(jnp.int32, (n, 1), 0)
    return jnp.clip(t + 1, 1, k).astype(F32)


def mixer_fwd(proj, conv_w, pool_w, pool_scale, name, ts=256):
    S = proj.shape[0]
    n = ts + HALO

    def body(pm_ref, pb_ref, cw_ref, pw_ref, ps_ref, o_ref):
        i = pl.program_id(0)
        before = jnp.where(i > 0, pb_ref[...], 0.0)
        ext = jnp.concatenate([before, pm_ref[...]], axis=0)
        cw = cw_ref[...]
        z = ext[:, 2 * A_WIDTH:3 * A_WIDTH] * ext[:, 0:A_WIDTH]
        cz = _conv3(z, cw)
        ya = pm_ref[:, A_WIDTH:2 * A_WIDTH] * cz[HALO:]
        o_ref[:, 0:A_WIDTH] = ya.astype(BF16)
        for g, k in enumerate(POOL_WINDOWS):
            lo = 3 * A_WIDTH + g * POOL_GROUP
            p = ext[:, lo:lo + POOL_GROUP]
            w = p
            s = 1
            while s < k:
                w = w + _shift_down(w, s)
                s *= 2
            pooled = w / _window_count(i * ts - HALO, n, k) - p
            yb = jnp.dot(pooled[HALO:].astype(BF16), pw_ref[g], preferred_element_type=F32)
            yb = yb * ps_ref[:, g * POOL_GROUP:(g + 1) * POOL_GROUP]
            o_ref[:, A_WIDTH + g * POOL_GROUP:A_WIDTH + (g + 1) * POOL_GROUP] = yb.astype(BF16)

    hb = ts // HALO
    return pl.pallas_call(
        body, name=name, grid=(S // ts,),
        in_specs=[
            pl.BlockSpec((ts, EVEN_IN), lambda i: (i, 0)),
            pl.BlockSpec((HALO, EVEN_IN), lambda i: (jnp.maximum(i * hb - 1, 0), 0)),
            pl.BlockSpec((3, A_WIDTH), lambda i: (0, 0)),
            pl.BlockSpec((4, POOL_GROUP, POOL_GROUP), lambda i: (0, 0, 0)),
            pl.BlockSpec((1, 4 * POOL_GROUP), lambda i: (0, 0)),
        ],
        out_specs=pl.BlockSpec((ts, D_MODEL), lambda i: (i, 0)),
        out_shape=_sds((S, D_MODEL), BF16), compiler_params=_params("parallel"),
    )(proj, proj, conv_w, pool_w, pool_scale)


def mixer_bwd(proj, dmix, conv_w, pool_w, pool_scale, name, ts=256):
    S = proj.shape[0]
    n = ts + 2 * HALO
    nt = S // ts
    tn_dims = (((0,), (0,)), ((), ()))
    nt_dims = (((1,), (1,)), ((), ()))

    def body(pm_ref, pb_ref, pa_ref, dm_ref, da_ref, cw_ref, pw_ref, ps_ref, o_ref, dcw_ref, dpw_ref, dps_ref):
        i = pl.program_id(0)
        last = i == nt - 1
        before = jnp.where(i > 0, pb_ref[...], 0.0)
        after = jnp.where(last, 0.0, pa_ref[...])
        ext = jnp.concatenate([before, pm_ref[...], after], axis=0)
        dafter = jnp.where(last, 0.0, da_ref[...])
        dext = jnp.concatenate([jnp.zeros((HALO, D_MODEL), F32), dm_ref[...], dafter], axis=0)
        cw = cw_ref[...]
        main = slice(HALO, HALO + ts)

        @pl.when(i == 0)
        def _():
            dcw_ref[...] = jnp.zeros_like(dcw_ref)
            dpw_ref[...] = jnp.zeros_like(dpw_ref)
            dps_ref[...] = jnp.zeros_like(dps_ref)

        h, gb, gc = ext[:, 0:A_WIDTH], ext[:, A_WIDTH:2 * A_WIDTH], ext[:, 2 * A_WIDTH:3 * A_WIDTH]
        z = gc * h
        z1, z2 = _shift_down(z, 1), _shift_down(z, 2)
        cz = (z2 * cw[0:1] + z1 * cw[1:2]) + z * cw[2:3]
        dya = dext[:, 0:A_WIDTH]
        dcz = dya * gb
        dz = dcz * cw[2:3] + _shift_up(dcz, 1) * cw[1:2] + _shift_up(dcz, 2) * cw[0:1]
        o_ref[:, 0:A_WIDTH] = (dz * gc)[main].astype(BF16)
        o_ref[:, A_WIDTH:2 * A_WIDTH] = (dya * cz)[main].astype(BF16)
        o_ref[:, 2 * A_WIDTH:3 * A_WIDTH] = (dz * h)[main].astype(BF16)
        dczm = dcz[main]
        dcw_ref[0:1, :] += jnp.sum(dczm * z2[main], axis=0, keepdims=True)
        dcw_ref[1:2, :] += jnp.sum(dczm * z1[main], axis=0, keepdims=True)
        dcw_ref[2:3, :] += jnp.sum(dczm * z[main], axis=0, keepdims=True)

        for g, k in enumerate(POOL_WINDOWS):
            lo = 3 * A_WIDTH + g * POOL_GROUP
            cols = slice(g * POOL_GROUP, (g + 1) * POOL_GROUP)
            p = ext[:, lo:lo + POOL_GROUP]
            w = p
            s = 1
            while s < k:
                w = w + _shift_down(w, s)
                s *= 2
            cnt = _window_count(i * ts - HALO, n, k)
            pooled = (w / cnt - p)[main].astype(BF16)
            dyb = dext[:, A_WIDTH + g * POOL_GROUP:A_WIDTH + (g + 1) * POOL_GROUP]
            e = dyb * ps_ref[:, cols]
            pre = jnp.dot(pooled, pw_ref[g], preferred_element_type=F32)
            dps_ref[:, cols] += jnp.sum(dyb[main] * pre, axis=0, keepdims=True)
            dpw_ref[g] += lax.dot_general(pooled, e[main].astype(BF16), tn_dims, preferred_element_type=F32)
            dpooled = lax.dot_general(e.astype(BF16), pw_ref[g], nt_dims, preferred_element_type=F32)
            q = dpooled / cnt
            a = q
            s = 1
            while s < k:
                a = a + _shift_up(a, s)
                s *= 2
            o_ref[:, lo:lo + POOL_GROUP] = (a - dpooled)[main].astype(BF16)

    hb = ts // HALO
    nh = S // HALO
    before_map = lambda i: (jnp.maximum(i * hb - 1, 0), 0)
    after_map = lambda i: (jnp.minimum((i + 1) * hb, nh - 1), 0)
    full = lambda *shape: pl.BlockSpec(shape, lambda i: (0,) * len(shape))
    return pl.pallas_call(
        body, name=name, grid=(nt,),
        in_specs=[
            pl.BlockSpec((ts, EVEN_IN), lambda i: (i, 0)),
            pl.BlockSpec((HALO, EVEN_IN), before_map),
            pl.BlockSpec((HALO, EVEN_IN), after_map),
            pl.BlockSpec((ts, D_MODEL), lambda i: (i, 0)),
            pl.BlockSpec((HALO, D_MODEL), after_map),
            full(3, A_WIDTH), full(4, POOL_GROUP, POOL_GROUP), full(1, 4 * POOL_GROUP),
        ],
        out_specs=[pl.BlockSpec((ts, EVEN_IN), lambda i: (i, 0)), full(3, A_WIDTH), full(4, POOL_GROUP, POOL_GROUP),
                   full(1, 4 * POOL_GROUP)],
        out_shape=[_sds((S, EVEN_IN), BF16), _sds((3, A_WIDTH), F32), _sds((4, POOL_GROUP, POOL_GROUP), F32),
                   _sds((1, 4 * POOL_GROUP), F32)],
        compiler_params=_params("arbitrary"),
    )(proj, proj, proj, dmix, dmix, conv_w, pool_w, pool_scale)


FFN_HALO = 16
FFN_TC = 1408


def glu_fwd(up, conv_w, conv_b, name, ts=256):
    S = up.shape[0]
    nc = D_FF // FFN_TC

    def body(gm_ref, gb_ref, um_ref, ub_ref, cwg_ref, cwu_ref, cbg_ref, cbu_ref, ug_ref, uu_ref, o_ref):
        i = pl.program_id(0)

        def conv(m_ref, b_ref, cw_ref, cb_ref):
            before = jnp.where(i > 0, b_ref[...].astype(F32), 0.0)
            ext = jnp.concatenate([before, m_ref[...].astype(F32)], axis=0)
            return _conv3(ext, cw_ref[...])[FFN_HALO:] + cb_ref[...]

        gate = conv(gm_ref, gb_ref, cwg_ref, cbg_ref)
        upv = conv(um_ref, ub_ref, cwu_ref, cbu_ref)
        ug_ref[...] = gate.astype(BF16)
        uu_ref[...] = upv.astype(BF16)
        o_ref[...] = ((gate * (1.0 / (1.0 + jnp.exp(-gate)))) * upv).astype(BF16)

    hb = ts // FFN_HALO
    main = lambda off: pl.BlockSpec((ts, FFN_TC), lambda i, c: (i, c + off))
    halo = lambda off: pl.BlockSpec((FFN_HALO, FFN_TC), lambda i, c: (jnp.maximum(i * hb - 1, 0), c + off))
    cw = lambda off: pl.BlockSpec((3, FFN_TC), lambda i, c: (0, c + off))
    cb = lambda off: pl.BlockSpec((1, FFN_TC), lambda i, c: (0, c + off))
    ug, uu, act = pl.pallas_call(
        body, name=name, grid=(S // ts, nc),
        in_specs=[main(0), halo(0), main(nc), halo(nc), cw(0), cw(nc), cb(0), cb(nc)],
        out_specs=[pl.BlockSpec((ts, FFN_TC), lambda i, c: (i, c))] * 3,
        out_shape=[_sds((S, D_FF), BF16)] * 3, compiler_params=_params("parallel", "parallel"),
    )(up, up, up, up, conv_w, conv_w, conv_b, conv_b)
    return (ug, uu), act


def glu_bwd(up, u, da, conv_w, name, ts=256):
    S = up.shape[0]
    nc = D_FF // FFN_TC
    nt = S // ts
    W = 2 * D_FF

    def body(x_ref, gm_ref, ga_ref, um_ref, ua_ref, dm_ref, da_ref, cw_ref, dx_ref, dcw_ref, dcb_ref):
        i = pl.program_id(0)
        last = i == nt - 1

        @pl.when(i == 0)
        def _():
            dcw_ref[...] = jnp.zeros_like(dcw_ref)
            dcb_ref[...] = jnp.zeros_like(dcb_ref)

        def rows(m_ref, a_ref, cols):
            return jnp.concatenate([m_ref[:, cols], a_ref[:, cols]], axis=0).astype(F32)

        def back(d, cols):
            cw = cw_ref[:, cols]
            d1, d2 = _shift_up(d, 1), _shift_up(d, 2)
            dx_ref[:, cols] = ((d * cw[2:3] + d1 * cw[1:2]) + d2 * cw[0:1])[:ts].astype(BF16)
            x = x_ref[:, cols].astype(F32)
            dcb_ref[:, cols] += jnp.sum(d[:ts], axis=0, keepdims=True)
            dcw_ref[0:1, cols] += jnp.sum(d2[:ts] * x, axis=0, keepdims=True)
            dcw_ref[1:2, cols] += jnp.sum(d1[:ts] * x, axis=0, keepdims=True)
            dcw_ref[2:3, cols] += jnp.sum(d[:ts] * x, axis=0, keepdims=True)

        for c in range(nc):
            cols = slice(c * FFN_TC, (c + 1) * FFN_TC)
            ug, uu = rows(gm_ref, ga_ref, cols), rows(um_ref, ua_ref, cols)
            dae = rows(dm_ref, da_ref, cols)
            dae = jnp.where(last & (lax.broadcasted_iota(jnp.int32, dae.shape, 0) >= ts), 0.0, dae)
            sg = 1.0 / (1.0 + jnp.exp(-ug))
            duu = dae * (ug * sg)
            dug = (dae * uu) * (sg * (1.0 + ug * (1.0 - sg)))
            back(dug, cols)
            back(duu, slice(D_FF + c * FFN_TC, D_FF + (c + 1) * FFN_TC))

    hb = ts // FFN_HALO
    nh = S // FFN_HALO
    after_map = lambda i: (jnp.minimum((i + 1) * hb, nh - 1), 0)
    main = pl.BlockSpec((ts, D_FF), lambda i: (i, 0))
    after = pl.BlockSpec((FFN_HALO, D_FF), after_map)
    return pl.pallas_call(
        body, name=name, grid=(nt,),
        in_specs=[pl.BlockSpec((ts, W), lambda i: (i, 0)), main, after, main, after, main, after,
                  pl.BlockSpec((3, W), lambda i: (0, 0))],
        out_specs=[pl.BlockSpec((ts, W), lambda i: (i, 0)), pl.BlockSpec((3, W), lambda i: (0, 0)),
                   pl.BlockSpec((1, W), lambda i: (0, 0))],
        out_shape=[_sds((S, W), BF16), _sds((3, W), F32), _sds((1, W), F32)],
        compiler_params=_params("arbitrary"),
    )(up, u[0], u[0], u[1], u[1], da, da, conv_w)


def _head_mean_matrix():
    h = np.arange(D_MODEL) // HEAD_DIM
    return jnp.asarray((h[:, None] == h[None, :]).astype(np.float32) / HEAD_DIM, dtype=BF16)


def _head_mean(v, gm):
    return jnp.dot(v.astype(BF16), gm, preferred_element_type=F32)


def qknorm_fwd(qkv, gqk, name, ts=512):
    S = qkv.shape[0]

    def body(x_ref, g_ref, gm_ref, o_ref):
        part = pl.program_id(0)
        x = x_ref[...]

        @pl.when(part < 2)
        def _():
            r = lax.rsqrt(_head_mean(x * x, gm_ref[...]) + EPS)
            o_ref[...] = ((x * r) * g_ref[...]).astype(BF16)

        @pl.when(part == 2)
        def _():
            o_ref[...] = x.astype(BF16)

    return pl.pallas_call(
        body, name=name, grid=(3, S // ts),
        in_specs=[pl.BlockSpec((ts, D_MODEL), lambda p, i: (i, p)), pl.BlockSpec((None, 1, D_MODEL), lambda p, i: (p, 0, 0)),
                  pl.BlockSpec((D_MODEL, D_MODEL), lambda p, i: (0, 0))],
        out_specs=pl.BlockSpec((ts, D_MODEL), lambda p, i: (i, p)),
        out_shape=_sds((S, 3 * D_MODEL), BF16), compiler_params=_params("parallel", "parallel"),
    )(qkv, gqk, _head_mean_matrix())


def qknorm_bwd(qkv, dq, dk, dv, gqk, name, ts=256):
    S = qkv.shape[0]

    def body(x_ref, dq_ref, dk_ref, dv_ref, g_ref, gm_ref, o_ref, dg_ref):
        @pl.when(pl.program_id(0) == 0)
        def _():
            dg_ref[...] = jnp.zeros_like(dg_ref)

        gm = gm_ref[...]
        for part, d_ref in enumerate((dq_ref, dk_ref)):
            cols = slice(part * D_MODEL, (part + 1) * D_MODEL)
            x = x_ref[:, cols]
            d = d_ref[...]
            r = lax.rsqrt(_head_mean(x * x, gm) + EPS)
            gx = d * g_ref[part]
            o_ref[:, cols] = (r * gx - x * ((r * r * r) * _head_mean(gx * x, gm))).astype(BF16)
            dg_ref[part] += jnp.sum(d * (x * r), axis=0, keepdims=True)
        o_ref[:, 2 * D_MODEL:] = dv_ref[...].astype(BF16)

    row = pl.BlockSpec((ts, D_MODEL), lambda i: (i, 0))
    wide = pl.BlockSpec((ts, 3 * D_MODEL), lambda i: (i, 0))
    gains = pl.BlockSpec((3, 1, D_MODEL), lambda i: (0, 0, 0))
    return pl.pallas_call(
        body, name=name, grid=(S // ts,),
        in_specs=[wide, row, row, row, gains, pl.BlockSpec((D_MODEL, D_MODEL), lambda i: (0, 0))],
        out_specs=[wide, gains],
        out_shape=[_sds((S, 3 * D_MODEL), BF16), _sds((3, 1, D_MODEL), F32)],
        compiler_params=_params("arbitrary"),
    )(qkv, dq, dk, dv, gqk, _head_mean_matrix())


RESIDUES = 16


def _block_order(dil):
    runs = RESIDUES // dil
    slot = np.arange(ATT_BLOCK)
    return (slot % (ATT_BLOCK // runs)) * runs + slot // (ATT_BLOCK // runs)


def _bucket_tables():
    n = ATT_BLOCK
    max_exact = N_REL_BUCKETS // 2
    buckets, valids = [], []
    for _, dil in DILATED_PAIRS:
        order = _block_order(dil)
        a = order[:, None]
        c = np.concatenate([order, n + order])[None, :]
        first_half = (np.arange(2 * n) < n)[None, :]
        rel = a + n - c
        band = (rel >= 0) & (rel <= n)
        dist = np.clip(rel, 0, n) * dil
        dd = np.maximum(dist, 1).astype(np.float32)
        large = max_exact + (np.log(dd / np.float32(max_exact)) / np.float32(math.log(REL_MAX_DISTANCE / max_exact))
                             * np.float32(N_REL_BUCKETS - max_exact)).astype(np.int32)
        large = np.minimum(large, N_REL_BUCKETS - 1)
        buckets.append(np.where(dist < max_exact, dist, large).reshape(1, -1))
        valids.append(np.stack([(band & ~first_half).reshape(1, -1), band.reshape(1, -1)]))
    return np.stack(buckets).astype(np.int32), np.stack(valids).astype(np.int32)


BIAS_CHUNK = 8192


def _split3(x):
    a = x.astype(BF16)
    r = x - a.astype(F32)
    b = r.astype(BF16)
    c = (r - b.astype(F32)).astype(BF16)
    return a, b, c


def bias_expand(rel_bias_t, name):
    bucket, valid = _bucket_tables()
    nq = bucket.shape[-1]

    def body(t_ref, b_ref, v_ref, o_ref):
        onehot = (lax.broadcasted_iota(jnp.int32, (N_REL_BUCKETS, BIAS_CHUNK), 0) == b_ref[...]).astype(BF16)
        acc = None
        for term in _split3(t_ref[...]):
            p = jnp.dot(term, onehot, preferred_element_type=F32)
            acc = p if acc is None else acc + p
        o_ref[...] = jnp.where(v_ref[...] > 0, acc, MASK_VALUE)

    return pl.pallas_call(
        body, name=name, grid=(3, 2, nq // BIAS_CHUNK),
        in_specs=[pl.BlockSpec((N_HEADS, N_REL_BUCKETS), lambda b, v, c: (0, 0)),
                  pl.BlockSpec((None, 1, BIAS_CHUNK), lambda b, v, c: (b, 0, c)),
                  pl.BlockSpec((None, None, 1, BIAS_CHUNK), lambda b, v, c: (b, v, 0, c))],
        out_specs=pl.BlockSpec((None, None, N_HEADS, BIAS_CHUNK), lambda b, v, c: (b, v, 0, c)),
        out_shape=_sds((3, 2, N_HEADS, nq), F32), compiler_params=_params("parallel", "parallel", "parallel"),
    )(rel_bias_t, jnp.asarray(bucket), jnp.asarray(valid))


def bias_reduce(dbias, name):
    bucket, _ = _bucket_tables()
    nq = bucket.shape[-1]
    dims = (((1,), (1,)), ((), ()))

    def body(d_ref, b_ref, o_ref):
        onehot = (lax.broadcasted_iota(jnp.int32, (N_REL_BUCKETS, BIAS_CHUNK), 0) == b_ref[...]).astype(BF16)
        acc = None
        for term in _split3(d_ref[...]):
            p = lax.dot_general(term, onehot, dims, preferred_element_type=F32)
            acc = p if acc is None else acc + p

        @pl.when(pl.program_id(1) == 0)
        def _():
            o_ref[...] = acc

        @pl.when(pl.program_id(1) > 0)
        def _():
            o_ref[...] += acc

    return pl.pallas_call(
        body, name=name, grid=(3, nq // BIAS_CHUNK),
        in_specs=[pl.BlockSpec((None, N_HEADS, BIAS_CHUNK), lambda b, c: (b, 0, c)),
                  pl.BlockSpec((None, 1, BIAS_CHUNK), lambda b, c: (b, 0, c))],
        out_specs=pl.BlockSpec((None, N_HEADS, N_REL_BUCKETS), lambda b, c: (b, 0, 0)),
        out_shape=_sds((3, N_HEADS, N_REL_BUCKETS), F32), compiler_params=_params("parallel", "arbitrary"),
    )(dbias, jnp.asarray(bucket))


PAIR = 2 * HEAD_DIM
N_PAIRS = N_HEADS // 2
_NT = (((1,), (1,)), ((), ()))
_TN = (((0,), (0,)), ((), ()))


def _low_lanes(shape):
    return lax.broadcasted_iota(jnp.int32, shape, 1) < HEAD_DIM


ATTN_VMEM_LIMIT_BYTES = 56 * 1024 * 1024
BRANCH_ORDER = (2, 1, 0)


def _regroup(dst, src, L16):
    for r in range(RESIDUES):
        dst[pl.ds(r * L16, L16), :] = src[pl.ds(r, L16, stride=RESIDUES), :]


def _ungroup(dst, src, L16):
    for r in range(RESIDUES):
        dst[pl.ds(r, L16, stride=RESIDUES), :] = src[pl.ds(r * L16, L16), :]


def _branch_geometry(branch, S):
    dil = DILATED_PAIRS[branch][1]
    runs = RESIDUES // dil
    return dil, runs, ATT_BLOCK // runs, S // dil // ATT_BLOCK


def _block_rows(it, branch, S):
    dil, runs, run_len, n_blocks = _branch_geometry(branch, S)
    L16 = S // RESIDUES
    r, b = it // n_blocks, it % n_blocks
    prev = jnp.maximum(b - 1, 0)
    cur_rows = [pl.multiple_of((j * dil + r) * L16 + run_len * b, 8) for j in range(runs)]
    prev_rows = [pl.multiple_of((j * dil + r) * L16 + run_len * prev, 8) for j in range(runs)]
    return cur_rows, prev_rows, jnp.minimum(b, 1)


def _load_block(ref, rows, run_len):
    parts = [ref[pl.ds(o, run_len), :] for o in rows]
    return parts[0] if len(parts) == 1 else jnp.concatenate(parts, axis=0)


def _store_block(ref, rows, run_len, value, add=False):
    for j, o in enumerate(rows):
        part = value[j * run_len:(j + 1) * run_len]
        if add:
            ref[pl.ds(o, run_len), :] += part
        else:
            ref[pl.ds(o, run_len), :] = part


ATTN_FWD_UNROLL = 8
ATTN_BWD_UNROLL = 4


def _stack_heads(x, low):
    zero = jnp.zeros_like(x)
    return jnp.concatenate([jnp.where(low, x, zero), jnp.where(low, zero, x)], axis=0)


def _unstack_heads(y, low):
    return jnp.where(low, y[:ATT_BLOCK], y[ATT_BLOCK:])


def attn_fwd(qkvn, bias, name):
    S = qkvn.shape[0]
    L16 = S // RESIDUES
    n_iter = S // ATT_BLOCK

    def body(q_ref, k_ref, v_ref, b_ref, o_ref, lse_ref, stage, qp, kp, vp, acc_s, m_s, l_s):
        for src, dst in ((q_ref, qp), (k_ref, kp), (v_ref, vp)):
            stage[...] = src[...].astype(F32)
            _regroup(dst, stage, L16)
        low = _low_lanes((ATT_BLOCK, PAIR))

        for branch in BRANCH_ORDER:
            _, _, run_len, _ = _branch_geometry(branch, S)
            first = branch == BRANCH_ORDER[0]

            def step(it, carry, branch=branch, run_len=run_len, first=first):
                cur, prev, variant = _block_rows(it, branch, S)
                q = _load_block(qp, cur, run_len).astype(BF16)
                k = jnp.concatenate([_load_block(kp, prev, run_len), _load_block(kp, cur, run_len)], axis=0).astype(BF16)
                v = jnp.concatenate([_load_block(vp, prev, run_len), _load_block(vp, cur, run_len)], axis=0).astype(BF16)
                s = lax.dot_general(_stack_heads(q, low), k, _NT, preferred_element_type=F32) * (HEAD_DIM ** -0.5)
                s = s + b_ref[2 * branch + variant].reshape(2 * ATT_BLOCK, 2 * ATT_BLOCK)
                mx = jnp.max(s, axis=-1, keepdims=True)
                p = jnp.exp(s - mx)
                den = jnp.sum(p, axis=-1, keepdims=True)
                pv = jnp.dot(p.astype(BF16), v, preferred_element_type=F32)
                acc = _unstack_heads(pv, low)
                m = _unstack_heads(mx, low)
                l = _unstack_heads(den, low)
                if not first:
                    m_old = _load_block(m_s, cur, run_len)
                    m_new = jnp.maximum(m_old, m)
                    a_old, a_new = jnp.exp(m_old - m_new), jnp.exp(m - m_new)
                    acc = _load_block(acc_s, cur, run_len) * a_old + acc * a_new
                    l = _load_block(l_s, cur, run_len) * a_old + l * a_new
                    m = m_new
                _store_block(acc_s, cur, run_len, acc)
                _store_block(m_s, cur, run_len, m)
                _store_block(l_s, cur, run_len, l)
                return carry

            lax.fori_loop(0, n_iter, step, 0, unroll=ATTN_FWD_UNROLL)

        acc_s[...] = acc_s[...] / l_s[...]
        _ungroup(stage, acc_s, L16)
        o_ref[...] = stage[...].astype(BF16)
        m_s[...] = m_s[...] + jnp.log(l_s[...])
        _ungroup(lse_ref, m_s, L16)

    col = lambda part: pl.BlockSpec((S, PAIR), lambda hp: (0, part * N_PAIRS + hp))
    out = pl.BlockSpec((S, PAIR), lambda hp: (0, hp))
    return pl.pallas_call(
        body, name=name, grid=(N_PAIRS,),
        in_specs=[col(0), col(1), col(2), pl.BlockSpec((6, 2, ATT_BLOCK, 2 * ATT_BLOCK), lambda hp: (0, hp, 0, 0))],
        out_specs=[out, out], out_shape=[_sds((S, D_MODEL), BF16), _sds((S, D_MODEL), F32)],
        scratch_shapes=[pltpu.VMEM((S, PAIR), F32)] * 7,
        compiler_params=pltpu.CompilerParams(dimension_semantics=("parallel",), vmem_limit_bytes=ATTN_VMEM_LIMIT_BYTES),
    )(qkvn, qkvn, qkvn, bias)


def attn_bwd(qkvn, att, datt, lse, bias, name):
    S = qkvn.shape[0]
    L16 = S // RESIDUES
    n_iter = S // ATT_BLOCK
    TILE = 512

    def body(q_ref, k_ref, v_ref, o_ref, do_ref, lse_ref, b_ref, dq_ref, dk_ref, dv_ref, db_ref,
             qp, kp, vp, dop, ldp, dqp, dkp, dvp):
        stage = dqp
        for src, dst in ((q_ref, qp), (k_ref, kp), (v_ref, vp), (do_ref, dop)):
            stage[...] = src[...].astype(F32)
            _regroup(dst, stage, L16)

        def pack(i, carry):
            rows = pl.ds(pl.multiple_of(i * TILE, TILE), TILE)
            low = _low_lanes((TILE, PAIR))
            lane = lax.broadcasted_iota(jnp.int32, (TILE, PAIR), 1)
            prod = do_ref[rows, :].astype(F32) * o_ref[rows, :].astype(F32)
            d0 = jnp.sum(jnp.where(low, prod, 0.0), axis=-1, keepdims=True)
            d1 = jnp.sum(jnp.where(low, 0.0, prod), axis=-1, keepdims=True)
            stage[rows, :] = jnp.where((lane & (HEAD_DIM // 2)) == 0, lse_ref[rows, :], jnp.where(low, d0, d1))
            return carry

        lax.fori_loop(0, S // TILE, pack, 0)
        _regroup(ldp, stage, L16)
        dqp[...] = jnp.zeros_like(dqp)
        dkp[...] = jnp.zeros_like(dkp)
        dvp[...] = jnp.zeros_like(dvp)
        db_ref[...] = jnp.zeros_like(db_ref)
        low = _low_lanes((ATT_BLOCK, PAIR))

        for branch in BRANCH_ORDER:
            _, _, run_len, _ = _branch_geometry(branch, S)

            def step(it, carry, branch=branch, run_len=run_len):
                cur, prev, variant = _block_rows(it, branch, S)
                q = _load_block(qp, cur, run_len).astype(BF16)
                dout = _load_block(dop, cur, run_len).astype(BF16)
                ld = _load_block(ldp, cur, run_len)
                k = jnp.concatenate([_load_block(kp, prev, run_len), _load_block(kp, cur, run_len)], axis=0).astype(BF16)
                v = jnp.concatenate([_load_block(vp, prev, run_len), _load_block(vp, cur, run_len)], axis=0).astype(BF16)
                half = HEAD_DIM // 2
                lse2 = jnp.concatenate([ld[:, 0:1], ld[:, HEAD_DIM:HEAD_DIM + 1]], axis=0)
                delta2 = jnp.concatenate([ld[:, half:half + 1], ld[:, HEAD_DIM + half:HEAD_DIM + half + 1]], axis=0)
                q2, do2 = _stack_heads(q, low), _stack_heads(dout, low)
                s = lax.dot_general(q2, k, _NT, preferred_element_type=F32) * (HEAD_DIM ** -0.5)
                p = jnp.exp(s + b_ref[2 * branch + variant].reshape(2 * ATT_BLOCK, 2 * ATT_BLOCK) - lse2)
                dp = lax.dot_general(do2, v, _NT, preferred_element_type=F32)
                ds = p * (dp - delta2)
                db_ref[branch] += ds.reshape(2, ATT_BLOCK, 2 * ATT_BLOCK)
                dsb = (ds * (HEAD_DIM ** -0.5)).astype(BF16)
                dq = _unstack_heads(jnp.dot(dsb, k, preferred_element_type=F32), low)
                dk = lax.dot_general(dsb, q2, _TN, preferred_element_type=F32)
                dv = lax.dot_general(p.astype(BF16), do2, _TN, preferred_element_type=F32)
                _store_block(dqp, cur, run_len, dq, add=True)
                _store_block(dkp, prev, run_len, dk[:ATT_BLOCK], add=True)
                _store_block(dvp, prev, run_len, dv[:ATT_BLOCK], add=True)
                _store_block(dkp, cur, run_len, dk[ATT_BLOCK:], add=True)
                _store_block(dvp, cur, run_len, dv[ATT_BLOCK:], add=True)
                return carry

            lax.fori_loop(0, n_iter, step, 0, unroll=ATTN_BWD_UNROLL)

        _ungroup(dq_ref, dqp, L16)
        _ungroup(dk_ref, dkp, L16)
        _ungroup(dv_ref, dvp, L16)

    col = lambda part: pl.BlockSpec((S, PAIR), lambda hp: (0, part * N_PAIRS + hp))
    one = pl.BlockSpec((S, PAIR), lambda hp: (0, hp))
    return pl.pallas_call(
        body, name=name, grid=(N_PAIRS,),
        in_specs=[col(0), col(1), col(2), one, one, one,
                  pl.BlockSpec((6, 2, ATT_BLOCK, 2 * ATT_BLOCK), lambda hp: (0, hp, 0, 0))],
        out_specs=[one, one, one, pl.BlockSpec((3, 2, ATT_BLOCK, 2 * ATT_BLOCK), lambda hp: (0, hp, 0, 0))],
        out_shape=[_sds((S, D_MODEL), F32)] * 3 + [_sds((3, N_HEADS, ATT_BLOCK, 2 * ATT_BLOCK), F32)],
        scratch_shapes=[pltpu.VMEM((S, PAIR), F32)] * 8,
        compiler_params=pltpu.CompilerParams(dimension_semantics=("parallel",), vmem_limit_bytes=ATTN_VMEM_LIMIT_BYTES),
    )(qkvn, qkvn, qkvn, att, datt, lse, bias)


def loss_grad(y, target, name, ts=512):
    S, Dm = y.shape

    def body(y_ref, t_ref, d_ref, db_ref, s_ref):
        e = y_ref[...] - t_ref[...]
        d = e * (1.0 / Dm)
        d_ref[...] = d
        db_ref[...] = d.astype(BF16)
        part = jnp.sum(e * e, axis=0, keepdims=True)

        @pl.when(pl.program_id(0) == 0)
        def _():
            s_ref[...] = part

        @pl.when(pl.program_id(0) > 0)
        def _():
            s_ref[...] += part

    row = pl.BlockSpec((ts, Dm), lambda i: (i, 0))
    vec = pl.BlockSpec((1, Dm), lambda i: (0, 0))
    return pl.pallas_call(
        body, name=name, grid=(S // ts,), in_specs=[row, row], out_specs=[row, row, vec],
        out_shape=[_sds((S, Dm), F32), _sds((S, Dm), BF16), _sds((1, Dm), F32)], compiler_params=_params("arbitrary"),
    )(y, target)


def adamw(w, g, m, v, name):
    n, R, C = w.shape

    def body(w_ref, g_ref, m_ref, v_ref, d_ref, nm_ref, nv_ref):
        gv = g_ref[...]
        m2 = ADAM_B1 * m_ref[...] + (1.0 - ADAM_B1) * gv
        v2 = ADAM_B2 * v_ref[...] + (1.0 - ADAM_B2) * (gv * gv)
        m_hat = m2 / (1.0 - ADAM_B1 ** ADAM_STEP)
        v_hat = v2 / (1.0 - ADAM_B2 ** ADAM_STEP)
        d_ref[...] = -ADAM_LR * (m_hat / (jnp.sqrt(v_hat) + ADAM_EPS) + ADAM_WD * w_ref[...])
        nm_ref[...] = m2
        nv_ref[...] = v2

    tr = R
    while tr * C * 4 > (1 << 21) and tr % 16 == 0:
        tr //= 2
    spec = pl.BlockSpec((None, tr, C), lambda i, r: (i, r, 0))
    return pl.pallas_call(
        body, name=name, grid=(n, R // tr), in_specs=[spec] * 4, out_specs=[spec] * 3,
        out_shape=[_sds((n, R, C), F32)] * 3, compiler_params=_params("parallel", "parallel"),
    )(w, g, m, v)


ANY = pl.BlockSpec(memory_space=pl.ANY)


def _coords():
    return lax.axis_index("x"), lax.axis_index("y"), lax.axis_index("c")


def _other_chips(mx, my):
    return [(1 - mx, my), (mx, 1 - my), (1 - mx, 1 - my)]


def _remote(src, dst, send, recv, dev):
    return pltpu.make_async_remote_copy(src_ref=src, dst_ref=dst, send_sem=send, recv_sem=recv, device_id=dev,
                                        device_id_type=MESH)


def allgather_devices(x, name):
    R, C = x.shape

    def body(x_ref, o_ref, send, recv, local_sem):
        mx, my, mc = _coords()
        me = 4 * mx + 2 * my + mc
        local = pltpu.make_async_copy(x_ref, o_ref.at[me], local_sem)
        local.start()
        peers = []
        for k in range(1, N_DEV):
            px = 1 - mx if k & 4 else mx
            py = 1 - my if k & 2 else my
            pc = 1 - mc if k & 1 else mc
            peers.append((px, py, pc))
        sends = [_remote(x_ref, o_ref.at[me], send.at[k], recv.at[k], p) for k, p in enumerate(peers)]
        for cp in sends:
            cp.start()
        for k, (px, py, pc) in enumerate(peers):
            _remote(x_ref, o_ref.at[4 * px + 2 * py + pc], send.at[k], recv.at[k], (px, py, pc)).wait_recv()
        for cp in sends:
            cp.wait_send()
        local.wait()

    return pl.pallas_call(
        body, name=name, in_specs=[ANY], out_specs=ANY, out_shape=_sds((N_DEV, R, C), x.dtype),
        scratch_shapes=[pltpu.SemaphoreType.DMA((N_DEV - 1,)), pltpu.SemaphoreType.DMA((N_DEV - 1,)),
                        pltpu.SemaphoreType.DMA],
    )(x)


HBM = pl.BlockSpec(memory_space=pltpu.HBM)
SEM = pl.BlockSpec(memory_space=pltpu.SEMAPHORE)
_SPLIT_COPY = pltpu.CompilerParams(has_side_effects=pltpu.SideEffectType.DATAFLOW_SIDE_EFFECTING)


def _in_hbm(a):
    return pltpu.with_memory_space_constraint(a, pltpu.HBM)


def cast_into_slot(w, layer, chip_core, name):
    _, _, hR, C = w.shape

    def body(s_ref, w_ref, o_ref):
        del s_ref
        o_ref[...] = w_ref[...].astype(BF16)

    grid_spec = pltpu.PrefetchScalarGridSpec(
        num_scalar_prefetch=1, grid=(2,),
        in_specs=[pl.BlockSpec((None, None, hR, C), lambda h, s: (layer, h, 0, 0))],
        out_specs=pl.BlockSpec((None, None, hR, C), lambda h, s: (s[0], h, 0, 0)))
    return pl.pallas_call(body, name=name, grid_spec=grid_spec, out_shape=_sds((N_CHIPS, 2, hR, C), BF16),
                          compiler_params=_params("parallel"))(chip_core, w)


def gather_start(lands, groups, name):
    n = len(lands)
    n_groups = len(groups)

    def body(*refs):
        ins = refs[:n]
        sems = refs[n:n + 2 * n_groups]
        token = refs[-1]
        mx, my, mc = _coords()
        chip = 2 * mx + my
        for g, members in enumerate(groups):
            send, recv = sems[2 * g], sems[2 * g + 1]
            for i, a in enumerate(members):
                mine = ins[a].at[chip, mc]
                for k, (px, py) in enumerate(_other_chips(mx, my)):
                    _remote(mine, mine, send.at[3 * i + k], recv.at[3 * i + k], (px, py, mc)).start()
        token[...] = jnp.zeros_like(token)

    sem_shapes = []
    for members in groups:
        sem_shapes += [pltpu.SemaphoreType.DMA((3 * len(members),))] * 2
    outs = pl.pallas_call(
        body, name=name, in_specs=[HBM] * n,
        out_specs=[SEM] * (2 * n_groups) + [HBM] * n + [pl.BlockSpec(memory_space=pltpu.VMEM)],
        out_shape=sem_shapes + [pltpu.HBM(a.shape, a.dtype) for a in lands] + [_sds((SUBLANES, LANES), F32)],
        input_output_aliases={a: 2 * n_groups + a for a in range(n)}, compiler_params=_SPLIT_COPY,
    )(*[_in_hbm(a) for a in lands])
    sems = [(outs[2 * g], outs[2 * g + 1]) for g in range(n_groups)]
    return sems, list(outs[2 * n_groups:2 * n_groups + n]), outs[-1]


def gather_forward(lands, sems, after, name):
    n = len(lands)

    def body(*refs):
        ins = refs[:n]
        send, recv = refs[n], refs[n + 1]
        fsend, frecv = refs[n + 3], refs[n + 4]
        mx, my, mc = _coords()
        for i in range(n):
            for k, (px, py) in enumerate(_other_chips(mx, my)):
                landed = ins[i].at[2 * px + py, mc]
                cp = _remote(landed, landed, send.at[3 * i + k], recv.at[3 * i + k], (px, py, mc))
                cp.wait_send()
                cp.wait_recv()
                _remote(landed, landed, fsend.at[3 * i + k], frecv.at[3 * i + k], (mx, my, 1 - mc)).start()

    outs = pl.pallas_call(
        body, name=name, in_specs=[HBM] * n + [SEM, SEM, ANY], out_specs=[SEM, SEM] + [HBM] * n,
        out_shape=[pltpu.SemaphoreType.DMA((3 * n,))] * 2 + [pltpu.HBM(a.shape, a.dtype) for a in lands],
        input_output_aliases={a: 2 + a for a in range(n)}, compiler_params=_SPLIT_COPY,
    )(*lands, sems[0], sems[1], after)
    return (outs[0], outs[1]), list(outs[2:])


def gather_wait(lands, sems, after, name):
    n = len(lands)

    def body(*refs):
        ins = refs[:n]
        fsend, frecv = refs[n], refs[n + 1]
        mx, my, mc = _coords()
        for i in range(n):
            for k, (px, py) in enumerate(_other_chips(mx, my)):
                theirs = ins[i].at[2 * px + py, 1 - mc]
                cp = _remote(theirs, theirs, fsend.at[3 * i + k], frecv.at[3 * i + k], (mx, my, 1 - mc))
                cp.wait_send()
                cp.wait_recv()

    outs = pl.pallas_call(
        body, name=name, in_specs=[HBM] * n + [SEM, SEM, ANY], out_specs=[HBM] * n,
        out_shape=[pltpu.HBM(a.shape, a.dtype) for a in lands],
        input_output_aliases={a: a for a in range(n)}, compiler_params=_SPLIT_COPY,
    )(*lands, sems[0], sems[1], after)
    return list(outs)


def _peers(mx, my, mc):
    return [(1 - mx if k & 4 else mx, 1 - my if k & 2 else my, 1 - mc if k & 1 else mc) for k in range(1, N_DEV)]


def devices_start(x, name):
    def body(x_ref, land_ref, send, recv, x_thru, land_thru):
        mx, my, mc = _coords()
        me = 4 * mx + 2 * my + mc
        for k, peer in enumerate(_peers(mx, my, mc)):
            _remote(x_ref, land_ref.at[me], send.at[k], recv.at[k], peer).start()

    land = lax.empty((N_DEV,) + x.shape, x.dtype)
    outs = pl.pallas_call(
        body, name=name, in_specs=[HBM, HBM], out_specs=[SEM, SEM, HBM, HBM],
        out_shape=[pltpu.SemaphoreType.DMA((N_DEV - 1,))] * 2 + [pltpu.HBM(x.shape, x.dtype), pltpu.HBM(land.shape, x.dtype)],
        input_output_aliases={0: 2, 1: 3}, compiler_params=_SPLIT_COPY,
    )(_in_hbm(x), _in_hbm(land))
    return (outs[0], outs[1]), outs[2], outs[3]


def devices_wait(x, land, sems, after, name):
    def body(x_ref, land_ref, send, recv, after_ref, x_thru, land_thru):
        mx, my, mc = _coords()
        for k, (px, py, pc) in enumerate(_peers(mx, my, mc)):
            cp = _remote(x_ref, land_ref.at[4 * px + 2 * py + pc], send.at[k], recv.at[k], (px, py, pc))
            cp.wait_send()
            cp.wait_recv()

    outs = pl.pallas_call(
        body, name=name, in_specs=[HBM, HBM, SEM, SEM, ANY], out_specs=[HBM, HBM],
        out_shape=[pltpu.HBM(x.shape, x.dtype), pltpu.HBM(land.shape, land.dtype)],
        input_output_aliases={0: 0, 1: 1}, compiler_params=_SPLIT_COPY,
    )(x, land, sems[0], sems[1], after)
    return outs[0], outs[1]


def device_sum(land, own, me, name):
    _, R, C = land.shape

    def body(s_ref, l_ref, o_ref_in, o_ref):
        acc = None
        for q in range(N_DEV):
            term = jnp.where(s_ref[0] == q, o_ref_in[...], l_ref[q])
            acc = term if acc is None else acc + term
        o_ref[...] = acc

    grid_spec = pltpu.PrefetchScalarGridSpec(
        num_scalar_prefetch=1, grid=(1,),
        in_specs=[pl.BlockSpec((N_DEV, R, C), lambda i, s: (0, 0, 0)), pl.BlockSpec((R, C), lambda i, s: (0, 0))],
        out_specs=pl.BlockSpec((R, C), lambda i, s: (0, 0)))
    return pl.pallas_call(body, name=name, grid_spec=grid_spec, out_shape=_sds((R, C), F32),
                          compiler_params=_params("arbitrary"))(me, land, own)


def reduce_send(grads, name):
    n = len(grads)

    def body(*refs):
        ins, lands = refs[:n], refs[n:2 * n]
        send, recv = refs[2 * n], refs[2 * n + 1]
        mx, my, mc = _coords()
        me = 4 * mx + 2 * my + mc
        for a in range(n):
            for k, (px, py, pc) in enumerate(_peers(mx, my, mc)):
                _remote(ins[a].at[2 * px + py, pc], lands[a].at[me], send.at[7 * a + k], recv.at[7 * a + k], (px, py, pc)).start()

    lands = [lax.empty((N_DEV,) + g.shape[2:], g.dtype) for g in grads]
    outs = pl.pallas_call(
        body, name=name, in_specs=[HBM] * (2 * n), out_specs=[SEM, SEM] + [HBM] * (2 * n),
        out_shape=[pltpu.SemaphoreType.DMA((7 * n,))] * 2 + [pltpu.HBM(a.shape, a.dtype) for a in grads + lands],
        input_output_aliases={a: 2 + a for a in range(2 * n)}, compiler_params=_SPLIT_COPY,
    )(*[_in_hbm(a) for a in grads + lands])
    return (outs[0], outs[1]), list(outs[2:2 + n]), list(outs[2 + n:])


def reduce_wait(grads, lands, sems, after, name):
    n = len(grads)

    def body(*refs):
        ins, zones = refs[:n], refs[n:2 * n]
        send, recv = refs[2 * n], refs[2 * n + 1]
        mx, my, mc = _coords()
        for a in range(n):
            for k, (px, py, pc) in enumerate(_peers(mx, my, mc)):
                cp = _remote(ins[a].at[2 * px + py, pc], zones[a].at[4 * px + 2 * py + pc], send.at[7 * a + k],
                             recv.at[7 * a + k], (px, py, pc))
                cp.wait_send()
                cp.wait_recv()

    outs = pl.pallas_call(
        body, name=name, in_specs=[HBM] * (2 * n) + [SEM, SEM, ANY], out_specs=[HBM] * (2 * n),
        out_shape=[pltpu.HBM(a.shape, a.dtype) for a in grads + lands],
        input_output_aliases={a: a for a in range(2 * n)}, compiler_params=_SPLIT_COPY,
    )(*grads, *lands, sems[0], sems[1], after)
    return list(outs[:n]), list(outs[n:])


def reduce_sum(land, grad, place, name, into=None, layer=None):
    _, hR, C = land.shape
    tr = hR
    while N_DEV * tr * C * 2 > (6 << 20) and tr % 32 == 0:
        tr //= 2

    def body(s_ref, l_ref, g_ref, *rest):
        o_ref = rest[-1]
        own = g_ref[...].astype(F32)
        acc = None
        for q in range(N_DEV):
            term = jnp.where(s_ref[2] == q, own, l_ref[q].astype(F32))
            acc = term if acc is None else acc + term
        o_ref[...] = acc

    in_specs = [pl.BlockSpec((N_DEV, tr, C), lambda i, s: (0, i, 0)),
                pl.BlockSpec((None, None, tr, C), lambda i, s: (s[0], s[1], i, 0))]
    args = [place, land, grad]
    aliases = {}
    if layer is None:
        out_spec = pl.BlockSpec((None, tr, C), lambda i, s: (s[1], i, 0))
        out_shape = _sds((2, hR, C), F32)
    else:
        out_spec = pl.BlockSpec((None, None, tr, C), lambda i, s: (layer, s[1], i, 0))
        out_shape = _sds((2, 2, hR, C), F32)
        if into is not None:
            in_specs.append(ANY)
            args.append(into)
            aliases = {3: 0}
    grid_spec = pltpu.PrefetchScalarGridSpec(num_scalar_prefetch=1, grid=(hR // tr,), in_specs=in_specs, out_specs=out_spec)
    return pl.pallas_call(body, name=name, grid_spec=grid_spec, out_shape=out_shape, input_output_aliases=aliases,
                          compiler_params=_params("arbitrary"))(*args)


def join_halves(arrays, name):
    n = len(arrays)
    pieces = [(a, l) for a, arr in enumerate(arrays) for l in (range(arr.shape[0]) if arr.ndim == 4 else [None])]

    def body(*refs):
        ins = refs[:n]
        send, recv = refs[2 * n:]
        mx, my, mc = _coords()

        def half(a, l, h):
            return ins[a].at[h] if l is None else ins[a].at[l, h]

        sends = [_remote(half(a, l, mc), half(a, l, mc), send.at[i], recv.at[i], (mx, my, 1 - mc))
                 for i, (a, l) in enumerate(pieces)]
        for cp in sends:
            cp.start()
        for i, (a, l) in enumerate(pieces):
            theirs = half(a, l, 1 - mc)
            _remote(theirs, theirs, send.at[i], recv.at[i], (mx, my, 1 - mc)).wait_recv()
        for cp in sends:
            cp.wait_send()

    return pl.pallas_call(
        body, name=name, in_specs=[ANY] * n, out_specs=[ANY] * n, out_shape=[_sds(a.shape, a.dtype) for a in arrays],
        input_output_aliases={a: a for a in range(n)},
        scratch_shapes=[pltpu.SemaphoreType.DMA((len(pieces),)), pltpu.SemaphoreType.DMA((len(pieces),))],
    )(*arrays)


LANES = 128
SUBLANES = 8


def _n_rows(shape):
    rows = -(-int(np.prod(shape)) // LANES)
    return -(-rows // SUBLANES) * SUBLANES


def _as_rows(a):
    flat = a.reshape(-1)
    rows = _n_rows(a.shape)
    return jnp.pad(flat, (0, rows * LANES - flat.shape[0])).reshape(rows, LANES)


def _pack(arrays):
    return jnp.concatenate([_as_rows(a) for a in arrays], axis=0)


def _unpack(rows, shapes):
    out, r0 = [], 0
    for s in shapes:
        n = _n_rows(s)
        out.append(rows[r0:r0 + n].reshape(-1)[:int(np.prod(s))].reshape(s))
        r0 += n
    return out


REPLICATED_SMALL = [("rel_bias", (32, 16)), ("even_norm", (1, 1024)), ("even_pool_w", (1, 4, 128, 128)),
                    ("even_pool_scale", (1, 512)), ("odd_q_norm", (1, 64)), ("odd_k_norm", (1, 64)),
                    ("ffn_norm", (2, 1024)), ("ffn_conv_b", (2, 5632))]
SHARDED_SMALL = [("even_conv_w", (1, 3, 128)), ("odd_norm", (1, 256)), ("ffn_conv_w", (2, 3, 1408))]
BIG = ["even_w_in", "even_w_out", "odd_w_qkv", "odd_w_o", "ffn_w_up", "ffn_w_down"]
WEIGHT_ORDER = ["rel_bias", "even_norm", "even_w_in", "even_conv_w", "even_pool_w", "even_pool_scale", "even_w_out",
                "odd_norm", "odd_w_qkv", "odd_q_norm", "odd_k_norm", "odd_w_o", "ffn_norm", "ffn_w_up", "ffn_conv_w",
                "ffn_conv_b", "ffn_w_down"]


def kernel(x, rel_bias, even_norm, even_w_in, even_conv_w, even_pool_w, even_pool_scale, even_w_out, odd_norm, odd_w_qkv, odd_q_norm, odd_k_norm, odd_w_o, ffn_norm, ffn_w_up, ffn_conv_w, ffn_conv_b, ffn_w_down, loss_target, m_rel_bias, m_even_norm, m_even_w_in, m_even_conv_w, m_even_pool_w, m_even_pool_scale, m_even_w_out, m_odd_norm, m_odd_w_qkv, m_odd_q_norm, m_odd_k_norm, m_odd_w_o, m_ffn_norm, m_ffn_w_up, m_ffn_conv_w, m_ffn_conv_b, m_ffn_w_down, v_rel_bias, v_even_norm, v_even_w_in, v_even_conv_w, v_even_pool_w, v_even_pool_scale, v_even_w_out, v_odd_norm, v_odd_w_qkv, v_odd_q_norm, v_odd_k_norm, v_odd_w_o, v_ffn_norm, v_ffn_w_up, v_ffn_conv_w, v_ffn_conv_b, v_ffn_w_down):
    W = dict(rel_bias=rel_bias, even_norm=even_norm, even_w_in=even_w_in, even_conv_w=even_conv_w, even_pool_w=even_pool_w,
             even_pool_scale=even_pool_scale, even_w_out=even_w_out, odd_norm=odd_norm, odd_w_qkv=odd_w_qkv,
             odd_q_norm=odd_q_norm, odd_k_norm=odd_k_norm, odd_w_o=odd_w_o, ffn_norm=ffn_norm, ffn_w_up=ffn_w_up,
             ffn_conv_w=ffn_conv_w, ffn_conv_b=ffn_conv_b, ffn_w_down=ffn_w_down)
    M1 = dict(rel_bias=m_rel_bias, even_norm=m_even_norm, even_w_in=m_even_w_in, even_conv_w=m_even_conv_w,
              even_pool_w=m_even_pool_w, even_pool_scale=m_even_pool_scale, even_w_out=m_even_w_out, odd_norm=m_odd_norm,
              odd_w_qkv=m_odd_w_qkv, odd_q_norm=m_odd_q_norm, odd_k_norm=m_odd_k_norm, odd_w_o=m_odd_w_o,
              ffn_norm=m_ffn_norm, ffn_w_up=m_ffn_w_up, ffn_conv_w=m_ffn_conv_w, ffn_conv_b=m_ffn_conv_b,
              ffn_w_down=m_ffn_w_down)
    M2 = dict(rel_bias=v_rel_bias, even_norm=v_even_norm, even_w_in=v_even_w_in, even_conv_w=v_even_conv_w,
              even_pool_w=v_even_pool_w, even_pool_scale=v_even_pool_scale, even_w_out=v_even_w_out, odd_norm=v_odd_norm,
              odd_w_qkv=v_odd_w_qkv, odd_q_norm=v_odd_q_norm, odd_k_norm=v_odd_k_norm, odd_w_o=v_odd_w_o,
              ffn_norm=v_ffn_norm, ffn_w_up=v_ffn_w_up, ffn_conv_w=v_ffn_conv_w, ffn_conv_b=v_ffn_conv_b,
              ffn_w_down=v_ffn_w_down)
    mx, my, mc = _coords()
    chip = 2 * mx + my
    me = 4 * mx + 2 * my + mc
    place = jnp.stack([chip, mc, me]).astype(jnp.int32)
    xs, target = x[0], loss_target[0]

    def halves(w):
        return w.reshape((w.shape[0], 2, w.shape[-2] // 2, w.shape[-1]))

    lands = [cast_into_slot(halves(even_w_in), 0, place, "cast_w_in"), cast_into_slot(halves(even_w_out), 0, place, "cast_w_out"),
             cast_into_slot(halves(ffn_w_up), 0, place, "cast_w_up0"), cast_into_slot(halves(ffn_w_down), 0, place, "cast_w_down0"),
             cast_into_slot(halves(odd_w_qkv), 0, place, "cast_w_qkv"), cast_into_slot(halves(odd_w_o), 0, place, "cast_w_o"),
             cast_into_slot(halves(ffn_w_up), 1, place, "cast_w_up1"), cast_into_slot(halves(ffn_w_down), 1, place, "cast_w_down1")]
    small = allgather_devices(_pack([even_conv_w, odd_norm, ffn_conv_w]), "allgather_small_weights")
    lands[0], small = lax.optimization_barrier((lands[0], small))
    gather_sems, lands, token = gather_start(lands, [[0, 1], [2, 3], [4, 5], [6, 7]], "gather_start")
    even_norm_after_start = even_norm + token[0:1, 0:1]
    small = small[0::2]
    conv_w_full = small[:, 0:3].transpose(1, 0, 2).reshape(3, A_WIDTH)
    odd_norm_full = small[:, 8:10].reshape(1, D_MODEL)
    ffn_cw_full = small[:, 16:82].reshape(N_CHIPS, 2, 3, 2 * D_FF // N_CHIPS).transpose(1, 2, 0, 3).reshape(2, 3, 2 * D_FF)
    pool_w = cast_bf16(even_pool_w[0], "cast_pool_w")
    gqk = jnp.stack([jnp.tile(odd_q_norm[0], N_HEADS), jnp.tile(odd_k_norm[0], N_HEADS),
                     jnp.ones((D_MODEL,), F32)])[:, None, :]
    bias = bias_expand(rel_bias.T, "bias_expand").reshape(6, N_HEADS, ATT_BLOCK, 2 * ATT_BLOCK)

    def ffn_fwd(l, xin):
        xn = rmsnorm_fwd(xin, ffn_norm[l:l + 1], f"ffn{l}_norm")
        up = mm_nn(xn, w_up[l], f"ffn{l}_up", out_dtype=BF16)
        u, act = glu_fwd(up, ffn_cw_full[l], ffn_conv_b[l:l + 1], f"ffn{l}_glu")
        return mm_nn(act, w_down[l], f"ffn{l}_down", res=xin), (xin, xn, up, u, act)

    def gathered(group, tag, after_landing, after_passing):
        sems, arrays = gather_forward(lands[2 * group:2 * group + 2], gather_sems[group], after_landing, "gather_forward_" + tag)
        return gather_wait(arrays, sems, after_passing, "gather_wait_" + tag)

    def ffn_weights(got):
        return got[0].reshape(N_CHIPS, 1, D_MODEL, 2 * D_FF // N_CHIPS), got[1].reshape(1, 1, D_FF, D_MODEL)

    w_up, w_down = [None, None], [None, None]
    xn0 = rmsnorm_fwd(xs, even_norm_after_start, "even_norm")
    got = gathered(0, "even", bias, xn0)
    w_in = got[0].reshape(N_CHIPS, 1, D_MODEL, EVEN_IN // N_CHIPS)
    w_out = got[1].reshape(1, 1, D_MODEL, D_MODEL)
    proj = mm_nn(xn0, w_in, "even_in")
    mix = mixer_fwd(proj, conv_w_full, pool_w, even_pool_scale, "even_mixer")
    x1 = mm_nn(mix, w_out, "even_out", res=xs)
    w_up[0], w_down[0] = ffn_weights(gathered(1, "ffn0", proj, x1))
    x2, ffn0 = ffn_fwd(0, x1)
    got = gathered(2, "odd", x1, x2)
    w_qkv = got[0].reshape(N_CHIPS, 1, D_MODEL, 3 * D_MODEL // N_CHIPS)
    w_o = got[1].reshape(1, 1, D_MODEL, D_MODEL)
    xn2 = rmsnorm_fwd(x2, odd_norm_full, "odd_norm")
    qkv = mm_nn(xn2, w_qkv, "odd_qkv")
    qkvn = qknorm_fwd(qkv, gqk, "odd_qknorm")
    att, lse = attn_fwd(qkvn, bias, "attn_fwd")
    x3 = mm_nn(att, w_o, "odd_out", res=x2)
    w_up[1], w_down[1] = ffn_weights(gathered(3, "ffn1", x2, x3))
    x4, ffn1 = ffn_fwd(1, x3)

    dy, dyb, sq = loss_grad(x4, target, "loss")
    loss = lax.psum(0.5 * jnp.sum(sq) * (1.0 / D_MODEL), ("x", "y", "c"))

    def ffn_bwd(l, dy, dyb, saved):
        xin, xn, up, u, act = saved
        dw_down = mm_tn(act, dyb, f"ffn{l}_dw_down", J=1, tk=D_FF // 2)
        dact = mm_nt(dyb, w_down[l], f"ffn{l}_dact", tr=D_FF // 2, out_dtype=BF16)
        dup, dcw, dcb = glu_bwd(up, u, dact, ffn_cw_full[l], f"ffn{l}_glu_bwd")
        dw_up = mm_tn(xn, dup, f"ffn{l}_dw_up", J=N_CHIPS, tk=512)
        dxn = mm_nt(dup, w_up[l], f"ffn{l}_dxn", tr=D_MODEL)
        dx, dxb, dg = rmsnorm_bwd(xin, ffn_norm[l:l + 1], dxn, dy, f"ffn{l}_norm_bwd")
        return dx, dxb, (dw_down, dw_up, dcw, dcb, dg)

    def quarters(g):
        return g.reshape(N_CHIPS, 2, g.shape[0] * g.shape[1] // (2 * N_CHIPS), g.shape[-1])

    def reduce_start(grads, tag, then):
        sems, parts, zones = reduce_send([quarters(g) for g in grads], "reduce_send_" + tag)
        then, parts = lax.optimization_barrier((then, parts))
        return (sems, parts, zones), then

    dx3, dx3b, g_ffn1 = ffn_bwd(1, dy, dyb, ffn1)
    red_ffn1, (dx3, dx3b) = reduce_start([g_ffn1[1], g_ffn1[0]], "ffn1", (dx3, dx3b))
    dw_o = mm_tn(att, dx3b, "odd_dw_o", J=1, tk=512)
    datt = mm_nt(dx3b, w_o, "odd_datt", tr=D_MODEL, out_dtype=BF16)
    dq, dk, dv, dbias = attn_bwd(qkvn, att, datt, lse, bias, "attn_bwd")
    dqkv, dgqk = qknorm_bwd(qkv, dq, dk, dv, gqk, "odd_qknorm_bwd")
    dw_qkv = mm_tn(xn2, dqkv, "odd_dw_qkv", J=N_CHIPS, tk=512)
    dxn2 = mm_nt(dqkv, w_qkv, "odd_dxn", tr=D_MODEL)
    red_odd, dxn2 = reduce_start([dw_qkv, dw_o], "odd", dxn2)
    dx2, dx2b, dg_odd = rmsnorm_bwd(x2, odd_norm_full, dxn2, dx3, "odd_norm_bwd")
    dx1, dx1b, g_ffn0 = ffn_bwd(0, dx2, dx2b, ffn0)
    red_ffn0, (dx1, dx1b) = reduce_start([g_ffn0[1], g_ffn0[0]], "ffn0", (dx1, dx1b))
    dw_out = mm_tn(mix, dx1b, "even_dw_out", J=1, tk=512)
    dmix = mm_nt(dx1b, w_out, "even_dmix", tr=D_MODEL)
    dproj, dcw_even, dpw, dps = mixer_bwd(proj, dmix, conv_w_full, pool_w, even_pool_scale, "even_mixer_bwd")
    dw_in = mm_tn(xn0, dproj, "even_dw_in", J=N_CHIPS, tk=512)
    dxn0 = mm_nt(dproj, w_in, "even_dxn", tr=D_MODEL)
    grad_x, _, dg_even = rmsnorm_bwd(xs, even_norm, dxn0, dx1, "even_norm_bwd")
    d_rel = jnp.sum(bias_reduce(dbias.reshape(3, N_HEADS, 2 * ATT_BLOCK * ATT_BLOCK), "bias_reduce"), axis=0).T

    red_even, grad_x = reduce_start([dw_in, dw_out], "even", grad_x)

    dcw_sh = dcw_even.reshape(3, N_CHIPS, A_WIDTH // N_CHIPS).transpose(1, 0, 2)
    don_sh = dg_odd.reshape(N_CHIPS, D_MODEL // N_CHIPS)
    dfcw = jnp.stack([g_ffn0[2], g_ffn1[2]])
    dfcw_sh = dfcw.reshape(2, 3, N_CHIPS, 2 * D_FF // N_CHIPS).transpose(2, 0, 1, 3)
    rep_grads = [d_rel, dg_even, dpw[None], dps, _head_sum(dgqk[0]), _head_sum(dgqk[1]),
                 jnp.concatenate([g_ffn0[4], g_ffn1[4]], axis=0), jnp.concatenate([g_ffn0[3], g_ffn1[3]], axis=0)]
    rep_rows = _pack(rep_grads)
    shard_rows = jnp.concatenate([_pack([dcw_sh[j], don_sh[j], dfcw_sh[j]]) for j in range(N_CHIPS)], axis=0)
    n_rep, n_shard = rep_rows.shape[0], shard_rows.shape[0] // N_CHIPS
    small_sems, small_rows, small_land = devices_start(jnp.concatenate([rep_rows, shard_rows], axis=0), "small_grads_start")
    grad_x, small_rows = lax.optimization_barrier((grad_x, small_rows))

    def reduce_end(red, tag, after):
        sems, parts, zones = red
        parts, zones = reduce_wait(parts, zones, sems, after, "reduce_wait_" + tag)
        return zones, parts

    z_ffn1, p_ffn1 = reduce_end(red_ffn1, "ffn1", grad_x)
    z_odd, p_odd = reduce_end(red_odd, "odd", grad_x)
    z_ffn0, p_ffn0 = reduce_end(red_ffn0, "ffn0", grad_x)
    r_up = reduce_sum(z_ffn0[0], p_ffn0[0], place, "reduce_sum_w_up0", layer=0)
    r_up = reduce_sum(z_ffn1[0], p_ffn1[0], place, "reduce_sum_w_up1", into=r_up, layer=1)
    r_down = reduce_sum(z_ffn0[1], p_ffn0[1], place, "reduce_sum_w_down0", layer=0)
    r_down = reduce_sum(z_ffn1[1], p_ffn1[1], place, "reduce_sum_w_down1", into=r_down, layer=1)
    later = ["odd_w_qkv", "odd_w_o", "ffn_w_up", "ffn_w_down"]
    joined = join_halves([reduce_sum(z_odd[0], p_odd[0], place, "reduce_sum_w_qkv"),
                          reduce_sum(z_odd[1], p_odd[1], place, "reduce_sum_w_o"), r_up, r_down], "grads_join_late_layers")
    G = {nm: g.reshape(W[nm].shape) for nm, g in zip(later, joined)}

    D_, NM, NV = {}, {}, {}

    def update(nm):
        as3 = lambda a: a.reshape((-1,) + a.shape[-2:])
        outs = adamw(as3(W[nm]), as3(G[nm]), as3(M1[nm]), as3(M2[nm]), "adamw_" + nm)
        D_[nm], NM[nm], NV[nm] = [o.reshape(W[nm].shape) for o in outs]

    for nm in later:
        update(nm)
    z_even, p_even = reduce_end(red_even, "even", D_[later[-1]])
    joined = join_halves([reduce_sum(z_even[0], p_even[0], place, "reduce_sum_w_in"),
                          reduce_sum(z_even[1], p_even[1], place, "reduce_sum_w_out")], "grads_join_first_layer")
    for nm, g in zip(["even_w_in", "even_w_out"], joined):
        G[nm] = g.reshape(W[nm].shape)
        update(nm)
    small_rows, small_land = devices_wait(small_rows, small_land, small_sems, D_["even_w_out"], "small_grads_wait")
    small_sum = device_sum(small_land, small_rows, place[2:3], "small_grads_sum")
    mine = lax.dynamic_slice_in_dim(small_sum, n_rep + chip * n_shard, n_shard, axis=0)
    g_small = jnp.concatenate([small_sum[:n_rep], mine], axis=0)
    small_names = [n for n, _ in REPLICATED_SMALL + SHARDED_SMALL]
    small_shapes = [s for _, s in REPLICATED_SMALL + SHARDED_SMALL]
    G.update(dict(zip(small_names, _unpack(g_small, small_shapes))))
    packs = [_pack([d[n] for n in small_names])[None] for d in (W, M1, M2)]
    outs = adamw(packs[0], g_small[None], packs[1], packs[2], "adamw_small")
    for dst, o in zip((D_, NM, NV), outs):
        dst.update(dict(zip(small_names, _unpack(o[0], small_shapes))))

    return (loss, grad_x[None], *[G[n] for n in WEIGHT_ORDER], *[D_[n] for n in WEIGHT_ORDER],
            *[NM[n] for n in WEIGHT_ORDER], *[NV[n] for n in WEIGHT_ORDER])


def _head_sum(dg):
    return jnp.sum(dg.reshape(N_HEADS, HEAD_DIM), axis=0, keepdims=True)
```

```python
import functools
import math

import numpy as np
import jax
import jax.numpy as jnp
from jax import lax
from jax.experimental import pallas as pl
from jax.experimental.pallas import tpu as pltpu

F32 = jnp.float32
BF16 = jnp.bfloat16

D_MODEL = 1024
N_HEADS = 16
HEAD_DIM = 64
A_WIDTH = 512
POOL_WINDOWS = (2, 4, 8, 16)
POOL_GROUP = 128
EVEN_IN = 2048
D_FF = 2816
DILATED_PAIRS = ((128, 1), (512, 4), (2048, 16))
ATT_BLOCK = 128
N_REL_BUCKETS = 32
REL_MAX_DISTANCE = 2048
EPS = 1e-6
MASK_VALUE = -1e30
ADAM_LR, ADAM_B1, ADAM_B2, ADAM_EPS, ADAM_WD, ADAM_STEP = 0.001, 0.9, 0.999, 1e-08, 0.01, 10

VMEM_LIMIT_BYTES = 48 * 1024 * 1024
N_CHIPS = 4
N_DEV = 8
MESH = pl.DeviceIdType.MESH


def _params(*sem):
    return pltpu.CompilerParams(dimension_semantics=sem if sem else None, vmem_limit_bytes=VMEM_LIMIT_BYTES)


def _sds(shape, dtype):
    return jax.ShapeDtypeStruct(tuple(shape), dtype)


def cast_bf16(x, name, tr=None):
    lead, (R, C) = x.shape[:-2], x.shape[-2:]
    n = int(np.prod(lead)) if lead else 1
    x3 = x.reshape((n, R, C))
    tr = tr or R

    def body(x_ref, o_ref):
        o_ref[...] = x_ref[...].astype(BF16)

    out = pl.pallas_call(
        body, name=name, grid=(n, R // tr),
        in_specs=[pl.BlockSpec((None, tr, C), lambda i, r: (i, r, 0))],
        out_specs=pl.BlockSpec((None, tr, C), lambda i, r: (i, r, 0)),
        out_shape=_sds((n, R, C), BF16), compiler_params=_params("parallel", "parallel"),
    )(x3)
    return out.reshape(lead + (R, C))


def rmsnorm_fwd(x, g, name, ts=512):
    S, Dm = x.shape

    def body(x_ref, g_ref, o_ref):
        xv = x_ref[...]
        r = lax.rsqrt(jnp.mean(xv * xv, axis=-1, keepdims=True) + EPS)
        o_ref[...] = ((xv * r) * g_ref[...]).astype(BF16)

    return pl.pallas_call(
        body, name=name, grid=(S // ts,),
        in_specs=[pl.BlockSpec((ts, Dm), lambda i: (i, 0)), pl.BlockSpec((1, Dm), lambda i: (0, 0))],
        out_specs=pl.BlockSpec((ts, Dm), lambda i: (i, 0)),
        out_shape=_sds((S, Dm), BF16), compiler_params=_params("parallel"),
    )(x, g)


def mm_nn(a, w, name, layer=0, res=None, out_dtype=F32, tm=1024):
    M, K = a.shape
    J, _, _, Ns = w.shape

    def body(*refs):
        a_ref, w_ref = refs[0], refs[1]
        o_ref = refs[-1]
        acc = jnp.dot(a_ref[...], w_ref[...], preferred_element_type=F32)
        if res is not None:
            acc = refs[2][...] + acc
        o_ref[...] = acc.astype(o_ref.dtype)

    in_specs = [pl.BlockSpec((tm, K), lambda j, m: (m, 0)),
                pl.BlockSpec((None, None, K, Ns), lambda j, m: (j, layer, 0, 0))]
    args = [a, w]
    if res is not None:
        in_specs.append(pl.BlockSpec((tm, Ns), lambda j, m: (m, j)))
        args.append(res)
    return pl.pallas_call(
        body, name=name, grid=(J, M // tm), in_specs=in_specs,
        out_specs=pl.BlockSpec((tm, Ns), lambda j, m: (m, j)),
        out_shape=_sds((M, J * Ns), out_dtype), compiler_params=_params("parallel", "parallel"),
    )(*args)


def mm_nt(dy, w, name, tr, layer=0, out_dtype=F32, tm=512):
    M = dy.shape[0]
    J, _, R, Ns = w.shape
    dims = (((1,), (1,)), ((), ()))

    def body(dy_ref, w_ref, o_ref):
        acc = None
        for j in range(J):
            p = lax.dot_general(dy_ref[:, j * Ns:(j + 1) * Ns], w_ref[j], dims, preferred_element_type=F32)
            acc = p if acc is None else acc + p
        o_ref[...] = acc.astype(o_ref.dtype)

    return pl.pallas_call(
        body, name=name, grid=(R // tr, M // tm),
        in_specs=[pl.BlockSpec((tm, J * Ns), lambda r, m: (m, 0)),
                  pl.BlockSpec((J, None, tr, Ns), lambda r, m: (0, layer, r, 0))],
        out_specs=pl.BlockSpec((tm, tr), lambda r, m: (m, r)),
        out_shape=_sds((M, R), out_dtype),
        compiler_params=_params("parallel", "parallel"),
    )(dy, w)


def mm_nt_norm_bwd(dy, w, x, g, dres, name, layer=0, tm=512):
    M = dy.shape[0]
    J, _, Dm, Ns = w.shape
    dims = (((1,), (1,)), ((), ()))

    def body(dy_ref, w_ref, x_ref, g_ref, r_ref, dx_ref, dxb_ref, dg_ref):
        dxn = None
        for j in range(J):
            p = lax.dot_general(dy_ref[:, j * Ns:(j + 1) * Ns], w_ref[j], dims, preferred_element_type=F32)
            dxn = p if dxn is None else dxn + p
        xv = x_ref[...]
        r = lax.rsqrt(jnp.mean(xv * xv, axis=-1, keepdims=True) + EPS)
        gx = dxn * g_ref[...]
        dot = jnp.sum(gx * xv, axis=-1, keepdims=True)
        dx = r_ref[...] + r * gx - xv * ((r * r * r) * (dot * (1.0 / Dm)))
        dx_ref[...] = dx
        dxb_ref[...] = dx.astype(BF16)
        part = jnp.sum(dxn * (xv * r), axis=0, keepdims=True)

        @pl.when(pl.program_id(0) == 0)
        def _():
            dg_ref[...] = part

        @pl.when(pl.program_id(0) > 0)
        def _():
            dg_ref[...] += part

    row = pl.BlockSpec((tm, Dm), lambda m: (m, 0))
    vec = pl.BlockSpec((1, Dm), lambda m: (0, 0))
    return pl.pallas_call(
        body, name=name, grid=(M // tm,),
        in_specs=[pl.BlockSpec((tm, J * Ns), lambda m: (m, 0)),
                  pl.BlockSpec((J, None, Dm, Ns), lambda m: (0, layer, 0, 0), pipeline_mode=pl.Buffered(1)), row, vec, row],
        out_specs=[row, row, vec],
        out_shape=[_sds((M, Dm), F32), _sds((M, Dm), BF16), _sds((1, Dm), F32)],
        compiler_params=_params("arbitrary"),
    )(dy, w, x, g, dres)


def mm_tn(a, dy, name, J, tk, tm=512):
    M, K = a.shape
    N = dy.shape[1]
    Ns = N // J
    n_m = M // tm
    dims = (((0,), (0,)), ((), ()))

    def body(a_ref, dy_ref, o_ref, acc_ref):
        p = lax.dot_general(a_ref[...], dy_ref[...], dims, preferred_element_type=F32)
        m = pl.program_id(1)

        @pl.when(m == 0)
        def _():
            acc_ref[...] = p

        @pl.when(m > 0)
        def _():
            acc_ref[...] += p

        @pl.when(m == n_m - 1)
        def _():
            for j in range(J):
                o_ref[j] = acc_ref[:, j * Ns:(j + 1) * Ns].astype(BF16)

    return pl.pallas_call(
        body, name=name, grid=(K // tk, n_m),
        in_specs=[pl.BlockSpec((tm, tk), lambda k, m: (m, k)), pl.BlockSpec((tm, N), lambda k, m: (m, 0))],
        out_specs=pl.BlockSpec((J, tk, Ns), lambda k, m: (0, k, 0)),
        out_shape=_sds((J, K, Ns), BF16), scratch_shapes=[pltpu.VMEM((tk, N), F32)],
        compiler_params=_params("parallel", "arbitrary"),
    )(a, dy)


HALO = 16


def _shift_down(x, s):
    return pltpu.roll(x, s, 0)


def _shift_up(x, s):
    return pltpu.roll(x, x.shape[0] - s, 0)


def _conv3(z, cw):
    return (_shift_down(z, 2) * cw[0:1] + _shift_down(z, 1) * cw[1:2]) + z * cw[2:3]


def _window_count(first_row, n, k):
    t = first_row + lax.broadcasted_iota(jnp.int32, (n, 1), 0)
    return jnp.clip(t + 1, 1, k).astype(F32)


def mixer_fwd(proj, conv_w, pool_w, pool_scale, name, ts=256):
    S = proj.shape[0]
    n = ts + HALO

    def body(pm_ref, pb_ref, cw_ref, pw_ref, ps_ref, o_ref):
        i = pl.program_id(0)
        before = jnp.where(i > 0, pb_ref[...], 0.0)
        ext = jnp.concatenate([before, pm_ref[...]], axis=0)
        cw = cw_ref[...]
        z = ext[:, 2 * A_WIDTH:3 * A_WIDTH] * ext[:, 0:A_WIDTH]
        cz = _conv3(z, cw)
        ya = pm_ref[:, A_WIDTH:2 * A_WIDTH] * cz[HALO:]
        o_ref[:, 0:A_WIDTH] = ya.astype(BF16)
        for g, k in enumerate(POOL_WINDOWS):
            lo = 3 * A_WIDTH + g * POOL_GROUP
            p = ext[:, lo:lo + POOL_GROUP]
            w = p
            s = 1
            while s < k:
                w = w + _shift_down(w, s)
                s *= 2
            pooled = w / _window_count(i * ts - HALO, n, k) - p
            yb = jnp.dot(pooled[HALO:].astype(BF16), pw_ref[g], preferred_element_type=F32)
            yb = yb * ps_ref[:, g * POOL_GROUP:(g + 1) * POOL_GROUP]
            o_ref[:, A_WIDTH + g * POOL_GROUP:A_WIDTH + (g + 1) * POOL_GROUP] = yb.astype(BF16)

    hb = ts // HALO
    return pl.pallas_call(
        body, name=name, grid=(S // ts,),
        in_specs=[
            pl.BlockSpec((ts, EVEN_IN), lambda i: (i, 0)),
            pl.BlockSpec((HALO, EVEN_IN), lambda i: (jnp.maximum(i * hb - 1, 0), 0)),
            pl.BlockSpec((3, A_WIDTH), lambda i: (0, 0)),
            pl.BlockSpec((4, POOL_GROUP, POOL_GROUP), lambda i: (0, 0, 0)),
            pl.BlockSpec((1, 4 * POOL_GROUP), lambda i: (0, 0)),
        ],
        out_specs=pl.BlockSpec((ts, D_MODEL), lambda i: (i, 0)),
        out_shape=_sds((S, D_MODEL), BF16), compiler_params=_params("parallel"),
    )(proj, proj, conv_w, pool_w, pool_scale)


def mixer_bwd(proj, dmix, conv_w, pool_w, pool_scale, name, ts=256):
    S = proj.shape[0]
    n = ts + 2 * HALO
    nt = S // ts
    tn_dims = (((0,), (0,)), ((), ()))
    nt_dims = (((1,), (1,)), ((), ()))

    def body(pm_ref, pb_ref, pa_ref, dm_ref, da_ref, cw_ref, pw_ref, ps_ref, o_ref, dcw_ref, dpw_ref, dps_ref):
        i = pl.program_id(0)
        last = i == nt - 1
        before = jnp.where(i > 0, pb_ref[...], 0.0)
        after = jnp.where(last, 0.0, pa_ref[...])
        ext = jnp.concatenate([before, pm_ref[...], after], axis=0)
        dafter = jnp.where(last, 0.0, da_ref[...])
        dext = jnp.concatenate([jnp.zeros((HALO, D_MODEL), F32), dm_ref[...], dafter], axis=0)
        cw = cw_ref[...]
        main = slice(HALO, HALO + ts)

        @pl.when(i == 0)
        def _():
            dcw_ref[...] = jnp.zeros_like(dcw_ref)
            dpw_ref[...] = jnp.zeros_like(dpw_ref)
            dps_ref[...] = jnp.zeros_like(dps_ref)

        h, gb, gc = ext[:, 0:A_WIDTH], ext[:, A_WIDTH:2 * A_WIDTH], ext[:, 2 * A_WIDTH:3 * A_WIDTH]
        z = gc * h
        z1, z2 = _shift_down(z, 1), _shift_down(z, 2)
        cz = (z2 * cw[0:1] + z1 * cw[1:2]) + z * cw[2:3]
        dya = dext[:, 0:A_WIDTH]
        dcz = dya * gb
        dz = dcz * cw[2:3] + _shift_up(dcz, 1) * cw[1:2] + _shift_up(dcz, 2) * cw[0:1]
        o_ref[:, 0:A_WIDTH] = (dz * gc)[main].astype(BF16)
        o_ref[:, A_WIDTH:2 * A_WIDTH] = (dya * cz)[main].astype(BF16)
        o_ref[:, 2 * A_WIDTH:3 * A_WIDTH] = (dz * h)[main].astype(BF16)
        dczm = dcz[main]
        dcw_ref[0:1, :] += jnp.sum(dczm * z2[main], axis=0, keepdims=True)
        dcw_ref[1:2, :] += jnp.sum(dczm * z1[main], axis=0, keepdims=True)
        dcw_ref[2:3, :] += jnp.sum(dczm * z[main], axis=0, keepdims=True)

        for g, k in enumerate(POOL_WINDOWS):
            lo = 3 * A_WIDTH + g * POOL_GROUP
            cols = slice(g * POOL_GROUP, (g + 1) * POOL_GROUP)
            p = ext[:, lo:lo + POOL_GROUP]
            w = p
            s = 1
            while s < k:
                w = w + _shift_down(w, s)
                s *= 2
            cnt = _window_count(i * ts - HALO, n, k)
            pooled = (w / cnt - p)[main].astype(BF16)
            dyb = dext[:, A_WIDTH + g * POOL_GROUP:A_WIDTH + (g + 1) * POOL_GROUP]
            e = dyb * ps_ref[:, cols]
            pre = jnp.dot(pooled, pw_ref[g], preferred_element_type=F32)
            dps_ref[:, cols] += jnp.sum(dyb[main] * pre, axis=0, keepdims=True)
            dpw_ref[g] += lax.dot_general(pooled, e[main].astype(BF16), tn_dims, preferred_element_type=F32)
            dpooled = lax.dot_general(e.astype(BF16), pw_ref[g], nt_dims, preferred_element_type=F32)
            q = dpooled / cnt
            a = q
            s = 1
            while s < k:
                a = a + _shift_up(a, s)
                s *= 2
            o_ref[:, lo:lo + POOL_GROUP] = (a - dpooled)[main].astype(BF16)

    hb = ts // HALO
    nh = S // HALO
    before_map = lambda i: (jnp.maximum(i * hb - 1, 0), 0)
    after_map = lambda i: (jnp.minimum((i + 1) * hb, nh - 1), 0)
    full = lambda *shape: pl.BlockSpec(shape, lambda i: (0,) * len(shape))
    return pl.pallas_call(
        body, name=name, grid=(nt,),
        in_specs=[
            pl.BlockSpec((ts, EVEN_IN), lambda i: (i, 0)),
            pl.BlockSpec((HALO, EVEN_IN), before_map),
            pl.BlockSpec((HALO, EVEN_IN), after_map),
            pl.BlockSpec((ts, D_MODEL), lambda i: (i, 0)),
            pl.BlockSpec((HALO, D_MODEL), after_map),
            full(3, A_WIDTH), full(4, POOL_GROUP, POOL_GROUP), full(1, 4 * POOL_GROUP),
        ],
        out_specs=[pl.BlockSpec((ts, EVEN_IN), lambda i: (i, 0)), full(3, A_WIDTH), full(4, POOL_GROUP, POOL_GROUP),
                   full(1, 4 * POOL_GROUP)],
        out_shape=[_sds((S, EVEN_IN), BF16), _sds((3, A_WIDTH), F32), _sds((4, POOL_GROUP, POOL_GROUP), F32),
                   _sds((1, 4 * POOL_GROUP), F32)],
        compiler_params=_params("arbitrary"),
    )(proj, proj, proj, dmix, dmix, conv_w, pool_w, pool_scale)


FFN_HALO = 16
FFN_TC = 1408


def glu_fwd(up, conv_w, conv_b, name, ts=256):
    S = up.shape[0]
    nc = D_FF // FFN_TC

    def body(gm_ref, gb_ref, um_ref, ub_ref, cwg_ref, cwu_ref, cbg_ref, cbu_ref, ug_ref, uu_ref, o_ref):
        i = pl.program_id(0)

        def conv(m_ref, b_ref, cw_ref, cb_ref):
            before = jnp.where(i > 0, b_ref[...].astype(F32), 0.0)
            ext = jnp.concatenate([before, m_ref[...].astype(F32)], axis=0)
            return _conv3(ext, cw_ref[...])[FFN_HALO:] + cb_ref[...]

        gate = conv(gm_ref, gb_ref, cwg_ref, cbg_ref)
        upv = conv(um_ref, ub_ref, cwu_ref, cbu_ref)
        ug_ref[...] = gate.astype(BF16)
        uu_ref[...] = upv.astype(BF16)
        o_ref[...] = ((gate * (1.0 / (1.0 + jnp.exp(-gate)))) * upv).astype(BF16)

    hb = ts // FFN_HALO
    main = lambda off: pl.BlockSpec((ts, FFN_TC), lambda i, c: (i, c + off))
    halo = lambda off: pl.BlockSpec((FFN_HALO, FFN_TC), lambda i, c: (jnp.maximum(i * hb - 1, 0), c + off))
    cw = lambda off: pl.BlockSpec((3, FFN_TC), lambda i, c: (0, c + off))
    cb = lambda off: pl.BlockSpec((1, FFN_TC), lambda i, c: (0, c + off))
    ug, uu, act = pl.pallas_call(
        body, name=name, grid=(S // ts, nc),
        in_specs=[main(0), halo(0), main(nc), halo(nc), cw(0), cw(nc), cb(0), cb(nc)],
        out_specs=[pl.BlockSpec((ts, FFN_TC), lambda i, c: (i, c))] * 3,
        out_shape=[_sds((S, D_FF), BF16)] * 3, compiler_params=_params("parallel", "parallel"),
    )(up, up, up, up, conv_w, conv_w, conv_b, conv_b)
    return (ug, uu), act


def glu_bwd(up, u, da, conv_w, name, ts=256):
    S = up.shape[0]
    nc = D_FF // FFN_TC
    nt = S // ts
    W = 2 * D_FF

    def body(x_ref, gm_ref, ga_ref, um_ref, ua_ref, dm_ref, da_ref, cw_ref, dx_ref, dcw_ref, dcb_ref):
        i = pl.program_id(0)
        last = i == nt - 1

        @pl.when(i == 0)
        def _():
            dcw_ref[...] = jnp.zeros_like(dcw_ref)
            dcb_ref[...] = jnp.zeros_like(dcb_ref)

        def rows(m_ref, a_ref, cols):
            return jnp.concatenate([m_ref[:, cols], a_ref[:, cols]], axis=0).astype(F32)

        def back(d, cols):
            cw = cw_ref[:, cols]
            d1, d2 = _shift_up(d, 1), _shift_up(d, 2)
            dx_ref[:, cols] = ((d * cw[2:3] + d1 * cw[1:2]) + d2 * cw[0:1])[:ts].astype(BF16)
            x = x_ref[:, cols].astype(F32)
            dcb_ref[:, cols] += jnp.sum(d[:ts], axis=0, keepdims=True)
            dcw_ref[0:1, cols] += jnp.sum(d2[:ts] * x, axis=0, keepdims=True)
            dcw_ref[1:2, cols] += jnp.sum(d1[:ts] * x, axis=0, keepdims=True)
            dcw_ref[2:3, cols] += jnp.sum(d[:ts] * x, axis=0, keepdims=True)

        for c in range(nc):
            cols = slice(c * FFN_TC, (c + 1) * FFN_TC)
            ug, uu = rows(gm_ref, ga_ref, cols), rows(um_ref, ua_ref, cols)
            dae = rows(dm_ref, da_ref, cols)
            dae = jnp.where(last & (lax.broadcasted_iota(jnp.int32, dae.shape, 0) >= ts), 0.0, dae)
            sg = 1.0 / (1.0 + jnp.exp(-ug))
            duu = dae * (ug * sg)
            dug = (dae * uu) * (sg * (1.0 + ug * (1.0 - sg)))
            back(dug, cols)
            back(duu, slice(D_FF + c * FFN_TC, D_FF + (c + 1) * FFN_TC))

    hb = ts // FFN_HALO
    nh = S // FFN_HALO
    after_map = lambda i: (jnp.minimum((i + 1) * hb, nh - 1), 0)
    main = pl.BlockSpec((ts, D_FF), lambda i: (i, 0))
    after = pl.BlockSpec((FFN_HALO, D_FF), after_map)
    return pl.pallas_call(
        body, name=name, grid=(nt,),
        in_specs=[pl.BlockSpec((ts, W), lambda i: (i, 0)), main, after, main, after, main, after,
                  pl.BlockSpec((3, W), lambda i: (0, 0))],
        out_specs=[pl.BlockSpec((ts, W), lambda i: (i, 0)), pl.BlockSpec((3, W), lambda i: (0, 0)),
                   pl.BlockSpec((1, W), lambda i: (0, 0))],
        out_shape=[_sds((S, W), BF16), _sds((3, W), F32), _sds((1, W), F32)],
        compiler_params=_params("arbitrary"),
    )(up, u[0], u[0], u[1], u[1], da, da, conv_w)


def _head_mean_matrix():
    h = np.arange(D_MODEL) // HEAD_DIM
    return jnp.asarray((h[:, None] == h[None, :]).astype(np.float32) / HEAD_DIM, dtype=BF16)


def _head_mean(v, gm):
    return jnp.dot(v.astype(BF16), gm, preferred_element_type=F32)


def qknorm_fwd(qkv, gqk, name, ts=512):
    S = qkv.shape[0]

    def body(x_ref, g_ref, gm_ref, o_ref):
        part = pl.program_id(0)
        x = x_ref[...]

        @pl.when(part < 2)
        def _():
            r = lax.rsqrt(_head_mean(x * x, gm_ref[...]) + EPS)
            o_ref[...] = ((x * r) * g_ref[...]).astype(BF16)

        @pl.when(part == 2)
        def _():
            o_ref[...] = x.astype(BF16)

    return pl.pallas_call(
        body, name=name, grid=(3, S // ts),
        in_specs=[pl.BlockSpec((ts, D_MODEL), lambda p, i: (i, p)), pl.BlockSpec((None, 1, D_MODEL), lambda p, i: (p, 0, 0)),
                  pl.BlockSpec((D_MODEL, D_MODEL), lambda p, i: (0, 0))],
        out_specs=pl.BlockSpec((ts, D_MODEL), lambda p, i: (i, p)),
        out_shape=_sds((S, 3 * D_MODEL), BF16), compiler_params=_params("parallel", "parallel"),
    )(qkv, gqk, _head_mean_matrix())


def qknorm_bwd(qkv, dq, dk, dv, gqk, name, ts=256):
    S = qkv.shape[0]

    def body(x_ref, dq_ref, dk_ref, dv_ref, g_ref, gm_ref, o_ref, dg_ref):
        @pl.when(pl.program_id(0) == 0)
        def _():
            dg_ref[...] = jnp.zeros_like(dg_ref)

        gm = gm_ref[...]
        for part, d_ref in enumerate((dq_ref, dk_ref)):
            cols = slice(part * D_MODEL, (part + 1) * D_MODEL)
            x = x_ref[:, cols]
            d = d_ref[...]
            r = lax.rsqrt(_head_mean(x * x, gm) + EPS)
            gx = d * g_ref[part]
            o_ref[:, cols] = (r * gx - x * ((r * r * r) * _head_mean(gx * x, gm))).astype(BF16)
            dg_ref[part] += jnp.sum(d * (x * r), axis=0, keepdims=True)
        o_ref[:, 2 * D_MODEL:] = dv_ref[...].astype(BF16)

    row = pl.BlockSpec((ts, D_MODEL), lambda i: (i, 0))
    wide = pl.BlockSpec((ts, 3 * D_MODEL), lambda i: (i, 0))
    gains = pl.BlockSpec((3, 1, D_MODEL), lambda i: (0, 0, 0))
    return pl.pallas_call(
        body, name=name, grid=(S // ts,),
        in_specs=[wide, row, row, row, gains, pl.BlockSpec((D_MODEL, D_MODEL), lambda i: (0, 0))],
        out_specs=[wide, gains],
        out_shape=[_sds((S, 3 * D_MODEL), BF16), _sds((3, 1, D_MODEL), F32)],
        compiler_params=_params("arbitrary"),
    )(qkv, dq, dk, dv, gqk, _head_mean_matrix())


RESIDUES = 16


def _block_order(dil):
    runs = RESIDUES // dil
    slot = np.arange(ATT_BLOCK)
    return (slot % (ATT_BLOCK // runs)) * runs + slot // (ATT_BLOCK // runs)


def _bucket_tables():
    n = ATT_BLOCK
    max_exact = N_REL_BUCKETS // 2
    buckets, valids = [], []
    for _, dil in DILATED_PAIRS:
        order = _block_order(dil)
        a = order[:, None]
        c = np.concatenate([order, n + order])[None, :]
        first_half = (np.arange(2 * n) < n)[None, :]
        rel = a + n - c
        band = (rel >= 0) & (rel <= n)
        dist = np.clip(rel, 0, n) * dil
        dd = np.maximum(dist, 1).astype(np.float32)
        large = max_exact + (np.log(dd / np.float32(max_exact)) / np.float32(math.log(REL_MAX_DISTANCE / max_exact))
                             * np.float32(N_REL_BUCKETS - max_exact)).astype(np.int32)
        large = np.minimum(large, N_REL_BUCKETS - 1)
        buckets.append(np.where(dist < max_exact, dist, large).reshape(1, -1))
        valids.append(np.stack([(band & ~first_half).reshape(1, -1), band.reshape(1, -1)]))
    return np.stack(buckets).astype(np.int32), np.stack(valids).astype(np.int32)


BIAS_CHUNK = 8192


def _split3(x):
    a = x.astype(BF16)
    r = x - a.astype(F32)
    b = r.astype(BF16)
    c = (r - b.astype(F32)).astype(BF16)
    return a, b, c


def bias_expand(rel_bias_t, name):
    bucket, valid = _bucket_tables()
    nq = bucket.shape[-1]

    def body(t_ref, b_ref, v_ref, o_ref):
        onehot = (lax.broadcasted_iota(jnp.int32, (N_REL_BUCKETS, BIAS_CHUNK), 0) == b_ref[...]).astype(BF16)
        acc = None
        for term in _split3(t_ref[...]):
            p = jnp.dot(term, onehot, preferred_element_type=F32)
            acc = p if acc is None else acc + p
        o_ref[...] = jnp.where(v_ref[...] > 0, acc, MASK_VALUE)

    return pl.pallas_call(
        body, name=name, grid=(3, 2, nq // BIAS_CHUNK),
        in_specs=[pl.BlockSpec((N_HEADS, N_REL_BUCKETS), lambda b, v, c: (0, 0)),
                  pl.BlockSpec((None, 1, BIAS_CHUNK), lambda b, v, c: (b, 0, c)),
                  pl.BlockSpec((None, None, 1, BIAS_CHUNK), lambda b, v, c: (b, v, 0, c))],
        out_specs=pl.BlockSpec((None, None, N_HEADS, BIAS_CHUNK), lambda b, v, c: (b, v, 0, c)),
        out_shape=_sds((3, 2, N_HEADS, nq), F32), compiler_params=_params("parallel", "parallel", "parallel"),
    )(rel_bias_t, jnp.asarray(bucket), jnp.asarray(valid))


def bias_reduce(dbias, name):
    bucket, _ = _bucket_tables()
    nq = bucket.shape[-1]
    dims = (((1,), (1,)), ((), ()))

    def body(d_ref, b_ref, o_ref):
        onehot = (lax.broadcasted_iota(jnp.int32, (N_REL_BUCKETS, BIAS_CHUNK), 0) == b_ref[...]).astype(BF16)
        acc = None
        for term in _split3(d_ref[...]):
            p = lax.dot_general(term, onehot, dims, preferred_element_type=F32)
            acc = p if acc is None else acc + p

        @pl.when(pl.program_id(1) == 0)
        def _():
            o_ref[...] = acc

        @pl.when(pl.program_id(1) > 0)
        def _():
            o_ref[...] += acc

    return pl.pallas_call(
        body, name=name, grid=(3, nq // BIAS_CHUNK),
        in_specs=[pl.BlockSpec((None, N_HEADS, BIAS_CHUNK), lambda b, c: (b, 0, c)),
                  pl.BlockSpec((None, 1, BIAS_CHUNK), lambda b, c: (b, 0, c))],
        out_specs=pl.BlockSpec((None, N_HEADS, N_REL_BUCKETS), lambda b, c: (b, 0, 0)),
        out_shape=_sds((3, N_HEADS, N_REL_BUCKETS), F32), compiler_params=_params("parallel", "arbitrary"),
    )(dbias, jnp.asarray(bucket))


PAIR = 2 * HEAD_DIM
N_PAIRS = N_HEADS // 2
_NT = (((1,), (1,)), ((), ()))
_TN = (((0,), (0,)), ((), ()))


def _low_lanes(shape):
    return lax.broadcasted_iota(jnp.int32, shape, 1) < HEAD_DIM


ATTN_VMEM_LIMIT_BYTES = 56 * 1024 * 1024
BRANCH_ORDER = (2, 1, 0)


def _regroup(dst, src, L16):
    for r in range(RESIDUES):
        dst[pl.ds(r * L16, L16), :] = src[pl.ds(r, L16, stride=RESIDUES), :]


def _ungroup(dst, src, L16):
    for r in range(RESIDUES):
        dst[pl.ds(r, L16, stride=RESIDUES), :] = src[pl.ds(r * L16, L16), :]


def _branch_geometry(branch, S):
    dil = DILATED_PAIRS[branch][1]
    runs = RESIDUES // dil
    return dil, runs, ATT_BLOCK // runs, S // dil // ATT_BLOCK


def _block_rows(it, branch, S):
    dil, runs, run_len, n_blocks = _branch_geometry(branch, S)
    L16 = S // RESIDUES
    r, b = it // n_blocks, it % n_blocks
    prev = jnp.maximum(b - 1, 0)
    cur_rows = [pl.multiple_of((j * dil + r) * L16 + run_len * b, 8) for j in range(runs)]
    prev_rows = [pl.multiple_of((j * dil + r) * L16 + run_len * prev, 8) for j in range(runs)]
    return cur_rows, prev_rows, jnp.minimum(b, 1)


def _load_block(ref, rows, run_len):
    parts = [ref[pl.ds(o, run_len), :] for o in rows]
    return parts[0] if len(parts) == 1 else jnp.concatenate(parts, axis=0)


def _store_block(ref, rows, run_len, value, add=False):
    for j, o in enumerate(rows):
        part = value[j * run_len:(j + 1) * run_len]
        if add:
            ref[pl.ds(o, run_len), :] += part
        else:
            ref[pl.ds(o, run_len), :] = part


ATTN_FWD_UNROLL = 8
ATTN_BWD_UNROLL = 4


def _stack_heads(x, low):
    zero = jnp.zeros_like(x)
    return jnp.concatenate([jnp.where(low, x, zero), jnp.where(low, zero, x)], axis=0)


def _unstack_heads(y, low):
    return jnp.where(low, y[:ATT_BLOCK], y[ATT_BLOCK:])


def attn_fwd(qkvn, bias, name):
    S = qkvn.shape[0]
    L16 = S // RESIDUES
    n_iter = S // ATT_BLOCK

    def body(q_ref, k_ref, v_ref, b_ref, o_ref, lse_ref, stage, qp, kp, vp, acc_s, m_s, l_s):
        for src, dst in ((q_ref, qp), (k_ref, kp), (v_ref, vp)):
            stage[...] = src[...].astype(F32)
            _regroup(dst, stage, L16)
        low = _low_lanes((ATT_BLOCK, PAIR))

        for branch in BRANCH_ORDER:
            _, _, run_len, _ = _branch_geometry(branch, S)
            first = branch == BRANCH_ORDER[0]

            def step(it, carry, branch=branch, run_len=run_len, first=first):
                cur, prev, variant = _block_rows(it, branch, S)
                q = _load_block(qp, cur, run_len).astype(BF16)
                k = jnp.concatenate([_load_block(kp, prev, run_len), _load_block(kp, cur, run_len)], axis=0).astype(BF16)
                v = jnp.concatenate([_load_block(vp, prev, run_len), _load_block(vp, cur, run_len)], axis=0).astype(BF16)
                s = lax.dot_general(_stack_heads(q, low), k, _NT, preferred_element_type=F32) * (HEAD_DIM ** -0.5)
                s = s + b_ref[2 * branch + variant].reshape(2 * ATT_BLOCK, 2 * ATT_BLOCK)
                mx = jnp.max(s, axis=-1, keepdims=True)
                p = jnp.exp(s - mx)
                den = jnp.sum(p, axis=-1, keepdims=True)
                pv = jnp.dot(p.astype(BF16), v, preferred_element_type=F32)
                acc = _unstack_heads(pv, low)
                m = _unstack_heads(mx, low)
                l = _unstack_heads(den, low)
                if not first:
                    m_old = _load_block(m_s, cur, run_len)
                    m_new = jnp.maximum(m_old, m)
                    a_old, a_new = jnp.exp(m_old - m_new), jnp.exp(m - m_new)
                    acc = _load_block(acc_s, cur, run_len) * a_old + acc * a_new
                    l = _load_block(l_s, cur, run_len) * a_old + l * a_new
                    m = m_new
                _store_block(acc_s, cur, run_len, acc)
                _store_block(m_s, cur, run_len, m)
                _store_block(l_s, cur, run_len, l)
                return carry

            lax.fori_loop(0, n_iter, step, 0, unroll=ATTN_FWD_UNROLL)

        acc_s[...] = acc_s[...] / l_s[...]
        _ungroup(stage, acc_s, L16)
        o_ref[...] = stage[...].astype(BF16)
        m_s[...] = m_s[...] + jnp.log(l_s[...])
        _ungroup(lse_ref, m_s, L16)

    col = lambda part: pl.BlockSpec((S, PAIR), lambda hp: (0, part * N_PAIRS + hp))
    out = pl.BlockSpec((S, PAIR), lambda hp: (0, hp))
    return pl.pallas_call(
        body, name=name, grid=(N_PAIRS,),
        in_specs=[col(0), col(1), col(2), pl.BlockSpec((6, 2, ATT_BLOCK, 2 * ATT_BLOCK), lambda hp: (0, hp, 0, 0))],
        out_specs=[out, out], out_shape=[_sds((S, D_MODEL), BF16), _sds((S, D_MODEL), F32)],
        scratch_shapes=[pltpu.VMEM((S, PAIR), F32)] * 7,
        compiler_params=pltpu.CompilerParams(dimension_semantics=("parallel",), vmem_limit_bytes=ATTN_VMEM_LIMIT_BYTES),
    )(qkvn, qkvn, qkvn, bias)


def attn_bwd(qkvn, att, datt, lse, bias, name):
    S = qkvn.shape[0]
    L16 = S // RESIDUES
    n_iter = S // ATT_BLOCK
    TILE = 512

    def body(q_ref, k_ref, v_ref, o_ref, do_ref, lse_ref, b_ref, dq_ref, dk_ref, dv_ref, db_ref,
             qp, kp, vp, dop, ldp, dqp, dkp, dvp):
        stage = dqp
        for src, dst in ((q_ref, qp), (k_ref, kp), (v_ref, vp), (do_ref, dop)):
            stage[...] = src[...].astype(F32)
            _regroup(dst, stage, L16)

        def pack(i, carry):
            rows = pl.ds(pl.multiple_of(i * TILE, TILE), TILE)
            low = _low_lanes((TILE, PAIR))
            lane = lax.broadcasted_iota(jnp.int32, (TILE, PAIR), 1)
            prod = do_ref[rows, :].astype(F32) * o_ref[rows, :].astype(F32)
            d0 = jnp.sum(jnp.where(low, prod, 0.0), axis=-1, keepdims=True)
            d1 = jnp.sum(jnp.where(low, 0.0, prod), axis=-1, keepdims=True)
            stage[rows, :] = jnp.where((lane & (HEAD_DIM // 2)) == 0, lse_ref[rows, :], jnp.where(low, d0, d1))
            return carry

        lax.fori_loop(0, S // TILE, pack, 0)
        _regroup(ldp, stage, L16)
        dqp[...] = jnp.zeros_like(dqp)
        dkp[...] = jnp.zeros_like(dkp)
        dvp[...] = jnp.zeros_like(dvp)
        db_ref[...] = jnp.zeros_like(db_ref)
        low = _low_lanes((ATT_BLOCK, PAIR))

        for branch in BRANCH_ORDER:
            _, _, run_len, _ = _branch_geometry(branch, S)

            def step(it, carry, branch=branch, run_len=run_len):
                cur, prev, variant = _block_rows(it, branch, S)
                q = _load_block(qp, cur, run_len).astype(BF16)
                dout = _load_block(dop, cur, run_len).astype(BF16)
                ld = _load_block(ldp, cur, run_len)
                k = jnp.concatenate([_load_block(kp, prev, run_len), _load_block(kp, cur, run_len)], axis=0).astype(BF16)
                v = jnp.concatenate([_load_block(vp, prev, run_len), _load_block(vp, cur, run_len)], axis=0).astype(BF16)
                half = HEAD_DIM // 2
                lse2 = jnp.concatenate([ld[:, 0:1], ld[:, HEAD_DIM:HEAD_DIM + 1]], axis=0)
                delta2 = jnp.concatenate([ld[:, half:half + 1], ld[:, HEAD_DIM + half:HEAD_DIM + half + 1]], axis=0)
                q2, do2 = _stack_heads(q, low), _stack_heads(dout, low)
                s = lax.dot_general(q2, k, _NT, preferred_element_type=F32) * (HEAD_DIM ** -0.5)
                p = jnp.exp(s + b_ref[2 * branch + variant].reshape(2 * ATT_BLOCK, 2 * ATT_BLOCK) - lse2)
                dp = lax.dot_general(do2, v, _NT, preferred_element_type=F32)
                ds = p * (dp - delta2)
                db_ref[branch] += ds.reshape(2, ATT_BLOCK, 2 * ATT_BLOCK)
                dsb = (ds * (HEAD_DIM ** -0.5)).astype(BF16)
                dq = _unstack_heads(jnp.dot(dsb, k, preferred_element_type=F32), low)
                dk = lax.dot_general(dsb, q2, _TN, preferred_element_type=F32)
                dv = lax.dot_general(p.astype(BF16), do2, _TN, preferred_element_type=F32)
                _store_block(dqp, cur, run_len, dq, add=True)
                _store_block(dkp, prev, run_len, dk[:ATT_BLOCK], add=True)
                _store_block(dvp, prev, run_len, dv[:ATT_BLOCK], add=True)
                _store_block(dkp, cur, run_len, dk[ATT_BLOCK:], add=True)
                _store_block(dvp, cur, run_len, dv[ATT_BLOCK:], add=True)
                return carry

            lax.fori_loop(0, n_iter, step, 0, unroll=ATTN_BWD_UNROLL)

        _ungroup(dq_ref, dqp, L16)
        _ungroup(dk_ref, dkp, L16)
        _ungroup(dv_ref, dvp, L16)

    col = lambda part: pl.BlockSpec((S, PAIR), lambda hp: (0, part * N_PAIRS + hp))
    one = pl.BlockSpec((S, PAIR), lambda hp: (0, hp))
    return pl.pallas_call(
        body, name=name, grid=(N_PAIRS,),
        in_specs=[col(0), col(1), col(2), one, one, one,
                  pl.BlockSpec((6, 2, ATT_BLOCK, 2 * ATT_BLOCK), lambda hp: (0, hp, 0, 0))],
        out_specs=[one, one, one, pl.BlockSpec((3, 2, ATT_BLOCK, 2 * ATT_BLOCK), lambda hp: (0, hp, 0, 0))],
        out_shape=[_sds((S, D_MODEL), F32)] * 3 + [_sds((3, N_HEADS, ATT_BLOCK, 2 * ATT_BLOCK), F32)],
        scratch_shapes=[pltpu.VMEM((S, PAIR), F32)] * 8,
        compiler_params=pltpu.CompilerParams(dimension_semantics=("parallel",), vmem_limit_bytes=ATTN_VMEM_LIMIT_BYTES),
    )(qkvn, qkvn, qkvn, att, datt, lse, bias)


def loss_grad(y, target, name, ts=512):
    S, Dm = y.shape

    def body(y_ref, t_ref, d_ref, db_ref, s_ref):
        e = y_ref[...] - t_ref[...]
        d = e * (1.0 / Dm)
        d_ref[...] = d
        db_ref[...] = d.astype(BF16)
        part = jnp.sum(e * e, axis=0, keepdims=True)

        @pl.when(pl.program_id(0) == 0)
        def _():
            s_ref[...] = part

        @pl.when(pl.program_id(0) > 0)
        def _():
            s_ref[...] += part

    row = pl.BlockSpec((ts, Dm), lambda i: (i, 0))
    vec = pl.BlockSpec((1, Dm), lambda i: (0, 0))
    return pl.pallas_call(
        body, name=name, grid=(S // ts,), in_specs=[row, row], out_specs=[row, row, vec],
        out_shape=[_sds((S, Dm), F32), _sds((S, Dm), BF16), _sds((1, Dm), F32)], compiler_params=_params("arbitrary"),
    )(y, target)


def adamw(w, g, m, v, name):
    n, R, C = w.shape

    def body(w_ref, g_ref, m_ref, v_ref, d_ref, nm_ref, nv_ref):
        gv = g_ref[...]
        m2 = ADAM_B1 * m_ref[...] + (1.0 - ADAM_B1) * gv
        v2 = ADAM_B2 * v_ref[...] + (1.0 - ADAM_B2) * (gv * gv)
        m_hat = m2 / (1.0 - ADAM_B1 ** ADAM_STEP)
        v_hat = v2 / (1.0 - ADAM_B2 ** ADAM_STEP)
        d_ref[...] = -ADAM_LR * (m_hat / (jnp.sqrt(v_hat) + ADAM_EPS) + ADAM_WD * w_ref[...])
        nm_ref[...] = m2
        nv_ref[...] = v2

    tr = R
    while tr * C * 4 > (1 << 21) and tr % 16 == 0:
        tr //= 2
    spec = pl.BlockSpec((None, tr, C), lambda i, r: (i, r, 0))
    return pl.pallas_call(
        body, name=name, grid=(n, R // tr), in_specs=[spec] * 4, out_specs=[spec] * 3,
        out_shape=[_sds((n, R, C), F32)] * 3, compiler_params=_params("parallel", "parallel"),
    )(w, g, m, v)


ANY = pl.BlockSpec(memory_space=pl.ANY)


def _coords():
    return lax.axis_index("x"), lax.axis_index("y"), lax.axis_index("c")


def _other_chips(mx, my):
    return [(1 - mx, my), (mx, 1 - my), (1 - mx, 1 - my)]


def _remote(src, dst, send, recv, dev):
    return pltpu.make_async_remote_copy(src_ref=src, dst_ref=dst, send_sem=send, recv_sem=recv, device_id=dev,
                                        device_id_type=MESH)


def allgather_devices(x, name):
    R, C = x.shape

    def body(x_ref, o_ref, send, recv, local_sem):
        mx, my, mc = _coords()
        me = 4 * mx + 2 * my + mc
        local = pltpu.make_async_copy(x_ref, o_ref.at[me], local_sem)
        local.start()
        peers = []
        for k in range(1, N_DEV):
            px = 1 - mx if k & 4 else mx
            py = 1 - my if k & 2 else my
            pc = 1 - mc if k & 1 else mc
            peers.append((px, py, pc))
        sends = [_remote(x_ref, o_ref.at[me], send.at[k], recv.at[k], p) for k, p in enumerate(peers)]
        for cp in sends:
            cp.start()
        for k, (px, py, pc) in enumerate(peers):
            _remote(x_ref, o_ref.at[4 * px + 2 * py + pc], send.at[k], recv.at[k], (px, py, pc)).wait_recv()
        for cp in sends:
            cp.wait_send()
        local.wait()

    return pl.pallas_call(
        body, name=name, in_specs=[ANY], out_specs=ANY, out_shape=_sds((N_DEV, R, C), x.dtype),
        scratch_shapes=[pltpu.SemaphoreType.DMA((N_DEV - 1,)), pltpu.SemaphoreType.DMA((N_DEV - 1,)),
                        pltpu.SemaphoreType.DMA],
    )(x)


HBM = pl.BlockSpec(memory_space=pltpu.HBM)
SEM = pl.BlockSpec(memory_space=pltpu.SEMAPHORE)
_SPLIT_COPY = pltpu.CompilerParams(has_side_effects=pltpu.SideEffectType.DATAFLOW_SIDE_EFFECTING)


def _in_hbm(a):
    return pltpu.with_memory_space_constraint(a, pltpu.HBM)


def cast_into_slot(w, layer, chip_core, name):
    _, _, hR, C = w.shape

    def body(s_ref, w_ref, o_ref):
        del s_ref
        o_ref[...] = w_ref[...].astype(BF16)

    grid_spec = pltpu.PrefetchScalarGridSpec(
        num_scalar_prefetch=1, grid=(2,),
        in_specs=[pl.BlockSpec((None, None, hR, C), lambda h, s: (layer, h, 0, 0))],
        out_specs=pl.BlockSpec((None, None, hR, C), lambda h, s: (s[0], h, 0, 0)))
    return pl.pallas_call(body, name=name, grid_spec=grid_spec, out_shape=_sds((N_CHIPS, 2, hR, C), BF16),
                          compiler_params=_params("parallel"))(chip_core, w)


def gather_start(lands, groups, name):
    n = len(lands)
    n_groups = len(groups)

    def body(*refs):
        ins = refs[:n]
        sems = refs[n:n + 2 * n_groups]
        token = refs[-1]
        mx, my, mc = _coords()
        chip = 2 * mx + my
        for g, members in enumerate(groups):
            send, recv = sems[2 * g], sems[2 * g + 1]
            for i, a in enumerate(members):
                mine = ins[a].at[chip, mc]
                for k, (px, py) in enumerate(_other_chips(mx, my)):
                    _remote(mine, mine, send.at[3 * i + k], recv.at[3 * i + k], (px, py, mc)).start()
        token[...] = jnp.zeros_like(token)

    sem_shapes = []
    for members in groups:
        sem_shapes += [pltpu.SemaphoreType.DMA((3 * len(members),))] * 2
    outs = pl.pallas_call(
        body, name=name, in_specs=[HBM] * n,
        out_specs=[SEM] * (2 * n_groups) + [HBM] * n + [pl.BlockSpec(memory_space=pltpu.VMEM)],
        out_shape=sem_shapes + [pltpu.HBM(a.shape, a.dtype) for a in lands] + [_sds((SUBLANES, LANES), F32)],
        input_output_aliases={a: 2 * n_groups + a for a in range(n)}, compiler_params=_SPLIT_COPY,
    )(*[_in_hbm(a) for a in lands])
    sems = [(outs[2 * g], outs[2 * g + 1]) for g in range(n_groups)]
    return sems, list(outs[2 * n_groups:2 * n_groups + n]), outs[-1]


def gather_forward(lands, sems, after, name):
    n = len(lands)

    def body(*refs):
        ins = refs[:n]
        send, recv = refs[n], refs[n + 1]
        fsend, frecv = refs[n + 3], refs[n + 4]
        mx, my, mc = _coords()
        for i in range(n):
            for k, (px, py) in enumerate(_other_chips(mx, my)):
                landed = ins[i].at[2 * px + py, mc]
                cp = _remote(landed, landed, send.at[3 * i + k], recv.at[3 * i + k], (px, py, mc))
                cp.wait_send()
                cp.wait_recv()
                _remote(landed, landed, fsend.at[3 * i + k], frecv.at[3 * i + k], (mx, my, 1 - mc)).start()

    outs = pl.pallas_call(
        body, name=name, in_specs=[HBM] * n + [SEM, SEM, ANY], out_specs=[SEM, SEM] + [HBM] * n,
        out_shape=[pltpu.SemaphoreType.DMA((3 * n,))] * 2 + [pltpu.HBM(a.shape, a.dtype) for a in lands],
        input_output_aliases={a: 2 + a for a in range(n)}, compiler_params=_SPLIT_COPY,
    )(*lands, sems[0], sems[1], after)
    return (outs[0], outs[1]), list(outs[2:])


def gather_wait(lands, sems, after, name):
    n = len(lands)

    def body(*refs):
        ins = refs[:n]
        fsend, frecv = refs[n], refs[n + 1]
        mx, my, mc = _coords()
        for i in range(n):
            for k, (px, py) in enumerate(_other_chips(mx, my)):
                theirs = ins[i].at[2 * px + py, 1 - mc]
                cp = _remote(theirs, theirs, fsend.at[3 * i + k], frecv.at[3 * i + k], (mx, my, 1 - mc))
                cp.wait_send()
                cp.wait_recv()

    outs = pl.pallas_call(
        body, name=name, in_specs=[HBM] * n + [SEM, SEM, ANY], out_specs=[HBM] * n,
        out_shape=[pltpu.HBM(a.shape, a.dtype) for a in lands],
        input_output_aliases={a: a for a in range(n)}, compiler_params=_SPLIT_COPY,
    )(*lands, sems[0], sems[1], after)
    return list(outs)


def _peers(mx, my, mc):
    return [(1 - mx if k & 4 else mx, 1 - my if k & 2 else my, 1 - mc if k & 1 else mc) for k in range(1, N_DEV)]


def devices_start(x, name):
    def body(x_ref, land_ref, send, recv, x_thru, land_thru):
        mx, my, mc = _coords()
        me = 4 * mx + 2 * my + mc
        for k, peer in enumerate(_peers(mx, my, mc)):
            _remote(x_ref, land_ref.at[me], send.at[k], recv.at[k], peer).start()

    land = lax.empty((N_DEV,) + x.shape, x.dtype)
    outs = pl.pallas_call(
        body, name=name, in_specs=[HBM, HBM], out_specs=[SEM, SEM, HBM, HBM],
        out_shape=[pltpu.SemaphoreType.DMA((N_DEV - 1,))] * 2 + [pltpu.HBM(x.shape, x.dtype), pltpu.HBM(land.shape, x.dtype)],
        input_output_aliases={0: 2, 1: 3}, compiler_params=_SPLIT_COPY,
    )(_in_hbm(x), _in_hbm(land))
    return (outs[0], outs[1]), outs[2], outs[3]


def devices_wait(x, land, sems, after, name):
    def body(x_ref, land_ref, send, recv, after_ref, x_thru, land_thru):
        mx, my, mc = _coords()
        for k, (px, py, pc) in enumerate(_peers(mx, my, mc)):
            cp = _remote(x_ref, land_ref.at[4 * px + 2 * py + pc], send.at[k], recv.at[k], (px, py, pc))
            cp.wait_send()
            cp.wait_recv()

    outs = pl.pallas_call(
        body, name=name, in_specs=[HBM, HBM, SEM, SEM, ANY], out_specs=[HBM, HBM],
        out_shape=[pltpu.HBM(x.shape, x.dtype), pltpu.HBM(land.shape, land.dtype)],
        input_output_aliases={0: 0, 1: 1}, compiler_params=_SPLIT_COPY,
    )(x, land, sems[0], sems[1], after)
    return outs[0], outs[1]


def device_sum(land, own, me, name):
    _, R, C = land.shape

    def body(s_ref, l_ref, o_ref_in, o_ref):
        acc = None
        for q in range(N_DEV):
            term = jnp.where(s_ref[0] == q, o_ref_in[...], l_ref[q])
            acc = term if acc is None else acc + term
        o_ref[...] = acc

    grid_spec = pltpu.PrefetchScalarGridSpec(
        num_scalar_prefetch=1, grid=(1,),
        in_specs=[pl.BlockSpec((N_DEV, R, C), lambda i, s: (0, 0, 0)), pl.BlockSpec((R, C), lambda i, s: (0, 0))],
        out_specs=pl.BlockSpec((R, C), lambda i, s: (0, 0)))
    return pl.pallas_call(body, name=name, grid_spec=grid_spec, out_shape=_sds((R, C), F32),
                          compiler_params=_params("arbitrary"))(me, land, own)


def reduce_send(grads, name):
    n = len(grads)

    def body(*refs):
        ins, lands = refs[:n], refs[n:2 * n]
        send, recv = refs[2 * n], refs[2 * n + 1]
        mx, my, mc = _coords()
        me = 4 * mx + 2 * my + mc
        for a in range(n):
            for k, (px, py, pc) in enumerate(_peers(mx, my, mc)):
                _remote(ins[a].at[2 * px + py, pc], lands[a].at[me], send.at[7 * a + k], recv.at[7 * a + k], (px, py, pc)).start()

    lands = [lax.empty((N_DEV,) + g.shape[2:], g.dtype) for g in grads]
    outs = pl.pallas_call(
        body, name=name, in_specs=[HBM] * (2 * n), out_specs=[SEM, SEM] + [HBM] * (2 * n),
        out_shape=[pltpu.SemaphoreType.DMA((7 * n,))] * 2 + [pltpu.HBM(a.shape, a.dtype) for a in grads + lands],
        input_output_aliases={a: 2 + a for a in range(2 * n)}, compiler_params=_SPLIT_COPY,
    )(*[_in_hbm(a) for a in grads + lands])
    return (outs[0], outs[1]), list(outs[2:2 + n]), list(outs[2 + n:])


def reduce_wait(grads, lands, sems, after, name):
    n = len(grads)

    def body(*refs):
        ins, zones = refs[:n], refs[n:2 * n]
        send, recv = refs[2 * n], refs[2 * n + 1]
        mx, my, mc = _coords()
        for a in range(n):
            for k, (px, py, pc) in enumerate(_peers(mx, my, mc)):
                cp = _remote(ins[a].at[2 * px + py, pc], zones[a].at[4 * px + 2 * py + pc], send.at[7 * a + k],
                             recv.at[7 * a + k], (px, py, pc))
                cp.wait_send()
                cp.wait_recv()

    outs = pl.pallas_call(
        body, name=name, in_specs=[HBM] * (2 * n) + [SEM, SEM, ANY], out_specs=[HBM] * (2 * n),
        out_shape=[pltpu.HBM(a.shape, a.dtype) for a in grads + lands],
        input_output_aliases={a: a for a in range(2 * n)}, compiler_params=_SPLIT_COPY,
    )(*grads, *lands, sems[0], sems[1], after)
    return list(outs[:n]), list(outs[n:])


def reduce_sum(land, grad, place, name, into=None, layer=None):
    _, hR, C = land.shape
    tr = hR
    while N_DEV * tr * C * 2 > (6 << 20) and tr % 32 == 0:
        tr //= 2

    def body(s_ref, l_ref, g_ref, *rest):
        o_ref = rest[-1]
        own = g_ref[...].astype(F32)
        acc = None
        for q in range(N_DEV):
            term = jnp.where(s_ref[2] == q, own, l_ref[q].astype(F32))
            acc = term if acc is None else acc + term
        o_ref[...] = acc

    in_specs = [pl.BlockSpec((N_DEV, tr, C), lambda i, s: (0, i, 0)),
                pl.BlockSpec((None, None, tr, C), lambda i, s: (s[0], s[1], i, 0))]
    args = [place, land, grad]
    aliases = {}
    if layer is None:
        out_spec = pl.BlockSpec((None, tr, C), lambda i, s: (s[1], i, 0))
        out_shape = _sds((2, hR, C), F32)
    else:
        out_spec = pl.BlockSpec((None, None, tr, C), lambda i, s: (layer, s[1], i, 0))
        out_shape = _sds((2, 2, hR, C), F32)
        if into is not None:
            in_specs.append(ANY)
            args.append(into)
            aliases = {3: 0}
    grid_spec = pltpu.PrefetchScalarGridSpec(num_scalar_prefetch=1, grid=(hR // tr,), in_specs=in_specs, out_specs=out_spec)
    return pl.pallas_call(body, name=name, grid_spec=grid_spec, out_shape=out_shape, input_output_aliases=aliases,
                          compiler_params=_params("arbitrary"))(*args)


def join_halves(arrays, name):
    n = len(arrays)
    pieces = [(a, l) for a, arr in enumerate(arrays) for l in (range(arr.shape[0]) if arr.ndim == 4 else [None])]

    def body(*refs):
        ins = refs[:n]
        send, recv = refs[2 * n:]
        mx, my, mc = _coords()

        def half(a, l, h):
            return ins[a].at[h] if l is None else ins[a].at[l, h]

        sends = [_remote(half(a, l, mc), half(a, l, mc), send.at[i], recv.at[i], (mx, my, 1 - mc))
                 for i, (a, l) in enumerate(pieces)]
        for cp in sends:
            cp.start()
        for i, (a, l) in enumerate(pieces):
            theirs = half(a, l, 1 - mc)
            _remote(theirs, theirs, send.at[i], recv.at[i], (mx, my, 1 - mc)).wait_recv()
        for cp in sends:
            cp.wait_send()

    return pl.pallas_call(
        body, name=name, in_specs=[ANY] * n, out_specs=[ANY] * n, out_shape=[_sds(a.shape, a.dtype) for a in arrays],
        input_output_aliases={a: a for a in range(n)},
        scratch_shapes=[pltpu.SemaphoreType.DMA((len(pieces),)), pltpu.SemaphoreType.DMA((len(pieces),))],
    )(*arrays)


LANES = 128
SUBLANES = 8


def _n_rows(shape):
    rows = -(-int(np.prod(shape)) // LANES)
    return -(-rows // SUBLANES) * SUBLANES


def _as_rows(a):
    flat = a.reshape(-1)
    rows = _n_rows(a.shape)
    return jnp.pad(flat, (0, rows * LANES - flat.shape[0])).reshape(rows, LANES)


def _pack(arrays):
    return jnp.concatenate([_as_rows(a) for a in arrays], axis=0)


def _unpack(rows, shapes):
    out, r0 = [], 0
    for s in shapes:
        n = _n_rows(s)
        out.append(rows[r0:r0 + n].reshape(-1)[:int(np.prod(s))].reshape(s))
        r0 += n
    return out


REPLICATED_SMALL = [("rel_bias", (32, 16)), ("even_norm", (1, 1024)), ("even_pool_w", (1, 4, 128, 128)),
                    ("even_pool_scale", (1, 512)), ("odd_q_norm", (1, 64)), ("odd_k_norm", (1, 64)),
                    ("ffn_norm", (2, 1024)), ("ffn_conv_b", (2, 5632))]
SHARDED_SMALL = [("even_conv_w", (1, 3, 128)), ("odd_norm", (1, 256)), ("ffn_conv_w", (2, 3, 1408))]
BIG = ["even_w_in", "even_w_out", "odd_w_qkv", "odd_w_o", "ffn_w_up", "ffn_w_down"]
WEIGHT_ORDER = ["rel_bias", "even_norm", "even_w_in", "even_conv_w", "even_pool_w", "even_pool_scale", "even_w_out",
                "odd_norm", "odd_w_qkv", "odd_q_norm", "odd_k_norm", "odd_w_o", "ffn_norm", "ffn_w_up", "ffn_conv_w",
                "ffn_conv_b", "ffn_w_down"]


def kernel(x, rel_bias, even_norm, even_w_in, even_conv_w, even_pool_w, even_pool_scale, even_w_out, odd_norm, odd_w_qkv, odd_q_norm, odd_k_norm, odd_w_o, ffn_norm, ffn_w_up, ffn_conv_w, ffn_conv_b, ffn_w_down, loss_target, m_rel_bias, m_even_norm, m_even_w_in, m_even_conv_w, m_even_pool_w, m_even_pool_scale, m_even_w_out, m_odd_norm, m_odd_w_qkv, m_odd_q_norm, m_odd_k_norm, m_odd_w_o, m_ffn_norm, m_ffn_w_up, m_ffn_conv_w, m_ffn_conv_b, m_ffn_w_down, v_rel_bias, v_even_norm, v_even_w_in, v_even_conv_w, v_even_pool_w, v_even_pool_scale, v_even_w_out, v_odd_norm, v_odd_w_qkv, v_odd_q_norm, v_odd_k_norm, v_odd_w_o, v_ffn_norm, v_ffn_w_up, v_ffn_conv_w, v_ffn_conv_b, v_ffn_w_down):
    W = dict(rel_bias=rel_bias, even_norm=even_norm, even_w_in=even_w_in, even_conv_w=even_conv_w, even_pool_w=even_pool_w,
             even_pool_scale=even_pool_scale, even_w_out=even_w_out, odd_norm=odd_norm, odd_w_qkv=odd_w_qkv,
             odd_q_norm=odd_q_norm, odd_k_norm=odd_k_norm, odd_w_o=odd_w_o, ffn_norm=ffn_norm, ffn_w_up=ffn_w_up,
             ffn_conv_w=ffn_conv_w, ffn_conv_b=ffn_conv_b, ffn_w_down=ffn_w_down)
    M1 = dict(rel_bias=m_rel_bias, even_norm=m_even_norm, even_w_in=m_even_w_in, even_conv_w=m_even_conv_w,
              even_pool_w=m_even_pool_w, even_pool_scale=m_even_pool_scale, even_w_out=m_even_w_out, odd_norm=m_odd_norm,
              odd_w_qkv=m_odd_w_qkv, odd_q_norm=m_odd_q_norm, odd_k_norm=m_odd_k_norm, odd_w_o=m_odd_w_o,
              ffn_norm=m_ffn_norm, ffn_w_up=m_ffn_w_up, ffn_conv_w=m_ffn_conv_w, ffn_conv_b=m_ffn_conv_b,
              ffn_w_down=m_ffn_w_down)
    M2 = dict(rel_bias=v_rel_bias, even_norm=v_even_norm, even_w_in=v_even_w_in, even_conv_w=v_even_conv_w,
              even_pool_w=v_even_pool_w, even_pool_scale=v_even_pool_scale, even_w_out=v_even_w_out, odd_norm=v_odd_norm,
              odd_w_qkv=v_odd_w_qkv, odd_q_norm=v_odd_q_norm, odd_k_norm=v_odd_k_norm, odd_w_o=v_odd_w_o,
              ffn_norm=v_ffn_norm, ffn_w_up=v_ffn_w_up, ffn_conv_w=v_ffn_conv_w, ffn_conv_b=v_ffn_conv_b,
              ffn_w_down=v_ffn_w_down)
    mx, my, mc = _coords()
    chip = 2 * mx + my
    me = 4 * mx + 2 * my + mc
    place = jnp.stack([chip, mc, me]).astype(jnp.int32)
    xs, target = x[0], loss_target[0]

    def halves(w):
        return w.reshape((w.shape[0], 2, w.shape[-2] // 2, w.shape[-1]))

    lands = [cast_into_slot(halves(even_w_in), 0, place, "cast_w_in"), cast_into_slot(halves(even_w_out), 0, place, "cast_w_out"),
             cast_into_slot(halves(ffn_w_up), 0, place, "cast_w_up0"), cast_into_slot(halves(ffn_w_down), 0, place, "cast_w_down0"),
             cast_into_slot(halves(odd_w_qkv), 0, place, "cast_w_qkv"), cast_into_slot(halves(odd_w_o), 0, place, "cast_w_o"),
             cast_into_slot(halves(ffn_w_up), 1, place, "cast_w_up1"), cast_into_slot(halves(ffn_w_down), 1, place, "cast_w_down1")]
    small = allgather_devices(_pack([even_conv_w, odd_norm, ffn_conv_w]), "allgather_small_weights")
    lands[0], small = lax.optimization_barrier((lands[0], small))
    gather_sems, lands, token = gather_start(lands, [[0, 1], [2, 3], [4, 5], [6, 7]], "gather_start")
    even_norm_after_start = even_norm + token[0:1, 0:1]
    small = small[0::2]
    conv_w_full = small[:, 0:3].transpose(1, 0, 2).reshape(3, A_WIDTH)
    odd_norm_full = small[:, 8:10].reshape(1, D_MODEL)
    ffn_cw_full = small[:, 16:82].reshape(N_CHIPS, 2, 3, 2 * D_FF // N_CHIPS).transpose(1, 2, 0, 3).reshape(2, 3, 2 * D_FF)
    pool_w = cast_bf16(even_pool_w[0], "cast_pool_w")
    gqk = jnp.stack([jnp.tile(odd_q_norm[0], N_HEADS), jnp.tile(odd_k_norm[0], N_HEADS),
                     jnp.ones((D_MODEL,), F32)])[:, None, :]
    bias = bias_expand(rel_bias.T, "bias_expand").reshape(6, N_HEADS, ATT_BLOCK, 2 * ATT_BLOCK)

    def ffn_fwd(l, xin):
        xn = rmsnorm_fwd(xin, ffn_norm[l:l + 1], f"ffn{l}_norm")
        up = mm_nn(xn, w_up[l], f"ffn{l}_up", out_dtype=BF16)
        u, act = glu_fwd(up, ffn_cw_full[l], ffn_conv_b[l:l + 1], f"ffn{l}_glu")
        return mm_nn(act, w_down[l], f"ffn{l}_down", res=xin), (xin, xn, up, u, act)

    def gathered(group, tag, after_landing, after_passing):
        sems, arrays = gather_forward(lands[2 * group:2 * group + 2], gather_sems[group], after_landing, "gather_forward_" + tag)
        return gather_wait(arrays, sems, after_passing, "gather_wait_" + tag)

    def ffn_weights(got):
        return got[0].reshape(N_CHIPS, 1, D_MODEL, 2 * D_FF // N_CHIPS), got[1].reshape(1, 1, D_FF, D_MODEL)

    w_up, w_down = [None, None], [None, None]
    xn0 = rmsnorm_fwd(xs, even_norm_after_start, "even_norm")
    got = gathered(0, "even", bias, xn0)
    w_in = got[0].reshape(N_CHIPS, 1, D_MODEL, EVEN_IN // N_CHIPS)
    w_out = got[1].reshape(1, 1, D_MODEL, D_MODEL)
    proj = mm_nn(xn0, w_in, "even_in")
    mix = mixer_fwd(proj, conv_w_full, pool_w, even_pool_scale, "even_mixer")
    x1 = mm_nn(mix, w_out, "even_out", res=xs)
    w_up[0], w_down[0] = ffn_weights(gathered(1, "ffn0", proj, x1))
    x2, ffn0 = ffn_fwd(0, x1)
    got = gathered(2, "odd", x1, x2)
    w_qkv = got[0].reshape(N_CHIPS, 1, D_MODEL, 3 * D_MODEL // N_CHIPS)
    w_o = got[1].reshape(1, 1, D_MODEL, D_MODEL)
    xn2 = rmsnorm_fwd(x2, odd_norm_full, "odd_norm")
    qkv = mm_nn(xn2, w_qkv, "odd_qkv")
    qkvn = qknorm_fwd(qkv, gqk, "odd_qknorm")
    att, lse = attn_fwd(qkvn, bias, "attn_fwd")
    x3 = mm_nn(att, w_o, "odd_out", res=x2)
    w_up[1], w_down[1] = ffn_weights(gathered(3, "ffn1", x2, x3))
    x4, ffn1 = ffn_fwd(1, x3)

    dy, dyb, sq = loss_grad(x4, target, "loss")
    loss = lax.psum(0.5 * jnp.sum(sq) * (1.0 / D_MODEL), ("x", "y", "c"))

    def ffn_bwd(l, dy, dyb, saved):
        xin, xn, up, u, act = saved
        dw_down = mm_tn(act, dyb, f"ffn{l}_dw_down", J=1, tk=D_FF // 2, tm=1024)
        dact = mm_nt(dyb, w_down[l], f"ffn{l}_dact", tr=D_FF // 2, out_dtype=BF16, tm=1024)
        dup, dcw, dcb = glu_bwd(up, u, dact, ffn_cw_full[l], f"ffn{l}_glu_bwd")
        dw_up = mm_tn(xn, dup, f"ffn{l}_dw_up", J=N_CHIPS, tk=256, tm=1024)
        dx, dxb, dg = mm_nt_norm_bwd(dup, w_up[l], xin, ffn_norm[l:l + 1], dy, f"ffn{l}_dx")
        return dx, dxb, (dw_down, dw_up, dcw, dcb, dg)

    def quarters(g):
        return g.reshape(N_CHIPS, 2, g.shape[0] * g.shape[1] // (2 * N_CHIPS), g.shape[-1])

    def reduce_start(grads, tag, then):
        sems, parts, zones = reduce_send([quarters(g) for g in grads], "reduce_send_" + tag)
        then, parts = lax.optimization_barrier((then, parts))
        return (sems, parts, zones), then

    dx3, dx3b, g_ffn1 = ffn_bwd(1, dy, dyb, ffn1)
    red_ffn1, (dx3, dx3b) = reduce_start([g_ffn1[1], g_ffn1[0]], "ffn1", (dx3, dx3b))
    dw_o = mm_tn(att, dx3b, "odd_dw_o", J=1, tk=512)
    datt = mm_nt(dx3b, w_o, "odd_datt", tr=D_MODEL, out_dtype=BF16)
    dq, dk, dv, dbias = attn_bwd(qkvn, att, datt, lse, bias, "attn_bwd")
    dqkv, dgqk = qknorm_bwd(qkv, dq, dk, dv, gqk, "odd_qknorm_bwd")
    dw_qkv = mm_tn(xn2, dqkv, "odd_dw_qkv", J=N_CHIPS, tk=256, tm=1024)
    red_odd, dqkv = reduce_start([dw_qkv, dw_o], "odd", dqkv)
    dx2, dx2b, dg_odd = mm_nt_norm_bwd(dqkv, w_qkv, x2, odd_norm_full, dx3, "odd_dx")
    dx1, dx1b, g_ffn0 = ffn_bwd(0, dx2, dx2b, ffn0)
    red_ffn0, (dx1, dx1b) = reduce_start([g_ffn0[1], g_ffn0[0]], "ffn0", (dx1, dx1b))
    dw_out = mm_tn(mix, dx1b, "even_dw_out", J=1, tk=512)
    dmix = mm_nt(dx1b, w_out, "even_dmix", tr=D_MODEL)
    dproj, dcw_even, dpw, dps = mixer_bwd(proj, dmix, conv_w_full, pool_w, even_pool_scale, "even_mixer_bwd")
    dw_in = mm_tn(xn0, dproj, "even_dw_in", J=N_CHIPS, tk=256, tm=1024)
    grad_x, _, dg_even = mm_nt_norm_bwd(dproj, w_in, xs, even_norm, dx1, "even_dx")
    d_rel = jnp.sum(bias_reduce(dbias.reshape(3, N_HEADS, 2 * ATT_BLOCK * ATT_BLOCK), "bias_reduce"), axis=0).T

    red_even, grad_x = reduce_start([dw_in, dw_out], "even", grad_x)

    dcw_sh = dcw_even.reshape(3, N_CHIPS, A_WIDTH // N_CHIPS).transpose(1, 0, 2)
    don_sh = dg_odd.reshape(N_CHIPS, D_MODEL // N_CHIPS)
    dfcw = jnp.stack([g_ffn0[2], g_ffn1[2]])
    dfcw_sh = dfcw.reshape(2, 3, N_CHIPS, 2 * D_FF // N_CHIPS).transpose(2, 0, 1, 3)
    rep_grads = [d_rel, dg_even, dpw[None], dps, _head_sum(dgqk[0]), _head_sum(dgqk[1]),
                 jnp.concatenate([g_ffn0[4], g_ffn1[4]], axis=0), jnp.concatenate([g_ffn0[3], g_ffn1[3]], axis=0)]
    rep_rows = _pack(rep_grads)
    shard_rows = jnp.concatenate([_pack([dcw_sh[j], don_sh[j], dfcw_sh[j]]) for j in range(N_CHIPS)], axis=0)
    n_rep, n_shard = rep_rows.shape[0], shard_rows.shape[0] // N_CHIPS
    small_sems, small_rows, small_land = devices_start(jnp.concatenate([rep_rows, shard_rows], axis=0), "small_grads_start")
    grad_x, small_rows = lax.optimization_barrier((grad_x, small_rows))

    def reduce_end(red, tag, after):
        sems, parts, zones = red
        parts, zones = reduce_wait(parts, zones, sems, after, "reduce_wait_" + tag)
        return zones, parts

    z_ffn1, p_ffn1 = reduce_end(red_ffn1, "ffn1", grad_x)
    z_odd, p_odd = reduce_end(red_odd, "odd", grad_x)
    z_ffn0, p_ffn0 = reduce_end(red_ffn0, "ffn0", grad_x)
    r_up = reduce_sum(z_ffn0[0], p_ffn0[0], place, "reduce_sum_w_up0", layer=0)
    r_up = reduce_sum(z_ffn1[0], p_ffn1[0], place, "reduce_sum_w_up1", into=r_up, layer=1)
    r_down = reduce_sum(z_ffn0[1], p_ffn0[1], place, "reduce_sum_w_down0", layer=0)
    r_down = reduce_sum(z_ffn1[1], p_ffn1[1], place, "reduce_sum_w_down1", into=r_down, layer=1)
    later = ["odd_w_qkv", "odd_w_o", "ffn_w_up", "ffn_w_down"]
    joined = join_halves([reduce_sum(z_odd[0], p_odd[0], place, "reduce_sum_w_qkv"),
                          reduce_sum(z_odd[1], p_odd[1], place, "reduce_sum_w_o"), r_up, r_down], "grads_join_late_layers")
    G = {nm: g.reshape(W[nm].shape) for nm, g in zip(later, joined)}

    D_, NM, NV = {}, {}, {}

    def update(nm):
        as3 = lambda a: a.reshape((-1,) + a.shape[-2:])
        outs = adamw(as3(W[nm]), as3(G[nm]), as3(M1[nm]), as3(M2[nm]), "adamw_" + nm)
        D_[nm], NM[nm], NV[nm] = [o.reshape(W[nm].shape) for o in outs]

    for nm in later:
        update(nm)
    z_even, p_even = reduce_end(red_even, "even", D_[later[-1]])
    joined = join_halves([reduce_sum(z_even[0], p_even[0], place, "reduce_sum_w_in"),
                          reduce_sum(z_even[1], p_even[1], place, "reduce_sum_w_out")], "grads_join_first_layer")
    for nm, g in zip(["even_w_in", "even_w_out"], joined):
        G[nm] = g.reshape(W[nm].shape)
        update(nm)
    small_rows, small_land = devices_wait(small_rows, small_land, small_sems, D_["even_w_out"], "small_grads_wait")
    small_sum = device_sum(small_land, small_rows, place[2:3], "small_grads_sum")
    mine = lax.dynamic_slice_in_dim(small_sum, n_rep + chip * n_shard, n_shard, axis=0)
    g_small = jnp.concatenate([small_sum[:n_rep], mine], axis=0)
    small_names = [n for n, _ in REPLICATED_SMALL + SHARDED_SMALL]
    small_shapes = [s for _, s in REPLICATED_SMALL + SHARDED_SMALL]
    G.update(dict(zip(small_names, _unpack(g_small, small_shapes))))
    packs = [_pack([d[n] for n in small_names])[None] for d in (W, M1, M2)]
    outs = adamw(packs[0], g_small[None], packs[1], packs[2], "adamw_small")
    for dst, o in zip((D_, NM, NV), outs):
        dst.update(dict(zip(small_names, _unpack(o[0], small_shapes))))

    return (loss, grad_x[None], *[G[n] for n in WEIGHT_ORDER], *[D_[n] for n in WEIGHT_ORDER],
            *[NM[n] for n in WEIGHT_ORDER], *[NV[n] for n in WEIGHT_ORDER])


def _head_sum(dg):
    return jnp.sum(dg.reshape(N_HEADS, HEAD_DIM), axis=0, keepdims=True)
```

```python
import functools
import math

import numpy as np
import jax
import jax.numpy as jnp
from jax import lax
from jax.experimental import pallas as pl
from jax.experimental.pallas import tpu as pltpu

F32 = jnp.float32
BF16 = jnp.bfloat16

D_MODEL = 1024
N_HEADS = 16
HEAD_DIM = 64
A_WIDTH = 512
POOL_WINDOWS = (2, 4, 8, 16)
POOL_GROUP = 128
EVEN_IN = 2048
D_FF = 2816
DILATED_PAIRS = ((128, 1), (512, 4), (2048, 16))
ATT_BLOCK = 128
N_REL_BUCKETS = 32
REL_MAX_DISTANCE = 2048
EPS = 1e-6
MASK_VALUE = -1e30
ADAM_LR, ADAM_B1, ADAM_B2, ADAM_EPS, ADAM_WD, ADAM_STEP = 0.001, 0.9, 0.999, 1e-08, 0.01, 10

VMEM_LIMIT_BYTES = 48 * 1024 * 1024
N_CHIPS = 4
N_DEV = 8
MESH = pl.DeviceIdType.MESH


def _params(*sem):
    return pltpu.CompilerParams(dimension_semantics=sem if sem else None, vmem_limit_bytes=VMEM_LIMIT_BYTES)


def _sds(shape, dtype):
    return jax.ShapeDtypeStruct(tuple(shape), dtype)


def cast_bf16(x, name, tr=None):
    lead, (R, C) = x.shape[:-2], x.shape[-2:]
    n = int(np.prod(lead)) if lead else 1
    x3 = x.reshape((n, R, C))
    tr = tr or R

    def body(x_ref, o_ref):
        o_ref[...] = x_ref[...].astype(BF16)

    out = pl.pallas_call(
        body, name=name, grid=(n, R // tr),
        in_specs=[pl.BlockSpec((None, tr, C), lambda i, r: (i, r, 0))],
        out_specs=pl.BlockSpec((None, tr, C), lambda i, r: (i, r, 0)),
        out_shape=_sds((n, R, C), BF16), compiler_params=_params("parallel", "parallel"),
    )(x3)
    return out.reshape(lead + (R, C))


def rmsnorm_fwd(x, g, name, ts=512):
    S, Dm = x.shape

    def body(x_ref, g_ref, o_ref):
        xv = x_ref[...]
        r = lax.rsqrt(jnp.mean(xv * xv, axis=-1, keepdims=True) + EPS)
        o_ref[...] = ((xv * r) * g_ref[...]).astype(BF16)

    return pl.pallas_call(
        body, name=name, grid=(S // ts,),
        in_specs=[pl.BlockSpec((ts, Dm), lambda i: (i, 0)), pl.BlockSpec((1, Dm), lambda i: (0, 0))],
        out_specs=pl.BlockSpec((ts, Dm), lambda i: (i, 0)),
        out_shape=_sds((S, Dm), BF16), compiler_params=_params("parallel"),
    )(x, g)


def mm_nn(a, w, name, layer=0, res=None, out_dtype=F32, tm=1024):
    M, K = a.shape
    J, _, _, Ns = w.shape

    def body(*refs):
        a_ref, w_ref = refs[0], refs[1]
        o_ref = refs[-1]
        acc = jnp.dot(a_ref[...], w_ref[...], preferred_element_type=F32)
        if res is not None:
            acc = refs[2][...] + acc
        o_ref[...] = acc.astype(o_ref.dtype)

    in_specs = [pl.BlockSpec((tm, K), lambda j, m: (m, 0)),
                pl.BlockSpec((None, None, K, Ns), lambda j, m: (j, layer, 0, 0))]
    args = [a, w]
    if res is not None:
        in_specs.append(pl.BlockSpec((tm, Ns), lambda j, m: (m, j)))
        args.append(res)
    return pl.pallas_call(
        body, name=name, grid=(J, M // tm), in_specs=in_specs,
        out_specs=pl.BlockSpec((tm, Ns), lambda j, m: (m, j)),
        out_shape=_sds((M, J * Ns), out_dtype), compiler_params=_params("parallel", "parallel"),
    )(*args)


def mm_res_norm(a, w, res, gain, name, tm=1024):
    M, K = a.shape
    Dm = w.shape[-1]

    def body(a_ref, w_ref, r_ref, g_ref, y_ref, yn_ref):
        y = r_ref[...] + jnp.dot(a_ref[...], w_ref[...], preferred_element_type=F32)
        y_ref[...] = y
        r = lax.rsqrt(jnp.mean(y * y, axis=-1, keepdims=True) + EPS)
        yn_ref[...] = ((y * r) * g_ref[...]).astype(BF16)

    row = pl.BlockSpec((tm, Dm), lambda m: (m, 0))
    return pl.pallas_call(
        body, name=name, grid=(M // tm,),
        in_specs=[pl.BlockSpec((tm, K), lambda m: (m, 0)),
                  pl.BlockSpec((None, None, K, Dm), lambda m: (0, 0, 0, 0), pipeline_mode=pl.Buffered(1)),
                  row, pl.BlockSpec((1, Dm), lambda m: (0, 0))],
        out_specs=[row, row], out_shape=[_sds((M, Dm), F32), _sds((M, Dm), BF16)],
        compiler_params=_params("parallel"),
    )(a, w, res, gain)


def mm_res_loss(a, w, res, target, name, tm=512):
    M, K = a.shape
    Dm = w.shape[-1]

    def body(a_ref, w_ref, r_ref, t_ref, d_ref, db_ref, s_ref):
        e = (r_ref[...] + jnp.dot(a_ref[...], w_ref[...], preferred_element_type=F32)) - t_ref[...]
        d = e * (1.0 / Dm)
        d_ref[...] = d
        db_ref[...] = d.astype(BF16)
        part = jnp.sum(e * e, axis=0, keepdims=True)

        @pl.when(pl.program_id(0) == 0)
        def _():
            s_ref[...] = part

        @pl.when(pl.program_id(0) > 0)
        def _():
            s_ref[...] += part

    row = pl.BlockSpec((tm, Dm), lambda m: (m, 0))
    return pl.pallas_call(
        body, name=name, grid=(M // tm,),
        in_specs=[pl.BlockSpec((tm, K), lambda m: (m, 0)),
                  pl.BlockSpec((None, None, K, Dm), lambda m: (0, 0, 0, 0), pipeline_mode=pl.Buffered(1)), row, row],
        out_specs=[row, row, pl.BlockSpec((1, Dm), lambda m: (0, 0))],
        out_shape=[_sds((M, Dm), F32), _sds((M, Dm), BF16), _sds((1, Dm), F32)],
        compiler_params=_params("arbitrary"),
    )(a, w, res, target)


def mm_nt(dy, w, name, tr, layer=0, out_dtype=F32, tm=512):
    M = dy.shape[0]
    J, _, R, Ns = w.shape
    dims = (((1,), (1,)), ((), ()))

    def body(dy_ref, w_ref, o_ref):
        acc = None
        for j in range(J):
            p = lax.dot_general(dy_ref[:, j * Ns:(j + 1) * Ns], w_ref[j], dims, preferred_element_type=F32)
            acc = p if acc is None else acc + p
        o_ref[...] = acc.astype(o_ref.dtype)

    return pl.pallas_call(
        body, name=name, grid=(R // tr, M // tm),
        in_specs=[pl.BlockSpec((tm, J * Ns), lambda r, m: (m, 0)),
                  pl.BlockSpec((J, None, tr, Ns), lambda r, m: (0, layer, r, 0))],
        out_specs=pl.BlockSpec((tm, tr), lambda r, m: (m, r)),
        out_shape=_sds((M, R), out_dtype),
        compiler_params=_params("parallel", "parallel"),
    )(dy, w)


def mm_nt_norm_bwd(dy, w, x, g, dres, name, layer=0, tm=512):
    M = dy.shape[0]
    J, _, Dm, Ns = w.shape
    dims = (((1,), (1,)), ((), ()))

    def body(dy_ref, w_ref, x_ref, g_ref, r_ref, dx_ref, dxb_ref, dg_ref):
        dxn = None
        for j in range(J):
            p = lax.dot_general(dy_ref[:, j * Ns:(j + 1) * Ns], w_ref[j], dims, preferred_element_type=F32)
            dxn = p if dxn is None else dxn + p
        xv = x_ref[...]
        r = lax.rsqrt(jnp.mean(xv * xv, axis=-1, keepdims=True) + EPS)
        gx = dxn * g_ref[...]
        dot = jnp.sum(gx * xv, axis=-1, keepdims=True)
        dx = r_ref[...] + r * gx - xv * ((r * r * r) * (dot * (1.0 / Dm)))
        dx_ref[...] = dx
        dxb_ref[...] = dx.astype(BF16)
        part = jnp.sum(dxn * (xv * r), axis=0, keepdims=True)

        @pl.when(pl.program_id(0) == 0)
        def _():
            dg_ref[...] = part

        @pl.when(pl.program_id(0) > 0)
        def _():
            dg_ref[...] += part

    row = pl.BlockSpec((tm, Dm), lambda m: (m, 0))
    vec = pl.BlockSpec((1, Dm), lambda m: (0, 0))
    return pl.pallas_call(
        body, name=name, grid=(M // tm,),
        in_specs=[pl.BlockSpec((tm, J * Ns), lambda m: (m, 0)),
                  pl.BlockSpec((J, None, Dm, Ns), lambda m: (0, layer, 0, 0), pipeline_mode=pl.Buffered(1)), row, vec, row],
        out_specs=[row, row, vec],
        out_shape=[_sds((M, Dm), F32), _sds((M, Dm), BF16), _sds((1, Dm), F32)],
        compiler_params=_params("arbitrary"),
    )(dy, w, x, g, dres)


def mm_tn(a, dy, name, J, tk, tm=512, jb=None):
    M, K = a.shape
    jb = jb or J
    Ns = dy.shape[1] // J
    N = jb * Ns
    n_m = M // tm
    dims = (((0,), (0,)), ((), ()))

    def body(a_ref, dy_ref, o_ref, acc_ref):
        p = lax.dot_general(a_ref[...], dy_ref[...], dims, preferred_element_type=F32)
        m = pl.program_id(2)

        @pl.when(m == 0)
        def _():
            acc_ref[...] = p

        @pl.when(m > 0)
        def _():
            acc_ref[...] += p

        @pl.when(m == n_m - 1)
        def _():
            for j in range(jb):
                o_ref[j] = acc_ref[:, j * Ns:(j + 1) * Ns].astype(BF16)

    return pl.pallas_call(
        body, name=name, grid=(J // jb, K // tk, n_m),
        in_specs=[pl.BlockSpec((tm, tk), lambda g, k, m: (m, k)), pl.BlockSpec((tm, N), lambda g, k, m: (m, g))],
        out_specs=pl.BlockSpec((jb, tk, Ns), lambda g, k, m: (g, k, 0)),
        out_shape=_sds((J, K, Ns), BF16), scratch_shapes=[pltpu.VMEM((tk, N), F32)],
        compiler_params=_params("parallel", "parallel", "arbitrary"),
    )(a, dy)


HALO = 16


def _shift_down(x, s):
    return pltpu.roll(x, s, 0)


def _shift_up(x, s):
    return pltpu.roll(x, x.shape[0] - s, 0)


def _conv3(z, cw):
    return (_shift_down(z, 2) * cw[0:1] + _shift_down(z, 1) * cw[1:2]) + z * cw[2:3]


def _window_count(first_row, n, k):
    t = first_row + lax.broadcasted_iota(jnp.int32, (n, 1), 0)
    return jnp.clip(t + 1, 1, k).astype(F32)


def mixer_fwd(proj, conv_w, pool_w, pool_scale, name, ts=256):
    S = proj.shape[0]
    n = ts + HALO

    def body(pm_ref, pb_ref, cw_ref, pw_ref, ps_ref, o_ref):
        i = pl.program_id(0)
        before = jnp.where(i > 0, pb_ref[...], 0.0)
        ext = jnp.concatenate([before, pm_ref[...]], axis=0)
        cw = cw_ref[...]
        z = ext[:, 2 * A_WIDTH:3 * A_WIDTH] * ext[:, 0:A_WIDTH]
        cz = _conv3(z, cw)
        ya = pm_ref[:, A_WIDTH:2 * A_WIDTH] * cz[HALO:]
        o_ref[:, 0:A_WIDTH] = ya.astype(BF16)
        for g, k in enumerate(POOL_WINDOWS):
            lo = 3 * A_WIDTH + g * POOL_GROUP
            p = ext[:, lo:lo + POOL_GROUP]
            w = p
            s = 1
            while s < k:
                w = w + _shift_down(w, s)
                s *= 2
            pooled = w / _window_count(i * ts - HALO, n, k) - p
            yb = jnp.dot(pooled[HALO:].astype(BF16), pw_ref[g], preferred_element_type=F32)
            yb = yb * ps_ref[:, g * POOL_GROUP:(g + 1) * POOL_GROUP]
            o_ref[:, A_WIDTH + g * POOL_GROUP:A_WIDTH + (g + 1) * POOL_GROUP] = yb.astype(BF16)

    hb = ts // HALO
    return pl.pallas_call(
        body, name=name, grid=(S // ts,),
        in_specs=[
            pl.BlockSpec((ts, EVEN_IN), lambda i: (i, 0)),
            pl.BlockSpec((HALO, EVEN_IN), lambda i: (jnp.maximum(i * hb - 1, 0), 0)),
            pl.BlockSpec((3, A_WIDTH), lambda i: (0, 0)),
            pl.BlockSpec((4, POOL_GROUP, POOL_GROUP), lambda i: (0, 0, 0)),
            pl.BlockSpec((1, 4 * POOL_GROUP), lambda i: (0, 0)),
        ],
        out_specs=pl.BlockSpec((ts, D_MODEL), lambda i: (i, 0)),
        out_shape=_sds((S, D_MODEL), BF16), compiler_params=_params("parallel"),
    )(proj, proj, conv_w, pool_w, pool_scale)


def mixer_bwd(proj, dmix, conv_w, pool_w, pool_scale, name, ts=256):
    S = proj.shape[0]
    n = ts + 2 * HALO
    nt = S // ts
    tn_dims = (((0,), (0,)), ((), ()))
    nt_dims = (((1,), (1,)), ((), ()))

    def body(pm_ref, pb_ref, pa_ref, dm_ref, da_ref, cw_ref, pw_ref, ps_ref, o_ref, dcw_ref, dpw_ref, dps_ref):
        i = pl.program_id(0)
        last = i == nt - 1
        before = jnp.where(i > 0, pb_ref[...], 0.0)
        after = jnp.where(last, 0.0, pa_ref[...])
        ext = jnp.concatenate([before, pm_ref[...], after], axis=0)
        dafter = jnp.where(last, 0.0, da_ref[...])
        dext = jnp.concatenate([jnp.zeros((HALO, D_MODEL), F32), dm_ref[...], dafter], axis=0)
        cw = cw_ref[...]
        main = slice(HALO, HALO + ts)

        @pl.when(i == 0)
        def _():
            dcw_ref[...] = jnp.zeros_like(dcw_ref)
            dpw_ref[...] = jnp.zeros_like(dpw_ref)
            dps_ref[...] = jnp.zeros_like(dps_ref)

        h, gb, gc = ext[:, 0:A_WIDTH], ext[:, A_WIDTH:2 * A_WIDTH], ext[:, 2 * A_WIDTH:3 * A_WIDTH]
        z = gc * h
        z1, z2 = _shift_down(z, 1), _shift_down(z, 2)
        cz = (z2 * cw[0:1] + z1 * cw[1:2]) + z * cw[2:3]
        dya = dext[:, 0:A_WIDTH]
        dcz = dya * gb
        dz = dcz * cw[2:3] + _shift_up(dcz, 1) * cw[1:2] + _shift_up(dcz, 2) * cw[0:1]
        o_ref[:, 0:A_WIDTH] = (dz * gc)[main].astype(BF16)
        o_ref[:, A_WIDTH:2 * A_WIDTH] = (dya * cz)[main].astype(BF16)
        o_ref[:, 2 * A_WIDTH:3 * A_WIDTH] = (dz * h)[main].astype(BF16)
        dczm = dcz[main]
        dcw_ref[0:1, :] += jnp.sum(dczm * z2[main], axis=0, keepdims=True)
        dcw_ref[1:2, :] += jnp.sum(dczm * z1[main], axis=0, keepdims=True)
        dcw_ref[2:3, :] += jnp.sum(dczm * z[main], axis=0, keepdims=True)

        for g, k in enumerate(POOL_WINDOWS):
            lo = 3 * A_WIDTH + g * POOL_GROUP
            cols = slice(g * POOL_GROUP, (g + 1) * POOL_GROUP)
            p = ext[:, lo:lo + POOL_GROUP]
            w = p
            s = 1
            while s < k:
                w = w + _shift_down(w, s)
                s *= 2
            cnt = _window_count(i * ts - HALO, n, k)
            pooled = (w / cnt - p)[main].astype(BF16)
            dyb = dext[:, A_WIDTH + g * POOL_GROUP:A_WIDTH + (g + 1) * POOL_GROUP]
            e = dyb * ps_ref[:, cols]
            pre = jnp.dot(pooled, pw_ref[g], preferred_element_type=F32)
            dps_ref[:, cols] += jnp.sum(dyb[main] * pre, axis=0, keepdims=True)
            dpw_ref[g] += lax.dot_general(pooled, e[main].astype(BF16), tn_dims, preferred_element_type=F32)
            dpooled = lax.dot_general(e.astype(BF16), pw_ref[g], nt_dims, preferred_element_type=F32)
            q = dpooled / cnt
            a = q
            s = 1
            while s < k:
                a = a + _shift_up(a, s)
                s *= 2
            o_ref[:, lo:lo + POOL_GROUP] = (a - dpooled)[main].astype(BF16)

    hb = ts // HALO
    nh = S // HALO
    before_map = lambda i: (jnp.maximum(i * hb - 1, 0), 0)
    after_map = lambda i: (jnp.minimum((i + 1) * hb, nh - 1), 0)
    full = lambda *shape: pl.BlockSpec(shape, lambda i: (0,) * len(shape))
    return pl.pallas_call(
        body, name=name, grid=(nt,),
        in_specs=[
            pl.BlockSpec((ts, EVEN_IN), lambda i: (i, 0)),
            pl.BlockSpec((HALO, EVEN_IN), before_map),
            pl.BlockSpec((HALO, EVEN_IN), after_map),
            pl.BlockSpec((ts, D_MODEL), lambda i: (i, 0)),
            pl.BlockSpec((HALO, D_MODEL), after_map),
            full(3, A_WIDTH), full(4, POOL_GROUP, POOL_GROUP), full(1, 4 * POOL_GROUP),
        ],
        out_specs=[pl.BlockSpec((ts, EVEN_IN), lambda i: (i, 0)), full(3, A_WIDTH), full(4, POOL_GROUP, POOL_GROUP),
                   full(1, 4 * POOL_GROUP)],
        out_shape=[_sds((S, EVEN_IN), BF16), _sds((3, A_WIDTH), F32), _sds((4, POOL_GROUP, POOL_GROUP), F32),
                   _sds((1, 4 * POOL_GROUP), F32)],
        compiler_params=_params("arbitrary"),
    )(proj, proj, proj, dmix, dmix, conv_w, pool_w, pool_scale)


FFN_HALO = 16
FFN_TC = 1408


def glu_fwd(up, conv_w, conv_b, name, ts=256):
    S = up.shape[0]
    nc = D_FF // FFN_TC

    def body(gm_ref, gb_ref, um_ref, ub_ref, cwg_ref, cwu_ref, cbg_ref, cbu_ref, ug_ref, uu_ref, o_ref):
        i = pl.program_id(0)

        def conv(m_ref, b_ref, cw_ref, cb_ref):
            before = jnp.where(i > 0, b_ref[...].astype(F32), 0.0)
            ext = jnp.concatenate([before, m_ref[...].astype(F32)], axis=0)
            return _conv3(ext, cw_ref[...])[FFN_HALO:] + cb_ref[...]

        gate = conv(gm_ref, gb_ref, cwg_ref, cbg_ref)
        upv = conv(um_ref, ub_ref, cwu_ref, cbu_ref)
        ug_ref[...] = gate.astype(BF16)
        uu_ref[...] = upv.astype(BF16)
        o_ref[...] = ((gate * (1.0 / (1.0 + jnp.exp(-gate)))) * upv).astype(BF16)

    hb = ts // FFN_HALO
    main = lambda off: pl.BlockSpec((ts, FFN_TC), lambda i, c: (i, c + off))
    halo = lambda off: pl.BlockSpec((FFN_HALO, FFN_TC), lambda i, c: (jnp.maximum(i * hb - 1, 0), c + off))
    cw = lambda off: pl.BlockSpec((3, FFN_TC), lambda i, c: (0, c + off))
    cb = lambda off: pl.BlockSpec((1, FFN_TC), lambda i, c: (0, c + off))
    ug, uu, act = pl.pallas_call(
        body, name=name, grid=(S // ts, nc),
        in_specs=[main(0), halo(0), main(nc), halo(nc), cw(0), cw(nc), cb(0), cb(nc)],
        out_specs=[pl.BlockSpec((ts, FFN_TC), lambda i, c: (i, c))] * 3,
        out_shape=[_sds((S, D_FF), BF16)] * 3, compiler_params=_params("parallel", "parallel"),
    )(up, up, up, up, conv_w, conv_w, conv_b, conv_b)
    return (ug, uu), act


def glu_bwd(up, u, da, conv_w, name, ts=256):
    S = up.shape[0]
    nc = D_FF // FFN_TC
    nt = S // ts
    W = 2 * D_FF

    def body(x_ref, gm_ref, ga_ref, um_ref, ua_ref, dm_ref, da_ref, cw_ref, dx_ref, dcw_ref, dcb_ref):
        i = pl.program_id(0)
        last = i == nt - 1

        @pl.when(i == 0)
        def _():
            dcw_ref[...] = jnp.zeros_like(dcw_ref)
            dcb_ref[...] = jnp.zeros_like(dcb_ref)

        def rows(m_ref, a_ref, cols):
            return jnp.concatenate([m_ref[:, cols], a_ref[:, cols]], axis=0).astype(F32)

        def back(d, cols):
            cw = cw_ref[:, cols]
            d1, d2 = _shift_up(d, 1), _shift_up(d, 2)
            dx_ref[:, cols] = ((d * cw[2:3] + d1 * cw[1:2]) + d2 * cw[0:1])[:ts].astype(BF16)
            x = x_ref[:, cols].astype(F32)
            dcb_ref[:, cols] += jnp.sum(d[:ts], axis=0, keepdims=True)
            dcw_ref[0:1, cols] += jnp.sum(d2[:ts] * x, axis=0, keepdims=True)
            dcw_ref[1:2, cols] += jnp.sum(d1[:ts] * x, axis=0, keepdims=True)
            dcw_ref[2:3, cols] += jnp.sum(d[:ts] * x, axis=0, keepdims=True)

        for c in range(nc):
            cols = slice(c * FFN_TC, (c + 1) * FFN_TC)
            ug, uu = rows(gm_ref, ga_ref, cols), rows(um_ref, ua_ref, cols)
            dae = rows(dm_ref, da_ref, cols)
            dae = jnp.where(last & (lax.broadcasted_iota(jnp.int32, dae.shape, 0) >= ts), 0.0, dae)
            sg = 1.0 / (1.0 + jnp.exp(-ug))
            duu = dae * (ug * sg)
            dug = (dae * uu) * (sg * (1.0 + ug * (1.0 - sg)))
            back(dug, cols)
            back(duu, slice(D_FF + c * FFN_TC, D_FF + (c + 1) * FFN_TC))

    hb = ts // FFN_HALO
    nh = S // FFN_HALO
    after_map = lambda i: (jnp.minimum((i + 1) * hb, nh - 1), 0)
    main = pl.BlockSpec((ts, D_FF), lambda i: (i, 0))
    after = pl.BlockSpec((FFN_HALO, D_FF), after_map)
    return pl.pallas_call(
        body, name=name, grid=(nt,),
        in_specs=[pl.BlockSpec((ts, W), lambda i: (i, 0)), main, after, main, after, main, after,
                  pl.BlockSpec((3, W), lambda i: (0, 0))],
        out_specs=[pl.BlockSpec((ts, W), lambda i: (i, 0)), pl.BlockSpec((3, W), lambda i: (0, 0)),
                   pl.BlockSpec((1, W), lambda i: (0, 0))],
        out_shape=[_sds((S, W), BF16), _sds((3, W), F32), _sds((1, W), F32)],
        compiler_params=_params("arbitrary"),
    )(up, u[0], u[0], u[1], u[1], da, da, conv_w)


def _head_mean_matrix():
    h = np.arange(D_MODEL) // HEAD_DIM
    return jnp.asarray((h[:, None] == h[None, :]).astype(np.float32) / HEAD_DIM, dtype=BF16)


def _head_mean(v, gm):
    return jnp.dot(v.astype(BF16), gm, preferred_element_type=F32)


def qknorm_fwd(qkv, gqk, name, ts=512):
    S = qkv.shape[0]

    def body(x_ref, g_ref, gm_ref, o_ref):
        part = pl.program_id(0)
        x = x_ref[...]

        @pl.when(part < 2)
        def _():
            r = lax.rsqrt(_head_mean(x * x, gm_ref[...]) + EPS)
            o_ref[...] = ((x * r) * g_ref[...]).astype(BF16)

        @pl.when(part == 2)
        def _():
            o_ref[...] = x.astype(BF16)

    return pl.pallas_call(
        body, name=name, grid=(3, S // ts),
        in_specs=[pl.BlockSpec((ts, D_MODEL), lambda p, i: (i, p)), pl.BlockSpec((None, 1, D_MODEL), lambda p, i: (p, 0, 0)),
                  pl.BlockSpec((D_MODEL, D_MODEL), lambda p, i: (0, 0))],
        out_specs=pl.BlockSpec((ts, D_MODEL), lambda p, i: (i, p)),
        out_shape=_sds((S, 3 * D_MODEL), BF16), compiler_params=_params("parallel", "parallel"),
    )(qkv, gqk, _head_mean_matrix())


def qknorm_bwd(qkv, dq, dk, dv, gqk, name, ts=256):
    S = qkv.shape[0]

    def body(x_ref, dq_ref, dk_ref, dv_ref, g_ref, gm_ref, o_ref, dg_ref):
        @pl.when(pl.program_id(0) == 0)
        def _():
            dg_ref[...] = jnp.zeros_like(dg_ref)

        gm = gm_ref[...]
        for part, d_ref in enumerate((dq_ref, dk_ref)):
            cols = slice(part * D_MODEL, (part + 1) * D_MODEL)
            x = x_ref[:, cols]
            d = d_ref[...]
            r = lax.rsqrt(_head_mean(x * x, gm) + EPS)
            gx = d * g_ref[part]
            o_ref[:, cols] = (r * gx - x * ((r * r * r) * _head_mean(gx * x, gm))).astype(BF16)
            dg_ref[part] += jnp.sum(d * (x * r), axis=0, keepdims=True)
        o_ref[:, 2 * D_MODEL:] = dv_ref[...].astype(BF16)

    row = pl.BlockSpec((ts, D_MODEL), lambda i: (i, 0))
    wide = pl.BlockSpec((ts, 3 * D_MODEL), lambda i: (i, 0))
    gains = pl.BlockSpec((3, 1, D_MODEL), lambda i: (0, 0, 0))
    return pl.pallas_call(
        body, name=name, grid=(S // ts,),
        in_specs=[wide, row, row, row, gains, pl.BlockSpec((D_MODEL, D_MODEL), lambda i: (0, 0))],
        out_specs=[wide, gains],
        out_shape=[_sds((S, 3 * D_MODEL), BF16), _sds((3, 1, D_MODEL), F32)],
        compiler_params=_params("arbitrary"),
    )(qkv, dq, dk, dv, gqk, _head_mean_matrix())


RESIDUES = 16


def _block_order(dil):
    runs = RESIDUES // dil
    slot = np.arange(ATT_BLOCK)
    return (slot % (ATT_BLOCK // runs)) * runs + slot // (ATT_BLOCK // runs)


def _bucket_tables():
    n = ATT_BLOCK
    max_exact = N_REL_BUCKETS // 2
    buckets, valids = [], []
    for _, dil in DILATED_PAIRS:
        order = _block_order(dil)
        a = order[:, None]
        c = np.concatenate([order, n + order])[None, :]
        first_half = (np.arange(2 * n) < n)[None, :]
        rel = a + n - c
        band = (rel >= 0) & (rel <= n)
        dist = np.clip(rel, 0, n) * dil
        dd = np.maximum(dist, 1).astype(np.float32)
        large = max_exact + (np.log(dd / np.float32(max_exact)) / np.float32(math.log(REL_MAX_DISTANCE / max_exact))
                             * np.float32(N_REL_BUCKETS - max_exact)).astype(np.int32)
        large = np.minimum(large, N_REL_BUCKETS - 1)
        buckets.append(np.where(dist < max_exact, dist, large).reshape(1, -1))
        valids.append(np.stack([(band & ~first_half).reshape(1, -1), band.reshape(1, -1)]))
    return np.stack(buckets).astype(np.int32), np.stack(valids).astype(np.int32)


BIAS_CHUNK = 8192


def _split3(x):
    a = x.astype(BF16)
    r = x - a.astype(F32)
    b = r.astype(BF16)
    c = (r - b.astype(F32)).astype(BF16)
    return a, b, c


def bias_expand(rel_bias_t, name):
    bucket, valid = _bucket_tables()
    nq = bucket.shape[-1]

    def body(t_ref, b_ref, v_ref, o_ref):
        onehot = (lax.broadcasted_iota(jnp.int32, (N_REL_BUCKETS, BIAS_CHUNK), 0) == b_ref[...]).astype(BF16)
        acc = None
        for term in _split3(t_ref[...]):
            p = jnp.dot(term, onehot, preferred_element_type=F32)
            acc = p if acc is None else acc + p
        o_ref[...] = jnp.where(v_ref[...] > 0, acc, MASK_VALUE)

    return pl.pallas_call(
        body, name=name, grid=(3, 2, nq // BIAS_CHUNK),
        in_specs=[pl.BlockSpec((N_HEADS, N_REL_BUCKETS), lambda b, v, c: (0, 0)),
                  pl.BlockSpec((None, 1, BIAS_CHUNK), lambda b, v, c: (b, 0, c)),
                  pl.BlockSpec((None, None, 1, BIAS_CHUNK), lambda b, v, c: (b, v, 0, c))],
        out_specs=pl.BlockSpec((None, None, N_HEADS, BIAS_CHUNK), lambda b, v, c: (b, v, 0, c)),
        out_shape=_sds((3, 2, N_HEADS, nq), F32), compiler_params=_params("parallel", "parallel", "parallel"),
    )(rel_bias_t, jnp.asarray(bucket), jnp.asarray(valid))


def bias_reduce(dbias, name):
    bucket, _ = _bucket_tables()
    nq = bucket.shape[-1]
    dims = (((1,), (1,)), ((), ()))

    def body(d_ref, b_ref, o_ref):
        onehot = (lax.broadcasted_iota(jnp.int32, (N_REL_BUCKETS, BIAS_CHUNK), 0) == b_ref[...]).astype(BF16)
        acc = None
        for term in _split3(d_ref[...]):
            p = lax.dot_general(term, onehot, dims, preferred_element_type=F32)
            acc = p if acc is None else acc + p

        @pl.when(pl.program_id(1) == 0)
        def _():
            o_ref[...] = acc

        @pl.when(pl.program_id(1) > 0)
        def _():
            o_ref[...] += acc

    return pl.pallas_call(
        body, name=name, grid=(3, nq // BIAS_CHUNK),
        in_specs=[pl.BlockSpec((None, N_HEADS, BIAS_CHUNK), lambda b, c: (b, 0, c)),
                  pl.BlockSpec((None, 1, BIAS_CHUNK), lambda b, c: (b, 0, c))],
        out_specs=pl.BlockSpec((None, N_HEADS, N_REL_BUCKETS), lambda b, c: (b, 0, 0)),
        out_shape=_sds((3, N_HEADS, N_REL_BUCKETS), F32), compiler_params=_params("parallel", "arbitrary"),
    )(dbias, jnp.asarray(bucket))


PAIR = 2 * HEAD_DIM
N_PAIRS = N_HEADS // 2
_NT = (((1,), (1,)), ((), ()))
_TN = (((0,), (0,)), ((), ()))


def _low_lanes(shape):
    return lax.broadcasted_iota(jnp.int32, shape, 1) < HEAD_DIM


ATTN_VMEM_LIMIT_BYTES = 56 * 1024 * 1024
BRANCH_ORDER = (2, 1, 0)


def _regroup(dst, src, L16):
    for r in range(RESIDUES):
        dst[pl.ds(r * L16, L16), :] = src[pl.ds(r, L16, stride=RESIDUES), :]


def _ungroup(dst, src, L16):
    for r in range(RESIDUES):
        dst[pl.ds(r, L16, stride=RESIDUES), :] = src[pl.ds(r * L16, L16), :]


def _branch_geometry(branch, S):
    dil = DILATED_PAIRS[branch][1]
    runs = RESIDUES // dil
    return dil, runs, ATT_BLOCK // runs, S // dil // ATT_BLOCK


def _block_rows(it, branch, S):
    dil, runs, run_len, n_blocks = _branch_geometry(branch, S)
    L16 = S // RESIDUES
    r, b = it // n_blocks, it % n_blocks
    prev = jnp.maximum(b - 1, 0)
    cur_rows = [pl.multiple_of((j * dil + r) * L16 + run_len * b, 8) for j in range(runs)]
    prev_rows = [pl.multiple_of((j * dil + r) * L16 + run_len * prev, 8) for j in range(runs)]
    return cur_rows, prev_rows, jnp.minimum(b, 1)


def _load_block(ref, rows, run_len):
    parts = [ref[pl.ds(o, run_len), :] for o in rows]
    return parts[0] if len(parts) == 1 else jnp.concatenate(parts, axis=0)


def _store_block(ref, rows, run_len, value, add=False):
    for j, o in enumerate(rows):
        part = value[j * run_len:(j + 1) * run_len]
        if add:
            ref[pl.ds(o, run_len), :] += part
        else:
            ref[pl.ds(o, run_len), :] = part


ATTN_FWD_UNROLL = 8
ATTN_BWD_UNROLL = 4


def _stack_heads(x, low):
    zero = jnp.zeros_like(x)
    return jnp.concatenate([jnp.where(low, x, zero), jnp.where(low, zero, x)], axis=0)


def _unstack_heads(y, low):
    return jnp.where(low, y[:ATT_BLOCK], y[ATT_BLOCK:])


def attn_fwd(qkvn, bias, name):
    S = qkvn.shape[0]
    L16 = S // RESIDUES
    n_iter = S // ATT_BLOCK

    def body(q_ref, k_ref, v_ref, b_ref, o_ref, lse_ref, stage, qp, kp, vp, acc_s, m_s, l_s):
        for src, dst in ((q_ref, qp), (k_ref, kp), (v_ref, vp)):
            stage[...] = src[...].astype(F32)
            _regroup(dst, stage, L16)
        low = _low_lanes((ATT_BLOCK, PAIR))

        for branch in BRANCH_ORDER:
            _, _, run_len, _ = _branch_geometry(branch, S)
            first = branch == BRANCH_ORDER[0]

            def step(it, carry, branch=branch, run_len=run_len, first=first):
                cur, prev, variant = _block_rows(it, branch, S)
                q = _load_block(qp, cur, run_len).astype(BF16)
                k = jnp.concatenate([_load_block(kp, prev, run_len), _load_block(kp, cur, run_len)], axis=0).astype(BF16)
                v = jnp.concatenate([_load_block(vp, prev, run_len), _load_block(vp, cur, run_len)], axis=0).astype(BF16)
                s = lax.dot_general(_stack_heads(q, low), k, _NT, preferred_element_type=F32) * (HEAD_DIM ** -0.5)
                s = s + b_ref[2 * branch + variant].reshape(2 * ATT_BLOCK, 2 * ATT_BLOCK)
                mx = jnp.max(s, axis=-1, keepdims=True)
                p = jnp.exp(s - mx)
                den = jnp.sum(p, axis=-1, keepdims=True)
                pv = jnp.dot(p.astype(BF16), v, preferred_element_type=F32)
                acc = _unstack_heads(pv, low)
                m = _unstack_heads(mx, low)
                l = _unstack_heads(den, low)
                if not first:
                    m_old = _load_block(m_s, cur, run_len)
                    m_new = jnp.maximum(m_old, m)
                    a_old, a_new = jnp.exp(m_old - m_new), jnp.exp(m - m_new)
                    acc = _load_block(acc_s, cur, run_len) * a_old + acc * a_new
                    l = _load_block(l_s, cur, run_len) * a_old + l * a_new
                    m = m_new
                _store_block(acc_s, cur, run_len, acc)
                _store_block(m_s, cur, run_len, m)
                _store_block(l_s, cur, run_len, l)
                return carry

            lax.fori_loop(0, n_iter, step, 0, unroll=ATTN_FWD_UNROLL)

        acc_s[...] = acc_s[...] / l_s[...]
        _ungroup(stage, acc_s, L16)
        o_ref[...] = stage[...].astype(BF16)
        m_s[...] = m_s[...] + jnp.log(l_s[...])
        _ungroup(lse_ref, m_s, L16)

    col = lambda part: pl.BlockSpec((S, PAIR), lambda hp: (0, part * N_PAIRS + hp))
    out = pl.BlockSpec((S, PAIR), lambda hp: (0, hp))
    return pl.pallas_call(
        body, name=name, grid=(N_PAIRS,),
        in_specs=[col(0), col(1), col(2), pl.BlockSpec((6, 2, ATT_BLOCK, 2 * ATT_BLOCK), lambda hp: (0, hp, 0, 0))],
        out_specs=[out, out], out_shape=[_sds((S, D_MODEL), BF16), _sds((S, D_MODEL), F32)],
        scratch_shapes=[pltpu.VMEM((S, PAIR), F32)] * 7,
        compiler_params=pltpu.CompilerParams(dimension_semantics=("parallel",), vmem_limit_bytes=ATTN_VMEM_LIMIT_BYTES),
    )(qkvn, qkvn, qkvn, bias)


def attn_bwd(qkvn, att, datt, lse, bias, name):
    S = qkvn.shape[0]
    L16 = S // RESIDUES
    n_iter = S // ATT_BLOCK
    TILE = 512

    def body(q_ref, k_ref, v_ref, o_ref, do_ref, lse_ref, b_ref, dq_ref, dk_ref, dv_ref, db_ref,
             qp, kp, vp, dop, ldp, dqp, dkp, dvp):
        stage = dqp
        for src, dst in ((q_ref, qp), (k_ref, kp), (v_ref, vp), (do_ref, dop)):
            stage[...] = src[...].astype(F32)
            _regroup(dst, stage, L16)

        def pack(i, carry):
            rows = pl.ds(pl.multiple_of(i * TILE, TILE), TILE)
            low = _low_lanes((TILE, PAIR))
            lane = lax.broadcasted_iota(jnp.int32, (TILE, PAIR), 1)
            prod = do_ref[rows, :].astype(F32) * o_ref[rows, :].astype(F32)
            d0 = jnp.sum(jnp.where(low, prod, 0.0), axis=-1, keepdims=True)
            d1 = jnp.sum(jnp.where(low, 0.0, prod), axis=-1, keepdims=True)
            stage[rows, :] = jnp.where((lane & (HEAD_DIM // 2)) == 0, lse_ref[rows, :], jnp.where(low, d0, d1))
            return carry

        lax.fori_loop(0, S // TILE, pack, 0)
        _regroup(ldp, stage, L16)
        dqp[...] = jnp.zeros_like(dqp)
        dkp[...] = jnp.zeros_like(dkp)
        dvp[...] = jnp.zeros_like(dvp)
        db_ref[...] = jnp.zeros_like(db_ref)
        low = _low_lanes((ATT_BLOCK, PAIR))

        for branch in BRANCH_ORDER:
            _, _, run_len, _ = _branch_geometry(branch, S)

            def step(it, carry, branch=branch, run_len=run_len):
                cur, prev, variant = _block_rows(it, branch, S)
                q = _load_block(qp, cur, run_len).astype(BF16)
                dout = _load_block(dop, cur, run_len).astype(BF16)
                ld = _load_block(ldp, cur, run_len)
                k = jnp.concatenate([_load_block(kp, prev, run_len), _load_block(kp, cur, run_len)], axis=0).astype(BF16)
                v = jnp.concatenate([_load_block(vp, prev, run_len), _load_block(vp, cur, run_len)], axis=0).astype(BF16)
                half = HEAD_DIM // 2
                lse2 = jnp.concatenate([ld[:, 0:1], ld[:, HEAD_DIM:HEAD_DIM + 1]], axis=0)
                delta2 = jnp.concatenate([ld[:, half:half + 1], ld[:, HEAD_DIM + half:HEAD_DIM + half + 1]], axis=0)
                q2, do2 = _stack_heads(q, low), _stack_heads(dout, low)
                s = lax.dot_general(q2, k, _NT, preferred_element_type=F32) * (HEAD_DIM ** -0.5)
                p = jnp.exp(s + b_ref[2 * branch + variant].reshape(2 * ATT_BLOCK, 2 * ATT_BLOCK) - lse2)
                dp = lax.dot_general(do2, v, _NT, preferred_element_type=F32)
                ds = p * (dp - delta2)
                db_ref[branch] += ds.reshape(2, ATT_BLOCK, 2 * ATT_BLOCK)
                dsb = (ds * (HEAD_DIM ** -0.5)).astype(BF16)
                dq = _unstack_heads(jnp.dot(dsb, k, preferred_element_type=F32), low)
                dk = lax.dot_general(dsb, q2, _TN, preferred_element_type=F32)
                dv = lax.dot_general(p.astype(BF16), do2, _TN, preferred_element_type=F32)
                _store_block(dqp, cur, run_len, dq, add=True)
                _store_block(dkp, prev, run_len, dk[:ATT_BLOCK], add=True)
                _store_block(dvp, prev, run_len, dv[:ATT_BLOCK], add=True)
                _store_block(dkp, cur, run_len, dk[ATT_BLOCK:], add=True)
                _store_block(dvp, cur, run_len, dv[ATT_BLOCK:], add=True)
                return carry

            lax.fori_loop(0, n_iter, step, 0, unroll=ATTN_BWD_UNROLL)

        _ungroup(dq_ref, dqp, L16)
        _ungroup(dk_ref, dkp, L16)
        _ungroup(dv_ref, dvp, L16)

    col = lambda part: pl.BlockSpec((S, PAIR), lambda hp: (0, part * N_PAIRS + hp))
    one = pl.BlockSpec((S, PAIR), lambda hp: (0, hp))
    return pl.pallas_call(
        body, name=name, grid=(N_PAIRS,),
        in_specs=[col(0), col(1), col(2), one, one, one,
                  pl.BlockSpec((6, 2, ATT_BLOCK, 2 * ATT_BLOCK), lambda hp: (0, hp, 0, 0))],
        out_specs=[one, one, one, pl.BlockSpec((3, 2, ATT_BLOCK, 2 * ATT_BLOCK), lambda hp: (0, hp, 0, 0))],
        out_shape=[_sds((S, D_MODEL), F32)] * 3 + [_sds((3, N_HEADS, ATT_BLOCK, 2 * ATT_BLOCK), F32)],
        scratch_shapes=[pltpu.VMEM((S, PAIR), F32)] * 8,
        compiler_params=pltpu.CompilerParams(dimension_semantics=("parallel",), vmem_limit_bytes=ATTN_VMEM_LIMIT_BYTES),
    )(qkvn, qkvn, qkvn, att, datt, lse, bias)


def adamw(w, g, m, v, name):
    n, R, C = w.shape

    def body(w_ref, g_ref, m_ref, v_ref, d_ref, nm_ref, nv_ref):
        gv = g_ref[...]
        m2 = ADAM_B1 * m_ref[...] + (1.0 - ADAM_B1) * gv
        v2 = ADAM_B2 * v_ref[...] + (1.0 - ADAM_B2) * (gv * gv)
        m_hat = m2 / (1.0 - ADAM_B1 ** ADAM_STEP)
        v_hat = v2 / (1.0 - ADAM_B2 ** ADAM_STEP)
        d_ref[...] = -ADAM_LR * (m_hat / (jnp.sqrt(v_hat) + ADAM_EPS) + ADAM_WD * w_ref[...])
        nm_ref[...] = m2
        nv_ref[...] = v2

    tr = R
    while tr * C * 4 > (1 << 21) and tr % 16 == 0:
        tr //= 2
    spec = pl.BlockSpec((None, tr, C), lambda i, r: (i, r, 0))
    return pl.pallas_call(
        body, name=name, grid=(n, R // tr), in_specs=[spec] * 4, out_specs=[spec] * 3,
        out_shape=[_sds((n, R, C), F32)] * 3, compiler_params=_params("parallel", "parallel"),
    )(w, g, m, v)


ANY = pl.BlockSpec(memory_space=pl.ANY)


def _coords():
    return lax.axis_index("x"), lax.axis_index("y"), lax.axis_index("c")


def _other_chips(mx, my):
    return [(1 - mx, my), (mx, 1 - my), (1 - mx, 1 - my)]


def _remote(src, dst, send, recv, dev):
    return pltpu.make_async_remote_copy(src_ref=src, dst_ref=dst, send_sem=send, recv_sem=recv, device_id=dev,
                                        device_id_type=MESH)


def allgather_devices(x, name):
    R, C = x.shape

    def body(x_ref, o_ref, send, recv, local_sem):
        mx, my, mc = _coords()
        me = 4 * mx + 2 * my + mc
        local = pltpu.make_async_copy(x_ref, o_ref.at[me], local_sem)
        local.start()
        peers = []
        for k in range(1, N_DEV):
            px = 1 - mx if k & 4 else mx
            py = 1 - my if k & 2 else my
            pc = 1 - mc if k & 1 else mc
            peers.append((px, py, pc))
        sends = [_remote(x_ref, o_ref.at[me], send.at[k], recv.at[k], p) for k, p in enumerate(peers)]
        for cp in sends:
            cp.start()
        for k, (px, py, pc) in enumerate(peers):
            _remote(x_ref, o_ref.at[4 * px + 2 * py + pc], send.at[k], recv.at[k], (px, py, pc)).wait_recv()
        for cp in sends:
            cp.wait_send()
        local.wait()

    return pl.pallas_call(
        body, name=name, in_specs=[ANY], out_specs=ANY, out_shape=_sds((N_DEV, R, C), x.dtype),
        scratch_shapes=[pltpu.SemaphoreType.DMA((N_DEV - 1,)), pltpu.SemaphoreType.DMA((N_DEV - 1,)),
                        pltpu.SemaphoreType.DMA],
    )(x)


HBM = pl.BlockSpec(memory_space=pltpu.HBM)
SEM = pl.BlockSpec(memory_space=pltpu.SEMAPHORE)
_SPLIT_COPY = pltpu.CompilerParams(has_side_effects=pltpu.SideEffectType.DATAFLOW_SIDE_EFFECTING)


def _in_hbm(a):
    return pltpu.with_memory_space_constraint(a, pltpu.HBM)


def cast_into_slot(w, layer, chip_core, name):
    _, _, hR, C = w.shape

    def body(s_ref, w_ref, o_ref):
        del s_ref
        o_ref[...] = w_ref[...].astype(BF16)

    grid_spec = pltpu.PrefetchScalarGridSpec(
        num_scalar_prefetch=1, grid=(2,),
        in_specs=[pl.BlockSpec((None, None, hR, C), lambda h, s: (layer, h, 0, 0))],
        out_specs=pl.BlockSpec((None, None, hR, C), lambda h, s: (s[0], h, 0, 0)))
    return pl.pallas_call(body, name=name, grid_spec=grid_spec, out_shape=_sds((N_CHIPS, 2, hR, C), BF16),
                          compiler_params=_params("parallel"))(chip_core, w)


def gather_start(lands, groups, name):
    n = len(lands)
    n_groups = len(groups)

    def body(*refs):
        ins = refs[:n]
        sems = refs[n:n + 2 * n_groups]
        token = refs[-1]
        mx, my, mc = _coords()
        chip = 2 * mx + my
        for g, members in enumerate(groups):
            send, recv = sems[2 * g], sems[2 * g + 1]
            for i, a in enumerate(members):
                mine = ins[a].at[chip, mc]
                for k, (px, py) in enumerate(_other_chips(mx, my)):
                    _remote(mine, mine, send.at[3 * i + k], recv.at[3 * i + k], (px, py, mc)).start()
        token[...] = jnp.zeros_like(token)

    sem_shapes = []
    for members in groups:
        sem_shapes += [pltpu.SemaphoreType.DMA((3 * len(members),))] * 2
    outs = pl.pallas_call(
        body, name=name, in_specs=[HBM] * n,
        out_specs=[SEM] * (2 * n_groups) + [HBM] * n + [pl.BlockSpec(memory_space=pltpu.VMEM)],
        out_shape=sem_shapes + [pltpu.HBM(a.shape, a.dtype) for a in lands] + [_sds((SUBLANES, LANES), F32)],
        input_output_aliases={a: 2 * n_groups + a for a in range(n)}, compiler_params=_SPLIT_COPY,
    )(*[_in_hbm(a) for a in lands])
    sems = [(outs[2 * g], outs[2 * g + 1]) for g in range(n_groups)]
    return sems, list(outs[2 * n_groups:2 * n_groups + n]), outs[-1]


def gather_forward(lands, sems, after, name):
    n = len(lands)

    def body(*refs):
        ins = refs[:n]
        send, recv = refs[n], refs[n + 1]
        fsend, frecv = refs[n + 3], refs[n + 4]
        mx, my, mc = _coords()
        for i in range(n):
            for k, (px, py) in enumerate(_other_chips(mx, my)):
                landed = ins[i].at[2 * px + py, mc]
                cp = _remote(landed, landed, send.at[3 * i + k], recv.at[3 * i + k], (px, py, mc))
                cp.wait_send()
                cp.wait_recv()
                _remote(landed, landed, fsend.at[3 * i + k], frecv.at[3 * i + k], (mx, my, 1 - mc)).start()

    outs = pl.pallas_call(
        body, name=name, in_specs=[HBM] * n + [SEM, SEM, ANY], out_specs=[SEM, SEM] + [HBM] * n,
        out_shape=[pltpu.SemaphoreType.DMA((3 * n,))] * 2 + [pltpu.HBM(a.shape, a.dtype) for a in lands],
        input_output_aliases={a: 2 + a for a in range(n)}, compiler_params=_SPLIT_COPY,
    )(*lands, sems[0], sems[1], after)
    return (outs[0], outs[1]), list(outs[2:])


def gather_wait(lands, sems, after, name):
    n = len(lands)

    def body(*refs):
        ins = refs[:n]
        fsend, frecv = refs[n], refs[n + 1]
        mx, my, mc = _coords()
        for i in range(n):
            for k, (px, py) in enumerate(_other_chips(mx, my)):
                theirs = ins[i].at[2 * px + py, 1 - mc]
                cp = _remote(theirs, theirs, fsend.at[3 * i + k], frecv.at[3 * i + k], (mx, my, 1 - mc))
                cp.wait_send()
                cp.wait_recv()

    outs = pl.pallas_call(
        body, name=name, in_specs=[HBM] * n + [SEM, SEM, ANY], out_specs=[HBM] * n,
        out_shape=[pltpu.HBM(a.shape, a.dtype) for a in lands],
        input_output_aliases={a: a for a in range(n)}, compiler_params=_SPLIT_COPY,
    )(*lands, sems[0], sems[1], after)
    return list(outs)


def _peers(mx, my, mc):
    return [(1 - mx if k & 4 else mx, 1 - my if k & 2 else my, 1 - mc if k & 1 else mc) for k in range(1, N_DEV)]


def devices_start(x, name):
    def body(x_ref, land_ref, send, recv, x_thru, land_thru):
        mx, my, mc = _coords()
        me = 4 * mx + 2 * my + mc
        for k, peer in enumerate(_peers(mx, my, mc)):
            _remote(x_ref, land_ref.at[me], send.at[k], recv.at[k], peer).start()

    land = lax.empty((N_DEV,) + x.shape, x.dtype)
    outs = pl.pallas_call(
        body, name=name, in_specs=[HBM, HBM], out_specs=[SEM, SEM, HBM, HBM],
        out_shape=[pltpu.SemaphoreType.DMA((N_DEV - 1,))] * 2 + [pltpu.HBM(x.shape, x.dtype), pltpu.HBM(land.shape, x.dtype)],
        input_output_aliases={0: 2, 1: 3}, compiler_params=_SPLIT_COPY,
    )(_in_hbm(x), _in_hbm(land))
    return (outs[0], outs[1]), outs[2], outs[3]


def devices_wait(x, land, sems, after, name):
    def body(x_ref, land_ref, send, recv, after_ref, x_thru, land_thru):
        mx, my, mc = _coords()
        for k, (px, py, pc) in enumerate(_peers(mx, my, mc)):
            cp = _remote(x_ref, land_ref.at[4 * px + 2 * py + pc], send.at[k], recv.at[k], (px, py, pc))
            cp.wait_send()
            cp.wait_recv()

    outs = pl.pallas_call(
        body, name=name, in_specs=[HBM, HBM, SEM, SEM, ANY], out_specs=[HBM, HBM],
        out_shape=[pltpu.HBM(x.shape, x.dtype), pltpu.HBM(land.shape, land.dtype)],
        input_output_aliases={0: 0, 1: 1}, compiler_params=_SPLIT_COPY,
    )(x, land, sems[0], sems[1], after)
    return outs[0], outs[1]


def device_sum(land, own, me, name):
    _, R, C = land.shape

    def body(s_ref, l_ref, o_ref_in, o_ref):
        acc = None
        for q in range(N_DEV):
            term = jnp.where(s_ref[0] == q, o_ref_in[...], l_ref[q])
            acc = term if acc is None else acc + term
        o_ref[...] = acc

    grid_spec = pltpu.PrefetchScalarGridSpec(
        num_scalar_prefetch=1, grid=(1,),
        in_specs=[pl.BlockSpec((N_DEV, R, C), lambda i, s: (0, 0, 0)), pl.BlockSpec((R, C), lambda i, s: (0, 0))],
        out_specs=pl.BlockSpec((R, C), lambda i, s: (0, 0)))
    return pl.pallas_call(body, name=name, grid_spec=grid_spec, out_shape=_sds((R, C), F32),
                          compiler_params=_params("arbitrary"))(me, land, own)


def reduce_send(grads, name):
    n = len(grads)

    def body(*refs):
        ins, lands = refs[:n], refs[n:2 * n]
        send, recv = refs[2 * n], refs[2 * n + 1]
        mx, my, mc = _coords()
        me = 4 * mx + 2 * my + mc
        for a in range(n):
            for k, (px, py, pc) in enumerate(_peers(mx, my, mc)):
                _remote(ins[a].at[2 * px + py, pc], lands[a].at[me], send.at[7 * a + k], recv.at[7 * a + k], (px, py, pc)).start()

    lands = [lax.empty((N_DEV,) + g.shape[2:], g.dtype) for g in grads]
    outs = pl.pallas_call(
        body, name=name, in_specs=[HBM] * (2 * n), out_specs=[SEM, SEM] + [HBM] * (2 * n),
        out_shape=[pltpu.SemaphoreType.DMA((7 * n,))] * 2 + [pltpu.HBM(a.shape, a.dtype) for a in grads + lands],
        input_output_aliases={a: 2 + a for a in range(2 * n)}, compiler_params=_SPLIT_COPY,
    )(*[_in_hbm(a) for a in grads + lands])
    return (outs[0], outs[1]), list(outs[2:2 + n]), list(outs[2 + n:])


def reduce_wait(grads, lands, sems, after, name):
    n = len(grads)

    def body(*refs):
        ins, zones = refs[:n], refs[n:2 * n]
        send, recv = refs[2 * n], refs[2 * n + 1]
        mx, my, mc = _coords()
        for a in range(n):
            for k, (px, py, pc) in enumerate(_peers(mx, my, mc)):
                cp = _remote(ins[a].at[2 * px + py, pc], zones[a].at[4 * px + 2 * py + pc], send.at[7 * a + k],
                             recv.at[7 * a + k], (px, py, pc))
                cp.wait_send()
                cp.wait_recv()

    outs = pl.pallas_call(
        body, name=name, in_specs=[HBM] * (2 * n) + [SEM, SEM, ANY], out_specs=[HBM] * (2 * n),
        out_shape=[pltpu.HBM(a.shape, a.dtype) for a in grads + lands],
        input_output_aliases={a: a for a in range(2 * n)}, compiler_params=_SPLIT_COPY,
    )(*grads, *lands, sems[0], sems[1], after)
    return list(outs[:n]), list(outs[n:])


def reduce_sum(land, grad, place, name, into=None, layer=None):
    _, hR, C = land.shape
    tr = hR
    while N_DEV * tr * C * 2 > (6 << 20) and tr % 32 == 0:
        tr //= 2

    def body(s_ref, l_ref, g_ref, *rest):
        o_ref = rest[-1]
        own = g_ref[...].astype(F32)
        acc = None
        for q in range(N_DEV):
            term = jnp.where(s_ref[2] == q, own, l_ref[q].astype(F32))
            acc = term if acc is None else acc + term
        o_ref[...] = acc

    in_specs = [pl.BlockSpec((N_DEV, tr, C), lambda i, s: (0, i, 0)),
                pl.BlockSpec((None, None, tr, C), lambda i, s: (s[0], s[1], i, 0))]
    args = [place, land, grad]
    aliases = {}
    if layer is None:
        out_spec = pl.BlockSpec((None, tr, C), lambda i, s: (s[1], i, 0))
        out_shape = _sds((2, hR, C), F32)
    else:
        out_spec = pl.BlockSpec((None, None, tr, C), lambda i, s: (layer, s[1], i, 0))
        out_shape = _sds((2, 2, hR, C), F32)
        if into is not None:
            in_specs.append(ANY)
            args.append(into)
            aliases = {3: 0}
    grid_spec = pltpu.PrefetchScalarGridSpec(num_scalar_prefetch=1, grid=(hR // tr,), in_specs=in_specs, out_specs=out_spec)
    return pl.pallas_call(body, name=name, grid_spec=grid_spec, out_shape=out_shape, input_output_aliases=aliases,
                          compiler_params=_params("arbitrary"))(*args)


def join_halves(arrays, name):
    n = len(arrays)
    pieces = [(a, l) for a, arr in enumerate(arrays) for l in (range(arr.shape[0]) if arr.ndim == 4 else [None])]

    def body(*refs):
        ins = refs[:n]
        send, recv = refs[2 * n:]
        mx, my, mc = _coords()

        def half(a, l, h):
            return ins[a].at[h] if l is None else ins[a].at[l, h]

        sends = [_remote(half(a, l, mc), half(a, l, mc), send.at[i], recv.at[i], (mx, my, 1 - mc))
                 for i, (a, l) in enumerate(pieces)]
        for cp in sends:
            cp.start()
        for i, (a, l) in enumerate(pieces):
            theirs = half(a, l, 1 - mc)
            _remote(theirs, theirs, send.at[i], recv.at[i], (mx, my, 1 - mc)).wait_recv()
        for cp in sends:
            cp.wait_send()

    return pl.pallas_call(
        body, name=name, in_specs=[ANY] * n, out_specs=[ANY] * n, out_shape=[_sds(a.shape, a.dtype) for a in arrays],
        input_output_aliases={a: a for a in range(n)},
        scratch_shapes=[pltpu.SemaphoreType.DMA((len(pieces),)), pltpu.SemaphoreType.DMA((len(pieces),))],
    )(*arrays)


LANES = 128
SUBLANES = 8


def _n_rows(shape):
    rows = -(-int(np.prod(shape)) // LANES)
    return -(-rows // SUBLANES) * SUBLANES


def _as_rows(a):
    flat = a.reshape(-1)
    rows = _n_rows(a.shape)
    return jnp.pad(flat, (0, rows * LANES - flat.shape[0])).reshape(rows, LANES)


def _pack(arrays):
    return jnp.concatenate([_as_rows(a) for a in arrays], axis=0)


def _unpack(rows, shapes):
    out, r0 = [], 0
    for s in shapes:
        n = _n_rows(s)
        out.append(rows[r0:r0 + n].reshape(-1)[:int(np.prod(s))].reshape(s))
        r0 += n
    return out


REPLICATED_SMALL = [("rel_bias", (32, 16)), ("even_norm", (1, 1024)), ("even_pool_w", (1, 4, 128, 128)),
                    ("even_pool_scale", (1, 512)), ("odd_q_norm", (1, 64)), ("odd_k_norm", (1, 64)),
                    ("ffn_norm", (2, 1024)), ("ffn_conv_b", (2, 5632))]
SHARDED_SMALL = [("even_conv_w", (1, 3, 128)), ("odd_norm", (1, 256)), ("ffn_conv_w", (2, 3, 1408))]
BIG = ["even_w_in", "even_w_out", "odd_w_qkv", "odd_w_o", "ffn_w_up", "ffn_w_down"]
WEIGHT_ORDER = ["rel_bias", "even_norm", "even_w_in", "even_conv_w", "even_pool_w", "even_pool_scale", "even_w_out",
                "odd_norm", "odd_w_qkv", "odd_q_norm", "odd_k_norm", "odd_w_o", "ffn_norm", "ffn_w_up", "ffn_conv_w",
                "ffn_conv_b", "ffn_w_down"]


def kernel(x, rel_bias, even_norm, even_w_in, even_conv_w, even_pool_w, even_pool_scale, even_w_out, odd_norm, odd_w_qkv, odd_q_norm, odd_k_norm, odd_w_o, ffn_norm, ffn_w_up, ffn_conv_w, ffn_conv_b, ffn_w_down, loss_target, m_rel_bias, m_even_norm, m_even_w_in, m_even_conv_w, m_even_pool_w, m_even_pool_scale, m_even_w_out, m_odd_norm, m_odd_w_qkv, m_odd_q_norm, m_odd_k_norm, m_odd_w_o, m_ffn_norm, m_ffn_w_up, m_ffn_conv_w, m_ffn_conv_b, m_ffn_w_down, v_rel_bias, v_even_norm, v_even_w_in, v_even_conv_w, v_even_pool_w, v_even_pool_scale, v_even_w_out, v_odd_norm, v_odd_w_qkv, v_odd_q_norm, v_odd_k_norm, v_odd_w_o, v_ffn_norm, v_ffn_w_up, v_ffn_conv_w, v_ffn_conv_b, v_ffn_w_down):
    W = dict(rel_bias=rel_bias, even_norm=even_norm, even_w_in=even_w_in, even_conv_w=even_conv_w, even_pool_w=even_pool_w,
             even_pool_scale=even_pool_scale, even_w_out=even_w_out, odd_norm=odd_norm, odd_w_qkv=odd_w_qkv,
             odd_q_norm=odd_q_norm, odd_k_norm=odd_k_norm, odd_w_o=odd_w_o, ffn_norm=ffn_norm, ffn_w_up=ffn_w_up,
             ffn_conv_w=ffn_conv_w, ffn_conv_b=ffn_conv_b, ffn_w_down=ffn_w_down)
    M1 = dict(rel_bias=m_rel_bias, even_norm=m_even_norm, even_w_in=m_even_w_in, even_conv_w=m_even_conv_w,
              even_pool_w=m_even_pool_w, even_pool_scale=m_even_pool_scale, even_w_out=m_even_w_out, odd_norm=m_odd_norm,
              odd_w_qkv=m_odd_w_qkv, odd_q_norm=m_odd_q_norm, odd_k_norm=m_odd_k_norm, odd_w_o=m_odd_w_o,
              ffn_norm=m_ffn_norm, ffn_w_up=m_ffn_w_up, ffn_conv_w=m_ffn_conv_w, ffn_conv_b=m_ffn_conv_b,
              ffn_w_down=m_ffn_w_down)
    M2 = dict(rel_bias=v_rel_bias, even_norm=v_even_norm, even_w_in=v_even_w_in, even_conv_w=v_even_conv_w,
              even_pool_w=v_even_pool_w, even_pool_scale=v_even_pool_scale, even_w_out=v_even_w_out, odd_norm=v_odd_norm,
              odd_w_qkv=v_odd_w_qkv, odd_q_norm=v_odd_q_norm, odd_k_norm=v_odd_k_norm, odd_w_o=v_odd_w_o,
              ffn_norm=v_ffn_norm, ffn_w_up=v_ffn_w_up, ffn_conv_w=v_ffn_conv_w, ffn_conv_b=v_ffn_conv_b,
              ffn_w_down=v_ffn_w_down)
    mx, my, mc = _coords()
    chip = 2 * mx + my
    me = 4 * mx + 2 * my + mc
    place = jnp.stack([chip, mc, me]).astype(jnp.int32)
    xs, target = x[0], loss_target[0]

    def halves(w):
        return w.reshape((w.shape[0], 2, w.shape[-2] // 2, w.shape[-1]))

    lands = [cast_into_slot(halves(even_w_in), 0, place, "cast_w_in"), cast_into_slot(halves(even_w_out), 0, place, "cast_w_out"),
             cast_into_slot(halves(ffn_w_up), 0, place, "cast_w_up0"), cast_into_slot(halves(ffn_w_down), 0, place, "cast_w_down0"),
             cast_into_slot(halves(odd_w_qkv), 0, place, "cast_w_qkv"), cast_into_slot(halves(odd_w_o), 0, place, "cast_w_o"),
             cast_into_slot(halves(ffn_w_up), 1, place, "cast_w_up1"), cast_into_slot(halves(ffn_w_down), 1, place, "cast_w_down1")]
    small = allgather_devices(_pack([even_conv_w, odd_norm, ffn_conv_w]), "allgather_small_weights")
    lands[0], small = lax.optimization_barrier((lands[0], small))
    gather_sems, lands, token = gather_start(lands, [[0, 1], [2, 3], [4, 5], [6, 7]], "gather_start")
    even_norm_after_start = even_norm + token[0:1, 0:1]
    small = small[0::2]
    conv_w_full = small[:, 0:3].transpose(1, 0, 2).reshape(3, A_WIDTH)
    odd_norm_full = small[:, 8:10].reshape(1, D_MODEL)
    ffn_cw_full = small[:, 16:82].reshape(N_CHIPS, 2, 3, 2 * D_FF // N_CHIPS).transpose(1, 2, 0, 3).reshape(2, 3, 2 * D_FF)
    pool_w = cast_bf16(even_pool_w[0], "cast_pool_w")
    gqk = jnp.stack([jnp.tile(odd_q_norm[0], N_HEADS), jnp.tile(odd_k_norm[0], N_HEADS),
                     jnp.ones((D_MODEL,), F32)])[:, None, :]
    bias = bias_expand(rel_bias.T, "bias_expand").reshape(6, N_HEADS, ATT_BLOCK, 2 * ATT_BLOCK)

    def ffn_fwd(l, xin, xn):
        up = mm_nn(xn, w_up[l], f"ffn{l}_up", out_dtype=BF16)
        u, act = glu_fwd(up, ffn_cw_full[l], ffn_conv_b[l:l + 1], f"ffn{l}_glu")
        return act, (xin, xn, up, u, act)

    def gathered(group, tag, after_landing, after_passing):
        sems, arrays = gather_forward(lands[2 * group:2 * group + 2], gather_sems[group], after_landing, "gather_forward_" + tag)
        return gather_wait(arrays, sems, after_passing, "gather_wait_" + tag)

    def ffn_weights(got):
        return got[0].reshape(N_CHIPS, 1, D_MODEL, 2 * D_FF // N_CHIPS), got[1].reshape(1, 1, D_FF, D_MODEL)

    w_up, w_down = [None, None], [None, None]
    xn0 = rmsnorm_fwd(xs, even_norm_after_start, "even_norm")
    got = gathered(0, "even", bias, xn0)
    w_in = got[0].reshape(N_CHIPS, 1, D_MODEL, EVEN_IN // N_CHIPS)
    w_out = got[1].reshape(1, 1, D_MODEL, D_MODEL)
    proj = mm_nn(xn0, w_in, "even_in")
    mix = mixer_fwd(proj, conv_w_full, pool_w, even_pool_scale, "even_mixer")
    x1, xn1 = mm_res_norm(mix, w_out, xs, ffn_norm[0:1], "even_out")
    w_up[0], w_down[0] = ffn_weights(gathered(1, "ffn0", proj, x1))
    act0, ffn0 = ffn_fwd(0, x1, xn1)
    x2, xn2 = mm_res_norm(act0, w_down[0], x1, odd_norm_full, "ffn0_down")
    got = gathered(2, "odd", x1, x2)
    w_qkv = got[0].reshape(N_CHIPS, 1, D_MODEL, 3 * D_MODEL // N_CHIPS)
    w_o = got[1].reshape(1, 1, D_MODEL, D_MODEL)
    qkv = mm_nn(xn2, w_qkv, "odd_qkv")
    qkvn = qknorm_fwd(qkv, gqk, "odd_qknorm")
    att, lse = attn_fwd(qkvn, bias, "attn_fwd")
    x3, xn3 = mm_res_norm(att, w_o, x2, ffn_norm[1:2], "odd_out")
    w_up[1], w_down[1] = ffn_weights(gathered(3, "ffn1", x2, x3))
    act1, ffn1 = ffn_fwd(1, x3, xn3)
    dy, dyb, sq = mm_res_loss(act1, w_down[1], x3, target, "ffn1_down_loss")
    loss = lax.psum(0.5 * jnp.sum(sq) * (1.0 / D_MODEL), ("x", "y", "c"))

    def ffn_bwd(l, dy, dyb, saved):
        xin, xn, up, u, act = saved
        dw_down = mm_tn(act, dyb, f"ffn{l}_dw_down", J=1, tk=D_FF // 2, tm=1024)
        dact = mm_nt(dyb, w_down[l], f"ffn{l}_dact", tr=D_FF // 2, out_dtype=BF16, tm=1024)
        dup, dcw, dcb = glu_bwd(up, u, dact, ffn_cw_full[l], f"ffn{l}_glu_bwd")
        dw_up = mm_tn(xn, dup, f"ffn{l}_dw_up", J=N_CHIPS, tk=512, tm=1024, jb=2)
        dx, dxb, dg = mm_nt_norm_bwd(dup, w_up[l], xin, ffn_norm[l:l + 1], dy, f"ffn{l}_dx")
        return dx, dxb, (dw_down, dw_up, dcw, dcb, dg)

    def quarters(g):
        return g.reshape(N_CHIPS, 2, g.shape[0] * g.shape[1] // (2 * N_CHIPS), g.shape[-1])

    def reduce_start(grads, tag, then):
        sems, parts, zones = reduce_send([quarters(g) for g in grads], "reduce_send_" + tag)
        then, parts = lax.optimization_barrier((then, parts))
        return (sems, parts, zones), then

    dx3, dx3b, g_ffn1 = ffn_bwd(1, dy, dyb, ffn1)
    red_ffn1, (dx3, dx3b) = reduce_start([g_ffn1[1], g_ffn1[0]], "ffn1", (dx3, dx3b))
    dw_o = mm_tn(att, dx3b, "odd_dw_o", J=1, tk=512, tm=1024)
    datt = mm_nt(dx3b, w_o, "odd_datt", tr=D_MODEL, out_dtype=BF16)
    dq, dk, dv, dbias = attn_bwd(qkvn, att, datt, lse, bias, "attn_bwd")
    dqkv, dgqk = qknorm_bwd(qkv, dq, dk, dv, gqk, "odd_qknorm_bwd")
    dw_qkv = mm_tn(xn2, dqkv, "odd_dw_qkv", J=N_CHIPS, tk=512, tm=1024)
    red_odd, dqkv = reduce_start([dw_qkv, dw_o], "odd", dqkv)
    dx2, dx2b, dg_odd = mm_nt_norm_bwd(dqkv, w_qkv, x2, odd_norm_full, dx3, "odd_dx")
    dx1, dx1b, g_ffn0 = ffn_bwd(0, dx2, dx2b, ffn0)
    red_ffn0, (dx1, dx1b) = reduce_start([g_ffn0[1], g_ffn0[0]], "ffn0", (dx1, dx1b))
    dw_out = mm_tn(mix, dx1b, "even_dw_out", J=1, tk=512, tm=1024)
    dmix = mm_nt(dx1b, w_out, "even_dmix", tr=D_MODEL)
    dproj, dcw_even, dpw, dps = mixer_bwd(proj, dmix, conv_w_full, pool_w, even_pool_scale, "even_mixer_bwd")
    dw_in = mm_tn(xn0, dproj, "even_dw_in", J=N_CHIPS, tk=512, tm=1024)
    grad_x, _, dg_even = mm_nt_norm_bwd(dproj, w_in, xs, even_norm, dx1, "even_dx")
    d_rel = jnp.sum(bias_reduce(dbias.reshape(3, N_HEADS, 2 * ATT_BLOCK * ATT_BLOCK), "bias_reduce"), axis=0).T

    red_even, grad_x = reduce_start([dw_in, dw_out], "even", grad_x)

    dcw_sh = dcw_even.reshape(3, N_CHIPS, A_WIDTH // N_CHIPS).transpose(1, 0, 2)
    don_sh = dg_odd.reshape(N_CHIPS, D_MODEL // N_CHIPS)
    dfcw = jnp.stack([g_ffn0[2], g_ffn1[2]])
    dfcw_sh = dfcw.reshape(2, 3, N_CHIPS, 2 * D_FF // N_CHIPS).transpose(2, 0, 1, 3)
    rep_grads = [d_rel, dg_even, dpw[None], dps, _head_sum(dgqk[0]), _head_sum(dgqk[1]),
                 jnp.concatenate([g_ffn0[4], g_ffn1[4]], axis=0), jnp.concatenate([g_ffn0[3], g_ffn1[3]], axis=0)]
    rep_rows = _pack(rep_grads)
    shard_rows = jnp.concatenate([_pack([dcw_sh[j], don_sh[j], dfcw_sh[j]]) for j in range(N_CHIPS)], axis=0)
    n_rep, n_shard = rep_rows.shape[0], shard_rows.shape[0] // N_CHIPS
    small_sems, small_rows, small_land = devices_start(jnp.concatenate([rep_rows, shard_rows], axis=0), "small_grads_start")
    grad_x, small_rows = lax.optimization_barrier((grad_x, small_rows))

    def reduce_end(red, tag, after):
        sems, parts, zones = red
        parts, zones = reduce_wait(parts, zones, sems, after, "reduce_wait_" + tag)
        return zones, parts

    z_ffn1, p_ffn1 = reduce_end(red_ffn1, "ffn1", grad_x)
    z_odd, p_odd = reduce_end(red_odd, "odd", grad_x)
    z_ffn0, p_ffn0 = reduce_end(red_ffn0, "ffn0", grad_x)
    r_up = reduce_sum(z_ffn0[0], p_ffn0[0], place, "reduce_sum_w_up0", layer=0)
    r_up = reduce_sum(z_ffn1[0], p_ffn1[0], place, "reduce_sum_w_up1", into=r_up, layer=1)
    r_down = reduce_sum(z_ffn0[1], p_ffn0[1], place, "reduce_sum_w_down0", layer=0)
    r_down = reduce_sum(z_ffn1[1], p_ffn1[1], place, "reduce_sum_w_down1", into=r_down, layer=1)
    later = ["odd_w_qkv", "odd_w_o", "ffn_w_up", "ffn_w_down"]
    joined = join_halves([reduce_sum(z_odd[0], p_odd[0], place, "reduce_sum_w_qkv"),
                          reduce_sum(z_odd[1], p_odd[1], place, "reduce_sum_w_o"), r_up, r_down], "grads_join_late_layers")
    G = {nm: g.reshape(W[nm].shape) for nm, g in zip(later, joined)}

    D_, NM, NV = {}, {}, {}

    def update(nm):
        as3 = lambda a: a.reshape((-1,) + a.shape[-2:])
        outs = adamw(as3(W[nm]), as3(G[nm]), as3(M1[nm]), as3(M2[nm]), "adamw_" + nm)
        D_[nm], NM[nm], NV[nm] = [o.reshape(W[nm].shape) for o in outs]

    for nm in later:
        update(nm)
    z_even, p_even = reduce_end(red_even, "even", D_[later[-1]])
    joined = join_halves([reduce_sum(z_even[0], p_even[0], place, "reduce_sum_w_in"),
                          reduce_sum(z_even[1], p_even[1], place, "reduce_sum_w_out")], "grads_join_first_layer")
    for nm, g in zip(["even_w_in", "even_w_out"], joined):
        G[nm] = g.reshape(W[nm].shape)
        update(nm)
    small_rows, small_land = devices_wait(small_rows, small_land, small_sems, D_["even_w_out"], "small_grads_wait")
    small_sum = device_sum(small_land, small_rows, place[2:3], "small_grads_sum")
    mine = lax.dynamic_slice_in_dim(small_sum, n_rep + chip * n_shard, n_shard, axis=0)
    g_small = jnp.concatenate([small_sum[:n_rep], mine], axis=0)
    small_names = [n for n, _ in REPLICATED_SMALL + SHARDED_SMALL]
    small_shapes = [s for _, s in REPLICATED_SMALL + SHARDED_SMALL]
    G.update(dict(zip(small_names, _unpack(g_small, small_shapes))))
    packs = [_pack([d[n] for n in small_names])[None] for d in (W, M1, M2)]
    outs = adamw(packs[0], g_small[None], packs[1], packs[2], "adamw_small")
    for dst, o in zip((D_, NM, NV), outs):
        dst.update(dict(zip(small_names, _unpack(o[0], small_shapes))))

    return (loss, grad_x[None], *[G[n] for n in WEIGHT_ORDER], *[D_[n] for n in WEIGHT_ORDER],
            *[NM[n] for n in WEIGHT_ORDER], *[NV[n] for n in WEIGHT_ORDER])


def _head_sum(dg):
    return jnp.sum(dg.reshape(N_HEADS, HEAD_DIM), axis=0, keepdims=True)
```

```python
import functools
import math

import numpy as np
import jax
import jax.numpy as jnp
from jax import lax
from jax.experimental import pallas as pl
from jax.experimental.pallas import tpu as pltpu

F32 = jnp.float32
BF16 = jnp.bfloat16

D_MODEL = 1024
N_HEADS = 16
HEAD_DIM = 64
A_WIDTH = 512
POOL_WINDOWS = (2, 4, 8, 16)
POOL_GROUP = 128
EVEN_IN = 2048
D_FF = 2816
DILATED_PAIRS = ((128, 1), (512, 4), (2048, 16))
ATT_BLOCK = 128
N_REL_BUCKETS = 32
REL_MAX_DISTANCE = 2048
EPS = 1e-6
MASK_VALUE = -1e30
ADAM_LR, ADAM_B1, ADAM_B2, ADAM_EPS, ADAM_WD, ADAM_STEP = 0.001, 0.9, 0.999, 1e-08, 0.01, 10

VMEM_LIMIT_BYTES = 48 * 1024 * 1024
N_CHIPS = 4
N_DEV = 8
MESH = pl.DeviceIdType.MESH


def _params(*sem):
    return pltpu.CompilerParams(dimension_semantics=sem if sem else None, vmem_limit_bytes=VMEM_LIMIT_BYTES)


def _sds(shape, dtype):
    return jax.ShapeDtypeStruct(tuple(shape), dtype)


def cast_bf16(x, name, tr=None):
    lead, (R, C) = x.shape[:-2], x.shape[-2:]
    n = int(np.prod(lead)) if lead else 1
    x3 = x.reshape((n, R, C))
    tr = tr or R

    def body(x_ref, o_ref):
        o_ref[...] = x_ref[...].astype(BF16)

    out = pl.pallas_call(
        body, name=name, grid=(n, R // tr),
        in_specs=[pl.BlockSpec((None, tr, C), lambda i, r: (i, r, 0))],
        out_specs=pl.BlockSpec((None, tr, C), lambda i, r: (i, r, 0)),
        out_shape=_sds((n, R, C), BF16), compiler_params=_params("parallel", "parallel"),
    )(x3)
    return out.reshape(lead + (R, C))


def rmsnorm_fwd(x, g, name, ts=512):
    S, Dm = x.shape

    def body(x_ref, g_ref, o_ref):
        xv = x_ref[...]
        r = lax.rsqrt(jnp.mean(xv * xv, axis=-1, keepdims=True) + EPS)
        o_ref[...] = ((xv * r) * g_ref[...]).astype(BF16)

    return pl.pallas_call(
        body, name=name, grid=(S // ts,),
        in_specs=[pl.BlockSpec((ts, Dm), lambda i: (i, 0)), pl.BlockSpec((1, Dm), lambda i: (0, 0))],
        out_specs=pl.BlockSpec((ts, Dm), lambda i: (i, 0)),
        out_shape=_sds((S, Dm), BF16), compiler_params=_params("parallel"),
    )(x, g)


def mm_nn(a, w, name, layer=0, res=None, out_dtype=F32, tm=1024):
    M, K = a.shape
    J, _, _, Ns = w.shape

    def body(*refs):
        a_ref, w_ref = refs[0], refs[1]
        o_ref = refs[-1]
        acc = jnp.dot(a_ref[...], w_ref[...], preferred_element_type=F32)
        if res is not None:
            acc = refs[2][...] + acc
        o_ref[...] = acc.astype(o_ref.dtype)

    in_specs = [pl.BlockSpec((tm, K), lambda j, m: (m, 0)),
                pl.BlockSpec((None, None, K, Ns), lambda j, m: (j, layer, 0, 0))]
    args = [a, w]
    if res is not None:
        in_specs.append(pl.BlockSpec((tm, Ns), lambda j, m: (m, j)))
        args.append(res)
    return pl.pallas_call(
        body, name=name, grid=(J, M // tm), in_specs=in_specs,
        out_specs=pl.BlockSpec((tm, Ns), lambda j, m: (m, j)),
        out_shape=_sds((M, J * Ns), out_dtype), compiler_params=_params("parallel", "parallel"),
    )(*args)


def mm_res_norm(a, w, res, gain, name, tm=1024):
    M, K = a.shape
    Dm = w.shape[-1]

    def body(a_ref, w_ref, r_ref, g_ref, y_ref, yn_ref):
        y = r_ref[...] + jnp.dot(a_ref[...], w_ref[...], preferred_element_type=F32)
        y_ref[...] = y
        r = lax.rsqrt(jnp.mean(y * y, axis=-1, keepdims=True) + EPS)
        yn_ref[...] = ((y * r) * g_ref[...]).astype(BF16)

    row = pl.BlockSpec((tm, Dm), lambda m: (m, 0))
    return pl.pallas_call(
        body, name=name, grid=(M // tm,),
        in_specs=[pl.BlockSpec((tm, K), lambda m: (m, 0)),
                  pl.BlockSpec((None, None, K, Dm), lambda m: (0, 0, 0, 0), pipeline_mode=pl.Buffered(1)),
                  row, pl.BlockSpec((1, Dm), lambda m: (0, 0))],
        out_specs=[row, row], out_shape=[_sds((M, Dm), F32), _sds((M, Dm), BF16)],
        compiler_params=_params("parallel"),
    )(a, w, res, gain)


def mm_res_loss(a, w, res, target, name, tm=512):
    M, K = a.shape
    Dm = w.shape[-1]

    def body(a_ref, w_ref, r_ref, t_ref, d_ref, db_ref, s_ref):
        e = (r_ref[...] + jnp.dot(a_ref[...], w_ref[...], preferred_element_type=F32)) - t_ref[...]
        d = e * (1.0 / Dm)
        d_ref[...] = d
        db_ref[...] = d.astype(BF16)
        part = jnp.sum(e * e, axis=0, keepdims=True)

        @pl.when(pl.program_id(0) == 0)
        def _():
            s_ref[...] = part

        @pl.when(pl.program_id(0) > 0)
        def _():
            s_ref[...] += part

    row = pl.BlockSpec((tm, Dm), lambda m: (m, 0))
    return pl.pallas_call(
        body, name=name, grid=(M // tm,),
        in_specs=[pl.BlockSpec((tm, K), lambda m: (m, 0)),
                  pl.BlockSpec((None, None, K, Dm), lambda m: (0, 0, 0, 0), pipeline_mode=pl.Buffered(1)), row, row],
        out_specs=[row, row, pl.BlockSpec((1, Dm), lambda m: (0, 0))],
        out_shape=[_sds((M, Dm), F32), _sds((M, Dm), BF16), _sds((1, Dm), F32)],
        compiler_params=_params("arbitrary"),
    )(a, w, res, target)


def mm_nt(dy, w, name, tr, layer=0, out_dtype=F32, tm=512):
    M = dy.shape[0]
    J, _, R, Ns = w.shape
    dims = (((1,), (1,)), ((), ()))

    def body(dy_ref, w_ref, o_ref):
        acc = None
        for j in range(J):
            p = lax.dot_general(dy_ref[:, j * Ns:(j + 1) * Ns], w_ref[j], dims, preferred_element_type=F32)
            acc = p if acc is None else acc + p
        o_ref[...] = acc.astype(o_ref.dtype)

    return pl.pallas_call(
        body, name=name, grid=(R // tr, M // tm),
        in_specs=[pl.BlockSpec((tm, J * Ns), lambda r, m: (m, 0)),
                  pl.BlockSpec((J, None, tr, Ns), lambda r, m: (0, layer, r, 0))],
        out_specs=pl.BlockSpec((tm, tr), lambda r, m: (m, r)),
        out_shape=_sds((M, R), out_dtype),
        compiler_params=_params("parallel", "parallel"),
    )(dy, w)


def mm_nt_norm_bwd(dy, w, x, g, dres, name, layer=0, tm=512):
    M = dy.shape[0]
    J, _, Dm, Ns = w.shape
    dims = (((1,), (1,)), ((), ()))

    def body(dy_ref, w_ref, x_ref, g_ref, r_ref, dx_ref, dxb_ref, dg_ref):
        dxn = None
        for j in range(J):
            p = lax.dot_general(dy_ref[:, j * Ns:(j + 1) * Ns], w_ref[j], dims, preferred_element_type=F32)
            dxn = p if dxn is None else dxn + p
        xv = x_ref[...]
        r = lax.rsqrt(jnp.mean(xv * xv, axis=-1, keepdims=True) + EPS)
        gx = dxn * g_ref[...]
        dot = jnp.sum(gx * xv, axis=-1, keepdims=True)
        dx = r_ref[...] + r * gx - xv * ((r * r * r) * (dot * (1.0 / Dm)))
        dx_ref[...] = dx
        dxb_ref[...] = dx.astype(BF16)
        part = jnp.sum(dxn * (xv * r), axis=0, keepdims=True)

        @pl.when(pl.program_id(0) == 0)
        def _():
            dg_ref[...] = part

        @pl.when(pl.program_id(0) > 0)
        def _():
            dg_ref[...] += part

    row = pl.BlockSpec((tm, Dm), lambda m: (m, 0))
    vec = pl.BlockSpec((1, Dm), lambda m: (0, 0))
    return pl.pallas_call(
        body, name=name, grid=(M // tm,),
        in_specs=[pl.BlockSpec((tm, J * Ns), lambda m: (m, 0)),
                  pl.BlockSpec((J, None, Dm, Ns), lambda m: (0, layer, 0, 0), pipeline_mode=pl.Buffered(1)), row, vec, row],
        out_specs=[row, row, vec],
        out_shape=[_sds((M, Dm), F32), _sds((M, Dm), BF16), _sds((1, Dm), F32)],
        compiler_params=_params("arbitrary"),
    )(dy, w, x, g, dres)


def mm_tn(a, dy, name, J, tk, tm=512, jb=None):
    M, K = a.shape
    jb = jb or J
    Ns = dy.shape[1] // J
    N = jb * Ns
    n_m = M // tm
    dims = (((0,), (0,)), ((), ()))

    def body(a_ref, dy_ref, o_ref, acc_ref):
        p = lax.dot_general(a_ref[...], dy_ref[...], dims, preferred_element_type=F32)
        m = pl.program_id(2)

        @pl.when(m == 0)
        def _():
            acc_ref[...] = p

        @pl.when(m > 0)
        def _():
            acc_ref[...] += p

        @pl.when(m == n_m - 1)
        def _():
            for j in range(jb):
                o_ref[j] = acc_ref[:, j * Ns:(j + 1) * Ns].astype(BF16)

    return pl.pallas_call(
        body, name=name, grid=(J // jb, K // tk, n_m),
        in_specs=[pl.BlockSpec((tm, tk), lambda g, k, m: (m, k)), pl.BlockSpec((tm, N), lambda g, k, m: (m, g))],
        out_specs=pl.BlockSpec((jb, tk, Ns), lambda g, k, m: (g, k, 0)),
        out_shape=_sds((J, K, Ns), BF16), scratch_shapes=[pltpu.VMEM((tk, N), F32)],
        compiler_params=_params("parallel", "parallel", "arbitrary"),
    )(a, dy)


HALO = 16


def _shift_down(x, s):
    return pltpu.roll(x, s, 0)


def _shift_up(x, s):
    return pltpu.roll(x, x.shape[0] - s, 0)


def _conv3(z, cw):
    return (_shift_down(z, 2) * cw[0:1] + _shift_down(z, 1) * cw[1:2]) + z * cw[2:3]


def _window_count(first_row, n, k):
    t = first_row + lax.broadcasted_iota(jnp.int32, (n, 1), 0)
    return jnp.clip(t + 1, 1, k).astype(F32)


def mixer_fwd(proj, conv_w, pool_w, pool_scale, name, ts=256):
    S = proj.shape[0]
    n = ts + HALO

    def body(pm_ref, pb_ref, cw_ref, pw_ref, ps_ref, o_ref):
        i = pl.program_id(0)
        before = jnp.where(i > 0, pb_ref[...], 0.0)
        ext = jnp.concatenate([before, pm_ref[...]], axis=0)
        cw = cw_ref[...]
        z = ext[:, 2 * A_WIDTH:3 * A_WIDTH] * ext[:, 0:A_WIDTH]
        cz = _conv3(z, cw)
        ya = pm_ref[:, A_WIDTH:2 * A_WIDTH] * cz[HALO:]
        o_ref[:, 0:A_WIDTH] = ya.astype(BF16)
        for g, k in enumerate(POOL_WINDOWS):
            lo = 3 * A_WIDTH + g * POOL_GROUP
            p = ext[:, lo:lo + POOL_GROUP]
            w = p
            s = 1
            while s < k:
                w = w + _shift_down(w, s)
                s *= 2
            pooled = w / _window_count(i * ts - HALO, n, k) - p
            yb = jnp.dot(pooled[HALO:].astype(BF16), pw_ref[g], preferred_element_type=F32)
            yb = yb * ps_ref[:, g * POOL_GROUP:(g + 1) * POOL_GROUP]
            o_ref[:, A_WIDTH + g * POOL_GROUP:A_WIDTH + (g + 1) * POOL_GROUP] = yb.astype(BF16)

    hb = ts // HALO
    return pl.pallas_call(
        body, name=name, grid=(S // ts,),
        in_specs=[
            pl.BlockSpec((ts, EVEN_IN), lambda i: (i, 0)),
            pl.BlockSpec((HALO, EVEN_IN), lambda i: (jnp.maximum(i * hb - 1, 0), 0)),
            pl.BlockSpec((3, A_WIDTH), lambda i: (0, 0)),
            pl.BlockSpec((4, POOL_GROUP, POOL_GROUP), lambda i: (0, 0, 0)),
            pl.BlockSpec((1, 4 * POOL_GROUP), lambda i: (0, 0)),
        ],
        out_specs=pl.BlockSpec((ts, D_MODEL), lambda i: (i, 0)),
        out_shape=_sds((S, D_MODEL), BF16), compiler_params=_params("parallel"),
    )(proj, proj, conv_w, pool_w, pool_scale)


def mixer_bwd(proj, dmix, conv_w, pool_w, pool_scale, name, ts=256):
    S = proj.shape[0]
    n = ts + 2 * HALO
    nt = S // ts
    tn_dims = (((0,), (0,)), ((), ()))
    nt_dims = (((1,), (1,)), ((), ()))

    def body(pm_ref, pb_ref, pa_ref, dm_ref, da_ref, cw_ref, pw_ref, ps_ref, o_ref, dcw_ref, dpw_ref, dps_ref):
        i = pl.program_id(0)
        last = i == nt - 1
        before = jnp.where(i > 0, pb_ref[...], 0.0)
        after = jnp.where(last, 0.0, pa_ref[...])
        ext = jnp.concatenate([before, pm_ref[...], after], axis=0)
        dafter = jnp.where(last, 0.0, da_ref[...])
        dext = jnp.concatenate([jnp.zeros((HALO, D_MODEL), F32), dm_ref[...], dafter], axis=0)
        cw = cw_ref[...]
        main = slice(HALO, HALO + ts)

        @pl.when(i == 0)
        def _():
            dcw_ref[...] = jnp.zeros_like(dcw_ref)
            dpw_ref[...] = jnp.zeros_like(dpw_ref)
            dps_ref[...] = jnp.zeros_like(dps_ref)

        h, gb, gc = ext[:, 0:A_WIDTH], ext[:, A_WIDTH:2 * A_WIDTH], ext[:, 2 * A_WIDTH:3 * A_WIDTH]
        z = gc * h
        z1, z2 = _shift_down(z, 1), _shift_down(z, 2)
        cz = (z2 * cw[0:1] + z1 * cw[1:2]) + z * cw[2:3]
        dya = dext[:, 0:A_WIDTH]
        dcz = dya * gb
        dz = dcz * cw[2:3] + _shift_up(dcz, 1) * cw[1:2] + _shift_up(dcz, 2) * cw[0:1]
        o_ref[:, 0:A_WIDTH] = (dz * gc)[main].astype(BF16)
        o_ref[:, A_WIDTH:2 * A_WIDTH] = (dya * cz)[main].astype(BF16)
        o_ref[:, 2 * A_WIDTH:3 * A_WIDTH] = (dz * h)[main].astype(BF16)
        dczm = dcz[main]
        dcw_ref[0:1, :] += jnp.sum(dczm * z2[main], axis=0, keepdims=True)
        dcw_ref[1:2, :] += jnp.sum(dczm * z1[main], axis=0, keepdims=True)
        dcw_ref[2:3, :] += jnp.sum(dczm * z[main], axis=0, keepdims=True)

        for g, k in enumerate(POOL_WINDOWS):
            lo = 3 * A_WIDTH + g * POOL_GROUP
            cols = slice(g * POOL_GROUP, (g + 1) * POOL_GROUP)
            p = ext[:, lo:lo + POOL_GROUP]
            w = p
            s = 1
            while s < k:
                w = w + _shift_down(w, s)
                s *= 2
            cnt = _window_count(i * ts - HALO, n, k)
            pooled = (w / cnt - p)[main].astype(BF16)
            dyb = dext[:, A_WIDTH + g * POOL_GROUP:A_WIDTH + (g + 1) * POOL_GROUP]
            e = dyb * ps_ref[:, cols]
            pre = jnp.dot(pooled, pw_ref[g], preferred_element_type=F32)
            dps_ref[:, cols] += jnp.sum(dyb[main] * pre, axis=0, keepdims=True)
            dpw_ref[g] += lax.dot_general(pooled, e[main].astype(BF16), tn_dims, preferred_element_type=F32)
            dpooled = lax.dot_general(e.astype(BF16), pw_ref[g], nt_dims, preferred_element_type=F32)
            q = dpooled / cnt
            a = q
            s = 1
            while s < k:
                a = a + _shift_up(a, s)
                s *= 2
            o_ref[:, lo:lo + POOL_GROUP] = (a - dpooled)[main].astype(BF16)

    hb = ts // HALO
    nh = S // HALO
    before_map = lambda i: (jnp.maximum(i * hb - 1, 0), 0)
    after_map = lambda i: (jnp.minimum((i + 1) * hb, nh - 1), 0)
    full = lambda *shape: pl.BlockSpec(shape, lambda i: (0,) * len(shape))
    return pl.pallas_call(
        body, name=name, grid=(nt,),
        in_specs=[
            pl.BlockSpec((ts, EVEN_IN), lambda i: (i, 0)),
            pl.BlockSpec((HALO, EVEN_IN), before_map),
            pl.BlockSpec((HALO, EVEN_IN), after_map),
            pl.BlockSpec((ts, D_MODEL), lambda i: (i, 0)),
            pl.BlockSpec((HALO, D_MODEL), after_map),
            full(3, A_WIDTH), full(4, POOL_GROUP, POOL_GROUP), full(1, 4 * POOL_GROUP),
        ],
        out_specs=[pl.BlockSpec((ts, EVEN_IN), lambda i: (i, 0)), full(3, A_WIDTH), full(4, POOL_GROUP, POOL_GROUP),
                   full(1, 4 * POOL_GROUP)],
        out_shape=[_sds((S, EVEN_IN), BF16), _sds((3, A_WIDTH), F32), _sds((4, POOL_GROUP, POOL_GROUP), F32),
                   _sds((1, 4 * POOL_GROUP), F32)],
        compiler_params=_params("arbitrary"),
    )(proj, proj, proj, dmix, dmix, conv_w, pool_w, pool_scale)


FFN_HALO = 16
FFN_TC = 1408


def glu_fwd(up, conv_w, conv_b, name, ts=256):
    S = up.shape[0]
    nc = D_FF // FFN_TC

    def body(gm_ref, gb_ref, um_ref, ub_ref, cwg_ref, cwu_ref, cbg_ref, cbu_ref, ug_ref, uu_ref, o_ref):
        i = pl.program_id(0)

        def conv(m_ref, b_ref, cw_ref, cb_ref):
            before = jnp.where(i > 0, b_ref[...].astype(F32), 0.0)
            ext = jnp.concatenate([before, m_ref[...].astype(F32)], axis=0)
            return _conv3(ext, cw_ref[...])[FFN_HALO:] + cb_ref[...]

        gate = conv(gm_ref, gb_ref, cwg_ref, cbg_ref)
        upv = conv(um_ref, ub_ref, cwu_ref, cbu_ref)
        ug_ref[...] = gate.astype(BF16)
        uu_ref[...] = upv.astype(BF16)
        o_ref[...] = ((gate * (1.0 / (1.0 + jnp.exp(-gate)))) * upv).astype(BF16)

    hb = ts // FFN_HALO
    main = lambda off: pl.BlockSpec((ts, FFN_TC), lambda i, c: (i, c + off))
    halo = lambda off: pl.BlockSpec((FFN_HALO, FFN_TC), lambda i, c: (jnp.maximum(i * hb - 1, 0), c + off))
    cw = lambda off: pl.BlockSpec((3, FFN_TC), lambda i, c: (0, c + off))
    cb = lambda off: pl.BlockSpec((1, FFN_TC), lambda i, c: (0, c + off))
    ug, uu, act = pl.pallas_call(
        body, name=name, grid=(S // ts, nc),
        in_specs=[main(0), halo(0), main(nc), halo(nc), cw(0), cw(nc), cb(0), cb(nc)],
        out_specs=[pl.BlockSpec((ts, FFN_TC), lambda i, c: (i, c))] * 3,
        out_shape=[_sds((S, D_FF), BF16)] * 3, compiler_params=_params("parallel", "parallel"),
    )(up, up, up, up, conv_w, conv_w, conv_b, conv_b)
    return (ug, uu), act


def glu_bwd(up, u, da, conv_w, name, ts=256):
    S = up.shape[0]
    nc = D_FF // FFN_TC
    nt = S // ts
    W = 2 * D_FF

    def body(x_ref, gm_ref, ga_ref, um_ref, ua_ref, dm_ref, da_ref, cw_ref, dx_ref, dcw_ref, dcb_ref):
        i = pl.program_id(0)
        last = i == nt - 1

        @pl.when(i == 0)
        def _():
            dcw_ref[...] = jnp.zeros_like(dcw_ref)
            dcb_ref[...] = jnp.zeros_like(dcb_ref)

        def rows(m_ref, a_ref, cols):
            return jnp.concatenate([m_ref[:, cols], a_ref[:, cols]], axis=0).astype(F32)

        def back(d, cols):
            cw = cw_ref[:, cols]
            d1, d2 = _shift_up(d, 1), _shift_up(d, 2)
            dx_ref[:, cols] = ((d * cw[2:3] + d1 * cw[1:2]) + d2 * cw[0:1])[:ts].astype(BF16)
            x = x_ref[:, cols].astype(F32)
            dcb_ref[:, cols] += jnp.sum(d[:ts], axis=0, keepdims=True)
            dcw_ref[0:1, cols] += jnp.sum(d2[:ts] * x, axis=0, keepdims=True)
            dcw_ref[1:2, cols] += jnp.sum(d1[:ts] * x, axis=0, keepdims=True)
            dcw_ref[2:3, cols] += jnp.sum(d[:ts] * x, axis=0, keepdims=True)

        for c in range(nc):
            cols = slice(c * FFN_TC, (c + 1) * FFN_TC)
            ug, uu = rows(gm_ref, ga_ref, cols), rows(um_ref, ua_ref, cols)
            dae = rows(dm_ref, da_ref, cols)
            dae = jnp.where(last & (lax.broadcasted_iota(jnp.int32, dae.shape, 0) >= ts), 0.0, dae)
            sg = 1.0 / (1.0 + jnp.exp(-ug))
            duu = dae * (ug * sg)
            dug = (dae * uu) * (sg * (1.0 + ug * (1.0 - sg)))
            back(dug, cols)
            back(duu, slice(D_FF + c * FFN_TC, D_FF + (c + 1) * FFN_TC))

    hb = ts // FFN_HALO
    nh = S // FFN_HALO
    after_map = lambda i: (jnp.minimum((i + 1) * hb, nh - 1), 0)
    main = pl.BlockSpec((ts, D_FF), lambda i: (i, 0))
    after = pl.BlockSpec((FFN_HALO, D_FF), after_map)
    return pl.pallas_call(
        body, name=name, grid=(nt,),
        in_specs=[pl.BlockSpec((ts, W), lambda i: (i, 0)), main, after, main, after, main, after,
                  pl.BlockSpec((3, W), lambda i: (0, 0))],
        out_specs=[pl.BlockSpec((ts, W), lambda i: (i, 0)), pl.BlockSpec((3, W), lambda i: (0, 0)),
                   pl.BlockSpec((1, W), lambda i: (0, 0))],
        out_shape=[_sds((S, W), BF16), _sds((3, W), F32), _sds((1, W), F32)],
        compiler_params=_params("arbitrary"),
    )(up, u[0], u[0], u[1], u[1], da, da, conv_w)


def _head_mean_matrix():
    h = np.arange(D_MODEL) // HEAD_DIM
    return jnp.asarray((h[:, None] == h[None, :]).astype(np.float32) / HEAD_DIM, dtype=BF16)


def _head_mean(v, gm):
    return jnp.dot(v.astype(BF16), gm, preferred_element_type=F32)


def qknorm_fwd(qkv, gqk, name, ts=512):
    S = qkv.shape[0]

    def body(x_ref, g_ref, gm_ref, o_ref):
        part = pl.program_id(0)
        x = x_ref[...]

        @pl.when(part < 2)
        def _():
            r = lax.rsqrt(_head_mean(x * x, gm_ref[...]) + EPS)
            o_ref[...] = ((x * r) * g_ref[...]).astype(BF16)

        @pl.when(part == 2)
        def _():
            o_ref[...] = x.astype(BF16)

    return pl.pallas_call(
        body, name=name, grid=(3, S // ts),
        in_specs=[pl.BlockSpec((ts, D_MODEL), lambda p, i: (i, p)), pl.BlockSpec((None, 1, D_MODEL), lambda p, i: (p, 0, 0)),
                  pl.BlockSpec((D_MODEL, D_MODEL), lambda p, i: (0, 0))],
        out_specs=pl.BlockSpec((ts, D_MODEL), lambda p, i: (i, p)),
        out_shape=_sds((S, 3 * D_MODEL), BF16), compiler_params=_params("parallel", "parallel"),
    )(qkv, gqk, _head_mean_matrix())


def qknorm_bwd(qkv, dq, dk, dv, gqk, name, ts=256):
    S = qkv.shape[0]

    def body(x_ref, dq_ref, dk_ref, dv_ref, g_ref, gm_ref, o_ref, dg_ref):
        @pl.when(pl.program_id(0) == 0)
        def _():
            dg_ref[...] = jnp.zeros_like(dg_ref)

        gm = gm_ref[...]
        for part, d_ref in enumerate((dq_ref, dk_ref)):
            cols = slice(part * D_MODEL, (part + 1) * D_MODEL)
            x = x_ref[:, cols]
            d = d_ref[...]
            r = lax.rsqrt(_head_mean(x * x, gm) + EPS)
            gx = d * g_ref[part]
            o_ref[:, cols] = (r * gx - x * ((r * r * r) * _head_mean(gx * x, gm))).astype(BF16)
            dg_ref[part] += jnp.sum(d * (x * r), axis=0, keepdims=True)
        o_ref[:, 2 * D_MODEL:] = dv_ref[...].astype(BF16)

    row = pl.BlockSpec((ts, D_MODEL), lambda i: (i, 0))
    wide = pl.BlockSpec((ts, 3 * D_MODEL), lambda i: (i, 0))
    gains = pl.BlockSpec((3, 1, D_MODEL), lambda i: (0, 0, 0))
    return pl.pallas_call(
        body, name=name, grid=(S // ts,),
        in_specs=[wide, row, row, row, gains, pl.BlockSpec((D_MODEL, D_MODEL), lambda i: (0, 0))],
        out_specs=[wide, gains],
        out_shape=[_sds((S, 3 * D_MODEL), BF16), _sds((3, 1, D_MODEL), F32)],
        compiler_params=_params("arbitrary"),
    )(qkv, dq, dk, dv, gqk, _head_mean_matrix())


RESIDUES = 16


def _block_order(dil):
    runs = RESIDUES // dil
    slot = np.arange(ATT_BLOCK)
    return (slot % (ATT_BLOCK // runs)) * runs + slot // (ATT_BLOCK // runs)


def _bucket_tables():
    n = ATT_BLOCK
    max_exact = N_REL_BUCKETS // 2
    buckets, valids = [], []
    for _, dil in DILATED_PAIRS:
        order = _block_order(dil)
        a = order[:, None]
        c = np.concatenate([order, n + order])[None, :]
        first_half = (np.arange(2 * n) < n)[None, :]
        rel = a + n - c
        band = (rel >= 0) & (rel <= n)
        dist = np.clip(rel, 0, n) * dil
        dd = np.maximum(dist, 1).astype(np.float32)
        large = max_exact + (np.log(dd / np.float32(max_exact)) / np.float32(math.log(REL_MAX_DISTANCE / max_exact))
                             * np.float32(N_REL_BUCKETS - max_exact)).astype(np.int32)
        large = np.minimum(large, N_REL_BUCKETS - 1)
        buckets.append(np.where(dist < max_exact, dist, large).reshape(1, -1))
        valids.append(np.stack([(band & ~first_half).reshape(1, -1), band.reshape(1, -1)]))
    return np.stack(buckets).astype(np.int32), np.stack(valids).astype(np.int32)


BIAS_CHUNK = 8192


def _split3(x):
    a = x.astype(BF16)
    r = x - a.astype(F32)
    b = r.astype(BF16)
    c = (r - b.astype(F32)).astype(BF16)
    return a, b, c


def bias_expand(rel_bias_t, name):
    bucket, valid = _bucket_tables()
    nq = bucket.shape[-1]

    def body(t_ref, b_ref, v_ref, o_ref):
        onehot = (lax.broadcasted_iota(jnp.int32, (N_REL_BUCKETS, BIAS_CHUNK), 0) == b_ref[...]).astype(BF16)
        acc = None
        for term in _split3(t_ref[...]):
            p = jnp.dot(term, onehot, preferred_element_type=F32)
            acc = p if acc is None else acc + p
        o_ref[...] = jnp.where(v_ref[...] > 0, acc, MASK_VALUE)

    return pl.pallas_call(
        body, name=name, grid=(3, 2, nq // BIAS_CHUNK),
        in_specs=[pl.BlockSpec((N_HEADS, N_REL_BUCKETS), lambda b, v, c: (0, 0)),
                  pl.BlockSpec((None, 1, BIAS_CHUNK), lambda b, v, c: (b, 0, c)),
                  pl.BlockSpec((None, None, 1, BIAS_CHUNK), lambda b, v, c: (b, v, 0, c))],
        out_specs=pl.BlockSpec((None, None, N_HEADS, BIAS_CHUNK), lambda b, v, c: (b, v, 0, c)),
        out_shape=_sds((3, 2, N_HEADS, nq), F32), compiler_params=_params("parallel", "parallel", "parallel"),
    )(rel_bias_t, jnp.asarray(bucket), jnp.asarray(valid))


def bias_reduce(dbias, name):
    bucket, _ = _bucket_tables()
    nq = bucket.shape[-1]
    dims = (((1,), (1,)), ((), ()))

    def body(d_ref, b_ref, o_ref):
        onehot = (lax.broadcasted_iota(jnp.int32, (N_REL_BUCKETS, BIAS_CHUNK), 0) == b_ref[...]).astype(BF16)
        acc = None
        for term in _split3(d_ref[...]):
            p = lax.dot_general(term, onehot, dims, preferred_element_type=F32)
            acc = p if acc is None else acc + p

        @pl.when(pl.program_id(1) == 0)
        def _():
            o_ref[...] = acc

        @pl.when(pl.program_id(1) > 0)
        def _():
            o_ref[...] += acc

    return pl.pallas_call(
        body, name=name, grid=(3, nq // BIAS_CHUNK),
        in_specs=[pl.BlockSpec((None, N_HEADS, BIAS_CHUNK), lambda b, c: (b, 0, c)),
                  pl.BlockSpec((None, 1, BIAS_CHUNK), lambda b, c: (b, 0, c))],
        out_specs=pl.BlockSpec((None, N_HEADS, N_REL_BUCKETS), lambda b, c: (b, 0, 0)),
        out_shape=_sds((3, N_HEADS, N_REL_BUCKETS), F32), compiler_params=_params("parallel", "arbitrary"),
    )(dbias, jnp.asarray(bucket))


PAIR = 2 * HEAD_DIM
N_PAIRS = N_HEADS // 2
_NT = (((1,), (1,)), ((), ()))
_TN = (((0,), (0,)), ((), ()))


def _low_lanes(shape):
    return lax.broadcasted_iota(jnp.int32, shape, 1) < HEAD_DIM


ATTN_VMEM_LIMIT_BYTES = 56 * 1024 * 1024
BRANCH_ORDER = (2, 1, 0)


def _regroup(dst, src, L16):
    for r in range(RESIDUES):
        dst[pl.ds(r * L16, L16), :] = src[pl.ds(r, L16, stride=RESIDUES), :]


def _ungroup(dst, src, L16):
    for r in range(RESIDUES):
        dst[pl.ds(r, L16, stride=RESIDUES), :] = src[pl.ds(r * L16, L16), :]


def _branch_geometry(branch, S):
    dil = DILATED_PAIRS[branch][1]
    runs = RESIDUES // dil
    return dil, runs, ATT_BLOCK // runs, S // dil // ATT_BLOCK


def _block_rows(it, branch, S):
    dil, runs, run_len, n_blocks = _branch_geometry(branch, S)
    L16 = S // RESIDUES
    r, b = it // n_blocks, it % n_blocks
    prev = jnp.maximum(b - 1, 0)
    cur_rows = [pl.multiple_of((j * dil + r) * L16 + run_len * b, 8) for j in range(runs)]
    prev_rows = [pl.multiple_of((j * dil + r) * L16 + run_len * prev, 8) for j in range(runs)]
    return cur_rows, prev_rows, jnp.minimum(b, 1)


def _load_block(ref, rows, run_len):
    parts = [ref[pl.ds(o, run_len), :] for o in rows]
    return parts[0] if len(parts) == 1 else jnp.concatenate(parts, axis=0)


def _store_block(ref, rows, run_len, value, add=False):
    for j, o in enumerate(rows):
        part = value[j * run_len:(j + 1) * run_len]
        if add:
            ref[pl.ds(o, run_len), :] += part
        else:
            ref[pl.ds(o, run_len), :] = part


ATTN_FWD_UNROLL = 8
ATTN_BWD_UNROLL = 4


def _stack_heads(x, low):
    zero = jnp.zeros_like(x)
    return jnp.concatenate([jnp.where(low, x, zero), jnp.where(low, zero, x)], axis=0)


def _unstack_heads(y, low):
    return jnp.where(low, y[:ATT_BLOCK], y[ATT_BLOCK:])


def attn_fwd(qkvn, bias, name):
    S = qkvn.shape[0]
    L16 = S // RESIDUES
    n_iter = S // ATT_BLOCK

    def body(q_ref, k_ref, v_ref, b_ref, o_ref, lse_ref, stage, qp, kp, vp, acc_s, m_s, l_s):
        for src, dst in ((q_ref, qp), (k_ref, kp), (v_ref, vp)):
            stage[...] = src[...].astype(F32)
            _regroup(dst, stage, L16)
        low = _low_lanes((ATT_BLOCK, PAIR))

        for branch in BRANCH_ORDER:
            _, _, run_len, _ = _branch_geometry(branch, S)
            first = branch == BRANCH_ORDER[0]

            def step(it, carry, branch=branch, run_len=run_len, first=first):
                cur, prev, variant = _block_rows(it, branch, S)
                q = _load_block(qp, cur, run_len).astype(BF16)
                k = jnp.concatenate([_load_block(kp, prev, run_len), _load_block(kp, cur, run_len)], axis=0).astype(BF16)
                v = jnp.concatenate([_load_block(vp, prev, run_len), _load_block(vp, cur, run_len)], axis=0).astype(BF16)
                s = lax.dot_general(_stack_heads(q, low), k, _NT, preferred_element_type=F32) * (HEAD_DIM ** -0.5)
                s = s + b_ref[2 * branch + variant].reshape(2 * ATT_BLOCK, 2 * ATT_BLOCK)
                mx = jnp.max(s, axis=-1, keepdims=True)
                p = jnp.exp(s - mx)
                den = jnp.sum(p, axis=-1, keepdims=True)
                pv = jnp.dot(p.astype(BF16), v, preferred_element_type=F32)
                acc = _unstack_heads(pv, low)
                m = _unstack_heads(mx, low)
                l = _unstack_heads(den, low)
                if not first:
                    m_old = _load_block(m_s, cur, run_len)
                    m_new = jnp.maximum(m_old, m)
                    a_old, a_new = jnp.exp(m_old - m_new), jnp.exp(m - m_new)
                    acc = _load_block(acc_s, cur, run_len) * a_old + acc * a_new
                    l = _load_block(l_s, cur, run_len) * a_old + l * a_new
                    m = m_new
                _store_block(acc_s, cur, run_len, acc)
                _store_block(m_s, cur, run_len, m)
                _store_block(l_s, cur, run_len, l)
                return carry

            lax.fori_loop(0, n_iter, step, 0, unroll=ATTN_FWD_UNROLL)

        acc_s[...] = acc_s[...] / l_s[...]
        _ungroup(stage, acc_s, L16)
        o_ref[...] = stage[...].astype(BF16)
        m_s[...] = m_s[...] + jnp.log(l_s[...])
        _ungroup(lse_ref, m_s, L16)

    col = lambda part: pl.BlockSpec((S, PAIR), lambda hp: (0, part * N_PAIRS + hp))
    out = pl.BlockSpec((S, PAIR), lambda hp: (0, hp))
    return pl.pallas_call(
        body, name=name, grid=(N_PAIRS,),
        in_specs=[col(0), col(1), col(2), pl.BlockSpec((6, 2, ATT_BLOCK, 2 * ATT_BLOCK), lambda hp: (0, hp, 0, 0))],
        out_specs=[out, out], out_shape=[_sds((S, D_MODEL), BF16), _sds((S, D_MODEL), F32)],
        scratch_shapes=[pltpu.VMEM((S, PAIR), F32)] * 7,
        compiler_params=pltpu.CompilerParams(dimension_semantics=("parallel",), vmem_limit_bytes=ATTN_VMEM_LIMIT_BYTES),
    )(qkvn, qkvn, qkvn, bias)


def attn_bwd(qkvn, att, datt, lse, bias, name):
    S = qkvn.shape[0]
    L16 = S // RESIDUES
    n_iter = S // ATT_BLOCK
    TILE = 512

    def body(q_ref, k_ref, v_ref, o_ref, do_ref, lse_ref, b_ref, dq_ref, dk_ref, dv_ref, db_ref,
             qp, kp, vp, dop, ldp, dqp, dkp, dvp):
        stage = dqp
        for src, dst in ((q_ref, qp), (k_ref, kp), (v_ref, vp), (do_ref, dop)):
            stage[...] = src[...].astype(F32)
            _regroup(dst, stage, L16)

        def pack(i, carry):
            rows = pl.ds(pl.multiple_of(i * TILE, TILE), TILE)
            low = _low_lanes((TILE, PAIR))
            lane = lax.broadcasted_iota(jnp.int32, (TILE, PAIR), 1)
            prod = do_ref[rows, :].astype(F32) * o_ref[rows, :].astype(F32)
            d0 = jnp.sum(jnp.where(low, prod, 0.0), axis=-1, keepdims=True)
            d1 = jnp.sum(jnp.where(low, 0.0, prod), axis=-1, keepdims=True)
            stage[rows, :] = jnp.where((lane & (HEAD_DIM // 2)) == 0, lse_ref[rows, :], jnp.where(low, d0, d1))
            return carry

        lax.fori_loop(0, S // TILE, pack, 0)
        _regroup(ldp, stage, L16)
        dqp[...] = jnp.zeros_like(dqp)
        dkp[...] = jnp.zeros_like(dkp)
        dvp[...] = jnp.zeros_like(dvp)
        db_ref[...] = jnp.zeros_like(db_ref)
        low = _low_lanes((ATT_BLOCK, PAIR))

        for branch in BRANCH_ORDER:
            _, _, run_len, _ = _branch_geometry(branch, S)

            def step(it, carry, branch=branch, run_len=run_len):
                cur, prev, variant = _block_rows(it, branch, S)
                q = _load_block(qp, cur, run_len).astype(BF16)
                dout = _load_block(dop, cur, run_len).astype(BF16)
                ld = _load_block(ldp, cur, run_len)
                k = jnp.concatenate([_load_block(kp, prev, run_len), _load_block(kp, cur, run_len)], axis=0).astype(BF16)
                v = jnp.concatenate([_load_block(vp, prev, run_len), _load_block(vp, cur, run_len)], axis=0).astype(BF16)
                half = HEAD_DIM // 2
                lse2 = jnp.concatenate([ld[:, 0:1], ld[:, HEAD_DIM:HEAD_DIM + 1]], axis=0)
                delta2 = jnp.concatenate([ld[:, half:half + 1], ld[:, HEAD_DIM + half:HEAD_DIM + half + 1]], axis=0)
                q2, do2 = _stack_heads(q, low), _stack_heads(dout, low)
                s = lax.dot_general(q2, k, _NT, preferred_element_type=F32) * (HEAD_DIM ** -0.5)
                p = jnp.exp(s + b_ref[2 * branch + variant].reshape(2 * ATT_BLOCK, 2 * ATT_BLOCK) - lse2)
                dp = lax.dot_general(do2, v, _NT, preferred_element_type=F32)
                ds = p * (dp - delta2)
                db_ref[branch] += ds.reshape(2, ATT_BLOCK, 2 * ATT_BLOCK)
                dsb = (ds * (HEAD_DIM ** -0.5)).astype(BF16)
                dq = _unstack_heads(jnp.dot(dsb, k, preferred_element_type=F32), low)
                dk = lax.dot_general(dsb, q2, _TN, preferred_element_type=F32)
                dv = lax.dot_general(p.astype(BF16), do2, _TN, preferred_element_type=F32)
                _store_block(dqp, cur, run_len, dq, add=True)
                _store_block(dkp, prev, run_len, dk[:ATT_BLOCK], add=True)
                _store_block(dvp, prev, run_len, dv[:ATT_BLOCK], add=True)
                _store_block(dkp, cur, run_len, dk[ATT_BLOCK:], add=True)
                _store_block(dvp, cur, run_len, dv[ATT_BLOCK:], add=True)
                return carry

            lax.fori_loop(0, n_iter, step, 0, unroll=ATTN_BWD_UNROLL)

        _ungroup(dq_ref, dqp, L16)
        _ungroup(dk_ref, dkp, L16)
        _ungroup(dv_ref, dvp, L16)

    col = lambda part: pl.BlockSpec((S, PAIR), lambda hp: (0, part * N_PAIRS + hp))
    one = pl.BlockSpec((S, PAIR), lambda hp: (0, hp))
    return pl.pallas_call(
        body, name=name, grid=(N_PAIRS,),
        in_specs=[col(0), col(1), col(2), one, one, one,
                  pl.BlockSpec((6, 2, ATT_BLOCK, 2 * ATT_BLOCK), lambda hp: (0, hp, 0, 0))],
        out_specs=[one, one, one, pl.BlockSpec((3, 2, ATT_BLOCK, 2 * ATT_BLOCK), lambda hp: (0, hp, 0, 0))],
        out_shape=[_sds((S, D_MODEL), F32)] * 3 + [_sds((3, N_HEADS, ATT_BLOCK, 2 * ATT_BLOCK), F32)],
        scratch_shapes=[pltpu.VMEM((S, PAIR), F32)] * 8,
        compiler_params=pltpu.CompilerParams(dimension_semantics=("parallel",), vmem_limit_bytes=ATTN_VMEM_LIMIT_BYTES),
    )(qkvn, qkvn, qkvn, att, datt, lse, bias)


def adamw(w, g, m, v, name):
    n, R, C = w.shape

    def body(w_ref, g_ref, m_ref, v_ref, d_ref, nm_ref, nv_ref, go_ref):
        gv = g_ref[...]
        go_ref[...] = gv
        m2 = ADAM_B1 * m_ref[...] + (1.0 - ADAM_B1) * gv
        v2 = ADAM_B2 * v_ref[...] + (1.0 - ADAM_B2) * (gv * gv)
        m_hat = m2 / (1.0 - ADAM_B1 ** ADAM_STEP)
        v_hat = v2 / (1.0 - ADAM_B2 ** ADAM_STEP)
        d_ref[...] = -ADAM_LR * (m_hat / (jnp.sqrt(v_hat) + ADAM_EPS) + ADAM_WD * w_ref[...])
        nm_ref[...] = m2
        nv_ref[...] = v2

    tr = R
    while tr * C * 4 > (1 << 21) and tr % 16 == 0:
        tr //= 2
    spec = pl.BlockSpec((None, tr, C), lambda i, r: (i, r, 0))
    return pl.pallas_call(
        body, name=name, grid=(n, R // tr), in_specs=[spec] * 4, out_specs=[spec] * 4,
        out_shape=[_sds((n, R, C), F32)] * 4, compiler_params=_params("parallel", "parallel"),
    )(w, g, m, v)


ANY = pl.BlockSpec(memory_space=pl.ANY)


def _coords():
    return lax.axis_index("x"), lax.axis_index("y"), lax.axis_index("c")


def _other_chips(mx, my):
    return [(1 - mx, my), (mx, 1 - my), (1 - mx, 1 - my)]


def _remote(src, dst, send, recv, dev):
    return pltpu.make_async_remote_copy(src_ref=src, dst_ref=dst, send_sem=send, recv_sem=recv, device_id=dev,
                                        device_id_type=MESH)


def allgather_devices(x, name):
    R, C = x.shape

    def body(x_ref, o_ref, send, recv, local_sem):
        mx, my, mc = _coords()
        me = 4 * mx + 2 * my + mc
        local = pltpu.make_async_copy(x_ref, o_ref.at[me], local_sem)
        local.start()
        peers = []
        for k in range(1, N_DEV):
            px = 1 - mx if k & 4 else mx
            py = 1 - my if k & 2 else my
            pc = 1 - mc if k & 1 else mc
            peers.append((px, py, pc))
        sends = [_remote(x_ref, o_ref.at[me], send.at[k], recv.at[k], p) for k, p in enumerate(peers)]
        for cp in sends:
            cp.start()
        for k, (px, py, pc) in enumerate(peers):
            _remote(x_ref, o_ref.at[4 * px + 2 * py + pc], send.at[k], recv.at[k], (px, py, pc)).wait_recv()
        for cp in sends:
            cp.wait_send()
        local.wait()

    return pl.pallas_call(
        body, name=name, in_specs=[ANY], out_specs=ANY, out_shape=_sds((N_DEV, R, C), x.dtype),
        scratch_shapes=[pltpu.SemaphoreType.DMA((N_DEV - 1,)), pltpu.SemaphoreType.DMA((N_DEV - 1,)),
                        pltpu.SemaphoreType.DMA],
    )(x)


HBM = pl.BlockSpec(memory_space=pltpu.HBM)
SEM = pl.BlockSpec(memory_space=pltpu.SEMAPHORE)
_SPLIT_COPY = pltpu.CompilerParams(has_side_effects=pltpu.SideEffectType.DATAFLOW_SIDE_EFFECTING)


def _in_hbm(a):
    return pltpu.with_memory_space_constraint(a, pltpu.HBM)


def cast_into_slot(w, layer, chip_core, name):
    _, _, hR, C = w.shape

    def body(s_ref, w_ref, o_ref):
        del s_ref
        o_ref[...] = w_ref[...].astype(BF16)

    grid_spec = pltpu.PrefetchScalarGridSpec(
        num_scalar_prefetch=1, grid=(2,),
        in_specs=[pl.BlockSpec((None, None, hR, C), lambda h, s: (layer, h, 0, 0))],
        out_specs=pl.BlockSpec((None, None, hR, C), lambda h, s: (s[0], h, 0, 0)))
    return pl.pallas_call(body, name=name, grid_spec=grid_spec, out_shape=_sds((N_CHIPS, 2, hR, C), BF16),
                          compiler_params=_params("parallel"))(chip_core, w)


def gather_start(lands, groups, name):
    n = len(lands)
    n_groups = len(groups)

    def body(*refs):
        ins = refs[:n]
        sems = refs[n:n + 2 * n_groups]
        token = refs[-1]
        mx, my, mc = _coords()
        chip = 2 * mx + my
        for g, members in enumerate(groups):
            send, recv = sems[2 * g], sems[2 * g + 1]
            for i, a in enumerate(members):
                mine = ins[a].at[chip, mc]
                for k, (px, py) in enumerate(_other_chips(mx, my)):
                    _remote(mine, mine, send.at[3 * i + k], recv.at[3 * i + k], (px, py, mc)).start()
        token[...] = jnp.zeros_like(token)

    sem_shapes = []
    for members in groups:
        sem_shapes += [pltpu.SemaphoreType.DMA((3 * len(members),))] * 2
    outs = pl.pallas_call(
        body, name=name, in_specs=[HBM] * n,
        out_specs=[SEM] * (2 * n_groups) + [HBM] * n + [pl.BlockSpec(memory_space=pltpu.VMEM)],
        out_shape=sem_shapes + [pltpu.HBM(a.shape, a.dtype) for a in lands] + [_sds((SUBLANES, LANES), F32)],
        input_output_aliases={a: 2 * n_groups + a for a in range(n)}, compiler_params=_SPLIT_COPY,
    )(*[_in_hbm(a) for a in lands])
    sems = [(outs[2 * g], outs[2 * g + 1]) for g in range(n_groups)]
    return sems, list(outs[2 * n_groups:2 * n_groups + n]), outs[-1]


def gather_forward(lands, sems, after, name):
    n = len(lands)

    def body(*refs):
        ins = refs[:n]
        send, recv = refs[n], refs[n + 1]
        fsend, frecv = refs[n + 3], refs[n + 4]
        mx, my, mc = _coords()
        for i in range(n):
            for k, (px, py) in enumerate(_other_chips(mx, my)):
                landed = ins[i].at[2 * px + py, mc]
                cp = _remote(landed, landed, send.at[3 * i + k], recv.at[3 * i + k], (px, py, mc))
                cp.wait_send()
                cp.wait_recv()
                _remote(landed, landed, fsend.at[3 * i + k], frecv.at[3 * i + k], (mx, my, 1 - mc)).start()

    outs = pl.pallas_call(
        body, name=name, in_specs=[HBM] * n + [SEM, SEM, ANY], out_specs=[SEM, SEM] + [HBM] * n,
        out_shape=[pltpu.SemaphoreType.DMA((3 * n,))] * 2 + [pltpu.HBM(a.shape, a.dtype) for a in lands],
        input_output_aliases={a: 2 + a for a in range(n)}, compiler_params=_SPLIT_COPY,
    )(*lands, sems[0], sems[1], after)
    return (outs[0], outs[1]), list(outs[2:])


def gather_wait(lands, sems, after, name):
    n = len(lands)

    def body(*refs):
        ins = refs[:n]
        fsend, frecv = refs[n], refs[n + 1]
        mx, my, mc = _coords()
        for i in range(n):
            for k, (px, py) in enumerate(_other_chips(mx, my)):
                theirs = ins[i].at[2 * px + py, 1 - mc]
                cp = _remote(theirs, theirs, fsend.at[3 * i + k], frecv.at[3 * i + k], (mx, my, 1 - mc))
                cp.wait_send()
                cp.wait_recv()

    outs = pl.pallas_call(
        body, name=name, in_specs=[HBM] * n + [SEM, SEM, ANY], out_specs=[HBM] * n,
        out_shape=[pltpu.HBM(a.shape, a.dtype) for a in lands],
        input_output_aliases={a: a for a in range(n)}, compiler_params=_SPLIT_COPY,
    )(*lands, sems[0], sems[1], after)
    return list(outs)


def _peers(mx, my, mc):
    return [(1 - mx if k & 4 else mx, 1 - my if k & 2 else my, 1 - mc if k & 1 else mc) for k in range(1, N_DEV)]


def devices_start(x, name):
    def body(x_ref, land_ref, send, recv, x_thru, land_thru):
        mx, my, mc = _coords()
        me = 4 * mx + 2 * my + mc
        for k, peer in enumerate(_peers(mx, my, mc)):
            _remote(x_ref, land_ref.at[me], send.at[k], recv.at[k], peer).start()

    land = lax.empty((N_DEV,) + x.shape, x.dtype)
    outs = pl.pallas_call(
        body, name=name, in_specs=[HBM, HBM], out_specs=[SEM, SEM, HBM, HBM],
        out_shape=[pltpu.SemaphoreType.DMA((N_DEV - 1,))] * 2 + [pltpu.HBM(x.shape, x.dtype), pltpu.HBM(land.shape, x.dtype)],
        input_output_aliases={0: 2, 1: 3}, compiler_params=_SPLIT_COPY,
    )(_in_hbm(x), _in_hbm(land))
    return (outs[0], outs[1]), outs[2], outs[3]


def devices_wait(x, land, sems, after, name):
    def body(x_ref, land_ref, send, recv, after_ref, x_thru, land_thru):
        mx, my, mc = _coords()
        for k, (px, py, pc) in enumerate(_peers(mx, my, mc)):
            cp = _remote(x_ref, land_ref.at[4 * px + 2 * py + pc], send.at[k], recv.at[k], (px, py, pc))
            cp.wait_send()
            cp.wait_recv()

    outs = pl.pallas_call(
        body, name=name, in_specs=[HBM, HBM, SEM, SEM, ANY], out_specs=[HBM, HBM],
        out_shape=[pltpu.HBM(x.shape, x.dtype), pltpu.HBM(land.shape, land.dtype)],
        input_output_aliases={0: 0, 1: 1}, compiler_params=_SPLIT_COPY,
    )(x, land, sems[0], sems[1], after)
    return outs[0], outs[1]


def device_sum(land, own, me, name):
    _, R, C = land.shape

    def body(s_ref, l_ref, o_ref_in, o_ref):
        acc = None
        for q in range(N_DEV):
            term = jnp.where(s_ref[0] == q, o_ref_in[...], l_ref[q])
            acc = term if acc is None else acc + term
        o_ref[...] = acc

    grid_spec = pltpu.PrefetchScalarGridSpec(
        num_scalar_prefetch=1, grid=(1,),
        in_specs=[pl.BlockSpec((N_DEV, R, C), lambda i, s: (0, 0, 0)), pl.BlockSpec((R, C), lambda i, s: (0, 0))],
        out_specs=pl.BlockSpec((R, C), lambda i, s: (0, 0)))
    return pl.pallas_call(body, name=name, grid_spec=grid_spec, out_shape=_sds((R, C), F32),
                          compiler_params=_params("arbitrary"))(me, land, own)


def reduce_send(grads, name):
    n = len(grads)

    def body(*refs):
        ins, lands = refs[:n], refs[n:2 * n]
        send, recv = refs[2 * n], refs[2 * n + 1]
        mx, my, mc = _coords()
        me = 4 * mx + 2 * my + mc
        for a in range(n):
            for k, (px, py, pc) in enumerate(_peers(mx, my, mc)):
                _remote(ins[a].at[2 * px + py, pc], lands[a].at[me], send.at[7 * a + k], recv.at[7 * a + k], (px, py, pc)).start()

    lands = [lax.empty((N_DEV,) + g.shape[2:], g.dtype) for g in grads]
    outs = pl.pallas_call(
        body, name=name, in_specs=[HBM] * (2 * n), out_specs=[SEM, SEM] + [HBM] * (2 * n),
        out_shape=[pltpu.SemaphoreType.DMA((7 * n,))] * 2 + [pltpu.HBM(a.shape, a.dtype) for a in grads + lands],
        input_output_aliases={a: 2 + a for a in range(2 * n)}, compiler_params=_SPLIT_COPY,
    )(*[_in_hbm(a) for a in grads + lands])
    return (outs[0], outs[1]), list(outs[2:2 + n]), list(outs[2 + n:])


def reduce_wait(grads, lands, sems, after, name):
    n = len(grads)

    def body(*refs):
        ins, zones = refs[:n], refs[n:2 * n]
        send, recv = refs[2 * n], refs[2 * n + 1]
        mx, my, mc = _coords()
        for a in range(n):
            for k, (px, py, pc) in enumerate(_peers(mx, my, mc)):
                cp = _remote(ins[a].at[2 * px + py, pc], zones[a].at[4 * px + 2 * py + pc], send.at[7 * a + k],
                             recv.at[7 * a + k], (px, py, pc))
                cp.wait_send()
                cp.wait_recv()

    outs = pl.pallas_call(
        body, name=name, in_specs=[HBM] * (2 * n) + [SEM, SEM, ANY], out_specs=[HBM] * (2 * n),
        out_shape=[pltpu.HBM(a.shape, a.dtype) for a in grads + lands],
        input_output_aliases={a: a for a in range(2 * n)}, compiler_params=_SPLIT_COPY,
    )(*grads, *lands, sems[0], sems[1], after)
    return list(outs[:n]), list(outs[n:])


def reduce_sum(land, grad, place, name, into=None, layer=None):
    _, hR, C = land.shape
    tr = hR
    while N_DEV * tr * C * 2 > (6 << 20) and tr % 32 == 0:
        tr //= 2

    def body(s_ref, l_ref, g_ref, *rest):
        o_ref = rest[-1]
        own = g_ref[...].astype(F32)
        acc = None
        for q in range(N_DEV):
            term = jnp.where(s_ref[2] == q, own, l_ref[q].astype(F32))
            acc = term if acc is None else acc + term
        o_ref[...] = acc

    in_specs = [pl.BlockSpec((N_DEV, tr, C), lambda i, s: (0, i, 0)),
                pl.BlockSpec((None, None, tr, C), lambda i, s: (s[0], s[1], i, 0))]
    args = [place, land, grad]
    aliases = {}
    if layer is None:
        out_spec = pl.BlockSpec((None, tr, C), lambda i, s: (s[1], i, 0))
        out_shape = _sds((2, hR, C), F32)
    else:
        out_spec = pl.BlockSpec((None, None, tr, C), lambda i, s: (layer, s[1], i, 0))
        out_shape = _sds((2, 2, hR, C), F32)
        if into is not None:
            in_specs.append(ANY)
            args.append(into)
            aliases = {3: 0}
    grid_spec = pltpu.PrefetchScalarGridSpec(num_scalar_prefetch=1, grid=(hR // tr,), in_specs=in_specs, out_specs=out_spec)
    return pl.pallas_call(body, name=name, grid_spec=grid_spec, out_shape=out_shape, input_output_aliases=aliases,
                          compiler_params=_params("arbitrary"))(*args)


def join_halves(arrays, name):
    n = len(arrays)
    pieces = [(a, l) for a, arr in enumerate(arrays) for l in (range(arr.shape[0]) if arr.ndim == 4 else [None])]

    def body(*refs):
        ins = refs[:n]
        send, recv = refs[2 * n:]
        mx, my, mc = _coords()

        def half(a, l, h):
            return ins[a].at[h] if l is None else ins[a].at[l, h]

        sends = [_remote(half(a, l, mc), half(a, l, mc), send.at[i], recv.at[i], (mx, my, 1 - mc))
                 for i, (a, l) in enumerate(pieces)]
        for cp in sends:
            cp.start()
        for i, (a, l) in enumerate(pieces):
            theirs = half(a, l, 1 - mc)
            _remote(theirs, theirs, send.at[i], recv.at[i], (mx, my, 1 - mc)).wait_recv()
        for cp in sends:
            cp.wait_send()

    return pl.pallas_call(
        body, name=name, in_specs=[ANY] * n, out_specs=[ANY] * n, out_shape=[_sds(a.shape, a.dtype) for a in arrays],
        input_output_aliases={a: a for a in range(n)},
        scratch_shapes=[pltpu.SemaphoreType.DMA((len(pieces),)), pltpu.SemaphoreType.DMA((len(pieces),))],
    )(*arrays)


LANES = 128
SUBLANES = 8


def _n_rows(shape):
    rows = -(-int(np.prod(shape)) // LANES)
    return -(-rows // SUBLANES) * SUBLANES


def _as_rows(a):
    flat = a.reshape(-1)
    rows = _n_rows(a.shape)
    return jnp.pad(flat, (0, rows * LANES - flat.shape[0])).reshape(rows, LANES)


def _pack(arrays):
    return jnp.concatenate([_as_rows(a) for a in arrays], axis=0)


def _unpack(rows, shapes):
    out, r0 = [], 0
    for s in shapes:
        n = _n_rows(s)
        out.append(rows[r0:r0 + n].reshape(-1)[:int(np.prod(s))].reshape(s))
        r0 += n
    return out


REPLICATED_SMALL = [("rel_bias", (32, 16)), ("even_norm", (1, 1024)), ("even_pool_w", (1, 4, 128, 128)),
                    ("even_pool_scale", (1, 512)), ("odd_q_norm", (1, 64)), ("odd_k_norm", (1, 64)),
                    ("ffn_norm", (2, 1024)), ("ffn_conv_b", (2, 5632))]
SHARDED_SMALL = [("even_conv_w", (1, 3, 128)), ("odd_norm", (1, 256)), ("ffn_conv_w", (2, 3, 1408))]
BIG = ["even_w_in", "even_w_out", "odd_w_qkv", "odd_w_o", "ffn_w_up", "ffn_w_down"]
WEIGHT_ORDER = ["rel_bias", "even_norm", "even_w_in", "even_conv_w", "even_pool_w", "even_pool_scale", "even_w_out",
                "odd_norm", "odd_w_qkv", "odd_q_norm", "odd_k_norm", "odd_w_o", "ffn_norm", "ffn_w_up", "ffn_conv_w",
                "ffn_conv_b", "ffn_w_down"]


def kernel(x, rel_bias, even_norm, even_w_in, even_conv_w, even_pool_w, even_pool_scale, even_w_out, odd_norm, odd_w_qkv, odd_q_norm, odd_k_norm, odd_w_o, ffn_norm, ffn_w_up, ffn_conv_w, ffn_conv_b, ffn_w_down, loss_target, m_rel_bias, m_even_norm, m_even_w_in, m_even_conv_w, m_even_pool_w, m_even_pool_scale, m_even_w_out, m_odd_norm, m_odd_w_qkv, m_odd_q_norm, m_odd_k_norm, m_odd_w_o, m_ffn_norm, m_ffn_w_up, m_ffn_conv_w, m_ffn_conv_b, m_ffn_w_down, v_rel_bias, v_even_norm, v_even_w_in, v_even_conv_w, v_even_pool_w, v_even_pool_scale, v_even_w_out, v_odd_norm, v_odd_w_qkv, v_odd_q_norm, v_odd_k_norm, v_odd_w_o, v_ffn_norm, v_ffn_w_up, v_ffn_conv_w, v_ffn_conv_b, v_ffn_w_down):
    W = dict(rel_bias=rel_bias, even_norm=even_norm, even_w_in=even_w_in, even_conv_w=even_conv_w, even_pool_w=even_pool_w,
             even_pool_scale=even_pool_scale, even_w_out=even_w_out, odd_norm=odd_norm, odd_w_qkv=odd_w_qkv,
             odd_q_norm=odd_q_norm, odd_k_norm=odd_k_norm, odd_w_o=odd_w_o, ffn_norm=ffn_norm, ffn_w_up=ffn_w_up,
             ffn_conv_w=ffn_conv_w, ffn_conv_b=ffn_conv_b, ffn_w_down=ffn_w_down)
    M1 = dict(rel_bias=m_rel_bias, even_norm=m_even_norm, even_w_in=m_even_w_in, even_conv_w=m_even_conv_w,
              even_pool_w=m_even_pool_w, even_pool_scale=m_even_pool_scale, even_w_out=m_even_w_out, odd_norm=m_odd_norm,
              odd_w_qkv=m_odd_w_qkv, odd_q_norm=m_odd_q_norm, odd_k_norm=m_odd_k_norm, odd_w_o=m_odd_w_o,
              ffn_norm=m_ffn_norm, ffn_w_up=m_ffn_w_up, ffn_conv_w=m_ffn_conv_w, ffn_conv_b=m_ffn_conv_b,
              ffn_w_down=m_ffn_w_down)
    M2 = dict(rel_bias=v_rel_bias, even_norm=v_even_norm, even_w_in=v_even_w_in, even_conv_w=v_even_conv_w,
              even_pool_w=v_even_pool_w, even_pool_scale=v_even_pool_scale, even_w_out=v_even_w_out, odd_norm=v_odd_norm,
              odd_w_qkv=v_odd_w_qkv, odd_q_norm=v_odd_q_norm, odd_k_norm=v_odd_k_norm, odd_w_o=v_odd_w_o,
              ffn_norm=v_ffn_norm, ffn_w_up=v_ffn_w_up, ffn_conv_w=v_ffn_conv_w, ffn_conv_b=v_ffn_conv_b,
              ffn_w_down=v_ffn_w_down)
    mx, my, mc = _coords()
    chip = 2 * mx + my
    me = 4 * mx + 2 * my + mc
    place = jnp.stack([chip, mc, me]).astype(jnp.int32)
    xs, target = x[0], loss_target[0]

    def halves(w):
        return w.reshape((w.shape[0], 2, w.shape[-2] // 2, w.shape[-1]))

    lands = [cast_into_slot(halves(even_w_in), 0, place, "cast_w_in"), cast_into_slot(halves(even_w_out), 0, place, "cast_w_out"),
             cast_into_slot(halves(ffn_w_up), 0, place, "cast_w_up0"), cast_into_slot(halves(ffn_w_down), 0, place, "cast_w_down0"),
             cast_into_slot(halves(odd_w_qkv), 0, place, "cast_w_qkv"), cast_into_slot(halves(odd_w_o), 0, place, "cast_w_o"),
             cast_into_slot(halves(ffn_w_up), 1, place, "cast_w_up1"), cast_into_slot(halves(ffn_w_down), 1, place, "cast_w_down1")]
    small = allgather_devices(_pack([even_conv_w, odd_norm, ffn_conv_w]), "allgather_small_weights")
    lands[0], small = lax.optimization_barrier((lands[0], small))
    gather_sems, lands, token = gather_start(lands, [[0, 1], [2, 3], [4, 5], [6, 7]], "gather_start")
    even_norm_after_start = even_norm + token[0:1, 0:1]
    small = small[0::2]
    conv_w_full = small[:, 0:3].transpose(1, 0, 2).reshape(3, A_WIDTH)
    odd_norm_full = small[:, 8:10].reshape(1, D_MODEL)
    ffn_cw_full = small[:, 16:82].reshape(N_CHIPS, 2, 3, 2 * D_FF // N_CHIPS).transpose(1, 2, 0, 3).reshape(2, 3, 2 * D_FF)
    pool_w = cast_bf16(even_pool_w[0], "cast_pool_w")
    gqk = jnp.stack([jnp.tile(odd_q_norm[0], N_HEADS), jnp.tile(odd_k_norm[0], N_HEADS),
                     jnp.ones((D_MODEL,), F32)])[:, None, :]
    bias = bias_expand(rel_bias.T, "bias_expand").reshape(6, N_HEADS, ATT_BLOCK, 2 * ATT_BLOCK)

    def ffn_fwd(l, xin, xn):
        up = mm_nn(xn, w_up[l], f"ffn{l}_up", out_dtype=BF16)
        u, act = glu_fwd(up, ffn_cw_full[l], ffn_conv_b[l:l + 1], f"ffn{l}_glu")
        return act, (xin, xn, up, u, act)

    def gathered(group, tag, after_landing, after_passing):
        sems, arrays = gather_forward(lands[2 * group:2 * group + 2], gather_sems[group], after_landing, "gather_forward_" + tag)
        return gather_wait(arrays, sems, after_passing, "gather_wait_" + tag)

    def ffn_weights(got):
        return got[0].reshape(N_CHIPS, 1, D_MODEL, 2 * D_FF // N_CHIPS), got[1].reshape(1, 1, D_FF, D_MODEL)

    w_up, w_down = [None, None], [None, None]
    xn0 = rmsnorm_fwd(xs, even_norm_after_start, "even_norm")
    got = gathered(0, "even", bias, xn0)
    w_in = got[0].reshape(N_CHIPS, 1, D_MODEL, EVEN_IN // N_CHIPS)
    w_out = got[1].reshape(1, 1, D_MODEL, D_MODEL)
    proj = mm_nn(xn0, w_in, "even_in")
    mix = mixer_fwd(proj, conv_w_full, pool_w, even_pool_scale, "even_mixer")
    x1, xn1 = mm_res_norm(mix, w_out, xs, ffn_norm[0:1], "even_out")
    w_up[0], w_down[0] = ffn_weights(gathered(1, "ffn0", proj, x1))
    act0, ffn0 = ffn_fwd(0, x1, xn1)
    x2, xn2 = mm_res_norm(act0, w_down[0], x1, odd_norm_full, "ffn0_down")
    got = gathered(2, "odd", x1, x2)
    w_qkv = got[0].reshape(N_CHIPS, 1, D_MODEL, 3 * D_MODEL // N_CHIPS)
    w_o = got[1].reshape(1, 1, D_MODEL, D_MODEL)
    qkv = mm_nn(xn2, w_qkv, "odd_qkv")
    qkvn = qknorm_fwd(qkv, gqk, "odd_qknorm")
    att, lse = attn_fwd(qkvn, bias, "attn_fwd")
    x3, xn3 = mm_res_norm(att, w_o, x2, ffn_norm[1:2], "odd_out")
    w_up[1], w_down[1] = ffn_weights(gathered(3, "ffn1", x2, x3))
    act1, ffn1 = ffn_fwd(1, x3, xn3)
    dy, dyb, sq = mm_res_loss(act1, w_down[1], x3, target, "ffn1_down_loss")
    loss = lax.psum(0.5 * jnp.sum(sq) * (1.0 / D_MODEL), ("x", "y", "c"))

    def ffn_bwd(l, dy, dyb, saved):
        xin, xn, up, u, act = saved
        dw_down = mm_tn(act, dyb, f"ffn{l}_dw_down", J=1, tk=D_FF // 2, tm=1024)
        dact = mm_nt(dyb, w_down[l], f"ffn{l}_dact", tr=D_FF // 2, out_dtype=BF16, tm=1024)
        dup, dcw, dcb = glu_bwd(up, u, dact, ffn_cw_full[l], f"ffn{l}_glu_bwd")
        dw_up = mm_tn(xn, dup, f"ffn{l}_dw_up", J=N_CHIPS, tk=512, tm=1024, jb=2)
        dx, dxb, dg = mm_nt_norm_bwd(dup, w_up[l], xin, ffn_norm[l:l + 1], dy, f"ffn{l}_dx")
        return dx, dxb, (dw_down, dw_up, dcw, dcb, dg)

    def quarters(g):
        return g.reshape(N_CHIPS, 2, g.shape[0] * g.shape[1] // (2 * N_CHIPS), g.shape[-1])

    def reduce_start(grads, tag, then):
        sems, parts, zones = reduce_send([quarters(g) for g in grads], "reduce_send_" + tag)
        then, parts = lax.optimization_barrier((then, parts))
        return (sems, parts, zones), then

    dx3, dx3b, g_ffn1 = ffn_bwd(1, dy, dyb, ffn1)
    red_ffn1, (dx3, dx3b) = reduce_start([g_ffn1[1], g_ffn1[0]], "ffn1", (dx3, dx3b))
    dw_o = mm_tn(att, dx3b, "odd_dw_o", J=1, tk=512, tm=1024)
    datt = mm_nt(dx3b, w_o, "odd_datt", tr=D_MODEL, out_dtype=BF16)
    dq, dk, dv, dbias = attn_bwd(qkvn, att, datt, lse, bias, "attn_bwd")
    dqkv, dgqk = qknorm_bwd(qkv, dq, dk, dv, gqk, "odd_qknorm_bwd")
    dw_qkv = mm_tn(xn2, dqkv, "odd_dw_qkv", J=N_CHIPS, tk=512, tm=1024)
    red_odd, dqkv = reduce_start([dw_qkv, dw_o], "odd", dqkv)
    dx2, dx2b, dg_odd = mm_nt_norm_bwd(dqkv, w_qkv, x2, odd_norm_full, dx3, "odd_dx")
    dx1, dx1b, g_ffn0 = ffn_bwd(0, dx2, dx2b, ffn0)
    red_ffn0, (dx1, dx1b) = reduce_start([g_ffn0[1], g_ffn0[0]], "ffn0", (dx1, dx1b))
    dw_out = mm_tn(mix, dx1b, "even_dw_out", J=1, tk=512, tm=1024)
    dmix = mm_nt(dx1b, w_out, "even_dmix", tr=D_MODEL)
    dproj, dcw_even, dpw, dps = mixer_bwd(proj, dmix, conv_w_full, pool_w, even_pool_scale, "even_mixer_bwd")
    dw_in = mm_tn(xn0, dproj, "even_dw_in", J=N_CHIPS, tk=512, tm=1024)
    grad_x, _, dg_even = mm_nt_norm_bwd(dproj, w_in, xs, even_norm, dx1, "even_dx")
    d_rel = jnp.sum(bias_reduce(dbias.reshape(3, N_HEADS, 2 * ATT_BLOCK * ATT_BLOCK), "bias_reduce"), axis=0).T

    red_even, grad_x = reduce_start([dw_in, dw_out], "even", grad_x)

    dcw_sh = dcw_even.reshape(3, N_CHIPS, A_WIDTH // N_CHIPS).transpose(1, 0, 2)
    don_sh = dg_odd.reshape(N_CHIPS, D_MODEL // N_CHIPS)
    dfcw = jnp.stack([g_ffn0[2], g_ffn1[2]])
    dfcw_sh = dfcw.reshape(2, 3, N_CHIPS, 2 * D_FF // N_CHIPS).transpose(2, 0, 1, 3)
    rep_grads = [d_rel, dg_even, dpw[None], dps, _head_sum(dgqk[0]), _head_sum(dgqk[1]),
                 jnp.concatenate([g_ffn0[4], g_ffn1[4]], axis=0), jnp.concatenate([g_ffn0[3], g_ffn1[3]], axis=0)]
    rep_rows = _pack(rep_grads)
    shard_rows = jnp.concatenate([_pack([dcw_sh[j], don_sh[j], dfcw_sh[j]]) for j in range(N_CHIPS)], axis=0)
    n_rep, n_shard = rep_rows.shape[0], shard_rows.shape[0] // N_CHIPS
    small_sems, small_rows, small_land = devices_start(jnp.concatenate([rep_rows, shard_rows], axis=0), "small_grads_start")
    grad_x, small_rows = lax.optimization_barrier((grad_x, small_rows))

    def reduce_end(red, tag, after):
        sems, parts, zones = red
        parts, zones = reduce_wait(parts, zones, sems, after, "reduce_wait_" + tag)
        return zones, parts

    z_ffn1, p_ffn1 = reduce_end(red_ffn1, "ffn1", grad_x)
    z_odd, p_odd = reduce_end(red_odd, "odd", grad_x)
    r_qkv = reduce_sum(z_odd[0], p_odd[0], place, "reduce_sum_w_qkv")
    r_o = reduce_sum(z_odd[1], p_odd[1], place, "reduce_sum_w_o")
    r_up = reduce_sum(z_ffn1[0], p_ffn1[0], place, "reduce_sum_w_up1", layer=1)
    r_down = reduce_sum(z_ffn1[1], p_ffn1[1], place, "reduce_sum_w_down1", layer=1)
    z_ffn0, p_ffn0 = reduce_end(red_ffn0, "ffn0", r_down)
    r_up = reduce_sum(z_ffn0[0], p_ffn0[0], place, "reduce_sum_w_up0", into=r_up, layer=0)
    r_down = reduce_sum(z_ffn0[1], p_ffn0[1], place, "reduce_sum_w_down0", into=r_down, layer=0)
    later = ["odd_w_qkv", "odd_w_o", "ffn_w_up", "ffn_w_down"]
    joined = join_halves([r_qkv, r_o, r_up, r_down], "grads_join_late_layers")
    G = {nm: g.reshape(W[nm].shape) for nm, g in zip(later, joined)}

    D_, NM, NV = {}, {}, {}

    def update(nm):
        as3 = lambda a: a.reshape((-1,) + a.shape[-2:])
        outs = adamw(as3(W[nm]), as3(G[nm]), as3(M1[nm]), as3(M2[nm]), "adamw_" + nm)
        D_[nm], NM[nm], NV[nm], G[nm] = [o.reshape(W[nm].shape) for o in outs]

    for nm in later:
        update(nm)
    z_even, p_even = reduce_end(red_even, "even", D_[later[-1]])
    joined = join_halves([reduce_sum(z_even[0], p_even[0], place, "reduce_sum_w_in"),
                          reduce_sum(z_even[1], p_even[1], place, "reduce_sum_w_out")], "grads_join_first_layer")
    for nm, g in zip(["even_w_in", "even_w_out"], joined):
        G[nm] = g.reshape(W[nm].shape)
        update(nm)
    small_rows, small_land = devices_wait(small_rows, small_land, small_sems, D_["even_w_out"], "small_grads_wait")
    small_sum = device_sum(small_land, small_rows, place[2:3], "small_grads_sum")
    mine = lax.dynamic_slice_in_dim(small_sum, n_rep + chip * n_shard, n_shard, axis=0)
    g_small = jnp.concatenate([small_sum[:n_rep], mine], axis=0)
    small_names = [n for n, _ in REPLICATED_SMALL + SHARDED_SMALL]
    small_shapes = [s for _, s in REPLICATED_SMALL + SHARDED_SMALL]
    G.update(dict(zip(small_names, _unpack(g_small, small_shapes))))
    packs = [_pack([d[n] for n in small_names])[None] for d in (W, M1, M2)]
    outs = adamw(packs[0], g_small[None], packs[1], packs[2], "adamw_small")
    for dst, o in zip((D_, NM, NV), outs[:3]):
        dst.update(dict(zip(small_names, _unpack(o[0], small_shapes))))

    return (loss, grad_x[None], *[G[n] for n in WEIGHT_ORDER], *[D_[n] for n in WEIGHT_ORDER],
            *[NM[n] for n in WEIGHT_ORDER], *[NV[n] for n in WEIGHT_ORDER])


def _head_sum(dg):
    return jnp.sum(dg.reshape(N_HEADS, HEAD_DIM), axis=0, keepdims=True)
```

```python
import functools
import math

import numpy as np
import jax
import jax.numpy as jnp
from jax import lax
from jax.experimental import pallas as pl
from jax.experimental.pallas import tpu as pltpu

F32 = jnp.float32
BF16 = jnp.bfloat16

D_MODEL = 1024
N_HEADS = 16
HEAD_DIM = 64
A_WIDTH = 512
POOL_WINDOWS = (2, 4, 8, 16)
POOL_GROUP = 128
EVEN_IN = 2048
D_FF = 2816
DILATED_PAIRS = ((128, 1), (512, 4), (2048, 16))
ATT_BLOCK = 128
N_REL_BUCKETS = 32
REL_MAX_DISTANCE = 2048
EPS = 1e-6
MASK_VALUE = -1e30
ADAM_LR, ADAM_B1, ADAM_B2, ADAM_EPS, ADAM_WD, ADAM_STEP = 0.001, 0.9, 0.999, 1e-08, 0.01, 10

VMEM_LIMIT_BYTES = 48 * 1024 * 1024
N_CHIPS = 4
N_DEV = 8
MESH = pl.DeviceIdType.MESH


def _params(*sem):
    return pltpu.CompilerParams(dimension_semantics=sem if sem else None, vmem_limit_bytes=VMEM_LIMIT_BYTES)


def _sds(shape, dtype):
    return jax.ShapeDtypeStruct(tuple(shape), dtype)


def cast_bf16(x, name, tr=None):
    lead, (R, C) = x.shape[:-2], x.shape[-2:]
    n = int(np.prod(lead)) if lead else 1
    x3 = x.reshape((n, R, C))
    tr = tr or R

    def body(x_ref, o_ref):
        o_ref[...] = x_ref[...].astype(BF16)

    out = pl.pallas_call(
        body, name=name, grid=(n, R // tr),
        in_specs=[pl.BlockSpec((None, tr, C), lambda i, r: (i, r, 0))],
        out_specs=pl.BlockSpec((None, tr, C), lambda i, r: (i, r, 0)),
        out_shape=_sds((n, R, C), BF16), compiler_params=_params("parallel", "parallel"),
    )(x3)
    return out.reshape(lead + (R, C))


def rmsnorm_fwd(x, g, name, ts=512):
    S, Dm = x.shape

    def body(x_ref, g_ref, o_ref):
        xv = x_ref[...]
        r = lax.rsqrt(jnp.mean(xv * xv, axis=-1, keepdims=True) + EPS)
        o_ref[...] = ((xv * r) * g_ref[...]).astype(BF16)

    return pl.pallas_call(
        body, name=name, grid=(S // ts,),
        in_specs=[pl.BlockSpec((ts, Dm), lambda i: (i, 0)), pl.BlockSpec((1, Dm), lambda i: (0, 0))],
        out_specs=pl.BlockSpec((ts, Dm), lambda i: (i, 0)),
        out_shape=_sds((S, Dm), BF16), compiler_params=_params("parallel"),
    )(x, g)


def mm_nn(a, w, name, layer=0, res=None, out_dtype=F32, tm=1024):
    M, K = a.shape
    J, _, _, Ns = w.shape

    def body(*refs):
        a_ref, w_ref = refs[0], refs[1]
        o_ref = refs[-1]
        acc = jnp.dot(a_ref[...], w_ref[...], preferred_element_type=F32)
        if res is not None:
            acc = refs[2][...] + acc
        o_ref[...] = acc.astype(o_ref.dtype)

    in_specs = [pl.BlockSpec((tm, K), lambda j, m: (m, 0)),
                pl.BlockSpec((None, None, K, Ns), lambda j, m: (j, layer, 0, 0))]
    args = [a, w]
    if res is not None:
        in_specs.append(pl.BlockSpec((tm, Ns), lambda j, m: (m, j)))
        args.append(res)
    return pl.pallas_call(
        body, name=name, grid=(J, M // tm), in_specs=in_specs,
        out_specs=pl.BlockSpec((tm, Ns), lambda j, m: (m, j)),
        out_shape=_sds((M, J * Ns), out_dtype), compiler_params=_params("parallel", "parallel"),
    )(*args)


def mm_res_norm(a, w, res, gain, name, tm=1024):
    M, K = a.shape
    Dm = w.shape[-1]

    def body(a_ref, w_ref, r_ref, g_ref, y_ref, yn_ref):
        y = r_ref[...] + jnp.dot(a_ref[...], w_ref[...], preferred_element_type=F32)
        y_ref[...] = y
        r = lax.rsqrt(jnp.mean(y * y, axis=-1, keepdims=True) + EPS)
        yn_ref[...] = ((y * r) * g_ref[...]).astype(BF16)

    row = pl.BlockSpec((tm, Dm), lambda m: (m, 0))
    return pl.pallas_call(
        body, name=name, grid=(M // tm,),
        in_specs=[pl.BlockSpec((tm, K), lambda m: (m, 0)),
                  pl.BlockSpec((None, None, K, Dm), lambda m: (0, 0, 0, 0), pipeline_mode=pl.Buffered(1)),
                  row, pl.BlockSpec((1, Dm), lambda m: (0, 0))],
        out_specs=[row, row], out_shape=[_sds((M, Dm), F32), _sds((M, Dm), BF16)],
        compiler_params=_params("parallel"),
    )(a, w, res, gain)


def mm_res_loss(a, w, res, target, name, tm=512):
    M, K = a.shape
    Dm = w.shape[-1]

    def body(a_ref, w_ref, r_ref, t_ref, d_ref, db_ref, s_ref):
        e = (r_ref[...] + jnp.dot(a_ref[...], w_ref[...], preferred_element_type=F32)) - t_ref[...]
        d = e * (1.0 / Dm)
        d_ref[...] = d
        db_ref[...] = d.astype(BF16)
        part = jnp.sum(e * e, axis=0, keepdims=True)

        @pl.when(pl.program_id(0) == 0)
        def _():
            s_ref[...] = part

        @pl.when(pl.program_id(0) > 0)
        def _():
            s_ref[...] += part

    row = pl.BlockSpec((tm, Dm), lambda m: (m, 0))
    return pl.pallas_call(
        body, name=name, grid=(M // tm,),
        in_specs=[pl.BlockSpec((tm, K), lambda m: (m, 0)),
                  pl.BlockSpec((None, None, K, Dm), lambda m: (0, 0, 0, 0), pipeline_mode=pl.Buffered(1)), row, row],
        out_specs=[row, row, pl.BlockSpec((1, Dm), lambda m: (0, 0))],
        out_shape=[_sds((M, Dm), F32), _sds((M, Dm), BF16), _sds((1, Dm), F32)],
        compiler_params=_params("arbitrary"),
    )(a, w, res, target)


def mm_nt(dy, w, name, tr, layer=0, out_dtype=F32, tm=512):
    M = dy.shape[0]
    J, _, R, Ns = w.shape
    dims = (((1,), (1,)), ((), ()))

    def body(dy_ref, w_ref, o_ref):
        acc = None
        for j in range(J):
            p = lax.dot_general(dy_ref[:, j * Ns:(j + 1) * Ns], w_ref[j], dims, preferred_element_type=F32)
            acc = p if acc is None else acc + p
        o_ref[...] = acc.astype(o_ref.dtype)

    return pl.pallas_call(
        body, name=name, grid=(R // tr, M // tm),
        in_specs=[pl.BlockSpec((tm, J * Ns), lambda r, m: (m, 0)),
                  pl.BlockSpec((J, None, tr, Ns), lambda r, m: (0, layer, r, 0))],
        out_specs=pl.BlockSpec((tm, tr), lambda r, m: (m, r)),
        out_shape=_sds((M, R), out_dtype),
        compiler_params=_params("parallel", "parallel"),
    )(dy, w)


def mm_nt_norm_bwd(dy, w, x, g, dres, name, layer=0, tm=512):
    M = dy.shape[0]
    J, _, Dm, Ns = w.shape
    dims = (((1,), (1,)), ((), ()))

    def body(dy_ref, w_ref, x_ref, g_ref, r_ref, dx_ref, dxb_ref, dg_ref):
        dxn = None
        for j in range(J):
            p = lax.dot_general(dy_ref[:, j * Ns:(j + 1) * Ns], w_ref[j], dims, preferred_element_type=F32)
            dxn = p if dxn is None else dxn + p
        xv = x_ref[...]
        r = lax.rsqrt(jnp.mean(xv * xv, axis=-1, keepdims=True) + EPS)
        gx = dxn * g_ref[...]
        dot = jnp.sum(gx * xv, axis=-1, keepdims=True)
        dx = r_ref[...] + r * gx - xv * ((r * r * r) * (dot * (1.0 / Dm)))
        dx_ref[...] = dx
        dxb_ref[...] = dx.astype(BF16)
        part = jnp.sum(dxn * (xv * r), axis=0, keepdims=True)

        @pl.when(pl.program_id(0) == 0)
        def _():
            dg_ref[...] = part

        @pl.when(pl.program_id(0) > 0)
        def _():
            dg_ref[...] += part

    row = pl.BlockSpec((tm, Dm), lambda m: (m, 0))
    vec = pl.BlockSpec((1, Dm), lambda m: (0, 0))
    return pl.pallas_call(
        body, name=name, grid=(M // tm,),
        in_specs=[pl.BlockSpec((tm, J * Ns), lambda m: (m, 0)),
                  pl.BlockSpec((J, None, Dm, Ns), lambda m: (0, layer, 0, 0), pipeline_mode=pl.Buffered(1)), row, vec, row],
        out_specs=[row, row, vec],
        out_shape=[_sds((M, Dm), F32), _sds((M, Dm), BF16), _sds((1, Dm), F32)],
        compiler_params=_params("arbitrary"),
    )(dy, w, x, g, dres)


def mm_tn(a, dy, name, J, tk, tm=512, jb=None):
    M, K = a.shape
    jb = jb or J
    Ns = dy.shape[1] // J
    N = jb * Ns
    n_m = M // tm
    dims = (((0,), (0,)), ((), ()))

    def body(a_ref, dy_ref, o_ref, acc_ref):
        p = lax.dot_general(a_ref[...], dy_ref[...], dims, preferred_element_type=F32)
        m = pl.program_id(2)

        @pl.when(m == 0)
        def _():
            acc_ref[...] = p

        @pl.when(m > 0)
        def _():
            acc_ref[...] += p

        @pl.when(m == n_m - 1)
        def _():
            for j in range(jb):
                o_ref[j] = acc_ref[:, j * Ns:(j + 1) * Ns].astype(BF16)

    return pl.pallas_call(
        body, name=name, grid=(J // jb, K // tk, n_m),
        in_specs=[pl.BlockSpec((tm, tk), lambda g, k, m: (m, k)), pl.BlockSpec((tm, N), lambda g, k, m: (m, g))],
        out_specs=pl.BlockSpec((jb, tk, Ns), lambda g, k, m: (g, k, 0)),
        out_shape=_sds((J, K, Ns), BF16), scratch_shapes=[pltpu.VMEM((tk, N), F32)],
        compiler_params=_params("parallel", "parallel", "arbitrary"),
    )(a, dy)


HALO = 16


def _shift_down(x, s):
    return pltpu.roll(x, s, 0)


def _shift_up(x, s):
    return pltpu.roll(x, x.shape[0] - s, 0)


def _conv3(z, cw):
    return (_shift_down(z, 2) * cw[0:1] + _shift_down(z, 1) * cw[1:2]) + z * cw[2:3]


def _window_count(first_row, n, k):
    t = first_row + lax.broadcasted_iota(jnp.int32, (n, 1), 0)
    return jnp.clip(t + 1, 1, k).astype(F32)


def mixer_fwd(proj, conv_w, pool_w, pool_scale, name, ts=256):
    S = proj.shape[0]
    n = ts + HALO

    def body(pm_ref, pb_ref, cw_ref, pw_ref, ps_ref, o_ref):
        i = pl.program_id(0)
        before = jnp.where(i > 0, pb_ref[...], 0.0)
        ext = jnp.concatenate([before, pm_ref[...]], axis=0)
        cw = cw_ref[...]
        z = ext[:, 2 * A_WIDTH:3 * A_WIDTH] * ext[:, 0:A_WIDTH]
        cz = _conv3(z, cw)
        ya = pm_ref[:, A_WIDTH:2 * A_WIDTH] * cz[HALO:]
        o_ref[:, 0:A_WIDTH] = ya.astype(BF16)
        for g, k in enumerate(POOL_WINDOWS):
            lo = 3 * A_WIDTH + g * POOL_GROUP
            p = ext[:, lo:lo + POOL_GROUP]
            w = p
            s = 1
            while s < k:
                w = w + _shift_down(w, s)
                s *= 2
            pooled = w / _window_count(i * ts - HALO, n, k) - p
            yb = jnp.dot(pooled[HALO:].astype(BF16), pw_ref[g], preferred_element_type=F32)
            yb = yb * ps_ref[:, g * POOL_GROUP:(g + 1) * POOL_GROUP]
            o_ref[:, A_WIDTH + g * POOL_GROUP:A_WIDTH + (g + 1) * POOL_GROUP] = yb.astype(BF16)

    hb = ts // HALO
    return pl.pallas_call(
        body, name=name, grid=(S // ts,),
        in_specs=[
            pl.BlockSpec((ts, EVEN_IN), lambda i: (i, 0)),
            pl.BlockSpec((HALO, EVEN_IN), lambda i: (jnp.maximum(i * hb - 1, 0), 0)),
            pl.BlockSpec((3, A_WIDTH), lambda i: (0, 0)),
            pl.BlockSpec((4, POOL_GROUP, POOL_GROUP), lambda i: (0, 0, 0)),
            pl.BlockSpec((1, 4 * POOL_GROUP), lambda i: (0, 0)),
        ],
        out_specs=pl.BlockSpec((ts, D_MODEL), lambda i: (i, 0)),
        out_shape=_sds((S, D_MODEL), BF16), compiler_params=_params("parallel"),
    )(proj, proj, conv_w, pool_w, pool_scale)


def mixer_bwd(proj, dmix, conv_w, pool_w, pool_scale, name, ts=256):
    S = proj.shape[0]
    n = ts + 2 * HALO
    nt = S // ts
    tn_dims = (((0,), (0,)), ((), ()))
    nt_dims = (((1,), (1,)), ((), ()))

    def body(pm_ref, pb_ref, pa_ref, dm_ref, da_ref, cw_ref, pw_ref, ps_ref, o_ref, dcw_ref, dpw_ref, dps_ref):
        i = pl.program_id(0)
        last = i == nt - 1
        before = jnp.where(i > 0, pb_ref[...], 0.0)
        after = jnp.where(last, 0.0, pa_ref[...])
        ext = jnp.concatenate([before, pm_ref[...], after], axis=0)
        dafter = jnp.where(last, 0.0, da_ref[...])
        dext = jnp.concatenate([jnp.zeros((HALO, D_MODEL), F32), dm_ref[...], dafter], axis=0)
        cw = cw_ref[...]
        main = slice(HALO, HALO + ts)

        @pl.when(i == 0)
        def _():
            dcw_ref[...] = jnp.zeros_like(dcw_ref)
            dpw_ref[...] = jnp.zeros_like(dpw_ref)
            dps_ref[...] = jnp.zeros_like(dps_ref)

        h, gb, gc = ext[:, 0:A_WIDTH], ext[:, A_WIDTH:2 * A_WIDTH], ext[:, 2 * A_WIDTH:3 * A_WIDTH]
        z = gc * h
        z1, z2 = _shift_down(z, 1), _shift_down(z, 2)
        cz = (z2 * cw[0:1] + z1 * cw[1:2]) + z * cw[2:3]
        dya = dext[:, 0:A_WIDTH]
        dcz = dya * gb
        dz = dcz * cw[2:3] + _shift_up(dcz, 1) * cw[1:2] + _shift_up(dcz, 2) * cw[0:1]
        o_ref[:, 0:A_WIDTH] = (dz * gc)[main].astype(BF16)
        o_ref[:, A_WIDTH:2 * A_WIDTH] = (dya * cz)[main].astype(BF16)
        o_ref[:, 2 * A_WIDTH:3 * A_WIDTH] = (dz * h)[main].astype(BF16)
        dczm = dcz[main]
        dcw_ref[0:1, :] += jnp.sum(dczm * z2[main], axis=0, keepdims=True)
        dcw_ref[1:2, :] += jnp.sum(dczm * z1[main], axis=0, keepdims=True)
        dcw_ref[2:3, :] += jnp.sum(dczm * z[main], axis=0, keepdims=True)

        for g, k in enumerate(POOL_WINDOWS):
            lo = 3 * A_WIDTH + g * POOL_GROUP
            cols = slice(g * POOL_GROUP, (g + 1) * POOL_GROUP)
            p = ext[:, lo:lo + POOL_GROUP]
            w = p
            s = 1
            while s < k:
                w = w + _shift_down(w, s)
                s *= 2
            cnt = _window_count(i * ts - HALO, n, k)
            pooled = (w / cnt - p)[main].astype(BF16)
            dyb = dext[:, A_WIDTH + g * POOL_GROUP:A_WIDTH + (g + 1) * POOL_GROUP]
            e = dyb * ps_ref[:, cols]
            pre = jnp.dot(pooled, pw_ref[g], preferred_element_type=F32)
            dps_ref[:, cols] += jnp.sum(dyb[main] * pre, axis=0, keepdims=True)
            dpw_ref[g] += lax.dot_general(pooled, e[main].astype(BF16), tn_dims, preferred_element_type=F32)
            dpooled = lax.dot_general(e.astype(BF16), pw_ref[g], nt_dims, preferred_element_type=F32)
            q = dpooled / cnt
            a = q
            s = 1
            while s < k:
                a = a + _shift_up(a, s)
                s *= 2
            o_ref[:, lo:lo + POOL_GROUP] = (a - dpooled)[main].astype(BF16)

    hb = ts // HALO
    nh = S // HALO
    before_map = lambda i: (jnp.maximum(i * hb - 1, 0), 0)
    after_map = lambda i: (jnp.minimum((i + 1) * hb, nh - 1), 0)
    full = lambda *shape: pl.BlockSpec(shape, lambda i: (0,) * len(shape))
    return pl.pallas_call(
        body, name=name, grid=(nt,),
        in_specs=[
            pl.BlockSpec((ts, EVEN_IN), lambda i: (i, 0)),
            pl.BlockSpec((HALO, EVEN_IN), before_map),
            pl.BlockSpec((HALO, EVEN_IN), after_map),
            pl.BlockSpec((ts, D_MODEL), lambda i: (i, 0)),
            pl.BlockSpec((HALO, D_MODEL), after_map),
            full(3, A_WIDTH), full(4, POOL_GROUP, POOL_GROUP), full(1, 4 * POOL_GROUP),
        ],
        out_specs=[pl.BlockSpec((ts, EVEN_IN), lambda i: (i, 0)), full(3, A_WIDTH), full(4, POOL_GROUP, POOL_GROUP),
                   full(1, 4 * POOL_GROUP)],
        out_shape=[_sds((S, EVEN_IN), BF16), _sds((3, A_WIDTH), F32), _sds((4, POOL_GROUP, POOL_GROUP), F32),
                   _sds((1, 4 * POOL_GROUP), F32)],
        compiler_params=_params("arbitrary"),
    )(proj, proj, proj, dmix, dmix, conv_w, pool_w, pool_scale)


FFN_HALO = 16
FFN_TC = 1408


def glu_fwd(up, conv_w, conv_b, name, ts=256):
    S = up.shape[0]
    nc = D_FF // FFN_TC

    def body(gm_ref, gb_ref, um_ref, ub_ref, cwg_ref, cwu_ref, cbg_ref, cbu_ref, ug_ref, uu_ref, o_ref):
        i = pl.program_id(0)

        def conv(m_ref, b_ref, cw_ref, cb_ref):
            before = jnp.where(i > 0, b_ref[...].astype(F32), 0.0)
            ext = jnp.concatenate([before, m_ref[...].astype(F32)], axis=0)
            return _conv3(ext, cw_ref[...])[FFN_HALO:] + cb_ref[...]

        gate = conv(gm_ref, gb_ref, cwg_ref, cbg_ref)
        upv = conv(um_ref, ub_ref, cwu_ref, cbu_ref)
        ug_ref[...] = gate.astype(BF16)
        uu_ref[...] = upv.astype(BF16)
        o_ref[...] = ((gate * (1.0 / (1.0 + jnp.exp(-gate)))) * upv).astype(BF16)

    hb = ts // FFN_HALO
    main = lambda off: pl.BlockSpec((ts, FFN_TC), lambda i, c: (i, c + off))
    halo = lambda off: pl.BlockSpec((FFN_HALO, FFN_TC), lambda i, c: (jnp.maximum(i * hb - 1, 0), c + off))
    cw = lambda off: pl.BlockSpec((3, FFN_TC), lambda i, c: (0, c + off))
    cb = lambda off: pl.BlockSpec((1, FFN_TC), lambda i, c: (0, c + off))
    ug, uu, act = pl.pallas_call(
        body, name=name, grid=(S // ts, nc),
        in_specs=[main(0), halo(0), main(nc), halo(nc), cw(0), cw(nc), cb(0), cb(nc)],
        out_specs=[pl.BlockSpec((ts, FFN_TC), lambda i, c: (i, c))] * 3,
        out_shape=[_sds((S, D_FF), BF16)] * 3, compiler_params=_params("parallel", "parallel"),
    )(up, up, up, up, conv_w, conv_w, conv_b, conv_b)
    return (ug, uu), act


def glu_bwd(up, u, da, conv_w, name, ts=256):
    S = up.shape[0]
    nc = D_FF // FFN_TC
    nt = S // ts
    W = 2 * D_FF

    def body(x_ref, gm_ref, ga_ref, um_ref, ua_ref, dm_ref, da_ref, cw_ref, dx_ref, dcw_ref, dcb_ref):
        i = pl.program_id(0)
        last = i == nt - 1

        @pl.when(i == 0)
        def _():
            dcw_ref[...] = jnp.zeros_like(dcw_ref)
            dcb_ref[...] = jnp.zeros_like(dcb_ref)

        def rows(m_ref, a_ref, cols):
            return jnp.concatenate([m_ref[:, cols], a_ref[:, cols]], axis=0).astype(F32)

        def back(d, cols):
            cw = cw_ref[:, cols]
            d1, d2 = _shift_up(d, 1), _shift_up(d, 2)
            dx_ref[:, cols] = ((d * cw[2:3] + d1 * cw[1:2]) + d2 * cw[0:1])[:ts].astype(BF16)
            x = x_ref[:, cols].astype(F32)
            dcb_ref[:, cols] += jnp.sum(d[:ts], axis=0, keepdims=True)
            dcw_ref[0:1, cols] += jnp.sum(d2[:ts] * x, axis=0, keepdims=True)
            dcw_ref[1:2, cols] += jnp.sum(d1[:ts] * x, axis=0, keepdims=True)
            dcw_ref[2:3, cols] += jnp.sum(d[:ts] * x, axis=0, keepdims=True)

        for c in range(nc):
            cols = slice(c * FFN_TC, (c + 1) * FFN_TC)
            ug, uu = rows(gm_ref, ga_ref, cols), rows(um_ref, ua_ref, cols)
            dae = rows(dm_ref, da_ref, cols)
            dae = jnp.where(last & (lax.broadcasted_iota(jnp.int32, dae.shape, 0) >= ts), 0.0, dae)
            sg = 1.0 / (1.0 + jnp.exp(-ug))
            duu = dae * (ug * sg)
            dug = (dae * uu) * (sg * (1.0 + ug * (1.0 - sg)))
            back(dug, cols)
            back(duu, slice(D_FF + c * FFN_TC, D_FF + (c + 1) * FFN_TC))

    hb = ts // FFN_HALO
    nh = S // FFN_HALO
    after_map = lambda i: (jnp.minimum((i + 1) * hb, nh - 1), 0)
    main = pl.BlockSpec((ts, D_FF), lambda i: (i, 0))
    after = pl.BlockSpec((FFN_HALO, D_FF), after_map)
    return pl.pallas_call(
        body, name=name, grid=(nt,),
        in_specs=[pl.BlockSpec((ts, W), lambda i: (i, 0)), main, after, main, after, main, after,
                  pl.BlockSpec((3, W), lambda i: (0, 0))],
        out_specs=[pl.BlockSpec((ts, W), lambda i: (i, 0)), pl.BlockSpec((3, W), lambda i: (0, 0)),
                   pl.BlockSpec((1, W), lambda i: (0, 0))],
        out_shape=[_sds((S, W), BF16), _sds((3, W), F32), _sds((1, W), F32)],
        compiler_params=_params("arbitrary"),
    )(up, u[0], u[0], u[1], u[1], da, da, conv_w)


def _head_mean_matrix():
    h = np.arange(D_MODEL) // HEAD_DIM
    return jnp.asarray((h[:, None] == h[None, :]).astype(np.float32) / HEAD_DIM, dtype=BF16)


def _head_mean(v, gm):
    return jnp.dot(v.astype(BF16), gm, preferred_element_type=F32)


def qknorm_fwd(qkv, gqk, name, ts=512):
    S = qkv.shape[0]

    def body(x_ref, g_ref, gm_ref, o_ref):
        part = pl.program_id(0)
        x = x_ref[...]

        @pl.when(part < 2)
        def _():
            r = lax.rsqrt(_head_mean(x * x, gm_ref[...]) + EPS)
            o_ref[...] = ((x * r) * g_ref[...]).astype(BF16)

        @pl.when(part == 2)
        def _():
            o_ref[...] = x.astype(BF16)

    return pl.pallas_call(
        body, name=name, grid=(3, S // ts),
        in_specs=[pl.BlockSpec((ts, D_MODEL), lambda p, i: (i, p)), pl.BlockSpec((None, 1, D_MODEL), lambda p, i: (p, 0, 0)),
                  pl.BlockSpec((D_MODEL, D_MODEL), lambda p, i: (0, 0))],
        out_specs=pl.BlockSpec((ts, D_MODEL), lambda p, i: (i, p)),
        out_shape=_sds((S, 3 * D_MODEL), BF16), compiler_params=_params("parallel", "parallel"),
    )(qkv, gqk, _head_mean_matrix())


def qknorm_bwd(qkv, dq, dk, dv, gqk, name, ts=256):
    S = qkv.shape[0]

    def body(x_ref, dq_ref, dk_ref, dv_ref, g_ref, gm_ref, o_ref, dg_ref):
        @pl.when(pl.program_id(0) == 0)
        def _():
            dg_ref[...] = jnp.zeros_like(dg_ref)

        gm = gm_ref[...]
        for part, d_ref in enumerate((dq_ref, dk_ref)):
            cols = slice(part * D_MODEL, (part + 1) * D_MODEL)
            x = x_ref[:, cols]
            d = d_ref[...]
            r = lax.rsqrt(_head_mean(x * x, gm) + EPS)
            gx = d * g_ref[part]
            o_ref[:, cols] = (r * gx - x * ((r * r * r) * _head_mean(gx * x, gm))).astype(BF16)
            dg_ref[part] += jnp.sum(d * (x * r), axis=0, keepdims=True)
        o_ref[:, 2 * D_MODEL:] = dv_ref[...].astype(BF16)

    row = pl.BlockSpec((ts, D_MODEL), lambda i: (i, 0))
    wide = pl.BlockSpec((ts, 3 * D_MODEL), lambda i: (i, 0))
    gains = pl.BlockSpec((3, 1, D_MODEL), lambda i: (0, 0, 0))
    return pl.pallas_call(
        body, name=name, grid=(S // ts,),
        in_specs=[wide, row, row, row, gains, pl.BlockSpec((D_MODEL, D_MODEL), lambda i: (0, 0))],
        out_specs=[wide, gains],
        out_shape=[_sds((S, 3 * D_MODEL), BF16), _sds((3, 1, D_MODEL), F32)],
        compiler_params=_params("arbitrary"),
    )(qkv, dq, dk, dv, gqk, _head_mean_matrix())


RESIDUES = 16


def _block_order(dil):
    runs = RESIDUES // dil
    slot = np.arange(ATT_BLOCK)
    return (slot % (ATT_BLOCK // runs)) * runs + slot // (ATT_BLOCK // runs)


def _bucket_tables():
    n = ATT_BLOCK
    max_exact = N_REL_BUCKETS // 2
    buckets, valids = [], []
    for _, dil in DILATED_PAIRS:
        order = _block_order(dil)
        a = order[:, None]
        c = np.concatenate([order, n + order])[None, :]
        first_half = (np.arange(2 * n) < n)[None, :]
        rel = a + n - c
        band = (rel >= 0) & (rel <= n)
        dist = np.clip(rel, 0, n) * dil
        dd = np.maximum(dist, 1).astype(np.float32)
        large = max_exact + (np.log(dd / np.float32(max_exact)) / np.float32(math.log(REL_MAX_DISTANCE / max_exact))
                             * np.float32(N_REL_BUCKETS - max_exact)).astype(np.int32)
        large = np.minimum(large, N_REL_BUCKETS - 1)
        buckets.append(np.where(dist < max_exact, dist, large).reshape(1, -1))
        valids.append(np.stack([(band & ~first_half).reshape(1, -1), band.reshape(1, -1)]))
    return np.stack(buckets).astype(np.int32), np.stack(valids).astype(np.int32)


BIAS_CHUNK = 8192


def _split3(x):
    a = x.astype(BF16)
    r = x - a.astype(F32)
    b = r.astype(BF16)
    c = (r - b.astype(F32)).astype(BF16)
    return a, b, c


def bias_expand(rel_bias_t, name):
    bucket, valid = _bucket_tables()
    nq = bucket.shape[-1]

    def body(t_ref, b_ref, v_ref, o_ref):
        onehot = (lax.broadcasted_iota(jnp.int32, (N_REL_BUCKETS, BIAS_CHUNK), 0) == b_ref[...]).astype(BF16)
        acc = None
        for term in _split3(t_ref[...]):
            p = jnp.dot(term, onehot, preferred_element_type=F32)
            acc = p if acc is None else acc + p
        o_ref[...] = jnp.where(v_ref[...] > 0, acc, MASK_VALUE)

    return pl.pallas_call(
        body, name=name, grid=(3, 2, nq // BIAS_CHUNK),
        in_specs=[pl.BlockSpec((N_HEADS, N_REL_BUCKETS), lambda b, v, c: (0, 0)),
                  pl.BlockSpec((None, 1, BIAS_CHUNK), lambda b, v, c: (b, 0, c)),
                  pl.BlockSpec((None, None, 1, BIAS_CHUNK), lambda b, v, c: (b, v, 0, c))],
        out_specs=pl.BlockSpec((None, None, N_HEADS, BIAS_CHUNK), lambda b, v, c: (b, v, 0, c)),
        out_shape=_sds((3, 2, N_HEADS, nq), F32), compiler_params=_params("parallel", "parallel", "parallel"),
    )(rel_bias_t, jnp.asarray(bucket), jnp.asarray(valid))


def bias_reduce(dbias, name):
    bucket, _ = _bucket_tables()
    nq = bucket.shape[-1]
    dims = (((1,), (1,)), ((), ()))

    def body(d_ref, b_ref, o_ref):
        onehot = (lax.broadcasted_iota(jnp.int32, (N_REL_BUCKETS, BIAS_CHUNK), 0) == b_ref[...]).astype(BF16)
        acc = None
        for term in _split3(d_ref[...]):
            p = lax.dot_general(term, onehot, dims, preferred_element_type=F32)
            acc = p if acc is None else acc + p

        @pl.when(pl.program_id(1) == 0)
        def _():
            o_ref[...] = acc

        @pl.when(pl.program_id(1) > 0)
        def _():
            o_ref[...] += acc

    return pl.pallas_call(
        body, name=name, grid=(3, nq // BIAS_CHUNK),
        in_specs=[pl.BlockSpec((None, N_HEADS, BIAS_CHUNK), lambda b, c: (b, 0, c)),
                  pl.BlockSpec((None, 1, BIAS_CHUNK), lambda b, c: (b, 0, c))],
        out_specs=pl.BlockSpec((None, N_HEADS, N_REL_BUCKETS), lambda b, c: (b, 0, 0)),
        out_shape=_sds((3, N_HEADS, N_REL_BUCKETS), F32), compiler_params=_params("parallel", "arbitrary"),
    )(dbias, jnp.asarray(bucket))


PAIR = 2 * HEAD_DIM
N_PAIRS = N_HEADS // 2
_NT = (((1,), (1,)), ((), ()))
_TN = (((0,), (0,)), ((), ()))


def _low_lanes(shape):
    return lax.broadcasted_iota(jnp.int32, shape, 1) < HEAD_DIM


ATTN_VMEM_LIMIT_BYTES = 56 * 1024 * 1024
BRANCH_ORDER = (2, 1, 0)


def _regroup(dst, src, L16):
    for r in range(RESIDUES):
        dst[pl.ds(r * L16, L16), :] = src[pl.ds(r, L16, stride=RESIDUES), :]


def _ungroup(dst, src, L16):
    for r in range(RESIDUES):
        dst[pl.ds(r, L16, stride=RESIDUES), :] = src[pl.ds(r * L16, L16), :]


def _branch_geometry(branch, S):
    dil = DILATED_PAIRS[branch][1]
    runs = RESIDUES // dil
    return dil, runs, ATT_BLOCK // runs, S // dil // ATT_BLOCK


def _block_rows(it, branch, S):
    dil, runs, run_len, n_blocks = _branch_geometry(branch, S)
    L16 = S // RESIDUES
    r, b = it // n_blocks, it % n_blocks
    prev = jnp.maximum(b - 1, 0)
    cur_rows = [pl.multiple_of((j * dil + r) * L16 + run_len * b, 8) for j in range(runs)]
    prev_rows = [pl.multiple_of((j * dil + r) * L16 + run_len * prev, 8) for j in range(runs)]
    return cur_rows, prev_rows, jnp.minimum(b, 1)


def _load_block(ref, rows, run_len):
    parts = [ref[pl.ds(o, run_len), :] for o in rows]
    return parts[0] if len(parts) == 1 else jnp.concatenate(parts, axis=0)


def _store_block(ref, rows, run_len, value, add=False):
    for j, o in enumerate(rows):
        part = value[j * run_len:(j + 1) * run_len]
        if add:
            ref[pl.ds(o, run_len), :] += part
        else:
            ref[pl.ds(o, run_len), :] = part


ATTN_FWD_UNROLL = 8
ATTN_BWD_UNROLL = 4


def _stack_heads(x, low):
    zero = jnp.zeros_like(x)
    return jnp.concatenate([jnp.where(low, x, zero), jnp.where(low, zero, x)], axis=0)


def _unstack_heads(y, low):
    return jnp.where(low, y[:ATT_BLOCK], y[ATT_BLOCK:])


def attn_fwd(qkvn, bias, name):
    S = qkvn.shape[0]
    L16 = S // RESIDUES
    n_iter = S // ATT_BLOCK

    def body(q_ref, k_ref, v_ref, b_ref, o_ref, lse_ref, stage, qp, kp, vp, acc_s, m_s, l_s):
        for src, dst in ((q_ref, qp), (k_ref, kp), (v_ref, vp)):
            stage[...] = src[...].astype(F32)
            _regroup(dst, stage, L16)
        low = _low_lanes((ATT_BLOCK, PAIR))

        for branch in BRANCH_ORDER:
            _, _, run_len, _ = _branch_geometry(branch, S)
            first = branch == BRANCH_ORDER[0]

            def step(it, carry, branch=branch, run_len=run_len, first=first):
                cur, prev, variant = _block_rows(it, branch, S)
                q = _load_block(qp, cur, run_len).astype(BF16)
                k = jnp.concatenate([_load_block(kp, prev, run_len), _load_block(kp, cur, run_len)], axis=0).astype(BF16)
                v = jnp.concatenate([_load_block(vp, prev, run_len), _load_block(vp, cur, run_len)], axis=0).astype(BF16)
                s = lax.dot_general(_stack_heads(q, low), k, _NT, preferred_element_type=F32) * (HEAD_DIM ** -0.5)
                s = s + b_ref[2 * branch + variant].reshape(2 * ATT_BLOCK, 2 * ATT_BLOCK)
                mx = jnp.max(s, axis=-1, keepdims=True)
                p = jnp.exp(s - mx)
                den = jnp.sum(p, axis=-1, keepdims=True)
                pv = jnp.dot(p.astype(BF16), v, preferred_element_type=F32)
                acc = _unstack_heads(pv, low)
                m = _unstack_heads(mx, low)
                l = _unstack_heads(den, low)
                if not first:
                    m_old = _load_block(m_s, cur, run_len)
                    m_new = jnp.maximum(m_old, m)
                    a_old, a_new = jnp.exp(m_old - m_new), jnp.exp(m - m_new)
                    acc = _load_block(acc_s, cur, run_len) * a_old + acc * a_new
                    l = _load_block(l_s, cur, run_len) * a_old + l * a_new
                    m = m_new
                _store_block(acc_s, cur, run_len, acc)
                _store_block(m_s, cur, run_len, m)
                _store_block(l_s, cur, run_len, l)
                return carry

            lax.fori_loop(0, n_iter, step, 0, unroll=ATTN_FWD_UNROLL)

        acc_s[...] = acc_s[...] / l_s[...]
        _ungroup(stage, acc_s, L16)
        o_ref[...] = stage[...].astype(BF16)
        m_s[...] = m_s[...] + jnp.log(l_s[...])
        _ungroup(lse_ref, m_s, L16)

    col = lambda part: pl.BlockSpec((S, PAIR), lambda hp: (0, part * N_PAIRS + hp))
    out = pl.BlockSpec((S, PAIR), lambda hp: (0, hp))
    return pl.pallas_call(
        body, name=name, grid=(N_PAIRS,),
        in_specs=[col(0), col(1), col(2), pl.BlockSpec((6, 2, ATT_BLOCK, 2 * ATT_BLOCK), lambda hp: (0, hp, 0, 0))],
        out_specs=[out, out], out_shape=[_sds((S, D_MODEL), BF16), _sds((S, D_MODEL), F32)],
        scratch_shapes=[pltpu.VMEM((S, PAIR), F32)] * 7,
        compiler_params=pltpu.CompilerParams(dimension_semantics=("parallel",), vmem_limit_bytes=ATTN_VMEM_LIMIT_BYTES),
    )(qkvn, qkvn, qkvn, bias)


def attn_bwd(qkvn, att, datt, lse, bias, name):
    S = qkvn.shape[0]
    L16 = S // RESIDUES
    n_iter = S // ATT_BLOCK
    TILE = 512

    def body(q_ref, k_ref, v_ref, o_ref, do_ref, lse_ref, b_ref, dq_ref, dk_ref, dv_ref, db_ref,
             qp, kp, vp, dop, ldp, dqp, dkp, dvp):
        stage = dqp
        for src, dst in ((q_ref, qp), (k_ref, kp), (v_ref, vp), (do_ref, dop)):
            stage[...] = src[...].astype(F32)
            _regroup(dst, stage, L16)

        def pack(i, carry):
            rows = pl.ds(pl.multiple_of(i * TILE, TILE), TILE)
            low = _low_lanes((TILE, PAIR))
            lane = lax.broadcasted_iota(jnp.int32, (TILE, PAIR), 1)
            prod = do_ref[rows, :].astype(F32) * o_ref[rows, :].astype(F32)
            d0 = jnp.sum(jnp.where(low, prod, 0.0), axis=-1, keepdims=True)
            d1 = jnp.sum(jnp.where(low, 0.0, prod), axis=-1, keepdims=True)
            stage[rows, :] = jnp.where((lane & (HEAD_DIM // 2)) == 0, lse_ref[rows, :], jnp.where(low, d0, d1))
            return carry

        lax.fori_loop(0, S // TILE, pack, 0)
        _regroup(ldp, stage, L16)
        dqp[...] = jnp.zeros_like(dqp)
        dkp[...] = jnp.zeros_like(dkp)
        dvp[...] = jnp.zeros_like(dvp)
        db_ref[...] = jnp.zeros_like(db_ref)
        low = _low_lanes((ATT_BLOCK, PAIR))

        for branch in BRANCH_ORDER:
            _, _, run_len, _ = _branch_geometry(branch, S)

            def step(it, carry, branch=branch, run_len=run_len):
                cur, prev, variant = _block_rows(it, branch, S)
                q = _load_block(qp, cur, run_len).astype(BF16)
                dout = _load_block(dop, cur, run_len).astype(BF16)
                ld = _load_block(ldp, cur, run_len)
                k = jnp.concatenate([_load_block(kp, prev, run_len), _load_block(kp, cur, run_len)], axis=0).astype(BF16)
                v = jnp.concatenate([_load_block(vp, prev, run_len), _load_block(vp, cur, run_len)], axis=0).astype(BF16)
                half = HEAD_DIM // 2
                lse2 = jnp.concatenate([ld[:, 0:1], ld[:, HEAD_DIM:HEAD_DIM + 1]], axis=0)
                delta2 = jnp.concatenate([ld[:, half:half + 1], ld[:, HEAD_DIM + half:HEAD_DIM + half + 1]], axis=0)
                q2, do2 = _stack_heads(q, low), _stack_heads(dout, low)
                s = lax.dot_general(q2, k, _NT, preferred_element_type=F32) * (HEAD_DIM ** -0.5)
                p = jnp.exp(s + b_ref[2 * branch + variant].reshape(2 * ATT_BLOCK, 2 * ATT_BLOCK) - lse2)
                dp = lax.dot_general(do2, v, _NT, preferred_element_type=F32)
                ds = p * (dp - delta2)
                db_ref[branch] += ds.reshape(2, ATT_BLOCK, 2 * ATT_BLOCK)
                dsb = (ds * (HEAD_DIM ** -0.5)).astype(BF16)
                dq = _unstack_heads(jnp.dot(dsb, k, preferred_element_type=F32), low)
                dk = lax.dot_general(dsb, q2, _TN, preferred_element_type=F32)
                dv = lax.dot_general(p.astype(BF16), do2, _TN, preferred_element_type=F32)
                _store_block(dqp, cur, run_len, dq, add=True)
                _store_block(dkp, prev, run_len, dk[:ATT_BLOCK], add=True)
                _store_block(dvp, prev, run_len, dv[:ATT_BLOCK], add=True)
                _store_block(dkp, cur, run_len, dk[ATT_BLOCK:], add=True)
                _store_block(dvp, cur, run_len, dv[ATT_BLOCK:], add=True)
                return carry

            lax.fori_loop(0, n_iter, step, 0, unroll=ATTN_BWD_UNROLL)

        _ungroup(dq_ref, dqp, L16)
        _ungroup(dk_ref, dkp, L16)
        _ungroup(dv_ref, dvp, L16)

    col = lambda part: pl.BlockSpec((S, PAIR), lambda hp: (0, part * N_PAIRS + hp))
    one = pl.BlockSpec((S, PAIR), lambda hp: (0, hp))
    return pl.pallas_call(
        body, name=name, grid=(N_PAIRS,),
        in_specs=[col(0), col(1), col(2), one, one, one,
                  pl.BlockSpec((6, 2, ATT_BLOCK, 2 * ATT_BLOCK), lambda hp: (0, hp, 0, 0))],
        out_specs=[one, one, one, pl.BlockSpec((3, 2, ATT_BLOCK, 2 * ATT_BLOCK), lambda hp: (0, hp, 0, 0))],
        out_shape=[_sds((S, D_MODEL), F32)] * 3 + [_sds((3, N_HEADS, ATT_BLOCK, 2 * ATT_BLOCK), F32)],
        scratch_shapes=[pltpu.VMEM((S, PAIR), F32)] * 8,
        compiler_params=pltpu.CompilerParams(dimension_semantics=("parallel",), vmem_limit_bytes=ATTN_VMEM_LIMIT_BYTES),
    )(qkvn, qkvn, qkvn, att, datt, lse, bias)


def adamw(w, g, m, v, name):
    n, R, C = w.shape

    def body(w_ref, g_ref, m_ref, v_ref, d_ref, nm_ref, nv_ref, go_ref):
        gv = g_ref[...]
        go_ref[...] = gv
        m2 = ADAM_B1 * m_ref[...] + (1.0 - ADAM_B1) * gv
        v2 = ADAM_B2 * v_ref[...] + (1.0 - ADAM_B2) * (gv * gv)
        m_hat = m2 / (1.0 - ADAM_B1 ** ADAM_STEP)
        v_hat = v2 / (1.0 - ADAM_B2 ** ADAM_STEP)
        d_ref[...] = -ADAM_LR * (m_hat / (jnp.sqrt(v_hat) + ADAM_EPS) + ADAM_WD * w_ref[...])
        nm_ref[...] = m2
        nv_ref[...] = v2

    tr = R
    while tr * C * 4 > (1 << 21) and tr % 16 == 0:
        tr //= 2
    spec = pl.BlockSpec((None, tr, C), lambda i, r: (i, r, 0))
    return pl.pallas_call(
        body, name=name, grid=(n, R // tr), in_specs=[spec] * 4, out_specs=[spec] * 4,
        out_shape=[_sds((n, R, C), F32)] * 4, compiler_params=_params("parallel", "parallel"),
    )(w, g, m, v)


ANY = pl.BlockSpec(memory_space=pl.ANY)


def _coords():
    return lax.axis_index("x"), lax.axis_index("y"), lax.axis_index("c")


def _other_chips(mx, my):
    return [(1 - mx, my), (mx, 1 - my), (1 - mx, 1 - my)]


def _remote(src, dst, send, recv, dev):
    return pltpu.make_async_remote_copy(src_ref=src, dst_ref=dst, send_sem=send, recv_sem=recv, device_id=dev,
                                        device_id_type=MESH)


HBM =pl.BlockSpec(memory_space=pltpu.HBM)
SEM = pl.BlockSpec(memory_space=pltpu.SEMAPHORE)
_SPLIT_COPY = pltpu.CompilerParams(has_side_effects=pltpu.SideEffectType.DATAFLOW_SIDE_EFFECTING)


def _in_hbm(a):
    return pltpu.with_memory_space_constraint(a, pltpu.HBM)


def cast_into_slot(w, layer, chip_core, name, dtype=BF16):
    _, _, hR, C = w.shape

    def body(s_ref, w_ref, o_ref):
        del s_ref
        o_ref[...] = w_ref[...].astype(dtype)

    grid_spec = pltpu.PrefetchScalarGridSpec(
        num_scalar_prefetch=1, grid=(2,),
        in_specs=[pl.BlockSpec((None, None, hR, C), lambda h, s: (layer, h, 0, 0))],
        out_specs=pl.BlockSpec((None, None, hR, C), lambda h, s: (s[0], h, 0, 0)))
    return pl.pallas_call(body, name=name, grid_spec=grid_spec, out_shape=_sds((N_CHIPS, 2, hR, C), dtype),
                          compiler_params=_params("parallel"))(chip_core, w)


def gather_start(lands, groups, name):
    n = len(lands)
    n_groups = len(groups)

    def body(*refs):
        ins = refs[:n]
        sems = refs[n:n + 2 * n_groups]
        token = refs[-1]
        mx, my, mc = _coords()
        chip = 2 * mx + my
        for g, members in enumerate(groups):
            send, recv = sems[2 * g], sems[2 * g + 1]
            for i, a in enumerate(members):
                mine = ins[a].at[chip, mc]
                for k, (px, py) in enumerate(_other_chips(mx, my)):
                    _remote(mine, mine, send.at[3 * i + k], recv.at[3 * i + k], (px, py, mc)).start()
        token[...] = jnp.zeros_like(token)

    sem_shapes = []
    for members in groups:
        sem_shapes += [pltpu.SemaphoreType.DMA((3 * len(members),))] * 2
    outs = pl.pallas_call(
        body, name=name, in_specs=[HBM] * n,
        out_specs=[SEM] * (2 * n_groups) + [HBM] * n + [pl.BlockSpec(memory_space=pltpu.VMEM)],
        out_shape=sem_shapes + [pltpu.HBM(a.shape, a.dtype) for a in lands] + [_sds((SUBLANES, LANES), F32)],
        input_output_aliases={a: 2 * n_groups + a for a in range(n)}, compiler_params=_SPLIT_COPY,
    )(*[_in_hbm(a) for a in lands])
    sems = [(outs[2 * g], outs[2 * g + 1]) for g in range(n_groups)]
    return sems, list(outs[2 * n_groups:2 * n_groups + n]), outs[-1]


def gather_forward(lands, sems, after, name):
    n = len(lands)

    def body(*refs):
        ins = refs[:n]
        send, recv = refs[n], refs[n + 1]
        fsend, frecv = refs[n + 3], refs[n + 4]
        mx, my, mc = _coords()
        for i in range(n):
            for k, (px, py) in enumerate(_other_chips(mx, my)):
                landed = ins[i].at[2 * px + py, mc]
                cp = _remote(landed, landed, send.at[3 * i + k], recv.at[3 * i + k], (px, py, mc))
                cp.wait_send()
                cp.wait_recv()
                _remote(landed, landed, fsend.at[3 * i + k], frecv.at[3 * i + k], (mx, my, 1 - mc)).start()

    outs = pl.pallas_call(
        body, name=name, in_specs=[HBM] * n + [SEM, SEM, ANY], out_specs=[SEM, SEM] + [HBM] * n,
        out_shape=[pltpu.SemaphoreType.DMA((3 * n,))] * 2 + [pltpu.HBM(a.shape, a.dtype) for a in lands],
        input_output_aliases={a: 2 + a for a in range(n)}, compiler_params=_SPLIT_COPY,
    )(*lands, sems[0], sems[1], after)
    return (outs[0], outs[1]), list(outs[2:])


def gather_wait(lands, sems, after, name):
    n = len(lands)

    def body(*refs):
        ins = refs[:n]
        fsend, frecv = refs[n], refs[n + 1]
        mx, my, mc = _coords()
        for i in range(n):
            for k, (px, py) in enumerate(_other_chips(mx, my)):
                theirs = ins[i].at[2 * px + py, 1 - mc]
                cp = _remote(theirs, theirs, fsend.at[3 * i + k], frecv.at[3 * i + k], (mx, my, 1 - mc))
                cp.wait_send()
                cp.wait_recv()

    outs = pl.pallas_call(
        body, name=name, in_specs=[HBM] * n + [SEM, SEM, ANY], out_specs=[HBM] * n,
        out_shape=[pltpu.HBM(a.shape, a.dtype) for a in lands],
        input_output_aliases={a: a for a in range(n)}, compiler_params=_SPLIT_COPY,
    )(*lands, sems[0], sems[1], after)
    return list(outs)


def _peers(mx, my, mc):
    return [(1 - mx if k & 4 else mx, 1 - my if k & 2 else my, 1 - mc if k & 1 else mc) for k in range(1, N_DEV)]


def devices_start(x, name):
    def body(x_ref, land_ref, send, recv, x_thru, land_thru):
        mx, my, mc = _coords()
        me = 4 * mx + 2 * my + mc
        for k, peer in enumerate(_peers(mx, my, mc)):
            _remote(x_ref, land_ref.at[me], send.at[k], recv.at[k], peer).start()

    land = lax.empty((N_DEV,) + x.shape, x.dtype)
    outs = pl.pallas_call(
        body, name=name, in_specs=[HBM, HBM], out_specs=[SEM, SEM, HBM, HBM],
        out_shape=[pltpu.SemaphoreType.DMA((N_DEV - 1,))] * 2 + [pltpu.HBM(x.shape, x.dtype), pltpu.HBM(land.shape, x.dtype)],
        input_output_aliases={0: 2, 1: 3}, compiler_params=_SPLIT_COPY,
    )(_in_hbm(x), _in_hbm(land))
    return (outs[0], outs[1]), outs[2], outs[3]


def devices_wait(x, land, sems, after, name):
    def body(x_ref, land_ref, send, recv, after_ref, x_thru, land_thru):
        mx, my, mc = _coords()
        for k, (px, py, pc) in enumerate(_peers(mx, my, mc)):
            cp = _remote(x_ref, land_ref.at[4 * px + 2 * py + pc], send.at[k], recv.at[k], (px, py, pc))
            cp.wait_send()
            cp.wait_recv()

    outs = pl.pallas_call(
        body, name=name, in_specs=[HBM, HBM, SEM, SEM, ANY], out_specs=[HBM, HBM],
        out_shape=[pltpu.HBM(x.shape, x.dtype), pltpu.HBM(land.shape, land.dtype)],
        input_output_aliases={0: 0, 1: 1}, compiler_params=_SPLIT_COPY,
    )(x, land, sems[0], sems[1], after)
    return outs[0], outs[1]


def device_sum(land, own, me, name):
    _, R, C = land.shape

    def body(s_ref, l_ref, o_ref_in, o_ref):
        acc = None
        for q in range(N_DEV):
            term = jnp.where(s_ref[0] == q, o_ref_in[...], l_ref[q])
            acc = term if acc is None else acc + term
        o_ref[...] = acc

    grid_spec = pltpu.PrefetchScalarGridSpec(
        num_scalar_prefetch=1, grid=(1,),
        in_specs=[pl.BlockSpec((N_DEV, R, C), lambda i, s: (0, 0, 0)), pl.BlockSpec((R, C), lambda i, s: (0, 0))],
        out_specs=pl.BlockSpec((R, C), lambda i, s: (0, 0)))
    return pl.pallas_call(body, name=name, grid_spec=grid_spec, out_shape=_sds((R, C), F32),
                          compiler_params=_params("arbitrary"))(me, land, own)


def reduce_send(grads, name):
    n = len(grads)

    def body(*refs):
        ins, lands = refs[:n], refs[n:2 * n]
        send, recv = refs[2 * n], refs[2 * n + 1]
        mx, my, mc = _coords()
        me = 4 * mx + 2 * my + mc
        for a in range(n):
            for k, (px, py, pc) in enumerate(_peers(mx, my, mc)):
                _remote(ins[a].at[2 * px + py, pc], lands[a].at[me], send.at[7 * a + k], recv.at[7 * a + k], (px, py, pc)).start()

    lands = [lax.empty((N_DEV,) + g.shape[2:], g.dtype) for g in grads]
    outs = pl.pallas_call(
        body, name=name, in_specs=[HBM] * (2 * n), out_specs=[SEM, SEM] + [HBM] * (2 * n),
        out_shape=[pltpu.SemaphoreType.DMA((7 * n,))] * 2 + [pltpu.HBM(a.shape, a.dtype) for a in grads + lands],
        input_output_aliases={a: 2 + a for a in range(2 * n)}, compiler_params=_SPLIT_COPY,
    )(*[_in_hbm(a) for a in grads + lands])
    return (outs[0], outs[1]), list(outs[2:2 + n]), list(outs[2 + n:])


def reduce_wait(grads, lands, sems, after, name):
    n = len(grads)

    def body(*refs):
        ins, zones = refs[:n], refs[n:2 * n]
        send, recv = refs[2 * n], refs[2 * n + 1]
        mx, my, mc = _coords()
        for a in range(n):
            for k, (px, py, pc) in enumerate(_peers(mx, my, mc)):
                cp = _remote(ins[a].at[2 * px + py, pc], zones[a].at[4 * px + 2 * py + pc], send.at[7 * a + k],
                             recv.at[7 * a + k], (px, py, pc))
                cp.wait_send()
                cp.wait_recv()

    outs = pl.pallas_call(
        body, name=name, in_specs=[HBM] * (2 * n) + [SEM, SEM, ANY], out_specs=[HBM] * (2 * n),
        out_shape=[pltpu.HBM(a.shape, a.dtype) for a in grads + lands],
        input_output_aliases={a: a for a in range(2 * n)}, compiler_params=_SPLIT_COPY,
    )(*grads, *lands, sems[0], sems[1], after)
    return list(outs[:n]), list(outs[n:])


def reduce_sum(land, grad, place, name, into=None, layer=None):
    _, hR, C = land.shape
    tr = hR
    while N_DEV * tr * C * 2 > (6 << 20) and tr % 32 == 0:
        tr //= 2

    def body(s_ref, l_ref, g_ref, *rest):
        o_ref = rest[-1]
        own = g_ref[...].astype(F32)
        acc = None
        for q in range(N_DEV):
            term = jnp.where(s_ref[2] == q, own, l_ref[q].astype(F32))
            acc = term if acc is None else acc + term
        o_ref[...] = acc

    in_specs = [pl.BlockSpec((N_DEV, tr, C), lambda i, s: (0, i, 0)),
                pl.BlockSpec((None, None, tr, C), lambda i, s: (s[0], s[1], i, 0))]
    args = [place, land, grad]
    aliases = {}
    if layer is None:
        out_spec = pl.BlockSpec((None, tr, C), lambda i, s: (s[1], i, 0))
        out_shape = _sds((2, hR, C), F32)
    else:
        out_spec = pl.BlockSpec((None, None, tr, C), lambda i, s: (layer, s[1], i, 0))
        out_shape = _sds((2, 2, hR, C), F32)
        if into is not None:
            in_specs.append(ANY)
            args.append(into)
            aliases = {3: 0}
    grid_spec = pltpu.PrefetchScalarGridSpec(num_scalar_prefetch=1, grid=(hR // tr,), in_specs=in_specs, out_specs=out_spec)
    return pl.pallas_call(body, name=name, grid_spec=grid_spec, out_shape=out_shape, input_output_aliases=aliases,
                          compiler_params=_params("arbitrary"))(*args)


def join_halves(arrays, name):
    n = len(arrays)
    pieces = [(a, l) for a, arr in enumerate(arrays) for l in (range(arr.shape[0]) if arr.ndim == 4 else [None])]

    def body(*refs):
        ins = refs[:n]
        send, recv = refs[2 * n:]
        mx, my, mc = _coords()

        def half(a, l, h):
            return ins[a].at[h] if l is None else ins[a].at[l, h]

        sends = [_remote(half(a, l, mc), half(a, l, mc), send.at[i], recv.at[i], (mx, my, 1 - mc))
                 for i, (a, l) in enumerate(pieces)]
        for cp in sends:
            cp.start()
        for i, (a, l) in enumerate(pieces):
            theirs = half(a, l, 1 - mc)
            _remote(theirs, theirs, send.at[i], recv.at[i], (mx, my, 1 - mc)).wait_recv()
        for cp in sends:
            cp.wait_send()

    return pl.pallas_call(
        body, name=name, in_specs=[ANY] * n, out_specs=[ANY] * n, out_shape=[_sds(a.shape, a.dtype) for a in arrays],
        input_output_aliases={a: a for a in range(n)},
        scratch_shapes=[pltpu.SemaphoreType.DMA((len(pieces),)), pltpu.SemaphoreType.DMA((len(pieces),))],
    )(*arrays)


LANES = 128
SUBLANES = 8


def _n_rows(shape):
    rows = -(-int(np.prod(shape)) // LANES)
    return -(-rows // SUBLANES) * SUBLANES


def _as_rows(a):
    flat = a.reshape(-1)
    rows = _n_rows(a.shape)
    return jnp.pad(flat, (0, rows * LANES - flat.shape[0])).reshape(rows, LANES)


def _pack(arrays):
    return jnp.concatenate([_as_rows(a) for a in arrays], axis=0)


def _unpack(rows, shapes):
    out, r0 = [], 0
    for s in shapes:
        n = _n_rows(s)
        out.append(rows[r0:r0 + n].reshape(-1)[:int(np.prod(s))].reshape(s))
        r0 += n
    return out


REPLICATED_SMALL = [("rel_bias", (32, 16)), ("even_norm", (1, 1024)), ("even_pool_w", (1, 4, 128, 128)),
                    ("even_pool_scale", (1, 512)), ("odd_q_norm", (1, 64)), ("odd_k_norm", (1, 64)),
                    ("ffn_norm", (2, 1024)), ("ffn_conv_b", (2, 5632))]
SHARDED_SMALL = [("even_conv_w", (1, 3, 128)), ("odd_norm", (1, 256)), ("ffn_conv_w", (2, 3, 1408))]
BIG = ["even_w_in", "even_w_out", "odd_w_qkv", "odd_w_o", "ffn_w_up", "ffn_w_down"]
WEIGHT_ORDER = ["rel_bias", "even_norm", "even_w_in", "even_conv_w", "even_pool_w", "even_pool_scale", "even_w_out",
                "odd_norm", "odd_w_qkv", "odd_q_norm", "odd_k_norm", "odd_w_o", "ffn_norm", "ffn_w_up", "ffn_conv_w",
                "ffn_conv_b", "ffn_w_down"]


def kernel(x, rel_bias, even_norm, even_w_in, even_conv_w, even_pool_w, even_pool_scale, even_w_out, odd_norm, odd_w_qkv, odd_q_norm, odd_k_norm, odd_w_o, ffn_norm, ffn_w_up, ffn_conv_w, ffn_conv_b, ffn_w_down, loss_target, m_rel_bias, m_even_norm, m_even_w_in, m_even_conv_w, m_even_pool_w, m_even_pool_scale, m_even_w_out, m_odd_norm, m_odd_w_qkv, m_odd_q_norm, m_odd_k_norm, m_odd_w_o, m_ffn_norm, m_ffn_w_up, m_ffn_conv_w, m_ffn_conv_b, m_ffn_w_down, v_rel_bias, v_even_norm, v_even_w_in, v_even_conv_w, v_even_pool_w, v_even_pool_scale, v_even_w_out, v_odd_norm, v_odd_w_qkv, v_odd_q_norm, v_odd_k_norm, v_odd_w_o, v_ffn_norm, v_ffn_w_up, v_ffn_conv_w, v_ffn_conv_b, v_ffn_w_down):
    W = dict(rel_bias=rel_bias, even_norm=even_norm, even_w_in=even_w_in, even_conv_w=even_conv_w, even_pool_w=even_pool_w,
             even_pool_scale=even_pool_scale, even_w_out=even_w_out, odd_norm=odd_norm, odd_w_qkv=odd_w_qkv,
             odd_q_norm=odd_q_norm, odd_k_norm=odd_k_norm, odd_w_o=odd_w_o, ffn_norm=ffn_norm, ffn_w_up=ffn_w_up,
             ffn_conv_w=ffn_conv_w, ffn_conv_b=ffn_conv_b, ffn_w_down=ffn_w_down)
    M1 = dict(rel_bias=m_rel_bias, even_norm=m_even_norm, even_w_in=m_even_w_in, even_conv_w=m_even_conv_w,
              even_pool_w=m_even_pool_w, even_pool_scale=m_even_pool_scale, even_w_out=m_even_w_out, odd_norm=m_odd_norm,
              odd_w_qkv=m_odd_w_qkv, odd_q_norm=m_odd_q_norm, odd_k_norm=m_odd_k_norm, odd_w_o=m_odd_w_o,
              ffn_norm=m_ffn_norm, ffn_w_up=m_ffn_w_up, ffn_conv_w=m_ffn_conv_w, ffn_conv_b=m_ffn_conv_b,
              ffn_w_down=m_ffn_w_down)
    M2 = dict(rel_bias=v_rel_bias, even_norm=v_even_norm, even_w_in=v_even_w_in, even_conv_w=v_even_conv_w,
              even_pool_w=v_even_pool_w, even_pool_scale=v_even_pool_scale, even_w_out=v_even_w_out, odd_norm=v_odd_norm,
              odd_w_qkv=v_odd_w_qkv, odd_q_norm=v_odd_q_norm, odd_k_norm=v_odd_k_norm, odd_w_o=v_odd_w_o,
              ffn_norm=v_ffn_norm, ffn_w_up=v_ffn_w_up, ffn_conv_w=v_ffn_conv_w, ffn_conv_b=v_ffn_conv_b,
              ffn_w_down=v_ffn_w_down)
    mx, my, mc = _coords()
    chip = 2 * mx + my
    me = 4 * mx + 2 * my + mc
    place = jnp.stack([chip, mc, me]).astype(jnp.int32)
    xs, target = x[0], loss_target[0]

    def halves(w):
        return w.reshape((w.shape[0], 2, w.shape[-2] // 2, w.shape[-1]))

    lands = [cast_into_slot(halves(even_w_in), 0, place, "cast_w_in"), cast_into_slot(halves(even_w_out), 0, place, "cast_w_out"),
             cast_into_slot(halves(ffn_w_up), 0, place, "cast_w_up0"), cast_into_slot(halves(ffn_w_down), 0, place, "cast_w_down0"),
             cast_into_slot(halves(odd_w_qkv), 0, place, "cast_w_qkv"), cast_into_slot(halves(odd_w_o), 0, place, "cast_w_o"),
             cast_into_slot(halves(ffn_w_up), 1, place, "cast_w_up1"), cast_into_slot(halves(ffn_w_down), 1, place, "cast_w_down1")]
    small_rows = jnp.pad(_pack([even_conv_w, odd_norm, ffn_conv_w]), ((0, SUBLANES), (0, 0)))
    lands.append(cast_into_slot(small_rows.reshape(1, 2, small_rows.shape[0] // 2, LANES), 0, place, "small_into_slot", dtype=F32))
    groups = [[0, 1, 8], [2, 3], [4, 5], [6, 7]]
    gather_sems, lands, token = gather_start(lands, groups, "gather_start")
    even_norm_after_start = even_norm + token[0:1, 0:1]

    def gathered(group, tag, after_landing, after_passing):
        mine = [lands[a] for a in groups[group]]
        sems, arrays = gather_forward(mine, gather_sems[group], after_landing, "gather_forward_" + tag)
        return gather_wait(arrays, sems, after_passing, "gather_wait_" + tag)

    pool_w = cast_bf16(even_pool_w[0], "cast_pool_w")
    gqk = jnp.stack([jnp.tile(odd_q_norm[0], N_HEADS), jnp.tile(odd_k_norm[0], N_HEADS),
                     jnp.ones((D_MODEL,), F32)])[:, None, :]
    bias = bias_expand(rel_bias.T, "bias_expand").reshape(6, N_HEADS, ATT_BLOCK, 2 * ATT_BLOCK)
    xn0 = rmsnorm_fwd(xs, even_norm_after_start, "even_norm")
    got = gathered(0, "even", bias, xn0)
    w_in = got[0].reshape(N_CHIPS, 1, D_MODEL, EVEN_IN // N_CHIPS)
    w_out = got[1].reshape(1, 1, D_MODEL, D_MODEL)
    small = got[2].reshape(N_CHIPS, small_rows.shape[0], LANES)
    conv_w_full = small[:, 0:3].transpose(1, 0, 2).reshape(3, A_WIDTH)
    odd_norm_full = small[:, 8:10].reshape(1, D_MODEL)
    ffn_cw_full = small[:, 16:82].reshape(N_CHIPS, 2, 3, 2 * D_FF // N_CHIPS).transpose(1, 2, 0, 3).reshape(2, 3, 2 * D_FF)

    def ffn_fwd(l, xin, xn):
        up = mm_nn(xn, w_up[l], f"ffn{l}_up", out_dtype=BF16)
        u, act = glu_fwd(up, ffn_cw_full[l], ffn_conv_b[l:l + 1], f"ffn{l}_glu")
        return act, (xin, xn, up, u, act)

    def ffn_weights(got):
        return got[0].reshape(N_CHIPS, 1, D_MODEL, 2 * D_FF // N_CHIPS), got[1].reshape(1, 1, D_FF, D_MODEL)

    w_up, w_down = [None, None], [None, None]
    proj = mm_nn(xn0, w_in, "even_in")
    mix = mixer_fwd(proj, conv_w_full, pool_w, even_pool_scale, "even_mixer")
    x1, xn1 = mm_res_norm(mix, w_out, xs, ffn_norm[0:1], "even_out")
    w_up[0], w_down[0] = ffn_weights(gathered(1, "ffn0", proj, x1))
    act0, ffn0 = ffn_fwd(0, x1, xn1)
    x2, xn2 = mm_res_norm(act0, w_down[0], x1, odd_norm_full, "ffn0_down")
    got = gathered(2, "odd", x1, x2)
    w_qkv = got[0].reshape(N_CHIPS, 1, D_MODEL, 3 * D_MODEL // N_CHIPS)
    w_o = got[1].reshape(1, 1, D_MODEL, D_MODEL)
    qkv = mm_nn(xn2, w_qkv, "odd_qkv")
    qkvn = qknorm_fwd(qkv, gqk, "odd_qknorm")
    att, lse = attn_fwd(qkvn, bias, "attn_fwd")
    x3, xn3 = mm_res_norm(att, w_o, x2, ffn_norm[1:2], "odd_out")
    w_up[1], w_down[1] = ffn_weights(gathered(3, "ffn1", x2, x3))
    act1, ffn1 = ffn_fwd(1, x3, xn3)
    dy, dyb, sq = mm_res_loss(act1, w_down[1], x3, target, "ffn1_down_loss")
    loss = lax.psum(0.5 * jnp.sum(sq) * (1.0 / D_MODEL), ("x", "y", "c"))

    def ffn_bwd(l, dy, dyb, saved):
        xin, xn, up, u, act = saved
        dw_down = mm_tn(act, dyb, f"ffn{l}_dw_down", J=1, tk=D_FF // 2, tm=1024)
        dact = mm_nt(dyb, w_down[l], f"ffn{l}_dact", tr=D_FF // 2, out_dtype=BF16, tm=1024)
        dup, dcw, dcb = glu_bwd(up, u, dact, ffn_cw_full[l], f"ffn{l}_glu_bwd")
        dw_up = mm_tn(xn, dup, f"ffn{l}_dw_up", J=N_CHIPS, tk=512, tm=1024, jb=2)
        dx, dxb, dg = mm_nt_norm_bwd(dup, w_up[l], xin, ffn_norm[l:l + 1], dy, f"ffn{l}_dx")
        return dx, dxb, (dw_down, dw_up, dcw, dcb, dg)

    def quarters(g):
        return g.reshape(N_CHIPS, 2, g.shape[0] * g.shape[1] // (2 * N_CHIPS), g.shape[-1])

    def reduce_start(grads, tag, then):
        sems, parts, zones = reduce_send([quarters(g) for g in grads], "reduce_send_" + tag)
        then, parts = lax.optimization_barrier((then, parts))
        return (sems, parts, zones), then

    dx3, dx3b, g_ffn1 = ffn_bwd(1, dy, dyb, ffn1)
    red_ffn1, (dx3, dx3b) = reduce_start([g_ffn1[1], g_ffn1[0]], "ffn1", (dx3, dx3b))
    dw_o = mm_tn(att, dx3b, "odd_dw_o", J=1, tk=512, tm=1024)
    datt = mm_nt(dx3b, w_o, "odd_datt", tr=D_MODEL, out_dtype=BF16)
    dq, dk, dv, dbias = attn_bwd(qkvn, att, datt, lse, bias, "attn_bwd")
    dqkv, dgqk = qknorm_bwd(qkv, dq, dk, dv, gqk, "odd_qknorm_bwd")
    dw_qkv = mm_tn(xn2, dqkv, "odd_dw_qkv", J=N_CHIPS, tk=512, tm=1024)
    red_odd, dqkv = reduce_start([dw_qkv, dw_o], "odd", dqkv)
    dx2, dx2b, dg_odd = mm_nt_norm_bwd(dqkv, w_qkv, x2, odd_norm_full, dx3, "odd_dx")
    dx1, dx1b, g_ffn0 = ffn_bwd(0, dx2, dx2b, ffn0)
    red_ffn0, (dx1, dx1b) = reduce_start([g_ffn0[1], g_ffn0[0]], "ffn0", (dx1, dx1b))
    dw_out = mm_tn(mix, dx1b, "even_dw_out", J=1, tk=512, tm=1024)
    dmix = mm_nt(dx1b, w_out, "even_dmix", tr=D_MODEL)
    dproj, dcw_even, dpw, dps = mixer_bwd(proj, dmix, conv_w_full, pool_w, even_pool_scale, "even_mixer_bwd")
    dw_in = mm_tn(xn0, dproj, "even_dw_in", J=N_CHIPS, tk=512, tm=1024)
    grad_x, _, dg_even = mm_nt_norm_bwd(dproj, w_in, xs, even_norm, dx1, "even_dx")
    d_rel = jnp.sum(bias_reduce(dbias.reshape(3, N_HEADS, 2 * ATT_BLOCK * ATT_BLOCK), "bias_reduce"), axis=0).T

    red_even, grad_x = reduce_start([dw_in, dw_out], "even", grad_x)

    dcw_sh = dcw_even.reshape(3, N_CHIPS, A_WIDTH // N_CHIPS).transpose(1, 0, 2)
    don_sh = dg_odd.reshape(N_CHIPS, D_MODEL // N_CHIPS)
    dfcw = jnp.stack([g_ffn0[2], g_ffn1[2]])
    dfcw_sh = dfcw.reshape(2, 3, N_CHIPS, 2 * D_FF // N_CHIPS).transpose(2, 0, 1, 3)
    rep_grads = [d_rel, dg_even, dpw[None], dps, _head_sum(dgqk[0]), _head_sum(dgqk[1]),
                 jnp.concatenate([g_ffn0[4], g_ffn1[4]], axis=0), jnp.concatenate([g_ffn0[3], g_ffn1[3]], axis=0)]
    rep_rows = _pack(rep_grads)
    shard_rows = jnp.concatenate([_pack([dcw_sh[j], don_sh[j], dfcw_sh[j]]) for j in range(N_CHIPS)], axis=0)
    n_rep, n_shard = rep_rows.shape[0], shard_rows.shape[0] // N_CHIPS
    small_sems, small_rows, small_land = devices_start(jnp.concatenate([rep_rows, shard_rows], axis=0), "small_grads_start")
    grad_x, small_rows = lax.optimization_barrier((grad_x, small_rows))

    def reduce_end(red, tag, after):
        sems, parts, zones = red
        parts, zones = reduce_wait(parts, zones, sems, after, "reduce_wait_" + tag)
        return zones, parts

    z_ffn1, p_ffn1 = reduce_end(red_ffn1, "ffn1", grad_x)
    z_odd, p_odd = reduce_end(red_odd, "odd", grad_x)
    r_qkv = reduce_sum(z_odd[0], p_odd[0], place, "reduce_sum_w_qkv")
    r_o = reduce_sum(z_odd[1], p_odd[1], place, "reduce_sum_w_o")
    r_up = reduce_sum(z_ffn1[0], p_ffn1[0], place, "reduce_sum_w_up1", layer=1)
    r_down = reduce_sum(z_ffn1[1], p_ffn1[1], place, "reduce_sum_w_down1", layer=1)
    r_qkv, r_o, r_up, r_down = lax.optimization_barrier((r_qkv, r_o, r_up, r_down))
    z_ffn0, p_ffn0 = reduce_end(red_ffn0, "ffn0", r_down)
    r_up = reduce_sum(z_ffn0[0], p_ffn0[0], place, "reduce_sum_w_up0", into=r_up, layer=0)
    r_down = reduce_sum(z_ffn0[1], p_ffn0[1], place, "reduce_sum_w_down0", into=r_down, layer=0)
    later = ["odd_w_qkv", "odd_w_o", "ffn_w_up", "ffn_w_down"]
    joined = join_halves([r_qkv, r_o, r_up, r_down], "grads_join_late_layers")
    G = {nm: g.reshape(W[nm].shape) for nm, g in zip(later, joined)}

    D_, NM, NV = {}, {}, {}

    def update(nm):
        as3 = lambda a: a.reshape((-1,) + a.shape[-2:])
        outs = adamw(as3(W[nm]), as3(G[nm]), as3(M1[nm]), as3(M2[nm]), "adamw_" + nm)
        D_[nm], NM[nm], NV[nm], G[nm] = [o.reshape(W[nm].shape) for o in outs]

    for nm in later:
        update(nm)
    z_even, p_even = reduce_end(red_even, "even", D_[later[-1]])
    joined = join_halves([reduce_sum(z_even[0], p_even[0], place, "reduce_sum_w_in"),
                          reduce_sum(z_even[1], p_even[1], place, "reduce_sum_w_out")], "grads_join_first_layer")
    for nm, g in zip(["even_w_in", "even_w_out"], joined):
        G[nm] = g.reshape(W[nm].shape)
        update(nm)
    small_rows, small_land = devices_wait(small_rows, small_land, small_sems, D_["even_w_out"], "small_grads_wait")
    small_sum = device_sum(small_land, small_rows, place[2:3], "small_grads_sum")
    mine = lax.dynamic_slice_in_dim(small_sum, n_rep + chip * n_shard, n_shard, axis=0)
    g_small = jnp.concatenate([small_sum[:n_rep], mine], axis=0)
    small_names = [n for n, _ in REPLICATED_SMALL + SHARDED_SMALL]
    small_shapes = [s for _, s in REPLICATED_SMALL + SHARDED_SMALL]
    G.update(dict(zip(small_names, _unpack(g_small, small_shapes))))
    packs = [_pack([d[n] for n in small_names])[None] for d in (W, M1, M2)]
    outs = adamw(packs[0], g_small[None], packs[1], packs[2], "adamw_small")
    for dst, o in zip((D_, NM, NV), outs[:3]):
        dst.update(dict(zip(small_names, _unpack(o[0], small_shapes))))

    return (loss, grad_x[None], *[G[n] for n in WEIGHT_ORDER], *[D_[n] for n in WEIGHT_ORDER],
            *[NM[n] for n in WEIGHT_ORDER], *[NV[n] for n in WEIGHT_ORDER])


def _head_sum(dg):
    return jnp.sum(dg.reshape(N_HEADS, HEAD_DIM), axis=0, keepdims=True)
```

```python
import functools
import math

import numpy as np
import jax
import jax.numpy as jnp
from jax import lax
from jax.experimental import pallas as pl
from jax.experimental.pallas import tpu as pltpu

F32 = jnp.float32
BF16 = jnp.bfloat16

D_MODEL = 1024
N_HEADS = 16
HEAD_DIM = 64
A_WIDTH = 512
POOL_WINDOWS = (2, 4, 8, 16)
POOL_GROUP = 128
EVEN_IN = 2048
D_FF = 2816
DILATED_PAIRS = ((128, 1), (512, 4), (2048, 16))
ATT_BLOCK = 128
N_REL_BUCKETS = 32
REL_MAX_DISTANCE = 2048
EPS = 1e-6
MASK_VALUE = -1e30
ADAM_LR, ADAM_B1, ADAM_B2, ADAM_EPS, ADAM_WD, ADAM_STEP = 0.001, 0.9, 0.999, 1e-08, 0.01, 10

VMEM_LIMIT_BYTES = 48 * 1024 * 1024
N_CHIPS = 4
N_DEV = 8
MESH = pl.DeviceIdType.MESH


def _params(*sem):
    return pltpu.CompilerParams(dimension_semantics=sem if sem else None, vmem_limit_bytes=VMEM_LIMIT_BYTES)


def _sds(shape, dtype):
    return jax.ShapeDtypeStruct(tuple(shape), dtype)


def cast_bf16(x, name, tr=None):
    lead, (R, C) = x.shape[:-2], x.shape[-2:]
    n = int(np.prod(lead)) if lead else 1
    x3 = x.reshape((n, R, C))
    tr = tr or R

    def body(x_ref, o_ref):
        o_ref[...] = x_ref[...].astype(BF16)

    out = pl.pallas_call(
        body, name=name, grid=(n, R // tr),
        in_specs=[pl.BlockSpec((None, tr, C), lambda i, r: (i, r, 0))],
        out_specs=pl.BlockSpec((None, tr, C), lambda i, r: (i, r, 0)),
        out_shape=_sds((n, R, C), BF16), compiler_params=_params("parallel", "parallel"),
    )(x3)
    return out.reshape(lead + (R, C))


def rmsnorm_fwd(x, g, name, ts=512):
    S, Dm = x.shape

    def body(x_ref, g_ref, o_ref):
        xv = x_ref[...]
        r = lax.rsqrt(jnp.mean(xv * xv, axis=-1, keepdims=True) + EPS)
        o_ref[...] = ((xv * r) * g_ref[...]).astype(BF16)

    return pl.pallas_call(
        body, name=name, grid=(S // ts,),
        in_specs=[pl.BlockSpec((ts, Dm), lambda i: (i, 0)), pl.BlockSpec((1, Dm), lambda i: (0, 0))],
        out_specs=pl.BlockSpec((ts, Dm), lambda i: (i, 0)),
        out_shape=_sds((S, Dm), BF16), compiler_params=_params("parallel"),
    )(x, g)


def mm_nn(a, w, name, layer=0, res=None, out_dtype=F32, tm=1024):
    M, K = a.shape
    J, _, _, Ns = w.shape

    def body(*refs):
        a_ref, w_ref = refs[0], refs[1]
        o_ref = refs[-1]
        acc = jnp.dot(a_ref[...], w_ref[...], preferred_element_type=F32)
        if res is not None:
            acc = refs[2][...] + acc
        o_ref[...] = acc.astype(o_ref.dtype)

    in_specs = [pl.BlockSpec((tm, K), lambda j, m: (m, 0)),
                pl.BlockSpec((None, None, K, Ns), lambda j, m: (j, layer, 0, 0))]
    args = [a, w]
    if res is not None:
        in_specs.append(pl.BlockSpec((tm, Ns), lambda j, m: (m, j)))
        args.append(res)
    return pl.pallas_call(
        body, name=name, grid=(J, M // tm), in_specs=in_specs,
        out_specs=pl.BlockSpec((tm, Ns), lambda j, m: (m, j)),
        out_shape=_sds((M, J * Ns), out_dtype), compiler_params=_params("parallel", "parallel"),
    )(*args)


def mm_res_norm(a, w, res, gain, name, tm=1024):
    M, K = a.shape
    Dm = w.shape[-1]

    def body(a_ref, w_ref, r_ref, g_ref, y_ref, yn_ref):
        y = r_ref[...] + jnp.dot(a_ref[...], w_ref[...], preferred_element_type=F32)
        y_ref[...] = y
        r = lax.rsqrt(jnp.mean(y * y, axis=-1, keepdims=True) + EPS)
        yn_ref[...] = ((y * r) * g_ref[...]).astype(BF16)

    row = pl.BlockSpec((tm, Dm), lambda m: (m, 0))
    return pl.pallas_call(
        body, name=name, grid=(M // tm,),
        in_specs=[pl.BlockSpec((tm, K), lambda m: (m, 0)),
                  pl.BlockSpec((None, None, K, Dm), lambda m: (0, 0, 0, 0), pipeline_mode=pl.Buffered(1)),
                  row, pl.BlockSpec((1, Dm), lambda m: (0, 0))],
        out_specs=[row, row], out_shape=[_sds((M, Dm), F32), _sds((M, Dm), BF16)],
        compiler_params=_params("parallel"),
    )(a, w, res, gain)


def mm_res_loss(a, w, res, target, name, tm=512):
    M, K = a.shape
    Dm = w.shape[-1]

    def body(a_ref, w_ref, r_ref, t_ref, d_ref, db_ref, s_ref):
        e = (r_ref[...] + jnp.dot(a_ref[...], w_ref[...], preferred_element_type=F32)) - t_ref[...]
        d = e * (1.0 / Dm)
        d_ref[...] = d
        db_ref[...] = d.astype(BF16)
        part = jnp.sum(e * e, axis=0, keepdims=True)

        @pl.when(pl.program_id(0) == 0)
        def _():
            s_ref[...] = part

        @pl.when(pl.program_id(0) > 0)
        def _():
            s_ref[...] += part

    row = pl.BlockSpec((tm, Dm), lambda m: (m, 0))
    return pl.pallas_call(
        body, name=name, grid=(M // tm,),
        in_specs=[pl.BlockSpec((tm, K), lambda m: (m, 0)),
                  pl.BlockSpec((None, None, K, Dm), lambda m: (0, 0, 0, 0), pipeline_mode=pl.Buffered(1)), row, row],
        out_specs=[row, row, pl.BlockSpec((1, Dm), lambda m: (0, 0))],
        out_shape=[_sds((M, Dm), F32), _sds((M, Dm), BF16), _sds((1, Dm), F32)],
        compiler_params=_params("arbitrary"),
    )(a, w, res, target)


def mm_nt(dy, w, name, tr, layer=0, out_dtype=F32, tm=512):
    M = dy.shape[0]
    J, _, R, Ns = w.shape
    dims = (((1,), (1,)), ((), ()))

    def body(dy_ref, w_ref, o_ref):
        acc = None
        for j in range(J):
            p = lax.dot_general(dy_ref[:, j * Ns:(j + 1) * Ns], w_ref[j], dims, preferred_element_type=F32)
            acc = p if acc is None else acc + p
        o_ref[...] = acc.astype(o_ref.dtype)

    return pl.pallas_call(
        body, name=name, grid=(R // tr, M // tm),
        in_specs=[pl.BlockSpec((tm, J * Ns), lambda r, m: (m, 0)),
                  pl.BlockSpec((J, None, tr, Ns), lambda r, m: (0, layer, r, 0))],
        out_specs=pl.BlockSpec((tm, tr), lambda r, m: (m, r)),
        out_shape=_sds((M, R), out_dtype),
        compiler_params=_params("parallel", "parallel"),
    )(dy, w)


def mm_nt_norm_bwd(dy, w, x, g, dres, name, layer=0, tm=512):
    M = dy.shape[0]
    J, _, Dm, Ns = w.shape
    dims = (((1,), (1,)), ((), ()))

    def body(dy_ref, w_ref, x_ref, g_ref, r_ref, dx_ref, dxb_ref, dg_ref):
        dxn = None
        for j in range(J):
            p = lax.dot_general(dy_ref[:, j * Ns:(j + 1) * Ns], w_ref[j], dims, preferred_element_type=F32)
            dxn = p if dxn is None else dxn + p
        xv = x_ref[...]
        r = lax.rsqrt(jnp.mean(xv * xv, axis=-1, keepdims=True) + EPS)
        gx = dxn * g_ref[...]
        dot = jnp.sum(gx * xv, axis=-1, keepdims=True)
        dx = r_ref[...] + r * gx - xv * ((r * r * r) * (dot * (1.0 / Dm)))
        dx_ref[...] = dx
        dxb_ref[...] = dx.astype(BF16)
        part = jnp.sum(dxn * (xv * r), axis=0, keepdims=True)

        @pl.when(pl.program_id(0) == 0)
        def _():
            dg_ref[...] = part

        @pl.when(pl.program_id(0) > 0)
        def _():
            dg_ref[...] += part

    row = pl.BlockSpec((tm, Dm), lambda m: (m, 0))
    vec = pl.BlockSpec((1, Dm), lambda m: (0, 0))
    return pl.pallas_call(
        body, name=name, grid=(M // tm,),
        in_specs=[pl.BlockSpec((tm, J * Ns), lambda m: (m, 0)),
                  pl.BlockSpec((J, None, Dm, Ns), lambda m: (0, layer, 0, 0), pipeline_mode=pl.Buffered(1)), row, vec, row],
        out_specs=[row, row, vec],
        out_shape=[_sds((M, Dm), F32), _sds((M, Dm), BF16), _sds((1, Dm), F32)],
        compiler_params=_params("arbitrary"),
    )(dy, w, x, g, dres)


def mm_tn(a, dy, name, J, tk, tm=512, jb=None):
    M, K = a.shape
    jb = jb or J
    Ns = dy.shape[1] // J
    N = jb * Ns
    n_m = M // tm
    dims = (((0,), (0,)), ((), ()))

    def body(a_ref, dy_ref, o_ref, acc_ref):
        p = lax.dot_general(a_ref[...], dy_ref[...], dims, preferred_element_type=F32)
        m = pl.program_id(2)

        @pl.when(m == 0)
        def _():
            acc_ref[...] = p

        @pl.when(m > 0)
        def _():
            acc_ref[...] += p

        @pl.when(m == n_m - 1)
        def _():
            for j in range(jb):
                o_ref[j] = acc_ref[:, j * Ns:(j + 1) * Ns].astype(BF16)

    return pl.pallas_call(
        body, name=name, grid=(J // jb, K // tk, n_m),
        in_specs=[pl.BlockSpec((tm, tk), lambda g, k, m: (m, k)), pl.BlockSpec((tm, N), lambda g, k, m: (m, g))],
        out_specs=pl.BlockSpec((jb, tk, Ns), lambda g, k, m: (g, k, 0)),
        out_shape=_sds((J, K, Ns), BF16), scratch_shapes=[pltpu.VMEM((tk, N), F32)],
        compiler_params=_params("parallel", "parallel", "arbitrary"),
    )(a, dy)


HALO = 16


def _shift_down(x, s):
    return pltpu.roll(x, s, 0)


def _shift_up(x, s):
    return pltpu.roll(x, x.shape[0] - s, 0)


def _conv3(z, cw):
    return (_shift_down(z, 2) * cw[0:1] + _shift_down(z, 1) * cw[1:2]) + z * cw[2:3]


def _window_count(first_row, n, k):
    t = first_row + lax.broadcasted_iota(jnp.int32, (n, 1), 0)
    return jnp.clip(t + 1, 1, k).astype(F32)


def mixer_fwd(proj, conv_w, pool_w, pool_scale, name, ts=256):
    S = proj.shape[0]
    n = ts + HALO

    def body(pm_ref, pb_ref, cw_ref, pw_ref, ps_ref, o_ref):
        i = pl.program_id(0)
        before = jnp.where(i > 0, pb_ref[...], 0.0)
        ext = jnp.concatenate([before, pm_ref[...]], axis=0)
        cw = cw_ref[...]
        z = ext[:, 2 * A_WIDTH:3 * A_WIDTH] * ext[:, 0:A_WIDTH]
        cz = _conv3(z, cw)
        ya = pm_ref[:, A_WIDTH:2 * A_WIDTH] * cz[HALO:]
        o_ref[:, 0:A_WIDTH] = ya.astype(BF16)
        for g, k in enumerate(POOL_WINDOWS):
            lo = 3 * A_WIDTH + g * POOL_GROUP
            p = ext[:, lo:lo + POOL_GROUP]
            w = p
            s = 1
            while s < k:
                w = w + _shift_down(w, s)
                s *= 2
            pooled = w / _window_count(i * ts - HALO, n, k) - p
            yb = jnp.dot(pooled[HALO:].astype(BF16), pw_ref[g], preferred_element_type=F32)
            yb = yb * ps_ref[:, g * POOL_GROUP:(g + 1) * POOL_GROUP]
            o_ref[:, A_WIDTH + g * POOL_GROUP:A_WIDTH + (g + 1) * POOL_GROUP] = yb.astype(BF16)

    hb = ts // HALO
    return pl.pallas_call(
        body, name=name, grid=(S // ts,),
        in_specs=[
            pl.BlockSpec((ts, EVEN_IN), lambda i: (i, 0)),
            pl.BlockSpec((HALO, EVEN_IN), lambda i: (jnp.maximum(i * hb - 1, 0), 0)),
            pl.BlockSpec((3, A_WIDTH), lambda i: (0, 0)),
            pl.BlockSpec((4, POOL_GROUP, POOL_GROUP), lambda i: (0, 0, 0)),
            pl.BlockSpec((1, 4 * POOL_GROUP), lambda i: (0, 0)),
        ],
        out_specs=pl.BlockSpec((ts, D_MODEL), lambda i: (i, 0)),
        out_shape=_sds((S, D_MODEL), BF16), compiler_params=_params("parallel"),
    )(proj, proj, conv_w, pool_w, pool_scale)


def mixer_bwd(proj, dmix, conv_w, pool_w, pool_scale, name, ts=256):
    S = proj.shape[0]
    n = ts + 2 * HALO
    nt = S // ts
    tn_dims = (((0,), (0,)), ((), ()))
    nt_dims = (((1,), (1,)), ((), ()))

    def body(pm_ref, pb_ref, pa_ref, dm_ref, da_ref, cw_ref, pw_ref, ps_ref, o_ref, dcw_ref, dpw_ref, dps_ref):
        i = pl.program_id(0)
        last = i == nt - 1
        before = jnp.where(i > 0, pb_ref[...], 0.0)
        after = jnp.where(last, 0.0, pa_ref[...])
        ext = jnp.concatenate([before, pm_ref[...], after], axis=0)
        dafter = jnp.where(last, 0.0, da_ref[...])
        dext = jnp.concatenate([jnp.zeros((HALO, D_MODEL), F32), dm_ref[...], dafter], axis=0)
        cw = cw_ref[...]
        main = slice(HALO, HALO + ts)

        @pl.when(i == 0)
        def _():
            dcw_ref[...] = jnp.zeros_like(dcw_ref)
            dpw_ref[...] = jnp.zeros_like(dpw_ref)
            dps_ref[...] = jnp.zeros_like(dps_ref)

        h, gb, gc = ext[:, 0:A_WIDTH], ext[:, A_WIDTH:2 * A_WIDTH], ext[:, 2 * A_WIDTH:3 * A_WIDTH]
        z = gc * h
        z1, z2 = _shift_down(z, 1), _shift_down(z, 2)
        cz = (z2 * cw[0:1] + z1 * cw[1:2]) + z * cw[2:3]
        dya = dext[:, 0:A_WIDTH]
        dcz = dya * gb
        dz = dcz * cw[2:3] + _shift_up(dcz, 1) * cw[1:2] + _shift_up(dcz, 2) * cw[0:1]
        o_ref[:, 0:A_WIDTH] = (dz * gc)[main].astype(BF16)
        o_ref[:, A_WIDTH:2 * A_WIDTH] = (dya * cz)[main].astype(BF16)
        o_ref[:, 2 * A_WIDTH:3 * A_WIDTH] = (dz * h)[main].astype(BF16)
        dczm = dcz[main]
        dcw_ref[0:1, :] += jnp.sum(dczm * z2[main], axis=0, keepdims=True)
        dcw_ref[1:2, :] += jnp.sum(dczm * z1[main], axis=0, keepdims=True)
        dcw_ref[2:3, :] += jnp.sum(dczm * z[main], axis=0, keepdims=True)

        for g, k in enumerate(POOL_WINDOWS):
            lo = 3 * A_WIDTH + g * POOL_GROUP
            cols = slice(g * POOL_GROUP, (g + 1) * POOL_GROUP)
            p = ext[:, lo:lo + POOL_GROUP]
            w = p
            s = 1
            while s < k:
                w = w + _shift_down(w, s)
                s *= 2
            cnt = _window_count(i * ts - HALO, n, k)
            pooled = (w / cnt - p)[main].astype(BF16)
            dyb = dext[:, A_WIDTH + g * POOL_GROUP:A_WIDTH + (g + 1) * POOL_GROUP]
            e = dyb * ps_ref[:, cols]
            pre = jnp.dot(pooled, pw_ref[g], preferred_element_type=F32)
            dps_ref[:, cols] += jnp.sum(dyb[main] * pre, axis=0, keepdims=True)
            dpw_ref[g] += lax.dot_general(pooled, e[main].astype(BF16), tn_dims, preferred_element_type=F32)
            dpooled = lax.dot_general(e.astype(BF16), pw_ref[g], nt_dims, preferred_element_type=F32)
            q = dpooled / cnt
            a = q
            s = 1
            while s < k:
                a = a + _shift_up(a, s)
                s *= 2
            o_ref[:, lo:lo + POOL_GROUP] = (a - dpooled)[main].astype(BF16)

    hb = ts // HALO
    nh = S // HALO
    before_map = lambda i: (jnp.maximum(i * hb - 1, 0), 0)
    after_map = lambda i: (jnp.minimum((i + 1) * hb, nh - 1), 0)
    full = lambda *shape: pl.BlockSpec(shape, lambda i: (0,) * len(shape))
    return pl.pallas_call(
        body, name=name, grid=(nt,),
        in_specs=[
            pl.BlockSpec((ts, EVEN_IN), lambda i: (i, 0)),
            pl.BlockSpec((HALO, EVEN_IN), before_map),
            pl.BlockSpec((HALO, EVEN_IN), after_map),
            pl.BlockSpec((ts, D_MODEL), lambda i: (i, 0)),
            pl.BlockSpec((HALO, D_MODEL), after_map),
            full(3, A_WIDTH), full(4, POOL_GROUP, POOL_GROUP), full(1, 4 * POOL_GROUP),
        ],
        out_specs=[pl.BlockSpec((ts, EVEN_IN), lambda i: (i, 0)), full(3, A_WIDTH), full(4, POOL_GROUP, POOL_GROUP),
                   full(1, 4 * POOL_GROUP)],
        out_shape=[_sds((S, EVEN_IN), BF16), _sds((3, A_WIDTH), F32), _sds((4, POOL_GROUP, POOL_GROUP), F32),
                   _sds((1, 4 * POOL_GROUP), F32)],
        compiler_params=_params("arbitrary"),
    )(proj, proj, proj, dmix, dmix, conv_w, pool_w, pool_scale)


FFN_HALO = 16
FFN_TC = 1408


def glu_fwd(up, conv_w, conv_b, name, ts=256):
    S = up.shape[0]
    nc = D_FF // FFN_TC

    def body(gm_ref, gb_ref, um_ref, ub_ref, cwg_ref, cwu_ref, cbg_ref, cbu_ref, ug_ref, uu_ref, o_ref):
        i = pl.program_id(0)

        def conv(m_ref, b_ref, cw_ref, cb_ref):
            before = jnp.where(i > 0, b_ref[...].astype(F32), 0.0)
            ext = jnp.concatenate([before, m_ref[...].astype(F32)], axis=0)
            return _conv3(ext, cw_ref[...])[FFN_HALO:] + cb_ref[...]

        gate = conv(gm_ref, gb_ref, cwg_ref, cbg_ref)
        upv = conv(um_ref, ub_ref, cwu_ref, cbu_ref)
        ug_ref[...] = gate.astype(BF16)
        uu_ref[...] = upv.astype(BF16)
        o_ref[...] = ((gate * (1.0 / (1.0 + jnp.exp(-gate)))) * upv).astype(BF16)

    hb = ts // FFN_HALO
    main = lambda off: pl.BlockSpec((ts, FFN_TC), lambda i, c: (i, c + off))
    halo = lambda off: pl.BlockSpec((FFN_HALO, FFN_TC), lambda i, c: (jnp.maximum(i * hb - 1, 0), c + off))
    cw = lambda off: pl.BlockSpec((3, FFN_TC), lambda i, c: (0, c + off))
    cb = lambda off: pl.BlockSpec((1, FFN_TC), lambda i, c: (0, c + off))
    ug, uu, act = pl.pallas_call(
        body, name=name, grid=(S // ts, nc),
        in_specs=[main(0), halo(0), main(nc), halo(nc), cw(0), cw(nc), cb(0), cb(nc)],
        out_specs=[pl.BlockSpec((ts, FFN_TC), lambda i, c: (i, c))] * 3,
        out_shape=[_sds((S, D_FF), BF16)] * 3, compiler_params=_params("parallel", "parallel"),
    )(up, up, up, up, conv_w, conv_w, conv_b, conv_b)
    return (ug, uu), act


def glu_bwd(up, u, da, conv_w, name, ts=256):
    S = up.shape[0]
    nc = D_FF // FFN_TC
    nt = S // ts
    W = 2 * D_FF

    def body(x_ref, gm_ref, ga_ref, um_ref, ua_ref, dm_ref, da_ref, cw_ref, dx_ref, dcw_ref, dcb_ref):
        i = pl.program_id(0)
        last = i == nt - 1

        @pl.when(i == 0)
        def _():
            dcw_ref[...] = jnp.zeros_like(dcw_ref)
            dcb_ref[...] = jnp.zeros_like(dcb_ref)

        def rows(m_ref, a_ref, cols):
            return jnp.concatenate([m_ref[:, cols], a_ref[:, cols]], axis=0).astype(F32)

        def back(d, cols):
            cw = cw_ref[:, cols]
            d1, d2 = _shift_up(d, 1), _shift_up(d, 2)
            dx_ref[:, cols] = ((d * cw[2:3] + d1 * cw[1:2]) + d2 * cw[0:1])[:ts].astype(BF16)
            x = x_ref[:, cols].astype(F32)
            dcb_ref[:, cols] += jnp.sum(d[:ts], axis=0, keepdims=True)
            dcw_ref[0:1, cols] += jnp.sum(d2[:ts] * x, axis=0, keepdims=True)
            dcw_ref[1:2, cols] += jnp.sum(d1[:ts] * x, axis=0, keepdims=True)
            dcw_ref[2:3, cols] += jnp.sum(d[:ts] * x, axis=0, keepdims=True)

        for c in range(nc):
            cols = slice(c * FFN_TC, (c + 1) * FFN_TC)
            ug, uu = rows(gm_ref, ga_ref, cols), rows(um_ref, ua_ref, cols)
            dae = rows(dm_ref, da_ref, cols)
            dae = jnp.where(last & (lax.broadcasted_iota(jnp.int32, dae.shape, 0) >= ts), 0.0, dae)
            sg = 1.0 / (1.0 + jnp.exp(-ug))
            duu = dae * (ug * sg)
            dug = (dae * uu) * (sg * (1.0 + ug * (1.0 - sg)))
            back(dug, cols)
            back(duu, slice(D_FF + c * FFN_TC, D_FF + (c + 1) * FFN_TC))

    hb = ts // FFN_HALO
    nh = S // FFN_HALO
    after_map = lambda i: (jnp.minimum((i + 1) * hb, nh - 1), 0)
    main = pl.BlockSpec((ts, D_FF), lambda i: (i, 0))
    after = pl.BlockSpec((FFN_HALO, D_FF), after_map)
    return pl.pallas_call(
        body, name=name, grid=(nt,),
        in_specs=[pl.BlockSpec((ts, W), lambda i: (i, 0)), main, after, main, after, main, after,
                  pl.BlockSpec((3, W), lambda i: (0, 0))],
        out_specs=[pl.BlockSpec((ts, W), lambda i: (i, 0)), pl.BlockSpec((3, W), lambda i: (0, 0)),
                   pl.BlockSpec((1, W), lambda i: (0, 0))],
        out_shape=[_sds((S, W), BF16), _sds((3, W), F32), _sds((1, W), F32)],
        compiler_params=_params("arbitrary"),
    )(up, u[0], u[0], u[1], u[1], da, da, conv_w)


def _head_mean_matrix():
    h = np.arange(D_MODEL) // HEAD_DIM
    return jnp.asarray((h[:, None] == h[None, :]).astype(np.float32) / HEAD_DIM, dtype=BF16)


def _head_mean(v, gm):
    return jnp.dot(v.astype(BF16), gm, preferred_element_type=F32)


def qknorm_fwd(qkv, gqk, name, ts=512):
    S = qkv.shape[0]

    def body(x_ref, g_ref, gm_ref, o_ref):
        part = pl.program_id(0)
        x = x_ref[...]

        @pl.when(part < 2)
        def _():
            r = lax.rsqrt(_head_mean(x * x, gm_ref[...]) + EPS)
            o_ref[...] = ((x * r) * g_ref[...]).astype(BF16)

        @pl.when(part == 2)
        def _():
            o_ref[...] = x.astype(BF16)

    return pl.pallas_call(
        body, name=name, grid=(3, S // ts),
        in_specs=[pl.BlockSpec((ts, D_MODEL), lambda p, i: (i, p)), pl.BlockSpec((None, 1, D_MODEL), lambda p, i: (p, 0, 0)),
                  pl.BlockSpec((D_MODEL, D_MODEL), lambda p, i: (0, 0))],
        out_specs=pl.BlockSpec((ts, D_MODEL), lambda p, i: (i, p)),
        out_shape=_sds((S, 3 * D_MODEL), BF16), compiler_params=_params("parallel", "parallel"),
    )(qkv, gqk, _head_mean_matrix())


def qknorm_bwd(qkv, dq, dk, dv, gqk, name, ts=256):
    S = qkv.shape[0]

    def body(x_ref, dq_ref, dk_ref, dv_ref, g_ref, gm_ref, o_ref, dg_ref):
        @pl.when(pl.program_id(0) == 0)
        def _():
            dg_ref[...] = jnp.zeros_like(dg_ref)

        gm = gm_ref[...]
        for part, d_ref in enumerate((dq_ref, dk_ref)):
            cols = slice(part * D_MODEL, (part + 1) * D_MODEL)
            x = x_ref[:, cols]
            d = d_ref[...]
            r = lax.rsqrt(_head_mean(x * x, gm) + EPS)
            gx = d * g_ref[part]
            o_ref[:, cols] = (r * gx - x * ((r * r * r) * _head_mean(gx * x, gm))).astype(BF16)
            dg_ref[part] += jnp.sum(d * (x * r), axis=0, keepdims=True)
        o_ref[:, 2 * D_MODEL:] = dv_ref[...].astype(BF16)

    row = pl.BlockSpec((ts, D_MODEL), lambda i: (i, 0))
    wide = pl.BlockSpec((ts, 3 * D_MODEL), lambda i: (i, 0))
    gains = pl.BlockSpec((3, 1, D_MODEL), lambda i: (0, 0, 0))
    return pl.pallas_call(
        body, name=name, grid=(S // ts,),
        in_specs=[wide, row, row, row, gains, pl.BlockSpec((D_MODEL, D_MODEL), lambda i: (0, 0))],
        out_specs=[wide, gains],
        out_shape=[_sds((S, 3 * D_MODEL), BF16), _sds((3, 1, D_MODEL), F32)],
        compiler_params=_params("arbitrary"),
    )(qkv, dq, dk, dv, gqk, _head_mean_matrix())


RESIDUES = 16


def _block_order(dil):
    runs = RESIDUES // dil
    slot = np.arange(ATT_BLOCK)
    return (slot % (ATT_BLOCK // runs)) * runs + slot // (ATT_BLOCK // runs)


def _bucket_tables():
    n = ATT_BLOCK
    max_exact = N_REL_BUCKETS // 2
    buckets, valids = [], []
    for _, dil in DILATED_PAIRS:
        order = _block_order(dil)
        a = order[:, None]
        c = np.concatenate([order, n + order])[None, :]
        first_half = (np.arange(2 * n) < n)[None, :]
        rel = a + n - c
        band = (rel >= 0) & (rel <= n)
        dist = np.clip(rel, 0, n) * dil
        dd = np.maximum(dist, 1).astype(np.float32)
        large = max_exact + (np.log(dd / np.float32(max_exact)) / np.float32(math.log(REL_MAX_DISTANCE / max_exact))
                             * np.float32(N_REL_BUCKETS - max_exact)).astype(np.int32)
        large = np.minimum(large, N_REL_BUCKETS - 1)
        buckets.append(np.where(dist < max_exact, dist, large).reshape(1, -1))
        valids.append(np.stack([(band & ~first_half).reshape(1, -1), band.reshape(1, -1)]))
    return np.stack(buckets).astype(np.int32), np.stack(valids).astype(np.int32)


BIAS_CHUNK = 8192


def _split3(x):
    a = x.astype(BF16)
    r = x - a.astype(F32)
    b = r.astype(BF16)
    c = (r - b.astype(F32)).astype(BF16)
    return a, b, c


def bias_expand(rel_bias_t, name):
    bucket, valid = _bucket_tables()
    nq = bucket.shape[-1]

    def body(t_ref, b_ref, v_ref, o_ref):
        onehot = (lax.broadcasted_iota(jnp.int32, (N_REL_BUCKETS, BIAS_CHUNK), 0) == b_ref[...]).astype(BF16)
        acc = None
        for term in _split3(t_ref[...]):
            p = jnp.dot(term, onehot, preferred_element_type=F32)
            acc = p if acc is None else acc + p
        o_ref[...] = jnp.where(v_ref[...] > 0, acc, MASK_VALUE)

    return pl.pallas_call(
        body, name=name, grid=(3, 2, nq // BIAS_CHUNK),
        in_specs=[pl.BlockSpec((N_HEADS, N_REL_BUCKETS), lambda b, v, c: (0, 0)),
                  pl.BlockSpec((None, 1, BIAS_CHUNK), lambda b, v, c: (b, 0, c)),
                  pl.BlockSpec((None, None, 1, BIAS_CHUNK), lambda b, v, c: (b, v, 0, c))],
        out_specs=pl.BlockSpec((None, None, N_HEADS, BIAS_CHUNK), lambda b, v, c: (b, v, 0, c)),
        out_shape=_sds((3, 2, N_HEADS, nq), F32), compiler_params=_params("parallel", "parallel", "parallel"),
    )(rel_bias_t, jnp.asarray(bucket), jnp.asarray(valid))


def bias_reduce(dbias, name):
    bucket, _ = _bucket_tables()
    nq = bucket.shape[-1]
    dims = (((1,), (1,)), ((), ()))

    def body(d_ref, b_ref, o_ref):
        onehot = (lax.broadcasted_iota(jnp.int32, (N_REL_BUCKETS, BIAS_CHUNK), 0) == b_ref[...]).astype(BF16)
        acc = None
        for term in _split3(d_ref[...]):
            p = lax.dot_general(term, onehot, dims, preferred_element_type=F32)
            acc = p if acc is None else acc + p

        @pl.when(pl.program_id(1) == 0)
        def _():
            o_ref[...] = acc

        @pl.when(pl.program_id(1) > 0)
        def _():
            o_ref[...] += acc

    return pl.pallas_call(
        body, name=name, grid=(3, nq // BIAS_CHUNK),
        in_specs=[pl.BlockSpec((None, N_HEADS, BIAS_CHUNK), lambda b, c: (b, 0, c)),
                  pl.BlockSpec((None, 1, BIAS_CHUNK), lambda b, c: (b, 0, c))],
        out_specs=pl.BlockSpec((None, N_HEADS, N_REL_BUCKETS), lambda b, c: (b, 0, 0)),
        out_shape=_sds((3, N_HEADS, N_REL_BUCKETS), F32), compiler_params=_params("parallel", "arbitrary"),
    )(dbias, jnp.asarray(bucket))


PAIR = 2 * HEAD_DIM
N_PAIRS = N_HEADS // 2
_NT = (((1,), (1,)), ((), ()))
_TN = (((0,), (0,)), ((), ()))


def _low_lanes(shape):
    return lax.broadcasted_iota(jnp.int32, shape, 1) < HEAD_DIM


ATTN_VMEM_LIMIT_BYTES = 56 * 1024 * 1024
BRANCH_ORDER = (2, 1, 0)


def _regroup(dst, src, L16):
    for r in range(RESIDUES):
        dst[pl.ds(r * L16, L16), :] = src[pl.ds(r, L16, stride=RESIDUES), :]


def _ungroup(dst, src, L16):
    for r in range(RESIDUES):
        dst[pl.ds(r, L16, stride=RESIDUES), :] = src[pl.ds(r * L16, L16), :]


def _branch_geometry(branch, S):
    dil = DILATED_PAIRS[branch][1]
    runs = RESIDUES // dil
    return dil, runs, ATT_BLOCK // runs, S // dil // ATT_BLOCK


def _block_rows(it, branch, S):
    dil, runs, run_len, n_blocks = _branch_geometry(branch, S)
    L16 = S // RESIDUES
    r, b = it // n_blocks, it % n_blocks
    prev = jnp.maximum(b - 1, 0)
    cur_rows = [pl.multiple_of((j * dil + r) * L16 + run_len * b, 8) for j in range(runs)]
    prev_rows = [pl.multiple_of((j * dil + r) * L16 + run_len * prev, 8) for j in range(runs)]
    return cur_rows, prev_rows, jnp.minimum(b, 1)


def _load_block(ref, rows, run_len):
    parts = [ref[pl.ds(o, run_len), :] for o in rows]
    return parts[0] if len(parts) == 1 else jnp.concatenate(parts, axis=0)


def _store_block(ref, rows, run_len, value, add=False):
    for j, o in enumerate(rows):
        part = value[j * run_len:(j + 1) * run_len]
        if add:
            ref[pl.ds(o, run_len), :] += part
        else:
            ref[pl.ds(o, run_len), :] = part


ATTN_FWD_UNROLL = 8
ATTN_BWD_UNROLL = 4


def _stack_heads(x, low):
    zero = jnp.zeros_like(x)
    return jnp.concatenate([jnp.where(low, x, zero), jnp.where(low, zero, x)], axis=0)


def _unstack_heads(y, low):
    return jnp.where(low, y[:ATT_BLOCK], y[ATT_BLOCK:])


def attn_fwd(qkvn, bias, name):
    S = qkvn.shape[0]
    L16 = S // RESIDUES
    n_iter = S // ATT_BLOCK

    def body(q_ref, k_ref, v_ref, b_ref, o_ref, lse_ref, stage, qp, kp, vp, acc_s, m_s, l_s):
        for src, dst in ((q_ref, qp), (k_ref, kp), (v_ref, vp)):
            stage[...] = src[...].astype(F32)
            _regroup(dst, stage, L16)
        low = _low_lanes((ATT_BLOCK, PAIR))

        for branch in BRANCH_ORDER:
            _, _, run_len, _ = _branch_geometry(branch, S)
            first = branch == BRANCH_ORDER[0]

            def step(it, carry, branch=branch, run_len=run_len, first=first):
                cur, prev, variant = _block_rows(it, branch, S)
                q = _load_block(qp, cur, run_len).astype(BF16)
                k = jnp.concatenate([_load_block(kp, prev, run_len), _load_block(kp, cur, run_len)], axis=0).astype(BF16)
                v = jnp.concatenate([_load_block(vp, prev, run_len), _load_block(vp, cur, run_len)], axis=0).astype(BF16)
                s = lax.dot_general(_stack_heads(q, low), k, _NT, preferred_element_type=F32) * (HEAD_DIM ** -0.5)
                s = s + b_ref[2 * branch + variant].reshape(2 * ATT_BLOCK, 2 * ATT_BLOCK)
                mx = jnp.max(s, axis=-1, keepdims=True)
                p = jnp.exp(s - mx)
                den = jnp.sum(p, axis=-1, keepdims=True)
                pv = jnp.dot(p.astype(BF16), v, preferred_element_type=F32)
                acc = _unstack_heads(pv, low)
                m = _unstack_heads(mx, low)
                l = _unstack_heads(den, low)
                if not first:
                    m_old = _load_block(m_s, cur, run_len)
                    m_new = jnp.maximum(m_old, m)
                    a_old, a_new = jnp.exp(m_old - m_new), jnp.exp(m - m_new)
                    acc = _load_block(acc_s, cur, run_len) * a_old + acc * a_new
                    l = _load_block(l_s, cur, run_len) * a_old + l * a_new
                    m = m_new
                _store_block(acc_s, cur, run_len, acc)
                _store_block(m_s, cur, run_len, m)
                _store_block(l_s, cur, run_len, l)
                return carry

            lax.fori_loop(0, n_iter, step, 0, unroll=ATTN_FWD_UNROLL)

        acc_s[...] = acc_s[...] / l_s[...]
        _ungroup(stage, acc_s, L16)
        o_ref[...] = stage[...].astype(BF16)
        m_s[...] = m_s[...] + jnp.log(l_s[...])
        _ungroup(lse_ref, m_s, L16)

    col = lambda part: pl.BlockSpec((S, PAIR), lambda hp: (0, part * N_PAIRS + hp))
    out = pl.BlockSpec((S, PAIR), lambda hp: (0, hp))
    return pl.pallas_call(
        body, name=name, grid=(N_PAIRS,),
        in_specs=[col(0), col(1), col(2), pl.BlockSpec((6, 2, ATT_BLOCK, 2 * ATT_BLOCK), lambda hp: (0, hp, 0, 0))],
        out_specs=[out, out], out_shape=[_sds((S, D_MODEL), BF16), _sds((S, D_MODEL), F32)],
        scratch_shapes=[pltpu.VMEM((S, PAIR), F32)] * 7,
        compiler_params=pltpu.CompilerParams(dimension_semantics=("parallel",), vmem_limit_bytes=ATTN_VMEM_LIMIT_BYTES),
    )(qkvn, qkvn, qkvn, bias)


def attn_bwd(qkvn, att, datt, lse, bias, name):
    S = qkvn.shape[0]
    L16 = S // RESIDUES
    n_iter = S // ATT_BLOCK
    TILE = 512

    def body(q_ref, k_ref, v_ref, o_ref, do_ref, lse_ref, b_ref, dq_ref, dk_ref, dv_ref, db_ref,
             qp, kp, vp, dop, ldp, dqp, dkp, dvp):
        stage = dqp
        for src, dst in ((q_ref, qp), (k_ref, kp), (v_ref, vp), (do_ref, dop)):
            stage[...] = src[...].astype(F32)
            _regroup(dst, stage, L16)

        def pack(i, carry):
            rows = pl.ds(pl.multiple_of(i * TILE, TILE), TILE)
            low = _low_lanes((TILE, PAIR))
            lane = lax.broadcasted_iota(jnp.int32, (TILE, PAIR), 1)
            prod = do_ref[rows, :].astype(F32) * o_ref[rows, :].astype(F32)
            d0 = jnp.sum(jnp.where(low, prod, 0.0), axis=-1, keepdims=True)
            d1 = jnp.sum(jnp.where(low, 0.0, prod), axis=-1, keepdims=True)
            stage[rows, :] = jnp.where((lane & (HEAD_DIM // 2)) == 0, lse_ref[rows, :], jnp.where(low, d0, d1))
            return carry

        lax.fori_loop(0, S // TILE, pack, 0)
        _regroup(ldp, stage, L16)
        dqp[...] = jnp.zeros_like(dqp)
        dkp[...] = jnp.zeros_like(dkp)
        dvp[...] = jnp.zeros_like(dvp)
        db_ref[...] = jnp.zeros_like(db_ref)
        low = _low_lanes((ATT_BLOCK, PAIR))

        for branch in BRANCH_ORDER:
            _, _, run_len, _ = _branch_geometry(branch, S)

            def step(it, carry, branch=branch, run_len=run_len):
                cur, prev, variant = _block_rows(it, branch, S)
                q = _load_block(qp, cur, run_len).astype(BF16)
                dout = _load_block(dop, cur, run_len).astype(BF16)
                ld = _load_block(ldp, cur, run_len)
                k = jnp.concatenate([_load_block(kp, prev, run_len), _load_block(kp, cur, run_len)], axis=0).astype(BF16)
                v = jnp.concatenate([_load_block(vp, prev, run_len), _load_block(vp, cur, run_len)], axis=0).astype(BF16)
                half = HEAD_DIM // 2
                lse2 = jnp.concatenate([ld[:, 0:1], ld[:, HEAD_DIM:HEAD_DIM + 1]], axis=0)
                delta2 = jnp.concatenate([ld[:, half:half + 1], ld[:, HEAD_DIM + half:HEAD_DIM + half + 1]], axis=0)
                q2, do2 = _stack_heads(q, low), _stack_heads(dout, low)
                s = lax.dot_general(q2, k, _NT, preferred_element_type=F32) * (HEAD_DIM ** -0.5)
                p = jnp.exp(s + b_ref[2 * branch + variant].reshape(2 * ATT_BLOCK, 2 * ATT_BLOCK) - lse2)
                dp = lax.dot_general(do2, v, _NT, preferred_element_type=F32)
                ds = p * (dp - delta2)
                db_ref[branch] += ds.reshape(2, ATT_BLOCK, 2 * ATT_BLOCK)
                dsb = (ds * (HEAD_DIM ** -0.5)).astype(BF16)
                dq = _unstack_heads(jnp.dot(dsb, k, preferred_element_type=F32), low)
                dk = lax.dot_general(dsb, q2, _TN, preferred_element_type=F32)
                dv = lax.dot_general(p.astype(BF16), do2, _TN, preferred_element_type=F32)
                _store_block(dqp, cur, run_len, dq, add=True)
                _store_block(dkp, prev, run_len, dk[:ATT_BLOCK], add=True)
                _store_block(dvp, prev, run_len, dv[:ATT_BLOCK], add=True)
                _store_block(dkp, cur, run_len, dk[ATT_BLOCK:], add=True)
                _store_block(dvp, cur, run_len, dv[ATT_BLOCK:], add=True)
                return carry

            lax.fori_loop(0, n_iter, step, 0, unroll=ATTN_BWD_UNROLL)

        _ungroup(dq_ref, dqp, L16)
        _ungroup(dk_ref, dkp, L16)
        _ungroup(dv_ref, dvp, L16)

    col = lambda part: pl.BlockSpec((S, PAIR), lambda hp: (0, part * N_PAIRS + hp))
    one = pl.BlockSpec((S, PAIR), lambda hp: (0, hp))
    return pl.pallas_call(
        body, name=name, grid=(N_PAIRS,),
        in_specs=[col(0), col(1), col(2), one, one, one,
                  pl.BlockSpec((6, 2, ATT_BLOCK, 2 * ATT_BLOCK), lambda hp: (0, hp, 0, 0))],
        out_specs=[one, one, one, pl.BlockSpec((3, 2, ATT_BLOCK, 2 * ATT_BLOCK), lambda hp: (0, hp, 0, 0))],
        out_shape=[_sds((S, D_MODEL), F32)] * 3 + [_sds((3, N_HEADS, ATT_BLOCK, 2 * ATT_BLOCK), F32)],
        scratch_shapes=[pltpu.VMEM((S, PAIR), F32)] * 8,
        compiler_params=pltpu.CompilerParams(dimension_semantics=("parallel",), vmem_limit_bytes=ATTN_VMEM_LIMIT_BYTES),
    )(qkvn, qkvn, qkvn, att, datt, lse, bias)


def adamw(w, g, m, v, name):
    n, R, C = w.shape

    def body(w_ref, g_ref, m_ref, v_ref, d_ref, nm_ref, nv_ref, go_ref):
        gv = g_ref[...]
        go_ref[...] = gv
        m2 = ADAM_B1 * m_ref[...] + (1.0 - ADAM_B1) * gv
        v2 = ADAM_B2 * v_ref[...] + (1.0 - ADAM_B2) * (gv * gv)
        m_hat = m2 / (1.0 - ADAM_B1 ** ADAM_STEP)
        v_hat = v2 / (1.0 - ADAM_B2 ** ADAM_STEP)
        d_ref[...] = -ADAM_LR * (m_hat / (jnp.sqrt(v_hat) + ADAM_EPS) + ADAM_WD * w_ref[...])
        nm_ref[...] = m2
        nv_ref[...] = v2

    tr = R
    while tr * C * 4 > (1 << 21) and tr % 16 == 0:
        tr //= 2
    spec = pl.BlockSpec((None, tr, C), lambda i, r: (i, r, 0))
    return pl.pallas_call(
        body, name=name, grid=(n, R // tr), in_specs=[spec] * 4, out_specs=[spec] * 4,
        out_shape=[_sds((n, R, C), F32)] * 4, compiler_params=_params("parallel", "parallel"),
    )(w, g, m, v)


ANY = pl.BlockSpec(memory_space=pl.ANY)


def _coords():
    return lax.axis_index("x"), lax.axis_index("y"), lax.axis_index("c")


def _other_chips(mx, my):
    return [(1 - mx, my), (mx, 1 - my), (1 - mx, 1 - my)]


def _remote(src, dst, send, recv, dev):
    return pltpu.make_async_remote_copy(src_ref=src, dst_ref=dst, send_sem=send, recv_sem=recv, device_id=dev,
                                        device_id_type=MESH)


HBM =pl.BlockSpec(memory_space=pltpu.HBM)
SEM = pl.BlockSpec(memory_space=pltpu.SEMAPHORE)
_SPLIT_COPY = pltpu.CompilerParams(has_side_effects=pltpu.SideEffectType.DATAFLOW_SIDE_EFFECTING)


def _in_hbm(a):
    return pltpu.with_memory_space_constraint(a, pltpu.HBM)


def cast_into_slot(w, layer, chip_core, name, dtype=BF16):
    _, _, hR, C = w.shape

    def body(s_ref, w_ref, o_ref):
        del s_ref
        o_ref[...] = w_ref[...].astype(dtype)

    grid_spec = pltpu.PrefetchScalarGridSpec(
        num_scalar_prefetch=1, grid=(2,),
        in_specs=[pl.BlockSpec((None, None, hR, C), lambda h, s: (layer, h, 0, 0))],
        out_specs=pl.BlockSpec((None, None, hR, C), lambda h, s: (s[0], h, 0, 0)))
    return pl.pallas_call(body, name=name, grid_spec=grid_spec, out_shape=_sds((N_CHIPS, 2, hR, C), dtype),
                          compiler_params=_params("parallel"))(chip_core, w)


def gather_start(lands, groups, name):
    n = len(lands)
    n_groups = len(groups)

    def body(*refs):
        ins = refs[:n]
        sems = refs[n:n + 2 * n_groups]
        token = refs[-1]
        mx, my, mc = _coords()
        chip = 2 * mx + my
        for g, members in enumerate(groups):
            send, recv = sems[2 * g], sems[2 * g + 1]
            for i, a in enumerate(members):
                mine = ins[a].at[chip, mc]
                for k, (px, py) in enumerate(_other_chips(mx, my)):
                    _remote(mine, mine, send.at[3 * i + k], recv.at[3 * i + k], (px, py, mc)).start()
        token[...] = jnp.zeros_like(token)

    sem_shapes = []
    for members in groups:
        sem_shapes += [pltpu.SemaphoreType.DMA((3 * len(members),))] * 2
    outs = pl.pallas_call(
        body, name=name, in_specs=[HBM] * n,
        out_specs=[SEM] * (2 * n_groups) + [HBM] * n + [pl.BlockSpec(memory_space=pltpu.VMEM)],
        out_shape=sem_shapes + [pltpu.HBM(a.shape, a.dtype) for a in lands] + [_sds((SUBLANES, LANES), F32)],
        input_output_aliases={a: 2 * n_groups + a for a in range(n)}, compiler_params=_SPLIT_COPY,
    )(*[_in_hbm(a) for a in lands])
    sems = [(outs[2 * g], outs[2 * g + 1]) for g in range(n_groups)]
    return sems, list(outs[2 * n_groups:2 * n_groups + n]), outs[-1]


def gather_forward(lands, sems, after, name):
    n = len(lands)

    def body(*refs):
        ins = refs[:n]
        send, recv = refs[n], refs[n + 1]
        fsend, frecv = refs[n + 3], refs[n + 4]
        mx, my, mc = _coords()
        for i in range(n):
            for k, (px, py) in enumerate(_other_chips(mx, my)):
                landed = ins[i].at[2 * px + py, mc]
                cp = _remote(landed, landed, send.at[3 * i + k], recv.at[3 * i + k], (px, py, mc))
                cp.wait_send()
                cp.wait_recv()
                _remote(landed, landed, fsend.at[3 * i + k], frecv.at[3 * i + k], (mx, my, 1 - mc)).start()

    outs = pl.pallas_call(
        body, name=name, in_specs=[HBM] * n + [SEM, SEM, ANY], out_specs=[SEM, SEM] + [HBM] * n,
        out_shape=[pltpu.SemaphoreType.DMA((3 * n,))] * 2 + [pltpu.HBM(a.shape, a.dtype) for a in lands],
        input_output_aliases={a: 2 + a for a in range(n)}, compiler_params=_SPLIT_COPY,
    )(*lands, sems[0], sems[1], after)
    return (outs[0], outs[1]), list(outs[2:])


def gather_wait(lands, sems, after, name):
    n = len(lands)

    def body(*refs):
        ins = refs[:n]
        fsend, frecv = refs[n], refs[n + 1]
        mx, my, mc = _coords()
        for i in range(n):
            for k, (px, py) in enumerate(_other_chips(mx, my)):
                theirs = ins[i].at[2 * px + py, 1 - mc]
                cp = _remote(theirs, theirs, fsend.at[3 * i + k], frecv.at[3 * i + k], (mx, my, 1 - mc))
                cp.wait_send()
                cp.wait_recv()

    outs = pl.pallas_call(
        body, name=name, in_specs=[HBM] * n + [SEM, SEM, ANY], out_specs=[HBM] * n,
        out_shape=[pltpu.HBM(a.shape, a.dtype) for a in lands],
        input_output_aliases={a: a for a in range(n)}, compiler_params=_SPLIT_COPY,
    )(*lands, sems[0], sems[1], after)
    return list(outs)


def _peers(mx, my, mc):
    return [(1 - mx if k & 4 else mx, 1 - my if k & 2 else my, 1 - mc if k & 1 else mc) for k in range(1, N_DEV)]


def devices_start(x, name):
    def body(x_ref, land_ref, send, recv, x_thru, land_thru):
        mx, my, mc = _coords()
        me = 4 * mx + 2 * my + mc
        for k, peer in enumerate(_peers(mx, my, mc)):
            _remote(x_ref, land_ref.at[me], send.at[k], recv.at[k], peer).start()

    land = lax.empty((N_DEV,) + x.shape, x.dtype)
    outs = pl.pallas_call(
        body, name=name, in_specs=[HBM, HBM], out_specs=[SEM, SEM, HBM, HBM],
        out_shape=[pltpu.SemaphoreType.DMA((N_DEV - 1,))] * 2 + [pltpu.HBM(x.shape, x.dtype), pltpu.HBM(land.shape, x.dtype)],
        input_output_aliases={0: 2, 1: 3}, compiler_params=_SPLIT_COPY,
    )(_in_hbm(x), _in_hbm(land))
    return (outs[0], outs[1]), outs[2], outs[3]


def devices_wait(x, land, sems, after, name):
    def body(x_ref, land_ref, send, recv, after_ref, x_thru, land_thru):
        mx, my, mc = _coords()
        for k, (px, py, pc) in enumerate(_peers(mx, my, mc)):
            cp = _remote(x_ref, land_ref.at[4 * px + 2 * py + pc], send.at[k], recv.at[k], (px, py, pc))
            cp.wait_send()
            cp.wait_recv()

    outs = pl.pallas_call(
        body, name=name, in_specs=[HBM, HBM, SEM, SEM, ANY], out_specs=[HBM, HBM],
        out_shape=[pltpu.HBM(x.shape, x.dtype), pltpu.HBM(land.shape, land.dtype)],
        input_output_aliases={0: 0, 1: 1}, compiler_params=_SPLIT_COPY,
    )(x, land, sems[0], sems[1], after)
    return outs[0], outs[1]


def device_sum(land, own, me, name):
    _, R, C = land.shape

    def body(s_ref, l_ref, o_ref_in, o_ref):
        acc = None
        for q in range(N_DEV):
            term = jnp.where(s_ref[0] == q, o_ref_in[...], l_ref[q])
            acc = term if acc is None else acc + term
        o_ref[...] = acc

    grid_spec = pltpu.PrefetchScalarGridSpec(
        num_scalar_prefetch=1, grid=(1,),
        in_specs=[pl.BlockSpec((N_DEV, R, C), lambda i, s: (0, 0, 0)), pl.BlockSpec((R, C), lambda i, s: (0, 0))],
        out_specs=pl.BlockSpec((R, C), lambda i, s: (0, 0)))
    return pl.pallas_call(body, name=name, grid_spec=grid_spec, out_shape=_sds((R, C), F32),
                          compiler_params=_params("arbitrary"))(me, land, own)


def reduce_send(grads, name):
    n = len(grads)

    def body(*refs):
        ins, lands = refs[:n], refs[n:2 * n]
        send, recv = refs[2 * n], refs[2 * n + 1]
        mx, my, mc = _coords()
        me = 4 * mx + 2 * my + mc
        for a in range(n):
            for k, (px, py, pc) in enumerate(_peers(mx, my, mc)):
                _remote(ins[a].at[2 * px + py, pc], lands[a].at[me], send.at[7 * a + k], recv.at[7 * a + k], (px, py, pc)).start()

    lands = [lax.empty((N_DEV,) + g.shape[2:], g.dtype) for g in grads]
    outs = pl.pallas_call(
        body, name=name, in_specs=[HBM] * (2 * n), out_specs=[SEM, SEM] + [HBM] * (2 * n),
        out_shape=[pltpu.SemaphoreType.DMA((7 * n,))] * 2 + [pltpu.HBM(a.shape, a.dtype) for a in grads + lands],
        input_output_aliases={a: 2 + a for a in range(2 * n)}, compiler_params=_SPLIT_COPY,
    )(*[_in_hbm(a) for a in grads + lands])
    return (outs[0], outs[1]), list(outs[2:2 + n]), list(outs[2 + n:])


def reduce_wait(grads, lands, sems, after, name):
    n = len(grads)

    def body(*refs):
        ins, zones = refs[:n], refs[n:2 * n]
        send, recv = refs[2 * n], refs[2 * n + 1]
        mx, my, mc = _coords()
        for a in range(n):
            for k, (px, py, pc) in enumerate(_peers(mx, my, mc)):
                cp = _remote(ins[a].at[2 * px + py, pc], zones[a].at[4 * px + 2 * py + pc], send.at[7 * a + k],
                             recv.at[7 * a + k], (px, py, pc))
                cp.wait_send()
                cp.wait_recv()

    outs = pl.pallas_call(
        body, name=name, in_specs=[HBM] * (2 * n) + [SEM, SEM, ANY], out_specs=[HBM] * (2 * n),
        out_shape=[pltpu.HBM(a.shape, a.dtype) for a in grads + lands],
        input_output_aliases={a: a for a in range(2 * n)}, compiler_params=_SPLIT_COPY,
    )(*grads, *lands, sems[0], sems[1], after)
    return list(outs[:n]), list(outs[n:])


def reduce_sum(land, grad, place, name, into=None, layer=None):
    _, hR, C = land.shape
    tr = hR
    while N_DEV * tr * C * 2 > (6 << 20) and tr % 32 == 0:
        tr //= 2

    def body(s_ref, l_ref, g_ref, *rest):
        o_ref = rest[-1]
        own = g_ref[...].astype(F32)
        acc = None
        for q in range(N_DEV):
            term = jnp.where(s_ref[2] == q, own, l_ref[q].astype(F32))
            acc = term if acc is None else acc + term
        o_ref[...] = acc

    in_specs = [pl.BlockSpec((N_DEV, tr, C), lambda i, s: (0, i, 0)),
                pl.BlockSpec((None, None, tr, C), lambda i, s: (s[0], s[1], i, 0))]
    args = [place, land, grad]
    aliases = {}
    if layer is None:
        out_spec = pl.BlockSpec((None, tr, C), lambda i, s: (s[1], i, 0))
        out_shape = _sds((2, hR, C), F32)
    else:
        out_spec = pl.BlockSpec((None, None, tr, C), lambda i, s: (layer, s[1], i, 0))
        out_shape = _sds((2, 2, hR, C), F32)
        if into is not None:
            in_specs.append(ANY)
            args.append(into)
            aliases = {3: 0}
    grid_spec = pltpu.PrefetchScalarGridSpec(num_scalar_prefetch=1, grid=(hR // tr,), in_specs=in_specs, out_specs=out_spec)
    return pl.pallas_call(body, name=name, grid_spec=grid_spec, out_shape=out_shape, input_output_aliases=aliases,
                          compiler_params=_params("arbitrary"))(*args)


def join_halves(arrays, name):
    n = len(arrays)
    pieces = [(a, l) for a, arr in enumerate(arrays) for l in (range(arr.shape[0]) if arr.ndim == 4 else [None])]

    def body(*refs):
        ins = refs[:n]
        send, recv = refs[2 * n:]
        mx, my, mc = _coords()

        def half(a, l, h):
            return ins[a].at[h] if l is None else ins[a].at[l, h]

        sends = [_remote(half(a, l, mc), half(a, l, mc), send.at[i], recv.at[i], (mx, my, 1 - mc))
                 for i, (a, l) in enumerate(pieces)]
        for cp in sends:
            cp.start()
        for i, (a, l) in enumerate(pieces):
            theirs = half(a, l, 1 - mc)
            _remote(theirs, theirs, send.at[i], recv.at[i], (mx, my, 1 - mc)).wait_recv()
        for cp in sends:
            cp.wait_send()

    return pl.pallas_call(
        body, name=name, in_specs=[ANY] * n, out_specs=[ANY] * n, out_shape=[_sds(a.shape, a.dtype) for a in arrays],
        input_output_aliases={a: a for a in range(n)},
        scratch_shapes=[pltpu.SemaphoreType.DMA((len(pieces),)), pltpu.SemaphoreType.DMA((len(pieces),))],
    )(*arrays)


LANES = 128
SUBLANES = 8


def _n_rows(shape):
    rows = -(-int(np.prod(shape)) // LANES)
    return -(-rows // SUBLANES) * SUBLANES


def _as_rows(a):
    flat = a.reshape(-1)
    rows = _n_rows(a.shape)
    return jnp.pad(flat, (0, rows * LANES - flat.shape[0])).reshape(rows, LANES)


def _pack(arrays):
    return jnp.concatenate([_as_rows(a) for a in arrays], axis=0)


def _unpack(rows, shapes):
    out, r0 = [], 0
    for s in shapes:
        n = _n_rows(s)
        out.append(rows[r0:r0 + n].reshape(-1)[:int(np.prod(s))].reshape(s))
        r0 += n
    return out


REPLICATED_SMALL = [("rel_bias", (32, 16)), ("even_norm", (1, 1024)), ("even_pool_w", (1, 4, 128, 128)),
                    ("even_pool_scale", (1, 512)), ("odd_q_norm", (1, 64)), ("odd_k_norm", (1, 64)),
                    ("ffn_norm", (2, 1024)), ("ffn_conv_b", (2, 5632))]
SHARDED_SMALL = [("even_conv_w", (1, 3, 128)), ("odd_norm", (1, 256)), ("ffn_conv_w", (2, 3, 1408))]
BIG = ["even_w_in", "even_w_out", "odd_w_qkv", "odd_w_o", "ffn_w_up", "ffn_w_down"]
WEIGHT_ORDER = ["rel_bias", "even_norm", "even_w_in", "even_conv_w", "even_pool_w", "even_pool_scale", "even_w_out",
                "odd_norm", "odd_w_qkv", "odd_q_norm", "odd_k_norm", "odd_w_o", "ffn_norm", "ffn_w_up", "ffn_conv_w",
                "ffn_conv_b", "ffn_w_down"]


def kernel(x, rel_bias, even_norm, even_w_in, even_conv_w, even_pool_w, even_pool_scale, even_w_out, odd_norm, odd_w_qkv, odd_q_norm, odd_k_norm, odd_w_o, ffn_norm, ffn_w_up, ffn_conv_w, ffn_conv_b, ffn_w_down, loss_target, m_rel_bias, m_even_norm, m_even_w_in, m_even_conv_w, m_even_pool_w, m_even_pool_scale, m_even_w_out, m_odd_norm, m_odd_w_qkv, m_odd_q_norm, m_odd_k_norm, m_odd_w_o, m_ffn_norm, m_ffn_w_up, m_ffn_conv_w, m_ffn_conv_b, m_ffn_w_down, v_rel_bias, v_even_norm, v_even_w_in, v_even_conv_w, v_even_pool_w, v_even_pool_scale, v_even_w_out, v_odd_norm, v_odd_w_qkv, v_odd_q_norm, v_odd_k_norm, v_odd_w_o, v_ffn_norm, v_ffn_w_up, v_ffn_conv_w, v_ffn_conv_b, v_ffn_w_down):
    W = dict(rel_bias=rel_bias, even_norm=even_norm, even_w_in=even_w_in, even_conv_w=even_conv_w, even_pool_w=even_pool_w,
             even_pool_scale=even_pool_scale, even_w_out=even_w_out, odd_norm=odd_norm, odd_w_qkv=odd_w_qkv,
             odd_q_norm=odd_q_norm, odd_k_norm=odd_k_norm, odd_w_o=odd_w_o, ffn_norm=ffn_norm, ffn_w_up=ffn_w_up,
             ffn_conv_w=ffn_conv_w, ffn_conv_b=ffn_conv_b, ffn_w_down=ffn_w_down)
    M1 = dict(rel_bias=m_rel_bias, even_norm=m_even_norm, even_w_in=m_even_w_in, even_conv_w=m_even_conv_w,
              even_pool_w=m_even_pool_w, even_pool_scale=m_even_pool_scale, even_w_out=m_even_w_out, odd_norm=m_odd_norm,
              odd_w_qkv=m_odd_w_qkv, odd_q_norm=m_odd_q_norm, odd_k_norm=m_odd_k_norm, odd_w_o=m_odd_w_o,
              ffn_norm=m_ffn_norm, ffn_w_up=m_ffn_w_up, ffn_conv_w=m_ffn_conv_w, ffn_conv_b=m_ffn_conv_b,
              ffn_w_down=m_ffn_w_down)
    M2 = dict(rel_bias=v_rel_bias, even_norm=v_even_norm, even_w_in=v_even_w_in, even_conv_w=v_even_conv_w,
              even_pool_w=v_even_pool_w, even_pool_scale=v_even_pool_scale, even_w_out=v_even_w_out, odd_norm=v_odd_norm,
              odd_w_qkv=v_odd_w_qkv, odd_q_norm=v_odd_q_norm, odd_k_norm=v_odd_k_norm, odd_w_o=v_odd_w_o,
              ffn_norm=v_ffn_norm, ffn_w_up=v_ffn_w_up, ffn_conv_w=v_ffn_conv_w, ffn_conv_b=v_ffn_conv_b,
              ffn_w_down=v_ffn_w_down)
    mx, my, mc = _coords()
    chip = 2 * mx + my
    me = 4 * mx + 2 * my + mc
    place = jnp.stack([chip, mc, me]).astype(jnp.int32)
    xs, target = x[0], loss_target[0]

    def halves(w):
        return w.reshape((w.shape[0], 2, w.shape[-2] // 2, w.shape[-1]))

    lands = [cast_into_slot(halves(even_w_in), 0, place, "cast_w_in"), cast_into_slot(halves(even_w_out), 0, place, "cast_w_out"),
             cast_into_slot(halves(ffn_w_up), 0, place, "cast_w_up0"), cast_into_slot(halves(ffn_w_down), 0, place, "cast_w_down0"),
             cast_into_slot(halves(odd_w_qkv), 0, place, "cast_w_qkv"), cast_into_slot(halves(odd_w_o), 0, place, "cast_w_o"),
             cast_into_slot(halves(ffn_w_up), 1, place, "cast_w_up1"), cast_into_slot(halves(ffn_w_down), 1, place, "cast_w_down1")]
    small_rows = jnp.pad(_pack([even_conv_w, odd_norm, ffn_conv_w]), ((0, SUBLANES), (0, 0)))
    lands.append(cast_into_slot(small_rows.reshape(1, 2, small_rows.shape[0] // 2, LANES), 0, place, "small_into_slot", dtype=F32))
    groups = [[0, 1, 8], [2, 3], [4, 5], [6, 7]]
    gather_sems, lands, token = gather_start(lands, groups, "gather_start")
    even_norm_after_start = even_norm + token[0:1, 0:1]

    def gathered(group, tag, after_landing, after_passing):
        mine = [lands[a] for a in groups[group]]
        sems, arrays = gather_forward(mine, gather_sems[group], after_landing, "gather_forward_" + tag)
        return gather_wait(arrays, sems, after_passing, "gather_wait_" + tag)

    pool_w = cast_bf16(even_pool_w[0], "cast_pool_w")
    gqk = jnp.stack([jnp.tile(odd_q_norm[0], N_HEADS), jnp.tile(odd_k_norm[0], N_HEADS),
                     jnp.ones((D_MODEL,), F32)])[:, None, :]
    bias = bias_expand(rel_bias.T, "bias_expand").reshape(6, N_HEADS, ATT_BLOCK, 2 * ATT_BLOCK)
    xn0 = rmsnorm_fwd(xs, even_norm_after_start, "even_norm")
    got = gathered(0, "even", bias, xn0)
    w_in = got[0].reshape(N_CHIPS, 1, D_MODEL, EVEN_IN // N_CHIPS)
    w_out = got[1].reshape(1, 1, D_MODEL, D_MODEL)
    small = got[2].reshape(N_CHIPS, small_rows.shape[0], LANES)
    conv_w_full = small[:, 0:3].transpose(1, 0, 2).reshape(3, A_WIDTH)
    odd_norm_full = small[:, 8:10].reshape(1, D_MODEL)
    ffn_cw_full = small[:, 16:82].reshape(N_CHIPS, 2, 3, 2 * D_FF // N_CHIPS).transpose(1, 2, 0, 3).reshape(2, 3, 2 * D_FF)

    def ffn_fwd(l, xin, xn):
        up = mm_nn(xn, w_up[l], f"ffn{l}_up", out_dtype=BF16)
        u, act = glu_fwd(up, ffn_cw_full[l], ffn_conv_b[l:l + 1], f"ffn{l}_glu")
        return act, (xin, xn, up, u, act)

    def ffn_weights(got):
        return got[0].reshape(N_CHIPS, 1, D_MODEL, 2 * D_FF // N_CHIPS), got[1].reshape(1, 1, D_FF, D_MODEL)

    w_up, w_down = [None, None], [None, None]
    proj = mm_nn(xn0, w_in, "even_in")
    mix = mixer_fwd(proj, conv_w_full, pool_w, even_pool_scale, "even_mixer")
    x1, xn1 = mm_res_norm(mix, w_out, xs, ffn_norm[0:1], "even_out")
    w_up[0], w_down[0] = ffn_weights(gathered(1, "ffn0", proj, x1))
    act0, ffn0 = ffn_fwd(0, x1, xn1)
    x2, xn2 = mm_res_norm(act0, w_down[0], x1, odd_norm_full, "ffn0_down")
    got = gathered(2, "odd", x1, x2)
    w_qkv = got[0].reshape(N_CHIPS, 1, D_MODEL, 3 * D_MODEL // N_CHIPS)
    w_o = got[1].reshape(1, 1, D_MODEL, D_MODEL)
    qkv = mm_nn(xn2, w_qkv, "odd_qkv")
    qkvn = qknorm_fwd(qkv, gqk, "odd_qknorm")
    att, lse = attn_fwd(qkvn, bias, "attn_fwd")
    x3, xn3 = mm_res_norm(att, w_o, x2, ffn_norm[1:2], "odd_out")
    w_up[1], w_down[1] = ffn_weights(gathered(3, "ffn1", x2, x3))
    act1, ffn1 = ffn_fwd(1, x3, xn3)
    dy, dyb, sq = mm_res_loss(act1, w_down[1], x3, target, "ffn1_down_loss")
    loss = lax.psum(0.5 * jnp.sum(sq) * (1.0 / D_MODEL), ("x", "y", "c"))

    def ffn_bwd(l, dy, dyb, saved):
        xin, xn, up, u, act = saved
        dw_down = mm_tn(act, dyb, f"ffn{l}_dw_down", J=1, tk=D_FF // 2, tm=1024)
        dact = mm_nt(dyb, w_down[l], f"ffn{l}_dact", tr=D_FF // 2, out_dtype=BF16, tm=1024)
        dup, dcw, dcb = glu_bwd(up, u, dact, ffn_cw_full[l], f"ffn{l}_glu_bwd")
        dw_up = mm_tn(xn, dup, f"ffn{l}_dw_up", J=N_CHIPS, tk=512, tm=1024, jb=2)
        dx, dxb, dg = mm_nt_norm_bwd(dup, w_up[l], xin, ffn_norm[l:l + 1], dy, f"ffn{l}_dx")
        return dx, dxb, (dw_down, dw_up, dcw, dcb, dg)

    def quarters(g):
        return g.reshape(N_CHIPS, 2, g.shape[0] * g.shape[1] // (2 * N_CHIPS), g.shape[-1])

    def reduce_start(grads, tag, then):
        sems, parts, zones = reduce_send([quarters(g) for g in grads], "reduce_send_" + tag)
        then, parts = lax.optimization_barrier((then, parts))
        return (sems, parts, zones), then

    dx3, dx3b, g_ffn1 = ffn_bwd(1, dy, dyb, ffn1)
    red_ffn1, (dx3, dx3b) = reduce_start([g_ffn1[1], g_ffn1[0]], "ffn1", (dx3, dx3b))
    dw_o = mm_tn(att, dx3b, "odd_dw_o", J=1, tk=512, tm=1024)
    datt = mm_nt(dx3b, w_o, "odd_datt", tr=D_MODEL, out_dtype=BF16)
    dq, dk, dv, dbias = attn_bwd(qkvn, att, datt, lse, bias, "attn_bwd")
    dqkv, dgqk = qknorm_bwd(qkv, dq, dk, dv, gqk, "odd_qknorm_bwd")
    dw_qkv = mm_tn(xn2, dqkv, "odd_dw_qkv", J=N_CHIPS, tk=512, tm=1024)
    red_odd, dqkv = reduce_start([dw_qkv, dw_o], "odd", dqkv)
    dx2, dx2b, dg_odd = mm_nt_norm_bwd(dqkv, w_qkv, x2, odd_norm_full, dx3, "odd_dx")
    dx1, dx1b, g_ffn0 = ffn_bwd(0, dx2, dx2b, ffn0)
    red_ffn0, (dx1, dx1b) = reduce_start([g_ffn0[1], g_ffn0[0]], "ffn0", (dx1, dx1b))
    dw_out = mm_tn(mix, dx1b, "even_dw_out", J=1, tk=512, tm=1024)
    dmix = mm_nt(dx1b, w_out, "even_dmix", tr=D_MODEL)
    dproj, dcw_even, dpw, dps = mixer_bwd(proj, dmix, conv_w_full, pool_w, even_pool_scale, "even_mixer_bwd")
    dw_in = mm_tn(xn0, dproj, "even_dw_in", J=N_CHIPS, tk=512, tm=1024)
    grad_x, _, dg_even = mm_nt_norm_bwd(dproj, w_in, xs, even_norm, dx1, "even_dx")
    d_rel = jnp.sum(bias_reduce(dbias.reshape(3, N_HEADS, 2 * ATT_BLOCK * ATT_BLOCK), "bias_reduce"), axis=0).T

    red_even, grad_x = reduce_start([dw_in, dw_out], "even", grad_x)

    dcw_sh = dcw_even.reshape(3, N_CHIPS, A_WIDTH // N_CHIPS).transpose(1, 0, 2)
    don_sh = dg_odd.reshape(N_CHIPS, D_MODEL // N_CHIPS)
    dfcw = jnp.stack([g_ffn0[2], g_ffn1[2]])
    dfcw_sh = dfcw.reshape(2, 3, N_CHIPS, 2 * D_FF // N_CHIPS).transpose(2, 0, 1, 3)
    rep_grads = [d_rel, dg_even, dpw[None], dps, _head_sum(dgqk[0]), _head_sum(dgqk[1]),
                 jnp.concatenate([g_ffn0[4], g_ffn1[4]], axis=0), jnp.concatenate([g_ffn0[3], g_ffn1[3]], axis=0)]
    rep_rows = _pack(rep_grads)
    shard_rows = jnp.concatenate([_pack([dcw_sh[j], don_sh[j], dfcw_sh[j]]) for j in range(N_CHIPS)], axis=0)
    n_rep, n_shard = rep_rows.shape[0], shard_rows.shape[0] // N_CHIPS
    small_sems, small_rows, small_land = devices_start(jnp.concatenate([rep_rows, shard_rows], axis=0), "small_grads_start")
    grad_x, small_rows = lax.optimization_barrier((grad_x, small_rows))

    def reduce_end(red, tag, after):
        sems, parts, zones = red
        parts, zones = reduce_wait(parts, zones, sems, after, "reduce_wait_" + tag)
        return zones, parts

    z_ffn1, p_ffn1 = reduce_end(red_ffn1, "ffn1", grad_x)
    z_odd, p_odd = reduce_end(red_odd, "odd", grad_x)
    r_qkv = reduce_sum(z_odd[0], p_odd[0], place, "reduce_sum_w_qkv")
    r_o = reduce_sum(z_odd[1], p_odd[1], place, "reduce_sum_w_o")
    r_up = reduce_sum(z_ffn1[0], p_ffn1[0], place, "reduce_sum_w_up1", layer=1)
    r_down = reduce_sum(z_ffn1[1], p_ffn1[1], place, "reduce_sum_w_down1", layer=1)
    r_qkv, r_o, r_up, r_down = lax.optimization_barrier((r_qkv, r_o, r_up, r_down))
    z_ffn0, p_ffn0 = reduce_end(red_ffn0, "ffn0", r_down)
    r_up = reduce_sum(z_ffn0[0], p_ffn0[0], place, "reduce_sum_w_up0", into=r_up, layer=0)
    r_down = reduce_sum(z_ffn0[1], p_ffn0[1], place, "reduce_sum_w_down0", into=r_down, layer=0)
    later = ["odd_w_qkv", "odd_w_o", "ffn_w_up", "ffn_w_down"]
    joined = join_halves([r_qkv, r_o, r_up, r_down], "grads_join_late_layers")
    G = {nm: g.reshape(W[nm].shape) for nm, g in zip(later, joined)}

    D_, NM, NV = {}, {}, {}

    def update(nm):
        as3 = lambda a: a.reshape((-1,) + a.shape[-2:])
        outs = adamw(as3(W[nm]), as3(G[nm]), as3(M1[nm]), as3(M2[nm]), "adamw_" + nm)
        D_[nm], NM[nm], NV[nm], G[nm] = [o.reshape(W[nm].shape) for o in outs]

    def all_before(names):
        tied = lax.optimization_barrier([D_[nm] for nm in names])
        for nm, d in zip(names, tied):
            D_[nm] = d
        return tied[0]

    for nm in later:
        update(nm)
    z_even, p_even = reduce_end(red_even, "even", all_before(later))
    joined = join_halves([reduce_sum(z_even[0], p_even[0], place, "reduce_sum_w_in"),
                          reduce_sum(z_even[1], p_even[1], place, "reduce_sum_w_out")], "grads_join_first_layer")
    first = ["even_w_in", "even_w_out"]
    for nm, g in zip(first, joined):
        G[nm] = g.reshape(W[nm].shape)
        update(nm)
    small_rows, small_land = devices_wait(small_rows, small_land, small_sems, all_before(first), "small_grads_wait")
    small_sum = device_sum(small_land, small_rows, place[2:3], "small_grads_sum")
    mine = lax.dynamic_slice_in_dim(small_sum, n_rep + chip * n_shard, n_shard, axis=0)
    g_small = jnp.concatenate([small_sum[:n_rep], mine], axis=0)
    small_names = [n for n, _ in REPLICATED_SMALL + SHARDED_SMALL]
    small_shapes = [s for _, s in REPLICATED_SMALL + SHARDED_SMALL]
    G.update(dict(zip(small_names, _unpack(g_small, small_shapes))))
    packs = [_pack([d[n] for n in small_names])[None] for d in (W, M1, M2)]
    outs = adamw(packs[0], g_small[None], packs[1], packs[2], "adamw_small")
    for dst, o in zip((D_, NM, NV), outs[:3]):
        dst.update(dict(zip(small_names, _unpack(o[0], small_shapes))))

    return (loss, grad_x[None], *[G[n] for n in WEIGHT_ORDER], *[D_[n] for n in WEIGHT_ORDER],
            *[NM[n] for n in WEIGHT_ORDER], *[NV[n] for n in WEIGHT_ORDER])


def _head_sum(dg):
    return jnp.sum(dg.reshape(N_HEADS, HEAD_DIM), axis=0, keepdims=True)
```

```python
import functools
import math

import numpy as np
import jax
import jax.numpy as jnp
from jax import lax
from jax.experimental import pallas as pl
from jax.experimental.pallas import tpu as pltpu

F32 = jnp.float32
BF16 = jnp.bfloat16

D_MODEL = 1024
N_HEADS = 16
HEAD_DIM = 64
A_WIDTH = 512
POOL_WINDOWS = (2, 4, 8, 16)
POOL_GROUP = 128
EVEN_IN = 2048
D_FF = 2816
DILATED_PAIRS = ((128, 1), (512, 4), (2048, 16))
ATT_BLOCK = 128
N_REL_BUCKETS = 32
REL_MAX_DISTANCE = 2048
EPS = 1e-6
MASK_VALUE = -1e30
ADAM_LR, ADAM_B1, ADAM_B2, ADAM_EPS, ADAM_WD, ADAM_STEP = 0.001, 0.9, 0.999, 1e-08, 0.01, 10

VMEM_LIMIT_BYTES = 48 * 1024 * 1024
N_CHIPS = 4
N_DEV = 8
MESH = pl.DeviceIdType.MESH


def _params(*sem):
    return pltpu.CompilerParams(dimension_semantics=sem if sem else None, vmem_limit_bytes=VMEM_LIMIT_BYTES)


def _sds(shape, dtype):
    return jax.ShapeDtypeStruct(tuple(shape), dtype)


def cast_bf16(x, name, tr=None):
    lead, (R, C) = x.shape[:-2], x.shape[-2:]
    n = int(np.prod(lead)) if lead else 1
    x3 = x.reshape((n, R, C))
    tr = tr or R

    def body(x_ref, o_ref):
        o_ref[...] = x_ref[...].astype(BF16)

    out = pl.pallas_call(
        body, name=name, grid=(n, R // tr),
        in_specs=[pl.BlockSpec((None, tr, C), lambda i, r: (i, r, 0))],
        out_specs=pl.BlockSpec((None, tr, C), lambda i, r: (i, r, 0)),
        out_shape=_sds((n, R, C), BF16), compiler_params=_params("parallel", "parallel"),
    )(x3)
    return out.reshape(lead + (R, C))


def rmsnorm_fwd(x, g, name, ts=512):
    S, Dm = x.shape

    def body(x_ref, g_ref, o_ref):
        xv = x_ref[...]
        r = lax.rsqrt(jnp.mean(xv * xv, axis=-1, keepdims=True) + EPS)
        o_ref[...] = ((xv * r) * g_ref[...]).astype(BF16)

    return pl.pallas_call(
        body, name=name, grid=(S // ts,),
        in_specs=[pl.BlockSpec((ts, Dm), lambda i: (i, 0)), pl.BlockSpec((1, Dm), lambda i: (0, 0))],
        out_specs=pl.BlockSpec((ts, Dm), lambda i: (i, 0)),
        out_shape=_sds((S, Dm), BF16), compiler_params=_params("parallel"),
    )(x, g)


def mm_nn(a, w, name, layer=0, res=None, out_dtype=F32, tm=1024):
    M, K = a.shape
    J, _, _, Ns = w.shape

    def body(*refs):
        a_ref, w_ref = refs[0], refs[1]
        o_ref = refs[-1]
        acc = jnp.dot(a_ref[...], w_ref[...], preferred_element_type=F32)
        if res is not None:
            acc = refs[2][...] + acc
        o_ref[...] = acc.astype(o_ref.dtype)

    in_specs = [pl.BlockSpec((tm, K), lambda j, m: (m, 0)),
                pl.BlockSpec((None, None, K, Ns), lambda j, m: (j, layer, 0, 0))]
    args = [a, w]
    if res is not None:
        in_specs.append(pl.BlockSpec((tm, Ns), lambda j, m: (m, j)))
        args.append(res)
    return pl.pallas_call(
        body, name=name, grid=(J, M // tm), in_specs=in_specs,
        out_specs=pl.BlockSpec((tm, Ns), lambda j, m: (m, j)),
        out_shape=_sds((M, J * Ns), out_dtype), compiler_params=_params("parallel", "parallel"),
    )(*args)


def mm_res_norm(a, w, res, gain, name, tm=1024):
    M, K = a.shape
    Dm = w.shape[-1]

    def body(a_ref, w_ref, r_ref, g_ref, y_ref, yn_ref):
        y = r_ref[...] + jnp.dot(a_ref[...], w_ref[...], preferred_element_type=F32)
        y_ref[...] = y
        r = lax.rsqrt(jnp.mean(y * y, axis=-1, keepdims=True) + EPS)
        yn_ref[...] = ((y * r) * g_ref[...]).astype(BF16)

    row = pl.BlockSpec((tm, Dm), lambda m: (m, 0))
    return pl.pallas_call(
        body, name=name, grid=(M // tm,),
        in_specs=[pl.BlockSpec((tm, K), lambda m: (m, 0)),
                  pl.BlockSpec((None, None, K, Dm), lambda m: (0, 0, 0, 0), pipeline_mode=pl.Buffered(1)),
                  row, pl.BlockSpec((1, Dm), lambda m: (0, 0))],
        out_specs=[row, row], out_shape=[_sds((M, Dm), F32), _sds((M, Dm), BF16)],
        compiler_params=_params("parallel"),
    )(a, w, res, gain)


def mm_res_loss(a, w, res, target, name, tm=512):
    M, K = a.shape
    Dm = w.shape[-1]

    def body(a_ref, w_ref, r_ref, t_ref, d_ref, db_ref, s_ref):
        e = (r_ref[...] + jnp.dot(a_ref[...], w_ref[...], preferred_element_type=F32)) - t_ref[...]
        d = e * (1.0 / Dm)
        d_ref[...] = d
        db_ref[...] = d.astype(BF16)
        part = jnp.sum(e * e, axis=0, keepdims=True)

        @pl.when(pl.program_id(0) == 0)
        def _():
            s_ref[...] = part

        @pl.when(pl.program_id(0) > 0)
        def _():
            s_ref[...] += part

    row = pl.BlockSpec((tm, Dm), lambda m: (m, 0))
    return pl.pallas_call(
        body, name=name, grid=(M // tm,),
        in_specs=[pl.BlockSpec((tm, K), lambda m: (m, 0)),
                  pl.BlockSpec((None, None, K, Dm), lambda m: (0, 0, 0, 0), pipeline_mode=pl.Buffered(1)), row, row],
        out_specs=[row, row, pl.BlockSpec((1, Dm), lambda m: (0, 0))],
        out_shape=[_sds((M, Dm), F32), _sds((M, Dm), BF16), _sds((1, Dm), F32)],
        compiler_params=_params("arbitrary"),
    )(a, w, res, target)


def mm_nt(dy, w, name, tr, layer=0, out_dtype=F32, tm=512):
    M = dy.shape[0]
    J, _, R, Ns = w.shape
    dims = (((1,), (1,)), ((), ()))

    def body(dy_ref, w_ref, o_ref):
        acc = None
        for j in range(J):
            p = lax.dot_general(dy_ref[:, j * Ns:(j + 1) * Ns], w_ref[j], dims, preferred_element_type=F32)
            acc = p if acc is None else acc + p
        o_ref[...] = acc.astype(o_ref.dtype)

    return pl.pallas_call(
        body, name=name, grid=(R // tr, M // tm),
        in_specs=[pl.BlockSpec((tm, J * Ns), lambda r, m: (m, 0)),
                  pl.BlockSpec((J, None, tr, Ns), lambda r, m: (0, layer, r, 0))],
        out_specs=pl.BlockSpec((tm, tr), lambda r, m: (m, r)),
        out_shape=_sds((M, R), out_dtype),
        compiler_params=_params("parallel", "parallel"),
    )(dy, w)


def mm_nt_norm_bwd(dy, w, x, g, dres, name, layer=0, tm=512):
    M = dy.shape[0]
    J, _, Dm, Ns = w.shape
    dims = (((1,), (1,)), ((), ()))

    def body(dy_ref, w_ref, x_ref, g_ref, r_ref, dx_ref, dxb_ref, dg_ref):
        dxn = None
        for j in range(J):
            p = lax.dot_general(dy_ref[:, j * Ns:(j + 1) * Ns], w_ref[j], dims, preferred_element_type=F32)
            dxn = p if dxn is None else dxn + p
        xv = x_ref[...]
        r = lax.rsqrt(jnp.mean(xv * xv, axis=-1, keepdims=True) + EPS)
        gx = dxn * g_ref[...]
        dot = jnp.sum(gx * xv, axis=-1, keepdims=True)
        dx = r_ref[...] + r * gx - xv * ((r * r * r) * (dot * (1.0 / Dm)))
        dx_ref[...] = dx
        dxb_ref[...] = dx.astype(BF16)
        part = jnp.sum(dxn * (xv * r), axis=0, keepdims=True)

        @pl.when(pl.program_id(0) == 0)
        def _():
            dg_ref[...] = part

        @pl.when(pl.program_id(0) > 0)
        def _():
            dg_ref[...] += part

    row = pl.BlockSpec((tm, Dm), lambda m: (m, 0))
    vec = pl.BlockSpec((1, Dm), lambda m: (0, 0))
    return pl.pallas_call(
        body, name=name, grid=(M // tm,),
        in_specs=[pl.BlockSpec((tm, J * Ns), lambda m: (m, 0)),
                  pl.BlockSpec((J, None, Dm, Ns), lambda m: (0, layer, 0, 0), pipeline_mode=pl.Buffered(1)), row, vec, row],
        out_specs=[row, row, vec],
        out_shape=[_sds((M, Dm), F32), _sds((M, Dm), BF16), _sds((1, Dm), F32)],
        compiler_params=_params("arbitrary"),
    )(dy, w, x, g, dres)


def mm_tn(a, dy, name, J, tk, tm=512, jb=None):
    M, K = a.shape
    jb = jb or J
    Ns = dy.shape[1] // J
    N = jb * Ns
    n_m = M // tm
    dims = (((0,), (0,)), ((), ()))

    def body(a_ref, dy_ref, o_ref, acc_ref):
        p = lax.dot_general(a_ref[...], dy_ref[...], dims, preferred_element_type=F32)
        m = pl.program_id(2)

        @pl.when(m == 0)
        def _():
            acc_ref[...] = p

        @pl.when(m > 0)
        def _():
            acc_ref[...] += p

        @pl.when(m == n_m - 1)
        def _():
            for j in range(jb):
                o_ref[j] = acc_ref[:, j * Ns:(j + 1) * Ns].astype(BF16)

    return pl.pallas_call(
        body, name=name, grid=(J // jb, K // tk, n_m),
        in_specs=[pl.BlockSpec((tm, tk), lambda g, k, m: (m, k)), pl.BlockSpec((tm, N), lambda g, k, m: (m, g))],
        out_specs=pl.BlockSpec((jb, tk, Ns), lambda g, k, m: (g, k, 0)),
        out_shape=_sds((J, K, Ns), BF16), scratch_shapes=[pltpu.VMEM((tk, N), F32)],
        compiler_params=_params("parallel", "parallel", "arbitrary"),
    )(a, dy)


HALO = 16


def _shift_down(x, s):
    return pltpu.roll(x, s, 0)


def _shift_up(x, s):
    return pltpu.roll(x, x.shape[0] - s, 0)


def _conv3(z, cw):
    return (_shift_down(z, 2) * cw[0:1] + _shift_down(z, 1) * cw[1:2]) + z * cw[2:3]


def _window_count(first_row, n, k):
    t = first_row + lax.broadcasted_iota(jnp.int32, (n, 1), 0)
    return jnp.clip(t + 1, 1, k).astype(F32)


def mixer_fwd(proj, conv_w, pool_w, pool_scale, name, ts=256):
    S = proj.shape[0]
    n = ts + HALO

    def body(pm_ref, pb_ref, cw_ref, pw_ref, ps_ref, o_ref):
        i = pl.program_id(0)
        before = jnp.where(i > 0, pb_ref[...], 0.0)
        ext = jnp.concatenate([before, pm_ref[...]], axis=0)
        cw = cw_ref[...]
        z = ext[:, 2 * A_WIDTH:3 * A_WIDTH] * ext[:, 0:A_WIDTH]
        cz = _conv3(z, cw)
        ya = pm_ref[:, A_WIDTH:2 * A_WIDTH] * cz[HALO:]
        o_ref[:, 0:A_WIDTH] = ya.astype(BF16)
        for g, k in enumerate(POOL_WINDOWS):
            lo = 3 * A_WIDTH + g * POOL_GROUP
            p = ext[:, lo:lo + POOL_GROUP]
            w = p
            s = 1
            while s < k:
                w = w + _shift_down(w, s)
                s *= 2
            pooled = w / _window_count(i * ts - HALO, n, k) - p
            yb = jnp.dot(pooled[HALO:].astype(BF16), pw_ref[g], preferred_element_type=F32)
            yb = yb * ps_ref[:, g * POOL_GROUP:(g + 1) * POOL_GROUP]
            o_ref[:, A_WIDTH + g * POOL_GROUP:A_WIDTH + (g + 1) * POOL_GROUP] = yb.astype(BF16)

    hb = ts // HALO
    return pl.pallas_call(
        body, name=name, grid=(S // ts,),
        in_specs=[
            pl.BlockSpec((ts, EVEN_IN), lambda i: (i, 0)),
            pl.BlockSpec((HALO, EVEN_IN), lambda i: (jnp.maximum(i * hb - 1, 0), 0)),
            pl.BlockSpec((3, A_WIDTH), lambda i: (0, 0)),
            pl.BlockSpec((4, POOL_GROUP, POOL_GROUP), lambda i: (0, 0, 0)),
            pl.BlockSpec((1, 4 * POOL_GROUP), lambda i: (0, 0)),
        ],
        out_specs=pl.BlockSpec((ts, D_MODEL), lambda i: (i, 0)),
        out_shape=_sds((S, D_MODEL), BF16), compiler_params=_params("parallel"),
    )(proj, proj, conv_w, pool_w, pool_scale)


def mixer_bwd(proj, dmix, conv_w, pool_w, pool_scale, name, ts=256):
    S = proj.shape[0]
    n = ts + 2 * HALO
    nt = S // ts
    tn_dims = (((0,), (0,)), ((), ()))
    nt_dims = (((1,), (1,)), ((), ()))

    def body(pm_ref, pb_ref, pa_ref, dm_ref, da_ref, cw_ref, pw_ref, ps_ref, o_ref, dcw_ref, dpw_ref, dps_ref):
        i = pl.program_id(0)
        last = i == nt - 1
        before = jnp.where(i > 0, pb_ref[...], 0.0)
        after = jnp.where(last, 0.0, pa_ref[...])
        ext = jnp.concatenate([before, pm_ref[...], after], axis=0)
        dafter = jnp.where(last, 0.0, da_ref[...])
        dext = jnp.concatenate([jnp.zeros((HALO, D_MODEL), F32), dm_ref[...], dafter], axis=0)
        cw = cw_ref[...]
        main = slice(HALO, HALO + ts)

        @pl.when(i == 0)
        def _():
            dcw_ref[...] = jnp.zeros_like(dcw_ref)
            dpw_ref[...] = jnp.zeros_like(dpw_ref)
            dps_ref[...] = jnp.zeros_like(dps_ref)

        h, gb, gc = ext[:, 0:A_WIDTH], ext[:, A_WIDTH:2 * A_WIDTH], ext[:, 2 * A_WIDTH:3 * A_WIDTH]
        z = gc * h
        z1, z2 = _shift_down(z, 1), _shift_down(z, 2)
        cz = (z2 * cw[0:1] + z1 * cw[1:2]) + z * cw[2:3]
        dya = dext[:, 0:A_WIDTH]
        dcz = dya * gb
        dz = dcz * cw[2:3] + _shift_up(dcz, 1) * cw[1:2] + _shift_up(dcz, 2) * cw[0:1]
        o_ref[:, 0:A_WIDTH] = (dz * gc)[main].astype(BF16)
        o_ref[:, A_WIDTH:2 * A_WIDTH] = (dya * cz)[main].astype(BF16)
        o_ref[:, 2 * A_WIDTH:3 * A_WIDTH] = (dz * h)[main].astype(BF16)
        dczm = dcz[main]
        dcw_ref[0:1, :] += jnp.sum(dczm * z2[main], axis=0, keepdims=True)
        dcw_ref[1:2, :] += jnp.sum(dczm * z1[main], axis=0, keepdims=True)
        dcw_ref[2:3, :] += jnp.sum(dczm * z[main], axis=0, keepdims=True)

        for g, k in enumerate(POOL_WINDOWS):
            lo = 3 * A_WIDTH + g * POOL_GROUP
            cols = slice(g * POOL_GROUP, (g + 1) * POOL_GROUP)
            p = ext[:, lo:lo + POOL_GROUP]
            w = p
            s = 1
            while s < k:
                w = w + _shift_down(w, s)
                s *= 2
            cnt = _window_count(i * ts - HALO, n, k)
            pooled = (w / cnt - p)[main].astype(BF16)
            dyb = dext[:, A_WIDTH + g * POOL_GROUP:A_WIDTH + (g + 1) * POOL_GROUP]
            e = dyb * ps_ref[:, cols]
            pre = jnp.dot(pooled, pw_ref[g], preferred_element_type=F32)
            dps_ref[:, cols] += jnp.sum(dyb[main] * pre, axis=0, keepdims=True)
            dpw_ref[g] += lax.dot_general(pooled, e[main].astype(BF16), tn_dims, preferred_element_type=F32)
            dpooled = lax.dot_general(e.astype(BF16), pw_ref[g], nt_dims, preferred_element_type=F32)
            q = dpooled / cnt
            a = q
            s = 1
            while s < k:
                a = a + _shift_up(a, s)
                s *= 2
            o_ref[:, lo:lo + POOL_GROUP] = (a - dpooled)[main].astype(BF16)

    hb = ts // HALO
    nh = S // HALO
    before_map = lambda i: (jnp.maximum(i * hb - 1, 0), 0)
    after_map = lambda i: (jnp.minimum((i + 1) * hb, nh - 1), 0)
    full = lambda *shape: pl.BlockSpec(shape, lambda i: (0,) * len(shape))
    return pl.pallas_call(
        body, name=name, grid=(nt,),
        in_specs=[
            pl.BlockSpec((ts, EVEN_IN), lambda i: (i, 0)),
            pl.BlockSpec((HALO, EVEN_IN), before_map),
            pl.BlockSpec((HALO, EVEN_IN), after_map),
            pl.BlockSpec((ts, D_MODEL), lambda i: (i, 0)),
            pl.BlockSpec((HALO, D_MODEL), after_map),
            full(3, A_WIDTH), full(4, POOL_GROUP, POOL_GROUP), full(1, 4 * POOL_GROUP),
        ],
        out_specs=[pl.BlockSpec((ts, EVEN_IN), lambda i: (i, 0)), full(3, A_WIDTH), full(4, POOL_GROUP, POOL_GROUP),
                   full(1, 4 * POOL_GROUP)],
        out_shape=[_sds((S, EVEN_IN), BF16), _sds((3, A_WIDTH), F32), _sds((4, POOL_GROUP, POOL_GROUP), F32),
                   _sds((1, 4 * POOL_GROUP), F32)],
        compiler_params=_params("arbitrary"),
    )(proj, proj, proj, dmix, dmix, conv_w, pool_w, pool_scale)


FFN_HALO = 16
FFN_TC = 1408


def glu_fwd(up, conv_w, conv_b, name, ts=256):
    S = up.shape[0]
    nc = D_FF // FFN_TC

    def body(gm_ref, gb_ref, um_ref, ub_ref, cwg_ref, cwu_ref, cbg_ref, cbu_ref, ug_ref, uu_ref, o_ref):
        i = pl.program_id(0)

        def conv(m_ref, b_ref, cw_ref, cb_ref):
            before = jnp.where(i > 0, b_ref[...].astype(F32), 0.0)
            ext = jnp.concatenate([before, m_ref[...].astype(F32)], axis=0)
            return _conv3(ext, cw_ref[...])[FFN_HALO:] + cb_ref[...]

        gate = conv(gm_ref, gb_ref, cwg_ref, cbg_ref)
        upv = conv(um_ref, ub_ref, cwu_ref, cbu_ref)
        ug_ref[...] = gate.astype(BF16)
        uu_ref[...] = upv.astype(BF16)
        o_ref[...] = ((gate * (1.0 / (1.0 + jnp.exp(-gate)))) * upv).astype(BF16)

    hb = ts // FFN_HALO
    main = lambda off: pl.BlockSpec((ts, FFN_TC), lambda i, c: (i, c + off))
    halo = lambda off: pl.BlockSpec((FFN_HALO, FFN_TC), lambda i, c: (jnp.maximum(i * hb - 1, 0), c + off))
    cw = lambda off: pl.BlockSpec((3, FFN_TC), lambda i, c: (0, c + off))
    cb = lambda off: pl.BlockSpec((1, FFN_TC), lambda i, c: (0, c + off))
    ug, uu, act = pl.pallas_call(
        body, name=name, grid=(S // ts, nc),
        in_specs=[main(0), halo(0), main(nc), halo(nc), cw(0), cw(nc), cb(0), cb(nc)],
        out_specs=[pl.BlockSpec((ts, FFN_TC), lambda i, c: (i, c))] * 3,
        out_shape=[_sds((S, D_FF), BF16)] * 3, compiler_params=_params("parallel", "parallel"),
    )(up, up, up, up, conv_w, conv_w, conv_b, conv_b)
    return (ug, uu), act


def glu_bwd(up, u, da, conv_w, name, ts=256):
    S = up.shape[0]
    nc = D_FF // FFN_TC
    nt = S // ts
    W = 2 * D_FF

    def body(x_ref, gm_ref, ga_ref, um_ref, ua_ref, dm_ref, da_ref, cw_ref, dx_ref, dcw_ref, dcb_ref):
        i = pl.program_id(0)
        last = i == nt - 1

        @pl.when(i == 0)
        def _():
            dcw_ref[...] = jnp.zeros_like(dcw_ref)
            dcb_ref[...] = jnp.zeros_like(dcb_ref)

        def rows(m_ref, a_ref, cols):
            return jnp.concatenate([m_ref[:, cols], a_ref[:, cols]], axis=0).astype(F32)

        def back(d, cols):
            cw = cw_ref[:, cols]
            d1, d2 = _shift_up(d, 1), _shift_up(d, 2)
            dx_ref[:, cols] = ((d * cw[2:3] + d1 * cw[1:2]) + d2 * cw[0:1])[:ts].astype(BF16)
            x = x_ref[:, cols].astype(F32)
            dcb_ref[:, cols] += jnp.sum(d[:ts], axis=0, keepdims=True)
            dcw_ref[0:1, cols] += jnp.sum(d2[:ts] * x, axis=0, keepdims=True)
            dcw_ref[1:2, cols] += jnp.sum(d1[:ts] * x, axis=0, keepdims=True)
            dcw_ref[2:3, cols] += jnp.sum(d[:ts] * x, axis=0, keepdims=True)

        for c in range(nc):
            cols = slice(c * FFN_TC, (c + 1) * FFN_TC)
            ug, uu = rows(gm_ref, ga_ref, cols), rows(um_ref, ua_ref, cols)
            dae = rows(dm_ref, da_ref, cols)
            dae = jnp.where(last & (lax.broadcasted_iota(jnp.int32, dae.shape, 0) >= ts), 0.0, dae)
            sg = 1.0 / (1.0 + jnp.exp(-ug))
            duu = dae * (ug * sg)
            dug = (dae * uu) * (sg * (1.0 + ug * (1.0 - sg)))
            back(dug, cols)
            back(duu, slice(D_FF + c * FFN_TC, D_FF + (c + 1) * FFN_TC))

    hb = ts // FFN_HALO
    nh = S // FFN_HALO
    after_map = lambda i: (jnp.minimum((i + 1) * hb, nh - 1), 0)
    main = pl.BlockSpec((ts, D_FF), lambda i: (i, 0))
    after = pl.BlockSpec((FFN_HALO, D_FF), after_map)
    return pl.pallas_call(
        body, name=name, grid=(nt,),
        in_specs=[pl.BlockSpec((ts, W), lambda i: (i, 0)), main, after, main, after, main, after,
                  pl.BlockSpec((3, W), lambda i: (0, 0))],
        out_specs=[pl.BlockSpec((ts, W), lambda i: (i, 0)), pl.BlockSpec((3, W), lambda i: (0, 0)),
                   pl.BlockSpec((1, W), lambda i: (0, 0))],
        out_shape=[_sds((S, W), BF16), _sds((3, W), F32), _sds((1, W), F32)],
        compiler_params=_params("arbitrary"),
    )(up, u[0], u[0], u[1], u[1], da, da, conv_w)


MEAN_GROUP = 256


def _head_mean_matrix():
    h = np.arange(MEAN_GROUP) // HEAD_DIM
    return jnp.asarray((h[:, None] == h[None, :]).astype(np.float32) / HEAD_DIM, dtype=BF16)


def _head_mean(v, gm):
    vb = v.astype(BF16)
    return jnp.concatenate([jnp.dot(vb[:, c:c + MEAN_GROUP], gm, preferred_element_type=F32)
                            for c in range(0, v.shape[1], MEAN_GROUP)], axis=1)


def qknorm_fwd(qkv, gqk, name, ts=512):
    S = qkv.shape[0]

    def body(x_ref, g_ref, gm_ref, o_ref):
        part = pl.program_id(0)
        x = x_ref[...]

        @pl.when(part < 2)
        def _():
            r = lax.rsqrt(_head_mean(x * x, gm_ref[...]) + EPS)
            o_ref[...] = ((x * r) * g_ref[...]).astype(BF16)

        @pl.when(part == 2)
        def _():
            o_ref[...] = x.astype(BF16)

    return pl.pallas_call(
        body, name=name, grid=(3, S // ts),
        in_specs=[pl.BlockSpec((ts, D_MODEL), lambda p, i: (i, p)), pl.BlockSpec((None, 1, D_MODEL), lambda p, i: (p, 0, 0)),
                  pl.BlockSpec((MEAN_GROUP, MEAN_GROUP), lambda p, i: (0, 0))],
        out_specs=pl.BlockSpec((ts, D_MODEL), lambda p, i: (i, p)),
        out_shape=_sds((S, 3 * D_MODEL), BF16), compiler_params=_params("parallel", "parallel"),
    )(qkv, gqk, _head_mean_matrix())


def qknorm_bwd(qkv, dq, dk, dv, gqk, name, ts=256):
    S = qkv.shape[0]

    def body(x_ref, dq_ref, dk_ref, dv_ref, g_ref, gm_ref, o_ref, dg_ref):
        @pl.when(pl.program_id(0) == 0)
        def _():
            dg_ref[...] = jnp.zeros_like(dg_ref)

        gm = gm_ref[...]
        for part, d_ref in enumerate((dq_ref, dk_ref)):
            cols = slice(part * D_MODEL, (part + 1) * D_MODEL)
            x = x_ref[:, cols]
            d = d_ref[...]
            r = lax.rsqrt(_head_mean(x * x, gm) + EPS)
            gx = d * g_ref[part]
            o_ref[:, cols] = (r * gx - x * ((r * r * r) * _head_mean(gx * x, gm))).astype(BF16)
            dg_ref[part] += jnp.sum(d * (x * r), axis=0, keepdims=True)
        o_ref[:, 2 * D_MODEL:] = dv_ref[...].astype(BF16)

    row = pl.BlockSpec((ts, D_MODEL), lambda i: (i, 0))
    wide = pl.BlockSpec((ts, 3 * D_MODEL), lambda i: (i, 0))
    gains = pl.BlockSpec((3, 1, D_MODEL), lambda i: (0, 0, 0))
    return pl.pallas_call(
        body, name=name, grid=(S // ts,),
        in_specs=[wide, row, row, row, gains, pl.BlockSpec((MEAN_GROUP, MEAN_GROUP), lambda i: (0, 0))],
        out_specs=[wide, gains],
        out_shape=[_sds((S, 3 * D_MODEL), BF16), _sds((3, 1, D_MODEL), F32)],
        compiler_params=_params("arbitrary"),
    )(qkv, dq, dk, dv, gqk, _head_mean_matrix())


RESIDUES = 16


def _block_order(dil):
    runs = RESIDUES // dil
    slot = np.arange(ATT_BLOCK)
    return (slot % (ATT_BLOCK // runs)) * runs + slot // (ATT_BLOCK // runs)


def _bucket_tables():
    n = ATT_BLOCK
    max_exact = N_REL_BUCKETS // 2
    buckets, valids = [], []
    for _, dil in DILATED_PAIRS:
        order = _block_order(dil)
        a = order[:, None]
        c = np.concatenate([order, n + order])[None, :]
        first_half = (np.arange(2 * n) < n)[None, :]
        rel = a + n - c
        band = (rel >= 0) & (rel <= n)
        dist = np.clip(rel, 0, n) * dil
        dd = np.maximum(dist, 1).astype(np.float32)
        large = max_exact + (np.log(dd / np.float32(max_exact)) / np.float32(math.log(REL_MAX_DISTANCE / max_exact))
                             * np.float32(N_REL_BUCKETS - max_exact)).astype(np.int32)
        large = np.minimum(large, N_REL_BUCKETS - 1)
        buckets.append(np.where(dist < max_exact, dist, large).reshape(1, -1))
        valids.append(np.stack([(band & ~first_half).reshape(1, -1), band.reshape(1, -1)]))
    return np.stack(buckets).astype(np.int32), np.stack(valids).astype(np.int32)


BIAS_CHUNK = 8192


def _split3(x):
    a = x.astype(BF16)
    r = x - a.astype(F32)
    b = r.astype(BF16)
    c = (r - b.astype(F32)).astype(BF16)
    return a, b, c


def bias_expand(rel_bias_t, name):
    bucket, valid = _bucket_tables()
    nq = bucket.shape[-1]

    def body(t_ref, b_ref, v_ref, o_ref):
        onehot = (lax.broadcasted_iota(jnp.int32, (N_REL_BUCKETS, BIAS_CHUNK), 0) == b_ref[...]).astype(BF16)
        acc = None
        for term in _split3(t_ref[...]):
            p = jnp.dot(term, onehot, preferred_element_type=F32)
            acc = p if acc is None else acc + p
        o_ref[...] = jnp.where(v_ref[...] > 0, acc, MASK_VALUE)

    return pl.pallas_call(
        body, name=name, grid=(3, 2, nq // BIAS_CHUNK),
        in_specs=[pl.BlockSpec((N_HEADS, N_REL_BUCKETS), lambda b, v, c: (0, 0)),
                  pl.BlockSpec((None, 1, BIAS_CHUNK), lambda b, v, c: (b, 0, c)),
                  pl.BlockSpec((None, None, 1, BIAS_CHUNK), lambda b, v, c: (b, v, 0, c))],
        out_specs=pl.BlockSpec((None, None, N_HEADS, BIAS_CHUNK), lambda b, v, c: (b, v, 0, c)),
        out_shape=_sds((3, 2, N_HEADS, nq), F32), compiler_params=_params("parallel", "parallel", "parallel"),
    )(rel_bias_t, jnp.asarray(bucket), jnp.asarray(valid))


def bias_reduce(dbias, name):
    bucket, _ = _bucket_tables()
    nq = bucket.shape[-1]
    dims = (((1,), (1,)), ((), ()))

    def body(d_ref, b_ref, o_ref):
        onehot = (lax.broadcasted_iota(jnp.int32, (N_REL_BUCKETS, BIAS_CHUNK), 0) == b_ref[...]).astype(BF16)
        acc = None
        for term in _split3(d_ref[...]):
            p = lax.dot_general(term, onehot, dims, preferred_element_type=F32)
            acc = p if acc is None else acc + p

        @pl.when(pl.program_id(1) == 0)
        def _():
            o_ref[...] = acc

        @pl.when(pl.program_id(1) > 0)
        def _():
            o_ref[...] += acc

    return pl.pallas_call(
        body, name=name, grid=(3, nq // BIAS_CHUNK),
        in_specs=[pl.BlockSpec((None, N_HEADS, BIAS_CHUNK), lambda b, c: (b, 0, c)),
                  pl.BlockSpec((None, 1, BIAS_CHUNK), lambda b, c: (b, 0, c))],
        out_specs=pl.BlockSpec((None, N_HEADS, N_REL_BUCKETS), lambda b, c: (b, 0, 0)),
        out_shape=_sds((3, N_HEADS, N_REL_BUCKETS), F32), compiler_params=_params("parallel", "arbitrary"),
    )(dbias, jnp.asarray(bucket))


PAIR = 2 * HEAD_DIM
N_PAIRS = N_HEADS // 2
_NT = (((1,), (1,)), ((), ()))
_TN = (((0,), (0,)), ((), ()))


def _low_lanes(shape):
    return lax.broadcasted_iota(jnp.int32, shape, 1) < HEAD_DIM


ATTN_VMEM_LIMIT_BYTES = 56 * 1024 * 1024
BRANCH_ORDER = (2, 1, 0)


def _regroup(dst, src, L16):
    for r in range(RESIDUES):
        dst[pl.ds(r * L16, L16), :] = src[pl.ds(r, L16, stride=RESIDUES), :]


def _ungroup(dst, src, L16):
    for r in range(RESIDUES):
        dst[pl.ds(r, L16, stride=RESIDUES), :] = src[pl.ds(r * L16, L16), :]


def _branch_geometry(branch, S):
    dil = DILATED_PAIRS[branch][1]
    runs = RESIDUES // dil
    return dil, runs, ATT_BLOCK // runs, S // dil // ATT_BLOCK


def _block_rows(it, branch, S):
    dil, runs, run_len, n_blocks = _branch_geometry(branch, S)
    L16 = S // RESIDUES
    r, b = it // n_blocks, it % n_blocks
    prev = jnp.maximum(b - 1, 0)
    cur_rows = [pl.multiple_of((j * dil + r) * L16 + run_len * b, 8) for j in range(runs)]
    prev_rows = [pl.multiple_of((j * dil + r) * L16 + run_len * prev, 8) for j in range(runs)]
    return cur_rows, prev_rows, jnp.minimum(b, 1)


def _load_block(ref, rows, run_len):
    parts = [ref[pl.ds(o, run_len), :] for o in rows]
    return parts[0] if len(parts) == 1 else jnp.concatenate(parts, axis=0)


def _store_block(ref, rows, run_len, value, add=False):
    for j, o in enumerate(rows):
        part = value[j * run_len:(j + 1) * run_len]
        if add:
            ref[pl.ds(o, run_len), :] += part
        else:
            ref[pl.ds(o, run_len), :] = part


ATTN_FWD_UNROLL = 8
ATTN_BWD_UNROLL = 4


def _stack_heads(x, low):
    zero = jnp.zeros_like(x)
    return jnp.concatenate([jnp.where(low, x, zero), jnp.where(low, zero, x)], axis=0)


def _unstack_heads(y, low):
    return jnp.where(low, y[:ATT_BLOCK], y[ATT_BLOCK:])


def attn_fwd(qkvn, bias, name):
    S = qkvn.shape[0]
    L16 = S // RESIDUES
    n_iter = S // ATT_BLOCK

    def body(q_ref, k_ref, v_ref, b_ref, o_ref, lse_ref, stage, qp, kp, vp, acc_s, m_s, l_s):
        for src, dst in ((q_ref, qp), (k_ref, kp), (v_ref, vp)):
            stage[...] = src[...].astype(F32)
            _regroup(dst, stage, L16)
        low = _low_lanes((ATT_BLOCK, PAIR))

        for branch in BRANCH_ORDER:
            _, _, run_len, _ = _branch_geometry(branch, S)
            first = branch == BRANCH_ORDER[0]

            def step(it, carry, branch=branch, run_len=run_len, first=first):
                cur, prev, variant = _block_rows(it, branch, S)
                q = _load_block(qp, cur, run_len).astype(BF16)
                k = jnp.concatenate([_load_block(kp, prev, run_len), _load_block(kp, cur, run_len)], axis=0).astype(BF16)
                v = jnp.concatenate([_load_block(vp, prev, run_len), _load_block(vp, cur, run_len)], axis=0).astype(BF16)
                s = lax.dot_general(_stack_heads(q, low), k, _NT, preferred_element_type=F32) * (HEAD_DIM ** -0.5)
                s = s + b_ref[2 * branch + variant].reshape(2 * ATT_BLOCK, 2 * ATT_BLOCK)
                mx = jnp.max(s, axis=-1, keepdims=True)
                p = jnp.exp(s - mx)
                den = jnp.sum(p, axis=-1, keepdims=True)
                pv = jnp.dot(p.astype(BF16), v, preferred_element_type=F32)
                acc = _unstack_heads(pv, low)
                m = _unstack_heads(mx, low)
                l = _unstack_heads(den, low)
                if not first:
                    m_old = _load_block(m_s, cur, run_len)
                    m_new = jnp.maximum(m_old, m)
                    a_old, a_new = jnp.exp(m_old - m_new), jnp.exp(m - m_new)
                    acc = _load_block(acc_s, cur, run_len) * a_old + acc * a_new
                    l = _load_block(l_s, cur, run_len) * a_old + l * a_new
                    m = m_new
                _store_block(acc_s, cur, run_len, acc)
                _store_block(m_s, cur, run_len, m)
                _store_block(l_s, cur, run_len, l)
                return carry

            lax.fori_loop(0, n_iter, step, 0, unroll=ATTN_FWD_UNROLL)

        acc_s[...] = acc_s[...] / l_s[...]
        _ungroup(stage, acc_s, L16)
        o_ref[...] = stage[...].astype(BF16)
        m_s[...] = m_s[...] + jnp.log(l_s[...])
        _ungroup(lse_ref, m_s, L16)

    col = lambda part: pl.BlockSpec((S, PAIR), lambda hp: (0, part * N_PAIRS + hp))
    out = pl.BlockSpec((S, PAIR), lambda hp: (0, hp))
    return pl.pallas_call(
        body, name=name, grid=(N_PAIRS,),
        in_specs=[col(0), col(1), col(2), pl.BlockSpec((6, 2, ATT_BLOCK, 2 * ATT_BLOCK), lambda hp: (0, hp, 0, 0))],
        out_specs=[out, out], out_shape=[_sds((S, D_MODEL), BF16), _sds((S, D_MODEL), F32)],
        scratch_shapes=[pltpu.VMEM((S, PAIR), F32)] * 7,
        compiler_params=pltpu.CompilerParams(dimension_semantics=("parallel",), vmem_limit_bytes=ATTN_VMEM_LIMIT_BYTES),
    )(qkvn, qkvn, qkvn, bias)


def attn_bwd(qkvn, att, datt, lse, bias, name):
    S = qkvn.shape[0]
    L16 = S // RESIDUES
    n_iter = S // ATT_BLOCK
    TILE = 512

    def body(q_ref, k_ref, v_ref, o_ref, do_ref, lse_ref, b_ref, dq_ref, dk_ref, dv_ref, db_ref,
             qp, kp, vp, dop, ldp, dqp, dkp, dvp):
        stage = dqp
        for src, dst in ((q_ref, qp), (k_ref, kp), (v_ref, vp), (do_ref, dop)):
            stage[...] = src[...].astype(F32)
            _regroup(dst, stage, L16)

        def pack(i, carry):
            rows = pl.ds(pl.multiple_of(i * TILE, TILE), TILE)
            low = _low_lanes((TILE, PAIR))
            lane = lax.broadcasted_iota(jnp.int32, (TILE, PAIR), 1)
            prod = do_ref[rows, :].astype(F32) * o_ref[rows, :].astype(F32)
            d0 = jnp.sum(jnp.where(low, prod, 0.0), axis=-1, keepdims=True)
            d1 = jnp.sum(jnp.where(low, 0.0, prod), axis=-1, keepdims=True)
            stage[rows, :] = jnp.where((lane & (HEAD_DIM // 2)) == 0, lse_ref[rows, :], jnp.where(low, d0, d1))
            return carry

        lax.fori_loop(0, S // TILE, pack, 0)
        _regroup(ldp, stage, L16)
        dqp[...] = jnp.zeros_like(dqp)
        dkp[...] = jnp.zeros_like(dkp)
        dvp[...] = jnp.zeros_like(dvp)
        db_ref[...] = jnp.zeros_like(db_ref)
        low = _low_lanes((ATT_BLOCK, PAIR))

        for branch in BRANCH_ORDER:
            _, _, run_len, _ = _branch_geometry(branch, S)

            def step(it, carry, branch=branch, run_len=run_len):
                cur, prev, variant = _block_rows(it, branch, S)
                q = _load_block(qp, cur, run_len).astype(BF16)
                dout = _load_block(dop, cur, run_len).astype(BF16)
                ld = _load_block(ldp, cur, run_len)
                k = jnp.concatenate([_load_block(kp, prev, run_len), _load_block(kp, cur, run_len)], axis=0).astype(BF16)
                v = jnp.concatenate([_load_block(vp, prev, run_len), _load_block(vp, cur, run_len)], axis=0).astype(BF16)
                half = HEAD_DIM // 2
                lse2 = jnp.concatenate([ld[:, 0:1], ld[:, HEAD_DIM:HEAD_DIM + 1]], axis=0)
                delta2 = jnp.concatenate([ld[:, half:half + 1], ld[:, HEAD_DIM + half:HEAD_DIM + half + 1]], axis=0)
                q2, do2 = _stack_heads(q, low), _stack_heads(dout, low)
                s = lax.dot_general(q2, k, _NT, preferred_element_type=F32) * (HEAD_DIM ** -0.5)
                p = jnp.exp(s + b_ref[2 * branch + variant].reshape(2 * ATT_BLOCK, 2 * ATT_BLOCK) - lse2)
                dp = lax.dot_general(do2, v, _NT, preferred_element_type=F32)
                ds = p * (dp - delta2)
                db_ref[branch] += ds.reshape(2, ATT_BLOCK, 2 * ATT_BLOCK)
                dsb = (ds * (HEAD_DIM ** -0.5)).astype(BF16)
                dq = _unstack_heads(jnp.dot(dsb, k, preferred_element_type=F32), low)
                dk = lax.dot_general(dsb, q2, _TN, preferred_element_type=F32)
                dv = lax.dot_general(p.astype(BF16), do2, _TN, preferred_element_type=F32)
                _store_block(dqp, cur, run_len, dq, add=True)
                _store_block(dkp, prev, run_len, dk[:ATT_BLOCK], add=True)
                _store_block(dvp, prev, run_len, dv[:ATT_BLOCK], add=True)
                _store_block(dkp, cur, run_len, dk[ATT_BLOCK:], add=True)
                _store_block(dvp, cur, run_len, dv[ATT_BLOCK:], add=True)
                return carry

            lax.fori_loop(0, n_iter, step, 0, unroll=ATTN_BWD_UNROLL)

        _ungroup(dq_ref, dqp, L16)
        _ungroup(dk_ref, dkp, L16)
        _ungroup(dv_ref, dvp, L16)

    col = lambda part: pl.BlockSpec((S, PAIR), lambda hp: (0, part * N_PAIRS + hp))
    one = pl.BlockSpec((S, PAIR), lambda hp: (0, hp))
    return pl.pallas_call(
        body, name=name, grid=(N_PAIRS,),
        in_specs=[col(0), col(1), col(2), one, one, one,
                  pl.BlockSpec((6, 2, ATT_BLOCK, 2 * ATT_BLOCK), lambda hp: (0, hp, 0, 0))],
        out_specs=[one, one, one, pl.BlockSpec((3, 2, ATT_BLOCK, 2 * ATT_BLOCK), lambda hp: (0, hp, 0, 0))],
        out_shape=[_sds((S, D_MODEL), F32)] * 3 + [_sds((3, N_HEADS, ATT_BLOCK, 2 * ATT_BLOCK), F32)],
        scratch_shapes=[pltpu.VMEM((S, PAIR), F32)] * 8,
        compiler_params=pltpu.CompilerParams(dimension_semantics=("parallel",), vmem_limit_bytes=ATTN_VMEM_LIMIT_BYTES),
    )(qkvn, qkvn, qkvn, att, datt, lse, bias)


def adamw(w, g, m, v, name):
    n, R, C = w.shape

    def body(w_ref, g_ref, m_ref, v_ref, d_ref, nm_ref, nv_ref, go_ref):
        gv = g_ref[...]
        go_ref[...] = gv
        m2 = ADAM_B1 * m_ref[...] + (1.0 - ADAM_B1) * gv
        v2 = ADAM_B2 * v_ref[...] + (1.0 - ADAM_B2) * (gv * gv)
        m_hat = m2 / (1.0 - ADAM_B1 ** ADAM_STEP)
        v_hat = v2 / (1.0 - ADAM_B2 ** ADAM_STEP)
        d_ref[...] = -ADAM_LR * (m_hat / (jnp.sqrt(v_hat) + ADAM_EPS) + ADAM_WD * w_ref[...])
        nm_ref[...] = m2
        nv_ref[...] = v2

    tr = R
    while tr * C * 4 > (1 << 21) and tr % 16 == 0:
        tr //= 2
    spec = pl.BlockSpec((None, tr, C), lambda i, r: (i, r, 0))
    return pl.pallas_call(
        body, name=name, grid=(n, R // tr), in_specs=[spec] * 4, out_specs=[spec] * 4,
        out_shape=[_sds((n, R, C), F32)] * 4, compiler_params=_params("parallel", "parallel"),
    )(w, g, m, v)


ANY = pl.BlockSpec(memory_space=pl.ANY)


def _coords():
    return lax.axis_index("x"), lax.axis_index("y"), lax.axis_index("c")


def _other_chips(mx, my):
    return [(1 - mx, my), (mx, 1 - my), (1 - mx, 1 - my)]


def _remote(src, dst, send, recv, dev):
    return pltpu.make_async_remote_copy(src_ref=src, dst_ref=dst, send_sem=send, recv_sem=recv, device_id=dev,
                                        device_id_type=MESH)


HBM =pl.BlockSpec(memory_space=pltpu.HBM)
SEM = pl.BlockSpec(memory_space=pltpu.SEMAPHORE)
_SPLIT_COPY = pltpu.CompilerParams(has_side_effects=pltpu.SideEffectType.DATAFLOW_SIDE_EFFECTING)


def _in_hbm(a):
    return pltpu.with_memory_space_constraint(a, pltpu.HBM)


def cast_into_slot(w, layer, chip_core, name, dtype=BF16):
    _, _, hR, C = w.shape

    def body(s_ref, w_ref, o_ref):
        del s_ref
        o_ref[...] = w_ref[...].astype(dtype)

    grid_spec = pltpu.PrefetchScalarGridSpec(
        num_scalar_prefetch=1, grid=(2,),
        in_specs=[pl.BlockSpec((None, None, hR, C), lambda h, s: (layer, h, 0, 0))],
        out_specs=pl.BlockSpec((None, None, hR, C), lambda h, s: (s[0], h, 0, 0)))
    return pl.pallas_call(body, name=name, grid_spec=grid_spec, out_shape=_sds((N_CHIPS, 2, hR, C), dtype),
                          compiler_params=_params("parallel"))(chip_core, w)


def gather_start(lands, groups, name):
    n = len(lands)
    n_groups = len(groups)

    def body(*refs):
        ins = refs[:n]
        sems = refs[n:n + 2 * n_groups]
        token = refs[-1]
        mx, my, mc = _coords()
        chip = 2 * mx + my
        for g, members in enumerate(groups):
            send, recv = sems[2 * g], sems[2 * g + 1]
            for i, a in enumerate(members):
                mine = ins[a].at[chip, mc]
                for k, (px, py) in enumerate(_other_chips(mx, my)):
                    _remote(mine, mine, send.at[3 * i + k], recv.at[3 * i + k], (px, py, mc)).start()
        token[...] = jnp.zeros_like(token)

    sem_shapes = []
    for members in groups:
        sem_shapes += [pltpu.SemaphoreType.DMA((3 * len(members),))] * 2
    outs = pl.pallas_call(
        body, name=name, in_specs=[HBM] * n,
        out_specs=[SEM] * (2 * n_groups) + [HBM] * n + [pl.BlockSpec(memory_space=pltpu.VMEM)],
        out_shape=sem_shapes + [pltpu.HBM(a.shape, a.dtype) for a in lands] + [_sds((SUBLANES, LANES), F32)],
        input_output_aliases={a: 2 * n_groups + a for a in range(n)}, compiler_params=_SPLIT_COPY,
    )(*[_in_hbm(a) for a in lands])
    sems = [(outs[2 * g], outs[2 * g + 1]) for g in range(n_groups)]
    return sems, list(outs[2 * n_groups:2 * n_groups + n]), outs[-1]


def gather_forward(lands, sems, after, name):
    n = len(lands)

    def body(*refs):
        ins = refs[:n]
        send, recv = refs[n], refs[n + 1]
        fsend, frecv = refs[n + 3], refs[n + 4]
        mx, my, mc = _coords()
        for i in range(n):
            for k, (px, py) in enumerate(_other_chips(mx, my)):
                landed = ins[i].at[2 * px + py, mc]
                cp = _remote(landed, landed, send.at[3 * i + k], recv.at[3 * i + k], (px, py, mc))
                cp.wait_send()
                cp.wait_recv()
                _remote(landed, landed, fsend.at[3 * i + k], frecv.at[3 * i + k], (mx, my, 1 - mc)).start()

    outs = pl.pallas_call(
        body, name=name, in_specs=[HBM] * n + [SEM, SEM, ANY], out_specs=[SEM, SEM] + [HBM] * n,
        out_shape=[pltpu.SemaphoreType.DMA((3 * n,))] * 2 + [pltpu.HBM(a.shape, a.dtype) for a in lands],
        input_output_aliases={a: 2 + a for a in range(n)}, compiler_params=_SPLIT_COPY,
    )(*lands, sems[0], sems[1], after)
    return (outs[0], outs[1]), list(outs[2:])


def gather_wait(lands, sems, after, name):
    n = len(lands)

    def body(*refs):
        ins = refs[:n]
        fsend, frecv = refs[n], refs[n + 1]
        mx, my, mc = _coords()
        for i in range(n):
            for k, (px, py) in enumerate(_other_chips(mx, my)):
                theirs = ins[i].at[2 * px + py, 1 - mc]
                cp = _remote(theirs, theirs, fsend.at[3 * i + k], frecv.at[3 * i + k], (mx, my, 1 - mc))
                cp.wait_send()
                cp.wait_recv()

    outs = pl.pallas_call(
        body, name=name, in_specs=[HBM] * n + [SEM, SEM, ANY], out_specs=[HBM] * n,
        out_shape=[pltpu.HBM(a.shape, a.dtype) for a in lands],
        input_output_aliases={a: a for a in range(n)}, compiler_params=_SPLIT_COPY,
    )(*lands, sems[0], sems[1], after)
    return list(outs)


def _peers(mx, my, mc):
    return [(1 - mx if k & 4 else mx, 1 - my if k & 2 else my, 1 - mc if k & 1 else mc) for k in range(1, N_DEV)]


def devices_start(x, name):
    def body(x_ref, land_ref, send, recv, x_thru, land_thru):
        mx, my, mc = _coords()
        me = 4 * mx + 2 * my + mc
        for k, peer in enumerate(_peers(mx, my, mc)):
            _remote(x_ref, land_ref.at[me], send.at[k], recv.at[k], peer).start()

    land = lax.empty((N_DEV,) + x.shape, x.dtype)
    outs = pl.pallas_call(
        body, name=name, in_specs=[HBM, HBM], out_specs=[SEM, SEM, HBM, HBM],
        out_shape=[pltpu.SemaphoreType.DMA((N_DEV - 1,))] * 2 + [pltpu.HBM(x.shape, x.dtype), pltpu.HBM(land.shape, x.dtype)],
        input_output_aliases={0: 2, 1: 3}, compiler_params=_SPLIT_COPY,
    )(_in_hbm(x), _in_hbm(land))
    return (outs[0], outs[1]), outs[2], outs[3]


def devices_wait(x, land, sems, after, name):
    def body(x_ref, land_ref, send, recv, after_ref, x_thru, land_thru):
        mx, my, mc = _coords()
        for k, (px, py, pc) in enumerate(_peers(mx, my, mc)):
            cp = _remote(x_ref, land_ref.at[4 * px + 2 * py + pc], send.at[k], recv.at[k], (px, py, pc))
            cp.wait_send()
            cp.wait_recv()

    outs = pl.pallas_call(
        body, name=name, in_specs=[HBM, HBM, SEM, SEM, ANY], out_specs=[HBM, HBM],
        out_shape=[pltpu.HBM(x.shape, x.dtype), pltpu.HBM(land.shape, land.dtype)],
        input_output_aliases={0: 0, 1: 1}, compiler_params=_SPLIT_COPY,
    )(x, land, sems[0], sems[1], after)
    return outs[0], outs[1]


def device_sum(land, own, me, name):
    _, R, C = land.shape

    def body(s_ref, l_ref, o_ref_in, o_ref):
        acc = None
        for q in range(N_DEV):
            term = jnp.where(s_ref[0] == q, o_ref_in[...], l_ref[q])
            acc = term if acc is None else acc + term
        o_ref[...] = acc

    grid_spec = pltpu.PrefetchScalarGridSpec(
        num_scalar_prefetch=1, grid=(1,),
        in_specs=[pl.BlockSpec((N_DEV, R, C), lambda i, s: (0, 0, 0)), pl.BlockSpec((R, C), lambda i, s: (0, 0))],
        out_specs=pl.BlockSpec((R, C), lambda i, s: (0, 0)))
    return pl.pallas_call(body, name=name, grid_spec=grid_spec, out_shape=_sds((R, C), F32),
                          compiler_params=_params("arbitrary"))(me, land, own)


def reduce_send(grads, name):
    n = len(grads)

    def body(*refs):
        ins, lands = refs[:n], refs[n:2 * n]
        send, recv = refs[2 * n], refs[2 * n + 1]
        mx, my, mc = _coords()
        me = 4 * mx + 2 * my + mc
        for a in range(n):
            for k, (px, py, pc) in enumerate(_peers(mx, my, mc)):
                _remote(ins[a].at[2 * px + py, pc], lands[a].at[me], send.at[7 * a + k], recv.at[7 * a + k], (px, py, pc)).start()

    lands = [lax.empty((N_DEV,) + g.shape[2:], g.dtype) for g in grads]
    outs = pl.pallas_call(
        body, name=name, in_specs=[HBM] * (2 * n), out_specs=[SEM, SEM] + [HBM] * (2 * n),
        out_shape=[pltpu.SemaphoreType.DMA((7 * n,))] * 2 + [pltpu.HBM(a.shape, a.dtype) for a in grads + lands],
        input_output_aliases={a: 2 + a for a in range(2 * n)}, compiler_params=_SPLIT_COPY,
    )(*[_in_hbm(a) for a in grads + lands])
    return (outs[0], outs[1]), list(outs[2:2 + n]), list(outs[2 + n:])


def reduce_wait(grads, lands, sems, after, name):
    n = len(grads)

    def body(*refs):
        ins, zones = refs[:n], refs[n:2 * n]
        send, recv = refs[2 * n], refs[2 * n + 1]
        mx, my, mc = _coords()
        for a in range(n):
            for k, (px, py, pc) in enumerate(_peers(mx, my, mc)):
                cp = _remote(ins[a].at[2 * px + py, pc], zones[a].at[4 * px + 2 * py + pc], send.at[7 * a + k],
                             recv.at[7 * a + k], (px, py, pc))
                cp.wait_send()
                cp.wait_recv()

    outs = pl.pallas_call(
        body, name=name, in_specs=[HBM] * (2 * n) + [SEM, SEM, ANY], out_specs=[HBM] * (2 * n),
        out_shape=[pltpu.HBM(a.shape, a.dtype) for a in grads + lands],
        input_output_aliases={a: a for a in range(2 * n)}, compiler_params=_SPLIT_COPY,
    )(*grads, *lands, sems[0], sems[1], after)
    return list(outs[:n]), list(outs[n:])


def reduce_sum(land, grad, place, name, into=None, layer=None):
    _, hR, C = land.shape
    tr = hR
    while N_DEV * tr * C * 2 > (6 << 20) and tr % 32 == 0:
        tr //= 2

    def body(s_ref, l_ref, g_ref, *rest):
        o_ref = rest[-1]
        own = g_ref[...].astype(F32)
        acc = None
        for q in range(N_DEV):
            term = jnp.where(s_ref[2] == q, own, l_ref[q].astype(F32))
            acc = term if acc is None else acc + term
        o_ref[...] = acc

    in_specs = [pl.BlockSpec((N_DEV, tr, C), lambda i, s: (0, i, 0)),
                pl.BlockSpec((None, None, tr, C), lambda i, s: (s[0], s[1], i, 0))]
    args = [place, land, grad]
    aliases = {}
    if layer is None:
        out_spec = pl.BlockSpec((None, tr, C), lambda i, s: (s[1], i, 0))
        out_shape = _sds((2, hR, C), F32)
    else:
        out_spec = pl.BlockSpec((None, None, tr, C), lambda i, s: (layer, s[1], i, 0))
        out_shape = _sds((2, 2, hR, C), F32)
        if into is not None:
            in_specs.append(ANY)
            args.append(into)
            aliases = {3: 0}
    grid_spec = pltpu.PrefetchScalarGridSpec(num_scalar_prefetch=1, grid=(hR // tr,), in_specs=in_specs, out_specs=out_spec)
    return pl.pallas_call(body, name=name, grid_spec=grid_spec, out_shape=out_shape, input_output_aliases=aliases,
                          compiler_params=_params("arbitrary"))(*args)


def join_halves(arrays, name):
    n = len(arrays)
    pieces = [(a, l) for a, arr in enumerate(arrays) for l in (range(arr.shape[0]) if arr.ndim == 4 else [None])]

    def body(*refs):
        ins = refs[:n]
        send, recv = refs[2 * n:]
        mx, my, mc = _coords()

        def half(a, l, h):
            return ins[a].at[h] if l is None else ins[a].at[l, h]

        sends = [_remote(half(a, l, mc), half(a, l, mc), send.at[i], recv.at[i], (mx, my, 1 - mc))
                 for i, (a, l) in enumerate(pieces)]
        for cp in sends:
            cp.start()
        for i, (a, l) in enumerate(pieces):
            theirs = half(a, l, 1 - mc)
            _remote(theirs, theirs, send.at[i], recv.at[i], (mx, my, 1 - mc)).wait_recv()
        for cp in sends:
            cp.wait_send()

    return pl.pallas_call(
        body, name=name, in_specs=[ANY] * n, out_specs=[ANY] * n, out_shape=[_sds(a.shape, a.dtype) for a in arrays],
        input_output_aliases={a: a for a in range(n)},
        scratch_shapes=[pltpu.SemaphoreType.DMA((len(pieces),)), pltpu.SemaphoreType.DMA((len(pieces),))],
    )(*arrays)


LANES = 128
SUBLANES = 8


def _n_rows(shape):
    rows = -(-int(np.prod(shape)) // LANES)
    return -(-rows // SUBLANES) * SUBLANES


def _as_rows(a):
    flat = a.reshape(-1)
    rows = _n_rows(a.shape)
    return jnp.pad(flat, (0, rows * LANES - flat.shape[0])).reshape(rows, LANES)


def _pack(arrays):
    return jnp.concatenate([_as_rows(a) for a in arrays], axis=0)


def _unpack(rows, shapes):
    out, r0 = [], 0
    for s in shapes:
        n = _n_rows(s)
        out.append(rows[r0:r0 + n].reshape(-1)[:int(np.prod(s))].reshape(s))
        r0 += n
    return out


REPLICATED_SMALL = [("rel_bias", (32, 16)), ("even_norm", (1, 1024)), ("even_pool_w", (1, 4, 128, 128)),
                    ("even_pool_scale", (1, 512)), ("odd_q_norm", (1, 64)), ("odd_k_norm", (1, 64)),
                    ("ffn_norm", (2, 1024)), ("ffn_conv_b", (2, 5632))]
SHARDED_SMALL = [("even_conv_w", (1, 3, 128)), ("odd_norm", (1, 256)), ("ffn_conv_w", (2, 3, 1408))]
BIG = ["even_w_in", "even_w_out", "odd_w_qkv", "odd_w_o", "ffn_w_up", "ffn_w_down"]
WEIGHT_ORDER = ["rel_bias", "even_norm", "even_w_in", "even_conv_w", "even_pool_w", "even_pool_scale", "even_w_out",
                "odd_norm", "odd_w_qkv", "odd_q_norm", "odd_k_norm", "odd_w_o", "ffn_norm", "ffn_w_up", "ffn_conv_w",
                "ffn_conv_b", "ffn_w_down"]


def kernel(x, rel_bias, even_norm, even_w_in, even_conv_w, even_pool_w, even_pool_scale, even_w_out, odd_norm, odd_w_qkv, odd_q_norm, odd_k_norm, odd_w_o, ffn_norm, ffn_w_up, ffn_conv_w, ffn_conv_b, ffn_w_down, loss_target, m_rel_bias, m_even_norm, m_even_w_in, m_even_conv_w, m_even_pool_w, m_even_pool_scale, m_even_w_out, m_odd_norm, m_odd_w_qkv, m_odd_q_norm, m_odd_k_norm, m_odd_w_o, m_ffn_norm, m_ffn_w_up, m_ffn_conv_w, m_ffn_conv_b, m_ffn_w_down, v_rel_bias, v_even_norm, v_even_w_in, v_even_conv_w, v_even_pool_w, v_even_pool_scale, v_even_w_out, v_odd_norm, v_odd_w_qkv, v_odd_q_norm, v_odd_k_norm, v_odd_w_o, v_ffn_norm, v_ffn_w_up, v_ffn_conv_w, v_ffn_conv_b, v_ffn_w_down):
    W = dict(rel_bias=rel_bias, even_norm=even_norm, even_w_in=even_w_in, even_conv_w=even_conv_w, even_pool_w=even_pool_w,
             even_pool_scale=even_pool_scale, even_w_out=even_w_out, odd_norm=odd_norm, odd_w_qkv=odd_w_qkv,
             odd_q_norm=odd_q_norm, odd_k_norm=odd_k_norm, odd_w_o=odd_w_o, ffn_norm=ffn_norm, ffn_w_up=ffn_w_up,
             ffn_conv_w=ffn_conv_w, ffn_conv_b=ffn_conv_b, ffn_w_down=ffn_w_down)
    M1 = dict(rel_bias=m_rel_bias, even_norm=m_even_norm, even_w_in=m_even_w_in, even_conv_w=m_even_conv_w,
              even_pool_w=m_even_pool_w, even_pool_scale=m_even_pool_scale, even_w_out=m_even_w_out, odd_norm=m_odd_norm,
              odd_w_qkv=m_odd_w_qkv, odd_q_norm=m_odd_q_norm, odd_k_norm=m_odd_k_norm, odd_w_o=m_odd_w_o,
              ffn_norm=m_ffn_norm, ffn_w_up=m_ffn_w_up, ffn_conv_w=m_ffn_conv_w, ffn_conv_b=m_ffn_conv_b,
              ffn_w_down=m_ffn_w_down)
    M2 = dict(rel_bias=v_rel_bias, even_norm=v_even_norm, even_w_in=v_even_w_in, even_conv_w=v_even_conv_w,
              even_pool_w=v_even_pool_w, even_pool_scale=v_even_pool_scale, even_w_out=v_even_w_out, odd_norm=v_odd_norm,
              odd_w_qkv=v_odd_w_qkv, odd_q_norm=v_odd_q_norm, odd_k_norm=v_odd_k_norm, odd_w_o=v_odd_w_o,
              ffn_norm=v_ffn_norm, ffn_w_up=v_ffn_w_up, ffn_conv_w=v_ffn_conv_w, ffn_conv_b=v_ffn_conv_b,
              ffn_w_down=v_ffn_w_down)
    mx, my, mc = _coords()
    chip = 2 * mx + my
    me = 4 * mx + 2 * my + mc
    place = jnp.stack([chip, mc, me]).astype(jnp.int32)
    xs, target = x[0], loss_target[0]

    def halves(w):
        return w.reshape((w.shape[0], 2, w.shape[-2] // 2, w.shape[-1]))

    lands = [cast_into_slot(halves(even_w_in), 0, place, "cast_w_in"), cast_into_slot(halves(even_w_out), 0, place, "cast_w_out"),
             cast_into_slot(halves(ffn_w_up), 0, place, "cast_w_up0"), cast_into_slot(halves(ffn_w_down), 0, place, "cast_w_down0"),
             cast_into_slot(halves(odd_w_qkv), 0, place, "cast_w_qkv"), cast_into_slot(halves(odd_w_o), 0, place, "cast_w_o"),
             cast_into_slot(halves(ffn_w_up), 1, place, "cast_w_up1"), cast_into_slot(halves(ffn_w_down), 1, place, "cast_w_down1")]
    small_rows = jnp.pad(_pack([even_conv_w, odd_norm, ffn_conv_w]), ((0, SUBLANES), (0, 0)))
    lands.append(cast_into_slot(small_rows.reshape(1, 2, small_rows.shape[0] // 2, LANES), 0, place, "small_into_slot", dtype=F32))
    groups = [[0, 1, 8], [2, 3], [4, 5], [6, 7]]
    gather_sems, lands, token = gather_start(lands, groups, "gather_start")
    even_norm_after_start = even_norm + token[0:1, 0:1]

    def gathered(group, tag, after_landing, after_passing):
        mine = [lands[a] for a in groups[group]]
        sems, arrays = gather_forward(mine, gather_sems[group], after_landing, "gather_forward_" + tag)
        return gather_wait(arrays, sems, after_passing, "gather_wait_" + tag)

    pool_w = cast_bf16(even_pool_w[0], "cast_pool_w")
    gqk = jnp.stack([jnp.tile(odd_q_norm[0], N_HEADS), jnp.tile(odd_k_norm[0], N_HEADS),
                     jnp.ones((D_MODEL,), F32)])[:, None, :]
    bias = bias_expand(rel_bias.T, "bias_expand").reshape(6, N_HEADS, ATT_BLOCK, 2 * ATT_BLOCK)
    xn0 = rmsnorm_fwd(xs, even_norm_after_start, "even_norm")
    got = gathered(0, "even", bias, xn0)
    w_in = got[0].reshape(N_CHIPS, 1, D_MODEL, EVEN_IN // N_CHIPS)
    w_out = got[1].reshape(1, 1, D_MODEL, D_MODEL)
    small = got[2].reshape(N_CHIPS, small_rows.shape[0], LANES)
    conv_w_full = small[:, 0:3].transpose(1, 0, 2).reshape(3, A_WIDTH)
    odd_norm_full = small[:, 8:10].reshape(1, D_MODEL)
    ffn_cw_full = small[:, 16:82].reshape(N_CHIPS, 2, 3, 2 * D_FF // N_CHIPS).transpose(1, 2, 0, 3).reshape(2, 3, 2 * D_FF)

    def ffn_fwd(l, xin, xn):
        up = mm_nn(xn, w_up[l], f"ffn{l}_up", out_dtype=BF16)
        u, act = glu_fwd(up, ffn_cw_full[l], ffn_conv_b[l:l + 1], f"ffn{l}_glu")
        return act, (xin, xn, up, u, act)

    def ffn_weights(got):
        return got[0].reshape(N_CHIPS, 1, D_MODEL, 2 * D_FF // N_CHIPS), got[1].reshape(1, 1, D_FF, D_MODEL)

    w_up, w_down = [None, None], [None, None]
    proj = mm_nn(xn0, w_in, "even_in")
    mix = mixer_fwd(proj, conv_w_full, pool_w, even_pool_scale, "even_mixer")
    x1, xn1 = mm_res_norm(mix, w_out, xs, ffn_norm[0:1], "even_out")
    w_up[0], w_down[0] = ffn_weights(gathered(1, "ffn0", proj, x1))
    act0, ffn0 = ffn_fwd(0, x1, xn1)
    x2, xn2 = mm_res_norm(act0, w_down[0], x1, odd_norm_full, "ffn0_down")
    got = gathered(2, "odd", x1, x2)
    w_qkv = got[0].reshape(N_CHIPS, 1, D_MODEL, 3 * D_MODEL // N_CHIPS)
    w_o = got[1].reshape(1, 1, D_MODEL, D_MODEL)
    qkv = mm_nn(xn2, w_qkv, "odd_qkv")
    qkvn = qknorm_fwd(qkv, gqk, "odd_qknorm")
    att, lse = attn_fwd(qkvn, bias, "attn_fwd")
    x3, xn3 = mm_res_norm(att, w_o, x2, ffn_norm[1:2], "odd_out")
    w_up[1], w_down[1] = ffn_weights(gathered(3, "ffn1", x2, x3))
    act1, ffn1 = ffn_fwd(1, x3, xn3)
    dy, dyb, sq = mm_res_loss(act1, w_down[1], x3, target, "ffn1_down_loss")
    loss = lax.psum(0.5 * jnp.sum(sq) * (1.0 / D_MODEL), ("x", "y", "c"))

    def ffn_bwd(l, dy, dyb, saved):
        xin, xn, up, u, act = saved
        dw_down = mm_tn(act, dyb, f"ffn{l}_dw_down", J=1, tk=D_FF // 2, tm=1024)
        dact = mm_nt(dyb, w_down[l], f"ffn{l}_dact", tr=D_FF // 2, out_dtype=BF16, tm=1024)
        dup, dcw, dcb = glu_bwd(up, u, dact, ffn_cw_full[l], f"ffn{l}_glu_bwd")
        dw_up = mm_tn(xn, dup, f"ffn{l}_dw_up", J=N_CHIPS, tk=512, tm=1024, jb=2)
        dx, dxb, dg = mm_nt_norm_bwd(dup, w_up[l], xin, ffn_norm[l:l + 1], dy, f"ffn{l}_dx")
        return dx, dxb, (dw_down, dw_up, dcw, dcb, dg)

    def quarters(g):
        return g.reshape(N_CHIPS, 2, g.shape[0] * g.shape[1] // (2 * N_CHIPS), g.shape[-1])

    def reduce_start(grads, tag, then):
        sems, parts, zones = reduce_send([quarters(g) for g in grads], "reduce_send_" + tag)
        then, parts = lax.optimization_barrier((then, parts))
        return (sems, parts, zones), then

    dx3, dx3b, g_ffn1 = ffn_bwd(1, dy, dyb, ffn1)
    red_ffn1, (dx3, dx3b) = reduce_start([g_ffn1[1], g_ffn1[0]], "ffn1", (dx3, dx3b))
    dw_o = mm_tn(att, dx3b, "odd_dw_o", J=1, tk=512, tm=1024)
    datt = mm_nt(dx3b, w_o, "odd_datt", tr=D_MODEL, out_dtype=BF16)
    dq, dk, dv, dbias = attn_bwd(qkvn, att, datt, lse, bias, "attn_bwd")
    dqkv, dgqk = qknorm_bwd(qkv, dq, dk, dv, gqk, "odd_qknorm_bwd")
    dw_qkv = mm_tn(xn2, dqkv, "odd_dw_qkv", J=N_CHIPS, tk=512, tm=1024)
    red_odd, dqkv = reduce_start([dw_qkv, dw_o], "odd", dqkv)
    dx2, dx2b, dg_odd = mm_nt_norm_bwd(dqkv, w_qkv, x2, odd_norm_full, dx3, "odd_dx")
    (dx2, dx2b), loss = lax.optimization_barrier(((dx2, dx2b), loss))
    dx1, dx1b, g_ffn0 = ffn_bwd(0, dx2, dx2b, ffn0)
    red_ffn0, (dx1, dx1b) = reduce_start([g_ffn0[1], g_ffn0[0]], "ffn0", (dx1, dx1b))
    dw_out = mm_tn(mix, dx1b, "even_dw_out", J=1, tk=512, tm=1024)
    dmix = mm_nt(dx1b, w_out, "even_dmix", tr=D_MODEL)
    dproj, dcw_even, dpw, dps = mixer_bwd(proj, dmix, conv_w_full, pool_w, even_pool_scale, "even_mixer_bwd")
    dw_in = mm_tn(xn0, dproj, "even_dw_in", J=N_CHIPS, tk=512, tm=1024)
    grad_x, _, dg_even = mm_nt_norm_bwd(dproj, w_in, xs, even_norm, dx1, "even_dx")
    d_rel = jnp.sum(bias_reduce(dbias.reshape(3, N_HEADS, 2 * ATT_BLOCK * ATT_BLOCK), "bias_reduce"), axis=0).T

    red_even, grad_x = reduce_start([dw_in, dw_out], "even", grad_x)

    dcw_sh = dcw_even.reshape(3, N_CHIPS, A_WIDTH // N_CHIPS).transpose(1, 0, 2)
    don_sh = dg_odd.reshape(N_CHIPS, D_MODEL // N_CHIPS)
    dfcw = jnp.stack([g_ffn0[2], g_ffn1[2]])
    dfcw_sh = dfcw.reshape(2, 3, N_CHIPS, 2 * D_FF // N_CHIPS).transpose(2, 0, 1, 3)
    rep_grads = [d_rel, dg_even, dpw[None], dps, _head_sum(dgqk[0]), _head_sum(dgqk[1]),
                 jnp.concatenate([g_ffn0[4], g_ffn1[4]], axis=0), jnp.concatenate([g_ffn0[3], g_ffn1[3]], axis=0)]
    rep_rows = _pack(rep_grads)
    shard_rows = jnp.concatenate([_pack([dcw_sh[j], don_sh[j], dfcw_sh[j]]) for j in range(N_CHIPS)], axis=0)
    n_rep, n_shard = rep_rows.shape[0], shard_rows.shape[0] // N_CHIPS
    small_sems, small_rows, small_land = devices_start(jnp.concatenate([rep_rows, shard_rows], axis=0), "small_grads_start")
    grad_x, small_rows = lax.optimization_barrier((grad_x, small_rows))

    def reduce_end(red, tag, after):
        sems, parts, zones = red
        parts, zones = reduce_wait(parts, zones, sems, after, "reduce_wait_" + tag)
        return zones, parts

    z_ffn1, p_ffn1 = reduce_end(red_ffn1, "ffn1", grad_x)
    z_odd, p_odd = reduce_end(red_odd, "odd", grad_x)
    r_qkv = reduce_sum(z_odd[0], p_odd[0], place, "reduce_sum_w_qkv")
    r_o = reduce_sum(z_odd[1], p_odd[1], place, "reduce_sum_w_o")
    r_up = reduce_sum(z_ffn1[0], p_ffn1[0], place, "reduce_sum_w_up1", layer=1)
    r_down = reduce_sum(z_ffn1[1], p_ffn1[1], place, "reduce_sum_w_down1", layer=1)
    r_qkv, r_o, r_up, r_down = lax.optimization_barrier((r_qkv, r_o, r_up, r_down))
    z_ffn0, p_ffn0 = reduce_end(red_ffn0, "ffn0", r_down)
    r_up = reduce_sum(z_ffn0[0], p_ffn0[0], place, "reduce_sum_w_up0", into=r_up, layer=0)
    r_down = reduce_sum(z_ffn0[1], p_ffn0[1], place, "reduce_sum_w_down0", into=r_down, layer=0)
    later = ["odd_w_qkv", "odd_w_o", "ffn_w_up", "ffn_w_down"]
    joined = join_halves([r_qkv, r_o, r_up, r_down], "grads_join_late_layers")
    G = {nm: g.reshape(W[nm].shape) for nm, g in zip(later, joined)}

    D_, NM, NV = {}, {}, {}

    def update(nm):
        as3 = lambda a: a.reshape((-1,) + a.shape[-2:])
        outs = adamw(as3(W[nm]), as3(G[nm]), as3(M1[nm]), as3(M2[nm]), "adamw_" + nm)
        D_[nm], NM[nm], NV[nm], G[nm] = [o.reshape(W[nm].shape) for o in outs]

    def all_before(names):
        tied = lax.optimization_barrier([D_[nm] for nm in names])
        for nm, d in zip(names, tied):
            D_[nm] = d
        return tied[0]

    for nm in later:
        update(nm)
    z_even, p_even = reduce_end(red_even, "even", all_before(later))
    joined = join_halves([reduce_sum(z_even[0], p_even[0], place, "reduce_sum_w_in"),
                          reduce_sum(z_even[1], p_even[1], place, "reduce_sum_w_out")], "grads_join_first_layer")
    first = ["even_w_in", "even_w_out"]
    for nm, g in zip(first, joined):
        G[nm] = g.reshape(W[nm].shape)
        update(nm)
    small_rows, small_land = devices_wait(small_rows, small_land, small_sems, all_before(first), "small_grads_wait")
    small_sum = device_sum(small_land, small_rows, place[2:3], "small_grads_sum")
    mine = lax.dynamic_slice_in_dim(small_sum, n_rep + chip * n_shard, n_shard, axis=0)
    g_small = jnp.concatenate([small_sum[:n_rep], mine], axis=0)
    small_names = [n for n, _ in REPLICATED_SMALL + SHARDED_SMALL]
    small_shapes = [s for _, s in REPLICATED_SMALL + SHARDED_SMALL]
    G.update(dict(zip(small_names, _unpack(g_small, small_shapes))))
    packs = [_pack([d[n] for n in small_names])[None] for d in (W, M1, M2)]
    outs = adamw(packs[0], g_small[None], packs[1], packs[2], "adamw_small")
    for dst, o in zip((D_, NM, NV), outs[:3]):
        dst.update(dict(zip(small_names, _unpack(o[0], small_shapes))))

    return (loss, grad_x[None], *[G[n] for n in WEIGHT_ORDER], *[D_[n] for n in WEIGHT_ORDER],
            *[NM[n] for n in WEIGHT_ORDER], *[NV[n] for n in WEIGHT_ORDER])


def _head_sum(dg):
    return jnp.sum(dg.reshape(N_HEADS, HEAD_DIM), axis=0, keepdims=True)
```

```python
import functools
import math

import numpy as np
import jax
import jax.numpy as jnp
from jax import lax
from jax.experimental import pallas as pl
from jax.experimental.pallas import tpu as pltpu

F32 = jnp.float32
BF16 = jnp.bfloat16

D_MODEL = 1024
N_HEADS = 16
HEAD_DIM = 64
A_WIDTH = 512
POOL_WINDOWS = (2, 4, 8, 16)
POOL_GROUP = 128
EVEN_IN = 2048
D_FF = 2816
DILATED_PAIRS = ((128, 1), (512, 4), (2048, 16))
ATT_BLOCK = 128
N_REL_BUCKETS = 32
REL_MAX_DISTANCE = 2048
EPS = 1e-6
MASK_VALUE = -1e30
ADAM_LR, ADAM_B1, ADAM_B2, ADAM_EPS, ADAM_WD, ADAM_STEP = 0.001, 0.9, 0.999, 1e-08, 0.01, 10

VMEM_LIMIT_BYTES = 48 * 1024 * 1024
N_CHIPS = 4
N_DEV = 8
MESH = pl.DeviceIdType.MESH


def _params(*sem):
    return pltpu.CompilerParams(dimension_semantics=sem if sem else None, vmem_limit_bytes=VMEM_LIMIT_BYTES)


def _sds(shape, dtype):
    return jax.ShapeDtypeStruct(tuple(shape), dtype)


def cast_bf16(x, name, tr=None):
    lead, (R, C) = x.shape[:-2], x.shape[-2:]
    n = int(np.prod(lead)) if lead else 1
    x3 = x.reshape((n, R, C))
    tr = tr or R

    def body(x_ref, o_ref):
        o_ref[...] = x_ref[...].astype(BF16)

    out = pl.pallas_call(
        body, name=name, grid=(n, R // tr),
        in_specs=[pl.BlockSpec((None, tr, C), lambda i, r: (i, r, 0))],
        out_specs=pl.BlockSpec((None, tr, C), lambda i, r: (i, r, 0)),
        out_shape=_sds((n, R, C), BF16), compiler_params=_params("parallel", "parallel"),
    )(x3)
    return out.reshape(lead + (R, C))


def rmsnorm_fwd(x, g, name, ts=512):
    S, Dm = x.shape

    def body(x_ref, g_ref, o_ref):
        xv = x_ref[...]
        r = lax.rsqrt(jnp.mean(xv * xv, axis=-1, keepdims=True) + EPS)
        o_ref[...] = ((xv * r) * g_ref[...]).astype(BF16)

    return pl.pallas_call(
        body, name=name, grid=(S // ts,),
        in_specs=[pl.BlockSpec((ts, Dm), lambda i: (i, 0)), pl.BlockSpec((1, Dm), lambda i: (0, 0))],
        out_specs=pl.BlockSpec((ts, Dm), lambda i: (i, 0)),
        out_shape=_sds((S, Dm), BF16), compiler_params=_params("parallel"),
    )(x, g)


def mm_nn(a, w, name, layer=0, res=None, out_dtype=F32, tm=1024):
    M, K = a.shape
    J, _, _, Ns = w.shape

    def body(*refs):
        a_ref, w_ref = refs[0], refs[1]
        o_ref = refs[-1]
        acc = jnp.dot(a_ref[...], w_ref[...], preferred_element_type=F32)
        if res is not None:
            acc = refs[2][...] + acc
        o_ref[...] = acc.astype(o_ref.dtype)

    in_specs = [pl.BlockSpec((tm, K), lambda j, m: (m, 0)),
                pl.BlockSpec((None, None, K, Ns), lambda j, m: (j, layer, 0, 0))]
    args = [a, w]
    if res is not None:
        in_specs.append(pl.BlockSpec((tm, Ns), lambda j, m: (m, j)))
        args.append(res)
    return pl.pallas_call(
        body, name=name, grid=(J, M // tm), in_specs=in_specs,
        out_specs=pl.BlockSpec((tm, Ns), lambda j, m: (m, j)),
        out_shape=_sds((M, J * Ns), out_dtype), compiler_params=_params("parallel", "parallel"),
    )(*args)


def mm_res_norm(a, w, res, gain, name, tm=1024):
    M, K = a.shape
    Dm = w.shape[-1]

    def body(a_ref, w_ref, r_ref, g_ref, y_ref, yn_ref):
        y = r_ref[...] + jnp.dot(a_ref[...], w_ref[...], preferred_element_type=F32)
        y_ref[...] = y
        r = lax.rsqrt(jnp.mean(y * y, axis=-1, keepdims=True) + EPS)
        yn_ref[...] = ((y * r) * g_ref[...]).astype(BF16)

    row = pl.BlockSpec((tm, Dm), lambda m: (m, 0))
    return pl.pallas_call(
        body, name=name, grid=(M // tm,),
        in_specs=[pl.BlockSpec((tm, K), lambda m: (m, 0)),
                  pl.BlockSpec((None, None, K, Dm), lambda m: (0, 0, 0, 0), pipeline_mode=pl.Buffered(1)),
                  row, pl.BlockSpec((1, Dm), lambda m: (0, 0))],
        out_specs=[row, row], out_shape=[_sds((M, Dm), F32), _sds((M, Dm), BF16)],
        compiler_params=_params("parallel"),
    )(a, w, res, gain)


def mm_res_loss(a, w, res, target, name, tm=512):
    M, K = a.shape
    Dm = w.shape[-1]

    def body(a_ref, w_ref, r_ref, t_ref, d_ref, db_ref, s_ref):
        e = (r_ref[...] + jnp.dot(a_ref[...], w_ref[...], preferred_element_type=F32)) - t_ref[...]
        d = e * (1.0 / Dm)
        d_ref[...] = d
        db_ref[...] = d.astype(BF16)
        part = jnp.sum(e * e, axis=0, keepdims=True)

        @pl.when(pl.program_id(0) == 0)
        def _():
            s_ref[...] = part

        @pl.when(pl.program_id(0) > 0)
        def _():
            s_ref[...] += part

    row = pl.BlockSpec((tm, Dm), lambda m: (m, 0))
    return pl.pallas_call(
        body, name=name, grid=(M // tm,),
        in_specs=[pl.BlockSpec((tm, K), lambda m: (m, 0)),
                  pl.BlockSpec((None, None, K, Dm), lambda m: (0, 0, 0, 0), pipeline_mode=pl.Buffered(1)), row, row],
        out_specs=[row, row, pl.BlockSpec((1, Dm), lambda m: (0, 0))],
        out_shape=[_sds((M, Dm), F32), _sds((M, Dm), BF16), _sds((1, Dm), F32)],
        compiler_params=_params("arbitrary"),
    )(a, w, res, target)


def mm_nt(dy, w, name, tr, layer=0, out_dtype=F32, tm=512):
    M = dy.shape[0]
    J, _, R, Ns = w.shape
    dims = (((1,), (1,)), ((), ()))

    def body(dy_ref, w_ref, o_ref):
        acc = None
        for j in range(J):
            p = lax.dot_general(dy_ref[:, j * Ns:(j + 1) * Ns], w_ref[j], dims, preferred_element_type=F32)
            acc = p if acc is None else acc + p
        o_ref[...] = acc.astype(o_ref.dtype)

    return pl.pallas_call(
        body, name=name, grid=(R // tr, M // tm),
        in_specs=[pl.BlockSpec((tm, J * Ns), lambda r, m: (m, 0)),
                  pl.BlockSpec((J, None, tr, Ns), lambda r, m: (0, layer, r, 0))],
        out_specs=pl.BlockSpec((tm, tr), lambda r, m: (m, r)),
        out_shape=_sds((M, R), out_dtype),
        compiler_params=_params("parallel", "parallel"),
    )(dy, w)


def mm_nt_norm_bwd(dy, w, x, g, dres, name, layer=0, tm=512):
    M = dy.shape[0]
    J, _, Dm, Ns = w.shape
    dims = (((1,), (1,)), ((), ()))

    def body(dy_ref, w_ref, x_ref, g_ref, r_ref, dx_ref, dxb_ref, dg_ref):
        dxn = None
        for j in range(J):
            p = lax.dot_general(dy_ref[:, j * Ns:(j + 1) * Ns], w_ref[j], dims, preferred_element_type=F32)
            dxn = p if dxn is None else dxn + p
        xv = x_ref[...]
        r = lax.rsqrt(jnp.mean(xv * xv, axis=-1, keepdims=True) + EPS)
        gx = dxn * g_ref[...]
        dot = jnp.sum(gx * xv, axis=-1, keepdims=True)
        dx = r_ref[...] + r * gx - xv * ((r * r * r) * (dot * (1.0 / Dm)))
        dx_ref[...] = dx
        dxb_ref[...] = dx.astype(BF16)
        part = jnp.sum(dxn * (xv * r), axis=0, keepdims=True)

        @pl.when(pl.program_id(0) == 0)
        def _():
            dg_ref[...] = part

        @pl.when(pl.program_id(0) > 0)
        def _():
            dg_ref[...] += part

    row = pl.BlockSpec((tm, Dm), lambda m: (m, 0))
    vec = pl.BlockSpec((1, Dm), lambda m: (0, 0))
    return pl.pallas_call(
        body, name=name, grid=(M // tm,),
        in_specs=[pl.BlockSpec((tm, J * Ns), lambda m: (m, 0)),
                  pl.BlockSpec((J, None, Dm, Ns), lambda m: (0, layer, 0, 0), pipeline_mode=pl.Buffered(1)), row, vec, row],
        out_specs=[row, row, vec],
        out_shape=[_sds((M, Dm), F32), _sds((M, Dm), BF16), _sds((1, Dm), F32)],
        compiler_params=_params("arbitrary"),
    )(dy, w, x, g, dres)


def mm_tn(a, dy, name, J, tk, tm=512, jb=None):
    M, K = a.shape
    jb = jb or J
    Ns = dy.shape[1] // J
    N = jb * Ns
    n_m = M // tm
    dims = (((0,), (0,)), ((), ()))

    def body(a_ref, dy_ref, o_ref, acc_ref):
        p = lax.dot_general(a_ref[...], dy_ref[...], dims, preferred_element_type=F32)
        m = pl.program_id(2)

        @pl.when(m == 0)
        def _():
            acc_ref[...] = p

        @pl.when(m > 0)
        def _():
            acc_ref[...] += p

        @pl.when(m == n_m - 1)
        def _():
            for j in range(jb):
                o_ref[j] = acc_ref[:, j * Ns:(j + 1) * Ns].astype(BF16)

    return pl.pallas_call(
        body, name=name, grid=(J // jb, K // tk, n_m),
        in_specs=[pl.BlockSpec((tm, tk), lambda g, k, m: (m, k)), pl.BlockSpec((tm, N), lambda g, k, m: (m, g))],
        out_specs=pl.BlockSpec((jb, tk, Ns), lambda g, k, m: (g, k, 0)),
        out_shape=_sds((J, K, Ns), BF16), scratch_shapes=[pltpu.VMEM((tk, N), F32)],
        compiler_params=_params("parallel", "parallel", "arbitrary"),
    )(a, dy)


HALO = 16


def _shift_down(x, s):
    return pltpu.roll(x, s, 0)


def _shift_up(x, s):
    return pltpu.roll(x, x.shape[0] - s, 0)


def _conv3(z, cw):
    return (_shift_down(z, 2) * cw[0:1] + _shift_down(z, 1) * cw[1:2]) + z * cw[2:3]


def _window_count(first_row, n, k):
    t = first_row + lax.broadcasted_iota(jnp.int32, (n, 1), 0)
    return jnp.clip(t + 1, 1, k).astype(F32)


def mixer_fwd(proj, conv_w, pool_w, pool_scale, name, ts=256):
    S = proj.shape[0]
    n = ts + HALO

    def body(pm_ref, pb_ref, cw_ref, pw_ref, ps_ref, o_ref):
        i = pl.program_id(0)
        before = jnp.where(i > 0, pb_ref[...], 0.0)
        ext = jnp.concatenate([before, pm_ref[...]], axis=0)
        cw = cw_ref[...]
        z = ext[:, 2 * A_WIDTH:3 * A_WIDTH] * ext[:, 0:A_WIDTH]
        cz = _conv3(z, cw)
        ya = pm_ref[:, A_WIDTH:2 * A_WIDTH] * cz[HALO:]
        o_ref[:, 0:A_WIDTH] = ya.astype(BF16)
        for g, k in enumerate(POOL_WINDOWS):
            lo = 3 * A_WIDTH + g * POOL_GROUP
            p = ext[:, lo:lo + POOL_GROUP]
            w = p
            s = 1
            while s < k:
                w = w + _shift_down(w, s)
                s *= 2
            pooled = w / _window_count(i * ts - HALO, n, k) - p
            yb = jnp.dot(pooled[HALO:].astype(BF16), pw_ref[g], preferred_element_type=F32)
            yb = yb * ps_ref[:, g * POOL_GROUP:(g + 1) * POOL_GROUP]
            o_ref[:, A_WIDTH + g * POOL_GROUP:A_WIDTH + (g + 1) * POOL_GROUP] = yb.astype(BF16)

    hb = ts // HALO
    return pl.pallas_call(
        body, name=name, grid=(S // ts,),
        in_specs=[
            pl.BlockSpec((ts, EVEN_IN), lambda i: (i, 0)),
            pl.BlockSpec((HALO, EVEN_IN), lambda i: (jnp.maximum(i * hb - 1, 0), 0)),
            pl.BlockSpec((3, A_WIDTH), lambda i: (0, 0)),
            pl.BlockSpec((4, POOL_GROUP, POOL_GROUP), lambda i: (0, 0, 0)),
            pl.BlockSpec((1, 4 * POOL_GROUP), lambda i: (0, 0)),
        ],
        out_specs=pl.BlockSpec((ts, D_MODEL), lambda i: (i, 0)),
        out_shape=_sds((S, D_MODEL), BF16), compiler_params=_params("parallel"),
    )(proj, proj, conv_w, pool_w, pool_scale)


def mixer_bwd(proj, dmix, conv_w, pool_w, pool_scale, name, ts=256):
    S = proj.shape[0]
    n = ts + 2 * HALO
    nt = S // ts
    tn_dims = (((0,), (0,)), ((), ()))
    nt_dims = (((1,), (1,)), ((), ()))

    def body(pm_ref, pb_ref, pa_ref, dm_ref, da_ref, cw_ref, pw_ref, ps_ref, o_ref, dcw_ref, dpw_ref, dps_ref):
        i = pl.program_id(0)
        last = i == nt - 1
        before = jnp.where(i > 0, pb_ref[...], 0.0)
        after = jnp.where(last, 0.0, pa_ref[...])
        ext = jnp.concatenate([before, pm_ref[...], after], axis=0)
        dafter = jnp.where(last, 0.0, da_ref[...])
        dext = jnp.concatenate([jnp.zeros((HALO, D_MODEL), F32), dm_ref[...], dafter], axis=0)
        cw = cw_ref[...]
        main = slice(HALO, HALO + ts)

        @pl.when(i == 0)
        def _():
            dcw_ref[...] = jnp.zeros_like(dcw_ref)
            dpw_ref[...] = jnp.zeros_like(dpw_ref)
            dps_ref[...] = jnp.zeros_like(dps_ref)

        h, gb, gc = ext[:, 0:A_WIDTH], ext[:, A_WIDTH:2 * A_WIDTH], ext[:, 2 * A_WIDTH:3 * A_WIDTH]
        z = gc * h
        z1, z2 = _shift_down(z, 1), _shift_down(z, 2)
        cz = (z2 * cw[0:1] + z1 * cw[1:2]) + z * cw[2:3]
        dya = dext[:, 0:A_WIDTH]
        dcz = dya * gb
        dz = dcz * cw[2:3] + _shift_up(dcz, 1) * cw[1:2] + _shift_up(dcz, 2) * cw[0:1]
        o_ref[:, 0:A_WIDTH] = (dz * gc)[main].astype(BF16)
        o_ref[:, A_WIDTH:2 * A_WIDTH] = (dya * cz)[main].astype(BF16)
        o_ref[:, 2 * A_WIDTH:3 * A_WIDTH] = (dz * h)[main].astype(BF16)
        dczm = dcz[main]
        dcw_ref[0:1, :] += jnp.sum(dczm * z2[main], axis=0, keepdims=True)
        dcw_ref[1:2, :] += jnp.sum(dczm * z1[main], axis=0, keepdims=True)
        dcw_ref[2:3, :] += jnp.sum(dczm * z[main], axis=0, keepdims=True)

        for g, k in enumerate(POOL_WINDOWS):
            lo = 3 * A_WIDTH + g * POOL_GROUP
            cols = slice(g * POOL_GROUP, (g + 1) * POOL_GROUP)
            p = ext[:, lo:lo + POOL_GROUP]
            w = p
            s = 1
            while s < k:
                w = w + _shift_down(w, s)
                s *= 2
            cnt = _window_count(i * ts - HALO, n, k)
            pooled = (w / cnt - p)[main].astype(BF16)
            dyb = dext[:, A_WIDTH + g * POOL_GROUP:A_WIDTH + (g + 1) * POOL_GROUP]
            e = dyb * ps_ref[:, cols]
            pre = jnp.dot(pooled, pw_ref[g], preferred_element_type=F32)
            dps_ref[:, cols] += jnp.sum(dyb[main] * pre, axis=0, keepdims=True)
            dpw_ref[g] += lax.dot_general(pooled, e[main].astype(BF16), tn_dims, preferred_element_type=F32)
            dpooled = lax.dot_general(e.astype(BF16), pw_ref[g], nt_dims, preferred_element_type=F32)
            q = dpooled / cnt
            a = q
            s = 1
            while s < k:
                a = a + _shift_up(a, s)
                s *= 2
            o_ref[:, lo:lo + POOL_GROUP] = (a - dpooled)[main].astype(BF16)

    hb = ts // HALO
    nh = S // HALO
    before_map = lambda i: (jnp.maximum(i * hb - 1, 0), 0)
    after_map = lambda i: (jnp.minimum((i + 1) * hb, nh - 1), 0)
    full = lambda *shape: pl.BlockSpec(shape, lambda i: (0,) * len(shape))
    return pl.pallas_call(
        body, name=name, grid=(nt,),
        in_specs=[
            pl.BlockSpec((ts, EVEN_IN), lambda i: (i, 0)),
            pl.BlockSpec((HALO, EVEN_IN), before_map),
            pl.BlockSpec((HALO, EVEN_IN), after_map),
            pl.BlockSpec((ts, D_MODEL), lambda i: (i, 0)),
            pl.BlockSpec((HALO, D_MODEL), after_map),
            full(3, A_WIDTH), full(4, POOL_GROUP, POOL_GROUP), full(1, 4 * POOL_GROUP),
        ],
        out_specs=[pl.BlockSpec((ts, EVEN_IN), lambda i: (i, 0)), full(3, A_WIDTH), full(4, POOL_GROUP, POOL_GROUP),
                   full(1, 4 * POOL_GROUP)],
        out_shape=[_sds((S, EVEN_IN), BF16), _sds((3, A_WIDTH), F32), _sds((4, POOL_GROUP, POOL_GROUP), F32),
                   _sds((1, 4 * POOL_GROUP), F32)],
        compiler_params=_params("arbitrary"),
    )(proj, proj, proj, dmix, dmix, conv_w, pool_w, pool_scale)


FFN_HALO = 16
FFN_TC = 1408


def glu_fwd(up, conv_w, conv_b, name, ts=256):
    S = up.shape[0]
    nc = D_FF // FFN_TC

    def body(gm_ref, gb_ref, um_ref, ub_ref, cwg_ref, cwu_ref, cbg_ref, cbu_ref, ug_ref, uu_ref, o_ref):
        i = pl.program_id(0)

        def conv(m_ref, b_ref, cw_ref, cb_ref):
            before = jnp.where(i > 0, b_ref[...].astype(F32), 0.0)
            ext = jnp.concatenate([before, m_ref[...].astype(F32)], axis=0)
            return _conv3(ext, cw_ref[...])[FFN_HALO:] + cb_ref[...]

        gate = conv(gm_ref, gb_ref, cwg_ref, cbg_ref)
        upv = conv(um_ref, ub_ref, cwu_ref, cbu_ref)
        ug_ref[...] = gate.astype(BF16)
        uu_ref[...] = upv.astype(BF16)
        o_ref[...] = ((gate * (1.0 / (1.0 + jnp.exp(-gate)))) * upv).astype(BF16)

    hb = ts // FFN_HALO
    main = lambda off: pl.BlockSpec((ts, FFN_TC), lambda i, c: (i, c + off))
    halo = lambda off: pl.BlockSpec((FFN_HALO, FFN_TC), lambda i, c: (jnp.maximum(i * hb - 1, 0), c + off))
    cw = lambda off: pl.BlockSpec((3, FFN_TC), lambda i, c: (0, c + off))
    cb = lambda off: pl.BlockSpec((1, FFN_TC), lambda i, c: (0, c + off))
    ug, uu, act = pl.pallas_call(
        body, name=name, grid=(S // ts, nc),
        in_specs=[main(0), halo(0), main(nc), halo(nc), cw(0), cw(nc), cb(0), cb(nc)],
        out_specs=[pl.BlockSpec((ts, FFN_TC), lambda i, c: (i, c))] * 3,
        out_shape=[_sds((S, D_FF), BF16)] * 3, compiler_params=_params("parallel", "parallel"),
    )(up, up, up, up, conv_w, conv_w, conv_b, conv_b)
    return (ug, uu), act


def glu_bwd(up, u, da, conv_w, name, ts=256):
    S = up.shape[0]
    nc = D_FF // FFN_TC
    nt = S // ts
    W = 2 * D_FF

    def body(x_ref, gm_ref, ga_ref, um_ref, ua_ref, dm_ref, da_ref, cw_ref, dx_ref, dcw_ref, dcb_ref):
        i = pl.program_id(0)
        last = i == nt - 1

        @pl.when(i == 0)
        def _():
            dcw_ref[...] = jnp.zeros_like(dcw_ref)
            dcb_ref[...] = jnp.zeros_like(dcb_ref)

        def rows(m_ref, a_ref, cols):
            return jnp.concatenate([m_ref[:, cols], a_ref[:, cols]], axis=0).astype(F32)

        def back(d, cols):
            cw = cw_ref[:, cols]
            d1, d2 = _shift_up(d, 1), _shift_up(d, 2)
            dx_ref[:, cols] = ((d * cw[2:3] + d1 * cw[1:2]) + d2 * cw[0:1])[:ts].astype(BF16)
            x = x_ref[:, cols].astype(F32)
            dcb_ref[:, cols] += jnp.sum(d[:ts], axis=0, keepdims=True)
            dcw_ref[0:1, cols] += jnp.sum(d2[:ts] * x, axis=0, keepdims=True)
            dcw_ref[1:2, cols] += jnp.sum(d1[:ts] * x, axis=0, keepdims=True)
            dcw_ref[2:3, cols] += jnp.sum(d[:ts] * x, axis=0, keepdims=True)

        for c in range(nc):
            cols = slice(c * FFN_TC, (c + 1) * FFN_TC)
            ug, uu = rows(gm_ref, ga_ref, cols), rows(um_ref, ua_ref, cols)
            dae = rows(dm_ref, da_ref, cols)
            dae = jnp.where(last & (lax.broadcasted_iota(jnp.int32, dae.shape, 0) >= ts), 0.0, dae)
            sg = 1.0 / (1.0 + jnp.exp(-ug))
            duu = dae * (ug * sg)
            dug = (dae * uu) * (sg * (1.0 + ug * (1.0 - sg)))
            back(dug, cols)
            back(duu, slice(D_FF + c * FFN_TC, D_FF + (c + 1) * FFN_TC))

    hb = ts // FFN_HALO
    nh = S // FFN_HALO
    after_map = lambda i: (jnp.minimum((i + 1) * hb, nh - 1), 0)
    main = pl.BlockSpec((ts, D_FF), lambda i: (i, 0))
    after = pl.BlockSpec((FFN_HALO, D_FF), after_map)
    return pl.pallas_call(
        body, name=name, grid=(nt,),
        in_specs=[pl.BlockSpec((ts, W), lambda i: (i, 0)), main, after, main, after, main, after,
                  pl.BlockSpec((3, W), lambda i: (0, 0))],
        out_specs=[pl.BlockSpec((ts, W), lambda i: (i, 0)), pl.BlockSpec((3, W), lambda i: (0, 0)),
                   pl.BlockSpec((1, W), lambda i: (0, 0))],
        out_shape=[_sds((S, W), BF16), _sds((3, W), F32), _sds((1, W), F32)],
        compiler_params=_params("arbitrary"),
    )(up, u[0], u[0], u[1], u[1], da, da, conv_w)


MEAN_GROUP = 256


def _head_mean_matrix():
    h = np.arange(MEAN_GROUP) // HEAD_DIM
    return jnp.asarray((h[:, None] == h[None, :]).astype(np.float32) / HEAD_DIM, dtype=BF16)


def _head_mean(v, gm):
    vb = v.astype(BF16)
    return jnp.concatenate([jnp.dot(vb[:, c:c + MEAN_GROUP], gm, preferred_element_type=F32)
                            for c in range(0, v.shape[1], MEAN_GROUP)], axis=1)


def qknorm_fwd(qkv, gqk, name, ts=512):
    S = qkv.shape[0]

    def body(x_ref, g_ref, gm_ref, o_ref):
        part = pl.program_id(0)
        x = x_ref[...]

        @pl.when(part < 2)
        def _():
            r = lax.rsqrt(_head_mean(x * x, gm_ref[...]) + EPS)
            o_ref[...] = ((x * r) * g_ref[...]).astype(BF16)

        @pl.when(part == 2)
        def _():
            o_ref[...] = x.astype(BF16)

    return pl.pallas_call(
        body, name=name, grid=(3, S // ts),
        in_specs=[pl.BlockSpec((ts, D_MODEL), lambda p, i: (i, p)), pl.BlockSpec((None, 1, D_MODEL), lambda p, i: (p, 0, 0)),
                  pl.BlockSpec((MEAN_GROUP, MEAN_GROUP), lambda p, i: (0, 0))],
        out_specs=pl.BlockSpec((ts, D_MODEL), lambda p, i: (i, p)),
        out_shape=_sds((S, 3 * D_MODEL), BF16), compiler_params=_params("parallel", "parallel"),
    )(qkv, gqk, _head_mean_matrix())


def qknorm_bwd(qkv, dq, dk, dv, gqk, name, ts=256):
    S = qkv.shape[0]

    def body(x_ref, dq_ref, dk_ref, dv_ref, g_ref, gm_ref, o_ref, dg_ref):
        @pl.when(pl.program_id(0) == 0)
        def _():
            dg_ref[...] = jnp.zeros_like(dg_ref)

        gm = gm_ref[...]
        for part, d_ref in enumerate((dq_ref, dk_ref)):
            cols = slice(part * D_MODEL, (part + 1) * D_MODEL)
            x = x_ref[:, cols]
            d = d_ref[...]
            r = lax.rsqrt(_head_mean(x * x, gm) + EPS)
            gx = d * g_ref[part]
            o_ref[:, cols] = (r * gx - x * ((r * r * r) * _head_mean(gx * x, gm))).astype(BF16)
            dg_ref[part] += jnp.sum(d * (x * r), axis=0, keepdims=True)
        o_ref[:, 2 * D_MODEL:] = dv_ref[...].astype(BF16)

    row = pl.BlockSpec((ts, D_MODEL), lambda i: (i, 0))
    wide = pl.BlockSpec((ts, 3 * D_MODEL), lambda i: (i, 0))
    gains = pl.BlockSpec((3, 1, D_MODEL), lambda i: (0, 0, 0))
    return pl.pallas_call(
        body, name=name, grid=(S // ts,),
        in_specs=[wide, row, row, row, gains, pl.BlockSpec((MEAN_GROUP, MEAN_GROUP), lambda i: (0, 0))],
        out_specs=[wide, gains],
        out_shape=[_sds((S, 3 * D_MODEL), BF16), _sds((3, 1, D_MODEL), F32)],
        compiler_params=_params("arbitrary"),
    )(qkv, dq, dk, dv, gqk, _head_mean_matrix())


RESIDUES = 16


def _block_order(dil):
    runs = RESIDUES // dil
    slot = np.arange(ATT_BLOCK)
    return (slot % (ATT_BLOCK // runs)) * runs + slot // (ATT_BLOCK // runs)


def _bucket_tables():
    n = ATT_BLOCK
    max_exact = N_REL_BUCKETS // 2
    buckets, valids = [], []
    for _, dil in DILATED_PAIRS:
        order = _block_order(dil)
        a = order[:, None]
        c = np.concatenate([order, n + order])[None, :]
        first_half = (np.arange(2 * n) < n)[None, :]
        rel = a + n - c
        band = (rel >= 0) & (rel <= n)
        dist = np.clip(rel, 0, n) * dil
        dd = np.maximum(dist, 1).astype(np.float32)
        large = max_exact + (np.log(dd / np.float32(max_exact)) / np.float32(math.log(REL_MAX_DISTANCE / max_exact))
                             * np.float32(N_REL_BUCKETS - max_exact)).astype(np.int32)
        large = np.minimum(large, N_REL_BUCKETS - 1)
        buckets.append(np.where(dist < max_exact, dist, large).reshape(1, -1))
        valids.append(np.stack([(band & ~first_half).reshape(1, -1), band.reshape(1, -1)]))
    return np.stack(buckets).astype(np.int32), np.stack(valids).astype(np.int32)


BIAS_CHUNK = 8192


def _split3(x):
    a = x.astype(BF16)
    r = x - a.astype(F32)
    b = r.astype(BF16)
    c = (r - b.astype(F32)).astype(BF16)
    return a, b, c


def bias_expand(rel_bias_t, name):
    bucket, valid = _bucket_tables()
    nq = bucket.shape[-1]

    def body(t_ref, b_ref, v_ref, o_ref):
        onehot = (lax.broadcasted_iota(jnp.int32, (N_REL_BUCKETS, BIAS_CHUNK), 0) == b_ref[...]).astype(BF16)
        acc = None
        for term in _split3(t_ref[...]):
            p = jnp.dot(term, onehot, preferred_element_type=F32)
            acc = p if acc is None else acc + p
        o_ref[...] = jnp.where(v_ref[...] > 0, acc, MASK_VALUE)

    return pl.pallas_call(
        body, name=name, grid=(3, 2, nq // BIAS_CHUNK),
        in_specs=[pl.BlockSpec((N_HEADS, N_REL_BUCKETS), lambda b, v, c: (0, 0)),
                  pl.BlockSpec((None, 1, BIAS_CHUNK), lambda b, v, c: (b, 0, c)),
                  pl.BlockSpec((None, None, 1, BIAS_CHUNK), lambda b, v, c: (b, v, 0, c))],
        out_specs=pl.BlockSpec((None, None, N_HEADS, BIAS_CHUNK), lambda b, v, c: (b, v, 0, c)),
        out_shape=_sds((3, 2, N_HEADS, nq), F32), compiler_params=_params("parallel", "parallel", "parallel"),
    )(rel_bias_t, jnp.asarray(bucket), jnp.asarray(valid))


def bias_reduce(dbias, name):
    bucket, _ = _bucket_tables()
    nq = bucket.shape[-1]
    dims = (((1,), (1,)), ((), ()))

    def body(d_ref, b_ref, o_ref):
        onehot = (lax.broadcasted_iota(jnp.int32, (N_REL_BUCKETS, BIAS_CHUNK), 0) == b_ref[...]).astype(BF16)
        acc = None
        for term in _split3(d_ref[...]):
            p = lax.dot_general(term, onehot, dims, preferred_element_type=F32)
            acc = p if acc is None else acc + p

        @pl.when(pl.program_id(1) == 0)
        def _():
            o_ref[...] = acc

        @pl.when(pl.program_id(1) > 0)
        def _():
            o_ref[...] += acc

    return pl.pallas_call(
        body, name=name, grid=(3, nq // BIAS_CHUNK),
        in_specs=[pl.BlockSpec((None, N_HEADS, BIAS_CHUNK), lambda b, c: (b, 0, c)),
                  pl.BlockSpec((None, 1, BIAS_CHUNK), lambda b, c: (b, 0, c))],
        out_specs=pl.BlockSpec((None, N_HEADS, N_REL_BUCKETS), lambda b, c: (b, 0, 0)),
        out_shape=_sds((3, N_HEADS, N_REL_BUCKETS), F32), compiler_params=_params("parallel", "arbitrary"),
    )(dbias, jnp.asarray(bucket))


PAIR = 2 * HEAD_DIM
N_PAIRS = N_HEADS // 2
_NT = (((1,), (1,)), ((), ()))
_TN = (((0,), (0,)), ((), ()))


def _low_lanes(shape):
    return lax.broadcasted_iota(jnp.int32, shape, 1) < HEAD_DIM


ATTN_VMEM_LIMIT_BYTES = 56 * 1024 * 1024
BRANCH_ORDER = (2, 1, 0)


def _regroup(dst, src, L16):
    for r in range(RESIDUES):
        dst[pl.ds(r * L16, L16), :] = src[pl.ds(r, L16, stride=RESIDUES), :]


def _ungroup(dst, src, L16):
    for r in range(RESIDUES):
        dst[pl.ds(r, L16, stride=RESIDUES), :] = src[pl.ds(r * L16, L16), :]


def _branch_geometry(branch, S):
    dil = DILATED_PAIRS[branch][1]
    runs = RESIDUES // dil
    return dil, runs, ATT_BLOCK // runs, S // dil // ATT_BLOCK


def _block_rows(it, branch, S):
    dil, runs, run_len, n_blocks = _branch_geometry(branch, S)
    L16 = S // RESIDUES
    r, b = it // n_blocks, it % n_blocks
    prev = jnp.maximum(b - 1, 0)
    cur_rows = [pl.multiple_of((j * dil + r) * L16 + run_len * b, 8) for j in range(runs)]
    prev_rows = [pl.multiple_of((j * dil + r) * L16 + run_len * prev, 8) for j in range(runs)]
    return cur_rows, prev_rows, jnp.minimum(b, 1)


def _load_block(ref, rows, run_len):
    parts = [ref[pl.ds(o, run_len), :] for o in rows]
    return parts[0] if len(parts) == 1 else jnp.concatenate(parts, axis=0)


def _store_block(ref, rows, run_len, value, add=False):
    for j, o in enumerate(rows):
        part = value[j * run_len:(j + 1) * run_len]
        if add:
            ref[pl.ds(o, run_len), :] += part
        else:
            ref[pl.ds(o, run_len), :] = part


ATTN_FWD_UNROLL = 8
ATTN_BWD_UNROLL = 4


def _stack_heads(x, low):
    zero = jnp.zeros_like(x)
    return jnp.concatenate([jnp.where(low, x, zero), jnp.where(low, zero, x)], axis=0)


def _unstack_heads(y, low):
    return jnp.where(low, y[:ATT_BLOCK], y[ATT_BLOCK:])


def attn_fwd(qkvn, bias, name):
    S = qkvn.shape[0]
    L16 = S // RESIDUES
    n_iter = S // ATT_BLOCK

    def body(q_ref, k_ref, v_ref, b_ref, o_ref, lse_ref, stage, qp, kp, vp, acc_s, m_s, l_s):
        for src, dst in ((q_ref, qp), (k_ref, kp), (v_ref, vp)):
            stage[...] = src[...].astype(F32)
            _regroup(dst, stage, L16)
        low = _low_lanes((ATT_BLOCK, PAIR))

        for branch in BRANCH_ORDER:
            _, _, run_len, _ = _branch_geometry(branch, S)
            first = branch == BRANCH_ORDER[0]

            def step(it, carry, branch=branch, run_len=run_len, first=first):
                cur, prev, variant = _block_rows(it, branch, S)
                q = _load_block(qp, cur, run_len).astype(BF16)
                k = jnp.concatenate([_load_block(kp, prev, run_len), _load_block(kp, cur, run_len)], axis=0).astype(BF16)
                v = jnp.concatenate([_load_block(vp, prev, run_len), _load_block(vp, cur, run_len)], axis=0).astype(BF16)
                s = lax.dot_general(_stack_heads(q, low), k, _NT, preferred_element_type=F32) * (HEAD_DIM ** -0.5)
                s = s + b_ref[2 * branch + variant].reshape(2 * ATT_BLOCK, 2 * ATT_BLOCK)
                mx = jnp.max(s, axis=-1, keepdims=True)
                p = jnp.exp(s - mx)
                den = jnp.sum(p, axis=-1, keepdims=True)
                pv = jnp.dot(p.astype(BF16), v, preferred_element_type=F32)
                acc = _unstack_heads(pv, low)
                m = _unstack_heads(mx, low)
                l = _unstack_heads(den, low)
                if not first:
                    m_old = _load_block(m_s, cur, run_len)
                    m_new = jnp.maximum(m_old, m)
                    a_old, a_new = jnp.exp(m_old - m_new), jnp.exp(m - m_new)
                    acc = _load_block(acc_s, cur, run_len) * a_old + acc * a_new
                    l = _load_block(l_s, cur, run_len) * a_old + l * a_new
                    m = m_new
                _store_block(acc_s, cur, run_len, acc)
                _store_block(m_s, cur, run_len, m)
                _store_block(l_s, cur, run_len, l)
                return carry

            lax.fori_loop(0, n_iter, step, 0, unroll=ATTN_FWD_UNROLL)

        acc_s[...] = acc_s[...] / l_s[...]
        _ungroup(stage, acc_s, L16)
        o_ref[...] = stage[...].astype(BF16)
        m_s[...] = m_s[...] + jnp.log(l_s[...])
        _ungroup(lse_ref, m_s, L16)

    col = lambda part: pl.BlockSpec((S, PAIR), lambda hp: (0, part * N_PAIRS + hp))
    out = pl.BlockSpec((S, PAIR), lambda hp: (0, hp))
    return pl.pallas_call(
        body, name=name, grid=(N_PAIRS,),
        in_specs=[col(0), col(1), col(2), pl.BlockSpec((6, 2, ATT_BLOCK, 2 * ATT_BLOCK), lambda hp: (0, hp, 0, 0))],
        out_specs=[out, out], out_shape=[_sds((S, D_MODEL), BF16), _sds((S, D_MODEL), F32)],
        scratch_shapes=[pltpu.VMEM((S, PAIR), F32)] * 7,
        compiler_params=pltpu.CompilerParams(dimension_semantics=("parallel",), vmem_limit_bytes=ATTN_VMEM_LIMIT_BYTES),
    )(qkvn, qkvn, qkvn, bias)


def attn_bwd(qkvn, att, datt, lse, bias, name):
    S = qkvn.shape[0]
    L16 = S // RESIDUES
    n_iter = S // ATT_BLOCK
    TILE = 512

    def body(q_ref, k_ref, v_ref, o_ref, do_ref, lse_ref, b_ref, dq_ref, dk_ref, dv_ref, db_ref,
             qp, kp, vp, dop, ldp, dqp, dkp, dvp):
        stage = dqp
        for src, dst in ((q_ref, qp), (k_ref, kp), (v_ref, vp), (do_ref, dop)):
            stage[...] = src[...].astype(F32)
            _regroup(dst, stage, L16)

        def pack(i, carry):
            rows = pl.ds(pl.multiple_of(i * TILE, TILE), TILE)
            low = _low_lanes((TILE, PAIR))
            lane = lax.broadcasted_iota(jnp.int32, (TILE, PAIR), 1)
            prod = do_ref[rows, :].astype(F32) * o_ref[rows, :].astype(F32)
            d0 = jnp.sum(jnp.where(low, prod, 0.0), axis=-1, keepdims=True)
            d1 = jnp.sum(jnp.where(low, 0.0, prod), axis=-1, keepdims=True)
            stage[rows, :] = jnp.where((lane & (HEAD_DIM // 2)) == 0, lse_ref[rows, :], jnp.where(low, d0, d1))
            return carry

        lax.fori_loop(0, S // TILE, pack, 0)
        _regroup(ldp, stage, L16)
        dqp[...] = jnp.zeros_like(dqp)
        dkp[...] = jnp.zeros_like(dkp)
        dvp[...] = jnp.zeros_like(dvp)
        db_ref[...] = jnp.zeros_like(db_ref)
        low = _low_lanes((ATT_BLOCK, PAIR))

        for branch in BRANCH_ORDER:
            _, _, run_len, _ = _branch_geometry(branch, S)

            def step(it, carry, branch=branch, run_len=run_len):
                cur, prev, variant = _block_rows(it, branch, S)
                q = _load_block(qp, cur, run_len).astype(BF16)
                dout = _load_block(dop, cur, run_len).astype(BF16)
                ld = _load_block(ldp, cur, run_len)
                k = jnp.concatenate([_load_block(kp, prev, run_len), _load_block(kp, cur, run_len)], axis=0).astype(BF16)
                v = jnp.concatenate([_load_block(vp, prev, run_len), _load_block(vp, cur, run_len)], axis=0).astype(BF16)
                half = HEAD_DIM // 2
                lse2 = jnp.concatenate([ld[:, 0:1], ld[:, HEAD_DIM:HEAD_DIM + 1]], axis=0)
                delta2 = jnp.concatenate([ld[:, half:half + 1], ld[:, HEAD_DIM + half:HEAD_DIM + half + 1]], axis=0)
                q2, do2 = _stack_heads(q, low), _stack_heads(dout, low)
                s = lax.dot_general(q2, k, _NT, preferred_element_type=F32) * (HEAD_DIM ** -0.5)
                p = jnp.exp(s + b_ref[2 * branch + variant].reshape(2 * ATT_BLOCK, 2 * ATT_BLOCK) - lse2)
                dp = lax.dot_general(do2, v, _NT, preferred_element_type=F32)
                ds = p * (dp - delta2)
                db_ref[branch] += ds.reshape(2, ATT_BLOCK, 2 * ATT_BLOCK)
                dsb = (ds * (HEAD_DIM ** -0.5)).astype(BF16)
                dq = _unstack_heads(jnp.dot(dsb, k, preferred_element_type=F32), low)
                dk = lax.dot_general(dsb, q2, _TN, preferred_element_type=F32)
                dv = lax.dot_general(p.astype(BF16), do2, _TN, preferred_element_type=F32)
                _store_block(dqp, cur, run_len, dq, add=True)
                _store_block(dkp, prev, run_len, dk[:ATT_BLOCK], add=True)
                _store_block(dvp, prev, run_len, dv[:ATT_BLOCK], add=True)
                _store_block(dkp, cur, run_len, dk[ATT_BLOCK:], add=True)
                _store_block(dvp, cur, run_len, dv[ATT_BLOCK:], add=True)
                return carry

            lax.fori_loop(0, n_iter, step, 0, unroll=ATTN_BWD_UNROLL)

        _ungroup(dq_ref, dqp, L16)
        _ungroup(dk_ref, dkp, L16)
        _ungroup(dv_ref, dvp, L16)

    col = lambda part: pl.BlockSpec((S, PAIR), lambda hp: (0, part * N_PAIRS + hp))
    one = pl.BlockSpec((S, PAIR), lambda hp: (0, hp))
    return pl.pallas_call(
        body, name=name, grid=(N_PAIRS,),
        in_specs=[col(0), col(1), col(2), one, one, one,
                  pl.BlockSpec((6, 2, ATT_BLOCK, 2 * ATT_BLOCK), lambda hp: (0, hp, 0, 0))],
        out_specs=[one, one, one, pl.BlockSpec((3, 2, ATT_BLOCK, 2 * ATT_BLOCK), lambda hp: (0, hp, 0, 0))],
        out_shape=[_sds((S, D_MODEL), F32)] * 3 + [_sds((3, N_HEADS, ATT_BLOCK, 2 * ATT_BLOCK), F32)],
        scratch_shapes=[pltpu.VMEM((S, PAIR), F32)] * 8,
        compiler_params=pltpu.CompilerParams(dimension_semantics=("parallel",), vmem_limit_bytes=ATTN_VMEM_LIMIT_BYTES),
    )(qkvn, qkvn, qkvn, att, datt, lse, bias)


def adamw(w, g, m, v, name):
    n, R, C = w.shape

    def body(w_ref, g_ref, m_ref, v_ref, d_ref, nm_ref, nv_ref, go_ref):
        gv = g_ref[...]
        go_ref[...] = gv
        m2 = ADAM_B1 * m_ref[...] + (1.0 - ADAM_B1) * gv
        v2 = ADAM_B2 * v_ref[...] + (1.0 - ADAM_B2) * (gv * gv)
        m_hat = m2 / (1.0 - ADAM_B1 ** ADAM_STEP)
        v_hat = v2 / (1.0 - ADAM_B2 ** ADAM_STEP)
        d_ref[...] = -ADAM_LR * (m_hat / (jnp.sqrt(v_hat) + ADAM_EPS) + ADAM_WD * w_ref[...])
        nm_ref[...] = m2
        nv_ref[...] = v2

    tr = R
    while tr * C * 4 > (1 << 21) and tr % 16 == 0:
        tr //= 2
    spec = pl.BlockSpec((None, tr, C), lambda i, r: (i, r, 0))
    return pl.pallas_call(
        body, name=name, grid=(n, R // tr), in_specs=[spec] * 4, out_specs=[spec] * 4,
        out_shape=[_sds((n, R, C), F32)] * 4, compiler_params=_params("parallel", "parallel"),
    )(w, g, m, v)


ANY = pl.BlockSpec(memory_space=pl.ANY)


def _coords():
    return lax.axis_index("x"), lax.axis_index("y"), lax.axis_index("c")


def _other_chips(mx, my):
    return [(1 - mx, my), (mx, 1 - my), (1 - mx, 1 - my)]


def _remote(src, dst, send, recv, dev):
    return pltpu.make_async_remote_copy(src_ref=src, dst_ref=dst, send_sem=send, recv_sem=recv, device_id=dev,
                                        device_id_type=MESH)


HBM =pl.BlockSpec(memory_space=pltpu.HBM)
SEM = pl.BlockSpec(memory_space=pltpu.SEMAPHORE)
_SPLIT_COPY = pltpu.CompilerParams(has_side_effects=pltpu.SideEffectType.DATAFLOW_SIDE_EFFECTING)


def _in_hbm(a):
    return pltpu.with_memory_space_constraint(a, pltpu.HBM)


def cast_into_slot(w, layer, chip_core, name, dtype=BF16):
    _, _, hR, C = w.shape

    def body(s_ref, w_ref, o_ref):
        del s_ref
        o_ref[...] = w_ref[...].astype(dtype)

    grid_spec = pltpu.PrefetchScalarGridSpec(
        num_scalar_prefetch=1, grid=(2,),
        in_specs=[pl.BlockSpec((None, None, hR, C), lambda h, s: (layer, h, 0, 0))],
        out_specs=pl.BlockSpec((None, None, hR, C), lambda h, s: (s[0], h, 0, 0)))
    return pl.pallas_call(body, name=name, grid_spec=grid_spec, out_shape=_sds((N_CHIPS, 2, hR, C), dtype),
                          compiler_params=_params("parallel"))(chip_core, w)


def gather_start(lands, groups, name):
    n = len(lands)
    n_groups = len(groups)

    def body(*refs):
        ins = refs[:n]
        sems = refs[n:n + 2 * n_groups]
        token = refs[-1]
        mx, my, mc = _coords()
        chip = 2 * mx + my
        for g, members in enumerate(groups):
            send, recv = sems[2 * g], sems[2 * g + 1]
            for i, a in enumerate(members):
                mine = ins[a].at[chip, mc]
                for k, (px, py) in enumerate(_other_chips(mx, my)):
                    _remote(mine, mine, send.at[3 * i + k], recv.at[3 * i + k], (px, py, mc)).start()
        token[...] = jnp.zeros_like(token)

    sem_shapes = []
    for members in groups:
        sem_shapes += [pltpu.SemaphoreType.DMA((3 * len(members),))] * 2
    outs = pl.pallas_call(
        body, name=name, in_specs=[HBM] * n,
        out_specs=[SEM] * (2 * n_groups) + [HBM] * n + [pl.BlockSpec(memory_space=pltpu.VMEM)],
        out_shape=sem_shapes + [pltpu.HBM(a.shape, a.dtype) for a in lands] + [_sds((SUBLANES, LANES), F32)],
        input_output_aliases={a: 2 * n_groups + a for a in range(n)}, compiler_params=_SPLIT_COPY,
    )(*[_in_hbm(a) for a in lands])
    sems = [(outs[2 * g], outs[2 * g + 1]) for g in range(n_groups)]
    return sems, list(outs[2 * n_groups:2 * n_groups + n]), outs[-1]


def gather_forward(lands, sems, after, name):
    n = len(lands)

    def body(*refs):
        ins = refs[:n]
        send, recv = refs[n], refs[n + 1]
        fsend, frecv = refs[n + 3], refs[n + 4]
        mx, my, mc = _coords()
        for i in range(n):
            for k, (px, py) in enumerate(_other_chips(mx, my)):
                landed = ins[i].at[2 * px + py, mc]
                cp = _remote(landed, landed, send.at[3 * i + k], recv.at[3 * i + k], (px, py, mc))
                cp.wait_send()
                cp.wait_recv()
                _remote(landed, landed, fsend.at[3 * i + k], frecv.at[3 * i + k], (mx, my, 1 - mc)).start()

    outs = pl.pallas_call(
        body, name=name, in_specs=[HBM] * n + [SEM, SEM, ANY], out_specs=[SEM, SEM] + [HBM] * n,
        out_shape=[pltpu.SemaphoreType.DMA((3 * n,))] * 2 + [pltpu.HBM(a.shape, a.dtype) for a in lands],
        input_output_aliases={a: 2 + a for a in range(n)}, compiler_params=_SPLIT_COPY,
    )(*lands, sems[0], sems[1], after)
    return (outs[0], outs[1]), list(outs[2:])


def gather_wait(lands, sems, after, name):
    n = len(lands)

    def body(*refs):
        ins = refs[:n]
        fsend, frecv = refs[n], refs[n + 1]
        mx, my, mc = _coords()
        for i in range(n):
            for k, (px, py) in enumerate(_other_chips(mx, my)):
                theirs = ins[i].at[2 * px + py, 1 - mc]
                cp = _remote(theirs, theirs, fsend.at[3 * i + k], frecv.at[3 * i + k], (mx, my, 1 - mc))
                cp.wait_send()
                cp.wait_recv()

    outs = pl.pallas_call(
        body, name=name, in_specs=[HBM] * n + [SEM, SEM, ANY], out_specs=[HBM] * n,
        out_shape=[pltpu.HBM(a.shape, a.dtype) for a in lands],
        input_output_aliases={a: a for a in range(n)}, compiler_params=_SPLIT_COPY,
    )(*lands, sems[0], sems[1], after)
    return list(outs)


def _peers(mx, my, mc):
    return [(1 - mx if k & 4 else mx, 1 - my if k & 2 else my, 1 - mc if k & 1 else mc) for k in range(1, N_DEV)]


def devices_start(x, name):
    def body(x_ref, land_ref, send, recv, x_thru, land_thru):
        mx, my, mc = _coords()
        me = 4 * mx + 2 * my + mc
        for k, peer in enumerate(_peers(mx, my, mc)):
            _remote(x_ref, land_ref.at[me], send.at[k], recv.at[k], peer).start()

    land = lax.empty((N_DEV,) + x.shape, x.dtype)
    outs = pl.pallas_call(
        body, name=name, in_specs=[HBM, HBM], out_specs=[SEM, SEM, HBM, HBM],
        out_shape=[pltpu.SemaphoreType.DMA((N_DEV - 1,))] * 2 + [pltpu.HBM(x.shape, x.dtype), pltpu.HBM(land.shape, x.dtype)],
        input_output_aliases={0: 2, 1: 3}, compiler_params=_SPLIT_COPY,
    )(_in_hbm(x), _in_hbm(land))
    return (outs[0], outs[1]), outs[2], outs[3]


def devices_wait(x, land, sems, after, name):
    def body(x_ref, land_ref, send, recv, after_ref, x_thru, land_thru):
        mx, my, mc = _coords()
        for k, (px, py, pc) in enumerate(_peers(mx, my, mc)):
            cp = _remote(x_ref, land_ref.at[4 * px + 2 * py + pc], send.at[k], recv.at[k], (px, py, pc))
            cp.wait_send()
            cp.wait_recv()

    outs = pl.pallas_call(
        body, name=name, in_specs=[HBM, HBM, SEM, SEM, ANY], out_specs=[HBM, HBM],
        out_shape=[pltpu.HBM(x.shape, x.dtype), pltpu.HBM(land.shape, land.dtype)],
        input_output_aliases={0: 0, 1: 1}, compiler_params=_SPLIT_COPY,
    )(x, land, sems[0], sems[1], after)
    return outs[0], outs[1]


def device_sum(land, own, me, name):
    _, R, C = land.shape

    def body(s_ref, l_ref, o_ref_in, o_ref):
        acc = None
        for q in range(N_DEV):
            term = jnp.where(s_ref[0] == q, o_ref_in[...], l_ref[q])
            acc = term if acc is None else acc + term
        o_ref[...] = acc

    grid_spec = pltpu.PrefetchScalarGridSpec(
        num_scalar_prefetch=1, grid=(1,),
        in_specs=[pl.BlockSpec((N_DEV, R, C), lambda i, s: (0, 0, 0)), pl.BlockSpec((R, C), lambda i, s: (0, 0))],
        out_specs=pl.BlockSpec((R, C), lambda i, s: (0, 0)))
    return pl.pallas_call(body, name=name, grid_spec=grid_spec, out_shape=_sds((R, C), F32),
                          compiler_params=_params("arbitrary"))(me, land, own)


def reduce_send(grads, name):
    n = len(grads)

    def body(*refs):
        ins, lands = refs[:n], refs[n:2 * n]
        send, recv = refs[2 * n], refs[2 * n + 1]
        mx, my, mc = _coords()
        me = 4 * mx + 2 * my + mc
        for a in range(n):
            for k, (px, py, pc) in enumerate(_peers(mx, my, mc)):
                _remote(ins[a].at[2 * px + py, pc], lands[a].at[me], send.at[7 * a + k], recv.at[7 * a + k], (px, py, pc)).start()

    lands = [lax.empty((N_DEV,) + g.shape[2:], g.dtype) for g in grads]
    outs = pl.pallas_call(
        body, name=name, in_specs=[HBM] * (2 * n), out_specs=[SEM, SEM] + [HBM] * (2 * n),
        out_shape=[pltpu.SemaphoreType.DMA((7 * n,))] * 2 + [pltpu.HBM(a.shape, a.dtype) for a in grads + lands],
        input_output_aliases={a: 2 + a for a in range(2 * n)}, compiler_params=_SPLIT_COPY,
    )(*[_in_hbm(a) for a in grads + lands])
    return (outs[0], outs[1]), list(outs[2:2 + n]), list(outs[2 + n:])


def reduce_wait(grads, lands, sems, after, name):
    n = len(grads)

    def body(*refs):
        ins, zones = refs[:n], refs[n:2 * n]
        send, recv = refs[2 * n], refs[2 * n + 1]
        mx, my, mc = _coords()
        for a in range(n):
            for k, (px, py, pc) in enumerate(_peers(mx, my, mc)):
                cp = _remote(ins[a].at[2 * px + py, pc], zones[a].at[4 * px + 2 * py + pc], send.at[7 * a + k],
                             recv.at[7 * a + k], (px, py, pc))
                cp.wait_send()
                cp.wait_recv()

    outs = pl.pallas_call(
        body, name=name, in_specs=[HBM] * (2 * n) + [SEM, SEM, ANY], out_specs=[HBM] * (2 * n),
        out_shape=[pltpu.HBM(a.shape, a.dtype) for a in grads + lands],
        input_output_aliases={a: a for a in range(2 * n)}, compiler_params=_SPLIT_COPY,
    )(*grads, *lands, sems[0], sems[1], after)
    return list(outs[:n]), list(outs[n:])


def reduce_sum(land, grad, place, name, into=None, layer=None):
    _, hR, C = land.shape
    tr = hR
    while N_DEV * tr * C * 2 > (6 << 20) and tr % 32 == 0:
        tr //= 2

    def body(s_ref, l_ref, g_ref, *rest):
        o_ref = rest[-1]
        own = g_ref[...].astype(F32)
        acc = None
        for q in range(N_DEV):
            term = jnp.where(s_ref[2] == q, own, l_ref[q].astype(F32))
            acc = term if acc is None else acc + term
        o_ref[...] = acc

    in_specs = [pl.BlockSpec((N_DEV, tr, C), lambda i, s: (0, i, 0)),
                pl.BlockSpec((None, None, tr, C), lambda i, s: (s[0], s[1], i, 0))]
    args = [place, land, grad]
    aliases = {}
    if layer is None:
        out_spec = pl.BlockSpec((None, tr, C), lambda i, s: (s[1], i, 0))
        out_shape = _sds((2, hR, C), F32)
    else:
        out_spec = pl.BlockSpec((None, None, tr, C), lambda i, s: (layer, s[1], i, 0))
        out_shape = _sds((2, 2, hR, C), F32)
        if into is not None:
            in_specs.append(ANY)
            args.append(into)
            aliases = {3: 0}
    grid_spec = pltpu.PrefetchScalarGridSpec(num_scalar_prefetch=1, grid=(hR // tr,), in_specs=in_specs, out_specs=out_spec)
    return pl.pallas_call(body, name=name, grid_spec=grid_spec, out_shape=out_shape, input_output_aliases=aliases,
                          compiler_params=_params("arbitrary"))(*args)


def join_halves(arrays, name):
    n = len(arrays)
    pieces = [(a, l) for a, arr in enumerate(arrays) for l in (range(arr.shape[0]) if arr.ndim == 4 else [None])]

    def body(*refs):
        ins = refs[:n]
        send, recv = refs[2 * n:]
        mx, my, mc = _coords()

        def half(a, l, h):
            return ins[a].at[h] if l is None else ins[a].at[l, h]

        sends = [_remote(half(a, l, mc), half(a, l, mc), send.at[i], recv.at[i], (mx, my, 1 - mc))
                 for i, (a, l) in enumerate(pieces)]
        for cp in sends:
            cp.start()
        for i, (a, l) in enumerate(pieces):
            theirs = half(a, l, 1 - mc)
            _remote(theirs, theirs, send.at[i], recv.at[i], (mx, my, 1 - mc)).wait_recv()
        for cp in sends:
            cp.wait_send()

    return pl.pallas_call(
        body, name=name, in_specs=[ANY] * n, out_specs=[ANY] * n, out_shape=[_sds(a.shape, a.dtype) for a in arrays],
        input_output_aliases={a: a for a in range(n)},
        scratch_shapes=[pltpu.SemaphoreType.DMA((len(pieces),)), pltpu.SemaphoreType.DMA((len(pieces),))],
    )(*arrays)


LANES = 128
SUBLANES = 8


def _n_rows(shape):
    rows = -(-int(np.prod(shape)) // LANES)
    return -(-rows // SUBLANES) * SUBLANES


def _as_rows(a):
    flat = a.reshape(-1)
    rows = _n_rows(a.shape)
    return jnp.pad(flat, (0, rows * LANES - flat.shape[0])).reshape(rows, LANES)


def _pack(arrays):
    return jnp.concatenate([_as_rows(a) for a in arrays], axis=0)


def _unpack(rows, shapes):
    out, r0 = [], 0
    for s in shapes:
        n = _n_rows(s)
        out.append(rows[r0:r0 + n].reshape(-1)[:int(np.prod(s))].reshape(s))
        r0 += n
    return out


REPLICATED_SMALL = [("rel_bias", (32, 16)), ("even_norm", (1, 1024)), ("even_pool_w", (1, 4, 128, 128)),
                    ("even_pool_scale", (1, 512)), ("odd_q_norm", (1, 64)), ("odd_k_norm", (1, 64)),
                    ("ffn_norm", (2, 1024)), ("ffn_conv_b", (2, 5632))]
SHARDED_SMALL = [("even_conv_w", (1, 3, 128)), ("odd_norm", (1, 256)), ("ffn_conv_w", (2, 3, 1408))]
BIG = ["even_w_in", "even_w_out", "odd_w_qkv", "odd_w_o", "ffn_w_up", "ffn_w_down"]
WEIGHT_ORDER = ["rel_bias", "even_norm", "even_w_in", "even_conv_w", "even_pool_w", "even_pool_scale", "even_w_out",
                "odd_norm", "odd_w_qkv", "odd_q_norm", "odd_k_norm", "odd_w_o", "ffn_norm", "ffn_w_up", "ffn_conv_w",
                "ffn_conv_b", "ffn_w_down"]


def kernel(x, rel_bias, even_norm, even_w_in, even_conv_w, even_pool_w, even_pool_scale, even_w_out, odd_norm, odd_w_qkv, odd_q_norm, odd_k_norm, odd_w_o, ffn_norm, ffn_w_up, ffn_conv_w, ffn_conv_b, ffn_w_down, loss_target, m_rel_bias, m_even_norm, m_even_w_in, m_even_conv_w, m_even_pool_w, m_even_pool_scale, m_even_w_out, m_odd_norm, m_odd_w_qkv, m_odd_q_norm, m_odd_k_norm, m_odd_w_o, m_ffn_norm, m_ffn_w_up, m_ffn_conv_w, m_ffn_conv_b, m_ffn_w_down, v_rel_bias, v_even_norm, v_even_w_in, v_even_conv_w, v_even_pool_w, v_even_pool_scale, v_even_w_out, v_odd_norm, v_odd_w_qkv, v_odd_q_norm, v_odd_k_norm, v_odd_w_o, v_ffn_norm, v_ffn_w_up, v_ffn_conv_w, v_ffn_conv_b, v_ffn_w_down):
    W = dict(rel_bias=rel_bias, even_norm=even_norm, even_w_in=even_w_in, even_conv_w=even_conv_w, even_pool_w=even_pool_w,
             even_pool_scale=even_pool_scale, even_w_out=even_w_out, odd_norm=odd_norm, odd_w_qkv=odd_w_qkv,
             odd_q_norm=odd_q_norm, odd_k_norm=odd_k_norm, odd_w_o=odd_w_o, ffn_norm=ffn_norm, ffn_w_up=ffn_w_up,
             ffn_conv_w=ffn_conv_w, ffn_conv_b=ffn_conv_b, ffn_w_down=ffn_w_down)
    M1 = dict(rel_bias=m_rel_bias, even_norm=m_even_norm, even_w_in=m_even_w_in, even_conv_w=m_even_conv_w,
              even_pool_w=m_even_pool_w, even_pool_scale=m_even_pool_scale, even_w_out=m_even_w_out, odd_norm=m_odd_norm,
              odd_w_qkv=m_odd_w_qkv, odd_q_norm=m_odd_q_norm, odd_k_norm=m_odd_k_norm, odd_w_o=m_odd_w_o,
              ffn_norm=m_ffn_norm, ffn_w_up=m_ffn_w_up, ffn_conv_w=m_ffn_conv_w, ffn_conv_b=m_ffn_conv_b,
              ffn_w_down=m_ffn_w_down)
    M2 = dict(rel_bias=v_rel_bias, even_norm=v_even_norm, even_w_in=v_even_w_in, even_conv_w=v_even_conv_w,
              even_pool_w=v_even_pool_w, even_pool_scale=v_even_pool_scale, even_w_out=v_even_w_out, odd_norm=v_odd_norm,
              odd_w_qkv=v_odd_w_qkv, odd_q_norm=v_odd_q_norm, odd_k_norm=v_odd_k_norm, odd_w_o=v_odd_w_o,
              ffn_norm=v_ffn_norm, ffn_w_up=v_ffn_w_up, ffn_conv_w=v_ffn_conv_w, ffn_conv_b=v_ffn_conv_b,
              ffn_w_down=v_ffn_w_down)
    mx, my, mc = _coords()
    chip = 2 * mx + my
    me = 4 * mx + 2 * my + mc
    place = jnp.stack([chip, mc, me]).astype(jnp.int32)
    xs, target = x[0], loss_target[0]

    def halves(w):
        return w.reshape((w.shape[0], 2, w.shape[-2] // 2, w.shape[-1]))

    lands = [cast_into_slot(halves(even_w_in), 0, place, "cast_w_in"), cast_into_slot(halves(even_w_out), 0, place, "cast_w_out"),
             cast_into_slot(halves(ffn_w_up), 0, place, "cast_w_up0"), cast_into_slot(halves(ffn_w_down), 0, place, "cast_w_down0"),
             cast_into_slot(halves(odd_w_qkv), 0, place, "cast_w_qkv"), cast_into_slot(halves(odd_w_o), 0, place, "cast_w_o"),
             cast_into_slot(halves(ffn_w_up), 1, place, "cast_w_up1"), cast_into_slot(halves(ffn_w_down), 1, place, "cast_w_down1")]
    small_rows = jnp.pad(_pack([even_conv_w, odd_norm, ffn_conv_w]), ((0, SUBLANES), (0, 0)))
    lands.append(cast_into_slot(small_rows.reshape(1, 2, small_rows.shape[0] // 2, LANES), 0, place, "small_into_slot", dtype=F32))
    groups = [[0, 1, 8], [2, 3], [4, 5], [6, 7]]
    gather_sems, lands, token = gather_start(lands, groups, "gather_start")
    even_norm_after_start = even_norm + token[0:1, 0:1]

    def gathered(group, tag, after_landing, after_passing):
        mine = [lands[a] for a in groups[group]]
        sems, arrays = gather_forward(mine, gather_sems[group], after_landing, "gather_forward_" + tag)
        return gather_wait(arrays, sems, after_passing, "gather_wait_" + tag)

    pool_w = cast_bf16(even_pool_w[0], "cast_pool_w")
    gqk = jnp.stack([jnp.tile(odd_q_norm[0], N_HEADS), jnp.tile(odd_k_norm[0], N_HEADS),
                     jnp.ones((D_MODEL,), F32)])[:, None, :]
    bias = bias_expand(rel_bias.T, "bias_expand").reshape(6, N_HEADS, ATT_BLOCK, 2 * ATT_BLOCK)
    xn0 = rmsnorm_fwd(xs, even_norm_after_start, "even_norm")
    got = gathered(0, "even", bias, xn0)
    w_in = got[0].reshape(N_CHIPS, 1, D_MODEL, EVEN_IN // N_CHIPS)
    w_out = got[1].reshape(1, 1, D_MODEL, D_MODEL)
    small = got[2].reshape(N_CHIPS, small_rows.shape[0], LANES)
    conv_w_full = small[:, 0:3].transpose(1, 0, 2).reshape(3, A_WIDTH)
    odd_norm_full = small[:, 8:10].reshape(1, D_MODEL)
    ffn_cw_full = small[:, 16:82].reshape(N_CHIPS, 2, 3, 2 * D_FF // N_CHIPS).transpose(1, 2, 0, 3).reshape(2, 3, 2 * D_FF)

    def ffn_fwd(l, xin, xn):
        up = mm_nn(xn, w_up[l], f"ffn{l}_up", out_dtype=BF16)
        u, act = glu_fwd(up, ffn_cw_full[l], ffn_conv_b[l:l + 1], f"ffn{l}_glu")
        return act, (xin, xn, up, u, act)

    def ffn_weights(got):
        return got[0].reshape(N_CHIPS, 1, D_MODEL, 2 * D_FF // N_CHIPS), got[1].reshape(1, 1, D_FF, D_MODEL)

    w_up, w_down = [None, None], [None, None]
    proj = mm_nn(xn0, w_in, "even_in")
    mix = mixer_fwd(proj, conv_w_full, pool_w, even_pool_scale, "even_mixer")
    x1, xn1 = mm_res_norm(mix, w_out, xs, ffn_norm[0:1], "even_out")
    w_up[0], w_down[0] = ffn_weights(gathered(1, "ffn0", proj, x1))
    act0, ffn0 = ffn_fwd(0, x1, xn1)
    x2, xn2 = mm_res_norm(act0, w_down[0], x1, odd_norm_full, "ffn0_down")
    got = gathered(2, "odd", x1, x2)
    w_qkv = got[0].reshape(N_CHIPS, 1, D_MODEL, 3 * D_MODEL // N_CHIPS)
    w_o = got[1].reshape(1, 1, D_MODEL, D_MODEL)
    qkv = mm_nn(xn2, w_qkv, "odd_qkv")
    qkvn = qknorm_fwd(qkv, gqk, "odd_qknorm")
    att, lse = attn_fwd(qkvn, bias, "attn_fwd")
    x3, xn3 = mm_res_norm(att, w_o, x2, ffn_norm[1:2], "odd_out")
    w_up[1], w_down[1] = ffn_weights(gathered(3, "ffn1", x2, x3))
    act1, ffn1 = ffn_fwd(1, x3, xn3)
    dy, dyb, sq = mm_res_loss(act1, w_down[1], x3, target, "ffn1_down_loss")
    loss = lax.psum(0.5 * jnp.sum(sq) * (1.0 / D_MODEL), ("x", "y", "c"))

    def ffn_bwd(l, dy, dyb, saved):
        xin, xn, up, u, act = saved
        dw_down = mm_tn(act, dyb, f"ffn{l}_dw_down", J=1, tk=D_FF // 2, tm=1024)
        dact = mm_nt(dyb, w_down[l], f"ffn{l}_dact", tr=D_FF // 2, out_dtype=BF16, tm=1024)
        dup, dcw, dcb = glu_bwd(up, u, dact, ffn_cw_full[l], f"ffn{l}_glu_bwd")
        dw_up = mm_tn(xn, dup, f"ffn{l}_dw_up", J=N_CHIPS, tk=512, tm=1024, jb=2)
        dx, dxb, dg = mm_nt_norm_bwd(dup, w_up[l], xin, ffn_norm[l:l + 1], dy, f"ffn{l}_dx")
        return dx, dxb, (dw_down, dw_up, dcw, dcb, dg)

    def quarters(g):
        return g.reshape(N_CHIPS, 2, g.shape[0] * g.shape[1] // (2 * N_CHIPS), g.shape[-1])

    def reduce_start(grads, tag, then):
        sems, parts, zones = reduce_send([quarters(g) for g in grads], "reduce_send_" + tag)
        then, parts = lax.optimization_barrier((then, parts))
        return (sems, parts, zones), then

    dx3, dx3b, g_ffn1 = ffn_bwd(1, dy, dyb, ffn1)
    red_ffn1, (dx3, dx3b) = reduce_start([g_ffn1[1], g_ffn1[0]], "ffn1", (dx3, dx3b))
    dw_o = mm_tn(att, dx3b, "odd_dw_o", J=1, tk=512, tm=1024)
    datt = mm_nt(dx3b, w_o, "odd_datt", tr=D_MODEL, out_dtype=BF16)
    dq, dk, dv, dbias = attn_bwd(qkvn, att, datt, lse, bias, "attn_bwd")
    dqkv, dgqk = qknorm_bwd(qkv, dq, dk, dv, gqk, "odd_qknorm_bwd")
    dw_qkv = mm_tn(xn2, dqkv, "odd_dw_qkv", J=N_CHIPS, tk=512, tm=1024)
    red_odd, dqkv = reduce_start([dw_qkv, dw_o], "odd", dqkv)
    dx2, dx2b, dg_odd = mm_nt_norm_bwd(dqkv, w_qkv, x2, odd_norm_full, dx3, "odd_dx")
    dx1, dx1b, g_ffn0 = ffn_bwd(0, dx2, dx2b, ffn0)
    red_ffn0, (dx1, dx1b) = reduce_start([g_ffn0[1], g_ffn0[0]], "ffn0", (dx1, dx1b))
    dw_out = mm_tn(mix, dx1b, "even_dw_out", J=1, tk=512, tm=1024)
    dmix = mm_nt(dx1b, w_out, "even_dmix", tr=D_MODEL)
    dproj, dcw_even, dpw, dps = mixer_bwd(proj, dmix, conv_w_full, pool_w, even_pool_scale, "even_mixer_bwd")
    dw_in = mm_tn(xn0, dproj, "even_dw_in", J=N_CHIPS, tk=512, tm=1024)
    grad_x, _, dg_even = mm_nt_norm_bwd(dproj, w_in, xs, even_norm, dx1, "even_dx")
    d_rel = jnp.sum(bias_reduce(dbias.reshape(3, N_HEADS, 2 * ATT_BLOCK * ATT_BLOCK), "bias_reduce"), axis=0).T

    red_even, grad_x = reduce_start([dw_in, dw_out], "even", grad_x)

    dcw_sh = dcw_even.reshape(3, N_CHIPS, A_WIDTH // N_CHIPS).transpose(1, 0, 2)
    don_sh = dg_odd.reshape(N_CHIPS, D_MODEL // N_CHIPS)
    dfcw = jnp.stack([g_ffn0[2], g_ffn1[2]])
    dfcw_sh = dfcw.reshape(2, 3, N_CHIPS, 2 * D_FF // N_CHIPS).transpose(2, 0, 1, 3)
    rep_grads = [d_rel, dg_even, dpw[None], dps, _head_sum(dgqk[0]), _head_sum(dgqk[1]),
                 jnp.concatenate([g_ffn0[4], g_ffn1[4]], axis=0), jnp.concatenate([g_ffn0[3], g_ffn1[3]], axis=0)]
    rep_rows = _pack(rep_grads)
    shard_rows = jnp.concatenate([_pack([dcw_sh[j], don_sh[j], dfcw_sh[j]]) for j in range(N_CHIPS)], axis=0)
    n_rep, n_shard = rep_rows.shape[0], shard_rows.shape[0] // N_CHIPS
    small_sems, small_rows, small_land = devices_start(jnp.concatenate([rep_rows, shard_rows], axis=0), "small_grads_start")
    grad_x, small_rows = lax.optimization_barrier((grad_x, small_rows))

    def reduce_end(red, tag, after):
        sems, parts, zones = red
        parts, zones = reduce_wait(parts, zones, sems, after, "reduce_wait_" + tag)
        return zones, parts

    z_ffn1, p_ffn1 = reduce_end(red_ffn1, "ffn1", grad_x)
    z_odd, p_odd = reduce_end(red_odd, "odd", grad_x)
    r_qkv = reduce_sum(z_odd[0], p_odd[0], place, "reduce_sum_w_qkv")
    r_o = reduce_sum(z_odd[1], p_odd[1], place, "reduce_sum_w_o")
    r_up = reduce_sum(z_ffn1[0], p_ffn1[0], place, "reduce_sum_w_up1", layer=1)
    r_down = reduce_sum(z_ffn1[1], p_ffn1[1], place, "reduce_sum_w_down1", layer=1)
    r_qkv, r_o, r_up, r_down = lax.optimization_barrier((r_qkv, r_o, r_up, r_down))
    z_ffn0, p_ffn0 = reduce_end(red_ffn0, "ffn0", r_down)
    r_up = reduce_sum(z_ffn0[0], p_ffn0[0], place, "reduce_sum_w_up0", into=r_up, layer=0)
    r_down = reduce_sum(z_ffn0[1], p_ffn0[1], place, "reduce_sum_w_down0", into=r_down, layer=0)
    later = ["odd_w_qkv", "odd_w_o", "ffn_w_up", "ffn_w_down"]
    joined = join_halves([r_qkv, r_o, r_up, r_down], "grads_join_late_layers")
    G = {nm: g.reshape(W[nm].shape) for nm, g in zip(later, joined)}

    D_, NM, NV = {}, {}, {}

    def update(nm):
        as3 = lambda a: a.reshape((-1,) + a.shape[-2:])
        outs = adamw(as3(W[nm]), as3(G[nm]), as3(M1[nm]), as3(M2[nm]), "adamw_" + nm)
        D_[nm], NM[nm], NV[nm], G[nm] = [o.reshape(W[nm].shape) for o in outs]

    def all_before(names):
        tied = lax.optimization_barrier([D_[nm] for nm in names])
        for nm, d in zip(names, tied):
            D_[nm] = d
        return tied[0]

    for nm in later:
        update(nm)
    z_even, p_even = reduce_end(red_even, "even", all_before(later))
    joined = join_halves([reduce_sum(z_even[0], p_even[0], place, "reduce_sum_w_in"),
                          reduce_sum(z_even[1], p_even[1], place, "reduce_sum_w_out")], "grads_join_first_layer")
    first = ["even_w_in", "even_w_out"]
    for nm, g in zip(first, joined):
        G[nm] = g.reshape(W[nm].shape)
        update(nm)
    small_rows, small_land = devices_wait(small_rows, small_land, small_sems, all_before(first), "small_grads_wait")
    small_sum = device_sum(small_land, small_rows, place[2:3], "small_grads_sum")
    mine = lax.dynamic_slice_in_dim(small_sum, n_rep + chip * n_shard, n_shard, axis=0)
    g_small = jnp.concatenate([small_sum[:n_rep], mine], axis=0)
    small_names = [n for n, _ in REPLICATED_SMALL + SHARDED_SMALL]
    small_shapes = [s for _, s in REPLICATED_SMALL + SHARDED_SMALL]
    G.update(dict(zip(small_names, _unpack(g_small, small_shapes))))
    packs = [_pack([d[n] for n in small_names])[None] for d in (W, M1, M2)]
    outs = adamw(packs[0], g_small[None], packs[1], packs[2], "adamw_small")
    for dst, o in zip((D_, NM, NV), outs[:3]):
        dst.update(dict(zip(small_names, _unpack(o[0], small_shapes))))

    return (loss, grad_x[None], *[G[n] for n in WEIGHT_ORDER], *[D_[n] for n in WEIGHT_ORDER],
            *[NM[n] for n in WEIGHT_ORDER], *[NV[n] for n in WEIGHT_ORDER])


def _head_sum(dg):
    return jnp.sum(dg.reshape(N_HEADS, HEAD_DIM), axis=0, keepdims=True)
```

```python
import functools
import math

import numpy as np
import jax
import jax.numpy as jnp
from jax import lax
from jax.experimental import pallas as pl
from jax.experimental.pallas import tpu as pltpu

F32 = jnp.float32
BF16 = jnp.bfloat16

D_MODEL = 1024
N_HEADS = 16
HEAD_DIM = 64
A_WIDTH = 512
POOL_WINDOWS = (2, 4, 8, 16)
POOL_GROUP = 128
EVEN_IN = 2048
D_FF = 2816
DILATED_PAIRS = ((128, 1), (512, 4), (2048, 16))
ATT_BLOCK = 128
N_REL_BUCKETS = 32
REL_MAX_DISTANCE = 2048
EPS = 1e-6
MASK_VALUE = -1e30
ADAM_LR, ADAM_B1, ADAM_B2, ADAM_EPS, ADAM_WD, ADAM_STEP = 0.001, 0.9, 0.999, 1e-08, 0.01, 10

VMEM_LIMIT_BYTES = 48 * 1024 * 1024
N_CHIPS = 4
N_DEV = 8
MESH = pl.DeviceIdType.MESH


def _params(*sem):
    return pltpu.CompilerParams(dimension_semantics=sem if sem else None, vmem_limit_bytes=VMEM_LIMIT_BYTES)


def _sds(shape, dtype):
    return jax.ShapeDtypeStruct(tuple(shape), dtype)


def cast_bf16(x, name, tr=None):
    lead, (R, C) = x.shape[:-2], x.shape[-2:]
    n = int(np.prod(lead)) if lead else 1
    x3 = x.reshape((n, R, C))
    tr = tr or R

    def body(x_ref, o_ref):
        o_ref[...] = x_ref[...].astype(BF16)

    out = pl.pallas_call(
        body, name=name, grid=(n, R // tr),
        in_specs=[pl.BlockSpec((None, tr, C), lambda i, r: (i, r, 0))],
        out_specs=pl.BlockSpec((None, tr, C), lambda i, r: (i, r, 0)),
        out_shape=_sds((n, R, C), BF16), compiler_params=_params("parallel", "parallel"),
    )(x3)
    return out.reshape(lead + (R, C))


def rmsnorm_fwd(x, g, name, ts=512):
    S, Dm = x.shape

    def body(x_ref, g_ref, o_ref):
        xv = x_ref[...]
        r = lax.rsqrt(jnp.mean(xv * xv, axis=-1, keepdims=True) + EPS)
        o_ref[...] = ((xv * r) * g_ref[...]).astype(BF16)

    return pl.pallas_call(
        body, name=name, grid=(S // ts,),
        in_specs=[pl.BlockSpec((ts, Dm), lambda i: (i, 0)), pl.BlockSpec((1, Dm), lambda i: (0, 0))],
        out_specs=pl.BlockSpec((ts, Dm), lambda i: (i, 0)),
        out_shape=_sds((S, Dm), BF16), compiler_params=_params("parallel"),
    )(x, g)


def mm_nn(a, w, name, layer=0, res=None, out_dtype=F32, tm=1024):
    M, K = a.shape
    J, _, _, Ns = w.shape

    def body(*refs):
        a_ref, w_ref = refs[0], refs[1]
        o_ref = refs[-1]
        acc = jnp.dot(a_ref[...], w_ref[...], preferred_element_type=F32)
        if res is not None:
            acc = refs[2][...] + acc
        o_ref[...] = acc.astype(o_ref.dtype)

    in_specs = [pl.BlockSpec((tm, K), lambda j, m: (m, 0)),
                pl.BlockSpec((None, None, K, Ns), lambda j, m: (j, layer, 0, 0))]
    args = [a, w]
    if res is not None:
        in_specs.append(pl.BlockSpec((tm, Ns), lambda j, m: (m, j)))
        args.append(res)
    return pl.pallas_call(
        body, name=name, grid=(J, M // tm), in_specs=in_specs,
        out_specs=pl.BlockSpec((tm, Ns), lambda j, m: (m, j)),
        out_shape=_sds((M, J * Ns), out_dtype), compiler_params=_params("parallel", "parallel"),
    )(*args)


def mm_res_norm(a, w, res, gain, name, tm=1024):
    M, K = a.shape
    Dm = w.shape[-1]

    def body(a_ref, w_ref, r_ref, g_ref, y_ref, yn_ref):
        y = r_ref[...] + jnp.dot(a_ref[...], w_ref[...], preferred_element_type=F32)
        y_ref[...] = y
        r = lax.rsqrt(jnp.mean(y * y, axis=-1, keepdims=True) + EPS)
        yn_ref[...] = ((y * r) * g_ref[...]).astype(BF16)

    row = pl.BlockSpec((tm, Dm), lambda m: (m, 0))
    return pl.pallas_call(
        body, name=name, grid=(M // tm,),
        in_specs=[pl.BlockSpec((tm, K), lambda m: (m, 0)),
                  pl.BlockSpec((None, None, K, Dm), lambda m: (0, 0, 0, 0), pipeline_mode=pl.Buffered(1)),
                  row, pl.BlockSpec((1, Dm), lambda m: (0, 0))],
        out_specs=[row, row], out_shape=[_sds((M, Dm), F32), _sds((M, Dm), BF16)],
        compiler_params=_params("parallel"),
    )(a, w, res, gain)


def mm_res_loss(a, w, res, target, name, tm=512):
    M, K = a.shape
    Dm = w.shape[-1]

    def body(a_ref, w_ref, r_ref, t_ref, d_ref, db_ref, s_ref):
        e = (r_ref[...] + jnp.dot(a_ref[...], w_ref[...], preferred_element_type=F32)) - t_ref[...]
        d = e * (1.0 / Dm)
        d_ref[...] = d
        db_ref[...] = d.astype(BF16)
        part = jnp.sum(e * e, axis=0, keepdims=True)

        @pl.when(pl.program_id(0) == 0)
        def _():
            s_ref[...] = part

        @pl.when(pl.program_id(0) > 0)
        def _():
            s_ref[...] += part

    row = pl.BlockSpec((tm, Dm), lambda m: (m, 0))
    return pl.pallas_call(
        body, name=name, grid=(M // tm,),
        in_specs=[pl.BlockSpec((tm, K), lambda m: (m, 0)),
                  pl.BlockSpec((None, None, K, Dm), lambda m: (0, 0, 0, 0), pipeline_mode=pl.Buffered(1)), row, row],
        out_specs=[row, row, pl.BlockSpec((1, Dm), lambda m: (0, 0))],
        out_shape=[_sds((M, Dm), F32), _sds((M, Dm), BF16), _sds((1, Dm), F32)],
        compiler_params=_params("arbitrary"),
    )(a, w, res, target)


def mm_nt(dy, w, name, tr, layer=0, out_dtype=F32, tm=512):
    M = dy.shape[0]
    J, _, R, Ns = w.shape
    dims = (((1,), (1,)), ((), ()))

    def body(dy_ref, w_ref, o_ref):
        acc = None
        for j in range(J):
            p = lax.dot_general(dy_ref[:, j * Ns:(j + 1) * Ns], w_ref[j], dims, preferred_element_type=F32)
            acc = p if acc is None else acc + p
        o_ref[...] = acc.astype(o_ref.dtype)

    return pl.pallas_call(
        body, name=name, grid=(R // tr, M // tm),
        in_specs=[pl.BlockSpec((tm, J * Ns), lambda r, m: (m, 0)),
                  pl.BlockSpec((J, None, tr, Ns), lambda r, m: (0, layer, r, 0))],
        out_specs=pl.BlockSpec((tm, tr), lambda r, m: (m, r)),
        out_shape=_sds((M, R), out_dtype),
        compiler_params=_params("parallel", "parallel"),
    )(dy, w)


def mm_nt_norm_bwd(dy, w, x, g, dres, name, layer=0, tm=512):
    M = dy.shape[0]
    J, _, Dm, Ns = w.shape
    dims = (((1,), (1,)), ((), ()))

    def body(dy_ref, w_ref, x_ref, g_ref, r_ref, dx_ref, dxb_ref, dg_ref):
        dxn = None
        for j in range(J):
            p = lax.dot_general(dy_ref[:, j * Ns:(j + 1) * Ns], w_ref[j], dims, preferred_element_type=F32)
            dxn = p if dxn is None else dxn + p
        xv = x_ref[...]
        r = lax.rsqrt(jnp.mean(xv * xv, axis=-1, keepdims=True) + EPS)
        gx = dxn * g_ref[...]
        dot = jnp.sum(gx * xv, axis=-1, keepdims=True)
        dx = r_ref[...] + r * gx - xv * ((r * r * r) * (dot * (1.0 / Dm)))
        dx_ref[...] = dx
        dxb_ref[...] = dx.astype(BF16)
        part = jnp.sum(dxn * (xv * r), axis=0, keepdims=True)

        @pl.when(pl.program_id(0) == 0)
        def _():
            dg_ref[...] = part

        @pl.when(pl.program_id(0) > 0)
        def _():
            dg_ref[...] += part

    row = pl.BlockSpec((tm, Dm), lambda m: (m, 0))
    vec = pl.BlockSpec((1, Dm), lambda m: (0, 0))
    return pl.pallas_call(
        body, name=name, grid=(M // tm,),
        in_specs=[pl.BlockSpec((tm, J * Ns), lambda m: (m, 0)),
                  pl.BlockSpec((J, None, Dm, Ns), lambda m: (0, layer, 0, 0), pipeline_mode=pl.Buffered(1)), row, vec, row],
        out_specs=[row, row, vec],
        out_shape=[_sds((M, Dm), F32), _sds((M, Dm), BF16), _sds((1, Dm), F32)],
        compiler_params=_params("arbitrary"),
    )(dy, w, x, g, dres)


def mm_tn(a, dy, name, J, tk, tm=512, jb=None):
    M, K = a.shape
    jb = jb or J
    Ns = dy.shape[1] // J
    N = jb * Ns
    n_m = M // tm
    dims = (((0,), (0,)), ((), ()))

    def body(a_ref, dy_ref, o_ref, acc_ref):
        p = lax.dot_general(a_ref[...], dy_ref[...], dims, preferred_element_type=F32)
        m = pl.program_id(2)

        @pl.when(m == 0)
        def _():
            acc_ref[...] = p

        @pl.when(m > 0)
        def _():
            acc_ref[...] += p

        @pl.when(m == n_m - 1)
        def _():
            for j in range(jb):
                o_ref[j] = acc_ref[:, j * Ns:(j + 1) * Ns].astype(BF16)

    return pl.pallas_call(
        body, name=name, grid=(J // jb, K // tk, n_m),
        in_specs=[pl.BlockSpec((tm, tk), lambda g, k, m: (m, k)), pl.BlockSpec((tm, N), lambda g, k, m: (m, g))],
        out_specs=pl.BlockSpec((jb, tk, Ns), lambda g, k, m: (g, k, 0)),
        out_shape=_sds((J, K, Ns), BF16), scratch_shapes=[pltpu.VMEM((tk, N), F32)],
        compiler_params=_params("parallel", "parallel", "arbitrary"),
    )(a, dy)


HALO = 16


def _shift_down(x, s):
    return pltpu.roll(x, s, 0)


def _shift_up(x, s):
    return pltpu.roll(x, x.shape[0] - s, 0)


def _conv3(z, cw):
    return (_shift_down(z, 2) * cw[0:1] + _shift_down(z, 1) * cw[1:2]) + z * cw[2:3]


def _window_count(first_row, n, k):
    t = first_row + lax.broadcasted_iota(jnp.int32, (n, 1), 0)
    return jnp.clip(t + 1, 1, k).astype(F32)


def mixer_fwd(proj, conv_w, pool_w, pool_scale, name, ts=256):
    S = proj.shape[0]
    n = ts + HALO

    def body(pm_ref, pb_ref, cw_ref, pw_ref, ps_ref, o_ref):
        i = pl.program_id(0)
        before = jnp.where(i > 0, pb_ref[...], 0.0)
        ext = jnp.concatenate([before, pm_ref[...]], axis=0)
        cw = cw_ref[...]
        z = ext[:, 2 * A_WIDTH:3 * A_WIDTH] * ext[:, 0:A_WIDTH]
        cz = _conv3(z, cw)
        ya = pm_ref[:, A_WIDTH:2 * A_WIDTH] * cz[HALO:]
        o_ref[:, 0:A_WIDTH] = ya.astype(BF16)
        for g, k in enumerate(POOL_WINDOWS):
            lo = 3 * A_WIDTH + g * POOL_GROUP
            p = ext[:, lo:lo + POOL_GROUP]
            w = p
            s = 1
            while s < k:
                w = w + _shift_down(w, s)
                s *= 2
            pooled = w / _window_count(i * ts - HALO, n, k) - p
            yb = jnp.dot(pooled[HALO:].astype(BF16), pw_ref[g], preferred_element_type=F32)
            yb = yb * ps_ref[:, g * POOL_GROUP:(g + 1) * POOL_GROUP]
            o_ref[:, A_WIDTH + g * POOL_GROUP:A_WIDTH + (g + 1) * POOL_GROUP] = yb.astype(BF16)

    hb = ts // HALO
    return pl.pallas_call(
        body, name=name, grid=(S // ts,),
        in_specs=[
            pl.BlockSpec((ts, EVEN_IN), lambda i: (i, 0)),
            pl.BlockSpec((HALO, EVEN_IN), lambda i: (jnp.maximum(i * hb - 1, 0), 0)),
            pl.BlockSpec((3, A_WIDTH), lambda i: (0, 0)),
            pl.BlockSpec((4, POOL_GROUP, POOL_GROUP), lambda i: (0, 0, 0)),
            pl.BlockSpec((1, 4 * POOL_GROUP), lambda i: (0, 0)),
        ],
        out_specs=pl.BlockSpec((ts, D_MODEL), lambda i: (i, 0)),
        out_shape=_sds((S, D_MODEL), BF16), compiler_params=_params("parallel"),
    )(proj, proj, conv_w, pool_w, pool_scale)


def mixer_bwd(proj, dmix, conv_w, pool_w, pool_scale, name, ts=256):
    S = proj.shape[0]
    n = ts + 2 * HALO
    nt = S // ts
    tn_dims = (((0,), (0,)), ((), ()))
    nt_dims = (((1,), (1,)), ((), ()))

    def body(pm_ref, pb_ref, pa_ref, dm_ref, da_ref, cw_ref, pw_ref, ps_ref, o_ref, dcw_ref, dpw_ref, dps_ref):
        i = pl.program_id(0)
        last = i == nt - 1
        before = jnp.where(i > 0, pb_ref[...], 0.0)
        after = jnp.where(last, 0.0, pa_ref[...])
        ext = jnp.concatenate([before, pm_ref[...], after], axis=0)
        dafter = jnp.where(last, 0.0, da_ref[...])
        dext = jnp.concatenate([jnp.zeros((HALO, D_MODEL), F32), dm_ref[...], dafter], axis=0)
        cw = cw_ref[...]
        main = slice(HALO, HALO + ts)

        @pl.when(i == 0)
        def _():
            dcw_ref[...] = jnp.zeros_like(dcw_ref)
            dpw_ref[...] = jnp.zeros_like(dpw_ref)
            dps_ref[...] = jnp.zeros_like(dps_ref)

        h, gb, gc = ext[:, 0:A_WIDTH], ext[:, A_WIDTH:2 * A_WIDTH], ext[:, 2 * A_WIDTH:3 * A_WIDTH]
        z = gc * h
        z1, z2 = _shift_down(z, 1), _shift_down(z, 2)
        cz = (z2 * cw[0:1] + z1 * cw[1:2]) + z * cw[2:3]
        dya = dext[:, 0:A_WIDTH]
        dcz = dya * gb
        dz = dcz * cw[2:3] + _shift_up(dcz, 1) * cw[1:2] + _shift_up(dcz, 2) * cw[0:1]
        o_ref[:, 0:A_WIDTH] = (dz * gc)[main].astype(BF16)
        o_ref[:, A_WIDTH:2 * A_WIDTH] = (dya * cz)[main].astype(BF16)
        o_ref[:, 2 * A_WIDTH:3 * A_WIDTH] = (dz * h)[main].astype(BF16)
        dczm = dcz[main]
        dcw_ref[0:1, :] += jnp.sum(dczm * z2[main], axis=0, keepdims=True)
        dcw_ref[1:2, :] += jnp.sum(dczm * z1[main], axis=0, keepdims=True)
        dcw_ref[2:3, :] += jnp.sum(dczm * z[main], axis=0, keepdims=True)

        for g, k in enumerate(POOL_WINDOWS):
            lo = 3 * A_WIDTH + g * POOL_GROUP
            cols = slice(g * POOL_GROUP, (g + 1) * POOL_GROUP)
            p = ext[:, lo:lo + POOL_GROUP]
            w = p
            s = 1
            while s < k:
                w = w + _shift_down(w, s)
                s *= 2
            cnt = _window_count(i * ts - HALO, n, k)
            pooled = (w / cnt - p)[main].astype(BF16)
            dyb = dext[:, A_WIDTH + g * POOL_GROUP:A_WIDTH + (g + 1) * POOL_GROUP]
            e = dyb * ps_ref[:, cols]
            pre = jnp.dot(pooled, pw_ref[g], preferred_element_type=F32)
            dps_ref[:, cols] += jnp.sum(dyb[main] * pre, axis=0, keepdims=True)
            dpw_ref[g] += lax.dot_general(pooled, e[main].astype(BF16), tn_dims, preferred_element_type=F32)
            dpooled = lax.dot_general(e.astype(BF16), pw_ref[g], nt_dims, preferred_element_type=F32)
            q = dpooled / cnt
            a = q
            s = 1
            while s < k:
                a = a + _shift_up(a, s)
                s *= 2
            o_ref[:, lo:lo + POOL_GROUP] = (a - dpooled)[main].astype(BF16)

    hb = ts // HALO
    nh = S // HALO
    before_map = lambda i: (jnp.maximum(i * hb - 1, 0), 0)
    after_map = lambda i: (jnp.minimum((i + 1) * hb, nh - 1), 0)
    full = lambda *shape: pl.BlockSpec(shape, lambda i: (0,) * len(shape))
    return pl.pallas_call(
        body, name=name, grid=(nt,),
        in_specs=[
            pl.BlockSpec((ts, EVEN_IN), lambda i: (i, 0)),
            pl.BlockSpec((HALO, EVEN_IN), before_map),
            pl.BlockSpec((HALO, EVEN_IN), after_map),
            pl.BlockSpec((ts, D_MODEL), lambda i: (i, 0)),
            pl.BlockSpec((HALO, D_MODEL), after_map),
            full(3, A_WIDTH), full(4, POOL_GROUP, POOL_GROUP), full(1, 4 * POOL_GROUP),
        ],
        out_specs=[pl.BlockSpec((ts, EVEN_IN), lambda i: (i, 0)), full(3, A_WIDTH), full(4, POOL_GROUP, POOL_GROUP),
                   full(1, 4 * POOL_GROUP)],
        out_shape=[_sds((S, EVEN_IN), BF16), _sds((3, A_WIDTH), F32), _sds((4, POOL_GROUP, POOL_GROUP), F32),
                   _sds((1, 4 * POOL_GROUP), F32)],
        compiler_params=_params("arbitrary"),
    )(proj, proj, proj, dmix, dmix, conv_w, pool_w, pool_scale)


FFN_HALO = 16
FFN_TC = 1408


GLU_CHUNKS = ((0, 512), (512, 512), (1024, 384))


def up_glu_fwd(xn, w_up, conv_w, conv_b, name, tm=512):
    S, K = xn.shape
    nc = D_FF // FFN_TC

    def body(xm_ref, xb_ref, wg_ref, wu_ref, cwg_ref, cwu_ref, cbg_ref, cbu_ref, pg_ref, pu_ref, ug_ref, uu_ref, o_ref):
        before = jnp.where(pl.program_id(1) > 0, xb_ref[...], jnp.zeros_like(xb_ref))
        rows = jnp.concatenate([before, xm_ref[...]], axis=0)
        for lo, width in GLU_CHUNKS:
            cols = slice(lo, lo + width)
            pre_g = jnp.dot(rows, wg_ref[:, cols], preferred_element_type=F32)
            pre_u = jnp.dot(rows, wu_ref[:, cols], preferred_element_type=F32)
            gate = _conv3(pre_g, cwg_ref[:, cols])[FFN_HALO:] + cbg_ref[:, cols]
            upv = _conv3(pre_u, cwu_ref[:, cols])[FFN_HALO:] + cbu_ref[:, cols]
            pg_ref[:, cols] = pre_g[FFN_HALO:].astype(BF16)
            pu_ref[:, cols] = pre_u[FFN_HALO:].astype(BF16)
            ug_ref[:, cols] = gate.astype(BF16)
            uu_ref[:, cols] = upv.astype(BF16)
            o_ref[:, cols] = ((gate * (1.0 / (1.0 + jnp.exp(-gate)))) * upv).astype(BF16)

    hb = tm // FFN_HALO
    wspec = lambda off: pl.BlockSpec((None, None, K, FFN_TC), lambda j, m: (j + off, 0, 0, 0))
    cw = lambda off: pl.BlockSpec((3, FFN_TC), lambda j, m: (0, j + off))
    cb = lambda off: pl.BlockSpec((1, FFN_TC), lambda j, m: (0, j + off))
    out = pl.BlockSpec((tm, FFN_TC), lambda j, m: (m, j))
    pg, pu, ug, uu, act = pl.pallas_call(
        body, name=name, grid=(nc, S // tm),
        in_specs=[pl.BlockSpec((tm, K), lambda j, m: (m, 0)),
                  pl.BlockSpec((FFN_HALO, K), lambda j, m: (jnp.maximum(m * hb - 1, 0), 0)),
                  wspec(0), wspec(nc), cw(0), cw(nc), cb(0), cb(nc)],
        out_specs=[out] * 5, out_shape=[_sds((S, D_FF), BF16)] * 5,
        compiler_params=_params("parallel", "parallel"),
    )(xn, xn, w_up, w_up, conv_w, conv_w, conv_b, conv_b)
    return (pg, pu), (ug, uu), act


def glu_bwd(up, u, da, conv_w, name, ts=256):
    S = up[0].shape[0]
    nc = D_FF // FFN_TC
    nt = S // ts
    W = 2 * D_FF

    def body(xg_ref, xu_ref, gm_ref, ga_ref, um_ref, ua_ref, dm_ref, da_ref, cw_ref, dx_ref, dcw_ref, dcb_ref):
        i = pl.program_id(0)
        last = i == nt - 1

        @pl.when(i == 0)
        def _():
            dcw_ref[...] = jnp.zeros_like(dcw_ref)
            dcb_ref[...] = jnp.zeros_like(dcb_ref)

        def rows(m_ref, a_ref, cols):
            return jnp.concatenate([m_ref[:, cols], a_ref[:, cols]], axis=0).astype(F32)

        def back(d, x, cols):
            cw = cw_ref[:, cols]
            d1, d2 = _shift_up(d, 1), _shift_up(d, 2)
            dx_ref[:, cols] = ((d * cw[2:3] + d1 * cw[1:2]) + d2 * cw[0:1])[:ts].astype(BF16)
            dcb_ref[:, cols] += jnp.sum(d[:ts], axis=0, keepdims=True)
            dcw_ref[0:1, cols] += jnp.sum(d2[:ts] * x, axis=0, keepdims=True)
            dcw_ref[1:2, cols] += jnp.sum(d1[:ts] * x, axis=0, keepdims=True)
            dcw_ref[2:3, cols] += jnp.sum(d[:ts] * x, axis=0, keepdims=True)

        for c in range(nc):
            cols = slice(c * FFN_TC, (c + 1) * FFN_TC)
            ug, uu = rows(gm_ref, ga_ref, cols), rows(um_ref, ua_ref, cols)
            dae = rows(dm_ref, da_ref, cols)
            dae = jnp.where(last & (lax.broadcasted_iota(jnp.int32, dae.shape, 0) >= ts), 0.0, dae)
            sg = 1.0 / (1.0 + jnp.exp(-ug))
            duu = dae * (ug * sg)
            dug = (dae * uu) * (sg * (1.0 + ug * (1.0 - sg)))
            back(dug, xg_ref[:, cols].astype(F32), cols)
            back(duu, xu_ref[:, cols].astype(F32), slice(D_FF + c * FFN_TC, D_FF + (c + 1) * FFN_TC))

    hb = ts // FFN_HALO
    nh = S // FFN_HALO
    after_map = lambda i: (jnp.minimum((i + 1) * hb, nh - 1), 0)
    main = pl.BlockSpec((ts, D_FF), lambda i: (i, 0))
    after = pl.BlockSpec((FFN_HALO, D_FF), after_map)
    return pl.pallas_call(
        body, name=name, grid=(nt,),
        in_specs=[main, main, main, after, main, after, main, after, pl.BlockSpec((3, W), lambda i: (0, 0))],
        out_specs=[pl.BlockSpec((ts, W), lambda i: (i, 0)), pl.BlockSpec((3, W), lambda i: (0, 0)),
                   pl.BlockSpec((1, W), lambda i: (0, 0))],
        out_shape=[_sds((S, W), BF16), _sds((3, W), F32), _sds((1, W), F32)],
        compiler_params=_params("arbitrary"),
    )(up[0], up[1], u[0], u[0], u[1], u[1], da, da, conv_w)


MEAN_GROUP = 256


def _head_mean_matrix():
    h = np.arange(MEAN_GROUP) // HEAD_DIM
    return jnp.asarray((h[:, None] == h[None, :]).astype(np.float32) / HEAD_DIM, dtype=BF16)


def _head_mean(v, gm):
    vb = v.astype(BF16)
    return jnp.concatenate([jnp.dot(vb[:, c:c + MEAN_GROUP], gm, preferred_element_type=F32)
                            for c in range(0, v.shape[1], MEAN_GROUP)], axis=1)


def qknorm_fwd(qkv, gqk, name, ts=512):
    S = qkv.shape[0]

    def body(x_ref, g_ref, gm_ref, o_ref):
        part = pl.program_id(0)
        x = x_ref[...]

        @pl.when(part < 2)
        def _():
            r = lax.rsqrt(_head_mean(x * x, gm_ref[...]) + EPS)
            o_ref[...] = ((x * r) * g_ref[...]).astype(BF16)

        @pl.when(part == 2)
        def _():
            o_ref[...] = x.astype(BF16)

    return pl.pallas_call(
        body, name=name, grid=(3, S // ts),
        in_specs=[pl.BlockSpec((ts, D_MODEL), lambda p, i: (i, p)), pl.BlockSpec((None, 1, D_MODEL), lambda p, i: (p, 0, 0)),
                  pl.BlockSpec((MEAN_GROUP, MEAN_GROUP), lambda p, i: (0, 0))],
        out_specs=pl.BlockSpec((ts, D_MODEL), lambda p, i: (i, p)),
        out_shape=_sds((S, 3 * D_MODEL), BF16), compiler_params=_params("parallel", "parallel"),
    )(qkv, gqk, _head_mean_matrix())


def qknorm_bwd(qkv, dq, dk, dv, gqk, name, ts=256):
    S = qkv.shape[0]

    def body(x_ref, dq_ref, dk_ref, dv_ref, g_ref, gm_ref, o_ref, dg_ref):
        @pl.when(pl.program_id(0) == 0)
        def _():
            dg_ref[...] = jnp.zeros_like(dg_ref)

        gm = gm_ref[...]
        for part, d_ref in enumerate((dq_ref, dk_ref)):
            cols = slice(part * D_MODEL, (part + 1) * D_MODEL)
            x = x_ref[:, cols]
            d = d_ref[...]
            r = lax.rsqrt(_head_mean(x * x, gm) + EPS)
            gx = d * g_ref[part]
            o_ref[:, cols] = (r * gx - x * ((r * r * r) * _head_mean(gx * x, gm))).astype(BF16)
            dg_ref[part] += jnp.sum(d * (x * r), axis=0, keepdims=True)
        o_ref[:, 2 * D_MODEL:] = dv_ref[...].astype(BF16)

    row = pl.BlockSpec((ts, D_MODEL), lambda i: (i, 0))
    wide = pl.BlockSpec((ts, 3 * D_MODEL), lambda i: (i, 0))
    gains = pl.BlockSpec((3, 1, D_MODEL), lambda i: (0, 0, 0))
    return pl.pallas_call(
        body, name=name, grid=(S // ts,),
        in_specs=[wide, row, row, row, gains, pl.BlockSpec((MEAN_GROUP, MEAN_GROUP), lambda i: (0, 0))],
        out_specs=[wide, gains],
        out_shape=[_sds((S, 3 * D_MODEL), BF16), _sds((3, 1, D_MODEL), F32)],
        compiler_params=_params("arbitrary"),
    )(qkv, dq, dk, dv, gqk, _head_mean_matrix())


RESIDUES = 16


def _block_order(dil):
    runs = RESIDUES // dil
    slot = np.arange(ATT_BLOCK)
    return (slot % (ATT_BLOCK // runs)) * runs + slot // (ATT_BLOCK // runs)


def _bucket_tables():
    n = ATT_BLOCK
    max_exact = N_REL_BUCKETS // 2
    buckets, valids = [], []
    for _, dil in DILATED_PAIRS:
        order = _block_order(dil)
        a = order[:, None]
        c = np.concatenate([order, n + order])[None, :]
        first_half = (np.arange(2 * n) < n)[None, :]
        rel = a + n - c
        band = (rel >= 0) & (rel <= n)
        dist = np.clip(rel, 0, n) * dil
        dd = np.maximum(dist, 1).astype(np.float32)
        large = max_exact + (np.log(dd / np.float32(max_exact)) / np.float32(math.log(REL_MAX_DISTANCE / max_exact))
                             * np.float32(N_REL_BUCKETS - max_exact)).astype(np.int32)
        large = np.minimum(large, N_REL_BUCKETS - 1)
        buckets.append(np.where(dist < max_exact, dist, large).reshape(1, -1))
        valids.append(np.stack([(band & ~first_half).reshape(1, -1), band.reshape(1, -1)]))
    return np.stack(buckets).astype(np.int32), np.stack(valids).astype(np.int32)


BIAS_CHUNK = 8192


def _split3(x):
    a = x.astype(BF16)
    r = x - a.astype(F32)
    b = r.astype(BF16)
    c = (r - b.astype(F32)).astype(BF16)
    return a, b, c


def bias_expand(rel_bias_t, name):
    bucket, valid = _bucket_tables()
    nq = bucket.shape[-1]

    def body(t_ref, b_ref, v_ref, o_ref):
        onehot = (lax.broadcasted_iota(jnp.int32, (N_REL_BUCKETS, BIAS_CHUNK), 0) == b_ref[...]).astype(BF16)
        acc = None
        for term in _split3(t_ref[...]):
            p = jnp.dot(term, onehot, preferred_element_type=F32)
            acc = p if acc is None else acc + p
        o_ref[...] = jnp.where(v_ref[...] > 0, acc, MASK_VALUE)

    return pl.pallas_call(
        body, name=name, grid=(3, 2, nq // BIAS_CHUNK),
        in_specs=[pl.BlockSpec((N_HEADS, N_REL_BUCKETS), lambda b, v, c: (0, 0)),
                  pl.BlockSpec((None, 1, BIAS_CHUNK), lambda b, v, c: (b, 0, c)),
                  pl.BlockSpec((None, None, 1, BIAS_CHUNK), lambda b, v, c: (b, v, 0, c))],
        out_specs=pl.BlockSpec((None, None, N_HEADS, BIAS_CHUNK), lambda b, v, c: (b, v, 0, c)),
        out_shape=_sds((3, 2, N_HEADS, nq), F32), compiler_params=_params("parallel", "parallel", "parallel"),
    )(rel_bias_t, jnp.asarray(bucket), jnp.asarray(valid))


def bias_reduce(dbias, name):
    bucket, _ = _bucket_tables()
    nq = bucket.shape[-1]
    dims = (((1,), (1,)), ((), ()))

    def body(d_ref, b_ref, o_ref):
        onehot = (lax.broadcasted_iota(jnp.int32, (N_REL_BUCKETS, BIAS_CHUNK), 0) == b_ref[...]).astype(BF16)
        acc = None
        for term in _split3(d_ref[...]):
            p = lax.dot_general(term, onehot, dims, preferred_element_type=F32)
            acc = p if acc is None else acc + p

        @pl.when(pl.program_id(1) == 0)
        def _():
            o_ref[...] = acc

        @pl.when(pl.program_id(1) > 0)
        def _():
            o_ref[...] += acc

    return pl.pallas_call(
        body, name=name, grid=(3, nq // BIAS_CHUNK),
        in_specs=[pl.BlockSpec((None, N_HEADS, BIAS_CHUNK), lambda b, c: (b, 0, c)),
                  pl.BlockSpec((None, 1, BIAS_CHUNK), lambda b, c: (b, 0, c))],
        out_specs=pl.BlockSpec((None, N_HEADS, N_REL_BUCKETS), lambda b, c: (b, 0, 0)),
        out_shape=_sds((3, N_HEADS, N_REL_BUCKETS), F32), compiler_params=_params("parallel", "arbitrary"),
    )(dbias, jnp.asarray(bucket))


PAIR = 2 * HEAD_DIM
N_PAIRS = N_HEADS // 2
_NT = (((1,), (1,)), ((), ()))
_TN = (((0,), (0,)), ((), ()))


def _low_lanes(shape):
    return lax.broadcasted_iota(jnp.int32, shape, 1) < HEAD_DIM


ATTN_VMEM_LIMIT_BYTES = 56 * 1024 * 1024
BRANCH_ORDER = (2, 1, 0)


def _regroup(dst, src, L16):
    for r in range(RESIDUES):
        dst[pl.ds(r * L16, L16), :] = src[pl.ds(r, L16, stride=RESIDUES), :]


def _ungroup(dst, src, L16):
    for r in range(RESIDUES):
        dst[pl.ds(r, L16, stride=RESIDUES), :] = src[pl.ds(r * L16, L16), :]


def _branch_geometry(branch, S):
    dil = DILATED_PAIRS[branch][1]
    runs = RESIDUES // dil
    return dil, runs, ATT_BLOCK // runs, S // dil // ATT_BLOCK


def _block_rows(it, branch, S):
    dil, runs, run_len, n_blocks = _branch_geometry(branch, S)
    L16 = S // RESIDUES
    r, b = it // n_blocks, it % n_blocks
    prev = jnp.maximum(b - 1, 0)
    cur_rows = [pl.multiple_of((j * dil + r) * L16 + run_len * b, 8) for j in range(runs)]
    prev_rows = [pl.multiple_of((j * dil + r) * L16 + run_len * prev, 8) for j in range(runs)]
    return cur_rows, prev_rows, jnp.minimum(b, 1)


def _load_block(ref, rows, run_len):
    parts = [ref[pl.ds(o, run_len), :] for o in rows]
    return parts[0] if len(parts) == 1 else jnp.concatenate(parts, axis=0)


def _store_block(ref, rows, run_len, value, add=False):
    for j, o in enumerate(rows):
        part = value[j * run_len:(j + 1) * run_len]
        if add:
            ref[pl.ds(o, run_len), :] += part
        else:
            ref[pl.ds(o, run_len), :] = part


ATTN_FWD_UNROLL = 8
ATTN_BWD_UNROLL = 4


def _stack_heads(x, low):
    zero = jnp.zeros_like(x)
    return jnp.concatenate([jnp.where(low, x, zero), jnp.where(low, zero, x)], axis=0)


def _unstack_heads(y, low):
    return jnp.where(low, y[:ATT_BLOCK], y[ATT_BLOCK:])


def attn_fwd(qkvn, bias, name):
    S = qkvn.shape[0]
    L16 = S // RESIDUES
    n_iter = S // ATT_BLOCK

    def body(q_ref, k_ref, v_ref, b_ref, o_ref, lse_ref, stage, qp, kp, vp, acc_s, m_s, l_s):
        for src, dst in ((q_ref, qp), (k_ref, kp), (v_ref, vp)):
            stage[...] = src[...].astype(F32)
            _regroup(dst, stage, L16)
        low = _low_lanes((ATT_BLOCK, PAIR))

        for branch in BRANCH_ORDER:
            _, _, run_len, _ = _branch_geometry(branch, S)
            first = branch == BRANCH_ORDER[0]

            def step(it, carry, branch=branch, run_len=run_len, first=first):
                cur, prev, variant = _block_rows(it, branch, S)
                q = _load_block(qp, cur, run_len).astype(BF16)
                k = jnp.concatenate([_load_block(kp, prev, run_len), _load_block(kp, cur, run_len)], axis=0).astype(BF16)
                v = jnp.concatenate([_load_block(vp, prev, run_len), _load_block(vp, cur, run_len)], axis=0).astype(BF16)
                s = lax.dot_general(_stack_heads(q, low), k, _NT, preferred_element_type=F32) * (HEAD_DIM ** -0.5)
                s = s + b_ref[2 * branch + variant].reshape(2 * ATT_BLOCK, 2 * ATT_BLOCK)
                mx = jnp.max(s, axis=-1, keepdims=True)
                p = jnp.exp(s - mx)
                den = jnp.sum(p, axis=-1, keepdims=True)
                pv = jnp.dot(p.astype(BF16), v, preferred_element_type=F32)
                acc = _unstack_heads(pv, low)
                m = _unstack_heads(mx, low)
                l = _unstack_heads(den, low)
                if not first:
                    m_old = _load_block(m_s, cur, run_len)
                    m_new = jnp.maximum(m_old, m)
                    a_old, a_new = jnp.exp(m_old - m_new), jnp.exp(m - m_new)
                    acc = _load_block(acc_s, cur, run_len) * a_old + acc * a_new
                    l = _load_block(l_s, cur, run_len) * a_old + l * a_new
                    m = m_new
                _store_block(acc_s, cur, run_len, acc)
                _store_block(m_s, cur, run_len, m)
                _store_block(l_s, cur, run_len, l)
                return carry

            lax.fori_loop(0, n_iter, step, 0, unroll=ATTN_FWD_UNROLL)

        acc_s[...] = acc_s[...] / l_s[...]
        _ungroup(stage, acc_s, L16)
        o_ref[...] = stage[...].astype(BF16)
        m_s[...] = m_s[...] + jnp.log(l_s[...])
        _ungroup(lse_ref, m_s, L16)

    col = lambda part: pl.BlockSpec((S, PAIR), lambda hp: (0, part * N_PAIRS + hp))
    out = pl.BlockSpec((S, PAIR), lambda hp: (0, hp))
    return pl.pallas_call(
        body, name=name, grid=(N_PAIRS,),
        in_specs=[col(0), col(1), col(2), pl.BlockSpec((6, 2, ATT_BLOCK, 2 * ATT_BLOCK), lambda hp: (0, hp, 0, 0))],
        out_specs=[out, out], out_shape=[_sds((S, D_MODEL), BF16), _sds((S, D_MODEL), F32)],
        scratch_shapes=[pltpu.VMEM((S, PAIR), F32)] * 7,
        compiler_params=pltpu.CompilerParams(dimension_semantics=("parallel",), vmem_limit_bytes=ATTN_VMEM_LIMIT_BYTES),
    )(qkvn, qkvn, qkvn, bias)


def attn_bwd(qkvn, att, datt, lse, bias, name):
    S = qkvn.shape[0]
    L16 = S // RESIDUES
    n_iter = S // ATT_BLOCK
    TILE = 512

    def body(q_ref, k_ref, v_ref, o_ref, do_ref, lse_ref, b_ref, dq_ref, dk_ref, dv_ref, db_ref,
             qp, kp, vp, dop, ldp, dqp, dkp, dvp):
        stage = dqp
        for src, dst in ((q_ref, qp), (k_ref, kp), (v_ref, vp), (do_ref, dop)):
            stage[...] = src[...].astype(F32)
            _regroup(dst, stage, L16)

        def pack(i, carry):
            rows = pl.ds(pl.multiple_of(i * TILE, TILE), TILE)
            low = _low_lanes((TILE, PAIR))
            lane = lax.broadcasted_iota(jnp.int32, (TILE, PAIR), 1)
            prod = do_ref[rows, :].astype(F32) * o_ref[rows, :].astype(F32)
            d0 = jnp.sum(jnp.where(low, prod, 0.0), axis=-1, keepdims=True)
            d1 = jnp.sum(jnp.where(low, 0.0, prod), axis=-1, keepdims=True)
            stage[rows, :] = jnp.where((lane & (HEAD_DIM // 2)) == 0, lse_ref[rows, :], jnp.where(low, d0, d1))
            return carry

        lax.fori_loop(0, S // TILE, pack, 0)
        _regroup(ldp, stage, L16)
        dqp[...] = jnp.zeros_like(dqp)
        dkp[...] = jnp.zeros_like(dkp)
        dvp[...] = jnp.zeros_like(dvp)
        db_ref[...] = jnp.zeros_like(db_ref)
        low = _low_lanes((ATT_BLOCK, PAIR))

        for branch in BRANCH_ORDER:
            _, _, run_len, _ = _branch_geometry(branch, S)

            def step(it, carry, branch=branch, run_len=run_len):
                cur, prev, variant = _block_rows(it, branch, S)
                q = _load_block(qp, cur, run_len).astype(BF16)
                dout = _load_block(dop, cur, run_len).astype(BF16)
                ld = _load_block(ldp, cur, run_len)
                k = jnp.concatenate([_load_block(kp, prev, run_len), _load_block(kp, cur, run_len)], axis=0).astype(BF16)
                v = jnp.concatenate([_load_block(vp, prev, run_len), _load_block(vp, cur, run_len)], axis=0).astype(BF16)
                half = HEAD_DIM // 2
                lse2 = jnp.concatenate([ld[:, 0:1], ld[:, HEAD_DIM:HEAD_DIM + 1]], axis=0)
                delta2 = jnp.concatenate([ld[:, half:half + 1], ld[:, HEAD_DIM + half:HEAD_DIM + half + 1]], axis=0)
                q2, do2 = _stack_heads(q, low), _stack_heads(dout, low)
                s = lax.dot_general(q2, k, _NT, preferred_element_type=F32) * (HEAD_DIM ** -0.5)
                p = jnp.exp(s + b_ref[2 * branch + variant].reshape(2 * ATT_BLOCK, 2 * ATT_BLOCK) - lse2)
                dp = lax.dot_general(do2, v, _NT, preferred_element_type=F32)
                ds = p * (dp - delta2)
                db_ref[branch] += ds.reshape(2, ATT_BLOCK, 2 * ATT_BLOCK)
                dsb = (ds * (HEAD_DIM ** -0.5)).astype(BF16)
                dq = _unstack_heads(jnp.dot(dsb, k, preferred_element_type=F32), low)
                dk = lax.dot_general(dsb, q2, _TN, preferred_element_type=F32)
                dv = lax.dot_general(p.astype(BF16), do2, _TN, preferred_element_type=F32)
                _store_block(dqp, cur, run_len, dq, add=True)
                _store_block(dkp, prev, run_len, dk[:ATT_BLOCK], add=True)
                _store_block(dvp, prev, run_len, dv[:ATT_BLOCK], add=True)
                _store_block(dkp, cur, run_len, dk[ATT_BLOCK:], add=True)
                _store_block(dvp, cur, run_len, dv[ATT_BLOCK:], add=True)
                return carry

            lax.fori_loop(0, n_iter, step, 0, unroll=ATTN_BWD_UNROLL)

        _ungroup(dq_ref, dqp, L16)
        _ungroup(dk_ref, dkp, L16)
        _ungroup(dv_ref, dvp, L16)

    col = lambda part: pl.BlockSpec((S, PAIR), lambda hp: (0, part * N_PAIRS + hp))
    one = pl.BlockSpec((S, PAIR), lambda hp: (0, hp))
    return pl.pallas_call(
        body, name=name, grid=(N_PAIRS,),
        in_specs=[col(0), col(1), col(2), one, one, one,
                  pl.BlockSpec((6, 2, ATT_BLOCK, 2 * ATT_BLOCK), lambda hp: (0, hp, 0, 0))],
        out_specs=[one, one, one, pl.BlockSpec((3, 2, ATT_BLOCK, 2 * ATT_BLOCK), lambda hp: (0, hp, 0, 0))],
        out_shape=[_sds((S, D_MODEL), F32)] * 3 + [_sds((3, N_HEADS, ATT_BLOCK, 2 * ATT_BLOCK), F32)],
        scratch_shapes=[pltpu.VMEM((S, PAIR), F32)] * 8,
        compiler_params=pltpu.CompilerParams(dimension_semantics=("parallel",), vmem_limit_bytes=ATTN_VMEM_LIMIT_BYTES),
    )(qkvn, qkvn, qkvn, att, datt, lse, bias)


def adamw(w, g, m, v, name):
    n, R, C = w.shape

    def body(w_ref, g_ref, m_ref, v_ref, d_ref, nm_ref, nv_ref, go_ref):
        gv = g_ref[...]
        go_ref[...] = gv
        m2 = ADAM_B1 * m_ref[...] + (1.0 - ADAM_B1) * gv
        v2 = ADAM_B2 * v_ref[...] + (1.0 - ADAM_B2) * (gv * gv)
        m_hat = m2 / (1.0 - ADAM_B1 ** ADAM_STEP)
        v_hat = v2 / (1.0 - ADAM_B2 ** ADAM_STEP)
        d_ref[...] = -ADAM_LR * (m_hat / (jnp.sqrt(v_hat) + ADAM_EPS) + ADAM_WD * w_ref[...])
        nm_ref[...] = m2
        nv_ref[...] = v2

    tr = R
    while tr * C * 4 > (1 << 21) and tr % 16 == 0:
        tr //= 2
    spec = pl.BlockSpec((None, tr, C), lambda i, r: (i, r, 0))
    return pl.pallas_call(
        body, name=name, grid=(n, R // tr), in_specs=[spec] * 4, out_specs=[spec] * 4,
        out_shape=[_sds((n, R, C), F32)] * 4, compiler_params=_params("parallel", "parallel"),
    )(w, g, m, v)


ANY = pl.BlockSpec(memory_space=pl.ANY)


def _coords():
    return lax.axis_index("x"), lax.axis_index("y"), lax.axis_index("c")


def _other_chips(mx, my):
    return [(1 - mx, my), (mx, 1 - my), (1 - mx, 1 - my)]


def _remote(src, dst, send, recv, dev):
    return pltpu.make_async_remote_copy(src_ref=src, dst_ref=dst, send_sem=send, recv_sem=recv, device_id=dev,
                                        device_id_type=MESH)


HBM =pl.BlockSpec(memory_space=pltpu.HBM)
SEM = pl.BlockSpec(memory_space=pltpu.SEMAPHORE)
_SPLIT_COPY = pltpu.CompilerParams(has_side_effects=pltpu.SideEffectType.DATAFLOW_SIDE_EFFECTING)


def _in_hbm(a):
    return pltpu.with_memory_space_constraint(a, pltpu.HBM)


def cast_into_slot(w, layer, chip_core, name, dtype=BF16):
    _, _, hR, C = w.shape

    def body(s_ref, w_ref, o_ref):
        del s_ref
        o_ref[...] = w_ref[...].astype(dtype)

    grid_spec = pltpu.PrefetchScalarGridSpec(
        num_scalar_prefetch=1, grid=(2,),
        in_specs=[pl.BlockSpec((None, None, hR, C), lambda h, s: (layer, h, 0, 0))],
        out_specs=pl.BlockSpec((None, None, hR, C), lambda h, s: (s[0], h, 0, 0)))
    return pl.pallas_call(body, name=name, grid_spec=grid_spec, out_shape=_sds((N_CHIPS, 2, hR, C), dtype),
                          compiler_params=_params("parallel"))(chip_core, w)


def gather_start(lands, groups, name):
    n = len(lands)
    n_groups = len(groups)

    def body(*refs):
        ins = refs[:n]
        sems = refs[n:n + 2 * n_groups]
        token = refs[-1]
        mx, my, mc = _coords()
        chip = 2 * mx + my
        for g, members in enumerate(groups):
            send, recv = sems[2 * g], sems[2 * g + 1]
            for i, a in enumerate(members):
                mine = ins[a].at[chip, mc]
                for k, (px, py) in enumerate(_other_chips(mx, my)):
                    _remote(mine, mine, send.at[3 * i + k], recv.at[3 * i + k], (px, py, mc)).start()
        token[...] = jnp.zeros_like(token)

    sem_shapes = []
    for members in groups:
        sem_shapes += [pltpu.SemaphoreType.DMA((3 * len(members),))] * 2
    outs = pl.pallas_call(
        body, name=name, in_specs=[HBM] * n,
        out_specs=[SEM] * (2 * n_groups) + [HBM] * n + [pl.BlockSpec(memory_space=pltpu.VMEM)],
        out_shape=sem_shapes + [pltpu.HBM(a.shape, a.dtype) for a in lands] + [_sds((SUBLANES, LANES), F32)],
        input_output_aliases={a: 2 * n_groups + a for a in range(n)}, compiler_params=_SPLIT_COPY,
    )(*[_in_hbm(a) for a in lands])
    sems = [(outs[2 * g], outs[2 * g + 1]) for g in range(n_groups)]
    return sems, list(outs[2 * n_groups:2 * n_groups + n]), outs[-1]


def gather_forward(lands, sems, after, name):
    n = len(lands)

    def body(*refs):
        ins = refs[:n]
        send, recv = refs[n], refs[n + 1]
        fsend, frecv = refs[n + 3], refs[n + 4]
        mx, my, mc = _coords()
        for i in range(n):
            for k, (px, py) in enumerate(_other_chips(mx, my)):
                landed = ins[i].at[2 * px + py, mc]
                cp = _remote(landed, landed, send.at[3 * i + k], recv.at[3 * i + k], (px, py, mc))
                cp.wait_send()
                cp.wait_recv()
                _remote(landed, landed, fsend.at[3 * i + k], frecv.at[3 * i + k], (mx, my, 1 - mc)).start()

    outs = pl.pallas_call(
        body, name=name, in_specs=[HBM] * n + [SEM, SEM, ANY], out_specs=[SEM, SEM] + [HBM] * n,
        out_shape=[pltpu.SemaphoreType.DMA((3 * n,))] * 2 + [pltpu.HBM(a.shape, a.dtype) for a in lands],
        input_output_aliases={a: 2 + a for a in range(n)}, compiler_params=_SPLIT_COPY,
    )(*lands, sems[0], sems[1], after)
    return (outs[0], outs[1]), list(outs[2:])


def gather_wait(lands, sems, after, name):
    n = len(lands)

    def body(*refs):
        ins = refs[:n]
        fsend, frecv = refs[n], refs[n + 1]
        mx, my, mc = _coords()
        for i in range(n):
            for k, (px, py) in enumerate(_other_chips(mx, my)):
                theirs = ins[i].at[2 * px + py, 1 - mc]
                cp = _remote(theirs, theirs, fsend.at[3 * i + k], frecv.at[3 * i + k], (mx, my, 1 - mc))
                cp.wait_send()
                cp.wait_recv()

    outs = pl.pallas_call(
        body, name=name, in_specs=[HBM] * n + [SEM, SEM, ANY], out_specs=[HBM] * n,
        out_shape=[pltpu.HBM(a.shape, a.dtype) for a in lands],
        input_output_aliases={a: a for a in range(n)}, compiler_params=_SPLIT_COPY,
    )(*lands, sems[0], sems[1], after)
    return list(outs)


def _peers(mx, my, mc):
    return [(1 - mx if k & 4 else mx, 1 - my if k & 2 else my, 1 - mc if k & 1 else mc) for k in range(1, N_DEV)]


def devices_start(x, name):
    def body(x_ref, land_ref, send, recv, x_thru, land_thru):
        mx, my, mc = _coords()
        me = 4 * mx + 2 * my + mc
        for k, peer in enumerate(_peers(mx, my, mc)):
            _remote(x_ref, land_ref.at[me], send.at[k], recv.at[k], peer).start()

    land = lax.empty((N_DEV,) + x.shape, x.dtype)
    outs = pl.pallas_call(
        body, name=name, in_specs=[HBM, HBM], out_specs=[SEM, SEM, HBM, HBM],
        out_shape=[pltpu.SemaphoreType.DMA((N_DEV - 1,))] * 2 + [pltpu.HBM(x.shape, x.dtype), pltpu.HBM(land.shape, x.dtype)],
        input_output_aliases={0: 2, 1: 3}, compiler_params=_SPLIT_COPY,
    )(_in_hbm(x), _in_hbm(land))
    return (outs[0], outs[1]), outs[2], outs[3]


def devices_wait(x, land, sems, after, name):
    def body(x_ref, land_ref, send, recv, after_ref, x_thru, land_thru):
        mx, my, mc = _coords()
        for k, (px, py, pc) in enumerate(_peers(mx, my, mc)):
            cp = _remote(x_ref, land_ref.at[4 * px + 2 * py + pc], send.at[k], recv.at[k], (px, py, pc))
            cp.wait_send()
            cp.wait_recv()

    outs = pl.pallas_call(
        body, name=name, in_specs=[HBM, HBM, SEM, SEM, ANY], out_specs=[HBM, HBM],
        out_shape=[pltpu.HBM(x.shape, x.dtype), pltpu.HBM(land.shape, land.dtype)],
        input_output_aliases={0: 0, 1: 1}, compiler_params=_SPLIT_COPY,
    )(x, land, sems[0], sems[1], after)
    return outs[0], outs[1]


def device_sum(land, own, me, name):
    _, R, C = land.shape

    def body(s_ref, l_ref, o_ref_in, o_ref):
        acc = None
        for q in range(N_DEV):
            term = jnp.where(s_ref[0] == q, o_ref_in[...], l_ref[q])
            acc = term if acc is None else acc + term
        o_ref[...] = acc

    grid_spec = pltpu.PrefetchScalarGridSpec(
        num_scalar_prefetch=1, grid=(1,),
        in_specs=[pl.BlockSpec((N_DEV, R, C), lambda i, s: (0, 0, 0)), pl.BlockSpec((R, C), lambda i, s: (0, 0))],
        out_specs=pl.BlockSpec((R, C), lambda i, s: (0, 0)))
    return pl.pallas_call(body, name=name, grid_spec=grid_spec, out_shape=_sds((R, C), F32),
                          compiler_params=_params("arbitrary"))(me, land, own)


def reduce_send(grads, name):
    n = len(grads)

    def body(*refs):
        ins, lands = refs[:n], refs[n:2 * n]
        send, recv = refs[2 * n], refs[2 * n + 1]
        mx, my, mc = _coords()
        me = 4 * mx + 2 * my + mc
        for a in range(n):
            for k, (px, py, pc) in enumerate(_peers(mx, my, mc)):
                _remote(ins[a].at[2 * px + py, pc], lands[a].at[me], send.at[7 * a + k], recv.at[7 * a + k], (px, py, pc)).start()

    lands = [lax.empty((N_DEV,) + g.shape[2:], g.dtype) for g in grads]
    outs = pl.pallas_call(
        body, name=name, in_specs=[HBM] * (2 * n), out_specs=[SEM, SEM] + [HBM] * (2 * n),
        out_shape=[pltpu.SemaphoreType.DMA((7 * n,))] * 2 + [pltpu.HBM(a.shape, a.dtype) for a in grads + lands],
        input_output_aliases={a: 2 + a for a in range(2 * n)}, compiler_params=_SPLIT_COPY,
    )(*[_in_hbm(a) for a in grads + lands])
    return (outs[0], outs[1]), list(outs[2:2 + n]), list(outs[2 + n:])


def reduce_wait(grads, lands, sems, after, name):
    n = len(grads)

    def body(*refs):
        ins, zones = refs[:n], refs[n:2 * n]
        send, recv = refs[2 * n], refs[2 * n + 1]
        mx, my, mc = _coords()
        for a in range(n):
            for k, (px, py, pc) in enumerate(_peers(mx, my, mc)):
                cp = _remote(ins[a].at[2 * px + py, pc], zones[a].at[4 * px + 2 * py + pc], send.at[7 * a + k],
                             recv.at[7 * a + k], (px, py, pc))
                cp.wait_send()
                cp.wait_recv()

    outs = pl.pallas_call(
        body, name=name, in_specs=[HBM] * (2 * n) + [SEM, SEM, ANY], out_specs=[HBM] * (2 * n),
        out_shape=[pltpu.HBM(a.shape, a.dtype) for a in grads + lands],
        input_output_aliases={a: a for a in range(2 * n)}, compiler_params=_SPLIT_COPY,
    )(*grads, *lands, sems[0], sems[1], after)
    return list(outs[:n]), list(outs[n:])


def reduce_sum(land, grad, place, name, into=None, layer=None):
    _, hR, C = land.shape
    tr = hR
    while N_DEV * tr * C * 2 > (6 << 20) and tr % 32 == 0:
        tr //= 2

    def body(s_ref, l_ref, g_ref, *rest):
        o_ref = rest[-1]
        own = g_ref[...].astype(F32)
        acc = None
        for q in range(N_DEV):
            term = jnp.where(s_ref[2] == q, own, l_ref[q].astype(F32))
            acc = term if acc is None else acc + term
        o_ref[...] = acc

    in_specs = [pl.BlockSpec((N_DEV, tr, C), lambda i, s: (0, i, 0)),
                pl.BlockSpec((None, None, tr, C), lambda i, s: (s[0], s[1], i, 0))]
    args = [place, land, grad]
    aliases = {}
    if layer is None:
        out_spec = pl.BlockSpec((None, tr, C), lambda i, s: (s[1], i, 0))
        out_shape = _sds((2, hR, C), F32)
    else:
        out_spec = pl.BlockSpec((None, None, tr, C), lambda i, s: (layer, s[1], i, 0))
        out_shape = _sds((2, 2, hR, C), F32)
        if into is not None:
            in_specs.append(ANY)
            args.append(into)
            aliases = {3: 0}
    grid_spec = pltpu.PrefetchScalarGridSpec(num_scalar_prefetch=1, grid=(hR // tr,), in_specs=in_specs, out_specs=out_spec)
    return pl.pallas_call(body, name=name, grid_spec=grid_spec, out_shape=out_shape, input_output_aliases=aliases,
                          compiler_params=_params("arbitrary"))(*args)


def join_halves(arrays, name):
    n = len(arrays)
    pieces = [(a, l) for a, arr in enumerate(arrays) for l in (range(arr.shape[0]) if arr.ndim == 4 else [None])]

    def body(*refs):
        ins = refs[:n]
        send, recv = refs[2 * n:]
        mx, my, mc = _coords()

        def half(a, l, h):
            return ins[a].at[h] if l is None else ins[a].at[l, h]

        sends = [_remote(half(a, l, mc), half(a, l, mc), send.at[i], recv.at[i], (mx, my, 1 - mc))
                 for i, (a, l) in enumerate(pieces)]
        for cp in sends:
            cp.start()
        for i, (a, l) in enumerate(pieces):
            theirs = half(a, l, 1 - mc)
            _remote(theirs, theirs, send.at[i], recv.at[i], (mx, my, 1 - mc)).wait_recv()
        for cp in sends:
            cp.wait_send()

    return pl.pallas_call(
        body, name=name, in_specs=[ANY] * n, out_specs=[ANY] * n, out_shape=[_sds(a.shape, a.dtype) for a in arrays],
        input_output_aliases={a: a for a in range(n)},
        scratch_shapes=[pltpu.SemaphoreType.DMA((len(pieces),)), pltpu.SemaphoreType.DMA((len(pieces),))],
    )(*arrays)


LANES = 128
SUBLANES = 8


def _n_rows(shape):
    rows = -(-int(np.prod(shape)) // LANES)
    return -(-rows // SUBLANES) * SUBLANES


def _as_rows(a):
    flat = a.reshape(-1)
    rows = _n_rows(a.shape)
    return jnp.pad(flat, (0, rows * LANES - flat.shape[0])).reshape(rows, LANES)


def _pack(arrays):
    return jnp.concatenate([_as_rows(a) for a in arrays], axis=0)


def _unpack(rows, shapes):
    out, r0 = [], 0
    for s in shapes:
        n = _n_rows(s)
        out.append(rows[r0:r0 + n].reshape(-1)[:int(np.prod(s))].reshape(s))
        r0 += n
    return out


REPLICATED_SMALL = [("rel_bias", (32, 16)), ("even_norm", (1, 1024)), ("even_pool_w", (1, 4, 128, 128)),
                    ("even_pool_scale", (1, 512)), ("odd_q_norm", (1, 64)), ("odd_k_norm", (1, 64)),
                    ("ffn_norm", (2, 1024)), ("ffn_conv_b", (2, 5632))]
SHARDED_SMALL = [("even_conv_w", (1, 3, 128)), ("odd_norm", (1, 256)), ("ffn_conv_w", (2, 3, 1408))]
BIG = ["even_w_in", "even_w_out", "odd_w_qkv", "odd_w_o", "ffn_w_up", "ffn_w_down"]
WEIGHT_ORDER = ["rel_bias", "even_norm", "even_w_in", "even_conv_w", "even_pool_w", "even_pool_scale", "even_w_out",
                "odd_norm", "odd_w_qkv", "odd_q_norm", "odd_k_norm", "odd_w_o", "ffn_norm", "ffn_w_up", "ffn_conv_w",
                "ffn_conv_b", "ffn_w_down"]


def kernel(x, rel_bias, even_norm, even_w_in, even_conv_w, even_pool_w, even_pool_scale, even_w_out, odd_norm, odd_w_qkv, odd_q_norm, odd_k_norm, odd_w_o, ffn_norm, ffn_w_up, ffn_conv_w, ffn_conv_b, ffn_w_down, loss_target, m_rel_bias, m_even_norm, m_even_w_in, m_even_conv_w, m_even_pool_w, m_even_pool_scale, m_even_w_out, m_odd_norm, m_odd_w_qkv, m_odd_q_norm, m_odd_k_norm, m_odd_w_o, m_ffn_norm, m_ffn_w_up, m_ffn_conv_w, m_ffn_conv_b, m_ffn_w_down, v_rel_bias, v_even_norm, v_even_w_in, v_even_conv_w, v_even_pool_w, v_even_pool_scale, v_even_w_out, v_odd_norm, v_odd_w_qkv, v_odd_q_norm, v_odd_k_norm, v_odd_w_o, v_ffn_norm, v_ffn_w_up, v_ffn_conv_w, v_ffn_conv_b, v_ffn_w_down):
    W = dict(rel_bias=rel_bias, even_norm=even_norm, even_w_in=even_w_in, even_conv_w=even_conv_w, even_pool_w=even_pool_w,
             even_pool_scale=even_pool_scale, even_w_out=even_w_out, odd_norm=odd_norm, odd_w_qkv=odd_w_qkv,
             odd_q_norm=odd_q_norm, odd_k_norm=odd_k_norm, odd_w_o=odd_w_o, ffn_norm=ffn_norm, ffn_w_up=ffn_w_up,
             ffn_conv_w=ffn_conv_w, ffn_conv_b=ffn_conv_b, ffn_w_down=ffn_w_down)
    M1 = dict(rel_bias=m_rel_bias, even_norm=m_even_norm, even_w_in=m_even_w_in, even_conv_w=m_even_conv_w,
              even_pool_w=m_even_pool_w, even_pool_scale=m_even_pool_scale, even_w_out=m_even_w_out, odd_norm=m_odd_norm,
              odd_w_qkv=m_odd_w_qkv, odd_q_norm=m_odd_q_norm, odd_k_norm=m_odd_k_norm, odd_w_o=m_odd_w_o,
              ffn_norm=m_ffn_norm, ffn_w_up=m_ffn_w_up, ffn_conv_w=m_ffn_conv_w, ffn_conv_b=m_ffn_conv_b,
              ffn_w_down=m_ffn_w_down)
    M2 = dict(rel_bias=v_rel_bias, even_norm=v_even_norm, even_w_in=v_even_w_in, even_conv_w=v_even_conv_w,
              even_pool_w=v_even_pool_w, even_pool_scale=v_even_pool_scale, even_w_out=v_even_w_out, odd_norm=v_odd_norm,
              odd_w_qkv=v_odd_w_qkv, odd_q_norm=v_odd_q_norm, odd_k_norm=v_odd_k_norm, odd_w_o=v_odd_w_o,
              ffn_norm=v_ffn_norm, ffn_w_up=v_ffn_w_up, ffn_conv_w=v_ffn_conv_w, ffn_conv_b=v_ffn_conv_b,
              ffn_w_down=v_ffn_w_down)
    mx, my, mc = _coords()
    chip = 2 * mx + my
    me = 4 * mx + 2 * my + mc
    place = jnp.stack([chip, mc, me]).astype(jnp.int32)
    xs, target = x[0], loss_target[0]

    def halves(w):
        return w.reshape((w.shape[0], 2, w.shape[-2] // 2, w.shape[-1]))

    lands = [cast_into_slot(halves(even_w_in), 0, place, "cast_w_in"), cast_into_slot(halves(even_w_out), 0, place, "cast_w_out"),
             cast_into_slot(halves(ffn_w_up), 0, place, "cast_w_up0"), cast_into_slot(halves(ffn_w_down), 0, place, "cast_w_down0"),
             cast_into_slot(halves(odd_w_qkv), 0, place, "cast_w_qkv"), cast_into_slot(halves(odd_w_o), 0, place, "cast_w_o"),
             cast_into_slot(halves(ffn_w_up), 1, place, "cast_w_up1"), cast_into_slot(halves(ffn_w_down), 1, place, "cast_w_down1")]
    small_rows = jnp.pad(_pack([even_conv_w, odd_norm, ffn_conv_w]), ((0, SUBLANES), (0, 0)))
    lands.append(cast_into_slot(small_rows.reshape(1, 2, small_rows.shape[0] // 2, LANES), 0, place, "small_into_slot", dtype=F32))
    groups = [[0, 1, 8], [2, 3], [4, 5], [6, 7]]
    gather_sems, lands, token = gather_start(lands, groups, "gather_start")
    even_norm_after_start = even_norm + token[0:1, 0:1]

    def gathered(group, tag, after_landing, after_passing):
        mine = [lands[a] for a in groups[group]]
        sems, arrays = gather_forward(mine, gather_sems[group], after_landing, "gather_forward_" + tag)
        return gather_wait(arrays, sems, after_passing, "gather_wait_" + tag)

    pool_w = cast_bf16(even_pool_w[0], "cast_pool_w")
    gqk = jnp.stack([jnp.tile(odd_q_norm[0], N_HEADS), jnp.tile(odd_k_norm[0], N_HEADS),
                     jnp.ones((D_MODEL,), F32)])[:, None, :]
    bias = bias_expand(rel_bias.T, "bias_expand").reshape(6, N_HEADS, ATT_BLOCK, 2 * ATT_BLOCK)
    xn0 = rmsnorm_fwd(xs, even_norm_after_start, "even_norm")
    got = gathered(0, "even", bias, xn0)
    w_in = got[0].reshape(N_CHIPS, 1, D_MODEL, EVEN_IN // N_CHIPS)
    w_out = got[1].reshape(1, 1, D_MODEL, D_MODEL)
    small = got[2].reshape(N_CHIPS, small_rows.shape[0], LANES)
    conv_w_full = small[:, 0:3].transpose(1, 0, 2).reshape(3, A_WIDTH)
    odd_norm_full = small[:, 8:10].reshape(1, D_MODEL)
    ffn_cw_full = small[:, 16:82].reshape(N_CHIPS, 2, 3, 2 * D_FF // N_CHIPS).transpose(1, 2, 0, 3).reshape(2, 3, 2 * D_FF)

    def ffn_fwd(l, xin, xn):
        up, u, act = up_glu_fwd(xn, w_up[l], ffn_cw_full[l], ffn_conv_b[l:l + 1], f"ffn{l}_up_glu")
        return act, (xin, xn, up, u, act)

    def ffn_weights(got):
        return got[0].reshape(N_CHIPS, 1, D_MODEL, 2 * D_FF // N_CHIPS), got[1].reshape(1, 1, D_FF, D_MODEL)

    w_up, w_down = [None, None], [None, None]
    proj = mm_nn(xn0, w_in, "even_in")
    mix = mixer_fwd(proj, conv_w_full, pool_w, even_pool_scale, "even_mixer")
    x1, xn1 = mm_res_norm(mix, w_out, xs, ffn_norm[0:1], "even_out")
    w_up[0], w_down[0] = ffn_weights(gathered(1, "ffn0", proj, x1))
    act0, ffn0 = ffn_fwd(0, x1, xn1)
    x2, xn2 = mm_res_norm(act0, w_down[0], x1, odd_norm_full, "ffn0_down")
    got = gathered(2, "odd", x1, x2)
    w_qkv = got[0].reshape(N_CHIPS, 1, D_MODEL, 3 * D_MODEL // N_CHIPS)
    w_o = got[1].reshape(1, 1, D_MODEL, D_MODEL)
    qkv = mm_nn(xn2, w_qkv, "odd_qkv")
    qkvn = qknorm_fwd(qkv, gqk, "odd_qknorm")
    att, lse = attn_fwd(qkvn, bias, "attn_fwd")
    x3, xn3 = mm_res_norm(att, w_o, x2, ffn_norm[1:2], "odd_out")
    w_up[1], w_down[1] = ffn_weights(gathered(3, "ffn1", x2, x3))
    act1, ffn1 = ffn_fwd(1, x3, xn3)
    dy, dyb, sq = mm_res_loss(act1, w_down[1], x3, target, "ffn1_down_loss")
    loss = lax.psum(0.5 * jnp.sum(sq) * (1.0 / D_MODEL), ("x", "y", "c"))

    def ffn_bwd(l, dy, dyb, saved):
        xin, xn, up, u, act = saved
        dw_down = mm_tn(act, dyb, f"ffn{l}_dw_down", J=1, tk=D_FF // 2, tm=1024)
        dact = mm_nt(dyb, w_down[l], f"ffn{l}_dact", tr=D_FF // 2, out_dtype=BF16, tm=1024)
        dup, dcw, dcb = glu_bwd(up, u, dact, ffn_cw_full[l], f"ffn{l}_glu_bwd")
        dw_up = mm_tn(xn, dup, f"ffn{l}_dw_up", J=N_CHIPS, tk=512, tm=1024, jb=2)
        dx, dxb, dg = mm_nt_norm_bwd(dup, w_up[l], xin, ffn_norm[l:l + 1], dy, f"ffn{l}_dx")
        return dx, dxb, (dw_down, dw_up, dcw, dcb, dg)

    def quarters(g):
        return g.reshape(N_CHIPS, 2, g.shape[0] * g.shape[1] // (2 * N_CHIPS), g.shape[-1])

    def reduce_start(grads, tag, then):
        sems, parts, zones = reduce_send([quarters(g) for g in grads], "reduce_send_" + tag)
        then, parts = lax.optimization_barrier((then, parts))
        return (sems, parts, zones), then

    dx3, dx3b, g_ffn1 = ffn_bwd(1, dy, dyb, ffn1)
    red_ffn1, (dx3, dx3b) = reduce_start([g_ffn1[1], g_ffn1[0]], "ffn1", (dx3, dx3b))
    dw_o = mm_tn(att, dx3b, "odd_dw_o", J=1, tk=512, tm=1024)
    datt = mm_nt(dx3b, w_o, "odd_datt", tr=D_MODEL, out_dtype=BF16)
    dq, dk, dv, dbias = attn_bwd(qkvn, att, datt, lse, bias, "attn_bwd")
    dqkv, dgqk = qknorm_bwd(qkv, dq, dk, dv, gqk, "odd_qknorm_bwd")
    dw_qkv = mm_tn(xn2, dqkv, "odd_dw_qkv", J=N_CHIPS, tk=512, tm=1024)
    red_odd, dqkv = reduce_start([dw_qkv, dw_o], "odd", dqkv)
    dx2, dx2b, dg_odd = mm_nt_norm_bwd(dqkv, w_qkv, x2, odd_norm_full, dx3, "odd_dx")
    dx1, dx1b, g_ffn0 = ffn_bwd(0, dx2, dx2b, ffn0)
    red_ffn0, (dx1, dx1b) = reduce_start([g_ffn0[1], g_ffn0[0]], "ffn0", (dx1, dx1b))
    dw_out = mm_tn(mix, dx1b, "even_dw_out", J=1, tk=512, tm=1024)
    dmix = mm_nt(dx1b, w_out, "even_dmix", tr=D_MODEL)
    dproj, dcw_even, dpw, dps = mixer_bwd(proj, dmix, conv_w_full, pool_w, even_pool_scale, "even_mixer_bwd")
    dw_in = mm_tn(xn0, dproj, "even_dw_in", J=N_CHIPS, tk=512, tm=1024)
    grad_x, _, dg_even = mm_nt_norm_bwd(dproj, w_in, xs, even_norm, dx1, "even_dx")
    d_rel = jnp.sum(bias_reduce(dbias.reshape(3, N_HEADS, 2 * ATT_BLOCK * ATT_BLOCK), "bias_reduce"), axis=0).T

    red_even, grad_x = reduce_start([dw_in, dw_out], "even", grad_x)

    dcw_sh = dcw_even.reshape(3, N_CHIPS, A_WIDTH // N_CHIPS).transpose(1, 0, 2)
    don_sh = dg_odd.reshape(N_CHIPS, D_MODEL // N_CHIPS)
    dfcw = jnp.stack([g_ffn0[2], g_ffn1[2]])
    dfcw_sh = dfcw.reshape(2, 3, N_CHIPS, 2 * D_FF // N_CHIPS).transpose(2, 0, 1, 3)
    rep_grads = [d_rel, dg_even, dpw[None], dps, _head_sum(dgqk[0]), _head_sum(dgqk[1]),
                 jnp.concatenate([g_ffn0[4], g_ffn1[4]], axis=0), jnp.concatenate([g_ffn0[3], g_ffn1[3]], axis=0)]
    rep_rows = _pack(rep_grads)
    shard_rows = jnp.concatenate([_pack([dcw_sh[j], don_sh[j], dfcw_sh[j]]) for j in range(N_CHIPS)], axis=0)
    n_rep, n_shard = rep_rows.shape[0], shard_rows.shape[0] // N_CHIPS
    small_sems, small_rows, small_land = devices_start(jnp.concatenate([rep_rows, shard_rows], axis=0), "small_grads_start")
    grad_x, small_rows = lax.optimization_barrier((grad_x, small_rows))

    def reduce_end(red, tag, after):
        sems, parts, zones = red
        parts, zones = reduce_wait(parts, zones, sems, after, "reduce_wait_" + tag)
        return zones, parts

    z_ffn1, p_ffn1 = reduce_end(red_ffn1, "ffn1", grad_x)
    z_odd, p_odd = reduce_end(red_odd, "odd", grad_x)
    r_qkv = reduce_sum(z_odd[0], p_odd[0], place, "reduce_sum_w_qkv")
    r_o = reduce_sum(z_odd[1], p_odd[1], place, "reduce_sum_w_o")
    r_up = reduce_sum(z_ffn1[0], p_ffn1[0], place, "reduce_sum_w_up1", layer=1)
    r_down = reduce_sum(z_ffn1[1], p_ffn1[1], place, "reduce_sum_w_down1", layer=1)
    r_qkv, r_o, r_up, r_down = lax.optimization_barrier((r_qkv, r_o, r_up, r_down))
    z_ffn0, p_ffn0 = reduce_end(red_ffn0, "ffn0", r_down)
    r_up = reduce_sum(z_ffn0[0], p_ffn0[0], place, "reduce_sum_w_up0", into=r_up, layer=0)
    r_down = reduce_sum(z_ffn0[1], p_ffn0[1], place, "reduce_sum_w_down0", into=r_down, layer=0)
    later = ["odd_w_qkv", "odd_w_o", "ffn_w_up", "ffn_w_down"]
    joined = join_halves([r_qkv, r_o, r_up, r_down], "grads_join_late_layers")
    G = {nm: g.reshape(W[nm].shape) for nm, g in zip(later, joined)}

    D_, NM, NV = {}, {}, {}

    def update(nm):
        as3 = lambda a: a.reshape((-1,) + a.shape[-2:])
        outs = adamw(as3(W[nm]), as3(G[nm]), as3(M1[nm]), as3(M2[nm]), "adamw_" + nm)
        D_[nm], NM[nm], NV[nm], G[nm] = [o.reshape(W[nm].shape) for o in outs]

    def all_before(names):
        tied = lax.optimization_barrier([D_[nm] for nm in names])
        for nm, d in zip(names, tied):
            D_[nm] = d
        return tied[0]

    for nm in later:
        update(nm)
    z_even, p_even = reduce_end(red_even, "even", all_before(later))
    joined = join_halves([reduce_sum(z_even[0], p_even[0], place, "reduce_sum_w_in"),
                          reduce_sum(z_even[1], p_even[1], place, "reduce_sum_w_out")], "grads_join_first_layer")
    first = ["even_w_in", "even_w_out"]
    for nm, g in zip(first, joined):
        G[nm] = g.reshape(W[nm].shape)
        update(nm)
    small_rows, small_land = devices_wait(small_rows, small_land, small_sems, all_before(first), "small_grads_wait")
    small_sum = device_sum(small_land, small_rows, place[2:3], "small_grads_sum")
    mine = lax.dynamic_slice_in_dim(small_sum, n_rep + chip * n_shard, n_shard, axis=0)
    g_small = jnp.concatenate([small_sum[:n_rep], mine], axis=0)
    small_names = [n for n, _ in REPLICATED_SMALL + SHARDED_SMALL]
    small_shapes = [s for _, s in REPLICATED_SMALL + SHARDED_SMALL]
    G.update(dict(zip(small_names, _unpack(g_small, small_shapes))))
    packs = [_pack([d[n] for n in small_names])[None] for d in (W, M1, M2)]
    outs = adamw(packs[0], g_small[None], packs[1], packs[2], "adamw_small")
    for dst, o in zip((D_, NM, NV), outs[:3]):
        dst.update(dict(zip(small_names, _unpack(o[0], small_shapes))))

    return (loss, grad_x[None], *[G[n] for n in WEIGHT_ORDER], *[D_[n] for n in WEIGHT_ORDER],
            *[NM[n] for n in WEIGHT_ORDER], *[NV[n] for n in WEIGHT_ORDER])


def _head_sum(dg):
    return jnp.sum(dg.reshape(N_HEADS, HEAD_DIM), axis=0, keepdims=True)
```

```python
import functools
import math

import numpy as np
import jax
import jax.numpy as jnp
from jax import lax
from jax.experimental import pallas as pl
from jax.experimental.pallas import tpu as pltpu

F32 = jnp.float32
BF16 = jnp.bfloat16

D_MODEL = 1024
N_HEADS = 16
HEAD_DIM = 64
A_WIDTH = 512
POOL_WINDOWS = (2, 4, 8, 16)
POOL_GROUP = 128
EVEN_IN = 2048
D_FF = 2816
DILATED_PAIRS = ((128, 1), (512, 4), (2048, 16))
ATT_BLOCK = 128
N_REL_BUCKETS = 32
REL_MAX_DISTANCE = 2048
EPS = 1e-6
MASK_VALUE = -1e30
ADAM_LR, ADAM_B1, ADAM_B2, ADAM_EPS, ADAM_WD, ADAM_STEP = 0.001, 0.9, 0.999, 1e-08, 0.01, 10

VMEM_LIMIT_BYTES = 48 * 1024 * 1024
ELEMENTWISE_BLOCK_BYTES = 2 * 1024 * 1024
N_CHIPS = 4
N_DEV = 8
MESH = pl.DeviceIdType.MESH


def _params(*sem):
    return pltpu.CompilerParams(dimension_semantics=sem if sem else None, vmem_limit_bytes=VMEM_LIMIT_BYTES)


def _sds(shape, dtype):
    return jax.ShapeDtypeStruct(tuple(shape), dtype)


def cast_bf16(x, name, tr=None):
    lead, (R, C) = x.shape[:-2], x.shape[-2:]
    n = int(np.prod(lead)) if lead else 1
    x3 = x.reshape((n, R, C))
    tr = tr or R

    def body(x_ref, o_ref):
        o_ref[...] = x_ref[...].astype(BF16)

    out = pl.pallas_call(
        body, name=name, grid=(n, R // tr),
        in_specs=[pl.BlockSpec((None, tr, C), lambda i, r: (i, r, 0))],
        out_specs=pl.BlockSpec((None, tr, C), lambda i, r: (i, r, 0)),
        out_shape=_sds((n, R, C), BF16), compiler_params=_params("parallel", "parallel"),
    )(x3)
    return out.reshape(lead + (R, C))


def rmsnorm_fwd(x, g, name, ts=512):
    S, Dm = x.shape

    def body(x_ref, g_ref, o_ref):
        xv = x_ref[...]
        r = lax.rsqrt(jnp.mean(xv * xv, axis=-1, keepdims=True) + EPS)
        o_ref[...] = ((xv * r) * g_ref[...]).astype(BF16)

    return pl.pallas_call(
        body, name=name, grid=(S // ts,),
        in_specs=[pl.BlockSpec((ts, Dm), lambda i: (i, 0)), pl.BlockSpec((1, Dm), lambda i: (0, 0))],
        out_specs=pl.BlockSpec((ts, Dm), lambda i: (i, 0)),
        out_shape=_sds((S, Dm), BF16), compiler_params=_params("parallel"),
    )(x, g)


def mm_nn(a, w, name, layer=0, res=None, out_dtype=F32, tm=1024):
    M, K = a.shape
    J, _, _, Ns = w.shape

    def body(*refs):
        a_ref, w_ref = refs[0], refs[1]
        o_ref = refs[-1]
        acc = jnp.dot(a_ref[...], w_ref[...], preferred_element_type=F32)
        if res is not None:
            acc = refs[2][...] + acc
        o_ref[...] = acc.astype(o_ref.dtype)

    in_specs = [pl.BlockSpec((tm, K), lambda j, m: (m, 0)),
                pl.BlockSpec((None, None, K, Ns), lambda j, m: (j, layer, 0, 0))]
    args = [a, w]
    if res is not None:
        in_specs.append(pl.BlockSpec((tm, Ns), lambda j, m: (m, j)))
        args.append(res)
    return pl.pallas_call(
        body, name=name, grid=(J, M // tm), in_specs=in_specs,
        out_specs=pl.BlockSpec((tm, Ns), lambda j, m: (m, j)),
        out_shape=_sds((M, J * Ns), out_dtype), compiler_params=_params("parallel", "parallel"),
    )(*args)


def mm_res_norm(a, w, res, gain, name, tm=1024):
    M, K = a.shape
    Dm = w.shape[-1]

    def body(a_ref, w_ref, r_ref, g_ref, y_ref, yn_ref):
        y = r_ref[...] + jnp.dot(a_ref[...], w_ref[...], preferred_element_type=F32)
        y_ref[...] = y
        r = lax.rsqrt(jnp.mean(y * y, axis=-1, keepdims=True) + EPS)
        yn_ref[...] = ((y * r) * g_ref[...]).astype(BF16)

    row = pl.BlockSpec((tm, Dm), lambda m: (m, 0))
    return pl.pallas_call(
        body, name=name, grid=(M // tm,),
        in_specs=[pl.BlockSpec((tm, K), lambda m: (m, 0)),
                  pl.BlockSpec((None, None, K, Dm), lambda m: (0, 0, 0, 0), pipeline_mode=pl.Buffered(1)),
                  row, pl.BlockSpec((1, Dm), lambda m: (0, 0))],
        out_specs=[row, row], out_shape=[_sds((M, Dm), F32), _sds((M, Dm), BF16)],
        compiler_params=_params("parallel"),
    )(a, w, res, gain)


def mm_res_loss(a, w, res, target, name, tm=512):
    M, K = a.shape
    Dm = w.shape[-1]

    def body(a_ref, w_ref, r_ref, t_ref, d_ref, db_ref, s_ref):
        e = (r_ref[...] + jnp.dot(a_ref[...], w_ref[...], preferred_element_type=F32)) - t_ref[...]
        d = e * (1.0 / Dm)
        d_ref[...] = d
        db_ref[...] = d.astype(BF16)
        part = jnp.sum(e * e, axis=0, keepdims=True)

        @pl.when(pl.program_id(0) == 0)
        def _():
            s_ref[...] = part

        @pl.when(pl.program_id(0) > 0)
        def _():
            s_ref[...] += part

    row = pl.BlockSpec((tm, Dm), lambda m: (m, 0))
    return pl.pallas_call(
        body, name=name, grid=(M // tm,),
        in_specs=[pl.BlockSpec((tm, K), lambda m: (m, 0)),
                  pl.BlockSpec((None, None, K, Dm), lambda m: (0, 0, 0, 0), pipeline_mode=pl.Buffered(1)), row, row],
        out_specs=[row, row, pl.BlockSpec((1, Dm), lambda m: (0, 0))],
        out_shape=[_sds((M, Dm), F32), _sds((M, Dm), BF16), _sds((1, Dm), F32)],
        compiler_params=_params("arbitrary"),
    )(a, w, res, target)


def mm_nt(dy, w, name, tr, layer=0, out_dtype=F32, tm=512):
    M = dy.shape[0]
    J, _, R, Ns = w.shape
    dims = (((1,), (1,)), ((), ()))

    def body(dy_ref, w_ref, o_ref):
        acc = None
        for j in range(J):
            p = lax.dot_general(dy_ref[:, j * Ns:(j + 1) * Ns], w_ref[j], dims, preferred_element_type=F32)
            acc = p if acc is None else acc + p
        o_ref[...] = acc.astype(o_ref.dtype)

    return pl.pallas_call(
        body, name=name, grid=(R // tr, M // tm),
        in_specs=[pl.BlockSpec((tm, J * Ns), lambda r, m: (m, 0)),
                  pl.BlockSpec((J, None, tr, Ns), lambda r, m: (0, layer, r, 0))],
        out_specs=pl.BlockSpec((tm, tr), lambda r, m: (m, r)),
        out_shape=_sds((M, R), out_dtype),
        compiler_params=_params("parallel", "parallel"),
    )(dy, w)


def mm_nt_norm_bwd(dy, w, x, g, dres, name, layer=0, tm=512):
    M = dy.shape[0]
    J, _, Dm, Ns = w.shape
    dims = (((1,), (1,)), ((), ()))

    def body(dy_ref, w_ref, x_ref, g_ref, r_ref, dx_ref, dxb_ref, dg_ref):
        dxn = None
        for j in range(J):
            p = lax.dot_general(dy_ref[:, j * Ns:(j + 1) * Ns], w_ref[j], dims, preferred_element_type=F32)
            dxn = p if dxn is None else dxn + p
        xv = x_ref[...]
        r = lax.rsqrt(jnp.mean(xv * xv, axis=-1, keepdims=True) + EPS)
        gx = dxn * g_ref[...]
        dot = jnp.sum(gx * xv, axis=-1, keepdims=True)
        dx = r_ref[...] + r * gx - xv * ((r * r * r) * (dot * (1.0 / Dm)))
        dx_ref[...] = dx
        dxb_ref[...] = dx.astype(BF16)
        part = jnp.sum(dxn * (xv * r), axis=0, keepdims=True)

        @pl.when(pl.program_id(0) == 0)
        def _():
            dg_ref[...] = part

        @pl.when(pl.program_id(0) > 0)
        def _():
            dg_ref[...] += part

    row = pl.BlockSpec((tm, Dm), lambda m: (m, 0))
    vec = pl.BlockSpec((1, Dm), lambda m: (0, 0))
    return pl.pallas_call(
        body, name=name, grid=(M // tm,),
        in_specs=[pl.BlockSpec((tm, J * Ns), lambda m: (m, 0)),
                  pl.BlockSpec((J, None, Dm, Ns), lambda m: (0, layer, 0, 0), pipeline_mode=pl.Buffered(1)), row, vec, row],
        out_specs=[row, row, vec],
        out_shape=[_sds((M, Dm), F32), _sds((M, Dm), BF16), _sds((1, Dm), F32)],
        compiler_params=_params("arbitrary"),
    )(dy, w, x, g, dres)


def mm_tn(a, dy, name, J, tk, tm=512, jb=None):
    M, K = a.shape
    jb = jb or J
    Ns = dy.shape[1] // J
    N = jb * Ns
    n_m = M // tm
    dims = (((0,), (0,)), ((), ()))

    def body(a_ref, dy_ref, o_ref, acc_ref):
        p = lax.dot_general(a_ref[...], dy_ref[...], dims, preferred_element_type=F32)
        m = pl.program_id(2)

        @pl.when(m == 0)
        def _():
            acc_ref[...] = p

        @pl.when(m > 0)
        def _():
            acc_ref[...] += p

        @pl.when(m == n_m - 1)
        def _():
            for j in range(jb):
                o_ref[j] = acc_ref[:, j * Ns:(j + 1) * Ns].astype(BF16)

    return pl.pallas_call(
        body, name=name, grid=(J // jb, K // tk, n_m),
        in_specs=[pl.BlockSpec((tm, tk), lambda g, k, m: (m, k)), pl.BlockSpec((tm, N), lambda g, k, m: (m, g))],
        out_specs=pl.BlockSpec((jb, tk, Ns), lambda g, k, m: (g, k, 0)),
        out_shape=_sds((J, K, Ns), BF16), scratch_shapes=[pltpu.VMEM((tk, N), F32)],
        compiler_params=_params("parallel", "parallel", "arbitrary"),
    )(a, dy)


HALO = 16


def _shift_down(x, s):
    return pltpu.roll(x, s, 0)


def _shift_up(x, s):
    return pltpu.roll(x, x.shape[0] - s, 0)


def _conv3(z, cw):
    return (_shift_down(z, 2) * cw[0:1] + _shift_down(z, 1) * cw[1:2]) + z * cw[2:3]


def _window_count(first_row, n, k):
    t = first_row + lax.broadcasted_iota(jnp.int32, (n, 1), 0)
    return jnp.clip(t + 1, 1, k).astype(F32)


def in_mixer_fwd(xn, w_in, conv_w, pool_w, pool_scale, name, ts=512):
    S, K = xn.shape
    n = ts + HALO

    def body(xm_ref, xb_ref, w_ref, cw_ref, pw_ref, ps_ref, p_ref, o_ref):
        i = pl.program_id(0)
        before = jnp.where(i > 0, xb_ref[...], jnp.zeros_like(xb_ref))
        rows = jnp.concatenate([before, xm_ref[...]], axis=0)
        h, gb, gc, pin = [jnp.dot(rows, w_ref[j], preferred_element_type=F32) for j in range(N_CHIPS)]
        for j, part in enumerate((h, gb, gc, pin)):
            p_ref[:, j * A_WIDTH:(j + 1) * A_WIDTH] = part[HALO:]
        cz = _conv3(gc * h, cw_ref[...])
        o_ref[:, 0:A_WIDTH] = (gb[HALO:] * cz[HALO:]).astype(BF16)
        for g, k in enumerate(POOL_WINDOWS):
            p = pin[:, g * POOL_GROUP:(g + 1) * POOL_GROUP]
            w = p
            s = 1
            while s < k:
                w = w + _shift_down(w, s)
                s *= 2
            pooled = w / _window_count(i * ts - HALO, n, k) - p
            yb = jnp.dot(pooled[HALO:].astype(BF16), pw_ref[g], preferred_element_type=F32)
            yb = yb * ps_ref[:, g * POOL_GROUP:(g + 1) * POOL_GROUP]
            o_ref[:, A_WIDTH + g * POOL_GROUP:A_WIDTH + (g + 1) * POOL_GROUP] = yb.astype(BF16)

    hb = ts // HALO
    return pl.pallas_call(
        body, name=name, grid=(S // ts,),
        in_specs=[
            pl.BlockSpec((ts, K), lambda i: (i, 0)),
            pl.BlockSpec((HALO, K), lambda i: (jnp.maximum(i * hb - 1, 0), 0)),
            pl.BlockSpec((N_CHIPS, None, K, A_WIDTH), lambda i: (0, 0, 0, 0), pipeline_mode=pl.Buffered(1)),
            pl.BlockSpec((3, A_WIDTH), lambda i: (0, 0)),
            pl.BlockSpec((4, POOL_GROUP, POOL_GROUP), lambda i: (0, 0, 0)),
            pl.BlockSpec((1, 4 * POOL_GROUP), lambda i: (0, 0)),
        ],
        out_specs=[pl.BlockSpec((ts, EVEN_IN), lambda i: (i, 0)), pl.BlockSpec((ts, D_MODEL), lambda i: (i, 0))],
        out_shape=[_sds((S, EVEN_IN), F32), _sds((S, D_MODEL), BF16)], compiler_params=_params("parallel"),
    )(xn, xn, w_in, conv_w, pool_w, pool_scale)


def mixer_bwd(proj, dmix, conv_w, pool_w, pool_scale, name, ts=256):
    S = proj.shape[0]
    n = ts + 2 * HALO
    nt = S // ts
    tn_dims = (((0,), (0,)), ((), ()))
    nt_dims = (((1,), (1,)), ((), ()))

    def body(pm_ref, pb_ref, pa_ref, dm_ref, da_ref, cw_ref, pw_ref, ps_ref, o_ref, dcw_ref, dpw_ref, dps_ref):
        i = pl.program_id(0)
        last = i == nt - 1
        before = jnp.where(i > 0, pb_ref[...], 0.0)
        after = jnp.where(last, 0.0, pa_ref[...])
        ext = jnp.concatenate([before, pm_ref[...], after], axis=0)
        dafter = jnp.where(last, 0.0, da_ref[...])
        dext = jnp.concatenate([jnp.zeros((HALO, D_MODEL), F32), dm_ref[...], dafter], axis=0)
        cw = cw_ref[...]
        main = slice(HALO, HALO + ts)

        @pl.when(i == 0)
        def _():
            dcw_ref[...] = jnp.zeros_like(dcw_ref)
            dpw_ref[...] = jnp.zeros_like(dpw_ref)
            dps_ref[...] = jnp.zeros_like(dps_ref)

        h, gb, gc = ext[:, 0:A_WIDTH], ext[:, A_WIDTH:2 * A_WIDTH], ext[:, 2 * A_WIDTH:3 * A_WIDTH]
        z = gc * h
        z1, z2 = _shift_down(z, 1), _shift_down(z, 2)
        cz = (z2 * cw[0:1] + z1 * cw[1:2]) + z * cw[2:3]
        dya = dext[:, 0:A_WIDTH]
        dcz = dya * gb
        dz = dcz * cw[2:3] + _shift_up(dcz, 1) * cw[1:2] + _shift_up(dcz, 2) * cw[0:1]
        o_ref[:, 0:A_WIDTH] = (dz * gc)[main].astype(BF16)
        o_ref[:, A_WIDTH:2 * A_WIDTH] = (dya * cz)[main].astype(BF16)
        o_ref[:, 2 * A_WIDTH:3 * A_WIDTH] = (dz * h)[main].astype(BF16)
        dczm = dcz[main]
        dcw_ref[0:1, :] += jnp.sum(dczm * z2[main], axis=0, keepdims=True)
        dcw_ref[1:2, :] += jnp.sum(dczm * z1[main], axis=0, keepdims=True)
        dcw_ref[2:3, :] += jnp.sum(dczm * z[main], axis=0, keepdims=True)

        for g, k in enumerate(POOL_WINDOWS):
            lo = 3 * A_WIDTH + g * POOL_GROUP
            cols = slice(g * POOL_GROUP, (g + 1) * POOL_GROUP)
            p = ext[:, lo:lo + POOL_GROUP]
            w = p
            s = 1
            while s < k:
                w = w + _shift_down(w, s)
                s *= 2
            cnt = _window_count(i * ts - HALO, n, k)
            pooled = (w / cnt - p)[main].astype(BF16)
            dyb = dext[:, A_WIDTH + g * POOL_GROUP:A_WIDTH + (g + 1) * POOL_GROUP]
            e = dyb * ps_ref[:, cols]
            pre = jnp.dot(pooled, pw_ref[g], preferred_element_type=F32)
            dps_ref[:, cols] += jnp.sum(dyb[main] * pre, axis=0, keepdims=True)
            dpw_ref[g] += lax.dot_general(pooled, e[main].astype(BF16), tn_dims, preferred_element_type=F32)
            dpooled = lax.dot_general(e.astype(BF16), pw_ref[g], nt_dims, preferred_element_type=F32)
            q = dpooled / cnt
            a = q
            s = 1
            while s < k:
                a = a + _shift_up(a, s)
                s *= 2
            o_ref[:, lo:lo + POOL_GROUP] = (a - dpooled)[main].astype(BF16)

    hb = ts // HALO
    nh = S // HALO
    before_map = lambda i: (jnp.maximum(i * hb - 1, 0), 0)
    after_map = lambda i: (jnp.minimum((i + 1) * hb, nh - 1), 0)
    full = lambda *shape: pl.BlockSpec(shape, lambda i: (0,) * len(shape))
    return pl.pallas_call(
        body, name=name, grid=(nt,),
        in_specs=[
            pl.BlockSpec((ts, EVEN_IN), lambda i: (i, 0)),
            pl.BlockSpec((HALO, EVEN_IN), before_map),
            pl.BlockSpec((HALO, EVEN_IN), after_map),
            pl.BlockSpec((ts, D_MODEL), lambda i: (i, 0)),
            pl.BlockSpec((HALO, D_MODEL), after_map),
            full(3, A_WIDTH), full(4, POOL_GROUP, POOL_GROUP), full(1, 4 * POOL_GROUP),
        ],
        out_specs=[pl.BlockSpec((ts, EVEN_IN), lambda i: (i, 0)), full(3, A_WIDTH), full(4, POOL_GROUP, POOL_GROUP),
                   full(1, 4 * POOL_GROUP)],
        out_shape=[_sds((S, EVEN_IN), BF16), _sds((3, A_WIDTH), F32), _sds((4, POOL_GROUP, POOL_GROUP), F32),
                   _sds((1, 4 * POOL_GROUP), F32)],
        compiler_params=_params("arbitrary"),
    )(proj, proj, proj, dmix, dmix, conv_w, pool_w, pool_scale)


FFN_HALO = 16
FFN_TC = 1408


GLU_CHUNKS = ((0, 512), (512, 512), (1024, 384))


def up_glu_fwd(xn, w_up, conv_w, conv_b, name, tm=512):
    S, K = xn.shape
    nc = D_FF // FFN_TC

    def body(xm_ref, xb_ref, wg_ref, wu_ref, cwg_ref, cwu_ref, cbg_ref, cbu_ref, pg_ref, pu_ref, ug_ref, uu_ref, o_ref):
        before = jnp.where(pl.program_id(1) > 0, xb_ref[...], jnp.zeros_like(xb_ref))
        rows = jnp.concatenate([before, xm_ref[...]], axis=0)
        for lo, width in GLU_CHUNKS:
            cols = slice(lo, lo + width)
            pre_g = jnp.dot(rows, wg_ref[:, cols], preferred_element_type=F32)
            pre_u = jnp.dot(rows, wu_ref[:, cols], preferred_element_type=F32)
            gate = _conv3(pre_g, cwg_ref[:, cols])[FFN_HALO:] + cbg_ref[:, cols]
            upv = _conv3(pre_u, cwu_ref[:, cols])[FFN_HALO:] + cbu_ref[:, cols]
            pg_ref[:, cols] = pre_g[FFN_HALO:].astype(BF16)
            pu_ref[:, cols] = pre_u[FFN_HALO:].astype(BF16)
            ug_ref[:, cols] = gate.astype(BF16)
            uu_ref[:, cols] = upv.astype(BF16)
            o_ref[:, cols] = ((gate * (1.0 / (1.0 + jnp.exp(-gate)))) * upv).astype(BF16)

    hb = tm // FFN_HALO
    wspec = lambda off: pl.BlockSpec((None, None, K, FFN_TC), lambda j, m: (j + off, 0, 0, 0))
    cw = lambda off: pl.BlockSpec((3, FFN_TC), lambda j, m: (0, j + off))
    cb = lambda off: pl.BlockSpec((1, FFN_TC), lambda j, m: (0, j + off))
    out = pl.BlockSpec((tm, FFN_TC), lambda j, m: (m, j))
    pg, pu, ug, uu, act = pl.pallas_call(
        body, name=name, grid=(nc, S // tm),
        in_specs=[pl.BlockSpec((tm, K), lambda j, m: (m, 0)),
                  pl.BlockSpec((FFN_HALO, K), lambda j, m: (jnp.maximum(m * hb - 1, 0), 0)),
                  wspec(0), wspec(nc), cw(0), cw(nc), cb(0), cb(nc)],
        out_specs=[out] * 5, out_shape=[_sds((S, D_FF), BF16)] * 5,
        compiler_params=_params("parallel", "parallel"),
    )(xn, xn, w_up, w_up, conv_w, conv_w, conv_b, conv_b)
    return (pg, pu), (ug, uu), act


def glu_bwd(up, u, da, conv_w, name, ts=256):
    S = up[0].shape[0]
    nc = D_FF // FFN_TC
    nt = S // ts
    W = 2 * D_FF

    def body(xg_ref, xu_ref, gm_ref, ga_ref, um_ref, ua_ref, dm_ref, da_ref, cw_ref, dx_ref, dcw_ref, dcb_ref):
        i = pl.program_id(0)
        last = i == nt - 1

        @pl.when(i == 0)
        def _():
            dcw_ref[...] = jnp.zeros_like(dcw_ref)
            dcb_ref[...] = jnp.zeros_like(dcb_ref)

        def rows(m_ref, a_ref, cols):
            return jnp.concatenate([m_ref[:, cols], a_ref[:, cols]], axis=0).astype(F32)

        def back(d, x, cols):
            cw = cw_ref[:, cols]
            d1, d2 = _shift_up(d, 1), _shift_up(d, 2)
            dx_ref[:, cols] = ((d * cw[2:3] + d1 * cw[1:2]) + d2 * cw[0:1])[:ts].astype(BF16)
            dcb_ref[:, cols] += jnp.sum(d[:ts], axis=0, keepdims=True)
            dcw_ref[0:1, cols] += jnp.sum(d2[:ts] * x, axis=0, keepdims=True)
            dcw_ref[1:2, cols] += jnp.sum(d1[:ts] * x, axis=0, keepdims=True)
            dcw_ref[2:3, cols] += jnp.sum(d[:ts] * x, axis=0, keepdims=True)

        for c in range(nc):
            cols = slice(c * FFN_TC, (c + 1) * FFN_TC)
            ug, uu = rows(gm_ref, ga_ref, cols), rows(um_ref, ua_ref, cols)
            dae = rows(dm_ref, da_ref, cols)
            dae = jnp.where(last & (lax.broadcasted_iota(jnp.int32, dae.shape, 0) >= ts), 0.0, dae)
            sg = 1.0 / (1.0 + jnp.exp(-ug))
            duu = dae * (ug * sg)
            dug = (dae * uu) * (sg * (1.0 + ug * (1.0 - sg)))
            back(dug, xg_ref[:, cols].astype(F32), cols)
            back(duu, xu_ref[:, cols].astype(F32), slice(D_FF + c * FFN_TC, D_FF + (c + 1) * FFN_TC))

    hb = ts // FFN_HALO
    nh = S // FFN_HALO
    after_map = lambda i: (jnp.minimum((i + 1) * hb, nh - 1), 0)
    main = pl.BlockSpec((ts, D_FF), lambda i: (i, 0))
    after = pl.BlockSpec((FFN_HALO, D_FF), after_map)
    return pl.pallas_call(
        body, name=name, grid=(nt,),
        in_specs=[main, main, main, after, main, after, main, after, pl.BlockSpec((3, W), lambda i: (0, 0))],
        out_specs=[pl.BlockSpec((ts, W), lambda i: (i, 0)), pl.BlockSpec((3, W), lambda i: (0, 0)),
                   pl.BlockSpec((1, W), lambda i: (0, 0))],
        out_shape=[_sds((S, W), BF16), _sds((3, W), F32), _sds((1, W), F32)],
        compiler_params=_params("arbitrary"),
    )(up[0], up[1], u[0], u[0], u[1], u[1], da, da, conv_w)


MEAN_GROUP = 256


def _head_mean_matrix():
    h = np.arange(MEAN_GROUP) // HEAD_DIM
    return jnp.asarray((h[:, None] == h[None, :]).astype(np.float32) / HEAD_DIM, dtype=BF16)


def _head_mean(v, gm):
    vb = v.astype(BF16)
    return jnp.concatenate([jnp.dot(vb[:, c:c + MEAN_GROUP], gm, preferred_element_type=F32)
                            for c in range(0, v.shape[1], MEAN_GROUP)], axis=1)


def qknorm_fwd(qkv, gqk, name, ts=512):
    S = qkv.shape[0]

    def body(x_ref, g_ref, gm_ref, o_ref):
        part = pl.program_id(0)
        x = x_ref[...]

        @pl.when(part < 2)
        def _():
            r = lax.rsqrt(_head_mean(x * x, gm_ref[...]) + EPS)
            o_ref[...] = ((x * r) * g_ref[...]).astype(BF16)

        @pl.when(part == 2)
        def _():
            o_ref[...] = x.astype(BF16)

    return pl.pallas_call(
        body, name=name, grid=(3, S // ts),
        in_specs=[pl.BlockSpec((ts, D_MODEL), lambda p, i: (i, p)), pl.BlockSpec((None, 1, D_MODEL), lambda p, i: (p, 0, 0)),
                  pl.BlockSpec((MEAN_GROUP, MEAN_GROUP), lambda p, i: (0, 0))],
        out_specs=pl.BlockSpec((ts, D_MODEL), lambda p, i: (i, p)),
        out_shape=_sds((S, 3 * D_MODEL), BF16), compiler_params=_params("parallel", "parallel"),
    )(qkv, gqk, _head_mean_matrix())


def qknorm_bwd(qkv, dq, dk, dv, gqk, name, ts=256):
    S = qkv.shape[0]

    def body(x_ref, dq_ref, dk_ref, dv_ref, g_ref, gm_ref, o_ref, dg_ref):
        @pl.when(pl.program_id(0) == 0)
        def _():
            dg_ref[...] = jnp.zeros_like(dg_ref)

        gm = gm_ref[...]
        for part, d_ref in enumerate((dq_ref, dk_ref)):
            cols = slice(part * D_MODEL, (part + 1) * D_MODEL)
            x = x_ref[:, cols]
            d = d_ref[...]
            r = lax.rsqrt(_head_mean(x * x, gm) + EPS)
            gx = d * g_ref[part]
            o_ref[:, cols] = (r * gx - x * ((r * r * r) * _head_mean(gx * x, gm))).astype(BF16)
            dg_ref[part] += jnp.sum(d * (x * r), axis=0, keepdims=True)
        o_ref[:, 2 * D_MODEL:] = dv_ref[...].astype(BF16)

    row = pl.BlockSpec((ts, D_MODEL), lambda i: (i, 0))
    wide = pl.BlockSpec((ts, 3 * D_MODEL), lambda i: (i, 0))
    gains = pl.BlockSpec((3, 1, D_MODEL), lambda i: (0, 0, 0))
    return pl.pallas_call(
        body, name=name, grid=(S // ts,),
        in_specs=[wide, row, row, row, gains, pl.BlockSpec((MEAN_GROUP, MEAN_GROUP), lambda i: (0, 0))],
        out_specs=[wide, gains],
        out_shape=[_sds((S, 3 * D_MODEL), BF16), _sds((3, 1, D_MODEL), F32)],
        compiler_params=_params("arbitrary"),
    )(qkv, dq, dk, dv, gqk, _head_mean_matrix())


RESIDUES = 16


def _block_order(dil):
    runs = RESIDUES // dil
    slot = np.arange(ATT_BLOCK)
    return (slot % (ATT_BLOCK // runs)) * runs + slot // (ATT_BLOCK // runs)


def _bucket_tables():
    n = ATT_BLOCK
    max_exact = N_REL_BUCKETS // 2
    buckets, valids = [], []
    for _, dil in DILATED_PAIRS:
        order = _block_order(dil)
        a = order[:, None]
        c = np.concatenate([order, n + order])[None, :]
        first_half = (np.arange(2 * n) < n)[None, :]
        rel = a + n - c
        band = (rel >= 0) & (rel <= n)
        dist = np.clip(rel, 0, n) * dil
        dd = np.maximum(dist, 1).astype(np.float32)
        large = max_exact + (np.log(dd / np.float32(max_exact)) / np.float32(math.log(REL_MAX_DISTANCE / max_exact))
                             * np.float32(N_REL_BUCKETS - max_exact)).astype(np.int32)
        large = np.minimum(large, N_REL_BUCKETS - 1)
        buckets.append(np.where(dist < max_exact, dist, large).reshape(1, -1))
        valids.append(np.stack([(band & ~first_half).reshape(1, -1), band.reshape(1, -1)]))
    return np.stack(buckets).astype(np.int32), np.stack(valids).astype(np.int32)


BIAS_CHUNK = 8192


def _split3(x):
    a = x.astype(BF16)
    r = x - a.astype(F32)
    b = r.astype(BF16)
    c = (r - b.astype(F32)).astype(BF16)
    return a, b, c


def bias_expand(rel_bias_t, name):
    bucket, valid = _bucket_tables()
    nq = bucket.shape[-1]

    def body(t_ref, b_ref, v_ref, o_ref):
        onehot = (lax.broadcasted_iota(jnp.int32, (N_REL_BUCKETS, BIAS_CHUNK), 0) == b_ref[...]).astype(BF16)
        acc = None
        for term in _split3(t_ref[...]):
            p = jnp.dot(term, onehot, preferred_element_type=F32)
            acc = p if acc is None else acc + p
        o_ref[...] = jnp.where(v_ref[...] > 0, acc, MASK_VALUE)

    return pl.pallas_call(
        body, name=name, grid=(3, 2, nq // BIAS_CHUNK),
        in_specs=[pl.BlockSpec((N_HEADS, N_REL_BUCKETS), lambda b, v, c: (0, 0)),
                  pl.BlockSpec((None, 1, BIAS_CHUNK), lambda b, v, c: (b, 0, c)),
                  pl.BlockSpec((None, None, 1, BIAS_CHUNK), lambda b, v, c: (b, v, 0, c))],
        out_specs=pl.BlockSpec((None, None, N_HEADS, BIAS_CHUNK), lambda b, v, c: (b, v, 0, c)),
        out_shape=_sds((3, 2, N_HEADS, nq), F32), compiler_params=_params("parallel", "parallel", "parallel"),
    )(rel_bias_t, jnp.asarray(bucket), jnp.asarray(valid))


def bias_reduce(dbias, name):
    bucket, _ = _bucket_tables()
    nq = bucket.shape[-1]
    dims = (((1,), (1,)), ((), ()))

    def body(d_ref, b_ref, o_ref):
        onehot = (lax.broadcasted_iota(jnp.int32, (N_REL_BUCKETS, BIAS_CHUNK), 0) == b_ref[...]).astype(BF16)
        acc = None
        for term in _split3(d_ref[...]):
            p = lax.dot_general(term, onehot, dims, preferred_element_type=F32)
            acc = p if acc is None else acc + p

        @pl.when(pl.program_id(1) == 0)
        def _():
            o_ref[...] = acc

        @pl.when(pl.program_id(1) > 0)
        def _():
            o_ref[...] += acc

    return pl.pallas_call(
        body, name=name, grid=(3, nq // BIAS_CHUNK),
        in_specs=[pl.BlockSpec((None, N_HEADS, BIAS_CHUNK), lambda b, c: (b, 0, c)),
                  pl.BlockSpec((None, 1, BIAS_CHUNK), lambda b, c: (b, 0, c))],
        out_specs=pl.BlockSpec((None, N_HEADS, N_REL_BUCKETS), lambda b, c: (b, 0, 0)),
        out_shape=_sds((3, N_HEADS, N_REL_BUCKETS), F32), compiler_params=_params("parallel", "arbitrary"),
    )(dbias, jnp.asarray(bucket))


PAIR = 2 * HEAD_DIM
N_PAIRS = N_HEADS // 2
_NT = (((1,), (1,)), ((), ()))
_TN = (((0,), (0,)), ((), ()))


def _low_lanes(shape):
    return lax.broadcasted_iota(jnp.int32, shape, 1) < HEAD_DIM


ATTN_VMEM_LIMIT_BYTES = 56 * 1024 * 1024
BRANCH_ORDER = (2, 1, 0)


def _regroup(dst, src, L16):
    for r in range(RESIDUES):
        dst[pl.ds(r * L16, L16), :] = src[pl.ds(r, L16, stride=RESIDUES), :]


def _ungroup(dst, src, L16):
    for r in range(RESIDUES):
        dst[pl.ds(r, L16, stride=RESIDUES), :] = src[pl.ds(r * L16, L16), :]


def _branch_geometry(branch, S):
    dil = DILATED_PAIRS[branch][1]
    runs = RESIDUES // dil
    return dil, runs, ATT_BLOCK // runs, S // dil // ATT_BLOCK


def _block_rows(it, branch, S):
    dil, runs, run_len, n_blocks = _branch_geometry(branch, S)
    L16 = S // RESIDUES
    r, b = it // n_blocks, it % n_blocks
    prev = jnp.maximum(b - 1, 0)
    cur_rows = [pl.multiple_of((j * dil + r) * L16 + run_len * b, 8) for j in range(runs)]
    prev_rows = [pl.multiple_of((j * dil + r) * L16 + run_len * prev, 8) for j in range(runs)]
    return cur_rows, prev_rows, jnp.minimum(b, 1)


def _load_block(ref, rows, run_len):
    parts = [ref[pl.ds(o, run_len), :] for o in rows]
    return parts[0] if len(parts) == 1 else jnp.concatenate(parts, axis=0)


def _store_block(ref, rows, run_len, value, add=False):
    for j, o in enumerate(rows):
        part = value[j * run_len:(j + 1) * run_len]
        if add:
            ref[pl.ds(o, run_len), :] += part
        else:
            ref[pl.ds(o, run_len), :] = part


ATTN_FWD_UNROLL = 8
ATTN_BWD_UNROLL = 4


def _stack_heads(x, low):
    zero = jnp.zeros_like(x)
    return jnp.concatenate([jnp.where(low, x, zero), jnp.where(low, zero, x)], axis=0)


def _unstack_heads(y, low):
    return jnp.where(low, y[:ATT_BLOCK], y[ATT_BLOCK:])


def attn_fwd(qkvn, bias, name):
    S = qkvn.shape[0]
    L16 = S // RESIDUES
    n_iter = S // ATT_BLOCK

    def body(q_ref, k_ref, v_ref, b_ref, o_ref, lse_ref, stage, qp, kp, vp, acc_s, m_s, l_s):
        for src, dst in ((q_ref, qp), (k_ref, kp), (v_ref, vp)):
            stage[...] = src[...].astype(F32)
            _regroup(dst, stage, L16)
        low = _low_lanes((ATT_BLOCK, PAIR))

        for branch in BRANCH_ORDER:
            _, _, run_len, _ = _branch_geometry(branch, S)
            first = branch == BRANCH_ORDER[0]

            def step(it, carry, branch=branch, run_len=run_len, first=first):
                cur, prev, variant = _block_rows(it, branch, S)
                q = _load_block(qp, cur, run_len).astype(BF16)
                k = jnp.concatenate([_load_block(kp, prev, run_len), _load_block(kp, cur, run_len)], axis=0).astype(BF16)
                v = jnp.concatenate([_load_block(vp, prev, run_len), _load_block(vp, cur, run_len)], axis=0).astype(BF16)
                s = lax.dot_general(_stack_heads(q, low), k, _NT, preferred_element_type=F32) * (HEAD_DIM ** -0.5)
                s = s + b_ref[2 * branch + variant].reshape(2 * ATT_BLOCK, 2 * ATT_BLOCK)
                mx = jnp.max(s, axis=-1, keepdims=True)
                p = jnp.exp(s - mx)
                den = jnp.sum(p, axis=-1, keepdims=True)
                pv = jnp.dot(p.astype(BF16), v, preferred_element_type=F32)
                acc = _unstack_heads(pv, low)
                m = _unstack_heads(mx, low)
                l = _unstack_heads(den, low)
                if not first:
                    m_old = _load_block(m_s, cur, run_len)
                    m_new = jnp.maximum(m_old, m)
                    a_old, a_new = jnp.exp(m_old - m_new), jnp.exp(m - m_new)
                    acc = _load_block(acc_s, cur, run_len) * a_old + acc * a_new
                    l = _load_block(l_s, cur, run_len) * a_old + l * a_new
                    m = m_new
                _store_block(acc_s, cur, run_len, acc)
                _store_block(m_s, cur, run_len, m)
                _store_block(l_s, cur, run_len, l)
                return carry

            lax.fori_loop(0, n_iter, step, 0, unroll=ATTN_FWD_UNROLL)

        acc_s[...] = acc_s[...] / l_s[...]
        _ungroup(stage, acc_s, L16)
        o_ref[...] = stage[...].astype(BF16)
        m_s[...] = m_s[...] + jnp.log(l_s[...])
        _ungroup(lse_ref, m_s, L16)

    col = lambda part: pl.BlockSpec((S, PAIR), lambda hp: (0, part * N_PAIRS + hp))
    out = pl.BlockSpec((S, PAIR), lambda hp: (0, hp))
    return pl.pallas_call(
        body, name=name, grid=(N_PAIRS,),
        in_specs=[col(0), col(1), col(2), pl.BlockSpec((6, 2, ATT_BLOCK, 2 * ATT_BLOCK), lambda hp: (0, hp, 0, 0))],
        out_specs=[out, out], out_shape=[_sds((S, D_MODEL), BF16), _sds((S, D_MODEL), F32)],
        scratch_shapes=[pltpu.VMEM((S, PAIR), F32)] * 7,
        compiler_params=pltpu.CompilerParams(dimension_semantics=("parallel",), vmem_limit_bytes=ATTN_VMEM_LIMIT_BYTES),
    )(qkvn, qkvn, qkvn, bias)


def attn_bwd(qkvn, att, datt, lse, bias, name):
    S = qkvn.shape[0]
    L16 = S // RESIDUES
    n_iter = S // ATT_BLOCK
    TILE = 512

    def body(q_ref, k_ref, v_ref, o_ref, do_ref, lse_ref, b_ref, dq_ref, dk_ref, dv_ref, db_ref,
             qp, kp, vp, dop, ldp, dqp, dkp, dvp):
        stage = dqp
        for src, dst in ((q_ref, qp), (k_ref, kp), (v_ref, vp), (do_ref, dop)):
            stage[...] = src[...].astype(F32)
            _regroup(dst, stage, L16)

        def pack(i, carry):
            rows = pl.ds(pl.multiple_of(i * TILE, TILE), TILE)
            low = _low_lanes((TILE, PAIR))
            lane = lax.broadcasted_iota(jnp.int32, (TILE, PAIR), 1)
            prod = do_ref[rows, :].astype(F32) * o_ref[rows, :].astype(F32)
            d0 = jnp.sum(jnp.where(low, prod, 0.0), axis=-1, keepdims=True)
            d1 = jnp.sum(jnp.where(low, 0.0, prod), axis=-1, keepdims=True)
            stage[rows, :] = jnp.where((lane & (HEAD_DIM // 2)) == 0, lse_ref[rows, :], jnp.where(low, d0, d1))
            return carry

        lax.fori_loop(0, S // TILE, pack, 0)
        _regroup(ldp, stage, L16)
        dqp[...] = jnp.zeros_like(dqp)
        dkp[...] = jnp.zeros_like(dkp)
        dvp[...] = jnp.zeros_like(dvp)
        db_ref[...] = jnp.zeros_like(db_ref)
        low = _low_lanes((ATT_BLOCK, PAIR))

        for branch in BRANCH_ORDER:
            _, _, run_len, _ = _branch_geometry(branch, S)

            def step(it, carry, branch=branch, run_len=run_len):
                cur, prev, variant = _block_rows(it, branch, S)
                q = _load_block(qp, cur, run_len).astype(BF16)
                dout = _load_block(dop, cur, run_len).astype(BF16)
                ld = _load_block(ldp, cur, run_len)
                k = jnp.concatenate([_load_block(kp, prev, run_len), _load_block(kp, cur, run_len)], axis=0).astype(BF16)
                v = jnp.concatenate([_load_block(vp, prev, run_len), _load_block(vp, cur, run_len)], axis=0).astype(BF16)
                half = HEAD_DIM // 2
                lse2 = jnp.concatenate([ld[:, 0:1], ld[:, HEAD_DIM:HEAD_DIM + 1]], axis=0)
                delta2 = jnp.concatenate([ld[:, half:half + 1], ld[:, HEAD_DIM + half:HEAD_DIM + half + 1]], axis=0)
                q2, do2 = _stack_heads(q, low), _stack_heads(dout, low)
                s = lax.dot_general(q2, k, _NT, preferred_element_type=F32) * (HEAD_DIM ** -0.5)
                p = jnp.exp(s + b_ref[2 * branch + variant].reshape(2 * ATT_BLOCK, 2 * ATT_BLOCK) - lse2)
                dp = lax.dot_general(do2, v, _NT, preferred_element_type=F32)
                ds = p * (dp - delta2)
                db_ref[branch] += ds.reshape(2, ATT_BLOCK, 2 * ATT_BLOCK)
                dsb = (ds * (HEAD_DIM ** -0.5)).astype(BF16)
                dq = _unstack_heads(jnp.dot(dsb, k, preferred_element_type=F32), low)
                dk = lax.dot_general(dsb, q2, _TN, preferred_element_type=F32)
                dv = lax.dot_general(p.astype(BF16), do2, _TN, preferred_element_type=F32)
                _store_block(dqp, cur, run_len, dq, add=True)
                _store_block(dkp, prev, run_len, dk[:ATT_BLOCK], add=True)
                _store_block(dvp, prev, run_len, dv[:ATT_BLOCK], add=True)
                _store_block(dkp, cur, run_len, dk[ATT_BLOCK:], add=True)
                _store_block(dvp, cur, run_len, dv[ATT_BLOCK:], add=True)
                return carry

            lax.fori_loop(0, n_iter, step, 0, unroll=ATTN_BWD_UNROLL)

        _ungroup(dq_ref, dqp, L16)
        _ungroup(dk_ref, dkp, L16)
        _ungroup(dv_ref, dvp, L16)

    col = lambda part: pl.BlockSpec((S, PAIR), lambda hp: (0, part * N_PAIRS + hp))
    one = pl.BlockSpec((S, PAIR), lambda hp: (0, hp))
    return pl.pallas_call(
        body, name=name, grid=(N_PAIRS,),
        in_specs=[col(0), col(1), col(2), one, one, one,
                  pl.BlockSpec((6, 2, ATT_BLOCK, 2 * ATT_BLOCK), lambda hp: (0, hp, 0, 0))],
        out_specs=[one, one, one, pl.BlockSpec((3, 2, ATT_BLOCK, 2 * ATT_BLOCK), lambda hp: (0, hp, 0, 0))],
        out_shape=[_sds((S, D_MODEL), F32)] * 3 + [_sds((3, N_HEADS, ATT_BLOCK, 2 * ATT_BLOCK), F32)],
        scratch_shapes=[pltpu.VMEM((S, PAIR), F32)] * 8,
        compiler_params=pltpu.CompilerParams(dimension_semantics=("parallel",), vmem_limit_bytes=ATTN_VMEM_LIMIT_BYTES),
    )(qkvn, qkvn, qkvn, att, datt, lse, bias)


def adamw(w, g, m, v, name):
    n, R, C = w.shape

    def body(w_ref, g_ref, m_ref, v_ref, d_ref, nm_ref, nv_ref, go_ref):
        gv = g_ref[...]
        go_ref[...] = gv
        m2 = ADAM_B1 * m_ref[...] + (1.0 - ADAM_B1) * gv
        v2 = ADAM_B2 * v_ref[...] + (1.0 - ADAM_B2) * (gv * gv)
        m_hat = m2 / (1.0 - ADAM_B1 ** ADAM_STEP)
        v_hat = v2 / (1.0 - ADAM_B2 ** ADAM_STEP)
        d_ref[...] = -ADAM_LR * (m_hat / (jnp.sqrt(v_hat) + ADAM_EPS) + ADAM_WD * w_ref[...])
        nm_ref[...] = m2
        nv_ref[...] = v2

    tr = R
    while tr * C * 4 > ELEMENTWISE_BLOCK_BYTES and tr % 16 == 0:
        tr //= 2
    spec = pl.BlockSpec((None, tr, C), lambda i, r: (i, r, 0))
    return pl.pallas_call(
        body, name=name, grid=(n, R // tr), in_specs=[spec] * 4, out_specs=[spec] * 4,
        out_shape=[_sds((n, R, C), F32)] * 4, compiler_params=_params("parallel", "parallel"),
    )(w, g, m, v)


ANY = pl.BlockSpec(memory_space=pl.ANY)


def _coords():
    return lax.axis_index("x"), lax.axis_index("y"), lax.axis_index("c")


def _other_chips(mx, my):
    return [(1 - mx, my), (mx, 1 - my), (1 - mx, 1 - my)]


def _remote(src, dst, send, recv, dev):
    return pltpu.make_async_remote_copy(src_ref=src, dst_ref=dst, send_sem=send, recv_sem=recv, device_id=dev,
                                        device_id_type=MESH)


HBM =pl.BlockSpec(memory_space=pltpu.HBM)
SEM = pl.BlockSpec(memory_space=pltpu.SEMAPHORE)
_SPLIT_COPY = pltpu.CompilerParams(has_side_effects=pltpu.SideEffectType.DATAFLOW_SIDE_EFFECTING)


def _in_hbm(a):
    return pltpu.with_memory_space_constraint(a, pltpu.HBM)


def cast_into_slot(w, layer, chip_core, name, dtype=BF16):
    _, _, hR, C = w.shape

    def body(s_ref, w_ref, o_ref):
        del s_ref
        o_ref[...] = w_ref[...].astype(dtype)

    grid_spec = pltpu.PrefetchScalarGridSpec(
        num_scalar_prefetch=1, grid=(2,),
        in_specs=[pl.BlockSpec((None, None, hR, C), lambda h, s: (layer, h, 0, 0))],
        out_specs=pl.BlockSpec((None, None, hR, C), lambda h, s: (s[0], h, 0, 0)))
    return pl.pallas_call(body, name=name, grid_spec=grid_spec, out_shape=_sds((N_CHIPS, 2, hR, C), dtype),
                          compiler_params=_params("parallel"))(chip_core, w)


def gather_start(lands, groups, name):
    n = len(lands)
    n_groups = len(groups)

    def body(*refs):
        ins = refs[:n]
        sems = refs[n:n + 2 * n_groups]
        token = refs[-1]
        mx, my, mc = _coords()
        chip = 2 * mx + my
        for g, members in enumerate(groups):
            send, recv = sems[2 * g], sems[2 * g + 1]
            for i, a in enumerate(members):
                mine = ins[a].at[chip, mc]
                for k, (px, py) in enumerate(_other_chips(mx, my)):
                    _remote(mine, mine, send.at[3 * i + k], recv.at[3 * i + k], (px, py, mc)).start()
        token[...] = jnp.zeros_like(token)

    sem_shapes = []
    for members in groups:
        sem_shapes += [pltpu.SemaphoreType.DMA((3 * len(members),))] * 2
    outs = pl.pallas_call(
        body, name=name, in_specs=[HBM] * n,
        out_specs=[SEM] * (2 * n_groups) + [HBM] * n + [pl.BlockSpec(memory_space=pltpu.VMEM)],
        out_shape=sem_shapes + [pltpu.HBM(a.shape, a.dtype) for a in lands] + [_sds((SUBLANES, LANES), F32)],
        input_output_aliases={a: 2 * n_groups + a for a in range(n)}, compiler_params=_SPLIT_COPY,
    )(*[_in_hbm(a) for a in lands])
    sems = [(outs[2 * g], outs[2 * g + 1]) for g in range(n_groups)]
    return sems, list(outs[2 * n_groups:2 * n_groups + n]), outs[-1]


def gather_forward(lands, sems, after, name):
    n = len(lands)

    def body(*refs):
        ins = refs[:n]
        send, recv = refs[n], refs[n + 1]
        fsend, frecv = refs[n + 3], refs[n + 4]
        mx, my, mc = _coords()
        for i in range(n):
            for k, (px, py) in enumerate(_other_chips(mx, my)):
                landed = ins[i].at[2 * px + py, mc]
                cp = _remote(landed, landed, send.at[3 * i + k], recv.at[3 * i + k], (px, py, mc))
                cp.wait_send()
                cp.wait_recv()
                _remote(landed, landed, fsend.at[3 * i + k], frecv.at[3 * i + k], (mx, my, 1 - mc)).start()

    outs = pl.pallas_call(
        body, name=name, in_specs=[HBM] * n + [SEM, SEM, ANY], out_specs=[SEM, SEM] + [HBM] * n,
        out_shape=[pltpu.SemaphoreType.DMA((3 * n,))] * 2 + [pltpu.HBM(a.shape, a.dtype) for a in lands],
        input_output_aliases={a: 2 + a for a in range(n)}, compiler_params=_SPLIT_COPY,
    )(*lands, sems[0], sems[1], after)
    return (outs[0], outs[1]), list(outs[2:])


def gather_wait(lands, sems, after, name):
    n = len(lands)

    def body(*refs):
        ins = refs[:n]
        fsend, frecv = refs[n], refs[n + 1]
        mx, my, mc = _coords()
        for i in range(n):
            for k, (px, py) in enumerate(_other_chips(mx, my)):
                theirs = ins[i].at[2 * px + py, 1 - mc]
                cp = _remote(theirs, theirs, fsend.at[3 * i + k], frecv.at[3 * i + k], (mx, my, 1 - mc))
                cp.wait_send()
                cp.wait_recv()

    outs = pl.pallas_call(
        body, name=name, in_specs=[HBM] * n + [SEM, SEM, ANY], out_specs=[HBM] * n,
        out_shape=[pltpu.HBM(a.shape, a.dtype) for a in lands],
        input_output_aliases={a: a for a in range(n)}, compiler_params=_SPLIT_COPY,
    )(*lands, sems[0], sems[1], after)
    return list(outs)


def _peers(mx, my, mc):
    return [(1 - mx if k & 4 else mx, 1 - my if k & 2 else my, 1 - mc if k & 1 else mc) for k in range(1, N_DEV)]


def devices_start(x, name):
    def body(x_ref, land_ref, send, recv, x_thru, land_thru):
        mx, my, mc = _coords()
        me = 4 * mx + 2 * my + mc
        for k, peer in enumerate(_peers(mx, my, mc)):
            _remote(x_ref, land_ref.at[me], send.at[k], recv.at[k], peer).start()

    land = lax.empty((N_DEV,) + x.shape, x.dtype)
    outs = pl.pallas_call(
        body, name=name, in_specs=[HBM, HBM], out_specs=[SEM, SEM, HBM, HBM],
        out_shape=[pltpu.SemaphoreType.DMA((N_DEV - 1,))] * 2 + [pltpu.HBM(x.shape, x.dtype), pltpu.HBM(land.shape, x.dtype)],
        input_output_aliases={0: 2, 1: 3}, compiler_params=_SPLIT_COPY,
    )(_in_hbm(x), _in_hbm(land))
    return (outs[0], outs[1]), outs[2], outs[3]


def devices_wait(x, land, sems, after, name):
    def body(x_ref, land_ref, send, recv, after_ref, x_thru, land_thru):
        mx, my, mc = _coords()
        for k, (px, py, pc) in enumerate(_peers(mx, my, mc)):
            cp = _remote(x_ref, land_ref.at[4 * px + 2 * py + pc], send.at[k], recv.at[k], (px, py, pc))
            cp.wait_send()
            cp.wait_recv()

    outs = pl.pallas_call(
        body, name=name, in_specs=[HBM, HBM, SEM, SEM, ANY], out_specs=[HBM, HBM],
        out_shape=[pltpu.HBM(x.shape, x.dtype), pltpu.HBM(land.shape, land.dtype)],
        input_output_aliases={0: 0, 1: 1}, compiler_params=_SPLIT_COPY,
    )(x, land, sems[0], sems[1], after)
    return outs[0], outs[1]


def device_sum(land, own, me, name):
    _, R, C = land.shape

    def body(s_ref, l_ref, o_ref_in, o_ref):
        acc = None
        for q in range(N_DEV):
            term = jnp.where(s_ref[0] == q, o_ref_in[...], l_ref[q])
            acc = term if acc is None else acc + term
        o_ref[...] = acc

    grid_spec = pltpu.PrefetchScalarGridSpec(
        num_scalar_prefetch=1, grid=(1,),
        in_specs=[pl.BlockSpec((N_DEV, R, C), lambda i, s: (0, 0, 0)), pl.BlockSpec((R, C), lambda i, s: (0, 0))],
        out_specs=pl.BlockSpec((R, C), lambda i, s: (0, 0)))
    return pl.pallas_call(body, name=name, grid_spec=grid_spec, out_shape=_sds((R, C), F32),
                          compiler_params=_params("arbitrary"))(me, land, own)


def reduce_send(grads, name):
    n = len(grads)

    def body(*refs):
        ins, lands = refs[:n], refs[n:2 * n]
        send, recv = refs[2 * n], refs[2 * n + 1]
        mx, my, mc = _coords()
        me = 4 * mx + 2 * my + mc
        for a in range(n):
            for k, (px, py, pc) in enumerate(_peers(mx, my, mc)):
                _remote(ins[a].at[2 * px + py, pc], lands[a].at[me], send.at[7 * a + k], recv.at[7 * a + k], (px, py, pc)).start()

    lands = [lax.empty((N_DEV,) + g.shape[2:], g.dtype) for g in grads]
    outs = pl.pallas_call(
        body, name=name, in_specs=[HBM] * (2 * n), out_specs=[SEM, SEM] + [HBM] * (2 * n),
        out_shape=[pltpu.SemaphoreType.DMA((7 * n,))] * 2 + [pltpu.HBM(a.shape, a.dtype) for a in grads + lands],
        input_output_aliases={a: 2 + a for a in range(2 * n)}, compiler_params=_SPLIT_COPY,
    )(*[_in_hbm(a) for a in grads + lands])
    return (outs[0], outs[1]), list(outs[2:2 + n]), list(outs[2 + n:])


def reduce_wait(grads, lands, sems, after, name):
    n = len(grads)

    def body(*refs):
        ins, zones = refs[:n], refs[n:2 * n]
        send, recv = refs[2 * n], refs[2 * n + 1]
        mx, my, mc = _coords()
        for a in range(n):
            for k, (px, py, pc) in enumerate(_peers(mx, my, mc)):
                cp = _remote(ins[a].at[2 * px + py, pc], zones[a].at[4 * px + 2 * py + pc], send.at[7 * a + k],
                             recv.at[7 * a + k], (px, py, pc))
                cp.wait_send()
                cp.wait_recv()

    outs = pl.pallas_call(
        body, name=name, in_specs=[HBM] * (2 * n) + [SEM, SEM, ANY], out_specs=[HBM] * (2 * n),
        out_shape=[pltpu.HBM(a.shape, a.dtype) for a in grads + lands],
        input_output_aliases={a: a for a in range(2 * n)}, compiler_params=_SPLIT_COPY,
    )(*grads, *lands, sems[0], sems[1], after)
    return list(outs[:n]), list(outs[n:])


def reduce_sum(land, grad, place, name, into=None, layer=None):
    _, hR, C = land.shape
    tr = hR
    while N_DEV * tr * C * 2 > 3 * ELEMENTWISE_BLOCK_BYTES and tr % 32 == 0:
        tr //= 2

    def body(s_ref, l_ref, g_ref, *rest):
        o_ref = rest[-1]
        own = g_ref[...].astype(F32)
        acc = None
        for q in range(N_DEV):
            term = jnp.where(s_ref[2] == q, own, l_ref[q].astype(F32))
            acc = term if acc is None else acc + term
        o_ref[...] = acc

    in_specs = [pl.BlockSpec((N_DEV, tr, C), lambda i, s: (0, i, 0)),
                pl.BlockSpec((None, None, tr, C), lambda i, s: (s[0], s[1], i, 0))]
    args = [place, land, grad]
    aliases = {}
    if layer is None:
        out_spec = pl.BlockSpec((None, tr, C), lambda i, s: (s[1], i, 0))
        out_shape = _sds((2, hR, C), F32)
    else:
        out_spec = pl.BlockSpec((None, None, tr, C), lambda i, s: (layer, s[1], i, 0))
        out_shape = _sds((2, 2, hR, C), F32)
        if into is not None:
            in_specs.append(ANY)
            args.append(into)
            aliases = {3: 0}
    grid_spec = pltpu.PrefetchScalarGridSpec(num_scalar_prefetch=1, grid=(hR // tr,), in_specs=in_specs, out_specs=out_spec)
    return pl.pallas_call(body, name=name, grid_spec=grid_spec, out_shape=out_shape, input_output_aliases=aliases,
                          compiler_params=_params("arbitrary"))(*args)


def join_halves(arrays, name):
    n = len(arrays)
    pieces = [(a, l) for a, arr in enumerate(arrays) for l in (range(arr.shape[0]) if arr.ndim == 4 else [None])]

    def body(*refs):
        ins = refs[:n]
        send, recv = refs[2 * n:]
        mx, my, mc = _coords()

        def half(a, l, h):
            return ins[a].at[h] if l is None else ins[a].at[l, h]

        sends = [_remote(half(a, l, mc), half(a, l, mc), send.at[i], recv.at[i], (mx, my, 1 - mc))
                 for i, (a, l) in enumerate(pieces)]
        for cp in sends:
            cp.start()
        for i, (a, l) in enumerate(pieces):
            theirs = half(a, l, 1 - mc)
            _remote(theirs, theirs, send.at[i], recv.at[i], (mx, my, 1 - mc)).wait_recv()
        for cp in sends:
            cp.wait_send()

    return pl.pallas_call(
        body, name=name, in_specs=[ANY] * n, out_specs=[ANY] * n, out_shape=[_sds(a.shape, a.dtype) for a in arrays],
        input_output_aliases={a: a for a in range(n)},
        scratch_shapes=[pltpu.SemaphoreType.DMA((len(pieces),)), pltpu.SemaphoreType.DMA((len(pieces),))],
    )(*arrays)


LANES = 128
SUBLANES = 8


def _n_rows(shape):
    rows = -(-int(np.prod(shape)) // LANES)
    return -(-rows // SUBLANES) * SUBLANES


def _as_rows(a):
    flat = a.reshape(-1)
    rows = _n_rows(a.shape)
    return jnp.pad(flat, (0, rows * LANES - flat.shape[0])).reshape(rows, LANES)


def _pack(arrays):
    return jnp.concatenate([_as_rows(a) for a in arrays], axis=0)


def _unpack(rows, shapes):
    out, r0 = [], 0
    for s in shapes:
        n = _n_rows(s)
        out.append(rows[r0:r0 + n].reshape(-1)[:int(np.prod(s))].reshape(s))
        r0 += n
    return out


REPLICATED_SMALL = [("rel_bias", (32, 16)), ("even_norm", (1, 1024)), ("even_pool_w", (1, 4, 128, 128)),
                    ("even_pool_scale", (1, 512)), ("odd_q_norm", (1, 64)), ("odd_k_norm", (1, 64)),
                    ("ffn_norm", (2, 1024)), ("ffn_conv_b", (2, 5632))]
SHARDED_SMALL = [("even_conv_w", (1, 3, 128)), ("odd_norm", (1, 256)), ("ffn_conv_w", (2, 3, 1408))]
BIG = ["even_w_in", "even_w_out", "odd_w_qkv", "odd_w_o", "ffn_w_up", "ffn_w_down"]
WEIGHT_ORDER = ["rel_bias", "even_norm", "even_w_in", "even_conv_w", "even_pool_w", "even_pool_scale", "even_w_out",
                "odd_norm", "odd_w_qkv", "odd_q_norm", "odd_k_norm", "odd_w_o", "ffn_norm", "ffn_w_up", "ffn_conv_w",
                "ffn_conv_b", "ffn_w_down"]


def kernel(x, rel_bias, even_norm, even_w_in, even_conv_w, even_pool_w, even_pool_scale, even_w_out, odd_norm, odd_w_qkv, odd_q_norm, odd_k_norm, odd_w_o, ffn_norm, ffn_w_up, ffn_conv_w, ffn_conv_b, ffn_w_down, loss_target, m_rel_bias, m_even_norm, m_even_w_in, m_even_conv_w, m_even_pool_w, m_even_pool_scale, m_even_w_out, m_odd_norm, m_odd_w_qkv, m_odd_q_norm, m_odd_k_norm, m_odd_w_o, m_ffn_norm, m_ffn_w_up, m_ffn_conv_w, m_ffn_conv_b, m_ffn_w_down, v_rel_bias, v_even_norm, v_even_w_in, v_even_conv_w, v_even_pool_w, v_even_pool_scale, v_even_w_out, v_odd_norm, v_odd_w_qkv, v_odd_q_norm, v_odd_k_norm, v_odd_w_o, v_ffn_norm, v_ffn_w_up, v_ffn_conv_w, v_ffn_conv_b, v_ffn_w_down):
    W = dict(rel_bias=rel_bias, even_norm=even_norm, even_w_in=even_w_in, even_conv_w=even_conv_w, even_pool_w=even_pool_w,
             even_pool_scale=even_pool_scale, even_w_out=even_w_out, odd_norm=odd_norm, odd_w_qkv=odd_w_qkv,
             odd_q_norm=odd_q_norm, odd_k_norm=odd_k_norm, odd_w_o=odd_w_o, ffn_norm=ffn_norm, ffn_w_up=ffn_w_up,
             ffn_conv_w=ffn_conv_w, ffn_conv_b=ffn_conv_b, ffn_w_down=ffn_w_down)
    M1 = dict(rel_bias=m_rel_bias, even_norm=m_even_norm, even_w_in=m_even_w_in, even_conv_w=m_even_conv_w,
              even_pool_w=m_even_pool_w, even_pool_scale=m_even_pool_scale, even_w_out=m_even_w_out, odd_norm=m_odd_norm,
              odd_w_qkv=m_odd_w_qkv, odd_q_norm=m_odd_q_norm, odd_k_norm=m_odd_k_norm, odd_w_o=m_odd_w_o,
              ffn_norm=m_ffn_norm, ffn_w_up=m_ffn_w_up, ffn_conv_w=m_ffn_conv_w, ffn_conv_b=m_ffn_conv_b,
              ffn_w_down=m_ffn_w_down)
    M2 = dict(rel_bias=v_rel_bias, even_norm=v_even_norm, even_w_in=v_even_w_in, even_conv_w=v_even_conv_w,
              even_pool_w=v_even_pool_w, even_pool_scale=v_even_pool_scale, even_w_out=v_even_w_out, odd_norm=v_odd_norm,
              odd_w_qkv=v_odd_w_qkv, odd_q_norm=v_odd_q_norm, odd_k_norm=v_odd_k_norm, odd_w_o=v_odd_w_o,
              ffn_norm=v_ffn_norm, ffn_w_up=v_ffn_w_up, ffn_conv_w=v_ffn_conv_w, ffn_conv_b=v_ffn_conv_b,
              ffn_w_down=v_ffn_w_down)
    mx, my, mc = _coords()
    chip = 2 * mx + my
    me = 4 * mx + 2 * my + mc
    place = jnp.stack([chip, mc, me]).astype(jnp.int32)
    xs, target = x[0], loss_target[0]

    def halves(w):
        return w.reshape((w.shape[0], 2, w.shape[-2] // 2, w.shape[-1]))

    lands = [cast_into_slot(halves(even_w_in), 0, place, "cast_w_in"), cast_into_slot(halves(even_w_out), 0, place, "cast_w_out"),
             cast_into_slot(halves(ffn_w_up), 0, place, "cast_w_up0"), cast_into_slot(halves(ffn_w_down), 0, place, "cast_w_down0"),
             cast_into_slot(halves(odd_w_qkv), 0, place, "cast_w_qkv"), cast_into_slot(halves(odd_w_o), 0, place, "cast_w_o"),
             cast_into_slot(halves(ffn_w_up), 1, place, "cast_w_up1"), cast_into_slot(halves(ffn_w_down), 1, place, "cast_w_down1")]
    small_rows = jnp.pad(_pack([even_conv_w, odd_norm, ffn_conv_w]), ((0, SUBLANES), (0, 0)))
    lands.append(cast_into_slot(small_rows.reshape(1, 2, small_rows.shape[0] // 2, LANES), 0, place, "small_into_slot", dtype=F32))
    groups = [[0, 1, 8], [2, 3], [4, 5], [6, 7]]
    gather_sems, lands, token = gather_start(lands, groups, "gather_start")
    even_norm_after_start = even_norm + token[0:1, 0:1]

    def gathered(group, tag, after_landing, after_passing):
        mine = [lands[a] for a in groups[group]]
        sems, arrays = gather_forward(mine, gather_sems[group], after_landing, "gather_forward_" + tag)
        return gather_wait(arrays, sems, after_passing, "gather_wait_" + tag)

    pool_w = cast_bf16(even_pool_w[0], "cast_pool_w")
    gqk = jnp.stack([jnp.tile(odd_q_norm[0], N_HEADS), jnp.tile(odd_k_norm[0], N_HEADS),
                     jnp.ones((D_MODEL,), F32)])[:, None, :]
    bias = bias_expand(rel_bias.T, "bias_expand").reshape(6, N_HEADS, ATT_BLOCK, 2 * ATT_BLOCK)
    xn0 = rmsnorm_fwd(xs, even_norm_after_start, "even_norm")
    got = gathered(0, "even", bias, xn0)
    w_in = got[0].reshape(N_CHIPS, 1, D_MODEL, EVEN_IN // N_CHIPS)
    w_out = got[1].reshape(1, 1, D_MODEL, D_MODEL)
    small = got[2].reshape(N_CHIPS, small_rows.shape[0], LANES)
    conv_w_full = small[:, 0:3].transpose(1, 0, 2).reshape(3, A_WIDTH)
    odd_norm_full = small[:, 8:10].reshape(1, D_MODEL)
    ffn_cw_full = small[:, 16:82].reshape(N_CHIPS, 2, 3, 2 * D_FF // N_CHIPS).transpose(1, 2, 0, 3).reshape(2, 3, 2 * D_FF)

    def ffn_fwd(l, xin, xn):
        up, u, act = up_glu_fwd(xn, w_up[l], ffn_cw_full[l], ffn_conv_b[l:l + 1], f"ffn{l}_up_glu")
        return act, (xin, xn, up, u, act)

    def ffn_weights(got):
        return got[0].reshape(N_CHIPS, 1, D_MODEL, 2 * D_FF // N_CHIPS), got[1].reshape(1, 1, D_FF, D_MODEL)

    w_up, w_down = [None, None], [None, None]
    proj, mix = in_mixer_fwd(xn0, w_in, conv_w_full, pool_w, even_pool_scale, "even_in_mixer")
    x1, xn1 = mm_res_norm(mix, w_out, xs, ffn_norm[0:1], "even_out")
    w_up[0], w_down[0] = ffn_weights(gathered(1, "ffn0", proj, x1))
    act0, ffn0 = ffn_fwd(0, x1, xn1)
    x2, xn2 = mm_res_norm(act0, w_down[0], x1, odd_norm_full, "ffn0_down")
    got = gathered(2, "odd", x1, x2)
    w_qkv = got[0].reshape(N_CHIPS, 1, D_MODEL, 3 * D_MODEL // N_CHIPS)
    w_o = got[1].reshape(1, 1, D_MODEL, D_MODEL)
    qkv = mm_nn(xn2, w_qkv, "odd_qkv")
    qkvn = qknorm_fwd(qkv, gqk, "odd_qknorm")
    att, lse = attn_fwd(qkvn, bias, "attn_fwd")
    x3, xn3 = mm_res_norm(att, w_o, x2, ffn_norm[1:2], "odd_out")
    w_up[1], w_down[1] = ffn_weights(gathered(3, "ffn1", x2, x3))
    act1, ffn1 = ffn_fwd(1, x3, xn3)
    dy, dyb, sq = mm_res_loss(act1, w_down[1], x3, target, "ffn1_down_loss")
    loss = lax.psum(0.5 * jnp.sum(sq) * (1.0 / D_MODEL), ("x", "y", "c"))

    def ffn_bwd(l, dy, dyb, saved):
        xin, xn, up, u, act = saved
        dw_down = mm_tn(act, dyb, f"ffn{l}_dw_down", J=1, tk=D_FF // 2, tm=1024)
        dact = mm_nt(dyb, w_down[l], f"ffn{l}_dact", tr=D_FF // 2, out_dtype=BF16, tm=1024)
        dup, dcw, dcb = glu_bwd(up, u, dact, ffn_cw_full[l], f"ffn{l}_glu_bwd")
        dw_up = mm_tn(xn, dup, f"ffn{l}_dw_up", J=N_CHIPS, tk=512, tm=1024, jb=2)
        dx, dxb, dg = mm_nt_norm_bwd(dup, w_up[l], xin, ffn_norm[l:l + 1], dy, f"ffn{l}_dx")
        return dx, dxb, (dw_down, dw_up, dcw, dcb, dg)

    def quarters(g):
        return g.reshape(N_CHIPS, 2, g.shape[0] * g.shape[1] // (2 * N_CHIPS), g.shape[-1])

    def reduce_start(grads, tag, then):
        sems, parts, zones = reduce_send([quarters(g) for g in grads], "reduce_send_" + tag)
        then, parts = lax.optimization_barrier((then, parts))
        return (sems, parts, zones), then

    dx3, dx3b, g_ffn1 = ffn_bwd(1, dy, dyb, ffn1)
    red_ffn1, (dx3, dx3b) = reduce_start([g_ffn1[1], g_ffn1[0]], "ffn1", (dx3, dx3b))
    dw_o = mm_tn(att, dx3b, "odd_dw_o", J=1, tk=512, tm=1024)
    datt = mm_nt(dx3b, w_o, "odd_datt", tr=D_MODEL, out_dtype=BF16)
    dq, dk, dv, dbias = attn_bwd(qkvn, att, datt, lse, bias, "attn_bwd")
    dqkv, dgqk = qknorm_bwd(qkv, dq, dk, dv, gqk, "odd_qknorm_bwd")
    dw_qkv = mm_tn(xn2, dqkv, "odd_dw_qkv", J=N_CHIPS, tk=512, tm=1024)
    red_odd, dqkv = reduce_start([dw_qkv, dw_o], "odd", dqkv)
    dx2, dx2b, dg_odd = mm_nt_norm_bwd(dqkv, w_qkv, x2, odd_norm_full, dx3, "odd_dx")
    dx1, dx1b, g_ffn0 = ffn_bwd(0, dx2, dx2b, ffn0)
    red_ffn0, (dx1, dx1b) = reduce_start([g_ffn0[1], g_ffn0[0]], "ffn0", (dx1, dx1b))
    dw_out = mm_tn(mix, dx1b, "even_dw_out", J=1, tk=512, tm=1024)
    dmix = mm_nt(dx1b, w_out, "even_dmix", tr=D_MODEL)
    dproj, dcw_even, dpw, dps = mixer_bwd(proj, dmix, conv_w_full, pool_w, even_pool_scale, "even_mixer_bwd")
    dw_in = mm_tn(xn0, dproj, "even_dw_in", J=N_CHIPS, tk=512, tm=1024)
    grad_x, _, dg_even = mm_nt_norm_bwd(dproj, w_in, xs, even_norm, dx1, "even_dx")
    d_rel = jnp.sum(bias_reduce(dbias.reshape(3, N_HEADS, 2 * ATT_BLOCK * ATT_BLOCK), "bias_reduce"), axis=0).T

    red_even, grad_x = reduce_start([dw_in, dw_out], "even", grad_x)

    dcw_sh = dcw_even.reshape(3, N_CHIPS, A_WIDTH // N_CHIPS).transpose(1, 0, 2)
    don_sh = dg_odd.reshape(N_CHIPS, D_MODEL // N_CHIPS)
    dfcw = jnp.stack([g_ffn0[2], g_ffn1[2]])
    dfcw_sh = dfcw.reshape(2, 3, N_CHIPS, 2 * D_FF // N_CHIPS).transpose(2, 0, 1, 3)
    rep_grads = [d_rel, dg_even, dpw[None], dps, _head_sum(dgqk[0]), _head_sum(dgqk[1]),
                 jnp.concatenate([g_ffn0[4], g_ffn1[4]], axis=0), jnp.concatenate([g_ffn0[3], g_ffn1[3]], axis=0)]
    rep_rows = _pack(rep_grads)
    shard_rows = jnp.concatenate([_pack([dcw_sh[j], don_sh[j], dfcw_sh[j]]) for j in range(N_CHIPS)], axis=0)
    n_rep, n_shard = rep_rows.shape[0], shard_rows.shape[0] // N_CHIPS
    small_sems, small_rows, small_land = devices_start(jnp.concatenate([rep_rows, shard_rows], axis=0), "small_grads_start")
    grad_x, small_rows = lax.optimization_barrier((grad_x, small_rows))

    def reduce_end(red, tag, after):
        sems, parts, zones = red
        parts, zones = reduce_wait(parts, zones, sems, after, "reduce_wait_" + tag)
        return zones, parts

    z_ffn1, p_ffn1 = reduce_end(red_ffn1, "ffn1", grad_x)
    z_odd, p_odd = reduce_end(red_odd, "odd", grad_x)
    r_qkv = reduce_sum(z_odd[0], p_odd[0], place, "reduce_sum_w_qkv")
    r_o = reduce_sum(z_odd[1], p_odd[1], place, "reduce_sum_w_o")
    r_up = reduce_sum(z_ffn1[0], p_ffn1[0], place, "reduce_sum_w_up1", layer=1)
    r_down = reduce_sum(z_ffn1[1], p_ffn1[1], place, "reduce_sum_w_down1", layer=1)
    r_qkv, r_o, r_up, r_down = lax.optimization_barrier((r_qkv, r_o, r_up, r_down))
    z_ffn0, p_ffn0 = reduce_end(red_ffn0, "ffn0", r_down)
    r_up = reduce_sum(z_ffn0[0], p_ffn0[0], place, "reduce_sum_w_up0", into=r_up, layer=0)
    r_down = reduce_sum(z_ffn0[1], p_ffn0[1], place, "reduce_sum_w_down0", into=r_down, layer=0)
    later = ["odd_w_qkv", "odd_w_o", "ffn_w_up", "ffn_w_down"]
    joined = join_halves([r_qkv, r_o, r_up, r_down], "grads_join_late_layers")
    G = {nm: g.reshape(W[nm].shape) for nm, g in zip(later, joined)}

    D_, NM, NV = {}, {}, {}

    def update(nm):
        as3 = lambda a: a.reshape((-1,) + a.shape[-2:])
        outs = adamw(as3(W[nm]), as3(G[nm]), as3(M1[nm]), as3(M2[nm]), "adamw_" + nm)
        D_[nm], NM[nm], NV[nm], G[nm] = [o.reshape(W[nm].shape) for o in outs]

    def all_before(names):
        tied = lax.optimization_barrier([D_[nm] for nm in names])
        for nm, d in zip(names, tied):
            D_[nm] = d
        return tied[0]

    for nm in later:
        update(nm)
    z_even, p_even = reduce_end(red_even, "even", all_before(later))
    joined = join_halves([reduce_sum(z_even[0], p_even[0], place, "reduce_sum_w_in"),
                          reduce_sum(z_even[1], p_even[1], place, "reduce_sum_w_out")], "grads_join_first_layer")
    first = ["even_w_in", "even_w_out"]
    for nm, g in zip(first, joined):
        G[nm] = g.reshape(W[nm].shape)
        update(nm)
    small_rows, small_land = devices_wait(small_rows, small_land, small_sems, all_before(first), "small_grads_wait")
    small_sum = device_sum(small_land, small_rows, place[2:3], "small_grads_sum")
    mine = lax.dynamic_slice_in_dim(small_sum, n_rep + chip * n_shard, n_shard, axis=0)
    g_small = jnp.concatenate([small_sum[:n_rep], mine], axis=0)
    small_names = [n for n, _ in REPLICATED_SMALL + SHARDED_SMALL]
    small_shapes = [s for _, s in REPLICATED_SMALL + SHARDED_SMALL]
    G.update(dict(zip(small_names, _unpack(g_small, small_shapes))))
    packs = [_pack([d[n] for n in small_names])[None] for d in (W, M1, M2)]
    outs = adamw(packs[0], g_small[None], packs[1], packs[2], "adamw_small")
    for dst, o in zip((D_, NM, NV), outs[:3]):
        dst.update(dict(zip(small_names, _unpack(o[0], small_shapes))))

    return (loss, grad_x[None], *[G[n] for n in WEIGHT_ORDER], *[D_[n] for n in WEIGHT_ORDER],
            *[NM[n] for n in WEIGHT_ORDER], *[NV[n] for n in WEIGHT_ORDER])


def _head_sum(dg):
    return jnp.sum(dg.reshape(N_HEADS, HEAD_DIM), axis=0, keepdims=True)
```

```python
import functools
import math

import numpy as np
import jax
import jax.numpy as jnp
from jax import lax
from jax.experimental import pallas as pl
from jax.experimental.pallas import tpu as pltpu

F32 = jnp.float32
BF16 = jnp.bfloat16

D_MODEL = 1024
N_HEADS = 16
HEAD_DIM = 64
A_WIDTH = 512
POOL_WINDOWS = (2, 4, 8, 16)
POOL_GROUP = 128
EVEN_IN = 2048
D_FF = 2816
DILATED_PAIRS = ((128, 1), (512, 4), (2048, 16))
ATT_BLOCK = 128
N_REL_BUCKETS = 32
REL_MAX_DISTANCE = 2048
EPS = 1e-6
MASK_VALUE = -1e30
ADAM_LR, ADAM_B1, ADAM_B2, ADAM_EPS, ADAM_WD, ADAM_STEP = 0.001, 0.9, 0.999, 1e-08, 0.01, 10

VMEM_LIMIT_BYTES = 48 * 1024 * 1024
ELEMENTWISE_BLOCK_BYTES = 2 * 1024 * 1024
N_CHIPS = 4
N_DEV = 8
MESH = pl.DeviceIdType.MESH


def _params(*sem):
    return pltpu.CompilerParams(dimension_semantics=sem if sem else None, vmem_limit_bytes=VMEM_LIMIT_BYTES)


def _sds(shape, dtype):
    return jax.ShapeDtypeStruct(tuple(shape), dtype)


def cast_bf16(x, name, tr=None):
    lead, (R, C) = x.shape[:-2], x.shape[-2:]
    n = int(np.prod(lead)) if lead else 1
    x3 = x.reshape((n, R, C))
    tr = tr or R

    def body(x_ref, o_ref):
        o_ref[...] = x_ref[...].astype(BF16)

    out = pl.pallas_call(
        body, name=name, grid=(n, R // tr),
        in_specs=[pl.BlockSpec((None, tr, C), lambda i, r: (i, r, 0))],
        out_specs=pl.BlockSpec((None, tr, C), lambda i, r: (i, r, 0)),
        out_shape=_sds((n, R, C), BF16), compiler_params=_params("parallel", "parallel"),
    )(x3)
    return out.reshape(lead + (R, C))


def rmsnorm_fwd(x, g, name, ts=512):
    S, Dm = x.shape

    def body(x_ref, g_ref, o_ref):
        xv = x_ref[...]
        r = lax.rsqrt(jnp.mean(xv * xv, axis=-1, keepdims=True) + EPS)
        o_ref[...] = ((xv * r) * g_ref[...]).astype(BF16)

    return pl.pallas_call(
        body, name=name, grid=(S // ts,),
        in_specs=[pl.BlockSpec((ts, Dm), lambda i: (i, 0)), pl.BlockSpec((1, Dm), lambda i: (0, 0))],
        out_specs=pl.BlockSpec((ts, Dm), lambda i: (i, 0)),
        out_shape=_sds((S, Dm), BF16), compiler_params=_params("parallel"),
    )(x, g)


def mm_nn(a, w, name, layer=0, res=None, out_dtype=F32, tm=1024):
    M, K = a.shape
    J, _, _, Ns = w.shape

    def body(*refs):
        a_ref, w_ref = refs[0], refs[1]
        o_ref = refs[-1]
        acc = jnp.dot(a_ref[...], w_ref[...], preferred_element_type=F32)
        if res is not None:
            acc = refs[2][...] + acc
        o_ref[...] = acc.astype(o_ref.dtype)

    in_specs = [pl.BlockSpec((tm, K), lambda j, m: (m, 0)),
                pl.BlockSpec((None, None, K, Ns), lambda j, m: (j, layer, 0, 0))]
    args = [a, w]
    if res is not None:
        in_specs.append(pl.BlockSpec((tm, Ns), lambda j, m: (m, j)))
        args.append(res)
    return pl.pallas_call(
        body, name=name, grid=(J, M // tm), in_specs=in_specs,
        out_specs=pl.BlockSpec((tm, Ns), lambda j, m: (m, j)),
        out_shape=_sds((M, J * Ns), out_dtype), compiler_params=_params("parallel", "parallel"),
    )(*args)


def mm_res_norm(a, w, res, gain, name, tm=1024):
    M, K = a.shape
    Dm = w.shape[-1]

    def body(a_ref, w_ref, r_ref, g_ref, y_ref, yn_ref):
        y = r_ref[...] + jnp.dot(a_ref[...], w_ref[...], preferred_element_type=F32)
        y_ref[...] = y
        r = lax.rsqrt(jnp.mean(y * y, axis=-1, keepdims=True) + EPS)
        yn_ref[...] = ((y * r) * g_ref[...]).astype(BF16)

    row = pl.BlockSpec((tm, Dm), lambda m: (m, 0))
    return pl.pallas_call(
        body, name=name, grid=(M // tm,),
        in_specs=[pl.BlockSpec((tm, K), lambda m: (m, 0)),
                  pl.BlockSpec((None, None, K, Dm), lambda m: (0, 0, 0, 0), pipeline_mode=pl.Buffered(1)),
                  row, pl.BlockSpec((1, Dm), lambda m: (0, 0))],
        out_specs=[row, row], out_shape=[_sds((M, Dm), F32), _sds((M, Dm), BF16)],
        compiler_params=_params("parallel"),
    )(a, w, res, gain)


def mm_res_loss(a, w, res, target, name, tm=512):
    M, K = a.shape
    Dm = w.shape[-1]

    def body(a_ref, w_ref, r_ref, t_ref, d_ref, db_ref, s_ref):
        e = (r_ref[...] + jnp.dot(a_ref[...], w_ref[...], preferred_element_type=F32)) - t_ref[...]
        d = e * (1.0 / Dm)
        d_ref[...] = d
        db_ref[...] = d.astype(BF16)
        part = jnp.sum(e * e, axis=0, keepdims=True)

        @pl.when(pl.program_id(0) == 0)
        def _():
            s_ref[...] = part

        @pl.when(pl.program_id(0) > 0)
        def _():
            s_ref[...] += part

    row = pl.BlockSpec((tm, Dm), lambda m: (m, 0))
    return pl.pallas_call(
        body, name=name, grid=(M // tm,),
        in_specs=[pl.BlockSpec((tm, K), lambda m: (m, 0)),
                  pl.BlockSpec((None, None, K, Dm), lambda m: (0, 0, 0, 0), pipeline_mode=pl.Buffered(1)), row, row],
        out_specs=[row, row, pl.BlockSpec((1, Dm), lambda m: (0, 0))],
        out_shape=[_sds((M, Dm), F32), _sds((M, Dm), BF16), _sds((1, Dm), F32)],
        compiler_params=_params("arbitrary"),
    )(a, w, res, target)


def mm_nt(dy, w, name, tr, layer=0, out_dtype=F32, tm=512):
    M = dy.shape[0]
    J, _, R, Ns = w.shape
    dims = (((1,), (1,)), ((), ()))

    def body(dy_ref, w_ref, o_ref):
        acc = None
        for j in range(J):
            p = lax.dot_general(dy_ref[:, j * Ns:(j + 1) * Ns], w_ref[j], dims, preferred_element_type=F32)
            acc = p if acc is None else acc + p
        o_ref[...] = acc.astype(o_ref.dtype)

    return pl.pallas_call(
        body, name=name, grid=(R // tr, M // tm),
        in_specs=[pl.BlockSpec((tm, J * Ns), lambda r, m: (m, 0)),
                  pl.BlockSpec((J, None, tr, Ns), lambda r, m: (0, layer, r, 0))],
        out_specs=pl.BlockSpec((tm, tr), lambda r, m: (m, r)),
        out_shape=_sds((M, R), out_dtype),
        compiler_params=_params("parallel", "parallel"),
    )(dy, w)


def mm_nt_norm_bwd(dy, w, x, g, dres, name, layer=0, tm=512):
    M = dy.shape[0]
    J, _, Dm, Ns = w.shape
    dims = (((1,), (1,)), ((), ()))

    def body(dy_ref, w_ref, x_ref, g_ref, r_ref, dx_ref, dxb_ref, dg_ref):
        dxn = None
        for j in range(J):
            p = lax.dot_general(dy_ref[:, j * Ns:(j + 1) * Ns], w_ref[j], dims, preferred_element_type=F32)
            dxn = p if dxn is None else dxn + p
        xv = x_ref[...]
        r = lax.rsqrt(jnp.mean(xv * xv, axis=-1, keepdims=True) + EPS)
        gx = dxn * g_ref[...]
        dot = jnp.sum(gx * xv, axis=-1, keepdims=True)
        dx = r_ref[...] + r * gx - xv * ((r * r * r) * (dot * (1.0 / Dm)))
        dx_ref[...] = dx
        dxb_ref[...] = dx.astype(BF16)
        part = jnp.sum(dxn * (xv * r), axis=0, keepdims=True)

        @pl.when(pl.program_id(0) == 0)
        def _():
            dg_ref[...] = part

        @pl.when(pl.program_id(0) > 0)
        def _():
            dg_ref[...] += part

    row = pl.BlockSpec((tm, Dm), lambda m: (m, 0))
    vec = pl.BlockSpec((1, Dm), lambda m: (0, 0))
    return pl.pallas_call(
        body, name=name, grid=(M // tm,),
        in_specs=[pl.BlockSpec((tm, J * Ns), lambda m: (m, 0)),
                  pl.BlockSpec((J, None, Dm, Ns), lambda m: (0, layer, 0, 0), pipeline_mode=pl.Buffered(1)), row, vec, row],
        out_specs=[row, row, vec],
        out_shape=[_sds((M, Dm), F32), _sds((M, Dm), BF16), _sds((1, Dm), F32)],
        compiler_params=_params("arbitrary"),
    )(dy, w, x, g, dres)


def mm_tn(a, dy, name, J, tk, tm=512, jb=None):
    M, K = a.shape
    jb = jb or J
    Ns = dy.shape[1] // J
    N = jb * Ns
    n_m = M // tm
    dims = (((0,), (0,)), ((), ()))

    def body(a_ref, dy_ref, o_ref, acc_ref):
        p = lax.dot_general(a_ref[...], dy_ref[...], dims, preferred_element_type=F32)
        m = pl.program_id(2)

        @pl.when(m == 0)
        def _():
            acc_ref[...] = p

        @pl.when(m > 0)
        def _():
            acc_ref[...] += p

        @pl.when(m == n_m - 1)
        def _():
            for j in range(jb):
                o_ref[j] = acc_ref[:, j * Ns:(j + 1) * Ns].astype(BF16)

    return pl.pallas_call(
        body, name=name, grid=(J // jb, K // tk, n_m),
        in_specs=[pl.BlockSpec((tm, tk), lambda g, k, m: (m, k)), pl.BlockSpec((tm, N), lambda g, k, m: (m, g))],
        out_specs=pl.BlockSpec((jb, tk, Ns), lambda g, k, m: (g, k, 0)),
        out_shape=_sds((J, K, Ns), BF16), scratch_shapes=[pltpu.VMEM((tk, N), F32)],
        compiler_params=_params("parallel", "parallel", "arbitrary"),
    )(a, dy)


HALO = 16


def _shift_down(x, s):
    return pltpu.roll(x, s, 0)


def _shift_up(x, s):
    return pltpu.roll(x, x.shape[0] - s, 0)


def _conv3(z, cw):
    return (_shift_down(z, 2) * cw[0:1] + _shift_down(z, 1) * cw[1:2]) + z * cw[2:3]


def _window_count(first_row, n, k):
    t = first_row + lax.broadcasted_iota(jnp.int32, (n, 1), 0)
    return jnp.clip(t + 1, 1, k).astype(F32)


def in_mixer_fwd(xn, w_in, conv_w, pool_w, pool_scale, name, ts=512):
    S, K = xn.shape
    n = ts + HALO

    def body(xm_ref, xb_ref, w_ref, cw_ref, pw_ref, ps_ref, p_ref, o_ref):
        i = pl.program_id(0)
        before = jnp.where(i > 0, xb_ref[...], jnp.zeros_like(xb_ref))
        rows = jnp.concatenate([before, xm_ref[...]], axis=0)
        h, gb, gc, pin = [jnp.dot(rows, w_ref[j], preferred_element_type=F32) for j in range(N_CHIPS)]
        for j, part in enumerate((h, gb, gc, pin)):
            p_ref[:, j * A_WIDTH:(j + 1) * A_WIDTH] = part[HALO:]
        cz = _conv3(gc * h, cw_ref[...])
        o_ref[:, 0:A_WIDTH] = (gb[HALO:] * cz[HALO:]).astype(BF16)
        for g, k in enumerate(POOL_WINDOWS):
            p = pin[:, g * POOL_GROUP:(g + 1) * POOL_GROUP]
            w = p
            s = 1
            while s < k:
                w = w + _shift_down(w, s)
                s *= 2
            pooled = w / _window_count(i * ts - HALO, n, k) - p
            yb = jnp.dot(pooled[HALO:].astype(BF16), pw_ref[g], preferred_element_type=F32)
            yb = yb * ps_ref[:, g * POOL_GROUP:(g + 1) * POOL_GROUP]
            o_ref[:, A_WIDTH + g * POOL_GROUP:A_WIDTH + (g + 1) * POOL_GROUP] = yb.astype(BF16)

    hb = ts // HALO
    return pl.pallas_call(
        body, name=name, grid=(S // ts,),
        in_specs=[
            pl.BlockSpec((ts, K), lambda i: (i, 0)),
            pl.BlockSpec((HALO, K), lambda i: (jnp.maximum(i * hb - 1, 0), 0)),
            pl.BlockSpec((N_CHIPS, None, K, A_WIDTH), lambda i: (0, 0, 0, 0), pipeline_mode=pl.Buffered(1)),
            pl.BlockSpec((3, A_WIDTH), lambda i: (0, 0)),
            pl.BlockSpec((4, POOL_GROUP, POOL_GROUP), lambda i: (0, 0, 0)),
            pl.BlockSpec((1, 4 * POOL_GROUP), lambda i: (0, 0)),
        ],
        out_specs=[pl.BlockSpec((ts, EVEN_IN), lambda i: (i, 0)), pl.BlockSpec((ts, D_MODEL), lambda i: (i, 0))],
        out_shape=[_sds((S, EVEN_IN), F32), _sds((S, D_MODEL), BF16)], compiler_params=_params("parallel"),
    )(xn, xn, w_in, conv_w, pool_w, pool_scale)


def mixer_bwd(proj, dmix, conv_w, pool_w, pool_scale, name, ts=256):
    S = proj.shape[0]
    n = ts + 2 * HALO
    nt = S // ts
    tn_dims = (((0,), (0,)), ((), ()))
    nt_dims = (((1,), (1,)), ((), ()))

    def body(pm_ref, pb_ref, pa_ref, dm_ref, da_ref, cw_ref, pw_ref, ps_ref, o_ref, dcw_ref, dpw_ref, dps_ref):
        i = pl.program_id(0)
        last = i == nt - 1
        before = jnp.where(i > 0, pb_ref[...], 0.0)
        after = jnp.where(last, 0.0, pa_ref[...])
        ext = jnp.concatenate([before, pm_ref[...], after], axis=0)
        dafter = jnp.where(last, 0.0, da_ref[...])
        dext = jnp.concatenate([jnp.zeros((HALO, D_MODEL), F32), dm_ref[...], dafter], axis=0)
        cw = cw_ref[...]
        main = slice(HALO, HALO + ts)

        @pl.when(i == 0)
        def _():
            dcw_ref[...] = jnp.zeros_like(dcw_ref)
            dpw_ref[...] = jnp.zeros_like(dpw_ref)
            dps_ref[...] = jnp.zeros_like(dps_ref)

        h, gb, gc = ext[:, 0:A_WIDTH], ext[:, A_WIDTH:2 * A_WIDTH], ext[:, 2 * A_WIDTH:3 * A_WIDTH]
        z = gc * h
        z1, z2 = _shift_down(z, 1), _shift_down(z, 2)
        cz = (z2 * cw[0:1] + z1 * cw[1:2]) + z * cw[2:3]
        dya = dext[:, 0:A_WIDTH]
        dcz = dya * gb
        dz = dcz * cw[2:3] + _shift_up(dcz, 1) * cw[1:2] + _shift_up(dcz, 2) * cw[0:1]
        o_ref[:, 0:A_WIDTH] = (dz * gc)[main].astype(BF16)
        o_ref[:, A_WIDTH:2 * A_WIDTH] = (dya * cz)[main].astype(BF16)
        o_ref[:, 2 * A_WIDTH:3 * A_WIDTH] = (dz * h)[main].astype(BF16)
        dczm = dcz[main]
        dcw_ref[0:1, :] += jnp.sum(dczm * z2[main], axis=0, keepdims=True)
        dcw_ref[1:2, :] += jnp.sum(dczm * z1[main], axis=0, keepdims=True)
        dcw_ref[2:3, :] += jnp.sum(dczm * z[main], axis=0, keepdims=True)

        for g, k in enumerate(POOL_WINDOWS):
            lo = 3 * A_WIDTH + g * POOL_GROUP
            cols = slice(g * POOL_GROUP, (g + 1) * POOL_GROUP)
            p = ext[:, lo:lo + POOL_GROUP]
            w = p
            s = 1
            while s < k:
                w = w + _shift_down(w, s)
                s *= 2
            cnt = _window_count(i * ts - HALO, n, k)
            pooled = (w / cnt - p)[main].astype(BF16)
            dyb = dext[:, A_WIDTH + g * POOL_GROUP:A_WIDTH + (g + 1) * POOL_GROUP]
            e = dyb * ps_ref[:, cols]
            pre = jnp.dot(pooled, pw_ref[g], preferred_element_type=F32)
            dps_ref[:, cols] += jnp.sum(dyb[main] * pre, axis=0, keepdims=True)
            dpw_ref[g] += lax.dot_general(pooled, e[main].astype(BF16), tn_dims, preferred_element_type=F32)
            dpooled = lax.dot_general(e.astype(BF16), pw_ref[g], nt_dims, preferred_element_type=F32)
            q = dpooled / cnt
            a = q
            s = 1
            while s < k:
                a = a + _shift_up(a, s)
                s *= 2
            o_ref[:, lo:lo + POOL_GROUP] = (a - dpooled)[main].astype(BF16)

    hb = ts // HALO
    nh = S // HALO
    before_map = lambda i: (jnp.maximum(i * hb - 1, 0), 0)
    after_map = lambda i: (jnp.minimum((i + 1) * hb, nh - 1), 0)
    full = lambda *shape: pl.BlockSpec(shape, lambda i: (0,) * len(shape))
    return pl.pallas_call(
        body, name=name, grid=(nt,),
        in_specs=[
            pl.BlockSpec((ts, EVEN_IN), lambda i: (i, 0)),
            pl.BlockSpec((HALO, EVEN_IN), before_map),
            pl.BlockSpec((HALO, EVEN_IN), after_map),
            pl.BlockSpec((ts, D_MODEL), lambda i: (i, 0)),
            pl.BlockSpec((HALO, D_MODEL), after_map),
            full(3, A_WIDTH), full(4, POOL_GROUP, POOL_GROUP), full(1, 4 * POOL_GROUP),
        ],
        out_specs=[pl.BlockSpec((ts, EVEN_IN), lambda i: (i, 0)), full(3, A_WIDTH), full(4, POOL_GROUP, POOL_GROUP),
                   full(1, 4 * POOL_GROUP)],
        out_shape=[_sds((S, EVEN_IN), BF16), _sds((3, A_WIDTH), F32), _sds((4, POOL_GROUP, POOL_GROUP), F32),
                   _sds((1, 4 * POOL_GROUP), F32)],
        compiler_params=_params("arbitrary"),
    )(proj, proj, proj, dmix, dmix, conv_w, pool_w, pool_scale)


FFN_HALO = 16
FFN_TC = 1408


GLU_CHUNKS = ((0, 512), (512, 512), (1024, 384))


def up_glu_fwd(xn, w_up, conv_w, conv_b, name, tm=512):
    S, K = xn.shape
    nc = D_FF // FFN_TC

    def body(xm_ref, xb_ref, wg_ref, wu_ref, cwg_ref, cwu_ref, cbg_ref, cbu_ref, pg_ref, pu_ref, ug_ref, uu_ref, o_ref):
        before = jnp.where(pl.program_id(1) > 0, xb_ref[...], jnp.zeros_like(xb_ref))
        rows = jnp.concatenate([before, xm_ref[...]], axis=0)
        for lo, width in GLU_CHUNKS:
            cols = slice(lo, lo + width)
            pre_g = jnp.dot(rows, wg_ref[:, cols], preferred_element_type=F32)
            pre_u = jnp.dot(rows, wu_ref[:, cols], preferred_element_type=F32)
            gate = _conv3(pre_g, cwg_ref[:, cols])[FFN_HALO:] + cbg_ref[:, cols]
            upv = _conv3(pre_u, cwu_ref[:, cols])[FFN_HALO:] + cbu_ref[:, cols]
            pg_ref[:, cols] = pre_g[FFN_HALO:].astype(BF16)
            pu_ref[:, cols] = pre_u[FFN_HALO:].astype(BF16)
            ug_ref[:, cols] = gate.astype(BF16)
            uu_ref[:, cols] = upv.astype(BF16)
            o_ref[:, cols] = ((gate * (1.0 / (1.0 + jnp.exp(-gate)))) * upv).astype(BF16)

    hb = tm // FFN_HALO
    wspec = lambda off: pl.BlockSpec((None, None, K, FFN_TC), lambda j, m: (j + off, 0, 0, 0))
    cw = lambda off: pl.BlockSpec((3, FFN_TC), lambda j, m: (0, j + off))
    cb = lambda off: pl.BlockSpec((1, FFN_TC), lambda j, m: (0, j + off))
    out = pl.BlockSpec((tm, FFN_TC), lambda j, m: (m, j))
    pg, pu, ug, uu, act = pl.pallas_call(
        body, name=name, grid=(nc, S // tm),
        in_specs=[pl.BlockSpec((tm, K), lambda j, m: (m, 0)),
                  pl.BlockSpec((FFN_HALO, K), lambda j, m: (jnp.maximum(m * hb - 1, 0), 0)),
                  wspec(0), wspec(nc), cw(0), cw(nc), cb(0), cb(nc)],
        out_specs=[out] * 5, out_shape=[_sds((S, D_FF), BF16)] * 5,
        compiler_params=_params("parallel", "parallel"),
    )(xn, xn, w_up, w_up, conv_w, conv_w, conv_b, conv_b)
    return (pg, pu), (ug, uu), act


def glu_bwd(up, u, da, conv_w, name, ts=256):
    S = up[0].shape[0]
    nc = D_FF // FFN_TC
    nt = S // ts
    W = 2 * D_FF

    def body(xg_ref, xu_ref, gm_ref, ga_ref, um_ref, ua_ref, dm_ref, da_ref, cw_ref, dx_ref, dcw_ref, dcb_ref):
        i = pl.program_id(0)
        last = i == nt - 1

        @pl.when(i == 0)
        def _():
            dcw_ref[...] = jnp.zeros_like(dcw_ref)
            dcb_ref[...] = jnp.zeros_like(dcb_ref)

        def rows(m_ref, a_ref, cols):
            return jnp.concatenate([m_ref[:, cols], a_ref[:, cols]], axis=0).astype(F32)

        def back(d, x, cols):
            cw = cw_ref[:, cols]
            d1, d2 = _shift_up(d, 1), _shift_up(d, 2)
            dx_ref[:, cols] = ((d * cw[2:3] + d1 * cw[1:2]) + d2 * cw[0:1])[:ts].astype(BF16)
            dcb_ref[:, cols] += jnp.sum(d[:ts], axis=0, keepdims=True)
            dcw_ref[0:1, cols] += jnp.sum(d2[:ts] * x, axis=0, keepdims=True)
            dcw_ref[1:2, cols] += jnp.sum(d1[:ts] * x, axis=0, keepdims=True)
            dcw_ref[2:3, cols] += jnp.sum(d[:ts] * x, axis=0, keepdims=True)

        for c in range(nc):
            cols = slice(c * FFN_TC, (c + 1) * FFN_TC)
            ug, uu = rows(gm_ref, ga_ref, cols), rows(um_ref, ua_ref, cols)
            dae = rows(dm_ref, da_ref, cols)
            dae = jnp.where(last & (lax.broadcasted_iota(jnp.int32, dae.shape, 0) >= ts), 0.0, dae)
            sg = 1.0 / (1.0 + jnp.exp(-ug))
            duu = dae * (ug * sg)
            dug = (dae * uu) * (sg * (1.0 + ug * (1.0 - sg)))
            back(dug, xg_ref[:, cols].astype(F32), cols)
            back(duu, xu_ref[:, cols].astype(F32), slice(D_FF + c * FFN_TC, D_FF + (c + 1) * FFN_TC))

    hb = ts // FFN_HALO
    nh = S // FFN_HALO
    after_map = lambda i: (jnp.minimum((i + 1) * hb, nh - 1), 0)
    main = pl.BlockSpec((ts, D_FF), lambda i: (i, 0))
    after = pl.BlockSpec((FFN_HALO, D_FF), after_map)
    return pl.pallas_call(
        body, name=name, grid=(nt,),
        in_specs=[main, main, main, after, main, after, main, after, pl.BlockSpec((3, W), lambda i: (0, 0))],
        out_specs=[pl.BlockSpec((ts, W), lambda i: (i, 0)), pl.BlockSpec((3, W), lambda i: (0, 0)),
                   pl.BlockSpec((1, W), lambda i: (0, 0))],
        out_shape=[_sds((S, W), BF16), _sds((3, W), F32), _sds((1, W), F32)],
        compiler_params=_params("arbitrary"),
    )(up[0], up[1], u[0], u[0], u[1], u[1], da, da, conv_w)


MEAN_GROUP = 256


def _head_mean_matrix():
    h = np.arange(MEAN_GROUP) // HEAD_DIM
    return jnp.asarray((h[:, None] == h[None, :]).astype(np.float32) / HEAD_DIM, dtype=BF16)


def _head_mean(v, gm):
    vb = v.astype(BF16)
    return jnp.concatenate([jnp.dot(vb[:, c:c + MEAN_GROUP], gm, preferred_element_type=F32)
                            for c in range(0, v.shape[1], MEAN_GROUP)], axis=1)


def qknorm_fwd(qkv, gqk, name, ts=512):
    S = qkv.shape[0]

    def body(x_ref, g_ref, gm_ref, o_ref):
        part = pl.program_id(0)
        x = x_ref[...]

        @pl.when(part < 2)
        def _():
            r = lax.rsqrt(_head_mean(x * x, gm_ref[...]) + EPS)
            o_ref[...] = ((x * r) * g_ref[...]).astype(BF16)

        @pl.when(part == 2)
        def _():
            o_ref[...] = x.astype(BF16)

    return pl.pallas_call(
        body, name=name, grid=(3, S // ts),
        in_specs=[pl.BlockSpec((ts, D_MODEL), lambda p, i: (i, p)), pl.BlockSpec((None, 1, D_MODEL), lambda p, i: (p, 0, 0)),
                  pl.BlockSpec((MEAN_GROUP, MEAN_GROUP), lambda p, i: (0, 0))],
        out_specs=pl.BlockSpec((ts, D_MODEL), lambda p, i: (i, p)),
        out_shape=_sds((S, 3 * D_MODEL), BF16), compiler_params=_params("parallel", "parallel"),
    )(qkv, gqk, _head_mean_matrix())


def qknorm_bwd(qkv, dq, dk, dv, gqk, name, ts=256):
    S = qkv.shape[0]

    def body(x_ref, dq_ref, dk_ref, dv_ref, g_ref, gm_ref, o_ref, dg_ref):
        @pl.when(pl.program_id(0) == 0)
        def _():
            dg_ref[...] = jnp.zeros_like(dg_ref)

        gm = gm_ref[...]
        for part, d_ref in enumerate((dq_ref, dk_ref)):
            cols = slice(part * D_MODEL, (part + 1) * D_MODEL)
            x = x_ref[:, cols]
            d = d_ref[...]
            r = lax.rsqrt(_head_mean(x * x, gm) + EPS)
            gx = d * g_ref[part]
            o_ref[:, cols] = (r * gx - x * ((r * r * r) * _head_mean(gx * x, gm))).astype(BF16)
            dg_ref[part] += jnp.sum(d * (x * r), axis=0, keepdims=True)
        o_ref[:, 2 * D_MODEL:] = dv_ref[...].astype(BF16)

    row = pl.BlockSpec((ts, D_MODEL), lambda i: (i, 0))
    wide = pl.BlockSpec((ts, 3 * D_MODEL), lambda i: (i, 0))
    gains = pl.BlockSpec((3, 1, D_MODEL), lambda i: (0, 0, 0))
    return pl.pallas_call(
        body, name=name, grid=(S // ts,),
        in_specs=[wide, row, row, row, gains, pl.BlockSpec((MEAN_GROUP, MEAN_GROUP), lambda i: (0, 0))],
        out_specs=[wide, gains],
        out_shape=[_sds((S, 3 * D_MODEL), BF16), _sds((3, 1, D_MODEL), F32)],
        compiler_params=_params("arbitrary"),
    )(qkv, dq, dk, dv, gqk, _head_mean_matrix())


RESIDUES = 16


def _block_order(dil):
    runs = RESIDUES // dil
    slot = np.arange(ATT_BLOCK)
    return (slot % (ATT_BLOCK // runs)) * runs + slot // (ATT_BLOCK // runs)


def _bucket_tables():
    n = ATT_BLOCK
    max_exact = N_REL_BUCKETS // 2
    buckets, valids = [], []
    for _, dil in DILATED_PAIRS:
        order = _block_order(dil)
        a = order[:, None]
        c = np.concatenate([order, n + order])[None, :]
        first_half = (np.arange(2 * n) < n)[None, :]
        rel = a + n - c
        band = (rel >= 0) & (rel <= n)
        dist = np.clip(rel, 0, n) * dil
        dd = np.maximum(dist, 1).astype(np.float32)
        large = max_exact + (np.log(dd / np.float32(max_exact)) / np.float32(math.log(REL_MAX_DISTANCE / max_exact))
                             * np.float32(N_REL_BUCKETS - max_exact)).astype(np.int32)
        large = np.minimum(large, N_REL_BUCKETS - 1)
        buckets.append(np.where(dist < max_exact, dist, large).reshape(1, -1))
        valids.append(np.stack([(band & ~first_half).reshape(1, -1), band.reshape(1, -1)]))
    return np.stack(buckets).astype(np.int32), np.stack(valids).astype(np.int32)


BIAS_CHUNK = 8192


def _split3(x):
    a = x.astype(BF16)
    r = x - a.astype(F32)
    b = r.astype(BF16)
    c = (r - b.astype(F32)).astype(BF16)
    return a, b, c


def bias_expand(rel_bias_t, name):
    bucket, valid = _bucket_tables()
    nq = bucket.shape[-1]

    def body(t_ref, b_ref, v_ref, o_ref):
        onehot = (lax.broadcasted_iota(jnp.int32, (N_REL_BUCKETS, BIAS_CHUNK), 0) == b_ref[...]).astype(BF16)
        acc = None
        for term in _split3(t_ref[...]):
            p = jnp.dot(term, onehot, preferred_element_type=F32)
            acc = p if acc is None else acc + p
        o_ref[...] = jnp.where(v_ref[...] > 0, acc, MASK_VALUE)

    return pl.pallas_call(
        body, name=name, grid=(3, 2, nq // BIAS_CHUNK),
        in_specs=[pl.BlockSpec((N_HEADS, N_REL_BUCKETS), lambda b, v, c: (0, 0)),
                  pl.BlockSpec((None, 1, BIAS_CHUNK), lambda b, v, c: (b, 0, c)),
                  pl.BlockSpec((None, None, 1, BIAS_CHUNK), lambda b, v, c: (b, v, 0, c))],
        out_specs=pl.BlockSpec((None, None, N_HEADS, BIAS_CHUNK), lambda b, v, c: (b, v, 0, c)),
        out_shape=_sds((3, 2, N_HEADS, nq), F32), compiler_params=_params("parallel", "parallel", "parallel"),
    )(rel_bias_t, jnp.asarray(bucket), jnp.asarray(valid))


def bias_reduce(dbias, name):
    bucket, _ = _bucket_tables()
    nq = bucket.shape[-1]
    dims = (((1,), (1,)), ((), ()))

    def body(d_ref, b_ref, o_ref):
        onehot = (lax.broadcasted_iota(jnp.int32, (N_REL_BUCKETS, BIAS_CHUNK), 0) == b_ref[...]).astype(BF16)
        acc = None
        for term in _split3(d_ref[...]):
            p = lax.dot_general(term, onehot, dims, preferred_element_type=F32)
            acc = p if acc is None else acc + p

        @pl.when(pl.program_id(1) == 0)
        def _():
            o_ref[...] = acc

        @pl.when(pl.program_id(1) > 0)
        def _():
            o_ref[...] += acc

    return pl.pallas_call(
        body, name=name, grid=(3, nq // BIAS_CHUNK),
        in_specs=[pl.BlockSpec((None, N_HEADS, BIAS_CHUNK), lambda b, c: (b, 0, c)),
                  pl.BlockSpec((None, 1, BIAS_CHUNK), lambda b, c: (b, 0, c))],
        out_specs=pl.BlockSpec((None, N_HEADS, N_REL_BUCKETS), lambda b, c: (b, 0, 0)),
        out_shape=_sds((3, N_HEADS, N_REL_BUCKETS), F32), compiler_params=_params("parallel", "arbitrary"),
    )(dbias, jnp.asarray(bucket))


PAIR = 2 * HEAD_DIM
N_PAIRS = N_HEADS // 2
_NT = (((1,), (1,)), ((), ()))
_TN = (((0,), (0,)), ((), ()))


def _low_lanes(shape):
    return lax.broadcasted_iota(jnp.int32, shape, 1) < HEAD_DIM


ATTN_VMEM_LIMIT_BYTES = 56 * 1024 * 1024
BRANCH_ORDER = (2, 1, 0)


def _regroup(dst, src, L16):
    for r in range(RESIDUES):
        dst[pl.ds(r * L16, L16), :] = src[pl.ds(r, L16, stride=RESIDUES), :]


def _ungroup(dst, src, L16):
    for r in range(RESIDUES):
        dst[pl.ds(r, L16, stride=RESIDUES), :] = src[pl.ds(r * L16, L16), :]


def _branch_geometry(branch, S):
    dil = DILATED_PAIRS[branch][1]
    runs = RESIDUES // dil
    return dil, runs, ATT_BLOCK // runs, S // dil // ATT_BLOCK


def _block_rows(it, branch, S):
    dil, runs, run_len, n_blocks = _branch_geometry(branch, S)
    L16 = S // RESIDUES
    r, b = it // n_blocks, it % n_blocks
    prev = jnp.maximum(b - 1, 0)
    cur_rows = [pl.multiple_of((j * dil + r) * L16 + run_len * b, 8) for j in range(runs)]
    prev_rows = [pl.multiple_of((j * dil + r) * L16 + run_len * prev, 8) for j in range(runs)]
    return cur_rows, prev_rows, jnp.minimum(b, 1)


def _load_block(ref, rows, run_len):
    parts = [ref[pl.ds(o, run_len), :] for o in rows]
    return parts[0] if len(parts) == 1 else jnp.concatenate(parts, axis=0)


def _store_block(ref, rows, run_len, value, add=False):
    for j, o in enumerate(rows):
        part = value[j * run_len:(j + 1) * run_len]
        if add:
            ref[pl.ds(o, run_len), :] += part
        else:
            ref[pl.ds(o, run_len), :] = part


ATTN_FWD_UNROLL = 8
ATTN_BWD_UNROLL = 4


def _stack_heads(x, low):
    zero = jnp.zeros_like(x)
    return jnp.concatenate([jnp.where(low, x, zero), jnp.where(low, zero, x)], axis=0)


def _unstack_heads(y, low):
    return jnp.where(low, y[:ATT_BLOCK], y[ATT_BLOCK:])


def attn_fwd(qkvn, bias, name):
    S = qkvn.shape[0]
    L16 = S // RESIDUES
    n_iter = S // ATT_BLOCK

    def body(q_ref, k_ref, v_ref, b_ref, o_ref, lse_ref, stage, qp, kp, vp, acc_s, m_s, l_s):
        for src, dst in ((q_ref, qp), (k_ref, kp), (v_ref, vp)):
            stage[...] = src[...].astype(F32)
            _regroup(dst, stage, L16)
        low = _low_lanes((ATT_BLOCK, PAIR))

        for branch in BRANCH_ORDER:
            _, _, run_len, _ = _branch_geometry(branch, S)
            first = branch == BRANCH_ORDER[0]

            def step(it, carry, branch=branch, run_len=run_len, first=first):
                cur, prev, variant = _block_rows(it, branch, S)
                q = _load_block(qp, cur, run_len).astype(BF16)
                k = jnp.concatenate([_load_block(kp, prev, run_len), _load_block(kp, cur, run_len)], axis=0).astype(BF16)
                v = jnp.concatenate([_load_block(vp, prev, run_len), _load_block(vp, cur, run_len)], axis=0).astype(BF16)
                s = lax.dot_general(_stack_heads(q, low), k, _NT, preferred_element_type=F32) * (HEAD_DIM ** -0.5)
                s = s + b_ref[2 * branch + variant].reshape(2 * ATT_BLOCK, 2 * ATT_BLOCK)
                mx = jnp.max(s, axis=-1, keepdims=True)
                p = jnp.exp(s - mx)
                den = jnp.sum(p, axis=-1, keepdims=True)
                pv = jnp.dot(p.astype(BF16), v, preferred_element_type=F32)
                acc = _unstack_heads(pv, low)
                m = _unstack_heads(mx, low)
                l = _unstack_heads(den, low)
                if not first:
                    m_old = _load_block(m_s, cur, run_len)
                    m_new = jnp.maximum(m_old, m)
                    a_old, a_new = jnp.exp(m_old - m_new), jnp.exp(m - m_new)
                    acc = _load_block(acc_s, cur, run_len) * a_old + acc * a_new
                    l = _load_block(l_s, cur, run_len) * a_old + l * a_new
                    m = m_new
                _store_block(acc_s, cur, run_len, acc)
                _store_block(m_s, cur, run_len, m)
                _store_block(l_s, cur, run_len, l)
                return carry

            lax.fori_loop(0, n_iter, step, 0, unroll=ATTN_FWD_UNROLL)

        acc_s[...] = acc_s[...] / l_s[...]
        _ungroup(stage, acc_s, L16)
        o_ref[...] = stage[...].astype(BF16)
        m_s[...] = m_s[...] + jnp.log(l_s[...])
        _ungroup(lse_ref, m_s, L16)

    col = lambda part: pl.BlockSpec((S, PAIR), lambda hp: (0, part * N_PAIRS + hp))
    out = pl.BlockSpec((S, PAIR), lambda hp: (0, hp))
    return pl.pallas_call(
        body, name=name, grid=(N_PAIRS,),
        in_specs=[col(0), col(1), col(2), pl.BlockSpec((6, 2, ATT_BLOCK, 2 * ATT_BLOCK), lambda hp: (0, hp, 0, 0))],
        out_specs=[out, out], out_shape=[_sds((S, D_MODEL), BF16), _sds((S, D_MODEL), F32)],
        scratch_shapes=[pltpu.VMEM((S, PAIR), F32)] * 7,
        compiler_params=pltpu.CompilerParams(dimension_semantics=("parallel",), vmem_limit_bytes=ATTN_VMEM_LIMIT_BYTES),
    )(qkvn, qkvn, qkvn, bias)


def attn_bwd(qkvn, att, datt, lse, bias, name):
    S = qkvn.shape[0]
    L16 = S // RESIDUES
    n_iter = S // ATT_BLOCK
    TILE = 512

    def body(q_ref, k_ref, v_ref, o_ref, do_ref, lse_ref, b_ref, dq_ref, dk_ref, dv_ref, db_ref,
             qp, kp, vp, dop, ldp, dqp, dkp, dvp):
        stage = dqp
        for src, dst in ((q_ref, qp), (k_ref, kp), (v_ref, vp), (do_ref, dop)):
            stage[...] = src[...].astype(F32)
            _regroup(dst, stage, L16)

        def pack(i, carry):
            rows = pl.ds(pl.multiple_of(i * TILE, TILE), TILE)
            low = _low_lanes((TILE, PAIR))
            lane = lax.broadcasted_iota(jnp.int32, (TILE, PAIR), 1)
            prod = do_ref[rows, :].astype(F32) * o_ref[rows, :].astype(F32)
            d0 = jnp.sum(jnp.where(low, prod, 0.0), axis=-1, keepdims=True)
            d1 = jnp.sum(jnp.where(low, 0.0, prod), axis=-1, keepdims=True)
            stage[rows, :] = jnp.where((lane & (HEAD_DIM // 2)) == 0, lse_ref[rows, :], jnp.where(low, d0, d1))
            return carry

        lax.fori_loop(0, S // TILE, pack, 0)
        _regroup(ldp, stage, L16)
        dqp[...] = jnp.zeros_like(dqp)
        dkp[...] = jnp.zeros_like(dkp)
        dvp[...] = jnp.zeros_like(dvp)
        db_ref[...] = jnp.zeros_like(db_ref)
        low = _low_lanes((ATT_BLOCK, PAIR))

        for branch in BRANCH_ORDER:
            _, _, run_len, _ = _branch_geometry(branch, S)

            def step(it, carry, branch=branch, run_len=run_len):
                cur, prev, variant = _block_rows(it, branch, S)
                q = _load_block(qp, cur, run_len).astype(BF16)
                dout = _load_block(dop, cur, run_len).astype(BF16)
                ld = _load_block(ldp, cur, run_len)
                k = jnp.concatenate([_load_block(kp, prev, run_len), _load_block(kp, cur, run_len)], axis=0).astype(BF16)
                v = jnp.concatenate([_load_block(vp, prev, run_len), _load_block(vp, cur, run_len)], axis=0).astype(BF16)
                half = HEAD_DIM // 2
                lse2 = jnp.concatenate([ld[:, 0:1], ld[:, HEAD_DIM:HEAD_DIM + 1]], axis=0)
                delta2 = jnp.concatenate([ld[:, half:half + 1], ld[:, HEAD_DIM + half:HEAD_DIM + half + 1]], axis=0)
                q2, do2 = _stack_heads(q, low), _stack_heads(dout, low)
                s = lax.dot_general(q2, k, _NT, preferred_element_type=F32) * (HEAD_DIM ** -0.5)
                p = jnp.exp(s + b_ref[2 * branch + variant].reshape(2 * ATT_BLOCK, 2 * ATT_BLOCK) - lse2)
                dp = lax.dot_general(do2, v, _NT, preferred_element_type=F32)
                ds = p * (dp - delta2)
                db_ref[branch] += ds.reshape(2, ATT_BLOCK, 2 * ATT_BLOCK)
                dsb = (ds * (HEAD_DIM ** -0.5)).astype(BF16)
                dq = _unstack_heads(jnp.dot(dsb, k, preferred_element_type=F32), low)
                dk = lax.dot_general(dsb, q2, _TN, preferred_element_type=F32)
                dv = lax.dot_general(p.astype(BF16), do2, _TN, preferred_element_type=F32)
                _store_block(dqp, cur, run_len, dq, add=True)
                _store_block(dkp, prev, run_len, dk[:ATT_BLOCK], add=True)
                _store_block(dvp, prev, run_len, dv[:ATT_BLOCK], add=True)
                _store_block(dkp, cur, run_len, dk[ATT_BLOCK:], add=True)
                _store_block(dvp, cur, run_len, dv[ATT_BLOCK:], add=True)
                return carry

            lax.fori_loop(0, n_iter, step, 0, unroll=ATTN_BWD_UNROLL)

        _ungroup(dq_ref, dqp, L16)
        _ungroup(dk_ref, dkp, L16)
        _ungroup(dv_ref, dvp, L16)

    col = lambda part: pl.BlockSpec((S, PAIR), lambda hp: (0, part * N_PAIRS + hp))
    one = pl.BlockSpec((S, PAIR), lambda hp: (0, hp))
    return pl.pallas_call(
        body, name=name, grid=(N_PAIRS,),
        in_specs=[col(0), col(1), col(2), one, one, one,
                  pl.BlockSpec((6, 2, ATT_BLOCK, 2 * ATT_BLOCK), lambda hp: (0, hp, 0, 0))],
        out_specs=[one, one, one, pl.BlockSpec((3, 2, ATT_BLOCK, 2 * ATT_BLOCK), lambda hp: (0, hp, 0, 0))],
        out_shape=[_sds((S, D_MODEL), F32)] * 3 + [_sds((3, N_HEADS, ATT_BLOCK, 2 * ATT_BLOCK), F32)],
        scratch_shapes=[pltpu.VMEM((S, PAIR), F32)] * 8,
        compiler_params=pltpu.CompilerParams(dimension_semantics=("parallel",), vmem_limit_bytes=ATTN_VMEM_LIMIT_BYTES),
    )(qkvn, qkvn, qkvn, att, datt, lse, bias)


def adamw(w, g, m, v, name):
    n, R, C = w.shape

    def body(w_ref, g_ref, m_ref, v_ref, d_ref, nm_ref, nv_ref, go_ref):
        gv = g_ref[...]
        go_ref[...] = gv
        m2 = ADAM_B1 * m_ref[...] + (1.0 - ADAM_B1) * gv
        v2 = ADAM_B2 * v_ref[...] + (1.0 - ADAM_B2) * (gv * gv)
        m_hat = m2 / (1.0 - ADAM_B1 ** ADAM_STEP)
        v_hat = v2 / (1.0 - ADAM_B2 ** ADAM_STEP)
        d_ref[...] = -ADAM_LR * (m_hat / (jnp.sqrt(v_hat) + ADAM_EPS) + ADAM_WD * w_ref[...])
        nm_ref[...] = m2
        nv_ref[...] = v2

    tr = R
    while tr * C * 4 > ELEMENTWISE_BLOCK_BYTES and tr % 16 == 0:
        tr //= 2
    spec = pl.BlockSpec((None, tr, C), lambda i, r: (i, r, 0))
    return pl.pallas_call(
        body, name=name, grid=(n, R // tr), in_specs=[spec] * 4, out_specs=[spec] * 4,
        out_shape=[_sds((n, R, C), F32)] * 4, compiler_params=_params("parallel", "parallel"),
    )(w, g, m, v)


ANY = pl.BlockSpec(memory_space=pl.ANY)


def _coords():
    return lax.axis_index("x"), lax.axis_index("y"), lax.axis_index("c")


def _other_chips(mx, my):
    return [(1 - mx, my), (mx, 1 - my), (1 - mx, 1 - my)]


def _remote(src, dst, send, recv, dev):
    return pltpu.make_async_remote_copy(src_ref=src, dst_ref=dst, send_sem=send, recv_sem=recv, device_id=dev,
                                        device_id_type=MESH)


HBM =pl.BlockSpec(memory_space=pltpu.HBM)
SEM = pl.BlockSpec(memory_space=pltpu.SEMAPHORE)
_SPLIT_COPY = pltpu.CompilerParams(has_side_effects=pltpu.SideEffectType.DATAFLOW_SIDE_EFFECTING)


def _in_hbm(a):
    return pltpu.with_memory_space_constraint(a, pltpu.HBM)


def cast_into_slot(w, layer, chip_core, name, dtype=BF16):
    _, _, hR, C = w.shape

    def body(s_ref, w_ref, o_ref):
        del s_ref
        o_ref[...] = w_ref[...].astype(dtype)

    grid_spec = pltpu.PrefetchScalarGridSpec(
        num_scalar_prefetch=1, grid=(2,),
        in_specs=[pl.BlockSpec((None, None, hR, C), lambda h, s: (layer, h, 0, 0))],
        out_specs=pl.BlockSpec((None, None, hR, C), lambda h, s: (s[0], h, 0, 0)))
    return pl.pallas_call(body, name=name, grid_spec=grid_spec, out_shape=_sds((N_CHIPS, 2, hR, C), dtype),
                          compiler_params=_params("parallel"))(chip_core, w)


def gather_start(lands, groups, name):
    n = len(lands)
    n_groups = len(groups)

    def body(*refs):
        ins = refs[:n]
        sems = refs[n:n + 2 * n_groups]
        token = refs[-1]
        mx, my, mc = _coords()
        chip = 2 * mx + my
        for g, members in enumerate(groups):
            send, recv = sems[2 * g], sems[2 * g + 1]
            for i, a in enumerate(members):
                mine = ins[a].at[chip, mc]
                for k, (px, py) in enumerate(_other_chips(mx, my)):
                    _remote(mine, mine, send.at[3 * i + k], recv.at[3 * i + k], (px, py, mc)).start()
        token[...] = jnp.zeros_like(token)

    sem_shapes = []
    for members in groups:
        sem_shapes += [pltpu.SemaphoreType.DMA((3 * len(members),))] * 2
    outs = pl.pallas_call(
        body, name=name, in_specs=[HBM] * n,
        out_specs=[SEM] * (2 * n_groups) + [HBM] * n + [pl.BlockSpec(memory_space=pltpu.VMEM)],
        out_shape=sem_shapes + [pltpu.HBM(a.shape, a.dtype) for a in lands] + [_sds((SUBLANES, LANES), F32)],
        input_output_aliases={a: 2 * n_groups + a for a in range(n)}, compiler_params=_SPLIT_COPY,
    )(*[_in_hbm(a) for a in lands])
    sems = [(outs[2 * g], outs[2 * g + 1]) for g in range(n_groups)]
    return sems, list(outs[2 * n_groups:2 * n_groups + n]), outs[-1]


def gather_forward(lands, sems, after, name):
    n = len(lands)

    def body(*refs):
        ins = refs[:n]
        send, recv = refs[n], refs[n + 1]
        fsend, frecv = refs[n + 3], refs[n + 4]
        mx, my, mc = _coords()
        for i in range(n):
            for k, (px, py) in enumerate(_other_chips(mx, my)):
                landed = ins[i].at[2 * px + py, mc]
                cp = _remote(landed, landed, send.at[3 * i + k], recv.at[3 * i + k], (px, py, mc))
                cp.wait_send()
                cp.wait_recv()
                _remote(landed, landed, fsend.at[3 * i + k], frecv.at[3 * i + k], (mx, my, 1 - mc)).start()

    outs = pl.pallas_call(
        body, name=name, in_specs=[HBM] * n + [SEM, SEM, ANY], out_specs=[SEM, SEM] + [HBM] * n,
        out_shape=[pltpu.SemaphoreType.DMA((3 * n,))] * 2 + [pltpu.HBM(a.shape, a.dtype) for a in lands],
        input_output_aliases={a: 2 + a for a in range(n)}, compiler_params=_SPLIT_COPY,
    )(*lands, sems[0], sems[1], after)
    return (outs[0], outs[1]), list(outs[2:])


def gather_wait(lands, sems, after, name):
    n = len(lands)

    def body(*refs):
        ins = refs[:n]
        fsend, frecv = refs[n], refs[n + 1]
        mx, my, mc = _coords()
        for i in range(n):
            for k, (px, py) in enumerate(_other_chips(mx, my)):
                theirs = ins[i].at[2 * px + py, 1 - mc]
                cp = _remote(theirs, theirs, fsend.at[3 * i + k], frecv.at[3 * i + k], (mx, my, 1 - mc))
                cp.wait_send()
                cp.wait_recv()

    outs = pl.pallas_call(
        body, name=name, in_specs=[HBM] * n + [SEM, SEM, ANY], out_specs=[HBM] * n,
        out_shape=[pltpu.HBM(a.shape, a.dtype) for a in lands],
        input_output_aliases={a: a for a in range(n)}, compiler_params=_SPLIT_COPY,
    )(*lands, sems[0], sems[1], after)
    return list(outs)


def _peers(mx, my, mc):
    return [(1 - mx if k & 4 else mx, 1 - my if k & 2 else my, 1 - mc if k & 1 else mc) for k in range(1, N_DEV)]


def devices_start(x, name):
    def body(x_ref, land_ref, send, recv, x_thru, land_thru):
        mx, my, mc = _coords()
        me = 4 * mx + 2 * my + mc
        for k, peer in enumerate(_peers(mx, my, mc)):
            _remote(x_ref, land_ref.at[me], send.at[k], recv.at[k], peer).start()

    land = lax.empty((N_DEV,) + x.shape, x.dtype)
    outs = pl.pallas_call(
        body, name=name, in_specs=[HBM, HBM], out_specs=[SEM, SEM, HBM, HBM],
        out_shape=[pltpu.SemaphoreType.DMA((N_DEV - 1,))] * 2 + [pltpu.HBM(x.shape, x.dtype), pltpu.HBM(land.shape, x.dtype)],
        input_output_aliases={0: 2, 1: 3}, compiler_params=_SPLIT_COPY,
    )(_in_hbm(x), _in_hbm(land))
    return (outs[0], outs[1]), outs[2], outs[3]


def devices_wait(x, land, sems, after, name):
    def body(x_ref, land_ref, send, recv, after_ref, x_thru, land_thru):
        mx, my, mc = _coords()
        for k, (px, py, pc) in enumerate(_peers(mx, my, mc)):
            cp = _remote(x_ref, land_ref.at[4 * px + 2 * py + pc], send.at[k], recv.at[k], (px, py, pc))
            cp.wait_send()
            cp.wait_recv()

    outs = pl.pallas_call(
        body, name=name, in_specs=[HBM, HBM, SEM, SEM, ANY], out_specs=[HBM, HBM],
        out_shape=[pltpu.HBM(x.shape, x.dtype), pltpu.HBM(land.shape, land.dtype)],
        input_output_aliases={0: 0, 1: 1}, compiler_params=_SPLIT_COPY,
    )(x, land, sems[0], sems[1], after)
    return outs[0], outs[1]


def device_sum(land, own, me, name):
    _, R, C = land.shape

    def body(s_ref, l_ref, o_ref_in, o_ref):
        acc = None
        for q in range(N_DEV):
            term = jnp.where(s_ref[0] == q, o_ref_in[...], l_ref[q])
            acc = term if acc is None else acc + term
        o_ref[...] = acc

    grid_spec = pltpu.PrefetchScalarGridSpec(
        num_scalar_prefetch=1, grid=(1,),
        in_specs=[pl.BlockSpec((N_DEV, R, C), lambda i, s: (0, 0, 0)), pl.BlockSpec((R, C), lambda i, s: (0, 0))],
        out_specs=pl.BlockSpec((R, C), lambda i, s: (0, 0)))
    return pl.pallas_call(body, name=name, grid_spec=grid_spec, out_shape=_sds((R, C), F32),
                          compiler_params=_params("arbitrary"))(me, land, own)


def reduce_send(grads, name):
    n = len(grads)

    def body(*refs):
        ins, lands = refs[:n], refs[n:2 * n]
        send, recv = refs[2 * n], refs[2 * n + 1]
        mx, my, mc = _coords()
        me = 4 * mx + 2 * my + mc
        for a in range(n):
            for k, (px, py, pc) in enumerate(_peers(mx, my, mc)):
                _remote(ins[a].at[2 * px + py, pc], lands[a].at[me], send.at[7 * a + k], recv.at[7 * a + k], (px, py, pc)).start()

    lands = [lax.empty((N_DEV,) + g.shape[2:], g.dtype) for g in grads]
    outs = pl.pallas_call(
        body, name=name, in_specs=[HBM] * (2 * n), out_specs=[SEM, SEM] + [HBM] * (2 * n),
        out_shape=[pltpu.SemaphoreType.DMA((7 * n,))] * 2 + [pltpu.HBM(a.shape, a.dtype) for a in grads + lands],
        input_output_aliases={a: 2 + a for a in range(2 * n)}, compiler_params=_SPLIT_COPY,
    )(*[_in_hbm(a) for a in grads + lands])
    return (outs[0], outs[1]), list(outs[2:2 + n]), list(outs[2 + n:])


def reduce_wait(grads, lands, sems, after, name):
    n = len(grads)

    def body(*refs):
        ins, zones = refs[:n], refs[n:2 * n]
        send, recv = refs[2 * n], refs[2 * n + 1]
        mx, my, mc = _coords()
        for a in range(n):
            for k, (px, py, pc) in enumerate(_peers(mx, my, mc)):
                cp = _remote(ins[a].at[2 * px + py, pc], zones[a].at[4 * px + 2 * py + pc], send.at[7 * a + k],
                             recv.at[7 * a + k], (px, py, pc))
                cp.wait_send()
                cp.wait_recv()

    outs = pl.pallas_call(
        body, name=name, in_specs=[HBM] * (2 * n) + [SEM, SEM, ANY], out_specs=[HBM] * (2 * n),
        out_shape=[pltpu.HBM(a.shape, a.dtype) for a in grads + lands],
        input_output_aliases={a: a for a in range(2 * n)}, compiler_params=_SPLIT_COPY,
    )(*grads, *lands, sems[0], sems[1], after)
    return list(outs[:n]), list(outs[n:])


def reduce_sum(land, grad, place, name, into=None, layer=None):
    _, hR, C = land.shape
    tr = hR
    while N_DEV * tr * C * 2 > 3 * ELEMENTWISE_BLOCK_BYTES and tr % 32 == 0:
        tr //= 2

    def body(s_ref, l_ref, g_ref, *rest):
        o_ref = rest[-1]
        own = g_ref[...].astype(F32)
        acc = None
        for q in range(N_DEV):
            term = jnp.where(s_ref[2] == q, own, l_ref[q].astype(F32))
            acc = term if acc is None else acc + term
        o_ref[...] = acc

    in_specs = [pl.BlockSpec((N_DEV, tr, C), lambda i, s: (0, i, 0)),
                pl.BlockSpec((None, None, tr, C), lambda i, s: (s[0], s[1], i, 0))]
    args = [place, land, grad]
    aliases = {}
    if layer is None:
        out_spec = pl.BlockSpec((None, tr, C), lambda i, s: (s[1], i, 0))
        out_shape = _sds((2, hR, C), F32)
    else:
        out_spec = pl.BlockSpec((None, None, tr, C), lambda i, s: (layer, s[1], i, 0))
        out_shape = _sds((2, 2, hR, C), F32)
        if into is not None:
            in_specs.append(ANY)
            args.append(into)
            aliases = {3: 0}
    grid_spec = pltpu.PrefetchScalarGridSpec(num_scalar_prefetch=1, grid=(hR // tr,), in_specs=in_specs, out_specs=out_spec)
    return pl.pallas_call(body, name=name, grid_spec=grid_spec, out_shape=out_shape, input_output_aliases=aliases,
                          compiler_params=_params("arbitrary"))(*args)


def join_halves(arrays, name):
    n = len(arrays)
    pieces = [(a, l) for a, arr in enumerate(arrays) for l in (range(arr.shape[0]) if arr.ndim == 4 else [None])]

    def body(*refs):
        ins = refs[:n]
        send, recv = refs[2 * n:]
        mx, my, mc = _coords()

        def half(a, l, h):
            return ins[a].at[h] if l is None else ins[a].at[l, h]

        sends = [_remote(half(a, l, mc), half(a, l, mc), send.at[i], recv.at[i], (mx, my, 1 - mc))
                 for i, (a, l) in enumerate(pieces)]
        for cp in sends:
            cp.start()
        for i, (a, l) in enumerate(pieces):
            theirs = half(a, l, 1 - mc)
            _remote(theirs, theirs, send.at[i], recv.at[i], (mx, my, 1 - mc)).wait_recv()
        for cp in sends:
            cp.wait_send()

    return pl.pallas_call(
        body, name=name, in_specs=[ANY] * n, out_specs=[ANY] * n, out_shape=[_sds(a.shape, a.dtype) for a in arrays],
        input_output_aliases={a: a for a in range(n)},
        scratch_shapes=[pltpu.SemaphoreType.DMA((len(pieces),)), pltpu.SemaphoreType.DMA((len(pieces),))],
    )(*arrays)


LANES = 128
SUBLANES = 8


def _n_rows(shape):
    rows = -(-int(np.prod(shape)) // LANES)
    return -(-rows // SUBLANES) * SUBLANES


def _as_rows(a):
    flat = a.reshape(-1)
    rows = _n_rows(a.shape)
    return jnp.pad(flat, (0, rows * LANES - flat.shape[0])).reshape(rows, LANES)


def _pack(arrays):
    return jnp.concatenate([_as_rows(a) for a in arrays], axis=0)


def _unpack(rows, shapes):
    out, r0 = [], 0
    for s in shapes:
        n = _n_rows(s)
        out.append(rows[r0:r0 + n].reshape(-1)[:int(np.prod(s))].reshape(s))
        r0 += n
    return out


REPLICATED_SMALL = [("rel_bias", (32, 16)), ("even_norm", (1, 1024)), ("even_pool_w", (1, 4, 128, 128)),
                    ("even_pool_scale", (1, 512)), ("odd_q_norm", (1, 64)), ("odd_k_norm", (1, 64)),
                    ("ffn_norm", (2, 1024)), ("ffn_conv_b", (2, 5632))]
SHARDED_SMALL = [("even_conv_w", (1, 3, 128)), ("odd_norm", (1, 256)), ("ffn_conv_w", (2, 3, 1408))]
BIG = ["even_w_in", "even_w_out", "odd_w_qkv", "odd_w_o", "ffn_w_up", "ffn_w_down"]
WEIGHT_ORDER = ["rel_bias", "even_norm", "even_w_in", "even_conv_w", "even_pool_w", "even_pool_scale", "even_w_out",
                "odd_norm", "odd_w_qkv", "odd_q_norm", "odd_k_norm", "odd_w_o", "ffn_norm", "ffn_w_up", "ffn_conv_w",
                "ffn_conv_b", "ffn_w_down"]


def kernel(x, rel_bias, even_norm, even_w_in, even_conv_w, even_pool_w, even_pool_scale, even_w_out, odd_norm, odd_w_qkv, odd_q_norm, odd_k_norm, odd_w_o, ffn_norm, ffn_w_up, ffn_conv_w, ffn_conv_b, ffn_w_down, loss_target, m_rel_bias, m_even_norm, m_even_w_in, m_even_conv_w, m_even_pool_w, m_even_pool_scale, m_even_w_out, m_odd_norm, m_odd_w_qkv, m_odd_q_norm, m_odd_k_norm, m_odd_w_o, m_ffn_norm, m_ffn_w_up, m_ffn_conv_w, m_ffn_conv_b, m_ffn_w_down, v_rel_bias, v_even_norm, v_even_w_in, v_even_conv_w, v_even_pool_w, v_even_pool_scale, v_even_w_out, v_odd_norm, v_odd_w_qkv, v_odd_q_norm, v_odd_k_norm, v_odd_w_o, v_ffn_norm, v_ffn_w_up, v_ffn_conv_w, v_ffn_conv_b, v_ffn_w_down):
    W = dict(rel_bias=rel_bias, even_norm=even_norm, even_w_in=even_w_in, even_conv_w=even_conv_w, even_pool_w=even_pool_w,
             even_pool_scale=even_pool_scale, even_w_out=even_w_out, odd_norm=odd_norm, odd_w_qkv=odd_w_qkv,
             odd_q_norm=odd_q_norm, odd_k_norm=odd_k_norm, odd_w_o=odd_w_o, ffn_norm=ffn_norm, ffn_w_up=ffn_w_up,
             ffn_conv_w=ffn_conv_w, ffn_conv_b=ffn_conv_b, ffn_w_down=ffn_w_down)
    M1 = dict(rel_bias=m_rel_bias, even_norm=m_even_norm, even_w_in=m_even_w_in, even_conv_w=m_even_conv_w,
              even_pool_w=m_even_pool_w, even_pool_scale=m_even_pool_scale, even_w_out=m_even_w_out, odd_norm=m_odd_norm,
              odd_w_qkv=m_odd_w_qkv, odd_q_norm=m_odd_q_norm, odd_k_norm=m_odd_k_norm, odd_w_o=m_odd_w_o,
              ffn_norm=m_ffn_norm, ffn_w_up=m_ffn_w_up, ffn_conv_w=m_ffn_conv_w, ffn_conv_b=m_ffn_conv_b,
              ffn_w_down=m_ffn_w_down)
    M2 = dict(rel_bias=v_rel_bias, even_norm=v_even_norm, even_w_in=v_even_w_in, even_conv_w=v_even_conv_w,
              even_pool_w=v_even_pool_w, even_pool_scale=v_even_pool_scale, even_w_out=v_even_w_out, odd_norm=v_odd_norm,
              odd_w_qkv=v_odd_w_qkv, odd_q_norm=v_odd_q_norm, odd_k_norm=v_odd_k_norm, odd_w_o=v_odd_w_o,
              ffn_norm=v_ffn_norm, ffn_w_up=v_ffn_w_up, ffn_conv_w=v_ffn_conv_w, ffn_conv_b=v_ffn_conv_b,
              ffn_w_down=v_ffn_w_down)
    mx, my, mc = _coords()
    chip = 2 * mx + my
    me = 4 * mx + 2 * my + mc
    place = jnp.stack([chip, mc, me]).astype(jnp.int32)
    xs, target = x[0], loss_target[0]

    def halves(w):
        return w.reshape((w.shape[0], 2, w.shape[-2] // 2, w.shape[-1]))

    lands = [cast_into_slot(halves(even_w_in), 0, place, "cast_w_in"), cast_into_slot(halves(even_w_out), 0, place, "cast_w_out"),
             cast_into_slot(halves(ffn_w_up), 0, place, "cast_w_up0"), cast_into_slot(halves(ffn_w_down), 0, place, "cast_w_down0"),
             cast_into_slot(halves(odd_w_qkv), 0, place, "cast_w_qkv"), cast_into_slot(halves(odd_w_o), 0, place, "cast_w_o"),
             cast_into_slot(halves(ffn_w_up), 1, place, "cast_w_up1"), cast_into_slot(halves(ffn_w_down), 1, place, "cast_w_down1")]
    small_rows = jnp.pad(_pack([even_conv_w, odd_norm, ffn_conv_w]), ((0, SUBLANES), (0, 0)))
    lands.append(cast_into_slot(small_rows.reshape(1, 2, small_rows.shape[0] // 2, LANES), 0, place, "small_into_slot", dtype=F32))
    groups = [[0, 1, 8], [2], [3], [4, 5], [6], [7]]
    gather_sems, lands, token = gather_start(lands, groups, "gather_start")
    even_norm_after_start = even_norm + token[0:1, 0:1]

    def gathered(group, tag, after_landing, after_passing):
        mine = [lands[a] for a in groups[group]]
        sems, arrays = gather_forward(mine, gather_sems[group], after_landing, "gather_forward_" + tag)
        return gather_wait(arrays, sems, after_passing, "gather_wait_" + tag)

    pool_w = cast_bf16(even_pool_w[0], "cast_pool_w")
    gqk = jnp.stack([jnp.tile(odd_q_norm[0], N_HEADS), jnp.tile(odd_k_norm[0], N_HEADS),
                     jnp.ones((D_MODEL,), F32)])[:, None, :]
    bias = bias_expand(rel_bias.T, "bias_expand").reshape(6, N_HEADS, ATT_BLOCK, 2 * ATT_BLOCK)
    xn0 = rmsnorm_fwd(xs, even_norm_after_start, "even_norm")
    got = gathered(0, "even", bias, xn0)
    w_in = got[0].reshape(N_CHIPS, 1, D_MODEL, EVEN_IN // N_CHIPS)
    w_out = got[1].reshape(1, 1, D_MODEL, D_MODEL)
    small = got[2].reshape(N_CHIPS, small_rows.shape[0], LANES)
    conv_w_full = small[:, 0:3].transpose(1, 0, 2).reshape(3, A_WIDTH)
    odd_norm_full = small[:, 8:10].reshape(1, D_MODEL)
    ffn_cw_full = small[:, 16:82].reshape(N_CHIPS, 2, 3, 2 * D_FF // N_CHIPS).transpose(1, 2, 0, 3).reshape(2, 3, 2 * D_FF)

    def ffn_fwd(l, xin, xn):
        up, u, act = up_glu_fwd(xn, w_up[l], ffn_cw_full[l], ffn_conv_b[l:l + 1], f"ffn{l}_up_glu")
        return act, (xin, xn, up, u, act)

    w_up, w_down = [None, None], [None, None]
    proj, mix = in_mixer_fwd(xn0, w_in, conv_w_full, pool_w, even_pool_scale, "even_in_mixer")
    x1, xn1 = mm_res_norm(mix, w_out, xs, ffn_norm[0:1], "even_out")
    w_up[0] = gathered(1, "up0", proj, x1)[0].reshape(N_CHIPS, 1, D_MODEL, 2 * D_FF // N_CHIPS)
    act0, ffn0 = ffn_fwd(0, x1, xn1)
    w_down[0] = gathered(2, "down0", x1, act0)[0].reshape(1, 1, D_FF, D_MODEL)
    x2, xn2 = mm_res_norm(act0, w_down[0], x1, odd_norm_full, "ffn0_down")
    got = gathered(3, "odd", x1, x2)
    w_qkv = got[0].reshape(N_CHIPS, 1, D_MODEL, 3 * D_MODEL // N_CHIPS)
    w_o = got[1].reshape(1, 1, D_MODEL, D_MODEL)
    qkv = mm_nn(xn2, w_qkv, "odd_qkv")
    qkvn = qknorm_fwd(qkv, gqk, "odd_qknorm")
    att, lse = attn_fwd(qkvn, bias, "attn_fwd")
    x3, xn3 = mm_res_norm(att, w_o, x2, ffn_norm[1:2], "odd_out")
    w_up[1] = gathered(4, "up1", x2, x3)[0].reshape(N_CHIPS, 1, D_MODEL, 2 * D_FF // N_CHIPS)
    act1, ffn1 = ffn_fwd(1, x3, xn3)
    w_down[1] = gathered(5, "down1", x3, act1)[0].reshape(1, 1, D_FF, D_MODEL)
    dy, dyb, sq = mm_res_loss(act1, w_down[1], x3, target, "ffn1_down_loss")
    loss = lax.psum(0.5 * jnp.sum(sq) * (1.0 / D_MODEL), ("x", "y", "c"))

    def ffn_bwd(l, dy, dyb, saved):
        xin, xn, up, u, act = saved
        dw_down = mm_tn(act, dyb, f"ffn{l}_dw_down", J=1, tk=D_FF // 2, tm=1024)
        dact = mm_nt(dyb, w_down[l], f"ffn{l}_dact", tr=D_FF // 2, out_dtype=BF16, tm=1024)
        dup, dcw, dcb = glu_bwd(up, u, dact, ffn_cw_full[l], f"ffn{l}_glu_bwd")
        dw_up = mm_tn(xn, dup, f"ffn{l}_dw_up", J=N_CHIPS, tk=512, tm=1024, jb=2)
        dx, dxb, dg = mm_nt_norm_bwd(dup, w_up[l], xin, ffn_norm[l:l + 1], dy, f"ffn{l}_dx")
        return dx, dxb, (dw_down, dw_up, dcw, dcb, dg)

    def quarters(g):
        return g.reshape(N_CHIPS, 2, g.shape[0] * g.shape[1] // (2 * N_CHIPS), g.shape[-1])

    def reduce_start(grads, tag, then):
        sems, parts, zones = reduce_send([quarters(g) for g in grads], "reduce_send_" + tag)
        then, parts = lax.optimization_barrier((then, parts))
        return (sems, parts, zones), then

    dx3, dx3b, g_ffn1 = ffn_bwd(1, dy, dyb, ffn1)
    red_ffn1, (dx3, dx3b) = reduce_start([g_ffn1[1], g_ffn1[0]], "ffn1", (dx3, dx3b))
    dw_o = mm_tn(att, dx3b, "odd_dw_o", J=1, tk=512, tm=1024)
    datt = mm_nt(dx3b, w_o, "odd_datt", tr=D_MODEL, out_dtype=BF16)
    dq, dk, dv, dbias = attn_bwd(qkvn, att, datt, lse, bias, "attn_bwd")
    dqkv, dgqk = qknorm_bwd(qkv, dq, dk, dv, gqk, "odd_qknorm_bwd")
    dw_qkv = mm_tn(xn2, dqkv, "odd_dw_qkv", J=N_CHIPS, tk=512, tm=1024)
    red_odd, dqkv = reduce_start([dw_qkv, dw_o], "odd", dqkv)
    dx2, dx2b, dg_odd = mm_nt_norm_bwd(dqkv, w_qkv, x2, odd_norm_full, dx3, "odd_dx")
    dx1, dx1b, g_ffn0 = ffn_bwd(0, dx2, dx2b, ffn0)
    red_ffn0, (dx1, dx1b) = reduce_start([g_ffn0[1], g_ffn0[0]], "ffn0", (dx1, dx1b))
    dw_out = mm_tn(mix, dx1b, "even_dw_out", J=1, tk=512, tm=1024)
    dmix = mm_nt(dx1b, w_out, "even_dmix", tr=D_MODEL)
    dproj, dcw_even, dpw, dps = mixer_bwd(proj, dmix, conv_w_full, pool_w, even_pool_scale, "even_mixer_bwd")
    dw_in = mm_tn(xn0, dproj, "even_dw_in", J=N_CHIPS, tk=512, tm=1024)
    grad_x, _, dg_even = mm_nt_norm_bwd(dproj, w_in, xs, even_norm, dx1, "even_dx")
    d_rel = jnp.sum(bias_reduce(dbias.reshape(3, N_HEADS, 2 * ATT_BLOCK * ATT_BLOCK), "bias_reduce"), axis=0).T

    red_even, grad_x = reduce_start([dw_in, dw_out], "even", grad_x)

    dcw_sh = dcw_even.reshape(3, N_CHIPS, A_WIDTH // N_CHIPS).transpose(1, 0, 2)
    don_sh = dg_odd.reshape(N_CHIPS, D_MODEL // N_CHIPS)
    dfcw = jnp.stack([g_ffn0[2], g_ffn1[2]])
    dfcw_sh = dfcw.reshape(2, 3, N_CHIPS, 2 * D_FF // N_CHIPS).transpose(2, 0, 1, 3)
    rep_grads = [d_rel, dg_even, dpw[None], dps, _head_sum(dgqk[0]), _head_sum(dgqk[1]),
                 jnp.concatenate([g_ffn0[4], g_ffn1[4]], axis=0), jnp.concatenate([g_ffn0[3], g_ffn1[3]], axis=0)]
    rep_rows = _pack(rep_grads)
    shard_rows = jnp.concatenate([_pack([dcw_sh[j], don_sh[j], dfcw_sh[j]]) for j in range(N_CHIPS)], axis=0)
    n_rep, n_shard = rep_rows.shape[0], shard_rows.shape[0] // N_CHIPS
    small_sems, small_rows, small_land = devices_start(jnp.concatenate([rep_rows, shard_rows], axis=0), "small_grads_start")
    grad_x, small_rows = lax.optimization_barrier((grad_x, small_rows))

    def reduce_end(red, tag, after):
        sems, parts, zones = red
        parts, zones = reduce_wait(parts, zones, sems, after, "reduce_wait_" + tag)
        return zones, parts

    z_ffn1, p_ffn1 = reduce_end(red_ffn1, "ffn1", grad_x)
    z_odd, p_odd = reduce_end(red_odd, "odd", grad_x)
    r_qkv = reduce_sum(z_odd[0], p_odd[0], place, "reduce_sum_w_qkv")
    r_o = reduce_sum(z_odd[1], p_odd[1], place, "reduce_sum_w_o")
    r_up = reduce_sum(z_ffn1[0], p_ffn1[0], place, "reduce_sum_w_up1", layer=1)
    r_down = reduce_sum(z_ffn1[1], p_ffn1[1], place, "reduce_sum_w_down1", layer=1)
    r_qkv, r_o, r_up, r_down = lax.optimization_barrier((r_qkv, r_o, r_up, r_down))
    z_ffn0, p_ffn0 = reduce_end(red_ffn0, "ffn0", r_down)
    r_up = reduce_sum(z_ffn0[0], p_ffn0[0], place, "reduce_sum_w_up0", into=r_up, layer=0)
    r_down = reduce_sum(z_ffn0[1], p_ffn0[1], place, "reduce_sum_w_down0", into=r_down, layer=0)
    later = ["odd_w_qkv", "odd_w_o", "ffn_w_up", "ffn_w_down"]
    joined = join_halves([r_qkv, r_o, r_up, r_down], "grads_join_late_layers")
    G = {nm: g.reshape(W[nm].shape) for nm, g in zip(later, joined)}

    D_, NM, NV = {}, {}, {}

    def update(nm):
        as3 = lambda a: a.reshape((-1,) + a.shape[-2:])
        outs = adamw(as3(W[nm]), as3(G[nm]), as3(M1[nm]), as3(M2[nm]), "adamw_" + nm)
        D_[nm], NM[nm], NV[nm], G[nm] = [o.reshape(W[nm].shape) for o in outs]

    def all_before(names):
        tied = lax.optimization_barrier([D_[nm] for nm in names])
        for nm, d in zip(names, tied):
            D_[nm] = d
        return tied[0]

    for nm in later:
        update(nm)
    z_even, p_even = reduce_end(red_even, "even", all_before(later))
    joined = join_halves([reduce_sum(z_even[0], p_even[0], place, "reduce_sum_w_in"),
                          reduce_sum(z_even[1], p_even[1], place, "reduce_sum_w_out")], "grads_join_first_layer")
    first = ["even_w_in", "even_w_out"]
    for nm, g in zip(first, joined):
        G[nm] = g.reshape(W[nm].shape)
        update(nm)
    small_rows, small_land = devices_wait(small_rows, small_land, small_sems, all_before(first), "small_grads_wait")
    small_sum = device_sum(small_land, small_rows, place[2:3], "small_grads_sum")
    mine = lax.dynamic_slice_in_dim(small_sum, n_rep + chip * n_shard, n_shard, axis=0)
    g_small = jnp.concatenate([small_sum[:n_rep], mine], axis=0)
    small_names = [n for n, _ in REPLICATED_SMALL + SHARDED_SMALL]
    small_shapes = [s for _, s in REPLICATED_SMALL + SHARDED_SMALL]
    G.update(dict(zip(small_names, _unpack(g_small, small_shapes))))
    packs = [_pack([d[n] for n in small_names])[None] for d in (W, M1, M2)]
    outs = adamw(packs[0], g_small[None], packs[1], packs[2], "adamw_small")
    for dst, o in zip((D_, NM, NV), outs[:3]):
        dst.update(dict(zip(small_names, _unpack(o[0], small_shapes))))

    return (loss, grad_x[None], *[G[n] for n in WEIGHT_ORDER], *[D_[n] for n in WEIGHT_ORDER],
            *[NM[n] for n in WEIGHT_ORDER], *[NV[n] for n in WEIGHT_ORDER])


def _head_sum(dg):
    return jnp.sum(dg.reshape(N_HEADS, HEAD_DIM), axis=0, keepdims=True)
```

```python
import functools
import math

import numpy as np
import jax
import jax.numpy as jnp
from jax import lax
from jax.experimental import pallas as pl
from jax.experimental.pallas import tpu as pltpu

F32 = jnp.float32
BF16 = jnp.bfloat16

D_MODEL = 1024
N_HEADS = 16
HEAD_DIM = 64
A_WIDTH = 512
POOL_WINDOWS = (2, 4, 8, 16)
POOL_GROUP = 128
EVEN_IN = 2048
D_FF = 2816
DILATED_PAIRS = ((128, 1), (512, 4), (2048, 16))
ATT_BLOCK = 128
N_REL_BUCKETS = 32
REL_MAX_DISTANCE = 2048
EPS = 1e-6
MASK_VALUE = -1e30
ADAM_LR, ADAM_B1, ADAM_B2, ADAM_EPS, ADAM_WD, ADAM_STEP = 0.001, 0.9, 0.999, 1e-08, 0.01, 10

VMEM_LIMIT_BYTES = 48 * 1024 * 1024
ELEMENTWISE_BLOCK_BYTES = 2 * 1024 * 1024
N_CHIPS = 4
N_DEV = 8
MESH = pl.DeviceIdType.MESH


def _params(*sem):
    return pltpu.CompilerParams(dimension_semantics=sem if sem else None, vmem_limit_bytes=VMEM_LIMIT_BYTES)


def _sds(shape, dtype):
    return jax.ShapeDtypeStruct(tuple(shape), dtype)


def cast_bf16(x, name, tr=None):
    lead, (R, C) = x.shape[:-2], x.shape[-2:]
    n = int(np.prod(lead)) if lead else 1
    x3 = x.reshape((n, R, C))
    tr = tr or R

    def body(x_ref, o_ref):
        o_ref[...] = x_ref[...].astype(BF16)

    out = pl.pallas_call(
        body, name=name, grid=(n, R // tr),
        in_specs=[pl.BlockSpec((None, tr, C), lambda i, r: (i, r, 0))],
        out_specs=pl.BlockSpec((None, tr, C), lambda i, r: (i, r, 0)),
        out_shape=_sds((n, R, C), BF16), compiler_params=_params("parallel", "parallel"),
    )(x3)
    return out.reshape(lead + (R, C))


def rmsnorm_fwd(x, g, name, ts=512):
    S, Dm = x.shape

    def body(x_ref, g_ref, o_ref):
        xv = x_ref[...]
        r = lax.rsqrt(jnp.mean(xv * xv, axis=-1, keepdims=True) + EPS)
        o_ref[...] = ((xv * r) * g_ref[...]).astype(BF16)

    return pl.pallas_call(
        body, name=name, grid=(S // ts,),
        in_specs=[pl.BlockSpec((ts, Dm), lambda i: (i, 0)), pl.BlockSpec((1, Dm), lambda i: (0, 0))],
        out_specs=pl.BlockSpec((ts, Dm), lambda i: (i, 0)),
        out_shape=_sds((S, Dm), BF16), compiler_params=_params("parallel"),
    )(x, g)


def mm_nn(a, w, name, layer=0, res=None, out_dtype=F32, tm=1024):
    M, K = a.shape
    J, _, _, Ns = w.shape

    def body(*refs):
        a_ref, w_ref = refs[0], refs[1]
        o_ref = refs[-1]
        acc = jnp.dot(a_ref[...], w_ref[...], preferred_element_type=F32)
        if res is not None:
            acc = refs[2][...] + acc
        o_ref[...] = acc.astype(o_ref.dtype)

    in_specs = [pl.BlockSpec((tm, K), lambda j, m: (m, 0)),
                pl.BlockSpec((None, None, K, Ns), lambda j, m: (j, layer, 0, 0))]
    args = [a, w]
    if res is not None:
        in_specs.append(pl.BlockSpec((tm, Ns), lambda j, m: (m, j)))
        args.append(res)
    return pl.pallas_call(
        body, name=name, grid=(J, M // tm), in_specs=in_specs,
        out_specs=pl.BlockSpec((tm, Ns), lambda j, m: (m, j)),
        out_shape=_sds((M, J * Ns), out_dtype), compiler_params=_params("parallel", "parallel"),
    )(*args)


def mm_res_norm(a, w, res, gain, name, tm=1024):
    M, K = a.shape
    Dm = w.shape[-1]

    def body(a_ref, w_ref, r_ref, g_ref, y_ref, yn_ref):
        y = r_ref[...] + jnp.dot(a_ref[...], w_ref[...], preferred_element_type=F32)
        y_ref[...] = y
        r = lax.rsqrt(jnp.mean(y * y, axis=-1, keepdims=True) + EPS)
        yn_ref[...] = ((y * r) * g_ref[...]).astype(BF16)

    row = pl.BlockSpec((tm, Dm), lambda m: (m, 0))
    return pl.pallas_call(
        body, name=name, grid=(M // tm,),
        in_specs=[pl.BlockSpec((tm, K), lambda m: (m, 0)),
                  pl.BlockSpec((None, None, K, Dm), lambda m: (0, 0, 0, 0), pipeline_mode=pl.Buffered(1)),
                  row, pl.BlockSpec((1, Dm), lambda m: (0, 0))],
        out_specs=[row, row], out_shape=[_sds((M, Dm), F32), _sds((M, Dm), BF16)],
        compiler_params=_params("parallel"),
    )(a, w, res, gain)


def mm_res_loss(a, w, res, target, name, tm=512):
    M, K = a.shape
    Dm = w.shape[-1]

    def body(a_ref, w_ref, r_ref, t_ref, d_ref, db_ref, s_ref):
        e = (r_ref[...] + jnp.dot(a_ref[...], w_ref[...], preferred_element_type=F32)) - t_ref[...]
        d = e * (1.0 / Dm)
        d_ref[...] = d
        db_ref[...] = d.astype(BF16)
        part = jnp.sum(e * e, axis=0, keepdims=True)

        @pl.when(pl.program_id(0) == 0)
        def _():
            s_ref[...] = part

        @pl.when(pl.program_id(0) > 0)
        def _():
            s_ref[...] += part

    row = pl.BlockSpec((tm, Dm), lambda m: (m, 0))
    return pl.pallas_call(
        body, name=name, grid=(M // tm,),
        in_specs=[pl.BlockSpec((tm, K), lambda m: (m, 0)),
                  pl.BlockSpec((None, None, K, Dm), lambda m: (0, 0, 0, 0), pipeline_mode=pl.Buffered(1)), row, row],
        out_specs=[row, row, pl.BlockSpec((1, Dm), lambda m: (0, 0))],
        out_shape=[_sds((M, Dm), F32), _sds((M, Dm), BF16), _sds((1, Dm), F32)],
        compiler_params=_params("arbitrary"),
    )(a, w, res, target)


def mm_nt(dy, w, name, tr, layer=0, out_dtype=F32, tm=512):
    M = dy.shape[0]
    J, _, R, Ns = w.shape
    dims = (((1,), (1,)), ((), ()))

    def body(dy_ref, w_ref, o_ref):
        acc = None
        for j in range(J):
            p = lax.dot_general(dy_ref[:, j * Ns:(j + 1) * Ns], w_ref[j], dims, preferred_element_type=F32)
            acc = p if acc is None else acc + p
        o_ref[...] = acc.astype(o_ref.dtype)

    return pl.pallas_call(
        body, name=name, grid=(R // tr, M // tm),
        in_specs=[pl.BlockSpec((tm, J * Ns), lambda r, m: (m, 0)),
                  pl.BlockSpec((J, None, tr, Ns), lambda r, m: (0, layer, r, 0))],
        out_specs=pl.BlockSpec((tm, tr), lambda r, m: (m, r)),
        out_shape=_sds((M, R), out_dtype),
        compiler_params=_params("parallel", "parallel"),
    )(dy, w)


def mm_nt_norm_bwd(dy, w, x, g, dres, name, layer=0, tm=512):
    M = dy.shape[0]
    J, _, Dm, Ns = w.shape
    dims = (((1,), (1,)), ((), ()))

    def body(dy_ref, w_ref, x_ref, g_ref, r_ref, dx_ref, dxb_ref, dg_ref):
        dxn = None
        for j in range(J):
            p = lax.dot_general(dy_ref[:, j * Ns:(j + 1) * Ns], w_ref[j], dims, preferred_element_type=F32)
            dxn = p if dxn is None else dxn + p
        xv = x_ref[...]
        r = lax.rsqrt(jnp.mean(xv * xv, axis=-1, keepdims=True) + EPS)
        gx = dxn * g_ref[...]
        dot = jnp.sum(gx * xv, axis=-1, keepdims=True)
        dx = r_ref[...] + r * gx - xv * ((r * r * r) * (dot * (1.0 / Dm)))
        dx_ref[...] = dx
        dxb_ref[...] = dx.astype(BF16)
        part = jnp.sum(dxn * (xv * r), axis=0, keepdims=True)

        @pl.when(pl.program_id(0) == 0)
        def _():
            dg_ref[...] = part

        @pl.when(pl.program_id(0) > 0)
        def _():
            dg_ref[...] += part

    row = pl.BlockSpec((tm, Dm), lambda m: (m, 0))
    vec = pl.BlockSpec((1, Dm), lambda m: (0, 0))
    return pl.pallas_call(
        body, name=name, grid=(M // tm,),
        in_specs=[pl.BlockSpec((tm, J * Ns), lambda m: (m, 0)),
                  pl.BlockSpec((J, None, Dm, Ns), lambda m: (0, layer, 0, 0), pipeline_mode=pl.Buffered(1)), row, vec, row],
        out_specs=[row, row, vec],
        out_shape=[_sds((M, Dm), F32), _sds((M, Dm), BF16), _sds((1, Dm), F32)],
        compiler_params=_params("arbitrary"),
    )(dy, w, x, g, dres)


def mm_tn(a, dy, name, J, tk, tm=512, jb=None):
    M, K = a.shape
    jb = jb or J
    Ns = dy.shape[1] // J
    N = jb * Ns
    n_m = M // tm
    dims = (((0,), (0,)), ((), ()))

    def body(a_ref, dy_ref, o_ref, acc_ref):
        p = lax.dot_general(a_ref[...], dy_ref[...], dims, preferred_element_type=F32)
        m = pl.program_id(2)

        @pl.when(m == 0)
        def _():
            acc_ref[...] = p

        @pl.when(m > 0)
        def _():
            acc_ref[...] += p

        @pl.when(m == n_m - 1)
        def _():
            for j in range(jb):
                o_ref[j] = acc_ref[:, j * Ns:(j + 1) * Ns].astype(BF16)

    return pl.pallas_call(
        body, name=name, grid=(J // jb, K // tk, n_m),
        in_specs=[pl.BlockSpec((tm, tk), lambda g, k, m: (m, k)), pl.BlockSpec((tm, N), lambda g, k, m: (m, g))],
        out_specs=pl.BlockSpec((jb, tk, Ns), lambda g, k, m: (g, k, 0)),
        out_shape=_sds((J, K, Ns), BF16), scratch_shapes=[pltpu.VMEM((tk, N), F32)],
        compiler_params=_params("parallel", "parallel", "arbitrary"),
    )(a, dy)


HALO = 16


def _shift_down(x, s):
    return pltpu.roll(x, s, 0)


def _shift_up(x, s):
    return pltpu.roll(x, x.shape[0] - s, 0)


def _conv3(z, cw):
    return (_shift_down(z, 2) * cw[0:1] + _shift_down(z, 1) * cw[1:2]) + z * cw[2:3]


def _window_count(first_row, n, k):
    t = first_row + lax.broadcasted_iota(jnp.int32, (n, 1), 0)
    return jnp.clip(t + 1, 1, k).astype(F32)


def in_mixer_fwd(xn, w_in, conv_w, pool_w, pool_scale, name, ts=512):
    S, K = xn.shape
    n = ts + HALO

    def body(xm_ref, xb_ref, w_ref, cw_ref, pw_ref, ps_ref, p_ref, o_ref):
        i = pl.program_id(0)
        before = jnp.where(i > 0, xb_ref[...], jnp.zeros_like(xb_ref))
        rows = jnp.concatenate([before, xm_ref[...]], axis=0)
        h, gb, gc, pin = [jnp.dot(rows, w_ref[j], preferred_element_type=F32) for j in range(N_CHIPS)]
        for j, part in enumerate((h, gb, gc, pin)):
            p_ref[:, j * A_WIDTH:(j + 1) * A_WIDTH] = part[HALO:]
        cz = _conv3(gc * h, cw_ref[...])
        o_ref[:, 0:A_WIDTH] = (gb[HALO:] * cz[HALO:]).astype(BF16)
        for g, k in enumerate(POOL_WINDOWS):
            p = pin[:, g * POOL_GROUP:(g + 1) * POOL_GROUP]
            w = p
            s = 1
            while s < k:
                w = w + _shift_down(w, s)
                s *= 2
            pooled = w / _window_count(i * ts - HALO, n, k) - p
            yb = jnp.dot(pooled[HALO:].astype(BF16), pw_ref[g], preferred_element_type=F32)
            yb = yb * ps_ref[:, g * POOL_GROUP:(g + 1) * POOL_GROUP]
            o_ref[:, A_WIDTH + g * POOL_GROUP:A_WIDTH + (g + 1) * POOL_GROUP] = yb.astype(BF16)

    hb = ts // HALO
    return pl.pallas_call(
        body, name=name, grid=(S // ts,),
        in_specs=[
            pl.BlockSpec((ts, K), lambda i: (i, 0)),
            pl.BlockSpec((HALO, K), lambda i: (jnp.maximum(i * hb - 1, 0), 0)),
            pl.BlockSpec((N_CHIPS, None, K, A_WIDTH), lambda i: (0, 0, 0, 0), pipeline_mode=pl.Buffered(1)),
            pl.BlockSpec((3, A_WIDTH), lambda i: (0, 0)),
            pl.BlockSpec((4, POOL_GROUP, POOL_GROUP), lambda i: (0, 0, 0)),
            pl.BlockSpec((1, 4 * POOL_GROUP), lambda i: (0, 0)),
        ],
        out_specs=[pl.BlockSpec((ts, EVEN_IN), lambda i: (i, 0)), pl.BlockSpec((ts, D_MODEL), lambda i: (i, 0))],
        out_shape=[_sds((S, EVEN_IN), F32), _sds((S, D_MODEL), BF16)], compiler_params=_params("parallel"),
    )(xn, xn, w_in, conv_w, pool_w, pool_scale)


def mixer_bwd(proj, dmix, conv_w, pool_w, pool_scale, name, ts=256):
    S = proj.shape[0]
    n = ts + 2 * HALO
    nt = S // ts
    tn_dims = (((0,), (0,)), ((), ()))
    nt_dims = (((1,), (1,)), ((), ()))

    def body(pm_ref, pb_ref, pa_ref, dm_ref, da_ref, cw_ref, pw_ref, ps_ref, o_ref, dcw_ref, dpw_ref, dps_ref):
        i = pl.program_id(0)
        last = i == nt - 1
        before = jnp.where(i > 0, pb_ref[...], 0.0)
        after = jnp.where(last, 0.0, pa_ref[...])
        ext = jnp.concatenate([before, pm_ref[...], after], axis=0)
        dafter = jnp.where(last, 0.0, da_ref[...])
        dext = jnp.concatenate([jnp.zeros((HALO, D_MODEL), F32), dm_ref[...], dafter], axis=0)
        cw = cw_ref[...]
        main = slice(HALO, HALO + ts)

        @pl.when(i == 0)
        def _():
            dcw_ref[...] = jnp.zeros_like(dcw_ref)
            dpw_ref[...] = jnp.zeros_like(dpw_ref)
            dps_ref[...] = jnp.zeros_like(dps_ref)

        h, gb, gc = ext[:, 0:A_WIDTH], ext[:, A_WIDTH:2 * A_WIDTH], ext[:, 2 * A_WIDTH:3 * A_WIDTH]
        z = gc * h
        z1, z2 = _shift_down(z, 1), _shift_down(z, 2)
        cz = (z2 * cw[0:1] + z1 * cw[1:2]) + z * cw[2:3]
        dya = dext[:, 0:A_WIDTH]
        dcz = dya * gb
        dz = dcz * cw[2:3] + _shift_up(dcz, 1) * cw[1:2] + _shift_up(dcz, 2) * cw[0:1]
        o_ref[:, 0:A_WIDTH] = (dz * gc)[main].astype(BF16)
        o_ref[:, A_WIDTH:2 * A_WIDTH] = (dya * cz)[main].astype(BF16)
        o_ref[:, 2 * A_WIDTH:3 * A_WIDTH] = (dz * h)[main].astype(BF16)
        dczm = dcz[main]
        dcw_ref[0:1, :] += jnp.sum(dczm * z2[main], axis=0, keepdims=True)
        dcw_ref[1:2, :] += jnp.sum(dczm * z1[main], axis=0, keepdims=True)
        dcw_ref[2:3, :] += jnp.sum(dczm * z[main], axis=0, keepdims=True)

        for g, k in enumerate(POOL_WINDOWS):
            lo = 3 * A_WIDTH + g * POOL_GROUP
            cols = slice(g * POOL_GROUP, (g + 1) * POOL_GROUP)
            p = ext[:, lo:lo + POOL_GROUP]
            w = p
            s = 1
            while s < k:
                w = w + _shift_down(w, s)
                s *= 2
            cnt = _window_count(i * ts - HALO, n, k)
            pooled = (w / cnt - p)[main].astype(BF16)
            dyb = dext[:, A_WIDTH + g * POOL_GROUP:A_WIDTH + (g + 1) * POOL_GROUP]
            e = dyb * ps_ref[:, cols]
            pre = jnp.dot(pooled, pw_ref[g], preferred_element_type=F32)
            dps_ref[:, cols] += jnp.sum(dyb[main] * pre, axis=0, keepdims=True)
            dpw_ref[g] += lax.dot_general(pooled, e[main].astype(BF16), tn_dims, preferred_element_type=F32)
            dpooled = lax.dot_general(e.astype(BF16), pw_ref[g], nt_dims, preferred_element_type=F32)
            q = dpooled / cnt
            a = q
            s = 1
            while s < k:
                a = a + _shift_up(a, s)
                s *= 2
            o_ref[:, lo:lo + POOL_GROUP] = (a - dpooled)[main].astype(BF16)

    hb = ts // HALO
    nh = S // HALO
    before_map = lambda i: (jnp.maximum(i * hb - 1, 0), 0)
    after_map = lambda i: (jnp.minimum((i + 1) * hb, nh - 1), 0)
    full = lambda *shape: pl.BlockSpec(shape, lambda i: (0,) * len(shape))
    return pl.pallas_call(
        body, name=name, grid=(nt,),
        in_specs=[
            pl.BlockSpec((ts, EVEN_IN), lambda i: (i, 0)),
            pl.BlockSpec((HALO, EVEN_IN), before_map),
            pl.BlockSpec((HALO, EVEN_IN), after_map),
            pl.BlockSpec((ts, D_MODEL), lambda i: (i, 0)),
            pl.BlockSpec((HALO, D_MODEL), after_map),
            full(3, A_WIDTH), full(4, POOL_GROUP, POOL_GROUP), full(1, 4 * POOL_GROUP),
        ],
        out_specs=[pl.BlockSpec((ts, EVEN_IN), lambda i: (i, 0)), full(3, A_WIDTH), full(4, POOL_GROUP, POOL_GROUP),
                   full(1, 4 * POOL_GROUP)],
        out_shape=[_sds((S, EVEN_IN), BF16), _sds((3, A_WIDTH), F32), _sds((4, POOL_GROUP, POOL_GROUP), F32),
                   _sds((1, 4 * POOL_GROUP), F32)],
        compiler_params=_params("arbitrary"),
    )(proj, proj, proj, dmix, dmix, conv_w, pool_w, pool_scale)


FFN_HALO = 16
FFN_TC = 1408


GLU_CHUNKS = ((0, 512), (512, 512), (1024, 384))


def up_glu_fwd(xn, w_up, conv_w, conv_b, name, tm=512):
    S, K = xn.shape
    nc = D_FF // FFN_TC

    def body(xm_ref, xb_ref, wg_ref, wu_ref, cwg_ref, cwu_ref, cbg_ref, cbu_ref, pg_ref, pu_ref, ug_ref, uu_ref, o_ref):
        before = jnp.where(pl.program_id(1) > 0, xb_ref[...], jnp.zeros_like(xb_ref))
        rows = jnp.concatenate([before, xm_ref[...]], axis=0)
        for lo, width in GLU_CHUNKS:
            cols = slice(lo, lo + width)
            pre_g = jnp.dot(rows, wg_ref[:, cols], preferred_element_type=F32)
            pre_u = jnp.dot(rows, wu_ref[:, cols], preferred_element_type=F32)
            gate = _conv3(pre_g, cwg_ref[:, cols])[FFN_HALO:] + cbg_ref[:, cols]
            upv = _conv3(pre_u, cwu_ref[:, cols])[FFN_HALO:] + cbu_ref[:, cols]
            pg_ref[:, cols] = pre_g[FFN_HALO:].astype(BF16)
            pu_ref[:, cols] = pre_u[FFN_HALO:].astype(BF16)
            ug_ref[:, cols] = gate.astype(BF16)
            uu_ref[:, cols] = upv.astype(BF16)
            o_ref[:, cols] = ((gate * (1.0 / (1.0 + jnp.exp(-gate)))) * upv).astype(BF16)

    hb = tm // FFN_HALO
    wspec = lambda off: pl.BlockSpec((None, None, K, FFN_TC), lambda j, m: (j + off, 0, 0, 0))
    cw = lambda off: pl.BlockSpec((3, FFN_TC), lambda j, m: (0, j + off))
    cb = lambda off: pl.BlockSpec((1, FFN_TC), lambda j, m: (0, j + off))
    out = pl.BlockSpec((tm, FFN_TC), lambda j, m: (m, j))
    pg, pu, ug, uu, act = pl.pallas_call(
        body, name=name, grid=(nc, S // tm),
        in_specs=[pl.BlockSpec((tm, K), lambda j, m: (m, 0)),
                  pl.BlockSpec((FFN_HALO, K), lambda j, m: (jnp.maximum(m * hb - 1, 0), 0)),
                  wspec(0), wspec(nc), cw(0), cw(nc), cb(0), cb(nc)],
        out_specs=[out] * 5, out_shape=[_sds((S, D_FF), BF16)] * 5,
        compiler_params=_params("parallel", "parallel"),
    )(xn, xn, w_up, w_up, conv_w, conv_w, conv_b, conv_b)
    return (pg, pu), (ug, uu), act


def glu_bwd(up, u, da, conv_w, name, ts=256):
    S = up[0].shape[0]
    nc = D_FF // FFN_TC
    nt = S // ts
    W = 2 * D_FF

    def body(xg_ref, xu_ref, gm_ref, ga_ref, um_ref, ua_ref, dm_ref, da_ref, cw_ref, dx_ref, dcw_ref, dcb_ref):
        i = pl.program_id(0)
        last = i == nt - 1

        @pl.when(i == 0)
        def _():
            dcw_ref[...] = jnp.zeros_like(dcw_ref)
            dcb_ref[...] = jnp.zeros_like(dcb_ref)

        def rows(m_ref, a_ref, cols):
            return jnp.concatenate([m_ref[:, cols], a_ref[:, cols]], axis=0).astype(F32)

        def back(d, x, cols):
            cw = cw_ref[:, cols]
            d1, d2 = _shift_up(d, 1), _shift_up(d, 2)
            dx_ref[:, cols] = ((d * cw[2:3] + d1 * cw[1:2]) + d2 * cw[0:1])[:ts].astype(BF16)
            dcb_ref[:, cols] += jnp.sum(d[:ts], axis=0, keepdims=True)
            dcw_ref[0:1, cols] += jnp.sum(d2[:ts] * x, axis=0, keepdims=True)
            dcw_ref[1:2, cols] += jnp.sum(d1[:ts] * x, axis=0, keepdims=True)
            dcw_ref[2:3, cols] += jnp.sum(d[:ts] * x, axis=0, keepdims=True)

        for c in range(nc):
            cols = slice(c * FFN_TC, (c + 1) * FFN_TC)
            ug, uu = rows(gm_ref, ga_ref, cols), rows(um_ref, ua_ref, cols)
            dae = rows(dm_ref, da_ref, cols)
            dae = jnp.where(last & (lax.broadcasted_iota(jnp.int32, dae.shape, 0) >= ts), 0.0, dae)
            sg = 1.0 / (1.0 + jnp.exp(-ug))
            duu = dae * (ug * sg)
            dug = (dae * uu) * (sg * (1.0 + ug * (1.0 - sg)))
            back(dug, xg_ref[:, cols].astype(F32), cols)
            back(duu, xu_ref[:, cols].astype(F32), slice(D_FF + c * FFN_TC, D_FF + (c + 1) * FFN_TC))

    hb = ts // FFN_HALO
    nh = S // FFN_HALO
    after_map = lambda i: (jnp.minimum((i + 1) * hb, nh - 1), 0)
    main = pl.BlockSpec((ts, D_FF), lambda i: (i, 0))
    after = pl.BlockSpec((FFN_HALO, D_FF), after_map)
    return pl.pallas_call(
        body, name=name, grid=(nt,),
        in_specs=[main, main, main, after, main, after, main, after, pl.BlockSpec((3, W), lambda i: (0, 0))],
        out_specs=[pl.BlockSpec((ts, W), lambda i: (i, 0)), pl.BlockSpec((3, W), lambda i: (0, 0)),
                   pl.BlockSpec((1, W), lambda i: (0, 0))],
        out_shape=[_sds((S, W), BF16), _sds((3, W), F32), _sds((1, W), F32)],
        compiler_params=_params("arbitrary"),
    )(up[0], up[1], u[0], u[0], u[1], u[1], da, da, conv_w)


MEAN_GROUP = 256


def _head_mean_matrix():
    h = np.arange(MEAN_GROUP) // HEAD_DIM
    return jnp.asarray((h[:, None] == h[None, :]).astype(np.float32) / HEAD_DIM, dtype=BF16)


def _head_mean(v, gm):
    vb = v.astype(BF16)
    return jnp.concatenate([jnp.dot(vb[:, c:c + MEAN_GROUP], gm, preferred_element_type=F32)
                            for c in range(0, v.shape[1], MEAN_GROUP)], axis=1)


def qknorm_fwd(qkv, gqk, name, ts=512):
    S = qkv.shape[0]

    def body(x_ref, g_ref, gm_ref, o_ref):
        part = pl.program_id(0)
        x = x_ref[...]

        @pl.when(part < 2)
        def _():
            r = lax.rsqrt(_head_mean(x * x, gm_ref[...]) + EPS)
            o_ref[...] = ((x * r) * g_ref[...]).astype(BF16)

        @pl.when(part == 2)
        def _():
            o_ref[...] = x.astype(BF16)

    return pl.pallas_call(
        body, name=name, grid=(3, S // ts),
        in_specs=[pl.BlockSpec((ts, D_MODEL), lambda p, i: (i, p)), pl.BlockSpec((None, 1, D_MODEL), lambda p, i: (p, 0, 0)),
                  pl.BlockSpec((MEAN_GROUP, MEAN_GROUP), lambda p, i: (0, 0))],
        out_specs=pl.BlockSpec((ts, D_MODEL), lambda p, i: (i, p)),
        out_shape=_sds((S, 3 * D_MODEL), BF16), compiler_params=_params("parallel", "parallel"),
    )(qkv, gqk, _head_mean_matrix())


def qknorm_bwd(qkv, dq, dk, dv, gqk, name, ts=256):
    S = qkv.shape[0]

    def body(x_ref, dq_ref, dk_ref, dv_ref, g_ref, gm_ref, o_ref, dg_ref):
        @pl.when(pl.program_id(0) == 0)
        def _():
            dg_ref[...] = jnp.zeros_like(dg_ref)

        gm = gm_ref[...]
        for part, d_ref in enumerate((dq_ref, dk_ref)):
            cols = slice(part * D_MODEL, (part + 1) * D_MODEL)
            x = x_ref[:, cols]
            d = d_ref[...]
            r = lax.rsqrt(_head_mean(x * x, gm) + EPS)
            gx = d * g_ref[part]
            o_ref[:, cols] = (r * gx - x * ((r * r * r) * _head_mean(gx * x, gm))).astype(BF16)
            dg_ref[part] += jnp.sum(d * (x * r), axis=0, keepdims=True)
        o_ref[:, 2 * D_MODEL:] = dv_ref[...].astype(BF16)

    row = pl.BlockSpec((ts, D_MODEL), lambda i: (i, 0))
    wide = pl.BlockSpec((ts, 3 * D_MODEL), lambda i: (i, 0))
    gains = pl.BlockSpec((3, 1, D_MODEL), lambda i: (0, 0, 0))
    return pl.pallas_call(
        body, name=name, grid=(S // ts,),
        in_specs=[wide, row, row, row, gains, pl.BlockSpec((MEAN_GROUP, MEAN_GROUP), lambda i: (0, 0))],
        out_specs=[wide, gains],
        out_shape=[_sds((S, 3 * D_MODEL), BF16), _sds((3, 1, D_MODEL), F32)],
        compiler_params=_params("arbitrary"),
    )(qkv, dq, dk, dv, gqk, _head_mean_matrix())


RESIDUES = 16


def _block_order(dil):
    runs = RESIDUES // dil
    slot = np.arange(ATT_BLOCK)
    return (slot % (ATT_BLOCK // runs)) * runs + slot // (ATT_BLOCK // runs)


def _bucket_tables():
    n = ATT_BLOCK
    max_exact = N_REL_BUCKETS // 2
    buckets, valids = [], []
    for _, dil in DILATED_PAIRS:
        order = _block_order(dil)
        a = order[:, None]
        c = np.concatenate([order, n + order])[None, :]
        first_half = (np.arange(2 * n) < n)[None, :]
        rel = a + n - c
        band = (rel >= 0) & (rel <= n)
        dist = np.clip(rel, 0, n) * dil
        dd = np.maximum(dist, 1).astype(np.float32)
        large = max_exact + (np.log(dd / np.float32(max_exact)) / np.float32(math.log(REL_MAX_DISTANCE / max_exact))
                             * np.float32(N_REL_BUCKETS - max_exact)).astype(np.int32)
        large = np.minimum(large, N_REL_BUCKETS - 1)
        buckets.append(np.where(dist < max_exact, dist, large).reshape(1, -1))
        valids.append(np.stack([(band & ~first_half).reshape(1, -1), band.reshape(1, -1)]))
    return np.stack(buckets).astype(np.int32), np.stack(valids).astype(np.int32)


BIAS_CHUNK = 8192


def _split3(x):
    a = x.astype(BF16)
    r = x - a.astype(F32)
    b = r.astype(BF16)
    c = (r - b.astype(F32)).astype(BF16)
    return a, b, c


def bias_expand(rel_bias_t, name):
    bucket, valid = _bucket_tables()
    nq = bucket.shape[-1]

    def body(t_ref, b_ref, v_ref, o_ref):
        onehot = (lax.broadcasted_iota(jnp.int32, (N_REL_BUCKETS, BIAS_CHUNK), 0) == b_ref[...]).astype(BF16)
        acc = None
        for term in _split3(t_ref[...]):
            p = jnp.dot(term, onehot, preferred_element_type=F32)
            acc = p if acc is None else acc + p
        o_ref[...] = jnp.where(v_ref[...] > 0, acc, MASK_VALUE)

    return pl.pallas_call(
        body, name=name, grid=(3, 2, nq // BIAS_CHUNK),
        in_specs=[pl.BlockSpec((N_HEADS, N_REL_BUCKETS), lambda b, v, c: (0, 0)),
                  pl.BlockSpec((None, 1, BIAS_CHUNK), lambda b, v, c: (b, 0, c)),
                  pl.BlockSpec((None, None, 1, BIAS_CHUNK), lambda b, v, c: (b, v, 0, c))],
        out_specs=pl.BlockSpec((None, None, N_HEADS, BIAS_CHUNK), lambda b, v, c: (b, v, 0, c)),
        out_shape=_sds((3, 2, N_HEADS, nq), F32), compiler_params=_params("parallel", "parallel", "parallel"),
    )(rel_bias_t, jnp.asarray(bucket), jnp.asarray(valid))


def bias_reduce(dbias, name):
    bucket, _ = _bucket_tables()
    nq = bucket.shape[-1]
    dims = (((1,), (1,)), ((), ()))

    def body(d_ref, b_ref, o_ref):
        onehot = (lax.broadcasted_iota(jnp.int32, (N_REL_BUCKETS, BIAS_CHUNK), 0) == b_ref[...]).astype(BF16)
        acc = None
        for term in _split3(d_ref[...]):
            p = lax.dot_general(term, onehot, dims, preferred_element_type=F32)
            acc = p if acc is None else acc + p

        @pl.when(pl.program_id(1) == 0)
        def _():
            o_ref[...] = acc

        @pl.when(pl.program_id(1) > 0)
        def _():
            o_ref[...] += acc

    return pl.pallas_call(
        body, name=name, grid=(3, nq // BIAS_CHUNK),
        in_specs=[pl.BlockSpec((None, N_HEADS, BIAS_CHUNK), lambda b, c: (b, 0, c)),
                  pl.BlockSpec((None, 1, BIAS_CHUNK), lambda b, c: (b, 0, c))],
        out_specs=pl.BlockSpec((None, N_HEADS, N_REL_BUCKETS), lambda b, c: (b, 0, 0)),
        out_shape=_sds((3, N_HEADS, N_REL_BUCKETS), F32), compiler_params=_params("parallel", "arbitrary"),
    )(dbias, jnp.asarray(bucket))


PAIR = 2 * HEAD_DIM
N_PAIRS = N_HEADS // 2
_NT = (((1,), (1,)), ((), ()))
_TN = (((0,), (0,)), ((), ()))


def _low_lanes(shape):
    return lax.broadcasted_iota(jnp.int32, shape, 1) < HEAD_DIM


ATTN_VMEM_LIMIT_BYTES = 56 * 1024 * 1024
BRANCH_ORDER = (2, 1, 0)


def _regroup(dst, src, L16):
    for r in range(RESIDUES):
        dst[pl.ds(r * L16, L16), :] = src[pl.ds(r, L16, stride=RESIDUES), :]


def _ungroup(dst, src, L16):
    for r in range(RESIDUES):
        dst[pl.ds(r, L16, stride=RESIDUES), :] = src[pl.ds(r * L16, L16), :]


def _branch_geometry(branch, S):
    dil = DILATED_PAIRS[branch][1]
    runs = RESIDUES // dil
    return dil, runs, ATT_BLOCK // runs, S // dil // ATT_BLOCK


def _block_rows(it, branch, S):
    dil, runs, run_len, n_blocks = _branch_geometry(branch, S)
    L16 = S // RESIDUES
    r, b = it // n_blocks, it % n_blocks
    prev = jnp.maximum(b - 1, 0)
    cur_rows = [pl.multiple_of((j * dil + r) * L16 + run_len * b, 8) for j in range(runs)]
    prev_rows = [pl.multiple_of((j * dil + r) * L16 + run_len * prev, 8) for j in range(runs)]
    return cur_rows, prev_rows, jnp.minimum(b, 1)


def _load_block(ref, rows, run_len):
    parts = [ref[pl.ds(o, run_len), :] for o in rows]
    return parts[0] if len(parts) == 1 else jnp.concatenate(parts, axis=0)


def _store_block(ref, rows, run_len, value, add=False):
    for j, o in enumerate(rows):
        part = value[j * run_len:(j + 1) * run_len]
        if add:
            ref[pl.ds(o, run_len), :] += part
        else:
            ref[pl.ds(o, run_len), :] = part


ATTN_FWD_UNROLL = 8
ATTN_BWD_UNROLL = 4


def _stack_heads(x, low):
    zero = jnp.zeros_like(x)
    return jnp.concatenate([jnp.where(low, x, zero), jnp.where(low, zero, x)], axis=0)


def _unstack_heads(y, low):
    return jnp.where(low, y[:ATT_BLOCK], y[ATT_BLOCK:])


def attn_fwd(qkvn, bias, name):
    S = qkvn.shape[0]
    L16 = S // RESIDUES
    n_iter = S // ATT_BLOCK

    def body(q_ref, k_ref, v_ref, b_ref, o_ref, lse_ref, stage, qp, kp, vp, acc_s, m_s, l_s):
        for src, dst in ((q_ref, qp), (k_ref, kp), (v_ref, vp)):
            stage[...] = src[...].astype(F32)
            _regroup(dst, stage, L16)
        low = _low_lanes((ATT_BLOCK, PAIR))

        for branch in BRANCH_ORDER:
            _, _, run_len, _ = _branch_geometry(branch, S)
            first = branch == BRANCH_ORDER[0]

            def step(it, carry, branch=branch, run_len=run_len, first=first):
                cur, prev, variant = _block_rows(it, branch, S)
                q = _load_block(qp, cur, run_len).astype(BF16)
                k = jnp.concatenate([_load_block(kp, prev, run_len), _load_block(kp, cur, run_len)], axis=0).astype(BF16)
                v = jnp.concatenate([_load_block(vp, prev, run_len), _load_block(vp, cur, run_len)], axis=0).astype(BF16)
                s = lax.dot_general(_stack_heads(q, low), k, _NT, preferred_element_type=F32) * (HEAD_DIM ** -0.5)
                s = s + b_ref[2 * branch + variant].reshape(2 * ATT_BLOCK, 2 * ATT_BLOCK)
                mx = jnp.max(s, axis=-1, keepdims=True)
                p = jnp.exp(s - mx)
                den = jnp.sum(p, axis=-1, keepdims=True)
                pv = jnp.dot(p.astype(BF16), v, preferred_element_type=F32)
                acc = _unstack_heads(pv, low)
                m = _unstack_heads(mx, low)
                l = _unstack_heads(den, low)
                if not first:
                    m_old = _load_block(m_s, cur, run_len)
                    m_new = jnp.maximum(m_old, m)
                    a_old, a_new = jnp.exp(m_old - m_new), jnp.exp(m - m_new)
                    acc = _load_block(acc_s, cur, run_len) * a_old + acc * a_new
                    l = _load_block(l_s, cur, run_len) * a_old + l * a_new
                    m = m_new
                _store_block(acc_s, cur, run_len, acc)
                _store_block(m_s, cur, run_len, m)
                _store_block(l_s, cur, run_len, l)
                return carry

            lax.fori_loop(0, n_iter, step, 0, unroll=ATTN_FWD_UNROLL)

        acc_s[...] = acc_s[...] / l_s[...]
        _ungroup(stage, acc_s, L16)
        o_ref[...] = stage[...].astype(BF16)
        m_s[...] = m_s[...] + jnp.log(l_s[...])
        _ungroup(lse_ref, m_s, L16)

    col = lambda part: pl.BlockSpec((S, PAIR), lambda hp: (0, part * N_PAIRS + hp))
    out = pl.BlockSpec((S, PAIR), lambda hp: (0, hp))
    return pl.pallas_call(
        body, name=name, grid=(N_PAIRS,),
        in_specs=[col(0), col(1), col(2), pl.BlockSpec((6, 2, ATT_BLOCK, 2 * ATT_BLOCK), lambda hp: (0, hp, 0, 0))],
        out_specs=[out, out], out_shape=[_sds((S, D_MODEL), BF16), _sds((S, D_MODEL), F32)],
        scratch_shapes=[pltpu.VMEM((S, PAIR), F32)] * 7,
        compiler_params=pltpu.CompilerParams(dimension_semantics=("parallel",), vmem_limit_bytes=ATTN_VMEM_LIMIT_BYTES),
    )(qkvn, qkvn, qkvn, bias)


def attn_bwd(qkvn, att, datt, lse, bias, name):
    S = qkvn.shape[0]
    L16 = S // RESIDUES
    n_iter = S // ATT_BLOCK
    TILE = 512

    def body(q_ref, k_ref, v_ref, o_ref, do_ref, lse_ref, b_ref, dq_ref, dk_ref, dv_ref, db_ref,
             qp, kp, vp, dop, ldp, dqp, dkp, dvp):
        stage = dqp
        for src, dst in ((q_ref, qp), (k_ref, kp), (v_ref, vp), (do_ref, dop)):
            stage[...] = src[...].astype(F32)
            _regroup(dst, stage, L16)

        def pack(i, carry):
            rows = pl.ds(pl.multiple_of(i * TILE, TILE), TILE)
            low = _low_lanes((TILE, PAIR))
            lane = lax.broadcasted_iota(jnp.int32, (TILE, PAIR), 1)
            prod = do_ref[rows, :].astype(F32) * o_ref[rows, :].astype(F32)
            d0 = jnp.sum(jnp.where(low, prod, 0.0), axis=-1, keepdims=True)
            d1 = jnp.sum(jnp.where(low, 0.0, prod), axis=-1, keepdims=True)
            stage[rows, :] = jnp.where((lane & (HEAD_DIM // 2)) == 0, lse_ref[rows, :], jnp.where(low, d0, d1))
            return carry

        lax.fori_loop(0, S // TILE, pack, 0)
        _regroup(ldp, stage, L16)
        dqp[...] = jnp.zeros_like(dqp)
        dkp[...] = jnp.zeros_like(dkp)
        dvp[...] = jnp.zeros_like(dvp)
        db_ref[...] = jnp.zeros_like(db_ref)
        low = _low_lanes((ATT_BLOCK, PAIR))

        for branch in BRANCH_ORDER:
            _, _, run_len, _ = _branch_geometry(branch, S)

            def step(it, carry, branch=branch, run_len=run_len):
                cur, prev, variant = _block_rows(it, branch, S)
                q = _load_block(qp, cur, run_len).astype(BF16)
                dout = _load_block(dop, cur, run_len).astype(BF16)
                ld = _load_block(ldp, cur, run_len)
                k = jnp.concatenate([_load_block(kp, prev, run_len), _load_block(kp, cur, run_len)], axis=0).astype(BF16)
                v = jnp.concatenate([_load_block(vp, prev, run_len), _load_block(vp, cur, run_len)], axis=0).astype(BF16)
                half = HEAD_DIM // 2
                lse2 = jnp.concatenate([ld[:, 0:1], ld[:, HEAD_DIM:HEAD_DIM + 1]], axis=0)
                delta2 = jnp.concatenate([ld[:, half:half + 1], ld[:, HEAD_DIM + half:HEAD_DIM + half + 1]], axis=0)
                q2, do2 = _stack_heads(q, low), _stack_heads(dout, low)
                s = lax.dot_general(q2, k, _NT, preferred_element_type=F32) * (HEAD_DIM ** -0.5)
                p = jnp.exp(s + b_ref[2 * branch + variant].reshape(2 * ATT_BLOCK, 2 * ATT_BLOCK) - lse2)
                dp = lax.dot_general(do2, v, _NT, preferred_element_type=F32)
                ds = p * (dp - delta2)
                db_ref[branch] += ds.reshape(2, ATT_BLOCK, 2 * ATT_BLOCK)
                dsb = (ds * (HEAD_DIM ** -0.5)).astype(BF16)
                dq = _unstack_heads(jnp.dot(dsb, k, preferred_element_type=F32), low)
                dk = lax.dot_general(dsb, q2, _TN, preferred_element_type=F32)
                dv = lax.dot_general(p.astype(BF16), do2, _TN, preferred_element_type=F32)
                _store_block(dqp, cur, run_len, dq, add=True)
                _store_block(dkp, prev, run_len, dk[:ATT_BLOCK], add=True)
                _store_block(dvp, prev, run_len, dv[:ATT_BLOCK], add=True)
                _store_block(dkp, cur, run_len, dk[ATT_BLOCK:], add=True)
                _store_block(dvp, cur, run_len, dv[ATT_BLOCK:], add=True)
                return carry

            lax.fori_loop(0, n_iter, step, 0, unroll=ATTN_BWD_UNROLL)

        _ungroup(dq_ref, dqp, L16)
        _ungroup(dk_ref, dkp, L16)
        _ungroup(dv_ref, dvp, L16)

    col = lambda part: pl.BlockSpec((S, PAIR), lambda hp: (0, part * N_PAIRS + hp))
    one = pl.BlockSpec((S, PAIR), lambda hp: (0, hp))
    return pl.pallas_call(
        body, name=name, grid=(N_PAIRS,),
        in_specs=[col(0), col(1), col(2), one, one, one,
                  pl.BlockSpec((6, 2, ATT_BLOCK, 2 * ATT_BLOCK), lambda hp: (0, hp, 0, 0))],
        out_specs=[one, one, one, pl.BlockSpec((3, 2, ATT_BLOCK, 2 * ATT_BLOCK), lambda hp: (0, hp, 0, 0))],
        out_shape=[_sds((S, D_MODEL), F32)] * 3 + [_sds((3, N_HEADS, ATT_BLOCK, 2 * ATT_BLOCK), F32)],
        scratch_shapes=[pltpu.VMEM((S, PAIR), F32)] * 8,
        compiler_params=pltpu.CompilerParams(dimension_semantics=("parallel",), vmem_limit_bytes=ATTN_VMEM_LIMIT_BYTES),
    )(qkvn, qkvn, qkvn, att, datt, lse, bias)


def adamw(w, g, m, v, name):
    n, R, C = w.shape

    def body(w_ref, g_ref, m_ref, v_ref, d_ref, nm_ref, nv_ref, go_ref):
        gv = g_ref[...]
        go_ref[...] = gv
        m2 = ADAM_B1 * m_ref[...] + (1.0 - ADAM_B1) * gv
        v2 = ADAM_B2 * v_ref[...] + (1.0 - ADAM_B2) * (gv * gv)
        m_hat = m2 / (1.0 - ADAM_B1 ** ADAM_STEP)
        v_hat = v2 / (1.0 - ADAM_B2 ** ADAM_STEP)
        d_ref[...] = -ADAM_LR * (m_hat / (jnp.sqrt(v_hat) + ADAM_EPS) + ADAM_WD * w_ref[...])
        nm_ref[...] = m2
        nv_ref[...] = v2

    tr = R
    while tr * C * 4 > ELEMENTWISE_BLOCK_BYTES and tr % 16 == 0:
        tr //= 2
    spec = pl.BlockSpec((None, tr, C), lambda i, r: (i, r, 0))
    return pl.pallas_call(
        body, name=name, grid=(n, R // tr), in_specs=[spec] * 4, out_specs=[spec] * 4,
        out_shape=[_sds((n, R, C), F32)] * 4, compiler_params=_params("parallel", "parallel"),
    )(w, g, m, v)


ANY = pl.BlockSpec(memory_space=pl.ANY)


def _coords():
    return lax.axis_index("x"), lax.axis_index("y"), lax.axis_index("c")


def _other_chips(mx, my):
    return [(1 - mx, my), (mx, 1 - my), (1 - mx, 1 - my)]


def _remote(src, dst, send, recv, dev):
    return pltpu.make_async_remote_copy(src_ref=src, dst_ref=dst, send_sem=send, recv_sem=recv, device_id=dev,
                                        device_id_type=MESH)


HBM =pl.BlockSpec(memory_space=pltpu.HBM)
SEM = pl.BlockSpec(memory_space=pltpu.SEMAPHORE)
_SPLIT_COPY = pltpu.CompilerParams(has_side_effects=pltpu.SideEffectType.DATAFLOW_SIDE_EFFECTING)


def _in_hbm(a):
    return pltpu.with_memory_space_constraint(a, pltpu.HBM)


def cast_into_slot(w, layer, chip_core, name, dtype=BF16):
    _, _, hR, C = w.shape

    def body(s_ref, w_ref, o_ref):
        del s_ref
        o_ref[...] = w_ref[...].astype(dtype)

    grid_spec = pltpu.PrefetchScalarGridSpec(
        num_scalar_prefetch=1, grid=(2,),
        in_specs=[pl.BlockSpec((None, None, hR, C), lambda h, s: (layer, h, 0, 0))],
        out_specs=pl.BlockSpec((None, None, hR, C), lambda h, s: (s[0], h, 0, 0)))
    return pl.pallas_call(body, name=name, grid_spec=grid_spec, out_shape=_sds((N_CHIPS, 2, hR, C), dtype),
                          compiler_params=_params("parallel"))(chip_core, w)


def gather_start(lands, groups, name):
    n = len(lands)
    n_groups = len(groups)

    def body(*refs):
        ins = refs[:n]
        sems = refs[n:n + 2 * n_groups]
        token = refs[-1]
        mx, my, mc = _coords()
        chip = 2 * mx + my
        for g, members in enumerate(groups):
            send, recv = sems[2 * g], sems[2 * g + 1]
            for i, a in enumerate(members):
                mine = ins[a].at[chip, mc]
                for k, (px, py) in enumerate(_other_chips(mx, my)):
                    _remote(mine, mine, send.at[3 * i + k], recv.at[3 * i + k], (px, py, mc)).start()
        token[...] = jnp.zeros_like(token)

    sem_shapes = []
    for members in groups:
        sem_shapes += [pltpu.SemaphoreType.DMA((3 * len(members),))] * 2
    outs = pl.pallas_call(
        body, name=name, in_specs=[HBM] * n,
        out_specs=[SEM] * (2 * n_groups) + [HBM] * n + [pl.BlockSpec(memory_space=pltpu.VMEM)],
        out_shape=sem_shapes + [pltpu.HBM(a.shape, a.dtype) for a in lands] + [_sds((SUBLANES, LANES), F32)],
        input_output_aliases={a: 2 * n_groups + a for a in range(n)}, compiler_params=_SPLIT_COPY,
    )(*[_in_hbm(a) for a in lands])
    sems = [(outs[2 * g], outs[2 * g + 1]) for g in range(n_groups)]
    return sems, list(outs[2 * n_groups:2 * n_groups + n]), outs[-1]


def gather_forward(lands, sems, after, name):
    n = len(lands)

    def body(*refs):
        ins = refs[:n]
        send, recv = refs[n], refs[n + 1]
        fsend, frecv = refs[n + 3], refs[n + 4]
        mx, my, mc = _coords()
        for i in range(n):
            for k, (px, py) in enumerate(_other_chips(mx, my)):
                landed = ins[i].at[2 * px + py, mc]
                cp = _remote(landed, landed, send.at[3 * i + k], recv.at[3 * i + k], (px, py, mc))
                cp.wait_send()
                cp.wait_recv()
                _remote(landed, landed, fsend.at[3 * i + k], frecv.at[3 * i + k], (mx, my, 1 - mc)).start()

    outs = pl.pallas_call(
        body, name=name, in_specs=[HBM] * n + [SEM, SEM, ANY], out_specs=[SEM, SEM] + [HBM] * n,
        out_shape=[pltpu.SemaphoreType.DMA((3 * n,))] * 2 + [pltpu.HBM(a.shape, a.dtype) for a in lands],
        input_output_aliases={a: 2 + a for a in range(n)}, compiler_params=_SPLIT_COPY,
    )(*lands, sems[0], sems[1], after)
    return (outs[0], outs[1]), list(outs[2:])


def gather_wait(lands, sems, after, name):
    n = len(lands)

    def body(*refs):
        ins = refs[:n]
        fsend, frecv = refs[n], refs[n + 1]
        mx, my, mc = _coords()
        for i in range(n):
            for k, (px, py) in enumerate(_other_chips(mx, my)):
                theirs = ins[i].at[2 * px + py, 1 - mc]
                cp = _remote(theirs, theirs, fsend.at[3 * i + k], frecv.at[3 * i + k], (mx, my, 1 - mc))
                cp.wait_send()
                cp.wait_recv()

    outs = pl.pallas_call(
        body, name=name, in_specs=[HBM] * n + [SEM, SEM, ANY], out_specs=[HBM] * n,
        out_shape=[pltpu.HBM(a.shape, a.dtype) for a in lands],
        input_output_aliases={a: a for a in range(n)}, compiler_params=_SPLIT_COPY,
    )(*lands, sems[0], sems[1], after)
    return list(outs)


def _peers(mx, my, mc):
    return [(1 - mx if k & 4 else mx, 1 - my if k & 2 else my, 1 - mc if k & 1 else mc) for k in range(1, N_DEV)]


def devices_start(x, name):
    def body(x_ref, land_ref, send, recv, x_thru, land_thru):
        mx, my, mc = _coords()
        me = 4 * mx + 2 * my + mc
        for k, peer in enumerate(_peers(mx, my, mc)):
            _remote(x_ref, land_ref.at[me], send.at[k], recv.at[k], peer).start()

    land = lax.empty((N_DEV,) + x.shape, x.dtype)
    outs = pl.pallas_call(
        body, name=name, in_specs=[HBM, HBM], out_specs=[SEM, SEM, HBM, HBM],
        out_shape=[pltpu.SemaphoreType.DMA((N_DEV - 1,))] * 2 + [pltpu.HBM(x.shape, x.dtype), pltpu.HBM(land.shape, x.dtype)],
        input_output_aliases={0: 2, 1: 3}, compiler_params=_SPLIT_COPY,
    )(_in_hbm(x), _in_hbm(land))
    return (outs[0], outs[1]), outs[2], outs[3]


def devices_wait(x, land, sems, after, name):
    def body(x_ref, land_ref, send, recv, after_ref, x_thru, land_thru):
        mx, my, mc = _coords()
        for k, (px, py, pc) in enumerate(_peers(mx, my, mc)):
            cp = _remote(x_ref, land_ref.at[4 * px + 2 * py + pc], send.at[k], recv.at[k], (px, py, pc))
            cp.wait_send()
            cp.wait_recv()

    outs = pl.pallas_call(
        body, name=name, in_specs=[HBM, HBM, SEM, SEM, ANY], out_specs=[HBM, HBM],
        out_shape=[pltpu.HBM(x.shape, x.dtype), pltpu.HBM(land.shape, land.dtype)],
        input_output_aliases={0: 0, 1: 1}, compiler_params=_SPLIT_COPY,
    )(x, land, sems[0], sems[1], after)
    return outs[0], outs[1]


def device_sum(land, own, me, name):
    _, R, C = land.shape

    def body(s_ref, l_ref, o_ref_in, o_ref):
        acc = None
        for q in range(N_DEV):
            term = jnp.where(s_ref[0] == q, o_ref_in[...], l_ref[q])
            acc = term if acc is None else acc + term
        o_ref[...] = acc

    grid_spec = pltpu.PrefetchScalarGridSpec(
        num_scalar_prefetch=1, grid=(1,),
        in_specs=[pl.BlockSpec((N_DEV, R, C), lambda i, s: (0, 0, 0)), pl.BlockSpec((R, C), lambda i, s: (0, 0))],
        out_specs=pl.BlockSpec((R, C), lambda i, s: (0, 0)))
    return pl.pallas_call(body, name=name, grid_spec=grid_spec, out_shape=_sds((R, C), F32),
                          compiler_params=_params("arbitrary"))(me, land, own)


def reduce_send(grads, name):
    n = len(grads)

    def body(*refs):
        ins, lands = refs[:n], refs[n:2 * n]
        send, recv = refs[2 * n], refs[2 * n + 1]
        mx, my, mc = _coords()
        me = 4 * mx + 2 * my + mc
        for a in range(n):
            for k, (px, py, pc) in enumerate(_peers(mx, my, mc)):
                _remote(ins[a].at[2 * px + py, pc], lands[a].at[me], send.at[7 * a + k], recv.at[7 * a + k], (px, py, pc)).start()

    lands = [lax.empty((N_DEV,) + g.shape[2:], g.dtype) for g in grads]
    outs = pl.pallas_call(
        body, name=name, in_specs=[HBM] * (2 * n), out_specs=[SEM, SEM] + [HBM] * (2 * n),
        out_shape=[pltpu.SemaphoreType.DMA((7 * n,))] * 2 + [pltpu.HBM(a.shape, a.dtype) for a in grads + lands],
        input_output_aliases={a: 2 + a for a in range(2 * n)}, compiler_params=_SPLIT_COPY,
    )(*[_in_hbm(a) for a in grads + lands])
    return (outs[0], outs[1]), list(outs[2:2 + n]), list(outs[2 + n:])


def reduce_wait(grads, lands, sems, after, name):
    n = len(grads)

    def body(*refs):
        ins, zones = refs[:n], refs[n:2 * n]
        send, recv = refs[2 * n], refs[2 * n + 1]
        mx, my, mc = _coords()
        for a in range(n):
            for k, (px, py, pc) in enumerate(_peers(mx, my, mc)):
                cp = _remote(ins[a].at[2 * px + py, pc], zones[a].at[4 * px + 2 * py + pc], send.at[7 * a + k],
                             recv.at[7 * a + k], (px, py, pc))
                cp.wait_send()
                cp.wait_recv()

    outs = pl.pallas_call(
        body, name=name, in_specs=[HBM] * (2 * n) + [SEM, SEM, ANY], out_specs=[HBM] * (2 * n),
        out_shape=[pltpu.HBM(a.shape, a.dtype) for a in grads + lands],
        input_output_aliases={a: a for a in range(2 * n)}, compiler_params=_SPLIT_COPY,
    )(*grads, *lands, sems[0], sems[1], after)
    return list(outs[:n]), list(outs[n:])


def reduce_sum(land, grad, place, name, into=None, layer=None):
    _, hR, C = land.shape
    tr = hR
    while N_DEV * tr * C * 2 > 3 * ELEMENTWISE_BLOCK_BYTES and tr % 32 == 0:
        tr //= 2

    def body(s_ref, l_ref, g_ref, *rest):
        o_ref = rest[-1]
        own = g_ref[...].astype(F32)
        acc = None
        for q in range(N_DEV):
            term = jnp.where(s_ref[2] == q, own, l_ref[q].astype(F32))
            acc = term if acc is None else acc + term
        o_ref[...] = acc

    in_specs = [pl.BlockSpec((N_DEV, tr, C), lambda i, s: (0, i, 0)),
                pl.BlockSpec((None, None, tr, C), lambda i, s: (s[0], s[1], i, 0))]
    args = [place, land, grad]
    aliases = {}
    if layer is None:
        out_spec = pl.BlockSpec((None, tr, C), lambda i, s: (s[1], i, 0))
        out_shape = _sds((2, hR, C), F32)
    else:
        out_spec = pl.BlockSpec((None, None, tr, C), lambda i, s: (layer, s[1], i, 0))
        out_shape = _sds((2, 2, hR, C), F32)
        if into is not None:
            in_specs.append(ANY)
            args.append(into)
            aliases = {3: 0}
    grid_spec = pltpu.PrefetchScalarGridSpec(num_scalar_prefetch=1, grid=(hR // tr,), in_specs=in_specs, out_specs=out_spec)
    return pl.pallas_call(body, name=name, grid_spec=grid_spec, out_shape=out_shape, input_output_aliases=aliases,
                          compiler_params=_params("arbitrary"))(*args)


def join_halves(arrays, name):
    n = len(arrays)
    pieces = [(a, l) for a, arr in enumerate(arrays) for l in (range(arr.shape[0]) if arr.ndim == 4 else [None])]

    def body(*refs):
        ins = refs[:n]
        send, recv = refs[2 * n:]
        mx, my, mc = _coords()

        def half(a, l, h):
            return ins[a].at[h] if l is None else ins[a].at[l, h]

        sends = [_remote(half(a, l, mc), half(a, l, mc), send.at[i], recv.at[i], (mx, my, 1 - mc))
                 for i, (a, l) in enumerate(pieces)]
        for cp in sends:
            cp.start()
        for i, (a, l) in enumerate(pieces):
            theirs = half(a, l, 1 - mc)
            _remote(theirs, theirs, send.at[i], recv.at[i], (mx, my, 1 - mc)).wait_recv()
        for cp in sends:
            cp.wait_send()

    return pl.pallas_call(
        body, name=name, in_specs=[ANY] * n, out_specs=[ANY] * n, out_shape=[_sds(a.shape, a.dtype) for a in arrays],
        input_output_aliases={a: a for a in range(n)},
        scratch_shapes=[pltpu.SemaphoreType.DMA((len(pieces),)), pltpu.SemaphoreType.DMA((len(pieces),))],
    )(*arrays)


LANES = 128
SUBLANES = 8


def _n_rows(shape):
    rows = -(-int(np.prod(shape)) // LANES)
    return -(-rows // SUBLANES) * SUBLANES


def _as_rows(a):
    flat = a.reshape(-1)
    rows = _n_rows(a.shape)
    return jnp.pad(flat, (0, rows * LANES - flat.shape[0])).reshape(rows, LANES)


def _pack(arrays):
    return jnp.concatenate([_as_rows(a) for a in arrays], axis=0)


def _unpack(rows, shapes):
    out, r0 = [], 0
    for s in shapes:
        n = _n_rows(s)
        out.append(rows[r0:r0 + n].reshape(-1)[:int(np.prod(s))].reshape(s))
        r0 += n
    return out


REPLICATED_SMALL = [("rel_bias", (32, 16)), ("even_norm", (1, 1024)), ("even_pool_w", (1, 4, 128, 128)),
                    ("even_pool_scale", (1, 512)), ("odd_q_norm", (1, 64)), ("odd_k_norm", (1, 64)),
                    ("ffn_norm", (2, 1024)), ("ffn_conv_b", (2, 5632))]
SHARDED_SMALL = [("even_conv_w", (1, 3, 128)), ("odd_norm", (1, 256)), ("ffn_conv_w", (2, 3, 1408))]
BIG = ["even_w_in", "even_w_out", "odd_w_qkv", "odd_w_o", "ffn_w_up", "ffn_w_down"]
WEIGHT_ORDER = ["rel_bias", "even_norm", "even_w_in", "even_conv_w", "even_pool_w", "even_pool_scale", "even_w_out",
                "odd_norm", "odd_w_qkv", "odd_q_norm", "odd_k_norm", "odd_w_o", "ffn_norm", "ffn_w_up", "ffn_conv_w",
                "ffn_conv_b", "ffn_w_down"]


def kernel(x, rel_bias, even_norm, even_w_in, even_conv_w, even_pool_w, even_pool_scale, even_w_out, odd_norm, odd_w_qkv, odd_q_norm, odd_k_norm, odd_w_o, ffn_norm, ffn_w_up, ffn_conv_w, ffn_conv_b, ffn_w_down, loss_target, m_rel_bias, m_even_norm, m_even_w_in, m_even_conv_w, m_even_pool_w, m_even_pool_scale, m_even_w_out, m_odd_norm, m_odd_w_qkv, m_odd_q_norm, m_odd_k_norm, m_odd_w_o, m_ffn_norm, m_ffn_w_up, m_ffn_conv_w, m_ffn_conv_b, m_ffn_w_down, v_rel_bias, v_even_norm, v_even_w_in, v_even_conv_w, v_even_pool_w, v_even_pool_scale, v_even_w_out, v_odd_norm, v_odd_w_qkv, v_odd_q_norm, v_odd_k_norm, v_odd_w_o, v_ffn_norm, v_ffn_w_up, v_ffn_conv_w, v_ffn_conv_b, v_ffn_w_down):
    W = dict(rel_bias=rel_bias, even_norm=even_norm, even_w_in=even_w_in, even_conv_w=even_conv_w, even_pool_w=even_pool_w,
             even_pool_scale=even_pool_scale, even_w_out=even_w_out, odd_norm=odd_norm, odd_w_qkv=odd_w_qkv,
             odd_q_norm=odd_q_norm, odd_k_norm=odd_k_norm, odd_w_o=odd_w_o, ffn_norm=ffn_norm, ffn_w_up=ffn_w_up,
             ffn_conv_w=ffn_conv_w, ffn_conv_b=ffn_conv_b, ffn_w_down=ffn_w_down)
    M1 = dict(rel_bias=m_rel_bias, even_norm=m_even_norm, even_w_in=m_even_w_in, even_conv_w=m_even_conv_w,
              even_pool_w=m_even_pool_w, even_pool_scale=m_even_pool_scale, even_w_out=m_even_w_out, odd_norm=m_odd_norm,
              odd_w_qkv=m_odd_w_qkv, odd_q_norm=m_odd_q_norm, odd_k_norm=m_odd_k_norm, odd_w_o=m_odd_w_o,
              ffn_norm=m_ffn_norm, ffn_w_up=m_ffn_w_up, ffn_conv_w=m_ffn_conv_w, ffn_conv_b=m_ffn_conv_b,
              ffn_w_down=m_ffn_w_down)
    M2 = dict(rel_bias=v_rel_bias, even_norm=v_even_norm, even_w_in=v_even_w_in, even_conv_w=v_even_conv_w,
              even_pool_w=v_even_pool_w, even_pool_scale=v_even_pool_scale, even_w_out=v_even_w_out, odd_norm=v_odd_norm,
              odd_w_qkv=v_odd_w_qkv, odd_q_norm=v_odd_q_norm, odd_k_norm=v_odd_k_norm, odd_w_o=v_odd_w_o,
              ffn_norm=v_ffn_norm, ffn_w_up=v_ffn_w_up, ffn_conv_w=v_ffn_conv_w, ffn_conv_b=v_ffn_conv_b,
              ffn_w_down=v_ffn_w_down)
    mx, my, mc = _coords()
    chip = 2 * mx + my
    me = 4 * mx + 2 * my + mc
    place = jnp.stack([chip, mc, me]).astype(jnp.int32)
    xs, target = x[0], loss_target[0]

    def halves(w):
        return w.reshape((w.shape[0], 2, w.shape[-2] // 2, w.shape[-1]))

    lands = [cast_into_slot(halves(even_w_in), 0, place, "cast_w_in"), cast_into_slot(halves(even_w_out), 0, place, "cast_w_out"),
             cast_into_slot(halves(ffn_w_up), 0, place, "cast_w_up0"), cast_into_slot(halves(ffn_w_down), 0, place, "cast_w_down0"),
             cast_into_slot(halves(odd_w_qkv), 0, place, "cast_w_qkv"), cast_into_slot(halves(odd_w_o), 0, place, "cast_w_o"),
             cast_into_slot(halves(ffn_w_up), 1, place, "cast_w_up1"), cast_into_slot(halves(ffn_w_down), 1, place, "cast_w_down1")]
    small_rows = jnp.pad(_pack([even_conv_w, odd_norm, ffn_conv_w]), ((0, SUBLANES), (0, 0)))
    lands.append(cast_into_slot(small_rows.reshape(1, 2, small_rows.shape[0] // 2, LANES), 0, place, "small_into_slot", dtype=F32))
    groups = [[0, 1, 8], [2], [3], [4, 5], [6], [7]]
    gather_sems, lands, token = gather_start(lands, groups, "gather_start")
    even_norm_after_start = even_norm + token[0:1, 0:1]

    def gathered(group, tag, after_landing, after_passing):
        mine = [lands[a] for a in groups[group]]
        sems, arrays = gather_forward(mine, gather_sems[group], after_landing, "gather_forward_" + tag)
        return gather_wait(arrays, sems, after_passing, "gather_wait_" + tag)

    pool_w = cast_bf16(even_pool_w[0], "cast_pool_w")
    gqk = jnp.stack([jnp.tile(odd_q_norm[0], N_HEADS), jnp.tile(odd_k_norm[0], N_HEADS),
                     jnp.ones((D_MODEL,), F32)])[:, None, :]
    bias = bias_expand(rel_bias.T, "bias_expand").reshape(6, N_HEADS, ATT_BLOCK, 2 * ATT_BLOCK)
    xn0 = rmsnorm_fwd(xs, even_norm_after_start, "even_norm")
    got = gathered(0, "even", bias, xn0)
    w_in = got[0].reshape(N_CHIPS, 1, D_MODEL, EVEN_IN // N_CHIPS)
    w_out = got[1].reshape(1, 1, D_MODEL, D_MODEL)
    small = got[2].reshape(N_CHIPS, small_rows.shape[0], LANES)
    conv_w_full = small[:, 0:3].transpose(1, 0, 2).reshape(3, A_WIDTH)
    odd_norm_full = small[:, 8:10].reshape(1, D_MODEL)
    ffn_cw_full = small[:, 16:82].reshape(N_CHIPS, 2, 3, 2 * D_FF // N_CHIPS).transpose(1, 2, 0, 3).reshape(2, 3, 2 * D_FF)

    def ffn_fwd(l, xin, xn):
        up, u, act = up_glu_fwd(xn, w_up[l], ffn_cw_full[l], ffn_conv_b[l:l + 1], f"ffn{l}_up_glu")
        return act, (xin, xn, up, u, act)

    w_up, w_down = [None, None], [None, None]
    proj, mix = in_mixer_fwd(xn0, w_in, conv_w_full, pool_w, even_pool_scale, "even_in_mixer")
    x1, xn1 = mm_res_norm(mix, w_out, xs, ffn_norm[0:1], "even_out")
    w_up[0] = gathered(1, "up0", proj, x1)[0].reshape(N_CHIPS, 1, D_MODEL, 2 * D_FF // N_CHIPS)
    act0, ffn0 = ffn_fwd(0, x1, xn1)
    w_down[0] = gathered(2, "down0", act0, act0)[0].reshape(1, 1, D_FF, D_MODEL)
    x2, xn2 = mm_res_norm(act0, w_down[0], x1, odd_norm_full, "ffn0_down")
    got = gathered(3, "odd", x1, x2)
    w_qkv = got[0].reshape(N_CHIPS, 1, D_MODEL, 3 * D_MODEL // N_CHIPS)
    w_o = got[1].reshape(1, 1, D_MODEL, D_MODEL)
    qkv = mm_nn(xn2, w_qkv, "odd_qkv")
    qkvn = qknorm_fwd(qkv, gqk, "odd_qknorm")
    att, lse = attn_fwd(qkvn, bias, "attn_fwd")
    x3, xn3 = mm_res_norm(att, w_o, x2, ffn_norm[1:2], "odd_out")
    w_up[1] = gathered(4, "up1", x2, x3)[0].reshape(N_CHIPS, 1, D_MODEL, 2 * D_FF // N_CHIPS)
    act1, ffn1 = ffn_fwd(1, x3, xn3)
    w_down[1] = gathered(5, "down1", act1, act1)[0].reshape(1, 1, D_FF, D_MODEL)
    dy, dyb, sq = mm_res_loss(act1, w_down[1], x3, target, "ffn1_down_loss")
    loss = lax.psum(0.5 * jnp.sum(sq) * (1.0 / D_MODEL), ("x", "y", "c"))

    def ffn_bwd(l, dy, dyb, saved):
        xin, xn, up, u, act = saved
        dw_down = mm_tn(act, dyb, f"ffn{l}_dw_down", J=1, tk=D_FF // 2, tm=1024)
        dact = mm_nt(dyb, w_down[l], f"ffn{l}_dact", tr=D_FF // 2, out_dtype=BF16, tm=1024)
        dup, dcw, dcb = glu_bwd(up, u, dact, ffn_cw_full[l], f"ffn{l}_glu_bwd")
        dw_up = mm_tn(xn, dup, f"ffn{l}_dw_up", J=N_CHIPS, tk=512, tm=1024, jb=2)
        dx, dxb, dg = mm_nt_norm_bwd(dup, w_up[l], xin, ffn_norm[l:l + 1], dy, f"ffn{l}_dx")
        return dx, dxb, (dw_down, dw_up, dcw, dcb, dg)

    def quarters(g):
        return g.reshape(N_CHIPS, 2, g.shape[0] * g.shape[1] // (2 * N_CHIPS), g.shape[-1])

    def reduce_start(grads, tag, then):
        sems, parts, zones = reduce_send([quarters(g) for g in grads], "reduce_send_" + tag)
        then, parts = lax.optimization_barrier((then, parts))
        return (sems, parts, zones), then

    dx3, dx3b, g_ffn1 = ffn_bwd(1, dy, dyb, ffn1)
    red_ffn1, (dx3, dx3b) = reduce_start([g_ffn1[1], g_ffn1[0]], "ffn1", (dx3, dx3b))
    dw_o = mm_tn(att, dx3b, "odd_dw_o", J=1, tk=512, tm=1024)
    datt = mm_nt(dx3b, w_o, "odd_datt", tr=D_MODEL, out_dtype=BF16)
    dq, dk, dv, dbias = attn_bwd(qkvn, att, datt, lse, bias, "attn_bwd")
    dqkv, dgqk = qknorm_bwd(qkv, dq, dk, dv, gqk, "odd_qknorm_bwd")
    dw_qkv = mm_tn(xn2, dqkv, "odd_dw_qkv", J=N_CHIPS, tk=512, tm=1024)
    red_odd, dqkv = reduce_start([dw_qkv, dw_o], "odd", dqkv)
    dx2, dx2b, dg_odd = mm_nt_norm_bwd(dqkv, w_qkv, x2, odd_norm_full, dx3, "odd_dx")
    dx1, dx1b, g_ffn0 = ffn_bwd(0, dx2, dx2b, ffn0)
    red_ffn0, (dx1, dx1b) = reduce_start([g_ffn0[1], g_ffn0[0]], "ffn0", (dx1, dx1b))
    dw_out = mm_tn(mix, dx1b, "even_dw_out", J=1, tk=512, tm=1024)
    dmix = mm_nt(dx1b, w_out, "even_dmix", tr=D_MODEL)
    dproj, dcw_even, dpw, dps = mixer_bwd(proj, dmix, conv_w_full, pool_w, even_pool_scale, "even_mixer_bwd")
    dw_in = mm_tn(xn0, dproj, "even_dw_in", J=N_CHIPS, tk=512, tm=1024)
    grad_x, _, dg_even = mm_nt_norm_bwd(dproj, w_in, xs, even_norm, dx1, "even_dx")
    d_rel = jnp.sum(bias_reduce(dbias.reshape(3, N_HEADS, 2 * ATT_BLOCK * ATT_BLOCK), "bias_reduce"), axis=0).T

    red_even, grad_x = reduce_start([dw_in, dw_out], "even", grad_x)

    dcw_sh = dcw_even.reshape(3, N_CHIPS, A_WIDTH // N_CHIPS).transpose(1, 0, 2)
    don_sh = dg_odd.reshape(N_CHIPS, D_MODEL // N_CHIPS)
    dfcw = jnp.stack([g_ffn0[2], g_ffn1[2]])
    dfcw_sh = dfcw.reshape(2, 3, N_CHIPS, 2 * D_FF // N_CHIPS).transpose(2, 0, 1, 3)
    rep_grads = [d_rel, dg_even, dpw[None], dps, _head_sum(dgqk[0]), _head_sum(dgqk[1]),
                 jnp.concatenate([g_ffn0[4], g_ffn1[4]], axis=0), jnp.concatenate([g_ffn0[3], g_ffn1[3]], axis=0)]
    rep_rows = _pack(rep_grads)
    shard_rows = jnp.concatenate([_pack([dcw_sh[j], don_sh[j], dfcw_sh[j]]) for j in range(N_CHIPS)], axis=0)
    n_rep, n_shard = rep_rows.shape[0], shard_rows.shape[0] // N_CHIPS
    small_sems, small_rows, small_land = devices_start(jnp.concatenate([rep_rows, shard_rows], axis=0), "small_grads_start")
    grad_x, small_rows = lax.optimization_barrier((grad_x, small_rows))

    def reduce_end(red, tag, after):
        sems, parts, zones = red
        parts, zones = reduce_wait(parts, zones, sems, after, "reduce_wait_" + tag)
        return zones, parts

    z_ffn1, p_ffn1 = reduce_end(red_ffn1, "ffn1", grad_x)
    z_odd, p_odd = reduce_end(red_odd, "odd", grad_x)
    r_qkv = reduce_sum(z_odd[0], p_odd[0], place, "reduce_sum_w_qkv")
    r_o = reduce_sum(z_odd[1], p_odd[1], place, "reduce_sum_w_o")
    r_up = reduce_sum(z_ffn1[0], p_ffn1[0], place, "reduce_sum_w_up1", layer=1)
    r_down = reduce_sum(z_ffn1[1], p_ffn1[1], place, "reduce_sum_w_down1", layer=1)
    r_qkv, r_o, r_up, r_down = lax.optimization_barrier((r_qkv, r_o, r_up, r_down))
    z_ffn0, p_ffn0 = reduce_end(red_ffn0, "ffn0", r_down)
    r_up = reduce_sum(z_ffn0[0], p_ffn0[0], place, "reduce_sum_w_up0", into=r_up, layer=0)
    r_down = reduce_sum(z_ffn0[1], p_ffn0[1], place, "reduce_sum_w_down0", into=r_down, layer=0)
    later = ["odd_w_qkv", "odd_w_o", "ffn_w_up", "ffn_w_down"]
    joined = join_halves([r_qkv, r_o, r_up, r_down], "grads_join_late_layers")
    G = {nm: g.reshape(W[nm].shape) for nm, g in zip(later, joined)}

    D_, NM, NV = {}, {}, {}

    def update(nm):
        as3 = lambda a: a.reshape((-1,) + a.shape[-2:])
        outs = adamw(as3(W[nm]), as3(G[nm]), as3(M1[nm]), as3(M2[nm]), "adamw_" + nm)
        D_[nm], NM[nm], NV[nm], G[nm] = [o.reshape(W[nm].shape) for o in outs]

    def all_before(names):
        tied = lax.optimization_barrier([D_[nm] for nm in names])
        for nm, d in zip(names, tied):
            D_[nm] = d
        return tied[0]

    for nm in later:
        update(nm)
    z_even, p_even = reduce_end(red_even, "even", all_before(later))
    joined = join_halves([reduce_sum(z_even[0], p_even[0], place, "reduce_sum_w_in"),
                          reduce_sum(z_even[1], p_even[1], place, "reduce_sum_w_out")], "grads_join_first_layer")
    first = ["even_w_in", "even_w_out"]
    for nm, g in zip(first, joined):
        G[nm] = g.reshape(W[nm].shape)
        update(nm)
    small_rows, small_land = devices_wait(small_rows, small_land, small_sems, all_before(first), "small_grads_wait")
    small_sum = device_sum(small_land, small_rows, place[2:3], "small_grads_sum")
    mine = lax.dynamic_slice_in_dim(small_sum, n_rep + chip * n_shard, n_shard, axis=0)
    g_small = jnp.concatenate([small_sum[:n_rep], mine], axis=0)
    small_names = [n for n, _ in REPLICATED_SMALL + SHARDED_SMALL]
    small_shapes = [s for _, s in REPLICATED_SMALL + SHARDED_SMALL]
    G.update(dict(zip(small_names, _unpack(g_small, small_shapes))))
    packs = [_pack([d[n] for n in small_names])[None] for d in (W, M1, M2)]
    outs = adamw(packs[0], g_small[None], packs[1], packs[2], "adamw_small")
    for dst, o in zip((D_, NM, NV), outs[:3]):
        dst.update(dict(zip(small_names, _unpack(o[0], small_shapes))))

    return (loss, grad_x[None], *[G[n] for n in WEIGHT_ORDER], *[D_[n] for n in WEIGHT_ORDER],
            *[NM[n] for n in WEIGHT_ORDER], *[NV[n] for n in WEIGHT_ORDER])


def _head_sum(dg):
    return jnp.sum(dg.reshape(N_HEADS, HEAD_DIM), axis=0, keepdims=True)
```

```python
import functools
import math

import numpy as np
import jax
import jax.numpy as jnp
from jax import lax
from jax.experimental import pallas as pl
from jax.experimental.pallas import tpu as pltpu

F32 = jnp.float32
BF16 = jnp.bfloat16

D_MODEL = 1024
N_HEADS = 16
HEAD_DIM = 64
A_WIDTH = 512
POOL_WINDOWS = (2, 4, 8, 16)
POOL_GROUP = 128
EVEN_IN = 2048
D_FF = 2816
DILATED_PAIRS = ((128, 1), (512, 4), (2048, 16))
ATT_BLOCK = 128
N_REL_BUCKETS = 32
REL_MAX_DISTANCE = 2048
EPS = 1e-6
MASK_VALUE = -1e30
ADAM_LR, ADAM_B1, ADAM_B2, ADAM_EPS, ADAM_WD, ADAM_STEP = 0.001, 0.9, 0.999, 1e-08, 0.01, 10

VMEM_LIMIT_BYTES = 48 * 1024 * 1024
ELEMENTWISE_BLOCK_BYTES = 2 * 1024 * 1024
N_CHIPS = 4
N_DEV = 8
MESH = pl.DeviceIdType.MESH


def _params(*sem):
    return pltpu.CompilerParams(dimension_semantics=sem if sem else None, vmem_limit_bytes=VMEM_LIMIT_BYTES)


def _sds(shape, dtype):
    return jax.ShapeDtypeStruct(tuple(shape), dtype)


def cast_bf16(x, name, tr=None):
    lead, (R, C) = x.shape[:-2], x.shape[-2:]
    n = int(np.prod(lead)) if lead else 1
    x3 = x.reshape((n, R, C))
    tr = tr or R

    def body(x_ref, o_ref):
        o_ref[...] = x_ref[...].astype(BF16)

    out = pl.pallas_call(
        body, name=name, grid=(n, R // tr),
        in_specs=[pl.BlockSpec((None, tr, C), lambda i, r: (i, r, 0))],
        out_specs=pl.BlockSpec((None, tr, C), lambda i, r: (i, r, 0)),
        out_shape=_sds((n, R, C), BF16), compiler_params=_params("parallel", "parallel"),
    )(x3)
    return out.reshape(lead + (R, C))


def rmsnorm_fwd(x, g, name, ts=512):
    S, Dm = x.shape

    def body(x_ref, g_ref, o_ref):
        xv = x_ref[...]
        r = lax.rsqrt(jnp.mean(xv * xv, axis=-1, keepdims=True) + EPS)
        o_ref[...] = ((xv * r) * g_ref[...]).astype(BF16)

    return pl.pallas_call(
        body, name=name, grid=(S // ts,),
        in_specs=[pl.BlockSpec((ts, Dm), lambda i: (i, 0)), pl.BlockSpec((1, Dm), lambda i: (0, 0))],
        out_specs=pl.BlockSpec((ts, Dm), lambda i: (i, 0)),
        out_shape=_sds((S, Dm), BF16), compiler_params=_params("parallel"),
    )(x, g)


def mm_res_norm(a, w, res, gain, name, tm=1024):
    M, K = a.shape
    Dm = w.shape[-1]

    def body(a_ref, w_ref, r_ref, g_ref, y_ref, yn_ref):
        y = r_ref[...] + jnp.dot(a_ref[...], w_ref[...], preferred_element_type=F32)
        y_ref[...] = y
        r = lax.rsqrt(jnp.mean(y * y, axis=-1, keepdims=True) + EPS)
        yn_ref[...] = ((y * r) * g_ref[...]).astype(BF16)

    row = pl.BlockSpec((tm, Dm), lambda m: (m, 0))
    return pl.pallas_call(
        body, name=name, grid=(M // tm,),
        in_specs=[pl.BlockSpec((tm, K), lambda m: (m, 0)),
                  pl.BlockSpec((None, None, K, Dm), lambda m: (0, 0, 0, 0), pipeline_mode=pl.Buffered(1)),
                  row, pl.BlockSpec((1, Dm), lambda m: (0, 0))],
        out_specs=[row, row], out_shape=[_sds((M, Dm), F32), _sds((M, Dm), BF16)],
        compiler_params=_params("parallel"),
    )(a, w, res, gain)


def mm_res_loss(a, w, res, target, name, tm=512):
    M, K = a.shape
    Dm = w.shape[-1]

    def body(a_ref, w_ref, r_ref, t_ref, d_ref, db_ref, s_ref):
        e = (r_ref[...] + jnp.dot(a_ref[...], w_ref[...], preferred_element_type=F32)) - t_ref[...]
        d = e * (1.0 / Dm)
        d_ref[...] = d
        db_ref[...] = d.astype(BF16)
        part = jnp.sum(e * e, axis=0, keepdims=True)

        @pl.when(pl.program_id(0) == 0)
        def _():
            s_ref[...] = part

        @pl.when(pl.program_id(0) > 0)
        def _():
            s_ref[...] += part

    row = pl.BlockSpec((tm, Dm), lambda m: (m, 0))
    return pl.pallas_call(
        body, name=name, grid=(M // tm,),
        in_specs=[pl.BlockSpec((tm, K), lambda m: (m, 0)),
                  pl.BlockSpec((None, None, K, Dm), lambda m: (0, 0, 0, 0), pipeline_mode=pl.Buffered(1)), row, row],
        out_specs=[row, row, pl.BlockSpec((1, Dm), lambda m: (0, 0))],
        out_shape=[_sds((M, Dm), F32), _sds((M, Dm), BF16), _sds((1, Dm), F32)],
        compiler_params=_params("arbitrary"),
    )(a, w, res, target)


def mm_nt(dy, w, name, tr, layer=0, out_dtype=F32, tm=512):
    M = dy.shape[0]
    J, _, R, Ns = w.shape
    dims = (((1,), (1,)), ((), ()))

    def body(dy_ref, w_ref, o_ref):
        acc = None
        for j in range(J):
            p = lax.dot_general(dy_ref[:, j * Ns:(j + 1) * Ns], w_ref[j], dims, preferred_element_type=F32)
            acc = p if acc is None else acc + p
        o_ref[...] = acc.astype(o_ref.dtype)

    return pl.pallas_call(
        body, name=name, grid=(R // tr, M // tm),
        in_specs=[pl.BlockSpec((tm, J * Ns), lambda r, m: (m, 0)),
                  pl.BlockSpec((J, None, tr, Ns), lambda r, m: (0, layer, r, 0))],
        out_specs=pl.BlockSpec((tm, tr), lambda r, m: (m, r)),
        out_shape=_sds((M, R), out_dtype),
        compiler_params=_params("parallel", "parallel"),
    )(dy, w)


def mm_nt_norm_bwd(dy, w, x, g, dres, name, layer=0, tm=512):
    M = dy.shape[0]
    J, _, Dm, Ns = w.shape
    dims = (((1,), (1,)), ((), ()))

    def body(dy_ref, w_ref, x_ref, g_ref, r_ref, dx_ref, dxb_ref, dg_ref):
        dxn = None
        for j in range(J):
            p = lax.dot_general(dy_ref[:, j * Ns:(j + 1) * Ns], w_ref[j], dims, preferred_element_type=F32)
            dxn = p if dxn is None else dxn + p
        xv = x_ref[...]
        r = lax.rsqrt(jnp.mean(xv * xv, axis=-1, keepdims=True) + EPS)
        gx = dxn * g_ref[...]
        dot = jnp.sum(gx * xv, axis=-1, keepdims=True)
        dx = r_ref[...] + r * gx - xv * ((r * r * r) * (dot * (1.0 / Dm)))
        dx_ref[...] = dx
        dxb_ref[...] = dx.astype(BF16)
        part = jnp.sum(dxn * (xv * r), axis=0, keepdims=True)

        @pl.when(pl.program_id(0) == 0)
        def _():
            dg_ref[...] = part

        @pl.when(pl.program_id(0) > 0)
        def _():
            dg_ref[...] += part

    row = pl.BlockSpec((tm, Dm), lambda m: (m, 0))
    vec = pl.BlockSpec((1, Dm), lambda m: (0, 0))
    return pl.pallas_call(
        body, name=name, grid=(M // tm,),
        in_specs=[pl.BlockSpec((tm, J * Ns), lambda m: (m, 0)),
                  pl.BlockSpec((J, None, Dm, Ns), lambda m: (0, layer, 0, 0), pipeline_mode=pl.Buffered(1)), row, vec, row],
        out_specs=[row, row, vec],
        out_shape=[_sds((M, Dm), F32), _sds((M, Dm), BF16), _sds((1, Dm), F32)],
        compiler_params=_params("arbitrary"),
    )(dy, w, x, g, dres)


def mm_tn(a, dy, name, J, tk, tm=512, jb=None):
    M, K = a.shape
    jb = jb or J
    Ns = dy.shape[1] // J
    N = jb * Ns
    n_m = M // tm
    dims = (((0,), (0,)), ((), ()))

    def body(a_ref, dy_ref, o_ref, acc_ref):
        p = lax.dot_general(a_ref[...], dy_ref[...], dims, preferred_element_type=F32)
        m = pl.program_id(2)

        @pl.when(m == 0)
        def _():
            acc_ref[...] = p

        @pl.when(m > 0)
        def _():
            acc_ref[...] += p

        @pl.when(m == n_m - 1)
        def _():
            for j in range(jb):
                o_ref[j] = acc_ref[:, j * Ns:(j + 1) * Ns].astype(BF16)

    return pl.pallas_call(
        body, name=name, grid=(J // jb, K // tk, n_m),
        in_specs=[pl.BlockSpec((tm, tk), lambda g, k, m: (m, k)), pl.BlockSpec((tm, N), lambda g, k, m: (m, g))],
        out_specs=pl.BlockSpec((jb, tk, Ns), lambda g, k, m: (g, k, 0)),
        out_shape=_sds((J, K, Ns), BF16), scratch_shapes=[pltpu.VMEM((tk, N), F32)],
        compiler_params=_params("parallel", "parallel", "arbitrary"),
    )(a, dy)


HALO = 16


def _shift_down(x, s):
    return pltpu.roll(x, s, 0)


def _shift_up(x, s):
    return pltpu.roll(x, x.shape[0] - s, 0)


def _conv3(z, cw):
    return (_shift_down(z, 2) * cw[0:1] + _shift_down(z, 1) * cw[1:2]) + z * cw[2:3]


def _window_count(first_row, n, k):
    t = first_row + lax.broadcasted_iota(jnp.int32, (n, 1), 0)
    return jnp.clip(t + 1, 1, k).astype(F32)


def in_mixer_fwd(xn, w_in, conv_w, pool_w, pool_scale, name, ts=512):
    S, K = xn.shape
    n = ts + HALO

    def body(xm_ref, xb_ref, w_ref, cw_ref, pw_ref, ps_ref, p_ref, o_ref):
        i = pl.program_id(0)
        before = jnp.where(i > 0, xb_ref[...], jnp.zeros_like(xb_ref))
        rows = jnp.concatenate([before, xm_ref[...]], axis=0)
        h, gb, gc, pin = [jnp.dot(rows, w_ref[j], preferred_element_type=F32) for j in range(N_CHIPS)]
        for j, part in enumerate((h, gb, gc, pin)):
            p_ref[:, j * A_WIDTH:(j + 1) * A_WIDTH] = part[HALO:]
        cz = _conv3(gc * h, cw_ref[...])
        o_ref[:, 0:A_WIDTH] = (gb[HALO:] * cz[HALO:]).astype(BF16)
        for g, k in enumerate(POOL_WINDOWS):
            p = pin[:, g * POOL_GROUP:(g + 1) * POOL_GROUP]
            w = p
            s = 1
            while s < k:
                w = w + _shift_down(w, s)
                s *= 2
            pooled = w / _window_count(i * ts - HALO, n, k) - p
            yb = jnp.dot(pooled[HALO:].astype(BF16), pw_ref[g], preferred_element_type=F32)
            yb = yb * ps_ref[:, g * POOL_GROUP:(g + 1) * POOL_GROUP]
            o_ref[:, A_WIDTH + g * POOL_GROUP:A_WIDTH + (g + 1) * POOL_GROUP] = yb.astype(BF16)

    hb = ts // HALO
    return pl.pallas_call(
        body, name=name, grid=(S // ts,),
        in_specs=[
            pl.BlockSpec((ts, K), lambda i: (i, 0)),
            pl.BlockSpec((HALO, K), lambda i: (jnp.maximum(i * hb - 1, 0), 0)),
            pl.BlockSpec((N_CHIPS, None, K, A_WIDTH), lambda i: (0, 0, 0, 0), pipeline_mode=pl.Buffered(1)),
            pl.BlockSpec((3, A_WIDTH), lambda i: (0, 0)),
            pl.BlockSpec((4, POOL_GROUP, POOL_GROUP), lambda i: (0, 0, 0)),
            pl.BlockSpec((1, 4 * POOL_GROUP), lambda i: (0, 0)),
        ],
        out_specs=[pl.BlockSpec((ts, EVEN_IN), lambda i: (i, 0)), pl.BlockSpec((ts, D_MODEL), lambda i: (i, 0))],
        out_shape=[_sds((S, EVEN_IN), F32), _sds((S, D_MODEL), BF16)], compiler_params=_params("parallel"),
    )(xn, xn, w_in, conv_w, pool_w, pool_scale)


def mixer_bwd(proj, dmix, conv_w, pool_w, pool_scale, name, ts=256):
    S = proj.shape[0]
    n = ts + 2 * HALO
    nt = S // ts
    tn_dims = (((0,), (0,)), ((), ()))
    nt_dims = (((1,), (1,)), ((), ()))

    def body(pm_ref, pb_ref, pa_ref, dm_ref, da_ref, cw_ref, pw_ref, ps_ref, o_ref, dcw_ref, dpw_ref, dps_ref):
        i = pl.program_id(0)
        last = i == nt - 1
        before = jnp.where(i > 0, pb_ref[...], 0.0)
        after = jnp.where(last, 0.0, pa_ref[...])
        ext = jnp.concatenate([before, pm_ref[...], after], axis=0)
        dafter = jnp.where(last, 0.0, da_ref[...])
        dext = jnp.concatenate([jnp.zeros((HALO, D_MODEL), F32), dm_ref[...], dafter], axis=0)
        cw = cw_ref[...]
        main = slice(HALO, HALO + ts)

        @pl.when(i == 0)
        def _():
            dcw_ref[...] = jnp.zeros_like(dcw_ref)
            dpw_ref[...] = jnp.zeros_like(dpw_ref)
            dps_ref[...] = jnp.zeros_like(dps_ref)

        h, gb, gc = ext[:, 0:A_WIDTH], ext[:, A_WIDTH:2 * A_WIDTH], ext[:, 2 * A_WIDTH:3 * A_WIDTH]
        z = gc * h
        z1, z2 = _shift_down(z, 1), _shift_down(z, 2)
        cz = (z2 * cw[0:1] + z1 * cw[1:2]) + z * cw[2:3]
        dya = dext[:, 0:A_WIDTH]
        dcz = dya * gb
        dz = dcz * cw[2:3] + _shift_up(dcz, 1) * cw[1:2] + _shift_up(dcz, 2) * cw[0:1]
        o_ref[:, 0:A_WIDTH] = (dz * gc)[main].astype(BF16)
        o_ref[:, A_WIDTH:2 * A_WIDTH] = (dya * cz)[main].astype(BF16)
        o_ref[:, 2 * A_WIDTH:3 * A_WIDTH] = (dz * h)[main].astype(BF16)
        dczm = dcz[main]
        dcw_ref[0:1, :] += jnp.sum(dczm * z2[main], axis=0, keepdims=True)
        dcw_ref[1:2, :] += jnp.sum(dczm * z1[main], axis=0, keepdims=True)
        dcw_ref[2:3, :] += jnp.sum(dczm * z[main], axis=0, keepdims=True)

        for g, k in enumerate(POOL_WINDOWS):
            lo = 3 * A_WIDTH + g * POOL_GROUP
            cols = slice(g * POOL_GROUP, (g + 1) * POOL_GROUP)
            p = ext[:, lo:lo + POOL_GROUP]
            w = p
            s = 1
            while s < k:
                w = w + _shift_down(w, s)
                s *= 2
            cnt = _window_count(i * ts - HALO, n, k)
            pooled = (w / cnt - p)[main].astype(BF16)
            dyb = dext[:, A_WIDTH + g * POOL_GROUP:A_WIDTH + (g + 1) * POOL_GROUP]
            e = dyb * ps_ref[:, cols]
            pre = jnp.dot(pooled, pw_ref[g], preferred_element_type=F32)
            dps_ref[:, cols] += jnp.sum(dyb[main] * pre, axis=0, keepdims=True)
            dpw_ref[g] += lax.dot_general(pooled, e[main].astype(BF16), tn_dims, preferred_element_type=F32)
            dpooled = lax.dot_general(e.astype(BF16), pw_ref[g], nt_dims, preferred_element_type=F32)
            q = dpooled / cnt
            a = q
            s = 1
            while s < k:
                a = a + _shift_up(a, s)
                s *= 2
            o_ref[:, lo:lo + POOL_GROUP] = (a - dpooled)[main].astype(BF16)

    hb = ts // HALO
    nh = S // HALO
    before_map = lambda i: (jnp.maximum(i * hb - 1, 0), 0)
    after_map = lambda i: (jnp.minimum((i + 1) * hb, nh - 1), 0)
    full = lambda *shape: pl.BlockSpec(shape, lambda i: (0,) * len(shape))
    return pl.pallas_call(
        body, name=name, grid=(nt,),
        in_specs=[
            pl.BlockSpec((ts, EVEN_IN), lambda i: (i, 0)),
            pl.BlockSpec((HALO, EVEN_IN), before_map),
            pl.BlockSpec((HALO, EVEN_IN), after_map),
            pl.BlockSpec((ts, D_MODEL), lambda i: (i, 0)),
            pl.BlockSpec((HALO, D_MODEL), after_map),
            full(3, A_WIDTH), full(4, POOL_GROUP, POOL_GROUP), full(1, 4 * POOL_GROUP),
        ],
        out_specs=[pl.BlockSpec((ts, EVEN_IN), lambda i: (i, 0)), full(3, A_WIDTH), full(4, POOL_GROUP, POOL_GROUP),
                   full(1, 4 * POOL_GROUP)],
        out_shape=[_sds((S, EVEN_IN), BF16), _sds((3, A_WIDTH), F32), _sds((4, POOL_GROUP, POOL_GROUP), F32),
                   _sds((1, 4 * POOL_GROUP), F32)],
        compiler_params=_params("arbitrary"),
    )(proj, proj, proj, dmix, dmix, conv_w, pool_w, pool_scale)


FFN_HALO = 16
FFN_TC = 1408


GLU_CHUNKS = ((0, 512), (512, 512), (1024, 384))


def up_glu_fwd(xn, w_up, conv_w, conv_b, name, tm=512):
    S, K = xn.shape
    nc = D_FF // FFN_TC

    def body(xm_ref, xb_ref, wg_ref, wu_ref, cwg_ref, cwu_ref, cbg_ref, cbu_ref, pg_ref, pu_ref, ug_ref, uu_ref, o_ref):
        before = jnp.where(pl.program_id(1) > 0, xb_ref[...], jnp.zeros_like(xb_ref))
        rows = jnp.concatenate([before, xm_ref[...]], axis=0)
        for lo, width in GLU_CHUNKS:
            cols = slice(lo, lo + width)
            pre_g = jnp.dot(rows, wg_ref[:, cols], preferred_element_type=F32)
            pre_u = jnp.dot(rows, wu_ref[:, cols], preferred_element_type=F32)
            gate = _conv3(pre_g, cwg_ref[:, cols])[FFN_HALO:] + cbg_ref[:, cols]
            upv = _conv3(pre_u, cwu_ref[:, cols])[FFN_HALO:] + cbu_ref[:, cols]
            pg_ref[:, cols] = pre_g[FFN_HALO:].astype(BF16)
            pu_ref[:, cols] = pre_u[FFN_HALO:].astype(BF16)
            ug_ref[:, cols] = gate.astype(BF16)
            uu_ref[:, cols] = upv.astype(BF16)
            o_ref[:, cols] = ((gate * (1.0 / (1.0 + jnp.exp(-gate)))) * upv).astype(BF16)

    hb = tm // FFN_HALO
    wspec = lambda off: pl.BlockSpec((None, None, K, FFN_TC), lambda j, m: (j + off, 0, 0, 0))
    cw = lambda off: pl.BlockSpec((3, FFN_TC), lambda j, m: (0, j + off))
    cb = lambda off: pl.BlockSpec((1, FFN_TC), lambda j, m: (0, j + off))
    out = pl.BlockSpec((tm, FFN_TC), lambda j, m: (m, j))
    pg, pu, ug, uu, act = pl.pallas_call(
        body, name=name, grid=(nc, S // tm),
        in_specs=[pl.BlockSpec((tm, K), lambda j, m: (m, 0)),
                  pl.BlockSpec((FFN_HALO, K), lambda j, m: (jnp.maximum(m * hb - 1, 0), 0)),
                  wspec(0), wspec(nc), cw(0), cw(nc), cb(0), cb(nc)],
        out_specs=[out] * 5, out_shape=[_sds((S, D_FF), BF16)] * 5,
        compiler_params=_params("parallel", "parallel"),
    )(xn, xn, w_up, w_up, conv_w, conv_w, conv_b, conv_b)
    return (pg, pu), (ug, uu), act


def glu_bwd(up, u, da, conv_w, name, ts=256):
    S = up[0].shape[0]
    nc = D_FF // FFN_TC
    nt = S // ts
    W = 2 * D_FF

    def body(xg_ref, xu_ref, gm_ref, ga_ref, um_ref, ua_ref, dm_ref, da_ref, cw_ref, dx_ref, dcw_ref, dcb_ref):
        i = pl.program_id(0)
        last = i == nt - 1

        @pl.when(i == 0)
        def _():
            dcw_ref[...] = jnp.zeros_like(dcw_ref)
            dcb_ref[...] = jnp.zeros_like(dcb_ref)

        def rows(m_ref, a_ref, cols):
            return jnp.concatenate([m_ref[:, cols], a_ref[:, cols]], axis=0).astype(F32)

        def back(d, x, cols):
            cw = cw_ref[:, cols]
            d1, d2 = _shift_up(d, 1), _shift_up(d, 2)
            dx_ref[:, cols] = ((d * cw[2:3] + d1 * cw[1:2]) + d2 * cw[0:1])[:ts].astype(BF16)
            dcb_ref[:, cols] += jnp.sum(d[:ts], axis=0, keepdims=True)
            dcw_ref[0:1, cols] += jnp.sum(d2[:ts] * x, axis=0, keepdims=True)
            dcw_ref[1:2, cols] += jnp.sum(d1[:ts] * x, axis=0, keepdims=True)
            dcw_ref[2:3, cols] += jnp.sum(d[:ts] * x, axis=0, keepdims=True)

        for c in range(nc):
            cols = slice(c * FFN_TC, (c + 1) * FFN_TC)
            ug, uu = rows(gm_ref, ga_ref, cols), rows(um_ref, ua_ref, cols)
            dae = rows(dm_ref, da_ref, cols)
            dae = jnp.where(last & (lax.broadcasted_iota(jnp.int32, dae.shape, 0) >= ts), 0.0, dae)
            sg = 1.0 / (1.0 + jnp.exp(-ug))
            duu = dae * (ug * sg)
            dug = (dae * uu) * (sg * (1.0 + ug * (1.0 - sg)))
            back(dug, xg_ref[:, cols].astype(F32), cols)
            back(duu, xu_ref[:, cols].astype(F32), slice(D_FF + c * FFN_TC, D_FF + (c + 1) * FFN_TC))

    hb = ts // FFN_HALO
    nh = S // FFN_HALO
    after_map = lambda i: (jnp.minimum((i + 1) * hb, nh - 1), 0)
    main = pl.BlockSpec((ts, D_FF), lambda i: (i, 0))
    after = pl.BlockSpec((FFN_HALO, D_FF), after_map)
    return pl.pallas_call(
        body, name=name, grid=(nt,),
        in_specs=[main, main, main, after, main, after, main, after, pl.BlockSpec((3, W), lambda i: (0, 0))],
        out_specs=[pl.BlockSpec((ts, W), lambda i: (i, 0)), pl.BlockSpec((3, W), lambda i: (0, 0)),
                   pl.BlockSpec((1, W), lambda i: (0, 0))],
        out_shape=[_sds((S, W), BF16), _sds((3, W), F32), _sds((1, W), F32)],
        compiler_params=_params("arbitrary"),
    )(up[0], up[1], u[0], u[0], u[1], u[1], da, da, conv_w)


MEAN_GROUP = 256


def _head_mean_matrix():
    h = np.arange(MEAN_GROUP) // HEAD_DIM
    return jnp.asarray((h[:, None] == h[None, :]).astype(np.float32) / HEAD_DIM, dtype=BF16)


def _head_mean(v, gm):
    vb = v.astype(BF16)
    return jnp.concatenate([jnp.dot(vb[:, c:c + MEAN_GROUP], gm, preferred_element_type=F32)
                            for c in range(0, v.shape[1], MEAN_GROUP)], axis=1)


def qkv_qknorm_fwd(xn, w_qkv, gqk, name, tm=1024):
    S, K = xn.shape
    J, _, _, Ns = w_qkv.shape
    gains = gqk.reshape(1, 3 * D_MODEL)

    def body(x_ref, w_ref, g_ref, gm_ref, raw_ref, o_ref):
        first_col = pl.program_id(0) * Ns
        acc = jnp.dot(x_ref[...], w_ref[...], preferred_element_type=F32)
        raw_ref[...] = acc
        gm = gm_ref[...]
        for c in range(0, Ns, MEAN_GROUP):
            cols = slice(c, c + MEAN_GROUP)
            x = acc[:, cols]
            mean = jnp.dot((x * x).astype(BF16), gm, preferred_element_type=F32)
            normed = (x * lax.rsqrt(mean + EPS)) * g_ref[:, cols]
            o_ref[:, cols] = jnp.where(first_col + c >= 2 * D_MODEL, x, normed).astype(BF16)

    return pl.pallas_call(
        body, name=name, grid=(J, S // tm),
        in_specs=[pl.BlockSpec((tm, K), lambda j, m: (m, 0)), pl.BlockSpec((None, None, K, Ns), lambda j, m: (j, 0, 0, 0)),
                  pl.BlockSpec((1, Ns), lambda j, m: (0, j)), pl.BlockSpec((MEAN_GROUP, MEAN_GROUP), lambda j, m: (0, 0))],
        out_specs=[pl.BlockSpec((tm, Ns), lambda j, m: (m, j))] * 2,
        out_shape=[_sds((S, J * Ns), F32), _sds((S, J * Ns), BF16)], compiler_params=_params("parallel", "parallel"),
    )(xn, w_qkv, gains, _head_mean_matrix())


def qknorm_bwd(qkv, dq, dk, dv, gqk, name, ts=256):
    S = qkv.shape[0]

    def body(x_ref, dq_ref, dk_ref, dv_ref, g_ref, gm_ref, o_ref, dg_ref):
        @pl.when(pl.program_id(0) == 0)
        def _():
            dg_ref[...] = jnp.zeros_like(dg_ref)

        gm = gm_ref[...]
        for part, d_ref in enumerate((dq_ref, dk_ref)):
            cols = slice(part * D_MODEL, (part + 1) * D_MODEL)
            x = x_ref[:, cols]
            d = d_ref[...]
            r = lax.rsqrt(_head_mean(x * x, gm) + EPS)
            gx = d * g_ref[part]
            o_ref[:, cols] = (r * gx - x * ((r * r * r) * _head_mean(gx * x, gm))).astype(BF16)
            dg_ref[part] += jnp.sum(d * (x * r), axis=0, keepdims=True)
        o_ref[:, 2 * D_MODEL:] = dv_ref[...].astype(BF16)

    row = pl.BlockSpec((ts, D_MODEL), lambda i: (i, 0))
    wide = pl.BlockSpec((ts, 3 * D_MODEL), lambda i: (i, 0))
    gains = pl.BlockSpec((3, 1, D_MODEL), lambda i: (0, 0, 0))
    return pl.pallas_call(
        body, name=name, grid=(S // ts,),
        in_specs=[wide, row, row, row, gains, pl.BlockSpec((MEAN_GROUP, MEAN_GROUP), lambda i: (0, 0))],
        out_specs=[wide, gains],
        out_shape=[_sds((S, 3 * D_MODEL), BF16), _sds((3, 1, D_MODEL), F32)],
        compiler_params=_params("arbitrary"),
    )(qkv, dq, dk, dv, gqk, _head_mean_matrix())


RESIDUES = 16


def _block_order(dil):
    runs = RESIDUES // dil
    slot = np.arange(ATT_BLOCK)
    return (slot % (ATT_BLOCK // runs)) * runs + slot // (ATT_BLOCK // runs)


def _bucket_tables():
    n = ATT_BLOCK
    max_exact = N_REL_BUCKETS // 2
    buckets, valids = [], []
    for _, dil in DILATED_PAIRS:
        order = _block_order(dil)
        a = order[:, None]
        c = np.concatenate([order, n + order])[None, :]
        first_half = (np.arange(2 * n) < n)[None, :]
        rel = a + n - c
        band = (rel >= 0) & (rel <= n)
        dist = np.clip(rel, 0, n) * dil
        dd = np.maximum(dist, 1).astype(np.float32)
        large = max_exact + (np.log(dd / np.float32(max_exact)) / np.float32(math.log(REL_MAX_DISTANCE / max_exact))
                             * np.float32(N_REL_BUCKETS - max_exact)).astype(np.int32)
        large = np.minimum(large, N_REL_BUCKETS - 1)
        buckets.append(np.where(dist < max_exact, dist, large).reshape(1, -1))
        valids.append(np.stack([(band & ~first_half).reshape(1, -1), band.reshape(1, -1)]))
    return np.stack(buckets).astype(np.int32), np.stack(valids).astype(np.int32)


BIAS_CHUNK = 8192


def _split3(x):
    a = x.astype(BF16)
    r = x - a.astype(F32)
    b = r.astype(BF16)
    c = (r - b.astype(F32)).astype(BF16)
    return a, b, c


def bias_expand(rel_bias_t, name):
    bucket, valid = _bucket_tables()
    nq = bucket.shape[-1]

    def body(t_ref, b_ref, v_ref, o_ref):
        onehot = (lax.broadcasted_iota(jnp.int32, (N_REL_BUCKETS, BIAS_CHUNK), 0) == b_ref[...]).astype(BF16)
        acc = None
        for term in _split3(t_ref[...]):
            p = jnp.dot(term, onehot, preferred_element_type=F32)
            acc = p if acc is None else acc + p
        o_ref[...] = jnp.where(v_ref[...] > 0, acc, MASK_VALUE)

    return pl.pallas_call(
        body, name=name, grid=(3, 2, nq // BIAS_CHUNK),
        in_specs=[pl.BlockSpec((N_HEADS, N_REL_BUCKETS), lambda b, v, c: (0, 0)),
                  pl.BlockSpec((None, 1, BIAS_CHUNK), lambda b, v, c: (b, 0, c)),
                  pl.BlockSpec((None, None, 1, BIAS_CHUNK), lambda b, v, c: (b, v, 0, c))],
        out_specs=pl.BlockSpec((None, None, N_HEADS, BIAS_CHUNK), lambda b, v, c: (b, v, 0, c)),
        out_shape=_sds((3, 2, N_HEADS, nq), F32), compiler_params=_params("parallel", "parallel", "parallel"),
    )(rel_bias_t, jnp.asarray(bucket), jnp.asarray(valid))


def bias_reduce(dbias, name):
    bucket, _ = _bucket_tables()
    nq = bucket.shape[-1]
    dims = (((1,), (1,)), ((), ()))

    def body(d_ref, b_ref, o_ref):
        onehot = (lax.broadcasted_iota(jnp.int32, (N_REL_BUCKETS, BIAS_CHUNK), 0) == b_ref[...]).astype(BF16)
        acc = None
        for term in _split3(d_ref[...]):
            p = lax.dot_general(term, onehot, dims, preferred_element_type=F32)
            acc = p if acc is None else acc + p

        @pl.when(pl.program_id(1) == 0)
        def _():
            o_ref[...] = acc

        @pl.when(pl.program_id(1) > 0)
        def _():
            o_ref[...] += acc

    return pl.pallas_call(
        body, name=name, grid=(3, nq // BIAS_CHUNK),
        in_specs=[pl.BlockSpec((None, N_HEADS, BIAS_CHUNK), lambda b, c: (b, 0, c)),
                  pl.BlockSpec((None, 1, BIAS_CHUNK), lambda b, c: (b, 0, c))],
        out_specs=pl.BlockSpec((None, N_HEADS, N_REL_BUCKETS), lambda b, c: (b, 0, 0)),
        out_shape=_sds((3, N_HEADS, N_REL_BUCKETS), F32), compiler_params=_params("parallel", "arbitrary"),
    )(dbias, jnp.asarray(bucket))


PAIR = 2 * HEAD_DIM
N_PAIRS = N_HEADS // 2
_NT = (((1,), (1,)), ((), ()))
_TN = (((0,), (0,)), ((), ()))


def _low_lanes(shape):
    return lax.broadcasted_iota(jnp.int32, shape, 1) < HEAD_DIM


ATTN_VMEM_LIMIT_BYTES = 56 * 1024 * 1024
BRANCH_ORDER = (2, 1, 0)


def _regroup(dst, src, L16):
    for r in range(RESIDUES):
        dst[pl.ds(r * L16, L16), :] = src[pl.ds(r, L16, stride=RESIDUES), :]


def _ungroup(dst, src, L16):
    for r in range(RESIDUES):
        dst[pl.ds(r, L16, stride=RESIDUES), :] = src[pl.ds(r * L16, L16), :]


def _branch_geometry(branch, S):
    dil = DILATED_PAIRS[branch][1]
    runs = RESIDUES // dil
    return dil, runs, ATT_BLOCK // runs, S // dil // ATT_BLOCK


def _block_rows(it, branch, S):
    dil, runs, run_len, n_blocks = _branch_geometry(branch, S)
    L16 = S // RESIDUES
    r, b = it // n_blocks, it % n_blocks
    prev = jnp.maximum(b - 1, 0)
    cur_rows = [pl.multiple_of((j * dil + r) * L16 + run_len * b, 8) for j in range(runs)]
    prev_rows = [pl.multiple_of((j * dil + r) * L16 + run_len * prev, 8) for j in range(runs)]
    return cur_rows, prev_rows, jnp.minimum(b, 1)


def _load_block(ref, rows, run_len):
    parts = [ref[pl.ds(o, run_len), :] for o in rows]
    return parts[0] if len(parts) == 1 else jnp.concatenate(parts, axis=0)


def _store_block(ref, rows, run_len, value, add=False):
    for j, o in enumerate(rows):
        part = value[j * run_len:(j + 1) * run_len]
        if add:
            ref[pl.ds(o, run_len), :] += part
        else:
            ref[pl.ds(o, run_len), :] = part


ATTN_FWD_UNROLL = 8
ATTN_BWD_UNROLL = 4


def _stack_heads(x, low):
    zero = jnp.zeros_like(x)
    return jnp.concatenate([jnp.where(low, x, zero), jnp.where(low, zero, x)], axis=0)


def _unstack_heads(y, low):
    return jnp.where(low, y[:ATT_BLOCK], y[ATT_BLOCK:])


def attn_fwd(qkvn, bias, name):
    S = qkvn.shape[0]
    L16 = S // RESIDUES
    n_iter = S // ATT_BLOCK

    def body(q_ref, k_ref, v_ref, b_ref, o_ref, lse_ref, stage, qp, kp, vp, acc_s, m_s, l_s):
        for src, dst in ((q_ref, qp), (k_ref, kp), (v_ref, vp)):
            stage[...] = src[...].astype(F32)
            _regroup(dst, stage, L16)
        low = _low_lanes((ATT_BLOCK, PAIR))

        for branch in BRANCH_ORDER:
            _, _, run_len, _ = _branch_geometry(branch, S)
            first = branch == BRANCH_ORDER[0]

            def step(it, carry, branch=branch, run_len=run_len, first=first):
                cur, prev, variant = _block_rows(it, branch, S)
                q = _load_block(qp, cur, run_len).astype(BF16)
                k = jnp.concatenate([_load_block(kp, prev, run_len), _load_block(kp, cur, run_len)], axis=0).astype(BF16)
                v = jnp.concatenate([_load_block(vp, prev, run_len), _load_block(vp, cur, run_len)], axis=0).astype(BF16)
                s = lax.dot_general(_stack_heads(q, low), k, _NT, preferred_element_type=F32) * (HEAD_DIM ** -0.5)
                s = s + b_ref[2 * branch + variant].reshape(2 * ATT_BLOCK, 2 * ATT_BLOCK)
                mx = jnp.max(s, axis=-1, keepdims=True)
                p = jnp.exp(s - mx)
                den = jnp.sum(p, axis=-1, keepdims=True)
                pv = jnp.dot(p.astype(BF16), v, preferred_element_type=F32)
                acc = _unstack_heads(pv, low)
                m = _unstack_heads(mx, low)
                l = _unstack_heads(den, low)
                if not first:
                    m_old = _load_block(m_s, cur, run_len)
                    m_new = jnp.maximum(m_old, m)
                    a_old, a_new = jnp.exp(m_old - m_new), jnp.exp(m - m_new)
                    acc = _load_block(acc_s, cur, run_len) * a_old + acc * a_new
                    l = _load_block(l_s, cur, run_len) * a_old + l * a_new
                    m = m_new
                _store_block(acc_s, cur, run_len, acc)
                _store_block(m_s, cur, run_len, m)
                _store_block(l_s, cur, run_len, l)
                return carry

            lax.fori_loop(0, n_iter, step, 0, unroll=ATTN_FWD_UNROLL)

        acc_s[...] = acc_s[...] / l_s[...]
        _ungroup(stage, acc_s, L16)
        o_ref[...] = stage[...].astype(BF16)
        m_s[...] = m_s[...] + jnp.log(l_s[...])
        _ungroup(lse_ref, m_s, L16)

    col = lambda part: pl.BlockSpec((S, PAIR), lambda hp: (0, part * N_PAIRS + hp))
    out = pl.BlockSpec((S, PAIR), lambda hp: (0, hp))
    return pl.pallas_call(
        body, name=name, grid=(N_PAIRS,),
        in_specs=[col(0), col(1), col(2), pl.BlockSpec((6, 2, ATT_BLOCK, 2 * ATT_BLOCK), lambda hp: (0, hp, 0, 0))],
        out_specs=[out, out], out_shape=[_sds((S, D_MODEL), BF16), _sds((S, D_MODEL), F32)],
        scratch_shapes=[pltpu.VMEM((S, PAIR), F32)] * 7,
        compiler_params=pltpu.CompilerParams(dimension_semantics=("parallel",), vmem_limit_bytes=ATTN_VMEM_LIMIT_BYTES),
    )(qkvn, qkvn, qkvn, bias)


def attn_bwd(qkvn, att, datt, lse, bias, name):
    S = qkvn.shape[0]
    L16 = S // RESIDUES
    n_iter = S // ATT_BLOCK
    TILE = 512

    def body(q_ref, k_ref, v_ref, o_ref, do_ref, lse_ref, b_ref, dq_ref, dk_ref, dv_ref, db_ref,
             qp, kp, vp, dop, ldp, dqp, dkp, dvp):
        stage = dqp
        for src, dst in ((q_ref, qp), (k_ref, kp), (v_ref, vp), (do_ref, dop)):
            stage[...] = src[...].astype(F32)
            _regroup(dst, stage, L16)

        def pack(i, carry):
            rows = pl.ds(pl.multiple_of(i * TILE, TILE), TILE)
            low = _low_lanes((TILE, PAIR))
            lane = lax.broadcasted_iota(jnp.int32, (TILE, PAIR), 1)
            prod = do_ref[rows, :].astype(F32) * o_ref[rows, :].astype(F32)
            d0 = jnp.sum(jnp.where(low, prod, 0.0), axis=-1, keepdims=True)
            d1 = jnp.sum(jnp.where(low, 0.0, prod), axis=-1, keepdims=True)
            stage[rows, :] = jnp.where((lane & (HEAD_DIM // 2)) == 0, lse_ref[rows, :], jnp.where(low, d0, d1))
            return carry

        lax.fori_loop(0, S // TILE, pack, 0)
        _regroup(ldp, stage, L16)
        dqp[...] = jnp.zeros_like(dqp)
        dkp[...] = jnp.zeros_like(dkp)
        dvp[...] = jnp.zeros_like(dvp)
        db_ref[...] = jnp.zeros_like(db_ref)
        low = _low_lanes((ATT_BLOCK, PAIR))

        for branch in BRANCH_ORDER:
            _, _, run_len, _ = _branch_geometry(branch, S)

            def step(it, carry, branch=branch, run_len=run_len):
                cur, prev, variant = _block_rows(it, branch, S)
                q = _load_block(qp, cur, run_len).astype(BF16)
                dout = _load_block(dop, cur, run_len).astype(BF16)
                ld = _load_block(ldp, cur, run_len)
                k = jnp.concatenate([_load_block(kp, prev, run_len), _load_block(kp, cur, run_len)], axis=0).astype(BF16)
                v = jnp.concatenate([_load_block(vp, prev, run_len), _load_block(vp, cur, run_len)], axis=0).astype(BF16)
                half = HEAD_DIM // 2
                lse2 = jnp.concatenate([ld[:, 0:1], ld[:, HEAD_DIM:HEAD_DIM + 1]], axis=0)
                delta2 = jnp.concatenate([ld[:, half:half + 1], ld[:, HEAD_DIM + half:HEAD_DIM + half + 1]], axis=0)
                q2, do2 = _stack_heads(q, low), _stack_heads(dout, low)
                s = lax.dot_general(q2, k, _NT, preferred_element_type=F32) * (HEAD_DIM ** -0.5)
                p = jnp.exp(s + b_ref[2 * branch + variant].reshape(2 * ATT_BLOCK, 2 * ATT_BLOCK) - lse2)
                dp = lax.dot_general(do2, v, _NT, preferred_element_type=F32)
                ds = p * (dp - delta2)
                db_ref[branch] += ds.reshape(2, ATT_BLOCK, 2 * ATT_BLOCK)
                dsb = (ds * (HEAD_DIM ** -0.5)).astype(BF16)
                dq = _unstack_heads(jnp.dot(dsb, k, preferred_element_type=F32), low)
                dk = lax.dot_general(dsb, q2, _TN, preferred_element_type=F32)
                dv = lax.dot_general(p.astype(BF16), do2, _TN, preferred_element_type=F32)
                _store_block(dqp, cur, run_len, dq, add=True)
                _store_block(dkp, prev, run_len, dk[:ATT_BLOCK], add=True)
                _store_block(dvp, prev, run_len, dv[:ATT_BLOCK], add=True)
                _store_block(dkp, cur, run_len, dk[ATT_BLOCK:], add=True)
                _store_block(dvp, cur, run_len, dv[ATT_BLOCK:], add=True)
                return carry

            lax.fori_loop(0, n_iter, step, 0, unroll=ATTN_BWD_UNROLL)

        _ungroup(dq_ref, dqp, L16)
        _ungroup(dk_ref, dkp, L16)
        _ungroup(dv_ref, dvp, L16)

    col = lambda part: pl.BlockSpec((S, PAIR), lambda hp: (0, part * N_PAIRS + hp))
    one = pl.BlockSpec((S, PAIR), lambda hp: (0, hp))
    return pl.pallas_call(
        body, name=name, grid=(N_PAIRS,),
        in_specs=[col(0), col(1), col(2), one, one, one,
                  pl.BlockSpec((6, 2, ATT_BLOCK, 2 * ATT_BLOCK), lambda hp: (0, hp, 0, 0))],
        out_specs=[one, one, one, pl.BlockSpec((3, 2, ATT_BLOCK, 2 * ATT_BLOCK), lambda hp: (0, hp, 0, 0))],
        out_shape=[_sds((S, D_MODEL), F32)] * 3 + [_sds((3, N_HEADS, ATT_BLOCK, 2 * ATT_BLOCK), F32)],
        scratch_shapes=[pltpu.VMEM((S, PAIR), F32)] * 8,
        compiler_params=pltpu.CompilerParams(dimension_semantics=("parallel",), vmem_limit_bytes=ATTN_VMEM_LIMIT_BYTES),
    )(qkvn, qkvn, qkvn, att, datt, lse, bias)


def adamw(w, g, m, v, name):
    n, R, C = w.shape

    def body(w_ref, g_ref, m_ref, v_ref, d_ref, nm_ref, nv_ref, go_ref):
        gv = g_ref[...]
        go_ref[...] = gv
        m2 = ADAM_B1 * m_ref[...] + (1.0 - ADAM_B1) * gv
        v2 = ADAM_B2 * v_ref[...] + (1.0 - ADAM_B2) * (gv * gv)
        m_hat = m2 / (1.0 - ADAM_B1 ** ADAM_STEP)
        v_hat = v2 / (1.0 - ADAM_B2 ** ADAM_STEP)
        d_ref[...] = -ADAM_LR * (m_hat / (jnp.sqrt(v_hat) + ADAM_EPS) + ADAM_WD * w_ref[...])
        nm_ref[...] = m2
        nv_ref[...] = v2

    tr = R
    while tr * C * 4 > ELEMENTWISE_BLOCK_BYTES and tr % 16 == 0:
        tr //= 2
    spec = pl.BlockSpec((None, tr, C), lambda i, r: (i, r, 0))
    return pl.pallas_call(
        body, name=name, grid=(n, R // tr), in_specs=[spec] * 4, out_specs=[spec] * 4,
        out_shape=[_sds((n, R, C), F32)] * 4, compiler_params=_params("parallel", "parallel"),
    )(w, g, m, v)


ANY = pl.BlockSpec(memory_space=pl.ANY)


def _coords():
    return lax.axis_index("x"), lax.axis_index("y"), lax.axis_index("c")


def _other_chips(mx, my):
    return [(1 - mx, my), (mx, 1 - my), (1 - mx, 1 - my)]


def _remote(src, dst, send, recv, dev):
    return pltpu.make_async_remote_copy(src_ref=src, dst_ref=dst, send_sem=send, recv_sem=recv, device_id=dev,
                                        device_id_type=MESH)


HBM =pl.BlockSpec(memory_space=pltpu.HBM)
SEM = pl.BlockSpec(memory_space=pltpu.SEMAPHORE)
_SPLIT_COPY = pltpu.CompilerParams(has_side_effects=pltpu.SideEffectType.DATAFLOW_SIDE_EFFECTING)


def _in_hbm(a):
    return pltpu.with_memory_space_constraint(a, pltpu.HBM)


def cast_into_slot(w, layer, chip_core, name, dtype=BF16):
    _, _, hR, C = w.shape

    def body(s_ref, w_ref, o_ref):
        del s_ref
        o_ref[...] = w_ref[...].astype(dtype)

    grid_spec = pltpu.PrefetchScalarGridSpec(
        num_scalar_prefetch=1, grid=(2,),
        in_specs=[pl.BlockSpec((None, None, hR, C), lambda h, s: (layer, h, 0, 0))],
        out_specs=pl.BlockSpec((None, None, hR, C), lambda h, s: (s[0], h, 0, 0)))
    return pl.pallas_call(body, name=name, grid_spec=grid_spec, out_shape=_sds((N_CHIPS, 2, hR, C), dtype),
                          compiler_params=_params("parallel"))(chip_core, w)


def gather_start(lands, groups, name):
    n = len(lands)
    n_groups = len(groups)

    def body(*refs):
        ins = refs[:n]
        sems = refs[n:n + 2 * n_groups]
        token = refs[-1]
        mx, my, mc = _coords()
        chip = 2 * mx + my
        for g, members in enumerate(groups):
            send, recv = sems[2 * g], sems[2 * g + 1]
            for i, a in enumerate(members):
                mine = ins[a].at[chip, mc]
                for k, (px, py) in enumerate(_other_chips(mx, my)):
                    _remote(mine, mine, send.at[3 * i + k], recv.at[3 * i + k], (px, py, mc)).start()
        token[...] = jnp.zeros_like(token)

    sem_shapes = []
    for members in groups:
        sem_shapes += [pltpu.SemaphoreType.DMA((3 * len(members),))] * 2
    outs = pl.pallas_call(
        body, name=name, in_specs=[HBM] * n,
        out_specs=[SEM] * (2 * n_groups) + [HBM] * n + [pl.BlockSpec(memory_space=pltpu.VMEM)],
        out_shape=sem_shapes + [pltpu.HBM(a.shape, a.dtype) for a in lands] + [_sds((SUBLANES, LANES), F32)],
        input_output_aliases={a: 2 * n_groups + a for a in range(n)}, compiler_params=_SPLIT_COPY,
    )(*[_in_hbm(a) for a in lands])
    sems = [(outs[2 * g], outs[2 * g + 1]) for g in range(n_groups)]
    return sems, list(outs[2 * n_groups:2 * n_groups + n]), outs[-1]


def gather_forward(lands, sems, after, name):
    n = len(lands)

    def body(*refs):
        ins = refs[:n]
        send, recv = refs[n], refs[n + 1]
        fsend, frecv = refs[n + 3], refs[n + 4]
        mx, my, mc = _coords()
        for i in range(n):
            for k, (px, py) in enumerate(_other_chips(mx, my)):
                landed = ins[i].at[2 * px + py, mc]
                cp = _remote(landed, landed, send.at[3 * i + k], recv.at[3 * i + k], (px, py, mc))
                cp.wait_send()
                cp.wait_recv()
                _remote(landed, landed, fsend.at[3 * i + k], frecv.at[3 * i + k], (mx, my, 1 - mc)).start()

    outs = pl.pallas_call(
        body, name=name, in_specs=[HBM] * n + [SEM, SEM, ANY], out_specs=[SEM, SEM] + [HBM] * n,
        out_shape=[pltpu.SemaphoreType.DMA((3 * n,))] * 2 + [pltpu.HBM(a.shape, a.dtype) for a in lands],
        input_output_aliases={a: 2 + a for a in range(n)}, compiler_params=_SPLIT_COPY,
    )(*lands, sems[0], sems[1], after)
    return (outs[0], outs[1]), list(outs[2:])


def gather_wait(lands, sems, after, name):
    n = len(lands)

    def body(*refs):
        ins = refs[:n]
        fsend, frecv = refs[n], refs[n + 1]
        mx, my, mc = _coords()
        for i in range(n):
            for k, (px, py) in enumerate(_other_chips(mx, my)):
                theirs = ins[i].at[2 * px + py, 1 - mc]
                cp = _remote(theirs, theirs, fsend.at[3 * i + k], frecv.at[3 * i + k], (mx, my, 1 - mc))
                cp.wait_send()
                cp.wait_recv()

    outs = pl.pallas_call(
        body, name=name, in_specs=[HBM] * n + [SEM, SEM, ANY], out_specs=[HBM] * n,
        out_shape=[pltpu.HBM(a.shape, a.dtype) for a in lands],
        input_output_aliases={a: a for a in range(n)}, compiler_params=_SPLIT_COPY,
    )(*lands, sems[0], sems[1], after)
    return list(outs)


def _peers(mx, my, mc):
    return [(1 - mx if k & 4 else mx, 1 - my if k & 2 else my, 1 - mc if k & 1 else mc) for k in range(1, N_DEV)]


def devices_start(x, name):
    def body(x_ref, land_ref, send, recv, x_thru, land_thru):
        mx, my, mc = _coords()
        me = 4 * mx + 2 * my + mc
        for k, peer in enumerate(_peers(mx, my, mc)):
            _remote(x_ref, land_ref.at[me], send.at[k], recv.at[k], peer).start()

    land = lax.empty((N_DEV,) + x.shape, x.dtype)
    outs = pl.pallas_call(
        body, name=name, in_specs=[HBM, HBM], out_specs=[SEM, SEM, HBM, HBM],
        out_shape=[pltpu.SemaphoreType.DMA((N_DEV - 1,))] * 2 + [pltpu.HBM(x.shape, x.dtype), pltpu.HBM(land.shape, x.dtype)],
        input_output_aliases={0: 2, 1: 3}, compiler_params=_SPLIT_COPY,
    )(_in_hbm(x), _in_hbm(land))
    return (outs[0], outs[1]), outs[2], outs[3]


def devices_wait(x, land, sems, after, name):
    def body(x_ref, land_ref, send, recv, after_ref, x_thru, land_thru):
        mx, my, mc = _coords()
        for k, (px, py, pc) in enumerate(_peers(mx, my, mc)):
            cp = _remote(x_ref, land_ref.at[4 * px + 2 * py + pc], send.at[k], recv.at[k], (px, py, pc))
            cp.wait_send()
            cp.wait_recv()

    outs = pl.pallas_call(
        body, name=name, in_specs=[HBM, HBM, SEM, SEM, ANY], out_specs=[HBM, HBM],
        out_shape=[pltpu.HBM(x.shape, x.dtype), pltpu.HBM(land.shape, land.dtype)],
        input_output_aliases={0: 0, 1: 1}, compiler_params=_SPLIT_COPY,
    )(x, land, sems[0], sems[1], after)
    return outs[0], outs[1]


def device_sum(land, own, me, name):
    _, R, C = land.shape

    def body(s_ref, l_ref, o_ref_in, o_ref):
        acc = None
        for q in range(N_DEV):
            term = jnp.where(s_ref[0] == q, o_ref_in[...], l_ref[q])
            acc = term if acc is None else acc + term
        o_ref[...] = acc

    grid_spec = pltpu.PrefetchScalarGridSpec(
        num_scalar_prefetch=1, grid=(1,),
        in_specs=[pl.BlockSpec((N_DEV, R, C), lambda i, s: (0, 0, 0)), pl.BlockSpec((R, C), lambda i, s: (0, 0))],
        out_specs=pl.BlockSpec((R, C), lambda i, s: (0, 0)))
    return pl.pallas_call(body, name=name, grid_spec=grid_spec, out_shape=_sds((R, C), F32),
                          compiler_params=_params("arbitrary"))(me, land, own)


def reduce_send(grads, name):
    n = len(grads)

    def body(*refs):
        ins, lands = refs[:n], refs[n:2 * n]
        send, recv = refs[2 * n], refs[2 * n + 1]
        mx, my, mc = _coords()
        me = 4 * mx + 2 * my + mc
        for a in range(n):
            for k, (px, py, pc) in enumerate(_peers(mx, my, mc)):
                _remote(ins[a].at[2 * px + py, pc], lands[a].at[me], send.at[7 * a + k], recv.at[7 * a + k], (px, py, pc)).start()

    lands = [lax.empty((N_DEV,) + g.shape[2:], g.dtype) for g in grads]
    outs = pl.pallas_call(
        body, name=name, in_specs=[HBM] * (2 * n), out_specs=[SEM, SEM] + [HBM] * (2 * n),
        out_shape=[pltpu.SemaphoreType.DMA((7 * n,))] * 2 + [pltpu.HBM(a.shape, a.dtype) for a in grads + lands],
        input_output_aliases={a: 2 + a for a in range(2 * n)}, compiler_params=_SPLIT_COPY,
    )(*[_in_hbm(a) for a in grads + lands])
    return (outs[0], outs[1]), list(outs[2:2 + n]), list(outs[2 + n:])


def reduce_wait(grads, lands, sems, after, name):
    n = len(grads)

    def body(*refs):
        ins, zones = refs[:n], refs[n:2 * n]
        send, recv = refs[2 * n], refs[2 * n + 1]
        mx, my, mc = _coords()
        for a in range(n):
            for k, (px, py, pc) in enumerate(_peers(mx, my, mc)):
                cp = _remote(ins[a].at[2 * px + py, pc], zones[a].at[4 * px + 2 * py + pc], send.at[7 * a + k],
                             recv.at[7 * a + k], (px, py, pc))
                cp.wait_send()
                cp.wait_recv()

    outs = pl.pallas_call(
        body, name=name, in_specs=[HBM] * (2 * n) + [SEM, SEM, ANY], out_specs=[HBM] * (2 * n),
        out_shape=[pltpu.HBM(a.shape, a.dtype) for a in grads + lands],
        input_output_aliases={a: a for a in range(2 * n)}, compiler_params=_SPLIT_COPY,
    )(*grads, *lands, sems[0], sems[1], after)
    return list(outs[:n]), list(outs[n:])


def reduce_sum(land, grad, place, name, into=None, layer=None):
    _, hR, C = land.shape
    tr = hR
    while N_DEV * tr * C * 2 > 3 * ELEMENTWISE_BLOCK_BYTES and tr % 32 == 0:
        tr //= 2

    def body(s_ref, l_ref, g_ref, *rest):
        o_ref = rest[-1]
        own = g_ref[...].astype(F32)
        acc = None
        for q in range(N_DEV):
            term = jnp.where(s_ref[2] == q, own, l_ref[q].astype(F32))
            acc = term if acc is None else acc + term
        o_ref[...] = acc

    in_specs = [pl.BlockSpec((N_DEV, tr, C), lambda i, s: (0, i, 0)),
                pl.BlockSpec((None, None, tr, C), lambda i, s: (s[0], s[1], i, 0))]
    args = [place, land, grad]
    aliases = {}
    if layer is None:
        out_spec = pl.BlockSpec((None, tr, C), lambda i, s: (s[1], i, 0))
        out_shape = _sds((2, hR, C), F32)
    else:
        out_spec = pl.BlockSpec((None, None, tr, C), lambda i, s: (layer, s[1], i, 0))
        out_shape = _sds((2, 2, hR, C), F32)
        if into is not None:
            in_specs.append(ANY)
            args.append(into)
            aliases = {3: 0}
    grid_spec = pltpu.PrefetchScalarGridSpec(num_scalar_prefetch=1, grid=(hR // tr,), in_specs=in_specs, out_specs=out_spec)
    return pl.pallas_call(body, name=name, grid_spec=grid_spec, out_shape=out_shape, input_output_aliases=aliases,
                          compiler_params=_params("arbitrary"))(*args)


def join_halves(arrays, name):
    n = len(arrays)
    pieces = [(a, l) for a, arr in enumerate(arrays) for l in (range(arr.shape[0]) if arr.ndim == 4 else [None])]

    def body(*refs):
        ins = refs[:n]
        send, recv = refs[2 * n:]
        mx, my, mc = _coords()

        def half(a, l, h):
            return ins[a].at[h] if l is None else ins[a].at[l, h]

        sends = [_remote(half(a, l, mc), half(a, l, mc), send.at[i], recv.at[i], (mx, my, 1 - mc))
                 for i, (a, l) in enumerate(pieces)]
        for cp in sends:
            cp.start()
        for i, (a, l) in enumerate(pieces):
            theirs = half(a, l, 1 - mc)
            _remote(theirs, theirs, send.at[i], recv.at[i], (mx, my, 1 - mc)).wait_recv()
        for cp in sends:
            cp.wait_send()

    return pl.pallas_call(
        body, name=name, in_specs=[ANY] * n, out_specs=[ANY] * n, out_shape=[_sds(a.shape, a.dtype) for a in arrays],
        input_output_aliases={a: a for a in range(n)},
        scratch_shapes=[pltpu.SemaphoreType.DMA((len(pieces),)), pltpu.SemaphoreType.DMA((len(pieces),))],
    )(*arrays)


LANES = 128
SUBLANES = 8


def _n_rows(shape):
    rows = -(-int(np.prod(shape)) // LANES)
    return -(-rows // SUBLANES) * SUBLANES


def _as_rows(a):
    flat = a.reshape(-1)
    rows = _n_rows(a.shape)
    return jnp.pad(flat, (0, rows * LANES - flat.shape[0])).reshape(rows, LANES)


def _pack(arrays):
    return jnp.concatenate([_as_rows(a) for a in arrays], axis=0)


def _unpack(rows, shapes):
    out, r0 = [], 0
    for s in shapes:
        n = _n_rows(s)
        out.append(rows[r0:r0 + n].reshape(-1)[:int(np.prod(s))].reshape(s))
        r0 += n
    return out


REPLICATED_SMALL = [("rel_bias", (32, 16)), ("even_norm", (1, 1024)), ("even_pool_w", (1, 4, 128, 128)),
                    ("even_pool_scale", (1, 512)), ("odd_q_norm", (1, 64)), ("odd_k_norm", (1, 64)),
                    ("ffn_norm", (2, 1024)), ("ffn_conv_b", (2, 5632))]
SHARDED_SMALL = [("even_conv_w", (1, 3, 128)), ("odd_norm", (1, 256)), ("ffn_conv_w", (2, 3, 1408))]
BIG = ["even_w_in", "even_w_out", "odd_w_qkv", "odd_w_o", "ffn_w_up", "ffn_w_down"]
WEIGHT_ORDER = ["rel_bias", "even_norm", "even_w_in", "even_conv_w", "even_pool_w", "even_pool_scale", "even_w_out",
                "odd_norm", "odd_w_qkv", "odd_q_norm", "odd_k_norm", "odd_w_o", "ffn_norm", "ffn_w_up", "ffn_conv_w",
                "ffn_conv_b", "ffn_w_down"]


def kernel(x, rel_bias, even_norm, even_w_in, even_conv_w, even_pool_w, even_pool_scale, even_w_out, odd_norm, odd_w_qkv, odd_q_norm, odd_k_norm, odd_w_o, ffn_norm, ffn_w_up, ffn_conv_w, ffn_conv_b, ffn_w_down, loss_target, m_rel_bias, m_even_norm, m_even_w_in, m_even_conv_w, m_even_pool_w, m_even_pool_scale, m_even_w_out, m_odd_norm, m_odd_w_qkv, m_odd_q_norm, m_odd_k_norm, m_odd_w_o, m_ffn_norm, m_ffn_w_up, m_ffn_conv_w, m_ffn_conv_b, m_ffn_w_down, v_rel_bias, v_even_norm, v_even_w_in, v_even_conv_w, v_even_pool_w, v_even_pool_scale, v_even_w_out, v_odd_norm, v_odd_w_qkv, v_odd_q_norm, v_odd_k_norm, v_odd_w_o, v_ffn_norm, v_ffn_w_up, v_ffn_conv_w, v_ffn_conv_b, v_ffn_w_down):
    W = dict(rel_bias=rel_bias, even_norm=even_norm, even_w_in=even_w_in, even_conv_w=even_conv_w, even_pool_w=even_pool_w,
             even_pool_scale=even_pool_scale, even_w_out=even_w_out, odd_norm=odd_norm, odd_w_qkv=odd_w_qkv,
             odd_q_norm=odd_q_norm, odd_k_norm=odd_k_norm, odd_w_o=odd_w_o, ffn_norm=ffn_norm, ffn_w_up=ffn_w_up,
             ffn_conv_w=ffn_conv_w, ffn_conv_b=ffn_conv_b, ffn_w_down=ffn_w_down)
    M1 = dict(rel_bias=m_rel_bias, even_norm=m_even_norm, even_w_in=m_even_w_in, even_conv_w=m_even_conv_w,
              even_pool_w=m_even_pool_w, even_pool_scale=m_even_pool_scale, even_w_out=m_even_w_out, odd_norm=m_odd_norm,
              odd_w_qkv=m_odd_w_qkv, odd_q_norm=m_odd_q_norm, odd_k_norm=m_odd_k_norm, odd_w_o=m_odd_w_o,
              ffn_norm=m_ffn_norm, ffn_w_up=m_ffn_w_up, ffn_conv_w=m_ffn_conv_w, ffn_conv_b=m_ffn_conv_b,
              ffn_w_down=m_ffn_w_down)
    M2 = dict(rel_bias=v_rel_bias, even_norm=v_even_norm, even_w_in=v_even_w_in, even_conv_w=v_even_conv_w,
              even_pool_w=v_even_pool_w, even_pool_scale=v_even_pool_scale, even_w_out=v_even_w_out, odd_norm=v_odd_norm,
              odd_w_qkv=v_odd_w_qkv, odd_q_norm=v_odd_q_norm, odd_k_norm=v_odd_k_norm, odd_w_o=v_odd_w_o,
              ffn_norm=v_ffn_norm, ffn_w_up=v_ffn_w_up, ffn_conv_w=v_ffn_conv_w, ffn_conv_b=v_ffn_conv_b,
              ffn_w_down=v_ffn_w_down)
    mx, my, mc = _coords()
    chip = 2 * mx + my
    me = 4 * mx + 2 * my + mc
    place = jnp.stack([chip, mc, me]).astype(jnp.int32)
    xs, target = x[0], loss_target[0]

    def halves(w):
        return w.reshape((w.shape[0], 2, w.shape[-2] // 2, w.shape[-1]))

    lands = [cast_into_slot(halves(even_w_in), 0, place, "cast_w_in"), cast_into_slot(halves(even_w_out), 0, place, "cast_w_out"),
             cast_into_slot(halves(ffn_w_up), 0, place, "cast_w_up0"), cast_into_slot(halves(ffn_w_down), 0, place, "cast_w_down0"),
             cast_into_slot(halves(odd_w_qkv), 0, place, "cast_w_qkv"), cast_into_slot(halves(odd_w_o), 0, place, "cast_w_o"),
             cast_into_slot(halves(ffn_w_up), 1, place, "cast_w_up1"), cast_into_slot(halves(ffn_w_down), 1, place, "cast_w_down1")]
    small_rows = jnp.pad(_pack([even_conv_w, odd_norm, ffn_conv_w]), ((0, SUBLANES), (0, 0)))
    lands.append(cast_into_slot(small_rows.reshape(1, 2, small_rows.shape[0] // 2, LANES), 0, place, "small_into_slot", dtype=F32))
    groups = [[0, 1, 8], [2], [3], [4, 5], [6], [7]]
    gather_sems, lands, token = gather_start(lands, groups, "gather_start")
    even_norm_after_start = even_norm + token[0:1, 0:1]

    def gathered(group, tag, after_landing, after_passing):
        mine = [lands[a] for a in groups[group]]
        sems, arrays = gather_forward(mine, gather_sems[group], after_landing, "gather_forward_" + tag)
        return gather_wait(arrays, sems, after_passing, "gather_wait_" + tag)

    pool_w = cast_bf16(even_pool_w[0], "cast_pool_w")
    gqk = jnp.stack([jnp.tile(odd_q_norm[0], N_HEADS), jnp.tile(odd_k_norm[0], N_HEADS),
                     jnp.ones((D_MODEL,), F32)])[:, None, :]
    bias = bias_expand(rel_bias.T, "bias_expand").reshape(6, N_HEADS, ATT_BLOCK, 2 * ATT_BLOCK)
    xn0 = rmsnorm_fwd(xs, even_norm_after_start, "even_norm")
    got = gathered(0, "even", bias, xn0)
    w_in = got[0].reshape(N_CHIPS, 1, D_MODEL, EVEN_IN // N_CHIPS)
    w_out = got[1].reshape(1, 1, D_MODEL, D_MODEL)
    small = got[2].reshape(N_CHIPS, small_rows.shape[0], LANES)
    conv_w_full = small[:, 0:3].transpose(1, 0, 2).reshape(3, A_WIDTH)
    odd_norm_full = small[:, 8:10].reshape(1, D_MODEL)
    ffn_cw_full = small[:, 16:82].reshape(N_CHIPS, 2, 3, 2 * D_FF // N_CHIPS).transpose(1, 2, 0, 3).reshape(2, 3, 2 * D_FF)

    def ffn_fwd(l, xin, xn):
        up, u, act = up_glu_fwd(xn, w_up[l], ffn_cw_full[l], ffn_conv_b[l:l + 1], f"ffn{l}_up_glu")
        return act, (xin, xn, up, u, act)

    w_up, w_down = [None, None], [None, None]
    proj, mix = in_mixer_fwd(xn0, w_in, conv_w_full, pool_w, even_pool_scale, "even_in_mixer")
    x1, xn1 = mm_res_norm(mix, w_out, xs, ffn_norm[0:1], "even_out")
    w_up[0] = gathered(1, "up0", proj, x1)[0].reshape(N_CHIPS, 1, D_MODEL, 2 * D_FF // N_CHIPS)
    act0, ffn0 = ffn_fwd(0, x1, xn1)
    w_down[0] = gathered(2, "down0", act0, act0)[0].reshape(1, 1, D_FF, D_MODEL)
    x2, xn2 = mm_res_norm(act0, w_down[0], x1, odd_norm_full, "ffn0_down")
    got = gathered(3, "odd", x1, x2)
    w_qkv = got[0].reshape(N_CHIPS, 1, D_MODEL, 3 * D_MODEL // N_CHIPS)
    w_o = got[1].reshape(1, 1, D_MODEL, D_MODEL)
    qkv, qkvn = qkv_qknorm_fwd(xn2, w_qkv, gqk, "odd_qkv_qknorm")
    att, lse = attn_fwd(qkvn, bias, "attn_fwd")
    x3, xn3 = mm_res_norm(att, w_o, x2, ffn_norm[1:2], "odd_out")
    w_up[1] = gathered(4, "up1", x2, x3)[0].reshape(N_CHIPS, 1, D_MODEL, 2 * D_FF // N_CHIPS)
    act1, ffn1 = ffn_fwd(1, x3, xn3)
    w_down[1] = gathered(5, "down1", act1, act1)[0].reshape(1, 1, D_FF, D_MODEL)
    dy, dyb, sq = mm_res_loss(act1, w_down[1], x3, target, "ffn1_down_loss")
    loss = lax.psum(0.5 * jnp.sum(sq) * (1.0 / D_MODEL), ("x", "y", "c"))

    def ffn_bwd(l, dy, dyb, saved):
        xin, xn, up, u, act = saved
        dw_down = mm_tn(act, dyb, f"ffn{l}_dw_down", J=1, tk=D_FF // 2, tm=1024)
        dact = mm_nt(dyb, w_down[l], f"ffn{l}_dact", tr=D_FF // 2, out_dtype=BF16, tm=1024)
        dup, dcw, dcb = glu_bwd(up, u, dact, ffn_cw_full[l], f"ffn{l}_glu_bwd")
        dw_up = mm_tn(xn, dup, f"ffn{l}_dw_up", J=N_CHIPS, tk=512, tm=1024, jb=2)
        dx, dxb, dg = mm_nt_norm_bwd(dup, w_up[l], xin, ffn_norm[l:l + 1], dy, f"ffn{l}_dx")
        return dx, dxb, (dw_down, dw_up, dcw, dcb, dg)

    def quarters(g):
        return g.reshape(N_CHIPS, 2, g.shape[0] * g.shape[1] // (2 * N_CHIPS), g.shape[-1])

    def reduce_start(grads, tag, then):
        sems, parts, zones = reduce_send([quarters(g) for g in grads], "reduce_send_" + tag)
        then, parts = lax.optimization_barrier((then, parts))
        return (sems, parts, zones), then

    dx3, dx3b, g_ffn1 = ffn_bwd(1, dy, dyb, ffn1)
    red_ffn1, (dx3, dx3b) = reduce_start([g_ffn1[1], g_ffn1[0]], "ffn1", (dx3, dx3b))
    dw_o = mm_tn(att, dx3b, "odd_dw_o", J=1, tk=512, tm=1024)
    datt = mm_nt(dx3b, w_o, "odd_datt", tr=D_MODEL, out_dtype=BF16)
    dq, dk, dv, dbias = attn_bwd(qkvn, att, datt, lse, bias, "attn_bwd")
    dqkv, dgqk = qknorm_bwd(qkv, dq, dk, dv, gqk, "odd_qknorm_bwd")
    dw_qkv = mm_tn(xn2, dqkv, "odd_dw_qkv", J=N_CHIPS, tk=512, tm=1024)
    red_odd, dqkv = reduce_start([dw_qkv, dw_o], "odd", dqkv)
    dx2, dx2b, dg_odd = mm_nt_norm_bwd(dqkv, w_qkv, x2, odd_norm_full, dx3, "odd_dx")
    dx1, dx1b, g_ffn0 = ffn_bwd(0, dx2, dx2b, ffn0)
    red_ffn0, (dx1, dx1b) = reduce_start([g_ffn0[1], g_ffn0[0]], "ffn0", (dx1, dx1b))
    dw_out = mm_tn(mix, dx1b, "even_dw_out", J=1, tk=512, tm=1024)
    dmix = mm_nt(dx1b, w_out, "even_dmix", tr=D_MODEL)
    dproj, dcw_even, dpw, dps = mixer_bwd(proj, dmix, conv_w_full, pool_w, even_pool_scale, "even_mixer_bwd")
    dw_in = mm_tn(xn0, dproj, "even_dw_in", J=N_CHIPS, tk=512, tm=1024)
    grad_x, _, dg_even = mm_nt_norm_bwd(dproj, w_in, xs, even_norm, dx1, "even_dx")
    d_rel = jnp.sum(bias_reduce(dbias.reshape(3, N_HEADS, 2 * ATT_BLOCK * ATT_BLOCK), "bias_reduce"), axis=0).T

    red_even, grad_x = reduce_start([dw_in, dw_out], "even", grad_x)

    dcw_sh = dcw_even.reshape(3, N_CHIPS, A_WIDTH // N_CHIPS).transpose(1, 0, 2)
    don_sh = dg_odd.reshape(N_CHIPS, D_MODEL // N_CHIPS)
    dfcw = jnp.stack([g_ffn0[2], g_ffn1[2]])
    dfcw_sh = dfcw.reshape(2, 3, N_CHIPS, 2 * D_FF // N_CHIPS).transpose(2, 0, 1, 3)
    rep_grads = [d_rel, dg_even, dpw[None], dps, _head_sum(dgqk[0]), _head_sum(dgqk[1]),
                 jnp.concatenate([g_ffn0[4], g_ffn1[4]], axis=0), jnp.concatenate([g_ffn0[3], g_ffn1[3]], axis=0)]
    rep_rows = _pack(rep_grads)
    shard_rows = jnp.concatenate([_pack([dcw_sh[j], don_sh[j], dfcw_sh[j]]) for j in range(N_CHIPS)], axis=0)
    n_rep, n_shard = rep_rows.shape[0], shard_rows.shape[0] // N_CHIPS
    small_sems, small_rows, small_land = devices_start(jnp.concatenate([rep_rows, shard_rows], axis=0), "small_grads_start")
    grad_x, small_rows = lax.optimization_barrier((grad_x, small_rows))

    def reduce_end(red, tag, after):
        sems, parts, zones = red
        parts, zones = reduce_wait(parts, zones, sems, after, "reduce_wait_" + tag)
        return zones, parts

    z_ffn1, p_ffn1 = reduce_end(red_ffn1, "ffn1", grad_x)
    z_odd, p_odd = reduce_end(red_odd, "odd", grad_x)
    r_qkv = reduce_sum(z_odd[0], p_odd[0], place, "reduce_sum_w_qkv")
    r_o = reduce_sum(z_odd[1], p_odd[1], place, "reduce_sum_w_o")
    r_up = reduce_sum(z_ffn1[0], p_ffn1[0], place, "reduce_sum_w_up1", layer=1)
    r_down = reduce_sum(z_ffn1[1], p_ffn1[1], place, "reduce_sum_w_down1", layer=1)
    r_qkv, r_o, r_up, r_down = lax.optimization_barrier((r_qkv, r_o, r_up, r_down))
    z_ffn0, p_ffn0 = reduce_end(red_ffn0, "ffn0", r_down)
    r_up = reduce_sum(z_ffn0[0], p_ffn0[0], place, "reduce_sum_w_up0", into=r_up, layer=0)
    r_down = reduce_sum(z_ffn0[1], p_ffn0[1], place, "reduce_sum_w_down0", into=r_down, layer=0)
    later = ["odd_w_qkv", "odd_w_o", "ffn_w_up", "ffn_w_down"]
    joined = join_halves([r_qkv, r_o, r_up, r_down], "grads_join_late_layers")
    G = {nm: g.reshape(W[nm].shape) for nm, g in zip(later, joined)}

    D_, NM, NV = {}, {}, {}

    def update(nm):
        as3 = lambda a: a.reshape((-1,) + a.shape[-2:])
        outs = adamw(as3(W[nm]), as3(G[nm]), as3(M1[nm]), as3(M2[nm]), "adamw_" + nm)
        D_[nm], NM[nm], NV[nm], G[nm] = [o.reshape(W[nm].shape) for o in outs]

    def all_before(names):
        tied = lax.optimization_barrier([D_[nm] for nm in names])
        for nm, d in zip(names, tied):
            D_[nm] = d
        return tied[0]

    for nm in later:
        update(nm)
    z_even, p_even = reduce_end(red_even, "even", all_before(later))
    joined = join_halves([reduce_sum(z_even[0], p_even[0], place, "reduce_sum_w_in"),
                          reduce_sum(z_even[1], p_even[1], place, "reduce_sum_w_out")], "grads_join_first_layer")
    first = ["even_w_in", "even_w_out"]
    for nm, g in zip(first, joined):
        G[nm] = g.reshape(W[nm].shape)
        update(nm)
    small_rows, small_land = devices_wait(small_rows, small_land, small_sems, all_before(first), "small_grads_wait")
    small_sum = device_sum(small_land, small_rows, place[2:3], "small_grads_sum")
    mine = lax.dynamic_slice_in_dim(small_sum, n_rep + chip * n_shard, n_shard, axis=0)
    g_small = jnp.concatenate([small_sum[:n_rep], mine], axis=0)
    small_names = [n for n, _ in REPLICATED_SMALL + SHARDED_SMALL]
    small_shapes = [s for _, s in REPLICATED_SMALL + SHARDED_SMALL]
    G.update(dict(zip(small_names, _unpack(g_small, small_shapes))))
    packs = [_pack([d[n] for n in small_names])[None] for d in (W, M1, M2)]
    outs = adamw(packs[0], g_small[None], packs[1], packs[2], "adamw_small")
    for dst, o in zip((D_, NM, NV), outs[:3]):
        dst.update(dict(zip(small_names, _unpack(o[0], small_shapes))))

    return (loss, grad_x[None], *[G[n] for n in WEIGHT_ORDER], *[D_[n] for n in WEIGHT_ORDER],
            *[NM[n] for n in WEIGHT_ORDER], *[NV[n] for n in WEIGHT_ORDER])


def _head_sum(dg):
    return jnp.sum(dg.reshape(N_HEADS, HEAD_DIM), axis=0, keepdims=True)
```

```python
import functools
import math

import numpy as np
import jax
import jax.numpy as jnp
from jax import lax
from jax.experimental import pallas as pl
from jax.experimental.pallas import tpu as pltpu

F32 = jnp.float32
BF16 = jnp.bfloat16

D_MODEL = 1024
N_HEADS = 16
HEAD_DIM = 64
A_WIDTH = 512
POOL_WINDOWS = (2, 4, 8, 16)
POOL_GROUP = 128
EVEN_IN = 2048
D_FF = 2816
DILATED_PAIRS = ((128, 1), (512, 4), (2048, 16))
ATT_BLOCK = 128
N_REL_BUCKETS = 32
REL_MAX_DISTANCE = 2048
EPS = 1e-6
MASK_VALUE = -1e30
ADAM_LR, ADAM_B1, ADAM_B2, ADAM_EPS, ADAM_WD, ADAM_STEP = 0.001, 0.9, 0.999, 1e-08, 0.01, 10

VMEM_LIMIT_BYTES = 48 * 1024 * 1024
ELEMENTWISE_BLOCK_BYTES = 2 * 1024 * 1024
N_CHIPS = 4
N_DEV = 8
MESH = pl.DeviceIdType.MESH


def _params(*sem):
    return pltpu.CompilerParams(dimension_semantics=sem if sem else None, vmem_limit_bytes=VMEM_LIMIT_BYTES)


def _sds(shape, dtype):
    return jax.ShapeDtypeStruct(tuple(shape), dtype)


def cast_bf16(x, name, tr=None):
    lead, (R, C) = x.shape[:-2], x.shape[-2:]
    n = int(np.prod(lead)) if lead else 1
    x3 = x.reshape((n, R, C))
    tr = tr or R

    def body(x_ref, o_ref):
        o_ref[...] = x_ref[...].astype(BF16)

    out = pl.pallas_call(
        body, name=name, grid=(n, R // tr),
        in_specs=[pl.BlockSpec((None, tr, C), lambda i, r: (i, r, 0))],
        out_specs=pl.BlockSpec((None, tr, C), lambda i, r: (i, r, 0)),
        out_shape=_sds((n, R, C), BF16), compiler_params=_params("parallel", "parallel"),
    )(x3)
    return out.reshape(lead + (R, C))


def rmsnorm_fwd(x, g, name, ts=512):
    S, Dm = x.shape

    def body(x_ref, g_ref, o_ref):
        xv = x_ref[...]
        r = lax.rsqrt(jnp.mean(xv * xv, axis=-1, keepdims=True) + EPS)
        o_ref[...] = ((xv * r) * g_ref[...]).astype(BF16)

    return pl.pallas_call(
        body, name=name, grid=(S // ts,),
        in_specs=[pl.BlockSpec((ts, Dm), lambda i: (i, 0)), pl.BlockSpec((1, Dm), lambda i: (0, 0))],
        out_specs=pl.BlockSpec((ts, Dm), lambda i: (i, 0)),
        out_shape=_sds((S, Dm), BF16), compiler_params=_params("parallel"),
    )(x, g)


def mm_res_norm(a, w, res, gain, name, tm=1024):
    M, K = a.shape
    Dm = w.shape[-1]

    def body(a_ref, w_ref, r_ref, g_ref, y_ref, yn_ref):
        y = r_ref[...] + jnp.dot(a_ref[...], w_ref[...], preferred_element_type=F32)
        y_ref[...] = y
        r = lax.rsqrt(jnp.mean(y * y, axis=-1, keepdims=True) + EPS)
        yn_ref[...] = ((y * r) * g_ref[...]).astype(BF16)

    row = pl.BlockSpec((tm, Dm), lambda m: (m, 0))
    return pl.pallas_call(
        body, name=name, grid=(M // tm,),
        in_specs=[pl.BlockSpec((tm, K), lambda m: (m, 0)),
                  pl.BlockSpec((None, None, K, Dm), lambda m: (0, 0, 0, 0), pipeline_mode=pl.Buffered(1)),
                  row, pl.BlockSpec((1, Dm), lambda m: (0, 0))],
        out_specs=[row, row], out_shape=[_sds((M, Dm), F32), _sds((M, Dm), BF16)],
        compiler_params=_params("parallel"),
    )(a, w, res, gain)


def mm_res_loss(a, w, res, target, name, tm=512):
    M, K = a.shape
    Dm = w.shape[-1]

    def body(a_ref, w_ref, r_ref, t_ref, d_ref, db_ref, s_ref):
        e = (r_ref[...] + jnp.dot(a_ref[...], w_ref[...], preferred_element_type=F32)) - t_ref[...]
        d = e * (1.0 / Dm)
        d_ref[...] = d
        db_ref[...] = d.astype(BF16)
        part = jnp.sum(e * e, axis=0, keepdims=True)

        @pl.when(pl.program_id(0) == 0)
        def _():
            s_ref[...] = part

        @pl.when(pl.program_id(0) > 0)
        def _():
            s_ref[...] += part

    row = pl.BlockSpec((tm, Dm), lambda m: (m, 0))
    return pl.pallas_call(
        body, name=name, grid=(M // tm,),
        in_specs=[pl.BlockSpec((tm, K), lambda m: (m, 0)),
                  pl.BlockSpec((None, None, K, Dm), lambda m: (0, 0, 0, 0), pipeline_mode=pl.Buffered(1)), row, row],
        out_specs=[row, row, pl.BlockSpec((1, Dm), lambda m: (0, 0))],
        out_shape=[_sds((M, Dm), F32), _sds((M, Dm), BF16), _sds((1, Dm), F32)],
        compiler_params=_params("arbitrary"),
    )(a, w, res, target)


def mm_nt(dy, w, name, tr, layer=0, out_dtype=F32, tm=512):
    M = dy.shape[0]
    J, _, R, Ns = w.shape
    dims = (((1,), (1,)), ((), ()))

    def body(dy_ref, w_ref, o_ref):
        acc = None
        for j in range(J):
            p = lax.dot_general(dy_ref[:, j * Ns:(j + 1) * Ns], w_ref[j], dims, preferred_element_type=F32)
            acc = p if acc is None else acc + p
        o_ref[...] = acc.astype(o_ref.dtype)

    return pl.pallas_call(
        body, name=name, grid=(R // tr, M // tm),
        in_specs=[pl.BlockSpec((tm, J * Ns), lambda r, m: (m, 0)),
                  pl.BlockSpec((J, None, tr, Ns), lambda r, m: (0, layer, r, 0))],
        out_specs=pl.BlockSpec((tm, tr), lambda r, m: (m, r)),
        out_shape=_sds((M, R), out_dtype),
        compiler_params=_params("parallel", "parallel"),
    )(dy, w)


def mm_nt_norm_bwd(dy, w, x, g, dres, name, layer=0, tm=512):
    M = dy.shape[0]
    J, _, Dm, Ns = w.shape
    dims = (((1,), (1,)), ((), ()))

    def body(dy_ref, w_ref, x_ref, g_ref, r_ref, dx_ref, dxb_ref, dg_ref):
        dxn = None
        for j in range(J):
            p = lax.dot_general(dy_ref[:, j * Ns:(j + 1) * Ns], w_ref[j], dims, preferred_element_type=F32)
            dxn = p if dxn is None else dxn + p
        xv = x_ref[...]
        r = lax.rsqrt(jnp.mean(xv * xv, axis=-1, keepdims=True) + EPS)
        gx = dxn * g_ref[...]
        dot = jnp.sum(gx * xv, axis=-1, keepdims=True)
        dx = r_ref[...] + r * gx - xv * ((r * r * r) * (dot * (1.0 / Dm)))
        dx_ref[...] = dx
        dxb_ref[...] = dx.astype(BF16)
        part = jnp.sum(dxn * (xv * r), axis=0, keepdims=True)

        @pl.when(pl.program_id(0) == 0)
        def _():
            dg_ref[...] = part

        @pl.when(pl.program_id(0) > 0)
        def _():
            dg_ref[...] += part

    row = pl.BlockSpec((tm, Dm), lambda m: (m, 0))
    vec = pl.BlockSpec((1, Dm), lambda m: (0, 0))
    return pl.pallas_call(
        body, name=name, grid=(M // tm,),
        in_specs=[pl.BlockSpec((tm, J * Ns), lambda m: (m, 0)),
                  pl.BlockSpec((J, None, Dm, Ns), lambda m: (0, layer, 0, 0), pipeline_mode=pl.Buffered(1)), row, vec, row],
        out_specs=[row, row, vec],
        out_shape=[_sds((M, Dm), F32), _sds((M, Dm), BF16), _sds((1, Dm), F32)],
        compiler_params=_params("arbitrary"),
    )(dy, w, x, g, dres)


def mm_tn(a, dy, name, J, tk, tm=512, jb=None):
    M, K = a.shape
    jb = jb or J
    Ns = dy.shape[1] // J
    N = jb * Ns
    n_m = M // tm
    dims = (((0,), (0,)), ((), ()))

    def body(a_ref, dy_ref, o_ref, acc_ref):
        p = lax.dot_general(a_ref[...], dy_ref[...], dims, preferred_element_type=F32)
        m = pl.program_id(2)

        @pl.when(m == 0)
        def _():
            acc_ref[...] = p

        @pl.when(m > 0)
        def _():
            acc_ref[...] += p

        @pl.when(m == n_m - 1)
        def _():
            for j in range(jb):
                o_ref[j] = acc_ref[:, j * Ns:(j + 1) * Ns].astype(BF16)

    return pl.pallas_call(
        body, name=name, grid=(J // jb, K // tk, n_m),
        in_specs=[pl.BlockSpec((tm, tk), lambda g, k, m: (m, k)), pl.BlockSpec((tm, N), lambda g, k, m: (m, g))],
        out_specs=pl.BlockSpec((jb, tk, Ns), lambda g, k, m: (g, k, 0)),
        out_shape=_sds((J, K, Ns), BF16), scratch_shapes=[pltpu.VMEM((tk, N), F32)],
        compiler_params=_params("parallel", "parallel", "arbitrary"),
    )(a, dy)


HALO = 16


def _shift_down(x, s):
    return pltpu.roll(x, s, 0)


def _shift_up(x, s):
    return pltpu.roll(x, x.shape[0] - s, 0)


def _conv3(z, cw):
    return (_shift_down(z, 2) * cw[0:1] + _shift_down(z, 1) * cw[1:2]) + z * cw[2:3]


def _window_count(first_row, n, k):
    t = first_row + lax.broadcasted_iota(jnp.int32, (n, 1), 0)
    return jnp.clip(t + 1, 1, k).astype(F32)


def in_mixer_fwd(xn, w_in, conv_w, pool_w, pool_scale, name, ts=512):
    S, K = xn.shape
    n = ts + HALO

    def body(xm_ref, xb_ref, w_ref, cw_ref, pw_ref, ps_ref, p_ref, o_ref):
        i = pl.program_id(0)
        before = jnp.where(i > 0, xb_ref[...], jnp.zeros_like(xb_ref))
        rows = jnp.concatenate([before, xm_ref[...]], axis=0)
        h, gb, gc, pin = [jnp.dot(rows, w_ref[j], preferred_element_type=F32) for j in range(N_CHIPS)]
        for j, part in enumerate((h, gb, gc, pin)):
            p_ref[:, j * A_WIDTH:(j + 1) * A_WIDTH] = part[HALO:]
        cz = _conv3(gc * h, cw_ref[...])
        o_ref[:, 0:A_WIDTH] = (gb[HALO:] * cz[HALO:]).astype(BF16)
        for g, k in enumerate(POOL_WINDOWS):
            p = pin[:, g * POOL_GROUP:(g + 1) * POOL_GROUP]
            w = p
            s = 1
            while s < k:
                w = w + _shift_down(w, s)
                s *= 2
            pooled = w / _window_count(i * ts - HALO, n, k) - p
            yb = jnp.dot(pooled[HALO:].astype(BF16), pw_ref[g], preferred_element_type=F32)
            yb = yb * ps_ref[:, g * POOL_GROUP:(g + 1) * POOL_GROUP]
            o_ref[:, A_WIDTH + g * POOL_GROUP:A_WIDTH + (g + 1) * POOL_GROUP] = yb.astype(BF16)

    hb = ts // HALO
    return pl.pallas_call(
        body, name=name, grid=(S // ts,),
        in_specs=[
            pl.BlockSpec((ts, K), lambda i: (i, 0)),
            pl.BlockSpec((HALO, K), lambda i: (jnp.maximum(i * hb - 1, 0), 0)),
            pl.BlockSpec((N_CHIPS, None, K, A_WIDTH), lambda i: (0, 0, 0, 0), pipeline_mode=pl.Buffered(1)),
            pl.BlockSpec((3, A_WIDTH), lambda i: (0, 0)),
            pl.BlockSpec((4, POOL_GROUP, POOL_GROUP), lambda i: (0, 0, 0)),
            pl.BlockSpec((1, 4 * POOL_GROUP), lambda i: (0, 0)),
        ],
        out_specs=[pl.BlockSpec((ts, EVEN_IN), lambda i: (i, 0)), pl.BlockSpec((ts, D_MODEL), lambda i: (i, 0))],
        out_shape=[_sds((S, EVEN_IN), F32), _sds((S, D_MODEL), BF16)], compiler_params=_params("parallel"),
    )(xn, xn, w_in, conv_w, pool_w, pool_scale)


def mixer_bwd(proj, dmix, conv_w, pool_w, pool_scale, name, ts=256):
    S = proj.shape[0]
    n = ts + 2 * HALO
    nt = S // ts
    tn_dims = (((0,), (0,)), ((), ()))
    nt_dims = (((1,), (1,)), ((), ()))

    def body(pm_ref, pb_ref, pa_ref, dm_ref, da_ref, cw_ref, pw_ref, ps_ref, o_ref, dcw_ref, dpw_ref, dps_ref):
        i = pl.program_id(0)
        last = i == nt - 1
        before = jnp.where(i > 0, pb_ref[...], 0.0)
        after = jnp.where(last, 0.0, pa_ref[...])
        ext = jnp.concatenate([before, pm_ref[...], after], axis=0)
        dafter = jnp.where(last, 0.0, da_ref[...])
        dext = jnp.concatenate([jnp.zeros((HALO, D_MODEL), F32), dm_ref[...], dafter], axis=0)
        cw = cw_ref[...]
        main = slice(HALO, HALO + ts)

        @pl.when(i == 0)
        def _():
            dcw_ref[...] = jnp.zeros_like(dcw_ref)
            dpw_ref[...] = jnp.zeros_like(dpw_ref)
            dps_ref[...] = jnp.zeros_like(dps_ref)

        h, gb, gc = ext[:, 0:A_WIDTH], ext[:, A_WIDTH:2 * A_WIDTH], ext[:, 2 * A_WIDTH:3 * A_WIDTH]
        z = gc * h
        z1, z2 = _shift_down(z, 1), _shift_down(z, 2)
        cz = (z2 * cw[0:1] + z1 * cw[1:2]) + z * cw[2:3]
        dya = dext[:, 0:A_WIDTH]
        dcz = dya * gb
        dz = dcz * cw[2:3] + _shift_up(dcz, 1) * cw[1:2] + _shift_up(dcz, 2) * cw[0:1]
        o_ref[:, 0:A_WIDTH] = (dz * gc)[main].astype(BF16)
        o_ref[:, A_WIDTH:2 * A_WIDTH] = (dya * cz)[main].astype(BF16)
        o_ref[:, 2 * A_WIDTH:3 * A_WIDTH] = (dz * h)[main].astype(BF16)
        dczm = dcz[main]
        dcw_ref[0:1, :] += jnp.sum(dczm * z2[main], axis=0, keepdims=True)
        dcw_ref[1:2, :] += jnp.sum(dczm * z1[main], axis=0, keepdims=True)
        dcw_ref[2:3, :] += jnp.sum(dczm * z[main], axis=0, keepdims=True)

        for g, k in enumerate(POOL_WINDOWS):
            lo = 3 * A_WIDTH + g * POOL_GROUP
            cols = slice(g * POOL_GROUP, (g + 1) * POOL_GROUP)
            p = ext[:, lo:lo + POOL_GROUP]
            w = p
            s = 1
            while s < k:
                w = w + _shift_down(w, s)
                s *= 2
            cnt = _window_count(i * ts - HALO, n, k)
            pooled = (w / cnt - p)[main].astype(BF16)
            dyb = dext[:, A_WIDTH + g * POOL_GROUP:A_WIDTH + (g + 1) * POOL_GROUP]
            e = dyb * ps_ref[:, cols]
            pre = jnp.dot(pooled, pw_ref[g], preferred_element_type=F32)
            dps_ref[:, cols] += jnp.sum(dyb[main] * pre, axis=0, keepdims=True)
            dpw_ref[g] += lax.dot_general(pooled, e[main].astype(BF16), tn_dims, preferred_element_type=F32)
            dpooled = lax.dot_general(e.astype(BF16), pw_ref[g], nt_dims, preferred_element_type=F32)
            q = dpooled / cnt
            a = q
            s = 1
            while s < k:
                a = a + _shift_up(a, s)
                s *= 2
            o_ref[:, lo:lo + POOL_GROUP] = (a - dpooled)[main].astype(BF16)

    hb = ts // HALO
    nh = S // HALO
    before_map = lambda i: (jnp.maximum(i * hb - 1, 0), 0)
    after_map = lambda i: (jnp.minimum((i + 1) * hb, nh - 1), 0)
    full = lambda *shape: pl.BlockSpec(shape, lambda i: (0,) * len(shape))
    return pl.pallas_call(
        body, name=name, grid=(nt,),
        in_specs=[
            pl.BlockSpec((ts, EVEN_IN), lambda i: (i, 0)),
            pl.BlockSpec((HALO, EVEN_IN), before_map),
            pl.BlockSpec((HALO, EVEN_IN), after_map),
            pl.BlockSpec((ts, D_MODEL), lambda i: (i, 0)),
            pl.BlockSpec((HALO, D_MODEL), after_map),
            full(3, A_WIDTH), full(4, POOL_GROUP, POOL_GROUP), full(1, 4 * POOL_GROUP),
        ],
        out_specs=[pl.BlockSpec((ts, EVEN_IN), lambda i: (i, 0)), full(3, A_WIDTH), full(4, POOL_GROUP, POOL_GROUP),
                   full(1, 4 * POOL_GROUP)],
        out_shape=[_sds((S, EVEN_IN), BF16), _sds((3, A_WIDTH), F32), _sds((4, POOL_GROUP, POOL_GROUP), F32),
                   _sds((1, 4 * POOL_GROUP), F32)],
        compiler_params=_params("arbitrary"),
    )(proj, proj, proj, dmix, dmix, conv_w, pool_w, pool_scale)


FFN_HALO = 16
FFN_TC = 1408


GLU_CHUNKS = ((0, 512), (512, 512), (1024, 384))


def up_glu_fwd(xn, w_up, conv_w, conv_b, name, tm=512):
    S, K = xn.shape
    nc = D_FF // FFN_TC

    def body(xm_ref, xb_ref, wg_ref, wu_ref, cwg_ref, cwu_ref, cbg_ref, cbu_ref, pg_ref, pu_ref, ug_ref, uu_ref, o_ref):
        before = jnp.where(pl.program_id(1) > 0, xb_ref[...], jnp.zeros_like(xb_ref))
        rows = jnp.concatenate([before, xm_ref[...]], axis=0)
        for lo, width in GLU_CHUNKS:
            cols = slice(lo, lo + width)
            pre_g = jnp.dot(rows, wg_ref[:, cols], preferred_element_type=F32)
            pre_u = jnp.dot(rows, wu_ref[:, cols], preferred_element_type=F32)
            gate = _conv3(pre_g, cwg_ref[:, cols])[FFN_HALO:] + cbg_ref[:, cols]
            upv = _conv3(pre_u, cwu_ref[:, cols])[FFN_HALO:] + cbu_ref[:, cols]
            pg_ref[:, cols] = pre_g[FFN_HALO:].astype(BF16)
            pu_ref[:, cols] = pre_u[FFN_HALO:].astype(BF16)
            ug_ref[:, cols] = gate.astype(BF16)
            uu_ref[:, cols] = upv.astype(BF16)
            o_ref[:, cols] = ((gate * (1.0 / (1.0 + jnp.exp(-gate)))) * upv).astype(BF16)

    hb = tm // FFN_HALO
    wspec = lambda off: pl.BlockSpec((None, None, K, FFN_TC), lambda j, m: (j + off, 0, 0, 0))
    cw = lambda off: pl.BlockSpec((3, FFN_TC), lambda j, m: (0, j + off))
    cb = lambda off: pl.BlockSpec((1, FFN_TC), lambda j, m: (0, j + off))
    out = pl.BlockSpec((tm, FFN_TC), lambda j, m: (m, j))
    pg, pu, ug, uu, act = pl.pallas_call(
        body, name=name, grid=(nc, S // tm),
        in_specs=[pl.BlockSpec((tm, K), lambda j, m: (m, 0)),
                  pl.BlockSpec((FFN_HALO, K), lambda j, m: (jnp.maximum(m * hb - 1, 0), 0)),
                  wspec(0), wspec(nc), cw(0), cw(nc), cb(0), cb(nc)],
        out_specs=[out] * 5, out_shape=[_sds((S, D_FF), BF16)] * 5,
        compiler_params=_params("parallel", "parallel"),
    )(xn, xn, w_up, w_up, conv_w, conv_w, conv_b, conv_b)
    return (pg, pu), (ug, uu), act


def glu_bwd(up, u, da, conv_w, name, ts=256):
    S = up[0].shape[0]
    nc = D_FF // FFN_TC
    nt = S // ts
    W = 2 * D_FF

    def body(xg_ref, xu_ref, gm_ref, ga_ref, um_ref, ua_ref, dm_ref, da_ref, cw_ref, dx_ref, dcw_ref, dcb_ref):
        i = pl.program_id(0)
        last = i == nt - 1

        @pl.when(i == 0)
        def _():
            dcw_ref[...] = jnp.zeros_like(dcw_ref)
            dcb_ref[...] = jnp.zeros_like(dcb_ref)

        def rows(m_ref, a_ref, cols):
            return jnp.concatenate([m_ref[:, cols], a_ref[:, cols]], axis=0).astype(F32)

        def back(d, x, cols):
            cw = cw_ref[:, cols]
            d1, d2 = _shift_up(d, 1), _shift_up(d, 2)
            dx_ref[:, cols] = ((d * cw[2:3] + d1 * cw[1:2]) + d2 * cw[0:1])[:ts].astype(BF16)
            dcb_ref[:, cols] += jnp.sum(d[:ts], axis=0, keepdims=True)
            dcw_ref[0:1, cols] += jnp.sum(d2[:ts] * x, axis=0, keepdims=True)
            dcw_ref[1:2, cols] += jnp.sum(d1[:ts] * x, axis=0, keepdims=True)
            dcw_ref[2:3, cols] += jnp.sum(d[:ts] * x, axis=0, keepdims=True)

        for c in range(nc):
            cols = slice(c * FFN_TC, (c + 1) * FFN_TC)
            ug, uu = rows(gm_ref, ga_ref, cols), rows(um_ref, ua_ref, cols)
            dae = rows(dm_ref, da_ref, cols)
            dae = jnp.where(last & (lax.broadcasted_iota(jnp.int32, dae.shape, 0) >= ts), 0.0, dae)
            sg = 1.0 / (1.0 + jnp.exp(-ug))
            duu = dae * (ug * sg)
            dug = (dae * uu) * (sg * (1.0 + ug * (1.0 - sg)))
            back(dug, xg_ref[:, cols].astype(F32), cols)
            back(duu, xu_ref[:, cols].astype(F32), slice(D_FF + c * FFN_TC, D_FF + (c + 1) * FFN_TC))

    hb = ts // FFN_HALO
    nh = S // FFN_HALO
    after_map = lambda i: (jnp.minimum((i + 1) * hb, nh - 1), 0)
    main = pl.BlockSpec((ts, D_FF), lambda i: (i, 0))
    after = pl.BlockSpec((FFN_HALO, D_FF), after_map)
    return pl.pallas_call(
        body, name=name, grid=(nt,),
        in_specs=[main, main, main, after, main, after, main, after, pl.BlockSpec((3, W), lambda i: (0, 0))],
        out_specs=[pl.BlockSpec((ts, W), lambda i: (i, 0)), pl.BlockSpec((3, W), lambda i: (0, 0)),
                   pl.BlockSpec((1, W), lambda i: (0, 0))],
        out_shape=[_sds((S, W), BF16), _sds((3, W), F32), _sds((1, W), F32)],
        compiler_params=_params("arbitrary"),
    )(up[0], up[1], u[0], u[0], u[1], u[1], da, da, conv_w)


MEAN_GROUP = 256


def _head_mean_matrix():
    h = np.arange(MEAN_GROUP) // HEAD_DIM
    return jnp.asarray((h[:, None] == h[None, :]).astype(np.float32) / HEAD_DIM, dtype=BF16)


def _head_mean(v, gm):
    vb = v.astype(BF16)
    return jnp.concatenate([jnp.dot(vb[:, c:c + MEAN_GROUP], gm, preferred_element_type=F32)
                            for c in range(0, v.shape[1], MEAN_GROUP)], axis=1)


def qkv_qknorm_fwd(xn, w_qkv, gqk, name, tm=1024):
    S, K = xn.shape
    J, _, _, Ns = w_qkv.shape
    gains = gqk.reshape(1, 3 * D_MODEL)

    def body(x_ref, w_ref, g_ref, gm_ref, raw_ref, o_ref):
        first_col = pl.program_id(0) * Ns
        acc = jnp.dot(x_ref[...], w_ref[...], preferred_element_type=F32)
        raw_ref[...] = acc
        gm = gm_ref[...]
        for c in range(0, Ns, MEAN_GROUP):
            cols = slice(c, c + MEAN_GROUP)
            x = acc[:, cols]
            mean = jnp.dot((x * x).astype(BF16), gm, preferred_element_type=F32)
            normed = (x * lax.rsqrt(mean + EPS)) * g_ref[:, cols]
            o_ref[:, cols] = jnp.where(first_col + c >= 2 * D_MODEL, x, normed).astype(BF16)

    return pl.pallas_call(
        body, name=name, grid=(J, S // tm),
        in_specs=[pl.BlockSpec((tm, K), lambda j, m: (m, 0)), pl.BlockSpec((None, None, K, Ns), lambda j, m: (j, 0, 0, 0)),
                  pl.BlockSpec((1, Ns), lambda j, m: (0, j)), pl.BlockSpec((MEAN_GROUP, MEAN_GROUP), lambda j, m: (0, 0))],
        out_specs=[pl.BlockSpec((tm, Ns), lambda j, m: (m, j))] * 2,
        out_shape=[_sds((S, J * Ns), F32), _sds((S, J * Ns), BF16)], compiler_params=_params("parallel", "parallel"),
    )(xn, w_qkv, gains, _head_mean_matrix())


def qknorm_bwd(qkv, dq, dk, dv, gqk, name, ts=256):
    S = qkv.shape[0]

    def body(x_ref, dq_ref, dk_ref, dv_ref, g_ref, gm_ref, o_ref, dg_ref):
        @pl.when(pl.program_id(0) == 0)
        def _():
            dg_ref[...] = jnp.zeros_like(dg_ref)

        gm = gm_ref[...]
        for part, d_ref in enumerate((dq_ref, dk_ref)):
            cols = slice(part * D_MODEL, (part + 1) * D_MODEL)
            x = x_ref[:, cols]
            d = d_ref[...]
            r = lax.rsqrt(_head_mean(x * x, gm) + EPS)
            gx = d * g_ref[part]
            o_ref[:, cols] = (r * gx - x * ((r * r * r) * _head_mean(gx * x, gm))).astype(BF16)
            dg_ref[part] += jnp.sum(d * (x * r), axis=0, keepdims=True)
        o_ref[:, 2 * D_MODEL:] = dv_ref[...].astype(BF16)

    row = pl.BlockSpec((ts, D_MODEL), lambda i: (i, 0))
    wide = pl.BlockSpec((ts, 3 * D_MODEL), lambda i: (i, 0))
    gains = pl.BlockSpec((3, 1, D_MODEL), lambda i: (0, 0, 0))
    return pl.pallas_call(
        body, name=name, grid=(S // ts,),
        in_specs=[wide, row, row, row, gains, pl.BlockSpec((MEAN_GROUP, MEAN_GROUP), lambda i: (0, 0))],
        out_specs=[wide, gains],
        out_shape=[_sds((S, 3 * D_MODEL), BF16), _sds((3, 1, D_MODEL), F32)],
        compiler_params=_params("arbitrary"),
    )(qkv, dq, dk, dv, gqk, _head_mean_matrix())


RESIDUES = 16


def _block_order(dil):
    runs = RESIDUES // dil
    slot = np.arange(ATT_BLOCK)
    return (slot % (ATT_BLOCK // runs)) * runs + slot // (ATT_BLOCK // runs)


def _bucket_tables():
    n = ATT_BLOCK
    max_exact = N_REL_BUCKETS // 2
    buckets, valids = [], []
    for _, dil in DILATED_PAIRS:
        order = _block_order(dil)
        a = order[:, None]
        c = np.concatenate([order, n + order])[None, :]
        first_half = (np.arange(2 * n) < n)[None, :]
        rel = a + n - c
        band = (rel >= 0) & (rel <= n)
        dist = np.clip(rel, 0, n) * dil
        dd = np.maximum(dist, 1).astype(np.float32)
        large = max_exact + (np.log(dd / np.float32(max_exact)) / np.float32(math.log(REL_MAX_DISTANCE / max_exact))
                             * np.float32(N_REL_BUCKETS - max_exact)).astype(np.int32)
        large = np.minimum(large, N_REL_BUCKETS - 1)
        buckets.append(np.where(dist < max_exact, dist, large).reshape(1, -1))
        valids.append(np.stack([(band & ~first_half).reshape(1, -1), band.reshape(1, -1)]))
    return np.stack(buckets).astype(np.int32), np.stack(valids).astype(np.int32)


BIAS_CHUNK = 8192


def _split3(x):
    a = x.astype(BF16)
    r = x - a.astype(F32)
    b = r.astype(BF16)
    c = (r - b.astype(F32)).astype(BF16)
    return a, b, c


def bias_expand(rel_bias_t, name):
    bucket, valid = _bucket_tables()
    nq = bucket.shape[-1]

    def body(t_ref, b_ref, v_ref, o_ref):
        onehot = (lax.broadcasted_iota(jnp.int32, (N_REL_BUCKETS, BIAS_CHUNK), 0) == b_ref[...]).astype(BF16)
        acc = None
        for term in _split3(t_ref[...]):
            p = jnp.dot(term, onehot, preferred_element_type=F32)
            acc = p if acc is None else acc + p
        o_ref[...] = jnp.where(v_ref[...] > 0, acc, MASK_VALUE)

    return pl.pallas_call(
        body, name=name, grid=(3, 2, nq // BIAS_CHUNK),
        in_specs=[pl.BlockSpec((N_HEADS, N_REL_BUCKETS), lambda b, v, c: (0, 0)),
                  pl.BlockSpec((None, 1, BIAS_CHUNK), lambda b, v, c: (b, 0, c)),
                  pl.BlockSpec((None, None, 1, BIAS_CHUNK), lambda b, v, c: (b, v, 0, c))],
        out_specs=pl.BlockSpec((None, None, N_HEADS, BIAS_CHUNK), lambda b, v, c: (b, v, 0, c)),
        out_shape=_sds((3, 2, N_HEADS, nq), F32), compiler_params=_params("parallel", "parallel", "parallel"),
    )(rel_bias_t, jnp.asarray(bucket), jnp.asarray(valid))


def bias_reduce(dbias, name):
    bucket, _ = _bucket_tables()
    nq = bucket.shape[-1]
    dims = (((1,), (1,)), ((), ()))

    def body(d_ref, b_ref, o_ref):
        onehot = (lax.broadcasted_iota(jnp.int32, (N_REL_BUCKETS, BIAS_CHUNK), 0) == b_ref[...]).astype(BF16)
        acc = None
        for term in _split3(d_ref[...]):
            p = lax.dot_general(term, onehot, dims, preferred_element_type=F32)
            acc = p if acc is None else acc + p

        @pl.when(pl.program_id(1) == 0)
        def _():
            o_ref[...] = acc

        @pl.when(pl.program_id(1) > 0)
        def _():
            o_ref[...] += acc

    return pl.pallas_call(
        body, name=name, grid=(3, nq // BIAS_CHUNK),
        in_specs=[pl.BlockSpec((None, N_HEADS, BIAS_CHUNK), lambda b, c: (b, 0, c)),
                  pl.BlockSpec((None, 1, BIAS_CHUNK), lambda b, c: (b, 0, c))],
        out_specs=pl.BlockSpec((None, N_HEADS, N_REL_BUCKETS), lambda b, c: (b, 0, 0)),
        out_shape=_sds((3, N_HEADS, N_REL_BUCKETS), F32), compiler_params=_params("parallel", "arbitrary"),
    )(dbias, jnp.asarray(bucket))


PAIR = 2 * HEAD_DIM
N_PAIRS = N_HEADS // 2
_NT = (((1,), (1,)), ((), ()))
_TN = (((0,), (0,)), ((), ()))


def _low_lanes(shape):
    return lax.broadcasted_iota(jnp.int32, shape, 1) < HEAD_DIM


ATTN_VMEM_LIMIT_BYTES = 56 * 1024 * 1024
BRANCH_ORDER = (2, 1, 0)


def _regroup(dst, src, L16):
    for r in range(RESIDUES):
        dst[pl.ds(r * L16, L16), :] = src[pl.ds(r, L16, stride=RESIDUES), :]


def _ungroup(dst, src, L16):
    for r in range(RESIDUES):
        dst[pl.ds(r, L16, stride=RESIDUES), :] = src[pl.ds(r * L16, L16), :]


def _branch_geometry(branch, S):
    dil = DILATED_PAIRS[branch][1]
    runs = RESIDUES // dil
    return dil, runs, ATT_BLOCK // runs, S // dil // ATT_BLOCK


def _block_rows(it, branch, S):
    dil, runs, run_len, n_blocks = _branch_geometry(branch, S)
    L16 = S // RESIDUES
    r, b = it // n_blocks, it % n_blocks
    prev = jnp.maximum(b - 1, 0)
    cur_rows = [pl.multiple_of((j * dil + r) * L16 + run_len * b, 8) for j in range(runs)]
    prev_rows = [pl.multiple_of((j * dil + r) * L16 + run_len * prev, 8) for j in range(runs)]
    return cur_rows, prev_rows, jnp.minimum(b, 1)


def _load_block(ref, rows, run_len):
    parts = [ref[pl.ds(o, run_len), :] for o in rows]
    return parts[0] if len(parts) == 1 else jnp.concatenate(parts, axis=0)


def _store_block(ref, rows, run_len, value, add=False):
    for j, o in enumerate(rows):
        part = value[j * run_len:(j + 1) * run_len]
        if add:
            ref[pl.ds(o, run_len), :] += part
        else:
            ref[pl.ds(o, run_len), :] = part


ATTN_FWD_UNROLL = 8
ATTN_BWD_UNROLL = 4


def _stack_heads(x, low):
    zero = jnp.zeros_like(x)
    return jnp.concatenate([jnp.where(low, x, zero), jnp.where(low, zero, x)], axis=0)


def _unstack_heads(y, low):
    return jnp.where(low, y[:ATT_BLOCK], y[ATT_BLOCK:])


def attn_fwd(qkvn, bias, name):
    S = qkvn.shape[0]
    L16 = S // RESIDUES
    n_iter = S // ATT_BLOCK

    def body(q_ref, k_ref, v_ref, b_ref, o_ref, lse_ref, stage, qp, kp, vp, acc_s, m_s, l_s):
        for src, dst in ((q_ref, qp), (k_ref, kp), (v_ref, vp)):
            stage[...] = src[...].astype(F32)
            _regroup(dst, stage, L16)
        low = _low_lanes((ATT_BLOCK, PAIR))

        for branch in BRANCH_ORDER:
            _, _, run_len, _ = _branch_geometry(branch, S)
            first = branch == BRANCH_ORDER[0]

            def step(it, carry, branch=branch, run_len=run_len, first=first):
                cur, prev, variant = _block_rows(it, branch, S)
                q = _load_block(qp, cur, run_len).astype(BF16)
                k = jnp.concatenate([_load_block(kp, prev, run_len), _load_block(kp, cur, run_len)], axis=0).astype(BF16)
                v = jnp.concatenate([_load_block(vp, prev, run_len), _load_block(vp, cur, run_len)], axis=0).astype(BF16)
                s = lax.dot_general(_stack_heads(q, low), k, _NT, preferred_element_type=F32) * (HEAD_DIM ** -0.5)
                s = s + b_ref[2 * branch + variant].reshape(2 * ATT_BLOCK, 2 * ATT_BLOCK)
                mx = jnp.max(s, axis=-1, keepdims=True)
                p = jnp.exp(s - mx)
                den = jnp.sum(p, axis=-1, keepdims=True)
                pv = jnp.dot(p.astype(BF16), v, preferred_element_type=F32)
                acc = _unstack_heads(pv, low)
                m = _unstack_heads(mx, low)
                l = _unstack_heads(den, low)
                if not first:
                    m_old = _load_block(m_s, cur, run_len)
                    m_new = jnp.maximum(m_old, m)
                    a_old, a_new = jnp.exp(m_old - m_new), jnp.exp(m - m_new)
                    acc = _load_block(acc_s, cur, run_len) * a_old + acc * a_new
                    l = _load_block(l_s, cur, run_len) * a_old + l * a_new
                    m = m_new
                _store_block(acc_s, cur, run_len, acc)
                _store_block(m_s, cur, run_len, m)
                _store_block(l_s, cur, run_len, l)
                return carry

            lax.fori_loop(0, n_iter, step, 0, unroll=ATTN_FWD_UNROLL)

        acc_s[...] = acc_s[...] / l_s[...]
        _ungroup(stage, acc_s, L16)
        o_ref[...] = stage[...].astype(BF16)
        m_s[...] = m_s[...] + jnp.log(l_s[...])
        _ungroup(lse_ref, m_s, L16)

    col = lambda part: pl.BlockSpec((S, PAIR), lambda hp: (0, part * N_PAIRS + hp))
    out = pl.BlockSpec((S, PAIR), lambda hp: (0, hp))
    return pl.pallas_call(
        body, name=name, grid=(N_PAIRS,),
        in_specs=[col(0), col(1), col(2), pl.BlockSpec((6, 2, ATT_BLOCK, 2 * ATT_BLOCK), lambda hp: (0, hp, 0, 0))],
        out_specs=[out, out], out_shape=[_sds((S, D_MODEL), BF16), _sds((S, D_MODEL), F32)],
        scratch_shapes=[pltpu.VMEM((S, PAIR), F32)] * 7,
        compiler_params=pltpu.CompilerParams(dimension_semantics=("parallel",), vmem_limit_bytes=ATTN_VMEM_LIMIT_BYTES),
    )(qkvn, qkvn, qkvn, bias)


def attn_bwd(qkvn, att, datt, lse, bias, name):
    S = qkvn.shape[0]
    L16 = S // RESIDUES
    n_iter = S // ATT_BLOCK
    TILE = 512

    def body(q_ref, k_ref, v_ref, o_ref, do_ref, lse_ref, b_ref, dq_ref, dk_ref, dv_ref, db_ref,
             qp, kp, vp, dop, ldp, dqp, dkp, dvp):
        stage = dqp
        for src, dst in ((q_ref, qp), (k_ref, kp), (v_ref, vp), (do_ref, dop)):
            stage[...] = src[...].astype(F32)
            _regroup(dst, stage, L16)

        def pack(i, carry):
            rows = pl.ds(pl.multiple_of(i * TILE, TILE), TILE)
            low = _low_lanes((TILE, PAIR))
            lane = lax.broadcasted_iota(jnp.int32, (TILE, PAIR), 1)
            prod = do_ref[rows, :].astype(F32) * o_ref[rows, :].astype(F32)
            d0 = jnp.sum(jnp.where(low, prod, 0.0), axis=-1, keepdims=True)
            d1 = jnp.sum(jnp.where(low, 0.0, prod), axis=-1, keepdims=True)
            stage[rows, :] = jnp.where((lane & (HEAD_DIM // 2)) == 0, lse_ref[rows, :], jnp.where(low, d0, d1))
            return carry

        lax.fori_loop(0, S // TILE, pack, 0)
        _regroup(ldp, stage, L16)
        dqp[...] = jnp.zeros_like(dqp)
        dkp[...] = jnp.zeros_like(dkp)
        dvp[...] = jnp.zeros_like(dvp)
        db_ref[...] = jnp.zeros_like(db_ref)
        low = _low_lanes((ATT_BLOCK, PAIR))

        for branch in BRANCH_ORDER:
            _, _, run_len, _ = _branch_geometry(branch, S)

            def step(it, carry, branch=branch, run_len=run_len):
                cur, prev, variant = _block_rows(it, branch, S)
                q = _load_block(qp, cur, run_len).astype(BF16)
                dout = _load_block(dop, cur, run_len).astype(BF16)
                ld = _load_block(ldp, cur, run_len)
                k = jnp.concatenate([_load_block(kp, prev, run_len), _load_block(kp, cur, run_len)], axis=0).astype(BF16)
                v = jnp.concatenate([_load_block(vp, prev, run_len), _load_block(vp, cur, run_len)], axis=0).astype(BF16)
                half = HEAD_DIM // 2
                lse2 = jnp.concatenate([ld[:, 0:1], ld[:, HEAD_DIM:HEAD_DIM + 1]], axis=0)
                delta2 = jnp.concatenate([ld[:, half:half + 1], ld[:, HEAD_DIM + half:HEAD_DIM + half + 1]], axis=0)
                q2, do2 = _stack_heads(q, low), _stack_heads(dout, low)
                s = lax.dot_general(q2, k, _NT, preferred_element_type=F32) * (HEAD_DIM ** -0.5)
                p = jnp.exp(s + b_ref[2 * branch + variant].reshape(2 * ATT_BLOCK, 2 * ATT_BLOCK) - lse2)
                dp = lax.dot_general(do2, v, _NT, preferred_element_type=F32)
                ds = p * (dp - delta2)
                db_ref[branch] += ds.reshape(2, ATT_BLOCK, 2 * ATT_BLOCK)
                dsb = (ds * (HEAD_DIM ** -0.5)).astype(BF16)
                dq = _unstack_heads(jnp.dot(dsb, k, preferred_element_type=F32), low)
                dk = lax.dot_general(dsb, q2, _TN, preferred_element_type=F32)
                dv = lax.dot_general(p.astype(BF16), do2, _TN, preferred_element_type=F32)
                _store_block(dqp, cur, run_len, dq, add=True)
                _store_block(dkp, prev, run_len, dk[:ATT_BLOCK], add=True)
                _store_block(dvp, prev, run_len, dv[:ATT_BLOCK], add=True)
                _store_block(dkp, cur, run_len, dk[ATT_BLOCK:], add=True)
                _store_block(dvp, cur, run_len, dv[ATT_BLOCK:], add=True)
                return carry

            lax.fori_loop(0, n_iter, step, 0, unroll=ATTN_BWD_UNROLL)

        _ungroup(dq_ref, dqp, L16)
        _ungroup(dk_ref, dkp, L16)
        _ungroup(dv_ref, dvp, L16)

    col = lambda part: pl.BlockSpec((S, PAIR), lambda hp: (0, part * N_PAIRS + hp))
    one = pl.BlockSpec((S, PAIR), lambda hp: (0, hp))
    return pl.pallas_call(
        body, name=name, grid=(N_PAIRS,),
        in_specs=[col(0), col(1), col(2), one, one, one,
                  pl.BlockSpec((6, 2, ATT_BLOCK, 2 * ATT_BLOCK), lambda hp: (0, hp, 0, 0))],
        out_specs=[one, one, one, pl.BlockSpec((3, 2, ATT_BLOCK, 2 * ATT_BLOCK), lambda hp: (0, hp, 0, 0))],
        out_shape=[_sds((S, D_MODEL), F32)] * 3 + [_sds((3, N_HEADS, ATT_BLOCK, 2 * ATT_BLOCK), F32)],
        scratch_shapes=[pltpu.VMEM((S, PAIR), F32)] * 8,
        compiler_params=pltpu.CompilerParams(dimension_semantics=("parallel",), vmem_limit_bytes=ATTN_VMEM_LIMIT_BYTES),
    )(qkvn, qkvn, qkvn, att, datt, lse, bias)


def adamw(w, g, m, v, name):
    n, R, C = w.shape

    def body(w_ref, g_ref, m_ref, v_ref, d_ref, nm_ref, nv_ref, go_ref):
        gv = g_ref[...]
        go_ref[...] = gv
        m2 = ADAM_B1 * m_ref[...] + (1.0 - ADAM_B1) * gv
        v2 = ADAM_B2 * v_ref[...] + (1.0 - ADAM_B2) * (gv * gv)
        m_hat = m2 / (1.0 - ADAM_B1 ** ADAM_STEP)
        v_hat = v2 / (1.0 - ADAM_B2 ** ADAM_STEP)
        d_ref[...] = -ADAM_LR * (m_hat / (jnp.sqrt(v_hat) + ADAM_EPS) + ADAM_WD * w_ref[...])
        nm_ref[...] = m2
        nv_ref[...] = v2

    tr = R
    while tr * C * 4 > ELEMENTWISE_BLOCK_BYTES and tr % 16 == 0:
        tr //= 2
    spec = pl.BlockSpec((None, tr, C), lambda i, r: (i, r, 0))
    return pl.pallas_call(
        body, name=name, grid=(n, R // tr), in_specs=[spec] * 4, out_specs=[spec] * 4,
        out_shape=[_sds((n, R, C), F32)] * 4, compiler_params=_params("parallel", "parallel"),
    )(w, g, m, v)


ANY = pl.BlockSpec(memory_space=pl.ANY)


def _coords():
    return lax.axis_index("x"), lax.axis_index("y"), lax.axis_index("c")


def _other_chips(mx, my):
    return [(1 - mx, my), (mx, 1 - my), (1 - mx, 1 - my)]


def _remote(src, dst, send, recv, dev):
    return pltpu.make_async_remote_copy(src_ref=src, dst_ref=dst, send_sem=send, recv_sem=recv, device_id=dev,
                                        device_id_type=MESH)


HBM =pl.BlockSpec(memory_space=pltpu.HBM)
SEM = pl.BlockSpec(memory_space=pltpu.SEMAPHORE)
_SPLIT_COPY = pltpu.CompilerParams(has_side_effects=pltpu.SideEffectType.DATAFLOW_SIDE_EFFECTING)


def _in_hbm(a):
    return pltpu.with_memory_space_constraint(a, pltpu.HBM)


def cast_into_slot(w, layer, chip_core, name, dtype=BF16):
    _, _, hR, C = w.shape

    def body(s_ref, w_ref, o_ref):
        del s_ref
        o_ref[...] = w_ref[...].astype(dtype)

    grid_spec = pltpu.PrefetchScalarGridSpec(
        num_scalar_prefetch=1, grid=(2,),
        in_specs=[pl.BlockSpec((None, None, hR, C), lambda h, s: (layer, h, 0, 0))],
        out_specs=pl.BlockSpec((None, None, hR, C), lambda h, s: (s[0], h, 0, 0)))
    return pl.pallas_call(body, name=name, grid_spec=grid_spec, out_shape=_sds((N_CHIPS, 2, hR, C), dtype),
                          compiler_params=_params("parallel"))(chip_core, w)


def gather_start(lands, groups, name):
    n = len(lands)
    n_groups = len(groups)

    def body(*refs):
        ins = refs[:n]
        sems = refs[n:n + 2 * n_groups]
        token = refs[-1]
        mx, my, mc = _coords()
        chip = 2 * mx + my
        for g, members in enumerate(groups):
            send, recv = sems[2 * g], sems[2 * g + 1]
            for i, a in enumerate(members):
                mine = ins[a].at[chip, mc]
                for k, (px, py) in enumerate(_other_chips(mx, my)):
                    _remote(mine, mine, send.at[3 * i + k], recv.at[3 * i + k], (px, py, mc)).start()
        token[...] = jnp.zeros_like(token)

    sem_shapes = []
    for members in groups:
        sem_shapes += [pltpu.SemaphoreType.DMA((3 * len(members),))] * 2
    outs = pl.pallas_call(
        body, name=name, in_specs=[HBM] * n,
        out_specs=[SEM] * (2 * n_groups) + [HBM] * n + [pl.BlockSpec(memory_space=pltpu.VMEM)],
        out_shape=sem_shapes + [pltpu.HBM(a.shape, a.dtype) for a in lands] + [_sds((SUBLANES, LANES), F32)],
        input_output_aliases={a: 2 * n_groups + a for a in range(n)}, compiler_params=_SPLIT_COPY,
    )(*[_in_hbm(a) for a in lands])
    sems = [(outs[2 * g], outs[2 * g + 1]) for g in range(n_groups)]
    return sems, list(outs[2 * n_groups:2 * n_groups + n]), outs[-1]


def gather_forward(lands, sems, after, name):
    n = len(lands)

    def body(*refs):
        ins = refs[:n]
        send, recv = refs[n], refs[n + 1]
        fsend, frecv = refs[n + 3], refs[n + 4]
        mx, my, mc = _coords()
        for i in range(n):
            for k, (px, py) in enumerate(_other_chips(mx, my)):
                landed = ins[i].at[2 * px + py, mc]
                cp = _remote(landed, landed, send.at[3 * i + k], recv.at[3 * i + k], (px, py, mc))
                cp.wait_send()
                cp.wait_recv()
                _remote(landed, landed, fsend.at[3 * i + k], frecv.at[3 * i + k], (mx, my, 1 - mc)).start()

    outs = pl.pallas_call(
        body, name=name, in_specs=[HBM] * n + [SEM, SEM, ANY], out_specs=[SEM, SEM] + [HBM] * n,
        out_shape=[pltpu.SemaphoreType.DMA((3 * n,))] * 2 + [pltpu.HBM(a.shape, a.dtype) for a in lands],
        input_output_aliases={a: 2 + a for a in range(n)}, compiler_params=_SPLIT_COPY,
    )(*lands, sems[0], sems[1], after)
    return (outs[0], outs[1]), list(outs[2:])


def gather_wait(lands, sems, after, name):
    n = len(lands)

    def body(*refs):
        ins = refs[:n]
        fsend, frecv = refs[n], refs[n + 1]
        mx, my, mc = _coords()
        for i in range(n):
            for k, (px, py) in enumerate(_other_chips(mx, my)):
                theirs = ins[i].at[2 * px + py, 1 - mc]
                cp = _remote(theirs, theirs, fsend.at[3 * i + k], frecv.at[3 * i + k], (mx, my, 1 - mc))
                cp.wait_send()
                cp.wait_recv()

    outs = pl.pallas_call(
        body, name=name, in_specs=[HBM] * n + [SEM, SEM, ANY], out_specs=[HBM] * n,
        out_shape=[pltpu.HBM(a.shape, a.dtype) for a in lands],
        input_output_aliases={a: a for a in range(n)}, compiler_params=_SPLIT_COPY,
    )(*lands, sems[0], sems[1], after)
    return list(outs)


def _peers(mx, my, mc):
    return [(1 - mx if k & 4 else mx, 1 - my if k & 2 else my, 1 - mc if k & 1 else mc) for k in range(1, N_DEV)]


def devices_start(x, name):
    def body(x_ref, land_ref, send, recv, x_thru, land_thru):
        mx, my, mc = _coords()
        me = 4 * mx + 2 * my + mc
        for k, peer in enumerate(_peers(mx, my, mc)):
            _remote(x_ref, land_ref.at[me], send.at[k], recv.at[k], peer).start()

    land = lax.empty((N_DEV,) + x.shape, x.dtype)
    outs = pl.pallas_call(
        body, name=name, in_specs=[HBM, HBM], out_specs=[SEM, SEM, HBM, HBM],
        out_shape=[pltpu.SemaphoreType.DMA((N_DEV - 1,))] * 2 + [pltpu.HBM(x.shape, x.dtype), pltpu.HBM(land.shape, x.dtype)],
        input_output_aliases={0: 2, 1: 3}, compiler_params=_SPLIT_COPY,
    )(_in_hbm(x), _in_hbm(land))
    return (outs[0], outs[1]), outs[2], outs[3]


def devices_wait(x, land, sems, after, name):
    def body(x_ref, land_ref, send, recv, after_ref, x_thru, land_thru):
        mx, my, mc = _coords()
        for k, (px, py, pc) in enumerate(_peers(mx, my, mc)):
            cp = _remote(x_ref, land_ref.at[4 * px + 2 * py + pc], send.at[k], recv.at[k], (px, py, pc))
            cp.wait_send()
            cp.wait_recv()

    outs = pl.pallas_call(
        body, name=name, in_specs=[HBM, HBM, SEM, SEM, ANY], out_specs=[HBM, HBM],
        out_shape=[pltpu.HBM(x.shape, x.dtype), pltpu.HBM(land.shape, land.dtype)],
        input_output_aliases={0: 0, 1: 1}, compiler_params=_SPLIT_COPY,
    )(x, land, sems[0], sems[1], after)
    return outs[0], outs[1]


def device_sum(land, own, me, name):
    _, R, C = land.shape

    def body(s_ref, l_ref, o_ref_in, o_ref):
        acc = None
        for q in range(N_DEV):
            term = jnp.where(s_ref[0] == q, o_ref_in[...], l_ref[q])
            acc = term if acc is None else acc + term
        o_ref[...] = acc

    grid_spec = pltpu.PrefetchScalarGridSpec(
        num_scalar_prefetch=1, grid=(1,),
        in_specs=[pl.BlockSpec((N_DEV, R, C), lambda i, s: (0, 0, 0)), pl.BlockSpec((R, C), lambda i, s: (0, 0))],
        out_specs=pl.BlockSpec((R, C), lambda i, s: (0, 0)))
    return pl.pallas_call(body, name=name, grid_spec=grid_spec, out_shape=_sds((R, C), F32),
                          compiler_params=_params("arbitrary"))(me, land, own)


def reduce_send(grads, name):
    n = len(grads)

    def body(*refs):
        ins, lands = refs[:n], refs[n:2 * n]
        send, recv = refs[2 * n], refs[2 * n + 1]
        mx, my, mc = _coords()
        me = 4 * mx + 2 * my + mc
        for a in range(n):
            for k, (px, py, pc) in enumerate(_peers(mx, my, mc)):
                _remote(ins[a].at[2 * px + py, pc], lands[a].at[me], send.at[7 * a + k], recv.at[7 * a + k], (px, py, pc)).start()

    lands = [lax.empty((N_DEV,) + g.shape[2:], g.dtype) for g in grads]
    outs = pl.pallas_call(
        body, name=name, in_specs=[HBM] * (2 * n), out_specs=[SEM, SEM] + [HBM] * (2 * n),
        out_shape=[pltpu.SemaphoreType.DMA((7 * n,))] * 2 + [pltpu.HBM(a.shape, a.dtype) for a in grads + lands],
        input_output_aliases={a: 2 + a for a in range(2 * n)}, compiler_params=_SPLIT_COPY,
    )(*[_in_hbm(a) for a in grads + lands])
    return (outs[0], outs[1]), list(outs[2:2 + n]), list(outs[2 + n:])


def reduce_wait(grads, lands, sems, after, name):
    n = len(grads)

    def body(*refs):
        ins, zones = refs[:n], refs[n:2 * n]
        send, recv = refs[2 * n], refs[2 * n + 1]
        mx, my, mc = _coords()
        for a in range(n):
            for k, (px, py, pc) in enumerate(_peers(mx, my, mc)):
                cp = _remote(ins[a].at[2 * px + py, pc], zones[a].at[4 * px + 2 * py + pc], send.at[7 * a + k],
                             recv.at[7 * a + k], (px, py, pc))
                cp.wait_send()
                cp.wait_recv()

    outs = pl.pallas_call(
        body, name=name, in_specs=[HBM] * (2 * n) + [SEM, SEM, ANY], out_specs=[HBM] * (2 * n),
        out_shape=[pltpu.HBM(a.shape, a.dtype) for a in grads + lands],
        input_output_aliases={a: a for a in range(2 * n)}, compiler_params=_SPLIT_COPY,
    )(*grads, *lands, sems[0], sems[1], after)
    return list(outs[:n]), list(outs[n:])


def reduce_sum(land, grad, place, name, into=None, layer=None):
    _, hR, C = land.shape
    tr = hR
    while N_DEV * tr * C * 2 > 3 * ELEMENTWISE_BLOCK_BYTES and tr % 32 == 0:
        tr //= 2

    def body(s_ref, l_ref, g_ref, *rest):
        o_ref = rest[-1]
        own = g_ref[...].astype(F32)
        acc = None
        for q in range(N_DEV):
            term = jnp.where(s_ref[2] == q, own, l_ref[q].astype(F32))
            acc = term if acc is None else acc + term
        o_ref[...] = acc

    in_specs = [pl.BlockSpec((N_DEV, tr, C), lambda i, s: (0, i, 0)),
                pl.BlockSpec((None, None, tr, C), lambda i, s: (s[0], s[1], i, 0))]
    args = [place, land, grad]
    aliases = {}
    if layer is None:
        out_spec = pl.BlockSpec((None, tr, C), lambda i, s: (s[1], i, 0))
        out_shape = _sds((2, hR, C), F32)
    else:
        out_spec = pl.BlockSpec((None, None, tr, C), lambda i, s: (layer, s[1], i, 0))
        out_shape = _sds((2, 2, hR, C), F32)
        if into is not None:
            in_specs.append(ANY)
            args.append(into)
            aliases = {3: 0}
    grid_spec = pltpu.PrefetchScalarGridSpec(num_scalar_prefetch=1, grid=(hR // tr,), in_specs=in_specs, out_specs=out_spec)
    return pl.pallas_call(body, name=name, grid_spec=grid_spec, out_shape=out_shape, input_output_aliases=aliases,
                          compiler_params=_params("arbitrary"))(*args)


def join_halves(arrays, name):
    n = len(arrays)
    pieces = [(a, l) for a, arr in enumerate(arrays) for l in (range(arr.shape[0]) if arr.ndim == 4 else [None])]

    def body(*refs):
        ins = refs[:n]
        send, recv = refs[2 * n:]
        mx, my, mc = _coords()

        def half(a, l, h):
            return ins[a].at[h] if l is None else ins[a].at[l, h]

        sends = [_remote(half(a, l, mc), half(a, l, mc), send.at[i], recv.at[i], (mx, my, 1 - mc))
                 for i, (a, l) in enumerate(pieces)]
        for cp in sends:
            cp.start()
        for i, (a, l) in enumerate(pieces):
            theirs = half(a, l, 1 - mc)
            _remote(theirs, theirs, send.at[i], recv.at[i], (mx, my, 1 - mc)).wait_recv()
        for cp in sends:
            cp.wait_send()

    return pl.pallas_call(
        body, name=name, in_specs=[ANY] * n, out_specs=[ANY] * n, out_shape=[_sds(a.shape, a.dtype) for a in arrays],
        input_output_aliases={a: a for a in range(n)},
        scratch_shapes=[pltpu.SemaphoreType.DMA((len(pieces),)), pltpu.SemaphoreType.DMA((len(pieces),))],
    )(*arrays)


LANES = 128
SUBLANES = 8


def _n_rows(shape):
    rows = -(-int(np.prod(shape)) // LANES)
    return -(-rows // SUBLANES) * SUBLANES


def _as_rows(a):
    flat = a.reshape(-1)
    rows = _n_rows(a.shape)
    return jnp.pad(flat, (0, rows * LANES - flat.shape[0])).reshape(rows, LANES)


def _pack(arrays):
    return jnp.concatenate([_as_rows(a) for a in arrays], axis=0)


def _unpack(rows, shapes):
    out, r0 = [], 0
    for s in shapes:
        n = _n_rows(s)
        out.append(rows[r0:r0 + n].reshape(-1)[:int(np.prod(s))].reshape(s))
        r0 += n
    return out


REPLICATED_SMALL = [("rel_bias", (32, 16)), ("even_norm", (1, 1024)), ("even_pool_w", (1, 4, 128, 128)),
                    ("even_pool_scale", (1, 512)), ("odd_q_norm", (1, 64)), ("odd_k_norm", (1, 64)),
                    ("ffn_norm", (2, 1024)), ("ffn_conv_b", (2, 5632))]
SHARDED_SMALL = [("even_conv_w", (1, 3, 128)), ("odd_norm", (1, 256)), ("ffn_conv_w", (2, 3, 1408))]
BIG = ["even_w_in", "even_w_out", "odd_w_qkv", "odd_w_o", "ffn_w_up", "ffn_w_down"]
WEIGHT_ORDER = ["rel_bias", "even_norm", "even_w_in", "even_conv_w", "even_pool_w", "even_pool_scale", "even_w_out",
                "odd_norm", "odd_w_qkv", "odd_q_norm", "odd_k_norm", "odd_w_o", "ffn_norm", "ffn_w_up", "ffn_conv_w",
                "ffn_conv_b", "ffn_w_down"]


def kernel(x, rel_bias, even_norm, even_w_in, even_conv_w, even_pool_w, even_pool_scale, even_w_out, odd_norm, odd_w_qkv, odd_q_norm, odd_k_norm, odd_w_o, ffn_norm, ffn_w_up, ffn_conv_w, ffn_conv_b, ffn_w_down, loss_target, m_rel_bias, m_even_norm, m_even_w_in, m_even_conv_w, m_even_pool_w, m_even_pool_scale, m_even_w_out, m_odd_norm, m_odd_w_qkv, m_odd_q_norm, m_odd_k_norm, m_odd_w_o, m_ffn_norm, m_ffn_w_up, m_ffn_conv_w, m_ffn_conv_b, m_ffn_w_down, v_rel_bias, v_even_norm, v_even_w_in, v_even_conv_w, v_even_pool_w, v_even_pool_scale, v_even_w_out, v_odd_norm, v_odd_w_qkv, v_odd_q_norm, v_odd_k_norm, v_odd_w_o, v_ffn_norm, v_ffn_w_up, v_ffn_conv_w, v_ffn_conv_b, v_ffn_w_down):
    W = dict(rel_bias=rel_bias, even_norm=even_norm, even_w_in=even_w_in, even_conv_w=even_conv_w, even_pool_w=even_pool_w,
             even_pool_scale=even_pool_scale, even_w_out=even_w_out, odd_norm=odd_norm, odd_w_qkv=odd_w_qkv,
             odd_q_norm=odd_q_norm, odd_k_norm=odd_k_norm, odd_w_o=odd_w_o, ffn_norm=ffn_norm, ffn_w_up=ffn_w_up,
             ffn_conv_w=ffn_conv_w, ffn_conv_b=ffn_conv_b, ffn_w_down=ffn_w_down)
    M1 = dict(rel_bias=m_rel_bias, even_norm=m_even_norm, even_w_in=m_even_w_in, even_conv_w=m_even_conv_w,
              even_pool_w=m_even_pool_w, even_pool_scale=m_even_pool_scale, even_w_out=m_even_w_out, odd_norm=m_odd_norm,
              odd_w_qkv=m_odd_w_qkv, odd_q_norm=m_odd_q_norm, odd_k_norm=m_odd_k_norm, odd_w_o=m_odd_w_o,
              ffn_norm=m_ffn_norm, ffn_w_up=m_ffn_w_up, ffn_conv_w=m_ffn_conv_w, ffn_conv_b=m_ffn_conv_b,
              ffn_w_down=m_ffn_w_down)
    M2 = dict(rel_bias=v_rel_bias, even_norm=v_even_norm, even_w_in=v_even_w_in, even_conv_w=v_even_conv_w,
              even_pool_w=v_even_pool_w, even_pool_scale=v_even_pool_scale, even_w_out=v_even_w_out, odd_norm=v_odd_norm,
              odd_w_qkv=v_odd_w_qkv, odd_q_norm=v_odd_q_norm, odd_k_norm=v_odd_k_norm, odd_w_o=v_odd_w_o,
              ffn_norm=v_ffn_norm, ffn_w_up=v_ffn_w_up, ffn_conv_w=v_ffn_conv_w, ffn_conv_b=v_ffn_conv_b,
              ffn_w_down=v_ffn_w_down)
    mx, my, mc = _coords()
    chip = 2 * mx + my
    me = 4 * mx + 2 * my + mc
    place = jnp.stack([chip, mc, me]).astype(jnp.int32)
    xs, target = x[0], loss_target[0]

    def halves(w):
        return w.reshape((w.shape[0], 2, w.shape[-2] // 2, w.shape[-1]))

    small_rows = jnp.pad(_pack([even_conv_w, odd_norm, ffn_conv_w]), ((0, SUBLANES), (0, 0)))
    first = [cast_into_slot(halves(even_w_in), 0, place, "cast_w_in"), cast_into_slot(halves(even_w_out), 0, place, "cast_w_out"),
             cast_into_slot(small_rows.reshape(1, 2, small_rows.shape[0] // 2, LANES), 0, place, "small_into_slot", dtype=F32)]
    first_sems, first, token = gather_start(first, [[0, 1, 2]], "gather_start_first")
    even_norm_after_start = even_norm + token[0:1, 0:1]

    def later(a):
        return lax.optimization_barrier((a, token))[0]

    up_f32, down_f32 = halves(later(ffn_w_up)), halves(later(ffn_w_down))
    rest = [cast_into_slot(up_f32, 0, place, "cast_w_up0"), cast_into_slot(down_f32, 0, place, "cast_w_down0"),
            cast_into_slot(halves(later(odd_w_qkv)), 0, place, "cast_w_qkv"), cast_into_slot(halves(later(odd_w_o)), 0, place, "cast_w_o"),
            cast_into_slot(up_f32, 1, place, "cast_w_up1"), cast_into_slot(down_f32, 1, place, "cast_w_down1")]
    rest_sems, rest, _ = gather_start(rest, [[0], [1], [2, 3], [4], [5]], "gather_start_rest")
    group_arrays = [first, [rest[0]], [rest[1]], [rest[2], rest[3]], [rest[4]], [rest[5]]]
    group_sems = first_sems + rest_sems

    def gathered(group, tag, after_landing, after_passing):
        sems, arrays = gather_forward(group_arrays[group], group_sems[group], after_landing, "gather_forward_" + tag)
        return gather_wait(arrays, sems, after_passing, "gather_wait_" + tag)

    pool_w = cast_bf16(even_pool_w[0], "cast_pool_w")
    gqk = jnp.stack([jnp.tile(odd_q_norm[0], N_HEADS), jnp.tile(odd_k_norm[0], N_HEADS),
                     jnp.ones((D_MODEL,), F32)])[:, None, :]
    bias = bias_expand(later(rel_bias).T, "bias_expand").reshape(6, N_HEADS, ATT_BLOCK, 2 * ATT_BLOCK)
    xn0 = rmsnorm_fwd(xs, even_norm_after_start, "even_norm")
    got = gathered(0, "even", bias, xn0)
    w_in = got[0].reshape(N_CHIPS, 1, D_MODEL, EVEN_IN // N_CHIPS)
    w_out = got[1].reshape(1, 1, D_MODEL, D_MODEL)
    small = got[2].reshape(N_CHIPS, small_rows.shape[0], LANES)
    conv_w_full = small[:, 0:3].transpose(1, 0, 2).reshape(3, A_WIDTH)
    odd_norm_full = small[:, 8:10].reshape(1, D_MODEL)
    ffn_cw_full = small[:, 16:82].reshape(N_CHIPS, 2, 3, 2 * D_FF // N_CHIPS).transpose(1, 2, 0, 3).reshape(2, 3, 2 * D_FF)

    def ffn_fwd(l, xin, xn):
        up, u, act = up_glu_fwd(xn, w_up[l], ffn_cw_full[l], ffn_conv_b[l:l + 1], f"ffn{l}_up_glu")
        return act, (xin, xn, up, u, act)

    w_up, w_down = [None, None], [None, None]
    proj, mix = in_mixer_fwd(xn0, w_in, conv_w_full, pool_w, even_pool_scale, "even_in_mixer")
    x1, xn1 = mm_res_norm(mix, w_out, xs, ffn_norm[0:1], "even_out")
    w_up[0] = gathered(1, "up0", proj, x1)[0].reshape(N_CHIPS, 1, D_MODEL, 2 * D_FF // N_CHIPS)
    act0, ffn0 = ffn_fwd(0, x1, xn1)
    w_down[0] = gathered(2, "down0", act0, act0)[0].reshape(1, 1, D_FF, D_MODEL)
    x2, xn2 = mm_res_norm(act0, w_down[0], x1, odd_norm_full, "ffn0_down")
    got = gathered(3, "odd", x1, x2)
    w_qkv = got[0].reshape(N_CHIPS, 1, D_MODEL, 3 * D_MODEL // N_CHIPS)
    w_o = got[1].reshape(1, 1, D_MODEL, D_MODEL)
    qkv, qkvn = qkv_qknorm_fwd(xn2, w_qkv, gqk, "odd_qkv_qknorm")
    att, lse = attn_fwd(qkvn, bias, "attn_fwd")
    x3, xn3 = mm_res_norm(att, w_o, x2, ffn_norm[1:2], "odd_out")
    w_up[1] = gathered(4, "up1", x2, x3)[0].reshape(N_CHIPS, 1, D_MODEL, 2 * D_FF // N_CHIPS)
    act1, ffn1 = ffn_fwd(1, x3, xn3)
    w_down[1] = gathered(5, "down1", act1, act1)[0].reshape(1, 1, D_FF, D_MODEL)
    dy, dyb, sq = mm_res_loss(act1, w_down[1], x3, target, "ffn1_down_loss")
    loss = lax.psum(0.5 * jnp.sum(sq) * (1.0 / D_MODEL), ("x", "y", "c"))

    def ffn_bwd(l, dy, dyb, saved):
        xin, xn, up, u, act = saved
        dw_down = mm_tn(act, dyb, f"ffn{l}_dw_down", J=1, tk=D_FF // 2, tm=1024)
        dact = mm_nt(dyb, w_down[l], f"ffn{l}_dact", tr=D_FF // 2, out_dtype=BF16, tm=1024)
        dup, dcw, dcb = glu_bwd(up, u, dact, ffn_cw_full[l], f"ffn{l}_glu_bwd")
        dw_up = mm_tn(xn, dup, f"ffn{l}_dw_up", J=N_CHIPS, tk=512, tm=1024, jb=2)
        dx, dxb, dg = mm_nt_norm_bwd(dup, w_up[l], xin, ffn_norm[l:l + 1], dy, f"ffn{l}_dx")
        return dx, dxb, (dw_down, dw_up, dcw, dcb, dg)

    def quarters(g):
        return g.reshape(N_CHIPS, 2, g.shape[0] * g.shape[1] // (2 * N_CHIPS), g.shape[-1])

    def reduce_start(grads, tag, then):
        sems, parts, zones = reduce_send([quarters(g) for g in grads], "reduce_send_" + tag)
        then, parts = lax.optimization_barrier((then, parts))
        return (sems, parts, zones), then

    dx3, dx3b, g_ffn1 = ffn_bwd(1, dy, dyb, ffn1)
    red_ffn1, (dx3, dx3b) = reduce_start([g_ffn1[1], g_ffn1[0]], "ffn1", (dx3, dx3b))
    dw_o = mm_tn(att, dx3b, "odd_dw_o", J=1, tk=512, tm=1024)
    datt = mm_nt(dx3b, w_o, "odd_datt", tr=D_MODEL, out_dtype=BF16)
    dq, dk, dv, dbias = attn_bwd(qkvn, att, datt, lse, bias, "attn_bwd")
    dqkv, dgqk = qknorm_bwd(qkv, dq, dk, dv, gqk, "odd_qknorm_bwd")
    dw_qkv = mm_tn(xn2, dqkv, "odd_dw_qkv", J=N_CHIPS, tk=512, tm=1024)
    red_odd, dqkv = reduce_start([dw_qkv, dw_o], "odd", dqkv)
    dx2, dx2b, dg_odd = mm_nt_norm_bwd(dqkv, w_qkv, x2, odd_norm_full, dx3, "odd_dx")
    dx1, dx1b, g_ffn0 = ffn_bwd(0, dx2, dx2b, ffn0)
    red_ffn0, (dx1, dx1b) = reduce_start([g_ffn0[1], g_ffn0[0]], "ffn0", (dx1, dx1b))
    dw_out = mm_tn(mix, dx1b, "even_dw_out", J=1, tk=512, tm=1024)
    dmix = mm_nt(dx1b, w_out, "even_dmix", tr=D_MODEL)
    dproj, dcw_even, dpw, dps = mixer_bwd(proj, dmix, conv_w_full, pool_w, even_pool_scale, "even_mixer_bwd")
    dw_in = mm_tn(xn0, dproj, "even_dw_in", J=N_CHIPS, tk=512, tm=1024)
    grad_x, _, dg_even = mm_nt_norm_bwd(dproj, w_in, xs, even_norm, dx1, "even_dx")
    d_rel = jnp.sum(bias_reduce(dbias.reshape(3, N_HEADS, 2 * ATT_BLOCK * ATT_BLOCK), "bias_reduce"), axis=0).T

    red_even, grad_x = reduce_start([dw_in, dw_out], "even", grad_x)

    dcw_sh = dcw_even.reshape(3, N_CHIPS, A_WIDTH // N_CHIPS).transpose(1, 0, 2)
    don_sh = dg_odd.reshape(N_CHIPS, D_MODEL // N_CHIPS)
    dfcw = jnp.stack([g_ffn0[2], g_ffn1[2]])
    dfcw_sh = dfcw.reshape(2, 3, N_CHIPS, 2 * D_FF // N_CHIPS).transpose(2, 0, 1, 3)
    rep_grads = [d_rel, dg_even, dpw[None], dps, _head_sum(dgqk[0]), _head_sum(dgqk[1]),
                 jnp.concatenate([g_ffn0[4], g_ffn1[4]], axis=0), jnp.concatenate([g_ffn0[3], g_ffn1[3]], axis=0)]
    rep_rows = _pack(rep_grads)
    shard_rows = jnp.concatenate([_pack([dcw_sh[j], don_sh[j], dfcw_sh[j]]) for j in range(N_CHIPS)], axis=0)
    n_rep, n_shard = rep_rows.shape[0], shard_rows.shape[0] // N_CHIPS
    small_sems, small_rows, small_land = devices_start(jnp.concatenate([rep_rows, shard_rows], axis=0), "small_grads_start")
    grad_x, small_rows = lax.optimization_barrier((grad_x, small_rows))

    def reduce_end(red, tag, after):
        sems, parts, zones = red
        parts, zones = reduce_wait(parts, zones, sems, after, "reduce_wait_" + tag)
        return zones, parts

    z_ffn1, p_ffn1 = reduce_end(red_ffn1, "ffn1", grad_x)
    z_odd, p_odd = reduce_end(red_odd, "odd", grad_x)
    r_qkv = reduce_sum(z_odd[0], p_odd[0], place, "reduce_sum_w_qkv")
    r_o = reduce_sum(z_odd[1], p_odd[1], place, "reduce_sum_w_o")
    r_up = reduce_sum(z_ffn1[0], p_ffn1[0], place, "reduce_sum_w_up1", layer=1)
    r_down = reduce_sum(z_ffn1[1], p_ffn1[1], place, "reduce_sum_w_down1", layer=1)
    r_qkv, r_o, r_up, r_down = lax.optimization_barrier((r_qkv, r_o, r_up, r_down))
    z_ffn0, p_ffn0 = reduce_end(red_ffn0, "ffn0", r_down)
    r_up = reduce_sum(z_ffn0[0], p_ffn0[0], place, "reduce_sum_w_up0", into=r_up, layer=0)
    r_down = reduce_sum(z_ffn0[1], p_ffn0[1], place, "reduce_sum_w_down0", into=r_down, layer=0)
    later = ["odd_w_qkv", "odd_w_o", "ffn_w_up", "ffn_w_down"]
    joined = join_halves([r_qkv, r_o, r_up, r_down], "grads_join_late_layers")
    G = {nm: g.reshape(W[nm].shape) for nm, g in zip(later, joined)}

    D_, NM, NV = {}, {}, {}

    def update(nm):
        as3 = lambda a: a.reshape((-1,) + a.shape[-2:])
        outs = adamw(as3(W[nm]), as3(G[nm]), as3(M1[nm]), as3(M2[nm]), "adamw_" + nm)
        D_[nm], NM[nm], NV[nm], G[nm] = [o.reshape(W[nm].shape) for o in outs]

    def all_before(names):
        tied = lax.optimization_barrier([D_[nm] for nm in names])
        for nm, d in zip(names, tied):
            D_[nm] = d
        return tied[0]

    for nm in later:
        update(nm)
    z_even, p_even = reduce_end(red_even, "even", all_before(later))
    joined = join_halves([reduce_sum(z_even[0], p_even[0], place, "reduce_sum_w_in"),
                          reduce_sum(z_even[1], p_even[1], place, "reduce_sum_w_out")], "grads_join_first_layer")
    first = ["even_w_in", "even_w_out"]
    for nm, g in zip(first, joined):
        G[nm] = g.reshape(W[nm].shape)
        update(nm)
    small_rows, small_land = devices_wait(small_rows, small_land, small_sems, all_before(first), "small_grads_wait")
    small_sum = device_sum(small_land, small_rows, place[2:3], "small_grads_sum")
    mine = lax.dynamic_slice_in_dim(small_sum, n_rep + chip * n_shard, n_shard, axis=0)
    g_small = jnp.concatenate([small_sum[:n_rep], mine], axis=0)
    small_names = [n for n, _ in REPLICATED_SMALL + SHARDED_SMALL]
    small_shapes = [s for _, s in REPLICATED_SMALL + SHARDED_SMALL]
    G.update(dict(zip(small_names, _unpack(g_small, small_shapes))))
    packs = [_pack([d[n] for n in small_names])[None] for d in (W, M1, M2)]
    outs = adamw(packs[0], g_small[None], packs[1], packs[2], "adamw_small")
    for dst, o in zip((D_, NM, NV), outs[:3]):
        dst.update(dict(zip(small_names, _unpack(o[0], small_shapes))))

    return (loss, grad_x[None], *[G[n] for n in WEIGHT_ORDER], *[D_[n] for n in WEIGHT_ORDER],
            *[NM[n] for n in WEIGHT_ORDER], *[NV[n] for n in WEIGHT_ORDER])


def _head_sum(dg):
    return jnp.sum(dg.reshape(N_HEADS, HEAD_DIM), axis=0, keepdims=True)
```

```python
import functools
import math

import numpy as np
import jax
import jax.numpy as jnp
from jax import lax
from jax.experimental import pallas as pl
from jax.experimental.pallas import tpu as pltpu

F32 = jnp.float32
BF16 = jnp.bfloat16

D_MODEL = 1024
N_HEADS = 16
HEAD_DIM = 64
A_WIDTH = 512
POOL_WINDOWS = (2, 4, 8, 16)
POOL_GROUP = 128
EVEN_IN = 2048
D_FF = 2816
DILATED_PAIRS = ((128, 1), (512, 4), (2048, 16))
ATT_BLOCK = 128
N_REL_BUCKETS = 32
REL_MAX_DISTANCE = 2048
EPS = 1e-6
MASK_VALUE = -1e30
ADAM_LR, ADAM_B1, ADAM_B2, ADAM_EPS, ADAM_WD, ADAM_STEP = 0.001, 0.9, 0.999, 1e-08, 0.01, 10

VMEM_LIMIT_BYTES = 48 * 1024 * 1024
ELEMENTWISE_BLOCK_BYTES = 2 * 1024 * 1024
N_CHIPS = 4
N_DEV = 8
MESH = pl.DeviceIdType.MESH


def _params(*sem):
    return pltpu.CompilerParams(dimension_semantics=sem if sem else None, vmem_limit_bytes=VMEM_LIMIT_BYTES)


def _sds(shape, dtype):
    return jax.ShapeDtypeStruct(tuple(shape), dtype)


def cast_bf16(x, name, tr=None):
    lead, (R, C) = x.shape[:-2], x.shape[-2:]
    n = int(np.prod(lead)) if lead else 1
    x3 = x.reshape((n, R, C))
    tr = tr or R

    def body(x_ref, o_ref):
        o_ref[...] = x_ref[...].astype(BF16)

    out = pl.pallas_call(
        body, name=name, grid=(n, R // tr),
        in_specs=[pl.BlockSpec((None, tr, C), lambda i, r: (i, r, 0))],
        out_specs=pl.BlockSpec((None, tr, C), lambda i, r: (i, r, 0)),
        out_shape=_sds((n, R, C), BF16), compiler_params=_params("parallel", "parallel"),
    )(x3)
    return out.reshape(lead + (R, C))


def rmsnorm_fwd(x, g, name, ts=512):
    S, Dm = x.shape

    def body(x_ref, g_ref, o_ref):
        xv = x_ref[...]
        r = lax.rsqrt(jnp.mean(xv * xv, axis=-1, keepdims=True) + EPS)
        o_ref[...] = ((xv * r) * g_ref[...]).astype(BF16)

    return pl.pallas_call(
        body, name=name, grid=(S // ts,),
        in_specs=[pl.BlockSpec((ts, Dm), lambda i: (i, 0)), pl.BlockSpec((1, Dm), lambda i: (0, 0))],
        out_specs=pl.BlockSpec((ts, Dm), lambda i: (i, 0)),
        out_shape=_sds((S, Dm), BF16), compiler_params=_params("parallel"),
    )(x, g)


def mm_res_norm(a, w, res, gain, name, tm=1024):
    M, K = a.shape
    Dm = w.shape[-1]

    def body(a_ref, w_ref, r_ref, g_ref, y_ref, yn_ref):
        y = r_ref[...] + jnp.dot(a_ref[...], w_ref[...], preferred_element_type=F32)
        y_ref[...] = y
        r = lax.rsqrt(jnp.mean(y * y, axis=-1, keepdims=True) + EPS)
        yn_ref[...] = ((y * r) * g_ref[...]).astype(BF16)

    row = pl.BlockSpec((tm, Dm), lambda m: (m, 0))
    return pl.pallas_call(
        body, name=name, grid=(M // tm,),
        in_specs=[pl.BlockSpec((tm, K), lambda m: (m, 0)),
                  pl.BlockSpec((None, None, K, Dm), lambda m: (0, 0, 0, 0), pipeline_mode=pl.Buffered(1)),
                  row, pl.BlockSpec((1, Dm), lambda m: (0, 0))],
        out_specs=[row, row], out_shape=[_sds((M, Dm), F32), _sds((M, Dm), BF16)],
        compiler_params=_params("parallel"),
    )(a, w, res, gain)


def mm_res_loss(a, w, res, target, name, tm=512):
    M, K = a.shape
    Dm = w.shape[-1]

    def body(a_ref, w_ref, r_ref, t_ref, d_ref, db_ref, s_ref):
        e = (r_ref[...] + jnp.dot(a_ref[...], w_ref[...], preferred_element_type=F32)) - t_ref[...]
        d = e * (1.0 / Dm)
        d_ref[...] = d
        db_ref[...] = d.astype(BF16)
        part = jnp.sum(e * e, axis=0, keepdims=True)

        @pl.when(pl.program_id(0) == 0)
        def _():
            s_ref[...] = part

        @pl.when(pl.program_id(0) > 0)
        def _():
            s_ref[...] += part

    row = pl.BlockSpec((tm, Dm), lambda m: (m, 0))
    return pl.pallas_call(
        body, name=name, grid=(M // tm,),
        in_specs=[pl.BlockSpec((tm, K), lambda m: (m, 0)),
                  pl.BlockSpec((None, None, K, Dm), lambda m: (0, 0, 0, 0), pipeline_mode=pl.Buffered(1)), row, row],
        out_specs=[row, row, pl.BlockSpec((1, Dm), lambda m: (0, 0))],
        out_shape=[_sds((M, Dm), F32), _sds((M, Dm), BF16), _sds((1, Dm), F32)],
        compiler_params=_params("arbitrary"),
    )(a, w, res, target)


def mm_nt(dy, w, name, tr, layer=0, out_dtype=F32, tm=512):
    M = dy.shape[0]
    J, _, R, Ns = w.shape
    dims = (((1,), (1,)), ((), ()))

    def body(dy_ref, w_ref, o_ref):
        acc = None
        for j in range(J):
            p = lax.dot_general(dy_ref[:, j * Ns:(j + 1) * Ns], w_ref[j], dims, preferred_element_type=F32)
            acc = p if acc is None else acc + p
        o_ref[...] = acc.astype(o_ref.dtype)

    return pl.pallas_call(
        body, name=name, grid=(R // tr, M // tm),
        in_specs=[pl.BlockSpec((tm, J * Ns), lambda r, m: (m, 0)),
                  pl.BlockSpec((J, None, tr, Ns), lambda r, m: (0, layer, r, 0))],
        out_specs=pl.BlockSpec((tm, tr), lambda r, m: (m, r)),
        out_shape=_sds((M, R), out_dtype),
        compiler_params=_params("parallel", "parallel"),
    )(dy, w)


def mm_nt_norm_bwd(dy, w, x, g, dres, name, layer=0, tm=512):
    M = dy.shape[0]
    J, _, Dm, Ns = w.shape
    dims = (((1,), (1,)), ((), ()))

    def body(dy_ref, w_ref, x_ref, g_ref, r_ref, dx_ref, dxb_ref, dg_ref):
        dxn = None
        for j in range(J):
            p = lax.dot_general(dy_ref[:, j * Ns:(j + 1) * Ns], w_ref[j], dims, preferred_element_type=F32)
            dxn = p if dxn is None else dxn + p
        xv = x_ref[...]
        r = lax.rsqrt(jnp.mean(xv * xv, axis=-1, keepdims=True) + EPS)
        gx = dxn * g_ref[...]
        dot = jnp.sum(gx * xv, axis=-1, keepdims=True)
        dx = r_ref[...] + r * gx - xv * ((r * r * r) * (dot * (1.0 / Dm)))
        dx_ref[...] = dx
        dxb_ref[...] = dx.astype(BF16)
        part = jnp.sum(dxn * (xv * r), axis=0, keepdims=True)

        @pl.when(pl.program_id(0) == 0)
        def _():
            dg_ref[...] = part

        @pl.when(pl.program_id(0) > 0)
        def _():
            dg_ref[...] += part

    row = pl.BlockSpec((tm, Dm), lambda m: (m, 0))
    vec = pl.BlockSpec((1, Dm), lambda m: (0, 0))
    return pl.pallas_call(
        body, name=name, grid=(M // tm,),
        in_specs=[pl.BlockSpec((tm, J * Ns), lambda m: (m, 0)),
                  pl.BlockSpec((J, None, Dm, Ns), lambda m: (0, layer, 0, 0), pipeline_mode=pl.Buffered(1)), row, vec, row],
        out_specs=[row, row, vec],
        out_shape=[_sds((M, Dm), F32), _sds((M, Dm), BF16), _sds((1, Dm), F32)],
        compiler_params=_params("arbitrary"),
    )(dy, w, x, g, dres)


def mm_tn(a, dy, name, J, tk, tm=512, jb=None):
    M, K = a.shape
    jb = jb or J
    Ns = dy.shape[1] // J
    N = jb * Ns
    n_m = M // tm
    dims = (((0,), (0,)), ((), ()))

    def body(a_ref, dy_ref, o_ref, acc_ref):
        p = lax.dot_general(a_ref[...], dy_ref[...], dims, preferred_element_type=F32)
        m = pl.program_id(2)

        @pl.when(m == 0)
        def _():
            acc_ref[...] = p

        @pl.when(m > 0)
        def _():
            acc_ref[...] += p

        @pl.when(m == n_m - 1)
        def _():
            for j in range(jb):
                o_ref[j] = acc_ref[:, j * Ns:(j + 1) * Ns].astype(BF16)

    return pl.pallas_call(
        body, name=name, grid=(J // jb, K // tk, n_m),
        in_specs=[pl.BlockSpec((tm, tk), lambda g, k, m: (m, k)), pl.BlockSpec((tm, N), lambda g, k, m: (m, g))],
        out_specs=pl.BlockSpec((jb, tk, Ns), lambda g, k, m: (g, k, 0)),
        out_shape=_sds((J, K, Ns), BF16), scratch_shapes=[pltpu.VMEM((tk, N), F32)],
        compiler_params=_params("parallel", "parallel", "arbitrary"),
    )(a, dy)


HALO = 16


def _shift_down(x, s):
    return pltpu.roll(x, s, 0)


def _shift_up(x, s):
    return pltpu.roll(x, x.shape[0] - s, 0)


def _conv3(z, cw):
    return (_shift_down(z, 2) * cw[0:1] + _shift_down(z, 1) * cw[1:2]) + z * cw[2:3]


def _window_count(first_row, n, k):
    t = first_row + lax.broadcasted_iota(jnp.int32, (n, 1), 0)
    return jnp.clip(t + 1, 1, k).astype(F32)


def in_mixer_fwd(xn, w_in, conv_w, pool_w, pool_scale, name, ts=512):
    S, K = xn.shape
    n = ts + HALO

    def body(xm_ref, xb_ref, w_ref, cw_ref, pw_ref, ps_ref, p_ref, o_ref):
        i = pl.program_id(0)
        before = jnp.where(i > 0, xb_ref[...], jnp.zeros_like(xb_ref))
        rows = jnp.concatenate([before, xm_ref[...]], axis=0)
        h, gb, gc, pin = [jnp.dot(rows, w_ref[j], preferred_element_type=F32) for j in range(N_CHIPS)]
        for j, part in enumerate((h, gb, gc, pin)):
            p_ref[:, j * A_WIDTH:(j + 1) * A_WIDTH] = part[HALO:]
        cz = _conv3(gc * h, cw_ref[...])
        o_ref[:, 0:A_WIDTH] = (gb[HALO:] * cz[HALO:]).astype(BF16)
        for g, k in enumerate(POOL_WINDOWS):
            p = pin[:, g * POOL_GROUP:(g + 1) * POOL_GROUP]
            w = p
            s = 1
            while s < k:
                w = w + _shift_down(w, s)
                s *= 2
            pooled = w / _window_count(i * ts - HALO, n, k) - p
            yb = jnp.dot(pooled[HALO:].astype(BF16), pw_ref[g], preferred_element_type=F32)
            yb = yb * ps_ref[:, g * POOL_GROUP:(g + 1) * POOL_GROUP]
            o_ref[:, A_WIDTH + g * POOL_GROUP:A_WIDTH + (g + 1) * POOL_GROUP] = yb.astype(BF16)

    hb = ts // HALO
    return pl.pallas_call(
        body, name=name, grid=(S // ts,),
        in_specs=[
            pl.BlockSpec((ts, K), lambda i: (i, 0)),
            pl.BlockSpec((HALO, K), lambda i: (jnp.maximum(i * hb - 1, 0), 0)),
            pl.BlockSpec((N_CHIPS, None, K, A_WIDTH), lambda i: (0, 0, 0, 0), pipeline_mode=pl.Buffered(1)),
            pl.BlockSpec((3, A_WIDTH), lambda i: (0, 0)),
            pl.BlockSpec((4, POOL_GROUP, POOL_GROUP), lambda i: (0, 0, 0)),
            pl.BlockSpec((1, 4 * POOL_GROUP), lambda i: (0, 0)),
        ],
        out_specs=[pl.BlockSpec((ts, EVEN_IN), lambda i: (i, 0)), pl.BlockSpec((ts, D_MODEL), lambda i: (i, 0))],
        out_shape=[_sds((S, EVEN_IN), F32), _sds((S, D_MODEL), BF16)], compiler_params=_params("parallel"),
    )(xn, xn, w_in, conv_w, pool_w, pool_scale)


def mixer_bwd(proj, dmix, conv_w, pool_w, pool_scale, name, ts=256):
    S = proj.shape[0]
    n = ts + 2 * HALO
    nt = S // ts
    tn_dims = (((0,), (0,)), ((), ()))
    nt_dims = (((1,), (1,)), ((), ()))

    def body(pm_ref, pb_ref, pa_ref, dm_ref, da_ref, cw_ref, pw_ref, ps_ref, o_ref, dcw_ref, dpw_ref, dps_ref):
        i = pl.program_id(0)
        last = i == nt - 1
        before = jnp.where(i > 0, pb_ref[...], 0.0)
        after = jnp.where(last, 0.0, pa_ref[...])
        ext = jnp.concatenate([before, pm_ref[...], after], axis=0)
        dafter = jnp.where(last, 0.0, da_ref[...])
        dext = jnp.concatenate([jnp.zeros((HALO, D_MODEL), F32), dm_ref[...], dafter], axis=0)
        cw = cw_ref[...]
        main = slice(HALO, HALO + ts)

        @pl.when(i == 0)
        def _():
            dcw_ref[...] = jnp.zeros_like(dcw_ref)
            dpw_ref[...] = jnp.zeros_like(dpw_ref)
            dps_ref[...] = jnp.zeros_like(dps_ref)

        h, gb, gc = ext[:, 0:A_WIDTH], ext[:, A_WIDTH:2 * A_WIDTH], ext[:, 2 * A_WIDTH:3 * A_WIDTH]
        z = gc * h
        z1, z2 = _shift_down(z, 1), _shift_down(z, 2)
        cz = (z2 * cw[0:1] + z1 * cw[1:2]) + z * cw[2:3]
        dya = dext[:, 0:A_WIDTH]
        dcz = dya * gb
        dz = dcz * cw[2:3] + _shift_up(dcz, 1) * cw[1:2] + _shift_up(dcz, 2) * cw[0:1]
        o_ref[:, 0:A_WIDTH] = (dz * gc)[main].astype(BF16)
        o_ref[:, A_WIDTH:2 * A_WIDTH] = (dya * cz)[main].astype(BF16)
        o_ref[:, 2 * A_WIDTH:3 * A_WIDTH] = (dz * h)[main].astype(BF16)
        dczm = dcz[main]
        dcw_ref[0:1, :] += jnp.sum(dczm * z2[main], axis=0, keepdims=True)
        dcw_ref[1:2, :] += jnp.sum(dczm * z1[main], axis=0, keepdims=True)
        dcw_ref[2:3, :] += jnp.sum(dczm * z[main], axis=0, keepdims=True)

        for g, k in enumerate(POOL_WINDOWS):
            lo = 3 * A_WIDTH + g * POOL_GROUP
            cols = slice(g * POOL_GROUP, (g + 1) * POOL_GROUP)
            p = ext[:, lo:lo + POOL_GROUP]
            w = p
            s = 1
            while s < k:
                w = w + _shift_down(w, s)
                s *= 2
            cnt = _window_count(i * ts - HALO, n, k)
            pooled = (w / cnt - p)[main].astype(BF16)
            dyb = dext[:, A_WIDTH + g * POOL_GROUP:A_WIDTH + (g + 1) * POOL_GROUP]
            e = dyb * ps_ref[:, cols]
            pre = jnp.dot(pooled, pw_ref[g], preferred_element_type=F32)
            dps_ref[:, cols] += jnp.sum(dyb[main] * pre, axis=0, keepdims=True)
            dpw_ref[g] += lax.dot_general(pooled, e[main].astype(BF16), tn_dims, preferred_element_type=F32)
            dpooled = lax.dot_general(e.astype(BF16), pw_ref[g], nt_dims, preferred_element_type=F32)
            q = dpooled / cnt
            a = q
            s = 1
            while s < k:
                a = a + _shift_up(a, s)
                s *= 2
            o_ref[:, lo:lo + POOL_GROUP] = (a - dpooled)[main].astype(BF16)

    hb = ts // HALO
    nh = S // HALO
    before_map = lambda i: (jnp.maximum(i * hb - 1, 0), 0)
    after_map = lambda i: (jnp.minimum((i + 1) * hb, nh - 1), 0)
    full = lambda *shape: pl.BlockSpec(shape, lambda i: (0,) * len(shape))
    return pl.pallas_call(
        body, name=name, grid=(nt,),
        in_specs=[
            pl.BlockSpec((ts, EVEN_IN), lambda i: (i, 0)),
            pl.BlockSpec((HALO, EVEN_IN), before_map),
            pl.BlockSpec((HALO, EVEN_IN), after_map),
            pl.BlockSpec((ts, D_MODEL), lambda i: (i, 0)),
            pl.BlockSpec((HALO, D_MODEL), after_map),
            full(3, A_WIDTH), full(4, POOL_GROUP, POOL_GROUP), full(1, 4 * POOL_GROUP),
        ],
        out_specs=[pl.BlockSpec((ts, EVEN_IN), lambda i: (i, 0)), full(3, A_WIDTH), full(4, POOL_GROUP, POOL_GROUP),
                   full(1, 4 * POOL_GROUP)],
        out_shape=[_sds((S, EVEN_IN), BF16), _sds((3, A_WIDTH), F32), _sds((4, POOL_GROUP, POOL_GROUP), F32),
                   _sds((1, 4 * POOL_GROUP), F32)],
        compiler_params=_params("arbitrary"),
    )(proj, proj, proj, dmix, dmix, conv_w, pool_w, pool_scale)


FFN_HALO = 16
FFN_TC = 1408


GLU_CHUNKS = ((0, 512), (512, 512), (1024, 384))


def up_glu_fwd(xn, w_up, conv_w, conv_b, name, tm=512):
    S, K = xn.shape
    nc = D_FF // FFN_TC

    def body(xm_ref, xb_ref, wg_ref, wu_ref, cwg_ref, cwu_ref, cbg_ref, cbu_ref, pg_ref, pu_ref, ug_ref, uu_ref, o_ref):
        before = jnp.where(pl.program_id(1) > 0, xb_ref[...], jnp.zeros_like(xb_ref))
        rows = jnp.concatenate([before, xm_ref[...]], axis=0)
        for lo, width in GLU_CHUNKS:
            cols = slice(lo, lo + width)
            pre_g = jnp.dot(rows, wg_ref[:, cols], preferred_element_type=F32)
            pre_u = jnp.dot(rows, wu_ref[:, cols], preferred_element_type=F32)
            gate = _conv3(pre_g, cwg_ref[:, cols])[FFN_HALO:] + cbg_ref[:, cols]
            upv = _conv3(pre_u, cwu_ref[:, cols])[FFN_HALO:] + cbu_ref[:, cols]
            pg_ref[:, cols] = pre_g[FFN_HALO:].astype(BF16)
            pu_ref[:, cols] = pre_u[FFN_HALO:].astype(BF16)
            ug_ref[:, cols] = gate.astype(BF16)
            uu_ref[:, cols] = upv.astype(BF16)
            o_ref[:, cols] = ((gate * (1.0 / (1.0 + jnp.exp(-gate)))) * upv).astype(BF16)

    hb = tm // FFN_HALO
    wspec = lambda off: pl.BlockSpec((None, None, K, FFN_TC), lambda j, m: (j + off, 0, 0, 0))
    cw = lambda off: pl.BlockSpec((3, FFN_TC), lambda j, m: (0, j + off))
    cb = lambda off: pl.BlockSpec((1, FFN_TC), lambda j, m: (0, j + off))
    out = pl.BlockSpec((tm, FFN_TC), lambda j, m: (m, j))
    pg, pu, ug, uu, act = pl.pallas_call(
        body, name=name, grid=(nc, S // tm),
        in_specs=[pl.BlockSpec((tm, K), lambda j, m: (m, 0)),
                  pl.BlockSpec((FFN_HALO, K), lambda j, m: (jnp.maximum(m * hb - 1, 0), 0)),
                  wspec(0), wspec(nc), cw(0), cw(nc), cb(0), cb(nc)],
        out_specs=[out] * 5, out_shape=[_sds((S, D_FF), BF16)] * 5,
        compiler_params=_params("parallel", "parallel"),
    )(xn, xn, w_up, w_up, conv_w, conv_w, conv_b, conv_b)
    return (pg, pu), (ug, uu), act


def glu_bwd(up, u, dy, w_down, conv_w, name, ts=256):
    S = up[0].shape[0]
    nc = D_FF // FFN_TC
    nt = S // ts
    W = 2 * D_FF
    Dm = dy.shape[1]

    def body(xg_ref, xu_ref, gm_ref, ga_ref, um_ref, ua_ref, dym_ref, dya_ref, wd_ref, cw_ref, dx_ref, dcw_ref, dcb_ref):
        i = pl.program_id(0)
        last = i == nt - 1

        @pl.when(i == 0)
        def _():
            dcw_ref[...] = jnp.zeros_like(dcw_ref)
            dcb_ref[...] = jnp.zeros_like(dcb_ref)

        def rows(m_ref, a_ref, cols):
            return jnp.concatenate([m_ref[:, cols], a_ref[:, cols]], axis=0).astype(F32)

        def back(d, x, cols):
            cw = cw_ref[:, cols]
            d1, d2 = _shift_up(d, 1), _shift_up(d, 2)
            dx_ref[:, cols] = ((d * cw[2:3] + d1 * cw[1:2]) + d2 * cw[0:1])[:ts].astype(BF16)
            dcb_ref[:, cols] += jnp.sum(d[:ts], axis=0, keepdims=True)
            dcw_ref[0:1, cols] += jnp.sum(d2[:ts] * x, axis=0, keepdims=True)
            dcw_ref[1:2, cols] += jnp.sum(d1[:ts] * x, axis=0, keepdims=True)
            dcw_ref[2:3, cols] += jnp.sum(d[:ts] * x, axis=0, keepdims=True)

        dy_rows = jnp.concatenate([dym_ref[...], jnp.where(last, jnp.zeros_like(dya_ref), dya_ref[...])], axis=0)
        for c in range(nc):
            for lo, width in GLU_CHUNKS:
                cols = slice(c * FFN_TC + lo, c * FFN_TC + lo + width)
                dae = lax.dot_general(dy_rows, wd_ref[cols, :], _NT, preferred_element_type=F32)
                ug, uu = rows(gm_ref, ga_ref, cols), rows(um_ref, ua_ref, cols)
                sg = 1.0 / (1.0 + jnp.exp(-ug))
                duu = dae * (ug * sg)
                dug = (dae * uu) * (sg * (1.0 + ug * (1.0 - sg)))
                back(dug, xg_ref[:, cols].astype(F32), cols)
                back(duu, xu_ref[:, cols].astype(F32), slice(D_FF + cols.start, D_FF + cols.stop))

    hb = ts // FFN_HALO
    nh = S // FFN_HALO
    after_map = lambda i: (jnp.minimum((i + 1) * hb, nh - 1), 0)
    main = pl.BlockSpec((ts, D_FF), lambda i: (i, 0))
    after = pl.BlockSpec((FFN_HALO, D_FF), after_map)
    return pl.pallas_call(
        body, name=name, grid=(nt,),
        in_specs=[main, main, main, after, main, after,
                  pl.BlockSpec((ts, Dm), lambda i: (i, 0)), pl.BlockSpec((FFN_HALO, Dm), after_map),
                  pl.BlockSpec((None, None, D_FF, Dm), lambda i: (0, 0, 0, 0), pipeline_mode=pl.Buffered(1)),
                  pl.BlockSpec((3, W), lambda i: (0, 0))],
        out_specs=[pl.BlockSpec((ts, W), lambda i: (i, 0)), pl.BlockSpec((3, W), lambda i: (0, 0)),
                   pl.BlockSpec((1, W), lambda i: (0, 0))],
        out_shape=[_sds((S, W), BF16), _sds((3, W), F32), _sds((1, W), F32)],
        compiler_params=_params("arbitrary"),
    )(up[0], up[1], u[0], u[0], u[1], u[1], dy, dy, w_down, conv_w)


MEAN_GROUP = 256


def _head_mean_matrix():
    h = np.arange(MEAN_GROUP) // HEAD_DIM
    return jnp.asarray((h[:, None] == h[None, :]).astype(np.float32) / HEAD_DIM, dtype=BF16)


def _head_mean(v, gm):
    vb = v.astype(BF16)
    return jnp.concatenate([jnp.dot(vb[:, c:c + MEAN_GROUP], gm, preferred_element_type=F32)
                            for c in range(0, v.shape[1], MEAN_GROUP)], axis=1)


def qkv_qknorm_fwd(xn, w_qkv, gqk, name, tm=1024):
    S, K = xn.shape
    J, _, _, Ns = w_qkv.shape
    gains = gqk.reshape(1, 3 * D_MODEL)

    def body(x_ref, w_ref, g_ref, gm_ref, raw_ref, o_ref):
        first_col = pl.program_id(0) * Ns
        acc = jnp.dot(x_ref[...], w_ref[...], preferred_element_type=F32)
        raw_ref[...] = acc
        gm = gm_ref[...]
        for c in range(0, Ns, MEAN_GROUP):
            cols = slice(c, c + MEAN_GROUP)
            x = acc[:, cols]
            mean = jnp.dot((x * x).astype(BF16), gm, preferred_element_type=F32)
            normed = (x * lax.rsqrt(mean + EPS)) * g_ref[:, cols]
            o_ref[:, cols] = jnp.where(first_col + c >= 2 * D_MODEL, x, normed).astype(BF16)

    return pl.pallas_call(
        body, name=name, grid=(J, S // tm),
        in_specs=[pl.BlockSpec((tm, K), lambda j, m: (m, 0)), pl.BlockSpec((None, None, K, Ns), lambda j, m: (j, 0, 0, 0)),
                  pl.BlockSpec((1, Ns), lambda j, m: (0, j)), pl.BlockSpec((MEAN_GROUP, MEAN_GROUP), lambda j, m: (0, 0))],
        out_specs=[pl.BlockSpec((tm, Ns), lambda j, m: (m, j))] * 2,
        out_shape=[_sds((S, J * Ns), F32), _sds((S, J * Ns), BF16)], compiler_params=_params("parallel", "parallel"),
    )(xn, w_qkv, gains, _head_mean_matrix())


def qknorm_bwd(qkv, dq, dk, dv, gqk, name, ts=256):
    S = qkv.shape[0]

    def body(x_ref, dq_ref, dk_ref, dv_ref, g_ref, gm_ref, o_ref, dg_ref):
        @pl.when(pl.program_id(0) == 0)
        def _():
            dg_ref[...] = jnp.zeros_like(dg_ref)

        gm = gm_ref[...]
        for part, d_ref in enumerate((dq_ref, dk_ref)):
            cols = slice(part * D_MODEL, (part + 1) * D_MODEL)
            x = x_ref[:, cols]
            d = d_ref[...]
            r = lax.rsqrt(_head_mean(x * x, gm) + EPS)
            gx = d * g_ref[part]
            o_ref[:, cols] = (r * gx - x * ((r * r * r) * _head_mean(gx * x, gm))).astype(BF16)
            dg_ref[part] += jnp.sum(d * (x * r), axis=0, keepdims=True)
        o_ref[:, 2 * D_MODEL:] = dv_ref[...].astype(BF16)

    row = pl.BlockSpec((ts, D_MODEL), lambda i: (i, 0))
    wide = pl.BlockSpec((ts, 3 * D_MODEL), lambda i: (i, 0))
    gains = pl.BlockSpec((3, 1, D_MODEL), lambda i: (0, 0, 0))
    return pl.pallas_call(
        body, name=name, grid=(S // ts,),
        in_specs=[wide, row, row, row, gains, pl.BlockSpec((MEAN_GROUP, MEAN_GROUP), lambda i: (0, 0))],
        out_specs=[wide, gains],
        out_shape=[_sds((S, 3 * D_MODEL), BF16), _sds((3, 1, D_MODEL), F32)],
        compiler_params=_params("arbitrary"),
    )(qkv, dq, dk, dv, gqk, _head_mean_matrix())


RESIDUES = 16


def _block_order(dil):
    runs = RESIDUES // dil
    slot = np.arange(ATT_BLOCK)
    return (slot % (ATT_BLOCK // runs)) * runs + slot // (ATT_BLOCK // runs)


def _bucket_tables():
    n = ATT_BLOCK
    max_exact = N_REL_BUCKETS // 2
    buckets, valids = [], []
    for _, dil in DILATED_PAIRS:
        order = _block_order(dil)
        a = order[:, None]
        c = np.concatenate([order, n + order])[None, :]
        first_half = (np.arange(2 * n) < n)[None, :]
        rel = a + n - c
        band = (rel >= 0) & (rel <= n)
        dist = np.clip(rel, 0, n) * dil
        dd = np.maximum(dist, 1).astype(np.float32)
        large = max_exact + (np.log(dd / np.float32(max_exact)) / np.float32(math.log(REL_MAX_DISTANCE / max_exact))
                             * np.float32(N_REL_BUCKETS - max_exact)).astype(np.int32)
        large = np.minimum(large, N_REL_BUCKETS - 1)
        buckets.append(np.where(dist < max_exact, dist, large).reshape(1, -1))
        valids.append(np.stack([(band & ~first_half).reshape(1, -1), band.reshape(1, -1)]))
    return np.stack(buckets).astype(np.int32), np.stack(valids).astype(np.int32)


BIAS_CHUNK = 8192


def _split3(x):
    a = x.astype(BF16)
    r = x - a.astype(F32)
    b = r.astype(BF16)
    c = (r - b.astype(F32)).astype(BF16)
    return a, b, c


def bias_expand(rel_bias_t, name):
    bucket, valid = _bucket_tables()
    nq = bucket.shape[-1]

    def body(t_ref, b_ref, v_ref, o_ref):
        onehot = (lax.broadcasted_iota(jnp.int32, (N_REL_BUCKETS, BIAS_CHUNK), 0) == b_ref[...]).astype(BF16)
        acc = None
        for term in _split3(t_ref[...]):
            p = jnp.dot(term, onehot, preferred_element_type=F32)
            acc = p if acc is None else acc + p
        o_ref[...] = jnp.where(v_ref[...] > 0, acc, MASK_VALUE)

    return pl.pallas_call(
        body, name=name, grid=(3, 2, nq // BIAS_CHUNK),
        in_specs=[pl.BlockSpec((N_HEADS, N_REL_BUCKETS), lambda b, v, c: (0, 0)),
                  pl.BlockSpec((None, 1, BIAS_CHUNK), lambda b, v, c: (b, 0, c)),
                  pl.BlockSpec((None, None, 1, BIAS_CHUNK), lambda b, v, c: (b, v, 0, c))],
        out_specs=pl.BlockSpec((None, None, N_HEADS, BIAS_CHUNK), lambda b, v, c: (b, v, 0, c)),
        out_shape=_sds((3, 2, N_HEADS, nq), F32), compiler_params=_params("parallel", "parallel", "parallel"),
    )(rel_bias_t, jnp.asarray(bucket), jnp.asarray(valid))


def bias_reduce(dbias, name):
    bucket, _ = _bucket_tables()
    nq = bucket.shape[-1]
    dims = (((1,), (1,)), ((), ()))

    def body(d_ref, b_ref, o_ref):
        onehot = (lax.broadcasted_iota(jnp.int32, (N_REL_BUCKETS, BIAS_CHUNK), 0) == b_ref[...]).astype(BF16)
        acc = None
        for term in _split3(d_ref[...]):
            p = lax.dot_general(term, onehot, dims, preferred_element_type=F32)
            acc = p if acc is None else acc + p

        @pl.when(pl.program_id(1) == 0)
        def _():
            o_ref[...] = acc

        @pl.when(pl.program_id(1) > 0)
        def _():
            o_ref[...] += acc

    return pl.pallas_call(
        body, name=name, grid=(3, nq // BIAS_CHUNK),
        in_specs=[pl.BlockSpec((None, N_HEADS, BIAS_CHUNK), lambda b, c: (b, 0, c)),
                  pl.BlockSpec((None, 1, BIAS_CHUNK), lambda b, c: (b, 0, c))],
        out_specs=pl.BlockSpec((None, N_HEADS, N_REL_BUCKETS), lambda b, c: (b, 0, 0)),
        out_shape=_sds((3, N_HEADS, N_REL_BUCKETS), F32), compiler_params=_params("parallel", "arbitrary"),
    )(dbias, jnp.asarray(bucket))


PAIR = 2 * HEAD_DIM
N_PAIRS = N_HEADS // 2
_NT = (((1,), (1,)), ((), ()))
_TN = (((0,), (0,)), ((), ()))


def _low_lanes(shape):
    return lax.broadcasted_iota(jnp.int32, shape, 1) < HEAD_DIM


ATTN_VMEM_LIMIT_BYTES = 56 * 1024 * 1024
BRANCH_ORDER = (2, 1, 0)


def _regroup(dst, src, L16):
    for r in range(RESIDUES):
        dst[pl.ds(r * L16, L16), :] = src[pl.ds(r, L16, stride=RESIDUES), :]


def _ungroup(dst, src, L16):
    for r in range(RESIDUES):
        dst[pl.ds(r, L16, stride=RESIDUES), :] = src[pl.ds(r * L16, L16), :]


def _branch_geometry(branch, S):
    dil = DILATED_PAIRS[branch][1]
    runs = RESIDUES // dil
    return dil, runs, ATT_BLOCK // runs, S // dil // ATT_BLOCK


def _block_rows(it, branch, S):
    dil, runs, run_len, n_blocks = _branch_geometry(branch, S)
    L16 = S // RESIDUES
    r, b = it // n_blocks, it % n_blocks
    prev = jnp.maximum(b - 1, 0)
    cur_rows = [pl.multiple_of((j * dil + r) * L16 + run_len * b, 8) for j in range(runs)]
    prev_rows = [pl.multiple_of((j * dil + r) * L16 + run_len * prev, 8) for j in range(runs)]
    return cur_rows, prev_rows, jnp.minimum(b, 1)


def _load_block(ref, rows, run_len):
    parts = [ref[pl.ds(o, run_len), :] for o in rows]
    return parts[0] if len(parts) == 1 else jnp.concatenate(parts, axis=0)


def _store_block(ref, rows, run_len, value, add=False):
    for j, o in enumerate(rows):
        part = value[j * run_len:(j + 1) * run_len]
        if add:
            ref[pl.ds(o, run_len), :] += part
        else:
            ref[pl.ds(o, run_len), :] = part


ATTN_FWD_UNROLL = 8
ATTN_BWD_UNROLL = 4


def _stack_heads(x, low):
    zero = jnp.zeros_like(x)
    return jnp.concatenate([jnp.where(low, x, zero), jnp.where(low, zero, x)], axis=0)


def _unstack_heads(y, low):
    return jnp.where(low, y[:ATT_BLOCK], y[ATT_BLOCK:])


def attn_fwd(qkvn, bias, name):
    S = qkvn.shape[0]
    L16 = S // RESIDUES
    n_iter = S // ATT_BLOCK

    def body(q_ref, k_ref, v_ref, b_ref, o_ref, lse_ref, stage, qp, kp, vp, acc_s, m_s, l_s):
        for src, dst in ((q_ref, qp), (k_ref, kp), (v_ref, vp)):
            stage[...] = src[...].astype(F32)
            _regroup(dst, stage, L16)
        low = _low_lanes((ATT_BLOCK, PAIR))

        for branch in BRANCH_ORDER:
            _, _, run_len, _ = _branch_geometry(branch, S)
            first = branch == BRANCH_ORDER[0]

            def step(it, carry, branch=branch, run_len=run_len, first=first):
                cur, prev, variant = _block_rows(it, branch, S)
                q = _load_block(qp, cur, run_len).astype(BF16)
                k = jnp.concatenate([_load_block(kp, prev, run_len), _load_block(kp, cur, run_len)], axis=0).astype(BF16)
                v = jnp.concatenate([_load_block(vp, prev, run_len), _load_block(vp, cur, run_len)], axis=0).astype(BF16)
                s = lax.dot_general(_stack_heads(q, low), k, _NT, preferred_element_type=F32) * (HEAD_DIM ** -0.5)
                s = s + b_ref[2 * branch + variant].reshape(2 * ATT_BLOCK, 2 * ATT_BLOCK)
                mx = jnp.max(s, axis=-1, keepdims=True)
                p = jnp.exp(s - mx)
                den = jnp.sum(p, axis=-1, keepdims=True)
                pv = jnp.dot(p.astype(BF16), v, preferred_element_type=F32)
                acc = _unstack_heads(pv, low)
                m = _unstack_heads(mx, low)
                l = _unstack_heads(den, low)
                if not first:
                    m_old = _load_block(m_s, cur, run_len)
                    m_new = jnp.maximum(m_old, m)
                    a_old, a_new = jnp.exp(m_old - m_new), jnp.exp(m - m_new)
                    acc = _load_block(acc_s, cur, run_len) * a_old + acc * a_new
                    l = _load_block(l_s, cur, run_len) * a_old + l * a_new
                    m = m_new
                _store_block(acc_s, cur, run_len, acc)
                _store_block(m_s, cur, run_len, m)
                _store_block(l_s, cur, run_len, l)
                return carry

            lax.fori_loop(0, n_iter, step, 0, unroll=ATTN_FWD_UNROLL)

        acc_s[...] = acc_s[...] / l_s[...]
        _ungroup(stage, acc_s, L16)
        o_ref[...] = stage[...].astype(BF16)
        m_s[...] = m_s[...] + jnp.log(l_s[...])
        _ungroup(lse_ref, m_s, L16)

    col = lambda part: pl.BlockSpec((S, PAIR), lambda hp: (0, part * N_PAIRS + hp))
    out = pl.BlockSpec((S, PAIR), lambda hp: (0, hp))
    return pl.pallas_call(
        body, name=name, grid=(N_PAIRS,),
        in_specs=[col(0), col(1), col(2), pl.BlockSpec((6, 2, ATT_BLOCK, 2 * ATT_BLOCK), lambda hp: (0, hp, 0, 0))],
        out_specs=[out, out], out_shape=[_sds((S, D_MODEL), BF16), _sds((S, D_MODEL), F32)],
        scratch_shapes=[pltpu.VMEM((S, PAIR), F32)] * 7,
        compiler_params=pltpu.CompilerParams(dimension_semantics=("parallel",), vmem_limit_bytes=ATTN_VMEM_LIMIT_BYTES),
    )(qkvn, qkvn, qkvn, bias)


def attn_bwd(qkvn, att, datt, lse, bias, name):
    S = qkvn.shape[0]
    L16 = S // RESIDUES
    n_iter = S // ATT_BLOCK
    TILE = 512

    def body(q_ref, k_ref, v_ref, o_ref, do_ref, lse_ref, b_ref, dq_ref, dk_ref, dv_ref, db_ref,
             qp, kp, vp, dop, ldp, dqp, dkp, dvp):
        stage = dqp
        for src, dst in ((q_ref, qp), (k_ref, kp), (v_ref, vp), (do_ref, dop)):
            stage[...] = src[...].astype(F32)
            _regroup(dst, stage, L16)

        def pack(i, carry):
            rows = pl.ds(pl.multiple_of(i * TILE, TILE), TILE)
            low = _low_lanes((TILE, PAIR))
            lane = lax.broadcasted_iota(jnp.int32, (TILE, PAIR), 1)
            prod = do_ref[rows, :].astype(F32) * o_ref[rows, :].astype(F32)
            d0 = jnp.sum(jnp.where(low, prod, 0.0), axis=-1, keepdims=True)
            d1 = jnp.sum(jnp.where(low, 0.0, prod), axis=-1, keepdims=True)
            stage[rows, :] = jnp.where((lane & (HEAD_DIM // 2)) == 0, lse_ref[rows, :], jnp.where(low, d0, d1))
            return carry

        lax.fori_loop(0, S // TILE, pack, 0)
        _regroup(ldp, stage, L16)
        dqp[...] = jnp.zeros_like(dqp)
        dkp[...] = jnp.zeros_like(dkp)
        dvp[...] = jnp.zeros_like(dvp)
        db_ref[...] = jnp.zeros_like(db_ref)
        low = _low_lanes((ATT_BLOCK, PAIR))

        for branch in BRANCH_ORDER:
            _, _, run_len, _ = _branch_geometry(branch, S)

            def step(it, carry, branch=branch, run_len=run_len):
                cur, prev, variant = _block_rows(it, branch, S)
                q = _load_block(qp, cur, run_len).astype(BF16)
                dout = _load_block(dop, cur, run_len).astype(BF16)
                ld = _load_block(ldp, cur, run_len)
                k = jnp.concatenate([_load_block(kp, prev, run_len), _load_block(kp, cur, run_len)], axis=0).astype(BF16)
                v = jnp.concatenate([_load_block(vp, prev, run_len), _load_block(vp, cur, run_len)], axis=0).astype(BF16)
                half = HEAD_DIM // 2
                lse2 = jnp.concatenate([ld[:, 0:1], ld[:, HEAD_DIM:HEAD_DIM + 1]], axis=0)
                delta2 = jnp.concatenate([ld[:, half:half + 1], ld[:, HEAD_DIM + half:HEAD_DIM + half + 1]], axis=0)
                q2, do2 = _stack_heads(q, low), _stack_heads(dout, low)
                s = lax.dot_general(q2, k, _NT, preferred_element_type=F32) * (HEAD_DIM ** -0.5)
                p = jnp.exp(s + b_ref[2 * branch + variant].reshape(2 * ATT_BLOCK, 2 * ATT_BLOCK) - lse2)
                dp = lax.dot_general(do2, v, _NT, preferred_element_type=F32)
                ds = p * (dp - delta2)
                db_ref[branch] += ds.reshape(2, ATT_BLOCK, 2 * ATT_BLOCK)
                dsb = (ds * (HEAD_DIM ** -0.5)).astype(BF16)
                dq = _unstack_heads(jnp.dot(dsb, k, preferred_element_type=F32), low)
                dk = lax.dot_general(dsb, q2, _TN, preferred_element_type=F32)
                dv = lax.dot_general(p.astype(BF16), do2, _TN, preferred_element_type=F32)
                _store_block(dqp, cur, run_len, dq, add=True)
                _store_block(dkp, prev, run_len, dk[:ATT_BLOCK], add=True)
                _store_block(dvp, prev, run_len, dv[:ATT_BLOCK], add=True)
                _store_block(dkp, cur, run_len, dk[ATT_BLOCK:], add=True)
                _store_block(dvp, cur, run_len, dv[ATT_BLOCK:], add=True)
                return carry

            lax.fori_loop(0, n_iter, step, 0, unroll=ATTN_BWD_UNROLL)

        _ungroup(dq_ref, dqp, L16)
        _ungroup(dk_ref, dkp, L16)
        _ungroup(dv_ref, dvp, L16)

    col = lambda part: pl.BlockSpec((S, PAIR), lambda hp: (0, part * N_PAIRS + hp))
    one = pl.BlockSpec((S, PAIR), lambda hp: (0, hp))
    return pl.pallas_call(
        body, name=name, grid=(N_PAIRS,),
        in_specs=[col(0), col(1), col(2), one, one, one,
                  pl.BlockSpec((6, 2, ATT_BLOCK, 2 * ATT_BLOCK), lambda hp: (0, hp, 0, 0))],
        out_specs=[one, one, one, pl.BlockSpec((3, 2, ATT_BLOCK, 2 * ATT_BLOCK), lambda hp: (0, hp, 0, 0))],
        out_shape=[_sds((S, D_MODEL), F32)] * 3 + [_sds((3, N_HEADS, ATT_BLOCK, 2 * ATT_BLOCK), F32)],
        scratch_shapes=[pltpu.VMEM((S, PAIR), F32)] * 8,
        compiler_params=pltpu.CompilerParams(dimension_semantics=("parallel",), vmem_limit_bytes=ATTN_VMEM_LIMIT_BYTES),
    )(qkvn, qkvn, qkvn, att, datt, lse, bias)


def adamw(w, g, m, v, name):
    n, R, C = w.shape

    def body(w_ref, g_ref, m_ref, v_ref, d_ref, nm_ref, nv_ref, go_ref):
        gv = g_ref[...]
        go_ref[...] = gv
        m2 = ADAM_B1 * m_ref[...] + (1.0 - ADAM_B1) * gv
        v2 = ADAM_B2 * v_ref[...] + (1.0 - ADAM_B2) * (gv * gv)
        m_hat = m2 / (1.0 - ADAM_B1 ** ADAM_STEP)
        v_hat = v2 / (1.0 - ADAM_B2 ** ADAM_STEP)
        d_ref[...] = -ADAM_LR * (m_hat / (jnp.sqrt(v_hat) + ADAM_EPS) + ADAM_WD * w_ref[...])
        nm_ref[...] = m2
        nv_ref[...] = v2

    tr = R
    while tr * C * 4 > ELEMENTWISE_BLOCK_BYTES and tr % 16 == 0:
        tr //= 2
    spec = pl.BlockSpec((None, tr, C), lambda i, r: (i, r, 0))
    return pl.pallas_call(
        body, name=name, grid=(n, R // tr), in_specs=[spec] * 4, out_specs=[spec] * 4,
        out_shape=[_sds((n, R, C), F32)] * 4, compiler_params=_params("parallel", "parallel"),
    )(w, g, m, v)


ANY = pl.BlockSpec(memory_space=pl.ANY)


def _coords():
    return lax.axis_index("x"), lax.axis_index("y"), lax.axis_index("c")


def _other_chips(mx, my):
    return [(1 - mx, my), (mx, 1 - my), (1 - mx, 1 - my)]


def _remote(src, dst, send, recv, dev):
    return pltpu.make_async_remote_copy(src_ref=src, dst_ref=dst, send_sem=send, recv_sem=recv, device_id=dev,
                                        device_id_type=MESH)


HBM =pl.BlockSpec(memory_space=pltpu.HBM)
SEM = pl.BlockSpec(memory_space=pltpu.SEMAPHORE)
_SPLIT_COPY = pltpu.CompilerParams(has_side_effects=pltpu.SideEffectType.DATAFLOW_SIDE_EFFECTING)


def _in_hbm(a):
    return pltpu.with_memory_space_constraint(a, pltpu.HBM)


def cast_into_slot(w, layer, chip_core, name, dtype=BF16):
    _, _, hR, C = w.shape

    def body(s_ref, w_ref, o_ref):
        del s_ref
        o_ref[...] = w_ref[...].astype(dtype)

    grid_spec = pltpu.PrefetchScalarGridSpec(
        num_scalar_prefetch=1, grid=(2,),
        in_specs=[pl.BlockSpec((None, None, hR, C), lambda h, s: (layer, h, 0, 0))],
        out_specs=pl.BlockSpec((None, None, hR, C), lambda h, s: (s[0], h, 0, 0)))
    return pl.pallas_call(body, name=name, grid_spec=grid_spec, out_shape=_sds((N_CHIPS, 2, hR, C), dtype),
                          compiler_params=_params("parallel"))(chip_core, w)


def gather_start(lands, groups, name):
    n = len(lands)
    n_groups = len(groups)

    def body(*refs):
        ins = refs[:n]
        sems = refs[n:n + 2 * n_groups]
        token = refs[-1]
        mx, my, mc = _coords()
        chip = 2 * mx + my
        for g, members in enumerate(groups):
            send, recv = sems[2 * g], sems[2 * g + 1]
            for i, a in enumerate(members):
                mine = ins[a].at[chip, mc]
                for k, (px, py) in enumerate(_other_chips(mx, my)):
                    _remote(mine, mine, send.at[3 * i + k], recv.at[3 * i + k], (px, py, mc)).start()
        token[...] = jnp.zeros_like(token)

    sem_shapes = []
    for members in groups:
        sem_shapes += [pltpu.SemaphoreType.DMA((3 * len(members),))] * 2
    outs = pl.pallas_call(
        body, name=name, in_specs=[HBM] * n,
        out_specs=[SEM] * (2 * n_groups) + [HBM] * n + [pl.BlockSpec(memory_space=pltpu.VMEM)],
        out_shape=sem_shapes + [pltpu.HBM(a.shape, a.dtype) for a in lands] + [_sds((SUBLANES, LANES), F32)],
        input_output_aliases={a: 2 * n_groups + a for a in range(n)}, compiler_params=_SPLIT_COPY,
    )(*[_in_hbm(a) for a in lands])
    sems = [(outs[2 * g], outs[2 * g + 1]) for g in range(n_groups)]
    return sems, list(outs[2 * n_groups:2 * n_groups + n]), outs[-1]


def gather_forward(lands, sems, after, name):
    n = len(lands)

    def body(*refs):
        ins = refs[:n]
        send, recv = refs[n], refs[n + 1]
        fsend, frecv = refs[n + 3], refs[n + 4]
        mx, my, mc = _coords()
        for i in range(n):
            for k, (px, py) in enumerate(_other_chips(mx, my)):
                landed = ins[i].at[2 * px + py, mc]
                cp = _remote(landed, landed, send.at[3 * i + k], recv.at[3 * i + k], (px, py, mc))
                cp.wait_send()
                cp.wait_recv()
                _remote(landed, landed, fsend.at[3 * i + k], frecv.at[3 * i + k], (mx, my, 1 - mc)).start()

    outs = pl.pallas_call(
        body, name=name, in_specs=[HBM] * n + [SEM, SEM, ANY], out_specs=[SEM, SEM] + [HBM] * n,
        out_shape=[pltpu.SemaphoreType.DMA((3 * n,))] * 2 + [pltpu.HBM(a.shape, a.dtype) for a in lands],
        input_output_aliases={a: 2 + a for a in range(n)}, compiler_params=_SPLIT_COPY,
    )(*lands, sems[0], sems[1], after)
    return (outs[0], outs[1]), list(outs[2:])


def gather_wait(lands, sems, after, name):
    n = len(lands)

    def body(*refs):
        ins = refs[:n]
        fsend, frecv = refs[n], refs[n + 1]
        mx, my, mc = _coords()
        for i in range(n):
            for k, (px, py) in enumerate(_other_chips(mx, my)):
                theirs = ins[i].at[2 * px + py, 1 - mc]
                cp = _remote(theirs, theirs, fsend.at[3 * i + k], frecv.at[3 * i + k], (mx, my, 1 - mc))
                cp.wait_send()
                cp.wait_recv()

    outs = pl.pallas_call(
        body, name=name, in_specs=[HBM] * n + [SEM, SEM, ANY], out_specs=[HBM] * n,
        out_shape=[pltpu.HBM(a.shape, a.dtype) for a in lands],
        input_output_aliases={a: a for a in range(n)}, compiler_params=_SPLIT_COPY,
    )(*lands, sems[0], sems[1], after)
    return list(outs)


def _peers(mx, my, mc):
    return [(1 - mx if k & 4 else mx, 1 - my if k & 2 else my, 1 - mc if k & 1 else mc) for k in range(1, N_DEV)]


def devices_start(x, name):
    def body(x_ref, land_ref, send, recv, x_thru, land_thru):
        mx, my, mc = _coords()
        me = 4 * mx + 2 * my + mc
        for k, peer in enumerate(_peers(mx, my, mc)):
            _remote(x_ref, land_ref.at[me], send.at[k], recv.at[k], peer).start()

    land = lax.empty((N_DEV,) + x.shape, x.dtype)
    outs = pl.pallas_call(
        body, name=name, in_specs=[HBM, HBM], out_specs=[SEM, SEM, HBM, HBM],
        out_shape=[pltpu.SemaphoreType.DMA((N_DEV - 1,))] * 2 + [pltpu.HBM(x.shape, x.dtype), pltpu.HBM(land.shape, x.dtype)],
        input_output_aliases={0: 2, 1: 3}, compiler_params=_SPLIT_COPY,
    )(_in_hbm(x), _in_hbm(land))
    return (outs[0], outs[1]), outs[2], outs[3]


def devices_wait(x, land, sems, after, name):
    def body(x_ref, land_ref, send, recv, after_ref, x_thru, land_thru):
        mx, my, mc = _coords()
        for k, (px, py, pc) in enumerate(_peers(mx, my, mc)):
            cp = _remote(x_ref, land_ref.at[4 * px + 2 * py + pc], send.at[k], recv.at[k], (px, py, pc))
            cp.wait_send()
            cp.wait_recv()

    outs = pl.pallas_call(
        body, name=name, in_specs=[HBM, HBM, SEM, SEM, ANY], out_specs=[HBM, HBM],
        out_shape=[pltpu.HBM(x.shape, x.dtype), pltpu.HBM(land.shape, land.dtype)],
        input_output_aliases={0: 0, 1: 1}, compiler_params=_SPLIT_COPY,
    )(x, land, sems[0], sems[1], after)
    return outs[0], outs[1]


def device_sum(land, own, me, name):
    _, R, C = land.shape

    def body(s_ref, l_ref, o_ref_in, o_ref):
        acc = None
        for q in range(N_DEV):
            term = jnp.where(s_ref[0] == q, o_ref_in[...], l_ref[q])
            acc = term if acc is None else acc + term
        o_ref[...] = acc

    grid_spec = pltpu.PrefetchScalarGridSpec(
        num_scalar_prefetch=1, grid=(1,),
        in_specs=[pl.BlockSpec((N_DEV, R, C), lambda i, s: (0, 0, 0)), pl.BlockSpec((R, C), lambda i, s: (0, 0))],
        out_specs=pl.BlockSpec((R, C), lambda i, s: (0, 0)))
    return pl.pallas_call(body, name=name, grid_spec=grid_spec, out_shape=_sds((R, C), F32),
                          compiler_params=_params("arbitrary"))(me, land, own)


def reduce_send(grads, name):
    n = len(grads)

    def body(*refs):
        ins, lands = refs[:n], refs[n:2 * n]
        send, recv = refs[2 * n], refs[2 * n + 1]
        mx, my, mc = _coords()
        me = 4 * mx + 2 * my + mc
        for a in range(n):
            for k, (px, py, pc) in enumerate(_peers(mx, my, mc)):
                _remote(ins[a].at[2 * px + py, pc], lands[a].at[me], send.at[7 * a + k], recv.at[7 * a + k], (px, py, pc)).start()

    lands = [lax.empty((N_DEV,) + g.shape[2:], g.dtype) for g in grads]
    outs = pl.pallas_call(
        body, name=name, in_specs=[HBM] * (2 * n), out_specs=[SEM, SEM] + [HBM] * (2 * n),
        out_shape=[pltpu.SemaphoreType.DMA((7 * n,))] * 2 + [pltpu.HBM(a.shape, a.dtype) for a in grads + lands],
        input_output_aliases={a: 2 + a for a in range(2 * n)}, compiler_params=_SPLIT_COPY,
    )(*[_in_hbm(a) for a in grads + lands])
    return (outs[0], outs[1]), list(outs[2:2 + n]), list(outs[2 + n:])


def reduce_wait(grads, lands, sems, after, name):
    n = len(grads)

    def body(*refs):
        ins, zones = refs[:n], refs[n:2 * n]
        send, recv = refs[2 * n], refs[2 * n + 1]
        mx, my, mc = _coords()
        for a in range(n):
            for k, (px, py, pc) in enumerate(_peers(mx, my, mc)):
                cp = _remote(ins[a].at[2 * px + py, pc], zones[a].at[4 * px + 2 * py + pc], send.at[7 * a + k],
                             recv.at[7 * a + k], (px, py, pc))
                cp.wait_send()
                cp.wait_recv()

    outs = pl.pallas_call(
        body, name=name, in_specs=[HBM] * (2 * n) + [SEM, SEM, ANY], out_specs=[HBM] * (2 * n),
        out_shape=[pltpu.HBM(a.shape, a.dtype) for a in grads + lands],
        input_output_aliases={a: a for a in range(2 * n)}, compiler_params=_SPLIT_COPY,
    )(*grads, *lands, sems[0], sems[1], after)
    return list(outs[:n]), list(outs[n:])


def reduce_sum(land, grad, place, name, into=None, layer=None):
    _, hR, C = land.shape
    tr = hR
    while N_DEV * tr * C * 2 > 3 * ELEMENTWISE_BLOCK_BYTES and tr % 32 == 0:
        tr //= 2

    def body(s_ref, l_ref, g_ref, *rest):
        o_ref = rest[-1]
        own = g_ref[...].astype(F32)
        acc = None
        for q in range(N_DEV):
            term = jnp.where(s_ref[2] == q, own, l_ref[q].astype(F32))
            acc = term if acc is None else acc + term
        o_ref[...] = acc

    in_specs = [pl.BlockSpec((N_DEV, tr, C), lambda i, s: (0, i, 0)),
                pl.BlockSpec((None, None, tr, C), lambda i, s: (s[0], s[1], i, 0))]
    args = [place, land, grad]
    aliases = {}
    if layer is None:
        out_spec = pl.BlockSpec((None, tr, C), lambda i, s: (s[1], i, 0))
        out_shape = _sds((2, hR, C), F32)
    else:
        out_spec = pl.BlockSpec((None, None, tr, C), lambda i, s: (layer, s[1], i, 0))
        out_shape = _sds((2, 2, hR, C), F32)
        if into is not None:
            in_specs.append(ANY)
            args.append(into)
            aliases = {3: 0}
    grid_spec = pltpu.PrefetchScalarGridSpec(num_scalar_prefetch=1, grid=(hR // tr,), in_specs=in_specs, out_specs=out_spec)
    return pl.pallas_call(body, name=name, grid_spec=grid_spec, out_shape=out_shape, input_output_aliases=aliases,
                          compiler_params=_params("arbitrary"))(*args)


def join_halves(arrays, name):
    n = len(arrays)
    pieces = [(a, l) for a, arr in enumerate(arrays) for l in (range(arr.shape[0]) if arr.ndim == 4 else [None])]

    def body(*refs):
        ins = refs[:n]
        send, recv = refs[2 * n:]
        mx, my, mc = _coords()

        def half(a, l, h):
            return ins[a].at[h] if l is None else ins[a].at[l, h]

        sends = [_remote(half(a, l, mc), half(a, l, mc), send.at[i], recv.at[i], (mx, my, 1 - mc))
                 for i, (a, l) in enumerate(pieces)]
        for cp in sends:
            cp.start()
        for i, (a, l) in enumerate(pieces):
            theirs = half(a, l, 1 - mc)
            _remote(theirs, theirs, send.at[i], recv.at[i], (mx, my, 1 - mc)).wait_recv()
        for cp in sends:
            cp.wait_send()

    return pl.pallas_call(
        body, name=name, in_specs=[ANY] * n, out_specs=[ANY] * n, out_shape=[_sds(a.shape, a.dtype) for a in arrays],
        input_output_aliases={a: a for a in range(n)},
        scratch_shapes=[pltpu.SemaphoreType.DMA((len(pieces),)), pltpu.SemaphoreType.DMA((len(pieces),))],
    )(*arrays)


LANES = 128
SUBLANES = 8


def _n_rows(shape):
    rows = -(-int(np.prod(shape)) // LANES)
    return -(-rows // SUBLANES) * SUBLANES


def _as_rows(a):
    flat = a.reshape(-1)
    rows = _n_rows(a.shape)
    return jnp.pad(flat, (0, rows * LANES - flat.shape[0])).reshape(rows, LANES)


def _pack(arrays):
    return jnp.concatenate([_as_rows(a) for a in arrays], axis=0)


def _unpack(rows, shapes):
    out, r0 = [], 0
    for s in shapes:
        n = _n_rows(s)
        out.append(rows[r0:r0 + n].reshape(-1)[:int(np.prod(s))].reshape(s))
        r0 += n
    return out


REPLICATED_SMALL = [("rel_bias", (32, 16)), ("even_norm", (1, 1024)), ("even_pool_w", (1, 4, 128, 128)),
                    ("even_pool_scale", (1, 512)), ("odd_q_norm", (1, 64)), ("odd_k_norm", (1, 64)),
                    ("ffn_norm", (2, 1024)), ("ffn_conv_b", (2, 5632))]
SHARDED_SMALL = [("even_conv_w", (1, 3, 128)), ("odd_norm", (1, 256)), ("ffn_conv_w", (2, 3, 1408))]
BIG = ["even_w_in", "even_w_out", "odd_w_qkv", "odd_w_o", "ffn_w_up", "ffn_w_down"]
WEIGHT_ORDER = ["rel_bias", "even_norm", "even_w_in", "even_conv_w", "even_pool_w", "even_pool_scale", "even_w_out",
                "odd_norm", "odd_w_qkv", "odd_q_norm", "odd_k_norm", "odd_w_o", "ffn_norm", "ffn_w_up", "ffn_conv_w",
                "ffn_conv_b", "ffn_w_down"]


def kernel(x, rel_bias, even_norm, even_w_in, even_conv_w, even_pool_w, even_pool_scale, even_w_out, odd_norm, odd_w_qkv, odd_q_norm, odd_k_norm, odd_w_o, ffn_norm, ffn_w_up, ffn_conv_w, ffn_conv_b, ffn_w_down, loss_target, m_rel_bias, m_even_norm, m_even_w_in, m_even_conv_w, m_even_pool_w, m_even_pool_scale, m_even_w_out, m_odd_norm, m_odd_w_qkv, m_odd_q_norm, m_odd_k_norm, m_odd_w_o, m_ffn_norm, m_ffn_w_up, m_ffn_conv_w, m_ffn_conv_b, m_ffn_w_down, v_rel_bias, v_even_norm, v_even_w_in, v_even_conv_w, v_even_pool_w, v_even_pool_scale, v_even_w_out, v_odd_norm, v_odd_w_qkv, v_odd_q_norm, v_odd_k_norm, v_odd_w_o, v_ffn_norm, v_ffn_w_up, v_ffn_conv_w, v_ffn_conv_b, v_ffn_w_down):
    W = dict(rel_bias=rel_bias, even_norm=even_norm, even_w_in=even_w_in, even_conv_w=even_conv_w, even_pool_w=even_pool_w,
             even_pool_scale=even_pool_scale, even_w_out=even_w_out, odd_norm=odd_norm, odd_w_qkv=odd_w_qkv,
             odd_q_norm=odd_q_norm, odd_k_norm=odd_k_norm, odd_w_o=odd_w_o, ffn_norm=ffn_norm, ffn_w_up=ffn_w_up,
             ffn_conv_w=ffn_conv_w, ffn_conv_b=ffn_conv_b, ffn_w_down=ffn_w_down)
    M1 = dict(rel_bias=m_rel_bias, even_norm=m_even_norm, even_w_in=m_even_w_in, even_conv_w=m_even_conv_w,
              even_pool_w=m_even_pool_w, even_pool_scale=m_even_pool_scale, even_w_out=m_even_w_out, odd_norm=m_odd_norm,
              odd_w_qkv=m_odd_w_qkv, odd_q_norm=m_odd_q_norm, odd_k_norm=m_odd_k_norm, odd_w_o=m_odd_w_o,
              ffn_norm=m_ffn_norm, ffn_w_up=m_ffn_w_up, ffn_conv_w=m_ffn_conv_w, ffn_conv_b=m_ffn_conv_b,
              ffn_w_down=m_ffn_w_down)
    M2 = dict(rel_bias=v_rel_bias, even_norm=v_even_norm, even_w_in=v_even_w_in, even_conv_w=v_even_conv_w,
              even_pool_w=v_even_pool_w, even_pool_scale=v_even_pool_scale, even_w_out=v_even_w_out, odd_norm=v_odd_norm,
              odd_w_qkv=v_odd_w_qkv, odd_q_norm=v_odd_q_norm, odd_k_norm=v_odd_k_norm, odd_w_o=v_odd_w_o,
              ffn_norm=v_ffn_norm, ffn_w_up=v_ffn_w_up, ffn_conv_w=v_ffn_conv_w, ffn_conv_b=v_ffn_conv_b,
              ffn_w_down=v_ffn_w_down)
    mx, my, mc = _coords()
    chip = 2 * mx + my
    me = 4 * mx + 2 * my + mc
    place = jnp.stack([chip, mc, me]).astype(jnp.int32)
    xs, target = x[0], loss_target[0]

    def halves(w):
        return w.reshape((w.shape[0], 2, w.shape[-2] // 2, w.shape[-1]))

    small_rows = jnp.pad(_pack([even_conv_w, odd_norm, ffn_conv_w]), ((0, SUBLANES), (0, 0)))
    first = [cast_into_slot(halves(even_w_in), 0, place, "cast_w_in"), cast_into_slot(halves(even_w_out), 0, place, "cast_w_out"),
             cast_into_slot(small_rows.reshape(1, 2, small_rows.shape[0] // 2, LANES), 0, place, "small_into_slot", dtype=F32)]
    first_sems, first, token = gather_start(first, [[0, 1, 2]], "gather_start_first")
    even_norm_after_start = even_norm + token[0:1, 0:1]

    def later(a):
        return lax.optimization_barrier((a, token))[0]

    up_f32, down_f32 = halves(later(ffn_w_up)), halves(later(ffn_w_down))
    rest = [cast_into_slot(up_f32, 0, place, "cast_w_up0"), cast_into_slot(down_f32, 0, place, "cast_w_down0"),
            cast_into_slot(halves(later(odd_w_qkv)), 0, place, "cast_w_qkv"), cast_into_slot(halves(later(odd_w_o)), 0, place, "cast_w_o"),
            cast_into_slot(up_f32, 1, place, "cast_w_up1"), cast_into_slot(down_f32, 1, place, "cast_w_down1")]
    rest_sems, rest, _ = gather_start(rest, [[0], [1], [2, 3], [4], [5]], "gather_start_rest")
    group_arrays = [first, [rest[0]], [rest[1]], [rest[2], rest[3]], [rest[4]], [rest[5]]]
    group_sems = first_sems + rest_sems

    def gathered(group, tag, after_landing, after_passing):
        sems, arrays = gather_forward(group_arrays[group], group_sems[group], after_landing, "gather_forward_" + tag)
        return gather_wait(arrays, sems, after_passing, "gather_wait_" + tag)

    pool_w = cast_bf16(even_pool_w[0], "cast_pool_w")
    gqk = jnp.stack([jnp.tile(odd_q_norm[0], N_HEADS), jnp.tile(odd_k_norm[0], N_HEADS),
                     jnp.ones((D_MODEL,), F32)])[:, None, :]
    bias = bias_expand(later(rel_bias).T, "bias_expand").reshape(6, N_HEADS, ATT_BLOCK, 2 * ATT_BLOCK)
    xn0 = rmsnorm_fwd(xs, even_norm_after_start, "even_norm")
    got = gathered(0, "even", bias, xn0)
    w_in = got[0].reshape(N_CHIPS, 1, D_MODEL, EVEN_IN // N_CHIPS)
    w_out = got[1].reshape(1, 1, D_MODEL, D_MODEL)
    small = got[2].reshape(N_CHIPS, small_rows.shape[0], LANES)
    conv_w_full = small[:, 0:3].transpose(1, 0, 2).reshape(3, A_WIDTH)
    odd_norm_full = small[:, 8:10].reshape(1, D_MODEL)
    ffn_cw_full = small[:, 16:82].reshape(N_CHIPS, 2, 3, 2 * D_FF // N_CHIPS).transpose(1, 2, 0, 3).reshape(2, 3, 2 * D_FF)

    def ffn_fwd(l, xin, xn):
        up, u, act = up_glu_fwd(xn, w_up[l], ffn_cw_full[l], ffn_conv_b[l:l + 1], f"ffn{l}_up_glu")
        return act, (xin, xn, up, u, act)

    w_up, w_down = [None, None], [None, None]
    proj, mix = in_mixer_fwd(xn0, w_in, conv_w_full, pool_w, even_pool_scale, "even_in_mixer")
    x1, xn1 = mm_res_norm(mix, w_out, xs, ffn_norm[0:1], "even_out")
    w_up[0] = gathered(1, "up0", proj, x1)[0].reshape(N_CHIPS, 1, D_MODEL, 2 * D_FF // N_CHIPS)
    act0, ffn0 = ffn_fwd(0, x1, xn1)
    w_down[0] = gathered(2, "down0", act0, act0)[0].reshape(1, 1, D_FF, D_MODEL)
    x2, xn2 = mm_res_norm(act0, w_down[0], x1, odd_norm_full, "ffn0_down")
    got = gathered(3, "odd", x1, x2)
    w_qkv = got[0].reshape(N_CHIPS, 1, D_MODEL, 3 * D_MODEL // N_CHIPS)
    w_o = got[1].reshape(1, 1, D_MODEL, D_MODEL)
    qkv, qkvn = qkv_qknorm_fwd(xn2, w_qkv, gqk, "odd_qkv_qknorm")
    att, lse = attn_fwd(qkvn, bias, "attn_fwd")
    x3, xn3 = mm_res_norm(att, w_o, x2, ffn_norm[1:2], "odd_out")
    w_up[1] = gathered(4, "up1", x2, x3)[0].reshape(N_CHIPS, 1, D_MODEL, 2 * D_FF // N_CHIPS)
    act1, ffn1 = ffn_fwd(1, x3, xn3)
    w_down[1] = gathered(5, "down1", act1, act1)[0].reshape(1, 1, D_FF, D_MODEL)
    dy, dyb, sq = mm_res_loss(act1, w_down[1], x3, target, "ffn1_down_loss")
    loss = lax.psum(0.5 * jnp.sum(sq) * (1.0 / D_MODEL), ("x", "y", "c"))

    def ffn_bwd(l, dy, dyb, saved):
        xin, xn, up, u, act = saved
        dw_down = mm_tn(act, dyb, f"ffn{l}_dw_down", J=1, tk=D_FF // 2, tm=1024)
        dup, dcw, dcb = glu_bwd(up, u, dyb, w_down[l], ffn_cw_full[l], f"ffn{l}_glu_bwd")
        dw_up = mm_tn(xn, dup, f"ffn{l}_dw_up", J=N_CHIPS, tk=512, tm=1024, jb=2)
        dx, dxb, dg = mm_nt_norm_bwd(dup, w_up[l], xin, ffn_norm[l:l + 1], dy, f"ffn{l}_dx")
        return dx, dxb, (dw_down, dw_up, dcw, dcb, dg)

    def quarters(g):
        return g.reshape(N_CHIPS, 2, g.shape[0] * g.shape[1] // (2 * N_CHIPS), g.shape[-1])

    def reduce_start(grads, tag, then):
        sems, parts, zones = reduce_send([quarters(g) for g in grads], "reduce_send_" + tag)
        then, parts = lax.optimization_barrier((then, parts))
        return (sems, parts, zones), then

    dx3, dx3b, g_ffn1 = ffn_bwd(1, dy, dyb, ffn1)
    red_ffn1, (dx3, dx3b) = reduce_start([g_ffn1[1], g_ffn1[0]], "ffn1", (dx3, dx3b))
    dw_o = mm_tn(att, dx3b, "odd_dw_o", J=1, tk=512, tm=1024)
    datt = mm_nt(dx3b, w_o, "odd_datt", tr=D_MODEL, out_dtype=BF16)
    dq, dk, dv, dbias = attn_bwd(qkvn, att, datt, lse, bias, "attn_bwd")
    dqkv, dgqk = qknorm_bwd(qkv, dq, dk, dv, gqk, "odd_qknorm_bwd")
    dw_qkv = mm_tn(xn2, dqkv, "odd_dw_qkv", J=N_CHIPS, tk=512, tm=1024)
    red_odd, dqkv = reduce_start([dw_qkv, dw_o], "odd", dqkv)
    dx2, dx2b, dg_odd = mm_nt_norm_bwd(dqkv, w_qkv, x2, odd_norm_full, dx3, "odd_dx")
    dx1, dx1b, g_ffn0 = ffn_bwd(0, dx2, dx2b, ffn0)
    red_ffn0, (dx1, dx1b) = reduce_start([g_ffn0[1], g_ffn0[0]], "ffn0", (dx1, dx1b))
    dw_out = mm_tn(mix, dx1b, "even_dw_out", J=1, tk=512, tm=1024)
    dmix = mm_nt(dx1b, w_out, "even_dmix", tr=D_MODEL)
    dproj, dcw_even, dpw, dps = mixer_bwd(proj, dmix, conv_w_full, pool_w, even_pool_scale, "even_mixer_bwd")
    dw_in = mm_tn(xn0, dproj, "even_dw_in", J=N_CHIPS, tk=512, tm=1024)
    grad_x, _, dg_even = mm_nt_norm_bwd(dproj, w_in, xs, even_norm, dx1, "even_dx")
    d_rel = jnp.sum(bias_reduce(dbias.reshape(3, N_HEADS, 2 * ATT_BLOCK * ATT_BLOCK), "bias_reduce"), axis=0).T

    red_even, grad_x = reduce_start([dw_in, dw_out], "even", grad_x)

    dcw_sh = dcw_even.reshape(3, N_CHIPS, A_WIDTH // N_CHIPS).transpose(1, 0, 2)
    don_sh = dg_odd.reshape(N_CHIPS, D_MODEL // N_CHIPS)
    dfcw = jnp.stack([g_ffn0[2], g_ffn1[2]])
    dfcw_sh = dfcw.reshape(2, 3, N_CHIPS, 2 * D_FF // N_CHIPS).transpose(2, 0, 1, 3)
    rep_grads = [d_rel, dg_even, dpw[None], dps, _head_sum(dgqk[0]), _head_sum(dgqk[1]),
                 jnp.concatenate([g_ffn0[4], g_ffn1[4]], axis=0), jnp.concatenate([g_ffn0[3], g_ffn1[3]], axis=0)]
    rep_rows = _pack(rep_grads)
    shard_rows = jnp.concatenate([_pack([dcw_sh[j], don_sh[j], dfcw_sh[j]]) for j in range(N_CHIPS)], axis=0)
    n_rep, n_shard = rep_rows.shape[0], shard_rows.shape[0] // N_CHIPS
    small_sems, small_rows, small_land = devices_start(jnp.concatenate([rep_rows, shard_rows], axis=0), "small_grads_start")
    grad_x, small_rows = lax.optimization_barrier((grad_x, small_rows))

    def reduce_end(red, tag, after):
        sems, parts, zones = red
        parts, zones = reduce_wait(parts, zones, sems, after, "reduce_wait_" + tag)
        return zones, parts

    z_ffn1, p_ffn1 = reduce_end(red_ffn1, "ffn1", grad_x)
    z_odd, p_odd = reduce_end(red_odd, "odd", grad_x)
    r_qkv = reduce_sum(z_odd[0], p_odd[0], place, "reduce_sum_w_qkv")
    r_o = reduce_sum(z_odd[1], p_odd[1], place, "reduce_sum_w_o")
    r_up = reduce_sum(z_ffn1[0], p_ffn1[0], place, "reduce_sum_w_up1", layer=1)
    r_down = reduce_sum(z_ffn1[1], p_ffn1[1], place, "reduce_sum_w_down1", layer=1)
    r_qkv, r_o, r_up, r_down = lax.optimization_barrier((r_qkv, r_o, r_up, r_down))
    z_ffn0, p_ffn0 = reduce_end(red_ffn0, "ffn0", r_down)
    r_up = reduce_sum(z_ffn0[0], p_ffn0[0], place, "reduce_sum_w_up0", into=r_up, layer=0)
    r_down = reduce_sum(z_ffn0[1], p_ffn0[1], place, "reduce_sum_w_down0", into=r_down, layer=0)
    later = ["odd_w_qkv", "odd_w_o", "ffn_w_up", "ffn_w_down"]
    joined = join_halves([r_qkv, r_o, r_up, r_down], "grads_join_late_layers")
    G = {nm: g.reshape(W[nm].shape) for nm, g in zip(later, joined)}

    D_, NM, NV = {}, {}, {}

    def update(nm):
        as3 = lambda a: a.reshape((-1,) + a.shape[-2:])
        outs = adamw(as3(W[nm]), as3(G[nm]), as3(M1[nm]), as3(M2[nm]), "adamw_" + nm)
        D_[nm], NM[nm], NV[nm], G[nm] = [o.reshape(W[nm].shape) for o in outs]

    def all_before(names):
        tied = lax.optimization_barrier([D_[nm] for nm in names])
        for nm, d in zip(names, tied):
            D_[nm] = d
        return tied[0]

    for nm in later:
        update(nm)
    z_even, p_even = reduce_end(red_even, "even", all_before(later))
    joined = join_halves([reduce_sum(z_even[0], p_even[0], place, "reduce_sum_w_in"),
                          reduce_sum(z_even[1], p_even[1], place, "reduce_sum_w_out")], "grads_join_first_layer")
    first = ["even_w_in", "even_w_out"]
    for nm, g in zip(first, joined):
        G[nm] = g.reshape(W[nm].shape)
        update(nm)
    small_rows, small_land = devices_wait(small_rows, small_land, small_sems, all_before(first), "small_grads_wait")
    small_sum = device_sum(small_land, small_rows, place[2:3], "small_grads_sum")
    mine = lax.dynamic_slice_in_dim(small_sum, n_rep + chip * n_shard, n_shard, axis=0)
    g_small = jnp.concatenate([small_sum[:n_rep], mine], axis=0)
    small_names = [n for n, _ in REPLICATED_SMALL + SHARDED_SMALL]
    small_shapes = [s for _, s in REPLICATED_SMALL + SHARDED_SMALL]
    G.update(dict(zip(small_names, _unpack(g_small, small_shapes))))
    packs = [_pack([d[n] for n in small_names])[None] for d in (W, M1, M2)]
    outs = adamw(packs[0], g_small[None], packs[1], packs[2], "adamw_small")
    for dst, o in zip((D_, NM, NV), outs[:3]):
        dst.update(dict(zip(small_names, _unpack(o[0], small_shapes))))

    return (loss, grad_x[None], *[G[n] for n in WEIGHT_ORDER], *[D_[n] for n in WEIGHT_ORDER],
            *[NM[n] for n in WEIGHT_ORDER], *[NV[n] for n in WEIGHT_ORDER])


def _head_sum(dg):
    return jnp.sum(dg.reshape(N_HEADS, HEAD_DIM), axis=0, keepdims=True)
```

```python
import functools
import math

import numpy as np
import jax
import jax.numpy as jnp
from jax import lax
from jax.experimental import pallas as pl
from jax.experimental.pallas import tpu as pltpu

F32 = jnp.float32
BF16 = jnp.bfloat16

D_MODEL = 1024
N_HEADS = 16
HEAD_DIM = 64
A_WIDTH = 512
POOL_WINDOWS = (2, 4, 8, 16)
POOL_GROUP = 128
EVEN_IN = 2048
D_FF = 2816
DILATED_PAIRS = ((128, 1), (512, 4), (2048, 16))
ATT_BLOCK = 128
N_REL_BUCKETS = 32
REL_MAX_DISTANCE = 2048
EPS = 1e-6
MASK_VALUE = -1e30
ADAM_LR, ADAM_B1, ADAM_B2, ADAM_EPS, ADAM_WD, ADAM_STEP = 0.001, 0.9, 0.999, 1e-08, 0.01, 10

VMEM_LIMIT_BYTES = 48 * 1024 * 1024
ELEMENTWISE_BLOCK_BYTES = 2 * 1024 * 1024
N_CHIPS = 4
N_DEV = 8
MESH = pl.DeviceIdType.MESH


def _params(*sem):
    return pltpu.CompilerParams(dimension_semantics=sem if sem else None, vmem_limit_bytes=VMEM_LIMIT_BYTES)


def _sds(shape, dtype):
    return pltpu.HBM(tuple(shape), dtype)


def cast_bf16(x, name, tr=None):
    lead, (R, C) = x.shape[:-2], x.shape[-2:]
    n = int(np.prod(lead)) if lead else 1
    x3 = x.reshape((n, R, C))
    tr = tr or R

    def body(x_ref, o_ref):
        o_ref[...] = x_ref[...].astype(BF16)

    out = pl.pallas_call(
        body, name=name, grid=(n, R // tr),
        in_specs=[pl.BlockSpec((None, tr, C), lambda i, r: (i, r, 0))],
        out_specs=pl.BlockSpec((None, tr, C), lambda i, r: (i, r, 0)),
        out_shape=_sds((n, R, C), BF16), compiler_params=_params("parallel", "parallel"),
    )(x3)
    return out.reshape(lead + (R, C))


def rmsnorm_fwd(x, g, name, ts=512):
    S, Dm = x.shape

    def body(x_ref, g_ref, o_ref):
        xv = x_ref[...]
        r = lax.rsqrt(jnp.mean(xv * xv, axis=-1, keepdims=True) + EPS)
        o_ref[...] = ((xv * r) * g_ref[...]).astype(BF16)

    return pl.pallas_call(
        body, name=name, grid=(S // ts,),
        in_specs=[pl.BlockSpec((ts, Dm), lambda i: (i, 0)), pl.BlockSpec((1, Dm), lambda i: (0, 0))],
        out_specs=pl.BlockSpec((ts, Dm), lambda i: (i, 0)),
        out_shape=_sds((S, Dm), BF16), compiler_params=_params("parallel"),
    )(x, g)


def mm_res_norm(a, w, res, gain, name, tm=1024):
    M, K = a.shape
    Dm = w.shape[-1]

    def body(a_ref, w_ref, r_ref, g_ref, y_ref, yn_ref):
        y = r_ref[...] + jnp.dot(a_ref[...], w_ref[...], preferred_element_type=F32)
        y_ref[...] = y
        r = lax.rsqrt(jnp.mean(y * y, axis=-1, keepdims=True) + EPS)
        yn_ref[...] = ((y * r) * g_ref[...]).astype(BF16)

    row = pl.BlockSpec((tm, Dm), lambda m: (m, 0))
    return pl.pallas_call(
        body, name=name, grid=(M // tm,),
        in_specs=[pl.BlockSpec((tm, K), lambda m: (m, 0)),
                  pl.BlockSpec((None, None, K, Dm), lambda m: (0, 0, 0, 0), pipeline_mode=pl.Buffered(1)),
                  row, pl.BlockSpec((1, Dm), lambda m: (0, 0))],
        out_specs=[row, row], out_shape=[_sds((M, Dm), F32), _sds((M, Dm), BF16)],
        compiler_params=_params("parallel"),
    )(a, w, res, gain)


def mm_res_loss(a, w, res, target, name, tm=512):
    M, K = a.shape
    Dm = w.shape[-1]

    def body(a_ref, w_ref, r_ref, t_ref, d_ref, db_ref, s_ref):
        e = (r_ref[...] + jnp.dot(a_ref[...], w_ref[...], preferred_element_type=F32)) - t_ref[...]
        d = e * (1.0 / Dm)
        d_ref[...] = d
        db_ref[...] = d.astype(BF16)
        part = jnp.sum(e * e, axis=0, keepdims=True)

        @pl.when(pl.program_id(0) == 0)
        def _():
            s_ref[...] = part

        @pl.when(pl.program_id(0) > 0)
        def _():
            s_ref[...] += part

    row = pl.BlockSpec((tm, Dm), lambda m: (m, 0))
    return pl.pallas_call(
        body, name=name, grid=(M // tm,),
        in_specs=[pl.BlockSpec((tm, K), lambda m: (m, 0)),
                  pl.BlockSpec((None, None, K, Dm), lambda m: (0, 0, 0, 0), pipeline_mode=pl.Buffered(1)), row, row],
        out_specs=[row, row, pl.BlockSpec((1, Dm), lambda m: (0, 0))],
        out_shape=[_sds((M, Dm), F32), _sds((M, Dm), BF16), _sds((1, Dm), F32)],
        compiler_params=_params("arbitrary"),
    )(a, w, res, target)


def mm_nt(dy, w, name, tr, layer=0, out_dtype=F32, tm=512):
    M = dy.shape[0]
    J, _, R, Ns = w.shape
    dims = (((1,), (1,)), ((), ()))

    def body(dy_ref, w_ref, o_ref):
        acc = None
        for j in range(J):
            p = lax.dot_general(dy_ref[:, j * Ns:(j + 1) * Ns], w_ref[j], dims, preferred_element_type=F32)
            acc = p if acc is None else acc + p
        o_ref[...] = acc.astype(o_ref.dtype)

    return pl.pallas_call(
        body, name=name, grid=(R // tr, M // tm),
        in_specs=[pl.BlockSpec((tm, J * Ns), lambda r, m: (m, 0)),
                  pl.BlockSpec((J, None, tr, Ns), lambda r, m: (0, layer, r, 0))],
        out_specs=pl.BlockSpec((tm, tr), lambda r, m: (m, r)),
        out_shape=_sds((M, R), out_dtype),
        compiler_params=_params("parallel", "parallel"),
    )(dy, w)


def mm_nt_norm_bwd(dy, w, x, g, dres, name, layer=0, tm=512):
    M = dy.shape[0]
    J, _, Dm, Ns = w.shape
    dims = (((1,), (1,)), ((), ()))

    def body(dy_ref, w_ref, x_ref, g_ref, r_ref, dx_ref, dxb_ref, dg_ref):
        dxn = None
        for j in range(J):
            p = lax.dot_general(dy_ref[:, j * Ns:(j + 1) * Ns], w_ref[j], dims, preferred_element_type=F32)
            dxn = p if dxn is None else dxn + p
        xv = x_ref[...]
        r = lax.rsqrt(jnp.mean(xv * xv, axis=-1, keepdims=True) + EPS)
        gx = dxn * g_ref[...]
        dot = jnp.sum(gx * xv, axis=-1, keepdims=True)
        dx = r_ref[...] + r * gx - xv * ((r * r * r) * (dot * (1.0 / Dm)))
        dx_ref[...] = dx
        dxb_ref[...] = dx.astype(BF16)
        part = jnp.sum(dxn * (xv * r), axis=0, keepdims=True)

        @pl.when(pl.program_id(0) == 0)
        def _():
            dg_ref[...] = part

        @pl.when(pl.program_id(0) > 0)
        def _():
            dg_ref[...] += part

    row = pl.BlockSpec((tm, Dm), lambda m: (m, 0))
    vec = pl.BlockSpec((1, Dm), lambda m: (0, 0))
    return pl.pallas_call(
        body, name=name, grid=(M // tm,),
        in_specs=[pl.BlockSpec((tm, J * Ns), lambda m: (m, 0)),
                  pl.BlockSpec((J, None, Dm, Ns), lambda m: (0, layer, 0, 0), pipeline_mode=pl.Buffered(1)), row, vec, row],
        out_specs=[row, row, vec],
        out_shape=[_sds((M, Dm), F32), _sds((M, Dm), BF16), _sds((1, Dm), F32)],
        compiler_params=_params("arbitrary"),
    )(dy, w, x, g, dres)


def mm_tn(a, dy, name, J, tk, tm=512, jb=None):
    M, K = a.shape
    jb = jb or J
    Ns = dy.shape[1] // J
    N = jb * Ns
    n_m = M // tm
    dims = (((0,), (0,)), ((), ()))

    def body(a_ref, dy_ref, o_ref, acc_ref):
        p = lax.dot_general(a_ref[...], dy_ref[...], dims, preferred_element_type=F32)
        m = pl.program_id(2)

        @pl.when(m == 0)
        def _():
            acc_ref[...] = p

        @pl.when(m > 0)
        def _():
            acc_ref[...] += p

        @pl.when(m == n_m - 1)
        def _():
            for j in range(jb):
                o_ref[j] = acc_ref[:, j * Ns:(j + 1) * Ns].astype(BF16)

    return pl.pallas_call(
        body, name=name, grid=(J // jb, K // tk, n_m),
        in_specs=[pl.BlockSpec((tm, tk), lambda g, k, m: (m, k)), pl.BlockSpec((tm, N), lambda g, k, m: (m, g))],
        out_specs=pl.BlockSpec((jb, tk, Ns), lambda g, k, m: (g, k, 0)),
        out_shape=_sds((J, K, Ns), BF16), scratch_shapes=[pltpu.VMEM((tk, N), F32)],
        compiler_params=_params("parallel", "parallel", "arbitrary"),
    )(a, dy)


HALO = 16


def _shift_down(x, s):
    return pltpu.roll(x, s, 0)


def _shift_up(x, s):
    return pltpu.roll(x, x.shape[0] - s, 0)


def _conv3(z, cw):
    return (_shift_down(z, 2) * cw[0:1] + _shift_down(z, 1) * cw[1:2]) + z * cw[2:3]


def _window_count(first_row, n, k):
    t = first_row + lax.broadcasted_iota(jnp.int32, (n, 1), 0)
    return jnp.clip(t + 1, 1, k).astype(F32)


def in_mixer_fwd(xn, w_in, conv_w, pool_w, pool_scale, name, ts=512):
    S, K = xn.shape
    n = ts + HALO

    def body(xm_ref, xb_ref, w_ref, cw_ref, pw_ref, ps_ref, p_ref, o_ref):
        i = pl.program_id(0)
        before = jnp.where(i > 0, xb_ref[...], jnp.zeros_like(xb_ref))
        rows = jnp.concatenate([before, xm_ref[...]], axis=0)
        h, gb, gc, pin = [jnp.dot(rows, w_ref[j], preferred_element_type=F32) for j in range(N_CHIPS)]
        for j, part in enumerate((h, gb, gc, pin)):
            p_ref[:, j * A_WIDTH:(j + 1) * A_WIDTH] = part[HALO:]
        cz = _conv3(gc * h, cw_ref[...])
        o_ref[:, 0:A_WIDTH] = (gb[HALO:] * cz[HALO:]).astype(BF16)
        for g, k in enumerate(POOL_WINDOWS):
            p = pin[:, g * POOL_GROUP:(g + 1) * POOL_GROUP]
            w = p
            s = 1
            while s < k:
                w = w + _shift_down(w, s)
                s *= 2
            pooled = w / _window_count(i * ts - HALO, n, k) - p
            yb = jnp.dot(pooled[HALO:].astype(BF16), pw_ref[g], preferred_element_type=F32)
            yb = yb * ps_ref[:, g * POOL_GROUP:(g + 1) * POOL_GROUP]
            o_ref[:, A_WIDTH + g * POOL_GROUP:A_WIDTH + (g + 1) * POOL_GROUP] = yb.astype(BF16)

    hb = ts // HALO
    return pl.pallas_call(
        body, name=name, grid=(S // ts,),
        in_specs=[
            pl.BlockSpec((ts, K), lambda i: (i, 0)),
            pl.BlockSpec((HALO, K), lambda i: (jnp.maximum(i * hb - 1, 0), 0)),
            pl.BlockSpec((N_CHIPS, None, K, A_WIDTH), lambda i: (0, 0, 0, 0), pipeline_mode=pl.Buffered(1)),
            pl.BlockSpec((3, A_WIDTH), lambda i: (0, 0)),
            pl.BlockSpec((4, POOL_GROUP, POOL_GROUP), lambda i: (0, 0, 0)),
            pl.BlockSpec((1, 4 * POOL_GROUP), lambda i: (0, 0)),
        ],
        out_specs=[pl.BlockSpec((ts, EVEN_IN), lambda i: (i, 0)), pl.BlockSpec((ts, D_MODEL), lambda i: (i, 0))],
        out_shape=[_sds((S, EVEN_IN), F32), _sds((S, D_MODEL), BF16)], compiler_params=_params("parallel"),
    )(xn, xn, w_in, conv_w, pool_w, pool_scale)


def mixer_bwd(proj, dmix, conv_w, pool_w, pool_scale, name, ts=256):
    S = proj.shape[0]
    n = ts + 2 * HALO
    nt = S // ts
    tn_dims = (((0,), (0,)), ((), ()))
    nt_dims = (((1,), (1,)), ((), ()))

    def body(pm_ref, pb_ref, pa_ref, dm_ref, da_ref, cw_ref, pw_ref, ps_ref, o_ref, dcw_ref, dpw_ref, dps_ref):
        i = pl.program_id(0)
        last = i == nt - 1
        before = jnp.where(i > 0, pb_ref[...], 0.0)
        after = jnp.where(last, 0.0, pa_ref[...])
        ext = jnp.concatenate([before, pm_ref[...], after], axis=0)
        dafter = jnp.where(last, 0.0, da_ref[...])
        dext = jnp.concatenate([jnp.zeros((HALO, D_MODEL), F32), dm_ref[...], dafter], axis=0)
        cw = cw_ref[...]
        main = slice(HALO, HALO + ts)

        @pl.when(i == 0)
        def _():
            dcw_ref[...] = jnp.zeros_like(dcw_ref)
            dpw_ref[...] = jnp.zeros_like(dpw_ref)
            dps_ref[...] = jnp.zeros_like(dps_ref)

        h, gb, gc = ext[:, 0:A_WIDTH], ext[:, A_WIDTH:2 * A_WIDTH], ext[:, 2 * A_WIDTH:3 * A_WIDTH]
        z = gc * h
        z1, z2 = _shift_down(z, 1), _shift_down(z, 2)
        cz = (z2 * cw[0:1] + z1 * cw[1:2]) + z * cw[2:3]
        dya = dext[:, 0:A_WIDTH]
        dcz = dya * gb
        dz = dcz * cw[2:3] + _shift_up(dcz, 1) * cw[1:2] + _shift_up(dcz, 2) * cw[0:1]
        o_ref[:, 0:A_WIDTH] = (dz * gc)[main].astype(BF16)
        o_ref[:, A_WIDTH:2 * A_WIDTH] = (dya * cz)[main].astype(BF16)
        o_ref[:, 2 * A_WIDTH:3 * A_WIDTH] = (dz * h)[main].astype(BF16)
        dczm = dcz[main]
        dcw_ref[0:1, :] += jnp.sum(dczm * z2[main], axis=0, keepdims=True)
        dcw_ref[1:2, :] += jnp.sum(dczm * z1[main], axis=0, keepdims=True)
        dcw_ref[2:3, :] += jnp.sum(dczm * z[main], axis=0, keepdims=True)

        for g, k in enumerate(POOL_WINDOWS):
            lo = 3 * A_WIDTH + g * POOL_GROUP
            cols = slice(g * POOL_GROUP, (g + 1) * POOL_GROUP)
            p = ext[:, lo:lo + POOL_GROUP]
            w = p
            s = 1
            while s < k:
                w = w + _shift_down(w, s)
                s *= 2
            cnt = _window_count(i * ts - HALO, n, k)
            pooled = (w / cnt - p)[main].astype(BF16)
            dyb = dext[:, A_WIDTH + g * POOL_GROUP:A_WIDTH + (g + 1) * POOL_GROUP]
            e = dyb * ps_ref[:, cols]
            pre = jnp.dot(pooled, pw_ref[g], preferred_element_type=F32)
            dps_ref[:, cols] += jnp.sum(dyb[main] * pre, axis=0, keepdims=True)
            dpw_ref[g] += lax.dot_general(pooled, e[main].astype(BF16), tn_dims, preferred_element_type=F32)
            dpooled = lax.dot_general(e.astype(BF16), pw_ref[g], nt_dims, preferred_element_type=F32)
            q = dpooled / cnt
            a = q
            s = 1
            while s < k:
                a = a + _shift_up(a, s)
                s *= 2
            o_ref[:, lo:lo + POOL_GROUP] = (a - dpooled)[main].astype(BF16)

    hb = ts // HALO
    nh = S // HALO
    before_map = lambda i: (jnp.maximum(i * hb - 1, 0), 0)
    after_map = lambda i: (jnp.minimum((i + 1) * hb, nh - 1), 0)
    full = lambda *shape: pl.BlockSpec(shape, lambda i: (0,) * len(shape))
    return pl.pallas_call(
        body, name=name, grid=(nt,),
        in_specs=[
            pl.BlockSpec((ts, EVEN_IN), lambda i: (i, 0)),
            pl.BlockSpec((HALO, EVEN_IN), before_map),
            pl.BlockSpec((HALO, EVEN_IN), after_map),
            pl.BlockSpec((ts, D_MODEL), lambda i: (i, 0)),
            pl.BlockSpec((HALO, D_MODEL), after_map),
            full(3, A_WIDTH), full(4, POOL_GROUP, POOL_GROUP), full(1, 4 * POOL_GROUP),
        ],
        out_specs=[pl.BlockSpec((ts, EVEN_IN), lambda i: (i, 0)), full(3, A_WIDTH), full(4, POOL_GROUP, POOL_GROUP),
                   full(1, 4 * POOL_GROUP)],
        out_shape=[_sds((S, EVEN_IN), BF16), _sds((3, A_WIDTH), F32), _sds((4, POOL_GROUP, POOL_GROUP), F32),
                   _sds((1, 4 * POOL_GROUP), F32)],
        compiler_params=_params("arbitrary"),
    )(proj, proj, proj, dmix, dmix, conv_w, pool_w, pool_scale)


FFN_HALO = 16
FFN_TC = 1408


GLU_CHUNKS = ((0, 512), (512, 512), (1024, 384))


def up_glu_fwd(xn, w_up, conv_w, conv_b, name, tm=512):
    S, K = xn.shape
    nc = D_FF // FFN_TC

    def body(xm_ref, xb_ref, wg_ref, wu_ref, cwg_ref, cwu_ref, cbg_ref, cbu_ref, pg_ref, pu_ref, ug_ref, uu_ref, o_ref):
        before = jnp.where(pl.program_id(1) > 0, xb_ref[...], jnp.zeros_like(xb_ref))
        rows = jnp.concatenate([before, xm_ref[...]], axis=0)
        for lo, width in GLU_CHUNKS:
            cols = slice(lo, lo + width)
            pre_g = jnp.dot(rows, wg_ref[:, cols], preferred_element_type=F32)
            pre_u = jnp.dot(rows, wu_ref[:, cols], preferred_element_type=F32)
            gate = _conv3(pre_g, cwg_ref[:, cols])[FFN_HALO:] + cbg_ref[:, cols]
            upv = _conv3(pre_u, cwu_ref[:, cols])[FFN_HALO:] + cbu_ref[:, cols]
            pg_ref[:, cols] = pre_g[FFN_HALO:].astype(BF16)
            pu_ref[:, cols] = pre_u[FFN_HALO:].astype(BF16)
            ug_ref[:, cols] = gate.astype(BF16)
            uu_ref[:, cols] = upv.astype(BF16)
            o_ref[:, cols] = ((gate * (1.0 / (1.0 + jnp.exp(-gate)))) * upv).astype(BF16)

    hb = tm // FFN_HALO
    wspec = lambda off: pl.BlockSpec((None, None, K, FFN_TC), lambda j, m: (j + off, 0, 0, 0))
    cw = lambda off: pl.BlockSpec((3, FFN_TC), lambda j, m: (0, j + off))
    cb = lambda off: pl.BlockSpec((1, FFN_TC), lambda j, m: (0, j + off))
    out = pl.BlockSpec((tm, FFN_TC), lambda j, m: (m, j))
    pg, pu, ug, uu, act = pl.pallas_call(
        body, name=name, grid=(nc, S // tm),
        in_specs=[pl.BlockSpec((tm, K), lambda j, m: (m, 0)),
                  pl.BlockSpec((FFN_HALO, K), lambda j, m: (jnp.maximum(m * hb - 1, 0), 0)),
                  wspec(0), wspec(nc), cw(0), cw(nc), cb(0), cb(nc)],
        out_specs=[out] * 5, out_shape=[_sds((S, D_FF), BF16)] * 5,
        compiler_params=_params("parallel", "parallel"),
    )(xn, xn, w_up, w_up, conv_w, conv_w, conv_b, conv_b)
    return (pg, pu), (ug, uu), act


def glu_bwd(up, u, da, conv_w, name, ts=256):
    S = up[0].shape[0]
    nc = D_FF // FFN_TC
    nt = S // ts
    W = 2 * D_FF

    def body(xg_ref, xu_ref, gm_ref, ga_ref, um_ref, ua_ref, dm_ref, da_ref, cw_ref, dx_ref, dcw_ref, dcb_ref):
        i = pl.program_id(0)
        last = i == nt - 1

        @pl.when(i == 0)
        def _():
            dcw_ref[...] = jnp.zeros_like(dcw_ref)
            dcb_ref[...] = jnp.zeros_like(dcb_ref)

        def rows(m_ref, a_ref, cols):
            return jnp.concatenate([m_ref[:, cols], a_ref[:, cols]], axis=0).astype(F32)

        def back(d, x, cols):
            cw = cw_ref[:, cols]
            d1, d2 = _shift_up(d, 1), _shift_up(d, 2)
            dx_ref[:, cols] = ((d * cw[2:3] + d1 * cw[1:2]) + d2 * cw[0:1])[:ts].astype(BF16)
            dcb_ref[:, cols] += jnp.sum(d[:ts], axis=0, keepdims=True)
            dcw_ref[0:1, cols] += jnp.sum(d2[:ts] * x, axis=0, keepdims=True)
            dcw_ref[1:2, cols] += jnp.sum(d1[:ts] * x, axis=0, keepdims=True)
            dcw_ref[2:3, cols] += jnp.sum(d[:ts] * x, axis=0, keepdims=True)

        for c in range(nc):
            cols = slice(c * FFN_TC, (c + 1) * FFN_TC)
            ug, uu = rows(gm_ref, ga_ref, cols), rows(um_ref, ua_ref, cols)
            dae = rows(dm_ref, da_ref, cols)
            dae = jnp.where(last & (lax.broadcasted_iota(jnp.int32, dae.shape, 0) >= ts), 0.0, dae)
            sg = 1.0 / (1.0 + jnp.exp(-ug))
            duu = dae * (ug * sg)
            dug = (dae * uu) * (sg * (1.0 + ug * (1.0 - sg)))
            back(dug, xg_ref[:, cols].astype(F32), cols)
            back(duu, xu_ref[:, cols].astype(F32), slice(D_FF + c * FFN_TC, D_FF + (c + 1) * FFN_TC))

    hb = ts // FFN_HALO
    nh = S // FFN_HALO
    after_map = lambda i: (jnp.minimum((i + 1) * hb, nh - 1), 0)
    main = pl.BlockSpec((ts, D_FF), lambda i: (i, 0))
    after = pl.BlockSpec((FFN_HALO, D_FF), after_map)
    return pl.pallas_call(
        body, name=name, grid=(nt,),
        in_specs=[main, main, main, after, main, after, main, after, pl.BlockSpec((3, W), lambda i: (0, 0))],
        out_specs=[pl.BlockSpec((ts, W), lambda i: (i, 0)), pl.BlockSpec((3, W), lambda i: (0, 0)),
                   pl.BlockSpec((1, W), lambda i: (0, 0))],
        out_shape=[_sds((S, W), BF16), _sds((3, W), F32), _sds((1, W), F32)],
        compiler_params=_params("arbitrary"),
    )(up[0], up[1], u[0], u[0], u[1], u[1], da, da, conv_w)


MEAN_GROUP = 256


def _head_mean_matrix():
    h = np.arange(MEAN_GROUP) // HEAD_DIM
    return jnp.asarray((h[:, None] == h[None, :]).astype(np.float32) / HEAD_DIM, dtype=BF16)


def _head_mean(v, gm):
    vb = v.astype(BF16)
    return jnp.concatenate([jnp.dot(vb[:, c:c + MEAN_GROUP], gm, preferred_element_type=F32)
                            for c in range(0, v.shape[1], MEAN_GROUP)], axis=1)


def qkv_qknorm_fwd(xn, w_qkv, gqk, name, tm=1024):
    S, K = xn.shape
    J, _, _, Ns = w_qkv.shape
    gains = gqk.reshape(1, 3 * D_MODEL)

    def body(x_ref, w_ref, g_ref, gm_ref, raw_ref, o_ref):
        first_col = pl.program_id(0) * Ns
        acc = jnp.dot(x_ref[...], w_ref[...], preferred_element_type=F32)
        raw_ref[...] = acc
        gm = gm_ref[...]
        for c in range(0, Ns, MEAN_GROUP):
            cols = slice(c, c + MEAN_GROUP)
            x = acc[:, cols]
            mean = jnp.dot((x * x).astype(BF16), gm, preferred_element_type=F32)
            normed = (x * lax.rsqrt(mean + EPS)) * g_ref[:, cols]
            o_ref[:, cols] = jnp.where(first_col + c >= 2 * D_MODEL, x, normed).astype(BF16)

    return pl.pallas_call(
        body, name=name, grid=(J, S // tm),
        in_specs=[pl.BlockSpec((tm, K), lambda j, m: (m, 0)), pl.BlockSpec((None, None, K, Ns), lambda j, m: (j, 0, 0, 0)),
                  pl.BlockSpec((1, Ns), lambda j, m: (0, j)), pl.BlockSpec((MEAN_GROUP, MEAN_GROUP), lambda j, m: (0, 0))],
        out_specs=[pl.BlockSpec((tm, Ns), lambda j, m: (m, j))] * 2,
        out_shape=[_sds((S, J * Ns), F32), _sds((S, J * Ns), BF16)], compiler_params=_params("parallel", "parallel"),
    )(xn, w_qkv, gains, _head_mean_matrix())


def qknorm_bwd(qkv, dq, dk, dv, gqk, name, ts=256):
    S = qkv.shape[0]

    def body(x_ref, dq_ref, dk_ref, dv_ref, g_ref, gm_ref, o_ref, dg_ref):
        @pl.when(pl.program_id(0) == 0)
        def _():
            dg_ref[...] = jnp.zeros_like(dg_ref)

        gm = gm_ref[...]
        for part, d_ref in enumerate((dq_ref, dk_ref)):
            cols = slice(part * D_MODEL, (part + 1) * D_MODEL)
            x = x_ref[:, cols]
            d = d_ref[...]
            r = lax.rsqrt(_head_mean(x * x, gm) + EPS)
            gx = d * g_ref[part]
            o_ref[:, cols] = (r * gx - x * ((r * r * r) * _head_mean(gx * x, gm))).astype(BF16)
            dg_ref[part] += jnp.sum(d * (x * r), axis=0, keepdims=True)
        o_ref[:, 2 * D_MODEL:] = dv_ref[...].astype(BF16)

    row = pl.BlockSpec((ts, D_MODEL), lambda i: (i, 0))
    wide = pl.BlockSpec((ts, 3 * D_MODEL), lambda i: (i, 0))
    gains = pl.BlockSpec((3, 1, D_MODEL), lambda i: (0, 0, 0))
    return pl.pallas_call(
        body, name=name, grid=(S // ts,),
        in_specs=[wide, row, row, row, gains, pl.BlockSpec((MEAN_GROUP, MEAN_GROUP), lambda i: (0, 0))],
        out_specs=[wide, gains],
        out_shape=[_sds((S, 3 * D_MODEL), BF16), _sds((3, 1, D_MODEL), F32)],
        compiler_params=_params("arbitrary"),
    )(qkv, dq, dk, dv, gqk, _head_mean_matrix())


RESIDUES = 16


def _block_order(dil):
    runs = RESIDUES // dil
    slot = np.arange(ATT_BLOCK)
    return (slot % (ATT_BLOCK // runs)) * runs + slot // (ATT_BLOCK // runs)


def _bucket_tables():
    n = ATT_BLOCK
    max_exact = N_REL_BUCKETS // 2
    buckets, valids = [], []
    for _, dil in DILATED_PAIRS:
        order = _block_order(dil)
        a = order[:, None]
        c = np.concatenate([order, n + order])[None, :]
        first_half = (np.arange(2 * n) < n)[None, :]
        rel = a + n - c
        band = (rel >= 0) & (rel <= n)
        dist = np.clip(rel, 0, n) * dil
        dd = np.maximum(dist, 1).astype(np.float32)
        large = max_exact + (np.log(dd / np.float32(max_exact)) / np.float32(math.log(REL_MAX_DISTANCE / max_exact))
                             * np.float32(N_REL_BUCKETS - max_exact)).astype(np.int32)
        large = np.minimum(large, N_REL_BUCKETS - 1)
        buckets.append(np.where(dist < max_exact, dist, large).reshape(1, -1))
        valids.append(np.stack([(band & ~first_half).reshape(1, -1), band.reshape(1, -1)]))
    return np.stack(buckets).astype(np.int32), np.stack(valids).astype(np.int32)


BIAS_CHUNK = 8192


def _split3(x):
    a = x.astype(BF16)
    r = x - a.astype(F32)
    b = r.astype(BF16)
    c = (r - b.astype(F32)).astype(BF16)
    return a, b, c


def bias_expand(rel_bias_t, name):
    bucket, valid = _bucket_tables()
    nq = bucket.shape[-1]

    def body(t_ref, b_ref, v_ref, o_ref):
        onehot = (lax.broadcasted_iota(jnp.int32, (N_REL_BUCKETS, BIAS_CHUNK), 0) == b_ref[...]).astype(BF16)
        acc = None
        for term in _split3(t_ref[...]):
            p = jnp.dot(term, onehot, preferred_element_type=F32)
            acc = p if acc is None else acc + p
        o_ref[...] = jnp.where(v_ref[...] > 0, acc, MASK_VALUE)

    return pl.pallas_call(
        body, name=name, grid=(3, 2, nq // BIAS_CHUNK),
        in_specs=[pl.BlockSpec((N_HEADS, N_REL_BUCKETS), lambda b, v, c: (0, 0)),
                  pl.BlockSpec((None, 1, BIAS_CHUNK), lambda b, v, c: (b, 0, c)),
                  pl.BlockSpec((None, None, 1, BIAS_CHUNK), lambda b, v, c: (b, v, 0, c))],
        out_specs=pl.BlockSpec((None, None, N_HEADS, BIAS_CHUNK), lambda b, v, c: (b, v, 0, c)),
        out_shape=_sds((3, 2, N_HEADS, nq), F32), compiler_params=_params("parallel", "parallel", "parallel"),
    )(rel_bias_t, jnp.asarray(bucket), jnp.asarray(valid))


def bias_reduce(dbias, name):
    bucket, _ = _bucket_tables()
    nq = bucket.shape[-1]
    dims = (((1,), (1,)), ((), ()))

    def body(d_ref, b_ref, o_ref):
        onehot = (lax.broadcasted_iota(jnp.int32, (N_REL_BUCKETS, BIAS_CHUNK), 0) == b_ref[...]).astype(BF16)
        acc = None
        for term in _split3(d_ref[...]):
            p = lax.dot_general(term, onehot, dims, preferred_element_type=F32)
            acc = p if acc is None else acc + p

        @pl.when(pl.program_id(1) == 0)
        def _():
            o_ref[...] = acc

        @pl.when(pl.program_id(1) > 0)
        def _():
            o_ref[...] += acc

    return pl.pallas_call(
        body, name=name, grid=(3, nq // BIAS_CHUNK),
        in_specs=[pl.BlockSpec((None, N_HEADS, BIAS_CHUNK), lambda b, c: (b, 0, c)),
                  pl.BlockSpec((None, 1, BIAS_CHUNK), lambda b, c: (b, 0, c))],
        out_specs=pl.BlockSpec((None, N_HEADS, N_REL_BUCKETS), lambda b, c: (b, 0, 0)),
        out_shape=_sds((3, N_HEADS, N_REL_BUCKETS), F32), compiler_params=_params("parallel", "arbitrary"),
    )(dbias, jnp.asarray(bucket))


PAIR = 2 * HEAD_DIM
N_PAIRS = N_HEADS // 2
_NT = (((1,), (1,)), ((), ()))
_TN = (((0,), (0,)), ((), ()))


def _low_lanes(shape):
    return lax.broadcasted_iota(jnp.int32, shape, 1) < HEAD_DIM


ATTN_VMEM_LIMIT_BYTES = 56 * 1024 * 1024
BRANCH_ORDER = (2, 1, 0)


def _regroup(dst, src, L16):
    for r in range(RESIDUES):
        dst[pl.ds(r * L16, L16), :] = src[pl.ds(r, L16, stride=RESIDUES), :]


def _ungroup(dst, src, L16):
    for r in range(RESIDUES):
        dst[pl.ds(r, L16, stride=RESIDUES), :] = src[pl.ds(r * L16, L16), :]


def _branch_geometry(branch, S):
    dil = DILATED_PAIRS[branch][1]
    runs = RESIDUES // dil
    return dil, runs, ATT_BLOCK // runs, S // dil // ATT_BLOCK


def _block_rows(it, branch, S):
    dil, runs, run_len, n_blocks = _branch_geometry(branch, S)
    L16 = S // RESIDUES
    r, b = it // n_blocks, it % n_blocks
    prev = jnp.maximum(b - 1, 0)
    cur_rows = [pl.multiple_of((j * dil + r) * L16 + run_len * b, 8) for j in range(runs)]
    prev_rows = [pl.multiple_of((j * dil + r) * L16 + run_len * prev, 8) for j in range(runs)]
    return cur_rows, prev_rows, jnp.minimum(b, 1)


def _load_block(ref, rows, run_len):
    parts = [ref[pl.ds(o, run_len), :] for o in rows]
    return parts[0] if len(parts) == 1 else jnp.concatenate(parts, axis=0)


def _store_block(ref, rows, run_len, value, add=False):
    for j, o in enumerate(rows):
        part = value[j * run_len:(j + 1) * run_len]
        if add:
            ref[pl.ds(o, run_len), :] += part
        else:
            ref[pl.ds(o, run_len), :] = part


ATTN_FWD_UNROLL = 8
ATTN_BWD_UNROLL = 4


def _stack_heads(x, low):
    zero = jnp.zeros_like(x)
    return jnp.concatenate([jnp.where(low, x, zero), jnp.where(low, zero, x)], axis=0)


def _unstack_heads(y, low):
    return jnp.where(low, y[:ATT_BLOCK], y[ATT_BLOCK:])


def attn_fwd(qkvn, bias, name):
    S = qkvn.shape[0]
    L16 = S // RESIDUES
    n_iter = S // ATT_BLOCK

    def body(q_ref, k_ref, v_ref, b_ref, o_ref, lse_ref, stage, qp, kp, vp, acc_s, m_s, l_s):
        for src, dst in ((q_ref, qp), (k_ref, kp), (v_ref, vp)):
            stage[...] = src[...].astype(F32)
            _regroup(dst, stage, L16)
        low = _low_lanes((ATT_BLOCK, PAIR))

        for branch in BRANCH_ORDER:
            _, _, run_len, _ = _branch_geometry(branch, S)
            first = branch == BRANCH_ORDER[0]

            def step(it, carry, branch=branch, run_len=run_len, first=first):
                cur, prev, variant = _block_rows(it, branch, S)
                q = _load_block(qp, cur, run_len).astype(BF16)
                k = jnp.concatenate([_load_block(kp, prev, run_len), _load_block(kp, cur, run_len)], axis=0).astype(BF16)
                v = jnp.concatenate([_load_block(vp, prev, run_len), _load_block(vp, cur, run_len)], axis=0).astype(BF16)
                s = lax.dot_general(_stack_heads(q, low), k, _NT, preferred_element_type=F32) * (HEAD_DIM ** -0.5)
                s = s + b_ref[2 * branch + variant].reshape(2 * ATT_BLOCK, 2 * ATT_BLOCK)
                mx = jnp.max(s, axis=-1, keepdims=True)
                p = jnp.exp(s - mx)
                den = jnp.sum(p, axis=-1, keepdims=True)
                pv = jnp.dot(p.astype(BF16), v, preferred_element_type=F32)
                acc = _unstack_heads(pv, low)
                m = _unstack_heads(mx, low)
                l = _unstack_heads(den, low)
                if not first:
                    m_old = _load_block(m_s, cur, run_len)
                    m_new = jnp.maximum(m_old, m)
                    a_old, a_new = jnp.exp(m_old - m_new), jnp.exp(m - m_new)
                    acc = _load_block(acc_s, cur, run_len) * a_old + acc * a_new
                    l = _load_block(l_s, cur, run_len) * a_old + l * a_new
                    m = m_new
                _store_block(acc_s, cur, run_len, acc)
                _store_block(m_s, cur, run_len, m)
                _store_block(l_s, cur, run_len, l)
                return carry

            lax.fori_loop(0, n_iter, step, 0, unroll=ATTN_FWD_UNROLL)

        acc_s[...] = acc_s[...] / l_s[...]
        _ungroup(stage, acc_s, L16)
        o_ref[...] = stage[...].astype(BF16)
        m_s[...] = m_s[...] + jnp.log(l_s[...])
        _ungroup(lse_ref, m_s, L16)

    col = lambda part: pl.BlockSpec((S, PAIR), lambda hp: (0, part * N_PAIRS + hp))
    out = pl.BlockSpec((S, PAIR), lambda hp: (0, hp))
    return pl.pallas_call(
        body, name=name, grid=(N_PAIRS,),
        in_specs=[col(0), col(1), col(2), pl.BlockSpec((6, 2, ATT_BLOCK, 2 * ATT_BLOCK), lambda hp: (0, hp, 0, 0))],
        out_specs=[out, out], out_shape=[_sds((S, D_MODEL), BF16), _sds((S, D_MODEL), F32)],
        scratch_shapes=[pltpu.VMEM((S, PAIR), F32)] * 7,
        compiler_params=pltpu.CompilerParams(dimension_semantics=("parallel",), vmem_limit_bytes=ATTN_VMEM_LIMIT_BYTES),
    )(qkvn, qkvn, qkvn, bias)


def attn_bwd(qkvn, att, datt, lse, bias, name):
    S = qkvn.shape[0]
    L16 = S // RESIDUES
    n_iter = S // ATT_BLOCK
    TILE = 512

    def body(q_ref, k_ref, v_ref, o_ref, do_ref, lse_ref, b_ref, dq_ref, dk_ref, dv_ref, db_ref,
             qp, kp, vp, dop, ldp, dqp, dkp, dvp):
        stage = dqp
        for src, dst in ((q_ref, qp), (k_ref, kp), (v_ref, vp), (do_ref, dop)):
            stage[...] = src[...].astype(F32)
            _regroup(dst, stage, L16)

        def pack(i, carry):
            rows = pl.ds(pl.multiple_of(i * TILE, TILE), TILE)
            low = _low_lanes((TILE, PAIR))
            lane = lax.broadcasted_iota(jnp.int32, (TILE, PAIR), 1)
            prod = do_ref[rows, :].astype(F32) * o_ref[rows, :].astype(F32)
            d0 = jnp.sum(jnp.where(low, prod, 0.0), axis=-1, keepdims=True)
            d1 = jnp.sum(jnp.where(low, 0.0, prod), axis=-1, keepdims=True)
            stage[rows, :] = jnp.where((lane & (HEAD_DIM // 2)) == 0, lse_ref[rows, :], jnp.where(low, d0, d1))
            return carry

        lax.fori_loop(0, S // TILE, pack, 0)
        _regroup(ldp, stage, L16)
        dqp[...] = jnp.zeros_like(dqp)
        dkp[...] = jnp.zeros_like(dkp)
        dvp[...] = jnp.zeros_like(dvp)
        db_ref[...] = jnp.zeros_like(db_ref)
        low = _low_lanes((ATT_BLOCK, PAIR))

        for branch in BRANCH_ORDER:
            _, _, run_len, _ = _branch_geometry(branch, S)

            def step(it, carry, branch=branch, run_len=run_len):
                cur, prev, variant = _block_rows(it, branch, S)
                q = _load_block(qp, cur, run_len).astype(BF16)
                dout = _load_block(dop, cur, run_len).astype(BF16)
                ld = _load_block(ldp, cur, run_len)
                k = jnp.concatenate([_load_block(kp, prev, run_len), _load_block(kp, cur, run_len)], axis=0).astype(BF16)
                v = jnp.concatenate([_load_block(vp, prev, run_len), _load_block(vp, cur, run_len)], axis=0).astype(BF16)
                half = HEAD_DIM // 2
                lse2 = jnp.concatenate([ld[:, 0:1], ld[:, HEAD_DIM:HEAD_DIM + 1]], axis=0)
                delta2 = jnp.concatenate([ld[:, half:half + 1], ld[:, HEAD_DIM + half:HEAD_DIM + half + 1]], axis=0)
                q2, do2 = _stack_heads(q, low), _stack_heads(dout, low)
                s = lax.dot_general(q2, k, _NT, preferred_element_type=F32) * (HEAD_DIM ** -0.5)
                p = jnp.exp(s + b_ref[2 * branch + variant].reshape(2 * ATT_BLOCK, 2 * ATT_BLOCK) - lse2)
                dp = lax.dot_general(do2, v, _NT, preferred_element_type=F32)
                ds = p * (dp - delta2)
                db_ref[branch] += ds.reshape(2, ATT_BLOCK, 2 * ATT_BLOCK)
                dsb = (ds * (HEAD_DIM ** -0.5)).astype(BF16)
                dq = _unstack_heads(jnp.dot(dsb, k, preferred_element_type=F32), low)
                dk = lax.dot_general(dsb, q2, _TN, preferred_element_type=F32)
                dv = lax.dot_general(p.astype(BF16), do2, _TN, preferred_element_type=F32)
                _store_block(dqp, cur, run_len, dq, add=True)
                _store_block(dkp, prev, run_len, dk[:ATT_BLOCK], add=True)
                _store_block(dvp, prev, run_len, dv[:ATT_BLOCK], add=True)
                _store_block(dkp, cur, run_len, dk[ATT_BLOCK:], add=True)
                _store_block(dvp, cur, run_len, dv[ATT_BLOCK:], add=True)
                return carry

            lax.fori_loop(0, n_iter, step, 0, unroll=ATTN_BWD_UNROLL)

        _ungroup(dq_ref, dqp, L16)
        _ungroup(dk_ref, dkp, L16)
        _ungroup(dv_ref, dvp, L16)

    col = lambda part: pl.BlockSpec((S, PAIR), lambda hp: (0, part * N_PAIRS + hp))
    one = pl.BlockSpec((S, PAIR), lambda hp: (0, hp))
    return pl.pallas_call(
        body, name=name, grid=(N_PAIRS,),
        in_specs=[col(0), col(1), col(2), one, one, one,
                  pl.BlockSpec((6, 2, ATT_BLOCK, 2 * ATT_BLOCK), lambda hp: (0, hp, 0, 0))],
        out_specs=[one, one, one, pl.BlockSpec((3, 2, ATT_BLOCK, 2 * ATT_BLOCK), lambda hp: (0, hp, 0, 0))],
        out_shape=[_sds((S, D_MODEL), F32)] * 3 + [_sds((3, N_HEADS, ATT_BLOCK, 2 * ATT_BLOCK), F32)],
        scratch_shapes=[pltpu.VMEM((S, PAIR), F32)] * 8,
        compiler_params=pltpu.CompilerParams(dimension_semantics=("parallel",), vmem_limit_bytes=ATTN_VMEM_LIMIT_BYTES),
    )(qkvn, qkvn, qkvn, att, datt, lse, bias)


def adamw(w, g, m, v, name):
    n, R, C = w.shape

    def body(w_ref, g_ref, m_ref, v_ref, d_ref, nm_ref, nv_ref, go_ref):
        gv = g_ref[...]
        go_ref[...] = gv
        m2 = ADAM_B1 * m_ref[...] + (1.0 - ADAM_B1) * gv
        v2 = ADAM_B2 * v_ref[...] + (1.0 - ADAM_B2) * (gv * gv)
        m_hat = m2 / (1.0 - ADAM_B1 ** ADAM_STEP)
        v_hat = v2 / (1.0 - ADAM_B2 ** ADAM_STEP)
        d_ref[...] = -ADAM_LR * (m_hat / (jnp.sqrt(v_hat) + ADAM_EPS) + ADAM_WD * w_ref[...])
        nm_ref[...] = m2
        nv_ref[...] = v2

    tr = R
    while tr * C * 4 > ELEMENTWISE_BLOCK_BYTES and tr % 16 == 0:
        tr //= 2
    spec = pl.BlockSpec((None, tr, C), lambda i, r: (i, r, 0))
    return pl.pallas_call(
        body, name=name, grid=(n, R // tr), in_specs=[spec] * 4, out_specs=[spec] * 4,
        out_shape=[_sds((n, R, C), F32)] * 4, compiler_params=_params("parallel", "parallel"),
    )(w, _in_hbm(g), m, v)


ANY = pl.BlockSpec(memory_space=pl.ANY)


def _coords():
    return lax.axis_index("x"), lax.axis_index("y"), lax.axis_index("c")


def _other_chips(mx, my):
    return [(1 - mx, my), (mx, 1 - my), (1 - mx, 1 - my)]


def _remote(src, dst, send, recv, dev):
    return pltpu.make_async_remote_copy(src_ref=src, dst_ref=dst, send_sem=send, recv_sem=recv, device_id=dev,
                                        device_id_type=MESH)


HBM =pl.BlockSpec(memory_space=pltpu.HBM)
SEM = pl.BlockSpec(memory_space=pltpu.SEMAPHORE)
_SPLIT_COPY = pltpu.CompilerParams(has_side_effects=pltpu.SideEffectType.DATAFLOW_SIDE_EFFECTING)


def _in_hbm(a):
    return pltpu.with_memory_space_constraint(a, pltpu.HBM)


def cast_into_slot(w, layer, chip_core, name, dtype=BF16):
    _, _, hR, C = w.shape

    def body(s_ref, w_ref, o_ref):
        del s_ref
        o_ref[...] = w_ref[...].astype(dtype)

    grid_spec = pltpu.PrefetchScalarGridSpec(
        num_scalar_prefetch=1, grid=(2,),
        in_specs=[pl.BlockSpec((None, None, hR, C), lambda h, s: (layer, h, 0, 0))],
        out_specs=pl.BlockSpec((None, None, hR, C), lambda h, s: (s[0], h, 0, 0)))
    return pl.pallas_call(body, name=name, grid_spec=grid_spec, out_shape=_sds((N_CHIPS, 2, hR, C), dtype),
                          compiler_params=_params("parallel"))(chip_core, w)


def gather_start(lands, groups, name):
    n = len(lands)
    n_groups = len(groups)

    def body(*refs):
        ins = refs[:n]
        sems = refs[n:n + 2 * n_groups]
        token = refs[-1]
        mx, my, mc = _coords()
        chip = 2 * mx + my
        for g, members in enumerate(groups):
            send, recv = sems[2 * g], sems[2 * g + 1]
            for i, a in enumerate(members):
                mine = ins[a].at[chip, mc]
                for k, (px, py) in enumerate(_other_chips(mx, my)):
                    _remote(mine, mine, send.at[3 * i + k], recv.at[3 * i + k], (px, py, mc)).start()
        token[...] = jnp.zeros_like(token)

    sem_shapes = []
    for members in groups:
        sem_shapes += [pltpu.SemaphoreType.DMA((3 * len(members),))] * 2
    outs = pl.pallas_call(
        body, name=name, in_specs=[HBM] * n,
        out_specs=[SEM] * (2 * n_groups) + [HBM] * n + [pl.BlockSpec(memory_space=pltpu.VMEM)],
        out_shape=sem_shapes + [pltpu.HBM(a.shape, a.dtype) for a in lands] + [_sds((SUBLANES, LANES), F32)],
        input_output_aliases={a: 2 * n_groups + a for a in range(n)}, compiler_params=_SPLIT_COPY,
    )(*[_in_hbm(a) for a in lands])
    sems = [(outs[2 * g], outs[2 * g + 1]) for g in range(n_groups)]
    return sems, list(outs[2 * n_groups:2 * n_groups + n]), outs[-1]


def gather_forward(lands, sems, after, name):
    n = len(lands)

    def body(*refs):
        ins = refs[:n]
        send, recv = refs[n], refs[n + 1]
        fsend, frecv = refs[n + 3], refs[n + 4]
        mx, my, mc = _coords()
        for i in range(n):
            for k, (px, py) in enumerate(_other_chips(mx, my)):
                landed = ins[i].at[2 * px + py, mc]
                cp = _remote(landed, landed, send.at[3 * i + k], recv.at[3 * i + k], (px, py, mc))
                cp.wait_send()
                cp.wait_recv()
                _remote(landed, landed, fsend.at[3 * i + k], frecv.at[3 * i + k], (mx, my, 1 - mc)).start()

    outs = pl.pallas_call(
        body, name=name, in_specs=[HBM] * n + [SEM, SEM, ANY], out_specs=[SEM, SEM] + [HBM] * n,
        out_shape=[pltpu.SemaphoreType.DMA((3 * n,))] * 2 + [pltpu.HBM(a.shape, a.dtype) for a in lands],
        input_output_aliases={a: 2 + a for a in range(n)}, compiler_params=_SPLIT_COPY,
    )(*lands, sems[0], sems[1], after)
    return (outs[0], outs[1]), list(outs[2:])


def gather_wait(lands, sems, after, name):
    n = len(lands)

    def body(*refs):
        ins = refs[:n]
        fsend, frecv = refs[n], refs[n + 1]
        mx, my, mc = _coords()
        for i in range(n):
            for k, (px, py) in enumerate(_other_chips(mx, my)):
                theirs = ins[i].at[2 * px + py, 1 - mc]
                cp = _remote(theirs, theirs, fsend.at[3 * i + k], frecv.at[3 * i + k], (mx, my, 1 - mc))
                cp.wait_send()
                cp.wait_recv()

    outs = pl.pallas_call(
        body, name=name, in_specs=[HBM] * n + [SEM, SEM, ANY], out_specs=[HBM] * n,
        out_shape=[pltpu.HBM(a.shape, a.dtype) for a in lands],
        input_output_aliases={a: a for a in range(n)}, compiler_params=_SPLIT_COPY,
    )(*lands, sems[0], sems[1], after)
    return list(outs)


def _peers(mx, my, mc):
    return [(1 - mx if k & 4 else mx, 1 - my if k & 2 else my, 1 - mc if k & 1 else mc) for k in range(1, N_DEV)]


def devices_start(x, name):
    def body(x_ref, land_ref, send, recv, x_thru, land_thru):
        mx, my, mc = _coords()
        me = 4 * mx + 2 * my + mc
        for k, peer in enumerate(_peers(mx, my, mc)):
            _remote(x_ref, land_ref.at[me], send.at[k], recv.at[k], peer).start()

    land = lax.empty((N_DEV,) + x.shape, x.dtype)
    outs = pl.pallas_call(
        body, name=name, in_specs=[HBM, HBM], out_specs=[SEM, SEM, HBM, HBM],
        out_shape=[pltpu.SemaphoreType.DMA((N_DEV - 1,))] * 2 + [pltpu.HBM(x.shape, x.dtype), pltpu.HBM(land.shape, x.dtype)],
        input_output_aliases={0: 2, 1: 3}, compiler_params=_SPLIT_COPY,
    )(_in_hbm(x), _in_hbm(land))
    return (outs[0], outs[1]), outs[2], outs[3]


def devices_wait(x, land, sems, after, name):
    def body(x_ref, land_ref, send, recv, after_ref, x_thru, land_thru):
        mx, my, mc = _coords()
        for k, (px, py, pc) in enumerate(_peers(mx, my, mc)):
            cp = _remote(x_ref, land_ref.at[4 * px + 2 * py + pc], send.at[k], recv.at[k], (px, py, pc))
            cp.wait_send()
            cp.wait_recv()

    outs = pl.pallas_call(
        body, name=name, in_specs=[HBM, HBM, SEM, SEM, ANY], out_specs=[HBM, HBM],
        out_shape=[pltpu.HBM(x.shape, x.dtype), pltpu.HBM(land.shape, land.dtype)],
        input_output_aliases={0: 0, 1: 1}, compiler_params=_SPLIT_COPY,
    )(x, land, sems[0], sems[1], after)
    return outs[0], outs[1]


def device_sum(land, own, me, name):
    _, R, C = land.shape

    def body(s_ref, l_ref, o_ref_in, o_ref):
        acc = None
        for q in range(N_DEV):
            term = jnp.where(s_ref[0] == q, o_ref_in[...], l_ref[q])
            acc = term if acc is None else acc + term
        o_ref[...] = acc

    grid_spec = pltpu.PrefetchScalarGridSpec(
        num_scalar_prefetch=1, grid=(1,),
        in_specs=[pl.BlockSpec((N_DEV, R, C), lambda i, s: (0, 0, 0)), pl.BlockSpec((R, C), lambda i, s: (0, 0))],
        out_specs=pl.BlockSpec((R, C), lambda i, s: (0, 0)))
    return pl.pallas_call(body, name=name, grid_spec=grid_spec, out_shape=_sds((R, C), F32),
                          compiler_params=_params("arbitrary"))(me, _in_hbm(land), _in_hbm(own))


def reduce_send(grads, name):
    n = len(grads)

    def body(*refs):
        ins, lands = refs[:n], refs[n:2 * n]
        send, recv = refs[2 * n], refs[2 * n + 1]
        mx, my, mc = _coords()
        me = 4 * mx + 2 * my + mc
        for a in range(n):
            for k, (px, py, pc) in enumerate(_peers(mx, my, mc)):
                _remote(ins[a].at[2 * px + py, pc], lands[a].at[me], send.at[7 * a + k], recv.at[7 * a + k], (px, py, pc)).start()

    lands = [lax.empty((N_DEV,) + g.shape[2:], g.dtype) for g in grads]
    outs = pl.pallas_call(
        body, name=name, in_specs=[HBM] * (2 * n), out_specs=[SEM, SEM] + [HBM] * (2 * n),
        out_shape=[pltpu.SemaphoreType.DMA((7 * n,))] * 2 + [pltpu.HBM(a.shape, a.dtype) for a in grads + lands],
        input_output_aliases={a: 2 + a for a in range(2 * n)}, compiler_params=_SPLIT_COPY,
    )(*[_in_hbm(a) for a in grads + lands])
    return (outs[0], outs[1]), list(outs[2:2 + n]), list(outs[2 + n:])


def reduce_wait(grads, lands, sems, after, name):
    n = len(grads)

    def body(*refs):
        ins, zones = refs[:n], refs[n:2 * n]
        send, recv = refs[2 * n], refs[2 * n + 1]
        mx, my, mc = _coords()
        for a in range(n):
            for k, (px, py, pc) in enumerate(_peers(mx, my, mc)):
                cp = _remote(ins[a].at[2 * px + py, pc], zones[a].at[4 * px + 2 * py + pc], send.at[7 * a + k],
                             recv.at[7 * a + k], (px, py, pc))
                cp.wait_send()
                cp.wait_recv()

    outs = pl.pallas_call(
        body, name=name, in_specs=[HBM] * (2 * n) + [SEM, SEM, ANY], out_specs=[HBM] * (2 * n),
        out_shape=[pltpu.HBM(a.shape, a.dtype) for a in grads + lands],
        input_output_aliases={a: a for a in range(2 * n)}, compiler_params=_SPLIT_COPY,
    )(*grads, *lands, sems[0], sems[1], after)
    return list(outs[:n]), list(outs[n:])


def reduce_sum(land, grad, place, name, into=None, layer=None):
    _, hR, C = land.shape
    tr = hR
    while N_DEV * tr * C * 2 > 3 * ELEMENTWISE_BLOCK_BYTES and tr % 32 == 0:
        tr //= 2

    def body(s_ref, l_ref, g_ref, *rest):
        o_ref = rest[-1]
        own = g_ref[...].astype(F32)
        acc = None
        for q in range(N_DEV):
            term = jnp.where(s_ref[2] == q, own, l_ref[q].astype(F32))
            acc = term if acc is None else acc + term
        o_ref[...] = acc

    in_specs = [pl.BlockSpec((N_DEV, tr, C), lambda i, s: (0, i, 0)),
                pl.BlockSpec((None, None, tr, C), lambda i, s: (s[0], s[1], i, 0))]
    args = [place, _in_hbm(land), _in_hbm(grad)]
    aliases = {}
    if layer is None:
        out_spec = pl.BlockSpec((None, tr, C), lambda i, s: (s[1], i, 0))
        out_shape = _sds((2, hR, C), F32)
    else:
        out_spec = pl.BlockSpec((None, None, tr, C), lambda i, s: (layer, s[1], i, 0))
        out_shape = _sds((2, 2, hR, C), F32)
        if into is not None:
            in_specs.append(ANY)
            args.append(into)
            aliases = {3: 0}
    grid_spec = pltpu.PrefetchScalarGridSpec(num_scalar_prefetch=1, grid=(hR // tr,), in_specs=in_specs, out_specs=out_spec)
    return pl.pallas_call(body, name=name, grid_spec=grid_spec, out_shape=out_shape, input_output_aliases=aliases,
                          compiler_params=_params("arbitrary"))(*args)


def join_halves(arrays, name):
    n = len(arrays)
    pieces = [(a, l) for a, arr in enumerate(arrays) for l in (range(arr.shape[0]) if arr.ndim == 4 else [None])]

    def body(*refs):
        ins = refs[:n]
        send, recv = refs[2 * n:]
        mx, my, mc = _coords()

        def half(a, l, h):
            return ins[a].at[h] if l is None else ins[a].at[l, h]

        sends = [_remote(half(a, l, mc), half(a, l, mc), send.at[i], recv.at[i], (mx, my, 1 - mc))
                 for i, (a, l) in enumerate(pieces)]
        for cp in sends:
            cp.start()
        for i, (a, l) in enumerate(pieces):
            theirs = half(a, l, 1 - mc)
            _remote(theirs, theirs, send.at[i], recv.at[i], (mx, my, 1 - mc)).wait_recv()
        for cp in sends:
            cp.wait_send()

    return pl.pallas_call(
        body, name=name, in_specs=[ANY] * n, out_specs=[ANY] * n, out_shape=[_sds(a.shape, a.dtype) for a in arrays],
        input_output_aliases={a: a for a in range(n)},
        scratch_shapes=[pltpu.SemaphoreType.DMA((len(pieces),)), pltpu.SemaphoreType.DMA((len(pieces),))],
    )(*arrays)


LANES = 128
SUBLANES = 8


def _n_rows(shape):
    rows = -(-int(np.prod(shape)) // LANES)
    return -(-rows // SUBLANES) * SUBLANES


def _as_rows(a):
    flat = a.reshape(-1)
    rows = _n_rows(a.shape)
    return jnp.pad(flat, (0, rows * LANES - flat.shape[0])).reshape(rows, LANES)


def _pack(arrays):
    return jnp.concatenate([_as_rows(a) for a in arrays], axis=0)


def _unpack(rows, shapes):
    out, r0 = [], 0
    for s in shapes:
        n = _n_rows(s)
        out.append(rows[r0:r0 + n].reshape(-1)[:int(np.prod(s))].reshape(s))
        r0 += n
    return out


REPLICATED_SMALL = [("rel_bias", (32, 16)), ("even_norm", (1, 1024)), ("even_pool_w", (1, 4, 128, 128)),
                    ("even_pool_scale", (1, 512)), ("odd_q_norm", (1, 64)), ("odd_k_norm", (1, 64)),
                    ("ffn_norm", (2, 1024)), ("ffn_conv_b", (2, 5632))]
SHARDED_SMALL = [("even_conv_w", (1, 3, 128)), ("odd_norm", (1, 256)), ("ffn_conv_w", (2, 3, 1408))]
BIG = ["even_w_in", "even_w_out", "odd_w_qkv", "odd_w_o", "ffn_w_up", "ffn_w_down"]
WEIGHT_ORDER = ["rel_bias", "even_norm", "even_w_in", "even_conv_w", "even_pool_w", "even_pool_scale", "even_w_out",
                "odd_norm", "odd_w_qkv", "odd_q_norm", "odd_k_norm", "odd_w_o", "ffn_norm", "ffn_w_up", "ffn_conv_w",
                "ffn_conv_b", "ffn_w_down"]


def kernel(x, rel_bias, even_norm, even_w_in, even_conv_w, even_pool_w, even_pool_scale, even_w_out, odd_norm, odd_w_qkv, odd_q_norm, odd_k_norm, odd_w_o, ffn_norm, ffn_w_up, ffn_conv_w, ffn_conv_b, ffn_w_down, loss_target, m_rel_bias, m_even_norm, m_even_w_in, m_even_conv_w, m_even_pool_w, m_even_pool_scale, m_even_w_out, m_odd_norm, m_odd_w_qkv, m_odd_q_norm, m_odd_k_norm, m_odd_w_o, m_ffn_norm, m_ffn_w_up, m_ffn_conv_w, m_ffn_conv_b, m_ffn_w_down, v_rel_bias, v_even_norm, v_even_w_in, v_even_conv_w, v_even_pool_w, v_even_pool_scale, v_even_w_out, v_odd_norm, v_odd_w_qkv, v_odd_q_norm, v_odd_k_norm, v_odd_w_o, v_ffn_norm, v_ffn_w_up, v_ffn_conv_w, v_ffn_conv_b, v_ffn_w_down):
    W = dict(rel_bias=rel_bias, even_norm=even_norm, even_w_in=even_w_in, even_conv_w=even_conv_w, even_pool_w=even_pool_w,
             even_pool_scale=even_pool_scale, even_w_out=even_w_out, odd_norm=odd_norm, odd_w_qkv=odd_w_qkv,
             odd_q_norm=odd_q_norm, odd_k_norm=odd_k_norm, odd_w_o=odd_w_o, ffn_norm=ffn_norm, ffn_w_up=ffn_w_up,
             ffn_conv_w=ffn_conv_w, ffn_conv_b=ffn_conv_b, ffn_w_down=ffn_w_down)
    M1 = dict(rel_bias=m_rel_bias, even_norm=m_even_norm, even_w_in=m_even_w_in, even_conv_w=m_even_conv_w,
              even_pool_w=m_even_pool_w, even_pool_scale=m_even_pool_scale, even_w_out=m_even_w_out, odd_norm=m_odd_norm,
              odd_w_qkv=m_odd_w_qkv, odd_q_norm=m_odd_q_norm, odd_k_norm=m_odd_k_norm, odd_w_o=m_odd_w_o,
              ffn_norm=m_ffn_norm, ffn_w_up=m_ffn_w_up, ffn_conv_w=m_ffn_conv_w, ffn_conv_b=m_ffn_conv_b,
              ffn_w_down=m_ffn_w_down)
    M2 = dict(rel_bias=v_rel_bias, even_norm=v_even_norm, even_w_in=v_even_w_in, even_conv_w=v_even_conv_w,
              even_pool_w=v_even_pool_w, even_pool_scale=v_even_pool_scale, even_w_out=v_even_w_out, odd_norm=v_odd_norm,
              odd_w_qkv=v_odd_w_qkv, odd_q_norm=v_odd_q_norm, odd_k_norm=v_odd_k_norm, odd_w_o=v_odd_w_o,
              ffn_norm=v_ffn_norm, ffn_w_up=v_ffn_w_up, ffn_conv_w=v_ffn_conv_w, ffn_conv_b=v_ffn_conv_b,
              ffn_w_down=v_ffn_w_down)
    mx, my, mc = _coords()
    chip = 2 * mx + my
    me = 4 * mx + 2 * my + mc
    place = jnp.stack([chip, mc, me]).astype(jnp.int32)
    xs, target = x[0], loss_target[0]

    def halves(w):
        return w.reshape((w.shape[0], 2, w.shape[-2] // 2, w.shape[-1]))

    small_rows = jnp.pad(_pack([even_conv_w, odd_norm, ffn_conv_w]), ((0, SUBLANES), (0, 0)))
    first = [cast_into_slot(halves(even_w_in), 0, place, "cast_w_in"), cast_into_slot(halves(even_w_out), 0, place, "cast_w_out"),
             cast_into_slot(small_rows.reshape(1, 2, small_rows.shape[0] // 2, LANES), 0, place, "small_into_slot", dtype=F32)]
    first_sems, first, token = gather_start(first, [[0, 1, 2]], "gather_start_first")
    even_norm_after_start = even_norm + token[0:1, 0:1]

    def later(a):
        return lax.optimization_barrier((a, token))[0]

    up_f32, down_f32 = halves(later(ffn_w_up)), halves(later(ffn_w_down))
    rest = [cast_into_slot(up_f32, 0, place, "cast_w_up0"), cast_into_slot(down_f32, 0, place, "cast_w_down0"),
            cast_into_slot(halves(later(odd_w_qkv)), 0, place, "cast_w_qkv"), cast_into_slot(halves(later(odd_w_o)), 0, place, "cast_w_o"),
            cast_into_slot(up_f32, 1, place, "cast_w_up1"), cast_into_slot(down_f32, 1, place, "cast_w_down1")]
    rest_sems, rest, _ = gather_start(rest, [[0], [1], [2, 3], [4], [5]], "gather_start_rest")
    group_arrays = [first, [rest[0]], [rest[1]], [rest[2], rest[3]], [rest[4]], [rest[5]]]
    group_sems = first_sems + rest_sems

    def gathered(group, tag, after_landing, after_passing):
        sems, arrays = gather_forward(group_arrays[group], group_sems[group], after_landing, "gather_forward_" + tag)
        return gather_wait(arrays, sems, after_passing, "gather_wait_" + tag)

    pool_w = cast_bf16(even_pool_w[0], "cast_pool_w")
    gqk = jnp.stack([jnp.tile(odd_q_norm[0], N_HEADS), jnp.tile(odd_k_norm[0], N_HEADS),
                     jnp.ones((D_MODEL,), F32)])[:, None, :]
    bias = bias_expand(later(rel_bias).T, "bias_expand").reshape(6, N_HEADS, ATT_BLOCK, 2 * ATT_BLOCK)
    xn0 = rmsnorm_fwd(xs, even_norm_after_start, "even_norm")
    got = gathered(0, "even", bias, xn0)
    w_in = _in_hbm(got[0].reshape(N_CHIPS, 1, D_MODEL, EVEN_IN // N_CHIPS))
    w_out = _in_hbm(got[1].reshape(1, 1, D_MODEL, D_MODEL))
    small = got[2].reshape(N_CHIPS, small_rows.shape[0], LANES)
    conv_w_full = small[:, 0:3].transpose(1, 0, 2).reshape(3, A_WIDTH)
    odd_norm_full = small[:, 8:10].reshape(1, D_MODEL)
    ffn_cw_full = small[:, 16:82].reshape(N_CHIPS, 2, 3, 2 * D_FF // N_CHIPS).transpose(1, 2, 0, 3).reshape(2, 3, 2 * D_FF)

    def ffn_fwd(l, xin, xn):
        up, u, act = up_glu_fwd(xn, w_up[l], ffn_cw_full[l], ffn_conv_b[l:l + 1], f"ffn{l}_up_glu")
        return act, (xin, xn, up, u, act)

    w_up, w_down = [None, None], [None, None]
    proj, mix = in_mixer_fwd(xn0, w_in, conv_w_full, pool_w, even_pool_scale, "even_in_mixer")
    x1, xn1 = mm_res_norm(mix, w_out, xs, ffn_norm[0:1], "even_out")
    w_up[0] = _in_hbm(gathered(1, "up0", proj, x1)[0].reshape(N_CHIPS, 1, D_MODEL, 2 * D_FF // N_CHIPS))
    act0, ffn0 = ffn_fwd(0, x1, xn1)
    w_down[0] = _in_hbm(gathered(2, "down0", act0, act0)[0].reshape(1, 1, D_FF, D_MODEL))
    x2, xn2 = mm_res_norm(act0, w_down[0], x1, odd_norm_full, "ffn0_down")
    got = gathered(3, "odd", x1, x2)
    w_qkv = _in_hbm(got[0].reshape(N_CHIPS, 1, D_MODEL, 3 * D_MODEL // N_CHIPS))
    w_o = _in_hbm(got[1].reshape(1, 1, D_MODEL, D_MODEL))
    qkv, qkvn = qkv_qknorm_fwd(xn2, w_qkv, gqk, "odd_qkv_qknorm")
    att, lse = attn_fwd(qkvn, bias, "attn_fwd")
    x3, xn3 = mm_res_norm(att, w_o, x2, ffn_norm[1:2], "odd_out")
    w_up[1] = _in_hbm(gathered(4, "up1", x2, x3)[0].reshape(N_CHIPS, 1, D_MODEL, 2 * D_FF // N_CHIPS))
    act1, ffn1 = ffn_fwd(1, x3, xn3)
    w_down[1] = _in_hbm(gathered(5, "down1", act1, act1)[0].reshape(1, 1, D_FF, D_MODEL))
    dy, dyb, sq = mm_res_loss(act1, w_down[1], x3, target, "ffn1_down_loss")
    loss = lax.psum(0.5 * jnp.sum(sq) * (1.0 / D_MODEL), ("x", "y", "c"))

    def ffn_bwd(l, dy, dyb, saved):
        xin, xn, up, u, act = saved
        dw_down = mm_tn(act, dyb, f"ffn{l}_dw_down", J=1, tk=D_FF // 2, tm=1024)
        dact = mm_nt(dyb, w_down[l], f"ffn{l}_dact", tr=D_FF // 2, out_dtype=BF16, tm=1024)
        dup, dcw, dcb = glu_bwd(up, u, dact, ffn_cw_full[l], f"ffn{l}_glu_bwd")
        dw_up = mm_tn(xn, dup, f"ffn{l}_dw_up", J=N_CHIPS, tk=512, tm=1024, jb=2)
        dx, dxb, dg = mm_nt_norm_bwd(dup, w_up[l], xin, ffn_norm[l:l + 1], dy, f"ffn{l}_dx")
        return dx, dxb, (dw_down, dw_up, dcw, dcb, dg)

    def quarters(g):
        return g.reshape(N_CHIPS, 2, g.shape[0] * g.shape[1] // (2 * N_CHIPS), g.shape[-1])

    def reduce_start(grads, tag, then):
        sems, parts, zones = reduce_send([quarters(g) for g in grads], "reduce_send_" + tag)
        then, parts = lax.optimization_barrier((then, parts))
        return (sems, parts, zones), then

    dx3, dx3b, g_ffn1 = ffn_bwd(1, dy, dyb, ffn1)
    red_ffn1, (dx3, dx3b) = reduce_start([g_ffn1[1], g_ffn1[0]], "ffn1", (dx3, dx3b))
    dw_o = mm_tn(att, dx3b, "odd_dw_o", J=1, tk=512, tm=1024)
    datt = mm_nt(dx3b, w_o, "odd_datt", tr=D_MODEL, out_dtype=BF16)
    dq, dk, dv, dbias = attn_bwd(qkvn, att, datt, lse, bias, "attn_bwd")
    dqkv, dgqk = qknorm_bwd(qkv, dq, dk, dv, gqk, "odd_qknorm_bwd")
    dw_qkv = mm_tn(xn2, dqkv, "odd_dw_qkv", J=N_CHIPS, tk=512, tm=1024)
    red_odd, dqkv = reduce_start([dw_qkv, dw_o], "odd", dqkv)
    dx2, dx2b, dg_odd = mm_nt_norm_bwd(dqkv, w_qkv, x2, odd_norm_full, dx3, "odd_dx")
    dx1, dx1b, g_ffn0 = ffn_bwd(0, dx2, dx2b, ffn0)
    red_ffn0, (dx1, dx1b) = reduce_start([g_ffn0[1], g_ffn0[0]], "ffn0", (dx1, dx1b))
    dw_out = mm_tn(mix, dx1b, "even_dw_out", J=1, tk=512, tm=1024)
    dmix = mm_nt(dx1b, w_out, "even_dmix", tr=D_MODEL)
    dproj, dcw_even, dpw, dps = mixer_bwd(proj, dmix, conv_w_full, pool_w, even_pool_scale, "even_mixer_bwd")
    dw_in = mm_tn(xn0, dproj, "even_dw_in", J=N_CHIPS, tk=512, tm=1024)
    grad_x, _, dg_even = mm_nt_norm_bwd(dproj, w_in, xs, even_norm, dx1, "even_dx")
    d_rel = jnp.sum(bias_reduce(dbias.reshape(3, N_HEADS, 2 * ATT_BLOCK * ATT_BLOCK), "bias_reduce"), axis=0).T

    red_even, grad_x = reduce_start([dw_in, dw_out], "even", grad_x)

    dcw_sh = dcw_even.reshape(3, N_CHIPS, A_WIDTH // N_CHIPS).transpose(1, 0, 2)
    don_sh = dg_odd.reshape(N_CHIPS, D_MODEL // N_CHIPS)
    dfcw = jnp.stack([g_ffn0[2], g_ffn1[2]])
    dfcw_sh = dfcw.reshape(2, 3, N_CHIPS, 2 * D_FF // N_CHIPS).transpose(2, 0, 1, 3)
    rep_grads = [d_rel, dg_even, dpw[None], dps, _head_sum(dgqk[0]), _head_sum(dgqk[1]),
                 jnp.concatenate([g_ffn0[4], g_ffn1[4]], axis=0), jnp.concatenate([g_ffn0[3], g_ffn1[3]], axis=0)]
    rep_rows = _pack(rep_grads)
    shard_rows = jnp.concatenate([_pack([dcw_sh[j], don_sh[j], dfcw_sh[j]]) for j in range(N_CHIPS)], axis=0)
    n_rep, n_shard = rep_rows.shape[0], shard_rows.shape[0] // N_CHIPS
    small_sems, small_rows, small_land = devices_start(jnp.concatenate([rep_rows, shard_rows], axis=0), "small_grads_start")
    grad_x, small_rows = lax.optimization_barrier((grad_x, small_rows))

    def reduce_end(red, tag, after):
        sems, parts, zones = red
        parts, zones = reduce_wait(parts, zones, sems, after, "reduce_wait_" + tag)
        return zones, parts

    z_ffn1, p_ffn1 = reduce_end(red_ffn1, "ffn1", grad_x)
    z_odd, p_odd = reduce_end(red_odd, "odd", grad_x)
    r_qkv = reduce_sum(z_odd[0], p_odd[0], place, "reduce_sum_w_qkv")
    r_o = reduce_sum(z_odd[1], p_odd[1], place, "reduce_sum_w_o")
    r_up = reduce_sum(z_ffn1[0], p_ffn1[0], place, "reduce_sum_w_up1", layer=1)
    r_down = reduce_sum(z_ffn1[1], p_ffn1[1], place, "reduce_sum_w_down1", layer=1)
    r_qkv, r_o, r_up, r_down = lax.optimization_barrier((r_qkv, r_o, r_up, r_down))
    z_ffn0, p_ffn0 = reduce_end(red_ffn0, "ffn0", r_down)
    r_up = reduce_sum(z_ffn0[0], p_ffn0[0], place, "reduce_sum_w_up0", into=r_up, layer=0)
    r_down = reduce_sum(z_ffn0[1], p_ffn0[1], place, "reduce_sum_w_down0", into=r_down, layer=0)
    later = ["odd_w_qkv", "odd_w_o", "ffn_w_up", "ffn_w_down"]
    joined = join_halves([r_qkv, r_o, r_up, r_down], "grads_join_late_layers")
    G = {nm: g.reshape(W[nm].shape) for nm, g in zip(later, joined)}

    D_, NM, NV = {}, {}, {}

    def update(nm):
        as3 = lambda a: a.reshape((-1,) + a.shape[-2:])
        outs = adamw(as3(W[nm]), as3(G[nm]), as3(M1[nm]), as3(M2[nm]), "adamw_" + nm)
        D_[nm], NM[nm], NV[nm], G[nm] = [o.reshape(W[nm].shape) for o in outs]

    def all_before(names):
        tied = lax.optimization_barrier([D_[nm] for nm in names])
        for nm, d in zip(names, tied):
            D_[nm] = d
        return tied[0]

    for nm in later:
        update(nm)
    z_even, p_even = reduce_end(red_even, "even", all_before(later))
    joined = join_halves([reduce_sum(z_even[0], p_even[0], place, "reduce_sum_w_in"),
                          reduce_sum(z_even[1], p_even[1], place, "reduce_sum_w_out")], "grads_join_first_layer")
    first = ["even_w_in", "even_w_out"]
    for nm, g in zip(first, joined):
        G[nm] = g.reshape(W[nm].shape)
        update(nm)
    small_rows, small_land = devices_wait(small_rows, small_land, small_sems, all_before(first), "small_grads_wait")
    small_sum = device_sum(small_land, small_rows, place[2:3], "small_grads_sum")
    mine = lax.dynamic_slice_in_dim(small_sum, n_rep + chip * n_shard, n_shard, axis=0)
    g_small = jnp.concatenate([small_sum[:n_rep], mine], axis=0)
    small_names = [n for n, _ in REPLICATED_SMALL + SHARDED_SMALL]
    small_shapes = [s for _, s in REPLICATED_SMALL + SHARDED_SMALL]
    G.update(dict(zip(small_names, _unpack(g_small, small_shapes))))
    packs = [_pack([d[n] for n in small_names])[None] for d in (W, M1, M2)]
    outs = adamw(packs[0], g_small[None], packs[1], packs[2], "adamw_small")
    for dst, o in zip((D_, NM, NV), outs[:3]):
        dst.update(dict(zip(small_names, _unpack(o[0], small_shapes))))

    return (loss, grad_x[None], *[G[n] for n in WEIGHT_ORDER], *[D_[n] for n in WEIGHT_ORDER],
            *[NM[n] for n in WEIGHT_ORDER], *[NV[n] for n in WEIGHT_ORDER])


def _head_sum(dg):
    return jnp.sum(dg.reshape(N_HEADS, HEAD_DIM), axis=0, keepdims=True)
```

```python
import functools
import math

import numpy as np
import jax
import jax.numpy as jnp
from jax import lax
from jax.experimental import pallas as pl
from jax.experimental.pallas import tpu as pltpu

F32 = jnp.float32
BF16 = jnp.bfloat16

D_MODEL = 1024
N_HEADS = 16
HEAD_DIM = 64
A_WIDTH = 512
POOL_WINDOWS = (2, 4, 8, 16)
POOL_GROUP = 128
EVEN_IN = 2048
D_FF = 2816
DILATED_PAIRS = ((128, 1), (512, 4), (2048, 16))
ATT_BLOCK = 128
N_REL_BUCKETS = 32
REL_MAX_DISTANCE = 2048
EPS = 1e-6
MASK_VALUE = -1e30
ADAM_LR, ADAM_B1, ADAM_B2, ADAM_EPS, ADAM_WD, ADAM_STEP = 0.001, 0.9, 0.999, 1e-08, 0.01, 10

VMEM_LIMIT_BYTES = 48 * 1024 * 1024
ELEMENTWISE_BLOCK_BYTES = 2 * 1024 * 1024
N_CHIPS = 4
N_DEV = 8
MESH = pl.DeviceIdType.MESH


def _params(*sem):
    return pltpu.CompilerParams(dimension_semantics=sem if sem else None, vmem_limit_bytes=VMEM_LIMIT_BYTES)


def _sds(shape, dtype):
    return jax.ShapeDtypeStruct(tuple(shape), dtype)


def _sds_hbm(shape, dtype):
    return pltpu.HBM(tuple(shape), dtype)


def cast_bf16(x, name, tr=None):
    lead, (R, C) = x.shape[:-2], x.shape[-2:]
    n = int(np.prod(lead)) if lead else 1
    x3 = x.reshape((n, R, C))
    tr = tr or R

    def body(x_ref, o_ref):
        o_ref[...] = x_ref[...].astype(BF16)

    out = pl.pallas_call(
        body, name=name, grid=(n, R // tr),
        in_specs=[pl.BlockSpec((None, tr, C), lambda i, r: (i, r, 0))],
        out_specs=pl.BlockSpec((None, tr, C), lambda i, r: (i, r, 0)),
        out_shape=_sds((n, R, C), BF16), compiler_params=_params("parallel", "parallel"),
    )(x3)
    return out.reshape(lead + (R, C))


def rmsnorm_fwd(x, g, name, ts=512):
    S, Dm = x.shape

    def body(x_ref, g_ref, o_ref):
        xv = x_ref[...]
        r = lax.rsqrt(jnp.mean(xv * xv, axis=-1, keepdims=True) + EPS)
        o_ref[...] = ((xv * r) * g_ref[...]).astype(BF16)

    return pl.pallas_call(
        body, name=name, grid=(S // ts,),
        in_specs=[pl.BlockSpec((ts, Dm), lambda i: (i, 0)), pl.BlockSpec((1, Dm), lambda i: (0, 0))],
        out_specs=pl.BlockSpec((ts, Dm), lambda i: (i, 0)),
        out_shape=_sds((S, Dm), BF16), compiler_params=_params("parallel"),
    )(x, g)


def mm_res_norm(a, w, res, gain, name, tm=1024):
    M, K = a.shape
    Dm = w.shape[-1]

    def body(a_ref, w_ref, r_ref, g_ref, y_ref, yn_ref):
        y = r_ref[...] + jnp.dot(a_ref[...], w_ref[...], preferred_element_type=F32)
        y_ref[...] = y
        r = lax.rsqrt(jnp.mean(y * y, axis=-1, keepdims=True) + EPS)
        yn_ref[...] = ((y * r) * g_ref[...]).astype(BF16)

    row = pl.BlockSpec((tm, Dm), lambda m: (m, 0))
    return pl.pallas_call(
        body, name=name, grid=(M // tm,),
        in_specs=[pl.BlockSpec((tm, K), lambda m: (m, 0)),
                  pl.BlockSpec((None, None, K, Dm), lambda m: (0, 0, 0, 0), pipeline_mode=pl.Buffered(1)),
                  row, pl.BlockSpec((1, Dm), lambda m: (0, 0))],
        out_specs=[row, row], out_shape=[_sds((M, Dm), F32), _sds((M, Dm), BF16)],
        compiler_params=_params("parallel"),
    )(a, w, res, gain)


def mm_res_loss(a, w, res, target, name, tm=512):
    M, K = a.shape
    Dm = w.shape[-1]

    def body(a_ref, w_ref, r_ref, t_ref, d_ref, db_ref, s_ref):
        e = (r_ref[...] + jnp.dot(a_ref[...], w_ref[...], preferred_element_type=F32)) - t_ref[...]
        d = e * (1.0 / Dm)
        d_ref[...] = d
        db_ref[...] = d.astype(BF16)
        part = jnp.sum(e * e, axis=0, keepdims=True)

        @pl.when(pl.program_id(0) == 0)
        def _():
            s_ref[...] = part

        @pl.when(pl.program_id(0) > 0)
        def _():
            s_ref[...] += part

    row = pl.BlockSpec((tm, Dm), lambda m: (m, 0))
    return pl.pallas_call(
        body, name=name, grid=(M // tm,),
        in_specs=[pl.BlockSpec((tm, K), lambda m: (m, 0)),
                  pl.BlockSpec((None, None, K, Dm), lambda m: (0, 0, 0, 0), pipeline_mode=pl.Buffered(1)), row, row],
        out_specs=[row, row, pl.BlockSpec((1, Dm), lambda m: (0, 0))],
        out_shape=[_sds((M, Dm), F32), _sds((M, Dm), BF16), _sds((1, Dm), F32)],
        compiler_params=_params("arbitrary"),
    )(a, w, res, target)


def mm_nt(dy, w, name, tr, layer=0, out_dtype=F32, tm=512):
    M = dy.shape[0]
    J, _, R, Ns = w.shape
    dims = (((1,), (1,)), ((), ()))

    def body(dy_ref, w_ref, o_ref):
        acc = None
        for j in range(J):
            p = lax.dot_general(dy_ref[:, j * Ns:(j + 1) * Ns], w_ref[j], dims, preferred_element_type=F32)
            acc = p if acc is None else acc + p
        o_ref[...] = acc.astype(o_ref.dtype)

    return pl.pallas_call(
        body, name=name, grid=(R // tr, M // tm),
        in_specs=[pl.BlockSpec((tm, J * Ns), lambda r, m: (m, 0)),
                  pl.BlockSpec((J, None, tr, Ns), lambda r, m: (0, layer, r, 0))],
        out_specs=pl.BlockSpec((tm, tr), lambda r, m: (m, r)),
        out_shape=_sds((M, R), out_dtype),
        compiler_params=_params("parallel", "parallel"),
    )(dy, w)


def mm_nt_norm_bwd(dy, w, x, g, dres, name, layer=0, tm=512):
    M = dy.shape[0]
    J, _, Dm, Ns = w.shape
    dims = (((1,), (1,)), ((), ()))

    def body(dy_ref, w_ref, x_ref, g_ref, r_ref, dx_ref, dxb_ref, dg_ref):
        dxn = None
        for j in range(J):
            p = lax.dot_general(dy_ref[:, j * Ns:(j + 1) * Ns], w_ref[j], dims, preferred_element_type=F32)
            dxn = p if dxn is None else dxn + p
        xv = x_ref[...]
        r = lax.rsqrt(jnp.mean(xv * xv, axis=-1, keepdims=True) + EPS)
        gx = dxn * g_ref[...]
        dot = jnp.sum(gx * xv, axis=-1, keepdims=True)
        dx = r_ref[...] + r * gx - xv * ((r * r * r) * (dot * (1.0 / Dm)))
        dx_ref[...] = dx
        dxb_ref[...] = dx.astype(BF16)
        part = jnp.sum(dxn * (xv * r), axis=0, keepdims=True)

        @pl.when(pl.program_id(0) == 0)
        def _():
            dg_ref[...] = part

        @pl.when(pl.program_id(0) > 0)
        def _():
            dg_ref[...] += part

    row = pl.BlockSpec((tm, Dm), lambda m: (m, 0))
    vec = pl.BlockSpec((1, Dm), lambda m: (0, 0))
    return pl.pallas_call(
        body, name=name, grid=(M // tm,),
        in_specs=[pl.BlockSpec((tm, J * Ns), lambda m: (m, 0)),
                  pl.BlockSpec((J, None, Dm, Ns), lambda m: (0, layer, 0, 0), pipeline_mode=pl.Buffered(1)), row, vec, row],
        out_specs=[row, row, vec],
        out_shape=[_sds((M, Dm), F32), _sds((M, Dm), BF16), _sds((1, Dm), F32)],
        compiler_params=_params("arbitrary"),
    )(dy, w, x, g, dres)


def mm_tn(a, dy, name, J, tk, tm=512, jb=None):
    M, K = a.shape
    jb = jb or J
    Ns = dy.shape[1] // J
    N = jb * Ns
    n_m = M // tm
    dims = (((0,), (0,)), ((), ()))

    def body(a_ref, dy_ref, o_ref, acc_ref):
        p = lax.dot_general(a_ref[...], dy_ref[...], dims, preferred_element_type=F32)
        m = pl.program_id(2)

        @pl.when(m == 0)
        def _():
            acc_ref[...] = p

        @pl.when(m > 0)
        def _():
            acc_ref[...] += p

        @pl.when(m == n_m - 1)
        def _():
            for j in range(jb):
                o_ref[j] = acc_ref[:, j * Ns:(j + 1) * Ns].astype(BF16)

    return pl.pallas_call(
        body, name=name, grid=(J // jb, K // tk, n_m),
        in_specs=[pl.BlockSpec((tm, tk), lambda g, k, m: (m, k)), pl.BlockSpec((tm, N), lambda g, k, m: (m, g))],
        out_specs=pl.BlockSpec((jb, tk, Ns), lambda g, k, m: (g, k, 0)),
        out_shape=_sds((J, K, Ns), BF16), scratch_shapes=[pltpu.VMEM((tk, N), F32)],
        compiler_params=_params("parallel", "parallel", "arbitrary"),
    )(a, dy)


HALO = 16


def _shift_down(x, s):
    return pltpu.roll(x, s, 0)


def _shift_up(x, s):
    return pltpu.roll(x, x.shape[0] - s, 0)


def _conv3(z, cw):
    return (_shift_down(z, 2) * cw[0:1] + _shift_down(z, 1) * cw[1:2]) + z * cw[2:3]


def _window_count(first_row, n, k):
    t = first_row + lax.broadcasted_iota(jnp.int32, (n, 1), 0)
    return jnp.clip(t + 1, 1, k).astype(F32)


def in_mixer_fwd(xn, w_in, conv_w, pool_w, pool_scale, name, ts=512):
    S, K = xn.shape
    n = ts + HALO

    def body(xm_ref, xb_ref, w_ref, cw_ref, pw_ref, ps_ref, p_ref, o_ref):
        i = pl.program_id(0)
        before = jnp.where(i > 0, xb_ref[...], jnp.zeros_like(xb_ref))
        rows = jnp.concatenate([before, xm_ref[...]], axis=0)
        h, gb, gc, pin = [jnp.dot(rows, w_ref[j], preferred_element_type=F32) for j in range(N_CHIPS)]
        for j, part in enumerate((h, gb, gc, pin)):
            p_ref[:, j * A_WIDTH:(j + 1) * A_WIDTH] = part[HALO:]
        cz = _conv3(gc * h, cw_ref[...])
        o_ref[:, 0:A_WIDTH] = (gb[HALO:] * cz[HALO:]).astype(BF16)
        for g, k in enumerate(POOL_WINDOWS):
            p = pin[:, g * POOL_GROUP:(g + 1) * POOL_GROUP]
            w = p
            s = 1
            while s < k:
                w = w + _shift_down(w, s)
                s *= 2
            pooled = w / _window_count(i * ts - HALO, n, k) - p
            yb = jnp.dot(pooled[HALO:].astype(BF16), pw_ref[g], preferred_element_type=F32)
            yb = yb * ps_ref[:, g * POOL_GROUP:(g + 1) * POOL_GROUP]
            o_ref[:, A_WIDTH + g * POOL_GROUP:A_WIDTH + (g + 1) * POOL_GROUP] = yb.astype(BF16)

    hb = ts // HALO
    return pl.pallas_call(
        body, name=name, grid=(S // ts,),
        in_specs=[
            pl.BlockSpec((ts, K), lambda i: (i, 0)),
            pl.BlockSpec((HALO, K), lambda i: (jnp.maximum(i * hb - 1, 0), 0)),
            pl.BlockSpec((N_CHIPS, None, K, A_WIDTH), lambda i: (0, 0, 0, 0), pipeline_mode=pl.Buffered(1)),
            pl.BlockSpec((3, A_WIDTH), lambda i: (0, 0)),
            pl.BlockSpec((4, POOL_GROUP, POOL_GROUP), lambda i: (0, 0, 0)),
            pl.BlockSpec((1, 4 * POOL_GROUP), lambda i: (0, 0)),
        ],
        out_specs=[pl.BlockSpec((ts, EVEN_IN), lambda i: (i, 0)), pl.BlockSpec((ts, D_MODEL), lambda i: (i, 0))],
        out_shape=[_sds((S, EVEN_IN), F32), _sds((S, D_MODEL), BF16)], compiler_params=_params("parallel"),
    )(xn, xn, w_in, conv_w, pool_w, pool_scale)


def mixer_bwd(proj, dmix, conv_w, pool_w, pool_scale, name, ts=256):
    S = proj.shape[0]
    n = ts + 2 * HALO
    nt = S // ts
    tn_dims = (((0,), (0,)), ((), ()))
    nt_dims = (((1,), (1,)), ((), ()))

    def body(pm_ref, pb_ref, pa_ref, dm_ref, da_ref, cw_ref, pw_ref, ps_ref, o_ref, dcw_ref, dpw_ref, dps_ref):
        i = pl.program_id(0)
        last = i == nt - 1
        before = jnp.where(i > 0, pb_ref[...], 0.0)
        after = jnp.where(last, 0.0, pa_ref[...])
        ext = jnp.concatenate([before, pm_ref[...], after], axis=0)
        dafter = jnp.where(last, 0.0, da_ref[...])
        dext = jnp.concatenate([jnp.zeros((HALO, D_MODEL), F32), dm_ref[...], dafter], axis=0)
        cw = cw_ref[...]
        main = slice(HALO, HALO + ts)

        @pl.when(i == 0)
        def _():
            dcw_ref[...] = jnp.zeros_like(dcw_ref)
            dpw_ref[...] = jnp.zeros_like(dpw_ref)
            dps_ref[...] = jnp.zeros_like(dps_ref)

        h, gb, gc = ext[:, 0:A_WIDTH], ext[:, A_WIDTH:2 * A_WIDTH], ext[:, 2 * A_WIDTH:3 * A_WIDTH]
        z = gc * h
        z1, z2 = _shift_down(z, 1), _shift_down(z, 2)
        cz = (z2 * cw[0:1] + z1 * cw[1:2]) + z * cw[2:3]
        dya = dext[:, 0:A_WIDTH]
        dcz = dya * gb
        dz = dcz * cw[2:3] + _shift_up(dcz, 1) * cw[1:2] + _shift_up(dcz, 2) * cw[0:1]
        o_ref[:, 0:A_WIDTH] = (dz * gc)[main].astype(BF16)
        o_ref[:, A_WIDTH:2 * A_WIDTH] = (dya * cz)[main].astype(BF16)
        o_ref[:, 2 * A_WIDTH:3 * A_WIDTH] = (dz * h)[main].astype(BF16)
        dczm = dcz[main]
        dcw_ref[0:1, :] += jnp.sum(dczm * z2[main], axis=0, keepdims=True)
        dcw_ref[1:2, :] += jnp.sum(dczm * z1[main], axis=0, keepdims=True)
        dcw_ref[2:3, :] += jnp.sum(dczm * z[main], axis=0, keepdims=True)

        for g, k in enumerate(POOL_WINDOWS):
            lo = 3 * A_WIDTH + g * POOL_GROUP
            cols = slice(g * POOL_GROUP, (g + 1) * POOL_GROUP)
            p = ext[:, lo:lo + POOL_GROUP]
            w = p
            s = 1
            while s < k:
                w = w + _shift_down(w, s)
                s *= 2
            cnt = _window_count(i * ts - HALO, n, k)
            pooled = (w / cnt - p)[main].astype(BF16)
            dyb = dext[:, A_WIDTH + g * POOL_GROUP:A_WIDTH + (g + 1) * POOL_GROUP]
            e = dyb * ps_ref[:, cols]
            pre = jnp.dot(pooled, pw_ref[g], preferred_element_type=F32)
            dps_ref[:, cols] += jnp.sum(dyb[main] * pre, axis=0, keepdims=True)
            dpw_ref[g] += lax.dot_general(pooled, e[main].astype(BF16), tn_dims, preferred_element_type=F32)
            dpooled = lax.dot_general(e.astype(BF16), pw_ref[g], nt_dims, preferred_element_type=F32)
            q = dpooled / cnt
            a = q
            s = 1
            while s < k:
                a = a + _shift_up(a, s)
                s *= 2
            o_ref[:, lo:lo + POOL_GROUP] = (a - dpooled)[main].astype(BF16)

    hb = ts // HALO
    nh = S // HALO
    before_map = lambda i: (jnp.maximum(i * hb - 1, 0), 0)
    after_map = lambda i: (jnp.minimum((i + 1) * hb, nh - 1), 0)
    full = lambda *shape: pl.BlockSpec(shape, lambda i: (0,) * len(shape))
    return pl.pallas_call(
        body, name=name, grid=(nt,),
        in_specs=[
            pl.BlockSpec((ts, EVEN_IN), lambda i: (i, 0)),
            pl.BlockSpec((HALO, EVEN_IN), before_map),
            pl.BlockSpec((HALO, EVEN_IN), after_map),
            pl.BlockSpec((ts, D_MODEL), lambda i: (i, 0)),
            pl.BlockSpec((HALO, D_MODEL), after_map),
            full(3, A_WIDTH), full(4, POOL_GROUP, POOL_GROUP), full(1, 4 * POOL_GROUP),
        ],
        out_specs=[pl.BlockSpec((ts, EVEN_IN), lambda i: (i, 0)), full(3, A_WIDTH), full(4, POOL_GROUP, POOL_GROUP),
                   full(1, 4 * POOL_GROUP)],
        out_shape=[_sds((S, EVEN_IN), BF16), _sds((3, A_WIDTH), F32), _sds((4, POOL_GROUP, POOL_GROUP), F32),
                   _sds((1, 4 * POOL_GROUP), F32)],
        compiler_params=_params("arbitrary"),
    )(proj, proj, proj, dmix, dmix, conv_w, pool_w, pool_scale)


FFN_HALO = 16
FFN_TC = 1408


GLU_CHUNKS = ((0, 512), (512, 512), (1024, 384))


def up_glu_fwd(xn, w_up, conv_w, conv_b, name, tm=512):
    S, K = xn.shape
    nc = D_FF // FFN_TC

    def body(xm_ref, xb_ref, wg_ref, wu_ref, cwg_ref, cwu_ref, cbg_ref, cbu_ref, pg_ref, pu_ref, ug_ref, uu_ref, o_ref):
        before = jnp.where(pl.program_id(1) > 0, xb_ref[...], jnp.zeros_like(xb_ref))
        rows = jnp.concatenate([before, xm_ref[...]], axis=0)
        for lo, width in GLU_CHUNKS:
            cols = slice(lo, lo + width)
            pre_g = jnp.dot(rows, wg_ref[:, cols], preferred_element_type=F32)
            pre_u = jnp.dot(rows, wu_ref[:, cols], preferred_element_type=F32)
            gate = _conv3(pre_g, cwg_ref[:, cols])[FFN_HALO:] + cbg_ref[:, cols]
            upv = _conv3(pre_u, cwu_ref[:, cols])[FFN_HALO:] + cbu_ref[:, cols]
            pg_ref[:, cols] = pre_g[FFN_HALO:].astype(BF16)
            pu_ref[:, cols] = pre_u[FFN_HALO:].astype(BF16)
            ug_ref[:, cols] = gate.astype(BF16)
            uu_ref[:, cols] = upv.astype(BF16)
            o_ref[:, cols] = ((gate * (1.0 / (1.0 + jnp.exp(-gate)))) * upv).astype(BF16)

    hb = tm // FFN_HALO
    wspec = lambda off: pl.BlockSpec((None, None, K, FFN_TC), lambda j, m: (j + off, 0, 0, 0))
    cw = lambda off: pl.BlockSpec((3, FFN_TC), lambda j, m: (0, j + off))
    cb = lambda off: pl.BlockSpec((1, FFN_TC), lambda j, m: (0, j + off))
    out = pl.BlockSpec((tm, FFN_TC), lambda j, m: (m, j))
    pg, pu, ug, uu, act = pl.pallas_call(
        body, name=name, grid=(nc, S // tm),
        in_specs=[pl.BlockSpec((tm, K), lambda j, m: (m, 0)),
                  pl.BlockSpec((FFN_HALO, K), lambda j, m: (jnp.maximum(m * hb - 1, 0), 0)),
                  wspec(0), wspec(nc), cw(0), cw(nc), cb(0), cb(nc)],
        out_specs=[out] * 5, out_shape=[_sds((S, D_FF), BF16)] * 5,
        compiler_params=_params("parallel", "parallel"),
    )(xn, xn, w_up, w_up, conv_w, conv_w, conv_b, conv_b)
    return (pg, pu), (ug, uu), act


def glu_bwd(up, u, da, conv_w, name, ts=256):
    S = up[0].shape[0]
    nc = D_FF // FFN_TC
    nt = S // ts
    W = 2 * D_FF

    def body(xg_ref, xu_ref, gm_ref, ga_ref, um_ref, ua_ref, dm_ref, da_ref, cw_ref, dx_ref, dcw_ref, dcb_ref):
        i = pl.program_id(0)
        last = i == nt - 1

        @pl.when(i == 0)
        def _():
            dcw_ref[...] = jnp.zeros_like(dcw_ref)
            dcb_ref[...] = jnp.zeros_like(dcb_ref)

        def rows(m_ref, a_ref, cols):
            return jnp.concatenate([m_ref[:, cols], a_ref[:, cols]], axis=0).astype(F32)

        def back(d, x, cols):
            cw = cw_ref[:, cols]
            d1, d2 = _shift_up(d, 1), _shift_up(d, 2)
            dx_ref[:, cols] = ((d * cw[2:3] + d1 * cw[1:2]) + d2 * cw[0:1])[:ts].astype(BF16)
            dcb_ref[:, cols] += jnp.sum(d[:ts], axis=0, keepdims=True)
            dcw_ref[0:1, cols] += jnp.sum(d2[:ts] * x, axis=0, keepdims=True)
            dcw_ref[1:2, cols] += jnp.sum(d1[:ts] * x, axis=0, keepdims=True)
            dcw_ref[2:3, cols] += jnp.sum(d[:ts] * x, axis=0, keepdims=True)

        for c in range(nc):
            cols = slice(c * FFN_TC, (c + 1) * FFN_TC)
            ug, uu = rows(gm_ref, ga_ref, cols), rows(um_ref, ua_ref, cols)
            dae = rows(dm_ref, da_ref, cols)
            dae = jnp.where(last & (lax.broadcasted_iota(jnp.int32, dae.shape, 0) >= ts), 0.0, dae)
            sg = 1.0 / (1.0 + jnp.exp(-ug))
            duu = dae * (ug * sg)
            dug = (dae * uu) * (sg * (1.0 + ug * (1.0 - sg)))
            back(dug, xg_ref[:, cols].astype(F32), cols)
            back(duu, xu_ref[:, cols].astype(F32), slice(D_FF + c * FFN_TC, D_FF + (c + 1) * FFN_TC))

    hb = ts // FFN_HALO
    nh = S // FFN_HALO
    after_map = lambda i: (jnp.minimum((i + 1) * hb, nh - 1), 0)
    main = pl.BlockSpec((ts, D_FF), lambda i: (i, 0))
    after = pl.BlockSpec((FFN_HALO, D_FF), after_map)
    return pl.pallas_call(
        body, name=name, grid=(nt,),
        in_specs=[main, main, main, after, main, after, main, after, pl.BlockSpec((3, W), lambda i: (0, 0))],
        out_specs=[pl.BlockSpec((ts, W), lambda i: (i, 0)), pl.BlockSpec((3, W), lambda i: (0, 0)),
                   pl.BlockSpec((1, W), lambda i: (0, 0))],
        out_shape=[_sds((S, W), BF16), _sds((3, W), F32), _sds((1, W), F32)],
        compiler_params=_params("arbitrary"),
    )(up[0], up[1], u[0], u[0], u[1], u[1], da, da, conv_w)


MEAN_GROUP = 256


def _head_mean_matrix():
    h = np.arange(MEAN_GROUP) // HEAD_DIM
    return jnp.asarray((h[:, None] == h[None, :]).astype(np.float32) / HEAD_DIM, dtype=BF16)


def _head_mean(v, gm):
    vb = v.astype(BF16)
    return jnp.concatenate([jnp.dot(vb[:, c:c + MEAN_GROUP], gm, preferred_element_type=F32)
                            for c in range(0, v.shape[1], MEAN_GROUP)], axis=1)


def qkv_qknorm_fwd(xn, w_qkv, gqk, name, tm=1024):
    S, K = xn.shape
    J, _, _, Ns = w_qkv.shape
    gains = gqk.reshape(1, 3 * D_MODEL)

    def body(x_ref, w_ref, g_ref, gm_ref, raw_ref, o_ref):
        first_col = pl.program_id(0) * Ns
        acc = jnp.dot(x_ref[...], w_ref[...], preferred_element_type=F32)
        raw_ref[...] = acc
        gm = gm_ref[...]
        for c in range(0, Ns, MEAN_GROUP):
            cols = slice(c, c + MEAN_GROUP)
            x = acc[:, cols]
            mean = jnp.dot((x * x).astype(BF16), gm, preferred_element_type=F32)
            normed = (x * lax.rsqrt(mean + EPS)) * g_ref[:, cols]
            o_ref[:, cols] = jnp.where(first_col + c >= 2 * D_MODEL, x, normed).astype(BF16)

    return pl.pallas_call(
        body, name=name, grid=(J, S // tm),
        in_specs=[pl.BlockSpec((tm, K), lambda j, m: (m, 0)), pl.BlockSpec((None, None, K, Ns), lambda j, m: (j, 0, 0, 0)),
                  pl.BlockSpec((1, Ns), lambda j, m: (0, j)), pl.BlockSpec((MEAN_GROUP, MEAN_GROUP), lambda j, m: (0, 0))],
        out_specs=[pl.BlockSpec((tm, Ns), lambda j, m: (m, j))] * 2,
        out_shape=[_sds((S, J * Ns), F32), _sds((S, J * Ns), BF16)], compiler_params=_params("parallel", "parallel"),
    )(xn, w_qkv, gains, _head_mean_matrix())


def qknorm_bwd(qkv, dq, dk, dv, gqk, name, ts=256):
    S = qkv.shape[0]

    def body(x_ref, dq_ref, dk_ref, dv_ref, g_ref, gm_ref, o_ref, dg_ref):
        @pl.when(pl.program_id(0) == 0)
        def _():
            dg_ref[...] = jnp.zeros_like(dg_ref)

        gm = gm_ref[...]
        for part, d_ref in enumerate((dq_ref, dk_ref)):
            cols = slice(part * D_MODEL, (part + 1) * D_MODEL)
            x = x_ref[:, cols]
            d = d_ref[...]
            r = lax.rsqrt(_head_mean(x * x, gm) + EPS)
            gx = d * g_ref[part]
            o_ref[:, cols] = (r * gx - x * ((r * r * r) * _head_mean(gx * x, gm))).astype(BF16)
            dg_ref[part] += jnp.sum(d * (x * r), axis=0, keepdims=True)
        o_ref[:, 2 * D_MODEL:] = dv_ref[...].astype(BF16)

    row = pl.BlockSpec((ts, D_MODEL), lambda i: (i, 0))
    wide = pl.BlockSpec((ts, 3 * D_MODEL), lambda i: (i, 0))
    gains = pl.BlockSpec((3, 1, D_MODEL), lambda i: (0, 0, 0))
    return pl.pallas_call(
        body, name=name, grid=(S // ts,),
        in_specs=[wide, row, row, row, gains, pl.BlockSpec((MEAN_GROUP, MEAN_GROUP), lambda i: (0, 0))],
        out_specs=[wide, gains],
        out_shape=[_sds((S, 3 * D_MODEL), BF16), _sds((3, 1, D_MODEL), F32)],
        compiler_params=_params("arbitrary"),
    )(qkv, dq, dk, dv, gqk, _head_mean_matrix())


RESIDUES = 16


def _block_order(dil):
    runs = RESIDUES // dil
    slot = np.arange(ATT_BLOCK)
    return (slot % (ATT_BLOCK // runs)) * runs + slot // (ATT_BLOCK // runs)


def _bucket_tables():
    n = ATT_BLOCK
    max_exact = N_REL_BUCKETS // 2
    buckets, valids = [], []
    for _, dil in DILATED_PAIRS:
        order = _block_order(dil)
        a = order[:, None]
        c = np.concatenate([order, n + order])[None, :]
        first_half = (np.arange(2 * n) < n)[None, :]
        rel = a + n - c
        band = (rel >= 0) & (rel <= n)
        dist = np.clip(rel, 0, n) * dil
        dd = np.maximum(dist, 1).astype(np.float32)
        large = max_exact + (np.log(dd / np.float32(max_exact)) / np.float32(math.log(REL_MAX_DISTANCE / max_exact))
                             * np.float32(N_REL_BUCKETS - max_exact)).astype(np.int32)
        large = np.minimum(large, N_REL_BUCKETS - 1)
        buckets.append(np.where(dist < max_exact, dist, large).reshape(1, -1))
        valids.append(np.stack([(band & ~first_half).reshape(1, -1), band.reshape(1, -1)]))
    return np.stack(buckets).astype(np.int32), np.stack(valids).astype(np.int32)


BIAS_CHUNK = 8192


def _split3(x):
    a = x.astype(BF16)
    r = x - a.astype(F32)
    b = r.astype(BF16)
    c = (r - b.astype(F32)).astype(BF16)
    return a, b, c


def bias_expand(rel_bias_t, name):
    bucket, valid = _bucket_tables()
    nq = bucket.shape[-1]

    def body(t_ref, b_ref, v_ref, o_ref):
        onehot = (lax.broadcasted_iota(jnp.int32, (N_REL_BUCKETS, BIAS_CHUNK), 0) == b_ref[...]).astype(BF16)
        acc = None
        for term in _split3(t_ref[...]):
            p = jnp.dot(term, onehot, preferred_element_type=F32)
            acc = p if acc is None else acc + p
        o_ref[...] = jnp.where(v_ref[...] > 0, acc, MASK_VALUE)

    return pl.pallas_call(
        body, name=name, grid=(3, 2, nq // BIAS_CHUNK),
        in_specs=[pl.BlockSpec((N_HEADS, N_REL_BUCKETS), lambda b, v, c: (0, 0)),
                  pl.BlockSpec((None, 1, BIAS_CHUNK), lambda b, v, c: (b, 0, c)),
                  pl.BlockSpec((None, None, 1, BIAS_CHUNK), lambda b, v, c: (b, v, 0, c))],
        out_specs=pl.BlockSpec((None, None, N_HEADS, BIAS_CHUNK), lambda b, v, c: (b, v, 0, c)),
        out_shape=_sds((3, 2, N_HEADS, nq), F32), compiler_params=_params("parallel", "parallel", "parallel"),
    )(rel_bias_t, jnp.asarray(bucket), jnp.asarray(valid))


def bias_reduce(dbias, name):
    bucket, _ = _bucket_tables()
    nq = bucket.shape[-1]
    dims = (((1,), (1,)), ((), ()))

    def body(d_ref, b_ref, o_ref):
        onehot = (lax.broadcasted_iota(jnp.int32, (N_REL_BUCKETS, BIAS_CHUNK), 0) == b_ref[...]).astype(BF16)
        acc = None
        for term in _split3(d_ref[...]):
            p = lax.dot_general(term, onehot, dims, preferred_element_type=F32)
            acc = p if acc is None else acc + p

        @pl.when(pl.program_id(1) == 0)
        def _():
            o_ref[...] = acc

        @pl.when(pl.program_id(1) > 0)
        def _():
            o_ref[...] += acc

    return pl.pallas_call(
        body, name=name, grid=(3, nq // BIAS_CHUNK),
        in_specs=[pl.BlockSpec((None, N_HEADS, BIAS_CHUNK), lambda b, c: (b, 0, c)),
                  pl.BlockSpec((None, 1, BIAS_CHUNK), lambda b, c: (b, 0, c))],
        out_specs=pl.BlockSpec((None, N_HEADS, N_REL_BUCKETS), lambda b, c: (b, 0, 0)),
        out_shape=_sds((3, N_HEADS, N_REL_BUCKETS), F32), compiler_params=_params("parallel", "arbitrary"),
    )(dbias, jnp.asarray(bucket))


PAIR = 2 * HEAD_DIM
N_PAIRS = N_HEADS // 2
_NT = (((1,), (1,)), ((), ()))
_TN = (((0,), (0,)), ((), ()))


def _low_lanes(shape):
    return lax.broadcasted_iota(jnp.int32, shape, 1) < HEAD_DIM


ATTN_VMEM_LIMIT_BYTES = 56 * 1024 * 1024
BRANCH_ORDER = (2, 1, 0)


def _regroup(dst, src, L16):
    for r in range(RESIDUES):
        dst[pl.ds(r * L16, L16), :] = src[pl.ds(r, L16, stride=RESIDUES), :]


def _ungroup(dst, src, L16):
    for r in range(RESIDUES):
        dst[pl.ds(r, L16, stride=RESIDUES), :] = src[pl.ds(r * L16, L16), :]


def _branch_geometry(branch, S):
    dil = DILATED_PAIRS[branch][1]
    runs = RESIDUES // dil
    return dil, runs, ATT_BLOCK // runs, S // dil // ATT_BLOCK


def _block_rows(it, branch, S):
    dil, runs, run_len, n_blocks = _branch_geometry(branch, S)
    L16 = S // RESIDUES
    r, b = it // n_blocks, it % n_blocks
    prev = jnp.maximum(b - 1, 0)
    cur_rows = [pl.multiple_of((j * dil + r) * L16 + run_len * b, 8) for j in range(runs)]
    prev_rows = [pl.multiple_of((j * dil + r) * L16 + run_len * prev, 8) for j in range(runs)]
    return cur_rows, prev_rows, jnp.minimum(b, 1)


def _load_block(ref, rows, run_len):
    parts = [ref[pl.ds(o, run_len), :] for o in rows]
    return parts[0] if len(parts) == 1 else jnp.concatenate(parts, axis=0)


def _store_block(ref, rows, run_len, value, add=False):
    for j, o in enumerate(rows):
        part = value[j * run_len:(j + 1) * run_len]
        if add:
            ref[pl.ds(o, run_len), :] += part
        else:
            ref[pl.ds(o, run_len), :] = part


ATTN_FWD_UNROLL = 8
ATTN_BWD_UNROLL = 4


def _stack_heads(x, low):
    zero = jnp.zeros_like(x)
    return jnp.concatenate([jnp.where(low, x, zero), jnp.where(low, zero, x)], axis=0)


def _unstack_heads(y, low):
    return jnp.where(low, y[:ATT_BLOCK], y[ATT_BLOCK:])


def attn_fwd(qkvn, bias, name):
    S = qkvn.shape[0]
    L16 = S // RESIDUES
    n_iter = S // ATT_BLOCK

    def body(q_ref, k_ref, v_ref, b_ref, o_ref, lse_ref, stage, qp, kp, vp, acc_s, m_s, l_s):
        for src, dst in ((q_ref, qp), (k_ref, kp), (v_ref, vp)):
            stage[...] = src[...].astype(F32)
            _regroup(dst, stage, L16)
        low = _low_lanes((ATT_BLOCK, PAIR))

        for branch in BRANCH_ORDER:
            _, _, run_len, _ = _branch_geometry(branch, S)
            first = branch == BRANCH_ORDER[0]

            def step(it, carry, branch=branch, run_len=run_len, first=first):
                cur, prev, variant = _block_rows(it, branch, S)
                q = _load_block(qp, cur, run_len).astype(BF16)
                k = jnp.concatenate([_load_block(kp, prev, run_len), _load_block(kp, cur, run_len)], axis=0).astype(BF16)
                v = jnp.concatenate([_load_block(vp, prev, run_len), _load_block(vp, cur, run_len)], axis=0).astype(BF16)
                s = lax.dot_general(_stack_heads(q, low), k, _NT, preferred_element_type=F32) * (HEAD_DIM ** -0.5)
                s = s + b_ref[2 * branch + variant].reshape(2 * ATT_BLOCK, 2 * ATT_BLOCK)
                mx = jnp.max(s, axis=-1, keepdims=True)
                p = jnp.exp(s - mx)
                den = jnp.sum(p, axis=-1, keepdims=True)
                pv = jnp.dot(p.astype(BF16), v, preferred_element_type=F32)
                acc = _unstack_heads(pv, low)
                m = _unstack_heads(mx, low)
                l = _unstack_heads(den, low)
                if not first:
                    m_old = _load_block(m_s, cur, run_len)
                    m_new = jnp.maximum(m_old, m)
                    a_old, a_new = jnp.exp(m_old - m_new), jnp.exp(m - m_new)
                    acc = _load_block(acc_s, cur, run_len) * a_old + acc * a_new
                    l = _load_block(l_s, cur, run_len) * a_old + l * a_new
                    m = m_new
                _store_block(acc_s, cur, run_len, acc)
                _store_block(m_s, cur, run_len, m)
                _store_block(l_s, cur, run_len, l)
                return carry

            lax.fori_loop(0, n_iter, step, 0, unroll=ATTN_FWD_UNROLL)

        acc_s[...] = acc_s[...] / l_s[...]
        _ungroup(stage, acc_s, L16)
        o_ref[...] = stage[...].astype(BF16)
        m_s[...] = m_s[...] + jnp.log(l_s[...])
        _ungroup(lse_ref, m_s, L16)

    col = lambda part: pl.BlockSpec((S, PAIR), lambda hp: (0, part * N_PAIRS + hp))
    out = pl.BlockSpec((S, PAIR), lambda hp: (0, hp))
    return pl.pallas_call(
        body, name=name, grid=(N_PAIRS,),
        in_specs=[col(0), col(1), col(2), pl.BlockSpec((6, 2, ATT_BLOCK, 2 * ATT_BLOCK), lambda hp: (0, hp, 0, 0))],
        out_specs=[out, out], out_shape=[_sds((S, D_MODEL), BF16), _sds((S, D_MODEL), F32)],
        scratch_shapes=[pltpu.VMEM((S, PAIR), F32)] * 7,
        compiler_params=pltpu.CompilerParams(dimension_semantics=("parallel",), vmem_limit_bytes=ATTN_VMEM_LIMIT_BYTES),
    )(qkvn, qkvn, qkvn, bias)


def attn_bwd(qkvn, att, datt, lse, bias, name):
    S = qkvn.shape[0]
    L16 = S // RESIDUES
    n_iter = S // ATT_BLOCK
    TILE = 512

    def body(q_ref, k_ref, v_ref, o_ref, do_ref, lse_ref, b_ref, dq_ref, dk_ref, dv_ref, db_ref,
             qp, kp, vp, dop, ldp, dqp, dkp, dvp):
        stage = dqp
        for src, dst in ((q_ref, qp), (k_ref, kp), (v_ref, vp), (do_ref, dop)):
            stage[...] = src[...].astype(F32)
            _regroup(dst, stage, L16)

        def pack(i, carry):
            rows = pl.ds(pl.multiple_of(i * TILE, TILE), TILE)
            low = _low_lanes((TILE, PAIR))
            lane = lax.broadcasted_iota(jnp.int32, (TILE, PAIR), 1)
            prod = do_ref[rows, :].astype(F32) * o_ref[rows, :].astype(F32)
            d0 = jnp.sum(jnp.where(low, prod, 0.0), axis=-1, keepdims=True)
            d1 = jnp.sum(jnp.where(low, 0.0, prod), axis=-1, keepdims=True)
            stage[rows, :] = jnp.where((lane & (HEAD_DIM // 2)) == 0, lse_ref[rows, :], jnp.where(low, d0, d1))
            return carry

        lax.fori_loop(0, S // TILE, pack, 0)
        _regroup(ldp, stage, L16)
        dqp[...] = jnp.zeros_like(dqp)
        dkp[...] = jnp.zeros_like(dkp)
        dvp[...] = jnp.zeros_like(dvp)
        db_ref[...] = jnp.zeros_like(db_ref)
        low = _low_lanes((ATT_BLOCK, PAIR))

        for branch in BRANCH_ORDER:
            _, _, run_len, _ = _branch_geometry(branch, S)

            def step(it, carry, branch=branch, run_len=run_len):
                cur, prev, variant = _block_rows(it, branch, S)
                q = _load_block(qp, cur, run_len).astype(BF16)
                dout = _load_block(dop, cur, run_len).astype(BF16)
                ld = _load_block(ldp, cur, run_len)
                k = jnp.concatenate([_load_block(kp, prev, run_len), _load_block(kp, cur, run_len)], axis=0).astype(BF16)
                v = jnp.concatenate([_load_block(vp, prev, run_len), _load_block(vp, cur, run_len)], axis=0).astype(BF16)
                half = HEAD_DIM // 2
                lse2 = jnp.concatenate([ld[:, 0:1], ld[:, HEAD_DIM:HEAD_DIM + 1]], axis=0)
                delta2 = jnp.concatenate([ld[:, half:half + 1], ld[:, HEAD_DIM + half:HEAD_DIM + half + 1]], axis=0)
                q2, do2 = _stack_heads(q, low), _stack_heads(dout, low)
                s = lax.dot_general(q2, k, _NT, preferred_element_type=F32) * (HEAD_DIM ** -0.5)
                p = jnp.exp(s + b_ref[2 * branch + variant].reshape(2 * ATT_BLOCK, 2 * ATT_BLOCK) - lse2)
                dp = lax.dot_general(do2, v, _NT, preferred_element_type=F32)
                ds = p * (dp - delta2)
                db_ref[branch] += ds.reshape(2, ATT_BLOCK, 2 * ATT_BLOCK)
                dsb = (ds * (HEAD_DIM ** -0.5)).astype(BF16)
                dq = _unstack_heads(jnp.dot(dsb, k, preferred_element_type=F32), low)
                dk = lax.dot_general(dsb, q2, _TN, preferred_element_type=F32)
                dv = lax.dot_general(p.astype(BF16), do2, _TN, preferred_element_type=F32)
                _store_block(dqp, cur, run_len, dq, add=True)
                _store_block(dkp, prev, run_len, dk[:ATT_BLOCK], add=True)
                _store_block(dvp, prev, run_len, dv[:ATT_BLOCK], add=True)
                _store_block(dkp, cur, run_len, dk[ATT_BLOCK:], add=True)
                _store_block(dvp, cur, run_len, dv[ATT_BLOCK:], add=True)
                return carry

            lax.fori_loop(0, n_iter, step, 0, unroll=ATTN_BWD_UNROLL)

        _ungroup(dq_ref, dqp, L16)
        _ungroup(dk_ref, dkp, L16)
        _ungroup(dv_ref, dvp, L16)

    col = lambda part: pl.BlockSpec((S, PAIR), lambda hp: (0, part * N_PAIRS + hp))
    one = pl.BlockSpec((S, PAIR), lambda hp: (0, hp))
    return pl.pallas_call(
        body, name=name, grid=(N_PAIRS,),
        in_specs=[col(0), col(1), col(2), one, one, one,
                  pl.BlockSpec((6, 2, ATT_BLOCK, 2 * ATT_BLOCK), lambda hp: (0, hp, 0, 0))],
        out_specs=[one, one, one, pl.BlockSpec((3, 2, ATT_BLOCK, 2 * ATT_BLOCK), lambda hp: (0, hp, 0, 0))],
        out_shape=[_sds((S, D_MODEL), F32)] * 3 + [_sds((3, N_HEADS, ATT_BLOCK, 2 * ATT_BLOCK), F32)],
        scratch_shapes=[pltpu.VMEM((S, PAIR), F32)] * 8,
        compiler_params=pltpu.CompilerParams(dimension_semantics=("parallel",), vmem_limit_bytes=ATTN_VMEM_LIMIT_BYTES),
    )(qkvn, qkvn, qkvn, att, datt, lse, bias)


def _adamw_step(w_ref, g_ref, m_ref, v_ref, d_ref, nm_ref, nv_ref):
    gv = g_ref[...]
    m2 = ADAM_B1 * m_ref[...] + (1.0 - ADAM_B1) * gv
    v2 = ADAM_B2 * v_ref[...] + (1.0 - ADAM_B2) * (gv * gv)
    m_hat = m2 / (1.0 - ADAM_B1 ** ADAM_STEP)
    v_hat = v2 / (1.0 - ADAM_B2 ** ADAM_STEP)
    d_ref[...] = -ADAM_LR * (m_hat / (jnp.sqrt(v_hat) + ADAM_EPS) + ADAM_WD * w_ref[...])
    nm_ref[...] = m2
    nv_ref[...] = v2


def adamw_small(ws, gs, ms, vs, name):
    n = len(ws)

    def body(*refs):
        groups = [refs[k * n:(k + 1) * n] for k in range(7)]
        for refs_of_one in zip(*groups):
            _adamw_step(*refs_of_one)

    outs = pl.pallas_call(body, name=name, out_shape=[_sds(a.shape, F32) for a in ws] * 3,
                          compiler_params=_params())(*ws, *gs, *ms, *vs)
    return outs[:n], outs[n:2 * n], outs[2 * n:]


def adamw(w, g, m, v, name):
    n, R, C = w.shape

    def body(w_ref, g_ref, m_ref, v_ref, d_ref, nm_ref, nv_ref, go_ref):
        go_ref[...] = g_ref[...]
        _adamw_step(w_ref, g_ref, m_ref, v_ref, d_ref, nm_ref, nv_ref)

    tr = R
    while tr * C * 4 > ELEMENTWISE_BLOCK_BYTES and tr % 16 == 0:
        tr //= 2
    spec = pl.BlockSpec((None, tr, C), lambda i, r: (i, r, 0))
    return pl.pallas_call(
        body, name=name, grid=(n, R // tr), in_specs=[spec] * 4, out_specs=[spec] * 4,
        out_shape=[_sds((n, R, C), F32)] * 4, compiler_params=_params("parallel", "parallel"),
    )(w, g, m, v)


ANY = pl.BlockSpec(memory_space=pl.ANY)


def _coords():
    return lax.axis_index("x"), lax.axis_index("y"), lax.axis_index("c")


def _other_chips(mx, my):
    return [(1 - mx, my), (mx, 1 - my), (1 - mx, 1 - my)]


def _remote(src, dst, send, recv, dev):
    return pltpu.make_async_remote_copy(src_ref=src, dst_ref=dst, send_sem=send, recv_sem=recv, device_id=dev,
                                        device_id_type=MESH)


HBM =pl.BlockSpec(memory_space=pltpu.HBM)
SEM = pl.BlockSpec(memory_space=pltpu.SEMAPHORE)
_SPLIT_COPY = pltpu.CompilerParams(has_side_effects=pltpu.SideEffectType.DATAFLOW_SIDE_EFFECTING)


def _in_hbm(a):
    return pltpu.with_memory_space_constraint(a, pltpu.HBM)


def cast_into_slot(w, layer, chip_core, name, dtype=BF16):
    _, _, hR, C = w.shape

    def body(s_ref, w_ref, o_ref):
        del s_ref
        o_ref[...] = w_ref[...].astype(dtype)

    grid_spec = pltpu.PrefetchScalarGridSpec(
        num_scalar_prefetch=1, grid=(2,),
        in_specs=[pl.BlockSpec((None, None, hR, C), lambda h, s: (layer, h, 0, 0))],
        out_specs=pl.BlockSpec((None, None, hR, C), lambda h, s: (s[0], h, 0, 0)))
    return pl.pallas_call(body, name=name, grid_spec=grid_spec, out_shape=_sds_hbm((N_CHIPS, 2, hR, C), dtype),
                          compiler_params=_params("parallel"))(chip_core, w)


def gather_start(lands, groups, name):
    n = len(lands)
    n_groups = len(groups)

    def body(*refs):
        ins = refs[:n]
        sems = refs[n:n + 2 * n_groups]
        token = refs[-1]
        mx, my, mc = _coords()
        chip = 2 * mx + my
        for g, members in enumerate(groups):
            send, recv = sems[2 * g], sems[2 * g + 1]
            for i, a in enumerate(members):
                mine = ins[a].at[chip, mc]
                for k, (px, py) in enumerate(_other_chips(mx, my)):
                    _remote(mine, mine, send.at[3 * i + k], recv.at[3 * i + k], (px, py, mc)).start()
        token[...] = jnp.zeros_like(token)

    sem_shapes = []
    for members in groups:
        sem_shapes += [pltpu.SemaphoreType.DMA((3 * len(members),))] * 2
    outs = pl.pallas_call(
        body, name=name, in_specs=[HBM] * n,
        out_specs=[SEM] * (2 * n_groups) + [HBM] * n + [pl.BlockSpec(memory_space=pltpu.VMEM)],
        out_shape=sem_shapes + [pltpu.HBM(a.shape, a.dtype) for a in lands] + [_sds((SUBLANES, LANES), F32)],
        input_output_aliases={a: 2 * n_groups + a for a in range(n)}, compiler_params=_SPLIT_COPY,
    )(*[_in_hbm(a) for a in lands])
    sems = [(outs[2 * g], outs[2 * g + 1]) for g in range(n_groups)]
    return sems, list(outs[2 * n_groups:2 * n_groups + n]), outs[-1]


def gather_forward(lands, sems, after, name):
    n = len(lands)

    def body(*refs):
        ins = refs[:n]
        send, recv = refs[n], refs[n + 1]
        fsend, frecv = refs[n + 3], refs[n + 4]
        mx, my, mc = _coords()
        for i in range(n):
            for k, (px, py) in enumerate(_other_chips(mx, my)):
                landed = ins[i].at[2 * px + py, mc]
                cp = _remote(landed, landed, send.at[3 * i + k], recv.at[3 * i + k], (px, py, mc))
                cp.wait_send()
                cp.wait_recv()
                _remote(landed, landed, fsend.at[3 * i + k], frecv.at[3 * i + k], (mx, my, 1 - mc)).start()

    outs = pl.pallas_call(
        body, name=name, in_specs=[HBM] * n + [SEM, SEM, ANY], out_specs=[SEM, SEM] + [HBM] * n,
        out_shape=[pltpu.SemaphoreType.DMA((3 * n,))] * 2 + [pltpu.HBM(a.shape, a.dtype) for a in lands],
        input_output_aliases={a: 2 + a for a in range(n)}, compiler_params=_SPLIT_COPY,
    )(*lands, sems[0], sems[1], after)
    return (outs[0], outs[1]), list(outs[2:])


def gather_wait(lands, sems, after, name):
    n = len(lands)

    def body(*refs):
        ins = refs[:n]
        fsend, frecv = refs[n], refs[n + 1]
        mx, my, mc = _coords()
        for i in range(n):
            for k, (px, py) in enumerate(_other_chips(mx, my)):
                theirs = ins[i].at[2 * px + py, 1 - mc]
                cp = _remote(theirs, theirs, fsend.at[3 * i + k], frecv.at[3 * i + k], (mx, my, 1 - mc))
                cp.wait_send()
                cp.wait_recv()

    outs = pl.pallas_call(
        body, name=name, in_specs=[HBM] * n + [SEM, SEM, ANY], out_specs=[HBM] * n,
        out_shape=[pltpu.HBM(a.shape, a.dtype) for a in lands],
        input_output_aliases={a: a for a in range(n)}, compiler_params=_SPLIT_COPY,
    )(*lands, sems[0], sems[1], after)
    return list(outs)


def _peers(mx, my, mc):
    return [(1 - mx if k & 4 else mx, 1 - my if k & 2 else my, 1 - mc if k & 1 else mc) for k in range(1, N_DEV)]


def devices_start(x, name):
    def body(x_ref, land_ref, send, recv, x_thru, land_thru):
        mx, my, mc = _coords()
        me = 4 * mx + 2 * my + mc
        for k, peer in enumerate(_peers(mx, my, mc)):
            _remote(x_ref, land_ref.at[me], send.at[k], recv.at[k], peer).start()

    land = lax.empty((N_DEV,) + x.shape, x.dtype)
    outs = pl.pallas_call(
        body, name=name, in_specs=[HBM, HBM], out_specs=[SEM, SEM, HBM, HBM],
        out_shape=[pltpu.SemaphoreType.DMA((N_DEV - 1,))] * 2 + [pltpu.HBM(x.shape, x.dtype), pltpu.HBM(land.shape, x.dtype)],
        input_output_aliases={0: 2, 1: 3}, compiler_params=_SPLIT_COPY,
    )(_in_hbm(x), _in_hbm(land))
    return (outs[0], outs[1]), outs[2], outs[3]


def devices_wait(x, land, sems, after, name):
    def body(x_ref, land_ref, send, recv, after_ref, x_thru, land_thru):
        mx, my, mc = _coords()
        for k, (px, py, pc) in enumerate(_peers(mx, my, mc)):
            cp = _remote(x_ref, land_ref.at[4 * px + 2 * py + pc], send.at[k], recv.at[k], (px, py, pc))
            cp.wait_send()
            cp.wait_recv()

    outs = pl.pallas_call(
        body, name=name, in_specs=[HBM, HBM, SEM, SEM, ANY], out_specs=[HBM, HBM],
        out_shape=[pltpu.HBM(x.shape, x.dtype), pltpu.HBM(land.shape, land.dtype)],
        input_output_aliases={0: 0, 1: 1}, compiler_params=_SPLIT_COPY,
    )(x, land, sems[0], sems[1], after)
    return outs[0], outs[1]


def device_sum(land, own, me, name):
    _, R, C = land.shape

    def body(s_ref, l_ref, o_ref_in, o_ref):
        acc = None
        for q in range(N_DEV):
            term = jnp.where(s_ref[0] == q, o_ref_in[...], l_ref[q])
            acc = term if acc is None else acc + term
        o_ref[...] = acc

    grid_spec = pltpu.PrefetchScalarGridSpec(
        num_scalar_prefetch=1, grid=(1,),
        in_specs=[pl.BlockSpec((N_DEV, R, C), lambda i, s: (0, 0, 0)), pl.BlockSpec((R, C), lambda i, s: (0, 0))],
        out_specs=pl.BlockSpec((R, C), lambda i, s: (0, 0)))
    return pl.pallas_call(body, name=name, grid_spec=grid_spec, out_shape=_sds((R, C), F32),
                          compiler_params=_params("arbitrary"))(me, land, own)


def reduce_send(grads, name):
    n = len(grads)

    def body(*refs):
        ins, lands = refs[:n], refs[n:2 * n]
        send, recv = refs[2 * n], refs[2 * n + 1]
        mx, my, mc = _coords()
        me = 4 * mx + 2 * my + mc
        for a in range(n):
            for k, (px, py, pc) in enumerate(_peers(mx, my, mc)):
                _remote(ins[a].at[2 * px + py, pc], lands[a].at[me], send.at[7 * a + k], recv.at[7 * a + k], (px, py, pc)).start()

    lands = [lax.empty((N_DEV,) + g.shape[2:], g.dtype) for g in grads]
    outs = pl.pallas_call(
        body, name=name, in_specs=[HBM] * (2 * n), out_specs=[SEM, SEM] + [HBM] * (2 * n),
        out_shape=[pltpu.SemaphoreType.DMA((7 * n,))] * 2 + [pltpu.HBM(a.shape, a.dtype) for a in grads + lands],
        input_output_aliases={a: 2 + a for a in range(2 * n)}, compiler_params=_SPLIT_COPY,
    )(*[_in_hbm(a) for a in grads + lands])
    return (outs[0], outs[1]), list(outs[2:2 + n]), list(outs[2 + n:])


def reduce_wait(grads, lands, sems, after, name):
    n = len(grads)

    def body(*refs):
        ins, zones = refs[:n], refs[n:2 * n]
        send, recv = refs[2 * n], refs[2 * n + 1]
        mx, my, mc = _coords()
        for a in range(n):
            for k, (px, py, pc) in enumerate(_peers(mx, my, mc)):
                cp = _remote(ins[a].at[2 * px + py, pc], zones[a].at[4 * px + 2 * py + pc], send.at[7 * a + k],
                             recv.at[7 * a + k], (px, py, pc))
                cp.wait_send()
                cp.wait_recv()

    outs = pl.pallas_call(
        body, name=name, in_specs=[HBM] * (2 * n) + [SEM, SEM, ANY], out_specs=[HBM] * (2 * n),
        out_shape=[pltpu.HBM(a.shape, a.dtype) for a in grads + lands],
        input_output_aliases={a: a for a in range(2 * n)}, compiler_params=_SPLIT_COPY,
    )(*grads, *lands, sems[0], sems[1], after)
    return list(outs[:n]), list(outs[n:])


def reduce_sum(land, grad, place, name, into=None, layer=None):
    _, hR, C = land.shape
    tr = hR
    while N_DEV * tr * C * 2 > 3 * ELEMENTWISE_BLOCK_BYTES and tr % 32 == 0:
        tr //= 2

    def body(s_ref, l_ref, g_ref, *rest):
        o_ref = rest[-1]
        own = g_ref[...].astype(F32)
        acc = None
        for q in range(N_DEV):
            term = jnp.where(s_ref[2] == q, own, l_ref[q].astype(F32))
            acc = term if acc is None else acc + term
        o_ref[...] = acc

    in_specs = [pl.BlockSpec((N_DEV, tr, C), lambda i, s: (0, i, 0)),
                pl.BlockSpec((None, None, tr, C), lambda i, s: (s[0], s[1], i, 0))]
    args = [place, _in_hbm(land), _in_hbm(grad)]
    aliases = {}
    if layer is None:
        out_spec = pl.BlockSpec((None, tr, C), lambda i, s: (s[1], i, 0))
        out_shape = _sds_hbm((2, hR, C), F32)
    else:
        out_spec = pl.BlockSpec((None, None, tr, C), lambda i, s: (layer, s[1], i, 0))
        out_shape = _sds_hbm((2, 2, hR, C), F32)
        if into is not None:
            in_specs.append(ANY)
            args.append(into)
            aliases = {3: 0}
    grid_spec = pltpu.PrefetchScalarGridSpec(num_scalar_prefetch=1, grid=(hR // tr,), in_specs=in_specs, out_specs=out_spec)
    return pl.pallas_call(body, name=name, grid_spec=grid_spec, out_shape=out_shape, input_output_aliases=aliases,
                          compiler_params=_params("arbitrary"))(*args)


def join_halves(arrays, name):
    n = len(arrays)
    pieces = [(a, l) for a, arr in enumerate(arrays) for l in (range(arr.shape[0]) if arr.ndim == 4 else [None])]

    def body(*refs):
        ins = refs[:n]
        send, recv = refs[2 * n:]
        mx, my, mc = _coords()

        def half(a, l, h):
            return ins[a].at[h] if l is None else ins[a].at[l, h]

        sends = [_remote(half(a, l, mc), half(a, l, mc), send.at[i], recv.at[i], (mx, my, 1 - mc))
                 for i, (a, l) in enumerate(pieces)]
        for cp in sends:
            cp.start()
        for i, (a, l) in enumerate(pieces):
            theirs = half(a, l, 1 - mc)
            _remote(theirs, theirs, send.at[i], recv.at[i], (mx, my, 1 - mc)).wait_recv()
        for cp in sends:
            cp.wait_send()

    return pl.pallas_call(
        body, name=name, in_specs=[ANY] * n, out_specs=[ANY] * n, out_shape=[_sds(a.shape, a.dtype) for a in arrays],
        input_output_aliases={a: a for a in range(n)},
        scratch_shapes=[pltpu.SemaphoreType.DMA((len(pieces),)), pltpu.SemaphoreType.DMA((len(pieces),))],
    )(*arrays)


LANES = 128
SUBLANES = 8


def _n_rows(shape):
    rows = -(-int(np.prod(shape)) // LANES)
    return -(-rows // SUBLANES) * SUBLANES


def _as_rows(a):
    flat = a.reshape(-1)
    rows = _n_rows(a.shape)
    return jnp.pad(flat, (0, rows * LANES - flat.shape[0])).reshape(rows, LANES)


def _pack(arrays):
    return jnp.concatenate([_as_rows(a) for a in arrays], axis=0)


def _unpack(rows, shapes):
    out, r0 = [], 0
    for s in shapes:
        n = _n_rows(s)
        out.append(rows[r0:r0 + n].reshape(-1)[:int(np.prod(s))].reshape(s))
        r0 += n
    return out


REPLICATED_SMALL = [("rel_bias", (32, 16)), ("even_norm", (1, 1024)), ("even_pool_w", (1, 4, 128, 128)),
                    ("even_pool_scale", (1, 512)), ("odd_q_norm", (1, 64)), ("odd_k_norm", (1, 64)),
                    ("ffn_norm", (2, 1024)), ("ffn_conv_b", (2, 5632))]
SHARDED_SMALL = [("even_conv_w", (1, 3, 128)), ("odd_norm", (1, 256)), ("ffn_conv_w", (2, 3, 1408))]
BIG = ["even_w_in", "even_w_out", "odd_w_qkv", "odd_w_o", "ffn_w_up", "ffn_w_down"]
WEIGHT_ORDER = ["rel_bias", "even_norm", "even_w_in", "even_conv_w", "even_pool_w", "even_pool_scale", "even_w_out",
                "odd_norm", "odd_w_qkv", "odd_q_norm", "odd_k_norm", "odd_w_o", "ffn_norm", "ffn_w_up", "ffn_conv_w",
                "ffn_conv_b", "ffn_w_down"]


def kernel(x, rel_bias, even_norm, even_w_in, even_conv_w, even_pool_w, even_pool_scale, even_w_out, odd_norm, odd_w_qkv, odd_q_norm, odd_k_norm, odd_w_o, ffn_norm, ffn_w_up, ffn_conv_w, ffn_conv_b, ffn_w_down, loss_target, m_rel_bias, m_even_norm, m_even_w_in, m_even_conv_w, m_even_pool_w, m_even_pool_scale, m_even_w_out, m_odd_norm, m_odd_w_qkv, m_odd_q_norm, m_odd_k_norm, m_odd_w_o, m_ffn_norm, m_ffn_w_up, m_ffn_conv_w, m_ffn_conv_b, m_ffn_w_down, v_rel_bias, v_even_norm, v_even_w_in, v_even_conv_w, v_even_pool_w, v_even_pool_scale, v_even_w_out, v_odd_norm, v_odd_w_qkv, v_odd_q_norm, v_odd_k_norm, v_odd_w_o, v_ffn_norm, v_ffn_w_up, v_ffn_conv_w, v_ffn_conv_b, v_ffn_w_down):
    W = dict(rel_bias=rel_bias, even_norm=even_norm, even_w_in=even_w_in, even_conv_w=even_conv_w, even_pool_w=even_pool_w,
             even_pool_scale=even_pool_scale, even_w_out=even_w_out, odd_norm=odd_norm, odd_w_qkv=odd_w_qkv,
             odd_q_norm=odd_q_norm, odd_k_norm=odd_k_norm, odd_w_o=odd_w_o, ffn_norm=ffn_norm, ffn_w_up=ffn_w_up,
             ffn_conv_w=ffn_conv_w, ffn_conv_b=ffn_conv_b, ffn_w_down=ffn_w_down)
    M1 = dict(rel_bias=m_rel_bias, even_norm=m_even_norm, even_w_in=m_even_w_in, even_conv_w=m_even_conv_w,
              even_pool_w=m_even_pool_w, even_pool_scale=m_even_pool_scale, even_w_out=m_even_w_out, odd_norm=m_odd_norm,
              odd_w_qkv=m_odd_w_qkv, odd_q_norm=m_odd_q_norm, odd_k_norm=m_odd_k_norm, odd_w_o=m_odd_w_o,
              ffn_norm=m_ffn_norm, ffn_w_up=m_ffn_w_up, ffn_conv_w=m_ffn_conv_w, ffn_conv_b=m_ffn_conv_b,
              ffn_w_down=m_ffn_w_down)
    M2 = dict(rel_bias=v_rel_bias, even_norm=v_even_norm, even_w_in=v_even_w_in, even_conv_w=v_even_conv_w,
              even_pool_w=v_even_pool_w, even_pool_scale=v_even_pool_scale, even_w_out=v_even_w_out, odd_norm=v_odd_norm,
              odd_w_qkv=v_odd_w_qkv, odd_q_norm=v_odd_q_norm, odd_k_norm=v_odd_k_norm, odd_w_o=v_odd_w_o,
              ffn_norm=v_ffn_norm, ffn_w_up=v_ffn_w_up, ffn_conv_w=v_ffn_conv_w, ffn_conv_b=v_ffn_conv_b,
              ffn_w_down=v_ffn_w_down)
    mx, my, mc = _coords()
    chip = 2 * mx + my
    me = 4 * mx + 2 * my + mc
    place = jnp.stack([chip, mc, me]).astype(jnp.int32)
    xs, target = x[0], loss_target[0]

    def halves(w):
        return w.reshape((w.shape[0], 2, w.shape[-2] // 2, w.shape[-1]))

    small_rows = jnp.pad(_pack([even_conv_w, odd_norm, ffn_conv_w]), ((0, SUBLANES), (0, 0)))
    first = [cast_into_slot(halves(even_w_in), 0, place, "cast_w_in"), cast_into_slot(halves(even_w_out), 0, place, "cast_w_out"),
             cast_into_slot(small_rows.reshape(1, 2, small_rows.shape[0] // 2, LANES), 0, place, "small_into_slot", dtype=F32)]
    first_sems, first, token = gather_start(first, [[0, 1, 2]], "gather_start_first")
    even_norm_after_start = even_norm + token[0:1, 0:1]

    def later(a):
        return lax.optimization_barrier((a, token))[0]

    up_f32, down_f32 = halves(later(ffn_w_up)), halves(later(ffn_w_down))
    rest = [cast_into_slot(up_f32, 0, place, "cast_w_up0"), cast_into_slot(down_f32, 0, place, "cast_w_down0"),
            cast_into_slot(halves(later(odd_w_qkv)), 0, place, "cast_w_qkv"), cast_into_slot(halves(later(odd_w_o)), 0, place, "cast_w_o"),
            cast_into_slot(up_f32, 1, place, "cast_w_up1"), cast_into_slot(down_f32, 1, place, "cast_w_down1")]
    rest_sems, rest, _ = gather_start(rest, [[0], [1], [2, 3], [4], [5]], "gather_start_rest")
    group_arrays = [first, [rest[0]], [rest[1]], [rest[2], rest[3]], [rest[4]], [rest[5]]]
    group_sems = first_sems + rest_sems

    def gathered(group, tag, after_landing, after_passing):
        sems, arrays = gather_forward(group_arrays[group], group_sems[group], after_landing, "gather_forward_" + tag)
        return gather_wait(arrays, sems, after_passing, "gather_wait_" + tag)

    pool_w = cast_bf16(even_pool_w[0], "cast_pool_w")
    gqk = jnp.stack([jnp.tile(odd_q_norm[0], N_HEADS), jnp.tile(odd_k_norm[0], N_HEADS),
                     jnp.ones((D_MODEL,), F32)])[:, None, :]
    bias = bias_expand(later(rel_bias).T, "bias_expand").reshape(6, N_HEADS, ATT_BLOCK, 2 * ATT_BLOCK)
    xn0 = rmsnorm_fwd(xs, even_norm_after_start, "even_norm")
    got = gathered(0, "even", bias, xn0)
    w_in = got[0].reshape(N_CHIPS, 1, D_MODEL, EVEN_IN // N_CHIPS)
    w_out = got[1].reshape(1, 1, D_MODEL, D_MODEL)
    small = got[2].reshape(N_CHIPS, small_rows.shape[0], LANES)
    conv_w_full = small[:, 0:3].transpose(1, 0, 2).reshape(3, A_WIDTH)
    odd_norm_full = small[:, 8:10].reshape(1, D_MODEL)
    ffn_cw_full = small[:, 16:82].reshape(N_CHIPS, 2, 3, 2 * D_FF // N_CHIPS).transpose(1, 2, 0, 3).reshape(2, 3, 2 * D_FF)

    def ffn_fwd(l, xin, xn):
        up, u, act = up_glu_fwd(xn, w_up[l], ffn_cw_full[l], ffn_conv_b[l:l + 1], f"ffn{l}_up_glu")
        return act, (xin, xn, up, u, act)

    w_up, w_down = [None, None], [None, None]
    proj, mix = in_mixer_fwd(xn0, w_in, conv_w_full, pool_w, even_pool_scale, "even_in_mixer")
    x1, xn1 = mm_res_norm(mix, w_out, xs, ffn_norm[0:1], "even_out")
    w_up[0] = gathered(1, "up0", proj, x1)[0].reshape(N_CHIPS, 1, D_MODEL, 2 * D_FF // N_CHIPS)
    act0, ffn0 = ffn_fwd(0, x1, xn1)
    w_down[0] = gathered(2, "down0", act0, act0)[0].reshape(1, 1, D_FF, D_MODEL)
    x2, xn2 = mm_res_norm(act0, w_down[0], x1, odd_norm_full, "ffn0_down")
    got = gathered(3, "odd", x1, x2)
    w_qkv = got[0].reshape(N_CHIPS, 1, D_MODEL, 3 * D_MODEL // N_CHIPS)
    w_o = got[1].reshape(1, 1, D_MODEL, D_MODEL)
    qkv, qkvn = qkv_qknorm_fwd(xn2, w_qkv, gqk, "odd_qkv_qknorm")
    att, lse = attn_fwd(qkvn, bias, "attn_fwd")
    x3, xn3 = mm_res_norm(att, w_o, x2, ffn_norm[1:2], "odd_out")
    w_up[1] = gathered(4, "up1", x2, x3)[0].reshape(N_CHIPS, 1, D_MODEL, 2 * D_FF // N_CHIPS)
    act1, ffn1 = ffn_fwd(1, x3, xn3)
    w_down[1] = gathered(5, "down1", act1, act1)[0].reshape(1, 1, D_FF, D_MODEL)
    dy, dyb, sq = mm_res_loss(act1, w_down[1], x3, target, "ffn1_down_loss")
    loss_part = (0.5 * jnp.sum(sq) * (1.0 / D_MODEL)).reshape(1, 1)

    def ffn_bwd(l, dy, dyb, saved):
        xin, xn, up, u, act = saved
        dw_down = mm_tn(act, dyb, f"ffn{l}_dw_down", J=1, tk=D_FF // 2, tm=1024)
        dact = mm_nt(dyb, w_down[l], f"ffn{l}_dact", tr=D_FF // 2, out_dtype=BF16, tm=1024)
        dup, dcw, dcb = glu_bwd(up, u, dact, ffn_cw_full[l], f"ffn{l}_glu_bwd")
        dw_up = mm_tn(xn, dup, f"ffn{l}_dw_up", J=N_CHIPS, tk=512, tm=1024, jb=2)
        dx, dxb, dg = mm_nt_norm_bwd(dup, w_up[l], xin, ffn_norm[l:l + 1], dy, f"ffn{l}_dx")
        return dx, dxb, (dw_down, dw_up, dcw, dcb, dg)

    def quarters(g):
        return g.reshape(N_CHIPS, 2, g.shape[0] * g.shape[1] // (2 * N_CHIPS), g.shape[-1])

    def reduce_start(grads, tag, then):
        sems, parts, zones = reduce_send([quarters(g) for g in grads], "reduce_send_" + tag)
        then, parts = lax.optimization_barrier((then, parts))
        return (sems, parts, zones), then

    dx3, dx3b, g_ffn1 = ffn_bwd(1, dy, dyb, ffn1)
    red_ffn1, (dx3, dx3b) = reduce_start([g_ffn1[1], g_ffn1[0]], "ffn1", (dx3, dx3b))
    dw_o = mm_tn(att, dx3b, "odd_dw_o", J=1, tk=512, tm=1024)
    datt = mm_nt(dx3b, w_o, "odd_datt", tr=D_MODEL, out_dtype=BF16)
    dq, dk, dv, dbias = attn_bwd(qkvn, att, datt, lse, bias, "attn_bwd")
    dqkv, dgqk = qknorm_bwd(qkv, dq, dk, dv, gqk, "odd_qknorm_bwd")
    dw_qkv = mm_tn(xn2, dqkv, "odd_dw_qkv", J=N_CHIPS, tk=512, tm=1024)
    red_odd, dqkv = reduce_start([dw_qkv, dw_o], "odd", dqkv)
    dx2, dx2b, dg_odd = mm_nt_norm_bwd(dqkv, w_qkv, x2, odd_norm_full, dx3, "odd_dx")
    dx1, dx1b, g_ffn0 = ffn_bwd(0, dx2, dx2b, ffn0)
    red_ffn0, (dx1, dx1b) = reduce_start([g_ffn0[1], g_ffn0[0]], "ffn0", (dx1, dx1b))
    dw_out = mm_tn(mix, dx1b, "even_dw_out", J=1, tk=512, tm=1024)
    dmix = mm_nt(dx1b, w_out, "even_dmix", tr=D_MODEL)
    dproj, dcw_even, dpw, dps = mixer_bwd(proj, dmix, conv_w_full, pool_w, even_pool_scale, "even_mixer_bwd")
    dw_in = mm_tn(xn0, dproj, "even_dw_in", J=N_CHIPS, tk=512, tm=1024)
    grad_x, _, dg_even = mm_nt_norm_bwd(dproj, w_in, xs, even_norm, dx1, "even_dx")
    d_rel = jnp.sum(bias_reduce(dbias.reshape(3, N_HEADS, 2 * ATT_BLOCK * ATT_BLOCK), "bias_reduce"), axis=0).T

    red_even, grad_x = reduce_start([dw_in, dw_out], "even", grad_x)

    dcw_sh = dcw_even.reshape(3, N_CHIPS, A_WIDTH // N_CHIPS).transpose(1, 0, 2)
    don_sh = dg_odd.reshape(N_CHIPS, D_MODEL // N_CHIPS)
    dfcw = jnp.stack([g_ffn0[2], g_ffn1[2]])
    dfcw_sh = dfcw.reshape(2, 3, N_CHIPS, 2 * D_FF // N_CHIPS).transpose(2, 0, 1, 3)
    rep_grads = [d_rel, dg_even, dpw[None], dps, _head_sum(dgqk[0]), _head_sum(dgqk[1]),
                 jnp.concatenate([g_ffn0[4], g_ffn1[4]], axis=0), jnp.concatenate([g_ffn0[3], g_ffn1[3]], axis=0)]
    rep_rows = _pack([loss_part] + rep_grads)
    n_loss = _n_rows(loss_part.shape)
    shard_rows = jnp.concatenate([_pack([dcw_sh[j], don_sh[j], dfcw_sh[j]]) for j in range(N_CHIPS)], axis=0)
    n_rep, n_shard = rep_rows.shape[0], shard_rows.shape[0] // N_CHIPS
    small_sems, small_rows, small_land = devices_start(jnp.concatenate([rep_rows, shard_rows], axis=0), "small_grads_start")
    grad_x, small_rows = lax.optimization_barrier((grad_x, small_rows))

    def reduce_end(red, tag, after):
        sems, parts, zones = red
        parts, zones = reduce_wait(parts, zones, sems, after, "reduce_wait_" + tag)
        return zones, parts

    z_ffn1, p_ffn1 = reduce_end(red_ffn1, "ffn1", grad_x)
    z_odd, p_odd = reduce_end(red_odd, "odd", grad_x)
    r_qkv = reduce_sum(z_odd[0], p_odd[0], place, "reduce_sum_w_qkv")
    r_o = reduce_sum(z_odd[1], p_odd[1], place, "reduce_sum_w_o")
    r_up = reduce_sum(z_ffn1[0], p_ffn1[0], place, "reduce_sum_w_up1", layer=1)
    r_down = reduce_sum(z_ffn1[1], p_ffn1[1], place, "reduce_sum_w_down1", layer=1)
    r_qkv, r_o, r_up, r_down = lax.optimization_barrier((r_qkv, r_o, r_up, r_down))
    z_ffn0, p_ffn0 = reduce_end(red_ffn0, "ffn0", r_down)
    r_up = reduce_sum(z_ffn0[0], p_ffn0[0], place, "reduce_sum_w_up0", into=r_up, layer=0)
    r_down = reduce_sum(z_ffn0[1], p_ffn0[1], place, "reduce_sum_w_down0", into=r_down, layer=0)
    later = ["odd_w_qkv", "odd_w_o", "ffn_w_up", "ffn_w_down"]
    joined = join_halves([r_qkv, r_o, r_up, r_down], "grads_join_late_layers")
    G = {nm: g.reshape(W[nm].shape) for nm, g in zip(later, joined)}

    D_, NM, NV = {}, {}, {}

    def update(nm):
        as3 = lambda a: a.reshape((-1,) + a.shape[-2:])
        outs = adamw(as3(W[nm]), as3(G[nm]), as3(M1[nm]), as3(M2[nm]), "adamw_" + nm)
        D_[nm], NM[nm], NV[nm], G[nm] = [o.reshape(W[nm].shape) for o in outs]

    def all_before(names):
        tied = lax.optimization_barrier([D_[nm] for nm in names])
        for nm, d in zip(names, tied):
            D_[nm] = d
        return tied[0]

    for nm in later:
        update(nm)
    z_even, p_even = reduce_end(red_even, "even", all_before(later))
    joined = join_halves([reduce_sum(z_even[0], p_even[0], place, "reduce_sum_w_in"),
                          reduce_sum(z_even[1], p_even[1], place, "reduce_sum_w_out")], "grads_join_first_layer")
    first = ["even_w_in", "even_w_out"]
    for nm, g in zip(first, joined):
        G[nm] = g.reshape(W[nm].shape)
        update(nm)
    small_rows, small_land = devices_wait(small_rows, small_land, small_sems, all_before(first), "small_grads_wait")
    small_sum = device_sum(small_land, small_rows, place[2:3], "small_grads_sum")
    mine = lax.dynamic_slice_in_dim(small_sum, n_rep + chip * n_shard, n_shard, axis=0)
    loss = small_sum[0, 0]
    g_small = jnp.concatenate([small_sum[n_loss:n_rep], mine], axis=0)
    small_names = [n for n, _ in REPLICATED_SMALL + SHARDED_SMALL]
    small_shapes = [s for _, s in REPLICATED_SMALL + SHARDED_SMALL]
    G.update(dict(zip(small_names, _unpack(g_small, small_shapes))))
    outs = adamw_small(*[[d[n] for n in small_names] for d in (W, G, M1, M2)], "adamw_small")
    for dst, o in zip((D_, NM, NV), outs):
        dst.update(dict(zip(small_names, o)))

    return (loss, grad_x[None], *[G[n] for n in WEIGHT_ORDER], *[D_[n] for n in WEIGHT_ORDER],
            *[NM[n] for n in WEIGHT_ORDER], *[NV[n] for n in WEIGHT_ORDER])


def _head_sum(dg):
    return jnp.sum(dg.reshape(N_HEADS, HEAD_DIM), axis=0, keepdims=True)
```

```python
import functools
import math

import numpy as np
import jax
import jax.numpy as jnp
from jax import lax
from jax.experimental import pallas as pl
from jax.experimental.pallas import tpu as pltpu

F32 = jnp.float32
BF16 = jnp.bfloat16

D_MODEL = 1024
N_HEADS = 16
HEAD_DIM = 64
A_WIDTH = 512
POOL_WINDOWS = (2, 4, 8, 16)
POOL_GROUP = 128
EVEN_IN = 2048
D_FF = 2816
DILATED_PAIRS = ((128, 1), (512, 4), (2048, 16))
ATT_BLOCK = 128
N_REL_BUCKETS = 32
REL_MAX_DISTANCE = 2048
EPS = 1e-6
MASK_VALUE = -1e30
ADAM_LR, ADAM_B1, ADAM_B2, ADAM_EPS, ADAM_WD, ADAM_STEP = 0.001, 0.9, 0.999, 1e-08, 0.01, 10

VMEM_LIMIT_BYTES = 48 * 1024 * 1024
ELEMENTWISE_BLOCK_BYTES = 2 * 1024 * 1024
N_CHIPS = 4
N_DEV = 8
MESH = pl.DeviceIdType.MESH


def _params(*sem):
    return pltpu.CompilerParams(dimension_semantics=sem if sem else None, vmem_limit_bytes=VMEM_LIMIT_BYTES)


def _sds(shape, dtype):
    return jax.ShapeDtypeStruct(tuple(shape), dtype)


def _sds_hbm(shape, dtype):
    return pltpu.HBM(tuple(shape), dtype)


def cast_bf16(x, name, tr=None):
    lead, (R, C) = x.shape[:-2], x.shape[-2:]
    n = int(np.prod(lead)) if lead else 1
    x3 = x.reshape((n, R, C))
    tr = tr or R

    def body(x_ref, o_ref):
        o_ref[...] = x_ref[...].astype(BF16)

    out = pl.pallas_call(
        body, name=name, grid=(n, R // tr),
        in_specs=[pl.BlockSpec((None, tr, C), lambda i, r: (i, r, 0))],
        out_specs=pl.BlockSpec((None, tr, C), lambda i, r: (i, r, 0)),
        out_shape=_sds((n, R, C), BF16), compiler_params=_params("parallel", "parallel"),
    )(x3)
    return out.reshape(lead + (R, C))


def rmsnorm_fwd(x, g, name, ts=512):
    S, Dm = x.shape

    def body(x_ref, g_ref, o_ref):
        xv = x_ref[...]
        r = lax.rsqrt(jnp.mean(xv * xv, axis=-1, keepdims=True) + EPS)
        o_ref[...] = ((xv * r) * g_ref[...]).astype(BF16)

    return pl.pallas_call(
        body, name=name, grid=(S // ts,),
        in_specs=[pl.BlockSpec((ts, Dm), lambda i: (i, 0)), pl.BlockSpec((1, Dm), lambda i: (0, 0))],
        out_specs=pl.BlockSpec((ts, Dm), lambda i: (i, 0)),
        out_shape=_sds((S, Dm), BF16), compiler_params=_params("parallel"),
    )(x, g)


def mm_res_norm(a, w, res, gain, name, tm=1024):
    M, K = a.shape
    Dm = w.shape[-1]

    def body(a_ref, w_ref, r_ref, g_ref, y_ref, yn_ref):
        y = r_ref[...] + jnp.dot(a_ref[...], w_ref[...], preferred_element_type=F32)
        y_ref[...] = y
        r = lax.rsqrt(jnp.mean(y * y, axis=-1, keepdims=True) + EPS)
        yn_ref[...] = ((y * r) * g_ref[...]).astype(BF16)

    row = pl.BlockSpec((tm, Dm), lambda m: (m, 0))
    return pl.pallas_call(
        body, name=name, grid=(M // tm,),
        in_specs=[pl.BlockSpec((tm, K), lambda m: (m, 0)),
                  pl.BlockSpec((None, None, K, Dm), lambda m: (0, 0, 0, 0), pipeline_mode=pl.Buffered(1)),
                  row, pl.BlockSpec((1, Dm), lambda m: (0, 0))],
        out_specs=[row, row], out_shape=[_sds((M, Dm), F32), _sds((M, Dm), BF16)],
        compiler_params=_params("parallel"),
    )(a, w, res, gain)


def mm_res_loss(a, w, res, target, name, tm=512):
    M, K = a.shape
    Dm = w.shape[-1]

    def body(a_ref, w_ref, r_ref, t_ref, d_ref, db_ref, s_ref):
        e = (r_ref[...] + jnp.dot(a_ref[...], w_ref[...], preferred_element_type=F32)) - t_ref[...]
        d = e * (1.0 / Dm)
        d_ref[...] = d
        db_ref[...] = d.astype(BF16)
        part = jnp.sum(e * e, axis=0, keepdims=True)

        @pl.when(pl.program_id(0) == 0)
        def _():
            s_ref[...] = part

        @pl.when(pl.program_id(0) > 0)
        def _():
            s_ref[...] += part

    row = pl.BlockSpec((tm, Dm), lambda m: (m, 0))
    return pl.pallas_call(
        body, name=name, grid=(M // tm,),
        in_specs=[pl.BlockSpec((tm, K), lambda m: (m, 0)),
                  pl.BlockSpec((None, None, K, Dm), lambda m: (0, 0, 0, 0), pipeline_mode=pl.Buffered(1)), row, row],
        out_specs=[row, row, pl.BlockSpec((1, Dm), lambda m: (0, 0))],
        out_shape=[_sds((M, Dm), F32), _sds((M, Dm), BF16), _sds((1, Dm), F32)],
        compiler_params=_params("arbitrary"),
    )(a, w, res, target)


def mm_nt(dy, w, name, tr, layer=0, out_dtype=F32, tm=512):
    M = dy.shape[0]
    J, _, R, Ns = w.shape
    dims = (((1,), (1,)), ((), ()))

    def body(dy_ref, w_ref, o_ref):
        acc = None
        for j in range(J):
            p = lax.dot_general(dy_ref[:, j * Ns:(j + 1) * Ns], w_ref[j], dims, preferred_element_type=F32)
            acc = p if acc is None else acc + p
        o_ref[...] = acc.astype(o_ref.dtype)

    return pl.pallas_call(
        body, name=name, grid=(R // tr, M // tm),
        in_specs=[pl.BlockSpec((tm, J * Ns), lambda r, m: (m, 0)),
                  pl.BlockSpec((J, None, tr, Ns), lambda r, m: (0, layer, r, 0))],
        out_specs=pl.BlockSpec((tm, tr), lambda r, m: (m, r)),
        out_shape=_sds((M, R), out_dtype),
        compiler_params=_params("parallel", "parallel"),
    )(dy, w)


def mm_nt_norm_bwd(dy, w, x, g, dres, name, layer=0, tm=512):
    M = dy.shape[0]
    J, _, Dm, Ns = w.shape
    dims = (((1,), (1,)), ((), ()))

    def body(dy_ref, w_ref, x_ref, g_ref, r_ref, dx_ref, dxb_ref, dg_ref):
        dxn = None
        for j in range(J):
            p = lax.dot_general(dy_ref[:, j * Ns:(j + 1) * Ns], w_ref[j], dims, preferred_element_type=F32)
            dxn = p if dxn is None else dxn + p
        xv = x_ref[...]
        r = lax.rsqrt(jnp.mean(xv * xv, axis=-1, keepdims=True) + EPS)
        gx = dxn * g_ref[...]
        dot = jnp.sum(gx * xv, axis=-1, keepdims=True)
        dx = r_ref[...] + r * gx - xv * ((r * r * r) * (dot * (1.0 / Dm)))
        dx_ref[...] = dx
        dxb_ref[...] = dx.astype(BF16)
        part = jnp.sum(dxn * (xv * r), axis=0, keepdims=True)

        @pl.when(pl.program_id(0) == 0)
        def _():
            dg_ref[...] = part

        @pl.when(pl.program_id(0) > 0)
        def _():
            dg_ref[...] += part

    row = pl.BlockSpec((tm, Dm), lambda m: (m, 0))
    vec = pl.BlockSpec((1, Dm), lambda m: (0, 0))
    return pl.pallas_call(
        body, name=name, grid=(M // tm,),
        in_specs=[pl.BlockSpec((tm, J * Ns), lambda m: (m, 0)),
                  pl.BlockSpec((J, None, Dm, Ns), lambda m: (0, layer, 0, 0), pipeline_mode=pl.Buffered(1)), row, vec, row],
        out_specs=[row, row, vec],
        out_shape=[_sds((M, Dm), F32), _sds((M, Dm), BF16), _sds((1, Dm), F32)],
        compiler_params=_params("arbitrary"),
    )(dy, w, x, g, dres)


def mm_tn(a, dy, name, J, tk, tm=512, jb=None):
    M, K = a.shape
    jb = jb or J
    Ns = dy.shape[1] // J
    N = jb * Ns
    n_m = M // tm
    dims = (((0,), (0,)), ((), ()))

    def body(a_ref, dy_ref, o_ref, acc_ref):
        p = lax.dot_general(a_ref[...], dy_ref[...], dims, preferred_element_type=F32)
        m = pl.program_id(2)

        @pl.when(m == 0)
        def _():
            acc_ref[...] = p

        @pl.when(m > 0)
        def _():
            acc_ref[...] += p

        @pl.when(m == n_m - 1)
        def _():
            for j in range(jb):
                o_ref[j] = acc_ref[:, j * Ns:(j + 1) * Ns].astype(BF16)

    return pl.pallas_call(
        body, name=name, grid=(J // jb, K // tk, n_m),
        in_specs=[pl.BlockSpec((tm, tk), lambda g, k, m: (m, k)), pl.BlockSpec((tm, N), lambda g, k, m: (m, g))],
        out_specs=pl.BlockSpec((jb, tk, Ns), lambda g, k, m: (g, k, 0)),
        out_shape=_sds((J, K, Ns), BF16), scratch_shapes=[pltpu.VMEM((tk, N), F32)],
        compiler_params=_params("parallel", "parallel", "arbitrary"),
    )(a, dy)


HALO = 16


def _shift_down(x, s):
    return pltpu.roll(x, s, 0)


def _shift_up(x, s):
    return pltpu.roll(x, x.shape[0] - s, 0)


def _conv3(z, cw):
    return (_shift_down(z, 2) * cw[0:1] + _shift_down(z, 1) * cw[1:2]) + z * cw[2:3]


def _window_count(first_row, n, k):
    t = first_row + lax.broadcasted_iota(jnp.int32, (n, 1), 0)
    return jnp.clip(t + 1, 1, k).astype(F32)


def in_mixer_fwd(xn, w_in, conv_w, pool_w, pool_scale, name, ts=512):
    S, K = xn.shape
    n = ts + HALO

    def body(xm_ref, xb_ref, w_ref, cw_ref, pw_ref, ps_ref, p_ref, o_ref):
        i = pl.program_id(0)
        before = jnp.where(i > 0, xb_ref[...], jnp.zeros_like(xb_ref))
        rows = jnp.concatenate([before, xm_ref[...]], axis=0)
        h, gb, gc, pin = [jnp.dot(rows, w_ref[j], preferred_element_type=F32) for j in range(N_CHIPS)]
        for j, part in enumerate((h, gb, gc, pin)):
            p_ref[:, j * A_WIDTH:(j + 1) * A_WIDTH] = part[HALO:]
        cz = _conv3(gc * h, cw_ref[...])
        o_ref[:, 0:A_WIDTH] = (gb[HALO:] * cz[HALO:]).astype(BF16)
        for g, k in enumerate(POOL_WINDOWS):
            p = pin[:, g * POOL_GROUP:(g + 1) * POOL_GROUP]
            w = p
            s = 1
            while s < k:
                w = w + _shift_down(w, s)
                s *= 2
            pooled = w / _window_count(i * ts - HALO, n, k) - p
            yb = jnp.dot(pooled[HALO:].astype(BF16), pw_ref[g], preferred_element_type=F32)
            yb = yb * ps_ref[:, g * POOL_GROUP:(g + 1) * POOL_GROUP]
            o_ref[:, A_WIDTH + g * POOL_GROUP:A_WIDTH + (g + 1) * POOL_GROUP] = yb.astype(BF16)

    hb = ts // HALO
    return pl.pallas_call(
        body, name=name, grid=(S // ts,),
        in_specs=[
            pl.BlockSpec((ts, K), lambda i: (i, 0)),
            pl.BlockSpec((HALO, K), lambda i: (jnp.maximum(i * hb - 1, 0), 0)),
            pl.BlockSpec((N_CHIPS, None, K, A_WIDTH), lambda i: (0, 0, 0, 0), pipeline_mode=pl.Buffered(1)),
            pl.BlockSpec((3, A_WIDTH), lambda i: (0, 0)),
            pl.BlockSpec((4, POOL_GROUP, POOL_GROUP), lambda i: (0, 0, 0)),
            pl.BlockSpec((1, 4 * POOL_GROUP), lambda i: (0, 0)),
        ],
        out_specs=[pl.BlockSpec((ts, EVEN_IN), lambda i: (i, 0)), pl.BlockSpec((ts, D_MODEL), lambda i: (i, 0))],
        out_shape=[_sds((S, EVEN_IN), F32), _sds((S, D_MODEL), BF16)], compiler_params=_params("parallel"),
    )(xn, xn, w_in, conv_w, pool_w, pool_scale)


def mixer_bwd(proj, dmix, conv_w, pool_w, pool_scale, name, ts=256):
    S = proj.shape[0]
    n = ts + 2 * HALO
    nt = S // ts
    tn_dims = (((0,), (0,)), ((), ()))
    nt_dims = (((1,), (1,)), ((), ()))

    def body(pm_ref, pb_ref, pa_ref, dm_ref, da_ref, cw_ref, pw_ref, ps_ref, o_ref, dcw_ref, dpw_ref, dps_ref):
        i = pl.program_id(0)
        last = i == nt - 1
        before = jnp.where(i > 0, pb_ref[...], 0.0)
        after = jnp.where(last, 0.0, pa_ref[...])
        ext = jnp.concatenate([before, pm_ref[...], after], axis=0)
        dafter = jnp.where(last, 0.0, da_ref[...])
        dext = jnp.concatenate([jnp.zeros((HALO, D_MODEL), F32), dm_ref[...], dafter], axis=0)
        cw = cw_ref[...]
        main = slice(HALO, HALO + ts)

        @pl.when(i == 0)
        def _():
            dcw_ref[...] = jnp.zeros_like(dcw_ref)
            dpw_ref[...] = jnp.zeros_like(dpw_ref)
            dps_ref[...] = jnp.zeros_like(dps_ref)

        h, gb, gc = ext[:, 0:A_WIDTH], ext[:, A_WIDTH:2 * A_WIDTH], ext[:, 2 * A_WIDTH:3 * A_WIDTH]
        z = gc * h
        z1, z2 = _shift_down(z, 1), _shift_down(z, 2)
        cz = (z2 * cw[0:1] + z1 * cw[1:2]) + z * cw[2:3]
        dya = dext[:, 0:A_WIDTH]
        dcz = dya * gb
        dz = dcz * cw[2:3] + _shift_up(dcz, 1) * cw[1:2] + _shift_up(dcz, 2) * cw[0:1]
        o_ref[:, 0:A_WIDTH] = (dz * gc)[main].astype(BF16)
        o_ref[:, A_WIDTH:2 * A_WIDTH] = (dya * cz)[main].astype(BF16)
        o_ref[:, 2 * A_WIDTH:3 * A_WIDTH] = (dz * h)[main].astype(BF16)
        dczm = dcz[main]
        dcw_ref[0:1, :] += jnp.sum(dczm * z2[main], axis=0, keepdims=True)
        dcw_ref[1:2, :] += jnp.sum(dczm * z1[main], axis=0, keepdims=True)
        dcw_ref[2:3, :] += jnp.sum(dczm * z[main], axis=0, keepdims=True)

        for g, k in enumerate(POOL_WINDOWS):
            lo = 3 * A_WIDTH + g * POOL_GROUP
            cols = slice(g * POOL_GROUP, (g + 1) * POOL_GROUP)
            p = ext[:, lo:lo + POOL_GROUP]
            w = p
            s = 1
            while s < k:
                w = w + _shift_down(w, s)
                s *= 2
            cnt = _window_count(i * ts - HALO, n, k)
            pooled = (w / cnt - p)[main].astype(BF16)
            dyb = dext[:, A_WIDTH + g * POOL_GROUP:A_WIDTH + (g + 1) * POOL_GROUP]
            e = dyb * ps_ref[:, cols]
            pre = jnp.dot(pooled, pw_ref[g], preferred_element_type=F32)
            dps_ref[:, cols] += jnp.sum(dyb[main] * pre, axis=0, keepdims=True)
            dpw_ref[g] += lax.dot_general(pooled, e[main].astype(BF16), tn_dims, preferred_element_type=F32)
            dpooled = lax.dot_general(e.astype(BF16), pw_ref[g], nt_dims, preferred_element_type=F32)
            q = dpooled / cnt
            a = q
            s = 1
            while s < k:
                a = a + _shift_up(a, s)
                s *= 2
            o_ref[:, lo:lo + POOL_GROUP] = (a - dpooled)[main].astype(BF16)

    hb = ts // HALO
    nh = S // HALO
    before_map = lambda i: (jnp.maximum(i * hb - 1, 0), 0)
    after_map = lambda i: (jnp.minimum((i + 1) * hb, nh - 1), 0)
    full = lambda *shape: pl.BlockSpec(shape, lambda i: (0,) * len(shape))
    return pl.pallas_call(
        body, name=name, grid=(nt,),
        in_specs=[
            pl.BlockSpec((ts, EVEN_IN), lambda i: (i, 0)),
            pl.BlockSpec((HALO, EVEN_IN), before_map),
            pl.BlockSpec((HALO, EVEN_IN), after_map),
            pl.BlockSpec((ts, D_MODEL), lambda i: (i, 0)),
            pl.BlockSpec((HALO, D_MODEL), after_map),
            full(3, A_WIDTH), full(4, POOL_GROUP, POOL_GROUP), full(1, 4 * POOL_GROUP),
        ],
        out_specs=[pl.BlockSpec((ts, EVEN_IN), lambda i: (i, 0)), full(3, A_WIDTH), full(4, POOL_GROUP, POOL_GROUP),
                   full(1, 4 * POOL_GROUP)],
        out_shape=[_sds((S, EVEN_IN), BF16), _sds((3, A_WIDTH), F32), _sds((4, POOL_GROUP, POOL_GROUP), F32),
                   _sds((1, 4 * POOL_GROUP), F32)],
        compiler_params=_params("arbitrary"),
    )(proj, proj, proj, dmix, dmix, conv_w, pool_w, pool_scale)


FFN_HALO = 16
FFN_TC = 1408


GLU_CHUNKS = ((0, 512), (512, 512), (1024, 384))


def up_glu_fwd(xn, w_up, conv_w, conv_b, name, tm=512):
    S, K = xn.shape
    nc = D_FF // FFN_TC

    def body(xm_ref, xb_ref, wg_ref, wu_ref, cwg_ref, cwu_ref, cbg_ref, cbu_ref, pg_ref, pu_ref, ug_ref, uu_ref, o_ref):
        before = jnp.where(pl.program_id(1) > 0, xb_ref[...], jnp.zeros_like(xb_ref))
        rows = jnp.concatenate([before, xm_ref[...]], axis=0)
        for lo, width in GLU_CHUNKS:
            cols = slice(lo, lo + width)
            pre_g = jnp.dot(rows, wg_ref[:, cols], preferred_element_type=F32)
            pre_u = jnp.dot(rows, wu_ref[:, cols], preferred_element_type=F32)
            gate = _conv3(pre_g, cwg_ref[:, cols])[FFN_HALO:] + cbg_ref[:, cols]
            upv = _conv3(pre_u, cwu_ref[:, cols])[FFN_HALO:] + cbu_ref[:, cols]
            pg_ref[:, cols] = pre_g[FFN_HALO:].astype(BF16)
            pu_ref[:, cols] = pre_u[FFN_HALO:].astype(BF16)
            ug_ref[:, cols] = gate.astype(BF16)
            uu_ref[:, cols] = upv.astype(BF16)
            o_ref[:, cols] = ((gate * (1.0 / (1.0 + jnp.exp(-gate)))) * upv).astype(BF16)

    hb = tm // FFN_HALO
    wspec = lambda off: pl.BlockSpec((None, None, K, FFN_TC), lambda j, m: (j + off, 0, 0, 0))
    cw = lambda off: pl.BlockSpec((3, FFN_TC), lambda j, m: (0, j + off))
    cb = lambda off: pl.BlockSpec((1, FFN_TC), lambda j, m: (0, j + off))
    out = pl.BlockSpec((tm, FFN_TC), lambda j, m: (m, j))
    pg, pu, ug, uu, act = pl.pallas_call(
        body, name=name, grid=(nc, S // tm),
        in_specs=[pl.BlockSpec((tm, K), lambda j, m: (m, 0)),
                  pl.BlockSpec((FFN_HALO, K), lambda j, m: (jnp.maximum(m * hb - 1, 0), 0)),
                  wspec(0), wspec(nc), cw(0), cw(nc), cb(0), cb(nc)],
        out_specs=[out] * 5, out_shape=[_sds((S, D_FF), BF16)] * 5,
        compiler_params=_params("parallel", "parallel"),
    )(xn, xn, w_up, w_up, conv_w, conv_w, conv_b, conv_b)
    return (pg, pu), (ug, uu), act


def glu_bwd(up, u, da, conv_w, name, ts=256):
    S = up[0].shape[0]
    nc = D_FF // FFN_TC
    nt = S // ts
    W = 2 * D_FF

    def body(xg_ref, xu_ref, gm_ref, ga_ref, um_ref, ua_ref, dm_ref, da_ref, cw_ref, dx_ref, dcw_ref, dcb_ref):
        i = pl.program_id(0)
        last = i == nt - 1

        @pl.when(i == 0)
        def _():
            dcw_ref[...] = jnp.zeros_like(dcw_ref)
            dcb_ref[...] = jnp.zeros_like(dcb_ref)

        def rows(m_ref, a_ref, cols):
            return jnp.concatenate([m_ref[:, cols], a_ref[:, cols]], axis=0).astype(F32)

        def back(d, x, cols):
            cw = cw_ref[:, cols]
            d1, d2 = _shift_up(d, 1), _shift_up(d, 2)
            dx_ref[:, cols] = ((d * cw[2:3] + d1 * cw[1:2]) + d2 * cw[0:1])[:ts].astype(BF16)
            dcb_ref[:, cols] += jnp.sum(d[:ts], axis=0, keepdims=True)
            dcw_ref[0:1, cols] += jnp.sum(d2[:ts] * x, axis=0, keepdims=True)
            dcw_ref[1:2, cols] += jnp.sum(d1[:ts] * x, axis=0, keepdims=True)
            dcw_ref[2:3, cols] += jnp.sum(d[:ts] * x, axis=0, keepdims=True)

        for c in range(nc):
            cols = slice(c * FFN_TC, (c + 1) * FFN_TC)
            ug, uu = rows(gm_ref, ga_ref, cols), rows(um_ref, ua_ref, cols)
            dae = rows(dm_ref, da_ref, cols)
            dae = jnp.where(last & (lax.broadcasted_iota(jnp.int32, dae.shape, 0) >= ts), 0.0, dae)
            sg = 1.0 / (1.0 + jnp.exp(-ug))
            duu = dae * (ug * sg)
            dug = (dae * uu) * (sg * (1.0 + ug * (1.0 - sg)))
            back(dug, xg_ref[:, cols].astype(F32), cols)
            back(duu, xu_ref[:, cols].astype(F32), slice(D_FF + c * FFN_TC, D_FF + (c + 1) * FFN_TC))

    hb = ts // FFN_HALO
    nh = S // FFN_HALO
    after_map = lambda i: (jnp.minimum((i + 1) * hb, nh - 1), 0)
    main = pl.BlockSpec((ts, D_FF), lambda i: (i, 0))
    after = pl.BlockSpec((FFN_HALO, D_FF), after_map)
    return pl.pallas_call(
        body, name=name, grid=(nt,),
        in_specs=[main, main, main, after, main, after, main, after, pl.BlockSpec((3, W), lambda i: (0, 0))],
        out_specs=[pl.BlockSpec((ts, W), lambda i: (i, 0)), pl.BlockSpec((3, W), lambda i: (0, 0)),
                   pl.BlockSpec((1, W), lambda i: (0, 0))],
        out_shape=[_sds((S, W), BF16), _sds((3, W), F32), _sds((1, W), F32)],
        compiler_params=_params("arbitrary"),
    )(up[0], up[1], u[0], u[0], u[1], u[1], da, da, conv_w)


MEAN_GROUP = 256


def _head_mean_matrix():
    h = np.arange(MEAN_GROUP) // HEAD_DIM
    return jnp.asarray((h[:, None] == h[None, :]).astype(np.float32) / HEAD_DIM, dtype=BF16)


def _head_mean(v, gm):
    vb = v.astype(BF16)
    return jnp.concatenate([jnp.dot(vb[:, c:c + MEAN_GROUP], gm, preferred_element_type=F32)
                            for c in range(0, v.shape[1], MEAN_GROUP)], axis=1)


def qkv_qknorm_fwd(xn, w_qkv, gqk, name, tm=1024):
    S, K = xn.shape
    J, _, _, Ns = w_qkv.shape
    gains = gqk.reshape(1, 3 * D_MODEL)

    def body(x_ref, w_ref, g_ref, gm_ref, raw_ref, o_ref):
        first_col = pl.program_id(0) * Ns
        acc = jnp.dot(x_ref[...], w_ref[...], preferred_element_type=F32)
        raw_ref[...] = acc
        gm = gm_ref[...]
        for c in range(0, Ns, MEAN_GROUP):
            cols = slice(c, c + MEAN_GROUP)
            x = acc[:, cols]
            mean = jnp.dot((x * x).astype(BF16), gm, preferred_element_type=F32)
            normed = (x * lax.rsqrt(mean + EPS)) * g_ref[:, cols]
            o_ref[:, cols] = jnp.where(first_col + c >= 2 * D_MODEL, x, normed).astype(BF16)

    return pl.pallas_call(
        body, name=name, grid=(J, S // tm),
        in_specs=[pl.BlockSpec((tm, K), lambda j, m: (m, 0)), pl.BlockSpec((None, None, K, Ns), lambda j, m: (j, 0, 0, 0)),
                  pl.BlockSpec((1, Ns), lambda j, m: (0, j)), pl.BlockSpec((MEAN_GROUP, MEAN_GROUP), lambda j, m: (0, 0))],
        out_specs=[pl.BlockSpec((tm, Ns), lambda j, m: (m, j))] * 2,
        out_shape=[_sds((S, J * Ns), F32), _sds((S, J * Ns), BF16)], compiler_params=_params("parallel", "parallel"),
    )(xn, w_qkv, gains, _head_mean_matrix())


def qknorm_bwd(qkv, dq, dk, dv, gqk, name, ts=256):
    S = qkv.shape[0]

    def body(x_ref, dq_ref, dk_ref, dv_ref, g_ref, gm_ref, o_ref, dg_ref):
        @pl.when(pl.program_id(0) == 0)
        def _():
            dg_ref[...] = jnp.zeros_like(dg_ref)

        gm = gm_ref[...]
        for part, d_ref in enumerate((dq_ref, dk_ref)):
            cols = slice(part * D_MODEL, (part + 1) * D_MODEL)
            x = x_ref[:, cols]
            d = d_ref[...]
            r = lax.rsqrt(_head_mean(x * x, gm) + EPS)
            gx = d * g_ref[part]
            o_ref[:, cols] = (r * gx - x * ((r * r * r) * _head_mean(gx * x, gm))).astype(BF16)
            dg_ref[part] += jnp.sum(d * (x * r), axis=0, keepdims=True)
        o_ref[:, 2 * D_MODEL:] = dv_ref[...].astype(BF16)

    row = pl.BlockSpec((ts, D_MODEL), lambda i: (i, 0))
    wide = pl.BlockSpec((ts, 3 * D_MODEL), lambda i: (i, 0))
    gains = pl.BlockSpec((3, 1, D_MODEL), lambda i: (0, 0, 0))
    return pl.pallas_call(
        body, name=name, grid=(S // ts,),
        in_specs=[wide, row, row, row, gains, pl.BlockSpec((MEAN_GROUP, MEAN_GROUP), lambda i: (0, 0))],
        out_specs=[wide, gains],
        out_shape=[_sds((S, 3 * D_MODEL), BF16), _sds((3, 1, D_MODEL), F32)],
        compiler_params=_params("arbitrary"),
    )(qkv, dq, dk, dv, gqk, _head_mean_matrix())


RESIDUES = 16


def _block_order(dil):
    runs = RESIDUES // dil
    slot = np.arange(ATT_BLOCK)
    return (slot % (ATT_BLOCK // runs)) * runs + slot // (ATT_BLOCK // runs)


def _bucket_tables():
    n = ATT_BLOCK
    max_exact = N_REL_BUCKETS // 2
    buckets, valids = [], []
    for _, dil in DILATED_PAIRS:
        order = _block_order(dil)
        a = order[:, None]
        c = np.concatenate([order, n + order])[None, :]
        first_half = (np.arange(2 * n) < n)[None, :]
        rel = a + n - c
        band = (rel >= 0) & (rel <= n)
        dist = np.clip(rel, 0, n) * dil
        dd = np.maximum(dist, 1).astype(np.float32)
        large = max_exact + (np.log(dd / np.float32(max_exact)) / np.float32(math.log(REL_MAX_DISTANCE / max_exact))
                             * np.float32(N_REL_BUCKETS - max_exact)).astype(np.int32)
        large = np.minimum(large, N_REL_BUCKETS - 1)
        buckets.append(np.where(dist < max_exact, dist, large).reshape(1, -1))
        valids.append(np.stack([(band & ~first_half).reshape(1, -1), band.reshape(1, -1)]))
    return np.stack(buckets).astype(np.int32), np.stack(valids).astype(np.int32)


BIAS_CHUNK = 8192


def _split3(x):
    a = x.astype(BF16)
    r = x - a.astype(F32)
    b = r.astype(BF16)
    c = (r - b.astype(F32)).astype(BF16)
    return a, b, c


def bias_expand(rel_bias_t, name):
    bucket, valid = _bucket_tables()
    nq = bucket.shape[-1]

    def body(t_ref, b_ref, v_ref, o_ref):
        onehot = (lax.broadcasted_iota(jnp.int32, (N_REL_BUCKETS, BIAS_CHUNK), 0) == b_ref[...]).astype(BF16)
        acc = None
        for term in _split3(t_ref[...]):
            p = jnp.dot(term, onehot, preferred_element_type=F32)
            acc = p if acc is None else acc + p
        o_ref[...] = jnp.where(v_ref[...] > 0, acc, MASK_VALUE)

    return pl.pallas_call(
        body, name=name, grid=(3, 2, nq // BIAS_CHUNK),
        in_specs=[pl.BlockSpec((N_HEADS, N_REL_BUCKETS), lambda b, v, c: (0, 0)),
                  pl.BlockSpec((None, 1, BIAS_CHUNK), lambda b, v, c: (b, 0, c)),
                  pl.BlockSpec((None, None, 1, BIAS_CHUNK), lambda b, v, c: (b, v, 0, c))],
        out_specs=pl.BlockSpec((None, None, N_HEADS, BIAS_CHUNK), lambda b, v, c: (b, v, 0, c)),
        out_shape=_sds((3, 2, N_HEADS, nq), F32), compiler_params=_params("parallel", "parallel", "parallel"),
    )(rel_bias_t, jnp.asarray(bucket), jnp.asarray(valid))


def bias_reduce(dbias, name):
    bucket, _ = _bucket_tables()
    nq = bucket.shape[-1]
    dims = (((1,), (1,)), ((), ()))

    def body(d_ref, b_ref, o_ref):
        onehot = (lax.broadcasted_iota(jnp.int32, (N_REL_BUCKETS, BIAS_CHUNK), 0) == b_ref[...]).astype(BF16)
        acc = None
        for term in _split3(d_ref[...]):
            p = lax.dot_general(term, onehot, dims, preferred_element_type=F32)
            acc = p if acc is None else acc + p

        @pl.when(pl.program_id(1) == 0)
        def _():
            o_ref[...] = acc

        @pl.when(pl.program_id(1) > 0)
        def _():
            o_ref[...] += acc

    return pl.pallas_call(
        body, name=name, grid=(3, nq // BIAS_CHUNK),
        in_specs=[pl.BlockSpec((None, N_HEADS, BIAS_CHUNK), lambda b, c: (b, 0, c)),
                  pl.BlockSpec((None, 1, BIAS_CHUNK), lambda b, c: (b, 0, c))],
        out_specs=pl.BlockSpec((None, N_HEADS, N_REL_BUCKETS), lambda b, c: (b, 0, 0)),
        out_shape=_sds((3, N_HEADS, N_REL_BUCKETS), F32), compiler_params=_params("parallel", "arbitrary"),
    )(dbias, jnp.asarray(bucket))


PAIR = 2 * HEAD_DIM
N_PAIRS = N_HEADS // 2
_NT = (((1,), (1,)), ((), ()))
_TN = (((0,), (0,)), ((), ()))


def _low_lanes(shape):
    return lax.broadcasted_iota(jnp.int32, shape, 1) < HEAD_DIM


ATTN_VMEM_LIMIT_BYTES = 56 * 1024 * 1024
BRANCH_ORDER = (2, 1, 0)


def _regroup(dst, src, L16):
    for r in range(RESIDUES):
        dst[pl.ds(r * L16, L16), :] = src[pl.ds(r, L16, stride=RESIDUES), :]


def _ungroup(dst, src, L16):
    for r in range(RESIDUES):
        dst[pl.ds(r, L16, stride=RESIDUES), :] = src[pl.ds(r * L16, L16), :]


def _branch_geometry(branch, S):
    dil = DILATED_PAIRS[branch][1]
    runs = RESIDUES // dil
    return dil, runs, ATT_BLOCK // runs, S // dil // ATT_BLOCK


def _block_rows(it, branch, S):
    dil, runs, run_len, n_blocks = _branch_geometry(branch, S)
    L16 = S // RESIDUES
    r, b = it // n_blocks, it % n_blocks
    prev = jnp.maximum(b - 1, 0)
    cur_rows = [pl.multiple_of((j * dil + r) * L16 + run_len * b, 8) for j in range(runs)]
    prev_rows = [pl.multiple_of((j * dil + r) * L16 + run_len * prev, 8) for j in range(runs)]
    return cur_rows, prev_rows, jnp.minimum(b, 1)


def _load_block(ref, rows, run_len):
    parts = [ref[pl.ds(o, run_len), :] for o in rows]
    return parts[0] if len(parts) == 1 else jnp.concatenate(parts, axis=0)


def _store_block(ref, rows, run_len, value, add=False):
    for j, o in enumerate(rows):
        part = value[j * run_len:(j + 1) * run_len]
        if add:
            ref[pl.ds(o, run_len), :] += part
        else:
            ref[pl.ds(o, run_len), :] = part


ATTN_FWD_UNROLL = 8
ATTN_BWD_UNROLL = 4


def _stack_heads(x, low):
    zero = jnp.zeros_like(x)
    return jnp.concatenate([jnp.where(low, x, zero), jnp.where(low, zero, x)], axis=0)


def _unstack_heads(y, low):
    return jnp.where(low, y[:ATT_BLOCK], y[ATT_BLOCK:])


def attn_fwd(qkvn, bias, name):
    S = qkvn.shape[0]
    L16 = S // RESIDUES
    n_iter = S // ATT_BLOCK

    def body(q_ref, k_ref, v_ref, b_ref, o_ref, lse_ref, stage, qp, kp, vp, acc_s, m_s, l_s):
        for src, dst in ((q_ref, qp), (k_ref, kp), (v_ref, vp)):
            stage[...] = src[...].astype(F32)
            _regroup(dst, stage, L16)
        low = _low_lanes((ATT_BLOCK, PAIR))

        for branch in BRANCH_ORDER:
            _, _, run_len, _ = _branch_geometry(branch, S)
            first = branch == BRANCH_ORDER[0]

            def step(it, carry, branch=branch, run_len=run_len, first=first):
                cur, prev, variant = _block_rows(it, branch, S)
                q = _load_block(qp, cur, run_len).astype(BF16)
                k = jnp.concatenate([_load_block(kp, prev, run_len), _load_block(kp, cur, run_len)], axis=0).astype(BF16)
                v = jnp.concatenate([_load_block(vp, prev, run_len), _load_block(vp, cur, run_len)], axis=0).astype(BF16)
                s = lax.dot_general(_stack_heads(q, low), k, _NT, preferred_element_type=F32) * (HEAD_DIM ** -0.5)
                s = s + b_ref[2 * branch + variant].reshape(2 * ATT_BLOCK, 2 * ATT_BLOCK)
                mx = jnp.max(s, axis=-1, keepdims=True)
                p = jnp.exp(s - mx)
                den = jnp.sum(p, axis=-1, keepdims=True)
                pv = jnp.dot(p.astype(BF16), v, preferred_element_type=F32)
                acc = _unstack_heads(pv, low)
                m = _unstack_heads(mx, low)
                l = _unstack_heads(den, low)
                if not first:
                    m_old = _load_block(m_s, cur, run_len)
                    m_new = jnp.maximum(m_old, m)
                    a_old, a_new = jnp.exp(m_old - m_new), jnp.exp(m - m_new)
                    acc = _load_block(acc_s, cur, run_len) * a_old + acc * a_new
                    l = _load_block(l_s, cur, run_len) * a_old + l * a_new
                    m = m_new
                _store_block(acc_s, cur, run_len, acc)
                _store_block(m_s, cur, run_len, m)
                _store_block(l_s, cur, run_len, l)
                return carry

            lax.fori_loop(0, n_iter, step, 0, unroll=ATTN_FWD_UNROLL)

        acc_s[...] = acc_s[...] / l_s[...]
        _ungroup(stage, acc_s, L16)
        o_ref[...] = stage[...].astype(BF16)
        m_s[...] = m_s[...] + jnp.log(l_s[...])
        _ungroup(lse_ref, m_s, L16)

    col = lambda part: pl.BlockSpec((S, PAIR), lambda hp: (0, part * N_PAIRS + hp))
    out = pl.BlockSpec((S, PAIR), lambda hp: (0, hp))
    return pl.pallas_call(
        body, name=name, grid=(N_PAIRS,),
        in_specs=[col(0), col(1), col(2), pl.BlockSpec((6, 2, ATT_BLOCK, 2 * ATT_BLOCK), lambda hp: (0, hp, 0, 0))],
        out_specs=[out, out], out_shape=[_sds((S, D_MODEL), BF16), _sds((S, D_MODEL), F32)],
        scratch_shapes=[pltpu.VMEM((S, PAIR), F32)] * 7,
        compiler_params=pltpu.CompilerParams(dimension_semantics=("parallel",), vmem_limit_bytes=ATTN_VMEM_LIMIT_BYTES),
    )(qkvn, qkvn, qkvn, bias)


def attn_bwd(qkvn, att, datt, lse, bias, name):
    S = qkvn.shape[0]
    L16 = S // RESIDUES
    n_iter = S // ATT_BLOCK
    TILE = 512

    def body(q_ref, k_ref, v_ref, o_ref, do_ref, lse_ref, b_ref, dq_ref, dk_ref, dv_ref, db_ref,
             qp, kp, vp, dop, ldp, dqp, dkp, dvp):
        stage = dqp
        for src, dst in ((q_ref, qp), (k_ref, kp), (v_ref, vp), (do_ref, dop)):
            stage[...] = src[...].astype(F32)
            _regroup(dst, stage, L16)

        def pack(i, carry):
            rows = pl.ds(pl.multiple_of(i * TILE, TILE), TILE)
            low = _low_lanes((TILE, PAIR))
            lane = lax.broadcasted_iota(jnp.int32, (TILE, PAIR), 1)
            prod = do_ref[rows, :].astype(F32) * o_ref[rows, :].astype(F32)
            d0 = jnp.sum(jnp.where(low, prod, 0.0), axis=-1, keepdims=True)
            d1 = jnp.sum(jnp.where(low, 0.0, prod), axis=-1, keepdims=True)
            stage[rows, :] = jnp.where((lane & (HEAD_DIM // 2)) == 0, lse_ref[rows, :], jnp.where(low, d0, d1))
            return carry

        lax.fori_loop(0, S // TILE, pack, 0)
        _regroup(ldp, stage, L16)
        dqp[...] = jnp.zeros_like(dqp)
        dkp[...] = jnp.zeros_like(dkp)
        dvp[...] = jnp.zeros_like(dvp)
        db_ref[...] = jnp.zeros_like(db_ref)
        low = _low_lanes((ATT_BLOCK, PAIR))

        for branch in BRANCH_ORDER:
            _, _, run_len, _ = _branch_geometry(branch, S)

            def step(it, carry, branch=branch, run_len=run_len):
                cur, prev, variant = _block_rows(it, branch, S)
                q = _load_block(qp, cur, run_len).astype(BF16)
                dout = _load_block(dop, cur, run_len).astype(BF16)
                ld = _load_block(ldp, cur, run_len)
                k = jnp.concatenate([_load_block(kp, prev, run_len), _load_block(kp, cur, run_len)], axis=0).astype(BF16)
                v = jnp.concatenate([_load_block(vp, prev, run_len), _load_block(vp, cur, run_len)], axis=0).astype(BF16)
                half = HEAD_DIM // 2
                lse2 = jnp.concatenate([ld[:, 0:1], ld[:, HEAD_DIM:HEAD_DIM + 1]], axis=0)
                delta2 = jnp.concatenate([ld[:, half:half + 1], ld[:, HEAD_DIM + half:HEAD_DIM + half + 1]], axis=0)
                q2, do2 = _stack_heads(q, low), _stack_heads(dout, low)
                s = lax.dot_general(q2, k, _NT, preferred_element_type=F32) * (HEAD_DIM ** -0.5)
                p = jnp.exp(s + b_ref[2 * branch + variant].reshape(2 * ATT_BLOCK, 2 * ATT_BLOCK) - lse2)
                dp = lax.dot_general(do2, v, _NT, preferred_element_type=F32)
                ds = p * (dp - delta2)
                db_ref[branch] += ds.reshape(2, ATT_BLOCK, 2 * ATT_BLOCK)
                dsb = (ds * (HEAD_DIM ** -0.5)).astype(BF16)
                dq = _unstack_heads(jnp.dot(dsb, k, preferred_element_type=F32), low)
                dk = lax.dot_general(dsb, q2, _TN, preferred_element_type=F32)
                dv = lax.dot_general(p.astype(BF16), do2, _TN, preferred_element_type=F32)
                _store_block(dqp, cur, run_len, dq, add=True)
                _store_block(dkp, prev, run_len, dk[:ATT_BLOCK], add=True)
                _store_block(dvp, prev, run_len, dv[:ATT_BLOCK], add=True)
                _store_block(dkp, cur, run_len, dk[ATT_BLOCK:], add=True)
                _store_block(dvp, cur, run_len, dv[ATT_BLOCK:], add=True)
                return carry

            lax.fori_loop(0, n_iter, step, 0, unroll=ATTN_BWD_UNROLL)

        _ungroup(dq_ref, dqp, L16)
        _ungroup(dk_ref, dkp, L16)
        _ungroup(dv_ref, dvp, L16)

    col = lambda part: pl.BlockSpec((S, PAIR), lambda hp: (0, part * N_PAIRS + hp))
    one = pl.BlockSpec((S, PAIR), lambda hp: (0, hp))
    return pl.pallas_call(
        body, name=name, grid=(N_PAIRS,),
        in_specs=[col(0), col(1), col(2), one, one, one,
                  pl.BlockSpec((6, 2, ATT_BLOCK, 2 * ATT_BLOCK), lambda hp: (0, hp, 0, 0))],
        out_specs=[one, one, one, pl.BlockSpec((3, 2, ATT_BLOCK, 2 * ATT_BLOCK), lambda hp: (0, hp, 0, 0))],
        out_shape=[_sds((S, D_MODEL), F32)] * 3 + [_sds((3, N_HEADS, ATT_BLOCK, 2 * ATT_BLOCK), F32)],
        scratch_shapes=[pltpu.VMEM((S, PAIR), F32)] * 8,
        compiler_params=pltpu.CompilerParams(dimension_semantics=("parallel",), vmem_limit_bytes=ATTN_VMEM_LIMIT_BYTES),
    )(qkvn, qkvn, qkvn, att, datt, lse, bias)


def _adamw_step(w_ref, g_ref, m_ref, v_ref, d_ref, nm_ref, nv_ref):
    gv = g_ref[...]
    m2 = ADAM_B1 * m_ref[...] + (1.0 - ADAM_B1) * gv
    v2 = ADAM_B2 * v_ref[...] + (1.0 - ADAM_B2) * (gv * gv)
    m_hat = m2 / (1.0 - ADAM_B1 ** ADAM_STEP)
    v_hat = v2 / (1.0 - ADAM_B2 ** ADAM_STEP)
    d_ref[...] = -ADAM_LR * (m_hat / (jnp.sqrt(v_hat) + ADAM_EPS) + ADAM_WD * w_ref[...])
    nm_ref[...] = m2
    nv_ref[...] = v2


def adamw_small(ws, gs, ms, vs, name):
    n = len(ws)

    def body(*refs):
        groups = [refs[k * n:(k + 1) * n] for k in range(7)]
        for refs_of_one in zip(*groups):
            _adamw_step(*refs_of_one)

    outs = pl.pallas_call(body, name=name, out_shape=[_sds(a.shape, F32) for a in ws] * 3,
                          compiler_params=_params())(*ws, *gs, *ms, *vs)
    return outs[:n], outs[n:2 * n], outs[2 * n:]


def adamw(w, g, m, v, name):
    n, R, C = w.shape

    def body(w_ref, g_ref, m_ref, v_ref, d_ref, nm_ref, nv_ref, go_ref):
        go_ref[...] = g_ref[...]
        _adamw_step(w_ref, g_ref, m_ref, v_ref, d_ref, nm_ref, nv_ref)

    tr = R
    while tr * C * 4 > ELEMENTWISE_BLOCK_BYTES and tr % 16 == 0:
        tr //= 2
    spec = pl.BlockSpec((None, tr, C), lambda i, r: (i, r, 0))
    return pl.pallas_call(
        body, name=name, grid=(n, R // tr), in_specs=[spec] * 4, out_specs=[spec] * 4,
        out_shape=[_sds((n, R, C), F32)] * 4, compiler_params=_params("parallel", "parallel"),
    )(w, g, m, v)


ANY = pl.BlockSpec(memory_space=pl.ANY)


def _coords():
    return lax.axis_index("x"), lax.axis_index("y"), lax.axis_index("c")


def _other_chips(mx, my):
    return [(1 - mx, my), (mx, 1 - my), (1 - mx, 1 - my)]


def _remote(src, dst, send, recv, dev):
    return pltpu.make_async_remote_copy(src_ref=src, dst_ref=dst, send_sem=send, recv_sem=recv, device_id=dev,
                                        device_id_type=MESH)


HBM =pl.BlockSpec(memory_space=pltpu.HBM)
SEM = pl.BlockSpec(memory_space=pltpu.SEMAPHORE)
_SPLIT_COPY = pltpu.CompilerParams(has_side_effects=pltpu.SideEffectType.DATAFLOW_SIDE_EFFECTING)


def _in_hbm(a):
    return pltpu.with_memory_space_constraint(a, pltpu.HBM)


def cast_into_slot(w, layer, chip_core, name, dtype=BF16):
    _, _, hR, C = w.shape

    def body(s_ref, w_ref, o_ref):
        del s_ref
        o_ref[...] = w_ref[...].astype(dtype)

    grid_spec = pltpu.PrefetchScalarGridSpec(
        num_scalar_prefetch=1, grid=(2,),
        in_specs=[pl.BlockSpec((None, None, hR, C), lambda h, s: (layer, h, 0, 0))],
        out_specs=pl.BlockSpec((None, None, hR, C), lambda h, s: (s[0], h, 0, 0)))
    return pl.pallas_call(body, name=name, grid_spec=grid_spec, out_shape=_sds_hbm((N_CHIPS, 2, hR, C), dtype),
                          compiler_params=_params("parallel"))(chip_core, w)


def gather_start(lands, groups, name):
    n = len(lands)
    n_groups = len(groups)

    def body(*refs):
        ins = refs[:n]
        sems = refs[n:n + 2 * n_groups]
        token = refs[-1]
        mx, my, mc = _coords()
        chip = 2 * mx + my
        for g, members in enumerate(groups):
            send, recv = sems[2 * g], sems[2 * g + 1]
            for i, a in enumerate(members):
                mine = ins[a].at[chip, mc]
                for k, (px, py) in enumerate(_other_chips(mx, my)):
                    _remote(mine, mine, send.at[3 * i + k], recv.at[3 * i + k], (px, py, mc)).start()
        token[...] = jnp.zeros_like(token)

    sem_shapes = []
    for members in groups:
        sem_shapes += [pltpu.SemaphoreType.DMA((3 * len(members),))] * 2
    outs = pl.pallas_call(
        body, name=name, in_specs=[HBM] * n,
        out_specs=[SEM] * (2 * n_groups) + [HBM] * n + [pl.BlockSpec(memory_space=pltpu.VMEM)],
        out_shape=sem_shapes + [pltpu.HBM(a.shape, a.dtype) for a in lands] + [_sds((SUBLANES, LANES), F32)],
        input_output_aliases={a: 2 * n_groups + a for a in range(n)}, compiler_params=_SPLIT_COPY,
    )(*[_in_hbm(a) for a in lands])
    sems = [(outs[2 * g], outs[2 * g + 1]) for g in range(n_groups)]
    return sems, list(outs[2 * n_groups:2 * n_groups + n]), outs[-1]


def gather_forward(lands, sems, after, name):
    n = len(lands)

    def body(*refs):
        ins = refs[:n]
        send, recv = refs[n], refs[n + 1]
        fsend, frecv = refs[n + 3], refs[n + 4]
        mx, my, mc = _coords()
        for i in range(n):
            for k, (px, py) in enumerate(_other_chips(mx, my)):
                landed = ins[i].at[2 * px + py, mc]
                cp = _remote(landed, landed, send.at[3 * i + k], recv.at[3 * i + k], (px, py, mc))
                cp.wait_send()
                cp.wait_recv()
                _remote(landed, landed, fsend.at[3 * i + k], frecv.at[3 * i + k], (mx, my, 1 - mc)).start()

    outs = pl.pallas_call(
        body, name=name, in_specs=[HBM] * n + [SEM, SEM, ANY], out_specs=[SEM, SEM] + [HBM] * n,
        out_shape=[pltpu.SemaphoreType.DMA((3 * n,))] * 2 + [pltpu.HBM(a.shape, a.dtype) for a in lands],
        input_output_aliases={a: 2 + a for a in range(n)}, compiler_params=_SPLIT_COPY,
    )(*lands, sems[0], sems[1], after)
    return (outs[0], outs[1]), list(outs[2:])


def gather_wait(lands, sems, after, name):
    n = len(lands)

    def body(*refs):
        ins = refs[:n]
        fsend, frecv = refs[n], refs[n + 1]
        mx, my, mc = _coords()
        for i in range(n):
            for k, (px, py) in enumerate(_other_chips(mx, my)):
                theirs = ins[i].at[2 * px + py, 1 - mc]
                cp = _remote(theirs, theirs, fsend.at[3 * i + k], frecv.at[3 * i + k], (mx, my, 1 - mc))
                cp.wait_send()
                cp.wait_recv()

    outs = pl.pallas_call(
        body, name=name, in_specs=[HBM] * n + [SEM, SEM, ANY], out_specs=[HBM] * n,
        out_shape=[pltpu.HBM(a.shape, a.dtype) for a in lands],
        input_output_aliases={a: a for a in range(n)}, compiler_params=_SPLIT_COPY,
    )(*lands, sems[0], sems[1], after)
    return list(outs)


def _peers(mx, my, mc):
    return [(1 - mx if k & 4 else mx, 1 - my if k & 2 else my, 1 - mc if k & 1 else mc) for k in range(1, N_DEV)]


def devices_start(x, name):
    def body(x_ref, land_ref, send, recv, x_thru, land_thru):
        mx, my, mc = _coords()
        me = 4 * mx + 2 * my + mc
        for k, peer in enumerate(_peers(mx, my, mc)):
            _remote(x_ref, land_ref.at[me], send.at[k], recv.at[k], peer).start()

    land = lax.empty((N_DEV,) + x.shape, x.dtype)
    outs = pl.pallas_call(
        body, name=name, in_specs=[HBM, HBM], out_specs=[SEM, SEM, HBM, HBM],
        out_shape=[pltpu.SemaphoreType.DMA((N_DEV - 1,))] * 2 + [pltpu.HBM(x.shape, x.dtype), pltpu.HBM(land.shape, x.dtype)],
        input_output_aliases={0: 2, 1: 3}, compiler_params=_SPLIT_COPY,
    )(_in_hbm(x), _in_hbm(land))
    return (outs[0], outs[1]), outs[2], outs[3]


def devices_wait(x, land, sems, after, name):
    def body(x_ref, land_ref, send, recv, after_ref, x_thru, land_thru):
        mx, my, mc = _coords()
        for k, (px, py, pc) in enumerate(_peers(mx, my, mc)):
            cp = _remote(x_ref, land_ref.at[4 * px + 2 * py + pc], send.at[k], recv.at[k], (px, py, pc))
            cp.wait_send()
            cp.wait_recv()

    outs = pl.pallas_call(
        body, name=name, in_specs=[HBM, HBM, SEM, SEM, ANY], out_specs=[HBM, HBM],
        out_shape=[pltpu.HBM(x.shape, x.dtype), pltpu.HBM(land.shape, land.dtype)],
        input_output_aliases={0: 0, 1: 1}, compiler_params=_SPLIT_COPY,
    )(x, land, sems[0], sems[1], after)
    return outs[0], outs[1]


def device_sum(land, own, me, name):
    _, R, C = land.shape

    def body(s_ref, l_ref, o_ref_in, o_ref):
        acc = None
        for q in range(N_DEV):
            term = jnp.where(s_ref[0] == q, o_ref_in[...], l_ref[q])
            acc = term if acc is None else acc + term
        o_ref[...] = acc

    grid_spec = pltpu.PrefetchScalarGridSpec(
        num_scalar_prefetch=1, grid=(1,),
        in_specs=[pl.BlockSpec((N_DEV, R, C), lambda i, s: (0, 0, 0)), pl.BlockSpec((R, C), lambda i, s: (0, 0))],
        out_specs=pl.BlockSpec((R, C), lambda i, s: (0, 0)))
    return pl.pallas_call(body, name=name, grid_spec=grid_spec, out_shape=_sds((R, C), F32),
                          compiler_params=_params("arbitrary"))(me, land, own)


def reduce_send(grads, name):
    n = len(grads)

    def body(*refs):
        ins, lands = refs[:n], refs[n:2 * n]
        send, recv = refs[2 * n], refs[2 * n + 1]
        mx, my, mc = _coords()
        me = 4 * mx + 2 * my + mc
        for a in range(n):
            for k, (px, py, pc) in enumerate(_peers(mx, my, mc)):
                _remote(ins[a].at[2 * px + py, pc], lands[a].at[me], send.at[7 * a + k], recv.at[7 * a + k], (px, py, pc)).start()

    lands = [lax.empty((N_DEV,) + g.shape[2:], g.dtype) for g in grads]
    outs = pl.pallas_call(
        body, name=name, in_specs=[HBM] * (2 * n), out_specs=[SEM, SEM] + [HBM] * (2 * n),
        out_shape=[pltpu.SemaphoreType.DMA((7 * n,))] * 2 + [pltpu.HBM(a.shape, a.dtype) for a in grads + lands],
        input_output_aliases={a: 2 + a for a in range(2 * n)}, compiler_params=_SPLIT_COPY,
    )(*[_in_hbm(a) for a in grads + lands])
    return (outs[0], outs[1]), list(outs[2:2 + n]), list(outs[2 + n:])


def reduce_wait(grads, lands, sems, after, name):
    n = len(grads)

    def body(*refs):
        ins, zones = refs[:n], refs[n:2 * n]
        send, recv = refs[2 * n], refs[2 * n + 1]
        mx, my, mc = _coords()
        for a in range(n):
            for k, (px, py, pc) in enumerate(_peers(mx, my, mc)):
                cp = _remote(ins[a].at[2 * px + py, pc], zones[a].at[4 * px + 2 * py + pc], send.at[7 * a + k],
                             recv.at[7 * a + k], (px, py, pc))
                cp.wait_send()
                cp.wait_recv()

    outs = pl.pallas_call(
        body, name=name, in_specs=[HBM] * (2 * n) + [SEM, SEM, ANY], out_specs=[HBM] * (2 * n),
        out_shape=[pltpu.HBM(a.shape, a.dtype) for a in grads + lands],
        input_output_aliases={a: a for a in range(2 * n)}, compiler_params=_SPLIT_COPY,
    )(*grads, *lands, sems[0], sems[1], after)
    return list(outs[:n]), list(outs[n:])


def reduce_sum(land, grad, place, name, into=None, layer=None):
    _, hR, C = land.shape
    tr = hR
    while N_DEV * tr * C * 2 > 3 * ELEMENTWISE_BLOCK_BYTES and tr % 32 == 0:
        tr //= 2

    def body(s_ref, l_ref, g_ref, *rest):
        o_ref = rest[-1]
        own = g_ref[...].astype(F32)
        acc = None
        for q in range(N_DEV):
            term = jnp.where(s_ref[2] == q, own, l_ref[q].astype(F32))
            acc = term if acc is None else acc + term
        o_ref[...] = acc

    in_specs = [pl.BlockSpec((N_DEV, tr, C), lambda i, s: (0, i, 0)),
                pl.BlockSpec((None, None, tr, C), lambda i, s: (s[0], s[1], i, 0))]
    args = [place, _in_hbm(land), _in_hbm(grad)]
    aliases = {}
    if layer is None:
        out_spec = pl.BlockSpec((None, tr, C), lambda i, s: (s[1], i, 0))
        out_shape = _sds_hbm((2, hR, C), F32)
    else:
        out_spec = pl.BlockSpec((None, None, tr, C), lambda i, s: (layer, s[1], i, 0))
        out_shape = _sds_hbm((2, 2, hR, C), F32)
        if into is not None:
            in_specs.append(ANY)
            args.append(into)
            aliases = {3: 0}
    grid_spec = pltpu.PrefetchScalarGridSpec(num_scalar_prefetch=1, grid=(hR // tr,), in_specs=in_specs, out_specs=out_spec)
    return pl.pallas_call(body, name=name, grid_spec=grid_spec, out_shape=out_shape, input_output_aliases=aliases,
                          compiler_params=_params("arbitrary"))(*args)


def _halves_of(arrays):
    return [(a, l) for a, arr in enumerate(arrays) for l in (range(arr.shape[0]) if arr.ndim == 4 else [None])]


def _half(refs, a, l, h):
    return refs[a].at[h] if l is None else refs[a].at[l, h]


def join_start(arrays, name):
    n = len(arrays)
    pieces = _halves_of(arrays)

    def body(*refs):
        ins = refs[:n]
        send, recv = refs[n], refs[n + 1]
        mx, my, mc = _coords()
        for i, (a, l) in enumerate(pieces):
            mine = _half(ins, a, l, mc)
            _remote(mine, mine, send.at[i], recv.at[i], (mx, my, 1 - mc)).start()

    outs = pl.pallas_call(
        body, name=name, in_specs=[HBM] * n, out_specs=[SEM, SEM] + [HBM] * n,
        out_shape=[pltpu.SemaphoreType.DMA((len(pieces),))] * 2 + [pltpu.HBM(a.shape, a.dtype) for a in arrays],
        input_output_aliases={a: 2 + a for a in range(n)}, compiler_params=_SPLIT_COPY,
    )(*[_in_hbm(a) for a in arrays])
    return (outs[0], outs[1]), list(outs[2:])


def join_wait(arrays, sems, after, name):
    n = len(arrays)
    pieces = _halves_of(arrays)

    def body(*refs):
        ins = refs[:n]
        send, recv = refs[n], refs[n + 1]
        mx, my, mc = _coords()
        for i, (a, l) in enumerate(pieces):
            theirs = _half(ins, a, l, 1 - mc)
            cp = _remote(theirs, theirs, send.at[i], recv.at[i], (mx, my, 1 - mc))
            cp.wait_send()
            cp.wait_recv()

    outs = pl.pallas_call(
        body, name=name, in_specs=[HBM] * n + [SEM, SEM, ANY], out_specs=[HBM] * n,
        out_shape=[pltpu.HBM(a.shape, a.dtype) for a in arrays],
        input_output_aliases={a: a for a in range(n)}, compiler_params=_SPLIT_COPY,
    )(*arrays, sems[0], sems[1], after)
    return list(outs)


LANES = 128
SUBLANES = 8


def _n_rows(shape):
    rows = -(-int(np.prod(shape)) // LANES)
    return -(-rows // SUBLANES) * SUBLANES


def _as_rows(a):
    flat = a.reshape(-1)
    rows = _n_rows(a.shape)
    return jnp.pad(flat, (0, rows * LANES - flat.shape[0])).reshape(rows, LANES)


def _pack(arrays):
    return jnp.concatenate([_as_rows(a) for a in arrays], axis=0)


def _unpack(rows, shapes):
    out, r0 = [], 0
    for s in shapes:
        n = _n_rows(s)
        out.append(rows[r0:r0 + n].reshape(-1)[:int(np.prod(s))].reshape(s))
        r0 += n
    return out


REPLICATED_SMALL = [("rel_bias", (32, 16)), ("even_norm", (1, 1024)), ("even_pool_w", (1, 4, 128, 128)),
                    ("even_pool_scale", (1, 512)), ("odd_q_norm", (1, 64)), ("odd_k_norm", (1, 64)),
                    ("ffn_norm", (2, 1024)), ("ffn_conv_b", (2, 5632))]
SHARDED_SMALL = [("even_conv_w", (1, 3, 128)), ("odd_norm", (1, 256)), ("ffn_conv_w", (2, 3, 1408))]
BIG = ["even_w_in", "even_w_out", "odd_w_qkv", "odd_w_o", "ffn_w_up", "ffn_w_down"]
WEIGHT_ORDER = ["rel_bias", "even_norm", "even_w_in", "even_conv_w", "even_pool_w", "even_pool_scale", "even_w_out",
                "odd_norm", "odd_w_qkv", "odd_q_norm", "odd_k_norm", "odd_w_o", "ffn_norm", "ffn_w_up", "ffn_conv_w",
                "ffn_conv_b", "ffn_w_down"]


def kernel(x, rel_bias, even_norm, even_w_in, even_conv_w, even_pool_w, even_pool_scale, even_w_out, odd_norm, odd_w_qkv, odd_q_norm, odd_k_norm, odd_w_o, ffn_norm, ffn_w_up, ffn_conv_w, ffn_conv_b, ffn_w_down, loss_target, m_rel_bias, m_even_norm, m_even_w_in, m_even_conv_w, m_even_pool_w, m_even_pool_scale, m_even_w_out, m_odd_norm, m_odd_w_qkv, m_odd_q_norm, m_odd_k_norm, m_odd_w_o, m_ffn_norm, m_ffn_w_up, m_ffn_conv_w, m_ffn_conv_b, m_ffn_w_down, v_rel_bias, v_even_norm, v_even_w_in, v_even_conv_w, v_even_pool_w, v_even_pool_scale, v_even_w_out, v_odd_norm, v_odd_w_qkv, v_odd_q_norm, v_odd_k_norm, v_odd_w_o, v_ffn_norm, v_ffn_w_up, v_ffn_conv_w, v_ffn_conv_b, v_ffn_w_down):
    W = dict(rel_bias=rel_bias, even_norm=even_norm, even_w_in=even_w_in, even_conv_w=even_conv_w, even_pool_w=even_pool_w,
             even_pool_scale=even_pool_scale, even_w_out=even_w_out, odd_norm=odd_norm, odd_w_qkv=odd_w_qkv,
             odd_q_norm=odd_q_norm, odd_k_norm=odd_k_norm, odd_w_o=odd_w_o, ffn_norm=ffn_norm, ffn_w_up=ffn_w_up,
             ffn_conv_w=ffn_conv_w, ffn_conv_b=ffn_conv_b, ffn_w_down=ffn_w_down)
    M1 = dict(rel_bias=m_rel_bias, even_norm=m_even_norm, even_w_in=m_even_w_in, even_conv_w=m_even_conv_w,
              even_pool_w=m_even_pool_w, even_pool_scale=m_even_pool_scale, even_w_out=m_even_w_out, odd_norm=m_odd_norm,
              odd_w_qkv=m_odd_w_qkv, odd_q_norm=m_odd_q_norm, odd_k_norm=m_odd_k_norm, odd_w_o=m_odd_w_o,
              ffn_norm=m_ffn_norm, ffn_w_up=m_ffn_w_up, ffn_conv_w=m_ffn_conv_w, ffn_conv_b=m_ffn_conv_b,
              ffn_w_down=m_ffn_w_down)
    M2 = dict(rel_bias=v_rel_bias, even_norm=v_even_norm, even_w_in=v_even_w_in, even_conv_w=v_even_conv_w,
              even_pool_w=v_even_pool_w, even_pool_scale=v_even_pool_scale, even_w_out=v_even_w_out, odd_norm=v_odd_norm,
              odd_w_qkv=v_odd_w_qkv, odd_q_norm=v_odd_q_norm, odd_k_norm=v_odd_k_norm, odd_w_o=v_odd_w_o,
              ffn_norm=v_ffn_norm, ffn_w_up=v_ffn_w_up, ffn_conv_w=v_ffn_conv_w, ffn_conv_b=v_ffn_conv_b,
              ffn_w_down=v_ffn_w_down)
    mx, my, mc = _coords()
    chip = 2 * mx + my
    me = 4 * mx + 2 * my + mc
    place = jnp.stack([chip, mc, me]).astype(jnp.int32)
    xs, target = x[0], loss_target[0]

    def halves(w):
        return w.reshape((w.shape[0], 2, w.shape[-2] // 2, w.shape[-1]))

    small_rows = jnp.pad(_pack([even_conv_w, odd_norm, ffn_conv_w]), ((0, SUBLANES), (0, 0)))
    first = [cast_into_slot(halves(even_w_in), 0, place, "cast_w_in"), cast_into_slot(halves(even_w_out), 0, place, "cast_w_out"),
             cast_into_slot(small_rows.reshape(1, 2, small_rows.shape[0] // 2, LANES), 0, place, "small_into_slot", dtype=F32)]
    first_sems, first, token = gather_start(first, [[0, 1, 2]], "gather_start_first")
    even_norm_after_start = even_norm + token[0:1, 0:1]

    def later(a):
        return lax.optimization_barrier((a, token))[0]

    up_f32, down_f32 = halves(later(ffn_w_up)), halves(later(ffn_w_down))
    rest = [cast_into_slot(up_f32, 0, place, "cast_w_up0"), cast_into_slot(down_f32, 0, place, "cast_w_down0"),
            cast_into_slot(halves(later(odd_w_qkv)), 0, place, "cast_w_qkv"), cast_into_slot(halves(later(odd_w_o)), 0, place, "cast_w_o"),
            cast_into_slot(up_f32, 1, place, "cast_w_up1"), cast_into_slot(down_f32, 1, place, "cast_w_down1")]
    rest_sems, rest, _ = gather_start(rest, [[0], [1], [2, 3], [4], [5]], "gather_start_rest")
    group_arrays = [first, [rest[0]], [rest[1]], [rest[2], rest[3]], [rest[4]], [rest[5]]]
    group_sems = first_sems + rest_sems

    def gathered(group, tag, after_landing, after_passing):
        sems, arrays = gather_forward(group_arrays[group], group_sems[group], after_landing, "gather_forward_" + tag)
        return gather_wait(arrays, sems, after_passing, "gather_wait_" + tag)

    pool_w = cast_bf16(even_pool_w[0], "cast_pool_w")
    gqk = jnp.stack([jnp.tile(odd_q_norm[0], N_HEADS), jnp.tile(odd_k_norm[0], N_HEADS),
                     jnp.ones((D_MODEL,), F32)])[:, None, :]
    bias = bias_expand(later(rel_bias).T, "bias_expand").reshape(6, N_HEADS, ATT_BLOCK, 2 * ATT_BLOCK)
    xn0 = rmsnorm_fwd(xs, even_norm_after_start, "even_norm")
    got = gathered(0, "even", bias, xn0)
    w_in = got[0].reshape(N_CHIPS, 1, D_MODEL, EVEN_IN // N_CHIPS)
    w_out = got[1].reshape(1, 1, D_MODEL, D_MODEL)
    small = got[2].reshape(N_CHIPS, small_rows.shape[0], LANES)
    conv_w_full = small[:, 0:3].transpose(1, 0, 2).reshape(3, A_WIDTH)
    odd_norm_full = small[:, 8:10].reshape(1, D_MODEL)
    ffn_cw_full = small[:, 16:82].reshape(N_CHIPS, 2, 3, 2 * D_FF // N_CHIPS).transpose(1, 2, 0, 3).reshape(2, 3, 2 * D_FF)

    def ffn_fwd(l, xin, xn):
        up, u, act = up_glu_fwd(xn, w_up[l], ffn_cw_full[l], ffn_conv_b[l:l + 1], f"ffn{l}_up_glu")
        return act, (xin, xn, up, u, act)

    w_up, w_down = [None, None], [None, None]
    proj, mix = in_mixer_fwd(xn0, w_in, conv_w_full, pool_w, even_pool_scale, "even_in_mixer")
    x1, xn1 = mm_res_norm(mix, w_out, xs, ffn_norm[0:1], "even_out")
    w_up[0] = gathered(1, "up0", proj, x1)[0].reshape(N_CHIPS, 1, D_MODEL, 2 * D_FF // N_CHIPS)
    act0, ffn0 = ffn_fwd(0, x1, xn1)
    w_down[0] = gathered(2, "down0", act0, act0)[0].reshape(1, 1, D_FF, D_MODEL)
    x2, xn2 = mm_res_norm(act0, w_down[0], x1, odd_norm_full, "ffn0_down")
    got = gathered(3, "odd", x1, x2)
    w_qkv = got[0].reshape(N_CHIPS, 1, D_MODEL, 3 * D_MODEL // N_CHIPS)
    w_o = got[1].reshape(1, 1, D_MODEL, D_MODEL)
    qkv, qkvn = qkv_qknorm_fwd(xn2, w_qkv, gqk, "odd_qkv_qknorm")
    att, lse = attn_fwd(qkvn, bias, "attn_fwd")
    x3, xn3 = mm_res_norm(att, w_o, x2, ffn_norm[1:2], "odd_out")
    w_up[1] = gathered(4, "up1", x2, x3)[0].reshape(N_CHIPS, 1, D_MODEL, 2 * D_FF // N_CHIPS)
    act1, ffn1 = ffn_fwd(1, x3, xn3)
    w_down[1] = gathered(5, "down1", act1, act1)[0].reshape(1, 1, D_FF, D_MODEL)
    dy, dyb, sq = mm_res_loss(act1, w_down[1], x3, target, "ffn1_down_loss")
    loss_part = (0.5 * jnp.sum(sq) * (1.0 / D_MODEL)).reshape(1, 1)

    def ffn_bwd(l, dy, dyb, saved):
        xin, xn, up, u, act = saved
        dw_down = mm_tn(act, dyb, f"ffn{l}_dw_down", J=1, tk=D_FF // 2, tm=1024)
        dact = mm_nt(dyb, w_down[l], f"ffn{l}_dact", tr=D_FF // 2, out_dtype=BF16, tm=1024)
        dup, dcw, dcb = glu_bwd(up, u, dact, ffn_cw_full[l], f"ffn{l}_glu_bwd")
        dw_up = mm_tn(xn, dup, f"ffn{l}_dw_up", J=N_CHIPS, tk=512, tm=1024, jb=2)
        dx, dxb, dg = mm_nt_norm_bwd(dup, w_up[l], xin, ffn_norm[l:l + 1], dy, f"ffn{l}_dx")
        return dx, dxb, (dw_down, dw_up, dcw, dcb, dg)

    def quarters(g):
        return g.reshape(N_CHIPS, 2, g.shape[0] * g.shape[1] // (2 * N_CHIPS), g.shape[-1])

    def reduce_start(grads, tag, then):
        sems, parts, zones = reduce_send([quarters(g) for g in grads], "reduce_send_" + tag)
        then, parts = lax.optimization_barrier((then, parts))
        return (sems, parts, zones), then

    dx3, dx3b, g_ffn1 = ffn_bwd(1, dy, dyb, ffn1)
    red_ffn1, (dx3, dx3b) = reduce_start([g_ffn1[1], g_ffn1[0]], "ffn1", (dx3, dx3b))
    dw_o = mm_tn(att, dx3b, "odd_dw_o", J=1, tk=512, tm=1024)
    datt = mm_nt(dx3b, w_o, "odd_datt", tr=D_MODEL, out_dtype=BF16)
    dq, dk, dv, dbias = attn_bwd(qkvn, att, datt, lse, bias, "attn_bwd")
    dqkv, dgqk = qknorm_bwd(qkv, dq, dk, dv, gqk, "odd_qknorm_bwd")
    dw_qkv = mm_tn(xn2, dqkv, "odd_dw_qkv", J=N_CHIPS, tk=512, tm=1024)
    red_odd, dqkv = reduce_start([dw_qkv, dw_o], "odd", dqkv)
    dx2, dx2b, dg_odd = mm_nt_norm_bwd(dqkv, w_qkv, x2, odd_norm_full, dx3, "odd_dx")
    dx1, dx1b, g_ffn0 = ffn_bwd(0, dx2, dx2b, ffn0)
    red_ffn0, (dx1, dx1b) = reduce_start([g_ffn0[1], g_ffn0[0]], "ffn0", (dx1, dx1b))
    dw_out = mm_tn(mix, dx1b, "even_dw_out", J=1, tk=512, tm=1024)
    dmix = mm_nt(dx1b, w_out, "even_dmix", tr=D_MODEL)
    dproj, dcw_even, dpw, dps = mixer_bwd(proj, dmix, conv_w_full, pool_w, even_pool_scale, "even_mixer_bwd")
    dw_in = mm_tn(xn0, dproj, "even_dw_in", J=N_CHIPS, tk=512, tm=1024)
    grad_x, _, dg_even = mm_nt_norm_bwd(dproj, w_in, xs, even_norm, dx1, "even_dx")
    d_rel = jnp.sum(bias_reduce(dbias.reshape(3, N_HEADS, 2 * ATT_BLOCK * ATT_BLOCK), "bias_reduce"), axis=0).T

    red_even, grad_x = reduce_start([dw_in, dw_out], "even", grad_x)

    dcw_sh = dcw_even.reshape(3, N_CHIPS, A_WIDTH // N_CHIPS).transpose(1, 0, 2)
    don_sh = dg_odd.reshape(N_CHIPS, D_MODEL // N_CHIPS)
    dfcw = jnp.stack([g_ffn0[2], g_ffn1[2]])
    dfcw_sh = dfcw.reshape(2, 3, N_CHIPS, 2 * D_FF // N_CHIPS).transpose(2, 0, 1, 3)
    rep_grads = [d_rel, dg_even, dpw[None], dps, _head_sum(dgqk[0]), _head_sum(dgqk[1]),
                 jnp.concatenate([g_ffn0[4], g_ffn1[4]], axis=0), jnp.concatenate([g_ffn0[3], g_ffn1[3]], axis=0)]
    rep_rows = _pack([loss_part] + rep_grads)
    n_loss = _n_rows(loss_part.shape)
    shard_rows = jnp.concatenate([_pack([dcw_sh[j], don_sh[j], dfcw_sh[j]]) for j in range(N_CHIPS)], axis=0)
    n_rep, n_shard = rep_rows.shape[0], shard_rows.shape[0] // N_CHIPS
    small_sems, small_rows, small_land = devices_start(jnp.concatenate([rep_rows, shard_rows], axis=0), "small_grads_start")
    grad_x, small_rows = lax.optimization_barrier((grad_x, small_rows))

    def reduce_end(red, tag, after):
        sems, parts, zones = red
        parts, zones = reduce_wait(parts, zones, sems, after, "reduce_wait_" + tag)
        return zones, parts

    z_ffn1, p_ffn1 = reduce_end(red_ffn1, "ffn1", grad_x)
    z_odd, p_odd = reduce_end(red_odd, "odd", grad_x)
    r_qkv = reduce_sum(z_odd[0], p_odd[0], place, "reduce_sum_w_qkv")
    r_o = reduce_sum(z_odd[1], p_odd[1], place, "reduce_sum_w_o")
    r_up = reduce_sum(z_ffn1[0], p_ffn1[0], place, "reduce_sum_w_up1", layer=1)
    r_down = reduce_sum(z_ffn1[1], p_ffn1[1], place, "reduce_sum_w_down1", layer=1)
    r_qkv, r_o, r_up, r_down = lax.optimization_barrier((r_qkv, r_o, r_up, r_down))
    join_odd, (r_qkv, r_o) = join_start([r_qkv, r_o], "grads_join_start_odd")
    z_ffn0, p_ffn0 = reduce_end(red_ffn0, "ffn0", r_qkv)
    r_up = reduce_sum(z_ffn0[0], p_ffn0[0], place, "reduce_sum_w_up0", into=r_up, layer=0)
    r_down = reduce_sum(z_ffn0[1], p_ffn0[1], place, "reduce_sum_w_down0", into=r_down, layer=0)
    join_ffn, (r_up, r_down) = join_start([r_up, r_down], "grads_join_start_ffn")

    G, D_, NM, NV = {}, {}, {}, {}

    def update(nm, g):
        as3 = lambda a: a.reshape((-1,) + a.shape[-2:])
        outs = adamw(as3(W[nm]), as3(g.reshape(W[nm].shape)), as3(M1[nm]), as3(M2[nm]), "adamw_" + nm)
        D_[nm], NM[nm], NV[nm], G[nm] = [o.reshape(W[nm].shape) for o in outs]

    def all_before(names):
        tied = lax.optimization_barrier([D_[nm] for nm in names])
        for nm, d in zip(names, tied):
            D_[nm] = d
        return tied[0]

    odd, ffn, first = ["odd_w_qkv", "odd_w_o"], ["ffn_w_up", "ffn_w_down"], ["even_w_in", "even_w_out"]
    for nm, g in zip(odd, join_wait([r_qkv, r_o], join_odd, r_down, "grads_join_wait_odd")):
        update(nm, g)
    for nm, g in zip(ffn, join_wait([r_up, r_down], join_ffn, all_before(odd), "grads_join_wait_ffn")):
        update(nm, g)
    z_even, p_even = reduce_end(red_even, "even", all_before(odd + ffn))
    join_even, r_first = join_start([reduce_sum(z_even[0], p_even[0], place, "reduce_sum_w_in"),
                                     reduce_sum(z_even[1], p_even[1], place, "reduce_sum_w_out")], "grads_join_start_even")
    small_rows, small_land = devices_wait(small_rows, small_land, small_sems, r_first[0], "small_grads_wait")
    small_sum = device_sum(small_land, small_rows, place[2:3], "small_grads_sum")
    mine = lax.dynamic_slice_in_dim(small_sum, n_rep + chip * n_shard, n_shard, axis=0)
    loss = small_sum[0, 0]
    g_small = jnp.concatenate([small_sum[n_loss:n_rep], mine], axis=0)
    small_names = [n for n, _ in REPLICATED_SMALL + SHARDED_SMALL]
    small_shapes = [s for _, s in REPLICATED_SMALL + SHARDED_SMALL]
    G.update(dict(zip(small_names, _unpack(g_small, small_shapes))))
    outs = adamw_small(*[[d[n] for n in small_names] for d in (W, G, M1, M2)], "adamw_small")
    for dst, o in zip((D_, NM, NV), outs):
        dst.update(dict(zip(small_names, o)))
    for nm, g in zip(first, join_wait(r_first, join_even, all_before(small_names), "grads_join_wait_even")):
        update(nm, g)

    return (loss, grad_x[None], *[G[n] for n in WEIGHT_ORDER], *[D_[n] for n in WEIGHT_ORDER],
            *[NM[n] for n in WEIGHT_ORDER], *[NV[n] for n in WEIGHT_ORDER])


def _head_sum(dg):
    return jnp.sum(dg.reshape(N_HEADS, HEAD_DIM), axis=0, keepdims=True)
```

```python
import functools
import math

import numpy as np
import jax
import jax.numpy as jnp
from jax import lax
from jax.experimental import pallas as pl
from jax.experimental.pallas import tpu as pltpu

F32 = jnp.float32
BF16 = jnp.bfloat16

D_MODEL = 1024
N_HEADS = 16
HEAD_DIM = 64
A_WIDTH = 512
POOL_WINDOWS = (2, 4, 8, 16)
POOL_GROUP = 128
EVEN_IN = 2048
D_FF = 2816
DILATED_PAIRS = ((128, 1), (512, 4), (2048, 16))
ATT_BLOCK = 128
N_REL_BUCKETS = 32
REL_MAX_DISTANCE = 2048
EPS = 1e-6
MASK_VALUE = -1e30
ADAM_LR, ADAM_B1, ADAM_B2, ADAM_EPS, ADAM_WD, ADAM_STEP = 0.001, 0.9, 0.999, 1e-08, 0.01, 10

VMEM_LIMIT_BYTES = 48 * 1024 * 1024
ELEMENTWISE_BLOCK_BYTES = 2 * 1024 * 1024
N_CHIPS = 4
N_DEV = 8
MESH = pl.DeviceIdType.MESH


def _params(*sem):
    return pltpu.CompilerParams(dimension_semantics=sem if sem else None, vmem_limit_bytes=VMEM_LIMIT_BYTES)


def _sds(shape, dtype):
    return jax.ShapeDtypeStruct(tuple(shape), dtype)


def _sds_hbm(shape, dtype):
    return pltpu.HBM(tuple(shape), dtype)


def cast_bf16(x, name, tr=None):
    lead, (R, C) = x.shape[:-2], x.shape[-2:]
    n = int(np.prod(lead)) if lead else 1
    x3 = x.reshape((n, R, C))
    tr = tr or R

    def body(x_ref, o_ref):
        o_ref[...] = x_ref[...].astype(BF16)

    out = pl.pallas_call(
        body, name=name, grid=(n, R // tr),
        in_specs=[pl.BlockSpec((None, tr, C), lambda i, r: (i, r, 0))],
        out_specs=pl.BlockSpec((None, tr, C), lambda i, r: (i, r, 0)),
        out_shape=_sds((n, R, C), BF16), compiler_params=_params("parallel", "parallel"),
    )(x3)
    return out.reshape(lead + (R, C))


def rmsnorm_fwd(x, g, name, ts=512):
    S, Dm = x.shape

    def body(x_ref, g_ref, o_ref):
        xv = x_ref[...]
        r = lax.rsqrt(jnp.mean(xv * xv, axis=-1, keepdims=True) + EPS)
        o_ref[...] = ((xv * r) * g_ref[...]).astype(BF16)

    return pl.pallas_call(
        body, name=name, grid=(S // ts,),
        in_specs=[pl.BlockSpec((ts, Dm), lambda i: (i, 0)), pl.BlockSpec((1, Dm), lambda i: (0, 0))],
        out_specs=pl.BlockSpec((ts, Dm), lambda i: (i, 0)),
        out_shape=_sds((S, Dm), BF16), compiler_params=_params("parallel"),
    )(x, g)


def mm_res_norm(a, w, res, gain, name, tm=1024):
    M, K = a.shape
    Dm = w.shape[-1]

    def body(a_ref, w_ref, r_ref, g_ref, y_ref, yn_ref):
        y = r_ref[...] + jnp.dot(a_ref[...], w_ref[...], preferred_element_type=F32)
        y_ref[...] = y
        r = lax.rsqrt(jnp.mean(y * y, axis=-1, keepdims=True) + EPS)
        yn_ref[...] = ((y * r) * g_ref[...]).astype(BF16)

    row = pl.BlockSpec((tm, Dm), lambda m: (m, 0))
    return pl.pallas_call(
        body, name=name, grid=(M // tm,),
        in_specs=[pl.BlockSpec((tm, K), lambda m: (m, 0)),
                  pl.BlockSpec((None, None, K, Dm), lambda m: (0, 0, 0, 0), pipeline_mode=pl.Buffered(1)),
                  row, pl.BlockSpec((1, Dm), lambda m: (0, 0))],
        out_specs=[row, row], out_shape=[_sds((M, Dm), F32), _sds((M, Dm), BF16)],
        compiler_params=_params("parallel"),
    )(a, w, res, gain)


def mm_res_loss(a, w, res, target, name, tm=512):
    M, K = a.shape
    Dm = w.shape[-1]

    def body(a_ref, w_ref, r_ref, t_ref, d_ref, db_ref, s_ref):
        e = (r_ref[...] + jnp.dot(a_ref[...], w_ref[...], preferred_element_type=F32)) - t_ref[...]
        d = e * (1.0 / Dm)
        d_ref[...] = d
        db_ref[...] = d.astype(BF16)
        part = jnp.sum(e * e, axis=0, keepdims=True)

        @pl.when(pl.program_id(0) == 0)
        def _():
            s_ref[...] = part

        @pl.when(pl.program_id(0) > 0)
        def _():
            s_ref[...] += part

    row = pl.BlockSpec((tm, Dm), lambda m: (m, 0))
    return pl.pallas_call(
        body, name=name, grid=(M // tm,),
        in_specs=[pl.BlockSpec((tm, K), lambda m: (m, 0)),
                  pl.BlockSpec((None, None, K, Dm), lambda m: (0, 0, 0, 0), pipeline_mode=pl.Buffered(1)), row, row],
        out_specs=[row, row, pl.BlockSpec((1, Dm), lambda m: (0, 0))],
        out_shape=[_sds((M, Dm), F32), _sds((M, Dm), BF16), _sds((1, Dm), F32)],
        compiler_params=_params("arbitrary"),
    )(a, w, res, target)


def mm_nt(dy, w, name, tr, layer=0, out_dtype=F32, tm=512):
    M = dy.shape[0]
    J, _, R, Ns = w.shape
    dims = (((1,), (1,)), ((), ()))

    def body(dy_ref, w_ref, o_ref):
        acc = None
        for j in range(J):
            p = lax.dot_general(dy_ref[:, j * Ns:(j + 1) * Ns], w_ref[j], dims, preferred_element_type=F32)
            acc = p if acc is None else acc + p
        o_ref[...] = acc.astype(o_ref.dtype)

    return pl.pallas_call(
        body, name=name, grid=(R // tr, M // tm),
        in_specs=[pl.BlockSpec((tm, J * Ns), lambda r, m: (m, 0)),
                  pl.BlockSpec((J, None, tr, Ns), lambda r, m: (0, layer, r, 0))],
        out_specs=pl.BlockSpec((tm, tr), lambda r, m: (m, r)),
        out_shape=_sds((M, R), out_dtype),
        compiler_params=_params("parallel", "parallel"),
    )(dy, w)


def mm_nt_norm_bwd(dy, w, x, g, dres, name, layer=0, tm=512):
    M = dy.shape[0]
    J, _, Dm, Ns = w.shape
    dims = (((1,), (1,)), ((), ()))

    def body(dy_ref, w_ref, x_ref, g_ref, r_ref, dx_ref, dxb_ref, dg_ref):
        dxn = None
        for j in range(J):
            p = lax.dot_general(dy_ref[:, j * Ns:(j + 1) * Ns], w_ref[j], dims, preferred_element_type=F32)
            dxn = p if dxn is None else dxn + p
        xv = x_ref[...]
        r = lax.rsqrt(jnp.mean(xv * xv, axis=-1, keepdims=True) + EPS)
        gx = dxn * g_ref[...]
        dot = jnp.sum(gx * xv, axis=-1, keepdims=True)
        dx = r_ref[...] + r * gx - xv * ((r * r * r) * (dot * (1.0 / Dm)))
        dx_ref[...] = dx
        dxb_ref[...] = dx.astype(BF16)
        part = jnp.sum(dxn * (xv * r), axis=0, keepdims=True)

        @pl.when(pl.program_id(0) == 0)
        def _():
            dg_ref[...] = part

        @pl.when(pl.program_id(0) > 0)
        def _():
            dg_ref[...] += part

    row = pl.BlockSpec((tm, Dm), lambda m: (m, 0))
    vec = pl.BlockSpec((1, Dm), lambda m: (0, 0))
    return pl.pallas_call(
        body, name=name, grid=(M // tm,),
        in_specs=[pl.BlockSpec((tm, J * Ns), lambda m: (m, 0)),
                  pl.BlockSpec((J, None, Dm, Ns), lambda m: (0, layer, 0, 0), pipeline_mode=pl.Buffered(1)), row, vec, row],
        out_specs=[row, row, vec],
        out_shape=[_sds((M, Dm), F32), _sds((M, Dm), BF16), _sds((1, Dm), F32)],
        compiler_params=_params("arbitrary"),
    )(dy, w, x, g, dres)


def mm_tn(a, dy, name, J, tk, tm=512, jb=None):
    M, K = a.shape
    jb = jb or J
    Ns = dy.shape[1] // J
    N = jb * Ns
    n_m = M // tm
    dims = (((0,), (0,)), ((), ()))

    def body(a_ref, dy_ref, o_ref, acc_ref):
        p = lax.dot_general(a_ref[...], dy_ref[...], dims, preferred_element_type=F32)
        m = pl.program_id(2)

        @pl.when(m == 0)
        def _():
            acc_ref[...] = p

        @pl.when(m > 0)
        def _():
            acc_ref[...] += p

        @pl.when(m == n_m - 1)
        def _():
            for j in range(jb):
                o_ref[j] = acc_ref[:, j * Ns:(j + 1) * Ns].astype(BF16)

    return pl.pallas_call(
        body, name=name, grid=(J // jb, K // tk, n_m),
        in_specs=[pl.BlockSpec((tm, tk), lambda g, k, m: (m, k)), pl.BlockSpec((tm, N), lambda g, k, m: (m, g))],
        out_specs=pl.BlockSpec((jb, tk, Ns), lambda g, k, m: (g, k, 0)),
        out_shape=_sds((J, K, Ns), BF16), scratch_shapes=[pltpu.VMEM((tk, N), F32)],
        compiler_params=_params("parallel", "parallel", "arbitrary"),
    )(a, dy)


HALO = 16


def _shift_down(x, s):
    return pltpu.roll(x, s, 0)


def _shift_up(x, s):
    return pltpu.roll(x, x.shape[0] - s, 0)


def _conv3(z, cw):
    return (_shift_down(z, 2) * cw[0:1] + _shift_down(z, 1) * cw[1:2]) + z * cw[2:3]


def _window_count(first_row, n, k):
    t = first_row + lax.broadcasted_iota(jnp.int32, (n, 1), 0)
    return jnp.clip(t + 1, 1, k).astype(F32)


def in_mixer_fwd(xn, w_in, conv_w, pool_w, pool_scale, name, ts=512):
    S, K = xn.shape
    n = ts + HALO

    def body(xm_ref, xb_ref, w_ref, cw_ref, pw_ref, ps_ref, p_ref, o_ref):
        i = pl.program_id(0)
        before = jnp.where(i > 0, xb_ref[...], jnp.zeros_like(xb_ref))
        rows = jnp.concatenate([before, xm_ref[...]], axis=0)
        h, gb, gc, pin = [jnp.dot(rows, w_ref[j], preferred_element_type=F32) for j in range(N_CHIPS)]
        for j, part in enumerate((h, gb, gc, pin)):
            p_ref[:, j * A_WIDTH:(j + 1) * A_WIDTH] = part[HALO:]
        cz = _conv3(gc * h, cw_ref[...])
        o_ref[:, 0:A_WIDTH] = (gb[HALO:] * cz[HALO:]).astype(BF16)
        for g, k in enumerate(POOL_WINDOWS):
            p = pin[:, g * POOL_GROUP:(g + 1) * POOL_GROUP]
            w = p
            s = 1
            while s < k:
                w = w + _shift_down(w, s)
                s *= 2
            pooled = w / _window_count(i * ts - HALO, n, k) - p
            yb = jnp.dot(pooled[HALO:].astype(BF16), pw_ref[g], preferred_element_type=F32)
            yb = yb * ps_ref[:, g * POOL_GROUP:(g + 1) * POOL_GROUP]
            o_ref[:, A_WIDTH + g * POOL_GROUP:A_WIDTH + (g + 1) * POOL_GROUP] = yb.astype(BF16)

    hb = ts // HALO
    return pl.pallas_call(
        body, name=name, grid=(S // ts,),
        in_specs=[
            pl.BlockSpec((ts, K), lambda i: (i, 0)),
            pl.BlockSpec((HALO, K), lambda i: (jnp.maximum(i * hb - 1, 0), 0)),
            pl.BlockSpec((N_CHIPS, None, K, A_WIDTH), lambda i: (0, 0, 0, 0), pipeline_mode=pl.Buffered(1)),
            pl.BlockSpec((3, A_WIDTH), lambda i: (0, 0)),
            pl.BlockSpec((4, POOL_GROUP, POOL_GROUP), lambda i: (0, 0, 0)),
            pl.BlockSpec((1, 4 * POOL_GROUP), lambda i: (0, 0)),
        ],
        out_specs=[pl.BlockSpec((ts, EVEN_IN), lambda i: (i, 0)), pl.BlockSpec((ts, D_MODEL), lambda i: (i, 0))],
        out_shape=[_sds((S, EVEN_IN), F32), _sds((S, D_MODEL), BF16)], compiler_params=_params("parallel"),
    )(xn, xn, w_in, conv_w, pool_w, pool_scale)


def mixer_bwd(proj, dmix, conv_w, pool_w, pool_scale, name, ts=256):
    S = proj.shape[0]
    n = ts + 2 * HALO
    nt = S // ts
    tn_dims = (((0,), (0,)), ((), ()))
    nt_dims = (((1,), (1,)), ((), ()))

    def body(pm_ref, pb_ref, pa_ref, dm_ref, da_ref, cw_ref, pw_ref, ps_ref, o_ref, dcw_ref, dpw_ref, dps_ref):
        i = pl.program_id(0)
        last = i == nt - 1
        before = jnp.where(i > 0, pb_ref[...], 0.0)
        after = jnp.where(last, 0.0, pa_ref[...])
        ext = jnp.concatenate([before, pm_ref[...], after], axis=0)
        dafter = jnp.where(last, 0.0, da_ref[...])
        dext = jnp.concatenate([jnp.zeros((HALO, D_MODEL), F32), dm_ref[...], dafter], axis=0)
        cw = cw_ref[...]
        main = slice(HALO, HALO + ts)

        @pl.when(i == 0)
        def _():
            dcw_ref[...] = jnp.zeros_like(dcw_ref)
            dpw_ref[...] = jnp.zeros_like(dpw_ref)
            dps_ref[...] = jnp.zeros_like(dps_ref)

        h, gb, gc = ext[:, 0:A_WIDTH], ext[:, A_WIDTH:2 * A_WIDTH], ext[:, 2 * A_WIDTH:3 * A_WIDTH]
        z = gc * h
        z1, z2 = _shift_down(z, 1), _shift_down(z, 2)
        cz = (z2 * cw[0:1] + z1 * cw[1:2]) + z * cw[2:3]
        dya = dext[:, 0:A_WIDTH]
        dcz = dya * gb
        dz = dcz * cw[2:3] + _shift_up(dcz, 1) * cw[1:2] + _shift_up(dcz, 2) * cw[0:1]
        o_ref[:, 0:A_WIDTH] = (dz * gc)[main].astype(BF16)
        o_ref[:, A_WIDTH:2 * A_WIDTH] = (dya * cz)[main].astype(BF16)
        o_ref[:, 2 * A_WIDTH:3 * A_WIDTH] = (dz * h)[main].astype(BF16)
        dczm = dcz[main]
        dcw_ref[0:1, :] += jnp.sum(dczm * z2[main], axis=0, keepdims=True)
        dcw_ref[1:2, :] += jnp.sum(dczm * z1[main], axis=0, keepdims=True)
        dcw_ref[2:3, :] += jnp.sum(dczm * z[main], axis=0, keepdims=True)

        for g, k in enumerate(POOL_WINDOWS):
            lo = 3 * A_WIDTH + g * POOL_GROUP
            cols = slice(g * POOL_GROUP, (g + 1) * POOL_GROUP)
            p = ext[:, lo:lo + POOL_GROUP]
            w = p
            s = 1
            while s < k:
                w = w + _shift_down(w, s)
                s *= 2
            cnt = _window_count(i * ts - HALO, n, k)
            pooled = (w / cnt - p)[main].astype(BF16)
            dyb = dext[:, A_WIDTH + g * POOL_GROUP:A_WIDTH + (g + 1) * POOL_GROUP]
            e = dyb * ps_ref[:, cols]
            pre = jnp.dot(pooled, pw_ref[g], preferred_element_type=F32)
            dps_ref[:, cols] += jnp.sum(dyb[main] * pre, axis=0, keepdims=True)
            dpw_ref[g] += lax.dot_general(pooled, e[main].astype(BF16), tn_dims, preferred_element_type=F32)
            dpooled = lax.dot_general(e.astype(BF16), pw_ref[g], nt_dims, preferred_element_type=F32)
            q = dpooled / cnt
            a = q
            s = 1
            while s < k:
                a = a + _shift_up(a, s)
                s *= 2
            o_ref[:, lo:lo + POOL_GROUP] = (a - dpooled)[main].astype(BF16)

    hb = ts // HALO
    nh = S // HALO
    before_map = lambda i: (jnp.maximum(i * hb - 1, 0), 0)
    after_map = lambda i: (jnp.minimum((i + 1) * hb, nh - 1), 0)
    full = lambda *shape: pl.BlockSpec(shape, lambda i: (0,) * len(shape))
    return pl.pallas_call(
        body, name=name, grid=(nt,),
        in_specs=[
            pl.BlockSpec((ts, EVEN_IN), lambda i: (i, 0)),
            pl.BlockSpec((HALO, EVEN_IN), before_map),
            pl.BlockSpec((HALO, EVEN_IN), after_map),
            pl.BlockSpec((ts, D_MODEL), lambda i: (i, 0)),
            pl.BlockSpec((HALO, D_MODEL), after_map),
            full(3, A_WIDTH), full(4, POOL_GROUP, POOL_GROUP), full(1, 4 * POOL_GROUP),
        ],
        out_specs=[pl.BlockSpec((ts, EVEN_IN), lambda i: (i, 0)), full(3, A_WIDTH), full(4, POOL_GROUP, POOL_GROUP),
                   full(1, 4 * POOL_GROUP)],
        out_shape=[_sds((S, EVEN_IN), BF16), _sds((3, A_WIDTH), F32), _sds((4, POOL_GROUP, POOL_GROUP), F32),
                   _sds((1, 4 * POOL_GROUP), F32)],
        compiler_params=_params("arbitrary"),
    )(proj, proj, proj, dmix, dmix, conv_w, pool_w, pool_scale)


FFN_HALO = 16
FFN_TC = 1408


GLU_CHUNKS = ((0, 512), (512, 512), (1024, 384))


def up_glu_fwd(xn, w_up, conv_w, conv_b, name, tm=512):
    S, K = xn.shape
    nc = D_FF // FFN_TC

    def body(xm_ref, xb_ref, wg_ref, wu_ref, cwg_ref, cwu_ref, cbg_ref, cbu_ref, pg_ref, pu_ref, ug_ref, uu_ref, o_ref):
        before = jnp.where(pl.program_id(1) > 0, xb_ref[...], jnp.zeros_like(xb_ref))
        rows = jnp.concatenate([before, xm_ref[...]], axis=0)
        for lo, width in GLU_CHUNKS:
            cols = slice(lo, lo + width)
            pre_g = jnp.dot(rows, wg_ref[:, cols], preferred_element_type=F32)
            pre_u = jnp.dot(rows, wu_ref[:, cols], preferred_element_type=F32)
            gate = _conv3(pre_g, cwg_ref[:, cols])[FFN_HALO:] + cbg_ref[:, cols]
            upv = _conv3(pre_u, cwu_ref[:, cols])[FFN_HALO:] + cbu_ref[:, cols]
            pg_ref[:, cols] = pre_g[FFN_HALO:].astype(BF16)
            pu_ref[:, cols] = pre_u[FFN_HALO:].astype(BF16)
            ug_ref[:, cols] = gate.astype(BF16)
            uu_ref[:, cols] = upv.astype(BF16)
            o_ref[:, cols] = ((gate * (1.0 / (1.0 + jnp.exp(-gate)))) * upv).astype(BF16)

    hb = tm // FFN_HALO
    wspec = lambda off: pl.BlockSpec((None, None, K, FFN_TC), lambda j, m: (j + off, 0, 0, 0))
    cw = lambda off: pl.BlockSpec((3, FFN_TC), lambda j, m: (0, j + off))
    cb = lambda off: pl.BlockSpec((1, FFN_TC), lambda j, m: (0, j + off))
    out = pl.BlockSpec((tm, FFN_TC), lambda j, m: (m, j))
    pg, pu, ug, uu, act = pl.pallas_call(
        body, name=name, grid=(nc, S // tm),
        in_specs=[pl.BlockSpec((tm, K), lambda j, m: (m, 0)),
                  pl.BlockSpec((FFN_HALO, K), lambda j, m: (jnp.maximum(m * hb - 1, 0), 0)),
                  wspec(0), wspec(nc), cw(0), cw(nc), cb(0), cb(nc)],
        out_specs=[out] * 5, out_shape=[_sds((S, D_FF), BF16)] * 5,
        compiler_params=_params("parallel", "parallel"),
    )(xn, xn, w_up, w_up, conv_w, conv_w, conv_b, conv_b)
    return (pg, pu), (ug, uu), act


def glu_bwd(up, u, da, conv_w, name, ts=256):
    S = up[0].shape[0]
    nc = D_FF // FFN_TC
    nt = S // ts
    W = 2 * D_FF

    def body(xg_ref, xu_ref, gm_ref, ga_ref, um_ref, ua_ref, dm_ref, da_ref, cw_ref, dx_ref, dcw_ref, dcb_ref):
        i = pl.program_id(0)
        last = i == nt - 1

        @pl.when(i == 0)
        def _():
            dcw_ref[...] = jnp.zeros_like(dcw_ref)
            dcb_ref[...] = jnp.zeros_like(dcb_ref)

        def rows(m_ref, a_ref, cols):
            return jnp.concatenate([m_ref[:, cols], a_ref[:, cols]], axis=0).astype(F32)

        def back(d, x, cols):
            cw = cw_ref[:, cols]
            d1, d2 = _shift_up(d, 1), _shift_up(d, 2)
            dx_ref[:, cols] = ((d * cw[2:3] + d1 * cw[1:2]) + d2 * cw[0:1])[:ts].astype(BF16)
            dcb_ref[:, cols] += jnp.sum(d[:ts], axis=0, keepdims=True)
            dcw_ref[0:1, cols] += jnp.sum(d2[:ts] * x, axis=0, keepdims=True)
            dcw_ref[1:2, cols] += jnp.sum(d1[:ts] * x, axis=0, keepdims=True)
            dcw_ref[2:3, cols] += jnp.sum(d[:ts] * x, axis=0, keepdims=True)

        for c in range(nc):
            cols = slice(c * FFN_TC, (c + 1) * FFN_TC)
            ug, uu = rows(gm_ref, ga_ref, cols), rows(um_ref, ua_ref, cols)
            dae = rows(dm_ref, da_ref, cols)
            dae = jnp.where(last & (lax.broadcasted_iota(jnp.int32, dae.shape, 0) >= ts), 0.0, dae)
            sg = 1.0 / (1.0 + jnp.exp(-ug))
            duu = dae * (ug * sg)
            dug = (dae * uu) * (sg * (1.0 + ug * (1.0 - sg)))
            back(dug, xg_ref[:, cols].astype(F32), cols)
            back(duu, xu_ref[:, cols].astype(F32), slice(D_FF + c * FFN_TC, D_FF + (c + 1) * FFN_TC))

    hb = ts // FFN_HALO
    nh = S // FFN_HALO
    after_map = lambda i: (jnp.minimum((i + 1) * hb, nh - 1), 0)
    main = pl.BlockSpec((ts, D_FF), lambda i: (i, 0))
    after = pl.BlockSpec((FFN_HALO, D_FF), after_map)
    return pl.pallas_call(
        body, name=name, grid=(nt,),
        in_specs=[main, main, main, after, main, after, main, after, pl.BlockSpec((3, W), lambda i: (0, 0))],
        out_specs=[pl.BlockSpec((ts, W), lambda i: (i, 0)), pl.BlockSpec((3, W), lambda i: (0, 0)),
                   pl.BlockSpec((1, W), lambda i: (0, 0))],
        out_shape=[_sds((S, W), BF16), _sds((3, W), F32), _sds((1, W), F32)],
        compiler_params=_params("arbitrary"),
    )(up[0], up[1], u[0], u[0], u[1], u[1], da, da, conv_w)


MEAN_GROUP = 256


def _head_mean_matrix():
    h = np.arange(MEAN_GROUP) // HEAD_DIM
    return jnp.asarray((h[:, None] == h[None, :]).astype(np.float32) / HEAD_DIM, dtype=BF16)


def _head_mean(v, gm):
    vb = v.astype(BF16)
    return jnp.concatenate([jnp.dot(vb[:, c:c + MEAN_GROUP], gm, preferred_element_type=F32)
                            for c in range(0, v.shape[1], MEAN_GROUP)], axis=1)


def qkv_qknorm_fwd(xn, w_qkv, gqk, name, tm=1024):
    S, K = xn.shape
    J, _, _, Ns = w_qkv.shape
    gains = gqk.reshape(1, 3 * D_MODEL)

    def body(x_ref, w_ref, g_ref, gm_ref, raw_ref, o_ref):
        first_col = pl.program_id(0) * Ns
        acc = jnp.dot(x_ref[...], w_ref[...], preferred_element_type=F32)
        raw_ref[...] = acc
        gm = gm_ref[...]
        for c in range(0, Ns, MEAN_GROUP):
            cols = slice(c, c + MEAN_GROUP)
            x = acc[:, cols]
            mean = jnp.dot((x * x).astype(BF16), gm, preferred_element_type=F32)
            normed = (x * lax.rsqrt(mean + EPS)) * g_ref[:, cols]
            o_ref[:, cols] = jnp.where(first_col + c >= 2 * D_MODEL, x, normed).astype(BF16)

    return pl.pallas_call(
        body, name=name, grid=(J, S // tm),
        in_specs=[pl.BlockSpec((tm, K), lambda j, m: (m, 0)), pl.BlockSpec((None, None, K, Ns), lambda j, m: (j, 0, 0, 0)),
                  pl.BlockSpec((1, Ns), lambda j, m: (0, j)), pl.BlockSpec((MEAN_GROUP, MEAN_GROUP), lambda j, m: (0, 0))],
        out_specs=[pl.BlockSpec((tm, Ns), lambda j, m: (m, j))] * 2,
        out_shape=[_sds((S, J * Ns), F32), _sds((S, J * Ns), BF16)], compiler_params=_params("parallel", "parallel"),
    )(xn, w_qkv, gains, _head_mean_matrix())


def qknorm_bwd(qkv, dq, dk, dv, gqk, name, ts=256):
    S = qkv.shape[0]

    def body(x_ref, dq_ref, dk_ref, dv_ref, g_ref, gm_ref, o_ref, dg_ref):
        @pl.when(pl.program_id(0) == 0)
        def _():
            dg_ref[...] = jnp.zeros_like(dg_ref)

        gm = gm_ref[...]
        for part, d_ref in enumerate((dq_ref, dk_ref)):
            cols = slice(part * D_MODEL, (part + 1) * D_MODEL)
            x = x_ref[:, cols]
            d = d_ref[...]
            r = lax.rsqrt(_head_mean(x * x, gm) + EPS)
            gx = d * g_ref[part]
            o_ref[:, cols] = (r * gx - x * ((r * r * r) * _head_mean(gx * x, gm))).astype(BF16)
            dg_ref[part] += jnp.sum(d * (x * r), axis=0, keepdims=True)
        o_ref[:, 2 * D_MODEL:] = dv_ref[...].astype(BF16)

    row = pl.BlockSpec((ts, D_MODEL), lambda i: (i, 0))
    wide = pl.BlockSpec((ts, 3 * D_MODEL), lambda i: (i, 0))
    gains = pl.BlockSpec((3, 1, D_MODEL), lambda i: (0, 0, 0))
    return pl.pallas_call(
        body, name=name, grid=(S // ts,),
        in_specs=[wide, row, row, row, gains, pl.BlockSpec((MEAN_GROUP, MEAN_GROUP), lambda i: (0, 0))],
        out_specs=[wide, gains],
        out_shape=[_sds((S, 3 * D_MODEL), BF16), _sds((3, 1, D_MODEL), F32)],
        compiler_params=_params("arbitrary"),
    )(qkv, dq, dk, dv, gqk, _head_mean_matrix())


RESIDUES = 16


def _block_order(dil):
    runs = RESIDUES // dil
    slot = np.arange(ATT_BLOCK)
    return (slot % (ATT_BLOCK // runs)) * runs + slot // (ATT_BLOCK // runs)


def _bucket_tables():
    n = ATT_BLOCK
    max_exact = N_REL_BUCKETS // 2
    buckets, valids = [], []
    for _, dil in DILATED_PAIRS:
        order = _block_order(dil)
        a = order[:, None]
        c = np.concatenate([order, n + order])[None, :]
        first_half = (np.arange(2 * n) < n)[None, :]
        rel = a + n - c
        band = (rel >= 0) & (rel <= n)
        dist = np.clip(rel, 0, n) * dil
        dd = np.maximum(dist, 1).astype(np.float32)
        large = max_exact + (np.log(dd / np.float32(max_exact)) / np.float32(math.log(REL_MAX_DISTANCE / max_exact))
                             * np.float32(N_REL_BUCKETS - max_exact)).astype(np.int32)
        large = np.minimum(large, N_REL_BUCKETS - 1)
        buckets.append(np.where(dist < max_exact, dist, large).reshape(1, -1))
        valids.append(np.stack([(band & ~first_half).reshape(1, -1), band.reshape(1, -1)]))
    return np.stack(buckets).astype(np.int32), np.stack(valids).astype(np.int32)


BIAS_CHUNK = 8192


def _split3(x):
    a = x.astype(BF16)
    r = x - a.astype(F32)
    b = r.astype(BF16)
    c = (r - b.astype(F32)).astype(BF16)
    return a, b, c


def bias_expand(rel_bias_t, name):
    bucket, valid = _bucket_tables()
    nq = bucket.shape[-1]

    def body(t_ref, b_ref, v_ref, o_ref):
        onehot = (lax.broadcasted_iota(jnp.int32, (N_REL_BUCKETS, BIAS_CHUNK), 0) == b_ref[...]).astype(BF16)
        acc = None
        for term in _split3(t_ref[...]):
            p = jnp.dot(term, onehot, preferred_element_type=F32)
            acc = p if acc is None else acc + p
        o_ref[...] = jnp.where(v_ref[...] > 0, acc, MASK_VALUE)

    return pl.pallas_call(
        body, name=name, grid=(3, 2, nq // BIAS_CHUNK),
        in_specs=[pl.BlockSpec((N_HEADS, N_REL_BUCKETS), lambda b, v, c: (0, 0)),
                  pl.BlockSpec((None, 1, BIAS_CHUNK), lambda b, v, c: (b, 0, c)),
                  pl.BlockSpec((None, None, 1, BIAS_CHUNK), lambda b, v, c: (b, v, 0, c))],
        out_specs=pl.BlockSpec((None, None, N_HEADS, BIAS_CHUNK), lambda b, v, c: (b, v, 0, c)),
        out_shape=_sds((3, 2, N_HEADS, nq), F32), compiler_params=_params("parallel", "parallel", "parallel"),
    )(rel_bias_t, jnp.asarray(bucket), jnp.asarray(valid))


def bias_reduce(dbias, name):
    bucket, _ = _bucket_tables()
    nq = bucket.shape[-1]
    dims = (((1,), (1,)), ((), ()))

    def body(d_ref, b_ref, o_ref):
        onehot = (lax.broadcasted_iota(jnp.int32, (N_REL_BUCKETS, BIAS_CHUNK), 0) == b_ref[...]).astype(BF16)
        acc = None
        for term in _split3(d_ref[...]):
            p = lax.dot_general(term, onehot, dims, preferred_element_type=F32)
            acc = p if acc is None else acc + p

        @pl.when(pl.program_id(1) == 0)
        def _():
            o_ref[...] = acc

        @pl.when(pl.program_id(1) > 0)
        def _():
            o_ref[...] += acc

    return pl.pallas_call(
        body, name=name, grid=(3, nq // BIAS_CHUNK),
        in_specs=[pl.BlockSpec((None, N_HEADS, BIAS_CHUNK), lambda b, c: (b, 0, c)),
                  pl.BlockSpec((None, 1, BIAS_CHUNK), lambda b, c: (b, 0, c))],
        out_specs=pl.BlockSpec((None, N_HEADS, N_REL_BUCKETS), lambda b, c: (b, 0, 0)),
        out_shape=_sds((3, N_HEADS, N_REL_BUCKETS), F32), compiler_params=_params("parallel", "arbitrary"),
    )(dbias, jnp.asarray(bucket))


PAIR = 2 * HEAD_DIM
N_PAIRS = N_HEADS // 2
_NT = (((1,), (1,)), ((), ()))
_TN = (((0,), (0,)), ((), ()))


def _low_lanes(shape):
    return lax.broadcasted_iota(jnp.int32, shape, 1) < HEAD_DIM


ATTN_VMEM_LIMIT_BYTES = 56 * 1024 * 1024
BRANCH_ORDER = (2, 1, 0)


def _regroup(dst, src, L16):
    for r in range(RESIDUES):
        dst[pl.ds(r * L16, L16), :] = src[pl.ds(r, L16, stride=RESIDUES), :]


def _ungroup(dst, src, L16):
    for r in range(RESIDUES):
        dst[pl.ds(r, L16, stride=RESIDUES), :] = src[pl.ds(r * L16, L16), :]


def _branch_geometry(branch, S):
    dil = DILATED_PAIRS[branch][1]
    runs = RESIDUES // dil
    return dil, runs, ATT_BLOCK // runs, S // dil // ATT_BLOCK


def _block_rows(it, branch, S):
    dil, runs, run_len, n_blocks = _branch_geometry(branch, S)
    L16 = S // RESIDUES
    r, b = it // n_blocks, it % n_blocks
    prev = jnp.maximum(b - 1, 0)
    cur_rows = [pl.multiple_of((j * dil + r) * L16 + run_len * b, 8) for j in range(runs)]
    prev_rows = [pl.multiple_of((j * dil + r) * L16 + run_len * prev, 8) for j in range(runs)]
    return cur_rows, prev_rows, jnp.minimum(b, 1)


def _load_block(ref, rows, run_len):
    parts = [ref[pl.ds(o, run_len), :] for o in rows]
    return parts[0] if len(parts) == 1 else jnp.concatenate(parts, axis=0)


def _store_block(ref, rows, run_len, value, add=False):
    for j, o in enumerate(rows):
        part = value[j * run_len:(j + 1) * run_len]
        if add:
            ref[pl.ds(o, run_len), :] += part
        else:
            ref[pl.ds(o, run_len), :] = part


ATTN_FWD_UNROLL = 8
ATTN_BWD_UNROLL = 4


def _stack_heads(x, low):
    zero = jnp.zeros_like(x)
    return jnp.concatenate([jnp.where(low, x, zero), jnp.where(low, zero, x)], axis=0)


def _unstack_heads(y, low):
    return jnp.where(low, y[:ATT_BLOCK], y[ATT_BLOCK:])


def attn_fwd(qkvn, bias, name):
    S = qkvn.shape[0]
    L16 = S // RESIDUES
    n_iter = S // ATT_BLOCK

    def body(q_ref, k_ref, v_ref, b_ref, o_ref, lse_ref, stage, qp, kp, vp, acc_s, m_s, l_s):
        for src, dst in ((q_ref, qp), (k_ref, kp), (v_ref, vp)):
            stage[...] = src[...].astype(F32)
            _regroup(dst, stage, L16)
        low = _low_lanes((ATT_BLOCK, PAIR))

        for branch in BRANCH_ORDER:
            _, _, run_len, _ = _branch_geometry(branch, S)
            first = branch == BRANCH_ORDER[0]

            def step(it, carry, branch=branch, run_len=run_len, first=first):
                cur, prev, variant = _block_rows(it, branch, S)
                q = _load_block(qp, cur, run_len).astype(BF16)
                k = jnp.concatenate([_load_block(kp, prev, run_len), _load_block(kp, cur, run_len)], axis=0).astype(BF16)
                v = jnp.concatenate([_load_block(vp, prev, run_len), _load_block(vp, cur, run_len)], axis=0).astype(BF16)
                s = lax.dot_general(_stack_heads(q, low), k, _NT, preferred_element_type=F32) * (HEAD_DIM ** -0.5)
                s = s + b_ref[2 * branch + variant].reshape(2 * ATT_BLOCK, 2 * ATT_BLOCK)
                mx = jnp.max(s, axis=-1, keepdims=True)
                p = jnp.exp(s - mx)
                den = jnp.sum(p, axis=-1, keepdims=True)
                pv = jnp.dot(p.astype(BF16), v, preferred_element_type=F32)
                acc = _unstack_heads(pv, low)
                m = _unstack_heads(mx, low)
                l = _unstack_heads(den, low)
                if not first:
                    m_old = _load_block(m_s, cur, run_len)
                    m_new = jnp.maximum(m_old, m)
                    a_old, a_new = jnp.exp(m_old - m_new), jnp.exp(m - m_new)
                    acc = _load_block(acc_s, cur, run_len) * a_old + acc * a_new
                    l = _load_block(l_s, cur, run_len) * a_old + l * a_new
                    m = m_new
                _store_block(acc_s, cur, run_len, acc)
                _store_block(m_s, cur, run_len, m)
                _store_block(l_s, cur, run_len, l)
                return carry

            lax.fori_loop(0, n_iter, step, 0, unroll=ATTN_FWD_UNROLL)

        acc_s[...] = acc_s[...] / l_s[...]
        _ungroup(stage, acc_s, L16)
        o_ref[...] = stage[...].astype(BF16)
        m_s[...] = m_s[...] + jnp.log(l_s[...])
        _ungroup(lse_ref, m_s, L16)

    col = lambda part: pl.BlockSpec((S, PAIR), lambda hp: (0, part * N_PAIRS + hp))
    out = pl.BlockSpec((S, PAIR), lambda hp: (0, hp))
    return pl.pallas_call(
        body, name=name, grid=(N_PAIRS,),
        in_specs=[col(0), col(1), col(2), pl.BlockSpec((6, 2, ATT_BLOCK, 2 * ATT_BLOCK), lambda hp: (0, hp, 0, 0))],
        out_specs=[out, out], out_shape=[_sds((S, D_MODEL), BF16), _sds((S, D_MODEL), F32)],
        scratch_shapes=[pltpu.VMEM((S, PAIR), F32)] * 7,
        compiler_params=pltpu.CompilerParams(dimension_semantics=("parallel",), vmem_limit_bytes=ATTN_VMEM_LIMIT_BYTES),
    )(qkvn, qkvn, qkvn, bias)


def attn_bwd(qkvn, att, datt, lse, bias, name):
    S = qkvn.shape[0]
    L16 = S // RESIDUES
    n_iter = S // ATT_BLOCK
    TILE = 512

    def body(q_ref, k_ref, v_ref, o_ref, do_ref, lse_ref, b_ref, dq_ref, dk_ref, dv_ref, db_ref,
             qp, kp, vp, dop, ldp, dqp, dkp, dvp):
        stage = dqp
        for src, dst in ((q_ref, qp), (k_ref, kp), (v_ref, vp), (do_ref, dop)):
            stage[...] = src[...].astype(F32)
            _regroup(dst, stage, L16)

        def pack(i, carry):
            rows = pl.ds(pl.multiple_of(i * TILE, TILE), TILE)
            low = _low_lanes((TILE, PAIR))
            lane = lax.broadcasted_iota(jnp.int32, (TILE, PAIR), 1)
            prod = do_ref[rows, :].astype(F32) * o_ref[rows, :].astype(F32)
            d0 = jnp.sum(jnp.where(low, prod, 0.0), axis=-1, keepdims=True)
            d1 = jnp.sum(jnp.where(low, 0.0, prod), axis=-1, keepdims=True)
            stage[rows, :] = jnp.where((lane & (HEAD_DIM // 2)) == 0, lse_ref[rows, :], jnp.where(low, d0, d1))
            return carry

        lax.fori_loop(0, S // TILE, pack, 0)
        _regroup(ldp, stage, L16)
        dqp[...] = jnp.zeros_like(dqp)
        dkp[...] = jnp.zeros_like(dkp)
        dvp[...] = jnp.zeros_like(dvp)
        db_ref[...] = jnp.zeros_like(db_ref)
        low = _low_lanes((ATT_BLOCK, PAIR))

        for branch in BRANCH_ORDER:
            _, _, run_len, _ = _branch_geometry(branch, S)

            def step(it, carry, branch=branch, run_len=run_len):
                cur, prev, variant = _block_rows(it, branch, S)
                q = _load_block(qp, cur, run_len).astype(BF16)
                dout = _load_block(dop, cur, run_len).astype(BF16)
                ld = _load_block(ldp, cur, run_len)
                k = jnp.concatenate([_load_block(kp, prev, run_len), _load_block(kp, cur, run_len)], axis=0).astype(BF16)
                v = jnp.concatenate([_load_block(vp, prev, run_len), _load_block(vp, cur, run_len)], axis=0).astype(BF16)
                half = HEAD_DIM // 2
                lse2 = jnp.concatenate([ld[:, 0:1], ld[:, HEAD_DIM:HEAD_DIM + 1]], axis=0)
                delta2 = jnp.concatenate([ld[:, half:half + 1], ld[:, HEAD_DIM + half:HEAD_DIM + half + 1]], axis=0)
                q2, do2 = _stack_heads(q, low), _stack_heads(dout, low)
                s = lax.dot_general(q2, k, _NT, preferred_element_type=F32) * (HEAD_DIM ** -0.5)
                p = jnp.exp(s + b_ref[2 * branch + variant].reshape(2 * ATT_BLOCK, 2 * ATT_BLOCK) - lse2)
                dp = lax.dot_general(do2, v, _NT, preferred_element_type=F32)
                ds = p * (dp - delta2)
                db_ref[branch] += ds.reshape(2, ATT_BLOCK, 2 * ATT_BLOCK)
                dsb = (ds * (HEAD_DIM ** -0.5)).astype(BF16)
                dq = _unstack_heads(jnp.dot(dsb, k, preferred_element_type=F32), low)
                dk = lax.dot_general(dsb, q2, _TN, preferred_element_type=F32)
                dv = lax.dot_general(p.astype(BF16), do2, _TN, preferred_element_type=F32)
                _store_block(dqp, cur, run_len, dq, add=True)
                _store_block(dkp, prev, run_len, dk[:ATT_BLOCK], add=True)
                _store_block(dvp, prev, run_len, dv[:ATT_BLOCK], add=True)
                _store_block(dkp, cur, run_len, dk[ATT_BLOCK:], add=True)
                _store_block(dvp, cur, run_len, dv[ATT_BLOCK:], add=True)
                return carry

            lax.fori_loop(0, n_iter, step, 0, unroll=ATTN_BWD_UNROLL)

        _ungroup(dq_ref, dqp, L16)
        _ungroup(dk_ref, dkp, L16)
        _ungroup(dv_ref, dvp, L16)

    col = lambda part: pl.BlockSpec((S, PAIR), lambda hp: (0, part * N_PAIRS + hp))
    one = pl.BlockSpec((S, PAIR), lambda hp: (0, hp))
    return pl.pallas_call(
        body, name=name, grid=(N_PAIRS,),
        in_specs=[col(0), col(1), col(2), one, one, one,
                  pl.BlockSpec((6, 2, ATT_BLOCK, 2 * ATT_BLOCK), lambda hp: (0, hp, 0, 0))],
        out_specs=[one, one, one, pl.BlockSpec((3, 2, ATT_BLOCK, 2 * ATT_BLOCK), lambda hp: (0, hp, 0, 0))],
        out_shape=[_sds((S, D_MODEL), F32)] * 3 + [_sds((3, N_HEADS, ATT_BLOCK, 2 * ATT_BLOCK), F32)],
        scratch_shapes=[pltpu.VMEM((S, PAIR), F32)] * 8,
        compiler_params=pltpu.CompilerParams(dimension_semantics=("parallel",), vmem_limit_bytes=ATTN_VMEM_LIMIT_BYTES),
    )(qkvn, qkvn, qkvn, att, datt, lse, bias)


def _adamw_step(w_ref, g_ref, m_ref, v_ref, d_ref, nm_ref, nv_ref):
    gv = g_ref[...]
    m2 = ADAM_B1 * m_ref[...] + (1.0 - ADAM_B1) * gv
    v2 = ADAM_B2 * v_ref[...] + (1.0 - ADAM_B2) * (gv * gv)
    m_hat = m2 / (1.0 - ADAM_B1 ** ADAM_STEP)
    v_hat = v2 / (1.0 - ADAM_B2 ** ADAM_STEP)
    d_ref[...] = -ADAM_LR * (m_hat / (jnp.sqrt(v_hat) + ADAM_EPS) + ADAM_WD * w_ref[...])
    nm_ref[...] = m2
    nv_ref[...] = v2


def adamw_small(ws, gs, ms, vs, name):
    n = len(ws)

    def body(*refs):
        groups = [refs[k * n:(k + 1) * n] for k in range(7)]
        for refs_of_one in zip(*groups):
            _adamw_step(*refs_of_one)

    outs = pl.pallas_call(body, name=name, out_shape=[_sds(a.shape, F32) for a in ws] * 3,
                          compiler_params=_params())(*ws, *gs, *ms, *vs)
    return outs[:n], outs[n:2 * n], outs[2 * n:]


def adamw(w, g, m, v, name):
    n, R, C = w.shape

    def body(w_ref, g_ref, m_ref, v_ref, d_ref, nm_ref, nv_ref, go_ref):
        go_ref[...] = g_ref[...]
        _adamw_step(w_ref, g_ref, m_ref, v_ref, d_ref, nm_ref, nv_ref)

    tr = R
    while tr * C * 4 > ELEMENTWISE_BLOCK_BYTES and tr % 16 == 0:
        tr //= 2
    spec = pl.BlockSpec((None, tr, C), lambda i, r: (i, r, 0))
    return pl.pallas_call(
        body, name=name, grid=(n, R // tr), in_specs=[spec] * 4, out_specs=[spec] * 4,
        out_shape=[_sds((n, R, C), F32)] * 4, compiler_params=_params("parallel", "parallel"),
    )(w, g, m, v)


ANY = pl.BlockSpec(memory_space=pl.ANY)


def _coords():
    return lax.axis_index("x"), lax.axis_index("y"), lax.axis_index("c")


def _other_chips(mx, my):
    return [(1 - mx, my), (mx, 1 - my), (1 - mx, 1 - my)]


def _remote(src, dst, send, recv, dev):
    return pltpu.make_async_remote_copy(src_ref=src, dst_ref=dst, send_sem=send, recv_sem=recv, device_id=dev,
                                        device_id_type=MESH)


HBM =pl.BlockSpec(memory_space=pltpu.HBM)
SEM = pl.BlockSpec(memory_space=pltpu.SEMAPHORE)
_SPLIT_COPY = pltpu.CompilerParams(has_side_effects=pltpu.SideEffectType.DATAFLOW_SIDE_EFFECTING)


def _in_hbm(a):
    return pltpu.with_memory_space_constraint(a, pltpu.HBM)


def cast_into_slot(w, layer, chip_core, name, dtype=BF16):
    _, _, hR, C = w.shape

    def body(s_ref, w_ref, o_ref):
        del s_ref
        o_ref[...] = w_ref[...].astype(dtype)

    grid_spec = pltpu.PrefetchScalarGridSpec(
        num_scalar_prefetch=1, grid=(2,),
        in_specs=[pl.BlockSpec((None, None, hR, C), lambda h, s: (layer, h, 0, 0))],
        out_specs=pl.BlockSpec((None, None, hR, C), lambda h, s: (s[0], h, 0, 0)))
    return pl.pallas_call(body, name=name, grid_spec=grid_spec, out_shape=_sds_hbm((N_CHIPS, 2, hR, C), dtype),
                          compiler_params=_params("parallel"))(chip_core, w)


def gather_start(lands, groups, name):
    n = len(lands)
    n_groups = len(groups)

    def body(*refs):
        ins = refs[:n]
        sems = refs[n:n + 2 * n_groups]
        token = refs[-1]
        mx, my, mc = _coords()
        chip = 2 * mx + my
        for g, members in enumerate(groups):
            send, recv = sems[2 * g], sems[2 * g + 1]
            for i, a in enumerate(members):
                mine = ins[a].at[chip, mc]
                for k, (px, py) in enumerate(_other_chips(mx, my)):
                    _remote(mine, mine, send.at[3 * i + k], recv.at[3 * i + k], (px, py, mc)).start()
        token[...] = jnp.zeros_like(token)

    sem_shapes = []
    for members in groups:
        sem_shapes += [pltpu.SemaphoreType.DMA((3 * len(members),))] * 2
    outs = pl.pallas_call(
        body, name=name, in_specs=[HBM] * n,
        out_specs=[SEM] * (2 * n_groups) + [HBM] * n + [pl.BlockSpec(memory_space=pltpu.VMEM)],
        out_shape=sem_shapes + [pltpu.HBM(a.shape, a.dtype) for a in lands] + [_sds((SUBLANES, LANES), F32)],
        input_output_aliases={a: 2 * n_groups + a for a in range(n)}, compiler_params=_SPLIT_COPY,
    )(*[_in_hbm(a) for a in lands])
    sems = [(outs[2 * g], outs[2 * g + 1]) for g in range(n_groups)]
    return sems, list(outs[2 * n_groups:2 * n_groups + n]), outs[-1]


def gather_forward(lands, sems, after, name):
    n = len(lands)

    def body(*refs):
        ins = refs[:n]
        send, recv = refs[n], refs[n + 1]
        fsend, frecv = refs[n + 3], refs[n + 4]
        mx, my, mc = _coords()
        for i in range(n):
            for k, (px, py) in enumerate(_other_chips(mx, my)):
                landed = ins[i].at[2 * px + py, mc]
                cp = _remote(landed, landed, send.at[3 * i + k], recv.at[3 * i + k], (px, py, mc))
                cp.wait_send()
                cp.wait_recv()
                _remote(landed, landed, fsend.at[3 * i + k], frecv.at[3 * i + k], (mx, my, 1 - mc)).start()

    outs = pl.pallas_call(
        body, name=name, in_specs=[HBM] * n + [SEM, SEM, ANY], out_specs=[SEM, SEM] + [HBM] * n,
        out_shape=[pltpu.SemaphoreType.DMA((3 * n,))] * 2 + [pltpu.HBM(a.shape, a.dtype) for a in lands],
        input_output_aliases={a: 2 + a for a in range(n)}, compiler_params=_SPLIT_COPY,
    )(*lands, sems[0], sems[1], after)
    return (outs[0], outs[1]), list(outs[2:])


def gather_wait(lands, sems, after, name):
    n = len(lands)

    def body(*refs):
        ins = refs[:n]
        fsend, frecv = refs[n], refs[n + 1]
        mx, my, mc = _coords()
        for i in range(n):
            for k, (px, py) in enumerate(_other_chips(mx, my)):
                theirs = ins[i].at[2 * px + py, 1 - mc]
                cp = _remote(theirs, theirs, fsend.at[3 * i + k], frecv.at[3 * i + k], (mx, my, 1 - mc))
                cp.wait_send()
                cp.wait_recv()

    outs = pl.pallas_call(
        body, name=name, in_specs=[HBM] * n + [SEM, SEM, ANY], out_specs=[HBM] * n,
        out_shape=[pltpu.HBM(a.shape, a.dtype) for a in lands],
        input_output_aliases={a: a for a in range(n)}, compiler_params=_SPLIT_COPY,
    )(*lands, sems[0], sems[1], after)
    return list(outs)


def _peers(mx, my, mc):
    return [(1 - mx if k & 4 else mx, 1 - my if k & 2 else my, 1 - mc if k & 1 else mc) for k in range(1, N_DEV)]


def devices_start(x, name):
    def body(x_ref, land_ref, send, recv, x_thru, land_thru):
        mx, my, mc = _coords()
        me = 4 * mx + 2 * my + mc
        for k, peer in enumerate(_peers(mx, my, mc)):
            _remote(x_ref, land_ref.at[me], send.at[k], recv.at[k], peer).start()

    land = lax.empty((N_DEV,) + x.shape, x.dtype)
    outs = pl.pallas_call(
        body, name=name, in_specs=[HBM, HBM], out_specs=[SEM, SEM, HBM, HBM],
        out_shape=[pltpu.SemaphoreType.DMA((N_DEV - 1,))] * 2 + [pltpu.HBM(x.shape, x.dtype), pltpu.HBM(land.shape, x.dtype)],
        input_output_aliases={0: 2, 1: 3}, compiler_params=_SPLIT_COPY,
    )(_in_hbm(x), _in_hbm(land))
    return (outs[0], outs[1]), outs[2], outs[3]


def devices_wait(x, land, sems, after, name):
    def body(x_ref, land_ref, send, recv, after_ref, x_thru, land_thru):
        mx, my, mc = _coords()
        for k, (px, py, pc) in enumerate(_peers(mx, my, mc)):
            cp = _remote(x_ref, land_ref.at[4 * px + 2 * py + pc], send.at[k], recv.at[k], (px, py, pc))
            cp.wait_send()
            cp.wait_recv()

    outs = pl.pallas_call(
        body, name=name, in_specs=[HBM, HBM, SEM, SEM, ANY], out_specs=[HBM, HBM],
        out_shape=[pltpu.HBM(x.shape, x.dtype), pltpu.HBM(land.shape, land.dtype)],
        input_output_aliases={0: 0, 1: 1}, compiler_params=_SPLIT_COPY,
    )(x, land, sems[0], sems[1], after)
    return outs[0], outs[1]


def device_sum(land, own, me, name):
    _, R, C = land.shape

    def body(s_ref, l_ref, o_ref_in, o_ref):
        acc = None
        for q in range(N_DEV):
            term = jnp.where(s_ref[0] == q, o_ref_in[...], l_ref[q])
            acc = term if acc is None else acc + term
        o_ref[...] = acc

    grid_spec = pltpu.PrefetchScalarGridSpec(
        num_scalar_prefetch=1, grid=(1,),
        in_specs=[pl.BlockSpec((N_DEV, R, C), lambda i, s: (0, 0, 0)), pl.BlockSpec((R, C), lambda i, s: (0, 0))],
        out_specs=pl.BlockSpec((R, C), lambda i, s: (0, 0)))
    return pl.pallas_call(body, name=name, grid_spec=grid_spec, out_shape=_sds((R, C), F32),
                          compiler_params=_params("arbitrary"))(me, land, own)


def reduce_send(grads, name):
    n = len(grads)

    def body(*refs):
        ins, lands = refs[:n], refs[n:2 * n]
        send, recv = refs[2 * n], refs[2 * n + 1]
        mx, my, mc = _coords()
        me = 4 * mx + 2 * my + mc
        for a in range(n):
            for k, (px, py, pc) in enumerate(_peers(mx, my, mc)):
                _remote(ins[a].at[2 * px + py, pc], lands[a].at[me], send.at[7 * a + k], recv.at[7 * a + k], (px, py, pc)).start()

    lands = [lax.empty((N_DEV,) + g.shape[2:], g.dtype) for g in grads]
    outs = pl.pallas_call(
        body, name=name, in_specs=[HBM] * (2 * n), out_specs=[SEM, SEM] + [HBM] * (2 * n),
        out_shape=[pltpu.SemaphoreType.DMA((7 * n,))] * 2 + [pltpu.HBM(a.shape, a.dtype) for a in grads + lands],
        input_output_aliases={a: 2 + a for a in range(2 * n)}, compiler_params=_SPLIT_COPY,
    )(*[_in_hbm(a) for a in grads + lands])
    return (outs[0], outs[1]), list(outs[2:2 + n]), list(outs[2 + n:])


def reduce_wait(grads, lands, sems, after, name):
    n = len(grads)

    def body(*refs):
        ins, zones = refs[:n], refs[n:2 * n]
        send, recv = refs[2 * n], refs[2 * n + 1]
        mx, my, mc = _coords()
        for a in range(n):
            for k, (px, py, pc) in enumerate(_peers(mx, my, mc)):
                cp = _remote(ins[a].at[2 * px + py, pc], zones[a].at[4 * px + 2 * py + pc], send.at[7 * a + k],
                             recv.at[7 * a + k], (px, py, pc))
                cp.wait_send()
                cp.wait_recv()

    outs = pl.pallas_call(
        body, name=name, in_specs=[HBM] * (2 * n) + [SEM, SEM, ANY], out_specs=[HBM] * (2 * n),
        out_shape=[pltpu.HBM(a.shape, a.dtype) for a in grads + lands],
        input_output_aliases={a: a for a in range(2 * n)}, compiler_params=_SPLIT_COPY,
    )(*grads, *lands, sems[0], sems[1], after)
    return list(outs[:n]), list(outs[n:])


def reduce_sum(land, grad, place, name, into=None, layer=None):
    _, hR, C = land.shape
    tr = hR
    while N_DEV * tr * C * 2 > 3 * ELEMENTWISE_BLOCK_BYTES and tr % 32 == 0:
        tr //= 2

    def body(s_ref, l_ref, g_ref, *rest):
        o_ref = rest[-1]
        own = g_ref[...].astype(F32)
        acc = None
        for q in range(N_DEV):
            term = jnp.where(s_ref[2] == q, own, l_ref[q].astype(F32))
            acc = term if acc is None else acc + term
        o_ref[...] = acc

    in_specs = [pl.BlockSpec((N_DEV, tr, C), lambda i, s: (0, i, 0)),
                pl.BlockSpec((None, None, tr, C), lambda i, s: (s[0], s[1], i, 0))]
    args = [place, _in_hbm(land), _in_hbm(grad)]
    aliases = {}
    if layer is None:
        out_spec = pl.BlockSpec((None, tr, C), lambda i, s: (s[1], i, 0))
        out_shape = _sds_hbm((2, hR, C), F32)
    else:
        out_spec = pl.BlockSpec((None, None, tr, C), lambda i, s: (layer, s[1], i, 0))
        out_shape = _sds_hbm((2, 2, hR, C), F32)
        if into is not None:
            in_specs.append(ANY)
            args.append(into)
            aliases = {3: 0}
    grid_spec = pltpu.PrefetchScalarGridSpec(num_scalar_prefetch=1, grid=(hR // tr,), in_specs=in_specs, out_specs=out_spec)
    return pl.pallas_call(body, name=name, grid_spec=grid_spec, out_shape=out_shape, input_output_aliases=aliases,
                          compiler_params=_params("arbitrary"))(*args)


def join_halves(arrays, name):
    n = len(arrays)
    pieces = [(a, l) for a, arr in enumerate(arrays) for l in (range(arr.shape[0]) if arr.ndim == 4 else [None])]

    def body(*refs):
        ins = refs[:n]
        send, recv = refs[2 * n:]
        mx, my, mc = _coords()

        def half(a, l, h):
            return ins[a].at[h] if l is None else ins[a].at[l, h]

        sends = [_remote(half(a, l, mc), half(a, l, mc), send.at[i], recv.at[i], (mx, my, 1 - mc))
                 for i, (a, l) in enumerate(pieces)]
        for cp in sends:
            cp.start()
        for i, (a, l) in enumerate(pieces):
            theirs = half(a, l, 1 - mc)
            _remote(theirs, theirs, send.at[i], recv.at[i], (mx, my, 1 - mc)).wait_recv()
        for cp in sends:
            cp.wait_send()

    return pl.pallas_call(
        body, name=name, in_specs=[ANY] * n, out_specs=[ANY] * n, out_shape=[_sds(a.shape, a.dtype) for a in arrays],
        input_output_aliases={a: a for a in range(n)},
        scratch_shapes=[pltpu.SemaphoreType.DMA((len(pieces),)), pltpu.SemaphoreType.DMA((len(pieces),))],
    )(*arrays)


LANES = 128
SUBLANES = 8


def _n_rows(shape):
    rows = -(-int(np.prod(shape)) // LANES)
    return -(-rows // SUBLANES) * SUBLANES


def _as_rows(a):
    flat = a.reshape(-1)
    rows = _n_rows(a.shape)
    return jnp.pad(flat, (0, rows * LANES - flat.shape[0])).reshape(rows, LANES)


def _pack(arrays):
    return jnp.concatenate([_as_rows(a) for a in arrays], axis=0)


def _unpack(rows, shapes):
    out, r0 = [], 0
    for s in shapes:
        n = _n_rows(s)
        out.append(rows[r0:r0 + n].reshape(-1)[:int(np.prod(s))].reshape(s))
        r0 += n
    return out


REPLICATED_SMALL = [("rel_bias", (32, 16)), ("even_norm", (1, 1024)), ("even_pool_w", (1, 4, 128, 128)),
                    ("even_pool_scale", (1, 512)), ("odd_q_norm", (1, 64)), ("odd_k_norm", (1, 64)),
                    ("ffn_norm", (2, 1024)), ("ffn_conv_b", (2, 5632))]
SHARDED_SMALL = [("even_conv_w", (1, 3, 128)), ("odd_norm", (1, 256)), ("ffn_conv_w", (2, 3, 1408))]
BIG = ["even_w_in", "even_w_out", "odd_w_qkv", "odd_w_o", "ffn_w_up", "ffn_w_down"]
WEIGHT_ORDER = ["rel_bias", "even_norm", "even_w_in", "even_conv_w", "even_pool_w", "even_pool_scale", "even_w_out",
                "odd_norm", "odd_w_qkv", "odd_q_norm", "odd_k_norm", "odd_w_o", "ffn_norm", "ffn_w_up", "ffn_conv_w",
                "ffn_conv_b", "ffn_w_down"]


def kernel(x, rel_bias, even_norm, even_w_in, even_conv_w, even_pool_w, even_pool_scale, even_w_out, odd_norm, odd_w_qkv, odd_q_norm, odd_k_norm, odd_w_o, ffn_norm, ffn_w_up, ffn_conv_w, ffn_conv_b, ffn_w_down, loss_target, m_rel_bias, m_even_norm, m_even_w_in, m_even_conv_w, m_even_pool_w, m_even_pool_scale, m_even_w_out, m_odd_norm, m_odd_w_qkv, m_odd_q_norm, m_odd_k_norm, m_odd_w_o, m_ffn_norm, m_ffn_w_up, m_ffn_conv_w, m_ffn_conv_b, m_ffn_w_down, v_rel_bias, v_even_norm, v_even_w_in, v_even_conv_w, v_even_pool_w, v_even_pool_scale, v_even_w_out, v_odd_norm, v_odd_w_qkv, v_odd_q_norm, v_odd_k_norm, v_odd_w_o, v_ffn_norm, v_ffn_w_up, v_ffn_conv_w, v_ffn_conv_b, v_ffn_w_down):
    W = dict(rel_bias=rel_bias, even_norm=even_norm, even_w_in=even_w_in, even_conv_w=even_conv_w, even_pool_w=even_pool_w,
             even_pool_scale=even_pool_scale, even_w_out=even_w_out, odd_norm=odd_norm, odd_w_qkv=odd_w_qkv,
             odd_q_norm=odd_q_norm, odd_k_norm=odd_k_norm, odd_w_o=odd_w_o, ffn_norm=ffn_norm, ffn_w_up=ffn_w_up,
             ffn_conv_w=ffn_conv_w, ffn_conv_b=ffn_conv_b, ffn_w_down=ffn_w_down)
    M1 = dict(rel_bias=m_rel_bias, even_norm=m_even_norm, even_w_in=m_even_w_in, even_conv_w=m_even_conv_w,
              even_pool_w=m_even_pool_w, even_pool_scale=m_even_pool_scale, even_w_out=m_even_w_out, odd_norm=m_odd_norm,
              odd_w_qkv=m_odd_w_qkv, odd_q_norm=m_odd_q_norm, odd_k_norm=m_odd_k_norm, odd_w_o=m_odd_w_o,
              ffn_norm=m_ffn_norm, ffn_w_up=m_ffn_w_up, ffn_conv_w=m_ffn_conv_w, ffn_conv_b=m_ffn_conv_b,
              ffn_w_down=m_ffn_w_down)
    M2 = dict(rel_bias=v_rel_bias, even_norm=v_even_norm, even_w_in=v_even_w_in, even_conv_w=v_even_conv_w,
              even_pool_w=v_even_pool_w, even_pool_scale=v_even_pool_scale, even_w_out=v_even_w_out, odd_norm=v_odd_norm,
              odd_w_qkv=v_odd_w_qkv, odd_q_norm=v_odd_q_norm, odd_k_norm=v_odd_k_norm, odd_w_o=v_odd_w_o,
              ffn_norm=v_ffn_norm, ffn_w_up=v_ffn_w_up, ffn_conv_w=v_ffn_conv_w, ffn_conv_b=v_ffn_conv_b,
              ffn_w_down=v_ffn_w_down)
    mx, my, mc = _coords()
    chip = 2 * mx + my
    me = 4 * mx + 2 * my + mc
    place = jnp.stack([chip, mc, me]).astype(jnp.int32)
    xs, target = x[0], loss_target[0]

    def halves(w):
        return w.reshape((w.shape[0], 2, w.shape[-2] // 2, w.shape[-1]))

    small_rows = jnp.pad(_pack([even_conv_w, odd_norm, ffn_conv_w]), ((0, SUBLANES), (0, 0)))
    first = [cast_into_slot(halves(even_w_in), 0, place, "cast_w_in"), cast_into_slot(halves(even_w_out), 0, place, "cast_w_out"),
             cast_into_slot(small_rows.reshape(1, 2, small_rows.shape[0] // 2, LANES), 0, place, "small_into_slot", dtype=F32)]
    first_sems, first, token = gather_start(first, [[0, 1, 2]], "gather_start_first")
    even_norm_after_start = even_norm + token[0:1, 0:1]

    def later(a):
        return lax.optimization_barrier((a, token))[0]

    up_f32, down_f32 = halves(later(ffn_w_up)), halves(later(ffn_w_down))
    rest = [cast_into_slot(up_f32, 0, place, "cast_w_up0"), cast_into_slot(down_f32, 0, place, "cast_w_down0"),
            cast_into_slot(halves(later(odd_w_qkv)), 0, place, "cast_w_qkv"), cast_into_slot(halves(later(odd_w_o)), 0, place, "cast_w_o"),
            cast_into_slot(up_f32, 1, place, "cast_w_up1"), cast_into_slot(down_f32, 1, place, "cast_w_down1")]
    rest_sems, rest, rest_token = gather_start(rest, [[0], [1], [2, 3], [4], [5]], "gather_start_rest")
    group_arrays = [first, [rest[0]], [rest[1]], [rest[2], rest[3]], [rest[4]], [rest[5]]]
    group_sems = first_sems + rest_sems

    passing = {}

    def pass_on(group, tag, after, then):
        sems, arrays = gather_forward(group_arrays[group], group_sems[group], after, "gather_forward_" + tag)
        then, arrays = lax.optimization_barrier((then, arrays))
        passing[group] = (sems, arrays)
        return then

    def gathered(group, tag, after):
        sems, arrays = passing.pop(group)
        return gather_wait(arrays, sems, after, "gather_wait_" + tag)

    pool_w = cast_bf16(even_pool_w[0], "cast_pool_w")
    gqk = jnp.stack([jnp.tile(odd_q_norm[0], N_HEADS), jnp.tile(odd_k_norm[0], N_HEADS),
                     jnp.ones((D_MODEL,), F32)])[:, None, :]
    bias = bias_expand(later(rel_bias).T, "bias_expand").reshape(6, N_HEADS, ATT_BLOCK, 2 * ATT_BLOCK)
    xn0 = rmsnorm_fwd(xs, pass_on(0, "even", rest_token, even_norm_after_start), "even_norm")
    got = gathered(0, "even", xn0)
    w_in = got[0].reshape(N_CHIPS, 1, D_MODEL, EVEN_IN // N_CHIPS)
    w_out = got[1].reshape(1, 1, D_MODEL, D_MODEL)
    small = got[2].reshape(N_CHIPS, small_rows.shape[0], LANES)
    conv_w_full = small[:, 0:3].transpose(1, 0, 2).reshape(3, A_WIDTH)
    odd_norm_full = small[:, 8:10].reshape(1, D_MODEL)
    ffn_cw_full = small[:, 16:82].reshape(N_CHIPS, 2, 3, 2 * D_FF // N_CHIPS).transpose(1, 2, 0, 3).reshape(2, 3, 2 * D_FF)

    def ffn_fwd(l, xin, xn):
        up, u, act = up_glu_fwd(xn, w_up[l], ffn_cw_full[l], ffn_conv_b[l:l + 1], f"ffn{l}_up_glu")
        return act, (xin, xn, up, u, act)

    w_up, w_down = [None, None], [None, None]
    proj, mix = in_mixer_fwd(xn0, w_in, conv_w_full, pool_w, even_pool_scale, "even_in_mixer")
    x1, xn1 = mm_res_norm(pass_on(1, "up0", proj, mix), w_out, xs, ffn_norm[0:1], "even_out")
    w_up[0] = gathered(1, "up0", x1)[0].reshape(N_CHIPS, 1, D_MODEL, 2 * D_FF // N_CHIPS)
    xn1 = pass_on(3, "odd", x1, pass_on(2, "down0", x1, xn1))
    act0, ffn0 = ffn_fwd(0, x1, xn1)
    w_down[0] = gathered(2, "down0", act0)[0].reshape(1, 1, D_FF, D_MODEL)
    x2, xn2 = mm_res_norm(act0, w_down[0], x1, odd_norm_full, "ffn0_down")
    got = gathered(3, "odd", x2)
    xn2 = pass_on(5, "down1", x2, pass_on(4, "up1", x2, xn2))
    w_qkv = got[0].reshape(N_CHIPS, 1, D_MODEL, 3 * D_MODEL // N_CHIPS)
    w_o = got[1].reshape(1, 1, D_MODEL, D_MODEL)
    qkv, qkvn = qkv_qknorm_fwd(xn2, w_qkv, gqk, "odd_qkv_qknorm")
    att, lse = attn_fwd(qkvn, bias, "attn_fwd")
    x3, xn3 = mm_res_norm(att, w_o, x2, ffn_norm[1:2], "odd_out")
    w_up[1] = gathered(4, "up1", x3)[0].reshape(N_CHIPS, 1, D_MODEL, 2 * D_FF // N_CHIPS)
    act1, ffn1 = ffn_fwd(1, x3, xn3)
    w_down[1] = gathered(5, "down1", act1)[0].reshape(1, 1, D_FF, D_MODEL)
    dy, dyb, sq = mm_res_loss(act1, w_down[1], x3, target, "ffn1_down_loss")
    loss_part = (0.5 * jnp.sum(sq) * (1.0 / D_MODEL)).reshape(1, 1)

    def ffn_bwd(l, dy, dyb, saved):
        xin, xn, up, u, act = saved
        dw_down = mm_tn(act, dyb, f"ffn{l}_dw_down", J=1, tk=D_FF // 2, tm=1024)
        dact = mm_nt(dyb, w_down[l], f"ffn{l}_dact", tr=D_FF // 2, out_dtype=BF16, tm=1024)
        dup, dcw, dcb = glu_bwd(up, u, dact, ffn_cw_full[l], f"ffn{l}_glu_bwd")
        dw_up = mm_tn(xn, dup, f"ffn{l}_dw_up", J=N_CHIPS, tk=512, tm=1024, jb=2)
        dx, dxb, dg = mm_nt_norm_bwd(dup, w_up[l], xin, ffn_norm[l:l + 1], dy, f"ffn{l}_dx")
        return dx, dxb, (dw_down, dw_up, dcw, dcb, dg)

    def quarters(g):
        return g.reshape(N_CHIPS, 2, g.shape[0] * g.shape[1] // (2 * N_CHIPS), g.shape[-1])

    def reduce_start(grads, tag, then):
        sems, parts, zones = reduce_send([quarters(g) for g in grads], "reduce_send_" + tag)
        then, parts = lax.optimization_barrier((then, parts))
        return (sems, parts, zones), then

    dx3, dx3b, g_ffn1 = ffn_bwd(1, dy, dyb, ffn1)
    red_ffn1, (dx3, dx3b) = reduce_start([g_ffn1[1], g_ffn1[0]], "ffn1", (dx3, dx3b))
    dw_o = mm_tn(att, dx3b, "odd_dw_o", J=1, tk=512, tm=1024)
    datt = mm_nt(dx3b, w_o, "odd_datt", tr=D_MODEL, out_dtype=BF16)
    dq, dk, dv, dbias = attn_bwd(qkvn, att, datt, lse, bias, "attn_bwd")
    dqkv, dgqk = qknorm_bwd(qkv, dq, dk, dv, gqk, "odd_qknorm_bwd")
    dw_qkv = mm_tn(xn2, dqkv, "odd_dw_qkv", J=N_CHIPS, tk=512, tm=1024)
    red_odd, dqkv = reduce_start([dw_qkv, dw_o], "odd", dqkv)
    dx2, dx2b, dg_odd = mm_nt_norm_bwd(dqkv, w_qkv, x2, odd_norm_full, dx3, "odd_dx")
    dx1, dx1b, g_ffn0 = ffn_bwd(0, dx2, dx2b, ffn0)
    red_ffn0, (dx1, dx1b) = reduce_start([g_ffn0[1], g_ffn0[0]], "ffn0", (dx1, dx1b))
    dw_out = mm_tn(mix, dx1b, "even_dw_out", J=1, tk=512, tm=1024)
    dmix = mm_nt(dx1b, w_out, "even_dmix", tr=D_MODEL)
    dproj, dcw_even, dpw, dps = mixer_bwd(proj, dmix, conv_w_full, pool_w, even_pool_scale, "even_mixer_bwd")
    dw_in = mm_tn(xn0, dproj, "even_dw_in", J=N_CHIPS, tk=512, tm=1024)
    grad_x, _, dg_even = mm_nt_norm_bwd(dproj, w_in, xs, even_norm, dx1, "even_dx")
    d_rel = jnp.sum(bias_reduce(dbias.reshape(3, N_HEADS, 2 * ATT_BLOCK * ATT_BLOCK), "bias_reduce"), axis=0).T

    red_even, grad_x = reduce_start([dw_in, dw_out], "even", grad_x)

    dcw_sh = dcw_even.reshape(3, N_CHIPS, A_WIDTH // N_CHIPS).transpose(1, 0, 2)
    don_sh = dg_odd.reshape(N_CHIPS, D_MODEL // N_CHIPS)
    dfcw = jnp.stack([g_ffn0[2], g_ffn1[2]])
    dfcw_sh = dfcw.reshape(2, 3, N_CHIPS, 2 * D_FF // N_CHIPS).transpose(2, 0, 1, 3)
    rep_grads = [d_rel, dg_even, dpw[None], dps, _head_sum(dgqk[0]), _head_sum(dgqk[1]),
                 jnp.concatenate([g_ffn0[4], g_ffn1[4]], axis=0), jnp.concatenate([g_ffn0[3], g_ffn1[3]], axis=0)]
    rep_rows = _pack([loss_part] + rep_grads)
    n_loss = _n_rows(loss_part.shape)
    shard_rows = jnp.concatenate([_pack([dcw_sh[j], don_sh[j], dfcw_sh[j]]) for j in range(N_CHIPS)], axis=0)
    n_rep, n_shard = rep_rows.shape[0], shard_rows.shape[0] // N_CHIPS
    small_sems, small_rows, small_land = devices_start(jnp.concatenate([rep_rows, shard_rows], axis=0), "small_grads_start")
    grad_x, small_rows = lax.optimization_barrier((grad_x, small_rows))

    def reduce_end(red, tag, after):
        sems, parts, zones = red
        parts, zones = reduce_wait(parts, zones, sems, after, "reduce_wait_" + tag)
        return zones, parts

    z_ffn1, p_ffn1 = reduce_end(red_ffn1, "ffn1", grad_x)
    z_odd, p_odd = reduce_end(red_odd, "odd", grad_x)
    r_qkv = reduce_sum(z_odd[0], p_odd[0], place, "reduce_sum_w_qkv")
    r_o = reduce_sum(z_odd[1], p_odd[1], place, "reduce_sum_w_o")
    r_up = reduce_sum(z_ffn1[0], p_ffn1[0], place, "reduce_sum_w_up1", layer=1)
    r_down = reduce_sum(z_ffn1[1], p_ffn1[1], place, "reduce_sum_w_down1", layer=1)
    r_qkv, r_o, r_up, r_down = lax.optimization_barrier((r_qkv, r_o, r_up, r_down))
    z_ffn0, p_ffn0 = reduce_end(red_ffn0, "ffn0", r_down)
    r_up = reduce_sum(z_ffn0[0], p_ffn0[0], place, "reduce_sum_w_up0", into=r_up, layer=0)
    r_down = reduce_sum(z_ffn0[1], p_ffn0[1], place, "reduce_sum_w_down0", into=r_down, layer=0)
    later = ["odd_w_qkv", "odd_w_o", "ffn_w_up", "ffn_w_down"]
    joined = join_halves([r_qkv, r_o, r_up, r_down], "grads_join_late_layers")
    G = {nm: g.reshape(W[nm].shape) for nm, g in zip(later, joined)}

    D_, NM, NV = {}, {}, {}

    def update(nm):
        as3 = lambda a: a.reshape((-1,) + a.shape[-2:])
        outs = adamw(as3(W[nm]), as3(G[nm]), as3(M1[nm]), as3(M2[nm]), "adamw_" + nm)
        D_[nm], NM[nm], NV[nm], G[nm] = [o.reshape(W[nm].shape) for o in outs]

    def all_before(names):
        tied = lax.optimization_barrier([D_[nm] for nm in names])
        for nm, d in zip(names, tied):
            D_[nm] = d
        return tied[0]

    for nm in later:
        update(nm)
    z_even, p_even = reduce_end(red_even, "even", all_before(later))
    joined = join_halves([reduce_sum(z_even[0], p_even[0], place, "reduce_sum_w_in"),
                          reduce_sum(z_even[1], p_even[1], place, "reduce_sum_w_out")], "grads_join_first_layer")
    first = ["even_w_in", "even_w_out"]
    for nm, g in zip(first, joined):
        G[nm] = g.reshape(W[nm].shape)
        update(nm)
    small_rows, small_land = devices_wait(small_rows, small_land, small_sems, all_before(first), "small_grads_wait")
    small_sum = device_sum(small_land, small_rows, place[2:3], "small_grads_sum")
    mine = lax.dynamic_slice_in_dim(small_sum, n_rep + chip * n_shard, n_shard, axis=0)
    loss = small_sum[0, 0]
    g_small = jnp.concatenate([small_sum[n_loss:n_rep], mine], axis=0)
    small_names = [n for n, _ in REPLICATED_SMALL + SHARDED_SMALL]
    small_shapes = [s for _, s in REPLICATED_SMALL + SHARDED_SMALL]
    G.update(dict(zip(small_names, _unpack(g_small, small_shapes))))
    outs = adamw_small(*[[d[n] for n in small_names] for d in (W, G, M1, M2)], "adamw_small")
    for dst, o in zip((D_, NM, NV), outs):
        dst.update(dict(zip(small_names, o)))

    return (loss, grad_x[None], *[G[n] for n in WEIGHT_ORDER], *[D_[n] for n in WEIGHT_ORDER],
            *[NM[n] for n in WEIGHT_ORDER], *[NV[n] for n in WEIGHT_ORDER])


def _head_sum(dg):
    return jnp.sum(dg.reshape(N_HEADS, HEAD_DIM), axis=0, keepdims=True)
```

```python
import functools
import math

import numpy as np
import jax
import jax.numpy as jnp
from jax import lax
from jax.experimental import pallas as pl
from jax.experimental.pallas import tpu as pltpu

F32 = jnp.float32
BF16 = jnp.bfloat16

D_MODEL = 1024
N_HEADS = 16
HEAD_DIM = 64
A_WIDTH = 512
POOL_WINDOWS = (2, 4, 8, 16)
POOL_GROUP = 128
EVEN_IN = 2048
D_FF = 2816
DILATED_PAIRS = ((128, 1), (512, 4), (2048, 16))
ATT_BLOCK = 128
N_REL_BUCKETS = 32
REL_MAX_DISTANCE = 2048
EPS = 1e-6
MASK_VALUE = -1e30
ADAM_LR, ADAM_B1, ADAM_B2, ADAM_EPS, ADAM_WD, ADAM_STEP = 0.001, 0.9, 0.999, 1e-08, 0.01, 10

VMEM_LIMIT_BYTES = 48 * 1024 * 1024
ELEMENTWISE_BLOCK_BYTES = 2 * 1024 * 1024
N_CHIPS = 4
N_DEV = 8
MESH = pl.DeviceIdType.MESH


def _params(*sem):
    return pltpu.CompilerParams(dimension_semantics=sem if sem else None, vmem_limit_bytes=VMEM_LIMIT_BYTES)


def _sds(shape, dtype):
    return jax.ShapeDtypeStruct(tuple(shape), dtype)


def _sds_hbm(shape, dtype):
    return pltpu.HBM(tuple(shape), dtype)


def cast_bf16(x, name, tr=None):
    lead, (R, C) = x.shape[:-2], x.shape[-2:]
    n = int(np.prod(lead)) if lead else 1
    x3 = x.reshape((n, R, C))
    tr = tr or R

    def body(x_ref, o_ref):
        o_ref[...] = x_ref[...].astype(BF16)

    out = pl.pallas_call(
        body, name=name, grid=(n, R // tr),
        in_specs=[pl.BlockSpec((None, tr, C), lambda i, r: (i, r, 0))],
        out_specs=pl.BlockSpec((None, tr, C), lambda i, r: (i, r, 0)),
        out_shape=_sds((n, R, C), BF16), compiler_params=_params("parallel", "parallel"),
    )(x3)
    return out.reshape(lead + (R, C))


def rmsnorm_fwd(x, g, name, ts=512):
    S, Dm = x.shape

    def body(x_ref, g_ref, o_ref):
        xv = x_ref[...]
        r = lax.rsqrt(jnp.mean(xv * xv, axis=-1, keepdims=True) + EPS)
        o_ref[...] = ((xv * r) * g_ref[...]).astype(BF16)

    return pl.pallas_call(
        body, name=name, grid=(S // ts,),
        in_specs=[pl.BlockSpec((ts, Dm), lambda i: (i, 0)), pl.BlockSpec((1, Dm), lambda i: (0, 0))],
        out_specs=pl.BlockSpec((ts, Dm), lambda i: (i, 0)),
        out_shape=_sds((S, Dm), BF16), compiler_params=_params("parallel"),
    )(x, g)


def mm_res_norm(a, w, res, gain, name, tm=1024):
    M, K = a.shape
    Dm = w.shape[-1]

    def body(a_ref, w_ref, r_ref, g_ref, y_ref, yn_ref):
        y = r_ref[...] + jnp.dot(a_ref[...], w_ref[...], preferred_element_type=F32)
        y_ref[...] = y
        r = lax.rsqrt(jnp.mean(y * y, axis=-1, keepdims=True) + EPS)
        yn_ref[...] = ((y * r) * g_ref[...]).astype(BF16)

    row = pl.BlockSpec((tm, Dm), lambda m: (m, 0))
    return pl.pallas_call(
        body, name=name, grid=(M // tm,),
        in_specs=[pl.BlockSpec((tm, K), lambda m: (m, 0)),
                  pl.BlockSpec((None, None, K, Dm), lambda m: (0, 0, 0, 0), pipeline_mode=pl.Buffered(1)),
                  row, pl.BlockSpec((1, Dm), lambda m: (0, 0))],
        out_specs=[row, row], out_shape=[_sds((M, Dm), F32), _sds((M, Dm), BF16)],
        compiler_params=_params("parallel"),
    )(a, w, res, gain)


def mm_res_loss(a, w, res, target, name, tm=512):
    M, K = a.shape
    Dm = w.shape[-1]

    def body(a_ref, w_ref, r_ref, t_ref, d_ref, db_ref, s_ref):
        e = (r_ref[...] + jnp.dot(a_ref[...], w_ref[...], preferred_element_type=F32)) - t_ref[...]
        d = e * (1.0 / Dm)
        d_ref[...] = d
        db_ref[...] = d.astype(BF16)
        part = jnp.sum(e * e, axis=0, keepdims=True)

        @pl.when(pl.program_id(0) == 0)
        def _():
            s_ref[...] = part

        @pl.when(pl.program_id(0) > 0)
        def _():
            s_ref[...] += part

    row = pl.BlockSpec((tm, Dm), lambda m: (m, 0))
    return pl.pallas_call(
        body, name=name, grid=(M // tm,),
        in_specs=[pl.BlockSpec((tm, K), lambda m: (m, 0)),
                  pl.BlockSpec((None, None, K, Dm), lambda m: (0, 0, 0, 0), pipeline_mode=pl.Buffered(1)), row, row],
        out_specs=[row, row, pl.BlockSpec((1, Dm), lambda m: (0, 0))],
        out_shape=[_sds((M, Dm), F32), _sds((M, Dm), BF16), _sds((1, Dm), F32)],
        compiler_params=_params("arbitrary"),
    )(a, w, res, target)


def mm_nt(dy, w, name, tr, layer=0, out_dtype=F32, tm=512):
    M = dy.shape[0]
    J, _, R, Ns = w.shape
    dims = (((1,), (1,)), ((), ()))

    def body(dy_ref, w_ref, o_ref):
        acc = None
        for j in range(J):
            p = lax.dot_general(dy_ref[:, j * Ns:(j + 1) * Ns], w_ref[j], dims, preferred_element_type=F32)
            acc = p if acc is None else acc + p
        o_ref[...] = acc.astype(o_ref.dtype)

    return pl.pallas_call(
        body, name=name, grid=(R // tr, M // tm),
        in_specs=[pl.BlockSpec((tm, J * Ns), lambda r, m: (m, 0)),
                  pl.BlockSpec((J, None, tr, Ns), lambda r, m: (0, layer, r, 0))],
        out_specs=pl.BlockSpec((tm, tr), lambda r, m: (m, r)),
        out_shape=_sds((M, R), out_dtype),
        compiler_params=_params("parallel", "parallel"),
    )(dy, w)


def mm_nt_norm_bwd(dy, w, x, g, dres, name, layer=0, tm=512):
    M = dy.shape[0]
    J, _, Dm, Ns = w.shape
    dims = (((1,), (1,)), ((), ()))

    def body(dy_ref, w_ref, x_ref, g_ref, r_ref, dx_ref, dxb_ref, dg_ref):
        dxn = None
        for j in range(J):
            p = lax.dot_general(dy_ref[:, j * Ns:(j + 1) * Ns], w_ref[j], dims, preferred_element_type=F32)
            dxn = p if dxn is None else dxn + p
        xv = x_ref[...]
        r = lax.rsqrt(jnp.mean(xv * xv, axis=-1, keepdims=True) + EPS)
        gx = dxn * g_ref[...]
        dot = jnp.sum(gx * xv, axis=-1, keepdims=True)
        dx = r_ref[...] + r * gx - xv * ((r * r * r) * (dot * (1.0 / Dm)))
        dx_ref[...] = dx
        dxb_ref[...] = dx.astype(BF16)
        part = jnp.sum(dxn * (xv * r), axis=0, keepdims=True)

        @pl.when(pl.program_id(0) == 0)
        def _():
            dg_ref[...] = part

        @pl.when(pl.program_id(0) > 0)
        def _():
            dg_ref[...] += part

    row = pl.BlockSpec((tm, Dm), lambda m: (m, 0))
    vec = pl.BlockSpec((1, Dm), lambda m: (0, 0))
    return pl.pallas_call(
        body, name=name, grid=(M // tm,),
        in_specs=[pl.BlockSpec((tm, J * Ns), lambda m: (m, 0)),
                  pl.BlockSpec((J, None, Dm, Ns), lambda m: (0, layer, 0, 0), pipeline_mode=pl.Buffered(1)), row, vec, row],
        out_specs=[row, row, vec],
        out_shape=[_sds((M, Dm), F32), _sds((M, Dm), BF16), _sds((1, Dm), F32)],
        compiler_params=_params("arbitrary"),
    )(dy, w, x, g, dres)


def mm_tn(a, dy, name, J, tk, tm=512, jb=None):
    M, K = a.shape
    jb = jb or J
    Ns = dy.shape[1] // J
    N = jb * Ns
    n_m = M // tm
    dims = (((0,), (0,)), ((), ()))

    def body(a_ref, dy_ref, o_ref, acc_ref):
        p = lax.dot_general(a_ref[...], dy_ref[...], dims, preferred_element_type=F32)
        m = pl.program_id(2)

        @pl.when(m == 0)
        def _():
            acc_ref[...] = p

        @pl.when(m > 0)
        def _():
            acc_ref[...] += p

        @pl.when(m == n_m - 1)
        def _():
            for j in range(jb):
                o_ref[j] = acc_ref[:, j * Ns:(j + 1) * Ns].astype(BF16)

    return pl.pallas_call(
        body, name=name, grid=(J // jb, K // tk, n_m),
        in_specs=[pl.BlockSpec((tm, tk), lambda g, k, m: (m, k)), pl.BlockSpec((tm, N), lambda g, k, m: (m, g))],
        out_specs=pl.BlockSpec((jb, tk, Ns), lambda g, k, m: (g, k, 0)),
        out_shape=_sds((J, K, Ns), BF16), scratch_shapes=[pltpu.VMEM((tk, N), F32)],
        compiler_params=_params("parallel", "parallel", "arbitrary"),
    )(a, dy)


HALO = 16


def _shift_down(x, s):
    return pltpu.roll(x, s, 0)


def _shift_up(x, s):
    return pltpu.roll(x, x.shape[0] - s, 0)


def _conv3(z, cw):
    return (_shift_down(z, 2) * cw[0:1] + _shift_down(z, 1) * cw[1:2]) + z * cw[2:3]


def _window_count(first_row, n, k):
    t = first_row + lax.broadcasted_iota(jnp.int32, (n, 1), 0)
    return jnp.clip(t + 1, 1, k).astype(F32)


def in_mixer_fwd(xn, w_in, conv_w, pool_w, pool_scale, name, ts=512):
    S, K = xn.shape
    n = ts + HALO

    def body(xm_ref, xb_ref, w_ref, cw_ref, pw_ref, ps_ref, p_ref, o_ref):
        i = pl.program_id(0)
        before = jnp.where(i > 0, xb_ref[...], jnp.zeros_like(xb_ref))
        rows = jnp.concatenate([before, xm_ref[...]], axis=0)
        h, gb, gc, pin = [jnp.dot(rows, w_ref[j], preferred_element_type=F32) for j in range(N_CHIPS)]
        for j, part in enumerate((h, gb, gc, pin)):
            p_ref[:, j * A_WIDTH:(j + 1) * A_WIDTH] = part[HALO:]
        cz = _conv3(gc * h, cw_ref[...])
        o_ref[:, 0:A_WIDTH] = (gb[HALO:] * cz[HALO:]).astype(BF16)
        for g, k in enumerate(POOL_WINDOWS):
            p = pin[:, g * POOL_GROUP:(g + 1) * POOL_GROUP]
            w = p
            s = 1
            while s < k:
                w = w + _shift_down(w, s)
                s *= 2
            pooled = w / _window_count(i * ts - HALO, n, k) - p
            yb = jnp.dot(pooled[HALO:].astype(BF16), pw_ref[g], preferred_element_type=F32)
            yb = yb * ps_ref[:, g * POOL_GROUP:(g + 1) * POOL_GROUP]
            o_ref[:, A_WIDTH + g * POOL_GROUP:A_WIDTH + (g + 1) * POOL_GROUP] = yb.astype(BF16)

    hb = ts // HALO
    return pl.pallas_call(
        body, name=name, grid=(S // ts,),
        in_specs=[
            pl.BlockSpec((ts, K), lambda i: (i, 0)),
            pl.BlockSpec((HALO, K), lambda i: (jnp.maximum(i * hb - 1, 0), 0)),
            pl.BlockSpec((N_CHIPS, None, K, A_WIDTH), lambda i: (0, 0, 0, 0), pipeline_mode=pl.Buffered(1)),
            pl.BlockSpec((3, A_WIDTH), lambda i: (0, 0)),
            pl.BlockSpec((4, POOL_GROUP, POOL_GROUP), lambda i: (0, 0, 0)),
            pl.BlockSpec((1, 4 * POOL_GROUP), lambda i: (0, 0)),
        ],
        out_specs=[pl.BlockSpec((ts, EVEN_IN), lambda i: (i, 0)), pl.BlockSpec((ts, D_MODEL), lambda i: (i, 0))],
        out_shape=[_sds((S, EVEN_IN), F32), _sds((S, D_MODEL), BF16)], compiler_params=_params("parallel"),
    )(xn, xn, w_in, conv_w, pool_w, pool_scale)


def mixer_bwd(proj, dmix, conv_w, pool_w, pool_scale, name, ts=256):
    S = proj.shape[0]
    n = ts + 2 * HALO
    nt = S // ts
    tn_dims = (((0,), (0,)), ((), ()))
    nt_dims = (((1,), (1,)), ((), ()))

    def body(pm_ref, pb_ref, pa_ref, dm_ref, da_ref, cw_ref, pw_ref, ps_ref, o_ref, dcw_ref, dpw_ref, dps_ref):
        i = pl.program_id(0)
        last = i == nt - 1
        before = jnp.where(i > 0, pb_ref[...], 0.0)
        after = jnp.where(last, 0.0, pa_ref[...])
        ext = jnp.concatenate([before, pm_ref[...], after], axis=0)
        dafter = jnp.where(last, 0.0, da_ref[...])
        dext = jnp.concatenate([jnp.zeros((HALO, D_MODEL), F32), dm_ref[...], dafter], axis=0)
        cw = cw_ref[...]
        main = slice(HALO, HALO + ts)

        @pl.when(i == 0)
        def _():
            dcw_ref[...] = jnp.zeros_like(dcw_ref)
            dpw_ref[...] = jnp.zeros_like(dpw_ref)
            dps_ref[...] = jnp.zeros_like(dps_ref)

        h, gb, gc = ext[:, 0:A_WIDTH], ext[:, A_WIDTH:2 * A_WIDTH], ext[:, 2 * A_WIDTH:3 * A_WIDTH]
        z = gc * h
        z1, z2 = _shift_down(z, 1), _shift_down(z, 2)
        cz = (z2 * cw[0:1] + z1 * cw[1:2]) + z * cw[2:3]
        dya = dext[:, 0:A_WIDTH]
        dcz = dya * gb
        dz = dcz * cw[2:3] + _shift_up(dcz, 1) * cw[1:2] + _shift_up(dcz, 2) * cw[0:1]
        o_ref[:, 0:A_WIDTH] = (dz * gc)[main].astype(BF16)
        o_ref[:, A_WIDTH:2 * A_WIDTH] = (dya * cz)[main].astype(BF16)
        o_ref[:, 2 * A_WIDTH:3 * A_WIDTH] = (dz * h)[main].astype(BF16)
        dczm = dcz[main]
        dcw_ref[0:1, :] += jnp.sum(dczm * z2[main], axis=0, keepdims=True)
        dcw_ref[1:2, :] += jnp.sum(dczm * z1[main], axis=0, keepdims=True)
        dcw_ref[2:3, :] += jnp.sum(dczm * z[main], axis=0, keepdims=True)

        for g, k in enumerate(POOL_WINDOWS):
            lo = 3 * A_WIDTH + g * POOL_GROUP
            cols = slice(g * POOL_GROUP, (g + 1) * POOL_GROUP)
            p = ext[:, lo:lo + POOL_GROUP]
            w = p
            s = 1
            while s < k:
                w = w + _shift_down(w, s)
                s *= 2
            cnt = _window_count(i * ts - HALO, n, k)
            pooled = (w / cnt - p)[main].astype(BF16)
            dyb = dext[:, A_WIDTH + g * POOL_GROUP:A_WIDTH + (g + 1) * POOL_GROUP]
            e = dyb * ps_ref[:, cols]
            pre = jnp.dot(pooled, pw_ref[g], preferred_element_type=F32)
            dps_ref[:, cols] += jnp.sum(dyb[main] * pre, axis=0, keepdims=True)
            dpw_ref[g] += lax.dot_general(pooled, e[main].astype(BF16), tn_dims, preferred_element_type=F32)
            dpooled = lax.dot_general(e.astype(BF16), pw_ref[g], nt_dims, preferred_element_type=F32)
            q = dpooled / cnt
            a = q
            s = 1
            while s < k:
                a = a + _shift_up(a, s)
                s *= 2
            o_ref[:, lo:lo + POOL_GROUP] = (a - dpooled)[main].astype(BF16)

    hb = ts // HALO
    nh = S // HALO
    before_map = lambda i: (jnp.maximum(i * hb - 1, 0), 0)
    after_map = lambda i: (jnp.minimum((i + 1) * hb, nh - 1), 0)
    full = lambda *shape: pl.BlockSpec(shape, lambda i: (0,) * len(shape))
    return pl.pallas_call(
        body, name=name, grid=(nt,),
        in_specs=[
            pl.BlockSpec((ts, EVEN_IN), lambda i: (i, 0)),
            pl.BlockSpec((HALO, EVEN_IN), before_map),
            pl.BlockSpec((HALO, EVEN_IN), after_map),
            pl.BlockSpec((ts, D_MODEL), lambda i: (i, 0)),
            pl.BlockSpec((HALO, D_MODEL), after_map),
            full(3, A_WIDTH), full(4, POOL_GROUP, POOL_GROUP), full(1, 4 * POOL_GROUP),
        ],
        out_specs=[pl.BlockSpec((ts, EVEN_IN), lambda i: (i, 0)), full(3, A_WIDTH), full(4, POOL_GROUP, POOL_GROUP),
                   full(1, 4 * POOL_GROUP)],
        out_shape=[_sds((S, EVEN_IN), BF16), _sds((3, A_WIDTH), F32), _sds((4, POOL_GROUP, POOL_GROUP), F32),
                   _sds((1, 4 * POOL_GROUP), F32)],
        compiler_params=_params("arbitrary"),
    )(proj, proj, proj, dmix, dmix, conv_w, pool_w, pool_scale)


FFN_HALO = 16
FFN_TC = 1408


GLU_CHUNKS = ((0, 512), (512, 512), (1024, 384))


def up_glu_fwd(xn, w_up, conv_w, conv_b, name, tm=512):
    S, K = xn.shape
    nc = D_FF // FFN_TC

    def body(xm_ref, xb_ref, wg_ref, wu_ref, cwg_ref, cwu_ref, cbg_ref, cbu_ref, pg_ref, pu_ref, ug_ref, uu_ref, o_ref):
        before = jnp.where(pl.program_id(1) > 0, xb_ref[...], jnp.zeros_like(xb_ref))
        rows = jnp.concatenate([before, xm_ref[...]], axis=0)
        for lo, width in GLU_CHUNKS:
            cols = slice(lo, lo + width)
            pre_g = jnp.dot(rows, wg_ref[:, cols], preferred_element_type=F32)
            pre_u = jnp.dot(rows, wu_ref[:, cols], preferred_element_type=F32)
            gate = _conv3(pre_g, cwg_ref[:, cols])[FFN_HALO:] + cbg_ref[:, cols]
            upv = _conv3(pre_u, cwu_ref[:, cols])[FFN_HALO:] + cbu_ref[:, cols]
            pg_ref[:, cols] = pre_g[FFN_HALO:].astype(BF16)
            pu_ref[:, cols] = pre_u[FFN_HALO:].astype(BF16)
            ug_ref[:, cols] = gate.astype(BF16)
            uu_ref[:, cols] = upv.astype(BF16)
            o_ref[:, cols] = ((gate * (1.0 / (1.0 + jnp.exp(-gate)))) * upv).astype(BF16)

    hb = tm // FFN_HALO
    wspec = lambda off: pl.BlockSpec((None, None, K, FFN_TC), lambda j, m: (j + off, 0, 0, 0))
    cw = lambda off: pl.BlockSpec((3, FFN_TC), lambda j, m: (0, j + off))
    cb = lambda off: pl.BlockSpec((1, FFN_TC), lambda j, m: (0, j + off))
    out = pl.BlockSpec((tm, FFN_TC), lambda j, m: (m, j))
    pg, pu, ug, uu, act = pl.pallas_call(
        body, name=name, grid=(nc, S // tm),
        in_specs=[pl.BlockSpec((tm, K), lambda j, m: (m, 0)),
                  pl.BlockSpec((FFN_HALO, K), lambda j, m: (jnp.maximum(m * hb - 1, 0), 0)),
                  wspec(0), wspec(nc), cw(0), cw(nc), cb(0), cb(nc)],
        out_specs=[out] * 5, out_shape=[_sds((S, D_FF), BF16)] * 5,
        compiler_params=_params("parallel", "parallel"),
    )(xn, xn, w_up, w_up, conv_w, conv_w, conv_b, conv_b)
    return (pg, pu), (ug, uu), act


def glu_bwd(up, u, da, conv_w, name, ts=256):
    S = up[0].shape[0]
    nc = D_FF // FFN_TC
    nt = S // ts
    W = 2 * D_FF

    def body(xg_ref, xu_ref, gm_ref, ga_ref, um_ref, ua_ref, dm_ref, da_ref, cw_ref, dx_ref, dcw_ref, dcb_ref):
        i = pl.program_id(0)
        last = i == nt - 1

        @pl.when(i == 0)
        def _():
            dcw_ref[...] = jnp.zeros_like(dcw_ref)
            dcb_ref[...] = jnp.zeros_like(dcb_ref)

        def rows(m_ref, a_ref, cols):
            return jnp.concatenate([m_ref[:, cols], a_ref[:, cols]], axis=0).astype(F32)

        def back(d, x, cols):
            cw = cw_ref[:, cols]
            d1, d2 = _shift_up(d, 1), _shift_up(d, 2)
            dx_ref[:, cols] = ((d * cw[2:3] + d1 * cw[1:2]) + d2 * cw[0:1])[:ts].astype(BF16)
            dcb_ref[:, cols] += jnp.sum(d[:ts], axis=0, keepdims=True)
            dcw_ref[0:1, cols] += jnp.sum(d2[:ts] * x, axis=0, keepdims=True)
            dcw_ref[1:2, cols] += jnp.sum(d1[:ts] * x, axis=0, keepdims=True)
            dcw_ref[2:3, cols] += jnp.sum(d[:ts] * x, axis=0, keepdims=True)

        for c in range(nc):
            cols = slice(c * FFN_TC, (c + 1) * FFN_TC)
            ug, uu = rows(gm_ref, ga_ref, cols), rows(um_ref, ua_ref, cols)
            dae = rows(dm_ref, da_ref, cols)
            dae = jnp.where(last & (lax.broadcasted_iota(jnp.int32, dae.shape, 0) >= ts), 0.0, dae)
            sg = 1.0 / (1.0 + jnp.exp(-ug))
            duu = dae * (ug * sg)
            dug = (dae * uu) * (sg * (1.0 + ug * (1.0 - sg)))
            back(dug, xg_ref[:, cols].astype(F32), cols)
            back(duu, xu_ref[:, cols].astype(F32), slice(D_FF + c * FFN_TC, D_FF + (c + 1) * FFN_TC))

    hb = ts // FFN_HALO
    nh = S // FFN_HALO
    after_map = lambda i: (jnp.minimum((i + 1) * hb, nh - 1), 0)
    main = pl.BlockSpec((ts, D_FF), lambda i: (i, 0))
    after = pl.BlockSpec((FFN_HALO, D_FF), after_map)
    return pl.pallas_call(
        body, name=name, grid=(nt,),
        in_specs=[main, main, main, after, main, after, main, after, pl.BlockSpec((3, W), lambda i: (0, 0))],
        out_specs=[pl.BlockSpec((ts, W), lambda i: (i, 0)), pl.BlockSpec((3, W), lambda i: (0, 0)),
                   pl.BlockSpec((1, W), lambda i: (0, 0))],
        out_shape=[_sds((S, W), BF16), _sds((3, W), F32), _sds((1, W), F32)],
        compiler_params=_params("arbitrary"),
    )(up[0], up[1], u[0], u[0], u[1], u[1], da, da, conv_w)


MEAN_GROUP = 256


def _head_mean_matrix():
    h = np.arange(MEAN_GROUP) // HEAD_DIM
    return jnp.asarray((h[:, None] == h[None, :]).astype(np.float32) / HEAD_DIM, dtype=BF16)


def _head_mean(v, gm):
    vb = v.astype(BF16)
    return jnp.concatenate([jnp.dot(vb[:, c:c + MEAN_GROUP], gm, preferred_element_type=F32)
                            for c in range(0, v.shape[1], MEAN_GROUP)], axis=1)


def qkv_qknorm_fwd(xn, w_qkv, gqk, name, tm=1024):
    S, K = xn.shape
    J, _, _, Ns = w_qkv.shape
    gains = gqk.reshape(1, 3 * D_MODEL)

    def body(x_ref, w_ref, g_ref, gm_ref, raw_ref, o_ref):
        first_col = pl.program_id(0) * Ns
        acc = jnp.dot(x_ref[...], w_ref[...], preferred_element_type=F32)
        raw_ref[...] = acc
        gm = gm_ref[...]
        for c in range(0, Ns, MEAN_GROUP):
            cols = slice(c, c + MEAN_GROUP)
            x = acc[:, cols]
            mean = jnp.dot((x * x).astype(BF16), gm, preferred_element_type=F32)
            normed = (x * lax.rsqrt(mean + EPS)) * g_ref[:, cols]
            o_ref[:, cols] = jnp.where(first_col + c >= 2 * D_MODEL, x, normed).astype(BF16)

    return pl.pallas_call(
        body, name=name, grid=(J, S // tm),
        in_specs=[pl.BlockSpec((tm, K), lambda j, m: (m, 0)), pl.BlockSpec((None, None, K, Ns), lambda j, m: (j, 0, 0, 0)),
                  pl.BlockSpec((1, Ns), lambda j, m: (0, j)), pl.BlockSpec((MEAN_GROUP, MEAN_GROUP), lambda j, m: (0, 0))],
        out_specs=[pl.BlockSpec((tm, Ns), lambda j, m: (m, j))] * 2,
        out_shape=[_sds((S, J * Ns), F32), _sds((S, J * Ns), BF16)], compiler_params=_params("parallel", "parallel"),
    )(xn, w_qkv, gains, _head_mean_matrix())


def qknorm_bwd(qkv, dq, dk, dv, gqk, name, ts=256):
    S = qkv.shape[0]

    def body(x_ref, dq_ref, dk_ref, dv_ref, g_ref, gm_ref, o_ref, dg_ref):
        @pl.when(pl.program_id(0) == 0)
        def _():
            dg_ref[...] = jnp.zeros_like(dg_ref)

        gm = gm_ref[...]
        for part, d_ref in enumerate((dq_ref, dk_ref)):
            cols = slice(part * D_MODEL, (part + 1) * D_MODEL)
            x = x_ref[:, cols]
            d = d_ref[...]
            r = lax.rsqrt(_head_mean(x * x, gm) + EPS)
            gx = d * g_ref[part]
            o_ref[:, cols] = (r * gx - x * ((r * r * r) * _head_mean(gx * x, gm))).astype(BF16)
            dg_ref[part] += jnp.sum(d * (x * r), axis=0, keepdims=True)
        o_ref[:, 2 * D_MODEL:] = dv_ref[...].astype(BF16)

    row = pl.BlockSpec((ts, D_MODEL), lambda i: (i, 0))
    wide = pl.BlockSpec((ts, 3 * D_MODEL), lambda i: (i, 0))
    gains = pl.BlockSpec((3, 1, D_MODEL), lambda i: (0, 0, 0))
    return pl.pallas_call(
        body, name=name, grid=(S // ts,),
        in_specs=[wide, row, row, row, gains, pl.BlockSpec((MEAN_GROUP, MEAN_GROUP), lambda i: (0, 0))],
        out_specs=[wide, gains],
        out_shape=[_sds((S, 3 * D_MODEL), BF16), _sds((3, 1, D_MODEL), F32)],
        compiler_params=_params("arbitrary"),
    )(qkv, dq, dk, dv, gqk, _head_mean_matrix())


RESIDUES = 16


def _block_order(dil):
    runs = RESIDUES // dil
    slot = np.arange(ATT_BLOCK)
    return (slot % (ATT_BLOCK // runs)) * runs + slot // (ATT_BLOCK // runs)


def _bucket_tables():
    n = ATT_BLOCK
    max_exact = N_REL_BUCKETS // 2
    buckets, valids = [], []
    for _, dil in DILATED_PAIRS:
        order = _block_order(dil)
        a = order[:, None]
        c = np.concatenate([order, n + order])[None, :]
        first_half = (np.arange(2 * n) < n)[None, :]
        rel = a + n - c
        band = (rel >= 0) & (rel <= n)
        dist = np.clip(rel, 0, n) * dil
        dd = np.maximum(dist, 1).astype(np.float32)
        large = max_exact + (np.log(dd / np.float32(max_exact)) / np.float32(math.log(REL_MAX_DISTANCE / max_exact))
                             * np.float32(N_REL_BUCKETS - max_exact)).astype(np.int32)
        large = np.minimum(large, N_REL_BUCKETS - 1)
        buckets.append(np.where(dist < max_exact, dist, large).reshape(1, -1))
        valids.append(np.stack([(band & ~first_half).reshape(1, -1), band.reshape(1, -1)]))
    return np.stack(buckets).astype(np.int32), np.stack(valids).astype(np.int32)


BIAS_CHUNK = 8192


def _split3(x):
    a = x.astype(BF16)
    r = x - a.astype(F32)
    b = r.astype(BF16)
    c = (r - b.astype(F32)).astype(BF16)
    return a, b, c


def bias_expand(rel_bias_t, name):
    bucket, valid = _bucket_tables()
    nq = bucket.shape[-1]

    def body(t_ref, b_ref, v_ref, o_ref):
        onehot = (lax.broadcasted_iota(jnp.int32, (N_REL_BUCKETS, BIAS_CHUNK), 0) == b_ref[...]).astype(BF16)
        acc = None
        for term in _split3(t_ref[...]):
            p = jnp.dot(term, onehot, preferred_element_type=F32)
            acc = p if acc is None else acc + p
        o_ref[...] = jnp.where(v_ref[...] > 0, acc, MASK_VALUE)

    return pl.pallas_call(
        body, name=name, grid=(3, 2, nq // BIAS_CHUNK),
        in_specs=[pl.BlockSpec((N_HEADS, N_REL_BUCKETS), lambda b, v, c: (0, 0)),
                  pl.BlockSpec((None, 1, BIAS_CHUNK), lambda b, v, c: (b, 0, c)),
                  pl.BlockSpec((None, None, 1, BIAS_CHUNK), lambda b, v, c: (b, v, 0, c))],
        out_specs=pl.BlockSpec((None, None, N_HEADS, BIAS_CHUNK), lambda b, v, c: (b, v, 0, c)),
        out_shape=_sds((3, 2, N_HEADS, nq), F32), compiler_params=_params("parallel", "parallel", "parallel"),
    )(rel_bias_t, jnp.asarray(bucket), jnp.asarray(valid))


def bias_reduce(dbias, name):
    bucket, _ = _bucket_tables()
    nq = bucket.shape[-1]
    dims = (((1,), (1,)), ((), ()))

    def body(d_ref, b_ref, o_ref):
        onehot = (lax.broadcasted_iota(jnp.int32, (N_REL_BUCKETS, BIAS_CHUNK), 0) == b_ref[...]).astype(BF16)
        acc = None
        for term in _split3(d_ref[...]):
            p = lax.dot_general(term, onehot, dims, preferred_element_type=F32)
            acc = p if acc is None else acc + p

        @pl.when(pl.program_id(1) == 0)
        def _():
            o_ref[...] = acc

        @pl.when(pl.program_id(1) > 0)
        def _():
            o_ref[...] += acc

    return pl.pallas_call(
        body, name=name, grid=(3, nq // BIAS_CHUNK),
        in_specs=[pl.BlockSpec((None, N_HEADS, BIAS_CHUNK), lambda b, c: (b, 0, c)),
                  pl.BlockSpec((None, 1, BIAS_CHUNK), lambda b, c: (b, 0, c))],
        out_specs=pl.BlockSpec((None, N_HEADS, N_REL_BUCKETS), lambda b, c: (b, 0, 0)),
        out_shape=_sds((3, N_HEADS, N_REL_BUCKETS), F32), compiler_params=_params("parallel", "arbitrary"),
    )(dbias, jnp.asarray(bucket))


PAIR = 2 * HEAD_DIM
N_PAIRS = N_HEADS // 2
_NT = (((1,), (1,)), ((), ()))
_TN = (((0,), (0,)), ((), ()))


def _low_lanes(shape):
    return lax.broadcasted_iota(jnp.int32, shape, 1) < HEAD_DIM


ATTN_VMEM_LIMIT_BYTES = 56 * 1024 * 1024
BRANCH_ORDER = (2, 1, 0)


def _regroup(dst, src, L16):
    for r in range(RESIDUES):
        dst[pl.ds(r * L16, L16), :] = src[pl.ds(r, L16, stride=RESIDUES), :]


def _ungroup(dst, src, L16):
    for r in range(RESIDUES):
        dst[pl.ds(r, L16, stride=RESIDUES), :] = src[pl.ds(r * L16, L16), :]


def _branch_geometry(branch, S):
    dil = DILATED_PAIRS[branch][1]
    runs = RESIDUES // dil
    return dil, runs, ATT_BLOCK // runs, S // dil // ATT_BLOCK


def _block_rows(it, branch, S):
    dil, runs, run_len, n_blocks = _branch_geometry(branch, S)
    L16 = S // RESIDUES
    r, b = it // n_blocks, it % n_blocks
    prev = jnp.maximum(b - 1, 0)
    cur_rows = [pl.multiple_of((j * dil + r) * L16 + run_len * b, 8) for j in range(runs)]
    prev_rows = [pl.multiple_of((j * dil + r) * L16 + run_len * prev, 8) for j in range(runs)]
    return cur_rows, prev_rows, jnp.minimum(b, 1)


def _load_block(ref, rows, run_len):
    parts = [ref[pl.ds(o, run_len), :] for o in rows]
    return parts[0] if len(parts) == 1 else jnp.concatenate(parts, axis=0)


def _store_block(ref, rows, run_len, value, add=False):
    for j, o in enumerate(rows):
        part = value[j * run_len:(j + 1) * run_len]
        if add:
            ref[pl.ds(o, run_len), :] += part
        else:
            ref[pl.ds(o, run_len), :] = part


ATTN_FWD_UNROLL = 8
ATTN_BWD_UNROLL = 4


def _stack_heads(x, low):
    zero = jnp.zeros_like(x)
    return jnp.concatenate([jnp.where(low, x, zero), jnp.where(low, zero, x)], axis=0)


def _unstack_heads(y, low):
    return jnp.where(low, y[:ATT_BLOCK], y[ATT_BLOCK:])


def attn_fwd(qkvn, bias, name):
    S = qkvn.shape[0]
    L16 = S // RESIDUES
    n_iter = S // ATT_BLOCK

    def body(q_ref, k_ref, v_ref, b_ref, o_ref, lse_ref, stage, qp, kp, vp, acc_s, m_s, l_s):
        for src, dst in ((q_ref, qp), (k_ref, kp), (v_ref, vp)):
            stage[...] = src[...].astype(F32)
            _regroup(dst, stage, L16)
        low = _low_lanes((ATT_BLOCK, PAIR))

        for branch in BRANCH_ORDER:
            _, _, run_len, _ = _branch_geometry(branch, S)
            first = branch == BRANCH_ORDER[0]

            def step(it, carry, branch=branch, run_len=run_len, first=first):
                cur, prev, variant = _block_rows(it, branch, S)
                q = _load_block(qp, cur, run_len).astype(BF16)
                k = jnp.concatenate([_load_block(kp, prev, run_len), _load_block(kp, cur, run_len)], axis=0).astype(BF16)
                v = jnp.concatenate([_load_block(vp, prev, run_len), _load_block(vp, cur, run_len)], axis=0).astype(BF16)
                s = lax.dot_general(_stack_heads(q, low), k, _NT, preferred_element_type=F32) * (HEAD_DIM ** -0.5)
                s = s + b_ref[2 * branch + variant].reshape(2 * ATT_BLOCK, 2 * ATT_BLOCK)
                mx = jnp.max(s, axis=-1, keepdims=True)
                p = jnp.exp(s - mx)
                den = jnp.sum(p, axis=-1, keepdims=True)
                pv = jnp.dot(p.astype(BF16), v, preferred_element_type=F32)
                acc = _unstack_heads(pv, low)
                m = _unstack_heads(mx, low)
                l = _unstack_heads(den, low)
                if not first:
                    m_old = _load_block(m_s, cur, run_len)
                    m_new = jnp.maximum(m_old, m)
                    a_old, a_new = jnp.exp(m_old - m_new), jnp.exp(m - m_new)
                    acc = _load_block(acc_s, cur, run_len) * a_old + acc * a_new
                    l = _load_block(l_s, cur, run_len) * a_old + l * a_new
                    m = m_new
                _store_block(acc_s, cur, run_len, acc)
                _store_block(m_s, cur, run_len, m)
                _store_block(l_s, cur, run_len, l)
                return carry

            lax.fori_loop(0, n_iter, step, 0, unroll=ATTN_FWD_UNROLL)

        acc_s[...] = acc_s[...] / l_s[...]
        _ungroup(stage, acc_s, L16)
        o_ref[...] = stage[...].astype(BF16)
        m_s[...] = m_s[...] + jnp.log(l_s[...])
        _ungroup(lse_ref, m_s, L16)

    col = lambda part: pl.BlockSpec((S, PAIR), lambda hp: (0, part * N_PAIRS + hp))
    out = pl.BlockSpec((S, PAIR), lambda hp: (0, hp))
    return pl.pallas_call(
        body, name=name, grid=(N_PAIRS,),
        in_specs=[col(0), col(1), col(2), pl.BlockSpec((6, 2, ATT_BLOCK, 2 * ATT_BLOCK), lambda hp: (0, hp, 0, 0))],
        out_specs=[out, out], out_shape=[_sds((S, D_MODEL), BF16), _sds((S, D_MODEL), F32)],
        scratch_shapes=[pltpu.VMEM((S, PAIR), F32)] * 7,
        compiler_params=pltpu.CompilerParams(dimension_semantics=("parallel",), vmem_limit_bytes=ATTN_VMEM_LIMIT_BYTES),
    )(qkvn, qkvn, qkvn, bias)


def attn_bwd(qkvn, att, datt, lse, bias, name):
    S = qkvn.shape[0]
    L16 = S // RESIDUES
    n_iter = S // ATT_BLOCK
    TILE = 512

    def body(q_ref, k_ref, v_ref, o_ref, do_ref, lse_ref, b_ref, dq_ref, dk_ref, dv_ref, db_ref,
             qp, kp, vp, dop, ldp, dqp, dkp, dvp):
        stage = dqp
        for src, dst in ((q_ref, qp), (k_ref, kp), (v_ref, vp), (do_ref, dop)):
            stage[...] = src[...].astype(F32)
            _regroup(dst, stage, L16)

        def pack(i, carry):
            rows = pl.ds(pl.multiple_of(i * TILE, TILE), TILE)
            low = _low_lanes((TILE, PAIR))
            lane = lax.broadcasted_iota(jnp.int32, (TILE, PAIR), 1)
            prod = do_ref[rows, :].astype(F32) * o_ref[rows, :].astype(F32)
            d0 = jnp.sum(jnp.where(low, prod, 0.0), axis=-1, keepdims=True)
            d1 = jnp.sum(jnp.where(low, 0.0, prod), axis=-1, keepdims=True)
            stage[rows, :] = jnp.where((lane & (HEAD_DIM // 2)) == 0, lse_ref[rows, :], jnp.where(low, d0, d1))
            return carry

        lax.fori_loop(0, S // TILE, pack, 0)
        _regroup(ldp, stage, L16)
        dqp[...] = jnp.zeros_like(dqp)
        dkp[...] = jnp.zeros_like(dkp)
        dvp[...] = jnp.zeros_like(dvp)
        db_ref[...] = jnp.zeros_like(db_ref)
        low = _low_lanes((ATT_BLOCK, PAIR))

        for branch in BRANCH_ORDER:
            _, _, run_len, _ = _branch_geometry(branch, S)

            def step(it, carry, branch=branch, run_len=run_len):
                cur, prev, variant = _block_rows(it, branch, S)
                q = _load_block(qp, cur, run_len).astype(BF16)
                dout = _load_block(dop, cur, run_len).astype(BF16)
                ld = _load_block(ldp, cur, run_len)
                k = jnp.concatenate([_load_block(kp, prev, run_len), _load_block(kp, cur, run_len)], axis=0).astype(BF16)
                v = jnp.concatenate([_load_block(vp, prev, run_len), _load_block(vp, cur, run_len)], axis=0).astype(BF16)
                half = HEAD_DIM // 2
                lse2 = jnp.concatenate([ld[:, 0:1], ld[:, HEAD_DIM:HEAD_DIM + 1]], axis=0)
                delta2 = jnp.concatenate([ld[:, half:half + 1], ld[:, HEAD_DIM + half:HEAD_DIM + half + 1]], axis=0)
                q2, do2 = _stack_heads(q, low), _stack_heads(dout, low)
                s = lax.dot_general(q2, k, _NT, preferred_element_type=F32) * (HEAD_DIM ** -0.5)
                p = jnp.exp(s + b_ref[2 * branch + variant].reshape(2 * ATT_BLOCK, 2 * ATT_BLOCK) - lse2)
                dp = lax.dot_general(do2, v, _NT, preferred_element_type=F32)
                ds = p * (dp - delta2)
                db_ref[branch] += ds.reshape(2, ATT_BLOCK, 2 * ATT_BLOCK)
                dsb = (ds * (HEAD_DIM ** -0.5)).astype(BF16)
                dq = _unstack_heads(jnp.dot(dsb, k, preferred_element_type=F32), low)
                dk = lax.dot_general(dsb, q2, _TN, preferred_element_type=F32)
                dv = lax.dot_general(p.astype(BF16), do2, _TN, preferred_element_type=F32)
                _store_block(dqp, cur, run_len, dq, add=True)
                _store_block(dkp, prev, run_len, dk[:ATT_BLOCK], add=True)
                _store_block(dvp, prev, run_len, dv[:ATT_BLOCK], add=True)
                _store_block(dkp, cur, run_len, dk[ATT_BLOCK:], add=True)
                _store_block(dvp, cur, run_len, dv[ATT_BLOCK:], add=True)
                return carry

            lax.fori_loop(0, n_iter, step, 0, unroll=ATTN_BWD_UNROLL)

        _ungroup(dq_ref, dqp, L16)
        _ungroup(dk_ref, dkp, L16)
        _ungroup(dv_ref, dvp, L16)

    col = lambda part: pl.BlockSpec((S, PAIR), lambda hp: (0, part * N_PAIRS + hp))
    one = pl.BlockSpec((S, PAIR), lambda hp: (0, hp))
    return pl.pallas_call(
        body, name=name, grid=(N_PAIRS,),
        in_specs=[col(0), col(1), col(2), one, one, one,
                  pl.BlockSpec((6, 2, ATT_BLOCK, 2 * ATT_BLOCK), lambda hp: (0, hp, 0, 0))],
        out_specs=[one, one, one, pl.BlockSpec((3, 2, ATT_BLOCK, 2 * ATT_BLOCK), lambda hp: (0, hp, 0, 0))],
        out_shape=[_sds((S, D_MODEL), F32)] * 3 + [_sds((3, N_HEADS, ATT_BLOCK, 2 * ATT_BLOCK), F32)],
        scratch_shapes=[pltpu.VMEM((S, PAIR), F32)] * 8,
        compiler_params=pltpu.CompilerParams(dimension_semantics=("parallel",), vmem_limit_bytes=ATTN_VMEM_LIMIT_BYTES),
    )(qkvn, qkvn, qkvn, att, datt, lse, bias)


def _adamw_step(w_ref, g_ref, m_ref, v_ref, d_ref, nm_ref, nv_ref):
    gv = g_ref[...]
    m2 = ADAM_B1 * m_ref[...] + (1.0 - ADAM_B1) * gv
    v2 = ADAM_B2 * v_ref[...] + (1.0 - ADAM_B2) * (gv * gv)
    m_hat = m2 / (1.0 - ADAM_B1 ** ADAM_STEP)
    v_hat = v2 / (1.0 - ADAM_B2 ** ADAM_STEP)
    d_ref[...] = -ADAM_LR * (m_hat / (jnp.sqrt(v_hat) + ADAM_EPS) + ADAM_WD * w_ref[...])
    nm_ref[...] = m2
    nv_ref[...] = v2


def adamw_small(ws, gs, ms, vs, name):
    n = len(ws)

    def body(*refs):
        groups = [refs[k * n:(k + 1) * n] for k in range(7)]
        for refs_of_one in zip(*groups):
            _adamw_step(*refs_of_one)

    outs = pl.pallas_call(body, name=name, out_shape=[_sds(a.shape, F32) for a in ws] * 3,
                          compiler_params=_params())(*ws, *gs, *ms, *vs)
    return outs[:n], outs[n:2 * n], outs[2 * n:]


def adamw(w, g, m, v, name):
    n, R, C = w.shape

    def body(w_ref, g_ref, m_ref, v_ref, d_ref, nm_ref, nv_ref, go_ref):
        go_ref[...] = g_ref[...]
        _adamw_step(w_ref, g_ref, m_ref, v_ref, d_ref, nm_ref, nv_ref)

    tr = R
    while tr * C * 4 > ELEMENTWISE_BLOCK_BYTES and tr % 16 == 0:
        tr //= 2
    spec = pl.BlockSpec((None, tr, C), lambda i, r: (i, r, 0))
    return pl.pallas_call(
        body, name=name, grid=(n, R // tr), in_specs=[spec] * 4, out_specs=[spec] * 4,
        out_shape=[_sds((n, R, C), F32)] * 4, compiler_params=_params("parallel", "parallel"),
    )(w, g, m, v)


ANY = pl.BlockSpec(memory_space=pl.ANY)


def _coords():
    return lax.axis_index("x"), lax.axis_index("y"), lax.axis_index("c")


def _other_chips(mx, my):
    return [(1 - mx, my), (mx, 1 - my), (1 - mx, 1 - my)]


def _remote(src, dst, send, recv, dev):
    return pltpu.make_async_remote_copy(src_ref=src, dst_ref=dst, send_sem=send, recv_sem=recv, device_id=dev,
                                        device_id_type=MESH)


HBM =pl.BlockSpec(memory_space=pltpu.HBM)
SEM = pl.BlockSpec(memory_space=pltpu.SEMAPHORE)
_SPLIT_COPY = pltpu.CompilerParams(has_side_effects=pltpu.SideEffectType.DATAFLOW_SIDE_EFFECTING)


def _in_hbm(a):
    return pltpu.with_memory_space_constraint(a, pltpu.HBM)


def cast_into_slot(w, layer, chip_core, name, dtype=BF16):
    _, _, hR, C = w.shape

    def body(s_ref, w_ref, o_ref):
        del s_ref
        o_ref[...] = w_ref[...].astype(dtype)

    grid_spec = pltpu.PrefetchScalarGridSpec(
        num_scalar_prefetch=1, grid=(2,),
        in_specs=[pl.BlockSpec((None, None, hR, C), lambda h, s: (layer, h, 0, 0))],
        out_specs=pl.BlockSpec((None, None, hR, C), lambda h, s: (s[0], h, 0, 0)))
    return pl.pallas_call(body, name=name, grid_spec=grid_spec, out_shape=_sds_hbm((N_CHIPS, 2, hR, C), dtype),
                          compiler_params=_params("parallel"))(chip_core, w)


def gather_start(lands, groups, name):
    n = len(lands)
    n_groups = len(groups)

    def body(*refs):
        ins = refs[:n]
        sems = refs[n:n + 2 * n_groups]
        token = refs[-1]
        mx, my, mc = _coords()
        chip = 2 * mx + my
        for g, members in enumerate(groups):
            send, recv = sems[2 * g], sems[2 * g + 1]
            for i, a in enumerate(members):
                mine = ins[a].at[chip, mc]
                for k, (px, py) in enumerate(_other_chips(mx, my)):
                    _remote(mine, mine, send.at[3 * i + k], recv.at[3 * i + k], (px, py, mc)).start()
        token[...] = jnp.zeros_like(token)

    sem_shapes = []
    for members in groups:
        sem_shapes += [pltpu.SemaphoreType.DMA((3 * len(members),))] * 2
    outs = pl.pallas_call(
        body, name=name, in_specs=[HBM] * n,
        out_specs=[SEM] * (2 * n_groups) + [HBM] * n + [pl.BlockSpec(memory_space=pltpu.VMEM)],
        out_shape=sem_shapes + [pltpu.HBM(a.shape, a.dtype) for a in lands] + [_sds((SUBLANES, LANES), F32)],
        input_output_aliases={a: 2 * n_groups + a for a in range(n)}, compiler_params=_SPLIT_COPY,
    )(*[_in_hbm(a) for a in lands])
    sems = [(outs[2 * g], outs[2 * g + 1]) for g in range(n_groups)]
    return sems, list(outs[2 * n_groups:2 * n_groups + n]), outs[-1]


def gather_forward(lands, sems, after, name):
    n = len(lands)

    def body(*refs):
        ins = refs[:n]
        send, recv = refs[n], refs[n + 1]
        fsend, frecv = refs[n + 3], refs[n + 4]
        mx, my, mc = _coords()
        for i in range(n):
            for k, (px, py) in enumerate(_other_chips(mx, my)):
                landed = ins[i].at[2 * px + py, mc]
                cp = _remote(landed, landed, send.at[3 * i + k], recv.at[3 * i + k], (px, py, mc))
                cp.wait_send()
                cp.wait_recv()
                _remote(landed, landed, fsend.at[3 * i + k], frecv.at[3 * i + k], (mx, my, 1 - mc)).start()

    outs = pl.pallas_call(
        body, name=name, in_specs=[HBM] * n + [SEM, SEM, ANY], out_specs=[SEM, SEM] + [HBM] * n,
        out_shape=[pltpu.SemaphoreType.DMA((3 * n,))] * 2 + [pltpu.HBM(a.shape, a.dtype) for a in lands],
        input_output_aliases={a: 2 + a for a in range(n)}, compiler_params=_SPLIT_COPY,
    )(*lands, sems[0], sems[1], after)
    return (outs[0], outs[1]), list(outs[2:])


def gather_wait(lands, sems, after, name):
    n = len(lands)

    def body(*refs):
        ins = refs[:n]
        fsend, frecv = refs[n], refs[n + 1]
        mx, my, mc = _coords()
        for i in range(n):
            for k, (px, py) in enumerate(_other_chips(mx, my)):
                theirs = ins[i].at[2 * px + py, 1 - mc]
                cp = _remote(theirs, theirs, fsend.at[3 * i + k], frecv.at[3 * i + k], (mx, my, 1 - mc))
                cp.wait_send()
                cp.wait_recv()

    outs = pl.pallas_call(
        body, name=name, in_specs=[HBM] * n + [SEM, SEM, ANY], out_specs=[HBM] * n,
        out_shape=[pltpu.HBM(a.shape, a.dtype) for a in lands],
        input_output_aliases={a: a for a in range(n)}, compiler_params=_SPLIT_COPY,
    )(*lands, sems[0], sems[1], after)
    return list(outs)


def _peers(mx, my, mc):
    return [(1 - mx if k & 4 else mx, 1 - my if k & 2 else my, 1 - mc if k & 1 else mc) for k in range(1, N_DEV)]


def devices_start(x, name):
    def body(x_ref, land_ref, send, recv, x_thru, land_thru):
        mx, my, mc = _coords()
        me = 4 * mx + 2 * my + mc
        for k, peer in enumerate(_peers(mx, my, mc)):
            _remote(x_ref, land_ref.at[me], send.at[k], recv.at[k], peer).start()

    land = lax.empty((N_DEV,) + x.shape, x.dtype)
    outs = pl.pallas_call(
        body, name=name, in_specs=[HBM, HBM], out_specs=[SEM, SEM, HBM, HBM],
        out_shape=[pltpu.SemaphoreType.DMA((N_DEV - 1,))] * 2 + [pltpu.HBM(x.shape, x.dtype), pltpu.HBM(land.shape, x.dtype)],
        input_output_aliases={0: 2, 1: 3}, compiler_params=_SPLIT_COPY,
    )(_in_hbm(x), _in_hbm(land))
    return (outs[0], outs[1]), outs[2], outs[3]


def devices_wait(x, land, sems, after, name):
    def body(x_ref, land_ref, send, recv, after_ref, x_thru, land_thru):
        mx, my, mc = _coords()
        for k, (px, py, pc) in enumerate(_peers(mx, my, mc)):
            cp = _remote(x_ref, land_ref.at[4 * px + 2 * py + pc], send.at[k], recv.at[k], (px, py, pc))
            cp.wait_send()
            cp.wait_recv()

    outs = pl.pallas_call(
        body, name=name, in_specs=[HBM, HBM, SEM, SEM, ANY], out_specs=[HBM, HBM],
        out_shape=[pltpu.HBM(x.shape, x.dtype), pltpu.HBM(land.shape, land.dtype)],
        input_output_aliases={0: 0, 1: 1}, compiler_params=_SPLIT_COPY,
    )(x, land, sems[0], sems[1], after)
    return outs[0], outs[1]


def device_sum(land, own, me, name):
    _, R, C = land.shape

    def body(s_ref, l_ref, o_ref_in, o_ref):
        acc = None
        for q in range(N_DEV):
            term = jnp.where(s_ref[0] == q, o_ref_in[...], l_ref[q])
            acc = term if acc is None else acc + term
        o_ref[...] = acc

    grid_spec = pltpu.PrefetchScalarGridSpec(
        num_scalar_prefetch=1, grid=(1,),
        in_specs=[pl.BlockSpec((N_DEV, R, C), lambda i, s: (0, 0, 0)), pl.BlockSpec((R, C), lambda i, s: (0, 0))],
        out_specs=pl.BlockSpec((R, C), lambda i, s: (0, 0)))
    return pl.pallas_call(body, name=name, grid_spec=grid_spec, out_shape=_sds((R, C), F32),
                          compiler_params=_params("arbitrary"))(me, land, own)


def reduce_send(grads, name):
    n = len(grads)

    def body(*refs):
        ins, lands = refs[:n], refs[n:2 * n]
        send, recv = refs[2 * n], refs[2 * n + 1]
        mx, my, mc = _coords()
        me = 4 * mx + 2 * my + mc
        for a in range(n):
            for k, (px, py, pc) in enumerate(_peers(mx, my, mc)):
                _remote(ins[a].at[2 * px + py, pc], lands[a].at[me], send.at[7 * a + k], recv.at[7 * a + k], (px, py, pc)).start()

    lands = [lax.empty((N_DEV,) + g.shape[2:], g.dtype) for g in grads]
    outs = pl.pallas_call(
        body, name=name, in_specs=[HBM] * (2 * n), out_specs=[SEM, SEM] + [HBM] * (2 * n),
        out_shape=[pltpu.SemaphoreType.DMA((7 * n,))] * 2 + [pltpu.HBM(a.shape, a.dtype) for a in grads + lands],
        input_output_aliases={a: 2 + a for a in range(2 * n)}, compiler_params=_SPLIT_COPY,
    )(*[_in_hbm(a) for a in grads + lands])
    return (outs[0], outs[1]), list(outs[2:2 + n]), list(outs[2 + n:])


def reduce_wait(grads, lands, sems, after, name):
    n = len(grads)

    def body(*refs):
        ins, zones = refs[:n], refs[n:2 * n]
        send, recv = refs[2 * n], refs[2 * n + 1]
        mx, my, mc = _coords()
        for a in range(n):
            for k, (px, py, pc) in enumerate(_peers(mx, my, mc)):
                cp = _remote(ins[a].at[2 * px + py, pc], zones[a].at[4 * px + 2 * py + pc], send.at[7 * a + k],
                             recv.at[7 * a + k], (px, py, pc))
                cp.wait_send()
                cp.wait_recv()

    outs = pl.pallas_call(
        body, name=name, in_specs=[HBM] * (2 * n) + [SEM, SEM, ANY], out_specs=[HBM] * (2 * n),
        out_shape=[pltpu.HBM(a.shape, a.dtype) for a in grads + lands],
        input_output_aliases={a: a for a in range(2 * n)}, compiler_params=_SPLIT_COPY,
    )(*grads, *lands, sems[0], sems[1], after)
    return list(outs[:n]), list(outs[n:])


def reduce_sum(land, grad, place, name, into=None, layer=None):
    _, hR, C = land.shape
    tr = hR
    while N_DEV * tr * C * 2 > 3 * ELEMENTWISE_BLOCK_BYTES and tr % 32 == 0:
        tr //= 2

    def body(s_ref, l_ref, g_ref, *rest):
        o_ref = rest[-1]
        own = g_ref[...].astype(F32)
        acc = None
        for q in range(N_DEV):
            term = jnp.where(s_ref[2] == q, own, l_ref[q].astype(F32))
            acc = term if acc is None else acc + term
        o_ref[...] = acc

    in_specs = [pl.BlockSpec((N_DEV, tr, C), lambda i, s: (0, i, 0)),
                pl.BlockSpec((None, None, tr, C), lambda i, s: (s[0], s[1], i, 0))]
    args = [place, _in_hbm(land), _in_hbm(grad)]
    aliases = {}
    if layer is None:
        out_spec = pl.BlockSpec((None, tr, C), lambda i, s: (s[1], i, 0))
        out_shape = _sds_hbm((2, hR, C), F32)
    else:
        out_spec = pl.BlockSpec((None, None, tr, C), lambda i, s: (layer, s[1], i, 0))
        out_shape = _sds_hbm((2, 2, hR, C), F32)
        if into is not None:
            in_specs.append(ANY)
            args.append(into)
            aliases = {3: 0}
    grid_spec = pltpu.PrefetchScalarGridSpec(num_scalar_prefetch=1, grid=(hR // tr,), in_specs=in_specs, out_specs=out_spec)
    return pl.pallas_call(body, name=name, grid_spec=grid_spec, out_shape=out_shape, input_output_aliases=aliases,
                          compiler_params=_params("arbitrary"))(*args)


def join_halves(arrays, name):
    n = len(arrays)
    pieces = [(a, l) for a, arr in enumerate(arrays) for l in (range(arr.shape[0]) if arr.ndim == 4 else [None])]

    def body(*refs):
        ins = refs[:n]
        send, recv = refs[2 * n:]
        mx, my, mc = _coords()

        def half(a, l, h):
            return ins[a].at[h] if l is None else ins[a].at[l, h]

        sends = [_remote(half(a, l, mc), half(a, l, mc), send.at[i], recv.at[i], (mx, my, 1 - mc))
                 for i, (a, l) in enumerate(pieces)]
        for cp in sends:
            cp.start()
        for i, (a, l) in enumerate(pieces):
            theirs = half(a, l, 1 - mc)
            _remote(theirs, theirs, send.at[i], recv.at[i], (mx, my, 1 - mc)).wait_recv()
        for cp in sends:
            cp.wait_send()

    return pl.pallas_call(
        body, name=name, in_specs=[ANY] * n, out_specs=[ANY] * n, out_shape=[_sds(a.shape, a.dtype) for a in arrays],
        input_output_aliases={a: a for a in range(n)},
        scratch_shapes=[pltpu.SemaphoreType.DMA((len(pieces),)), pltpu.SemaphoreType.DMA((len(pieces),))],
    )(*arrays)


LANES = 128
SUBLANES = 8


def _n_rows(shape):
    rows = -(-int(np.prod(shape)) // LANES)
    return -(-rows // SUBLANES) * SUBLANES


def _as_rows(a):
    flat = a.reshape(-1)
    rows = _n_rows(a.shape)
    return jnp.pad(flat, (0, rows * LANES - flat.shape[0])).reshape(rows, LANES)


def _pack(arrays):
    return jnp.concatenate([_as_rows(a) for a in arrays], axis=0)


def _unpack(rows, shapes):
    out, r0 = [], 0
    for s in shapes:
        n = _n_rows(s)
        out.append(rows[r0:r0 + n].reshape(-1)[:int(np.prod(s))].reshape(s))
        r0 += n
    return out


REPLICATED_SMALL = [("rel_bias", (32, 16)), ("even_norm", (1, 1024)), ("even_pool_w", (1, 4, 128, 128)),
                    ("even_pool_scale", (1, 512)), ("odd_q_norm", (1, 64)), ("odd_k_norm", (1, 64)),
                    ("ffn_norm", (2, 1024)), ("ffn_conv_b", (2, 5632))]
SHARDED_SMALL = [("even_conv_w", (1, 3, 128)), ("odd_norm", (1, 256)), ("ffn_conv_w", (2, 3, 1408))]
BIG = ["even_w_in", "even_w_out", "odd_w_qkv", "odd_w_o", "ffn_w_up", "ffn_w_down"]
WEIGHT_ORDER = ["rel_bias", "even_norm", "even_w_in", "even_conv_w", "even_pool_w", "even_pool_scale", "even_w_out",
                "odd_norm", "odd_w_qkv", "odd_q_norm", "odd_k_norm", "odd_w_o", "ffn_norm", "ffn_w_up", "ffn_conv_w",
                "ffn_conv_b", "ffn_w_down"]


def kernel(x, rel_bias, even_norm, even_w_in, even_conv_w, even_pool_w, even_pool_scale, even_w_out, odd_norm, odd_w_qkv, odd_q_norm, odd_k_norm, odd_w_o, ffn_norm, ffn_w_up, ffn_conv_w, ffn_conv_b, ffn_w_down, loss_target, m_rel_bias, m_even_norm, m_even_w_in, m_even_conv_w, m_even_pool_w, m_even_pool_scale, m_even_w_out, m_odd_norm, m_odd_w_qkv, m_odd_q_norm, m_odd_k_norm, m_odd_w_o, m_ffn_norm, m_ffn_w_up, m_ffn_conv_w, m_ffn_conv_b, m_ffn_w_down, v_rel_bias, v_even_norm, v_even_w_in, v_even_conv_w, v_even_pool_w, v_even_pool_scale, v_even_w_out, v_odd_norm, v_odd_w_qkv, v_odd_q_norm, v_odd_k_norm, v_odd_w_o, v_ffn_norm, v_ffn_w_up, v_ffn_conv_w, v_ffn_conv_b, v_ffn_w_down):
    W = dict(rel_bias=rel_bias, even_norm=even_norm, even_w_in=even_w_in, even_conv_w=even_conv_w, even_pool_w=even_pool_w,
             even_pool_scale=even_pool_scale, even_w_out=even_w_out, odd_norm=odd_norm, odd_w_qkv=odd_w_qkv,
             odd_q_norm=odd_q_norm, odd_k_norm=odd_k_norm, odd_w_o=odd_w_o, ffn_norm=ffn_norm, ffn_w_up=ffn_w_up,
             ffn_conv_w=ffn_conv_w, ffn_conv_b=ffn_conv_b, ffn_w_down=ffn_w_down)
    M1 = dict(rel_bias=m_rel_bias, even_norm=m_even_norm, even_w_in=m_even_w_in, even_conv_w=m_even_conv_w,
              even_pool_w=m_even_pool_w, even_pool_scale=m_even_pool_scale, even_w_out=m_even_w_out, odd_norm=m_odd_norm,
              odd_w_qkv=m_odd_w_qkv, odd_q_norm=m_odd_q_norm, odd_k_norm=m_odd_k_norm, odd_w_o=m_odd_w_o,
              ffn_norm=m_ffn_norm, ffn_w_up=m_ffn_w_up, ffn_conv_w=m_ffn_conv_w, ffn_conv_b=m_ffn_conv_b,
              ffn_w_down=m_ffn_w_down)
    M2 = dict(rel_bias=v_rel_bias, even_norm=v_even_norm, even_w_in=v_even_w_in, even_conv_w=v_even_conv_w,
              even_pool_w=v_even_pool_w, even_pool_scale=v_even_pool_scale, even_w_out=v_even_w_out, odd_norm=v_odd_norm,
              odd_w_qkv=v_odd_w_qkv, odd_q_norm=v_odd_q_norm, odd_k_norm=v_odd_k_norm, odd_w_o=v_odd_w_o,
              ffn_norm=v_ffn_norm, ffn_w_up=v_ffn_w_up, ffn_conv_w=v_ffn_conv_w, ffn_conv_b=v_ffn_conv_b,
              ffn_w_down=v_ffn_w_down)
    mx, my, mc = _coords()
    chip = 2 * mx + my
    me = 4 * mx + 2 * my + mc
    place = jnp.stack([chip, mc, me]).astype(jnp.int32)
    xs, target = x[0], loss_target[0]

    def halves(w):
        return w.reshape((w.shape[0], 2, w.shape[-2] // 2, w.shape[-1]))

    small_rows = jnp.pad(_pack([even_conv_w, odd_norm, ffn_conv_w]), ((0, SUBLANES), (0, 0)))
    first = [cast_into_slot(halves(even_w_in), 0, place, "cast_w_in"), cast_into_slot(halves(even_w_out), 0, place, "cast_w_out"),
             cast_into_slot(small_rows.reshape(1, 2, small_rows.shape[0] // 2, LANES), 0, place, "small_into_slot", dtype=F32)]
    first_sems, first, token = gather_start(first, [[0, 1, 2]], "gather_start_first")
    even_norm_after_start = even_norm + token[0:1, 0:1]

    def later(a):
        return lax.optimization_barrier((a, token))[0]

    up_f32, down_f32 = halves(later(ffn_w_up)), halves(later(ffn_w_down))
    rest = [cast_into_slot(up_f32, 0, place, "cast_w_up0"), cast_into_slot(down_f32, 0, place, "cast_w_down0"),
            cast_into_slot(halves(later(odd_w_qkv)), 0, place, "cast_w_qkv"), cast_into_slot(halves(later(odd_w_o)), 0, place, "cast_w_o"),
            cast_into_slot(up_f32, 1, place, "cast_w_up1"), cast_into_slot(down_f32, 1, place, "cast_w_down1")]
    rest_sems, rest, rest_token = gather_start(rest, [[0], [1], [2, 3], [4], [5]], "gather_start_rest")
    group_arrays = [first, [rest[0]], [rest[1]], [rest[2], rest[3]], [rest[4]], [rest[5]]]
    group_sems = first_sems + rest_sems

    passing = {}

    def pass_on(group, tag, after, then):
        sems, arrays = gather_forward(group_arrays[group], group_sems[group], after, "gather_forward_" + tag)
        then, arrays = lax.optimization_barrier((then, arrays))
        passing[group] = (sems, arrays)
        return then

    def gathered(group, tag, after):
        sems, arrays = passing.pop(group)
        return gather_wait(arrays, sems, after, "gather_wait_" + tag)

    pool_w = cast_bf16(even_pool_w[0], "cast_pool_w")
    gqk = jnp.stack([jnp.tile(odd_q_norm[0], N_HEADS), jnp.tile(odd_k_norm[0], N_HEADS),
                     jnp.ones((D_MODEL,), F32)])[:, None, :]
    bias = bias_expand(later(rel_bias).T, "bias_expand").reshape(6, N_HEADS, ATT_BLOCK, 2 * ATT_BLOCK)
    xn0 = rmsnorm_fwd(xs, pass_on(0, "even", rest_token, even_norm_after_start), "even_norm")
    got = gathered(0, "even", xn0)
    w_in = got[0].reshape(N_CHIPS, 1, D_MODEL, EVEN_IN // N_CHIPS)
    w_out = got[1].reshape(1, 1, D_MODEL, D_MODEL)
    small = got[2].reshape(N_CHIPS, small_rows.shape[0], LANES)
    conv_w_full = small[:, 0:3].transpose(1, 0, 2).reshape(3, A_WIDTH)
    odd_norm_full = small[:, 8:10].reshape(1, D_MODEL)
    ffn_cw_full = small[:, 16:82].reshape(N_CHIPS, 2, 3, 2 * D_FF // N_CHIPS).transpose(1, 2, 0, 3).reshape(2, 3, 2 * D_FF)

    def ffn_fwd(l, xin, xn):
        up, u, act = up_glu_fwd(xn, w_up[l], ffn_cw_full[l], ffn_conv_b[l:l + 1], f"ffn{l}_up_glu")
        return act, (xin, xn, up, u, act)

    w_up, w_down = [None, None], [None, None]
    proj, mix = in_mixer_fwd(xn0, w_in, conv_w_full, pool_w, even_pool_scale, "even_in_mixer")
    x1, xn1 = mm_res_norm(mix, w_out, xs, ffn_norm[0:1], "even_out")
    pass_on(1, "up0", x1, x1)
    w_up[0] = gathered(1, "up0", x1)[0].reshape(N_CHIPS, 1, D_MODEL, 2 * D_FF // N_CHIPS)
    act0, ffn0 = ffn_fwd(0, x1, pass_on(2, "down0", x1, xn1))
    w_down[0] = gathered(2, "down0", act0)[0].reshape(1, 1, D_FF, D_MODEL)
    x2, xn2 = mm_res_norm(pass_on(3, "odd", act0, act0), w_down[0], x1, odd_norm_full, "ffn0_down")
    got = gathered(3, "odd", x2)
    xn2 = pass_on(5, "down1", x2, pass_on(4, "up1", x2, xn2))
    w_qkv = got[0].reshape(N_CHIPS, 1, D_MODEL, 3 * D_MODEL // N_CHIPS)
    w_o = got[1].reshape(1, 1, D_MODEL, D_MODEL)
    qkv, qkvn = qkv_qknorm_fwd(xn2, w_qkv, gqk, "odd_qkv_qknorm")
    att, lse = attn_fwd(qkvn, bias, "attn_fwd")
    x3, xn3 = mm_res_norm(att, w_o, x2, ffn_norm[1:2], "odd_out")
    w_up[1] = gathered(4, "up1", x3)[0].reshape(N_CHIPS, 1, D_MODEL, 2 * D_FF // N_CHIPS)
    act1, ffn1 = ffn_fwd(1, x3, xn3)
    w_down[1] = gathered(5, "down1", act1)[0].reshape(1, 1, D_FF, D_MODEL)
    dy, dyb, sq = mm_res_loss(act1, w_down[1], x3, target, "ffn1_down_loss")
    loss_part = (0.5 * jnp.sum(sq) * (1.0 / D_MODEL)).reshape(1, 1)

    def ffn_bwd(l, dy, dyb, saved):
        xin, xn, up, u, act = saved
        dw_down = mm_tn(act, dyb, f"ffn{l}_dw_down", J=1, tk=D_FF // 2, tm=1024)
        dact = mm_nt(dyb, w_down[l], f"ffn{l}_dact", tr=D_FF // 2, out_dtype=BF16, tm=1024)
        dup, dcw, dcb = glu_bwd(up, u, dact, ffn_cw_full[l], f"ffn{l}_glu_bwd")
        dw_up = mm_tn(xn, dup, f"ffn{l}_dw_up", J=N_CHIPS, tk=512, tm=1024, jb=2)
        dx, dxb, dg = mm_nt_norm_bwd(dup, w_up[l], xin, ffn_norm[l:l + 1], dy, f"ffn{l}_dx")
        return dx, dxb, (dw_down, dw_up, dcw, dcb, dg)

    def quarters(g):
        return g.reshape(N_CHIPS, 2, g.shape[0] * g.shape[1] // (2 * N_CHIPS), g.shape[-1])

    def reduce_start(grads, tag, then):
        sems, parts, zones = reduce_send([quarters(g) for g in grads], "reduce_send_" + tag)
        then, parts = lax.optimization_barrier((then, parts))
        return (sems, parts, zones), then

    dx3, dx3b, g_ffn1 = ffn_bwd(1, dy, dyb, ffn1)
    red_ffn1, (dx3, dx3b) = reduce_start([g_ffn1[1], g_ffn1[0]], "ffn1", (dx3, dx3b))
    dw_o = mm_tn(att, dx3b, "odd_dw_o", J=1, tk=512, tm=1024)
    datt = mm_nt(dx3b, w_o, "odd_datt", tr=D_MODEL, out_dtype=BF16)
    dq, dk, dv, dbias = attn_bwd(qkvn, att, datt, lse, bias, "attn_bwd")
    dqkv, dgqk = qknorm_bwd(qkv, dq, dk, dv, gqk, "odd_qknorm_bwd")
    dw_qkv = mm_tn(xn2, dqkv, "odd_dw_qkv", J=N_CHIPS, tk=512, tm=1024)
    red_odd, dqkv = reduce_start([dw_qkv, dw_o], "odd", dqkv)
    dx2, dx2b, dg_odd = mm_nt_norm_bwd(dqkv, w_qkv, x2, odd_norm_full, dx3, "odd_dx")
    dx1, dx1b, g_ffn0 = ffn_bwd(0, dx2, dx2b, ffn0)
    red_ffn0, (dx1, dx1b) = reduce_start([g_ffn0[1], g_ffn0[0]], "ffn0", (dx1, dx1b))
    dw_out = mm_tn(mix, dx1b, "even_dw_out", J=1, tk=512, tm=1024)
    dmix = mm_nt(dx1b, w_out, "even_dmix", tr=D_MODEL)
    dproj, dcw_even, dpw, dps = mixer_bwd(proj, dmix, conv_w_full, pool_w, even_pool_scale, "even_mixer_bwd")
    dw_in = mm_tn(xn0, dproj, "even_dw_in", J=N_CHIPS, tk=512, tm=1024)
    grad_x, _, dg_even = mm_nt_norm_bwd(dproj, w_in, xs, even_norm, dx1, "even_dx")
    d_rel = jnp.sum(bias_reduce(dbias.reshape(3, N_HEADS, 2 * ATT_BLOCK * ATT_BLOCK), "bias_reduce"), axis=0).T

    red_even, grad_x = reduce_start([dw_in, dw_out], "even", grad_x)

    dcw_sh = dcw_even.reshape(3, N_CHIPS, A_WIDTH // N_CHIPS).transpose(1, 0, 2)
    don_sh = dg_odd.reshape(N_CHIPS, D_MODEL // N_CHIPS)
    dfcw = jnp.stack([g_ffn0[2], g_ffn1[2]])
    dfcw_sh = dfcw.reshape(2, 3, N_CHIPS, 2 * D_FF // N_CHIPS).transpose(2, 0, 1, 3)
    rep_grads = [d_rel, dg_even, dpw[None], dps, _head_sum(dgqk[0]), _head_sum(dgqk[1]),
                 jnp.concatenate([g_ffn0[4], g_ffn1[4]], axis=0), jnp.concatenate([g_ffn0[3], g_ffn1[3]], axis=0)]
    rep_rows = _pack([loss_part] + rep_grads)
    n_loss = _n_rows(loss_part.shape)
    shard_rows = jnp.concatenate([_pack([dcw_sh[j], don_sh[j], dfcw_sh[j]]) for j in range(N_CHIPS)], axis=0)
    n_rep, n_shard = rep_rows.shape[0], shard_rows.shape[0] // N_CHIPS
    small_sems, small_rows, small_land = devices_start(jnp.concatenate([rep_rows, shard_rows], axis=0), "small_grads_start")
    grad_x, small_rows = lax.optimization_barrier((grad_x, small_rows))

    def reduce_end(red, tag, after):
        sems, parts, zones = red
        parts, zones = reduce_wait(parts, zones, sems, after, "reduce_wait_" + tag)
        return zones, parts

    z_ffn1, p_ffn1 = reduce_end(red_ffn1, "ffn1", grad_x)
    z_odd, p_odd = reduce_end(red_odd, "odd", grad_x)
    r_qkv = reduce_sum(z_odd[0], p_odd[0], place, "reduce_sum_w_qkv")
    r_o = reduce_sum(z_odd[1], p_odd[1], place, "reduce_sum_w_o")
    r_up = reduce_sum(z_ffn1[0], p_ffn1[0], place, "reduce_sum_w_up1", layer=1)
    r_down = reduce_sum(z_ffn1[1], p_ffn1[1], place, "reduce_sum_w_down1", layer=1)
    r_qkv, r_o, r_up, r_down = lax.optimization_barrier((r_qkv, r_o, r_up, r_down))
    z_ffn0, p_ffn0 = reduce_end(red_ffn0, "ffn0", r_down)
    r_up = reduce_sum(z_ffn0[0], p_ffn0[0], place, "reduce_sum_w_up0", into=r_up, layer=0)
    r_down = reduce_sum(z_ffn0[1], p_ffn0[1], place, "reduce_sum_w_down0", into=r_down, layer=0)
    later = ["odd_w_qkv", "odd_w_o", "ffn_w_up", "ffn_w_down"]
    joined = join_halves([r_qkv, r_o, r_up, r_down], "grads_join_late_layers")
    G = {nm: g.reshape(W[nm].shape) for nm, g in zip(later, joined)}

    D_, NM, NV = {}, {}, {}

    def update(nm):
        as3 = lambda a: a.reshape((-1,) + a.shape[-2:])
        outs = adamw(as3(W[nm]), as3(G[nm]), as3(M1[nm]), as3(M2[nm]), "adamw_" + nm)
        D_[nm], NM[nm], NV[nm], G[nm] = [o.reshape(W[nm].shape) for o in outs]

    def all_before(names):
        tied = lax.optimization_barrier([D_[nm] for nm in names])
        for nm, d in zip(names, tied):
            D_[nm] = d
        return tied[0]

    for nm in later:
        update(nm)
    z_even, p_even = reduce_end(red_even, "even", all_before(later))
    joined = join_halves([reduce_sum(z_even[0], p_even[0], place, "reduce_sum_w_in"),
                          reduce_sum(z_even[1], p_even[1], place, "reduce_sum_w_out")], "grads_join_first_layer")
    first = ["even_w_in", "even_w_out"]
    for nm, g in zip(first, joined):
        G[nm] = g.reshape(W[nm].shape)
        update(nm)
    small_rows, small_land = devices_wait(small_rows, small_land, small_sems, all_before(first), "small_grads_wait")
    small_sum = device_sum(small_land, small_rows, place[2:3], "small_grads_sum")
    mine = lax.dynamic_slice_in_dim(small_sum, n_rep + chip * n_shard, n_shard, axis=0)
    loss = small_sum[0, 0]
    g_small = jnp.concatenate([small_sum[n_loss:n_rep], mine], axis=0)
    small_names = [n for n, _ in REPLICATED_SMALL + SHARDED_SMALL]
    small_shapes = [s for _, s in REPLICATED_SMALL + SHARDED_SMALL]
    G.update(dict(zip(small_names, _unpack(g_small, small_shapes))))
    outs = adamw_small(*[[d[n] for n in small_names] for d in (W, G, M1, M2)], "adamw_small")
    for dst, o in zip((D_, NM, NV), outs):
        dst.update(dict(zip(small_names, o)))

    return (loss, grad_x[None], *[G[n] for n in WEIGHT_ORDER], *[D_[n] for n in WEIGHT_ORDER],
            *[NM[n] for n in WEIGHT_ORDER], *[NV[n] for n in WEIGHT_ORDER])


def _head_sum(dg):
    return jnp.sum(dg.reshape(N_HEADS, HEAD_DIM), axis=0, keepdims=True)
```

```python
import functools
import math

import numpy as np
import jax
import jax.numpy as jnp
from jax import lax
from jax.experimental import pallas as pl
from jax.experimental.pallas import tpu as pltpu

F32 = jnp.float32
BF16 = jnp.bfloat16

D_MODEL = 1024
N_HEADS = 16
HEAD_DIM = 64
A_WIDTH = 512
POOL_WINDOWS = (2, 4, 8, 16)
POOL_GROUP = 128
EVEN_IN = 2048
D_FF = 2816
DILATED_PAIRS = ((128, 1), (512, 4), (2048, 16))
ATT_BLOCK = 128
N_REL_BUCKETS = 32
REL_MAX_DISTANCE = 2048
EPS = 1e-6
MASK_VALUE = -1e30
ADAM_LR, ADAM_B1, ADAM_B2, ADAM_EPS, ADAM_WD, ADAM_STEP = 0.001, 0.9, 0.999, 1e-08, 0.01, 10

VMEM_LIMIT_BYTES = 48 * 1024 * 1024
ELEMENTWISE_BLOCK_BYTES = 2 * 1024 * 1024
N_CHIPS = 4
N_DEV = 8
MESH = pl.DeviceIdType.MESH


def _params(*sem):
    return pltpu.CompilerParams(dimension_semantics=sem if sem else None, vmem_limit_bytes=VMEM_LIMIT_BYTES)


def _sds(shape, dtype):
    return jax.ShapeDtypeStruct(tuple(shape), dtype)


def _sds_hbm(shape, dtype):
    return pltpu.HBM(tuple(shape), dtype)


def cast_bf16(x, name, tr=None):
    lead, (R, C) = x.shape[:-2], x.shape[-2:]
    n = int(np.prod(lead)) if lead else 1
    x3 = x.reshape((n, R, C))
    tr = tr or R

    def body(x_ref, o_ref):
        o_ref[...] = x_ref[...].astype(BF16)

    out = pl.pallas_call(
        body, name=name, grid=(n, R // tr),
        in_specs=[pl.BlockSpec((None, tr, C), lambda i, r: (i, r, 0))],
        out_specs=pl.BlockSpec((None, tr, C), lambda i, r: (i, r, 0)),
        out_shape=_sds((n, R, C), BF16), compiler_params=_params("parallel", "parallel"),
    )(x3)
    return out.reshape(lead + (R, C))


def rmsnorm_fwd(x, g, name, ts=512):
    S, Dm = x.shape

    def body(x_ref, g_ref, o_ref):
        xv = x_ref[...]
        r = lax.rsqrt(jnp.mean(xv * xv, axis=-1, keepdims=True) + EPS)
        o_ref[...] = ((xv * r) * g_ref[...]).astype(BF16)

    return pl.pallas_call(
        body, name=name, grid=(S // ts,),
        in_specs=[pl.BlockSpec((ts, Dm), lambda i: (i, 0)), pl.BlockSpec((1, Dm), lambda i: (0, 0))],
        out_specs=pl.BlockSpec((ts, Dm), lambda i: (i, 0)),
        out_shape=_sds((S, Dm), BF16), compiler_params=_params("parallel"),
    )(x, g)


def mm_res_norm(a, w, res, gain, name, tm=1024):
    M, K = a.shape
    Dm = w.shape[-1]

    def body(a_ref, w_ref, r_ref, g_ref, y_ref, yn_ref):
        y = r_ref[...] + jnp.dot(a_ref[...], w_ref[...], preferred_element_type=F32)
        y_ref[...] = y
        r = lax.rsqrt(jnp.mean(y * y, axis=-1, keepdims=True) + EPS)
        yn_ref[...] = ((y * r) * g_ref[...]).astype(BF16)

    row = pl.BlockSpec((tm, Dm), lambda m: (m, 0))
    return pl.pallas_call(
        body, name=name, grid=(M // tm,),
        in_specs=[pl.BlockSpec((tm, K), lambda m: (m, 0)),
                  pl.BlockSpec((None, None, K, Dm), lambda m: (0, 0, 0, 0), pipeline_mode=pl.Buffered(1)),
                  row, pl.BlockSpec((1, Dm), lambda m: (0, 0))],
        out_specs=[row, row], out_shape=[_sds((M, Dm), F32), _sds((M, Dm), BF16)],
        compiler_params=_params("parallel"),
    )(a, w, res, gain)


def mm_res_loss(a, w, res, target, name, tm=512):
    M, K = a.shape
    Dm = w.shape[-1]

    def body(a_ref, w_ref, r_ref, t_ref, d_ref, db_ref, s_ref):
        e = (r_ref[...] + jnp.dot(a_ref[...], w_ref[...], preferred_element_type=F32)) - t_ref[...]
        d = e * (1.0 / Dm)
        d_ref[...] = d
        db_ref[...] = d.astype(BF16)
        part = jnp.sum(e * e, axis=0, keepdims=True)

        @pl.when(pl.program_id(0) == 0)
        def _():
            s_ref[...] = part

        @pl.when(pl.program_id(0) > 0)
        def _():
            s_ref[...] += part

    row = pl.BlockSpec((tm, Dm), lambda m: (m, 0))
    return pl.pallas_call(
        body, name=name, grid=(M // tm,),
        in_specs=[pl.BlockSpec((tm, K), lambda m: (m, 0)),
                  pl.BlockSpec((None, None, K, Dm), lambda m: (0, 0, 0, 0), pipeline_mode=pl.Buffered(1)), row, row],
        out_specs=[row, row, pl.BlockSpec((1, Dm), lambda m: (0, 0))],
        out_shape=[_sds((M, Dm), F32), _sds((M, Dm), BF16), _sds((1, Dm), F32)],
        compiler_params=_params("arbitrary"),
    )(a, w, res, target)


def mm_nt(dy, w, name, tr, layer=0, out_dtype=F32, tm=512):
    M = dy.shape[0]
    J, _, R, Ns = w.shape
    dims = (((1,), (1,)), ((), ()))

    def body(dy_ref, w_ref, o_ref):
        acc = None
        for j in range(J):
            p = lax.dot_general(dy_ref[:, j * Ns:(j + 1) * Ns], w_ref[j], dims, preferred_element_type=F32)
            acc = p if acc is None else acc + p
        o_ref[...] = acc.astype(o_ref.dtype)

    return pl.pallas_call(
        body, name=name, grid=(R // tr, M // tm),
        in_specs=[pl.BlockSpec((tm, J * Ns), lambda r, m: (m, 0)),
                  pl.BlockSpec((J, None, tr, Ns), lambda r, m: (0, layer, r, 0))],
        out_specs=pl.BlockSpec((tm, tr), lambda r, m: (m, r)),
        out_shape=_sds((M, R), out_dtype),
        compiler_params=_params("parallel", "parallel"),
    )(dy, w)


def mm_nt_norm_bwd(dy, w, x, g, dres, name, layer=0, tm=512):
    M = dy.shape[0]
    J, _, Dm, Ns = w.shape
    dims = (((1,), (1,)), ((), ()))

    def body(dy_ref, w_ref, x_ref, g_ref, r_ref, dx_ref, dxb_ref, dg_ref):
        dxn = None
        for j in range(J):
            p = lax.dot_general(dy_ref[:, j * Ns:(j + 1) * Ns], w_ref[j], dims, preferred_element_type=F32)
            dxn = p if dxn is None else dxn + p
        xv = x_ref[...]
        r = lax.rsqrt(jnp.mean(xv * xv, axis=-1, keepdims=True) + EPS)
        gx = dxn * g_ref[...]
        dot = jnp.sum(gx * xv, axis=-1, keepdims=True)
        dx = r_ref[...] + r * gx - xv * ((r * r * r) * (dot * (1.0 / Dm)))
        dx_ref[...] = dx
        dxb_ref[...] = dx.astype(BF16)
        part = jnp.sum(dxn * (xv * r), axis=0, keepdims=True)

        @pl.when(pl.program_id(0) == 0)
        def _():
            dg_ref[...] = part

        @pl.when(pl.program_id(0) > 0)
        def _():
            dg_ref[...] += part

    row = pl.BlockSpec((tm, Dm), lambda m: (m, 0))
    vec = pl.BlockSpec((1, Dm), lambda m: (0, 0))
    return pl.pallas_call(
        body, name=name, grid=(M // tm,),
        in_specs=[pl.BlockSpec((tm, J * Ns), lambda m: (m, 0)),
                  pl.BlockSpec((J, None, Dm, Ns), lambda m: (0, layer, 0, 0), pipeline_mode=pl.Buffered(1)), row, vec, row],
        out_specs=[row, row, vec],
        out_shape=[_sds((M, Dm), F32), _sds((M, Dm), BF16), _sds((1, Dm), F32)],
        compiler_params=_params("arbitrary"),
    )(dy, w, x, g, dres)


def mm_tn(a, dy, name, J, tk, tm=512, jb=None):
    M, K = a.shape
    jb = jb or J
    Ns = dy.shape[1] // J
    N = jb * Ns
    n_m = M // tm
    dims = (((0,), (0,)), ((), ()))

    def body(a_ref, dy_ref, o_ref, acc_ref):
        p = lax.dot_general(a_ref[...], dy_ref[...], dims, preferred_element_type=F32)
        m = pl.program_id(2)

        @pl.when(m == 0)
        def _():
            acc_ref[...] = p

        @pl.when(m > 0)
        def _():
            acc_ref[...] += p

        @pl.when(m == n_m - 1)
        def _():
            for j in range(jb):
                o_ref[j] = acc_ref[:, j * Ns:(j + 1) * Ns].astype(BF16)

    return pl.pallas_call(
        body, name=name, grid=(J // jb, K // tk, n_m),
        in_specs=[pl.BlockSpec((tm, tk), lambda g, k, m: (m, k)), pl.BlockSpec((tm, N), lambda g, k, m: (m, g))],
        out_specs=pl.BlockSpec((jb, tk, Ns), lambda g, k, m: (g, k, 0)),
        out_shape=_sds((J, K, Ns), BF16), scratch_shapes=[pltpu.VMEM((tk, N), F32)],
        compiler_params=_params("parallel", "parallel", "arbitrary"),
    )(a, dy)


HALO = 16


def _shift_down(x, s):
    return pltpu.roll(x, s, 0)


def _shift_up(x, s):
    return pltpu.roll(x, x.shape[0] - s, 0)


def _conv3(z, cw):
    return (_shift_down(z, 2) * cw[0:1] + _shift_down(z, 1) * cw[1:2]) + z * cw[2:3]


def _window_count(first_row, n, k):
    t = first_row + lax.broadcasted_iota(jnp.int32, (n, 1), 0)
    return jnp.clip(t + 1, 1, k).astype(F32)


def in_mixer_fwd(xn, w_in, conv_w, pool_w, pool_scale, name, ts=512):
    S, K = xn.shape
    n = ts + HALO

    def body(xm_ref, xb_ref, w_ref, cw_ref, pw_ref, ps_ref, p_ref, o_ref):
        i = pl.program_id(0)
        before = jnp.where(i > 0, xb_ref[...], jnp.zeros_like(xb_ref))
        rows = jnp.concatenate([before, xm_ref[...]], axis=0)
        h, gb, gc, pin = [jnp.dot(rows, w_ref[j], preferred_element_type=F32) for j in range(N_CHIPS)]
        for j, part in enumerate((h, gb, gc, pin)):
            p_ref[:, j * A_WIDTH:(j + 1) * A_WIDTH] = part[HALO:]
        cz = _conv3(gc * h, cw_ref[...])
        o_ref[:, 0:A_WIDTH] = (gb[HALO:] * cz[HALO:]).astype(BF16)
        for g, k in enumerate(POOL_WINDOWS):
            p = pin[:, g * POOL_GROUP:(g + 1) * POOL_GROUP]
            w = p
            s = 1
            while s < k:
                w = w + _shift_down(w, s)
                s *= 2
            pooled = w / _window_count(i * ts - HALO, n, k) - p
            yb = jnp.dot(pooled[HALO:].astype(BF16), pw_ref[g], preferred_element_type=F32)
            yb = yb * ps_ref[:, g * POOL_GROUP:(g + 1) * POOL_GROUP]
            o_ref[:, A_WIDTH + g * POOL_GROUP:A_WIDTH + (g + 1) * POOL_GROUP] = yb.astype(BF16)

    hb = ts // HALO
    return pl.pallas_call(
        body, name=name, grid=(S // ts,),
        in_specs=[
            pl.BlockSpec((ts, K), lambda i: (i, 0)),
            pl.BlockSpec((HALO, K), lambda i: (jnp.maximum(i * hb - 1, 0), 0)),
            pl.BlockSpec((N_CHIPS, None, K, A_WIDTH), lambda i: (0, 0, 0, 0), pipeline_mode=pl.Buffered(1)),
            pl.BlockSpec((3, A_WIDTH), lambda i: (0, 0)),
            pl.BlockSpec((4, POOL_GROUP, POOL_GROUP), lambda i: (0, 0, 0)),
            pl.BlockSpec((1, 4 * POOL_GROUP), lambda i: (0, 0)),
        ],
        out_specs=[pl.BlockSpec((ts, EVEN_IN), lambda i: (i, 0)), pl.BlockSpec((ts, D_MODEL), lambda i: (i, 0))],
        out_shape=[_sds((S, EVEN_IN), F32), _sds((S, D_MODEL), BF16)], compiler_params=_params("parallel"),
    )(xn, xn, w_in, conv_w, pool_w, pool_scale)


def mixer_bwd(proj, dmix, conv_w, pool_w, pool_scale, name, ts=256):
    S = proj.shape[0]
    n = ts + 2 * HALO
    nt = S // ts
    tn_dims = (((0,), (0,)), ((), ()))
    nt_dims = (((1,), (1,)), ((), ()))

    def body(pm_ref, pb_ref, pa_ref, dm_ref, da_ref, cw_ref, pw_ref, ps_ref, o_ref, dcw_ref, dpw_ref, dps_ref):
        i = pl.program_id(0)
        last = i == nt - 1
        before = jnp.where(i > 0, pb_ref[...], 0.0)
        after = jnp.where(last, 0.0, pa_ref[...])
        ext = jnp.concatenate([before, pm_ref[...], after], axis=0)
        dafter = jnp.where(last, 0.0, da_ref[...])
        dext = jnp.concatenate([jnp.zeros((HALO, D_MODEL), F32), dm_ref[...], dafter], axis=0)
        cw = cw_ref[...]
        main = slice(HALO, HALO + ts)

        @pl.when(i == 0)
        def _():
            dcw_ref[...] = jnp.zeros_like(dcw_ref)
            dpw_ref[...] = jnp.zeros_like(dpw_ref)
            dps_ref[...] = jnp.zeros_like(dps_ref)

        h, gb, gc = ext[:, 0:A_WIDTH], ext[:, A_WIDTH:2 * A_WIDTH], ext[:, 2 * A_WIDTH:3 * A_WIDTH]
        z = gc * h
        z1, z2 = _shift_down(z, 1), _shift_down(z, 2)
        cz = (z2 * cw[0:1] + z1 * cw[1:2]) + z * cw[2:3]
        dya = dext[:, 0:A_WIDTH]
        dcz = dya * gb
        dz = dcz * cw[2:3] + _shift_up(dcz, 1) * cw[1:2] + _shift_up(dcz, 2) * cw[0:1]
        o_ref[:, 0:A_WIDTH] = (dz * gc)[main].astype(BF16)
        o_ref[:, A_WIDTH:2 * A_WIDTH] = (dya * cz)[main].astype(BF16)
        o_ref[:, 2 * A_WIDTH:3 * A_WIDTH] = (dz * h)[main].astype(BF16)
        dczm = dcz[main]
        dcw_ref[0:1, :] += jnp.sum(dczm * z2[main], axis=0, keepdims=True)
        dcw_ref[1:2, :] += jnp.sum(dczm * z1[main], axis=0, keepdims=True)
        dcw_ref[2:3, :] += jnp.sum(dczm * z[main], axis=0, keepdims=True)

        for g, k in enumerate(POOL_WINDOWS):
            lo = 3 * A_WIDTH + g * POOL_GROUP
            cols = slice(g * POOL_GROUP, (g + 1) * POOL_GROUP)
            p = ext[:, lo:lo + POOL_GROUP]
            w = p
            s = 1
            while s < k:
                w = w + _shift_down(w, s)
                s *= 2
            cnt = _window_count(i * ts - HALO, n, k)
            pooled = (w / cnt - p)[main].astype(BF16)
            dyb = dext[:, A_WIDTH + g * POOL_GROUP:A_WIDTH + (g + 1) * POOL_GROUP]
            e = dyb * ps_ref[:, cols]
            pre = jnp.dot(pooled, pw_ref[g], preferred_element_type=F32)
            dps_ref[:, cols] += jnp.sum(dyb[main] * pre, axis=0, keepdims=True)
            dpw_ref[g] += lax.dot_general(pooled, e[main].astype(BF16), tn_dims, preferred_element_type=F32)
            dpooled = lax.dot_general(e.astype(BF16), pw_ref[g], nt_dims, preferred_element_type=F32)
            q = dpooled / cnt
            a = q
            s = 1
            while s < k:
                a = a + _shift_up(a, s)
                s *= 2
            o_ref[:, lo:lo + POOL_GROUP] = (a - dpooled)[main].astype(BF16)

    hb = ts // HALO
    nh = S // HALO
    before_map = lambda i: (jnp.maximum(i * hb - 1, 0), 0)
    after_map = lambda i: (jnp.minimum((i + 1) * hb, nh - 1), 0)
    full = lambda *shape: pl.BlockSpec(shape, lambda i: (0,) * len(shape))
    return pl.pallas_call(
        body, name=name, grid=(nt,),
        in_specs=[
            pl.BlockSpec((ts, EVEN_IN), lambda i: (i, 0)),
            pl.BlockSpec((HALO, EVEN_IN), before_map),
            pl.BlockSpec((HALO, EVEN_IN), after_map),
            pl.BlockSpec((ts, D_MODEL), lambda i: (i, 0)),
            pl.BlockSpec((HALO, D_MODEL), after_map),
            full(3, A_WIDTH), full(4, POOL_GROUP, POOL_GROUP), full(1, 4 * POOL_GROUP),
        ],
        out_specs=[pl.BlockSpec((ts, EVEN_IN), lambda i: (i, 0)), full(3, A_WIDTH), full(4, POOL_GROUP, POOL_GROUP),
                   full(1, 4 * POOL_GROUP)],
        out_shape=[_sds((S, EVEN_IN), BF16), _sds((3, A_WIDTH), F32), _sds((4, POOL_GROUP, POOL_GROUP), F32),
                   _sds((1, 4 * POOL_GROUP), F32)],
        compiler_params=_params("arbitrary"),
    )(proj, proj, proj, dmix, dmix, conv_w, pool_w, pool_scale)


FFN_HALO = 16
FFN_TC = 1408


GLU_CHUNKS = ((0, 512), (512, 512), (1024, 384))


def up_glu_fwd(xn, w_up, conv_w, conv_b, name, tm=512):
    S, K = xn.shape
    nc = D_FF // FFN_TC

    def body(xm_ref, xb_ref, wg_ref, wu_ref, cwg_ref, cwu_ref, cbg_ref, cbu_ref, pg_ref, pu_ref, ug_ref, uu_ref, o_ref):
        before = jnp.where(pl.program_id(1) > 0, xb_ref[...], jnp.zeros_like(xb_ref))
        rows = jnp.concatenate([before, xm_ref[...]], axis=0)
        for lo, width in GLU_CHUNKS:
            cols = slice(lo, lo + width)
            pre_g = jnp.dot(rows, wg_ref[:, cols], preferred_element_type=F32)
            pre_u = jnp.dot(rows, wu_ref[:, cols], preferred_element_type=F32)
            gate = _conv3(pre_g, cwg_ref[:, cols])[FFN_HALO:] + cbg_ref[:, cols]
            upv = _conv3(pre_u, cwu_ref[:, cols])[FFN_HALO:] + cbu_ref[:, cols]
            pg_ref[:, cols] = pre_g[FFN_HALO:].astype(BF16)
            pu_ref[:, cols] = pre_u[FFN_HALO:].astype(BF16)
            ug_ref[:, cols] = gate.astype(BF16)
            uu_ref[:, cols] = upv.astype(BF16)
            o_ref[:, cols] = ((gate * (1.0 / (1.0 + jnp.exp(-gate)))) * upv).astype(BF16)

    hb = tm // FFN_HALO
    wspec = lambda off: pl.BlockSpec((None, None, K, FFN_TC), lambda j, m: (j + off, 0, 0, 0))
    cw = lambda off: pl.BlockSpec((3, FFN_TC), lambda j, m: (0, j + off))
    cb = lambda off: pl.BlockSpec((1, FFN_TC), lambda j, m: (0, j + off))
    out = pl.BlockSpec((tm, FFN_TC), lambda j, m: (m, j))
    pg, pu, ug, uu, act = pl.pallas_call(
        body, name=name, grid=(nc, S // tm),
        in_specs=[pl.BlockSpec((tm, K), lambda j, m: (m, 0)),
                  pl.BlockSpec((FFN_HALO, K), lambda j, m: (jnp.maximum(m * hb - 1, 0), 0)),
                  wspec(0), wspec(nc), cw(0), cw(nc), cb(0), cb(nc)],
        out_specs=[out] * 5, out_shape=[_sds((S, D_FF), BF16)] * 5,
        compiler_params=_params("parallel", "parallel"),
    )(xn, xn, w_up, w_up, conv_w, conv_w, conv_b, conv_b)
    return (pg, pu), (ug, uu), act


def glu_bwd(up, u, da, conv_w, name, ts=256):
    S = up[0].shape[0]
    nc = D_FF // FFN_TC
    nt = S // ts
    W = 2 * D_FF

    def body(xg_ref, xu_ref, gm_ref, ga_ref, um_ref, ua_ref, dm_ref, da_ref, cw_ref, dx_ref, dcw_ref, dcb_ref):
        i = pl.program_id(0)
        last = i == nt - 1

        @pl.when(i == 0)
        def _():
            dcw_ref[...] = jnp.zeros_like(dcw_ref)
            dcb_ref[...] = jnp.zeros_like(dcb_ref)

        def rows(m_ref, a_ref, cols):
            return jnp.concatenate([m_ref[:, cols], a_ref[:, cols]], axis=0).astype(F32)

        def back(d, x, cols):
            cw = cw_ref[:, cols]
            d1, d2 = _shift_up(d, 1), _shift_up(d, 2)
            dx_ref[:, cols] = ((d * cw[2:3] + d1 * cw[1:2]) + d2 * cw[0:1])[:ts].astype(BF16)
            dcb_ref[:, cols] += jnp.sum(d[:ts], axis=0, keepdims=True)
            dcw_ref[0:1, cols] += jnp.sum(d2[:ts] * x, axis=0, keepdims=True)
            dcw_ref[1:2, cols] += jnp.sum(d1[:ts] * x, axis=0, keepdims=True)
            dcw_ref[2:3, cols] += jnp.sum(d[:ts] * x, axis=0, keepdims=True)

        for c in range(nc):
            cols = slice(c * FFN_TC, (c + 1) * FFN_TC)
            ug, uu = rows(gm_ref, ga_ref, cols), rows(um_ref, ua_ref, cols)
            dae = rows(dm_ref, da_ref, cols)
            dae = jnp.where(last & (lax.broadcasted_iota(jnp.int32, dae.shape, 0) >= ts), 0.0, dae)
            sg = 1.0 / (1.0 + jnp.exp(-ug))
            duu = dae * (ug * sg)
            dug = (dae * uu) * (sg * (1.0 + ug * (1.0 - sg)))
            back(dug, xg_ref[:, cols].astype(F32), cols)
            back(duu, xu_ref[:, cols].astype(F32), slice(D_FF + c * FFN_TC, D_FF + (c + 1) * FFN_TC))

    hb = ts // FFN_HALO
    nh = S // FFN_HALO
    after_map = lambda i: (jnp.minimum((i + 1) * hb, nh - 1), 0)
    main = pl.BlockSpec((ts, D_FF), lambda i: (i, 0))
    after = pl.BlockSpec((FFN_HALO, D_FF), after_map)
    return pl.pallas_call(
        body, name=name, grid=(nt,),
        in_specs=[main, main, main, after, main, after, main, after, pl.BlockSpec((3, W), lambda i: (0, 0))],
        out_specs=[pl.BlockSpec((ts, W), lambda i: (i, 0)), pl.BlockSpec((3, W), lambda i: (0, 0)),
                   pl.BlockSpec((1, W), lambda i: (0, 0))],
        out_shape=[_sds((S, W), BF16), _sds((3, W), F32), _sds((1, W), F32)],
        compiler_params=_params("arbitrary"),
    )(up[0], up[1], u[0], u[0], u[1], u[1], da, da, conv_w)


MEAN_GROUP = 256


def _head_mean_matrix():
    h = np.arange(MEAN_GROUP) // HEAD_DIM
    return jnp.asarray((h[:, None] == h[None, :]).astype(np.float32) / HEAD_DIM, dtype=BF16)


def _head_mean(v, gm):
    vb = v.astype(BF16)
    return jnp.concatenate([jnp.dot(vb[:, c:c + MEAN_GROUP], gm, preferred_element_type=F32)
                            for c in range(0, v.shape[1], MEAN_GROUP)], axis=1)


def qkv_qknorm_fwd(xn, w_qkv, gqk, name, tm=1024):
    S, K = xn.shape
    J, _, _, Ns = w_qkv.shape
    gains = gqk.reshape(1, 3 * D_MODEL)

    def body(x_ref, w_ref, g_ref, gm_ref, raw_ref, o_ref):
        first_col = pl.program_id(0) * Ns
        acc = jnp.dot(x_ref[...], w_ref[...], preferred_element_type=F32)
        raw_ref[...] = acc
        gm = gm_ref[...]
        for c in range(0, Ns, MEAN_GROUP):
            cols = slice(c, c + MEAN_GROUP)
            x = acc[:, cols]
            mean = jnp.dot((x * x).astype(BF16), gm, preferred_element_type=F32)
            normed = (x * lax.rsqrt(mean + EPS)) * g_ref[:, cols]
            o_ref[:, cols] = jnp.where(first_col + c >= 2 * D_MODEL, x, normed).astype(BF16)

    return pl.pallas_call(
        body, name=name, grid=(J, S // tm),
        in_specs=[pl.BlockSpec((tm, K), lambda j, m: (m, 0)), pl.BlockSpec((None, None, K, Ns), lambda j, m: (j, 0, 0, 0)),
                  pl.BlockSpec((1, Ns), lambda j, m: (0, j)), pl.BlockSpec((MEAN_GROUP, MEAN_GROUP), lambda j, m: (0, 0))],
        out_specs=[pl.BlockSpec((tm, Ns), lambda j, m: (m, j))] * 2,
        out_shape=[_sds((S, J * Ns), F32), _sds((S, J * Ns), BF16)], compiler_params=_params("parallel", "parallel"),
    )(xn, w_qkv, gains, _head_mean_matrix())


def qknorm_bwd(qkv, dq, dk, dv, gqk, name, ts=256):
    S = qkv.shape[0]

    def body(x_ref, dq_ref, dk_ref, dv_ref, g_ref, gm_ref, o_ref, dg_ref):
        @pl.when(pl.program_id(0) == 0)
        def _():
            dg_ref[...] = jnp.zeros_like(dg_ref)

        gm = gm_ref[...]
        for part, d_ref in enumerate((dq_ref, dk_ref)):
            cols = slice(part * D_MODEL, (part + 1) * D_MODEL)
            x = x_ref[:, cols]
            d = d_ref[...]
            r = lax.rsqrt(_head_mean(x * x, gm) + EPS)
            gx = d * g_ref[part]
            o_ref[:, cols] = (r * gx - x * ((r * r * r) * _head_mean(gx * x, gm))).astype(BF16)
            dg_ref[part] += jnp.sum(d * (x * r), axis=0, keepdims=True)
        o_ref[:, 2 * D_MODEL:] = dv_ref[...].astype(BF16)

    row = pl.BlockSpec((ts, D_MODEL), lambda i: (i, 0))
    wide = pl.BlockSpec((ts, 3 * D_MODEL), lambda i: (i, 0))
    gains = pl.BlockSpec((3, 1, D_MODEL), lambda i: (0, 0, 0))
    return pl.pallas_call(
        body, name=name, grid=(S // ts,),
        in_specs=[wide, row, row, row, gains, pl.BlockSpec((MEAN_GROUP, MEAN_GROUP), lambda i: (0, 0))],
        out_specs=[wide, gains],
        out_shape=[_sds((S, 3 * D_MODEL), BF16), _sds((3, 1, D_MODEL), F32)],
        compiler_params=_params("arbitrary"),
    )(qkv, dq, dk, dv, gqk, _head_mean_matrix())


RESIDUES = 16


def _block_order(dil):
    runs = RESIDUES // dil
    slot = np.arange(ATT_BLOCK)
    return (slot % (ATT_BLOCK // runs)) * runs + slot // (ATT_BLOCK // runs)


def _bucket_tables():
    n = ATT_BLOCK
    max_exact = N_REL_BUCKETS // 2
    buckets, valids = [], []
    for _, dil in DILATED_PAIRS:
        order = _block_order(dil)
        a = order[:, None]
        c = np.concatenate([order, n + order])[None, :]
        first_half = (np.arange(2 * n) < n)[None, :]
        rel = a + n - c
        band = (rel >= 0) & (rel <= n)
        dist = np.clip(rel, 0, n) * dil
        dd = np.maximum(dist, 1).astype(np.float32)
        large = max_exact + (np.log(dd / np.float32(max_exact)) / np.float32(math.log(REL_MAX_DISTANCE / max_exact))
                             * np.float32(N_REL_BUCKETS - max_exact)).astype(np.int32)
        large = np.minimum(large, N_REL_BUCKETS - 1)
        buckets.append(np.where(dist < max_exact, dist, large).reshape(1, -1))
        valids.append(np.stack([(band & ~first_half).reshape(1, -1), band.reshape(1, -1)]))
    return np.stack(buckets).astype(np.int32), np.stack(valids).astype(np.int32)


BIAS_CHUNK = 8192


def _split3(x):
    a = x.astype(BF16)
    r = x - a.astype(F32)
    b = r.astype(BF16)
    c = (r - b.astype(F32)).astype(BF16)
    return a, b, c


def bias_expand(rel_bias_t, name):
    bucket, valid = _bucket_tables()
    nq = bucket.shape[-1]

    def body(t_ref, b_ref, v_ref, o_ref):
        onehot = (lax.broadcasted_iota(jnp.int32, (N_REL_BUCKETS, BIAS_CHUNK), 0) == b_ref[...]).astype(BF16)
        acc = None
        for term in _split3(t_ref[...]):
            p = jnp.dot(term, onehot, preferred_element_type=F32)
            acc = p if acc is None else acc + p
        o_ref[...] = jnp.where(v_ref[...] > 0, acc, MASK_VALUE)

    return pl.pallas_call(
        body, name=name, grid=(3, 2, nq // BIAS_CHUNK),
        in_specs=[pl.BlockSpec((N_HEADS, N_REL_BUCKETS), lambda b, v, c: (0, 0)),
                  pl.BlockSpec((None, 1, BIAS_CHUNK), lambda b, v, c: (b, 0, c)),
                  pl.BlockSpec((None, None, 1, BIAS_CHUNK), lambda b, v, c: (b, v, 0, c))],
        out_specs=pl.BlockSpec((None, None, N_HEADS, BIAS_CHUNK), lambda b, v, c: (b, v, 0, c)),
        out_shape=_sds((3, 2, N_HEADS, nq), F32), compiler_params=_params("parallel", "parallel", "parallel"),
    )(rel_bias_t, jnp.asarray(bucket), jnp.asarray(valid))


def bias_reduce(dbias, name):
    bucket, _ = _bucket_tables()
    nq = bucket.shape[-1]
    dims = (((1,), (1,)), ((), ()))

    def body(d_ref, b_ref, o_ref):
        onehot = (lax.broadcasted_iota(jnp.int32, (N_REL_BUCKETS, BIAS_CHUNK), 0) == b_ref[...]).astype(BF16)
        acc = None
        for term in _split3(d_ref[...]):
            p = lax.dot_general(term, onehot, dims, preferred_element_type=F32)
            acc = p if acc is None else acc + p

        @pl.when(pl.program_id(1) == 0)
        def _():
            o_ref[...] = acc

        @pl.when(pl.program_id(1) > 0)
        def _():
            o_ref[...] += acc

    return pl.pallas_call(
        body, name=name, grid=(3, nq // BIAS_CHUNK),
        in_specs=[pl.BlockSpec((None, N_HEADS, BIAS_CHUNK), lambda b, c: (b, 0, c)),
                  pl.BlockSpec((None, 1, BIAS_CHUNK), lambda b, c: (b, 0, c))],
        out_specs=pl.BlockSpec((None, N_HEADS, N_REL_BUCKETS), lambda b, c: (b, 0, 0)),
        out_shape=_sds((3, N_HEADS, N_REL_BUCKETS), F32), compiler_params=_params("parallel", "arbitrary"),
    )(dbias, jnp.asarray(bucket))


PAIR = 2 * HEAD_DIM
N_PAIRS = N_HEADS // 2
_NT = (((1,), (1,)), ((), ()))
_TN = (((0,), (0,)), ((), ()))


def _low_lanes(shape):
    return lax.broadcasted_iota(jnp.int32, shape, 1) < HEAD_DIM


ATTN_VMEM_LIMIT_BYTES = 56 * 1024 * 1024
BRANCH_ORDER = (2, 1, 0)


def _regroup(dst, src, L16):
    for r in range(RESIDUES):
        dst[pl.ds(r * L16, L16), :] = src[pl.ds(r, L16, stride=RESIDUES), :]


def _ungroup(dst, src, L16):
    for r in range(RESIDUES):
        dst[pl.ds(r, L16, stride=RESIDUES), :] = src[pl.ds(r * L16, L16), :]


def _branch_geometry(branch, S):
    dil = DILATED_PAIRS[branch][1]
    runs = RESIDUES // dil
    return dil, runs, ATT_BLOCK // runs, S // dil // ATT_BLOCK


def _block_rows(it, branch, S):
    dil, runs, run_len, n_blocks = _branch_geometry(branch, S)
    L16 = S // RESIDUES
    r, b = it // n_blocks, it % n_blocks
    prev = jnp.maximum(b - 1, 0)
    cur_rows = [pl.multiple_of((j * dil + r) * L16 + run_len * b, 8) for j in range(runs)]
    prev_rows = [pl.multiple_of((j * dil + r) * L16 + run_len * prev, 8) for j in range(runs)]
    return cur_rows, prev_rows, jnp.minimum(b, 1)


def _load_block(ref, rows, run_len):
    parts = [ref[pl.ds(o, run_len), :] for o in rows]
    return parts[0] if len(parts) == 1 else jnp.concatenate(parts, axis=0)


def _store_block(ref, rows, run_len, value, add=False):
    for j, o in enumerate(rows):
        part = value[j * run_len:(j + 1) * run_len]
        if add:
            ref[pl.ds(o, run_len), :] += part
        else:
            ref[pl.ds(o, run_len), :] = part


ATTN_FWD_UNROLL = 8
ATTN_BWD_UNROLL = 4


def _stack_heads(x, low):
    zero = jnp.zeros_like(x)
    return jnp.concatenate([jnp.where(low, x, zero), jnp.where(low, zero, x)], axis=0)


def _unstack_heads(y, low):
    return jnp.where(low, y[:ATT_BLOCK], y[ATT_BLOCK:])


def attn_fwd(qkvn, bias, name):
    S = qkvn.shape[0]
    L16 = S // RESIDUES
    n_iter = S // ATT_BLOCK

    def body(q_ref, k_ref, v_ref, b_ref, o_ref, lse_ref, stage, qp, kp, vp, acc_s, m_s, l_s):
        for src, dst in ((q_ref, qp), (k_ref, kp), (v_ref, vp)):
            stage[...] = src[...].astype(F32)
            _regroup(dst, stage, L16)
        low = _low_lanes((ATT_BLOCK, PAIR))

        for branch in BRANCH_ORDER:
            _, _, run_len, _ = _branch_geometry(branch, S)
            first = branch == BRANCH_ORDER[0]

            def step(it, carry, branch=branch, run_len=run_len, first=first):
                cur, prev, variant = _block_rows(it, branch, S)
                q = _load_block(qp, cur, run_len).astype(BF16)
                k = jnp.concatenate([_load_block(kp, prev, run_len), _load_block(kp, cur, run_len)], axis=0).astype(BF16)
                v = jnp.concatenate([_load_block(vp, prev, run_len), _load_block(vp, cur, run_len)], axis=0).astype(BF16)
                s = lax.dot_general(_stack_heads(q, low), k, _NT, preferred_element_type=F32) * (HEAD_DIM ** -0.5)
                s = s + b_ref[2 * branch + variant].reshape(2 * ATT_BLOCK, 2 * ATT_BLOCK)
                mx = jnp.max(s, axis=-1, keepdims=True)
                p = jnp.exp(s - mx)
                den = jnp.sum(p, axis=-1, keepdims=True)
                pv = jnp.dot(p.astype(BF16), v, preferred_element_type=F32)
                acc = _unstack_heads(pv, low)
                m = _unstack_heads(mx, low)
                l = _unstack_heads(den, low)
                if not first:
                    m_old = _load_block(m_s, cur, run_len)
                    m_new = jnp.maximum(m_old, m)
                    a_old, a_new = jnp.exp(m_old - m_new), jnp.exp(m - m_new)
                    acc = _load_block(acc_s, cur, run_len) * a_old + acc * a_new
                    l = _load_block(l_s, cur, run_len) * a_old + l * a_new
                    m = m_new
                _store_block(acc_s, cur, run_len, acc)
                _store_block(m_s, cur, run_len, m)
                _store_block(l_s, cur, run_len, l)
                return carry

            lax.fori_loop(0, n_iter, step, 0, unroll=ATTN_FWD_UNROLL)

        acc_s[...] = acc_s[...] / l_s[...]
        _ungroup(stage, acc_s, L16)
        o_ref[...] = stage[...].astype(BF16)
        m_s[...] = m_s[...] + jnp.log(l_s[...])
        _ungroup(lse_ref, m_s, L16)

    col = lambda part: pl.BlockSpec((S, PAIR), lambda hp: (0, part * N_PAIRS + hp))
    out = pl.BlockSpec((S, PAIR), lambda hp: (0, hp))
    return pl.pallas_call(
        body, name=name, grid=(N_PAIRS,),
        in_specs=[col(0), col(1), col(2), pl.BlockSpec((6, 2, ATT_BLOCK, 2 * ATT_BLOCK), lambda hp: (0, hp, 0, 0))],
        out_specs=[out, out], out_shape=[_sds((S, D_MODEL), BF16), _sds((S, D_MODEL), F32)],
        scratch_shapes=[pltpu.VMEM((S, PAIR), F32)] * 7,
        compiler_params=pltpu.CompilerParams(dimension_semantics=("parallel",), vmem_limit_bytes=ATTN_VMEM_LIMIT_BYTES),
    )(qkvn, qkvn, qkvn, bias)


def attn_bwd(qkvn, att, datt, lse, bias, name):
    S = qkvn.shape[0]
    L16 = S // RESIDUES
    n_iter = S // ATT_BLOCK
    TILE = 512

    def body(q_ref, k_ref, v_ref, o_ref, do_ref, lse_ref, b_ref, dq_ref, dk_ref, dv_ref, db_ref,
             qp, kp, vp, dop, ldp, dqp, dkp, dvp):
        stage = dqp
        for src, dst in ((q_ref, qp), (k_ref, kp), (v_ref, vp), (do_ref, dop)):
            stage[...] = src[...].astype(F32)
            _regroup(dst, stage, L16)

        def pack(i, carry):
            rows = pl.ds(pl.multiple_of(i * TILE, TILE), TILE)
            low = _low_lanes((TILE, PAIR))
            lane = lax.broadcasted_iota(jnp.int32, (TILE, PAIR), 1)
            prod = do_ref[rows, :].astype(F32) * o_ref[rows, :].astype(F32)
            d0 = jnp.sum(jnp.where(low, prod, 0.0), axis=-1, keepdims=True)
            d1 = jnp.sum(jnp.where(low, 0.0, prod), axis=-1, keepdims=True)
            stage[rows, :] = jnp.where((lane & (HEAD_DIM // 2)) == 0, lse_ref[rows, :], jnp.where(low, d0, d1))
            return carry

        lax.fori_loop(0, S // TILE, pack, 0)
        _regroup(ldp, stage, L16)
        dqp[...] = jnp.zeros_like(dqp)
        dkp[...] = jnp.zeros_like(dkp)
        dvp[...] = jnp.zeros_like(dvp)
        db_ref[...] = jnp.zeros_like(db_ref)
        low = _low_lanes((ATT_BLOCK, PAIR))

        for branch in BRANCH_ORDER:
            _, _, run_len, _ = _branch_geometry(branch, S)

            def step(it, carry, branch=branch, run_len=run_len):
                cur, prev, variant = _block_rows(it, branch, S)
                q = _load_block(qp, cur, run_len).astype(BF16)
                dout = _load_block(dop, cur, run_len).astype(BF16)
                ld = _load_block(ldp, cur, run_len)
                k = jnp.concatenate([_load_block(kp, prev, run_len), _load_block(kp, cur, run_len)], axis=0).astype(BF16)
                v = jnp.concatenate([_load_block(vp, prev, run_len), _load_block(vp, cur, run_len)], axis=0).astype(BF16)
                half = HEAD_DIM // 2
                lse2 = jnp.concatenate([ld[:, 0:1], ld[:, HEAD_DIM:HEAD_DIM + 1]], axis=0)
                delta2 = jnp.concatenate([ld[:, half:half + 1], ld[:, HEAD_DIM + half:HEAD_DIM + half + 1]], axis=0)
                q2, do2 = _stack_heads(q, low), _stack_heads(dout, low)
                s = lax.dot_general(q2, k, _NT, preferred_element_type=F32) * (HEAD_DIM ** -0.5)
                p = jnp.exp(s + b_ref[2 * branch + variant].reshape(2 * ATT_BLOCK, 2 * ATT_BLOCK) - lse2)
                dp = lax.dot_general(do2, v, _NT, preferred_element_type=F32)
                ds = p * (dp - delta2)
                db_ref[branch] += ds.reshape(2, ATT_BLOCK, 2 * ATT_BLOCK)
                dsb = (ds * (HEAD_DIM ** -0.5)).astype(BF16)
                dq = _unstack_heads(jnp.dot(dsb, k, preferred_element_type=F32), low)
                dk = lax.dot_general(dsb, q2, _TN, preferred_element_type=F32)
                dv = lax.dot_general(p.astype(BF16), do2, _TN, preferred_element_type=F32)
                _store_block(dqp, cur, run_len, dq, add=True)
                _store_block(dkp, prev, run_len, dk[:ATT_BLOCK], add=True)
                _store_block(dvp, prev, run_len, dv[:ATT_BLOCK], add=True)
                _store_block(dkp, cur, run_len, dk[ATT_BLOCK:], add=True)
                _store_block(dvp, cur, run_len, dv[ATT_BLOCK:], add=True)
                return carry

            lax.fori_loop(0, n_iter, step, 0, unroll=ATTN_BWD_UNROLL)

        _ungroup(dq_ref, dqp, L16)
        _ungroup(dk_ref, dkp, L16)
        _ungroup(dv_ref, dvp, L16)

    col = lambda part: pl.BlockSpec((S, PAIR), lambda hp: (0, part * N_PAIRS + hp))
    one = pl.BlockSpec((S, PAIR), lambda hp: (0, hp))
    return pl.pallas_call(
        body, name=name, grid=(N_PAIRS,),
        in_specs=[col(0), col(1), col(2), one, one, one,
                  pl.BlockSpec((6, 2, ATT_BLOCK, 2 * ATT_BLOCK), lambda hp: (0, hp, 0, 0))],
        out_specs=[one, one, one, pl.BlockSpec((3, 2, ATT_BLOCK, 2 * ATT_BLOCK), lambda hp: (0, hp, 0, 0))],
        out_shape=[_sds((S, D_MODEL), F32)] * 3 + [_sds((3, N_HEADS, ATT_BLOCK, 2 * ATT_BLOCK), F32)],
        scratch_shapes=[pltpu.VMEM((S, PAIR), F32)] * 8,
        compiler_params=pltpu.CompilerParams(dimension_semantics=("parallel",), vmem_limit_bytes=ATTN_VMEM_LIMIT_BYTES),
    )(qkvn, qkvn, qkvn, att, datt, lse, bias)


def _adamw_step(w_ref, g_ref, m_ref, v_ref, d_ref, nm_ref, nv_ref):
    gv = g_ref[...]
    m2 = ADAM_B1 * m_ref[...] + (1.0 - ADAM_B1) * gv
    v2 = ADAM_B2 * v_ref[...] + (1.0 - ADAM_B2) * (gv * gv)
    m_hat = m2 / (1.0 - ADAM_B1 ** ADAM_STEP)
    v_hat = v2 / (1.0 - ADAM_B2 ** ADAM_STEP)
    d_ref[...] = -ADAM_LR * (m_hat / (jnp.sqrt(v_hat) + ADAM_EPS) + ADAM_WD * w_ref[...])
    nm_ref[...] = m2
    nv_ref[...] = v2


def adamw_small(ws, gs, ms, vs, name):
    n = len(ws)

    def body(*refs):
        groups = [refs[k * n:(k + 1) * n] for k in range(7)]
        for refs_of_one in zip(*groups):
            _adamw_step(*refs_of_one)

    outs = pl.pallas_call(body, name=name, out_shape=[_sds(a.shape, F32) for a in ws] * 3,
                          compiler_params=_params())(*ws, *gs, *ms, *vs)
    return outs[:n], outs[n:2 * n], outs[2 * n:]


def adamw(w, g, m, v, name):
    n, R, C = w.shape

    def body(w_ref, g_ref, m_ref, v_ref, d_ref, nm_ref, nv_ref, go_ref):
        go_ref[...] = g_ref[...]
        _adamw_step(w_ref, g_ref, m_ref, v_ref, d_ref, nm_ref, nv_ref)

    tr = R
    while tr * C * 4 > ELEMENTWISE_BLOCK_BYTES and tr % 16 == 0:
        tr //= 2
    spec = pl.BlockSpec((None, tr, C), lambda i, r: (i, r, 0))
    return pl.pallas_call(
        body, name=name, grid=(n, R // tr), in_specs=[spec] * 4, out_specs=[spec] * 4,
        out_shape=[_sds((n, R, C), F32)] * 4, compiler_params=_params("parallel", "parallel"),
    )(w, g, m, v)


ANY = pl.BlockSpec(memory_space=pl.ANY)


def _coords():
    return lax.axis_index("x"), lax.axis_index("y"), lax.axis_index("c")


def _other_chips(mx, my):
    return [(1 - mx, my), (mx, 1 - my), (1 - mx, 1 - my)]


def _remote(src, dst, send, recv, dev):
    return pltpu.make_async_remote_copy(src_ref=src, dst_ref=dst, send_sem=send, recv_sem=recv, device_id=dev,
                                        device_id_type=MESH)


HBM =pl.BlockSpec(memory_space=pltpu.HBM)
SEM = pl.BlockSpec(memory_space=pltpu.SEMAPHORE)
_SPLIT_COPY = pltpu.CompilerParams(has_side_effects=pltpu.SideEffectType.DATAFLOW_SIDE_EFFECTING)


def _in_hbm(a):
    return pltpu.with_memory_space_constraint(a, pltpu.HBM)


def cast_into_slot(w, layer, chip_core, name, dtype=BF16):
    _, _, hR, C = w.shape

    def body(s_ref, w_ref, o_ref):
        del s_ref
        o_ref[...] = w_ref[...].astype(dtype)

    grid_spec = pltpu.PrefetchScalarGridSpec(
        num_scalar_prefetch=1, grid=(2,),
        in_specs=[pl.BlockSpec((None, None, hR, C), lambda h, s: (layer, h, 0, 0))],
        out_specs=pl.BlockSpec((None, None, hR, C), lambda h, s: (s[0], h, 0, 0)))
    return pl.pallas_call(body, name=name, grid_spec=grid_spec, out_shape=_sds_hbm((N_CHIPS, 2, hR, C), dtype),
                          compiler_params=_params("parallel"))(chip_core, w)


def gather_start(lands, groups, name):
    n = len(lands)
    n_groups = len(groups)

    def body(*refs):
        ins = refs[:n]
        sems = refs[n:n + 2 * n_groups]
        token = refs[-1]
        mx, my, mc = _coords()
        chip = 2 * mx + my
        for g, members in enumerate(groups):
            send, recv = sems[2 * g], sems[2 * g + 1]
            for i, a in enumerate(members):
                mine = ins[a].at[chip, mc]
                for k, (px, py) in enumerate(_other_chips(mx, my)):
                    _remote(mine, mine, send.at[3 * i + k], recv.at[3 * i + k], (px, py, mc)).start()
        token[...] = jnp.zeros_like(token)

    sem_shapes = []
    for members in groups:
        sem_shapes += [pltpu.SemaphoreType.DMA((3 * len(members),))] * 2
    outs = pl.pallas_call(
        body, name=name, in_specs=[HBM] * n,
        out_specs=[SEM] * (2 * n_groups) + [HBM] * n + [pl.BlockSpec(memory_space=pltpu.VMEM)],
        out_shape=sem_shapes + [pltpu.HBM(a.shape, a.dtype) for a in lands] + [_sds((SUBLANES, LANES), F32)],
        input_output_aliases={a: 2 * n_groups + a for a in range(n)}, compiler_params=_SPLIT_COPY,
    )(*[_in_hbm(a) for a in lands])
    sems = [(outs[2 * g], outs[2 * g + 1]) for g in range(n_groups)]
    return sems, list(outs[2 * n_groups:2 * n_groups + n]), outs[-1]


def gather_forward(lands, sems, after, name):
    n = len(lands)

    def body(*refs):
        ins = refs[:n]
        send, recv = refs[n], refs[n + 1]
        fsend, frecv = refs[n + 3], refs[n + 4]
        mx, my, mc = _coords()
        for i in range(n):
            for k, (px, py) in enumerate(_other_chips(mx, my)):
                landed = ins[i].at[2 * px + py, mc]
                cp = _remote(landed, landed, send.at[3 * i + k], recv.at[3 * i + k], (px, py, mc))
                cp.wait_send()
                cp.wait_recv()
                _remote(landed, landed, fsend.at[3 * i + k], frecv.at[3 * i + k], (mx, my, 1 - mc)).start()

    outs = pl.pallas_call(
        body, name=name, in_specs=[HBM] * n + [SEM, SEM, ANY], out_specs=[SEM, SEM] + [HBM] * n,
        out_shape=[pltpu.SemaphoreType.DMA((3 * n,))] * 2 + [pltpu.HBM(a.shape, a.dtype) for a in lands],
        input_output_aliases={a: 2 + a for a in range(n)}, compiler_params=_SPLIT_COPY,
    )(*lands, sems[0], sems[1], after)
    return (outs[0], outs[1]), list(outs[2:])


def gather_wait(lands, sems, after, name):
    n = len(lands)

    def body(*refs):
        ins = refs[:n]
        fsend, frecv = refs[n], refs[n + 1]
        mx, my, mc = _coords()
        for i in range(n):
            for k, (px, py) in enumerate(_other_chips(mx, my)):
                theirs = ins[i].at[2 * px + py, 1 - mc]
                cp = _remote(theirs, theirs, fsend.at[3 * i + k], frecv.at[3 * i + k], (mx, my, 1 - mc))
                cp.wait_send()
                cp.wait_recv()

    outs = pl.pallas_call(
        body, name=name, in_specs=[HBM] * n + [SEM, SEM, ANY], out_specs=[HBM] * n,
        out_shape=[pltpu.HBM(a.shape, a.dtype) for a in lands],
        input_output_aliases={a: a for a in range(n)}, compiler_params=_SPLIT_COPY,
    )(*lands, sems[0], sems[1], after)
    return list(outs)


def _peers(mx, my, mc):
    return [(1 - mx if k & 4 else mx, 1 - my if k & 2 else my, 1 - mc if k & 1 else mc) for k in range(1, N_DEV)]


def devices_start(x, name):
    def body(x_ref, land_ref, send, recv, x_thru, land_thru):
        mx, my, mc = _coords()
        me = 4 * mx + 2 * my + mc
        for k, peer in enumerate(_peers(mx, my, mc)):
            _remote(x_ref, land_ref.at[me], send.at[k], recv.at[k], peer).start()

    land = lax.empty((N_DEV,) + x.shape, x.dtype)
    outs = pl.pallas_call(
        body, name=name, in_specs=[HBM, HBM], out_specs=[SEM, SEM, HBM, HBM],
        out_shape=[pltpu.SemaphoreType.DMA((N_DEV - 1,))] * 2 + [pltpu.HBM(x.shape, x.dtype), pltpu.HBM(land.shape, x.dtype)],
        input_output_aliases={0: 2, 1: 3}, compiler_params=_SPLIT_COPY,
    )(_in_hbm(x), _in_hbm(land))
    return (outs[0], outs[1]), outs[2], outs[3]


def devices_wait(x, land, sems, after, name):
    def body(x_ref, land_ref, send, recv, after_ref, x_thru, land_thru):
        mx, my, mc = _coords()
        for k, (px, py, pc) in enumerate(_peers(mx, my, mc)):
            cp = _remote(x_ref, land_ref.at[4 * px + 2 * py + pc], send.at[k], recv.at[k], (px, py, pc))
            cp.wait_send()
            cp.wait_recv()

    outs = pl.pallas_call(
        body, name=name, in_specs=[HBM, HBM, SEM, SEM, ANY], out_specs=[HBM, HBM],
        out_shape=[pltpu.HBM(x.shape, x.dtype), pltpu.HBM(land.shape, land.dtype)],
        input_output_aliases={0: 0, 1: 1}, compiler_params=_SPLIT_COPY,
    )(x, land, sems[0], sems[1], after)
    return outs[0], outs[1]


def device_sum(land, own, me, name):
    _, R, C = land.shape

    def body(s_ref, l_ref, o_ref_in, o_ref):
        acc = None
        for q in range(N_DEV):
            term = jnp.where(s_ref[0] == q, o_ref_in[...], l_ref[q])
            acc = term if acc is None else acc + term
        o_ref[...] = acc

    grid_spec = pltpu.PrefetchScalarGridSpec(
        num_scalar_prefetch=1, grid=(1,),
        in_specs=[pl.BlockSpec((N_DEV, R, C), lambda i, s: (0, 0, 0)), pl.BlockSpec((R, C), lambda i, s: (0, 0))],
        out_specs=pl.BlockSpec((R, C), lambda i, s: (0, 0)))
    return pl.pallas_call(body, name=name, grid_spec=grid_spec, out_shape=_sds((R, C), F32),
                          compiler_params=_params("arbitrary"))(me, land, own)


def reduce_send(grads, name):
    n = len(grads)

    def body(*refs):
        ins, lands = refs[:n], refs[n:2 * n]
        send, recv = refs[2 * n], refs[2 * n + 1]
        mx, my, mc = _coords()
        me = 4 * mx + 2 * my + mc
        for a in range(n):
            for k, (px, py, pc) in enumerate(_peers(mx, my, mc)):
                _remote(ins[a].at[2 * px + py, pc], lands[a].at[me], send.at[7 * a + k], recv.at[7 * a + k], (px, py, pc)).start()

    lands = [lax.empty((N_DEV,) + g.shape[2:], g.dtype) for g in grads]
    outs = pl.pallas_call(
        body, name=name, in_specs=[HBM] * (2 * n), out_specs=[SEM, SEM] + [HBM] * (2 * n),
        out_shape=[pltpu.SemaphoreType.DMA((7 * n,))] * 2 + [pltpu.HBM(a.shape, a.dtype) for a in grads + lands],
        input_output_aliases={a: 2 + a for a in range(2 * n)}, compiler_params=_SPLIT_COPY,
    )(*[_in_hbm(a) for a in grads + lands])
    return (outs[0], outs[1]), list(outs[2:2 + n]), list(outs[2 + n:])


def reduce_wait(grads, lands, sems, after, name):
    n = len(grads)

    def body(*refs):
        ins, zones = refs[:n], refs[n:2 * n]
        send, recv = refs[2 * n], refs[2 * n + 1]
        mx, my, mc = _coords()
        for a in range(n):
            for k, (px, py, pc) in enumerate(_peers(mx, my, mc)):
                cp = _remote(ins[a].at[2 * px + py, pc], zones[a].at[4 * px + 2 * py + pc], send.at[7 * a + k],
                             recv.at[7 * a + k], (px, py, pc))
                cp.wait_send()
                cp.wait_recv()

    outs = pl.pallas_call(
        body, name=name, in_specs=[HBM] * (2 * n) + [SEM, SEM, ANY], out_specs=[HBM] * (2 * n),
        out_shape=[pltpu.HBM(a.shape, a.dtype) for a in grads + lands],
        input_output_aliases={a: a for a in range(2 * n)}, compiler_params=_SPLIT_COPY,
    )(*grads, *lands, sems[0], sems[1], after)
    return list(outs[:n]), list(outs[n:])


def reduce_sum(land, grad, place, name, into=None, layer=None):
    _, hR, C = land.shape
    tr = hR
    while N_DEV * tr * C * 2 > 3 * ELEMENTWISE_BLOCK_BYTES and tr % 32 == 0:
        tr //= 2

    def body(s_ref, l_ref, g_ref, *rest):
        o_ref = rest[-1]
        own = g_ref[...].astype(F32)
        acc = None
        for q in range(N_DEV):
            term = jnp.where(s_ref[2] == q, own, l_ref[q].astype(F32))
            acc = term if acc is None else acc + term
        o_ref[...] = acc

    in_specs = [pl.BlockSpec((N_DEV, tr, C), lambda i, s: (0, i, 0)),
                pl.BlockSpec((None, None, tr, C), lambda i, s: (s[0], s[1], i, 0))]
    args = [place, _in_hbm(land), _in_hbm(grad)]
    aliases = {}
    if layer is None:
        out_spec = pl.BlockSpec((None, tr, C), lambda i, s: (s[1], i, 0))
        out_shape = _sds_hbm((2, hR, C), F32)
    else:
        out_spec = pl.BlockSpec((None, None, tr, C), lambda i, s: (layer, s[1], i, 0))
        out_shape = _sds_hbm((2, 2, hR, C), F32)
        if into is not None:
            in_specs.append(ANY)
            args.append(into)
            aliases = {3: 0}
    grid_spec = pltpu.PrefetchScalarGridSpec(num_scalar_prefetch=1, grid=(hR // tr,), in_specs=in_specs, out_specs=out_spec)
    return pl.pallas_call(body, name=name, grid_spec=grid_spec, out_shape=out_shape, input_output_aliases=aliases,
                          compiler_params=_params("arbitrary"))(*args)


def join_halves(arrays, name):
    n = len(arrays)
    pieces = [(a, l) for a, arr in enumerate(arrays) for l in (range(arr.shape[0]) if arr.ndim == 4 else [None])]

    def body(*refs):
        ins = refs[:n]
        send, recv = refs[2 * n:]
        mx, my, mc = _coords()

        def half(a, l, h):
            return ins[a].at[h] if l is None else ins[a].at[l, h]

        sends = [_remote(half(a, l, mc), half(a, l, mc), send.at[i], recv.at[i], (mx, my, 1 - mc))
                 for i, (a, l) in enumerate(pieces)]
        for cp in sends:
            cp.start()
        for i, (a, l) in enumerate(pieces):
            theirs = half(a, l, 1 - mc)
            _remote(theirs, theirs, send.at[i], recv.at[i], (mx, my, 1 - mc)).wait_recv()
        for cp in sends:
            cp.wait_send()

    return pl.pallas_call(
        body, name=name, in_specs=[ANY] * n, out_specs=[ANY] * n, out_shape=[_sds(a.shape, a.dtype) for a in arrays],
        input_output_aliases={a: a for a in range(n)},
        scratch_shapes=[pltpu.SemaphoreType.DMA((len(pieces),)), pltpu.SemaphoreType.DMA((len(pieces),))],
    )(*arrays)


LANES = 128
SUBLANES = 8


def _n_rows(shape):
    rows = -(-int(np.prod(shape)) // LANES)
    return -(-rows // SUBLANES) * SUBLANES


def _as_rows(a):
    flat = a.reshape(-1)
    rows = _n_rows(a.shape)
    return jnp.pad(flat, (0, rows * LANES - flat.shape[0])).reshape(rows, LANES)


def _pack(arrays):
    return jnp.concatenate([_as_rows(a) for a in arrays], axis=0)


def _unpack(rows, shapes):
    out, r0 = [], 0
    for s in shapes:
        n = _n_rows(s)
        out.append(rows[r0:r0 + n].reshape(-1)[:int(np.prod(s))].reshape(s))
        r0 += n
    return out


REPLICATED_SMALL = [("rel_bias", (32, 16)), ("even_norm", (1, 1024)), ("even_pool_w", (1, 4, 128, 128)),
                    ("even_pool_scale", (1, 512)), ("odd_q_norm", (1, 64)), ("odd_k_norm", (1, 64)),
                    ("ffn_norm", (2, 1024)), ("ffn_conv_b", (2, 5632))]
SHARDED_SMALL = [("even_conv_w", (1, 3, 128)), ("odd_norm", (1, 256)), ("ffn_conv_w", (2, 3, 1408))]
BIG = ["even_w_in", "even_w_out", "odd_w_qkv", "odd_w_o", "ffn_w_up", "ffn_w_down"]
WEIGHT_ORDER = ["rel_bias", "even_norm", "even_w_in", "even_conv_w", "even_pool_w", "even_pool_scale", "even_w_out",
                "odd_norm", "odd_w_qkv", "odd_q_norm", "odd_k_norm", "odd_w_o", "ffn_norm", "ffn_w_up", "ffn_conv_w",
                "ffn_conv_b", "ffn_w_down"]


def kernel(x, rel_bias, even_norm, even_w_in, even_conv_w, even_pool_w, even_pool_scale, even_w_out, odd_norm, odd_w_qkv, odd_q_norm, odd_k_norm, odd_w_o, ffn_norm, ffn_w_up, ffn_conv_w, ffn_conv_b, ffn_w_down, loss_target, m_rel_bias, m_even_norm, m_even_w_in, m_even_conv_w, m_even_pool_w, m_even_pool_scale, m_even_w_out, m_odd_norm, m_odd_w_qkv, m_odd_q_norm, m_odd_k_norm, m_odd_w_o, m_ffn_norm, m_ffn_w_up, m_ffn_conv_w, m_ffn_conv_b, m_ffn_w_down, v_rel_bias, v_even_norm, v_even_w_in, v_even_conv_w, v_even_pool_w, v_even_pool_scale, v_even_w_out, v_odd_norm, v_odd_w_qkv, v_odd_q_norm, v_odd_k_norm, v_odd_w_o, v_ffn_norm, v_ffn_w_up, v_ffn_conv_w, v_ffn_conv_b, v_ffn_w_down):
    W = dict(rel_bias=rel_bias, even_norm=even_norm, even_w_in=even_w_in, even_conv_w=even_conv_w, even_pool_w=even_pool_w,
             even_pool_scale=even_pool_scale, even_w_out=even_w_out, odd_norm=odd_norm, odd_w_qkv=odd_w_qkv,
             odd_q_norm=odd_q_norm, odd_k_norm=odd_k_norm, odd_w_o=odd_w_o, ffn_norm=ffn_norm, ffn_w_up=ffn_w_up,
             ffn_conv_w=ffn_conv_w, ffn_conv_b=ffn_conv_b, ffn_w_down=ffn_w_down)
    M1 = dict(rel_bias=m_rel_bias, even_norm=m_even_norm, even_w_in=m_even_w_in, even_conv_w=m_even_conv_w,
              even_pool_w=m_even_pool_w, even_pool_scale=m_even_pool_scale, even_w_out=m_even_w_out, odd_norm=m_odd_norm,
              odd_w_qkv=m_odd_w_qkv, odd_q_norm=m_odd_q_norm, odd_k_norm=m_odd_k_norm, odd_w_o=m_odd_w_o,
              ffn_norm=m_ffn_norm, ffn_w_up=m_ffn_w_up, ffn_conv_w=m_ffn_conv_w, ffn_conv_b=m_ffn_conv_b,
              ffn_w_down=m_ffn_w_down)
    M2 = dict(rel_bias=v_rel_bias, even_norm=v_even_norm, even_w_in=v_even_w_in, even_conv_w=v_even_conv_w,
              even_pool_w=v_even_pool_w, even_pool_scale=v_even_pool_scale, even_w_out=v_even_w_out, odd_norm=v_odd_norm,
              odd_w_qkv=v_odd_w_qkv, odd_q_norm=v_odd_q_norm, odd_k_norm=v_odd_k_norm, odd_w_o=v_odd_w_o,
              ffn_norm=v_ffn_norm, ffn_w_up=v_ffn_w_up, ffn_conv_w=v_ffn_conv_w, ffn_conv_b=v_ffn_conv_b,
              ffn_w_down=v_ffn_w_down)
    mx, my, mc = _coords()
    chip = 2 * mx + my
    me = 4 * mx + 2 * my + mc
    place = jnp.stack([chip, mc, me]).astype(jnp.int32)
    xs, target = x[0], loss_target[0]

    def halves(w):
        return w.reshape((w.shape[0], 2, w.shape[-2] // 2, w.shape[-1]))

    small_rows = jnp.pad(_pack([even_conv_w, odd_norm, ffn_conv_w]), ((0, SUBLANES), (0, 0)))
    first = [cast_into_slot(halves(even_w_in), 0, place, "cast_w_in"), cast_into_slot(halves(even_w_out), 0, place, "cast_w_out"),
             cast_into_slot(small_rows.reshape(1, 2, small_rows.shape[0] // 2, LANES), 0, place, "small_into_slot", dtype=F32),
             cast_into_slot(halves(ffn_w_up), 0, place, "cast_w_up0")]
    first_sems, first, token = gather_start(first, [[0, 1, 2], [3]], "gather_start_first")
    even_norm_after_start = even_norm + token[0:1, 0:1]

    def later(a):
        return lax.optimization_barrier((a, token))[0]

    down_f32 = halves(later(ffn_w_down))
    rest = [cast_into_slot(down_f32, 0, place, "cast_w_down0"),
            cast_into_slot(halves(later(odd_w_qkv)), 0, place, "cast_w_qkv"), cast_into_slot(halves(later(odd_w_o)), 0, place, "cast_w_o"),
            cast_into_slot(halves(later(ffn_w_up)), 1, place, "cast_w_up1"), cast_into_slot(down_f32, 1, place, "cast_w_down1")]
    rest_sems, rest, rest_token = gather_start(rest, [[0], [1, 2], [3], [4]], "gather_start_rest")
    group_arrays = [first[:3], [first[3]], [rest[0]], [rest[1], rest[2]], [rest[3]], [rest[4]]]
    group_sems = first_sems + rest_sems

    passing = {}

    def pass_on(group, tag, after, then):
        sems, arrays = gather_forward(group_arrays[group], group_sems[group], after, "gather_forward_" + tag)
        then, arrays = lax.optimization_barrier((then, arrays))
        passing[group] = (sems, arrays)
        return then

    def gathered(group, tag, after):
        sems, arrays = passing.pop(group)
        return gather_wait(arrays, sems, after, "gather_wait_" + tag)

    pool_w = cast_bf16(even_pool_w[0], "cast_pool_w")
    gqk = jnp.stack([jnp.tile(odd_q_norm[0], N_HEADS), jnp.tile(odd_k_norm[0], N_HEADS),
                     jnp.ones((D_MODEL,), F32)])[:, None, :]
    bias = bias_expand(later(rel_bias).T, "bias_expand").reshape(6, N_HEADS, ATT_BLOCK, 2 * ATT_BLOCK)
    xn0 = rmsnorm_fwd(xs, pass_on(0, "even", rest_token, even_norm_after_start), "even_norm")
    got = gathered(0, "even", xn0)
    w_in = got[0].reshape(N_CHIPS, 1, D_MODEL, EVEN_IN // N_CHIPS)
    w_out = got[1].reshape(1, 1, D_MODEL, D_MODEL)
    small = got[2].reshape(N_CHIPS, small_rows.shape[0], LANES)
    conv_w_full = small[:, 0:3].transpose(1, 0, 2).reshape(3, A_WIDTH)
    odd_norm_full = small[:, 8:10].reshape(1, D_MODEL)
    ffn_cw_full = small[:, 16:82].reshape(N_CHIPS, 2, 3, 2 * D_FF // N_CHIPS).transpose(1, 2, 0, 3).reshape(2, 3, 2 * D_FF)

    def ffn_fwd(l, xin, xn):
        up, u, act = up_glu_fwd(xn, w_up[l], ffn_cw_full[l], ffn_conv_b[l:l + 1], f"ffn{l}_up_glu")
        return act, (xin, xn, up, u, act)

    w_up, w_down = [None, None], [None, None]
    proj, mix = in_mixer_fwd(xn0, w_in, conv_w_full, pool_w, even_pool_scale, "even_in_mixer")
    x1, xn1 = mm_res_norm(mix, w_out, xs, ffn_norm[0:1], "even_out")
    pass_on(1, "up0", x1, x1)
    w_up[0] = gathered(1, "up0", x1)[0].reshape(N_CHIPS, 1, D_MODEL, 2 * D_FF // N_CHIPS)
    act0, ffn0 = ffn_fwd(0, x1, pass_on(2, "down0", x1, xn1))
    w_down[0] = gathered(2, "down0", act0)[0].reshape(1, 1, D_FF, D_MODEL)
    x2, xn2 = mm_res_norm(pass_on(3, "odd", act0, act0), w_down[0], x1, odd_norm_full, "ffn0_down")
    got = gathered(3, "odd", x2)
    xn2 = pass_on(5, "down1", x2, pass_on(4, "up1", x2, xn2))
    w_qkv = got[0].reshape(N_CHIPS, 1, D_MODEL, 3 * D_MODEL // N_CHIPS)
    w_o = got[1].reshape(1, 1, D_MODEL, D_MODEL)
    qkv, qkvn = qkv_qknorm_fwd(xn2, w_qkv, gqk, "odd_qkv_qknorm")
    att, lse = attn_fwd(qkvn, bias, "attn_fwd")
    x3, xn3 = mm_res_norm(att, w_o, x2, ffn_norm[1:2], "odd_out")
    w_up[1] = gathered(4, "up1", x3)[0].reshape(N_CHIPS, 1, D_MODEL, 2 * D_FF // N_CHIPS)
    act1, ffn1 = ffn_fwd(1, x3, xn3)
    w_down[1] = gathered(5, "down1", act1)[0].reshape(1, 1, D_FF, D_MODEL)
    dy, dyb, sq = mm_res_loss(act1, w_down[1], x3, target, "ffn1_down_loss")
    loss_part = (0.5 * jnp.sum(sq) * (1.0 / D_MODEL)).reshape(1, 1)

    def ffn_bwd(l, dy, dyb, saved):
        xin, xn, up, u, act = saved
        dw_down = mm_tn(act, dyb, f"ffn{l}_dw_down", J=1, tk=D_FF // 2, tm=1024)
        dact = mm_nt(dyb, w_down[l], f"ffn{l}_dact", tr=D_FF // 2, out_dtype=BF16, tm=1024)
        dup, dcw, dcb = glu_bwd(up, u, dact, ffn_cw_full[l], f"ffn{l}_glu_bwd")
        dw_up = mm_tn(xn, dup, f"ffn{l}_dw_up", J=N_CHIPS, tk=512, tm=1024, jb=2)
        dx, dxb, dg = mm_nt_norm_bwd(dup, w_up[l], xin, ffn_norm[l:l + 1], dy, f"ffn{l}_dx")
        return dx, dxb, (dw_down, dw_up, dcw, dcb, dg)

    def quarters(g):
        return g.reshape(N_CHIPS, 2, g.shape[0] * g.shape[1] // (2 * N_CHIPS), g.shape[-1])

    def reduce_start(grads, tag, then):
        sems, parts, zones = reduce_send([quarters(g) for g in grads], "reduce_send_" + tag)
        then, parts = lax.optimization_barrier((then, parts))
        return (sems, parts, zones), then

    dx3, dx3b, g_ffn1 = ffn_bwd(1, dy, dyb, ffn1)
    red_ffn1, (dx3, dx3b) = reduce_start([g_ffn1[1], g_ffn1[0]], "ffn1", (dx3, dx3b))
    dw_o = mm_tn(att, dx3b, "odd_dw_o", J=1, tk=512, tm=1024)
    datt = mm_nt(dx3b, w_o, "odd_datt", tr=D_MODEL, out_dtype=BF16)
    dq, dk, dv, dbias = attn_bwd(qkvn, att, datt, lse, bias, "attn_bwd")
    dqkv, dgqk = qknorm_bwd(qkv, dq, dk, dv, gqk, "odd_qknorm_bwd")
    dw_qkv = mm_tn(xn2, dqkv, "odd_dw_qkv", J=N_CHIPS, tk=512, tm=1024)
    red_odd, dqkv = reduce_start([dw_qkv, dw_o], "odd", dqkv)
    dx2, dx2b, dg_odd = mm_nt_norm_bwd(dqkv, w_qkv, x2, odd_norm_full, dx3, "odd_dx")
    dx1, dx1b, g_ffn0 = ffn_bwd(0, dx2, dx2b, ffn0)
    red_ffn0, (dx1, dx1b) = reduce_start([g_ffn0[1], g_ffn0[0]], "ffn0", (dx1, dx1b))
    dw_out = mm_tn(mix, dx1b, "even_dw_out", J=1, tk=512, tm=1024)
    dmix = mm_nt(dx1b, w_out, "even_dmix", tr=D_MODEL)
    dproj, dcw_even, dpw, dps = mixer_bwd(proj, dmix, conv_w_full, pool_w, even_pool_scale, "even_mixer_bwd")
    dw_in = mm_tn(xn0, dproj, "even_dw_in", J=N_CHIPS, tk=512, tm=1024)
    grad_x, _, dg_even = mm_nt_norm_bwd(dproj, w_in, xs, even_norm, dx1, "even_dx")
    d_rel = jnp.sum(bias_reduce(dbias.reshape(3, N_HEADS, 2 * ATT_BLOCK * ATT_BLOCK), "bias_reduce"), axis=0).T

    red_even, grad_x = reduce_start([dw_in, dw_out], "even", grad_x)

    dcw_sh = dcw_even.reshape(3, N_CHIPS, A_WIDTH // N_CHIPS).transpose(1, 0, 2)
    don_sh = dg_odd.reshape(N_CHIPS, D_MODEL // N_CHIPS)
    dfcw = jnp.stack([g_ffn0[2], g_ffn1[2]])
    dfcw_sh = dfcw.reshape(2, 3, N_CHIPS, 2 * D_FF // N_CHIPS).transpose(2, 0, 1, 3)
    rep_grads = [d_rel, dg_even, dpw[None], dps, _head_sum(dgqk[0]), _head_sum(dgqk[1]),
                 jnp.concatenate([g_ffn0[4], g_ffn1[4]], axis=0), jnp.concatenate([g_ffn0[3], g_ffn1[3]], axis=0)]
    rep_rows = _pack([loss_part] + rep_grads)
    n_loss = _n_rows(loss_part.shape)
    shard_rows = jnp.concatenate([_pack([dcw_sh[j], don_sh[j], dfcw_sh[j]]) for j in range(N_CHIPS)], axis=0)
    n_rep, n_shard = rep_rows.shape[0], shard_rows.shape[0] // N_CHIPS
    small_sems, small_rows, small_land = devices_start(jnp.concatenate([rep_rows, shard_rows], axis=0), "small_grads_start")
    grad_x, small_rows = lax.optimization_barrier((grad_x, small_rows))

    def reduce_end(red, tag, after):
        sems, parts, zones = red
        parts, zones = reduce_wait(parts, zones, sems, after, "reduce_wait_" + tag)
        return zones, parts

    z_ffn1, p_ffn1 = reduce_end(red_ffn1, "ffn1", grad_x)
    z_odd, p_odd = reduce_end(red_odd, "odd", grad_x)
    r_qkv = reduce_sum(z_odd[0], p_odd[0], place, "reduce_sum_w_qkv")
    r_o = reduce_sum(z_odd[1], p_odd[1], place, "reduce_sum_w_o")
    r_up = reduce_sum(z_ffn1[0], p_ffn1[0], place, "reduce_sum_w_up1", layer=1)
    r_down = reduce_sum(z_ffn1[1], p_ffn1[1], place, "reduce_sum_w_down1", layer=1)
    r_qkv, r_o, r_up, r_down = lax.optimization_barrier((r_qkv, r_o, r_up, r_down))
    z_ffn0, p_ffn0 = reduce_end(red_ffn0, "ffn0", r_down)
    r_up = reduce_sum(z_ffn0[0], p_ffn0[0], place, "reduce_sum_w_up0", into=r_up, layer=0)
    r_down = reduce_sum(z_ffn0[1], p_ffn0[1], place, "reduce_sum_w_down0", into=r_down, layer=0)
    later = ["odd_w_qkv", "odd_w_o", "ffn_w_up", "ffn_w_down"]
    joined = join_halves([r_qkv, r_o, r_up, r_down], "grads_join_late_layers")
    G = {nm: g.reshape(W[nm].shape) for nm, g in zip(later, joined)}

    D_, NM, NV = {}, {}, {}

    def update(nm):
        as3 = lambda a: a.reshape((-1,) + a.shape[-2:])
        outs = adamw(as3(W[nm]), as3(G[nm]), as3(M1[nm]), as3(M2[nm]), "adamw_" + nm)
        D_[nm], NM[nm], NV[nm], G[nm] = [o.reshape(W[nm].shape) for o in outs]

    def all_before(names):
        tied = lax.optimization_barrier([D_[nm] for nm in names])
        for nm, d in zip(names, tied):
            D_[nm] = d
        return tied[0]

    for nm in later:
        update(nm)
    z_even, p_even = reduce_end(red_even, "even", all_before(later))
    joined = join_halves([reduce_sum(z_even[0], p_even[0], place, "reduce_sum_w_in"),
                          reduce_sum(z_even[1], p_even[1], place, "reduce_sum_w_out")], "grads_join_first_layer")
    first = ["even_w_in", "even_w_out"]
    for nm, g in zip(first, joined):
        G[nm] = g.reshape(W[nm].shape)
        update(nm)
    small_rows, small_land = devices_wait(small_rows, small_land, small_sems, all_before(first), "small_grads_wait")
    small_sum = device_sum(small_land, small_rows, place[2:3], "small_grads_sum")
    mine = lax.dynamic_slice_in_dim(small_sum, n_rep + chip * n_shard, n_shard, axis=0)
    loss = small_sum[0, 0]
    g_small = jnp.concatenate([small_sum[n_loss:n_rep], mine], axis=0)
    small_names = [n for n, _ in REPLICATED_SMALL + SHARDED_SMALL]
    small_shapes = [s for _, s in REPLICATED_SMALL + SHARDED_SMALL]
    G.update(dict(zip(small_names, _unpack(g_small, small_shapes))))
    outs = adamw_small(*[[d[n] for n in small_names] for d in (W, G, M1, M2)], "adamw_small")
    for dst, o in zip((D_, NM, NV), outs):
        dst.update(dict(zip(small_names, o)))

    return (loss, grad_x[None], *[G[n] for n in WEIGHT_ORDER], *[D_[n] for n in WEIGHT_ORDER],
            *[NM[n] for n in WEIGHT_ORDER], *[NV[n] for n in WEIGHT_ORDER])


def _head_sum(dg):
    return jnp.sum(dg.reshape(N_HEADS, HEAD_DIM), axis=0, keepdims=True)
```

```python
import functools
import math

import numpy as np
import jax
import jax.numpy as jnp
from jax import lax
from jax.experimental import pallas as pl
from jax.experimental.pallas import tpu as pltpu

F32 = jnp.float32
BF16 = jnp.bfloat16

D_MODEL = 1024
N_HEADS = 16
HEAD_DIM = 64
A_WIDTH = 512
POOL_WINDOWS = (2, 4, 8, 16)
POOL_GROUP = 128
EVEN_IN = 2048
D_FF = 2816
DILATED_PAIRS = ((128, 1), (512, 4), (2048, 16))
ATT_BLOCK = 128
N_REL_BUCKETS = 32
REL_MAX_DISTANCE = 2048
EPS = 1e-6
MASK_VALUE = -1e30
ADAM_LR, ADAM_B1, ADAM_B2, ADAM_EPS, ADAM_WD, ADAM_STEP = 0.001, 0.9, 0.999, 1e-08, 0.01, 10

VMEM_LIMIT_BYTES = 48 * 1024 * 1024
ELEMENTWISE_BLOCK_BYTES = 2 * 1024 * 1024
N_CHIPS = 4
N_DEV = 8
MESH = pl.DeviceIdType.MESH


def _params(*sem):
    return pltpu.CompilerParams(dimension_semantics=sem if sem else None, vmem_limit_bytes=VMEM_LIMIT_BYTES)


def _sds(shape, dtype):
    return jax.ShapeDtypeStruct(tuple(shape), dtype)


def _sds_hbm(shape, dtype):
    return pltpu.HBM(tuple(shape), dtype)


def cast_bf16(x, name, tr=None):
    lead, (R, C) = x.shape[:-2], x.shape[-2:]
    n = int(np.prod(lead)) if lead else 1
    x3 = x.reshape((n, R, C))
    tr = tr or R

    def body(x_ref, o_ref):
        o_ref[...] = x_ref[...].astype(BF16)

    out = pl.pallas_call(
        body, name=name, grid=(n, R // tr),
        in_specs=[pl.BlockSpec((None, tr, C), lambda i, r: (i, r, 0))],
        out_specs=pl.BlockSpec((None, tr, C), lambda i, r: (i, r, 0)),
        out_shape=_sds((n, R, C), BF16), compiler_params=_params("parallel", "parallel"),
    )(x3)
    return out.reshape(lead + (R, C))


def rmsnorm_fwd(x, g, name, ts=512):
    S, Dm = x.shape

    def body(x_ref, g_ref, o_ref):
        xv = x_ref[...]
        r = lax.rsqrt(jnp.mean(xv * xv, axis=-1, keepdims=True) + EPS)
        o_ref[...] = ((xv * r) * g_ref[...]).astype(BF16)

    return pl.pallas_call(
        body, name=name, grid=(S // ts,),
        in_specs=[pl.BlockSpec((ts, Dm), lambda i: (i, 0)), pl.BlockSpec((1, Dm), lambda i: (0, 0))],
        out_specs=pl.BlockSpec((ts, Dm), lambda i: (i, 0)),
        out_shape=_sds((S, Dm), BF16), compiler_params=_params("parallel"),
    )(x, g)


def mm_res_norm(a, w, res, gain, name, tm=1024):
    M, K = a.shape
    Dm = w.shape[-1]

    def body(a_ref, w_ref, r_ref, g_ref, y_ref, yn_ref):
        y = r_ref[...] + jnp.dot(a_ref[...], w_ref[...], preferred_element_type=F32)
        y_ref[...] = y
        r = lax.rsqrt(jnp.mean(y * y, axis=-1, keepdims=True) + EPS)
        yn_ref[...] = ((y * r) * g_ref[...]).astype(BF16)

    row = pl.BlockSpec((tm, Dm), lambda m: (m, 0))
    return pl.pallas_call(
        body, name=name, grid=(M // tm,),
        in_specs=[pl.BlockSpec((tm, K), lambda m: (m, 0)),
                  pl.BlockSpec((None, None, K, Dm), lambda m: (0, 0, 0, 0), pipeline_mode=pl.Buffered(1)),
                  row, pl.BlockSpec((1, Dm), lambda m: (0, 0))],
        out_specs=[row, row], out_shape=[_sds((M, Dm), F32), _sds((M, Dm), BF16)],
        compiler_params=_params("parallel"),
    )(a, w, res, gain)


def mm_res_loss(a, w, res, target, name, tm=512):
    M, K = a.shape
    Dm = w.shape[-1]

    def body(a_ref, w_ref, r_ref, t_ref, d_ref, db_ref, s_ref):
        e = (r_ref[...] + jnp.dot(a_ref[...], w_ref[...], preferred_element_type=F32)) - t_ref[...]
        d = e * (1.0 / Dm)
        d_ref[...] = d
        db_ref[...] = d.astype(BF16)
        part = jnp.sum(e * e, axis=0, keepdims=True)

        @pl.when(pl.program_id(0) == 0)
        def _():
            s_ref[...] = part

        @pl.when(pl.program_id(0) > 0)
        def _():
            s_ref[...] += part

    row = pl.BlockSpec((tm, Dm), lambda m: (m, 0))
    return pl.pallas_call(
        body, name=name, grid=(M // tm,),
        in_specs=[pl.BlockSpec((tm, K), lambda m: (m, 0)),
                  pl.BlockSpec((None, None, K, Dm), lambda m: (0, 0, 0, 0), pipeline_mode=pl.Buffered(1)), row, row],
        out_specs=[row, row, pl.BlockSpec((1, Dm), lambda m: (0, 0))],
        out_shape=[_sds((M, Dm), F32), _sds((M, Dm), BF16), _sds((1, Dm), F32)],
        compiler_params=_params("arbitrary"),
    )(a, w, res, target)


def mm_nt(dy, w, name, tr, layer=0, out_dtype=F32, tm=512):
    M = dy.shape[0]
    J, _, R, Ns = w.shape
    dims = (((1,), (1,)), ((), ()))

    def body(dy_ref, w_ref, o_ref):
        acc = None
        for j in range(J):
            p = lax.dot_general(dy_ref[:, j * Ns:(j + 1) * Ns], w_ref[j], dims, preferred_element_type=F32)
            acc = p if acc is None else acc + p
        o_ref[...] = acc.astype(o_ref.dtype)

    return pl.pallas_call(
        body, name=name, grid=(R // tr, M // tm),
        in_specs=[pl.BlockSpec((tm, J * Ns), lambda r, m: (m, 0)),
                  pl.BlockSpec((J, None, tr, Ns), lambda r, m: (0, layer, r, 0))],
        out_specs=pl.BlockSpec((tm, tr), lambda r, m: (m, r)),
        out_shape=_sds((M, R), out_dtype),
        compiler_params=_params("parallel", "parallel"),
    )(dy, w)


def mm_nt_norm_bwd(dy, w, x, g, dres, name, layer=0, tm=512):
    M = dy.shape[0]
    J, _, Dm, Ns = w.shape
    dims = (((1,), (1,)), ((), ()))

    def body(dy_ref, w_ref, x_ref, g_ref, r_ref, dx_ref, dxb_ref, dg_ref):
        dxn = None
        for j in range(J):
            p = lax.dot_general(dy_ref[:, j * Ns:(j + 1) * Ns], w_ref[j], dims, preferred_element_type=F32)
            dxn = p if dxn is None else dxn + p
        xv = x_ref[...]
        r = lax.rsqrt(jnp.mean(xv * xv, axis=-1, keepdims=True) + EPS)
        gx = dxn * g_ref[...]
        dot = jnp.sum(gx * xv, axis=-1, keepdims=True)
        dx = r_ref[...] + r * gx - xv * ((r * r * r) * (dot * (1.0 / Dm)))
        dx_ref[...] = dx
        dxb_ref[...] = dx.astype(BF16)
        part = jnp.sum(dxn * (xv * r), axis=0, keepdims=True)

        @pl.when(pl.program_id(0) == 0)
        def _():
            dg_ref[...] = part

        @pl.when(pl.program_id(0) > 0)
        def _():
            dg_ref[...] += part

    row = pl.BlockSpec((tm, Dm), lambda m: (m, 0))
    vec = pl.BlockSpec((1, Dm), lambda m: (0, 0))
    return pl.pallas_call(
        body, name=name, grid=(M // tm,),
        in_specs=[pl.BlockSpec((tm, J * Ns), lambda m: (m, 0)),
                  pl.BlockSpec((J, None, Dm, Ns), lambda m: (0, layer, 0, 0), pipeline_mode=pl.Buffered(1)), row, vec, row],
        out_specs=[row, row, vec],
        out_shape=[_sds((M, Dm), F32), _sds((M, Dm), BF16), _sds((1, Dm), F32)],
        compiler_params=_params("arbitrary"),
    )(dy, w, x, g, dres)


def mm_tn(a, dy, name, J, tk, tm=512, jb=None):
    M, K = a.shape
    jb = jb or J
    Ns = dy.shape[1] // J
    N = jb * Ns
    n_m = M // tm
    dims = (((0,), (0,)), ((), ()))

    def body(a_ref, dy_ref, o_ref, acc_ref):
        p = lax.dot_general(a_ref[...], dy_ref[...], dims, preferred_element_type=F32)
        m = pl.program_id(2)

        @pl.when(m == 0)
        def _():
            acc_ref[...] = p

        @pl.when(m > 0)
        def _():
            acc_ref[...] += p

        @pl.when(m == n_m - 1)
        def _():
            for j in range(jb):
                o_ref[j] = acc_ref[:, j * Ns:(j + 1) * Ns].astype(BF16)

    return pl.pallas_call(
        body, name=name, grid=(J // jb, K // tk, n_m),
        in_specs=[pl.BlockSpec((tm, tk), lambda g, k, m: (m, k)), pl.BlockSpec((tm, N), lambda g, k, m: (m, g))],
        out_specs=pl.BlockSpec((jb, tk, Ns), lambda g, k, m: (g, k, 0)),
        out_shape=_sds((J, K, Ns), BF16), scratch_shapes=[pltpu.VMEM((tk, N), F32)],
        compiler_params=_params("parallel", "parallel", "arbitrary"),
    )(a, dy)


HALO = 16


def _shift_down(x, s):
    return pltpu.roll(x, s, 0)


def _shift_up(x, s):
    return pltpu.roll(x, x.shape[0] - s, 0)


def _conv3(z, cw):
    return (_shift_down(z, 2) * cw[0:1] + _shift_down(z, 1) * cw[1:2]) + z * cw[2:3]


def _window_count(first_row, n, k):
    t = first_row + lax.broadcasted_iota(jnp.int32, (n, 1), 0)
    return jnp.clip(t + 1, 1, k).astype(F32)


def in_mixer_fwd(xn, w_in, conv_w, pool_w, pool_scale, name, ts=512):
    S, K = xn.shape
    n = ts + HALO

    def body(xm_ref, xb_ref, w_ref, cw_ref, pw_ref, ps_ref, p_ref, o_ref):
        i = pl.program_id(0)
        before = jnp.where(i > 0, xb_ref[...], jnp.zeros_like(xb_ref))
        rows = jnp.concatenate([before, xm_ref[...]], axis=0)
        h, gb, gc, pin = [jnp.dot(rows, w_ref[j], preferred_element_type=F32) for j in range(N_CHIPS)]
        for j, part in enumerate((h, gb, gc, pin)):
            p_ref[:, j * A_WIDTH:(j + 1) * A_WIDTH] = part[HALO:]
        cz = _conv3(gc * h, cw_ref[...])
        o_ref[:, 0:A_WIDTH] = (gb[HALO:] * cz[HALO:]).astype(BF16)
        for g, k in enumerate(POOL_WINDOWS):
            p = pin[:, g * POOL_GROUP:(g + 1) * POOL_GROUP]
            w = p
            s = 1
            while s < k:
                w = w + _shift_down(w, s)
                s *= 2
            pooled = w / _window_count(i * ts - HALO, n, k) - p
            yb = jnp.dot(pooled[HALO:].astype(BF16), pw_ref[g], preferred_element_type=F32)
            yb = yb * ps_ref[:, g * POOL_GROUP:(g + 1) * POOL_GROUP]
            o_ref[:, A_WIDTH + g * POOL_GROUP:A_WIDTH + (g + 1) * POOL_GROUP] = yb.astype(BF16)

    hb = ts // HALO
    return pl.pallas_call(
        body, name=name, grid=(S // ts,),
        in_specs=[
            pl.BlockSpec((ts, K), lambda i: (i, 0)),
            pl.BlockSpec((HALO, K), lambda i: (jnp.maximum(i * hb - 1, 0), 0)),
            pl.BlockSpec((N_CHIPS, None, K, A_WIDTH), lambda i: (0, 0, 0, 0), pipeline_mode=pl.Buffered(1)),
            pl.BlockSpec((3, A_WIDTH), lambda i: (0, 0)),
            pl.BlockSpec((4, POOL_GROUP, POOL_GROUP), lambda i: (0, 0, 0)),
            pl.BlockSpec((1, 4 * POOL_GROUP), lambda i: (0, 0)),
        ],
        out_specs=[pl.BlockSpec((ts, EVEN_IN), lambda i: (i, 0)), pl.BlockSpec((ts, D_MODEL), lambda i: (i, 0))],
        out_shape=[_sds((S, EVEN_IN), F32), _sds((S, D_MODEL), BF16)], compiler_params=_params("parallel"),
    )(xn, xn, w_in, conv_w, pool_w, pool_scale)


def mixer_bwd(proj, dmix, conv_w, pool_w, pool_scale, name, ts=256):
    S = proj.shape[0]
    n = ts + 2 * HALO
    nt = S // ts
    tn_dims = (((0,), (0,)), ((), ()))
    nt_dims = (((1,), (1,)), ((), ()))

    def body(pm_ref, pb_ref, pa_ref, dm_ref, da_ref, cw_ref, pw_ref, ps_ref, o_ref, dcw_ref, dpw_ref, dps_ref):
        i = pl.program_id(0)
        last = i == nt - 1
        before = jnp.where(i > 0, pb_ref[...], 0.0)
        after = jnp.where(last, 0.0, pa_ref[...])
        ext = jnp.concatenate([before, pm_ref[...], after], axis=0)
        dafter = jnp.where(last, 0.0, da_ref[...])
        dext = jnp.concatenate([jnp.zeros((HALO, D_MODEL), F32), dm_ref[...], dafter], axis=0)
        cw = cw_ref[...]
        main = slice(HALO, HALO + ts)

        @pl.when(i == 0)
        def _():
            dcw_ref[...] = jnp.zeros_like(dcw_ref)
            dpw_ref[...] = jnp.zeros_like(dpw_ref)
            dps_ref[...] = jnp.zeros_like(dps_ref)

        h, gb, gc = ext[:, 0:A_WIDTH], ext[:, A_WIDTH:2 * A_WIDTH], ext[:, 2 * A_WIDTH:3 * A_WIDTH]
        z = gc * h
        z1, z2 = _shift_down(z, 1), _shift_down(z, 2)
        cz = (z2 * cw[0:1] + z1 * cw[1:2]) + z * cw[2:3]
        dya = dext[:, 0:A_WIDTH]
        dcz = dya * gb
        dz = dcz * cw[2:3] + _shift_up(dcz, 1) * cw[1:2] + _shift_up(dcz, 2) * cw[0:1]
        o_ref[:, 0:A_WIDTH] = (dz * gc)[main].astype(BF16)
        o_ref[:, A_WIDTH:2 * A_WIDTH] = (dya * cz)[main].astype(BF16)
        o_ref[:, 2 * A_WIDTH:3 * A_WIDTH] = (dz * h)[main].astype(BF16)
        dczm = dcz[main]
        dcw_ref[0:1, :] += jnp.sum(dczm * z2[main], axis=0, keepdims=True)
        dcw_ref[1:2, :] += jnp.sum(dczm * z1[main], axis=0, keepdims=True)
        dcw_ref[2:3, :] += jnp.sum(dczm * z[main], axis=0, keepdims=True)

        for g, k in enumerate(POOL_WINDOWS):
            lo = 3 * A_WIDTH + g * POOL_GROUP
            cols = slice(g * POOL_GROUP, (g + 1) * POOL_GROUP)
            p = ext[:, lo:lo + POOL_GROUP]
            w = p
            s = 1
            while s < k:
                w = w + _shift_down(w, s)
                s *= 2
            cnt = _window_count(i * ts - HALO, n, k)
            pooled = (w / cnt - p)[main].astype(BF16)
            dyb = dext[:, A_WIDTH + g * POOL_GROUP:A_WIDTH + (g + 1) * POOL_GROUP]
            e = dyb * ps_ref[:, cols]
            pre = jnp.dot(pooled, pw_ref[g], preferred_element_type=F32)
            dps_ref[:, cols] += jnp.sum(dyb[main] * pre, axis=0, keepdims=True)
            dpw_ref[g] += lax.dot_general(pooled, e[main].astype(BF16), tn_dims, preferred_element_type=F32)
            dpooled = lax.dot_general(e.astype(BF16), pw_ref[g], nt_dims, preferred_element_type=F32)
            q = dpooled / cnt
            a = q
            s = 1
            while s < k:
                a = a + _shift_up(a, s)
                s *= 2
            o_ref[:, lo:lo + POOL_GROUP] = (a - dpooled)[main].astype(BF16)

    hb = ts // HALO
    nh = S // HALO
    before_map = lambda i: (jnp.maximum(i * hb - 1, 0), 0)
    after_map = lambda i: (jnp.minimum((i + 1) * hb, nh - 1), 0)
    full = lambda *shape: pl.BlockSpec(shape, lambda i: (0,) * len(shape))
    return pl.pallas_call(
        body, name=name, grid=(nt,),
        in_specs=[
            pl.BlockSpec((ts, EVEN_IN), lambda i: (i, 0)),
            pl.BlockSpec((HALO, EVEN_IN), before_map),
            pl.BlockSpec((HALO, EVEN_IN), after_map),
            pl.BlockSpec((ts, D_MODEL), lambda i: (i, 0)),
            pl.BlockSpec((HALO, D_MODEL), after_map),
            full(3, A_WIDTH), full(4, POOL_GROUP, POOL_GROUP), full(1, 4 * POOL_GROUP),
        ],
        out_specs=[pl.BlockSpec((ts, EVEN_IN), lambda i: (i, 0)), full(3, A_WIDTH), full(4, POOL_GROUP, POOL_GROUP),
                   full(1, 4 * POOL_GROUP)],
        out_shape=[_sds((S, EVEN_IN), BF16), _sds((3, A_WIDTH), F32), _sds((4, POOL_GROUP, POOL_GROUP), F32),
                   _sds((1, 4 * POOL_GROUP), F32)],
        compiler_params=_params("arbitrary"),
    )(proj, proj, proj, dmix, dmix, conv_w, pool_w, pool_scale)


FFN_HALO = 16
FFN_TC = 1408


GLU_CHUNKS = ((0, 512), (512, 512), (1024, 384))


def up_glu_fwd(xn, w_up, conv_w, conv_b, name, tm=512):
    S, K = xn.shape
    nc = D_FF // FFN_TC

    def body(xm_ref, xb_ref, wg_ref, wu_ref, cwg_ref, cwu_ref, cbg_ref, cbu_ref, pg_ref, pu_ref, ug_ref, uu_ref, o_ref):
        before = jnp.where(pl.program_id(1) > 0, xb_ref[...], jnp.zeros_like(xb_ref))
        rows = jnp.concatenate([before, xm_ref[...]], axis=0)
        for lo, width in GLU_CHUNKS:
            cols = slice(lo, lo + width)
            pre_g = jnp.dot(rows, wg_ref[:, cols], preferred_element_type=F32)
            pre_u = jnp.dot(rows, wu_ref[:, cols], preferred_element_type=F32)
            gate = _conv3(pre_g, cwg_ref[:, cols])[FFN_HALO:] + cbg_ref[:, cols]
            upv = _conv3(pre_u, cwu_ref[:, cols])[FFN_HALO:] + cbu_ref[:, cols]
            pg_ref[:, cols] = pre_g[FFN_HALO:].astype(BF16)
            pu_ref[:, cols] = pre_u[FFN_HALO:].astype(BF16)
            ug_ref[:, cols] = gate.astype(BF16)
            uu_ref[:, cols] = upv.astype(BF16)
            o_ref[:, cols] = ((gate * (1.0 / (1.0 + jnp.exp(-gate)))) * upv).astype(BF16)

    hb = tm // FFN_HALO
    wspec = lambda off: pl.BlockSpec((None, None, K, FFN_TC), lambda j, m: (j + off, 0, 0, 0))
    cw = lambda off: pl.BlockSpec((3, FFN_TC), lambda j, m: (0, j + off))
    cb = lambda off: pl.BlockSpec((1, FFN_TC), lambda j, m: (0, j + off))
    out = pl.BlockSpec((tm, FFN_TC), lambda j, m: (m, j))
    pg, pu, ug, uu, act = pl.pallas_call(
        body, name=name, grid=(nc, S // tm),
        in_specs=[pl.BlockSpec((tm, K), lambda j, m: (m, 0)),
                  pl.BlockSpec((FFN_HALO, K), lambda j, m: (jnp.maximum(m * hb - 1, 0), 0)),
                  wspec(0), wspec(nc), cw(0), cw(nc), cb(0), cb(nc)],
        out_specs=[out] * 5, out_shape=[_sds((S, D_FF), BF16)] * 5,
        compiler_params=_params("parallel", "parallel"),
    )(xn, xn, w_up, w_up, conv_w, conv_w, conv_b, conv_b)
    return (pg, pu), (ug, uu), act


def glu_bwd(up, u, da, conv_w, name, ts=256):
    S = up[0].shape[0]
    nc = D_FF // FFN_TC
    nt = S // ts
    W = 2 * D_FF

    def body(xg_ref, xu_ref, gm_ref, ga_ref, um_ref, ua_ref, dm_ref, da_ref, cw_ref, dx_ref, dcw_ref, dcb_ref):
        i = pl.program_id(0)
        last = i == nt - 1

        @pl.when(i == 0)
        def _():
            dcw_ref[...] = jnp.zeros_like(dcw_ref)
            dcb_ref[...] = jnp.zeros_like(dcb_ref)

        def rows(m_ref, a_ref, cols):
            return jnp.concatenate([m_ref[:, cols], a_ref[:, cols]], axis=0).astype(F32)

        def back(d, x, cols):
            cw = cw_ref[:, cols]
            d1, d2 = _shift_up(d, 1), _shift_up(d, 2)
            dx_ref[:, cols] = ((d * cw[2:3] + d1 * cw[1:2]) + d2 * cw[0:1])[:ts].astype(BF16)
            dcb_ref[:, cols] += jnp.sum(d[:ts], axis=0, keepdims=True)
            dcw_ref[0:1, cols] += jnp.sum(d2[:ts] * x, axis=0, keepdims=True)
            dcw_ref[1:2, cols] += jnp.sum(d1[:ts] * x, axis=0, keepdims=True)
            dcw_ref[2:3, cols] += jnp.sum(d[:ts] * x, axis=0, keepdims=True)

        for c in range(nc):
            cols = slice(c * FFN_TC, (c + 1) * FFN_TC)
            ug, uu = rows(gm_ref, ga_ref, cols), rows(um_ref, ua_ref, cols)
            dae = rows(dm_ref, da_ref, cols)
            dae = jnp.where(last & (lax.broadcasted_iota(jnp.int32, dae.shape, 0) >= ts), 0.0, dae)
            sg = 1.0 / (1.0 + jnp.exp(-ug))
            duu = dae * (ug * sg)
            dug = (dae * uu) * (sg * (1.0 + ug * (1.0 - sg)))
            back(dug, xg_ref[:, cols].astype(F32), cols)
            back(duu, xu_ref[:, cols].astype(F32), slice(D_FF + c * FFN_TC, D_FF + (c + 1) * FFN_TC))

    hb = ts // FFN_HALO
    nh = S // FFN_HALO
    after_map = lambda i: (jnp.minimum((i + 1) * hb, nh - 1), 0)
    main = pl.BlockSpec((ts, D_FF), lambda i: (i, 0))
    after = pl.BlockSpec((FFN_HALO, D_FF), after_map)
    return pl.pallas_call(
        body, name=name, grid=(nt,),
        in_specs=[main, main, main, after, main, after, main, after, pl.BlockSpec((3, W), lambda i: (0, 0))],
        out_specs=[pl.BlockSpec((ts, W), lambda i: (i, 0)), pl.BlockSpec((3, W), lambda i: (0, 0)),
                   pl.BlockSpec((1, W), lambda i: (0, 0))],
        out_shape=[_sds((S, W), BF16), _sds((3, W), F32), _sds((1, W), F32)],
        compiler_params=_params("arbitrary"),
    )(up[0], up[1], u[0], u[0], u[1], u[1], da, da, conv_w)


MEAN_GROUP = 256


def _head_mean_matrix():
    h = np.arange(MEAN_GROUP) // HEAD_DIM
    return jnp.asarray((h[:, None] == h[None, :]).astype(np.float32) / HEAD_DIM, dtype=BF16)


def _head_mean(v, gm):
    vb = v.astype(BF16)
    return jnp.concatenate([jnp.dot(vb[:, c:c + MEAN_GROUP], gm, preferred_element_type=F32)
                            for c in range(0, v.shape[1], MEAN_GROUP)], axis=1)


def qkv_qknorm_fwd(xn, w_qkv, gqk, name, tm=1024):
    S, K = xn.shape
    J, _, _, Ns = w_qkv.shape
    gains = gqk.reshape(1, 3 * D_MODEL)

    def body(x_ref, w_ref, g_ref, gm_ref, raw_ref, o_ref):
        first_col = pl.program_id(0) * Ns
        acc = jnp.dot(x_ref[...], w_ref[...], preferred_element_type=F32)
        raw_ref[...] = acc
        gm = gm_ref[...]
        for c in range(0, Ns, MEAN_GROUP):
            cols = slice(c, c + MEAN_GROUP)
            x = acc[:, cols]
            mean = jnp.dot((x * x).astype(BF16), gm, preferred_element_type=F32)
            normed = (x * lax.rsqrt(mean + EPS)) * g_ref[:, cols]
            o_ref[:, cols] = jnp.where(first_col + c >= 2 * D_MODEL, x, normed).astype(BF16)

    return pl.pallas_call(
        body, name=name, grid=(J, S // tm),
        in_specs=[pl.BlockSpec((tm, K), lambda j, m: (m, 0)), pl.BlockSpec((None, None, K, Ns), lambda j, m: (j, 0, 0, 0)),
                  pl.BlockSpec((1, Ns), lambda j, m: (0, j)), pl.BlockSpec((MEAN_GROUP, MEAN_GROUP), lambda j, m: (0, 0))],
        out_specs=[pl.BlockSpec((tm, Ns), lambda j, m: (m, j))] * 2,
        out_shape=[_sds((S, J * Ns), F32), _sds((S, J * Ns), BF16)], compiler_params=_params("parallel", "parallel"),
    )(xn, w_qkv, gains, _head_mean_matrix())


def qknorm_bwd(qkv, dq, dk, dv, gqk, name, ts=256):
    S = qkv.shape[0]

    def body(x_ref, dq_ref, dk_ref, dv_ref, g_ref, gm_ref, o_ref, dg_ref):
        @pl.when(pl.program_id(0) == 0)
        def _():
            dg_ref[...] = jnp.zeros_like(dg_ref)

        gm = gm_ref[...]
        for part, d_ref in enumerate((dq_ref, dk_ref)):
            cols = slice(part * D_MODEL, (part + 1) * D_MODEL)
            x = x_ref[:, cols]
            d = d_ref[...]
            r = lax.rsqrt(_head_mean(x * x, gm) + EPS)
            gx = d * g_ref[part]
            o_ref[:, cols] = (r * gx - x * ((r * r * r) * _head_mean(gx * x, gm))).astype(BF16)
            dg_ref[part] += jnp.sum(d * (x * r), axis=0, keepdims=True)
        o_ref[:, 2 * D_MODEL:] = dv_ref[...].astype(BF16)

    row = pl.BlockSpec((ts, D_MODEL), lambda i: (i, 0))
    wide = pl.BlockSpec((ts, 3 * D_MODEL), lambda i: (i, 0))
    gains = pl.BlockSpec((3, 1, D_MODEL), lambda i: (0, 0, 0))
    return pl.pallas_call(
        body, name=name, grid=(S // ts,),
        in_specs=[wide, row, row, row, gains, pl.BlockSpec((MEAN_GROUP, MEAN_GROUP), lambda i: (0, 0))],
        out_specs=[wide, gains],
        out_shape=[_sds((S, 3 * D_MODEL), BF16), _sds((3, 1, D_MODEL), F32)],
        compiler_params=_params("arbitrary"),
    )(qkv, dq, dk, dv, gqk, _head_mean_matrix())


RESIDUES = 16


def _block_order(dil):
    runs = RESIDUES // dil
    slot = np.arange(ATT_BLOCK)
    return (slot % (ATT_BLOCK // runs)) * runs + slot // (ATT_BLOCK // runs)


def _bucket_tables():
    n = ATT_BLOCK
    max_exact = N_REL_BUCKETS // 2
    buckets, valids = [], []
    for _, dil in DILATED_PAIRS:
        order = _block_order(dil)
        a = order[:, None]
        c = np.concatenate([order, n + order])[None, :]
        first_half = (np.arange(2 * n) < n)[None, :]
        rel = a + n - c
        band = (rel >= 0) & (rel <= n)
        dist = np.clip(rel, 0, n) * dil
        dd = np.maximum(dist, 1).astype(np.float32)
        large = max_exact + (np.log(dd / np.float32(max_exact)) / np.float32(math.log(REL_MAX_DISTANCE / max_exact))
                             * np.float32(N_REL_BUCKETS - max_exact)).astype(np.int32)
        large = np.minimum(large, N_REL_BUCKETS - 1)
        buckets.append(np.where(dist < max_exact, dist, large).reshape(1, -1))
        valids.append(np.stack([(band & ~first_half).reshape(1, -1), band.reshape(1, -1)]))
    return np.stack(buckets).astype(np.int32), np.stack(valids).astype(np.int32)


BIAS_CHUNK = 8192


def _split3(x):
    a = x.astype(BF16)
    r = x - a.astype(F32)
    b = r.astype(BF16)
    c = (r - b.astype(F32)).astype(BF16)
    return a, b, c


def bias_expand(rel_bias_t, name):
    bucket, valid = _bucket_tables()
    nq = bucket.shape[-1]

    def body(t_ref, b_ref, v_ref, o_ref):
        onehot = (lax.broadcasted_iota(jnp.int32, (N_REL_BUCKETS, BIAS_CHUNK), 0) == b_ref[...]).astype(BF16)
        acc = None
        for term in _split3(t_ref[...]):
            p = jnp.dot(term, onehot, preferred_element_type=F32)
            acc = p if acc is None else acc + p
        for v in range(2):
            o_ref[v] = jnp.where(v_ref[v] > 0, acc, MASK_VALUE)

    return pl.pallas_call(
        body, name=name, grid=(3, nq // BIAS_CHUNK),
        in_specs=[pl.BlockSpec((N_HEADS, N_REL_BUCKETS), lambda b, c: (0, 0)),
                  pl.BlockSpec((None, 1, BIAS_CHUNK), lambda b, c: (b, 0, c)),
                  pl.BlockSpec((None, 2, 1, BIAS_CHUNK), lambda b, c: (b, 0, 0, c))],
        out_specs=pl.BlockSpec((None, 2, N_HEADS, BIAS_CHUNK), lambda b, c: (b, 0, 0, c)),
        out_shape=_sds((3, 2, N_HEADS, nq), F32), compiler_params=_params("parallel", "parallel"),
    )(rel_bias_t, jnp.asarray(bucket), jnp.asarray(valid))


def bias_reduce(dbias, name):
    bucket, _ = _bucket_tables()
    nq = bucket.shape[-1]
    dims = (((1,), (1,)), ((), ()))

    def body(d_ref, b_ref, o_ref):
        onehot = (lax.broadcasted_iota(jnp.int32, (N_REL_BUCKETS, BIAS_CHUNK), 0) == b_ref[...]).astype(BF16)
        acc = None
        for term in _split3(d_ref[...]):
            p = lax.dot_general(term, onehot, dims, preferred_element_type=F32)
            acc = p if acc is None else acc + p

        @pl.when(pl.program_id(1) == 0)
        def _():
            o_ref[...] = acc

        @pl.when(pl.program_id(1) > 0)
        def _():
            o_ref[...] += acc

    return pl.pallas_call(
        body, name=name, grid=(3, nq // BIAS_CHUNK),
        in_specs=[pl.BlockSpec((None, N_HEADS, BIAS_CHUNK), lambda b, c: (b, 0, c)),
                  pl.BlockSpec((None, 1, BIAS_CHUNK), lambda b, c: (b, 0, c))],
        out_specs=pl.BlockSpec((None, N_HEADS, N_REL_BUCKETS), lambda b, c: (b, 0, 0)),
        out_shape=_sds((3, N_HEADS, N_REL_BUCKETS), F32), compiler_params=_params("parallel", "arbitrary"),
    )(dbias, jnp.asarray(bucket))


PAIR = 2 * HEAD_DIM
N_PAIRS = N_HEADS // 2
_NT = (((1,), (1,)), ((), ()))
_TN = (((0,), (0,)), ((), ()))


def _low_lanes(shape):
    return lax.broadcasted_iota(jnp.int32, shape, 1) < HEAD_DIM


ATTN_VMEM_LIMIT_BYTES = 56 * 1024 * 1024
BRANCH_ORDER = (2, 1, 0)


def _regroup(dst, src, L16):
    for r in range(RESIDUES):
        dst[pl.ds(r * L16, L16), :] = src[pl.ds(r, L16, stride=RESIDUES), :]


def _ungroup(dst, src, L16):
    for r in range(RESIDUES):
        dst[pl.ds(r, L16, stride=RESIDUES), :] = src[pl.ds(r * L16, L16), :]


def _branch_geometry(branch, S):
    dil = DILATED_PAIRS[branch][1]
    runs = RESIDUES // dil
    return dil, runs, ATT_BLOCK // runs, S // dil // ATT_BLOCK


def _block_rows(it, branch, S):
    dil, runs, run_len, n_blocks = _branch_geometry(branch, S)
    L16 = S // RESIDUES
    r, b = it // n_blocks, it % n_blocks
    prev = jnp.maximum(b - 1, 0)
    cur_rows = [pl.multiple_of((j * dil + r) * L16 + run_len * b, 8) for j in range(runs)]
    prev_rows = [pl.multiple_of((j * dil + r) * L16 + run_len * prev, 8) for j in range(runs)]
    return cur_rows, prev_rows, jnp.minimum(b, 1)


def _load_block(ref, rows, run_len):
    parts = [ref[pl.ds(o, run_len), :] for o in rows]
    return parts[0] if len(parts) == 1 else jnp.concatenate(parts, axis=0)


def _store_block(ref, rows, run_len, value, add=False):
    for j, o in enumerate(rows):
        part = value[j * run_len:(j + 1) * run_len]
        if add:
            ref[pl.ds(o, run_len), :] += part
        else:
            ref[pl.ds(o, run_len), :] = part


ATTN_FWD_UNROLL = 8
ATTN_BWD_UNROLL = 4


def _stack_heads(x, low):
    zero = jnp.zeros_like(x)
    return jnp.concatenate([jnp.where(low, x, zero), jnp.where(low, zero, x)], axis=0)


def _unstack_heads(y, low):
    return jnp.where(low, y[:ATT_BLOCK], y[ATT_BLOCK:])


def attn_fwd(qkvn, bias, name):
    S = qkvn.shape[0]
    L16 = S // RESIDUES
    n_iter = S // ATT_BLOCK

    def body(q_ref, k_ref, v_ref, b_ref, o_ref, lse_ref, stage, qp, kp, vp, acc_s, m_s, l_s):
        for src, dst in ((q_ref, qp), (k_ref, kp), (v_ref, vp)):
            stage[...] = src[...].astype(F32)
            _regroup(dst, stage, L16)
        low = _low_lanes((ATT_BLOCK, PAIR))

        for branch in BRANCH_ORDER:
            _, _, run_len, _ = _branch_geometry(branch, S)
            first = branch == BRANCH_ORDER[0]

            def step(it, carry, branch=branch, run_len=run_len, first=first):
                cur, prev, variant = _block_rows(it, branch, S)
                q = _load_block(qp, cur, run_len).astype(BF16)
                k = jnp.concatenate([_load_block(kp, prev, run_len), _load_block(kp, cur, run_len)], axis=0).astype(BF16)
                v = jnp.concatenate([_load_block(vp, prev, run_len), _load_block(vp, cur, run_len)], axis=0).astype(BF16)
                s = lax.dot_general(_stack_heads(q, low), k, _NT, preferred_element_type=F32) * (HEAD_DIM ** -0.5)
                s = s + b_ref[2 * branch + variant].reshape(2 * ATT_BLOCK, 2 * ATT_BLOCK)
                mx = jnp.max(s, axis=-1, keepdims=True)
                p = jnp.exp(s - mx)
                den = jnp.sum(p, axis=-1, keepdims=True)
                pv = jnp.dot(p.astype(BF16), v, preferred_element_type=F32)
                acc = _unstack_heads(pv, low)
                m = _unstack_heads(mx, low)
                l = _unstack_heads(den, low)
                if not first:
                    m_old = _load_block(m_s, cur, run_len)
                    m_new = jnp.maximum(m_old, m)
                    a_old, a_new = jnp.exp(m_old - m_new), jnp.exp(m - m_new)
                    acc = _load_block(acc_s, cur, run_len) * a_old + acc * a_new
                    l = _load_block(l_s, cur, run_len) * a_old + l * a_new
                    m = m_new
                _store_block(acc_s, cur, run_len, acc)
                _store_block(m_s, cur, run_len, m)
                _store_block(l_s, cur, run_len, l)
                return carry

            lax.fori_loop(0, n_iter, step, 0, unroll=ATTN_FWD_UNROLL)

        acc_s[...] = acc_s[...] / l_s[...]
        _ungroup(stage, acc_s, L16)
        o_ref[...] = stage[...].astype(BF16)
        m_s[...] = m_s[...] + jnp.log(l_s[...])
        _ungroup(lse_ref, m_s, L16)

    col = lambda part: pl.BlockSpec((S, PAIR), lambda hp: (0, part * N_PAIRS + hp))
    out = pl.BlockSpec((S, PAIR), lambda hp: (0, hp))
    return pl.pallas_call(
        body, name=name, grid=(N_PAIRS,),
        in_specs=[col(0), col(1), col(2), pl.BlockSpec((6, 2, ATT_BLOCK, 2 * ATT_BLOCK), lambda hp: (0, hp, 0, 0))],
        out_specs=[out, out], out_shape=[_sds((S, D_MODEL), BF16), _sds((S, D_MODEL), F32)],
        scratch_shapes=[pltpu.VMEM((S, PAIR), F32)] * 7,
        compiler_params=pltpu.CompilerParams(dimension_semantics=("parallel",), vmem_limit_bytes=ATTN_VMEM_LIMIT_BYTES),
    )(qkvn, qkvn, qkvn, bias)


def attn_bwd(qkvn, att, datt, lse, bias, name):
    S = qkvn.shape[0]
    L16 = S // RESIDUES
    n_iter = S // ATT_BLOCK
    TILE = 512

    def body(q_ref, k_ref, v_ref, o_ref, do_ref, lse_ref, b_ref, dq_ref, dk_ref, dv_ref, db_ref,
             qp, kp, vp, dop, ldp, dqp, dkp, dvp):
        stage = dqp
        for src, dst in ((q_ref, qp), (k_ref, kp), (v_ref, vp), (do_ref, dop)):
            stage[...] = src[...].astype(F32)
            _regroup(dst, stage, L16)

        def pack(i, carry):
            rows = pl.ds(pl.multiple_of(i * TILE, TILE), TILE)
            low = _low_lanes((TILE, PAIR))
            lane = lax.broadcasted_iota(jnp.int32, (TILE, PAIR), 1)
            prod = do_ref[rows, :].astype(F32) * o_ref[rows, :].astype(F32)
            d0 = jnp.sum(jnp.where(low, prod, 0.0), axis=-1, keepdims=True)
            d1 = jnp.sum(jnp.where(low, 0.0, prod), axis=-1, keepdims=True)
            stage[rows, :] = jnp.where((lane & (HEAD_DIM // 2)) == 0, lse_ref[rows, :], jnp.where(low, d0, d1))
            return carry

        lax.fori_loop(0, S // TILE, pack, 0)
        _regroup(ldp, stage, L16)
        dqp[...] = jnp.zeros_like(dqp)
        dkp[...] = jnp.zeros_like(dkp)
        dvp[...] = jnp.zeros_like(dvp)
        db_ref[...] = jnp.zeros_like(db_ref)
        low = _low_lanes((ATT_BLOCK, PAIR))

        for branch in BRANCH_ORDER:
            _, _, run_len, _ = _branch_geometry(branch, S)

            def step(it, carry, branch=branch, run_len=run_len):
                cur, prev, variant = _block_rows(it, branch, S)
                q = _load_block(qp, cur, run_len).astype(BF16)
                dout = _load_block(dop, cur, run_len).astype(BF16)
                ld = _load_block(ldp, cur, run_len)
                k = jnp.concatenate([_load_block(kp, prev, run_len), _load_block(kp, cur, run_len)], axis=0).astype(BF16)
                v = jnp.concatenate([_load_block(vp, prev, run_len), _load_block(vp, cur, run_len)], axis=0).astype(BF16)
                half = HEAD_DIM // 2
                lse2 = jnp.concatenate([ld[:, 0:1], ld[:, HEAD_DIM:HEAD_DIM + 1]], axis=0)
                delta2 = jnp.concatenate([ld[:, half:half + 1], ld[:, HEAD_DIM + half:HEAD_DIM + half + 1]], axis=0)
                q2, do2 = _stack_heads(q, low), _stack_heads(dout, low)
                s = lax.dot_general(q2, k, _NT, preferred_element_type=F32) * (HEAD_DIM ** -0.5)
                p = jnp.exp(s + b_ref[2 * branch + variant].reshape(2 * ATT_BLOCK, 2 * ATT_BLOCK) - lse2)
                dp = lax.dot_general(do2, v, _NT, preferred_element_type=F32)
                ds = p * (dp - delta2)
                db_ref[branch] += ds.reshape(2, ATT_BLOCK, 2 * ATT_BLOCK)
                dsb = (ds * (HEAD_DIM ** -0.5)).astype(BF16)
                dq = _unstack_heads(jnp.dot(dsb, k, preferred_element_type=F32), low)
                dk = lax.dot_general(dsb, q2, _TN, preferred_element_type=F32)
                dv = lax.dot_general(p.astype(BF16), do2, _TN, preferred_element_type=F32)
                _store_block(dqp, cur, run_len, dq, add=True)
                _store_block(dkp, prev, run_len, dk[:ATT_BLOCK], add=True)
                _store_block(dvp, prev, run_len, dv[:ATT_BLOCK], add=True)
                _store_block(dkp, cur, run_len, dk[ATT_BLOCK:], add=True)
                _store_block(dvp, cur, run_len, dv[ATT_BLOCK:], add=True)
                return carry

            lax.fori_loop(0, n_iter, step, 0, unroll=ATTN_BWD_UNROLL)

        _ungroup(dq_ref, dqp, L16)
        _ungroup(dk_ref, dkp, L16)
        _ungroup(dv_ref, dvp, L16)

    col = lambda part: pl.BlockSpec((S, PAIR), lambda hp: (0, part * N_PAIRS + hp))
    one = pl.BlockSpec((S, PAIR), lambda hp: (0, hp))
    return pl.pallas_call(
        body, name=name, grid=(N_PAIRS,),
        in_specs=[col(0), col(1), col(2), one, one, one,
                  pl.BlockSpec((6, 2, ATT_BLOCK, 2 * ATT_BLOCK), lambda hp: (0, hp, 0, 0))],
        out_specs=[one, one, one, pl.BlockSpec((3, 2, ATT_BLOCK, 2 * ATT_BLOCK), lambda hp: (0, hp, 0, 0))],
        out_shape=[_sds((S, D_MODEL), F32)] * 3 + [_sds((3, N_HEADS, ATT_BLOCK, 2 * ATT_BLOCK), F32)],
        scratch_shapes=[pltpu.VMEM((S, PAIR), F32)] * 8,
        compiler_params=pltpu.CompilerParams(dimension_semantics=("parallel",), vmem_limit_bytes=ATTN_VMEM_LIMIT_BYTES),
    )(qkvn, qkvn, qkvn, att, datt, lse, bias)


def _adamw_step(w_ref, g_ref, m_ref, v_ref, d_ref, nm_ref, nv_ref):
    gv = g_ref[...]
    m2 = ADAM_B1 * m_ref[...] + (1.0 - ADAM_B1) * gv
    v2 = ADAM_B2 * v_ref[...] + (1.0 - ADAM_B2) * (gv * gv)
    m_hat = m2 / (1.0 - ADAM_B1 ** ADAM_STEP)
    v_hat = v2 / (1.0 - ADAM_B2 ** ADAM_STEP)
    d_ref[...] = -ADAM_LR * (m_hat / (jnp.sqrt(v_hat) + ADAM_EPS) + ADAM_WD * w_ref[...])
    nm_ref[...] = m2
    nv_ref[...] = v2


def adamw_small(ws, gs, ms, vs, name):
    n = len(ws)

    def body(*refs):
        groups = [refs[k * n:(k + 1) * n] for k in range(7)]
        for refs_of_one in zip(*groups):
            _adamw_step(*refs_of_one)

    outs = pl.pallas_call(body, name=name, out_shape=[_sds(a.shape, F32) for a in ws] * 3,
                          compiler_params=_params())(*ws, *gs, *ms, *vs)
    return outs[:n], outs[n:2 * n], outs[2 * n:]


def adamw(w, g, m, v, name):
    n, R, C = w.shape

    def body(w_ref, g_ref, m_ref, v_ref, d_ref, nm_ref, nv_ref, go_ref):
        go_ref[...] = g_ref[...]
        _adamw_step(w_ref, g_ref, m_ref, v_ref, d_ref, nm_ref, nv_ref)

    tr = R
    while tr * C * 4 > ELEMENTWISE_BLOCK_BYTES and tr % 16 == 0:
        tr //= 2
    spec = pl.BlockSpec((None, tr, C), lambda i, r: (i, r, 0))
    return pl.pallas_call(
        body, name=name, grid=(n, R // tr), in_specs=[spec] * 4, out_specs=[spec] * 4,
        out_shape=[_sds((n, R, C), F32)] * 4, compiler_params=_params("parallel", "parallel"),
    )(w, g, m, v)


ANY = pl.BlockSpec(memory_space=pl.ANY)


def _coords():
    return lax.axis_index("x"), lax.axis_index("y"), lax.axis_index("c")


def _other_chips(mx, my):
    return [(1 - mx, my), (mx, 1 - my), (1 - mx, 1 - my)]


def _remote(src, dst, send, recv, dev):
    return pltpu.make_async_remote_copy(src_ref=src, dst_ref=dst, send_sem=send, recv_sem=recv, device_id=dev,
                                        device_id_type=MESH)


HBM =pl.BlockSpec(memory_space=pltpu.HBM)
SEM = pl.BlockSpec(memory_space=pltpu.SEMAPHORE)
_SPLIT_COPY = pltpu.CompilerParams(has_side_effects=pltpu.SideEffectType.DATAFLOW_SIDE_EFFECTING)


def _in_hbm(a):
    return pltpu.with_memory_space_constraint(a, pltpu.HBM)


def cast_into_slot(w, layer, chip_core, name, dtype=BF16):
    _, _, hR, C = w.shape

    def body(s_ref, w_ref, o_ref):
        del s_ref
        o_ref[...] = w_ref[...].astype(dtype)

    grid_spec = pltpu.PrefetchScalarGridSpec(
        num_scalar_prefetch=1, grid=(2,),
        in_specs=[pl.BlockSpec((None, None, hR, C), lambda h, s: (layer, h, 0, 0))],
        out_specs=pl.BlockSpec((None, None, hR, C), lambda h, s: (s[0], h, 0, 0)))
    return pl.pallas_call(body, name=name, grid_spec=grid_spec, out_shape=_sds_hbm((N_CHIPS, 2, hR, C), dtype),
                          compiler_params=_params("parallel"))(chip_core, w)


def gather_start(lands, groups, name):
    n = len(lands)
    n_groups = len(groups)

    def body(*refs):
        ins = refs[:n]
        sems = refs[n:n + 2 * n_groups]
        token = refs[-1]
        mx, my, mc = _coords()
        chip = 2 * mx + my
        for g, members in enumerate(groups):
            send, recv = sems[2 * g], sems[2 * g + 1]
            for i, a in enumerate(members):
                mine = ins[a].at[chip, mc]
                for k, (px, py) in enumerate(_other_chips(mx, my)):
                    _remote(mine, mine, send.at[3 * i + k], recv.at[3 * i + k], (px, py, mc)).start()
        token[...] = jnp.zeros_like(token)

    sem_shapes = []
    for members in groups:
        sem_shapes += [pltpu.SemaphoreType.DMA((3 * len(members),))] * 2
    outs = pl.pallas_call(
        body, name=name, in_specs=[HBM] * n,
        out_specs=[SEM] * (2 * n_groups) + [HBM] * n + [pl.BlockSpec(memory_space=pltpu.VMEM)],
        out_shape=sem_shapes + [pltpu.HBM(a.shape, a.dtype) for a in lands] + [_sds((SUBLANES, LANES), F32)],
        input_output_aliases={a: 2 * n_groups + a for a in range(n)}, compiler_params=_SPLIT_COPY,
    )(*[_in_hbm(a) for a in lands])
    sems = [(outs[2 * g], outs[2 * g + 1]) for g in range(n_groups)]
    return sems, list(outs[2 * n_groups:2 * n_groups + n]), outs[-1]


def gather_forward(lands, sems, after, name):
    n = len(lands)

    def body(*refs):
        ins = refs[:n]
        send, recv = refs[n], refs[n + 1]
        fsend, frecv = refs[n + 3], refs[n + 4]
        mx, my, mc = _coords()
        for i in range(n):
            for k, (px, py) in enumerate(_other_chips(mx, my)):
                landed = ins[i].at[2 * px + py, mc]
                cp = _remote(landed, landed, send.at[3 * i + k], recv.at[3 * i + k], (px, py, mc))
                cp.wait_send()
                cp.wait_recv()
                _remote(landed, landed, fsend.at[3 * i + k], frecv.at[3 * i + k], (mx, my, 1 - mc)).start()

    outs = pl.pallas_call(
        body, name=name, in_specs=[HBM] * n + [SEM, SEM, ANY], out_specs=[SEM, SEM] + [HBM] * n,
        out_shape=[pltpu.SemaphoreType.DMA((3 * n,))] * 2 + [pltpu.HBM(a.shape, a.dtype) for a in lands],
        input_output_aliases={a: 2 + a for a in range(n)}, compiler_params=_SPLIT_COPY,
    )(*lands, sems[0], sems[1], after)
    return (outs[0], outs[1]), list(outs[2:])


def gather_wait(lands, sems, after, name):
    n = len(lands)

    def body(*refs):
        ins = refs[:n]
        fsend, frecv = refs[n], refs[n + 1]
        mx, my, mc = _coords()
        for i in range(n):
            for k, (px, py) in enumerate(_other_chips(mx, my)):
                theirs = ins[i].at[2 * px + py, 1 - mc]
                cp = _remote(theirs, theirs, fsend.at[3 * i + k], frecv.at[3 * i + k], (mx, my, 1 - mc))
                cp.wait_send()
                cp.wait_recv()

    outs = pl.pallas_call(
        body, name=name, in_specs=[HBM] * n + [SEM, SEM, ANY], out_specs=[HBM] * n,
        out_shape=[pltpu.HBM(a.shape, a.dtype) for a in lands],
        input_output_aliases={a: a for a in range(n)}, compiler_params=_SPLIT_COPY,
    )(*lands, sems[0], sems[1], after)
    return list(outs)


def _peers(mx, my, mc):
    return [(1 - mx if k & 4 else mx, 1 - my if k & 2 else my, 1 - mc if k & 1 else mc) for k in range(1, N_DEV)]


def devices_start(x, name):
    def body(x_ref, land_ref, send, recv, x_thru, land_thru):
        mx, my, mc = _coords()
        me = 4 * mx + 2 * my + mc
        for k, peer in enumerate(_peers(mx, my, mc)):
            _remote(x_ref, land_ref.at[me], send.at[k], recv.at[k], peer).start()

    land = lax.empty((N_DEV,) + x.shape, x.dtype)
    outs = pl.pallas_call(
        body, name=name, in_specs=[HBM, HBM], out_specs=[SEM, SEM, HBM, HBM],
        out_shape=[pltpu.SemaphoreType.DMA((N_DEV - 1,))] * 2 + [pltpu.HBM(x.shape, x.dtype), pltpu.HBM(land.shape, x.dtype)],
        input_output_aliases={0: 2, 1: 3}, compiler_params=_SPLIT_COPY,
    )(_in_hbm(x), _in_hbm(land))
    return (outs[0], outs[1]), outs[2], outs[3]


def devices_wait(x, land, sems, after, name):
    def body(x_ref, land_ref, send, recv, after_ref, x_thru, land_thru):
        mx, my, mc = _coords()
        for k, (px, py, pc) in enumerate(_peers(mx, my, mc)):
            cp = _remote(x_ref, land_ref.at[4 * px + 2 * py + pc], send.at[k], recv.at[k], (px, py, pc))
            cp.wait_send()
            cp.wait_recv()

    outs = pl.pallas_call(
        body, name=name, in_specs=[HBM, HBM, SEM, SEM, ANY], out_specs=[HBM, HBM],
        out_shape=[pltpu.HBM(x.shape, x.dtype), pltpu.HBM(land.shape, land.dtype)],
        input_output_aliases={0: 0, 1: 1}, compiler_params=_SPLIT_COPY,
    )(x, land, sems[0], sems[1], after)
    return outs[0], outs[1]


def device_sum(land, own, me, name):
    _, R, C = land.shape

    def body(s_ref, l_ref, o_ref_in, o_ref):
        acc = None
        for q in range(N_DEV):
            term = jnp.where(s_ref[0] == q, o_ref_in[...], l_ref[q])
            acc = term if acc is None else acc + term
        o_ref[...] = acc

    grid_spec = pltpu.PrefetchScalarGridSpec(
        num_scalar_prefetch=1, grid=(1,),
        in_specs=[pl.BlockSpec((N_DEV, R, C), lambda i, s: (0, 0, 0)), pl.BlockSpec((R, C), lambda i, s: (0, 0))],
        out_specs=pl.BlockSpec((R, C), lambda i, s: (0, 0)))
    return pl.pallas_call(body, name=name, grid_spec=grid_spec, out_shape=_sds((R, C), F32),
                          compiler_params=_params("arbitrary"))(me, land, own)


def reduce_send(grads, name):
    n = len(grads)

    def body(*refs):
        ins, lands = refs[:n], refs[n:2 * n]
        send, recv = refs[2 * n], refs[2 * n + 1]
        mx, my, mc = _coords()
        me = 4 * mx + 2 * my + mc
        for a in range(n):
            for k, (px, py, pc) in enumerate(_peers(mx, my, mc)):
                _remote(ins[a].at[2 * px + py, pc], lands[a].at[me], send.at[7 * a + k], recv.at[7 * a + k], (px, py, pc)).start()

    lands = [lax.empty((N_DEV,) + g.shape[2:], g.dtype) for g in grads]
    outs = pl.pallas_call(
        body, name=name, in_specs=[HBM] * (2 * n), out_specs=[SEM, SEM] + [HBM] * (2 * n),
        out_shape=[pltpu.SemaphoreType.DMA((7 * n,))] * 2 + [pltpu.HBM(a.shape, a.dtype) for a in grads + lands],
        input_output_aliases={a: 2 + a for a in range(2 * n)}, compiler_params=_SPLIT_COPY,
    )(*[_in_hbm(a) for a in grads + lands])
    return (outs[0], outs[1]), list(outs[2:2 + n]), list(outs[2 + n:])


def reduce_wait(grads, lands, sems, after, name):
    n = len(grads)

    def body(*refs):
        ins, zones = refs[:n], refs[n:2 * n]
        send, recv = refs[2 * n], refs[2 * n + 1]
        mx, my, mc = _coords()
        for a in range(n):
            for k, (px, py, pc) in enumerate(_peers(mx, my, mc)):
                cp = _remote(ins[a].at[2 * px + py, pc], zones[a].at[4 * px + 2 * py + pc], send.at[7 * a + k],
                             recv.at[7 * a + k], (px, py, pc))
                cp.wait_send()
                cp.wait_recv()

    outs = pl.pallas_call(
        body, name=name, in_specs=[HBM] * (2 * n) + [SEM, SEM, ANY], out_specs=[HBM] * (2 * n),
        out_shape=[pltpu.HBM(a.shape, a.dtype) for a in grads + lands],
        input_output_aliases={a: a for a in range(2 * n)}, compiler_params=_SPLIT_COPY,
    )(*grads, *lands, sems[0], sems[1], after)
    return list(outs[:n]), list(outs[n:])


def reduce_sum(land, grad, place, name, into=None, layer=None):
    _, hR, C = land.shape
    tr = hR
    while N_DEV * tr * C * 2 > 3 * ELEMENTWISE_BLOCK_BYTES and tr % 32 == 0:
        tr //= 2

    def body(s_ref, l_ref, g_ref, *rest):
        o_ref = rest[-1]
        own = g_ref[...].astype(F32)
        acc = None
        for q in range(N_DEV):
            term = jnp.where(s_ref[2] == q, own, l_ref[q].astype(F32))
            acc = term if acc is None else acc + term
        o_ref[...] = acc

    in_specs = [pl.BlockSpec((N_DEV, tr, C), lambda i, s: (0, i, 0)),
                pl.BlockSpec((None, None, tr, C), lambda i, s: (s[0], s[1], i, 0))]
    args = [place, _in_hbm(land), _in_hbm(grad)]
    aliases = {}
    if layer is None:
        out_spec = pl.BlockSpec((None, tr, C), lambda i, s: (s[1], i, 0))
        out_shape = _sds_hbm((2, hR, C), F32)
    else:
        out_spec = pl.BlockSpec((None, None, tr, C), lambda i, s: (layer, s[1], i, 0))
        out_shape = _sds_hbm((2, 2, hR, C), F32)
        if into is not None:
            in_specs.append(ANY)
            args.append(into)
            aliases = {3: 0}
    grid_spec = pltpu.PrefetchScalarGridSpec(num_scalar_prefetch=1, grid=(hR // tr,), in_specs=in_specs, out_specs=out_spec)
    return pl.pallas_call(body, name=name, grid_spec=grid_spec, out_shape=out_shape, input_output_aliases=aliases,
                          compiler_params=_params("arbitrary"))(*args)


def join_halves(arrays, name):
    n = len(arrays)
    pieces = [(a, l) for a, arr in enumerate(arrays) for l in (range(arr.shape[0]) if arr.ndim == 4 else [None])]

    def body(*refs):
        ins = refs[:n]
        send, recv = refs[2 * n:]
        mx, my, mc = _coords()

        def half(a, l, h):
            return ins[a].at[h] if l is None else ins[a].at[l, h]

        sends = [_remote(half(a, l, mc), half(a, l, mc), send.at[i], recv.at[i], (mx, my, 1 - mc))
                 for i, (a, l) in enumerate(pieces)]
        for cp in sends:
            cp.start()
        for i, (a, l) in enumerate(pieces):
            theirs = half(a, l, 1 - mc)
            _remote(theirs, theirs, send.at[i], recv.at[i], (mx, my, 1 - mc)).wait_recv()
        for cp in sends:
            cp.wait_send()

    return pl.pallas_call(
        body, name=name, in_specs=[ANY] * n, out_specs=[ANY] * n, out_shape=[_sds(a.shape, a.dtype) for a in arrays],
        input_output_aliases={a: a for a in range(n)},
        scratch_shapes=[pltpu.SemaphoreType.DMA((len(pieces),)), pltpu.SemaphoreType.DMA((len(pieces),))],
    )(*arrays)


LANES = 128
SUBLANES = 8


def _n_rows(shape):
    rows = -(-int(np.prod(shape)) // LANES)
    return -(-rows // SUBLANES) * SUBLANES


def _as_rows(a):
    flat = a.reshape(-1)
    rows = _n_rows(a.shape)
    return jnp.pad(flat, (0, rows * LANES - flat.shape[0])).reshape(rows, LANES)


def _pack(arrays):
    return jnp.concatenate([_as_rows(a) for a in arrays], axis=0)


def _unpack(rows, shapes):
    out, r0 = [], 0
    for s in shapes:
        n = _n_rows(s)
        out.append(rows[r0:r0 + n].reshape(-1)[:int(np.prod(s))].reshape(s))
        r0 += n
    return out


REPLICATED_SMALL = [("rel_bias", (32, 16)), ("even_norm", (1, 1024)), ("even_pool_w", (1, 4, 128, 128)),
                    ("even_pool_scale", (1, 512)), ("odd_q_norm", (1, 64)), ("odd_k_norm", (1, 64)),
                    ("ffn_norm", (2, 1024)), ("ffn_conv_b", (2, 5632))]
SHARDED_SMALL = [("even_conv_w", (1, 3, 128)), ("odd_norm", (1, 256)), ("ffn_conv_w", (2, 3, 1408))]
WEIGHT_ORDER = ["rel_bias", "even_norm", "even_w_in", "even_conv_w", "even_pool_w", "even_pool_scale", "even_w_out",
                "odd_norm", "odd_w_qkv", "odd_q_norm", "odd_k_norm", "odd_w_o", "ffn_norm", "ffn_w_up", "ffn_conv_w",
                "ffn_conv_b", "ffn_w_down"]


def kernel(x, rel_bias, even_norm, even_w_in, even_conv_w, even_pool_w, even_pool_scale, even_w_out, odd_norm, odd_w_qkv, odd_q_norm, odd_k_norm, odd_w_o, ffn_norm, ffn_w_up, ffn_conv_w, ffn_conv_b, ffn_w_down, loss_target, m_rel_bias, m_even_norm, m_even_w_in, m_even_conv_w, m_even_pool_w, m_even_pool_scale, m_even_w_out, m_odd_norm, m_odd_w_qkv, m_odd_q_norm, m_odd_k_norm, m_odd_w_o, m_ffn_norm, m_ffn_w_up, m_ffn_conv_w, m_ffn_conv_b, m_ffn_w_down, v_rel_bias, v_even_norm, v_even_w_in, v_even_conv_w, v_even_pool_w, v_even_pool_scale, v_even_w_out, v_odd_norm, v_odd_w_qkv, v_odd_q_norm, v_odd_k_norm, v_odd_w_o, v_ffn_norm, v_ffn_w_up, v_ffn_conv_w, v_ffn_conv_b, v_ffn_w_down):
    W = dict(rel_bias=rel_bias, even_norm=even_norm, even_w_in=even_w_in, even_conv_w=even_conv_w, even_pool_w=even_pool_w,
             even_pool_scale=even_pool_scale, even_w_out=even_w_out, odd_norm=odd_norm, odd_w_qkv=odd_w_qkv,
             odd_q_norm=odd_q_norm, odd_k_norm=odd_k_norm, odd_w_o=odd_w_o, ffn_norm=ffn_norm, ffn_w_up=ffn_w_up,
             ffn_conv_w=ffn_conv_w, ffn_conv_b=ffn_conv_b, ffn_w_down=ffn_w_down)
    M1 = dict(rel_bias=m_rel_bias, even_norm=m_even_norm, even_w_in=m_even_w_in, even_conv_w=m_even_conv_w,
              even_pool_w=m_even_pool_w, even_pool_scale=m_even_pool_scale, even_w_out=m_even_w_out, odd_norm=m_odd_norm,
              odd_w_qkv=m_odd_w_qkv, odd_q_norm=m_odd_q_norm, odd_k_norm=m_odd_k_norm, odd_w_o=m_odd_w_o,
              ffn_norm=m_ffn_norm, ffn_w_up=m_ffn_w_up, ffn_conv_w=m_ffn_conv_w, ffn_conv_b=m_ffn_conv_b,
              ffn_w_down=m_ffn_w_down)
    M2 = dict(rel_bias=v_rel_bias, even_norm=v_even_norm, even_w_in=v_even_w_in, even_conv_w=v_even_conv_w,
              even_pool_w=v_even_pool_w, even_pool_scale=v_even_pool_scale, even_w_out=v_even_w_out, odd_norm=v_odd_norm,
              odd_w_qkv=v_odd_w_qkv, odd_q_norm=v_odd_q_norm, odd_k_norm=v_odd_k_norm, odd_w_o=v_odd_w_o,
              ffn_norm=v_ffn_norm, ffn_w_up=v_ffn_w_up, ffn_conv_w=v_ffn_conv_w, ffn_conv_b=v_ffn_conv_b,
              ffn_w_down=v_ffn_w_down)
    mx, my, mc = _coords()
    chip = 2 * mx + my
    me = 4 * mx + 2 * my + mc
    place = jnp.stack([chip, mc, me]).astype(jnp.int32)
    xs, target = x[0], loss_target[0]

    def halves(w):
        return w.reshape((w.shape[0], 2, w.shape[-2] // 2, w.shape[-1]))

    small_rows = jnp.pad(_pack([even_conv_w, odd_norm, ffn_conv_w]), ((0, SUBLANES), (0, 0)))
    first = [cast_into_slot(halves(even_w_in), 0, place, "cast_w_in"), cast_into_slot(halves(even_w_out), 0, place, "cast_w_out"),
             cast_into_slot(small_rows.reshape(1, 2, small_rows.shape[0] // 2, LANES), 0, place, "small_into_slot", dtype=F32),
             cast_into_slot(halves(ffn_w_up), 0, place, "cast_w_up0")]
    first_sems, first, token = gather_start(first, [[0, 1, 2], [3]], "gather_start_first")
    even_norm_after_start = even_norm + token[0:1, 0:1]

    def later(a):
        return lax.optimization_barrier((a, token))[0]

    down_f32 = halves(later(ffn_w_down))
    rest = [cast_into_slot(down_f32, 0, place, "cast_w_down0"),
            cast_into_slot(halves(later(odd_w_qkv)), 0, place, "cast_w_qkv"), cast_into_slot(halves(later(odd_w_o)), 0, place, "cast_w_o"),
            cast_into_slot(halves(later(ffn_w_up)), 1, place, "cast_w_up1"), cast_into_slot(down_f32, 1, place, "cast_w_down1")]
    rest_sems, rest, rest_token = gather_start(rest, [[0], [1, 2], [3], [4]], "gather_start_rest")
    group_arrays = [first[:3], [first[3]], [rest[0]], [rest[1], rest[2]], [rest[3]], [rest[4]]]
    group_sems = first_sems + rest_sems

    passing = {}

    def pass_on(group, tag, after, then):
        sems, arrays = gather_forward(group_arrays[group], group_sems[group], after, "gather_forward_" + tag)
        then, arrays = lax.optimization_barrier((then, arrays))
        passing[group] = (sems, arrays)
        return then

    def gathered(group, tag, after):
        sems, arrays = passing.pop(group)
        return gather_wait(arrays, sems, after, "gather_wait_" + tag)

    pool_w = cast_bf16(even_pool_w[0], "cast_pool_w")
    gqk = jnp.stack([jnp.tile(odd_q_norm[0], N_HEADS), jnp.tile(odd_k_norm[0], N_HEADS),
                     jnp.ones((D_MODEL,), F32)])[:, None, :]
    bias = bias_expand(later(rel_bias).T, "bias_expand").reshape(6, N_HEADS, ATT_BLOCK, 2 * ATT_BLOCK)
    xn0 = rmsnorm_fwd(xs, pass_on(0, "even", rest_token, even_norm_after_start), "even_norm")
    got = gathered(0, "even", xn0)
    w_in = got[0].reshape(N_CHIPS, 1, D_MODEL, EVEN_IN // N_CHIPS)
    w_out = got[1].reshape(1, 1, D_MODEL, D_MODEL)
    small = got[2].reshape(N_CHIPS, small_rows.shape[0], LANES)
    conv_w_full = small[:, 0:3].transpose(1, 0, 2).reshape(3, A_WIDTH)
    odd_norm_full = small[:, 8:10].reshape(1, D_MODEL)
    ffn_cw_full = small[:, 16:82].reshape(N_CHIPS, 2, 3, 2 * D_FF // N_CHIPS).transpose(1, 2, 0, 3).reshape(2, 3, 2 * D_FF)

    def ffn_fwd(l, xin, xn):
        up, u, act = up_glu_fwd(xn, w_up[l], ffn_cw_full[l], ffn_conv_b[l:l + 1], f"ffn{l}_up_glu")
        return act, (xin, xn, up, u, act)

    w_up, w_down = [None, None], [None, None]
    proj, mix = in_mixer_fwd(xn0, w_in, conv_w_full, pool_w, even_pool_scale, "even_in_mixer")
    x1, xn1 = mm_res_norm(mix, w_out, xs, ffn_norm[0:1], "even_out")
    pass_on(1, "up0", x1, x1)
    w_up[0] = gathered(1, "up0", x1)[0].reshape(N_CHIPS, 1, D_MODEL, 2 * D_FF // N_CHIPS)
    act0, ffn0 = ffn_fwd(0, x1, pass_on(2, "down0", x1, xn1))
    w_down[0] = gathered(2, "down0", act0)[0].reshape(1, 1, D_FF, D_MODEL)
    x2, xn2 = mm_res_norm(pass_on(3, "odd", act0, act0), w_down[0], x1, odd_norm_full, "ffn0_down")
    got = gathered(3, "odd", x2)
    xn2 = pass_on(5, "down1", x2, pass_on(4, "up1", x2, xn2))
    w_qkv = got[0].reshape(N_CHIPS, 1, D_MODEL, 3 * D_MODEL // N_CHIPS)
    w_o = got[1].reshape(1, 1, D_MODEL, D_MODEL)
    qkv, qkvn = qkv_qknorm_fwd(xn2, w_qkv, gqk, "odd_qkv_qknorm")
    att, lse = attn_fwd(qkvn, bias, "attn_fwd")
    x3, xn3 = mm_res_norm(att, w_o, x2, ffn_norm[1:2], "odd_out")
    w_up[1] = gathered(4, "up1", x3)[0].reshape(N_CHIPS, 1, D_MODEL, 2 * D_FF // N_CHIPS)
    act1, ffn1 = ffn_fwd(1, x3, xn3)
    w_down[1] = gathered(5, "down1", x3)[0].reshape(1, 1, D_FF, D_MODEL)
    dy, dyb, sq = mm_res_loss(act1, w_down[1], x3, target, "ffn1_down_loss")
    loss_part = (0.5 * jnp.sum(sq) * (1.0 / D_MODEL)).reshape(1, 1)

    def ffn_bwd(l, dy, dyb, saved):
        xin, xn, up, u, act = saved
        dw_down = mm_tn(act, dyb, f"ffn{l}_dw_down", J=1, tk=D_FF // 2, tm=1024)
        dact = mm_nt(dyb, w_down[l], f"ffn{l}_dact", tr=D_FF // 2, out_dtype=BF16, tm=1024)
        dup, dcw, dcb = glu_bwd(up, u, dact, ffn_cw_full[l], f"ffn{l}_glu_bwd")
        dw_up = mm_tn(xn, dup, f"ffn{l}_dw_up", J=N_CHIPS, tk=512, tm=1024, jb=2)
        dx, dxb, dg = mm_nt_norm_bwd(dup, w_up[l], xin, ffn_norm[l:l + 1], dy, f"ffn{l}_dx")
        return dx, dxb, (dw_down, dw_up, dcw, dcb, dg)

    def quarters(g):
        return g.reshape(N_CHIPS, 2, g.shape[0] * g.shape[1] // (2 * N_CHIPS), g.shape[-1])

    def reduce_start(grads, tag, then):
        sems, parts, zones = reduce_send([quarters(g) for g in grads], "reduce_send_" + tag)
        then, parts = lax.optimization_barrier((then, parts))
        return (sems, parts, zones), then

    dx3, dx3b, g_ffn1 = ffn_bwd(1, dy, dyb, ffn1)
    red_ffn1, (dx3, dx3b) = reduce_start([g_ffn1[1], g_ffn1[0]], "ffn1", (dx3, dx3b))
    dw_o = mm_tn(att, dx3b, "odd_dw_o", J=1, tk=512, tm=1024)
    datt = mm_nt(dx3b, w_o, "odd_datt", tr=D_MODEL, out_dtype=BF16)
    dq, dk, dv, dbias = attn_bwd(qkvn, att, datt, lse, bias, "attn_bwd")
    dqkv, dgqk = qknorm_bwd(qkv, dq, dk, dv, gqk, "odd_qknorm_bwd")
    dw_qkv = mm_tn(xn2, dqkv, "odd_dw_qkv", J=N_CHIPS, tk=512, tm=1024)
    red_odd, dqkv = reduce_start([dw_qkv, dw_o], "odd", dqkv)
    dx2, dx2b, dg_odd = mm_nt_norm_bwd(dqkv, w_qkv, x2, odd_norm_full, dx3, "odd_dx")
    dx1, dx1b, g_ffn0 = ffn_bwd(0, dx2, dx2b, ffn0)
    red_ffn0, (dx1, dx1b) = reduce_start([g_ffn0[1], g_ffn0[0]], "ffn0", (dx1, dx1b))
    dw_out = mm_tn(mix, dx1b, "even_dw_out", J=1, tk=512, tm=1024)
    dmix = mm_nt(dx1b, w_out, "even_dmix", tr=D_MODEL)
    dproj, dcw_even, dpw, dps = mixer_bwd(proj, dmix, conv_w_full, pool_w, even_pool_scale, "even_mixer_bwd")
    dw_in = mm_tn(xn0, dproj, "even_dw_in", J=N_CHIPS, tk=512, tm=1024)
    grad_x, _, dg_even = mm_nt_norm_bwd(dproj, w_in, xs, even_norm, dx1, "even_dx")
    d_rel = jnp.sum(bias_reduce(dbias.reshape(3, N_HEADS, 2 * ATT_BLOCK * ATT_BLOCK), "bias_reduce"), axis=0).T

    red_even, grad_x = reduce_start([dw_in, dw_out], "even", grad_x)

    dcw_sh = dcw_even.reshape(3, N_CHIPS, A_WIDTH // N_CHIPS).transpose(1, 0, 2)
    don_sh = dg_odd.reshape(N_CHIPS, D_MODEL // N_CHIPS)
    dfcw = jnp.stack([g_ffn0[2], g_ffn1[2]])
    dfcw_sh = dfcw.reshape(2, 3, N_CHIPS, 2 * D_FF // N_CHIPS).transpose(2, 0, 1, 3)
    rep_grads = [d_rel, dg_even, dpw[None], dps, _head_sum(dgqk[0]), _head_sum(dgqk[1]),
                 jnp.concatenate([g_ffn0[4], g_ffn1[4]], axis=0), jnp.concatenate([g_ffn0[3], g_ffn1[3]], axis=0)]
    rep_rows = _pack([loss_part] + rep_grads)
    n_loss = _n_rows(loss_part.shape)
    shard_rows = jnp.concatenate([_pack([dcw_sh[j], don_sh[j], dfcw_sh[j]]) for j in range(N_CHIPS)], axis=0)
    n_rep, n_shard = rep_rows.shape[0], shard_rows.shape[0] // N_CHIPS
    small_sems, small_rows, small_land = devices_start(jnp.concatenate([rep_rows, shard_rows], axis=0), "small_grads_start")
    grad_x, small_rows = lax.optimization_barrier((grad_x, small_rows))

    def reduce_end(red, tag, after):
        sems, parts, zones = red
        parts, zones = reduce_wait(parts, zones, sems, after, "reduce_wait_" + tag)
        return zones, parts

    z_ffn1, p_ffn1 = reduce_end(red_ffn1, "ffn1", grad_x)
    z_odd, p_odd = reduce_end(red_odd, "odd", grad_x)
    r_qkv = reduce_sum(z_odd[0], p_odd[0], place, "reduce_sum_w_qkv")
    r_o = reduce_sum(z_odd[1], p_odd[1], place, "reduce_sum_w_o")
    r_up = reduce_sum(z_ffn1[0], p_ffn1[0], place, "reduce_sum_w_up1", layer=1)
    r_down = reduce_sum(z_ffn1[1], p_ffn1[1], place, "reduce_sum_w_down1", layer=1)
    r_qkv, r_o, r_up, r_down = lax.optimization_barrier((r_qkv, r_o, r_up, r_down))
    z_ffn0, p_ffn0 = reduce_end(red_ffn0, "ffn0", r_down)
    r_up = reduce_sum(z_ffn0[0], p_ffn0[0], place, "reduce_sum_w_up0", into=r_up, layer=0)
    r_down = reduce_sum(z_ffn0[1], p_ffn0[1], place, "reduce_sum_w_down0", into=r_down, layer=0)
    later = ["odd_w_qkv", "odd_w_o", "ffn_w_up", "ffn_w_down"]
    joined = join_halves([r_qkv, r_o, r_up, r_down], "grads_join_late_layers")
    G = {nm: g.reshape(W[nm].shape) for nm, g in zip(later, joined)}

    D_, NM, NV = {}, {}, {}

    def update(nm):
        as3 = lambda a: a.reshape((-1,) + a.shape[-2:])
        outs = adamw(as3(W[nm]), as3(G[nm]), as3(M1[nm]), as3(M2[nm]), "adamw_" + nm)
        D_[nm], NM[nm], NV[nm], G[nm] = [o.reshape(W[nm].shape) for o in outs]

    def all_before(names):
        tied = lax.optimization_barrier([D_[nm] for nm in names])
        for nm, d in zip(names, tied):
            D_[nm] = d
        return tied[0]

    for nm in later:
        update(nm)
    z_even, p_even = reduce_end(red_even, "even", all_before(later))
    joined = join_halves([reduce_sum(z_even[0], p_even[0], place, "reduce_sum_w_in"),
                          reduce_sum(z_even[1], p_even[1], place, "reduce_sum_w_out")], "grads_join_first_layer")
    first = ["even_w_in", "even_w_out"]
    for nm, g in zip(first, joined):
        G[nm] = g.reshape(W[nm].shape)
        update(nm)
    small_rows, small_land = devices_wait(small_rows, small_land, small_sems, all_before(first), "small_grads_wait")
    small_sum = device_sum(small_land, small_rows, place[2:3], "small_grads_sum")
    mine = lax.dynamic_slice_in_dim(small_sum, n_rep + chip * n_shard, n_shard, axis=0)
    loss = small_sum[0, 0]
    g_small = jnp.concatenate([small_sum[n_loss:n_rep], mine], axis=0)
    small_names = [n for n, _ in REPLICATED_SMALL + SHARDED_SMALL]
    small_shapes = [s for _, s in REPLICATED_SMALL + SHARDED_SMALL]
    G.update(dict(zip(small_names, _unpack(g_small, small_shapes))))
    outs = adamw_small(*[[d[n] for n in small_names] for d in (W, G, M1, M2)], "adamw_small")
    for dst, o in zip((D_, NM, NV), outs):
        dst.update(dict(zip(small_names, o)))

    return (loss, grad_x[None], *[G[n] for n in WEIGHT_ORDER], *[D_[n] for n in WEIGHT_ORDER],
            *[NM[n] for n in WEIGHT_ORDER], *[NV[n] for n in WEIGHT_ORDER])


def _head_sum(dg):
    return jnp.sum(dg.reshape(N_HEADS, HEAD_DIM), axis=0, keepdims=True)
```

```python
import functools
import math

import numpy as np
import jax
import jax.numpy as jnp
from jax import lax
from jax.experimental import pallas as pl
from jax.experimental.pallas import tpu as pltpu

F32 = jnp.float32
BF16 = jnp.bfloat16

D_MODEL = 1024
N_HEADS = 16
HEAD_DIM = 64
A_WIDTH = 512
POOL_WINDOWS = (2, 4, 8, 16)
POOL_GROUP = 128
EVEN_IN = 2048
D_FF = 2816
DILATED_PAIRS = ((128, 1), (512, 4), (2048, 16))
ATT_BLOCK = 128
N_REL_BUCKETS = 32
REL_MAX_DISTANCE = 2048
EPS = 1e-6
MASK_VALUE = -1e30
ADAM_LR, ADAM_B1, ADAM_B2, ADAM_EPS, ADAM_WD, ADAM_STEP = 0.001, 0.9, 0.999, 1e-08, 0.01, 10

VMEM_LIMIT_BYTES = 48 * 1024 * 1024
ELEMENTWISE_BLOCK_BYTES = 2 * 1024 * 1024
N_CHIPS = 4
N_DEV = 8
MESH = pl.DeviceIdType.MESH


def _params(*sem):
    return pltpu.CompilerParams(dimension_semantics=sem if sem else None, vmem_limit_bytes=VMEM_LIMIT_BYTES)


def _sds(shape, dtype):
    return jax.ShapeDtypeStruct(tuple(shape), dtype)


def _sds_hbm(shape, dtype):
    return pltpu.HBM(tuple(shape), dtype)


def cast_bf16(x, name, tr=None):
    lead, (R, C) = x.shape[:-2], x.shape[-2:]
    n = int(np.prod(lead)) if lead else 1
    x3 = x.reshape((n, R, C))
    tr = tr or R

    def body(x_ref, o_ref):
        o_ref[...] = x_ref[...].astype(BF16)

    out = pl.pallas_call(
        body, name=name, grid=(n, R // tr),
        in_specs=[pl.BlockSpec((None, tr, C), lambda i, r: (i, r, 0))],
        out_specs=pl.BlockSpec((None, tr, C), lambda i, r: (i, r, 0)),
        out_shape=_sds((n, R, C), BF16), compiler_params=_params("parallel", "parallel"),
    )(x3)
    return out.reshape(lead + (R, C))


def rmsnorm_fwd(x, g, name, ts=512):
    S, Dm = x.shape

    def body(x_ref, g_ref, o_ref):
        xv = x_ref[...]
        r = lax.rsqrt(jnp.mean(xv * xv, axis=-1, keepdims=True) + EPS)
        o_ref[...] = ((xv * r) * g_ref[...]).astype(BF16)

    return pl.pallas_call(
        body, name=name, grid=(S // ts,),
        in_specs=[pl.BlockSpec((ts, Dm), lambda i: (i, 0)), pl.BlockSpec((1, Dm), lambda i: (0, 0))],
        out_specs=pl.BlockSpec((ts, Dm), lambda i: (i, 0)),
        out_shape=_sds((S, Dm), BF16), compiler_params=_params("parallel"),
    )(x, g)


def mm_res_norm(a, w, res, gain, name, tm=1024):
    M, K = a.shape
    Dm = w.shape[-1]

    def body(a_ref, w_ref, r_ref, g_ref, y_ref, yn_ref):
        y = r_ref[...] + jnp.dot(a_ref[...], w_ref[...], preferred_element_type=F32)
        y_ref[...] = y
        r = lax.rsqrt(jnp.mean(y * y, axis=-1, keepdims=True) + EPS)
        yn_ref[...] = ((y * r) * g_ref[...]).astype(BF16)

    row = pl.BlockSpec((tm, Dm), lambda m: (m, 0))
    return pl.pallas_call(
        body, name=name, grid=(M // tm,),
        in_specs=[pl.BlockSpec((tm, K), lambda m: (m, 0)),
                  pl.BlockSpec((None, None, K, Dm), lambda m: (0, 0, 0, 0), pipeline_mode=pl.Buffered(1)),
                  row, pl.BlockSpec((1, Dm), lambda m: (0, 0))],
        out_specs=[row, row], out_shape=[_sds((M, Dm), F32), _sds((M, Dm), BF16)],
        compiler_params=_params("parallel"),
    )(a, w, res, gain)


def mm_res_loss(a, w, res, target, name, tm=512):
    M, K = a.shape
    Dm = w.shape[-1]

    def body(a_ref, w_ref, r_ref, t_ref, d_ref, db_ref, s_ref):
        e = (r_ref[...] + jnp.dot(a_ref[...], w_ref[...], preferred_element_type=F32)) - t_ref[...]
        d = e * (1.0 / Dm)
        d_ref[...] = d
        db_ref[...] = d.astype(BF16)
        part = jnp.sum(e * e, axis=0, keepdims=True)

        @pl.when(pl.program_id(0) == 0)
        def _():
            s_ref[...] = part

        @pl.when(pl.program_id(0) > 0)
        def _():
            s_ref[...] += part

    row = pl.BlockSpec((tm, Dm), lambda m: (m, 0))
    return pl.pallas_call(
        body, name=name, grid=(M // tm,),
        in_specs=[pl.BlockSpec((tm, K), lambda m: (m, 0)),
                  pl.BlockSpec((None, None, K, Dm), lambda m: (0, 0, 0, 0), pipeline_mode=pl.Buffered(1)), row, row],
        out_specs=[row, row, pl.BlockSpec((1, Dm), lambda m: (0, 0))],
        out_shape=[_sds((M, Dm), F32), _sds((M, Dm), BF16), _sds((1, Dm), F32)],
        compiler_params=_params("arbitrary"),
    )(a, w, res, target)


def mm_nt(dy, w, name, tr, layer=0, out_dtype=F32, tm=512):
    M = dy.shape[0]
    J, _, R, Ns = w.shape
    dims = (((1,), (1,)), ((), ()))

    def body(dy_ref, w_ref, o_ref):
        acc = None
        for j in range(J):
            p = lax.dot_general(dy_ref[:, j * Ns:(j + 1) * Ns], w_ref[j], dims, preferred_element_type=F32)
            acc = p if acc is None else acc + p
        o_ref[...] = acc.astype(o_ref.dtype)

    return pl.pallas_call(
        body, name=name, grid=(R // tr, M // tm),
        in_specs=[pl.BlockSpec((tm, J * Ns), lambda r, m: (m, 0)),
                  pl.BlockSpec((J, None, tr, Ns), lambda r, m: (0, layer, r, 0))],
        out_specs=pl.BlockSpec((tm, tr), lambda r, m: (m, r)),
        out_shape=_sds((M, R), out_dtype),
        compiler_params=_params("parallel", "parallel"),
    )(dy, w)


def mm_nt_norm_bwd(dy, w, x, g, dres, name, layer=0, tm=512):
    M = dy.shape[0]
    J, _, Dm, Ns = w.shape
    dims = (((1,), (1,)), ((), ()))

    def body(dy_ref, w_ref, x_ref, g_ref, r_ref, dx_ref, dxb_ref, dg_ref):
        dxn = None
        for j in range(J):
            p = lax.dot_general(dy_ref[:, j * Ns:(j + 1) * Ns], w_ref[j], dims, preferred_element_type=F32)
            dxn = p if dxn is None else dxn + p
        xv = x_ref[...]
        r = lax.rsqrt(jnp.mean(xv * xv, axis=-1, keepdims=True) + EPS)
        gx = dxn * g_ref[...]
        dot = jnp.sum(gx * xv, axis=-1, keepdims=True)
        dx = r_ref[...] + r * gx - xv * ((r * r * r) * (dot * (1.0 / Dm)))
        dx_ref[...] = dx
        dxb_ref[...] = dx.astype(BF16)
        part = jnp.sum(dxn * (xv * r), axis=0, keepdims=True)

        @pl.when(pl.program_id(0) == 0)
        def _():
            dg_ref[...] = part

        @pl.when(pl.program_id(0) > 0)
        def _():
            dg_ref[...] += part

    row = pl.BlockSpec((tm, Dm), lambda m: (m, 0))
    vec = pl.BlockSpec((1, Dm), lambda m: (0, 0))
    return pl.pallas_call(
        body, name=name, grid=(M // tm,),
        in_specs=[pl.BlockSpec((tm, J * Ns), lambda m: (m, 0)),
                  pl.BlockSpec((J, None, Dm, Ns), lambda m: (0, layer, 0, 0), pipeline_mode=pl.Buffered(1)), row, vec, row],
        out_specs=[row, row, vec],
        out_shape=[_sds((M, Dm), F32), _sds((M, Dm), BF16), _sds((1, Dm), F32)],
        compiler_params=_params("arbitrary"),
    )(dy, w, x, g, dres)


def mm_tn(a, dy, name, J, tk, tm=512, jb=None):
    M, K = a.shape
    jb = jb or J
    Ns = dy.shape[1] // J
    N = jb * Ns
    n_m = M // tm
    dims = (((0,), (0,)), ((), ()))

    def body(a_ref, dy_ref, o_ref, acc_ref):
        p = lax.dot_general(a_ref[...], dy_ref[...], dims, preferred_element_type=F32)
        m = pl.program_id(2)

        @pl.when(m == 0)
        def _():
            acc_ref[...] = p

        @pl.when(m > 0)
        def _():
            acc_ref[...] += p

        @pl.when(m == n_m - 1)
        def _():
            for j in range(jb):
                o_ref[j] = acc_ref[:, j * Ns:(j + 1) * Ns].astype(BF16)

    return pl.pallas_call(
        body, name=name, grid=(J // jb, K // tk, n_m),
        in_specs=[pl.BlockSpec((tm, tk), lambda g, k, m: (m, k)), pl.BlockSpec((tm, N), lambda g, k, m: (m, g))],
        out_specs=pl.BlockSpec((jb, tk, Ns), lambda g, k, m: (g, k, 0)),
        out_shape=_sds((J, K, Ns), BF16), scratch_shapes=[pltpu.VMEM((tk, N), F32)],
        compiler_params=_params("parallel", "parallel", "arbitrary"),
    )(a, dy)


HALO = 16


def _shift_down(x, s):
    return pltpu.roll(x, s, 0)


def _shift_up(x, s):
    return pltpu.roll(x, x.shape[0] - s, 0)


def _conv3(z, cw):
    return (_shift_down(z, 2) * cw[0:1] + _shift_down(z, 1) * cw[1:2]) + z * cw[2:3]


def _window_count(first_row, n, k):
    t = first_row + lax.broadcasted_iota(jnp.int32, (n, 1), 0)
    return jnp.clip(t + 1, 1, k).astype(F32)


def in_mixer_fwd(xn, w_in, conv_w, pool_w, pool_scale, name, ts=512):
    S, K = xn.shape
    n = ts + HALO

    def body(xm_ref, xb_ref, w_ref, cw_ref, pw_ref, ps_ref, p_ref, o_ref):
        i = pl.program_id(0)
        before = jnp.where(i > 0, xb_ref[...], jnp.zeros_like(xb_ref))
        rows = jnp.concatenate([before, xm_ref[...]], axis=0)
        h, gb, gc, pin = [jnp.dot(rows, w_ref[j], preferred_element_type=F32) for j in range(N_CHIPS)]
        for j, part in enumerate((h, gb, gc, pin)):
            p_ref[:, j * A_WIDTH:(j + 1) * A_WIDTH] = part[HALO:]
        cz = _conv3(gc * h, cw_ref[...])
        o_ref[:, 0:A_WIDTH] = (gb[HALO:] * cz[HALO:]).astype(BF16)
        for g, k in enumerate(POOL_WINDOWS):
            p = pin[:, g * POOL_GROUP:(g + 1) * POOL_GROUP]
            w = p
            s = 1
            while s < k:
                w = w + _shift_down(w, s)
                s *= 2
            pooled = w / _window_count(i * ts - HALO, n, k) - p
            yb = jnp.dot(pooled[HALO:].astype(BF16), pw_ref[g], preferred_element_type=F32)
            yb = yb * ps_ref[:, g * POOL_GROUP:(g + 1) * POOL_GROUP]
            o_ref[:, A_WIDTH + g * POOL_GROUP:A_WIDTH + (g + 1) * POOL_GROUP] = yb.astype(BF16)

    hb = ts // HALO
    return pl.pallas_call(
        body, name=name, grid=(S // ts,),
        in_specs=[
            pl.BlockSpec((ts, K), lambda i: (i, 0)),
            pl.BlockSpec((HALO, K), lambda i: (jnp.maximum(i * hb - 1, 0), 0)),
            pl.BlockSpec((N_CHIPS, None, K, A_WIDTH), lambda i: (0, 0, 0, 0), pipeline_mode=pl.Buffered(1)),
            pl.BlockSpec((3, A_WIDTH), lambda i: (0, 0)),
            pl.BlockSpec((4, POOL_GROUP, POOL_GROUP), lambda i: (0, 0, 0)),
            pl.BlockSpec((1, 4 * POOL_GROUP), lambda i: (0, 0)),
        ],
        out_specs=[pl.BlockSpec((ts, EVEN_IN), lambda i: (i, 0)), pl.BlockSpec((ts, D_MODEL), lambda i: (i, 0))],
        out_shape=[_sds((S, EVEN_IN), F32), _sds((S, D_MODEL), BF16)], compiler_params=_params("parallel"),
    )(xn, xn, w_in, conv_w, pool_w, pool_scale)


def mixer_bwd(proj, dmix, conv_w, pool_w, pool_scale, name, ts=256):
    S = proj.shape[0]
    n = ts + 2 * HALO
    nt = S // ts
    tn_dims = (((0,), (0,)), ((), ()))
    nt_dims = (((1,), (1,)), ((), ()))

    def body(pm_ref, pb_ref, pa_ref, dm_ref, da_ref, cw_ref, pw_ref, ps_ref, o_ref, dcw_ref, dpw_ref, dps_ref):
        i = pl.program_id(0)
        last = i == nt - 1
        before = jnp.where(i > 0, pb_ref[...], 0.0)
        after = jnp.where(last, 0.0, pa_ref[...])
        ext = jnp.concatenate([before, pm_ref[...], after], axis=0)
        dafter = jnp.where(last, 0.0, da_ref[...])
        dext = jnp.concatenate([jnp.zeros((HALO, D_MODEL), F32), dm_ref[...], dafter], axis=0)
        cw = cw_ref[...]
        main = slice(HALO, HALO + ts)

        @pl.when(i == 0)
        def _():
            dcw_ref[...] = jnp.zeros_like(dcw_ref)
            dpw_ref[...] = jnp.zeros_like(dpw_ref)
            dps_ref[...] = jnp.zeros_like(dps_ref)

        h, gb, gc = ext[:, 0:A_WIDTH], ext[:, A_WIDTH:2 * A_WIDTH], ext[:, 2 * A_WIDTH:3 * A_WIDTH]
        z = gc * h
        z1, z2 = _shift_down(z, 1), _shift_down(z, 2)
        cz = (z2 * cw[0:1] + z1 * cw[1:2]) + z * cw[2:3]
        dya = dext[:, 0:A_WIDTH]
        dcz = dya * gb
        dz = dcz * cw[2:3] + _shift_up(dcz, 1) * cw[1:2] + _shift_up(dcz, 2) * cw[0:1]
        o_ref[:, 0:A_WIDTH] = (dz * gc)[main].astype(BF16)
        o_ref[:, A_WIDTH:2 * A_WIDTH] = (dya * cz)[main].astype(BF16)
        o_ref[:, 2 * A_WIDTH:3 * A_WIDTH] = (dz * h)[main].astype(BF16)
        dczm = dcz[main]
        dcw_ref[0:1, :] += jnp.sum(dczm * z2[main], axis=0, keepdims=True)
        dcw_ref[1:2, :] += jnp.sum(dczm * z1[main], axis=0, keepdims=True)
        dcw_ref[2:3, :] += jnp.sum(dczm * z[main], axis=0, keepdims=True)

        for g, k in enumerate(POOL_WINDOWS):
            lo = 3 * A_WIDTH + g * POOL_GROUP
            cols = slice(g * POOL_GROUP, (g + 1) * POOL_GROUP)
            p = ext[:, lo:lo + POOL_GROUP]
            w = p
            s = 1
            while s < k:
                w = w + _shift_down(w, s)
                s *= 2
            cnt = _window_count(i * ts - HALO, n, k)
            pooled = (w / cnt - p)[main].astype(BF16)
            dyb = dext[:, A_WIDTH + g * POOL_GROUP:A_WIDTH + (g + 1) * POOL_GROUP]
            e = dyb * ps_ref[:, cols]
            pre = jnp.dot(pooled, pw_ref[g], preferred_element_type=F32)
            dps_ref[:, cols] += jnp.sum(dyb[main] * pre, axis=0, keepdims=True)
            dpw_ref[g] += lax.dot_general(pooled, e[main].astype(BF16), tn_dims, preferred_element_type=F32)
            dpooled = lax.dot_general(e.astype(BF16), pw_ref[g], nt_dims, preferred_element_type=F32)
            q = dpooled / cnt
            a = q
            s = 1
            while s < k:
                a = a + _shift_up(a, s)
                s *= 2
            o_ref[:, lo:lo + POOL_GROUP] = (a - dpooled)[main].astype(BF16)

    hb = ts // HALO
    nh = S // HALO
    before_map = lambda i: (jnp.maximum(i * hb - 1, 0), 0)
    after_map = lambda i: (jnp.minimum((i + 1) * hb, nh - 1), 0)
    full = lambda *shape: pl.BlockSpec(shape, lambda i: (0,) * len(shape))
    return pl.pallas_call(
        body, name=name, grid=(nt,),
        in_specs=[
            pl.BlockSpec((ts, EVEN_IN), lambda i: (i, 0)),
            pl.BlockSpec((HALO, EVEN_IN), before_map),
            pl.BlockSpec((HALO, EVEN_IN), after_map),
            pl.BlockSpec((ts, D_MODEL), lambda i: (i, 0)),
            pl.BlockSpec((HALO, D_MODEL), after_map),
            full(3, A_WIDTH), full(4, POOL_GROUP, POOL_GROUP), full(1, 4 * POOL_GROUP),
        ],
        out_specs=[pl.BlockSpec((ts, EVEN_IN), lambda i: (i, 0)), full(3, A_WIDTH), full(4, POOL_GROUP, POOL_GROUP),
                   full(1, 4 * POOL_GROUP)],
        out_shape=[_sds((S, EVEN_IN), BF16), _sds((3, A_WIDTH), F32), _sds((4, POOL_GROUP, POOL_GROUP), F32),
                   _sds((1, 4 * POOL_GROUP), F32)],
        compiler_params=_params("arbitrary"),
    )(proj, proj, proj, dmix, dmix, conv_w, pool_w, pool_scale)


FFN_HALO = 16
FFN_TC = 1408


GLU_CHUNKS = ((0, 512), (512, 512), (1024, 384))


def up_glu_fwd(xn, w_up, conv_w, conv_b, name, tm=512):
    S, K = xn.shape
    nc = D_FF // FFN_TC

    def body(xm_ref, xb_ref, wg_ref, wu_ref, cwg_ref, cwu_ref, cbg_ref, cbu_ref, pg_ref, pu_ref, ug_ref, uu_ref, o_ref):
        before = jnp.where(pl.program_id(1) > 0, xb_ref[...], jnp.zeros_like(xb_ref))
        rows = jnp.concatenate([before, xm_ref[...]], axis=0)
        for lo, width in GLU_CHUNKS:
            cols = slice(lo, lo + width)
            pre_g = jnp.dot(rows, wg_ref[:, cols], preferred_element_type=F32)
            pre_u = jnp.dot(rows, wu_ref[:, cols], preferred_element_type=F32)
            gate = _conv3(pre_g, cwg_ref[:, cols])[FFN_HALO:] + cbg_ref[:, cols]
            upv = _conv3(pre_u, cwu_ref[:, cols])[FFN_HALO:] + cbu_ref[:, cols]
            pg_ref[:, cols] = pre_g[FFN_HALO:].astype(BF16)
            pu_ref[:, cols] = pre_u[FFN_HALO:].astype(BF16)
            ug_ref[:, cols] = gate.astype(BF16)
            uu_ref[:, cols] = upv.astype(BF16)
            o_ref[:, cols] = ((gate * (1.0 / (1.0 + jnp.exp(-gate)))) * upv).astype(BF16)

    hb = tm // FFN_HALO
    wspec = lambda off: pl.BlockSpec((None, None, K, FFN_TC), lambda j, m: (j + off, 0, 0, 0))
    cw = lambda off: pl.BlockSpec((3, FFN_TC), lambda j, m: (0, j + off))
    cb = lambda off: pl.BlockSpec((1, FFN_TC), lambda j, m: (0, j + off))
    out = pl.BlockSpec((tm, FFN_TC), lambda j, m: (m, j))
    pg, pu, ug, uu, act = pl.pallas_call(
        body, name=name, grid=(nc, S // tm),
        in_specs=[pl.BlockSpec((tm, K), lambda j, m: (m, 0)),
                  pl.BlockSpec((FFN_HALO, K), lambda j, m: (jnp.maximum(m * hb - 1, 0), 0)),
                  wspec(0), wspec(nc), cw(0), cw(nc), cb(0), cb(nc)],
        out_specs=[out] * 5, out_shape=[_sds((S, D_FF), BF16)] * 5,
        compiler_params=_params("parallel", "parallel"),
    )(xn, xn, w_up, w_up, conv_w, conv_w, conv_b, conv_b)
    return (pg, pu), (ug, uu), act


def glu_bwd(up, u, da, conv_w, name, ts=256):
    S = up[0].shape[0]
    nc = D_FF // FFN_TC
    nt = S // ts
    W = 2 * D_FF

    def body(xg_ref, xu_ref, gm_ref, ga_ref, um_ref, ua_ref, dm_ref, da_ref, cw_ref, dx_ref, dcw_ref, dcb_ref):
        i = pl.program_id(0)
        last = i == nt - 1

        @pl.when(i == 0)
        def _():
            dcw_ref[...] = jnp.zeros_like(dcw_ref)
            dcb_ref[...] = jnp.zeros_like(dcb_ref)

        def rows(m_ref, a_ref, cols):
            return jnp.concatenate([m_ref[:, cols], a_ref[:, cols]], axis=0).astype(F32)

        def back(d, x, cols):
            cw = cw_ref[:, cols]
            d1, d2 = _shift_up(d, 1), _shift_up(d, 2)
            dx_ref[:, cols] = ((d * cw[2:3] + d1 * cw[1:2]) + d2 * cw[0:1])[:ts].astype(BF16)
            dcb_ref[:, cols] += jnp.sum(d[:ts], axis=0, keepdims=True)
            dcw_ref[0:1, cols] += jnp.sum(d2[:ts] * x, axis=0, keepdims=True)
            dcw_ref[1:2, cols] += jnp.sum(d1[:ts] * x, axis=0, keepdims=True)
            dcw_ref[2:3, cols] += jnp.sum(d[:ts] * x, axis=0, keepdims=True)

        for c in range(nc):
            cols = slice(c * FFN_TC, (c + 1) * FFN_TC)
            ug, uu = rows(gm_ref, ga_ref, cols), rows(um_ref, ua_ref, cols)
            dae = rows(dm_ref, da_ref, cols)
            dae = jnp.where(last & (lax.broadcasted_iota(jnp.int32, dae.shape, 0) >= ts), 0.0, dae)
            sg = 1.0 / (1.0 + jnp.exp(-ug))
            duu = dae * (ug * sg)
            dug = (dae * uu) * (sg * (1.0 + ug * (1.0 - sg)))
            back(dug, xg_ref[:, cols].astype(F32), cols)
            back(duu, xu_ref[:, cols].astype(F32), slice(D_FF + c * FFN_TC, D_FF + (c + 1) * FFN_TC))

    hb = ts // FFN_HALO
    nh = S // FFN_HALO
    after_map = lambda i: (jnp.minimum((i + 1) * hb, nh - 1), 0)
    main = pl.BlockSpec((ts, D_FF), lambda i: (i, 0))
    after = pl.BlockSpec((FFN_HALO, D_FF), after_map)
    return pl.pallas_call(
        body, name=name, grid=(nt,),
        in_specs=[main, main, main, after, main, after, main, after, pl.BlockSpec((3, W), lambda i: (0, 0))],
        out_specs=[pl.BlockSpec((ts, W), lambda i: (i, 0)), pl.BlockSpec((3, W), lambda i: (0, 0)),
                   pl.BlockSpec((1, W), lambda i: (0, 0))],
        out_shape=[_sds((S, W), BF16), _sds((3, W), F32), _sds((1, W), F32)],
        compiler_params=_params("arbitrary"),
    )(up[0], up[1], u[0], u[0], u[1], u[1], da, da, conv_w)


MEAN_GROUP = 256


def _head_mean_matrix():
    h = np.arange(MEAN_GROUP) // HEAD_DIM
    return jnp.asarray((h[:, None] == h[None, :]).astype(np.float32) / HEAD_DIM, dtype=BF16)


def _head_mean(v, gm):
    vb = v.astype(BF16)
    return jnp.concatenate([jnp.dot(vb[:, c:c + MEAN_GROUP], gm, preferred_element_type=F32)
                            for c in range(0, v.shape[1], MEAN_GROUP)], axis=1)


def qkv_qknorm_fwd(xn, w_qkv, gqk, name, tm=1024):
    S, K = xn.shape
    J, _, _, Ns = w_qkv.shape
    gains = gqk.reshape(1, 3 * D_MODEL)

    def body(x_ref, w_ref, g_ref, gm_ref, raw_ref, o_ref):
        first_col = pl.program_id(0) * Ns
        acc = jnp.dot(x_ref[...], w_ref[...], preferred_element_type=F32)
        raw_ref[...] = acc
        gm = gm_ref[...]
        for c in range(0, Ns, MEAN_GROUP):
            cols = slice(c, c + MEAN_GROUP)
            x = acc[:, cols]
            mean = jnp.dot((x * x).astype(BF16), gm, preferred_element_type=F32)
            normed = (x * lax.rsqrt(mean + EPS)) * g_ref[:, cols]
            o_ref[:, cols] = jnp.where(first_col + c >= 2 * D_MODEL, x, normed).astype(BF16)

    return pl.pallas_call(
        body, name=name, grid=(J, S // tm),
        in_specs=[pl.BlockSpec((tm, K), lambda j, m: (m, 0)), pl.BlockSpec((None, None, K, Ns), lambda j, m: (j, 0, 0, 0)),
                  pl.BlockSpec((1, Ns), lambda j, m: (0, j)), pl.BlockSpec((MEAN_GROUP, MEAN_GROUP), lambda j, m: (0, 0))],
        out_specs=[pl.BlockSpec((tm, Ns), lambda j, m: (m, j))] * 2,
        out_shape=[_sds((S, J * Ns), F32), _sds((S, J * Ns), BF16)], compiler_params=_params("parallel", "parallel"),
    )(xn, w_qkv, gains, _head_mean_matrix())


def qknorm_bwd(qkv, dq, dk, dv, gqk, name, ts=256):
    S = qkv.shape[0]

    def body(x_ref, dq_ref, dk_ref, dv_ref, g_ref, gm_ref, o_ref, dg_ref):
        @pl.when(pl.program_id(0) == 0)
        def _():
            dg_ref[...] = jnp.zeros_like(dg_ref)

        gm = gm_ref[...]
        for part, d_ref in enumerate((dq_ref, dk_ref)):
            cols = slice(part * D_MODEL, (part + 1) * D_MODEL)
            x = x_ref[:, cols]
            d = d_ref[...]
            r = lax.rsqrt(_head_mean(x * x, gm) + EPS)
            gx = d * g_ref[part]
            o_ref[:, cols] = (r * gx - x * ((r * r * r) * _head_mean(gx * x, gm))).astype(BF16)
            dg_ref[part] += jnp.sum(d * (x * r), axis=0, keepdims=True)
        o_ref[:, 2 * D_MODEL:] = dv_ref[...].astype(BF16)

    row = pl.BlockSpec((ts, D_MODEL), lambda i: (i, 0))
    wide = pl.BlockSpec((ts, 3 * D_MODEL), lambda i: (i, 0))
    gains = pl.BlockSpec((3, 1, D_MODEL), lambda i: (0, 0, 0))
    return pl.pallas_call(
        body, name=name, grid=(S // ts,),
        in_specs=[wide, row, row, row, gains, pl.BlockSpec((MEAN_GROUP, MEAN_GROUP), lambda i: (0, 0))],
        out_specs=[wide, gains],
        out_shape=[_sds((S, 3 * D_MODEL), BF16), _sds((3, 1, D_MODEL), F32)],
        compiler_params=_params("arbitrary"),
    )(qkv, dq, dk, dv, gqk, _head_mean_matrix())


RESIDUES = 16


def _block_order(dil):
    runs = RESIDUES // dil
    slot = np.arange(ATT_BLOCK)
    return (slot % (ATT_BLOCK // runs)) * runs + slot // (ATT_BLOCK // runs)


def _bucket_tables():
    n = ATT_BLOCK
    max_exact = N_REL_BUCKETS // 2
    buckets, valids = [], []
    for _, dil in DILATED_PAIRS:
        order = _block_order(dil)
        a = order[:, None]
        c = np.concatenate([order, n + order])[None, :]
        first_half = (np.arange(2 * n) < n)[None, :]
        rel = a + n - c
        band = (rel >= 0) & (rel <= n)
        dist = np.clip(rel, 0, n) * dil
        dd = np.maximum(dist, 1).astype(np.float32)
        large = max_exact + (np.log(dd / np.float32(max_exact)) / np.float32(math.log(REL_MAX_DISTANCE / max_exact))
                             * np.float32(N_REL_BUCKETS - max_exact)).astype(np.int32)
        large = np.minimum(large, N_REL_BUCKETS - 1)
        buckets.append(np.where(dist < max_exact, dist, large).reshape(1, -1))
        valids.append(np.stack([(band & ~first_half).reshape(1, -1), band.reshape(1, -1)]))
    return np.stack(buckets).astype(np.int32), np.stack(valids).astype(np.int32)


BIAS_CHUNK = 8192


def _split3(x):
    a = x.astype(BF16)
    r = x - a.astype(F32)
    b = r.astype(BF16)
    c = (r - b.astype(F32)).astype(BF16)
    return a, b, c


def bias_expand(rel_bias_t, name):
    bucket, valid = _bucket_tables()
    nq = bucket.shape[-1]

    def body(t_ref, b_ref, v_ref, o_ref):
        onehot = (lax.broadcasted_iota(jnp.int32, (N_REL_BUCKETS, BIAS_CHUNK), 0) == b_ref[...]).astype(BF16)
        acc = None
        for term in _split3(t_ref[...]):
            p = jnp.dot(term, onehot, preferred_element_type=F32)
            acc = p if acc is None else acc + p
        for v in range(2):
            o_ref[v] = jnp.where(v_ref[v] > 0, acc, MASK_VALUE)

    return pl.pallas_call(
        body, name=name, grid=(3, nq // BIAS_CHUNK),
        in_specs=[pl.BlockSpec((N_HEADS, N_REL_BUCKETS), lambda b, c: (0, 0)),
                  pl.BlockSpec((None, 1, BIAS_CHUNK), lambda b, c: (b, 0, c)),
                  pl.BlockSpec((None, 2, 1, BIAS_CHUNK), lambda b, c: (b, 0, 0, c))],
        out_specs=pl.BlockSpec((None, 2, N_HEADS, BIAS_CHUNK), lambda b, c: (b, 0, 0, c)),
        out_shape=_sds((3, 2, N_HEADS, nq), F32), compiler_params=_params("parallel", "parallel"),
    )(rel_bias_t, jnp.asarray(bucket), jnp.asarray(valid))


def bias_reduce(dbias, name):
    bucket, _ = _bucket_tables()
    nq = bucket.shape[-1]
    dims = (((1,), (1,)), ((), ()))

    def body(d_ref, b_ref, o_ref):
        onehot = (lax.broadcasted_iota(jnp.int32, (N_REL_BUCKETS, BIAS_CHUNK), 0) == b_ref[...]).astype(BF16)
        acc = None
        for term in _split3(d_ref[...]):
            p = lax.dot_general(term, onehot, dims, preferred_element_type=F32)
            acc = p if acc is None else acc + p

        @pl.when(pl.program_id(1) == 0)
        def _():
            o_ref[...] = acc

        @pl.when(pl.program_id(1) > 0)
        def _():
            o_ref[...] += acc

    return pl.pallas_call(
        body, name=name, grid=(3, nq // BIAS_CHUNK),
        in_specs=[pl.BlockSpec((None, N_HEADS, BIAS_CHUNK), lambda b, c: (b, 0, c)),
                  pl.BlockSpec((None, 1, BIAS_CHUNK), lambda b, c: (b, 0, c))],
        out_specs=pl.BlockSpec((None, N_HEADS, N_REL_BUCKETS), lambda b, c: (b, 0, 0)),
        out_shape=_sds((3, N_HEADS, N_REL_BUCKETS), F32), compiler_params=_params("parallel", "arbitrary"),
    )(dbias, jnp.asarray(bucket))


PAIR = 2 * HEAD_DIM
N_PAIRS = N_HEADS // 2
_NT = (((1,), (1,)), ((), ()))
_TN = (((0,), (0,)), ((), ()))


def _low_lanes(shape):
    return lax.broadcasted_iota(jnp.int32, shape, 1) < HEAD_DIM


ATTN_VMEM_LIMIT_BYTES = 56 * 1024 * 1024
BRANCH_ORDER = (2, 1, 0)


def _regroup(dst, src, L16):
    for r in range(RESIDUES):
        dst[pl.ds(r * L16, L16), :] = src[pl.ds(r, L16, stride=RESIDUES), :]


def _ungroup(dst, src, L16):
    for r in range(RESIDUES):
        dst[pl.ds(r, L16, stride=RESIDUES), :] = src[pl.ds(r * L16, L16), :]


def _branch_geometry(branch, S):
    dil = DILATED_PAIRS[branch][1]
    runs = RESIDUES // dil
    return dil, runs, ATT_BLOCK // runs, S // dil // ATT_BLOCK


def _block_rows(it, branch, S):
    dil, runs, run_len, n_blocks = _branch_geometry(branch, S)
    L16 = S // RESIDUES
    r, b = it // n_blocks, it % n_blocks
    prev = jnp.maximum(b - 1, 0)
    cur_rows = [pl.multiple_of((j * dil + r) * L16 + run_len * b, 8) for j in range(runs)]
    prev_rows = [pl.multiple_of((j * dil + r) * L16 + run_len * prev, 8) for j in range(runs)]
    return cur_rows, prev_rows, jnp.minimum(b, 1)


def _load_block(ref, rows, run_len):
    parts = [ref[pl.ds(o, run_len), :] for o in rows]
    return parts[0] if len(parts) == 1 else jnp.concatenate(parts, axis=0)


def _store_block(ref, rows, run_len, value, add=False):
    for j, o in enumerate(rows):
        part = value[j * run_len:(j + 1) * run_len]
        if add:
            ref[pl.ds(o, run_len), :] += part
        else:
            ref[pl.ds(o, run_len), :] = part


ATTN_FWD_UNROLL = 8
ATTN_BWD_UNROLL = 8


def _stack_heads(x, low):
    zero = jnp.zeros_like(x)
    return jnp.concatenate([jnp.where(low, x, zero), jnp.where(low, zero, x)], axis=0)


def _unstack_heads(y, low):
    return jnp.where(low, y[:ATT_BLOCK], y[ATT_BLOCK:])


def attn_fwd(qkvn, bias, name):
    S = qkvn.shape[0]
    L16 = S // RESIDUES
    n_iter = S // ATT_BLOCK

    def body(q_ref, k_ref, v_ref, b_ref, o_ref, lse_ref, stage, qp, kp, vp, acc_s, m_s, l_s):
        for src, dst in ((q_ref, qp), (k_ref, kp), (v_ref, vp)):
            stage[...] = src[...].astype(F32)
            _regroup(dst, stage, L16)
        low = _low_lanes((ATT_BLOCK, PAIR))

        for branch in BRANCH_ORDER:
            _, _, run_len, _ = _branch_geometry(branch, S)
            first = branch == BRANCH_ORDER[0]

            def step(it, carry, branch=branch, run_len=run_len, first=first):
                cur, prev, variant = _block_rows(it, branch, S)
                q = _load_block(qp, cur, run_len).astype(BF16)
                k = jnp.concatenate([_load_block(kp, prev, run_len), _load_block(kp, cur, run_len)], axis=0).astype(BF16)
                v = jnp.concatenate([_load_block(vp, prev, run_len), _load_block(vp, cur, run_len)], axis=0).astype(BF16)
                s = lax.dot_general(_stack_heads(q, low), k, _NT, preferred_element_type=F32) * (HEAD_DIM ** -0.5)
                s = s + b_ref[2 * branch + variant].reshape(2 * ATT_BLOCK, 2 * ATT_BLOCK)
                mx = jnp.max(s, axis=-1, keepdims=True)
                p = jnp.exp(s - mx)
                den = jnp.sum(p, axis=-1, keepdims=True)
                pv = jnp.dot(p.astype(BF16), v, preferred_element_type=F32)
                acc = _unstack_heads(pv, low)
                m = _unstack_heads(mx, low)
                l = _unstack_heads(den, low)
                if not first:
                    m_old = _load_block(m_s, cur, run_len)
                    m_new = jnp.maximum(m_old, m)
                    a_old, a_new = jnp.exp(m_old - m_new), jnp.exp(m - m_new)
                    acc = _load_block(acc_s, cur, run_len) * a_old + acc * a_new
                    l = _load_block(l_s, cur, run_len) * a_old + l * a_new
                    m = m_new
                _store_block(acc_s, cur, run_len, acc)
                _store_block(m_s, cur, run_len, m)
                _store_block(l_s, cur, run_len, l)
                return carry

            lax.fori_loop(0, n_iter, step, 0, unroll=ATTN_FWD_UNROLL)

        acc_s[...] = acc_s[...] / l_s[...]
        _ungroup(stage, acc_s, L16)
        o_ref[...] = stage[...].astype(BF16)
        m_s[...] = m_s[...] + jnp.log(l_s[...])
        _ungroup(lse_ref, m_s, L16)

    col = lambda part: pl.BlockSpec((S, PAIR), lambda hp: (0, part * N_PAIRS + hp))
    out = pl.BlockSpec((S, PAIR), lambda hp: (0, hp))
    return pl.pallas_call(
        body, name=name, grid=(N_PAIRS,),
        in_specs=[col(0), col(1), col(2), pl.BlockSpec((6, 2, ATT_BLOCK, 2 * ATT_BLOCK), lambda hp: (0, hp, 0, 0))],
        out_specs=[out, out], out_shape=[_sds((S, D_MODEL), BF16), _sds((S, D_MODEL), F32)],
        scratch_shapes=[pltpu.VMEM((S, PAIR), F32)] * 7,
        compiler_params=pltpu.CompilerParams(dimension_semantics=("parallel",), vmem_limit_bytes=ATTN_VMEM_LIMIT_BYTES),
    )(qkvn, qkvn, qkvn, bias)


def attn_bwd(qkvn, att, datt, lse, bias, name):
    S = qkvn.shape[0]
    L16 = S // RESIDUES
    n_iter = S // ATT_BLOCK
    TILE = 512

    def body(q_ref, k_ref, v_ref, o_ref, do_ref, lse_ref, b_ref, dq_ref, dk_ref, dv_ref, db_ref,
             qp, kp, vp, dop, ldp, dqp, dkp, dvp):
        stage = dqp
        for src, dst in ((q_ref, qp), (k_ref, kp), (v_ref, vp), (do_ref, dop)):
            stage[...] = src[...].astype(F32)
            _regroup(dst, stage, L16)

        def pack(i, carry):
            rows = pl.ds(pl.multiple_of(i * TILE, TILE), TILE)
            low = _low_lanes((TILE, PAIR))
            lane = lax.broadcasted_iota(jnp.int32, (TILE, PAIR), 1)
            prod = do_ref[rows, :].astype(F32) * o_ref[rows, :].astype(F32)
            d0 = jnp.sum(jnp.where(low, prod, 0.0), axis=-1, keepdims=True)
            d1 = jnp.sum(jnp.where(low, 0.0, prod), axis=-1, keepdims=True)
            stage[rows, :] = jnp.where((lane & (HEAD_DIM // 2)) == 0, lse_ref[rows, :], jnp.where(low, d0, d1))
            return carry

        lax.fori_loop(0, S // TILE, pack, 0)
        _regroup(ldp, stage, L16)
        dqp[...] = jnp.zeros_like(dqp)
        dkp[...] = jnp.zeros_like(dkp)
        dvp[...] = jnp.zeros_like(dvp)
        db_ref[...] = jnp.zeros_like(db_ref)
        low = _low_lanes((ATT_BLOCK, PAIR))

        for branch in BRANCH_ORDER:
            _, _, run_len, _ = _branch_geometry(branch, S)

            def step(it, carry, branch=branch, run_len=run_len):
                cur, prev, variant = _block_rows(it, branch, S)
                q = _load_block(qp, cur, run_len).astype(BF16)
                dout = _load_block(dop, cur, run_len).astype(BF16)
                ld = _load_block(ldp, cur, run_len)
                k = jnp.concatenate([_load_block(kp, prev, run_len), _load_block(kp, cur, run_len)], axis=0).astype(BF16)
                v = jnp.concatenate([_load_block(vp, prev, run_len), _load_block(vp, cur, run_len)], axis=0).astype(BF16)
                half = HEAD_DIM // 2
                lse2 = jnp.concatenate([ld[:, 0:1], ld[:, HEAD_DIM:HEAD_DIM + 1]], axis=0)
                delta2 = jnp.concatenate([ld[:, half:half + 1], ld[:, HEAD_DIM + half:HEAD_DIM + half + 1]], axis=0)
                q2, do2 = _stack_heads(q, low), _stack_heads(dout, low)
                s = lax.dot_general(q2, k, _NT, preferred_element_type=F32) * (HEAD_DIM ** -0.5)
                p = jnp.exp(s + b_ref[2 * branch + variant].reshape(2 * ATT_BLOCK, 2 * ATT_BLOCK) - lse2)
                dp = lax.dot_general(do2, v, _NT, preferred_element_type=F32)
                ds = p * (dp - delta2)
                db_ref[branch] += ds.reshape(2, ATT_BLOCK, 2 * ATT_BLOCK)
                dsb = (ds * (HEAD_DIM ** -0.5)).astype(BF16)
                dq = _unstack_heads(jnp.dot(dsb, k, preferred_element_type=F32), low)
                dk = lax.dot_general(dsb, q2, _TN, preferred_element_type=F32)
                dv = lax.dot_general(p.astype(BF16), do2, _TN, preferred_element_type=F32)
                _store_block(dqp, cur, run_len, dq, add=True)
                _store_block(dkp, prev, run_len, dk[:ATT_BLOCK], add=True)
                _store_block(dvp, prev, run_len, dv[:ATT_BLOCK], add=True)
                _store_block(dkp, cur, run_len, dk[ATT_BLOCK:], add=True)
                _store_block(dvp, cur, run_len, dv[ATT_BLOCK:], add=True)
                return carry

            lax.fori_loop(0, n_iter, step, 0, unroll=ATTN_BWD_UNROLL)

        _ungroup(dq_ref, dqp, L16)
        _ungroup(dk_ref, dkp, L16)
        _ungroup(dv_ref, dvp, L16)

    col = lambda part: pl.BlockSpec((S, PAIR), lambda hp: (0, part * N_PAIRS + hp))
    one = pl.BlockSpec((S, PAIR), lambda hp: (0, hp))
    return pl.pallas_call(
        body, name=name, grid=(N_PAIRS,),
        in_specs=[col(0), col(1), col(2), one, one, one,
                  pl.BlockSpec((6, 2, ATT_BLOCK, 2 * ATT_BLOCK), lambda hp: (0, hp, 0, 0))],
        out_specs=[one, one, one, pl.BlockSpec((3, 2, ATT_BLOCK, 2 * ATT_BLOCK), lambda hp: (0, hp, 0, 0))],
        out_shape=[_sds((S, D_MODEL), F32)] * 3 + [_sds((3, N_HEADS, ATT_BLOCK, 2 * ATT_BLOCK), F32)],
        scratch_shapes=[pltpu.VMEM((S, PAIR), F32)] * 8,
        compiler_params=pltpu.CompilerParams(dimension_semantics=("parallel",), vmem_limit_bytes=ATTN_VMEM_LIMIT_BYTES),
    )(qkvn, qkvn, qkvn, att, datt, lse, bias)


def _adamw_step(w_ref, g_ref, m_ref, v_ref, d_ref, nm_ref, nv_ref):
    gv = g_ref[...]
    m2 = ADAM_B1 * m_ref[...] + (1.0 - ADAM_B1) * gv
    v2 = ADAM_B2 * v_ref[...] + (1.0 - ADAM_B2) * (gv * gv)
    m_hat = m2 / (1.0 - ADAM_B1 ** ADAM_STEP)
    v_hat = v2 / (1.0 - ADAM_B2 ** ADAM_STEP)
    d_ref[...] = -ADAM_LR * (m_hat / (jnp.sqrt(v_hat) + ADAM_EPS) + ADAM_WD * w_ref[...])
    nm_ref[...] = m2
    nv_ref[...] = v2


def adamw_small(ws, gs, ms, vs, name):
    n = len(ws)

    def body(*refs):
        groups = [refs[k * n:(k + 1) * n] for k in range(7)]
        for refs_of_one in zip(*groups):
            _adamw_step(*refs_of_one)

    outs = pl.pallas_call(body, name=name, out_shape=[_sds(a.shape, F32) for a in ws] * 3,
                          compiler_params=_params())(*ws, *gs, *ms, *vs)
    return outs[:n], outs[n:2 * n], outs[2 * n:]


def adamw(w, g, m, v, name):
    n, R, C = w.shape

    def body(w_ref, g_ref, m_ref, v_ref, d_ref, nm_ref, nv_ref, go_ref):
        go_ref[...] = g_ref[...]
        _adamw_step(w_ref, g_ref, m_ref, v_ref, d_ref, nm_ref, nv_ref)

    tr = R
    while tr * C * 4 > ELEMENTWISE_BLOCK_BYTES and tr % 16 == 0:
        tr //= 2
    spec = pl.BlockSpec((None, tr, C), lambda i, r: (i, r, 0))
    return pl.pallas_call(
        body, name=name, grid=(n, R // tr), in_specs=[spec] * 4, out_specs=[spec] * 4,
        out_shape=[_sds((n, R, C), F32)] * 4, compiler_params=_params("parallel", "parallel"),
    )(w, g, m, v)


ANY = pl.BlockSpec(memory_space=pl.ANY)


def _coords():
    return lax.axis_index("x"), lax.axis_index("y"), lax.axis_index("c")


def _other_chips(mx, my):
    return [(1 - mx, my), (mx, 1 - my), (1 - mx, 1 - my)]


def _remote(src, dst, send, recv, dev):
    return pltpu.make_async_remote_copy(src_ref=src, dst_ref=dst, send_sem=send, recv_sem=recv, device_id=dev,
                                        device_id_type=MESH)


HBM =pl.BlockSpec(memory_space=pltpu.HBM)
SEM = pl.BlockSpec(memory_space=pltpu.SEMAPHORE)
_SPLIT_COPY = pltpu.CompilerParams(has_side_effects=pltpu.SideEffectType.DATAFLOW_SIDE_EFFECTING)


def _in_hbm(a):
    return pltpu.with_memory_space_constraint(a, pltpu.HBM)


def cast_into_slot(w, layer, chip_core, name, dtype=BF16):
    _, _, hR, C = w.shape

    def body(s_ref, w_ref, o_ref):
        del s_ref
        o_ref[...] = w_ref[...].astype(dtype)

    grid_spec = pltpu.PrefetchScalarGridSpec(
        num_scalar_prefetch=1, grid=(2,),
        in_specs=[pl.BlockSpec((None, None, hR, C), lambda h, s: (layer, h, 0, 0))],
        out_specs=pl.BlockSpec((None, None, hR, C), lambda h, s: (s[0], h, 0, 0)))
    return pl.pallas_call(body, name=name, grid_spec=grid_spec, out_shape=_sds_hbm((N_CHIPS, 2, hR, C), dtype),
                          compiler_params=_params("parallel"))(chip_core, w)


def gather_start(lands, groups, name):
    n = len(lands)
    n_groups = len(groups)

    def body(*refs):
        ins = refs[:n]
        sems = refs[n:n + 2 * n_groups]
        token = refs[-1]
        mx, my, mc = _coords()
        chip = 2 * mx + my
        for g, members in enumerate(groups):
            send, recv = sems[2 * g], sems[2 * g + 1]
            for i, a in enumerate(members):
                mine = ins[a].at[chip, mc]
                for k, (px, py) in enumerate(_other_chips(mx, my)):
                    _remote(mine, mine, send.at[3 * i + k], recv.at[3 * i + k], (px, py, mc)).start()
        token[...] = jnp.zeros_like(token)

    sem_shapes = []
    for members in groups:
        sem_shapes += [pltpu.SemaphoreType.DMA((3 * len(members),))] * 2
    outs = pl.pallas_call(
        body, name=name, in_specs=[HBM] * n,
        out_specs=[SEM] * (2 * n_groups) + [HBM] * n + [pl.BlockSpec(memory_space=pltpu.VMEM)],
        out_shape=sem_shapes + [pltpu.HBM(a.shape, a.dtype) for a in lands] + [_sds((SUBLANES, LANES), F32)],
        input_output_aliases={a: 2 * n_groups + a for a in range(n)}, compiler_params=_SPLIT_COPY,
    )(*[_in_hbm(a) for a in lands])
    sems = [(outs[2 * g], outs[2 * g + 1]) for g in range(n_groups)]
    return sems, list(outs[2 * n_groups:2 * n_groups + n]), outs[-1]


def gather_forward(lands, sems, after, name):
    n = len(lands)

    def body(*refs):
        ins = refs[:n]
        send, recv = refs[n], refs[n + 1]
        fsend, frecv = refs[n + 3], refs[n + 4]
        mx, my, mc = _coords()
        for i in range(n):
            for k, (px, py) in enumerate(_other_chips(mx, my)):
                landed = ins[i].at[2 * px + py, mc]
                cp = _remote(landed, landed, send.at[3 * i + k], recv.at[3 * i + k], (px, py, mc))
                cp.wait_send()
                cp.wait_recv()
                _remote(landed, landed, fsend.at[3 * i + k], frecv.at[3 * i + k], (mx, my, 1 - mc)).start()

    outs = pl.pallas_call(
        body, name=name, in_specs=[HBM] * n + [SEM, SEM, ANY], out_specs=[SEM, SEM] + [HBM] * n,
        out_shape=[pltpu.SemaphoreType.DMA((3 * n,))] * 2 + [pltpu.HBM(a.shape, a.dtype) for a in lands],
        input_output_aliases={a: 2 + a for a in range(n)}, compiler_params=_SPLIT_COPY,
    )(*lands, sems[0], sems[1], after)
    return (outs[0], outs[1]), list(outs[2:])


def gather_wait(lands, sems, after, name):
    n = len(lands)

    def body(*refs):
        ins = refs[:n]
        fsend, frecv = refs[n], refs[n + 1]
        mx, my, mc = _coords()
        for i in range(n):
            for k, (px, py) in enumerate(_other_chips(mx, my)):
                theirs = ins[i].at[2 * px + py, 1 - mc]
                cp = _remote(theirs, theirs, fsend.at[3 * i + k], frecv.at[3 * i + k], (mx, my, 1 - mc))
                cp.wait_send()
                cp.wait_recv()

    outs = pl.pallas_call(
        body, name=name, in_specs=[HBM] * n + [SEM, SEM, ANY], out_specs=[HBM] * n,
        out_shape=[pltpu.HBM(a.shape, a.dtype) for a in lands],
        input_output_aliases={a: a for a in range(n)}, compiler_params=_SPLIT_COPY,
    )(*lands, sems[0], sems[1], after)
    return list(outs)


def _peers(mx, my, mc):
    return [(1 - mx if k & 4 else mx, 1 - my if k & 2 else my, 1 - mc if k & 1 else mc) for k in range(1, N_DEV)]


def devices_start(x, name):
    def body(x_ref, land_ref, send, recv, x_thru, land_thru):
        mx, my, mc = _coords()
        me = 4 * mx + 2 * my + mc
        for k, peer in enumerate(_peers(mx, my, mc)):
            _remote(x_ref, land_ref.at[me], send.at[k], recv.at[k], peer).start()

    land = lax.empty((N_DEV,) + x.shape, x.dtype)
    outs = pl.pallas_call(
        body, name=name, in_specs=[HBM, HBM], out_specs=[SEM, SEM, HBM, HBM],
        out_shape=[pltpu.SemaphoreType.DMA((N_DEV - 1,))] * 2 + [pltpu.HBM(x.shape, x.dtype), pltpu.HBM(land.shape, x.dtype)],
        input_output_aliases={0: 2, 1: 3}, compiler_params=_SPLIT_COPY,
    )(_in_hbm(x), _in_hbm(land))
    return (outs[0], outs[1]), outs[2], outs[3]


def devices_wait(x, land, sems, after, name):
    def body(x_ref, land_ref, send, recv, after_ref, x_thru, land_thru):
        mx, my, mc = _coords()
        for k, (px, py, pc) in enumerate(_peers(mx, my, mc)):
            cp = _remote(x_ref, land_ref.at[4 * px + 2 * py + pc], send.at[k], recv.at[k], (px, py, pc))
            cp.wait_send()
            cp.wait_recv()

    outs = pl.pallas_call(
        body, name=name, in_specs=[HBM, HBM, SEM, SEM, ANY], out_specs=[HBM, HBM],
        out_shape=[pltpu.HBM(x.shape, x.dtype), pltpu.HBM(land.shape, land.dtype)],
        input_output_aliases={0: 0, 1: 1}, compiler_params=_SPLIT_COPY,
    )(x, land, sems[0], sems[1], after)
    return outs[0], outs[1]


def device_sum(land, own, me, name):
    _, R, C = land.shape

    def body(s_ref, l_ref, o_ref_in, o_ref):
        acc = None
        for q in range(N_DEV):
            term = jnp.where(s_ref[0] == q, o_ref_in[...], l_ref[q])
            acc = term if acc is None else acc + term
        o_ref[...] = acc

    grid_spec = pltpu.PrefetchScalarGridSpec(
        num_scalar_prefetch=1, grid=(1,),
        in_specs=[pl.BlockSpec((N_DEV, R, C), lambda i, s: (0, 0, 0)), pl.BlockSpec((R, C), lambda i, s: (0, 0))],
        out_specs=pl.BlockSpec((R, C), lambda i, s: (0, 0)))
    return pl.pallas_call(body, name=name, grid_spec=grid_spec, out_shape=_sds((R, C), F32),
                          compiler_params=_params("arbitrary"))(me, land, own)


def reduce_send(grads, name):
    n = len(grads)

    def body(*refs):
        ins, lands = refs[:n], refs[n:2 * n]
        send, recv = refs[2 * n], refs[2 * n + 1]
        mx, my, mc = _coords()
        me = 4 * mx + 2 * my + mc
        for a in range(n):
            for k, (px, py, pc) in enumerate(_peers(mx, my, mc)):
                _remote(ins[a].at[2 * px + py, pc], lands[a].at[me], send.at[7 * a + k], recv.at[7 * a + k], (px, py, pc)).start()

    lands = [lax.empty((N_DEV,) + g.shape[2:], g.dtype) for g in grads]
    outs = pl.pallas_call(
        body, name=name, in_specs=[HBM] * (2 * n), out_specs=[SEM, SEM] + [HBM] * (2 * n),
        out_shape=[pltpu.SemaphoreType.DMA((7 * n,))] * 2 + [pltpu.HBM(a.shape, a.dtype) for a in grads + lands],
        input_output_aliases={a: 2 + a for a in range(2 * n)}, compiler_params=_SPLIT_COPY,
    )(*[_in_hbm(a) for a in grads + lands])
    return (outs[0], outs[1]), list(outs[2:2 + n]), list(outs[2 + n:])


def reduce_wait(grads, lands, sems, after, name):
    n = len(grads)

    def body(*refs):
        ins, zones = refs[:n], refs[n:2 * n]
        send, recv = refs[2 * n], refs[2 * n + 1]
        mx, my, mc = _coords()
        for a in range(n):
            for k, (px, py, pc) in enumerate(_peers(mx, my, mc)):
                cp = _remote(ins[a].at[2 * px + py, pc], zones[a].at[4 * px + 2 * py + pc], send.at[7 * a + k],
                             recv.at[7 * a + k], (px, py, pc))
                cp.wait_send()
                cp.wait_recv()

    outs = pl.pallas_call(
        body, name=name, in_specs=[HBM] * (2 * n) + [SEM, SEM, ANY], out_specs=[HBM] * (2 * n),
        out_shape=[pltpu.HBM(a.shape, a.dtype) for a in grads + lands],
        input_output_aliases={a: a for a in range(2 * n)}, compiler_params=_SPLIT_COPY,
    )(*grads, *lands, sems[0], sems[1], after)
    return list(outs[:n]), list(outs[n:])


def reduce_sum(land, grad, place, name, into=None, layer=None):
    _, hR, C = land.shape
    tr = hR
    while N_DEV * tr * C * 2 > 3 * ELEMENTWISE_BLOCK_BYTES and tr % 32 == 0:
        tr //= 2

    def body(s_ref, l_ref, g_ref, *rest):
        o_ref = rest[-1]
        own = g_ref[...].astype(F32)
        acc = None
        for q in range(N_DEV):
            term = jnp.where(s_ref[2] == q, own, l_ref[q].astype(F32))
            acc = term if acc is None else acc + term
        o_ref[...] = acc

    in_specs = [pl.BlockSpec((N_DEV, tr, C), lambda i, s: (0, i, 0)),
                pl.BlockSpec((None, None, tr, C), lambda i, s: (s[0], s[1], i, 0))]
    args = [place, _in_hbm(land), _in_hbm(grad)]
    aliases = {}
    if layer is None:
        out_spec = pl.BlockSpec((None, tr, C), lambda i, s: (s[1], i, 0))
        out_shape = _sds_hbm((2, hR, C), F32)
    else:
        out_spec = pl.BlockSpec((None, None, tr, C), lambda i, s: (layer, s[1], i, 0))
        out_shape = _sds_hbm((2, 2, hR, C), F32)
        if into is not None:
            in_specs.append(ANY)
            args.append(into)
            aliases = {3: 0}
    grid_spec = pltpu.PrefetchScalarGridSpec(num_scalar_prefetch=1, grid=(hR // tr,), in_specs=in_specs, out_specs=out_spec)
    return pl.pallas_call(body, name=name, grid_spec=grid_spec, out_shape=out_shape, input_output_aliases=aliases,
                          compiler_params=_params("arbitrary"))(*args)


def join_halves(arrays, name):
    n = len(arrays)
    pieces = [(a, l) for a, arr in enumerate(arrays) for l in (range(arr.shape[0]) if arr.ndim == 4 else [None])]

    def body(*refs):
        ins = refs[:n]
        send, recv = refs[2 * n:]
        mx, my, mc = _coords()

        def half(a, l, h):
            return ins[a].at[h] if l is None else ins[a].at[l, h]

        sends = [_remote(half(a, l, mc), half(a, l, mc), send.at[i], recv.at[i], (mx, my, 1 - mc))
                 for i, (a, l) in enumerate(pieces)]
        for cp in sends:
            cp.start()
        for i, (a, l) in enumerate(pieces):
            theirs = half(a, l, 1 - mc)
            _remote(theirs, theirs, send.at[i], recv.at[i], (mx, my, 1 - mc)).wait_recv()
        for cp in sends:
            cp.wait_send()

    return pl.pallas_call(
        body, name=name, in_specs=[ANY] * n, out_specs=[ANY] * n, out_shape=[_sds(a.shape, a.dtype) for a in arrays],
        input_output_aliases={a: a for a in range(n)},
        scratch_shapes=[pltpu.SemaphoreType.DMA((len(pieces),)), pltpu.SemaphoreType.DMA((len(pieces),))],
    )(*arrays)


LANES = 128
SUBLANES = 8


def _n_rows(shape):
    rows = -(-int(np.prod(shape)) // LANES)
    return -(-rows // SUBLANES) * SUBLANES


def _as_rows(a):
    flat = a.reshape(-1)
    rows = _n_rows(a.shape)
    return jnp.pad(flat, (0, rows * LANES - flat.shape[0])).reshape(rows, LANES)


def _pack(arrays):
    return jnp.concatenate([_as_rows(a) for a in arrays], axis=0)


def _unpack(rows, shapes):
    out, r0 = [], 0
    for s in shapes:
        n = _n_rows(s)
        out.append(rows[r0:r0 + n].reshape(-1)[:int(np.prod(s))].reshape(s))
        r0 += n
    return out


REPLICATED_SMALL = [("rel_bias", (32, 16)), ("even_norm", (1, 1024)), ("even_pool_w", (1, 4, 128, 128)),
                    ("even_pool_scale", (1, 512)), ("odd_q_norm", (1, 64)), ("odd_k_norm", (1, 64)),
                    ("ffn_norm", (2, 1024)), ("ffn_conv_b", (2, 5632))]
SHARDED_SMALL = [("even_conv_w", (1, 3, 128)), ("odd_norm", (1, 256)), ("ffn_conv_w", (2, 3, 1408))]
WEIGHT_ORDER = ["rel_bias", "even_norm", "even_w_in", "even_conv_w", "even_pool_w", "even_pool_scale", "even_w_out",
                "odd_norm", "odd_w_qkv", "odd_q_norm", "odd_k_norm", "odd_w_o", "ffn_norm", "ffn_w_up", "ffn_conv_w",
                "ffn_conv_b", "ffn_w_down"]


def kernel(x, rel_bias, even_norm, even_w_in, even_conv_w, even_pool_w, even_pool_scale, even_w_out, odd_norm, odd_w_qkv, odd_q_norm, odd_k_norm, odd_w_o, ffn_norm, ffn_w_up, ffn_conv_w, ffn_conv_b, ffn_w_down, loss_target, m_rel_bias, m_even_norm, m_even_w_in, m_even_conv_w, m_even_pool_w, m_even_pool_scale, m_even_w_out, m_odd_norm, m_odd_w_qkv, m_odd_q_norm, m_odd_k_norm, m_odd_w_o, m_ffn_norm, m_ffn_w_up, m_ffn_conv_w, m_ffn_conv_b, m_ffn_w_down, v_rel_bias, v_even_norm, v_even_w_in, v_even_conv_w, v_even_pool_w, v_even_pool_scale, v_even_w_out, v_odd_norm, v_odd_w_qkv, v_odd_q_norm, v_odd_k_norm, v_odd_w_o, v_ffn_norm, v_ffn_w_up, v_ffn_conv_w, v_ffn_conv_b, v_ffn_w_down):
    W = dict(rel_bias=rel_bias, even_norm=even_norm, even_w_in=even_w_in, even_conv_w=even_conv_w, even_pool_w=even_pool_w,
             even_pool_scale=even_pool_scale, even_w_out=even_w_out, odd_norm=odd_norm, odd_w_qkv=odd_w_qkv,
             odd_q_norm=odd_q_norm, odd_k_norm=odd_k_norm, odd_w_o=odd_w_o, ffn_norm=ffn_norm, ffn_w_up=ffn_w_up,
             ffn_conv_w=ffn_conv_w, ffn_conv_b=ffn_conv_b, ffn_w_down=ffn_w_down)
    M1 = dict(rel_bias=m_rel_bias, even_norm=m_even_norm, even_w_in=m_even_w_in, even_conv_w=m_even_conv_w,
              even_pool_w=m_even_pool_w, even_pool_scale=m_even_pool_scale, even_w_out=m_even_w_out, odd_norm=m_odd_norm,
              odd_w_qkv=m_odd_w_qkv, odd_q_norm=m_odd_q_norm, odd_k_norm=m_odd_k_norm, odd_w_o=m_odd_w_o,
              ffn_norm=m_ffn_norm, ffn_w_up=m_ffn_w_up, ffn_conv_w=m_ffn_conv_w, ffn_conv_b=m_ffn_conv_b,
              ffn_w_down=m_ffn_w_down)
    M2 = dict(rel_bias=v_rel_bias, even_norm=v_even_norm, even_w_in=v_even_w_in, even_conv_w=v_even_conv_w,
              even_pool_w=v_even_pool_w, even_pool_scale=v_even_pool_scale, even_w_out=v_even_w_out, odd_norm=v_odd_norm,
              odd_w_qkv=v_odd_w_qkv, odd_q_norm=v_odd_q_norm, odd_k_norm=v_odd_k_norm, odd_w_o=v_odd_w_o,
              ffn_norm=v_ffn_norm, ffn_w_up=v_ffn_w_up, ffn_conv_w=v_ffn_conv_w, ffn_conv_b=v_ffn_conv_b,
              ffn_w_down=v_ffn_w_down)
    mx, my, mc = _coords()
    chip = 2 * mx + my
    me = 4 * mx + 2 * my + mc
    place = jnp.stack([chip, mc, me]).astype(jnp.int32)
    xs, target = x[0], loss_target[0]

    def halves(w):
        return w.reshape((w.shape[0], 2, w.shape[-2] // 2, w.shape[-1]))

    small_rows = jnp.pad(_pack([even_conv_w, odd_norm, ffn_conv_w]), ((0, SUBLANES), (0, 0)))
    first = [cast_into_slot(halves(even_w_in), 0, place, "cast_w_in"), cast_into_slot(halves(even_w_out), 0, place, "cast_w_out"),
             cast_into_slot(small_rows.reshape(1, 2, small_rows.shape[0] // 2, LANES), 0, place, "small_into_slot", dtype=F32),
             cast_into_slot(halves(ffn_w_up), 0, place, "cast_w_up0")]
    first_sems, first, token = gather_start(first, [[0, 1, 2], [3]], "gather_start_first")
    even_norm_after_start = even_norm + token[0:1, 0:1]

    def later(a):
        return lax.optimization_barrier((a, token))[0]

    down_f32 = halves(later(ffn_w_down))
    rest = [cast_into_slot(down_f32, 0, place, "cast_w_down0"),
            cast_into_slot(halves(later(odd_w_qkv)), 0, place, "cast_w_qkv"), cast_into_slot(halves(later(odd_w_o)), 0, place, "cast_w_o"),
            cast_into_slot(halves(later(ffn_w_up)), 1, place, "cast_w_up1"), cast_into_slot(down_f32, 1, place, "cast_w_down1")]
    rest_sems, rest, rest_token = gather_start(rest, [[0], [1, 2], [3], [4]], "gather_start_rest")
    group_arrays = [first[:3], [first[3]], [rest[0]], [rest[1], rest[2]], [rest[3]], [rest[4]]]
    group_sems = first_sems + rest_sems

    passing = {}

    def pass_on(group, tag, after, then):
        sems, arrays = gather_forward(group_arrays[group], group_sems[group], after, "gather_forward_" + tag)
        then, arrays = lax.optimization_barrier((then, arrays))
        passing[group] = (sems, arrays)
        return then

    def gathered(group, tag, after):
        sems, arrays = passing.pop(group)
        return gather_wait(arrays, sems, after, "gather_wait_" + tag)

    pool_w = cast_bf16(even_pool_w[0], "cast_pool_w")
    gqk = jnp.stack([jnp.tile(odd_q_norm[0], N_HEADS), jnp.tile(odd_k_norm[0], N_HEADS),
                     jnp.ones((D_MODEL,), F32)])[:, None, :]
    bias = bias_expand(later(rel_bias).T, "bias_expand").reshape(6, N_HEADS, ATT_BLOCK, 2 * ATT_BLOCK)
    xn0 = rmsnorm_fwd(xs, pass_on(0, "even", rest_token, even_norm_after_start), "even_norm")
    got = gathered(0, "even", xn0)
    w_in = got[0].reshape(N_CHIPS, 1, D_MODEL, EVEN_IN // N_CHIPS)
    w_out = got[1].reshape(1, 1, D_MODEL, D_MODEL)
    small = got[2].reshape(N_CHIPS, small_rows.shape[0], LANES)
    conv_w_full = small[:, 0:3].transpose(1, 0, 2).reshape(3, A_WIDTH)
    odd_norm_full = small[:, 8:10].reshape(1, D_MODEL)
    ffn_cw_full = small[:, 16:82].reshape(N_CHIPS, 2, 3, 2 * D_FF // N_CHIPS).transpose(1, 2, 0, 3).reshape(2, 3, 2 * D_FF)

    def ffn_fwd(l, xin, xn):
        up, u, act = up_glu_fwd(xn, w_up[l], ffn_cw_full[l], ffn_conv_b[l:l + 1], f"ffn{l}_up_glu")
        return act, (xin, xn, up, u, act)

    w_up, w_down = [None, None], [None, None]
    proj, mix = in_mixer_fwd(xn0, w_in, conv_w_full, pool_w, even_pool_scale, "even_in_mixer")
    x1, xn1 = mm_res_norm(mix, w_out, xs, ffn_norm[0:1], "even_out")
    pass_on(1, "up0", x1, x1)
    w_up[0] = gathered(1, "up0", x1)[0].reshape(N_CHIPS, 1, D_MODEL, 2 * D_FF // N_CHIPS)
    act0, ffn0 = ffn_fwd(0, x1, pass_on(2, "down0", x1, xn1))
    w_down[0] = gathered(2, "down0", act0)[0].reshape(1, 1, D_FF, D_MODEL)
    x2, xn2 = mm_res_norm(pass_on(3, "odd", act0, act0), w_down[0], x1, odd_norm_full, "ffn0_down")
    got = gathered(3, "odd", x2)
    xn2 = pass_on(5, "down1", x2, pass_on(4, "up1", x2, xn2))
    w_qkv = got[0].reshape(N_CHIPS, 1, D_MODEL, 3 * D_MODEL // N_CHIPS)
    w_o = got[1].reshape(1, 1, D_MODEL, D_MODEL)
    qkv, qkvn = qkv_qknorm_fwd(xn2, w_qkv, gqk, "odd_qkv_qknorm")
    att, lse = attn_fwd(qkvn, bias, "attn_fwd")
    x3, xn3 = mm_res_norm(att, w_o, x2, ffn_norm[1:2], "odd_out")
    w_up[1] = gathered(4, "up1", x3)[0].reshape(N_CHIPS, 1, D_MODEL, 2 * D_FF // N_CHIPS)
    act1, ffn1 = ffn_fwd(1, x3, xn3)
    w_down[1] = gathered(5, "down1", x3)[0].reshape(1, 1, D_FF, D_MODEL)
    dy, dyb, sq = mm_res_loss(act1, w_down[1], x3, target, "ffn1_down_loss")
    loss_part = (0.5 * jnp.sum(sq) * (1.0 / D_MODEL)).reshape(1, 1)

    def ffn_bwd(l, dy, dyb, saved):
        xin, xn, up, u, act = saved
        dw_down = mm_tn(act, dyb, f"ffn{l}_dw_down", J=1, tk=D_FF // 2, tm=1024)
        dact = mm_nt(dyb, w_down[l], f"ffn{l}_dact", tr=D_FF // 2, out_dtype=BF16, tm=1024)
        dup, dcw, dcb = glu_bwd(up, u, dact, ffn_cw_full[l], f"ffn{l}_glu_bwd")
        dw_up = mm_tn(xn, dup, f"ffn{l}_dw_up", J=N_CHIPS, tk=512, tm=1024, jb=2)
        dx, dxb, dg = mm_nt_norm_bwd(dup, w_up[l], xin, ffn_norm[l:l + 1], dy, f"ffn{l}_dx")
        return dx, dxb, (dw_down, dw_up, dcw, dcb, dg)

    def quarters(g):
        return g.reshape(N_CHIPS, 2, g.shape[0] * g.shape[1] // (2 * N_CHIPS), g.shape[-1])

    def reduce_start(grads, tag, then):
        sems, parts, zones = reduce_send([quarters(g) for g in grads], "reduce_send_" + tag)
        then, parts = lax.optimization_barrier((then, parts))
        return (sems, parts, zones), then

    dx3, dx3b, g_ffn1 = ffn_bwd(1, dy, dyb, ffn1)
    red_ffn1, (dx3, dx3b) = reduce_start([g_ffn1[1], g_ffn1[0]], "ffn1", (dx3, dx3b))
    dw_o = mm_tn(att, dx3b, "odd_dw_o", J=1, tk=512, tm=1024)
    datt = mm_nt(dx3b, w_o, "odd_datt", tr=D_MODEL, out_dtype=BF16)
    dq, dk, dv, dbias = attn_bwd(qkvn, att, datt, lse, bias, "attn_bwd")
    dqkv, dgqk = qknorm_bwd(qkv, dq, dk, dv, gqk, "odd_qknorm_bwd")
    dw_qkv = mm_tn(xn2, dqkv, "odd_dw_qkv", J=N_CHIPS, tk=512, tm=1024)
    red_odd, dqkv = reduce_start([dw_qkv, dw_o], "odd", dqkv)
    dx2, dx2b, dg_odd = mm_nt_norm_bwd(dqkv, w_qkv, x2, odd_norm_full, dx3, "odd_dx")
    dx1, dx1b, g_ffn0 = ffn_bwd(0, dx2, dx2b, ffn0)
    red_ffn0, (dx1, dx1b) = reduce_start([g_ffn0[1], g_ffn0[0]], "ffn0", (dx1, dx1b))
    dw_out = mm_tn(mix, dx1b, "even_dw_out", J=1, tk=512, tm=1024)
    dmix = mm_nt(dx1b, w_out, "even_dmix", tr=D_MODEL)
    dproj, dcw_even, dpw, dps = mixer_bwd(proj, dmix, conv_w_full, pool_w, even_pool_scale, "even_mixer_bwd")
    dw_in = mm_tn(xn0, dproj, "even_dw_in", J=N_CHIPS, tk=512, tm=1024)
    grad_x, _, dg_even = mm_nt_norm_bwd(dproj, w_in, xs, even_norm, dx1, "even_dx")
    d_rel = jnp.sum(bias_reduce(dbias.reshape(3, N_HEADS, 2 * ATT_BLOCK * ATT_BLOCK), "bias_reduce"), axis=0).T

    red_even, grad_x = reduce_start([dw_in, dw_out], "even", grad_x)

    dcw_sh = dcw_even.reshape(3, N_CHIPS, A_WIDTH // N_CHIPS).transpose(1, 0, 2)
    don_sh = dg_odd.reshape(N_CHIPS, D_MODEL // N_CHIPS)
    dfcw = jnp.stack([g_ffn0[2], g_ffn1[2]])
    dfcw_sh = dfcw.reshape(2, 3, N_CHIPS, 2 * D_FF // N_CHIPS).transpose(2, 0, 1, 3)
    rep_grads = [d_rel, dg_even, dpw[None], dps, _head_sum(dgqk[0]), _head_sum(dgqk[1]),
                 jnp.concatenate([g_ffn0[4], g_ffn1[4]], axis=0), jnp.concatenate([g_ffn0[3], g_ffn1[3]], axis=0)]
    rep_rows = _pack([loss_part] + rep_grads)
    n_loss = _n_rows(loss_part.shape)
    shard_rows = jnp.concatenate([_pack([dcw_sh[j], don_sh[j], dfcw_sh[j]]) for j in range(N_CHIPS)], axis=0)
    n_rep, n_shard = rep_rows.shape[0], shard_rows.shape[0] // N_CHIPS
    small_sems, small_rows, small_land = devices_start(jnp.concatenate([rep_rows, shard_rows], axis=0), "small_grads_start")
    grad_x, small_rows = lax.optimization_barrier((grad_x, small_rows))

    def reduce_end(red, tag, after):
        sems, parts, zones = red
        parts, zones = reduce_wait(parts, zones, sems, after, "reduce_wait_" + tag)
        return zones, parts

    z_ffn1, p_ffn1 = reduce_end(red_ffn1, "ffn1", grad_x)
    z_odd, p_odd = reduce_end(red_odd, "odd", grad_x)
    r_qkv = reduce_sum(z_odd[0], p_odd[0], place, "reduce_sum_w_qkv")
    r_o = reduce_sum(z_odd[1], p_odd[1], place, "reduce_sum_w_o")
    r_up = reduce_sum(z_ffn1[0], p_ffn1[0], place, "reduce_sum_w_up1", layer=1)
    r_down = reduce_sum(z_ffn1[1], p_ffn1[1], place, "reduce_sum_w_down1", layer=1)
    r_qkv, r_o, r_up, r_down = lax.optimization_barrier((r_qkv, r_o, r_up, r_down))
    z_ffn0, p_ffn0 = reduce_end(red_ffn0, "ffn0", r_down)
    r_up = reduce_sum(z_ffn0[0], p_ffn0[0], place, "reduce_sum_w_up0", into=r_up, layer=0)
    r_down = reduce_sum(z_ffn0[1], p_ffn0[1], place, "reduce_sum_w_down0", into=r_down, layer=0)
    later = ["odd_w_qkv", "odd_w_o", "ffn_w_up", "ffn_w_down"]
    joined = join_halves([r_qkv, r_o, r_up, r_down], "grads_join_late_layers")
    G = {nm: g.reshape(W[nm].shape) for nm, g in zip(later, joined)}

    D_, NM, NV = {}, {}, {}

    def update(nm):
        as3 = lambda a: a.reshape((-1,) + a.shape[-2:])
        outs = adamw(as3(W[nm]), as3(G[nm]), as3(M1[nm]), as3(M2[nm]), "adamw_" + nm)
        D_[nm], NM[nm], NV[nm], G[nm] = [o.reshape(W[nm].shape) for o in outs]

    def all_before(names):
        tied = lax.optimization_barrier([D_[nm] for nm in names])
        for nm, d in zip(names, tied):
            D_[nm] = d
        return tied[0]

    for nm in later:
        update(nm)
    z_even, p_even = reduce_end(red_even, "even", all_before(later))
    joined = join_halves([reduce_sum(z_even[0], p_even[0], place, "reduce_sum_w_in"),
                          reduce_sum(z_even[1], p_even[1], place, "reduce_sum_w_out")], "grads_join_first_layer")
    first = ["even_w_in", "even_w_out"]
    for nm, g in zip(first, joined):
        G[nm] = g.reshape(W[nm].shape)
        update(nm)
    small_rows, small_land = devices_wait(small_rows, small_land, small_sems, all_before(first), "small_grads_wait")
    small_sum = device_sum(small_land, small_rows, place[2:3], "small_grads_sum")
    mine = lax.dynamic_slice_in_dim(small_sum, n_rep + chip * n_shard, n_shard, axis=0)
    loss = small_sum[0, 0]
    g_small = jnp.concatenate([small_sum[n_loss:n_rep], mine], axis=0)
    small_names = [n for n, _ in REPLICATED_SMALL + SHARDED_SMALL]
    small_shapes = [s for _, s in REPLICATED_SMALL + SHARDED_SMALL]
    G.update(dict(zip(small_names, _unpack(g_small, small_shapes))))
    outs = adamw_small(*[[d[n] for n in small_names] for d in (W, G, M1, M2)], "adamw_small")
    for dst, o in zip((D_, NM, NV), outs):
        dst.update(dict(zip(small_names, o)))

    return (loss, grad_x[None], *[G[n] for n in WEIGHT_ORDER], *[D_[n] for n in WEIGHT_ORDER],
            *[NM[n] for n in WEIGHT_ORDER], *[NV[n] for n in WEIGHT_ORDER])


def _head_sum(dg):
    return jnp.sum(dg.reshape(N_HEADS, HEAD_DIM), axis=0, keepdims=True)
```

```python
import functools
import math

import numpy as np
import jax
import jax.numpy as jnp
from jax import lax
from jax.experimental import pallas as pl
from jax.experimental.pallas import tpu as pltpu

F32 = jnp.float32
BF16 = jnp.bfloat16

D_MODEL = 1024
N_HEADS = 16
HEAD_DIM = 64
A_WIDTH = 512
POOL_WINDOWS = (2, 4, 8, 16)
POOL_GROUP = 128
EVEN_IN = 2048
D_FF = 2816
DILATED_PAIRS = ((128, 1), (512, 4), (2048, 16))
ATT_BLOCK = 128
N_REL_BUCKETS = 32
REL_MAX_DISTANCE = 2048
EPS = 1e-6
MASK_VALUE = -1e30
ADAM_LR, ADAM_B1, ADAM_B2, ADAM_EPS, ADAM_WD, ADAM_STEP = 0.001, 0.9, 0.999, 1e-08, 0.01, 10

VMEM_LIMIT_BYTES = 48 * 1024 * 1024
ELEMENTWISE_BLOCK_BYTES = 2 * 1024 * 1024
N_CHIPS = 4
N_DEV = 8
MESH = pl.DeviceIdType.MESH


def _params(*sem):
    return pltpu.CompilerParams(dimension_semantics=sem if sem else None, vmem_limit_bytes=VMEM_LIMIT_BYTES)


def _sds(shape, dtype):
    return jax.ShapeDtypeStruct(tuple(shape), dtype)


def _sds_hbm(shape, dtype):
    return pltpu.HBM(tuple(shape), dtype)


def cast_bf16(x, name, tr=None):
    lead, (R, C) = x.shape[:-2], x.shape[-2:]
    n = int(np.prod(lead)) if lead else 1
    x3 = x.reshape((n, R, C))
    tr = tr or R

    def body(x_ref, o_ref):
        o_ref[...] = x_ref[...].astype(BF16)

    out = pl.pallas_call(
        body, name=name, grid=(n, R // tr),
        in_specs=[pl.BlockSpec((None, tr, C), lambda i, r: (i, r, 0))],
        out_specs=pl.BlockSpec((None, tr, C), lambda i, r: (i, r, 0)),
        out_shape=_sds((n, R, C), BF16), compiler_params=_params("parallel", "parallel"),
    )(x3)
    return out.reshape(lead + (R, C))


def rmsnorm_fwd(x, g, name, ts=512):
    S, Dm = x.shape

    def body(x_ref, g_ref, o_ref):
        xv = x_ref[...]
        r = lax.rsqrt(jnp.mean(xv * xv, axis=-1, keepdims=True) + EPS)
        o_ref[...] = ((xv * r) * g_ref[...]).astype(BF16)

    return pl.pallas_call(
        body, name=name, grid=(S // ts,),
        in_specs=[pl.BlockSpec((ts, Dm), lambda i: (i, 0)), pl.BlockSpec((1, Dm), lambda i: (0, 0))],
        out_specs=pl.BlockSpec((ts, Dm), lambda i: (i, 0)),
        out_shape=_sds((S, Dm), BF16), compiler_params=_params("parallel"),
    )(x, g)


def mm_res_norm(a, w, res, gain, name, tm=1024):
    M, K = a.shape
    Dm = w.shape[-1]

    def body(a_ref, w_ref, r_ref, g_ref, y_ref, yn_ref):
        y = r_ref[...] + jnp.dot(a_ref[...], w_ref[...], preferred_element_type=F32)
        y_ref[...] = y
        r = lax.rsqrt(jnp.mean(y * y, axis=-1, keepdims=True) + EPS)
        yn_ref[...] = ((y * r) * g_ref[...]).astype(BF16)

    row = pl.BlockSpec((tm, Dm), lambda m: (m, 0))
    return pl.pallas_call(
        body, name=name, grid=(M // tm,),
        in_specs=[pl.BlockSpec((tm, K), lambda m: (m, 0)),
                  pl.BlockSpec((None, None, K, Dm), lambda m: (0, 0, 0, 0), pipeline_mode=pl.Buffered(1)),
                  row, pl.BlockSpec((1, Dm), lambda m: (0, 0))],
        out_specs=[row, row], out_shape=[_sds((M, Dm), F32), _sds((M, Dm), BF16)],
        compiler_params=_params("parallel"),
    )(a, w, res, gain)


def mm_res_loss(a, w, res, target, name, tm=512):
    M, K = a.shape
    Dm = w.shape[-1]

    def body(a_ref, w_ref, r_ref, t_ref, d_ref, db_ref, s_ref):
        e = (r_ref[...] + jnp.dot(a_ref[...], w_ref[...], preferred_element_type=F32)) - t_ref[...]
        d = e * (1.0 / Dm)
        d_ref[...] = d
        db_ref[...] = d.astype(BF16)
        part = jnp.sum(e * e, axis=0, keepdims=True)

        @pl.when(pl.program_id(0) == 0)
        def _():
            s_ref[...] = part

        @pl.when(pl.program_id(0) > 0)
        def _():
            s_ref[...] += part

    row = pl.BlockSpec((tm, Dm), lambda m: (m, 0))
    return pl.pallas_call(
        body, name=name, grid=(M // tm,),
        in_specs=[pl.BlockSpec((tm, K), lambda m: (m, 0)),
                  pl.BlockSpec((None, None, K, Dm), lambda m: (0, 0, 0, 0), pipeline_mode=pl.Buffered(1)), row, row],
        out_specs=[row, row, pl.BlockSpec((1, Dm), lambda m: (0, 0))],
        out_shape=[_sds((M, Dm), F32), _sds((M, Dm), BF16), _sds((1, Dm), F32)],
        compiler_params=_params("arbitrary"),
    )(a, w, res, target)


def mm_nt(dy, w, name, tr, layer=0, out_dtype=F32, tm=512):
    M = dy.shape[0]
    J, _, R, Ns = w.shape
    dims = (((1,), (1,)), ((), ()))

    def body(dy_ref, w_ref, o_ref):
        acc = None
        for j in range(J):
            p = lax.dot_general(dy_ref[:, j * Ns:(j + 1) * Ns], w_ref[j], dims, preferred_element_type=F32)
            acc = p if acc is None else acc + p
        o_ref[...] = acc.astype(o_ref.dtype)

    return pl.pallas_call(
        body, name=name, grid=(R // tr, M // tm),
        in_specs=[pl.BlockSpec((tm, J * Ns), lambda r, m: (m, 0)),
                  pl.BlockSpec((J, None, tr, Ns), lambda r, m: (0, layer, r, 0))],
        out_specs=pl.BlockSpec((tm, tr), lambda r, m: (m, r)),
        out_shape=_sds((M, R), out_dtype),
        compiler_params=_params("parallel", "parallel"),
    )(dy, w)


def mm_nt_norm_bwd(dy, w, x, g, dres, name, layer=0, tm=512):
    M = dy.shape[0]
    J, _, Dm, Ns = w.shape
    dims = (((1,), (1,)), ((), ()))

    def body(dy_ref, w_ref, x_ref, g_ref, r_ref, dx_ref, dxb_ref, dg_ref):
        dxn = None
        for j in range(J):
            p = lax.dot_general(dy_ref[:, j * Ns:(j + 1) * Ns], w_ref[j], dims, preferred_element_type=F32)
            dxn = p if dxn is None else dxn + p
        xv = x_ref[...]
        r = lax.rsqrt(jnp.mean(xv * xv, axis=-1, keepdims=True) + EPS)
        gx = dxn * g_ref[...]
        dot = jnp.sum(gx * xv, axis=-1, keepdims=True)
        dx = r_ref[...] + r * gx - xv * ((r * r * r) * (dot * (1.0 / Dm)))
        dx_ref[...] = dx
        dxb_ref[...] = dx.astype(BF16)
        part = jnp.sum(dxn * (xv * r), axis=0, keepdims=True)

        @pl.when(pl.program_id(0) == 0)
        def _():
            dg_ref[...] = part

        @pl.when(pl.program_id(0) > 0)
        def _():
            dg_ref[...] += part

    row = pl.BlockSpec((tm, Dm), lambda m: (m, 0))
    vec = pl.BlockSpec((1, Dm), lambda m: (0, 0))
    return pl.pallas_call(
        body, name=name, grid=(M // tm,),
        in_specs=[pl.BlockSpec((tm, J * Ns), lambda m: (m, 0)),
                  pl.BlockSpec((J, None, Dm, Ns), lambda m: (0, layer, 0, 0), pipeline_mode=pl.Buffered(1)), row, vec, row],
        out_specs=[row, row, vec],
        out_shape=[_sds((M, Dm), F32), _sds((M, Dm), BF16), _sds((1, Dm), F32)],
        compiler_params=_params("arbitrary"),
    )(dy, w, x, g, dres)


def mm_tn(a, dy, name, J, tk, tm=512, jb=None):
    M, K = a.shape
    jb = jb or J
    Ns = dy.shape[1] // J
    N = jb * Ns
    n_m = M // tm
    dims = (((0,), (0,)), ((), ()))

    def body(a_ref, dy_ref, o_ref, acc_ref):
        p = lax.dot_general(a_ref[...], dy_ref[...], dims, preferred_element_type=F32)
        m = pl.program_id(2)

        @pl.when(m == 0)
        def _():
            acc_ref[...] = p

        @pl.when(m > 0)
        def _():
            acc_ref[...] += p

        @pl.when(m == n_m - 1)
        def _():
            for j in range(jb):
                o_ref[j] = acc_ref[:, j * Ns:(j + 1) * Ns].astype(BF16)

    return pl.pallas_call(
        body, name=name, grid=(J // jb, K // tk, n_m),
        in_specs=[pl.BlockSpec((tm, tk), lambda g, k, m: (m, k)), pl.BlockSpec((tm, N), lambda g, k, m: (m, g))],
        out_specs=pl.BlockSpec((jb, tk, Ns), lambda g, k, m: (g, k, 0)),
        out_shape=_sds((J, K, Ns), BF16), scratch_shapes=[pltpu.VMEM((tk, N), F32)],
        compiler_params=_params("parallel", "parallel", "arbitrary"),
    )(a, dy)


HALO = 16


def _shift_down(x, s):
    return pltpu.roll(x, s, 0)


def _shift_up(x, s):
    return pltpu.roll(x, x.shape[0] - s, 0)


def _conv3(z, cw):
    return (_shift_down(z, 2) * cw[0:1] + _shift_down(z, 1) * cw[1:2]) + z * cw[2:3]


def _window_count(first_row, n, k):
    t = first_row + lax.broadcasted_iota(jnp.int32, (n, 1), 0)
    return jnp.clip(t + 1, 1, k).astype(F32)


def in_mixer_fwd(xn, w_in, conv_w, pool_w, pool_scale, name, ts=512):
    S, K = xn.shape
    n = ts + HALO

    def body(xm_ref, xb_ref, w_ref, cw_ref, pw_ref, ps_ref, p_ref, o_ref):
        i = pl.program_id(0)
        before = jnp.where(i > 0, xb_ref[...], jnp.zeros_like(xb_ref))
        rows = jnp.concatenate([before, xm_ref[...]], axis=0)
        h, gb, gc, pin = [jnp.dot(rows, w_ref[j], preferred_element_type=F32) for j in range(N_CHIPS)]
        for j, part in enumerate((h, gb, gc, pin)):
            p_ref[:, j * A_WIDTH:(j + 1) * A_WIDTH] = part[HALO:]
        cz = _conv3(gc * h, cw_ref[...])
        o_ref[:, 0:A_WIDTH] = (gb[HALO:] * cz[HALO:]).astype(BF16)
        for g, k in enumerate(POOL_WINDOWS):
            p = pin[:, g * POOL_GROUP:(g + 1) * POOL_GROUP]
            w = p
            s = 1
            while s < k:
                w = w + _shift_down(w, s)
                s *= 2
            pooled = w / _window_count(i * ts - HALO, n, k) - p
            yb = jnp.dot(pooled[HALO:].astype(BF16), pw_ref[g], preferred_element_type=F32)
            yb = yb * ps_ref[:, g * POOL_GROUP:(g + 1) * POOL_GROUP]
            o_ref[:, A_WIDTH + g * POOL_GROUP:A_WIDTH + (g + 1) * POOL_GROUP] = yb.astype(BF16)

    hb = ts // HALO
    return pl.pallas_call(
        body, name=name, grid=(S // ts,),
        in_specs=[
            pl.BlockSpec((ts, K), lambda i: (i, 0)),
            pl.BlockSpec((HALO, K), lambda i: (jnp.maximum(i * hb - 1, 0), 0)),
            pl.BlockSpec((N_CHIPS, None, K, A_WIDTH), lambda i: (0, 0, 0, 0), pipeline_mode=pl.Buffered(1)),
            pl.BlockSpec((3, A_WIDTH), lambda i: (0, 0)),
            pl.BlockSpec((4, POOL_GROUP, POOL_GROUP), lambda i: (0, 0, 0)),
            pl.BlockSpec((1, 4 * POOL_GROUP), lambda i: (0, 0)),
        ],
        out_specs=[pl.BlockSpec((ts, EVEN_IN), lambda i: (i, 0)), pl.BlockSpec((ts, D_MODEL), lambda i: (i, 0))],
        out_shape=[_sds((S, EVEN_IN), F32), _sds((S, D_MODEL), BF16)], compiler_params=_params("parallel"),
    )(xn, xn, w_in, conv_w, pool_w, pool_scale)


def mixer_bwd(proj, dmix, conv_w, pool_w, pool_scale, name, ts=256):
    S = proj.shape[0]
    n = ts + 2 * HALO
    nt = S // ts
    tn_dims = (((0,), (0,)), ((), ()))
    nt_dims = (((1,), (1,)), ((), ()))

    def body(pm_ref, pb_ref, pa_ref, dm_ref, da_ref, cw_ref, pw_ref, ps_ref, o_ref, dcw_ref, dpw_ref, dps_ref):
        i = pl.program_id(0)
        last = i == nt - 1
        before = jnp.where(i > 0, pb_ref[...], 0.0)
        after = jnp.where(last, 0.0, pa_ref[...])
        ext = jnp.concatenate([before, pm_ref[...], after], axis=0)
        dafter = jnp.where(last, 0.0, da_ref[...])
        dext = jnp.concatenate([jnp.zeros((HALO, D_MODEL), F32), dm_ref[...], dafter], axis=0)
        cw = cw_ref[...]
        main = slice(HALO, HALO + ts)

        @pl.when(i == 0)
        def _():
            dcw_ref[...] = jnp.zeros_like(dcw_ref)
            dpw_ref[...] = jnp.zeros_like(dpw_ref)
            dps_ref[...] = jnp.zeros_like(dps_ref)

        h, gb, gc = ext[:, 0:A_WIDTH], ext[:, A_WIDTH:2 * A_WIDTH], ext[:, 2 * A_WIDTH:3 * A_WIDTH]
        z = gc * h
        z1, z2 = _shift_down(z, 1), _shift_down(z, 2)
        cz = (z2 * cw[0:1] + z1 * cw[1:2]) + z * cw[2:3]
        dya = dext[:, 0:A_WIDTH]
        dcz = dya * gb
        dz = dcz * cw[2:3] + _shift_up(dcz, 1) * cw[1:2] + _shift_up(dcz, 2) * cw[0:1]
        o_ref[:, 0:A_WIDTH] = (dz * gc)[main].astype(BF16)
        o_ref[:, A_WIDTH:2 * A_WIDTH] = (dya * cz)[main].astype(BF16)
        o_ref[:, 2 * A_WIDTH:3 * A_WIDTH] = (dz * h)[main].astype(BF16)
        dczm = dcz[main]
        dcw_ref[0:1, :] += jnp.sum(dczm * z2[main], axis=0, keepdims=True)
        dcw_ref[1:2, :] += jnp.sum(dczm * z1[main], axis=0, keepdims=True)
        dcw_ref[2:3, :] += jnp.sum(dczm * z[main], axis=0, keepdims=True)

        for g, k in enumerate(POOL_WINDOWS):
            lo = 3 * A_WIDTH + g * POOL_GROUP
            cols = slice(g * POOL_GROUP, (g + 1) * POOL_GROUP)
            p = ext[:, lo:lo + POOL_GROUP]
            w = p
            s = 1
            while s < k:
                w = w + _shift_down(w, s)
                s *= 2
            cnt = _window_count(i * ts - HALO, n, k)
            pooled = (w / cnt - p)[main].astype(BF16)
            dyb = dext[:, A_WIDTH + g * POOL_GROUP:A_WIDTH + (g + 1) * POOL_GROUP]
            e = dyb * ps_ref[:, cols]
            pre = jnp.dot(pooled, pw_ref[g], preferred_element_type=F32)
            dps_ref[:, cols] += jnp.sum(dyb[main] * pre, axis=0, keepdims=True)
            dpw_ref[g] += lax.dot_general(pooled, e[main].astype(BF16), tn_dims, preferred_element_type=F32)
            dpooled = lax.dot_general(e.astype(BF16), pw_ref[g], nt_dims, preferred_element_type=F32)
            q = dpooled / cnt
            a = q
            s = 1
            while s < k:
                a = a + _shift_up(a, s)
                s *= 2
            o_ref[:, lo:lo + POOL_GROUP] = (a - dpooled)[main].astype(BF16)

    hb = ts // HALO
    nh = S // HALO
    before_map = lambda i: (jnp.maximum(i * hb - 1, 0), 0)
    after_map = lambda i: (jnp.minimum((i + 1) * hb, nh - 1), 0)
    full = lambda *shape: pl.BlockSpec(shape, lambda i: (0,) * len(shape))
    return pl.pallas_call(
        body, name=name, grid=(nt,),
        in_specs=[
            pl.BlockSpec((ts, EVEN_IN), lambda i: (i, 0)),
            pl.BlockSpec((HALO, EVEN_IN), before_map),
            pl.BlockSpec((HALO, EVEN_IN), after_map),
            pl.BlockSpec((ts, D_MODEL), lambda i: (i, 0)),
            pl.BlockSpec((HALO, D_MODEL), after_map),
            full(3, A_WIDTH), full(4, POOL_GROUP, POOL_GROUP), full(1, 4 * POOL_GROUP),
        ],
        out_specs=[pl.BlockSpec((ts, EVEN_IN), lambda i: (i, 0)), full(3, A_WIDTH), full(4, POOL_GROUP, POOL_GROUP),
                   full(1, 4 * POOL_GROUP)],
        out_shape=[_sds((S, EVEN_IN), BF16), _sds((3, A_WIDTH), F32), _sds((4, POOL_GROUP, POOL_GROUP), F32),
                   _sds((1, 4 * POOL_GROUP), F32)],
        compiler_params=_params("arbitrary"),
    )(proj, proj, proj, dmix, dmix, conv_w, pool_w, pool_scale)


FFN_HALO = 16
FFN_TC = 1408


GLU_CHUNKS = ((0, 512), (512, 512), (1024, 384))


def up_glu_fwd(xn, w_up, conv_w, conv_b, name, tm=512):
    S, K = xn.shape
    nc = D_FF // FFN_TC

    def body(xm_ref, xb_ref, wg_ref, wu_ref, cwg_ref, cwu_ref, cbg_ref, cbu_ref, pg_ref, pu_ref, ug_ref, uu_ref, o_ref):
        before = jnp.where(pl.program_id(1) > 0, xb_ref[...], jnp.zeros_like(xb_ref))
        rows = jnp.concatenate([before, xm_ref[...]], axis=0)
        for lo, width in GLU_CHUNKS:
            cols = slice(lo, lo + width)
            pre_g = jnp.dot(rows, wg_ref[:, cols], preferred_element_type=F32)
            pre_u = jnp.dot(rows, wu_ref[:, cols], preferred_element_type=F32)
            gate = _conv3(pre_g, cwg_ref[:, cols])[FFN_HALO:] + cbg_ref[:, cols]
            upv = _conv3(pre_u, cwu_ref[:, cols])[FFN_HALO:] + cbu_ref[:, cols]
            pg_ref[:, cols] = pre_g[FFN_HALO:].astype(BF16)
            pu_ref[:, cols] = pre_u[FFN_HALO:].astype(BF16)
            ug_ref[:, cols] = gate.astype(BF16)
            uu_ref[:, cols] = upv.astype(BF16)
            o_ref[:, cols] = ((gate * (1.0 / (1.0 + jnp.exp(-gate)))) * upv).astype(BF16)

    hb = tm // FFN_HALO
    wspec = lambda off: pl.BlockSpec((None, None, K, FFN_TC), lambda j, m: (j + off, 0, 0, 0))
    cw = lambda off: pl.BlockSpec((3, FFN_TC), lambda j, m: (0, j + off))
    cb = lambda off: pl.BlockSpec((1, FFN_TC), lambda j, m: (0, j + off))
    out = pl.BlockSpec((tm, FFN_TC), lambda j, m: (m, j))
    pg, pu, ug, uu, act = pl.pallas_call(
        body, name=name, grid=(nc, S // tm),
        in_specs=[pl.BlockSpec((tm, K), lambda j, m: (m, 0)),
                  pl.BlockSpec((FFN_HALO, K), lambda j, m: (jnp.maximum(m * hb - 1, 0), 0)),
                  wspec(0), wspec(nc), cw(0), cw(nc), cb(0), cb(nc)],
        out_specs=[out] * 5, out_shape=[_sds((S, D_FF), BF16)] * 5,
        compiler_params=_params("parallel", "parallel"),
    )(xn, xn, w_up, w_up, conv_w, conv_w, conv_b, conv_b)
    return (pg, pu), (ug, uu), act


def glu_bwd(up, u, da, conv_w, name, ts=256):
    S = up[0].shape[0]
    nc = D_FF // FFN_TC
    nt = S // ts
    W = 2 * D_FF

    def body(xg_ref, xu_ref, gm_ref, ga_ref, um_ref, ua_ref, dm_ref, da_ref, cw_ref, dx_ref, dcw_ref, dcb_ref):
        i = pl.program_id(0)
        last = i == nt - 1

        @pl.when(i == 0)
        def _():
            dcw_ref[...] = jnp.zeros_like(dcw_ref)
            dcb_ref[...] = jnp.zeros_like(dcb_ref)

        def rows(m_ref, a_ref, cols):
            return jnp.concatenate([m_ref[:, cols], a_ref[:, cols]], axis=0).astype(F32)

        def back(d, x, cols):
            cw = cw_ref[:, cols]
            d1, d2 = _shift_up(d, 1), _shift_up(d, 2)
            dx_ref[:, cols] = ((d * cw[2:3] + d1 * cw[1:2]) + d2 * cw[0:1])[:ts].astype(BF16)
            dcb_ref[:, cols] += jnp.sum(d[:ts], axis=0, keepdims=True)
            dcw_ref[0:1, cols] += jnp.sum(d2[:ts] * x, axis=0, keepdims=True)
            dcw_ref[1:2, cols] += jnp.sum(d1[:ts] * x, axis=0, keepdims=True)
            dcw_ref[2:3, cols] += jnp.sum(d[:ts] * x, axis=0, keepdims=True)

        for c in range(nc):
            cols = slice(c * FFN_TC, (c + 1) * FFN_TC)
            ug, uu = rows(gm_ref, ga_ref, cols), rows(um_ref, ua_ref, cols)
            dae = rows(dm_ref, da_ref, cols)
            dae = jnp.where(last & (lax.broadcasted_iota(jnp.int32, dae.shape, 0) >= ts), 0.0, dae)
            sg = 1.0 / (1.0 + jnp.exp(-ug))
            duu = dae * (ug * sg)
            dug = (dae * uu) * (sg * (1.0 + ug * (1.0 - sg)))
            back(dug, xg_ref[:, cols].astype(F32), cols)
            back(duu, xu_ref[:, cols].astype(F32), slice(D_FF + c * FFN_TC, D_FF + (c + 1) * FFN_TC))

    hb = ts // FFN_HALO
    nh = S // FFN_HALO
    after_map = lambda i: (jnp.minimum((i + 1) * hb, nh - 1), 0)
    main = pl.BlockSpec((ts, D_FF), lambda i: (i, 0))
    after = pl.BlockSpec((FFN_HALO, D_FF), after_map)
    return pl.pallas_call(
        body, name=name, grid=(nt,),
        in_specs=[main, main, main, after, main, after, main, after, pl.BlockSpec((3, W), lambda i: (0, 0))],
        out_specs=[pl.BlockSpec((ts, W), lambda i: (i, 0)), pl.BlockSpec((3, W), lambda i: (0, 0)),
                   pl.BlockSpec((1, W), lambda i: (0, 0))],
        out_shape=[_sds((S, W), BF16), _sds((3, W), F32), _sds((1, W), F32)],
        compiler_params=_params("arbitrary"),
    )(up[0], up[1], u[0], u[0], u[1], u[1], da, da, conv_w)


MEAN_GROUP = 256


def _head_mean_matrix():
    h = np.arange(MEAN_GROUP) // HEAD_DIM
    return jnp.asarray((h[:, None] == h[None, :]).astype(np.float32) / HEAD_DIM, dtype=BF16)


def _head_mean(v, gm):
    vb = v.astype(BF16)
    return jnp.concatenate([jnp.dot(vb[:, c:c + MEAN_GROUP], gm, preferred_element_type=F32)
                            for c in range(0, v.shape[1], MEAN_GROUP)], axis=1)


def qkv_qknorm_fwd(xn, w_qkv, gqk, name, tm=1024):
    S, K = xn.shape
    J, _, _, Ns = w_qkv.shape
    gains = gqk.reshape(1, 3 * D_MODEL)

    def body(x_ref, w_ref, g_ref, gm_ref, raw_ref, o_ref):
        first_col = pl.program_id(0) * Ns
        acc = jnp.dot(x_ref[...], w_ref[...], preferred_element_type=F32)
        raw_ref[...] = acc
        gm = gm_ref[...]
        for c in range(0, Ns, MEAN_GROUP):
            cols = slice(c, c + MEAN_GROUP)
            x = acc[:, cols]
            mean = jnp.dot((x * x).astype(BF16), gm, preferred_element_type=F32)
            normed = (x * lax.rsqrt(mean + EPS)) * g_ref[:, cols]
            o_ref[:, cols] = jnp.where(first_col + c >= 2 * D_MODEL, x, normed).astype(BF16)

    return pl.pallas_call(
        body, name=name, grid=(J, S // tm),
        in_specs=[pl.BlockSpec((tm, K), lambda j, m: (m, 0)), pl.BlockSpec((None, None, K, Ns), lambda j, m: (j, 0, 0, 0)),
                  pl.BlockSpec((1, Ns), lambda j, m: (0, j)), pl.BlockSpec((MEAN_GROUP, MEAN_GROUP), lambda j, m: (0, 0))],
        out_specs=[pl.BlockSpec((tm, Ns), lambda j, m: (m, j))] * 2,
        out_shape=[_sds((S, J * Ns), F32), _sds((S, J * Ns), BF16)], compiler_params=_params("parallel", "parallel"),
    )(xn, w_qkv, gains, _head_mean_matrix())


def qknorm_bwd(qkv, dq, dk, dv, gqk, name, ts=256):
    S = qkv.shape[0]

    def body(x_ref, dq_ref, dk_ref, dv_ref, g_ref, gm_ref, o_ref, dg_ref):
        @pl.when(pl.program_id(0) == 0)
        def _():
            dg_ref[...] = jnp.zeros_like(dg_ref)

        gm = gm_ref[...]
        for part, d_ref in enumerate((dq_ref, dk_ref)):
            cols = slice(part * D_MODEL, (part + 1) * D_MODEL)
            x = x_ref[:, cols]
            d = d_ref[...]
            r = lax.rsqrt(_head_mean(x * x, gm) + EPS)
            gx = d * g_ref[part]
            o_ref[:, cols] = (r * gx - x * ((r * r * r) * _head_mean(gx * x, gm))).astype(BF16)
            dg_ref[part] += jnp.sum(d * (x * r), axis=0, keepdims=True)
        o_ref[:, 2 * D_MODEL:] = dv_ref[...].astype(BF16)

    row = pl.BlockSpec((ts, D_MODEL), lambda i: (i, 0))
    wide = pl.BlockSpec((ts, 3 * D_MODEL), lambda i: (i, 0))
    gains = pl.BlockSpec((3, 1, D_MODEL), lambda i: (0, 0, 0))
    return pl.pallas_call(
        body, name=name, grid=(S // ts,),
        in_specs=[wide, row, row, row, gains, pl.BlockSpec((MEAN_GROUP, MEAN_GROUP), lambda i: (0, 0))],
        out_specs=[wide, gains],
        out_shape=[_sds((S, 3 * D_MODEL), BF16), _sds((3, 1, D_MODEL), F32)],
        compiler_params=_params("arbitrary"),
    )(qkv, dq, dk, dv, gqk, _head_mean_matrix())


RESIDUES = 16


def _block_order(dil):
    runs = RESIDUES // dil
    slot = np.arange(ATT_BLOCK)
    return (slot % (ATT_BLOCK // runs)) * runs + slot // (ATT_BLOCK // runs)


def _bucket_tables():
    n = ATT_BLOCK
    max_exact = N_REL_BUCKETS // 2
    buckets, valids = [], []
    for _, dil in DILATED_PAIRS:
        order = _block_order(dil)
        a = order[:, None]
        c = np.concatenate([order, n + order])[None, :]
        first_half = (np.arange(2 * n) < n)[None, :]
        rel = a + n - c
        band = (rel >= 0) & (rel <= n)
        dist = np.clip(rel, 0, n) * dil
        dd = np.maximum(dist, 1).astype(np.float32)
        large = max_exact + (np.log(dd / np.float32(max_exact)) / np.float32(math.log(REL_MAX_DISTANCE / max_exact))
                             * np.float32(N_REL_BUCKETS - max_exact)).astype(np.int32)
        large = np.minimum(large, N_REL_BUCKETS - 1)
        buckets.append(np.where(dist < max_exact, dist, large).reshape(1, -1))
        valids.append(np.stack([(band & ~first_half).reshape(1, -1), band.reshape(1, -1)]))
    return np.stack(buckets).astype(np.int32), np.stack(valids).astype(np.int32)


BIAS_CHUNK = 8192


def _split3(x):
    a = x.astype(BF16)
    r = x - a.astype(F32)
    b = r.astype(BF16)
    c = (r - b.astype(F32)).astype(BF16)
    return a, b, c


def bias_expand(rel_bias_t, name):
    bucket, valid = _bucket_tables()
    nq = bucket.shape[-1]

    def body(t_ref, b_ref, v_ref, o_ref):
        onehot = (lax.broadcasted_iota(jnp.int32, (N_REL_BUCKETS, BIAS_CHUNK), 0) == b_ref[...]).astype(BF16)
        acc = None
        for term in _split3(t_ref[...]):
            p = jnp.dot(term, onehot, preferred_element_type=F32)
            acc = p if acc is None else acc + p
        for v in range(2):
            o_ref[v] = jnp.where(v_ref[v] > 0, acc, MASK_VALUE)

    return pl.pallas_call(
        body, name=name, grid=(3, nq // BIAS_CHUNK),
        in_specs=[pl.BlockSpec((N_HEADS, N_REL_BUCKETS), lambda b, c: (0, 0)),
                  pl.BlockSpec((None, 1, BIAS_CHUNK), lambda b, c: (b, 0, c)),
                  pl.BlockSpec((None, 2, 1, BIAS_CHUNK), lambda b, c: (b, 0, 0, c))],
        out_specs=pl.BlockSpec((None, 2, N_HEADS, BIAS_CHUNK), lambda b, c: (b, 0, 0, c)),
        out_shape=_sds((3, 2, N_HEADS, nq), F32), compiler_params=_params("parallel", "parallel"),
    )(rel_bias_t, jnp.asarray(bucket), jnp.asarray(valid))


def bias_reduce(dbias, name):
    bucket, _ = _bucket_tables()
    nq = bucket.shape[-1]
    dims = (((1,), (1,)), ((), ()))

    def body(d_ref, b_ref, o_ref):
        onehot = (lax.broadcasted_iota(jnp.int32, (N_REL_BUCKETS, BIAS_CHUNK), 0) == b_ref[...]).astype(BF16)
        acc = None
        for term in _split3(d_ref[...]):
            p = lax.dot_general(term, onehot, dims, preferred_element_type=F32)
            acc = p if acc is None else acc + p

        @pl.when(pl.program_id(1) == 0)
        def _():
            o_ref[...] = acc

        @pl.when(pl.program_id(1) > 0)
        def _():
            o_ref[...] += acc

    return pl.pallas_call(
        body, name=name, grid=(3, nq // BIAS_CHUNK),
        in_specs=[pl.BlockSpec((None, N_HEADS, BIAS_CHUNK), lambda b, c: (b, 0, c)),
                  pl.BlockSpec((None, 1, BIAS_CHUNK), lambda b, c: (b, 0, c))],
        out_specs=pl.BlockSpec((None, N_HEADS, N_REL_BUCKETS), lambda b, c: (b, 0, 0)),
        out_shape=_sds((3, N_HEADS, N_REL_BUCKETS), F32), compiler_params=_params("parallel", "arbitrary"),
    )(dbias, jnp.asarray(bucket))


PAIR = 2 * HEAD_DIM
N_PAIRS = N_HEADS // 2
_NT = (((1,), (1,)), ((), ()))
_TN = (((0,), (0,)), ((), ()))


def _low_lanes(shape):
    return lax.broadcasted_iota(jnp.int32, shape, 1) < HEAD_DIM


ATTN_VMEM_LIMIT_BYTES = 56 * 1024 * 1024
BRANCH_ORDER = (2, 1, 0)


def _regroup(dst, src, L16):
    for r in range(RESIDUES):
        dst[pl.ds(r * L16, L16), :] = src[pl.ds(r, L16, stride=RESIDUES), :]


def _ungroup(dst, src, L16):
    for r in range(RESIDUES):
        dst[pl.ds(r, L16, stride=RESIDUES), :] = src[pl.ds(r * L16, L16), :]


def _branch_geometry(branch, S):
    dil = DILATED_PAIRS[branch][1]
    runs = RESIDUES // dil
    return dil, runs, ATT_BLOCK // runs, S // dil // ATT_BLOCK


def _block_rows(it, branch, S):
    dil, runs, run_len, n_blocks = _branch_geometry(branch, S)
    L16 = S // RESIDUES
    r, b = it // n_blocks, it % n_blocks
    prev = jnp.maximum(b - 1, 0)
    cur_rows = [pl.multiple_of((j * dil + r) * L16 + run_len * b, 8) for j in range(runs)]
    prev_rows = [pl.multiple_of((j * dil + r) * L16 + run_len * prev, 8) for j in range(runs)]
    return cur_rows, prev_rows, jnp.minimum(b, 1)


def _load_block(ref, rows, run_len):
    parts = [ref[pl.ds(o, run_len), :] for o in rows]
    return parts[0] if len(parts) == 1 else jnp.concatenate(parts, axis=0)


def _store_block(ref, rows, run_len, value, add=False):
    for j, o in enumerate(rows):
        part = value[j * run_len:(j + 1) * run_len]
        if add:
            ref[pl.ds(o, run_len), :] += part
        else:
            ref[pl.ds(o, run_len), :] = part


ATTN_FWD_UNROLL = 16
ATTN_BWD_UNROLL = 16


def _stack_heads(x, low):
    zero = jnp.zeros_like(x)
    return jnp.concatenate([jnp.where(low, x, zero), jnp.where(low, zero, x)], axis=0)


def _unstack_heads(y, low):
    return jnp.where(low, y[:ATT_BLOCK], y[ATT_BLOCK:])


def attn_fwd(qkvn, bias, name):
    S = qkvn.shape[0]
    L16 = S // RESIDUES
    n_iter = S // ATT_BLOCK

    def body(q_ref, k_ref, v_ref, b_ref, o_ref, lse_ref, stage, qp, kp, vp, acc_s, m_s, l_s):
        for src, dst in ((q_ref, qp), (k_ref, kp), (v_ref, vp)):
            stage[...] = src[...].astype(F32)
            _regroup(dst, stage, L16)
        low = _low_lanes((ATT_BLOCK, PAIR))

        for branch in BRANCH_ORDER:
            _, _, run_len, _ = _branch_geometry(branch, S)
            first = branch == BRANCH_ORDER[0]

            def step(it, carry, branch=branch, run_len=run_len, first=first):
                cur, prev, variant = _block_rows(it, branch, S)
                q = _load_block(qp, cur, run_len).astype(BF16)
                k = jnp.concatenate([_load_block(kp, prev, run_len), _load_block(kp, cur, run_len)], axis=0).astype(BF16)
                v = jnp.concatenate([_load_block(vp, prev, run_len), _load_block(vp, cur, run_len)], axis=0).astype(BF16)
                s = lax.dot_general(_stack_heads(q, low), k, _NT, preferred_element_type=F32) * (HEAD_DIM ** -0.5)
                s = s + b_ref[2 * branch + variant].reshape(2 * ATT_BLOCK, 2 * ATT_BLOCK)
                mx = jnp.max(s, axis=-1, keepdims=True)
                p = jnp.exp(s - mx)
                den = jnp.sum(p, axis=-1, keepdims=True)
                pv = jnp.dot(p.astype(BF16), v, preferred_element_type=F32)
                acc = _unstack_heads(pv, low)
                m = _unstack_heads(mx, low)
                l = _unstack_heads(den, low)
                if not first:
                    m_old = _load_block(m_s, cur, run_len)
                    m_new = jnp.maximum(m_old, m)
                    a_old, a_new = jnp.exp(m_old - m_new), jnp.exp(m - m_new)
                    acc = _load_block(acc_s, cur, run_len) * a_old + acc * a_new
                    l = _load_block(l_s, cur, run_len) * a_old + l * a_new
                    m = m_new
                _store_block(acc_s, cur, run_len, acc)
                _store_block(m_s, cur, run_len, m)
                _store_block(l_s, cur, run_len, l)
                return carry

            lax.fori_loop(0, n_iter, step, 0, unroll=ATTN_FWD_UNROLL)

        acc_s[...] = acc_s[...] / l_s[...]
        _ungroup(stage, acc_s, L16)
        o_ref[...] = stage[...].astype(BF16)
        m_s[...] = m_s[...] + jnp.log(l_s[...])
        _ungroup(lse_ref, m_s, L16)

    col = lambda part: pl.BlockSpec((S, PAIR), lambda hp: (0, part * N_PAIRS + hp))
    out = pl.BlockSpec((S, PAIR), lambda hp: (0, hp))
    return pl.pallas_call(
        body, name=name, grid=(N_PAIRS,),
        in_specs=[col(0), col(1), col(2), pl.BlockSpec((6, 2, ATT_BLOCK, 2 * ATT_BLOCK), lambda hp: (0, hp, 0, 0))],
        out_specs=[out, out], out_shape=[_sds((S, D_MODEL), BF16), _sds((S, D_MODEL), F32)],
        scratch_shapes=[pltpu.VMEM((S, PAIR), F32)] * 7,
        compiler_params=pltpu.CompilerParams(dimension_semantics=("parallel",), vmem_limit_bytes=ATTN_VMEM_LIMIT_BYTES),
    )(qkvn, qkvn, qkvn, bias)


def attn_bwd(qkvn, att, datt, lse, bias, name):
    S = qkvn.shape[0]
    L16 = S // RESIDUES
    n_iter = S // ATT_BLOCK
    TILE = 512

    def body(q_ref, k_ref, v_ref, o_ref, do_ref, lse_ref, b_ref, dq_ref, dk_ref, dv_ref, db_ref,
             qp, kp, vp, dop, ldp, dqp, dkp, dvp):
        stage = dqp
        for src, dst in ((q_ref, qp), (k_ref, kp), (v_ref, vp), (do_ref, dop)):
            stage[...] = src[...].astype(F32)
            _regroup(dst, stage, L16)

        def pack(i, carry):
            rows = pl.ds(pl.multiple_of(i * TILE, TILE), TILE)
            low = _low_lanes((TILE, PAIR))
            lane = lax.broadcasted_iota(jnp.int32, (TILE, PAIR), 1)
            prod = do_ref[rows, :].astype(F32) * o_ref[rows, :].astype(F32)
            d0 = jnp.sum(jnp.where(low, prod, 0.0), axis=-1, keepdims=True)
            d1 = jnp.sum(jnp.where(low, 0.0, prod), axis=-1, keepdims=True)
            stage[rows, :] = jnp.where((lane & (HEAD_DIM // 2)) == 0, lse_ref[rows, :], jnp.where(low, d0, d1))
            return carry

        lax.fori_loop(0, S // TILE, pack, 0)
        _regroup(ldp, stage, L16)
        dqp[...] = jnp.zeros_like(dqp)
        dkp[...] = jnp.zeros_like(dkp)
        dvp[...] = jnp.zeros_like(dvp)
        db_ref[...] = jnp.zeros_like(db_ref)
        low = _low_lanes((ATT_BLOCK, PAIR))

        for branch in BRANCH_ORDER:
            _, _, run_len, _ = _branch_geometry(branch, S)

            def step(it, carry, branch=branch, run_len=run_len):
                cur, prev, variant = _block_rows(it, branch, S)
                q = _load_block(qp, cur, run_len).astype(BF16)
                dout = _load_block(dop, cur, run_len).astype(BF16)
                ld = _load_block(ldp, cur, run_len)
                k = jnp.concatenate([_load_block(kp, prev, run_len), _load_block(kp, cur, run_len)], axis=0).astype(BF16)
                v = jnp.concatenate([_load_block(vp, prev, run_len), _load_block(vp, cur, run_len)], axis=0).astype(BF16)
                half = HEAD_DIM // 2
                lse2 = jnp.concatenate([ld[:, 0:1], ld[:, HEAD_DIM:HEAD_DIM + 1]], axis=0)
                delta2 = jnp.concatenate([ld[:, half:half + 1], ld[:, HEAD_DIM + half:HEAD_DIM + half + 1]], axis=0)
                q2, do2 = _stack_heads(q, low), _stack_heads(dout, low)
                s = lax.dot_general(q2, k, _NT, preferred_element_type=F32) * (HEAD_DIM ** -0.5)
                p = jnp.exp(s + b_ref[2 * branch + variant].reshape(2 * ATT_BLOCK, 2 * ATT_BLOCK) - lse2)
                dp = lax.dot_general(do2, v, _NT, preferred_element_type=F32)
                ds = p * (dp - delta2)
                db_ref[branch] += ds.reshape(2, ATT_BLOCK, 2 * ATT_BLOCK)
                dsb = (ds * (HEAD_DIM ** -0.5)).astype(BF16)
                dq = _unstack_heads(jnp.dot(dsb, k, preferred_element_type=F32), low)
                dk = lax.dot_general(dsb, q2, _TN, preferred_element_type=F32)
                dv = lax.dot_general(p.astype(BF16), do2, _TN, preferred_element_type=F32)
                _store_block(dqp, cur, run_len, dq, add=True)
                _store_block(dkp, prev, run_len, dk[:ATT_BLOCK], add=True)
                _store_block(dvp, prev, run_len, dv[:ATT_BLOCK], add=True)
                _store_block(dkp, cur, run_len, dk[ATT_BLOCK:], add=True)
                _store_block(dvp, cur, run_len, dv[ATT_BLOCK:], add=True)
                return carry

            lax.fori_loop(0, n_iter, step, 0, unroll=ATTN_BWD_UNROLL)

        _ungroup(dq_ref, dqp, L16)
        _ungroup(dk_ref, dkp, L16)
        _ungroup(dv_ref, dvp, L16)

    col = lambda part: pl.BlockSpec((S, PAIR), lambda hp: (0, part * N_PAIRS + hp))
    one = pl.BlockSpec((S, PAIR), lambda hp: (0, hp))
    return pl.pallas_call(
        body, name=name, grid=(N_PAIRS,),
        in_specs=[col(0), col(1), col(2), one, one, one,
                  pl.BlockSpec((6, 2, ATT_BLOCK, 2 * ATT_BLOCK), lambda hp: (0, hp, 0, 0))],
        out_specs=[one, one, one, pl.BlockSpec((3, 2, ATT_BLOCK, 2 * ATT_BLOCK), lambda hp: (0, hp, 0, 0))],
        out_shape=[_sds((S, D_MODEL), F32)] * 3 + [_sds((3, N_HEADS, ATT_BLOCK, 2 * ATT_BLOCK), F32)],
        scratch_shapes=[pltpu.VMEM((S, PAIR), F32)] * 8,
        compiler_params=pltpu.CompilerParams(dimension_semantics=("parallel",), vmem_limit_bytes=ATTN_VMEM_LIMIT_BYTES),
    )(qkvn, qkvn, qkvn, att, datt, lse, bias)


def _adamw_step(w_ref, g_ref, m_ref, v_ref, d_ref, nm_ref, nv_ref):
    gv = g_ref[...]
    m2 = ADAM_B1 * m_ref[...] + (1.0 - ADAM_B1) * gv
    v2 = ADAM_B2 * v_ref[...] + (1.0 - ADAM_B2) * (gv * gv)
    m_hat = m2 / (1.0 - ADAM_B1 ** ADAM_STEP)
    v_hat = v2 / (1.0 - ADAM_B2 ** ADAM_STEP)
    d_ref[...] = -ADAM_LR * (m_hat / (jnp.sqrt(v_hat) + ADAM_EPS) + ADAM_WD * w_ref[...])
    nm_ref[...] = m2
    nv_ref[...] = v2


def adamw_small(ws, gs, ms, vs, name):
    n = len(ws)

    def body(*refs):
        groups = [refs[k * n:(k + 1) * n] for k in range(7)]
        for refs_of_one in zip(*groups):
            _adamw_step(*refs_of_one)

    outs = pl.pallas_call(body, name=name, out_shape=[_sds(a.shape, F32) for a in ws] * 3,
                          compiler_params=_params())(*ws, *gs, *ms, *vs)
    return outs[:n], outs[n:2 * n], outs[2 * n:]


def adamw(w, g, m, v, name):
    n, R, C = w.shape

    def body(w_ref, g_ref, m_ref, v_ref, d_ref, nm_ref, nv_ref, go_ref):
        go_ref[...] = g_ref[...]
        _adamw_step(w_ref, g_ref, m_ref, v_ref, d_ref, nm_ref, nv_ref)

    tr = R
    while tr * C * 4 > ELEMENTWISE_BLOCK_BYTES and tr % 16 == 0:
        tr //= 2
    spec = pl.BlockSpec((None, tr, C), lambda i, r: (i, r, 0))
    return pl.pallas_call(
        body, name=name, grid=(n, R // tr), in_specs=[spec] * 4, out_specs=[spec] * 4,
        out_shape=[_sds((n, R, C), F32)] * 4, compiler_params=_params("parallel", "parallel"),
    )(w, g, m, v)


ANY = pl.BlockSpec(memory_space=pl.ANY)


def _coords():
    return lax.axis_index("x"), lax.axis_index("y"), lax.axis_index("c")


def _other_chips(mx, my):
    return [(1 - mx, my), (mx, 1 - my), (1 - mx, 1 - my)]


def _remote(src, dst, send, recv, dev):
    return pltpu.make_async_remote_copy(src_ref=src, dst_ref=dst, send_sem=send, recv_sem=recv, device_id=dev,
                                        device_id_type=MESH)


HBM =pl.BlockSpec(memory_space=pltpu.HBM)
SEM = pl.BlockSpec(memory_space=pltpu.SEMAPHORE)
_SPLIT_COPY = pltpu.CompilerParams(has_side_effects=pltpu.SideEffectType.DATAFLOW_SIDE_EFFECTING)


def _in_hbm(a):
    return pltpu.with_memory_space_constraint(a, pltpu.HBM)


def cast_into_slot(w, layer, chip_core, name, dtype=BF16):
    _, _, hR, C = w.shape

    def body(s_ref, w_ref, o_ref):
        del s_ref
        o_ref[...] = w_ref[...].astype(dtype)

    grid_spec = pltpu.PrefetchScalarGridSpec(
        num_scalar_prefetch=1, grid=(2,),
        in_specs=[pl.BlockSpec((None, None, hR, C), lambda h, s: (layer, h, 0, 0))],
        out_specs=pl.BlockSpec((None, None, hR, C), lambda h, s: (s[0], h, 0, 0)))
    return pl.pallas_call(body, name=name, grid_spec=grid_spec, out_shape=_sds_hbm((N_CHIPS, 2, hR, C), dtype),
                          compiler_params=_params("parallel"))(chip_core, w)


def gather_start(lands, groups, name):
    n = len(lands)
    n_groups = len(groups)

    def body(*refs):
        ins = refs[:n]
        sems = refs[n:n + 2 * n_groups]
        token = refs[-1]
        mx, my, mc = _coords()
        chip = 2 * mx + my
        for g, members in enumerate(groups):
            send, recv = sems[2 * g], sems[2 * g + 1]
            for i, a in enumerate(members):
                mine = ins[a].at[chip, mc]
                for k, (px, py) in enumerate(_other_chips(mx, my)):
                    _remote(mine, mine, send.at[3 * i + k], recv.at[3 * i + k], (px, py, mc)).start()
        token[...] = jnp.zeros_like(token)

    sem_shapes = []
    for members in groups:
        sem_shapes += [pltpu.SemaphoreType.DMA((3 * len(members),))] * 2
    outs = pl.pallas_call(
        body, name=name, in_specs=[HBM] * n,
        out_specs=[SEM] * (2 * n_groups) + [HBM] * n + [pl.BlockSpec(memory_space=pltpu.VMEM)],
        out_shape=sem_shapes + [pltpu.HBM(a.shape, a.dtype) for a in lands] + [_sds((SUBLANES, LANES), F32)],
        input_output_aliases={a: 2 * n_groups + a for a in range(n)}, compiler_params=_SPLIT_COPY,
    )(*[_in_hbm(a) for a in lands])
    sems = [(outs[2 * g], outs[2 * g + 1]) for g in range(n_groups)]
    return sems, list(outs[2 * n_groups:2 * n_groups + n]), outs[-1]


def gather_forward(lands, sems, after, name):
    n = len(lands)

    def body(*refs):
        ins = refs[:n]
        send, recv = refs[n], refs[n + 1]
        fsend, frecv = refs[n + 3], refs[n + 4]
        mx, my, mc = _coords()
        for i in range(n):
            for k, (px, py) in enumerate(_other_chips(mx, my)):
                landed = ins[i].at[2 * px + py, mc]
                cp = _remote(landed, landed, send.at[3 * i + k], recv.at[3 * i + k], (px, py, mc))
                cp.wait_send()
                cp.wait_recv()
                _remote(landed, landed, fsend.at[3 * i + k], frecv.at[3 * i + k], (mx, my, 1 - mc)).start()

    outs = pl.pallas_call(
        body, name=name, in_specs=[HBM] * n + [SEM, SEM, ANY], out_specs=[SEM, SEM] + [HBM] * n,
        out_shape=[pltpu.SemaphoreType.DMA((3 * n,))] * 2 + [pltpu.HBM(a.shape, a.dtype) for a in lands],
        input_output_aliases={a: 2 + a for a in range(n)}, compiler_params=_SPLIT_COPY,
    )(*lands, sems[0], sems[1], after)
    return (outs[0], outs[1]), list(outs[2:])


def gather_wait(lands, sems, after, name):
    n = len(lands)

    def body(*refs):
        ins = refs[:n]
        fsend, frecv = refs[n], refs[n + 1]
        mx, my, mc = _coords()
        for i in range(n):
            for k, (px, py) in enumerate(_other_chips(mx, my)):
                theirs = ins[i].at[2 * px + py, 1 - mc]
                cp = _remote(theirs, theirs, fsend.at[3 * i + k], frecv.at[3 * i + k], (mx, my, 1 - mc))
                cp.wait_send()
                cp.wait_recv()

    outs = pl.pallas_call(
        body, name=name, in_specs=[HBM] * n + [SEM, SEM, ANY], out_specs=[HBM] * n,
        out_shape=[pltpu.HBM(a.shape, a.dtype) for a in lands],
        input_output_aliases={a: a for a in range(n)}, compiler_params=_SPLIT_COPY,
    )(*lands, sems[0], sems[1], after)
    return list(outs)


def _peers(mx, my, mc):
    return [(1 - mx if k & 4 else mx, 1 - my if k & 2 else my, 1 - mc if k & 1 else mc) for k in range(1, N_DEV)]


def devices_start(x, name):
    def body(x_ref, land_ref, send, recv, x_thru, land_thru):
        mx, my, mc = _coords()
        me = 4 * mx + 2 * my + mc
        for k, peer in enumerate(_peers(mx, my, mc)):
            _remote(x_ref, land_ref.at[me], send.at[k], recv.at[k], peer).start()

    land = lax.empty((N_DEV,) + x.shape, x.dtype)
    outs = pl.pallas_call(
        body, name=name, in_specs=[HBM, HBM], out_specs=[SEM, SEM, HBM, HBM],
        out_shape=[pltpu.SemaphoreType.DMA((N_DEV - 1,))] * 2 + [pltpu.HBM(x.shape, x.dtype), pltpu.HBM(land.shape, x.dtype)],
        input_output_aliases={0: 2, 1: 3}, compiler_params=_SPLIT_COPY,
    )(_in_hbm(x), _in_hbm(land))
    return (outs[0], outs[1]), outs[2], outs[3]


def devices_wait(x, land, sems, after, name):
    def body(x_ref, land_ref, send, recv, after_ref, x_thru, land_thru):
        mx, my, mc = _coords()
        for k, (px, py, pc) in enumerate(_peers(mx, my, mc)):
            cp = _remote(x_ref, land_ref.at[4 * px + 2 * py + pc], send.at[k], recv.at[k], (px, py, pc))
            cp.wait_send()
            cp.wait_recv()

    outs = pl.pallas_call(
        body, name=name, in_specs=[HBM, HBM, SEM, SEM, ANY], out_specs=[HBM, HBM],
        out_shape=[pltpu.HBM(x.shape, x.dtype), pltpu.HBM(land.shape, land.dtype)],
        input_output_aliases={0: 0, 1: 1}, compiler_params=_SPLIT_COPY,
    )(x, land, sems[0], sems[1], after)
    return outs[0], outs[1]


def device_sum(land, own, me, name):
    _, R, C = land.shape

    def body(s_ref, l_ref, o_ref_in, o_ref):
        acc = None
        for q in range(N_DEV):
            term = jnp.where(s_ref[0] == q, o_ref_in[...], l_ref[q])
            acc = term if acc is None else acc + term
        o_ref[...] = acc

    grid_spec = pltpu.PrefetchScalarGridSpec(
        num_scalar_prefetch=1, grid=(1,),
        in_specs=[pl.BlockSpec((N_DEV, R, C), lambda i, s: (0, 0, 0)), pl.BlockSpec((R, C), lambda i, s: (0, 0))],
        out_specs=pl.BlockSpec((R, C), lambda i, s: (0, 0)))
    return pl.pallas_call(body, name=name, grid_spec=grid_spec, out_shape=_sds((R, C), F32),
                          compiler_params=_params("arbitrary"))(me, land, own)


def reduce_send(grads, name):
    n = len(grads)

    def body(*refs):
        ins, lands = refs[:n], refs[n:2 * n]
        send, recv = refs[2 * n], refs[2 * n + 1]
        mx, my, mc = _coords()
        me = 4 * mx + 2 * my + mc
        for a in range(n):
            for k, (px, py, pc) in enumerate(_peers(mx, my, mc)):
                _remote(ins[a].at[2 * px + py, pc], lands[a].at[me], send.at[7 * a + k], recv.at[7 * a + k], (px, py, pc)).start()

    lands = [lax.empty((N_DEV,) + g.shape[2:], g.dtype) for g in grads]
    outs = pl.pallas_call(
        body, name=name, in_specs=[HBM] * (2 * n), out_specs=[SEM, SEM] + [HBM] * (2 * n),
        out_shape=[pltpu.SemaphoreType.DMA((7 * n,))] * 2 + [pltpu.HBM(a.shape, a.dtype) for a in grads + lands],
        input_output_aliases={a: 2 + a for a in range(2 * n)}, compiler_params=_SPLIT_COPY,
    )(*[_in_hbm(a) for a in grads + lands])
    return (outs[0], outs[1]), list(outs[2:2 + n]), list(outs[2 + n:])


def reduce_wait(grads, lands, sems, after, name):
    n = len(grads)

    def body(*refs):
        ins, zones = refs[:n], refs[n:2 * n]
        send, recv = refs[2 * n], refs[2 * n + 1]
        mx, my, mc = _coords()
        for a in range(n):
            for k, (px, py, pc) in enumerate(_peers(mx, my, mc)):
                cp = _remote(ins[a].at[2 * px + py, pc], zones[a].at[4 * px + 2 * py + pc], send.at[7 * a + k],
                             recv.at[7 * a + k], (px, py, pc))
                cp.wait_send()
                cp.wait_recv()

    outs = pl.pallas_call(
        body, name=name, in_specs=[HBM] * (2 * n) + [SEM, SEM, ANY], out_specs=[HBM] * (2 * n),
        out_shape=[pltpu.HBM(a.shape, a.dtype) for a in grads + lands],
        input_output_aliases={a: a for a in range(2 * n)}, compiler_params=_SPLIT_COPY,
    )(*grads, *lands, sems[0], sems[1], after)
    return list(outs[:n]), list(outs[n:])


def reduce_sum(land, grad, place, name, into=None, layer=None):
    _, hR, C = land.shape
    tr = hR
    while N_DEV * tr * C * 2 > 3 * ELEMENTWISE_BLOCK_BYTES and tr % 32 == 0:
        tr //= 2

    def body(s_ref, l_ref, g_ref, *rest):
        o_ref = rest[-1]
        own = g_ref[...].astype(F32)
        acc = None
        for q in range(N_DEV):
            term = jnp.where(s_ref[2] == q, own, l_ref[q].astype(F32))
            acc = term if acc is None else acc + term
        o_ref[...] = acc

    in_specs = [pl.BlockSpec((N_DEV, tr, C), lambda i, s: (0, i, 0)),
                pl.BlockSpec((None, None, tr, C), lambda i, s: (s[0], s[1], i, 0))]
    args = [place, _in_hbm(land), _in_hbm(grad)]
    aliases = {}
    if layer is None:
        out_spec = pl.BlockSpec((None, tr, C), lambda i, s: (s[1], i, 0))
        out_shape = _sds_hbm((2, hR, C), F32)
    else:
        out_spec = pl.BlockSpec((None, None, tr, C), lambda i, s: (layer, s[1], i, 0))
        out_shape = _sds_hbm((2, 2, hR, C), F32)
        if into is not None:
            in_specs.append(ANY)
            args.append(into)
            aliases = {3: 0}
    grid_spec = pltpu.PrefetchScalarGridSpec(num_scalar_prefetch=1, grid=(hR // tr,), in_specs=in_specs, out_specs=out_spec)
    return pl.pallas_call(body, name=name, grid_spec=grid_spec, out_shape=out_shape, input_output_aliases=aliases,
                          compiler_params=_params("arbitrary"))(*args)


def join_halves(arrays, name):
    n = len(arrays)
    pieces = [(a, l) for a, arr in enumerate(arrays) for l in (range(arr.shape[0]) if arr.ndim == 4 else [None])]

    def body(*refs):
        ins = refs[:n]
        send, recv = refs[2 * n:]
        mx, my, mc = _coords()

        def half(a, l, h):
            return ins[a].at[h] if l is None else ins[a].at[l, h]

        sends = [_remote(half(a, l, mc), half(a, l, mc), send.at[i], recv.at[i], (mx, my, 1 - mc))
                 for i, (a, l) in enumerate(pieces)]
        for cp in sends:
            cp.start()
        for i, (a, l) in enumerate(pieces):
            theirs = half(a, l, 1 - mc)
            _remote(theirs, theirs, send.at[i], recv.at[i], (mx, my, 1 - mc)).wait_recv()
        for cp in sends:
            cp.wait_send()

    return pl.pallas_call(
        body, name=name, in_specs=[ANY] * n, out_specs=[ANY] * n, out_shape=[_sds(a.shape, a.dtype) for a in arrays],
        input_output_aliases={a: a for a in range(n)},
        scratch_shapes=[pltpu.SemaphoreType.DMA((len(pieces),)), pltpu.SemaphoreType.DMA((len(pieces),))],
    )(*arrays)


LANES = 128
SUBLANES = 8


def _n_rows(shape):
    rows = -(-int(np.prod(shape)) // LANES)
    return -(-rows // SUBLANES) * SUBLANES


def _as_rows(a):
    flat = a.reshape(-1)
    rows = _n_rows(a.shape)
    return jnp.pad(flat, (0, rows * LANES - flat.shape[0])).reshape(rows, LANES)


def _pack(arrays):
    return jnp.concatenate([_as_rows(a) for a in arrays], axis=0)


def _unpack(rows, shapes):
    out, r0 = [], 0
    for s in shapes:
        n = _n_rows(s)
        out.append(rows[r0:r0 + n].reshape(-1)[:int(np.prod(s))].reshape(s))
        r0 += n
    return out


REPLICATED_SMALL = [("rel_bias", (32, 16)), ("even_norm", (1, 1024)), ("even_pool_w", (1, 4, 128, 128)),
                    ("even_pool_scale", (1, 512)), ("odd_q_norm", (1, 64)), ("odd_k_norm", (1, 64)),
                    ("ffn_norm", (2, 1024)), ("ffn_conv_b", (2, 5632))]
SHARDED_SMALL = [("even_conv_w", (1, 3, 128)), ("odd_norm", (1, 256)), ("ffn_conv_w", (2, 3, 1408))]
WEIGHT_ORDER = ["rel_bias", "even_norm", "even_w_in", "even_conv_w", "even_pool_w", "even_pool_scale", "even_w_out",
                "odd_norm", "odd_w_qkv", "odd_q_norm", "odd_k_norm", "odd_w_o", "ffn_norm", "ffn_w_up", "ffn_conv_w",
                "ffn_conv_b", "ffn_w_down"]


def kernel(x, rel_bias, even_norm, even_w_in, even_conv_w, even_pool_w, even_pool_scale, even_w_out, odd_norm, odd_w_qkv, odd_q_norm, odd_k_norm, odd_w_o, ffn_norm, ffn_w_up, ffn_conv_w, ffn_conv_b, ffn_w_down, loss_target, m_rel_bias, m_even_norm, m_even_w_in, m_even_conv_w, m_even_pool_w, m_even_pool_scale, m_even_w_out, m_odd_norm, m_odd_w_qkv, m_odd_q_norm, m_odd_k_norm, m_odd_w_o, m_ffn_norm, m_ffn_w_up, m_ffn_conv_w, m_ffn_conv_b, m_ffn_w_down, v_rel_bias, v_even_norm, v_even_w_in, v_even_conv_w, v_even_pool_w, v_even_pool_scale, v_even_w_out, v_odd_norm, v_odd_w_qkv, v_odd_q_norm, v_odd_k_norm, v_odd_w_o, v_ffn_norm, v_ffn_w_up, v_ffn_conv_w, v_ffn_conv_b, v_ffn_w_down):
    W = dict(rel_bias=rel_bias, even_norm=even_norm, even_w_in=even_w_in, even_conv_w=even_conv_w, even_pool_w=even_pool_w,
             even_pool_scale=even_pool_scale, even_w_out=even_w_out, odd_norm=odd_norm, odd_w_qkv=odd_w_qkv,
             odd_q_norm=odd_q_norm, odd_k_norm=odd_k_norm, odd_w_o=odd_w_o, ffn_norm=ffn_norm, ffn_w_up=ffn_w_up,
             ffn_conv_w=ffn_conv_w, ffn_conv_b=ffn_conv_b, ffn_w_down=ffn_w_down)
    M1 = dict(rel_bias=m_rel_bias, even_norm=m_even_norm, even_w_in=m_even_w_in, even_conv_w=m_even_conv_w,
              even_pool_w=m_even_pool_w, even_pool_scale=m_even_pool_scale, even_w_out=m_even_w_out, odd_norm=m_odd_norm,
              odd_w_qkv=m_odd_w_qkv, odd_q_norm=m_odd_q_norm, odd_k_norm=m_odd_k_norm, odd_w_o=m_odd_w_o,
              ffn_norm=m_ffn_norm, ffn_w_up=m_ffn_w_up, ffn_conv_w=m_ffn_conv_w, ffn_conv_b=m_ffn_conv_b,
              ffn_w_down=m_ffn_w_down)
    M2 = dict(rel_bias=v_rel_bias, even_norm=v_even_norm, even_w_in=v_even_w_in, even_conv_w=v_even_conv_w,
              even_pool_w=v_even_pool_w, even_pool_scale=v_even_pool_scale, even_w_out=v_even_w_out, odd_norm=v_odd_norm,
              odd_w_qkv=v_odd_w_qkv, odd_q_norm=v_odd_q_norm, odd_k_norm=v_odd_k_norm, odd_w_o=v_odd_w_o,
              ffn_norm=v_ffn_norm, ffn_w_up=v_ffn_w_up, ffn_conv_w=v_ffn_conv_w, ffn_conv_b=v_ffn_conv_b,
              ffn_w_down=v_ffn_w_down)
    mx, my, mc = _coords()
    chip = 2 * mx + my
    me = 4 * mx + 2 * my + mc
    place = jnp.stack([chip, mc, me]).astype(jnp.int32)
    xs, target = x[0], loss_target[0]

    def halves(w):
        return w.reshape((w.shape[0], 2, w.shape[-2] // 2, w.shape[-1]))

    small_rows = jnp.pad(_pack([even_conv_w, odd_norm, ffn_conv_w]), ((0, SUBLANES), (0, 0)))
    first = [cast_into_slot(halves(even_w_in), 0, place, "cast_w_in"), cast_into_slot(halves(even_w_out), 0, place, "cast_w_out"),
             cast_into_slot(small_rows.reshape(1, 2, small_rows.shape[0] // 2, LANES), 0, place, "small_into_slot", dtype=F32),
             cast_into_slot(halves(ffn_w_up), 0, place, "cast_w_up0")]
    first_sems, first, token = gather_start(first, [[0, 1, 2], [3]], "gather_start_first")
    even_norm_after_start = even_norm + token[0:1, 0:1]

    def later(a):
        return lax.optimization_barrier((a, token))[0]

    down_f32 = halves(later(ffn_w_down))
    rest = [cast_into_slot(down_f32, 0, place, "cast_w_down0"),
            cast_into_slot(halves(later(odd_w_qkv)), 0, place, "cast_w_qkv"), cast_into_slot(halves(later(odd_w_o)), 0, place, "cast_w_o"),
            cast_into_slot(halves(later(ffn_w_up)), 1, place, "cast_w_up1"), cast_into_slot(down_f32, 1, place, "cast_w_down1")]
    rest_sems, rest, rest_token = gather_start(rest, [[0], [1, 2], [3], [4]], "gather_start_rest")
    group_arrays = [first[:3], [first[3]], [rest[0]], [rest[1], rest[2]], [rest[3]], [rest[4]]]
    group_sems = first_sems + rest_sems

    passing = {}

    def pass_on(group, tag, after, then):
        sems, arrays = gather_forward(group_arrays[group], group_sems[group], after, "gather_forward_" + tag)
        then, arrays = lax.optimization_barrier((then, arrays))
        passing[group] = (sems, arrays)
        return then

    def gathered(group, tag, after):
        sems, arrays = passing.pop(group)
        return gather_wait(arrays, sems, after, "gather_wait_" + tag)

    pool_w = cast_bf16(even_pool_w[0], "cast_pool_w")
    gqk = jnp.stack([jnp.tile(odd_q_norm[0], N_HEADS), jnp.tile(odd_k_norm[0], N_HEADS),
                     jnp.ones((D_MODEL,), F32)])[:, None, :]
    bias = bias_expand(later(rel_bias).T, "bias_expand").reshape(6, N_HEADS, ATT_BLOCK, 2 * ATT_BLOCK)
    xn0 = rmsnorm_fwd(xs, pass_on(0, "even", rest_token, even_norm_after_start), "even_norm")
    got = gathered(0, "even", xn0)
    w_in = got[0].reshape(N_CHIPS, 1, D_MODEL, EVEN_IN // N_CHIPS)
    w_out = got[1].reshape(1, 1, D_MODEL, D_MODEL)
    small = got[2].reshape(N_CHIPS, small_rows.shape[0], LANES)
    conv_w_full = small[:, 0:3].transpose(1, 0, 2).reshape(3, A_WIDTH)
    odd_norm_full = small[:, 8:10].reshape(1, D_MODEL)
    ffn_cw_full = small[:, 16:82].reshape(N_CHIPS, 2, 3, 2 * D_FF // N_CHIPS).transpose(1, 2, 0, 3).reshape(2, 3, 2 * D_FF)

    def ffn_fwd(l, xin, xn):
        up, u, act = up_glu_fwd(xn, w_up[l], ffn_cw_full[l], ffn_conv_b[l:l + 1], f"ffn{l}_up_glu")
        return act, (xin, xn, up, u, act)

    w_up, w_down = [None, None], [None, None]
    proj, mix = in_mixer_fwd(xn0, w_in, conv_w_full, pool_w, even_pool_scale, "even_in_mixer")
    x1, xn1 = mm_res_norm(mix, w_out, xs, ffn_norm[0:1], "even_out")
    pass_on(1, "up0", x1, x1)
    w_up[0] = gathered(1, "up0", x1)[0].reshape(N_CHIPS, 1, D_MODEL, 2 * D_FF // N_CHIPS)
    act0, ffn0 = ffn_fwd(0, x1, pass_on(2, "down0", x1, xn1))
    w_down[0] = gathered(2, "down0", act0)[0].reshape(1, 1, D_FF, D_MODEL)
    x2, xn2 = mm_res_norm(pass_on(3, "odd", act0, act0), w_down[0], x1, odd_norm_full, "ffn0_down")
    got = gathered(3, "odd", x2)
    xn2 = pass_on(5, "down1", x2, pass_on(4, "up1", x2, xn2))
    w_qkv = got[0].reshape(N_CHIPS, 1, D_MODEL, 3 * D_MODEL // N_CHIPS)
    w_o = got[1].reshape(1, 1, D_MODEL, D_MODEL)
    qkv, qkvn = qkv_qknorm_fwd(xn2, w_qkv, gqk, "odd_qkv_qknorm")
    att, lse = attn_fwd(qkvn, bias, "attn_fwd")
    x3, xn3 = mm_res_norm(att, w_o, x2, ffn_norm[1:2], "odd_out")
    w_up[1] = gathered(4, "up1", x3)[0].reshape(N_CHIPS, 1, D_MODEL, 2 * D_FF // N_CHIPS)
    act1, ffn1 = ffn_fwd(1, x3, xn3)
    w_down[1] = gathered(5, "down1", x3)[0].reshape(1, 1, D_FF, D_MODEL)
    dy, dyb, sq = mm_res_loss(act1, w_down[1], x3, target, "ffn1_down_loss")
    loss_part = (0.5 * jnp.sum(sq) * (1.0 / D_MODEL)).reshape(1, 1)

    def ffn_bwd(l, dy, dyb, saved):
        xin, xn, up, u, act = saved
        dw_down = mm_tn(act, dyb, f"ffn{l}_dw_down", J=1, tk=D_FF // 2, tm=1024)
        dact = mm_nt(dyb, w_down[l], f"ffn{l}_dact", tr=D_FF // 2, out_dtype=BF16, tm=1024)
        dup, dcw, dcb = glu_bwd(up, u, dact, ffn_cw_full[l], f"ffn{l}_glu_bwd")
        dw_up = mm_tn(xn, dup, f"ffn{l}_dw_up", J=N_CHIPS, tk=512, tm=1024, jb=2)
        dx, dxb, dg = mm_nt_norm_bwd(dup, w_up[l], xin, ffn_norm[l:l + 1], dy, f"ffn{l}_dx")
        return dx, dxb, (dw_down, dw_up, dcw, dcb, dg)

    def quarters(g):
        return g.reshape(N_CHIPS, 2, g.shape[0] * g.shape[1] // (2 * N_CHIPS), g.shape[-1])

    def reduce_start(grads, tag, then):
        sems, parts, zones = reduce_send([quarters(g) for g in grads], "reduce_send_" + tag)
        then, parts = lax.optimization_barrier((then, parts))
        return (sems, parts, zones), then

    dx3, dx3b, g_ffn1 = ffn_bwd(1, dy, dyb, ffn1)
    red_ffn1, (dx3, dx3b) = reduce_start([g_ffn1[1], g_ffn1[0]], "ffn1", (dx3, dx3b))
    dw_o = mm_tn(att, dx3b, "odd_dw_o", J=1, tk=512, tm=1024)
    datt = mm_nt(dx3b, w_o, "odd_datt", tr=D_MODEL, out_dtype=BF16)
    dq, dk, dv, dbias = attn_bwd(qkvn, att, datt, lse, bias, "attn_bwd")
    dqkv, dgqk = qknorm_bwd(qkv, dq, dk, dv, gqk, "odd_qknorm_bwd")
    dw_qkv = mm_tn(xn2, dqkv, "odd_dw_qkv", J=N_CHIPS, tk=512, tm=1024)
    red_odd, dqkv = reduce_start([dw_qkv, dw_o], "odd", dqkv)
    dx2, dx2b, dg_odd = mm_nt_norm_bwd(dqkv, w_qkv, x2, odd_norm_full, dx3, "odd_dx")
    dx1, dx1b, g_ffn0 = ffn_bwd(0, dx2, dx2b, ffn0)
    red_ffn0, (dx1, dx1b) = reduce_start([g_ffn0[1], g_ffn0[0]], "ffn0", (dx1, dx1b))
    dw_out = mm_tn(mix, dx1b, "even_dw_out", J=1, tk=512, tm=1024)
    dmix = mm_nt(dx1b, w_out, "even_dmix", tr=D_MODEL)
    dproj, dcw_even, dpw, dps = mixer_bwd(proj, dmix, conv_w_full, pool_w, even_pool_scale, "even_mixer_bwd")
    dw_in = mm_tn(xn0, dproj, "even_dw_in", J=N_CHIPS, tk=512, tm=1024)
    grad_x, _, dg_even = mm_nt_norm_bwd(dproj, w_in, xs, even_norm, dx1, "even_dx")
    d_rel = jnp.sum(bias_reduce(dbias.reshape(3, N_HEADS, 2 * ATT_BLOCK * ATT_BLOCK), "bias_reduce"), axis=0).T

    red_even, grad_x = reduce_start([dw_in, dw_out], "even", grad_x)

    dcw_sh = dcw_even.reshape(3, N_CHIPS, A_WIDTH // N_CHIPS).transpose(1, 0, 2)
    don_sh = dg_odd.reshape(N_CHIPS, D_MODEL // N_CHIPS)
    dfcw = jnp.stack([g_ffn0[2], g_ffn1[2]])
    dfcw_sh = dfcw.reshape(2, 3, N_CHIPS, 2 * D_FF // N_CHIPS).transpose(2, 0, 1, 3)
    rep_grads = [d_rel, dg_even, dpw[None], dps, _head_sum(dgqk[0]), _head_sum(dgqk[1]),
                 jnp.concatenate([g_ffn0[4], g_ffn1[4]], axis=0), jnp.concatenate([g_ffn0[3], g_ffn1[3]], axis=0)]
    rep_rows = _pack([loss_part] + rep_grads)
    n_loss = _n_rows(loss_part.shape)
    shard_rows = jnp.concatenate([_pack([dcw_sh[j], don_sh[j], dfcw_sh[j]]) for j in range(N_CHIPS)], axis=0)
    n_rep, n_shard = rep_rows.shape[0], shard_rows.shape[0] // N_CHIPS
    small_sems, small_rows, small_land = devices_start(jnp.concatenate([rep_rows, shard_rows], axis=0), "small_grads_start")
    grad_x, small_rows = lax.optimization_barrier((grad_x, small_rows))

    def reduce_end(red, tag, after):
        sems, parts, zones = red
        parts, zones = reduce_wait(parts, zones, sems, after, "reduce_wait_" + tag)
        return zones, parts

    z_ffn1, p_ffn1 = reduce_end(red_ffn1, "ffn1", grad_x)
    z_odd, p_odd = reduce_end(red_odd, "odd", grad_x)
    r_qkv = reduce_sum(z_odd[0], p_odd[0], place, "reduce_sum_w_qkv")
    r_o = reduce_sum(z_odd[1], p_odd[1], place, "reduce_sum_w_o")
    r_up = reduce_sum(z_ffn1[0], p_ffn1[0], place, "reduce_sum_w_up1", layer=1)
    r_down = reduce_sum(z_ffn1[1], p_ffn1[1], place, "reduce_sum_w_down1", layer=1)
    r_qkv, r_o, r_up, r_down = lax.optimization_barrier((r_qkv, r_o, r_up, r_down))
    z_ffn0, p_ffn0 = reduce_end(red_ffn0, "ffn0", r_down)
    r_up = reduce_sum(z_ffn0[0], p_ffn0[0], place, "reduce_sum_w_up0", into=r_up, layer=0)
    r_down = reduce_sum(z_ffn0[1], p_ffn0[1], place, "reduce_sum_w_down0", into=r_down, layer=0)
    later = ["odd_w_qkv", "odd_w_o", "ffn_w_up", "ffn_w_down"]
    joined = join_halves([r_qkv, r_o, r_up, r_down], "grads_join_late_layers")
    G = {nm: g.reshape(W[nm].shape) for nm, g in zip(later, joined)}

    D_, NM, NV = {}, {}, {}

    def update(nm):
        as3 = lambda a: a.reshape((-1,) + a.shape[-2:])
        outs = adamw(as3(W[nm]), as3(G[nm]), as3(M1[nm]), as3(M2[nm]), "adamw_" + nm)
        D_[nm], NM[nm], NV[nm], G[nm] = [o.reshape(W[nm].shape) for o in outs]

    def all_before(names):
        tied = lax.optimization_barrier([D_[nm] for nm in names])
        for nm, d in zip(names, tied):
            D_[nm] = d
        return tied[0]

    for nm in later:
        update(nm)
    z_even, p_even = reduce_end(red_even, "even", all_before(later))
    joined = join_halves([reduce_sum(z_even[0], p_even[0], place, "reduce_sum_w_in"),
                          reduce_sum(z_even[1], p_even[1], place, "reduce_sum_w_out")], "grads_join_first_layer")
    first = ["even_w_in", "even_w_out"]
    for nm, g in zip(first, joined):
        G[nm] = g.reshape(W[nm].shape)
        update(nm)
    small_rows, small_land = devices_wait(small_rows, small_land, small_sems, all_before(first), "small_grads_wait")
    small_sum = device_sum(small_land, small_rows, place[2:3], "small_grads_sum")
    mine = lax.dynamic_slice_in_dim(small_sum, n_rep + chip * n_shard, n_shard, axis=0)
    loss = small_sum[0, 0]
    g_small = jnp.concatenate([small_sum[n_loss:n_rep], mine], axis=0)
    small_names = [n for n, _ in REPLICATED_SMALL + SHARDED_SMALL]
    small_shapes = [s for _, s in REPLICATED_SMALL + SHARDED_SMALL]
    G.update(dict(zip(small_names, _unpack(g_small, small_shapes))))
    outs = adamw_small(*[[d[n] for n in small_names] for d in (W, G, M1, M2)], "adamw_small")
    for dst, o in zip((D_, NM, NV), outs):
        dst.update(dict(zip(small_names, o)))

    return (loss, grad_x[None], *[G[n] for n in WEIGHT_ORDER], *[D_[n] for n in WEIGHT_ORDER],
            *[NM[n] for n in WEIGHT_ORDER], *[NV[n] for n in WEIGHT_ORDER])


def _head_sum(dg):
    return jnp.sum(dg.reshape(N_HEADS, HEAD_DIM), axis=0, keepdims=True)
```

```python
import functools
import math

import numpy as np
import jax
import jax.numpy as jnp
from jax import lax
from jax.experimental import pallas as pl
from jax.experimental.pallas import tpu as pltpu

F32 = jnp.float32
BF16 = jnp.bfloat16

D_MODEL = 1024
N_HEADS = 16
HEAD_DIM = 64
A_WIDTH = 512
POOL_WINDOWS = (2, 4, 8, 16)
POOL_GROUP = 128
EVEN_IN = 2048
D_FF = 2816
DILATED_PAIRS = ((128, 1), (512, 4), (2048, 16))
ATT_BLOCK = 128
N_REL_BUCKETS = 32
REL_MAX_DISTANCE = 2048
EPS = 1e-6
MASK_VALUE = -1e30
ADAM_LR, ADAM_B1, ADAM_B2, ADAM_EPS, ADAM_WD, ADAM_STEP = 0.001, 0.9, 0.999, 1e-08, 0.01, 10

VMEM_LIMIT_BYTES = 48 * 1024 * 1024
ELEMENTWISE_BLOCK_BYTES = 2 * 1024 * 1024
N_CHIPS = 4
N_DEV = 8
MESH = pl.DeviceIdType.MESH


def _params(*sem):
    return pltpu.CompilerParams(dimension_semantics=sem if sem else None, vmem_limit_bytes=VMEM_LIMIT_BYTES)


def _sds(shape, dtype):
    return jax.ShapeDtypeStruct(tuple(shape), dtype)


def _sds_hbm(shape, dtype):
    return pltpu.HBM(tuple(shape), dtype)


def cast_bf16(x, name, tr=None):
    lead, (R, C) = x.shape[:-2], x.shape[-2:]
    n = int(np.prod(lead)) if lead else 1
    x3 = x.reshape((n, R, C))
    tr = tr or R

    def body(x_ref, o_ref):
        o_ref[...] = x_ref[...].astype(BF16)

    out = pl.pallas_call(
        body, name=name, grid=(n, R // tr),
        in_specs=[pl.BlockSpec((None, tr, C), lambda i, r: (i, r, 0))],
        out_specs=pl.BlockSpec((None, tr, C), lambda i, r: (i, r, 0)),
        out_shape=_sds((n, R, C), BF16), compiler_params=_params("parallel", "parallel"),
    )(x3)
    return out.reshape(lead + (R, C))


def rmsnorm_fwd(x, g, name, ts=512):
    S, Dm = x.shape

    def body(x_ref, g_ref, o_ref):
        xv = x_ref[...]
        r = lax.rsqrt(jnp.mean(xv * xv, axis=-1, keepdims=True) + EPS)
        o_ref[...] = ((xv * r) * g_ref[...]).astype(BF16)

    return pl.pallas_call(
        body, name=name, grid=(S // ts,),
        in_specs=[pl.BlockSpec((ts, Dm), lambda i: (i, 0)), pl.BlockSpec((1, Dm), lambda i: (0, 0))],
        out_specs=pl.BlockSpec((ts, Dm), lambda i: (i, 0)),
        out_shape=_sds((S, Dm), BF16), compiler_params=_params("parallel"),
    )(x, g)


def mm_res_norm(a, w, res, gain, name, tm=1024):
    M, K = a.shape
    Dm = w.shape[-1]

    def body(a_ref, w_ref, r_ref, g_ref, y_ref, yn_ref):
        y = r_ref[...] + jnp.dot(a_ref[...], w_ref[...], preferred_element_type=F32)
        y_ref[...] = y
        r = lax.rsqrt(jnp.mean(y * y, axis=-1, keepdims=True) + EPS)
        yn_ref[...] = ((y * r) * g_ref[...]).astype(BF16)

    row = pl.BlockSpec((tm, Dm), lambda m: (m, 0))
    return pl.pallas_call(
        body, name=name, grid=(M // tm,),
        in_specs=[pl.BlockSpec((tm, K), lambda m: (m, 0)),
                  pl.BlockSpec((None, None, K, Dm), lambda m: (0, 0, 0, 0), pipeline_mode=pl.Buffered(1)),
                  row, pl.BlockSpec((1, Dm), lambda m: (0, 0))],
        out_specs=[row, row], out_shape=[_sds((M, Dm), F32), _sds((M, Dm), BF16)],
        compiler_params=_params("parallel"),
    )(a, w, res, gain)


def mm_res_loss(a, w, res, target, name, tm=512):
    M, K = a.shape
    Dm = w.shape[-1]

    def body(a_ref, w_ref, r_ref, t_ref, d_ref, db_ref, s_ref):
        e = (r_ref[...] + jnp.dot(a_ref[...], w_ref[...], preferred_element_type=F32)) - t_ref[...]
        d = e * (1.0 / Dm)
        d_ref[...] = d
        db_ref[...] = d.astype(BF16)
        part = jnp.sum(e * e, axis=0, keepdims=True)

        @pl.when(pl.program_id(0) == 0)
        def _():
            s_ref[...] = part

        @pl.when(pl.program_id(0) > 0)
        def _():
            s_ref[...] += part

    row = pl.BlockSpec((tm, Dm), lambda m: (m, 0))
    return pl.pallas_call(
        body, name=name, grid=(M // tm,),
        in_specs=[pl.BlockSpec((tm, K), lambda m: (m, 0)),
                  pl.BlockSpec((None, None, K, Dm), lambda m: (0, 0, 0, 0), pipeline_mode=pl.Buffered(1)), row, row],
        out_specs=[row, row, pl.BlockSpec((1, Dm), lambda m: (0, 0))],
        out_shape=[_sds((M, Dm), F32), _sds((M, Dm), BF16), _sds((1, Dm), F32)],
        compiler_params=_params("arbitrary"),
    )(a, w, res, target)


def mm_nt(dy, w, name, tr, layer=0, out_dtype=F32, tm=512):
    M = dy.shape[0]
    J, _, R, Ns = w.shape
    dims = (((1,), (1,)), ((), ()))

    def body(dy_ref, w_ref, o_ref):
        acc = None
        for j in range(J):
            p = lax.dot_general(dy_ref[:, j * Ns:(j + 1) * Ns], w_ref[j], dims, preferred_element_type=F32)
            acc = p if acc is None else acc + p
        o_ref[...] = acc.astype(o_ref.dtype)

    return pl.pallas_call(
        body, name=name, grid=(R // tr, M // tm),
        in_specs=[pl.BlockSpec((tm, J * Ns), lambda r, m: (m, 0)),
                  pl.BlockSpec((J, None, tr, Ns), lambda r, m: (0, layer, r, 0))],
        out_specs=pl.BlockSpec((tm, tr), lambda r, m: (m, r)),
        out_shape=_sds((M, R), out_dtype),
        compiler_params=_params("parallel", "parallel"),
    )(dy, w)


def mm_nt_norm_bwd(dy, w, x, g, dres, name, layer=0, tm=512):
    M = dy.shape[0]
    J, _, Dm, Ns = w.shape
    dims = (((1,), (1,)), ((), ()))

    def body(dy_ref, w_ref, x_ref, g_ref, r_ref, dx_ref, dxb_ref, dg_ref):
        dxn = None
        for j in range(J):
            p = lax.dot_general(dy_ref[:, j * Ns:(j + 1) * Ns], w_ref[j], dims, preferred_element_type=F32)
            dxn = p if dxn is None else dxn + p
        xv = x_ref[...]
        r = lax.rsqrt(jnp.mean(xv * xv, axis=-1, keepdims=True) + EPS)
        gx = dxn * g_ref[...]
        dot = jnp.sum(gx * xv, axis=-1, keepdims=True)
        dx = r_ref[...] + r * gx - xv * ((r * r * r) * (dot * (1.0 / Dm)))
        dx_ref[...] = dx
        dxb_ref[...] = dx.astype(BF16)
        part = jnp.sum(dxn * (xv * r), axis=0, keepdims=True)

        @pl.when(pl.program_id(0) == 0)
        def _():
            dg_ref[...] = part

        @pl.when(pl.program_id(0) > 0)
        def _():
            dg_ref[...] += part

    row = pl.BlockSpec((tm, Dm), lambda m: (m, 0))
    vec = pl.BlockSpec((1, Dm), lambda m: (0, 0))
    return pl.pallas_call(
        body, name=name, grid=(M // tm,),
        in_specs=[pl.BlockSpec((tm, J * Ns), lambda m: (m, 0)),
                  pl.BlockSpec((J, None, Dm, Ns), lambda m: (0, layer, 0, 0), pipeline_mode=pl.Buffered(1)), row, vec, row],
        out_specs=[row, row, vec],
        out_shape=[_sds((M, Dm), F32), _sds((M, Dm), BF16), _sds((1, Dm), F32)],
        compiler_params=_params("arbitrary"),
    )(dy, w, x, g, dres)


def mm_tn(a, dy, name, J, tk, tm=512, jb=None):
    M, K = a.shape
    jb = jb or J
    Ns = dy.shape[1] // J
    N = jb * Ns
    n_m = M // tm
    dims = (((0,), (0,)), ((), ()))

    def body(a_ref, dy_ref, o_ref, acc_ref):
        p = lax.dot_general(a_ref[...], dy_ref[...], dims, preferred_element_type=F32)
        m = pl.program_id(2)

        @pl.when(m == 0)
        def _():
            acc_ref[...] = p

        @pl.when(m > 0)
        def _():
            acc_ref[...] += p

        @pl.when(m == n_m - 1)
        def _():
            for j in range(jb):
                o_ref[j] = acc_ref[:, j * Ns:(j + 1) * Ns].astype(BF16)

    return pl.pallas_call(
        body, name=name, grid=(J // jb, K // tk, n_m),
        in_specs=[pl.BlockSpec((tm, tk), lambda g, k, m: (m, k)), pl.BlockSpec((tm, N), lambda g, k, m: (m, g))],
        out_specs=pl.BlockSpec((jb, tk, Ns), lambda g, k, m: (g, k, 0)),
        out_shape=_sds((J, K, Ns), BF16), scratch_shapes=[pltpu.VMEM((tk, N), F32)],
        compiler_params=_params("parallel", "parallel", "arbitrary"),
    )(a, dy)


HALO = 16


def _shift_down(x, s):
    return pltpu.roll(x, s, 0)


def _shift_up(x, s):
    return pltpu.roll(x, x.shape[0] - s, 0)


def _conv3(z, cw):
    return (_shift_down(z, 2) * cw[0:1] + _shift_down(z, 1) * cw[1:2]) + z * cw[2:3]


def _window_count(first_row, n, k):
    t = first_row + lax.broadcasted_iota(jnp.int32, (n, 1), 0)
    return jnp.clip(t + 1, 1, k).astype(F32)


def in_mixer_fwd(xn, w_in, conv_w, pool_w, pool_scale, name, ts=512):
    S, K = xn.shape
    n = ts + HALO

    def body(xm_ref, xb_ref, w_ref, cw_ref, pw_ref, ps_ref, p_ref, o_ref):
        i = pl.program_id(0)
        before = jnp.where(i > 0, xb_ref[...], jnp.zeros_like(xb_ref))
        rows = jnp.concatenate([before, xm_ref[...]], axis=0)
        h, gb, gc, pin = [jnp.dot(rows, w_ref[j], preferred_element_type=F32) for j in range(N_CHIPS)]
        for j, part in enumerate((h, gb, gc, pin)):
            p_ref[:, j * A_WIDTH:(j + 1) * A_WIDTH] = part[HALO:]
        cz = _conv3(gc * h, cw_ref[...])
        o_ref[:, 0:A_WIDTH] = (gb[HALO:] * cz[HALO:]).astype(BF16)
        for g, k in enumerate(POOL_WINDOWS):
            p = pin[:, g * POOL_GROUP:(g + 1) * POOL_GROUP]
            w = p
            s = 1
            while s < k:
                w = w + _shift_down(w, s)
                s *= 2
            pooled = w / _window_count(i * ts - HALO, n, k) - p
            yb = jnp.dot(pooled[HALO:].astype(BF16), pw_ref[g], preferred_element_type=F32)
            yb = yb * ps_ref[:, g * POOL_GROUP:(g + 1) * POOL_GROUP]
            o_ref[:, A_WIDTH + g * POOL_GROUP:A_WIDTH + (g + 1) * POOL_GROUP] = yb.astype(BF16)

    hb = ts // HALO
    return pl.pallas_call(
        body, name=name, grid=(S // ts,),
        in_specs=[
            pl.BlockSpec((ts, K), lambda i: (i, 0)),
            pl.BlockSpec((HALO, K), lambda i: (jnp.maximum(i * hb - 1, 0), 0)),
            pl.BlockSpec((N_CHIPS, None, K, A_WIDTH), lambda i: (0, 0, 0, 0), pipeline_mode=pl.Buffered(1)),
            pl.BlockSpec((3, A_WIDTH), lambda i: (0, 0)),
            pl.BlockSpec((4, POOL_GROUP, POOL_GROUP), lambda i: (0, 0, 0)),
            pl.BlockSpec((1, 4 * POOL_GROUP), lambda i: (0, 0)),
        ],
        out_specs=[pl.BlockSpec((ts, EVEN_IN), lambda i: (i, 0)), pl.BlockSpec((ts, D_MODEL), lambda i: (i, 0))],
        out_shape=[_sds((S, EVEN_IN), F32), _sds((S, D_MODEL), BF16)], compiler_params=_params("parallel"),
    )(xn, xn, w_in, conv_w, pool_w, pool_scale)


def mixer_bwd(proj, dmix, conv_w, pool_w, pool_scale, name, ts=256):
    S = proj.shape[0]
    n = ts + 2 * HALO
    nt = S // ts
    tn_dims = (((0,), (0,)), ((), ()))
    nt_dims = (((1,), (1,)), ((), ()))

    def body(pm_ref, pb_ref, pa_ref, dm_ref, da_ref, cw_ref, pw_ref, ps_ref, o_ref, dcw_ref, dpw_ref, dps_ref):
        i = pl.program_id(0)
        last = i == nt - 1
        before = jnp.where(i > 0, pb_ref[...], 0.0)
        after = jnp.where(last, 0.0, pa_ref[...])
        ext = jnp.concatenate([before, pm_ref[...], after], axis=0)
        dafter = jnp.where(last, 0.0, da_ref[...])
        dext = jnp.concatenate([jnp.zeros((HALO, D_MODEL), F32), dm_ref[...], dafter], axis=0)
        cw = cw_ref[...]
        main = slice(HALO, HALO + ts)

        @pl.when(i == 0)
        def _():
            dcw_ref[...] = jnp.zeros_like(dcw_ref)
            dpw_ref[...] = jnp.zeros_like(dpw_ref)
            dps_ref[...] = jnp.zeros_like(dps_ref)

        h, gb, gc = ext[:, 0:A_WIDTH], ext[:, A_WIDTH:2 * A_WIDTH], ext[:, 2 * A_WIDTH:3 * A_WIDTH]
        z = gc * h
        z1, z2 = _shift_down(z, 1), _shift_down(z, 2)
        cz = (z2 * cw[0:1] + z1 * cw[1:2]) + z * cw[2:3]
        dya = dext[:, 0:A_WIDTH]
        dcz = dya * gb
        dz = dcz * cw[2:3] + _shift_up(dcz, 1) * cw[1:2] + _shift_up(dcz, 2) * cw[0:1]
        o_ref[:, 0:A_WIDTH] = (dz * gc)[main].astype(BF16)
        o_ref[:, A_WIDTH:2 * A_WIDTH] = (dya * cz)[main].astype(BF16)
        o_ref[:, 2 * A_WIDTH:3 * A_WIDTH] = (dz * h)[main].astype(BF16)
        dczm = dcz[main]
        dcw_ref[0:1, :] += jnp.sum(dczm * z2[main], axis=0, keepdims=True)
        dcw_ref[1:2, :] += jnp.sum(dczm * z1[main], axis=0, keepdims=True)
        dcw_ref[2:3, :] += jnp.sum(dczm * z[main], axis=0, keepdims=True)

        for g, k in enumerate(POOL_WINDOWS):
            lo = 3 * A_WIDTH + g * POOL_GROUP
            cols = slice(g * POOL_GROUP, (g + 1) * POOL_GROUP)
            p = ext[:, lo:lo + POOL_GROUP]
            w = p
            s = 1
            while s < k:
                w = w + _shift_down(w, s)
                s *= 2
            cnt = _window_count(i * ts - HALO, n, k)
            pooled = (w / cnt - p)[main].astype(BF16)
            dyb = dext[:, A_WIDTH + g * POOL_GROUP:A_WIDTH + (g + 1) * POOL_GROUP]
            e = dyb * ps_ref[:, cols]
            pre = jnp.dot(pooled, pw_ref[g], preferred_element_type=F32)
            dps_ref[:, cols] += jnp.sum(dyb[main] * pre, axis=0, keepdims=True)
            dpw_ref[g] += lax.dot_general(pooled, e[main].astype(BF16), tn_dims, preferred_element_type=F32)
            dpooled = lax.dot_general(e.astype(BF16), pw_ref[g], nt_dims, preferred_element_type=F32)
            q = dpooled / cnt
            a = q
            s = 1
            while s < k:
                a = a + _shift_up(a, s)
                s *= 2
            o_ref[:, lo:lo + POOL_GROUP] = (a - dpooled)[main].astype(BF16)

    hb = ts // HALO
    nh = S // HALO
    before_map = lambda i: (jnp.maximum(i * hb - 1, 0), 0)
    after_map = lambda i: (jnp.minimum((i + 1) * hb, nh - 1), 0)
    full = lambda *shape: pl.BlockSpec(shape, lambda i: (0,) * len(shape))
    return pl.pallas_call(
        body, name=name, grid=(nt,),
        in_specs=[
            pl.BlockSpec((ts, EVEN_IN), lambda i: (i, 0)),
            pl.BlockSpec((HALO, EVEN_IN), before_map),
            pl.BlockSpec((HALO, EVEN_IN), after_map),
            pl.BlockSpec((ts, D_MODEL), lambda i: (i, 0)),
            pl.BlockSpec((HALO, D_MODEL), after_map),
            full(3, A_WIDTH), full(4, POOL_GROUP, POOL_GROUP), full(1, 4 * POOL_GROUP),
        ],
        out_specs=[pl.BlockSpec((ts, EVEN_IN), lambda i: (i, 0)), full(3, A_WIDTH), full(4, POOL_GROUP, POOL_GROUP),
                   full(1, 4 * POOL_GROUP)],
        out_shape=[_sds((S, EVEN_IN), BF16), _sds((3, A_WIDTH), F32), _sds((4, POOL_GROUP, POOL_GROUP), F32),
                   _sds((1, 4 * POOL_GROUP), F32)],
        compiler_params=_params("arbitrary"),
    )(proj, proj, proj, dmix, dmix, conv_w, pool_w, pool_scale)


FFN_HALO = 16
FFN_TC = 1408


GLU_CHUNKS = ((0, 512), (512, 512), (1024, 384))


def up_glu_fwd(xn, w_up, conv_w, conv_b, name, tm=512):
    S, K = xn.shape
    nc = D_FF // FFN_TC

    def body(xm_ref, xb_ref, wg_ref, wu_ref, cwg_ref, cwu_ref, cbg_ref, cbu_ref, pg_ref, pu_ref, ug_ref, uu_ref, o_ref):
        before = jnp.where(pl.program_id(1) > 0, xb_ref[...], jnp.zeros_like(xb_ref))
        rows = jnp.concatenate([before, xm_ref[...]], axis=0)
        for lo, width in GLU_CHUNKS:
            cols = slice(lo, lo + width)
            pre_g = jnp.dot(rows, wg_ref[:, cols], preferred_element_type=F32)
            pre_u = jnp.dot(rows, wu_ref[:, cols], preferred_element_type=F32)
            gate = _conv3(pre_g, cwg_ref[:, cols])[FFN_HALO:] + cbg_ref[:, cols]
            upv = _conv3(pre_u, cwu_ref[:, cols])[FFN_HALO:] + cbu_ref[:, cols]
            pg_ref[:, cols] = pre_g[FFN_HALO:].astype(BF16)
            pu_ref[:, cols] = pre_u[FFN_HALO:].astype(BF16)
            ug_ref[:, cols] = gate.astype(BF16)
            uu_ref[:, cols] = upv.astype(BF16)
            o_ref[:, cols] = ((gate * (1.0 / (1.0 + jnp.exp(-gate)))) * upv).astype(BF16)

    hb = tm // FFN_HALO
    wspec = lambda off: pl.BlockSpec((None, None, K, FFN_TC), lambda j, m: (j + off, 0, 0, 0))
    cw = lambda off: pl.BlockSpec((3, FFN_TC), lambda j, m: (0, j + off))
    cb = lambda off: pl.BlockSpec((1, FFN_TC), lambda j, m: (0, j + off))
    out = pl.BlockSpec((tm, FFN_TC), lambda j, m: (m, j))
    pg, pu, ug, uu, act = pl.pallas_call(
        body, name=name, grid=(nc, S // tm),
        in_specs=[pl.BlockSpec((tm, K), lambda j, m: (m, 0)),
                  pl.BlockSpec((FFN_HALO, K), lambda j, m: (jnp.maximum(m * hb - 1, 0), 0)),
                  wspec(0), wspec(nc), cw(0), cw(nc), cb(0), cb(nc)],
        out_specs=[out] * 5, out_shape=[_sds((S, D_FF), BF16)] * 5,
        compiler_params=_params("parallel", "parallel"),
    )(xn, xn, w_up, w_up, conv_w, conv_w, conv_b, conv_b)
    return (pg, pu), (ug, uu), act


def glu_bwd(up, u, da, conv_w, name, ts=256):
    S = up[0].shape[0]
    nc = D_FF // FFN_TC
    nt = S // ts
    W = 2 * D_FF

    def body(xg_ref, xu_ref, gm_ref, ga_ref, um_ref, ua_ref, dm_ref, da_ref, cw_ref, dx_ref, dcw_ref, dcb_ref):
        i = pl.program_id(0)
        last = i == nt - 1

        @pl.when(i == 0)
        def _():
            dcw_ref[...] = jnp.zeros_like(dcw_ref)
            dcb_ref[...] = jnp.zeros_like(dcb_ref)

        def rows(m_ref, a_ref, cols):
            return jnp.concatenate([m_ref[:, cols], a_ref[:, cols]], axis=0).astype(F32)

        def back(d, x, cols):
            cw = cw_ref[:, cols]
            d1, d2 = _shift_up(d, 1), _shift_up(d, 2)
            dx_ref[:, cols] = ((d * cw[2:3] + d1 * cw[1:2]) + d2 * cw[0:1])[:ts].astype(BF16)
            dcb_ref[:, cols] += jnp.sum(d[:ts], axis=0, keepdims=True)
            dcw_ref[0:1, cols] += jnp.sum(d2[:ts] * x, axis=0, keepdims=True)
            dcw_ref[1:2, cols] += jnp.sum(d1[:ts] * x, axis=0, keepdims=True)
            dcw_ref[2:3, cols] += jnp.sum(d[:ts] * x, axis=0, keepdims=True)

        for c in range(nc):
            cols = slice(c * FFN_TC, (c + 1) * FFN_TC)
            ug, uu = rows(gm_ref, ga_ref, cols), rows(um_ref, ua_ref, cols)
            dae = rows(dm_ref, da_ref, cols)
            dae = jnp.where(last & (lax.broadcasted_iota(jnp.int32, dae.shape, 0) >= ts), 0.0, dae)
            sg = 1.0 / (1.0 + jnp.exp(-ug))
            duu = dae * (ug * sg)
            dug = (dae * uu) * (sg * (1.0 + ug * (1.0 - sg)))
            back(dug, xg_ref[:, cols].astype(F32), cols)
            back(duu, xu_ref[:, cols].astype(F32), slice(D_FF + c * FFN_TC, D_FF + (c + 1) * FFN_TC))

    hb = ts // FFN_HALO
    nh = S // FFN_HALO
    after_map = lambda i: (jnp.minimum((i + 1) * hb, nh - 1), 0)
    main = pl.BlockSpec((ts, D_FF), lambda i: (i, 0))
    after = pl.BlockSpec((FFN_HALO, D_FF), after_map)
    return pl.pallas_call(
        body, name=name, grid=(nt,),
        in_specs=[main, main, main, after, main, after, main, after, pl.BlockSpec((3, W), lambda i: (0, 0))],
        out_specs=[pl.BlockSpec((ts, W), lambda i: (i, 0)), pl.BlockSpec((3, W), lambda i: (0, 0)),
                   pl.BlockSpec((1, W), lambda i: (0, 0))],
        out_shape=[_sds((S, W), BF16), _sds((3, W), F32), _sds((1, W), F32)],
        compiler_params=_params("arbitrary"),
    )(up[0], up[1], u[0], u[0], u[1], u[1], da, da, conv_w)


MEAN_GROUP = 256


def _head_mean_matrix():
    h = np.arange(MEAN_GROUP) // HEAD_DIM
    return jnp.asarray((h[:, None] == h[None, :]).astype(np.float32) / HEAD_DIM, dtype=BF16)


def _head_mean(v, gm):
    vb = v.astype(BF16)
    return jnp.concatenate([jnp.dot(vb[:, c:c + MEAN_GROUP], gm, preferred_element_type=F32)
                            for c in range(0, v.shape[1], MEAN_GROUP)], axis=1)


def qkv_qknorm_fwd(xn, w_qkv, gqk, name, tm=1024):
    S, K = xn.shape
    J, _, _, Ns = w_qkv.shape
    gains = gqk.reshape(1, 3 * D_MODEL)

    def body(x_ref, w_ref, g_ref, gm_ref, raw_ref, o_ref):
        first_col = pl.program_id(0) * Ns
        acc = jnp.dot(x_ref[...], w_ref[...], preferred_element_type=F32)
        raw_ref[...] = acc
        gm = gm_ref[...]
        for c in range(0, Ns, MEAN_GROUP):
            cols = slice(c, c + MEAN_GROUP)
            x = acc[:, cols]
            mean = jnp.dot((x * x).astype(BF16), gm, preferred_element_type=F32)
            normed = (x * lax.rsqrt(mean + EPS)) * g_ref[:, cols]
            o_ref[:, cols] = jnp.where(first_col + c >= 2 * D_MODEL, x, normed).astype(BF16)

    return pl.pallas_call(
        body, name=name, grid=(J, S // tm),
        in_specs=[pl.BlockSpec((tm, K), lambda j, m: (m, 0)), pl.BlockSpec((None, None, K, Ns), lambda j, m: (j, 0, 0, 0)),
                  pl.BlockSpec((1, Ns), lambda j, m: (0, j)), pl.BlockSpec((MEAN_GROUP, MEAN_GROUP), lambda j, m: (0, 0))],
        out_specs=[pl.BlockSpec((tm, Ns), lambda j, m: (m, j))] * 2,
        out_shape=[_sds((S, J * Ns), F32), _sds((S, J * Ns), BF16)], compiler_params=_params("parallel", "parallel"),
    )(xn, w_qkv, gains, _head_mean_matrix())


def qknorm_bwd(qkv, dq, dk, dv, gqk, name, ts=256):
    S = qkv.shape[0]

    def body(x_ref, dq_ref, dk_ref, dv_ref, g_ref, gm_ref, o_ref, dg_ref):
        @pl.when(pl.program_id(0) == 0)
        def _():
            dg_ref[...] = jnp.zeros_like(dg_ref)

        gm = gm_ref[...]
        for part, d_ref in enumerate((dq_ref, dk_ref)):
            cols = slice(part * D_MODEL, (part + 1) * D_MODEL)
            x = x_ref[:, cols]
            d = d_ref[...]
            r = lax.rsqrt(_head_mean(x * x, gm) + EPS)
            gx = d * g_ref[part]
            o_ref[:, cols] = (r * gx - x * ((r * r * r) * _head_mean(gx * x, gm))).astype(BF16)
            dg_ref[part] += jnp.sum(d * (x * r), axis=0, keepdims=True)
        o_ref[:, 2 * D_MODEL:] = dv_ref[...].astype(BF16)

    row = pl.BlockSpec((ts, D_MODEL), lambda i: (i, 0))
    wide = pl.BlockSpec((ts, 3 * D_MODEL), lambda i: (i, 0))
    gains = pl.BlockSpec((3, 1, D_MODEL), lambda i: (0, 0, 0))
    return pl.pallas_call(
        body, name=name, grid=(S // ts,),
        in_specs=[wide, row, row, row, gains, pl.BlockSpec((MEAN_GROUP, MEAN_GROUP), lambda i: (0, 0))],
        out_specs=[wide, gains],
        out_shape=[_sds((S, 3 * D_MODEL), BF16), _sds((3, 1, D_MODEL), F32)],
        compiler_params=_params("arbitrary"),
    )(qkv, dq, dk, dv, gqk, _head_mean_matrix())


RESIDUES = 16


def _block_order(dil):
    runs = RESIDUES // dil
    slot = np.arange(ATT_BLOCK)
    return (slot % (ATT_BLOCK // runs)) * runs + slot // (ATT_BLOCK // runs)


def _bucket_tables():
    n = ATT_BLOCK
    max_exact = N_REL_BUCKETS // 2
    buckets, valids = [], []
    for _, dil in DILATED_PAIRS:
        order = _block_order(dil)
        a = order[:, None]
        c = np.concatenate([order, n + order])[None, :]
        first_half = (np.arange(2 * n) < n)[None, :]
        rel = a + n - c
        band = (rel >= 0) & (rel <= n)
        dist = np.clip(rel, 0, n) * dil
        dd = np.maximum(dist, 1).astype(np.float32)
        large = max_exact + (np.log(dd / np.float32(max_exact)) / np.float32(math.log(REL_MAX_DISTANCE / max_exact))
                             * np.float32(N_REL_BUCKETS - max_exact)).astype(np.int32)
        large = np.minimum(large, N_REL_BUCKETS - 1)
        buckets.append(np.where(dist < max_exact, dist, large).reshape(1, -1))
        valids.append(np.stack([(band & ~first_half).reshape(1, -1), band.reshape(1, -1)]))
    return np.stack(buckets).astype(np.int32), np.stack(valids).astype(np.int32)


BIAS_CHUNK = 8192


def _split3(x):
    a = x.astype(BF16)
    r = x - a.astype(F32)
    b = r.astype(BF16)
    c = (r - b.astype(F32)).astype(BF16)
    return a, b, c


def bias_expand(rel_bias_t, name):
    bucket, valid = _bucket_tables()
    nq = bucket.shape[-1]

    def body(t_ref, b_ref, v_ref, o_ref):
        onehot = (lax.broadcasted_iota(jnp.int32, (N_REL_BUCKETS, BIAS_CHUNK), 0) == b_ref[...]).astype(BF16)
        acc = None
        for term in _split3(t_ref[...]):
            p = jnp.dot(term, onehot, preferred_element_type=F32)
            acc = p if acc is None else acc + p
        for v in range(2):
            o_ref[v] = jnp.where(v_ref[v] > 0, acc, MASK_VALUE)

    return pl.pallas_call(
        body, name=name, grid=(3, nq // BIAS_CHUNK),
        in_specs=[pl.BlockSpec((N_HEADS, N_REL_BUCKETS), lambda b, c: (0, 0)),
                  pl.BlockSpec((None, 1, BIAS_CHUNK), lambda b, c: (b, 0, c)),
                  pl.BlockSpec((None, 2, 1, BIAS_CHUNK), lambda b, c: (b, 0, 0, c))],
        out_specs=pl.BlockSpec((None, 2, N_HEADS, BIAS_CHUNK), lambda b, c: (b, 0, 0, c)),
        out_shape=_sds((3, 2, N_HEADS, nq), F32), compiler_params=_params("parallel", "parallel"),
    )(rel_bias_t, jnp.asarray(bucket), jnp.asarray(valid))


def bias_reduce(dbias, name):
    bucket, _ = _bucket_tables()
    nq = bucket.shape[-1]
    dims = (((1,), (1,)), ((), ()))

    def body(d_ref, b_ref, o_ref):
        onehot = (lax.broadcasted_iota(jnp.int32, (N_REL_BUCKETS, BIAS_CHUNK), 0) == b_ref[...]).astype(BF16)
        acc = None
        for term in _split3(d_ref[...]):
            p = lax.dot_general(term, onehot, dims, preferred_element_type=F32)
            acc = p if acc is None else acc + p

        @pl.when(pl.program_id(1) == 0)
        def _():
            o_ref[...] = acc

        @pl.when(pl.program_id(1) > 0)
        def _():
            o_ref[...] += acc

    return pl.pallas_call(
        body, name=name, grid=(3, nq // BIAS_CHUNK),
        in_specs=[pl.BlockSpec((None, N_HEADS, BIAS_CHUNK), lambda b, c: (b, 0, c)),
                  pl.BlockSpec((None, 1, BIAS_CHUNK), lambda b, c: (b, 0, c))],
        out_specs=pl.BlockSpec((None, N_HEADS, N_REL_BUCKETS), lambda b, c: (b, 0, 0)),
        out_shape=_sds((3, N_HEADS, N_REL_BUCKETS), F32), compiler_params=_params("parallel", "arbitrary"),
    )(dbias, jnp.asarray(bucket))


PAIR = 2 * HEAD_DIM
N_PAIRS = N_HEADS // 2
_NT = (((1,), (1,)), ((), ()))
_TN = (((0,), (0,)), ((), ()))


def _low_lanes(shape):
    return lax.broadcasted_iota(jnp.int32, shape, 1) < HEAD_DIM


ATTN_VMEM_LIMIT_BYTES = 56 * 1024 * 1024
BRANCH_ORDER = (2, 1, 0)


def _regroup(dst, src, L16):
    for r in range(RESIDUES):
        dst[pl.ds(r * L16, L16), :] = src[pl.ds(r, L16, stride=RESIDUES), :]


def _ungroup(dst, src, L16):
    for r in range(RESIDUES):
        dst[pl.ds(r, L16, stride=RESIDUES), :] = src[pl.ds(r * L16, L16), :]


def _branch_geometry(branch, S):
    dil = DILATED_PAIRS[branch][1]
    runs = RESIDUES // dil
    return dil, runs, ATT_BLOCK // runs, S // dil // ATT_BLOCK


def _block_rows(it, branch, S):
    dil, runs, run_len, n_blocks = _branch_geometry(branch, S)
    L16 = S // RESIDUES
    r, b = it // n_blocks, it % n_blocks
    prev = jnp.maximum(b - 1, 0)
    cur_rows = [pl.multiple_of((j * dil + r) * L16 + run_len * b, 8) for j in range(runs)]
    prev_rows = [pl.multiple_of((j * dil + r) * L16 + run_len * prev, 8) for j in range(runs)]
    return cur_rows, prev_rows, jnp.minimum(b, 1)


def _load_block(ref, rows, run_len):
    parts = [ref[pl.ds(o, run_len), :] for o in rows]
    return parts[0] if len(parts) == 1 else jnp.concatenate(parts, axis=0)


def _store_block(ref, rows, run_len, value, add=False):
    for j, o in enumerate(rows):
        part = value[j * run_len:(j + 1) * run_len]
        if add:
            ref[pl.ds(o, run_len), :] += part
        else:
            ref[pl.ds(o, run_len), :] = part


ATTN_FWD_UNROLL = 32
ATTN_BWD_UNROLL = 16


def _stack_heads(x, low):
    zero = jnp.zeros_like(x)
    return jnp.concatenate([jnp.where(low, x, zero), jnp.where(low, zero, x)], axis=0)


def _unstack_heads(y, low):
    return jnp.where(low, y[:ATT_BLOCK], y[ATT_BLOCK:])


def attn_fwd(qkvn, bias, name):
    S = qkvn.shape[0]
    L16 = S // RESIDUES
    n_iter = S // ATT_BLOCK

    def body(q_ref, k_ref, v_ref, b_ref, o_ref, lse_ref, stage, qp, kp, vp, acc_s, m_s, l_s):
        for src, dst in ((q_ref, qp), (k_ref, kp), (v_ref, vp)):
            stage[...] = src[...].astype(F32)
            _regroup(dst, stage, L16)
        low = _low_lanes((ATT_BLOCK, PAIR))

        for branch in BRANCH_ORDER:
            _, _, run_len, _ = _branch_geometry(branch, S)
            first = branch == BRANCH_ORDER[0]

            def step(it, carry, branch=branch, run_len=run_len, first=first):
                cur, prev, variant = _block_rows(it, branch, S)
                q = _load_block(qp, cur, run_len).astype(BF16)
                k = jnp.concatenate([_load_block(kp, prev, run_len), _load_block(kp, cur, run_len)], axis=0).astype(BF16)
                v = jnp.concatenate([_load_block(vp, prev, run_len), _load_block(vp, cur, run_len)], axis=0).astype(BF16)
                s = lax.dot_general(_stack_heads(q, low), k, _NT, preferred_element_type=F32) * (HEAD_DIM ** -0.5)
                s = s + b_ref[2 * branch + variant].reshape(2 * ATT_BLOCK, 2 * ATT_BLOCK)
                mx = jnp.max(s, axis=-1, keepdims=True)
                p = jnp.exp(s - mx)
                den = jnp.sum(p, axis=-1, keepdims=True)
                pv = jnp.dot(p.astype(BF16), v, preferred_element_type=F32)
                acc = _unstack_heads(pv, low)
                m = _unstack_heads(mx, low)
                l = _unstack_heads(den, low)
                if not first:
                    m_old = _load_block(m_s, cur, run_len)
                    m_new = jnp.maximum(m_old, m)
                    a_old, a_new = jnp.exp(m_old - m_new), jnp.exp(m - m_new)
                    acc = _load_block(acc_s, cur, run_len) * a_old + acc * a_new
                    l = _load_block(l_s, cur, run_len) * a_old + l * a_new
                    m = m_new
                _store_block(acc_s, cur, run_len, acc)
                _store_block(m_s, cur, run_len, m)
                _store_block(l_s, cur, run_len, l)
                return carry

            lax.fori_loop(0, n_iter, step, 0, unroll=ATTN_FWD_UNROLL)

        acc_s[...] = acc_s[...] / l_s[...]
        _ungroup(stage, acc_s, L16)
        o_ref[...] = stage[...].astype(BF16)
        m_s[...] = m_s[...] + jnp.log(l_s[...])
        _ungroup(lse_ref, m_s, L16)

    col = lambda part: pl.BlockSpec((S, PAIR), lambda hp: (0, part * N_PAIRS + hp))
    out = pl.BlockSpec((S, PAIR), lambda hp: (0, hp))
    return pl.pallas_call(
        body, name=name, grid=(N_PAIRS,),
        in_specs=[col(0), col(1), col(2), pl.BlockSpec((6, 2, ATT_BLOCK, 2 * ATT_BLOCK), lambda hp: (0, hp, 0, 0))],
        out_specs=[out, out], out_shape=[_sds((S, D_MODEL), BF16), _sds((S, D_MODEL), F32)],
        scratch_shapes=[pltpu.VMEM((S, PAIR), F32)] * 7,
        compiler_params=pltpu.CompilerParams(dimension_semantics=("parallel",), vmem_limit_bytes=ATTN_VMEM_LIMIT_BYTES),
    )(qkvn, qkvn, qkvn, bias)


def attn_bwd(qkvn, att, datt, lse, bias, name):
    S = qkvn.shape[0]
    L16 = S // RESIDUES
    n_iter = S // ATT_BLOCK
    TILE = 512

    def body(q_ref, k_ref, v_ref, o_ref, do_ref, lse_ref, b_ref, dq_ref, dk_ref, dv_ref, db_ref,
             qp, kp, vp, dop, ldp, dqp, dkp, dvp):
        stage = dqp
        for src, dst in ((q_ref, qp), (k_ref, kp), (v_ref, vp), (do_ref, dop)):
            stage[...] = src[...].astype(F32)
            _regroup(dst, stage, L16)

        def pack(i, carry):
            rows = pl.ds(pl.multiple_of(i * TILE, TILE), TILE)
            low = _low_lanes((TILE, PAIR))
            lane = lax.broadcasted_iota(jnp.int32, (TILE, PAIR), 1)
            prod = do_ref[rows, :].astype(F32) * o_ref[rows, :].astype(F32)
            d0 = jnp.sum(jnp.where(low, prod, 0.0), axis=-1, keepdims=True)
            d1 = jnp.sum(jnp.where(low, 0.0, prod), axis=-1, keepdims=True)
            stage[rows, :] = jnp.where((lane & (HEAD_DIM // 2)) == 0, lse_ref[rows, :], jnp.where(low, d0, d1))
            return carry

        lax.fori_loop(0, S // TILE, pack, 0)
        _regroup(ldp, stage, L16)
        dqp[...] = jnp.zeros_like(dqp)
        dkp[...] = jnp.zeros_like(dkp)
        dvp[...] = jnp.zeros_like(dvp)
        db_ref[...] = jnp.zeros_like(db_ref)
        low = _low_lanes((ATT_BLOCK, PAIR))

        for branch in BRANCH_ORDER:
            _, _, run_len, _ = _branch_geometry(branch, S)

            def step(it, carry, branch=branch, run_len=run_len):
                cur, prev, variant = _block_rows(it, branch, S)
                q = _load_block(qp, cur, run_len).astype(BF16)
                dout = _load_block(dop, cur, run_len).astype(BF16)
                ld = _load_block(ldp, cur, run_len)
                k = jnp.concatenate([_load_block(kp, prev, run_len), _load_block(kp, cur, run_len)], axis=0).astype(BF16)
                v = jnp.concatenate([_load_block(vp, prev, run_len), _load_block(vp, cur, run_len)], axis=0).astype(BF16)
                half = HEAD_DIM // 2
                lse2 = jnp.concatenate([ld[:, 0:1], ld[:, HEAD_DIM:HEAD_DIM + 1]], axis=0)
                delta2 = jnp.concatenate([ld[:, half:half + 1], ld[:, HEAD_DIM + half:HEAD_DIM + half + 1]], axis=0)
                q2, do2 = _stack_heads(q, low), _stack_heads(dout, low)
                s = lax.dot_general(q2, k, _NT, preferred_element_type=F32) * (HEAD_DIM ** -0.5)
                p = jnp.exp(s + b_ref[2 * branch + variant].reshape(2 * ATT_BLOCK, 2 * ATT_BLOCK) - lse2)
                dp = lax.dot_general(do2, v, _NT, preferred_element_type=F32)
                ds = p * (dp - delta2)
                db_ref[branch] += ds.reshape(2, ATT_BLOCK, 2 * ATT_BLOCK)
                dsb = (ds * (HEAD_DIM ** -0.5)).astype(BF16)
                dq = _unstack_heads(jnp.dot(dsb, k, preferred_element_type=F32), low)
                dk = lax.dot_general(dsb, q2, _TN, preferred_element_type=F32)
                dv = lax.dot_general(p.astype(BF16), do2, _TN, preferred_element_type=F32)
                _store_block(dqp, cur, run_len, dq, add=True)
                _store_block(dkp, prev, run_len, dk[:ATT_BLOCK], add=True)
                _store_block(dvp, prev, run_len, dv[:ATT_BLOCK], add=True)
                _store_block(dkp, cur, run_len, dk[ATT_BLOCK:], add=True)
                _store_block(dvp, cur, run_len, dv[ATT_BLOCK:], add=True)
                return carry

            lax.fori_loop(0, n_iter, step, 0, unroll=ATTN_BWD_UNROLL)

        _ungroup(dq_ref, dqp, L16)
        _ungroup(dk_ref, dkp, L16)
        _ungroup(dv_ref, dvp, L16)

    col = lambda part: pl.BlockSpec((S, PAIR), lambda hp: (0, part * N_PAIRS + hp))
    one = pl.BlockSpec((S, PAIR), lambda hp: (0, hp))
    return pl.pallas_call(
        body, name=name, grid=(N_PAIRS,),
        in_specs=[col(0), col(1), col(2), one, one, one,
                  pl.BlockSpec((6, 2, ATT_BLOCK, 2 * ATT_BLOCK), lambda hp: (0, hp, 0, 0))],
        out_specs=[one, one, one, pl.BlockSpec((3, 2, ATT_BLOCK, 2 * ATT_BLOCK), lambda hp: (0, hp, 0, 0))],
        out_shape=[_sds((S, D_MODEL), F32)] * 3 + [_sds((3, N_HEADS, ATT_BLOCK, 2 * ATT_BLOCK), F32)],
        scratch_shapes=[pltpu.VMEM((S, PAIR), F32)] * 8,
        compiler_params=pltpu.CompilerParams(dimension_semantics=("parallel",), vmem_limit_bytes=ATTN_VMEM_LIMIT_BYTES),
    )(qkvn, qkvn, qkvn, att, datt, lse, bias)


def _adamw_step(w_ref, g_ref, m_ref, v_ref, d_ref, nm_ref, nv_ref):
    gv = g_ref[...]
    m2 = ADAM_B1 * m_ref[...] + (1.0 - ADAM_B1) * gv
    v2 = ADAM_B2 * v_ref[...] + (1.0 - ADAM_B2) * (gv * gv)
    m_hat = m2 / (1.0 - ADAM_B1 ** ADAM_STEP)
    v_hat = v2 / (1.0 - ADAM_B2 ** ADAM_STEP)
    d_ref[...] = -ADAM_LR * (m_hat / (jnp.sqrt(v_hat) + ADAM_EPS) + ADAM_WD * w_ref[...])
    nm_ref[...] = m2
    nv_ref[...] = v2


def adamw_small(ws, gs, ms, vs, name):
    n = len(ws)

    def body(*refs):
        groups = [refs[k * n:(k + 1) * n] for k in range(7)]
        for refs_of_one in zip(*groups):
            _adamw_step(*refs_of_one)

    outs = pl.pallas_call(body, name=name, out_shape=[_sds(a.shape, F32) for a in ws] * 3,
                          compiler_params=_params())(*ws, *gs, *ms, *vs)
    return outs[:n], outs[n:2 * n], outs[2 * n:]


def adamw(w, g, m, v, name):
    n, R, C = w.shape

    def body(w_ref, g_ref, m_ref, v_ref, d_ref, nm_ref, nv_ref, go_ref):
        go_ref[...] = g_ref[...]
        _adamw_step(w_ref, g_ref, m_ref, v_ref, d_ref, nm_ref, nv_ref)

    tr = R
    while tr * C * 4 > ELEMENTWISE_BLOCK_BYTES and tr % 16 == 0:
        tr //= 2
    spec = pl.BlockSpec((None, tr, C), lambda i, r: (i, r, 0))
    return pl.pallas_call(
        body, name=name, grid=(n, R // tr), in_specs=[spec] * 4, out_specs=[spec] * 4,
        out_shape=[_sds((n, R, C), F32)] * 4, compiler_params=_params("parallel", "parallel"),
    )(w, g, m, v)


ANY = pl.BlockSpec(memory_space=pl.ANY)


def _coords():
    return lax.axis_index("x"), lax.axis_index("y"), lax.axis_index("c")


def _other_chips(mx, my):
    return [(1 - mx, my), (mx, 1 - my), (1 - mx, 1 - my)]


def _remote(src, dst, send, recv, dev):
    return pltpu.make_async_remote_copy(src_ref=src, dst_ref=dst, send_sem=send, recv_sem=recv, device_id=dev,
                                        device_id_type=MESH)


HBM =pl.BlockSpec(memory_space=pltpu.HBM)
SEM = pl.BlockSpec(memory_space=pltpu.SEMAPHORE)
_SPLIT_COPY = pltpu.CompilerParams(has_side_effects=pltpu.SideEffectType.DATAFLOW_SIDE_EFFECTING)


def _in_hbm(a):
    return pltpu.with_memory_space_constraint(a, pltpu.HBM)


def cast_into_slot(w, layer, chip_core, name, dtype=BF16):
    _, _, hR, C = w.shape

    def body(s_ref, w_ref, o_ref):
        del s_ref
        o_ref[...] = w_ref[...].astype(dtype)

    grid_spec = pltpu.PrefetchScalarGridSpec(
        num_scalar_prefetch=1, grid=(2,),
        in_specs=[pl.BlockSpec((None, None, hR, C), lambda h, s: (layer, h, 0, 0))],
        out_specs=pl.BlockSpec((None, None, hR, C), lambda h, s: (s[0], h, 0, 0)))
    return pl.pallas_call(body, name=name, grid_spec=grid_spec, out_shape=_sds_hbm((N_CHIPS, 2, hR, C), dtype),
                          compiler_params=_params("parallel"))(chip_core, w)


def gather_start(lands, groups, name):
    n = len(lands)
    n_groups = len(groups)

    def body(*refs):
        ins = refs[:n]
        sems = refs[n:n + 2 * n_groups]
        token = refs[-1]
        mx, my, mc = _coords()
        chip = 2 * mx + my
        for g, members in enumerate(groups):
            send, recv = sems[2 * g], sems[2 * g + 1]
            for i, a in enumerate(members):
                mine = ins[a].at[chip, mc]
                for k, (px, py) in enumerate(_other_chips(mx, my)):
                    _remote(mine, mine, send.at[3 * i + k], recv.at[3 * i + k], (px, py, mc)).start()
        token[...] = jnp.zeros_like(token)

    sem_shapes = []
    for members in groups:
        sem_shapes += [pltpu.SemaphoreType.DMA((3 * len(members),))] * 2
    outs = pl.pallas_call(
        body, name=name, in_specs=[HBM] * n,
        out_specs=[SEM] * (2 * n_groups) + [HBM] * n + [pl.BlockSpec(memory_space=pltpu.VMEM)],
        out_shape=sem_shapes + [pltpu.HBM(a.shape, a.dtype) for a in lands] + [_sds((SUBLANES, LANES), F32)],
        input_output_aliases={a: 2 * n_groups + a for a in range(n)}, compiler_params=_SPLIT_COPY,
    )(*[_in_hbm(a) for a in lands])
    sems = [(outs[2 * g], outs[2 * g + 1]) for g in range(n_groups)]
    return sems, list(outs[2 * n_groups:2 * n_groups + n]), outs[-1]


def gather_forward(lands, sems, after, name):
    n = len(lands)

    def body(*refs):
        ins = refs[:n]
        send, recv = refs[n], refs[n + 1]
        fsend, frecv = refs[n + 3], refs[n + 4]
        mx, my, mc = _coords()
        for i in range(n):
            for k, (px, py) in enumerate(_other_chips(mx, my)):
                landed = ins[i].at[2 * px + py, mc]
                cp = _remote(landed, landed, send.at[3 * i + k], recv.at[3 * i + k], (px, py, mc))
                cp.wait_send()
                cp.wait_recv()
                _remote(landed, landed, fsend.at[3 * i + k], frecv.at[3 * i + k], (mx, my, 1 - mc)).start()

    outs = pl.pallas_call(
        body, name=name, in_specs=[HBM] * n + [SEM, SEM, ANY], out_specs=[SEM, SEM] + [HBM] * n,
        out_shape=[pltpu.SemaphoreType.DMA((3 * n,))] * 2 + [pltpu.HBM(a.shape, a.dtype) for a in lands],
        input_output_aliases={a: 2 + a for a in range(n)}, compiler_params=_SPLIT_COPY,
    )(*lands, sems[0], sems[1], after)
    return (outs[0], outs[1]), list(outs[2:])


def gather_wait(lands, sems, after, name):
    n = len(lands)

    def body(*refs):
        ins = refs[:n]
        fsend, frecv = refs[n], refs[n + 1]
        mx, my, mc = _coords()
        for i in range(n):
            for k, (px, py) in enumerate(_other_chips(mx, my)):
                theirs = ins[i].at[2 * px + py, 1 - mc]
                cp = _remote(theirs, theirs, fsend.at[3 * i + k], frecv.at[3 * i + k], (mx, my, 1 - mc))
                cp.wait_send()
                cp.wait_recv()

    outs = pl.pallas_call(
        body, name=name, in_specs=[HBM] * n + [SEM, SEM, ANY], out_specs=[HBM] * n,
        out_shape=[pltpu.HBM(a.shape, a.dtype) for a in lands],
        input_output_aliases={a: a for a in range(n)}, compiler_params=_SPLIT_COPY,
    )(*lands, sems[0], sems[1], after)
    return list(outs)


def _peers(mx, my, mc):
    return [(1 - mx if k & 4 else mx, 1 - my if k & 2 else my, 1 - mc if k & 1 else mc) for k in range(1, N_DEV)]


def devices_start(x, name):
    def body(x_ref, land_ref, send, recv, x_thru, land_thru):
        mx, my, mc = _coords()
        me = 4 * mx + 2 * my + mc
        for k, peer in enumerate(_peers(mx, my, mc)):
            _remote(x_ref, land_ref.at[me], send.at[k], recv.at[k], peer).start()

    land = lax.empty((N_DEV,) + x.shape, x.dtype)
    outs = pl.pallas_call(
        body, name=name, in_specs=[HBM, HBM], out_specs=[SEM, SEM, HBM, HBM],
        out_shape=[pltpu.SemaphoreType.DMA((N_DEV - 1,))] * 2 + [pltpu.HBM(x.shape, x.dtype), pltpu.HBM(land.shape, x.dtype)],
        input_output_aliases={0: 2, 1: 3}, compiler_params=_SPLIT_COPY,
    )(_in_hbm(x), _in_hbm(land))
    return (outs[0], outs[1]), outs[2], outs[3]


def devices_wait(x, land, sems, after, name):
    def body(x_ref, land_ref, send, recv, after_ref, x_thru, land_thru):
        mx, my, mc = _coords()
        for k, (px, py, pc) in enumerate(_peers(mx, my, mc)):
            cp = _remote(x_ref, land_ref.at[4 * px + 2 * py + pc], send.at[k], recv.at[k], (px, py, pc))
            cp.wait_send()
            cp.wait_recv()

    outs = pl.pallas_call(
        body, name=name, in_specs=[HBM, HBM, SEM, SEM, ANY], out_specs=[HBM, HBM],
        out_shape=[pltpu.HBM(x.shape, x.dtype), pltpu.HBM(land.shape, land.dtype)],
        input_output_aliases={0: 0, 1: 1}, compiler_params=_SPLIT_COPY,
    )(x, land, sems[0], sems[1], after)
    return outs[0], outs[1]


def device_sum(land, own, me, name):
    _, R, C = land.shape

    def body(s_ref, l_ref, o_ref_in, o_ref):
        acc = None
        for q in range(N_DEV):
            term = jnp.where(s_ref[0] == q, o_ref_in[...], l_ref[q])
            acc = term if acc is None else acc + term
        o_ref[...] = acc

    grid_spec = pltpu.PrefetchScalarGridSpec(
        num_scalar_prefetch=1, grid=(1,),
        in_specs=[pl.BlockSpec((N_DEV, R, C), lambda i, s: (0, 0, 0)), pl.BlockSpec((R, C), lambda i, s: (0, 0))],
        out_specs=pl.BlockSpec((R, C), lambda i, s: (0, 0)))
    return pl.pallas_call(body, name=name, grid_spec=grid_spec, out_shape=_sds((R, C), F32),
                          compiler_params=_params("arbitrary"))(me, land, own)


def reduce_send(grads, name):
    n = len(grads)

    def body(*refs):
        ins, lands = refs[:n], refs[n:2 * n]
        send, recv = refs[2 * n], refs[2 * n + 1]
        mx, my, mc = _coords()
        me = 4 * mx + 2 * my + mc
        for a in range(n):
            for k, (px, py, pc) in enumerate(_peers(mx, my, mc)):
                _remote(ins[a].at[2 * px + py, pc], lands[a].at[me], send.at[7 * a + k], recv.at[7 * a + k], (px, py, pc)).start()

    lands = [lax.empty((N_DEV,) + g.shape[2:], g.dtype) for g in grads]
    outs = pl.pallas_call(
        body, name=name, in_specs=[HBM] * (2 * n), out_specs=[SEM, SEM] + [HBM] * (2 * n),
        out_shape=[pltpu.SemaphoreType.DMA((7 * n,))] * 2 + [pltpu.HBM(a.shape, a.dtype) for a in grads + lands],
        input_output_aliases={a: 2 + a for a in range(2 * n)}, compiler_params=_SPLIT_COPY,
    )(*[_in_hbm(a) for a in grads + lands])
    return (outs[0], outs[1]), list(outs[2:2 + n]), list(outs[2 + n:])


def reduce_wait(grads, lands, sems, after, name):
    n = len(grads)

    def body(*refs):
        ins, zones = refs[:n], refs[n:2 * n]
        send, recv = refs[2 * n], refs[2 * n + 1]
        mx, my, mc = _coords()
        for a in range(n):
            for k, (px, py, pc) in enumerate(_peers(mx, my, mc)):
                cp = _remote(ins[a].at[2 * px + py, pc], zones[a].at[4 * px + 2 * py + pc], send.at[7 * a + k],
                             recv.at[7 * a + k], (px, py, pc))
                cp.wait_send()
                cp.wait_recv()

    outs = pl.pallas_call(
        body, name=name, in_specs=[HBM] * (2 * n) + [SEM, SEM, ANY], out_specs=[HBM] * (2 * n),
        out_shape=[pltpu.HBM(a.shape, a.dtype) for a in grads + lands],
        input_output_aliases={a: a for a in range(2 * n)}, compiler_params=_SPLIT_COPY,
    )(*grads, *lands, sems[0], sems[1], after)
    return list(outs[:n]), list(outs[n:])


def reduce_sum(land, grad, place, name, into=None, layer=None):
    _, hR, C = land.shape
    tr = hR
    while N_DEV * tr * C * 2 > 3 * ELEMENTWISE_BLOCK_BYTES and tr % 32 == 0:
        tr //= 2

    def body(s_ref, l_ref, g_ref, *rest):
        o_ref = rest[-1]
        own = g_ref[...].astype(F32)
        acc = None
        for q in range(N_DEV):
            term = jnp.where(s_ref[2] == q, own, l_ref[q].astype(F32))
            acc = term if acc is None else acc + term
        o_ref[...] = acc

    in_specs = [pl.BlockSpec((N_DEV, tr, C), lambda i, s: (0, i, 0)),
                pl.BlockSpec((None, None, tr, C), lambda i, s: (s[0], s[1], i, 0))]
    args = [place, _in_hbm(land), _in_hbm(grad)]
    aliases = {}
    if layer is None:
        out_spec = pl.BlockSpec((None, tr, C), lambda i, s: (s[1], i, 0))
        out_shape = _sds_hbm((2, hR, C), F32)
    else:
        out_spec = pl.BlockSpec((None, None, tr, C), lambda i, s: (layer, s[1], i, 0))
        out_shape = _sds_hbm((2, 2, hR, C), F32)
        if into is not None:
            in_specs.append(ANY)
            args.append(into)
            aliases = {3: 0}
    grid_spec = pltpu.PrefetchScalarGridSpec(num_scalar_prefetch=1, grid=(hR // tr,), in_specs=in_specs, out_specs=out_spec)
    return pl.pallas_call(body, name=name, grid_spec=grid_spec, out_shape=out_shape, input_output_aliases=aliases,
                          compiler_params=_params("arbitrary"))(*args)


def join_halves(arrays, name):
    n = len(arrays)
    pieces = [(a, l) for a, arr in enumerate(arrays) for l in (range(arr.shape[0]) if arr.ndim == 4 else [None])]

    def body(*refs):
        ins = refs[:n]
        send, recv = refs[2 * n:]
        mx, my, mc = _coords()

        def half(a, l, h):
            return ins[a].at[h] if l is None else ins[a].at[l, h]

        sends = [_remote(half(a, l, mc), half(a, l, mc), send.at[i], recv.at[i], (mx, my, 1 - mc))
                 for i, (a, l) in enumerate(pieces)]
        for cp in sends:
            cp.start()
        for i, (a, l) in enumerate(pieces):
            theirs = half(a, l, 1 - mc)
            _remote(theirs, theirs, send.at[i], recv.at[i], (mx, my, 1 - mc)).wait_recv()
        for cp in sends:
            cp.wait_send()

    return pl.pallas_call(
        body, name=name, in_specs=[ANY] * n, out_specs=[ANY] * n, out_shape=[_sds(a.shape, a.dtype) for a in arrays],
        input_output_aliases={a: a for a in range(n)},
        scratch_shapes=[pltpu.SemaphoreType.DMA((len(pieces),)), pltpu.SemaphoreType.DMA((len(pieces),))],
    )(*arrays)


LANES = 128
SUBLANES = 8


def _n_rows(shape):
    rows = -(-int(np.prod(shape)) // LANES)
    return -(-rows // SUBLANES) * SUBLANES


def _as_rows(a):
    flat = a.reshape(-1)
    rows = _n_rows(a.shape)
    return jnp.pad(flat, (0, rows * LANES - flat.shape[0])).reshape(rows, LANES)


def _pack(arrays):
    return jnp.concatenate([_as_rows(a) for a in arrays], axis=0)


def _unpack(rows, shapes):
    out, r0 = [], 0
    for s in shapes:
        n = _n_rows(s)
        out.append(rows[r0:r0 + n].reshape(-1)[:int(np.prod(s))].reshape(s))
        r0 += n
    return out


REPLICATED_SMALL = [("rel_bias", (32, 16)), ("even_norm", (1, 1024)), ("even_pool_w", (1, 4, 128, 128)),
                    ("even_pool_scale", (1, 512)), ("odd_q_norm", (1, 64)), ("odd_k_norm", (1, 64)),
                    ("ffn_norm", (2, 1024)), ("ffn_conv_b", (2, 5632))]
SHARDED_SMALL = [("even_conv_w", (1, 3, 128)), ("odd_norm", (1, 256)), ("ffn_conv_w", (2, 3, 1408))]
WEIGHT_ORDER = ["rel_bias", "even_norm", "even_w_in", "even_conv_w", "even_pool_w", "even_pool_scale", "even_w_out",
                "odd_norm", "odd_w_qkv", "odd_q_norm", "odd_k_norm", "odd_w_o", "ffn_norm", "ffn_w_up", "ffn_conv_w",
                "ffn_conv_b", "ffn_w_down"]


def kernel(x, rel_bias, even_norm, even_w_in, even_conv_w, even_pool_w, even_pool_scale, even_w_out, odd_norm, odd_w_qkv, odd_q_norm, odd_k_norm, odd_w_o, ffn_norm, ffn_w_up, ffn_conv_w, ffn_conv_b, ffn_w_down, loss_target, m_rel_bias, m_even_norm, m_even_w_in, m_even_conv_w, m_even_pool_w, m_even_pool_scale, m_even_w_out, m_odd_norm, m_odd_w_qkv, m_odd_q_norm, m_odd_k_norm, m_odd_w_o, m_ffn_norm, m_ffn_w_up, m_ffn_conv_w, m_ffn_conv_b, m_ffn_w_down, v_rel_bias, v_even_norm, v_even_w_in, v_even_conv_w, v_even_pool_w, v_even_pool_scale, v_even_w_out, v_odd_norm, v_odd_w_qkv, v_odd_q_norm, v_odd_k_norm, v_odd_w_o, v_ffn_norm, v_ffn_w_up, v_ffn_conv_w, v_ffn_conv_b, v_ffn_w_down):
    W = dict(rel_bias=rel_bias, even_norm=even_norm, even_w_in=even_w_in, even_conv_w=even_conv_w, even_pool_w=even_pool_w,
             even_pool_scale=even_pool_scale, even_w_out=even_w_out, odd_norm=odd_norm, odd_w_qkv=odd_w_qkv,
             odd_q_norm=odd_q_norm, odd_k_norm=odd_k_norm, odd_w_o=odd_w_o, ffn_norm=ffn_norm, ffn_w_up=ffn_w_up,
             ffn_conv_w=ffn_conv_w, ffn_conv_b=ffn_conv_b, ffn_w_down=ffn_w_down)
    M1 = dict(rel_bias=m_rel_bias, even_norm=m_even_norm, even_w_in=m_even_w_in, even_conv_w=m_even_conv_w,
              even_pool_w=m_even_pool_w, even_pool_scale=m_even_pool_scale, even_w_out=m_even_w_out, odd_norm=m_odd_norm,
              odd_w_qkv=m_odd_w_qkv, odd_q_norm=m_odd_q_norm, odd_k_norm=m_odd_k_norm, odd_w_o=m_odd_w_o,
              ffn_norm=m_ffn_norm, ffn_w_up=m_ffn_w_up, ffn_conv_w=m_ffn_conv_w, ffn_conv_b=m_ffn_conv_b,
              ffn_w_down=m_ffn_w_down)
    M2 = dict(rel_bias=v_rel_bias, even_norm=v_even_norm, even_w_in=v_even_w_in, even_conv_w=v_even_conv_w,
              even_pool_w=v_even_pool_w, even_pool_scale=v_even_pool_scale, even_w_out=v_even_w_out, odd_norm=v_odd_norm,
              odd_w_qkv=v_odd_w_qkv, odd_q_norm=v_odd_q_norm, odd_k_norm=v_odd_k_norm, odd_w_o=v_odd_w_o,
              ffn_norm=v_ffn_norm, ffn_w_up=v_ffn_w_up, ffn_conv_w=v_ffn_conv_w, ffn_conv_b=v_ffn_conv_b,
              ffn_w_down=v_ffn_w_down)
    mx, my, mc = _coords()
    chip = 2 * mx + my
    me = 4 * mx + 2 * my + mc
    place = jnp.stack([chip, mc, me]).astype(jnp.int32)
    xs, target = x[0], loss_target[0]

    def halves(w):
        return w.reshape((w.shape[0], 2, w.shape[-2] // 2, w.shape[-1]))

    small_rows = jnp.pad(_pack([even_conv_w, odd_norm, ffn_conv_w]), ((0, SUBLANES), (0, 0)))
    first = [cast_into_slot(halves(even_w_in), 0, place, "cast_w_in"), cast_into_slot(halves(even_w_out), 0, place, "cast_w_out"),
             cast_into_slot(small_rows.reshape(1, 2, small_rows.shape[0] // 2, LANES), 0, place, "small_into_slot", dtype=F32),
             cast_into_slot(halves(ffn_w_up), 0, place, "cast_w_up0")]
    first_sems, first, token = gather_start(first, [[0, 1, 2], [3]], "gather_start_first")
    even_norm_after_start = even_norm + token[0:1, 0:1]

    def later(a):
        return lax.optimization_barrier((a, token))[0]

    down_f32 = halves(later(ffn_w_down))
    rest = [cast_into_slot(down_f32, 0, place, "cast_w_down0"),
            cast_into_slot(halves(later(odd_w_qkv)), 0, place, "cast_w_qkv"), cast_into_slot(halves(later(odd_w_o)), 0, place, "cast_w_o"),
            cast_into_slot(halves(later(ffn_w_up)), 1, place, "cast_w_up1"), cast_into_slot(down_f32, 1, place, "cast_w_down1")]
    rest_sems, rest, rest_token = gather_start(rest, [[0], [1, 2], [3], [4]], "gather_start_rest")
    group_arrays = [first[:3], [first[3]], [rest[0]], [rest[1], rest[2]], [rest[3]], [rest[4]]]
    group_sems = first_sems + rest_sems

    passing = {}

    def pass_on(group, tag, after, then):
        sems, arrays = gather_forward(group_arrays[group], group_sems[group], after, "gather_forward_" + tag)
        then, arrays = lax.optimization_barrier((then, arrays))
        passing[group] = (sems, arrays)
        return then

    def gathered(group, tag, after):
        sems, arrays = passing.pop(group)
        return gather_wait(arrays, sems, after, "gather_wait_" + tag)

    pool_w = cast_bf16(even_pool_w[0], "cast_pool_w")
    gqk = jnp.stack([jnp.tile(odd_q_norm[0], N_HEADS), jnp.tile(odd_k_norm[0], N_HEADS),
                     jnp.ones((D_MODEL,), F32)])[:, None, :]
    bias = bias_expand(later(rel_bias).T, "bias_expand").reshape(6, N_HEADS, ATT_BLOCK, 2 * ATT_BLOCK)
    xn0 = rmsnorm_fwd(xs, pass_on(0, "even", rest_token, even_norm_after_start), "even_norm")
    got = gathered(0, "even", xn0)
    w_in = got[0].reshape(N_CHIPS, 1, D_MODEL, EVEN_IN // N_CHIPS)
    w_out = got[1].reshape(1, 1, D_MODEL, D_MODEL)
    small = got[2].reshape(N_CHIPS, small_rows.shape[0], LANES)
    conv_w_full = small[:, 0:3].transpose(1, 0, 2).reshape(3, A_WIDTH)
    odd_norm_full = small[:, 8:10].reshape(1, D_MODEL)
    ffn_cw_full = small[:, 16:82].reshape(N_CHIPS, 2, 3, 2 * D_FF // N_CHIPS).transpose(1, 2, 0, 3).reshape(2, 3, 2 * D_FF)

    def ffn_fwd(l, xin, xn):
        up, u, act = up_glu_fwd(xn, w_up[l], ffn_cw_full[l], ffn_conv_b[l:l + 1], f"ffn{l}_up_glu")
        return act, (xin, xn, up, u, act)

    w_up, w_down = [None, None], [None, None]
    proj, mix = in_mixer_fwd(xn0, w_in, conv_w_full, pool_w, even_pool_scale, "even_in_mixer")
    x1, xn1 = mm_res_norm(mix, w_out, xs, ffn_norm[0:1], "even_out")
    pass_on(1, "up0", x1, x1)
    w_up[0] = gathered(1, "up0", x1)[0].reshape(N_CHIPS, 1, D_MODEL, 2 * D_FF // N_CHIPS)
    act0, ffn0 = ffn_fwd(0, x1, pass_on(2, "down0", x1, xn1))
    w_down[0] = gathered(2, "down0", act0)[0].reshape(1, 1, D_FF, D_MODEL)
    x2, xn2 = mm_res_norm(pass_on(3, "odd", act0, act0), w_down[0], x1, odd_norm_full, "ffn0_down")
    got = gathered(3, "odd", x2)
    xn2 = pass_on(5, "down1", x2, pass_on(4, "up1", x2, xn2))
    w_qkv = got[0].reshape(N_CHIPS, 1, D_MODEL, 3 * D_MODEL // N_CHIPS)
    w_o = got[1].reshape(1, 1, D_MODEL, D_MODEL)
    qkv, qkvn = qkv_qknorm_fwd(xn2, w_qkv, gqk, "odd_qkv_qknorm")
    att, lse = attn_fwd(qkvn, bias, "attn_fwd")
    x3, xn3 = mm_res_norm(att, w_o, x2, ffn_norm[1:2], "odd_out")
    w_up[1] = gathered(4, "up1", x3)[0].reshape(N_CHIPS, 1, D_MODEL, 2 * D_FF // N_CHIPS)
    act1, ffn1 = ffn_fwd(1, x3, xn3)
    w_down[1] = gathered(5, "down1", x3)[0].reshape(1, 1, D_FF, D_MODEL)
    dy, dyb, sq = mm_res_loss(act1, w_down[1], x3, target, "ffn1_down_loss")
    loss_part = (0.5 * jnp.sum(sq) * (1.0 / D_MODEL)).reshape(1, 1)

    def ffn_bwd(l, dy, dyb, saved):
        xin, xn, up, u, act = saved
        dw_down = mm_tn(act, dyb, f"ffn{l}_dw_down", J=1, tk=D_FF // 2, tm=1024)
        dact = mm_nt(dyb, w_down[l], f"ffn{l}_dact", tr=D_FF // 2, out_dtype=BF16, tm=1024)
        dup, dcw, dcb = glu_bwd(up, u, dact, ffn_cw_full[l], f"ffn{l}_glu_bwd")
        dw_up = mm_tn(xn, dup, f"ffn{l}_dw_up", J=N_CHIPS, tk=512, tm=1024, jb=2)
        dx, dxb, dg = mm_nt_norm_bwd(dup, w_up[l], xin, ffn_norm[l:l + 1], dy, f"ffn{l}_dx")
        return dx, dxb, (dw_down, dw_up, dcw, dcb, dg)

    def quarters(g):
        return g.reshape(N_CHIPS, 2, g.shape[0] * g.shape[1] // (2 * N_CHIPS), g.shape[-1])

    def reduce_start(grads, tag, then):
        sems, parts, zones = reduce_send([quarters(g) for g in grads], "reduce_send_" + tag)
        then, parts = lax.optimization_barrier((then, parts))
        return (sems, parts, zones), then

    dx3, dx3b, g_ffn1 = ffn_bwd(1, dy, dyb, ffn1)
    red_ffn1, (dx3, dx3b) = reduce_start([g_ffn1[1], g_ffn1[0]], "ffn1", (dx3, dx3b))
    dw_o = mm_tn(att, dx3b, "odd_dw_o", J=1, tk=512, tm=1024)
    datt = mm_nt(dx3b, w_o, "odd_datt", tr=D_MODEL, out_dtype=BF16)
    dq, dk, dv, dbias = attn_bwd(qkvn, att, datt, lse, bias, "attn_bwd")
    dqkv, dgqk = qknorm_bwd(qkv, dq, dk, dv, gqk, "odd_qknorm_bwd")
    dw_qkv = mm_tn(xn2, dqkv, "odd_dw_qkv", J=N_CHIPS, tk=512, tm=1024)
    red_odd, dqkv = reduce_start([dw_qkv, dw_o], "odd", dqkv)
    dx2, dx2b, dg_odd = mm_nt_norm_bwd(dqkv, w_qkv, x2, odd_norm_full, dx3, "odd_dx")
    dx1, dx1b, g_ffn0 = ffn_bwd(0, dx2, dx2b, ffn0)
    red_ffn0, (dx1, dx1b) = reduce_start([g_ffn0[1], g_ffn0[0]], "ffn0", (dx1, dx1b))
    dw_out = mm_tn(mix, dx1b, "even_dw_out", J=1, tk=512, tm=1024)
    dmix = mm_nt(dx1b, w_out, "even_dmix", tr=D_MODEL)
    dproj, dcw_even, dpw, dps = mixer_bwd(proj, dmix, conv_w_full, pool_w, even_pool_scale, "even_mixer_bwd")
    dw_in = mm_tn(xn0, dproj, "even_dw_in", J=N_CHIPS, tk=512, tm=1024)
    grad_x, _, dg_even = mm_nt_norm_bwd(dproj, w_in, xs, even_norm, dx1, "even_dx")
    d_rel = jnp.sum(bias_reduce(dbias.reshape(3, N_HEADS, 2 * ATT_BLOCK * ATT_BLOCK), "bias_reduce"), axis=0).T

    red_even, grad_x = reduce_start([dw_in, dw_out], "even", grad_x)

    dcw_sh = dcw_even.reshape(3, N_CHIPS, A_WIDTH // N_CHIPS).transpose(1, 0, 2)
    don_sh = dg_odd.reshape(N_CHIPS, D_MODEL // N_CHIPS)
    dfcw = jnp.stack([g_ffn0[2], g_ffn1[2]])
    dfcw_sh = dfcw.reshape(2, 3, N_CHIPS, 2 * D_FF // N_CHIPS).transpose(2, 0, 1, 3)
    rep_grads = [d_rel, dg_even, dpw[None], dps, _head_sum(dgqk[0]), _head_sum(dgqk[1]),
                 jnp.concatenate([g_ffn0[4], g_ffn1[4]], axis=0), jnp.concatenate([g_ffn0[3], g_ffn1[3]], axis=0)]
    rep_rows = _pack([loss_part] + rep_grads)
    n_loss = _n_rows(loss_part.shape)
    shard_rows = jnp.concatenate([_pack([dcw_sh[j], don_sh[j], dfcw_sh[j]]) for j in range(N_CHIPS)], axis=0)
    n_rep, n_shard = rep_rows.shape[0], shard_rows.shape[0] // N_CHIPS
    small_sems, small_rows, small_land = devices_start(jnp.concatenate([rep_rows, shard_rows], axis=0), "small_grads_start")
    grad_x, small_rows = lax.optimization_barrier((grad_x, small_rows))

    def reduce_end(red, tag, after):
        sems, parts, zones = red
        parts, zones = reduce_wait(parts, zones, sems, after, "reduce_wait_" + tag)
        return zones, parts

    z_ffn1, p_ffn1 = reduce_end(red_ffn1, "ffn1", grad_x)
    z_odd, p_odd = reduce_end(red_odd, "odd", grad_x)
    r_qkv = reduce_sum(z_odd[0], p_odd[0], place, "reduce_sum_w_qkv")
    r_o = reduce_sum(z_odd[1], p_odd[1], place, "reduce_sum_w_o")
    r_up = reduce_sum(z_ffn1[0], p_ffn1[0], place, "reduce_sum_w_up1", layer=1)
    r_down = reduce_sum(z_ffn1[1], p_ffn1[1], place, "reduce_sum_w_down1", layer=1)
    r_qkv, r_o, r_up, r_down = lax.optimization_barrier((r_qkv, r_o, r_up, r_down))
    z_ffn0, p_ffn0 = reduce_end(red_ffn0, "ffn0", r_down)
    r_up = reduce_sum(z_ffn0[0], p_ffn0[0], place, "reduce_sum_w_up0", into=r_up, layer=0)
    r_down = reduce_sum(z_ffn0[1], p_ffn0[1], place, "reduce_sum_w_down0", into=r_down, layer=0)
    later = ["odd_w_qkv", "odd_w_o", "ffn_w_up", "ffn_w_down"]
    joined = join_halves([r_qkv, r_o, r_up, r_down], "grads_join_late_layers")
    G = {nm: g.reshape(W[nm].shape) for nm, g in zip(later, joined)}

    D_, NM, NV = {}, {}, {}

    def update(nm):
        as3 = lambda a: a.reshape((-1,) + a.shape[-2:])
        outs = adamw(as3(W[nm]), as3(G[nm]), as3(M1[nm]), as3(M2[nm]), "adamw_" + nm)
        D_[nm], NM[nm], NV[nm], G[nm] = [o.reshape(W[nm].shape) for o in outs]

    def all_before(names):
        tied = lax.optimization_barrier([D_[nm] for nm in names])
        for nm, d in zip(names, tied):
            D_[nm] = d
        return tied[0]

    for nm in later:
        update(nm)
    z_even, p_even = reduce_end(red_even, "even", all_before(later))
    joined = join_halves([reduce_sum(z_even[0], p_even[0], place, "reduce_sum_w_in"),
                          reduce_sum(z_even[1], p_even[1], place, "reduce_sum_w_out")], "grads_join_first_layer")
    first = ["even_w_in", "even_w_out"]
    for nm, g in zip(first, joined):
        G[nm] = g.reshape(W[nm].shape)
        update(nm)
    small_rows, small_land = devices_wait(small_rows, small_land, small_sems, all_before(first), "small_grads_wait")
    small_sum = device_sum(small_land, small_rows, place[2:3], "small_grads_sum")
    mine = lax.dynamic_slice_in_dim(small_sum, n_rep + chip * n_shard, n_shard, axis=0)
    loss = small_sum[0, 0]
    g_small = jnp.concatenate([small_sum[n_loss:n_rep], mine], axis=0)
    small_names = [n for n, _ in REPLICATED_SMALL + SHARDED_SMALL]
    small_shapes = [s for _, s in REPLICATED_SMALL + SHARDED_SMALL]
    G.update(dict(zip(small_names, _unpack(g_small, small_shapes))))
    outs = adamw_small(*[[d[n] for n in small_names] for d in (W, G, M1, M2)], "adamw_small")
    for dst, o in zip((D_, NM, NV), outs):
        dst.update(dict(zip(small_names, o)))

    return (loss, grad_x[None], *[G[n] for n in WEIGHT_ORDER], *[D_[n] for n in WEIGHT_ORDER],
            *[NM[n] for n in WEIGHT_ORDER], *[NV[n] for n in WEIGHT_ORDER])


def _head_sum(dg):
    return jnp.sum(dg.reshape(N_HEADS, HEAD_DIM), axis=0, keepdims=True)
```

```python
import functools
import math

import numpy as np
import jax
import jax.numpy as jnp
from jax import lax
from jax.experimental import pallas as pl
from jax.experimental.pallas import tpu as pltpu

F32 = jnp.float32
BF16 = jnp.bfloat16

D_MODEL = 1024
N_HEADS = 16
HEAD_DIM = 64
A_WIDTH = 512
POOL_WINDOWS = (2, 4, 8, 16)
POOL_GROUP = 128
EVEN_IN = 2048
D_FF = 2816
DILATED_PAIRS = ((128, 1), (512, 4), (2048, 16))
ATT_BLOCK = 128
N_REL_BUCKETS = 32
REL_MAX_DISTANCE = 2048
EPS = 1e-6
MASK_VALUE = -1e30
ADAM_LR, ADAM_B1, ADAM_B2, ADAM_EPS, ADAM_WD, ADAM_STEP = 0.001, 0.9, 0.999, 1e-08, 0.01, 10

VMEM_LIMIT_BYTES = 48 * 1024 * 1024
ELEMENTWISE_BLOCK_BYTES = 2 * 1024 * 1024
N_CHIPS = 4
N_DEV = 8
MESH = pl.DeviceIdType.MESH


def _params(*sem):
    return pltpu.CompilerParams(dimension_semantics=sem if sem else None, vmem_limit_bytes=VMEM_LIMIT_BYTES)


def _sds(shape, dtype):
    return jax.ShapeDtypeStruct(tuple(shape), dtype)


def _sds_hbm(shape, dtype):
    return pltpu.HBM(tuple(shape), dtype)


def cast_bf16(x, name, tr=None):
    lead, (R, C) = x.shape[:-2], x.shape[-2:]
    n = int(np.prod(lead)) if lead else 1
    x3 = x.reshape((n, R, C))
    tr = tr or R

    def body(x_ref, o_ref):
        o_ref[...] = x_ref[...].astype(BF16)

    out = pl.pallas_call(
        body, name=name, grid=(n, R // tr),
        in_specs=[pl.BlockSpec((None, tr, C), lambda i, r: (i, r, 0))],
        out_specs=pl.BlockSpec((None, tr, C), lambda i, r: (i, r, 0)),
        out_shape=_sds((n, R, C), BF16), compiler_params=_params("parallel", "parallel"),
    )(x3)
    return out.reshape(lead + (R, C))


def rmsnorm_fwd(x, g, name, ts=512):
    S, Dm = x.shape

    def body(x_ref, g_ref, o_ref):
        xv = x_ref[...]
        r = lax.rsqrt(jnp.mean(xv * xv, axis=-1, keepdims=True) + EPS)
        o_ref[...] = ((xv * r) * g_ref[...]).astype(BF16)

    return pl.pallas_call(
        body, name=name, grid=(S // ts,),
        in_specs=[pl.BlockSpec((ts, Dm), lambda i: (i, 0)), pl.BlockSpec((1, Dm), lambda i: (0, 0))],
        out_specs=pl.BlockSpec((ts, Dm), lambda i: (i, 0)),
        out_shape=_sds((S, Dm), BF16), compiler_params=_params("parallel"),
    )(x, g)


def mm_res_norm(a, w, res, gain, name, tm=1024):
    M, K = a.shape
    Dm = w.shape[-1]

    def body(a_ref, w_ref, r_ref, g_ref, y_ref, yn_ref):
        y = r_ref[...] + jnp.dot(a_ref[...], w_ref[...], preferred_element_type=F32)
        y_ref[...] = y
        r = lax.rsqrt(jnp.mean(y * y, axis=-1, keepdims=True) + EPS)
        yn_ref[...] = ((y * r) * g_ref[...]).astype(BF16)

    row = pl.BlockSpec((tm, Dm), lambda m: (m, 0))
    return pl.pallas_call(
        body, name=name, grid=(M // tm,),
        in_specs=[pl.BlockSpec((tm, K), lambda m: (m, 0)),
                  pl.BlockSpec((None, None, K, Dm), lambda m: (0, 0, 0, 0), pipeline_mode=pl.Buffered(1)),
                  row, pl.BlockSpec((1, Dm), lambda m: (0, 0))],
        out_specs=[row, row], out_shape=[_sds((M, Dm), F32), _sds((M, Dm), BF16)],
        compiler_params=_params("parallel"),
    )(a, w, res, gain)


def mm_res_loss(a, w, res, target, name, tm=512):
    M, K = a.shape
    Dm = w.shape[-1]

    def body(a_ref, w_ref, r_ref, t_ref, d_ref, db_ref, s_ref):
        e = (r_ref[...] + jnp.dot(a_ref[...], w_ref[...], preferred_element_type=F32)) - t_ref[...]
        d = e * (1.0 / Dm)
        d_ref[...] = d
        db_ref[...] = d.astype(BF16)
        part = jnp.sum(e * e, axis=0, keepdims=True)

        @pl.when(pl.program_id(0) == 0)
        def _():
            s_ref[...] = part

        @pl.when(pl.program_id(0) > 0)
        def _():
            s_ref[...] += part

    row = pl.BlockSpec((tm, Dm), lambda m: (m, 0))
    return pl.pallas_call(
        body, name=name, grid=(M // tm,),
        in_specs=[pl.BlockSpec((tm, K), lambda m: (m, 0)),
                  pl.BlockSpec((None, None, K, Dm), lambda m: (0, 0, 0, 0), pipeline_mode=pl.Buffered(1)), row, row],
        out_specs=[row, row, pl.BlockSpec((1, Dm), lambda m: (0, 0))],
        out_shape=[_sds((M, Dm), F32), _sds((M, Dm), BF16), _sds((1, Dm), F32)],
        compiler_params=_params("arbitrary"),
    )(a, w, res, target)


def mm_nt(dy, w, name, tr, layer=0, out_dtype=F32, tm=512):
    M = dy.shape[0]
    J, _, R, Ns = w.shape
    dims = (((1,), (1,)), ((), ()))

    def body(dy_ref, w_ref, o_ref):
        acc = None
        for j in range(J):
            p = lax.dot_general(dy_ref[:, j * Ns:(j + 1) * Ns], w_ref[j], dims, preferred_element_type=F32)
            acc = p if acc is None else acc + p
        o_ref[...] = acc.astype(o_ref.dtype)

    return pl.pallas_call(
        body, name=name, grid=(R // tr, M // tm),
        in_specs=[pl.BlockSpec((tm, J * Ns), lambda r, m: (m, 0)),
                  pl.BlockSpec((J, None, tr, Ns), lambda r, m: (0, layer, r, 0))],
        out_specs=pl.BlockSpec((tm, tr), lambda r, m: (m, r)),
        out_shape=_sds((M, R), out_dtype),
        compiler_params=_params("parallel", "parallel"),
    )(dy, w)


def mm_nt_norm_bwd(dy, w, x, g, dres, name, layer=0, tm=512):
    M = dy.shape[0]
    J, _, Dm, Ns = w.shape
    dims = (((1,), (1,)), ((), ()))

    def body(dy_ref, w_ref, x_ref, g_ref, r_ref, dx_ref, dxb_ref, dg_ref):
        dxn = None
        for j in range(J):
            p = lax.dot_general(dy_ref[:, j * Ns:(j + 1) * Ns], w_ref[j], dims, preferred_element_type=F32)
            dxn = p if dxn is None else dxn + p
        xv = x_ref[...]
        r = lax.rsqrt(jnp.mean(xv * xv, axis=-1, keepdims=True) + EPS)
        gx = dxn * g_ref[...]
        dot = jnp.sum(gx * xv, axis=-1, keepdims=True)
        dx = r_ref[...] + r * gx - xv * ((r * r * r) * (dot * (1.0 / Dm)))
        dx_ref[...] = dx
        dxb_ref[...] = dx.astype(BF16)
        part = jnp.sum(dxn * (xv * r), axis=0, keepdims=True)

        @pl.when(pl.program_id(0) == 0)
        def _():
            dg_ref[...] = part

        @pl.when(pl.program_id(0) > 0)
        def _():
            dg_ref[...] += part

    row = pl.BlockSpec((tm, Dm), lambda m: (m, 0))
    vec = pl.BlockSpec((1, Dm), lambda m: (0, 0))
    return pl.pallas_call(
        body, name=name, grid=(M // tm,),
        in_specs=[pl.BlockSpec((tm, J * Ns), lambda m: (m, 0)),
                  pl.BlockSpec((J, None, Dm, Ns), lambda m: (0, layer, 0, 0), pipeline_mode=pl.Buffered(1)), row, vec, row],
        out_specs=[row, row, vec],
        out_shape=[_sds((M, Dm), F32), _sds((M, Dm), BF16), _sds((1, Dm), F32)],
        compiler_params=_params("arbitrary"),
    )(dy, w, x, g, dres)


def mm_tn(a, dy, name, J, tk, tm=512, jb=None):
    M, K = a.shape
    jb = jb or J
    Ns = dy.shape[1] // J
    N = jb * Ns
    n_m = M // tm
    dims = (((0,), (0,)), ((), ()))

    def body(a_ref, dy_ref, o_ref, acc_ref):
        p = lax.dot_general(a_ref[...], dy_ref[...], dims, preferred_element_type=F32)
        m = pl.program_id(2)

        @pl.when(m == 0)
        def _():
            acc_ref[...] = p

        @pl.when(m > 0)
        def _():
            acc_ref[...] += p

        @pl.when(m == n_m - 1)
        def _():
            for j in range(jb):
                o_ref[j] = acc_ref[:, j * Ns:(j + 1) * Ns].astype(BF16)

    return pl.pallas_call(
        body, name=name, grid=(J // jb, K // tk, n_m),
        in_specs=[pl.BlockSpec((tm, tk), lambda g, k, m: (m, k)), pl.BlockSpec((tm, N), lambda g, k, m: (m, g))],
        out_specs=pl.BlockSpec((jb, tk, Ns), lambda g, k, m: (g, k, 0)),
        out_shape=_sds((J, K, Ns), BF16), scratch_shapes=[pltpu.VMEM((tk, N), F32)],
        compiler_params=_params("parallel", "parallel", "arbitrary"),
    )(a, dy)


HALO = 16


def _shift_down(x, s):
    return pltpu.roll(x, s, 0)


def _shift_up(x, s):
    return pltpu.roll(x, x.shape[0] - s, 0)


def _conv3(z, cw):
    return (_shift_down(z, 2) * cw[0:1] + _shift_down(z, 1) * cw[1:2]) + z * cw[2:3]


def _window_count(first_row, n, k):
    t = first_row + lax.broadcasted_iota(jnp.int32, (n, 1), 0)
    return jnp.clip(t + 1, 1, k).astype(F32)


def in_mixer_fwd(xn, w_in, conv_w, pool_w, pool_scale, name, ts=512):
    S, K = xn.shape
    n = ts + HALO

    def body(xm_ref, xb_ref, w_ref, cw_ref, pw_ref, ps_ref, p_ref, o_ref):
        i = pl.program_id(0)
        before = jnp.where(i > 0, xb_ref[...], jnp.zeros_like(xb_ref))
        rows = jnp.concatenate([before, xm_ref[...]], axis=0)
        h, gb, gc, pin = [jnp.dot(rows, w_ref[j], preferred_element_type=F32) for j in range(N_CHIPS)]
        for j, part in enumerate((h, gb, gc, pin)):
            p_ref[:, j * A_WIDTH:(j + 1) * A_WIDTH] = part[HALO:]
        cz = _conv3(gc * h, cw_ref[...])
        o_ref[:, 0:A_WIDTH] = (gb[HALO:] * cz[HALO:]).astype(BF16)
        for g, k in enumerate(POOL_WINDOWS):
            p = pin[:, g * POOL_GROUP:(g + 1) * POOL_GROUP]
            w = p
            s = 1
            while s < k:
                w = w + _shift_down(w, s)
                s *= 2
            pooled = w / _window_count(i * ts - HALO, n, k) - p
            yb = jnp.dot(pooled[HALO:].astype(BF16), pw_ref[g], preferred_element_type=F32)
            yb = yb * ps_ref[:, g * POOL_GROUP:(g + 1) * POOL_GROUP]
            o_ref[:, A_WIDTH + g * POOL_GROUP:A_WIDTH + (g + 1) * POOL_GROUP] = yb.astype(BF16)

    hb = ts // HALO
    return pl.pallas_call(
        body, name=name, grid=(S // ts,),
        in_specs=[
            pl.BlockSpec((ts, K), lambda i: (i, 0)),
            pl.BlockSpec((HALO, K), lambda i: (jnp.maximum(i * hb - 1, 0), 0)),
            pl.BlockSpec((N_CHIPS, None, K, A_WIDTH), lambda i: (0, 0, 0, 0), pipeline_mode=pl.Buffered(1)),
            pl.BlockSpec((3, A_WIDTH), lambda i: (0, 0)),
            pl.BlockSpec((4, POOL_GROUP, POOL_GROUP), lambda i: (0, 0, 0)),
            pl.BlockSpec((1, 4 * POOL_GROUP), lambda i: (0, 0)),
        ],
        out_specs=[pl.BlockSpec((ts, EVEN_IN), lambda i: (i, 0)), pl.BlockSpec((ts, D_MODEL), lambda i: (i, 0))],
        out_shape=[_sds((S, EVEN_IN), F32), _sds((S, D_MODEL), BF16)], compiler_params=_params("parallel"),
    )(xn, xn, w_in, conv_w, pool_w, pool_scale)


def mixer_bwd(proj, dmix, conv_w, pool_w, pool_scale, name, ts=256):
    S = proj.shape[0]
    n = ts + 2 * HALO
    nt = S // ts
    tn_dims = (((0,), (0,)), ((), ()))
    nt_dims = (((1,), (1,)), ((), ()))

    def body(pm_ref, pb_ref, pa_ref, dm_ref, da_ref, cw_ref, pw_ref, ps_ref, o_ref, dcw_ref, dpw_ref, dps_ref):
        i = pl.program_id(0)
        last = i == nt - 1
        before = jnp.where(i > 0, pb_ref[...], 0.0)
        after = jnp.where(last, 0.0, pa_ref[...])
        ext = jnp.concatenate([before, pm_ref[...], after], axis=0)
        dafter = jnp.where(last, 0.0, da_ref[...])
        dext = jnp.concatenate([jnp.zeros((HALO, D_MODEL), F32), dm_ref[...], dafter], axis=0)
        cw = cw_ref[...]
        main = slice(HALO, HALO + ts)

        @pl.when(i == 0)
        def _():
            dcw_ref[...] = jnp.zeros_like(dcw_ref)
            dpw_ref[...] = jnp.zeros_like(dpw_ref)
            dps_ref[...] = jnp.zeros_like(dps_ref)

        h, gb, gc = ext[:, 0:A_WIDTH], ext[:, A_WIDTH:2 * A_WIDTH], ext[:, 2 * A_WIDTH:3 * A_WIDTH]
        z = gc * h
        z1, z2 = _shift_down(z, 1), _shift_down(z, 2)
        cz = (z2 * cw[0:1] + z1 * cw[1:2]) + z * cw[2:3]
        dya = dext[:, 0:A_WIDTH]
        dcz = dya * gb
        dz = dcz * cw[2:3] + _shift_up(dcz, 1) * cw[1:2] + _shift_up(dcz, 2) * cw[0:1]
        o_ref[:, 0:A_WIDTH] = (dz * gc)[main].astype(BF16)
        o_ref[:, A_WIDTH:2 * A_WIDTH] = (dya * cz)[main].astype(BF16)
        o_ref[:, 2 * A_WIDTH:3 * A_WIDTH] = (dz * h)[main].astype(BF16)
        dczm = dcz[main]
        dcw_ref[0:1, :] += jnp.sum(dczm * z2[main], axis=0, keepdims=True)
        dcw_ref[1:2, :] += jnp.sum(dczm * z1[main], axis=0, keepdims=True)
        dcw_ref[2:3, :] += jnp.sum(dczm * z[main], axis=0, keepdims=True)

        for g, k in enumerate(POOL_WINDOWS):
            lo = 3 * A_WIDTH + g * POOL_GROUP
            cols = slice(g * POOL_GROUP, (g + 1) * POOL_GROUP)
            p = ext[:, lo:lo + POOL_GROUP]
            w = p
            s = 1
            while s < k:
                w = w + _shift_down(w, s)
                s *= 2
            cnt = _window_count(i * ts - HALO, n, k)
            pooled = (w / cnt - p)[main].astype(BF16)
            dyb = dext[:, A_WIDTH + g * POOL_GROUP:A_WIDTH + (g + 1) * POOL_GROUP]
            e = dyb * ps_ref[:, cols]
            pre = jnp.dot(pooled, pw_ref[g], preferred_element_type=F32)
            dps_ref[:, cols] += jnp.sum(dyb[main] * pre, axis=0, keepdims=True)
            dpw_ref[g] += lax.dot_general(pooled, e[main].astype(BF16), tn_dims, preferred_element_type=F32)
            dpooled = lax.dot_general(e.astype(BF16), pw_ref[g], nt_dims, preferred_element_type=F32)
            q = dpooled / cnt
            a = q
            s = 1
            while s < k:
                a = a + _shift_up(a, s)
                s *= 2
            o_ref[:, lo:lo + POOL_GROUP] = (a - dpooled)[main].astype(BF16)

    hb = ts // HALO
    nh = S // HALO
    before_map = lambda i: (jnp.maximum(i * hb - 1, 0), 0)
    after_map = lambda i: (jnp.minimum((i + 1) * hb, nh - 1), 0)
    full = lambda *shape: pl.BlockSpec(shape, lambda i: (0,) * len(shape))
    return pl.pallas_call(
        body, name=name, grid=(nt,),
        in_specs=[
            pl.BlockSpec((ts, EVEN_IN), lambda i: (i, 0)),
            pl.BlockSpec((HALO, EVEN_IN), before_map),
            pl.BlockSpec((HALO, EVEN_IN), after_map),
            pl.BlockSpec((ts, D_MODEL), lambda i: (i, 0)),
            pl.BlockSpec((HALO, D_MODEL), after_map),
            full(3, A_WIDTH), full(4, POOL_GROUP, POOL_GROUP), full(1, 4 * POOL_GROUP),
        ],
        out_specs=[pl.BlockSpec((ts, EVEN_IN), lambda i: (i, 0)), full(3, A_WIDTH), full(4, POOL_GROUP, POOL_GROUP),
                   full(1, 4 * POOL_GROUP)],
        out_shape=[_sds((S, EVEN_IN), BF16), _sds((3, A_WIDTH), F32), _sds((4, POOL_GROUP, POOL_GROUP), F32),
                   _sds((1, 4 * POOL_GROUP), F32)],
        compiler_params=_params("arbitrary"),
    )(proj, proj, proj, dmix, dmix, conv_w, pool_w, pool_scale)


FFN_HALO = 16
FFN_TC = 1408


GLU_CHUNKS = ((0, 512), (512, 512), (1024, 384))


def up_glu_fwd(xn, w_up, conv_w, conv_b, name, tm=512):
    S, K = xn.shape
    nc = D_FF // FFN_TC

    def body(xm_ref, xb_ref, wg_ref, wu_ref, cwg_ref, cwu_ref, cbg_ref, cbu_ref, pg_ref, pu_ref, ug_ref, uu_ref, o_ref):
        before = jnp.where(pl.program_id(1) > 0, xb_ref[...], jnp.zeros_like(xb_ref))
        rows = jnp.concatenate([before, xm_ref[...]], axis=0)
        for lo, width in GLU_CHUNKS:
            cols = slice(lo, lo + width)
            pre_g = jnp.dot(rows, wg_ref[:, cols], preferred_element_type=F32)
            pre_u = jnp.dot(rows, wu_ref[:, cols], preferred_element_type=F32)
            gate = _conv3(pre_g, cwg_ref[:, cols])[FFN_HALO:] + cbg_ref[:, cols]
            upv = _conv3(pre_u, cwu_ref[:, cols])[FFN_HALO:] + cbu_ref[:, cols]
            pg_ref[:, cols] = pre_g[FFN_HALO:].astype(BF16)
            pu_ref[:, cols] = pre_u[FFN_HALO:].astype(BF16)
            ug_ref[:, cols] = gate.astype(BF16)
            uu_ref[:, cols] = upv.astype(BF16)
            o_ref[:, cols] = ((gate * (1.0 / (1.0 + jnp.exp(-gate)))) * upv).astype(BF16)

    hb = tm // FFN_HALO
    wspec = lambda off: pl.BlockSpec((None, None, K, FFN_TC), lambda j, m: (j + off, 0, 0, 0))
    cw = lambda off: pl.BlockSpec((3, FFN_TC), lambda j, m: (0, j + off))
    cb = lambda off: pl.BlockSpec((1, FFN_TC), lambda j, m: (0, j + off))
    out = pl.BlockSpec((tm, FFN_TC), lambda j, m: (m, j))
    pg, pu, ug, uu, act = pl.pallas_call(
        body, name=name, grid=(nc, S // tm),
        in_specs=[pl.BlockSpec((tm, K), lambda j, m: (m, 0)),
                  pl.BlockSpec((FFN_HALO, K), lambda j, m: (jnp.maximum(m * hb - 1, 0), 0)),
                  wspec(0), wspec(nc), cw(0), cw(nc), cb(0), cb(nc)],
        out_specs=[out] * 5, out_shape=[_sds((S, D_FF), BF16)] * 5,
        compiler_params=_params("parallel", "parallel"),
    )(xn, xn, w_up, w_up, conv_w, conv_w, conv_b, conv_b)
    return (pg, pu), (ug, uu), act


def glu_bwd(up, u, da, conv_w, name, ts=256):
    S = up[0].shape[0]
    nc = D_FF // FFN_TC
    nt = S // ts
    W = 2 * D_FF

    def body(xg_ref, xu_ref, gm_ref, ga_ref, um_ref, ua_ref, dm_ref, da_ref, cw_ref, dx_ref, dcw_ref, dcb_ref):
        i = pl.program_id(0)
        last = i == nt - 1

        @pl.when(i == 0)
        def _():
            dcw_ref[...] = jnp.zeros_like(dcw_ref)
            dcb_ref[...] = jnp.zeros_like(dcb_ref)

        def rows(m_ref, a_ref, cols):
            return jnp.concatenate([m_ref[:, cols], a_ref[:, cols]], axis=0).astype(F32)

        def back(d, x, cols):
            cw = cw_ref[:, cols]
            d1, d2 = _shift_up(d, 1), _shift_up(d, 2)
            dx_ref[:, cols] = ((d * cw[2:3] + d1 * cw[1:2]) + d2 * cw[0:1])[:ts].astype(BF16)
            dcb_ref[:, cols] += jnp.sum(d[:ts], axis=0, keepdims=True)
            dcw_ref[0:1, cols] += jnp.sum(d2[:ts] * x, axis=0, keepdims=True)
            dcw_ref[1:2, cols] += jnp.sum(d1[:ts] * x, axis=0, keepdims=True)
            dcw_ref[2:3, cols] += jnp.sum(d[:ts] * x, axis=0, keepdims=True)

        for c in range(nc):
            cols = slice(c * FFN_TC, (c + 1) * FFN_TC)
            ug, uu = rows(gm_ref, ga_ref, cols), rows(um_ref, ua_ref, cols)
            dae = rows(dm_ref, da_ref, cols)
            dae = jnp.where(last & (lax.broadcasted_iota(jnp.int32, dae.shape, 0) >= ts), 0.0, dae)
            sg = 1.0 / (1.0 + jnp.exp(-ug))
            duu = dae * (ug * sg)
            dug = (dae * uu) * (sg * (1.0 + ug * (1.0 - sg)))
            back(dug, xg_ref[:, cols].astype(F32), cols)
            back(duu, xu_ref[:, cols].astype(F32), slice(D_FF + c * FFN_TC, D_FF + (c + 1) * FFN_TC))

    hb = ts // FFN_HALO
    nh = S // FFN_HALO
    after_map = lambda i: (jnp.minimum((i + 1) * hb, nh - 1), 0)
    main = pl.BlockSpec((ts, D_FF), lambda i: (i, 0))
    after = pl.BlockSpec((FFN_HALO, D_FF), after_map)
    return pl.pallas_call(
        body, name=name, grid=(nt,),
        in_specs=[main, main, main, after, main, after, main, after, pl.BlockSpec((3, W), lambda i: (0, 0))],
        out_specs=[pl.BlockSpec((ts, W), lambda i: (i, 0)), pl.BlockSpec((3, W), lambda i: (0, 0)),
                   pl.BlockSpec((1, W), lambda i: (0, 0))],
        out_shape=[_sds((S, W), BF16), _sds((3, W), F32), _sds((1, W), F32)],
        compiler_params=_params("arbitrary"),
    )(up[0], up[1], u[0], u[0], u[1], u[1], da, da, conv_w)


MEAN_GROUP = 256


def _head_mean_matrix():
    h = np.arange(MEAN_GROUP) // HEAD_DIM
    return jnp.asarray((h[:, None] == h[None, :]).astype(np.float32) / HEAD_DIM, dtype=BF16)


def _head_mean(v, gm):
    vb = v.astype(BF16)
    return jnp.concatenate([jnp.dot(vb[:, c:c + MEAN_GROUP], gm, preferred_element_type=F32)
                            for c in range(0, v.shape[1], MEAN_GROUP)], axis=1)


def qkv_qknorm_fwd(xn, w_qkv, gqk, name, tm=1024):
    S, K = xn.shape
    J, _, _, Ns = w_qkv.shape
    gains = gqk.reshape(1, 3 * D_MODEL)

    def body(x_ref, w_ref, g_ref, gm_ref, raw_ref, o_ref):
        first_col = pl.program_id(0) * Ns
        acc = jnp.dot(x_ref[...], w_ref[...], preferred_element_type=F32)
        raw_ref[...] = acc
        gm = gm_ref[...]
        for c in range(0, Ns, MEAN_GROUP):
            cols = slice(c, c + MEAN_GROUP)
            x = acc[:, cols]
            mean = jnp.dot((x * x).astype(BF16), gm, preferred_element_type=F32)
            normed = (x * lax.rsqrt(mean + EPS)) * g_ref[:, cols]
            o_ref[:, cols] = jnp.where(first_col + c >= 2 * D_MODEL, x, normed).astype(BF16)

    return pl.pallas_call(
        body, name=name, grid=(J, S // tm),
        in_specs=[pl.BlockSpec((tm, K), lambda j, m: (m, 0)), pl.BlockSpec((None, None, K, Ns), lambda j, m: (j, 0, 0, 0)),
                  pl.BlockSpec((1, Ns), lambda j, m: (0, j)), pl.BlockSpec((MEAN_GROUP, MEAN_GROUP), lambda j, m: (0, 0))],
        out_specs=[pl.BlockSpec((tm, Ns), lambda j, m: (m, j))] * 2,
        out_shape=[_sds((S, J * Ns), F32), _sds((S, J * Ns), BF16)], compiler_params=_params("parallel", "parallel"),
    )(xn, w_qkv, gains, _head_mean_matrix())


def qknorm_bwd(qkv, dq, dk, dv, gqk, name, ts=256):
    S = qkv.shape[0]

    def body(x_ref, dq_ref, dk_ref, dv_ref, g_ref, gm_ref, o_ref, dg_ref):
        @pl.when(pl.program_id(0) == 0)
        def _():
            dg_ref[...] = jnp.zeros_like(dg_ref)

        gm = gm_ref[...]
        for part, d_ref in enumerate((dq_ref, dk_ref)):
            cols = slice(part * D_MODEL, (part + 1) * D_MODEL)
            x = x_ref[:, cols]
            d = d_ref[...]
            r = lax.rsqrt(_head_mean(x * x, gm) + EPS)
            gx = d * g_ref[part]
            o_ref[:, cols] = (r * gx - x * ((r * r * r) * _head_mean(gx * x, gm))).astype(BF16)
            dg_ref[part] += jnp.sum(d * (x * r), axis=0, keepdims=True)
        o_ref[:, 2 * D_MODEL:] = dv_ref[...].astype(BF16)

    row = pl.BlockSpec((ts, D_MODEL), lambda i: (i, 0))
    wide = pl.BlockSpec((ts, 3 * D_MODEL), lambda i: (i, 0))
    gains = pl.BlockSpec((3, 1, D_MODEL), lambda i: (0, 0, 0))
    return pl.pallas_call(
        body, name=name, grid=(S // ts,),
        in_specs=[wide, row, row, row, gains, pl.BlockSpec((MEAN_GROUP, MEAN_GROUP), lambda i: (0, 0))],
        out_specs=[wide, gains],
        out_shape=[_sds((S, 3 * D_MODEL), BF16), _sds((3, 1, D_MODEL), F32)],
        compiler_params=_params("arbitrary"),
    )(qkv, dq, dk, dv, gqk, _head_mean_matrix())


RESIDUES = 16


def _block_order(dil):
    runs = RESIDUES // dil
    slot = np.arange(ATT_BLOCK)
    return (slot % (ATT_BLOCK // runs)) * runs + slot // (ATT_BLOCK // runs)


def _bucket_tables():
    n = ATT_BLOCK
    max_exact = N_REL_BUCKETS // 2
    buckets, valids = [], []
    for _, dil in DILATED_PAIRS:
        order = _block_order(dil)
        a = order[:, None]
        c = np.concatenate([order, n + order])[None, :]
        first_half = (np.arange(2 * n) < n)[None, :]
        rel = a + n - c
        band = (rel >= 0) & (rel <= n)
        dist = np.clip(rel, 0, n) * dil
        dd = np.maximum(dist, 1).astype(np.float32)
        large = max_exact + (np.log(dd / np.float32(max_exact)) / np.float32(math.log(REL_MAX_DISTANCE / max_exact))
                             * np.float32(N_REL_BUCKETS - max_exact)).astype(np.int32)
        large = np.minimum(large, N_REL_BUCKETS - 1)
        buckets.append(np.where(dist < max_exact, dist, large).reshape(1, -1))
        valids.append(np.stack([(band & ~first_half).reshape(1, -1), band.reshape(1, -1)]))
    return np.stack(buckets).astype(np.int32), np.stack(valids).astype(np.int32)


BIAS_CHUNK = 8192


def _split3(x):
    a = x.astype(BF16)
    r = x - a.astype(F32)
    b = r.astype(BF16)
    c = (r - b.astype(F32)).astype(BF16)
    return a, b, c


def bias_expand(rel_bias_t, name):
    bucket, valid = _bucket_tables()
    nq = bucket.shape[-1]

    def body(t_ref, b_ref, v_ref, o_ref):
        onehot = (lax.broadcasted_iota(jnp.int32, (N_REL_BUCKETS, BIAS_CHUNK), 0) == b_ref[...]).astype(BF16)
        acc = None
        for term in _split3(t_ref[...]):
            p = jnp.dot(term, onehot, preferred_element_type=F32)
            acc = p if acc is None else acc + p
        for v in range(2):
            o_ref[v] = jnp.where(v_ref[v] > 0, acc, MASK_VALUE)

    return pl.pallas_call(
        body, name=name, grid=(3, nq // BIAS_CHUNK),
        in_specs=[pl.BlockSpec((N_HEADS, N_REL_BUCKETS), lambda b, c: (0, 0)),
                  pl.BlockSpec((None, 1, BIAS_CHUNK), lambda b, c: (b, 0, c)),
                  pl.BlockSpec((None, 2, 1, BIAS_CHUNK), lambda b, c: (b, 0, 0, c))],
        out_specs=pl.BlockSpec((None, 2, N_HEADS, BIAS_CHUNK), lambda b, c: (b, 0, 0, c)),
        out_shape=_sds((3, 2, N_HEADS, nq), F32), compiler_params=_params("parallel", "parallel"),
    )(rel_bias_t, jnp.asarray(bucket), jnp.asarray(valid))


def bias_reduce(dbias, name):
    bucket, _ = _bucket_tables()
    nq = bucket.shape[-1]
    dims = (((1,), (1,)), ((), ()))

    def body(d_ref, b_ref, o_ref):
        onehot = (lax.broadcasted_iota(jnp.int32, (N_REL_BUCKETS, BIAS_CHUNK), 0) == b_ref[...]).astype(BF16)
        acc = None
        for term in _split3(d_ref[...]):
            p = lax.dot_general(term, onehot, dims, preferred_element_type=F32)
            acc = p if acc is None else acc + p

        @pl.when(pl.program_id(1) == 0)
        def _():
            o_ref[...] = acc

        @pl.when(pl.program_id(1) > 0)
        def _():
            o_ref[...] += acc

    return pl.pallas_call(
        body, name=name, grid=(3, nq // BIAS_CHUNK),
        in_specs=[pl.BlockSpec((None, N_HEADS, BIAS_CHUNK), lambda b, c: (b, 0, c)),
                  pl.BlockSpec((None, 1, BIAS_CHUNK), lambda b, c: (b, 0, c))],
        out_specs=pl.BlockSpec((None, N_HEADS, N_REL_BUCKETS), lambda b, c: (b, 0, 0)),
        out_shape=_sds((3, N_HEADS, N_REL_BUCKETS), F32), compiler_params=_params("parallel", "arbitrary"),
    )(dbias, jnp.asarray(bucket))


PAIR = 2 * HEAD_DIM
N_PAIRS = N_HEADS // 2
_NT = (((1,), (1,)), ((), ()))
_TN = (((0,), (0,)), ((), ()))


def _low_lanes(shape):
    return lax.broadcasted_iota(jnp.int32, shape, 1) < HEAD_DIM


ATTN_VMEM_LIMIT_BYTES = 56 * 1024 * 1024
BRANCH_ORDER = (2, 1, 0)


def _regroup(dst, src, L16):
    for r in range(RESIDUES):
        dst[pl.ds(r * L16, L16), :] = src[pl.ds(r, L16, stride=RESIDUES), :]


def _ungroup(dst, src, L16):
    for r in range(RESIDUES):
        dst[pl.ds(r, L16, stride=RESIDUES), :] = src[pl.ds(r * L16, L16), :]


def _branch_geometry(branch, S):
    dil = DILATED_PAIRS[branch][1]
    runs = RESIDUES // dil
    return dil, runs, ATT_BLOCK // runs, S // dil // ATT_BLOCK


def _block_rows(it, branch, S):
    dil, runs, run_len, n_blocks = _branch_geometry(branch, S)
    L16 = S // RESIDUES
    r, b = it // n_blocks, it % n_blocks
    prev = jnp.maximum(b - 1, 0)
    cur_rows = [pl.multiple_of((j * dil + r) * L16 + run_len * b, 8) for j in range(runs)]
    prev_rows = [pl.multiple_of((j * dil + r) * L16 + run_len * prev, 8) for j in range(runs)]
    return cur_rows, prev_rows, jnp.minimum(b, 1)


def _load_block(ref, rows, run_len):
    parts = [ref[pl.ds(o, run_len), :] for o in rows]
    return parts[0] if len(parts) == 1 else jnp.concatenate(parts, axis=0)


def _store_block(ref, rows, run_len, value, add=False):
    for j, o in enumerate(rows):
        part = value[j * run_len:(j + 1) * run_len]
        if add:
            ref[pl.ds(o, run_len), :] += part
        else:
            ref[pl.ds(o, run_len), :] = part


ATTN_FWD_UNROLL = 32
ATTN_BWD_UNROLL = 32


def _stack_heads(x, low):
    zero = jnp.zeros_like(x)
    return jnp.concatenate([jnp.where(low, x, zero), jnp.where(low, zero, x)], axis=0)


def _unstack_heads(y, low):
    return jnp.where(low, y[:ATT_BLOCK], y[ATT_BLOCK:])


def attn_fwd(qkvn, bias, name):
    S = qkvn.shape[0]
    L16 = S // RESIDUES
    n_iter = S // ATT_BLOCK

    def body(q_ref, k_ref, v_ref, b_ref, o_ref, lse_ref, stage, qp, kp, vp, acc_s, m_s, l_s):
        for src, dst in ((q_ref, qp), (k_ref, kp), (v_ref, vp)):
            stage[...] = src[...].astype(F32)
            _regroup(dst, stage, L16)
        low = _low_lanes((ATT_BLOCK, PAIR))

        for branch in BRANCH_ORDER:
            _, _, run_len, _ = _branch_geometry(branch, S)
            first = branch == BRANCH_ORDER[0]

            def step(it, carry, branch=branch, run_len=run_len, first=first):
                cur, prev, variant = _block_rows(it, branch, S)
                q = _load_block(qp, cur, run_len).astype(BF16)
                k = jnp.concatenate([_load_block(kp, prev, run_len), _load_block(kp, cur, run_len)], axis=0).astype(BF16)
                v = jnp.concatenate([_load_block(vp, prev, run_len), _load_block(vp, cur, run_len)], axis=0).astype(BF16)
                s = lax.dot_general(_stack_heads(q, low), k, _NT, preferred_element_type=F32) * (HEAD_DIM ** -0.5)
                s = s + b_ref[2 * branch + variant].reshape(2 * ATT_BLOCK, 2 * ATT_BLOCK)
                mx = jnp.max(s, axis=-1, keepdims=True)
                p = jnp.exp(s - mx)
                den = jnp.sum(p, axis=-1, keepdims=True)
                pv = jnp.dot(p.astype(BF16), v, preferred_element_type=F32)
                acc = _unstack_heads(pv, low)
                m = _unstack_heads(mx, low)
                l = _unstack_heads(den, low)
                if not first:
                    m_old = _load_block(m_s, cur, run_len)
                    m_new = jnp.maximum(m_old, m)
                    a_old, a_new = jnp.exp(m_old - m_new), jnp.exp(m - m_new)
                    acc = _load_block(acc_s, cur, run_len) * a_old + acc * a_new
                    l = _load_block(l_s, cur, run_len) * a_old + l * a_new
                    m = m_new
                _store_block(acc_s, cur, run_len, acc)
                _store_block(m_s, cur, run_len, m)
                _store_block(l_s, cur, run_len, l)
                return carry

            lax.fori_loop(0, n_iter, step, 0, unroll=ATTN_FWD_UNROLL)

        acc_s[...] = acc_s[...] / l_s[...]
        _ungroup(stage, acc_s, L16)
        o_ref[...] = stage[...].astype(BF16)
        m_s[...] = m_s[...] + jnp.log(l_s[...])
        _ungroup(lse_ref, m_s, L16)

    col = lambda part: pl.BlockSpec((S, PAIR), lambda hp: (0, part * N_PAIRS + hp))
    out = pl.BlockSpec((S, PAIR), lambda hp: (0, hp))
    return pl.pallas_call(
        body, name=name, grid=(N_PAIRS,),
        in_specs=[col(0), col(1), col(2), pl.BlockSpec((6, 2, ATT_BLOCK, 2 * ATT_BLOCK), lambda hp: (0, hp, 0, 0))],
        out_specs=[out, out], out_shape=[_sds((S, D_MODEL), BF16), _sds((S, D_MODEL), F32)],
        scratch_shapes=[pltpu.VMEM((S, PAIR), F32)] * 7,
        compiler_params=pltpu.CompilerParams(dimension_semantics=("parallel",), vmem_limit_bytes=ATTN_VMEM_LIMIT_BYTES),
    )(qkvn, qkvn, qkvn, bias)


def attn_bwd(qkvn, att, datt, lse, bias, name):
    S = qkvn.shape[0]
    L16 = S // RESIDUES
    n_iter = S // ATT_BLOCK
    TILE = 512

    def body(q_ref, k_ref, v_ref, o_ref, do_ref, lse_ref, b_ref, dq_ref, dk_ref, dv_ref, db_ref,
             qp, kp, vp, dop, ldp, dqp, dkp, dvp):
        stage = dqp
        for src, dst in ((q_ref, qp), (k_ref, kp), (v_ref, vp), (do_ref, dop)):
            stage[...] = src[...].astype(F32)
            _regroup(dst, stage, L16)

        def pack(i, carry):
            rows = pl.ds(pl.multiple_of(i * TILE, TILE), TILE)
            low = _low_lanes((TILE, PAIR))
            lane = lax.broadcasted_iota(jnp.int32, (TILE, PAIR), 1)
            prod = do_ref[rows, :].astype(F32) * o_ref[rows, :].astype(F32)
            d0 = jnp.sum(jnp.where(low, prod, 0.0), axis=-1, keepdims=True)
            d1 = jnp.sum(jnp.where(low, 0.0, prod), axis=-1, keepdims=True)
            stage[rows, :] = jnp.where((lane & (HEAD_DIM // 2)) == 0, lse_ref[rows, :], jnp.where(low, d0, d1))
            return carry

        lax.fori_loop(0, S // TILE, pack, 0)
        _regroup(ldp, stage, L16)
        dqp[...] = jnp.zeros_like(dqp)
        dkp[...] = jnp.zeros_like(dkp)
        dvp[...] = jnp.zeros_like(dvp)
        db_ref[...] = jnp.zeros_like(db_ref)
        low = _low_lanes((ATT_BLOCK, PAIR))

        for branch in BRANCH_ORDER:
            _, _, run_len, _ = _branch_geometry(branch, S)

            def step(it, carry, branch=branch, run_len=run_len):
                cur, prev, variant = _block_rows(it, branch, S)
                q = _load_block(qp, cur, run_len).astype(BF16)
                dout = _load_block(dop, cur, run_len).astype(BF16)
                ld = _load_block(ldp, cur, run_len)
                k = jnp.concatenate([_load_block(kp, prev, run_len), _load_block(kp, cur, run_len)], axis=0).astype(BF16)
                v = jnp.concatenate([_load_block(vp, prev, run_len), _load_block(vp, cur, run_len)], axis=0).astype(BF16)
                half = HEAD_DIM // 2
                lse2 = jnp.concatenate([ld[:, 0:1], ld[:, HEAD_DIM:HEAD_DIM + 1]], axis=0)
                delta2 = jnp.concatenate([ld[:, half:half + 1], ld[:, HEAD_DIM + half:HEAD_DIM + half + 1]], axis=0)
                q2, do2 = _stack_heads(q, low), _stack_heads(dout, low)
                s = lax.dot_general(q2, k, _NT, preferred_element_type=F32) * (HEAD_DIM ** -0.5)
                p = jnp.exp(s + b_ref[2 * branch + variant].reshape(2 * ATT_BLOCK, 2 * ATT_BLOCK) - lse2)
                dp = lax.dot_general(do2, v, _NT, preferred_element_type=F32)
                ds = p * (dp - delta2)
                db_ref[branch] += ds.reshape(2, ATT_BLOCK, 2 * ATT_BLOCK)
                dsb = (ds * (HEAD_DIM ** -0.5)).astype(BF16)
                dq = _unstack_heads(jnp.dot(dsb, k, preferred_element_type=F32), low)
                dk = lax.dot_general(dsb, q2, _TN, preferred_element_type=F32)
                dv = lax.dot_general(p.astype(BF16), do2, _TN, preferred_element_type=F32)
                _store_block(dqp, cur, run_len, dq, add=True)
                _store_block(dkp, prev, run_len, dk[:ATT_BLOCK], add=True)
                _store_block(dvp, prev, run_len, dv[:ATT_BLOCK], add=True)
                _store_block(dkp, cur, run_len, dk[ATT_BLOCK:], add=True)
                _store_block(dvp, cur, run_len, dv[ATT_BLOCK:], add=True)
                return carry

            lax.fori_loop(0, n_iter, step, 0, unroll=ATTN_BWD_UNROLL)

        _ungroup(dq_ref, dqp, L16)
        _ungroup(dk_ref, dkp, L16)
        _ungroup(dv_ref, dvp, L16)

    col = lambda part: pl.BlockSpec((S, PAIR), lambda hp: (0, part * N_PAIRS + hp))
    one = pl.BlockSpec((S, PAIR), lambda hp: (0, hp))
    return pl.pallas_call(
        body, name=name, grid=(N_PAIRS,),
        in_specs=[col(0), col(1), col(2), one, one, one,
                  pl.BlockSpec((6, 2, ATT_BLOCK, 2 * ATT_BLOCK), lambda hp: (0, hp, 0, 0))],
        out_specs=[one, one, one, pl.BlockSpec((3, 2, ATT_BLOCK, 2 * ATT_BLOCK), lambda hp: (0, hp, 0, 0))],
        out_shape=[_sds((S, D_MODEL), F32)] * 3 + [_sds((3, N_HEADS, ATT_BLOCK, 2 * ATT_BLOCK), F32)],
        scratch_shapes=[pltpu.VMEM((S, PAIR), F32)] * 8,
        compiler_params=pltpu.CompilerParams(dimension_semantics=("parallel",), vmem_limit_bytes=ATTN_VMEM_LIMIT_BYTES),
    )(qkvn, qkvn, qkvn, att, datt, lse, bias)


def _adamw_step(w_ref, g_ref, m_ref, v_ref, d_ref, nm_ref, nv_ref):
    gv = g_ref[...]
    m2 = ADAM_B1 * m_ref[...] + (1.0 - ADAM_B1) * gv
    v2 = ADAM_B2 * v_ref[...] + (1.0 - ADAM_B2) * (gv * gv)
    m_hat = m2 / (1.0 - ADAM_B1 ** ADAM_STEP)
    v_hat = v2 / (1.0 - ADAM_B2 ** ADAM_STEP)
    d_ref[...] = -ADAM_LR * (m_hat / (jnp.sqrt(v_hat) + ADAM_EPS) + ADAM_WD * w_ref[...])
    nm_ref[...] = m2
    nv_ref[...] = v2


def adamw_small(ws, gs, ms, vs, name):
    n = len(ws)

    def body(*refs):
        groups = [refs[k * n:(k + 1) * n] for k in range(7)]
        for refs_of_one in zip(*groups):
            _adamw_step(*refs_of_one)

    outs = pl.pallas_call(body, name=name, out_shape=[_sds(a.shape, F32) for a in ws] * 3,
                          compiler_params=_params())(*ws, *gs, *ms, *vs)
    return outs[:n], outs[n:2 * n], outs[2 * n:]


def adamw(w, g, m, v, name):
    n, R, C = w.shape

    def body(w_ref, g_ref, m_ref, v_ref, d_ref, nm_ref, nv_ref, go_ref):
        go_ref[...] = g_ref[...]
        _adamw_step(w_ref, g_ref, m_ref, v_ref, d_ref, nm_ref, nv_ref)

    tr = R
    while tr * C * 4 > ELEMENTWISE_BLOCK_BYTES and tr % 16 == 0:
        tr //= 2
    spec = pl.BlockSpec((None, tr, C), lambda i, r: (i, r, 0))
    return pl.pallas_call(
        body, name=name, grid=(n, R // tr), in_specs=[spec] * 4, out_specs=[spec] * 4,
        out_shape=[_sds((n, R, C), F32)] * 4, compiler_params=_params("parallel", "parallel"),
    )(w, g, m, v)


ANY = pl.BlockSpec(memory_space=pl.ANY)


def _coords():
    return lax.axis_index("x"), lax.axis_index("y"), lax.axis_index("c")


def _other_chips(mx, my):
    return [(1 - mx, my), (mx, 1 - my), (1 - mx, 1 - my)]


def _remote(src, dst, send, recv, dev):
    return pltpu.make_async_remote_copy(src_ref=src, dst_ref=dst, send_sem=send, recv_sem=recv, device_id=dev,
                                        device_id_type=MESH)


HBM =pl.BlockSpec(memory_space=pltpu.HBM)
SEM = pl.BlockSpec(memory_space=pltpu.SEMAPHORE)
_SPLIT_COPY = pltpu.CompilerParams(has_side_effects=pltpu.SideEffectType.DATAFLOW_SIDE_EFFECTING)


def _in_hbm(a):
    return pltpu.with_memory_space_constraint(a, pltpu.HBM)


def cast_into_slot(w, layer, chip_core, name, dtype=BF16):
    _, _, hR, C = w.shape

    def body(s_ref, w_ref, o_ref):
        del s_ref
        o_ref[...] = w_ref[...].astype(dtype)

    grid_spec = pltpu.PrefetchScalarGridSpec(
        num_scalar_prefetch=1, grid=(2,),
        in_specs=[pl.BlockSpec((None, None, hR, C), lambda h, s: (layer, h, 0, 0))],
        out_specs=pl.BlockSpec((None, None, hR, C), lambda h, s: (s[0], h, 0, 0)))
    return pl.pallas_call(body, name=name, grid_spec=grid_spec, out_shape=_sds_hbm((N_CHIPS, 2, hR, C), dtype),
                          compiler_params=_params("parallel"))(chip_core, w)


def gather_start(lands, groups, name):
    n = len(lands)
    n_groups = len(groups)

    def body(*refs):
        ins = refs[:n]
        sems = refs[n:n + 2 * n_groups]
        token = refs[-1]
        mx, my, mc = _coords()
        chip = 2 * mx + my
        for g, members in enumerate(groups):
            send, recv = sems[2 * g], sems[2 * g + 1]
            for i, a in enumerate(members):
                mine = ins[a].at[chip, mc]
                for k, (px, py) in enumerate(_other_chips(mx, my)):
                    _remote(mine, mine, send.at[3 * i + k], recv.at[3 * i + k], (px, py, mc)).start()
        token[...] = jnp.zeros_like(token)

    sem_shapes = []
    for members in groups:
        sem_shapes += [pltpu.SemaphoreType.DMA((3 * len(members),))] * 2
    outs = pl.pallas_call(
        body, name=name, in_specs=[HBM] * n,
        out_specs=[SEM] * (2 * n_groups) + [HBM] * n + [pl.BlockSpec(memory_space=pltpu.VMEM)],
        out_shape=sem_shapes + [pltpu.HBM(a.shape, a.dtype) for a in lands] + [_sds((SUBLANES, LANES), F32)],
        input_output_aliases={a: 2 * n_groups + a for a in range(n)}, compiler_params=_SPLIT_COPY,
    )(*[_in_hbm(a) for a in lands])
    sems = [(outs[2 * g], outs[2 * g + 1]) for g in range(n_groups)]
    return sems, list(outs[2 * n_groups:2 * n_groups + n]), outs[-1]


def gather_forward(lands, sems, after, name):
    n = len(lands)

    def body(*refs):
        ins = refs[:n]
        send, recv = refs[n], refs[n + 1]
        fsend, frecv = refs[n + 3], refs[n + 4]
        mx, my, mc = _coords()
        for i in range(n):
            for k, (px, py) in enumerate(_other_chips(mx, my)):
                landed = ins[i].at[2 * px + py, mc]
                cp = _remote(landed, landed, send.at[3 * i + k], recv.at[3 * i + k], (px, py, mc))
                cp.wait_send()
                cp.wait_recv()
                _remote(landed, landed, fsend.at[3 * i + k], frecv.at[3 * i + k], (mx, my, 1 - mc)).start()

    outs = pl.pallas_call(
        body, name=name, in_specs=[HBM] * n + [SEM, SEM, ANY], out_specs=[SEM, SEM] + [HBM] * n,
        out_shape=[pltpu.SemaphoreType.DMA((3 * n,))] * 2 + [pltpu.HBM(a.shape, a.dtype) for a in lands],
        input_output_aliases={a: 2 + a for a in range(n)}, compiler_params=_SPLIT_COPY,
    )(*lands, sems[0], sems[1], after)
    return (outs[0], outs[1]), list(outs[2:])


def gather_wait(lands, sems, after, name):
    n = len(lands)

    def body(*refs):
        ins = refs[:n]
        fsend, frecv = refs[n], refs[n + 1]
        mx, my, mc = _coords()
        for i in range(n):
            for k, (px, py) in enumerate(_other_chips(mx, my)):
                theirs = ins[i].at[2 * px + py, 1 - mc]
                cp = _remote(theirs, theirs, fsend.at[3 * i + k], frecv.at[3 * i + k], (mx, my, 1 - mc))
                cp.wait_send()
                cp.wait_recv()

    outs = pl.pallas_call(
        body, name=name, in_specs=[HBM] * n + [SEM, SEM, ANY], out_specs=[HBM] * n,
        out_shape=[pltpu.HBM(a.shape, a.dtype) for a in lands],
        input_output_aliases={a: a for a in range(n)}, compiler_params=_SPLIT_COPY,
    )(*lands, sems[0], sems[1], after)
    return list(outs)


def _peers(mx, my, mc):
    return [(1 - mx if k & 4 else mx, 1 - my if k & 2 else my, 1 - mc if k & 1 else mc) for k in range(1, N_DEV)]


def devices_start(x, name):
    def body(x_ref, land_ref, send, recv, x_thru, land_thru):
        mx, my, mc = _coords()
        me = 4 * mx + 2 * my + mc
        for k, peer in enumerate(_peers(mx, my, mc)):
            _remote(x_ref, land_ref.at[me], send.at[k], recv.at[k], peer).start()

    land = lax.empty((N_DEV,) + x.shape, x.dtype)
    outs = pl.pallas_call(
        body, name=name, in_specs=[HBM, HBM], out_specs=[SEM, SEM, HBM, HBM],
        out_shape=[pltpu.SemaphoreType.DMA((N_DEV - 1,))] * 2 + [pltpu.HBM(x.shape, x.dtype), pltpu.HBM(land.shape, x.dtype)],
        input_output_aliases={0: 2, 1: 3}, compiler_params=_SPLIT_COPY,
    )(_in_hbm(x), _in_hbm(land))
    return (outs[0], outs[1]), outs[2], outs[3]


def devices_wait(x, land, sems, after, name):
    def body(x_ref, land_ref, send, recv, after_ref, x_thru, land_thru):
        mx, my, mc = _coords()
        for k, (px, py, pc) in enumerate(_peers(mx, my, mc)):
            cp = _remote(x_ref, land_ref.at[4 * px + 2 * py + pc], send.at[k], recv.at[k], (px, py, pc))
            cp.wait_send()
            cp.wait_recv()

    outs = pl.pallas_call(
        body, name=name, in_specs=[HBM, HBM, SEM, SEM, ANY], out_specs=[HBM, HBM],
        out_shape=[pltpu.HBM(x.shape, x.dtype), pltpu.HBM(land.shape, land.dtype)],
        input_output_aliases={0: 0, 1: 1}, compiler_params=_SPLIT_COPY,
    )(x, land, sems[0], sems[1], after)
    return outs[0], outs[1]


def device_sum(land, own, me, name):
    _, R, C = land.shape

    def body(s_ref, l_ref, o_ref_in, o_ref):
        acc = None
        for q in range(N_DEV):
            term = jnp.where(s_ref[0] == q, o_ref_in[...], l_ref[q])
            acc = term if acc is None else acc + term
        o_ref[...] = acc

    grid_spec = pltpu.PrefetchScalarGridSpec(
        num_scalar_prefetch=1, grid=(1,),
        in_specs=[pl.BlockSpec((N_DEV, R, C), lambda i, s: (0, 0, 0)), pl.BlockSpec((R, C), lambda i, s: (0, 0))],
        out_specs=pl.BlockSpec((R, C), lambda i, s: (0, 0)))
    return pl.pallas_call(body, name=name, grid_spec=grid_spec, out_shape=_sds((R, C), F32),
                          compiler_params=_params("arbitrary"))(me, land, own)


def reduce_send(grads, name):
    n = len(grads)

    def body(*refs):
        ins, lands = refs[:n], refs[n:2 * n]
        send, recv = refs[2 * n], refs[2 * n + 1]
        mx, my, mc = _coords()
        me = 4 * mx + 2 * my + mc
        for a in range(n):
            for k, (px, py, pc) in enumerate(_peers(mx, my, mc)):
                _remote(ins[a].at[2 * px + py, pc], lands[a].at[me], send.at[7 * a + k], recv.at[7 * a + k], (px, py, pc)).start()

    lands = [lax.empty((N_DEV,) + g.shape[2:], g.dtype) for g in grads]
    outs = pl.pallas_call(
        body, name=name, in_specs=[HBM] * (2 * n), out_specs=[SEM, SEM] + [HBM] * (2 * n),
        out_shape=[pltpu.SemaphoreType.DMA((7 * n,))] * 2 + [pltpu.HBM(a.shape, a.dtype) for a in grads + lands],
        input_output_aliases={a: 2 + a for a in range(2 * n)}, compiler_params=_SPLIT_COPY,
    )(*[_in_hbm(a) for a in grads + lands])
    return (outs[0], outs[1]), list(outs[2:2 + n]), list(outs[2 + n:])


def reduce_wait(grads, lands, sems, after, name):
    n = len(grads)

    def body(*refs):
        ins, zones = refs[:n], refs[n:2 * n]
        send, recv = refs[2 * n], refs[2 * n + 1]
        mx, my, mc = _coords()
        for a in range(n):
            for k, (px, py, pc) in enumerate(_peers(mx, my, mc)):
                cp = _remote(ins[a].at[2 * px + py, pc], zones[a].at[4 * px + 2 * py + pc], send.at[7 * a + k],
                             recv.at[7 * a + k], (px, py, pc))
                cp.wait_send()
                cp.wait_recv()

    outs = pl.pallas_call(
        body, name=name, in_specs=[HBM] * (2 * n) + [SEM, SEM, ANY], out_specs=[HBM] * (2 * n),
        out_shape=[pltpu.HBM(a.shape, a.dtype) for a in grads + lands],
        input_output_aliases={a: a for a in range(2 * n)}, compiler_params=_SPLIT_COPY,
    )(*grads, *lands, sems[0], sems[1], after)
    return list(outs[:n]), list(outs[n:])


def reduce_sum(land, grad, place, name, into=None, layer=None):
    _, hR, C = land.shape
    tr = hR
    while N_DEV * tr * C * 2 > 3 * ELEMENTWISE_BLOCK_BYTES and tr % 32 == 0:
        tr //= 2

    def body(s_ref, l_ref, g_ref, *rest):
        o_ref = rest[-1]
        own = g_ref[...].astype(F32)
        acc = None
        for q in range(N_DEV):
            term = jnp.where(s_ref[2] == q, own, l_ref[q].astype(F32))
            acc = term if acc is None else acc + term
        o_ref[...] = acc

    in_specs = [pl.BlockSpec((N_DEV, tr, C), lambda i, s: (0, i, 0)),
                pl.BlockSpec((None, None, tr, C), lambda i, s: (s[0], s[1], i, 0))]
    args = [place, _in_hbm(land), _in_hbm(grad)]
    aliases = {}
    if layer is None:
        out_spec = pl.BlockSpec((None, tr, C), lambda i, s: (s[1], i, 0))
        out_shape = _sds_hbm((2, hR, C), F32)
    else:
        out_spec = pl.BlockSpec((None, None, tr, C), lambda i, s: (layer, s[1], i, 0))
        out_shape = _sds_hbm((2, 2, hR, C), F32)
        if into is not None:
            in_specs.append(ANY)
            args.append(into)
            aliases = {3: 0}
    grid_spec = pltpu.PrefetchScalarGridSpec(num_scalar_prefetch=1, grid=(hR // tr,), in_specs=in_specs, out_specs=out_spec)
    return pl.pallas_call(body, name=name, grid_spec=grid_spec, out_shape=out_shape, input_output_aliases=aliases,
                          compiler_params=_params("arbitrary"))(*args)


def join_halves(arrays, name):
    n = len(arrays)
    pieces = [(a, l) for a, arr in enumerate(arrays) for l in (range(arr.shape[0]) if arr.ndim == 4 else [None])]

    def body(*refs):
        ins = refs[:n]
        send, recv = refs[2 * n:]
        mx, my, mc = _coords()

        def half(a, l, h):
            return ins[a].at[h] if l is None else ins[a].at[l, h]

        sends = [_remote(half(a, l, mc), half(a, l, mc), send.at[i], recv.at[i], (mx, my, 1 - mc))
                 for i, (a, l) in enumerate(pieces)]
        for cp in sends:
            cp.start()
        for i, (a, l) in enumerate(pieces):
            theirs = half(a, l, 1 - mc)
            _remote(theirs, theirs, send.at[i], recv.at[i], (mx, my, 1 - mc)).wait_recv()
        for cp in sends:
            cp.wait_send()

    return pl.pallas_call(
        body, name=name, in_specs=[ANY] * n, out_specs=[ANY] * n, out_shape=[_sds(a.shape, a.dtype) for a in arrays],
        input_output_aliases={a: a for a in range(n)},
        scratch_shapes=[pltpu.SemaphoreType.DMA((len(pieces),)), pltpu.SemaphoreType.DMA((len(pieces),))],
    )(*arrays)


LANES = 128
SUBLANES = 8


def _n_rows(shape):
    rows = -(-int(np.prod(shape)) // LANES)
    return -(-rows // SUBLANES) * SUBLANES


def _as_rows(a):
    flat = a.reshape(-1)
    rows = _n_rows(a.shape)
    return jnp.pad(flat, (0, rows * LANES - flat.shape[0])).reshape(rows, LANES)


def _pack(arrays):
    return jnp.concatenate([_as_rows(a) for a in arrays], axis=0)


def _unpack(rows, shapes):
    out, r0 = [], 0
    for s in shapes:
        n = _n_rows(s)
        out.append(rows[r0:r0 + n].reshape(-1)[:int(np.prod(s))].reshape(s))
        r0 += n
    return out


REPLICATED_SMALL = [("rel_bias", (32, 16)), ("even_norm", (1, 1024)), ("even_pool_w", (1, 4, 128, 128)),
                    ("even_pool_scale", (1, 512)), ("odd_q_norm", (1, 64)), ("odd_k_norm", (1, 64)),
                    ("ffn_norm", (2, 1024)), ("ffn_conv_b", (2, 5632))]
SHARDED_SMALL = [("even_conv_w", (1, 3, 128)), ("odd_norm", (1, 256)), ("ffn_conv_w", (2, 3, 1408))]
WEIGHT_ORDER = ["rel_bias", "even_norm", "even_w_in", "even_conv_w", "even_pool_w", "even_pool_scale", "even_w_out",
                "odd_norm", "odd_w_qkv", "odd_q_norm", "odd_k_norm", "odd_w_o", "ffn_norm", "ffn_w_up", "ffn_conv_w",
                "ffn_conv_b", "ffn_w_down"]


def kernel(x, rel_bias, even_norm, even_w_in, even_conv_w, even_pool_w, even_pool_scale, even_w_out, odd_norm, odd_w_qkv, odd_q_norm, odd_k_norm, odd_w_o, ffn_norm, ffn_w_up, ffn_conv_w, ffn_conv_b, ffn_w_down, loss_target, m_rel_bias, m_even_norm, m_even_w_in, m_even_conv_w, m_even_pool_w, m_even_pool_scale, m_even_w_out, m_odd_norm, m_odd_w_qkv, m_odd_q_norm, m_odd_k_norm, m_odd_w_o, m_ffn_norm, m_ffn_w_up, m_ffn_conv_w, m_ffn_conv_b, m_ffn_w_down, v_rel_bias, v_even_norm, v_even_w_in, v_even_conv_w, v_even_pool_w, v_even_pool_scale, v_even_w_out, v_odd_norm, v_odd_w_qkv, v_odd_q_norm, v_odd_k_norm, v_odd_w_o, v_ffn_norm, v_ffn_w_up, v_ffn_conv_w, v_ffn_conv_b, v_ffn_w_down):
    W = dict(rel_bias=rel_bias, even_norm=even_norm, even_w_in=even_w_in, even_conv_w=even_conv_w, even_pool_w=even_pool_w,
             even_pool_scale=even_pool_scale, even_w_out=even_w_out, odd_norm=odd_norm, odd_w_qkv=odd_w_qkv,
             odd_q_norm=odd_q_norm, odd_k_norm=odd_k_norm, odd_w_o=odd_w_o, ffn_norm=ffn_norm, ffn_w_up=ffn_w_up,
             ffn_conv_w=ffn_conv_w, ffn_conv_b=ffn_conv_b, ffn_w_down=ffn_w_down)
    M1 = dict(rel_bias=m_rel_bias, even_norm=m_even_norm, even_w_in=m_even_w_in, even_conv_w=m_even_conv_w,
              even_pool_w=m_even_pool_w, even_pool_scale=m_even_pool_scale, even_w_out=m_even_w_out, odd_norm=m_odd_norm,
              odd_w_qkv=m_odd_w_qkv, odd_q_norm=m_odd_q_norm, odd_k_norm=m_odd_k_norm, odd_w_o=m_odd_w_o,
              ffn_norm=m_ffn_norm, ffn_w_up=m_ffn_w_up, ffn_conv_w=m_ffn_conv_w, ffn_conv_b=m_ffn_conv_b,
              ffn_w_down=m_ffn_w_down)
    M2 = dict(rel_bias=v_rel_bias, even_norm=v_even_norm, even_w_in=v_even_w_in, even_conv_w=v_even_conv_w,
              even_pool_w=v_even_pool_w, even_pool_scale=v_even_pool_scale, even_w_out=v_even_w_out, odd_norm=v_odd_norm,
              odd_w_qkv=v_odd_w_qkv, odd_q_norm=v_odd_q_norm, odd_k_norm=v_odd_k_norm, odd_w_o=v_odd_w_o,
              ffn_norm=v_ffn_norm, ffn_w_up=v_ffn_w_up, ffn_conv_w=v_ffn_conv_w, ffn_conv_b=v_ffn_conv_b,
              ffn_w_down=v_ffn_w_down)
    mx, my, mc = _coords()
    chip = 2 * mx + my
    me = 4 * mx + 2 * my + mc
    place = jnp.stack([chip, mc, me]).astype(jnp.int32)
    xs, target = x[0], loss_target[0]

    def halves(w):
        return w.reshape((w.shape[0], 2, w.shape[-2] // 2, w.shape[-1]))

    small_rows = jnp.pad(_pack([even_conv_w, odd_norm, ffn_conv_w]), ((0, SUBLANES), (0, 0)))
    first = [cast_into_slot(halves(even_w_in), 0, place, "cast_w_in"), cast_into_slot(halves(even_w_out), 0, place, "cast_w_out"),
             cast_into_slot(small_rows.reshape(1, 2, small_rows.shape[0] // 2, LANES), 0, place, "small_into_slot", dtype=F32),
             cast_into_slot(halves(ffn_w_up), 0, place, "cast_w_up0")]
    first_sems, first, token = gather_start(first, [[0, 1, 2], [3]], "gather_start_first")
    even_norm_after_start = even_norm + token[0:1, 0:1]

    def later(a):
        return lax.optimization_barrier((a, token))[0]

    down_f32 = halves(later(ffn_w_down))
    rest = [cast_into_slot(down_f32, 0, place, "cast_w_down0"),
            cast_into_slot(halves(later(odd_w_qkv)), 0, place, "cast_w_qkv"), cast_into_slot(halves(later(odd_w_o)), 0, place, "cast_w_o"),
            cast_into_slot(halves(later(ffn_w_up)), 1, place, "cast_w_up1"), cast_into_slot(down_f32, 1, place, "cast_w_down1")]
    rest_sems, rest, rest_token = gather_start(rest, [[0], [1, 2], [3], [4]], "gather_start_rest")
    group_arrays = [first[:3], [first[3]], [rest[0]], [rest[1], rest[2]], [rest[3]], [rest[4]]]
    group_sems = first_sems + rest_sems

    passing = {}

    def pass_on(group, tag, after, then):
        sems, arrays = gather_forward(group_arrays[group], group_sems[group], after, "gather_forward_" + tag)
        then, arrays = lax.optimization_barrier((then, arrays))
        passing[group] = (sems, arrays)
        return then

    def gathered(group, tag, after):
        sems, arrays = passing.pop(group)
        return gather_wait(arrays, sems, after, "gather_wait_" + tag)

    pool_w = cast_bf16(even_pool_w[0], "cast_pool_w")
    gqk = jnp.stack([jnp.tile(odd_q_norm[0], N_HEADS), jnp.tile(odd_k_norm[0], N_HEADS),
                     jnp.ones((D_MODEL,), F32)])[:, None, :]
    bias = bias_expand(later(rel_bias).T, "bias_expand").reshape(6, N_HEADS, ATT_BLOCK, 2 * ATT_BLOCK)
    xn0 = rmsnorm_fwd(xs, pass_on(0, "even", rest_token, even_norm_after_start), "even_norm")
    got = gathered(0, "even", xn0)
    w_in = got[0].reshape(N_CHIPS, 1, D_MODEL, EVEN_IN // N_CHIPS)
    w_out = got[1].reshape(1, 1, D_MODEL, D_MODEL)
    small = got[2].reshape(N_CHIPS, small_rows.shape[0], LANES)
    conv_w_full = small[:, 0:3].transpose(1, 0, 2).reshape(3, A_WIDTH)
    odd_norm_full = small[:, 8:10].reshape(1, D_MODEL)
    ffn_cw_full = small[:, 16:82].reshape(N_CHIPS, 2, 3, 2 * D_FF // N_CHIPS).transpose(1, 2, 0, 3).reshape(2, 3, 2 * D_FF)

    def ffn_fwd(l, xin, xn):
        up, u, act = up_glu_fwd(xn, w_up[l], ffn_cw_full[l], ffn_conv_b[l:l + 1], f"ffn{l}_up_glu")
        return act, (xin, xn, up, u, act)

    w_up, w_down = [None, None], [None, None]
    proj, mix = in_mixer_fwd(xn0, w_in, conv_w_full, pool_w, even_pool_scale, "even_in_mixer")
    x1, xn1 = mm_res_norm(mix, w_out, xs, ffn_norm[0:1], "even_out")
    pass_on(1, "up0", x1, x1)
    w_up[0] = gathered(1, "up0", x1)[0].reshape(N_CHIPS, 1, D_MODEL, 2 * D_FF // N_CHIPS)
    act0, ffn0 = ffn_fwd(0, x1, pass_on(2, "down0", x1, xn1))
    w_down[0] = gathered(2, "down0", act0)[0].reshape(1, 1, D_FF, D_MODEL)
    x2, xn2 = mm_res_norm(pass_on(3, "odd", act0, act0), w_down[0], x1, odd_norm_full, "ffn0_down")
    got = gathered(3, "odd", x2)
    xn2 = pass_on(5, "down1", x2, pass_on(4, "up1", x2, xn2))
    w_qkv = got[0].reshape(N_CHIPS, 1, D_MODEL, 3 * D_MODEL // N_CHIPS)
    w_o = got[1].reshape(1, 1, D_MODEL, D_MODEL)
    qkv, qkvn = qkv_qknorm_fwd(xn2, w_qkv, gqk, "odd_qkv_qknorm")
    att, lse = attn_fwd(qkvn, bias, "attn_fwd")
    x3, xn3 = mm_res_norm(att, w_o, x2, ffn_norm[1:2], "odd_out")
    w_up[1] = gathered(4, "up1", x3)[0].reshape(N_CHIPS, 1, D_MODEL, 2 * D_FF // N_CHIPS)
    act1, ffn1 = ffn_fwd(1, x3, xn3)
    w_down[1] = gathered(5, "down1", x3)[0].reshape(1, 1, D_FF, D_MODEL)
    dy, dyb, sq = mm_res_loss(act1, w_down[1], x3, target, "ffn1_down_loss")
    loss_part = (0.5 * jnp.sum(sq) * (1.0 / D_MODEL)).reshape(1, 1)

    def ffn_bwd(l, dy, dyb, saved):
        xin, xn, up, u, act = saved
        dw_down = mm_tn(act, dyb, f"ffn{l}_dw_down", J=1, tk=D_FF // 2, tm=1024)
        dact = mm_nt(dyb, w_down[l], f"ffn{l}_dact", tr=D_FF // 2, out_dtype=BF16, tm=1024)
        dup, dcw, dcb = glu_bwd(up, u, dact, ffn_cw_full[l], f"ffn{l}_glu_bwd")
        dw_up = mm_tn(xn, dup, f"ffn{l}_dw_up", J=N_CHIPS, tk=512, tm=1024, jb=2)
        dx, dxb, dg = mm_nt_norm_bwd(dup, w_up[l], xin, ffn_norm[l:l + 1], dy, f"ffn{l}_dx")
        return dx, dxb, (dw_down, dw_up, dcw, dcb, dg)

    def quarters(g):
        return g.reshape(N_CHIPS, 2, g.shape[0] * g.shape[1] // (2 * N_CHIPS), g.shape[-1])

    def reduce_start(grads, tag, then):
        sems, parts, zones = reduce_send([quarters(g) for g in grads], "reduce_send_" + tag)
        then, parts = lax.optimization_barrier((then, parts))
        return (sems, parts, zones), then

    dx3, dx3b, g_ffn1 = ffn_bwd(1, dy, dyb, ffn1)
    red_ffn1, (dx3, dx3b) = reduce_start([g_ffn1[1], g_ffn1[0]], "ffn1", (dx3, dx3b))
    dw_o = mm_tn(att, dx3b, "odd_dw_o", J=1, tk=512, tm=1024)
    datt = mm_nt(dx3b, w_o, "odd_datt", tr=D_MODEL, out_dtype=BF16)
    dq, dk, dv, dbias = attn_bwd(qkvn, att, datt, lse, bias, "attn_bwd")
    dqkv, dgqk = qknorm_bwd(qkv, dq, dk, dv, gqk, "odd_qknorm_bwd")
    dw_qkv = mm_tn(xn2, dqkv, "odd_dw_qkv", J=N_CHIPS, tk=512, tm=1024)
    red_odd, dqkv = reduce_start([dw_qkv, dw_o], "odd", dqkv)
    dx2, dx2b, dg_odd = mm_nt_norm_bwd(dqkv, w_qkv, x2, odd_norm_full, dx3, "odd_dx")
    dx1, dx1b, g_ffn0 = ffn_bwd(0, dx2, dx2b, ffn0)
    red_ffn0, (dx1, dx1b) = reduce_start([g_ffn0[1], g_ffn0[0]], "ffn0", (dx1, dx1b))
    dw_out = mm_tn(mix, dx1b, "even_dw_out", J=1, tk=512, tm=1024)
    dmix = mm_nt(dx1b, w_out, "even_dmix", tr=D_MODEL)
    dproj, dcw_even, dpw, dps = mixer_bwd(proj, dmix, conv_w_full, pool_w, even_pool_scale, "even_mixer_bwd")
    dw_in = mm_tn(xn0, dproj, "even_dw_in", J=N_CHIPS, tk=512, tm=1024)
    grad_x, _, dg_even = mm_nt_norm_bwd(dproj, w_in, xs, even_norm, dx1, "even_dx")
    d_rel = jnp.sum(bias_reduce(dbias.reshape(3, N_HEADS, 2 * ATT_BLOCK * ATT_BLOCK), "bias_reduce"), axis=0).T

    red_even, grad_x = reduce_start([dw_in, dw_out], "even", grad_x)

    dcw_sh = dcw_even.reshape(3, N_CHIPS, A_WIDTH // N_CHIPS).transpose(1, 0, 2)
    don_sh = dg_odd.reshape(N_CHIPS, D_MODEL // N_CHIPS)
    dfcw = jnp.stack([g_ffn0[2], g_ffn1[2]])
    dfcw_sh = dfcw.reshape(2, 3, N_CHIPS, 2 * D_FF // N_CHIPS).transpose(2, 0, 1, 3)
    rep_grads = [d_rel, dg_even, dpw[None], dps, _head_sum(dgqk[0]), _head_sum(dgqk[1]),
                 jnp.concatenate([g_ffn0[4], g_ffn1[4]], axis=0), jnp.concatenate([g_ffn0[3], g_ffn1[3]], axis=0)]
    rep_rows = _pack([loss_part] + rep_grads)
    n_loss = _n_rows(loss_part.shape)
    shard_rows = jnp.concatenate([_pack([dcw_sh[j], don_sh[j], dfcw_sh[j]]) for j in range(N_CHIPS)], axis=0)
    n_rep, n_shard = rep_rows.shape[0], shard_rows.shape[0] // N_CHIPS
    small_sems, small_rows, small_land = devices_start(jnp.concatenate([rep_rows, shard_rows], axis=0), "small_grads_start")
    grad_x, small_rows = lax.optimization_barrier((grad_x, small_rows))

    def reduce_end(red, tag, after):
        sems, parts, zones = red
        parts, zones = reduce_wait(parts, zones, sems, after, "reduce_wait_" + tag)
        return zones, parts

    z_ffn1, p_ffn1 = reduce_end(red_ffn1, "ffn1", grad_x)
    z_odd, p_odd = reduce_end(red_odd, "odd", grad_x)
    r_qkv = reduce_sum(z_odd[0], p_odd[0], place, "reduce_sum_w_qkv")
    r_o = reduce_sum(z_odd[1], p_odd[1], place, "reduce_sum_w_o")
    r_up = reduce_sum(z_ffn1[0], p_ffn1[0], place, "reduce_sum_w_up1", layer=1)
    r_down = reduce_sum(z_ffn1[1], p_ffn1[1], place, "reduce_sum_w_down1", layer=1)
    r_qkv, r_o, r_up, r_down = lax.optimization_barrier((r_qkv, r_o, r_up, r_down))
    z_ffn0, p_ffn0 = reduce_end(red_ffn0, "ffn0", r_down)
    r_up = reduce_sum(z_ffn0[0], p_ffn0[0], place, "reduce_sum_w_up0", into=r_up, layer=0)
    r_down = reduce_sum(z_ffn0[1], p_ffn0[1], place, "reduce_sum_w_down0", into=r_down, layer=0)
    later = ["odd_w_qkv", "odd_w_o", "ffn_w_up", "ffn_w_down"]
    joined = join_halves([r_qkv, r_o, r_up, r_down], "grads_join_late_layers")
    G = {nm: g.reshape(W[nm].shape) for nm, g in zip(later, joined)}

    D_, NM, NV = {}, {}, {}

    def update(nm):
        as3 = lambda a: a.reshape((-1,) + a.shape[-2:])
        outs = adamw(as3(W[nm]), as3(G[nm]), as3(M1[nm]), as3(M2[nm]), "adamw_" + nm)
        D_[nm], NM[nm], NV[nm], G[nm] = [o.reshape(W[nm].shape) for o in outs]

    def all_before(names):
        tied = lax.optimization_barrier([D_[nm] for nm in names])
        for nm, d in zip(names, tied):
            D_[nm] = d
        return tied[0]

    for nm in later:
        update(nm)
    z_even, p_even = reduce_end(red_even, "even", all_before(later))
    joined = join_halves([reduce_sum(z_even[0], p_even[0], place, "reduce_sum_w_in"),
                          reduce_sum(z_even[1], p_even[1], place, "reduce_sum_w_out")], "grads_join_first_layer")
    first = ["even_w_in", "even_w_out"]
    for nm, g in zip(first, joined):
        G[nm] = g.reshape(W[nm].shape)
        update(nm)
    small_rows, small_land = devices_wait(small_rows, small_land, small_sems, all_before(first), "small_grads_wait")
    small_sum = device_sum(small_land, small_rows, place[2:3], "small_grads_sum")
    mine = lax.dynamic_slice_in_dim(small_sum, n_rep + chip * n_shard, n_shard, axis=0)
    loss = small_sum[0, 0]
    g_small = jnp.concatenate([small_sum[n_loss:n_rep], mine], axis=0)
    small_names = [n for n, _ in REPLICATED_SMALL + SHARDED_SMALL]
    small_shapes = [s for _, s in REPLICATED_SMALL + SHARDED_SMALL]
    G.update(dict(zip(small_names, _unpack(g_small, small_shapes))))
    outs = adamw_small(*[[d[n] for n in small_names] for d in (W, G, M1, M2)], "adamw_small")
    for dst, o in zip((D_, NM, NV), outs):
        dst.update(dict(zip(small_names, o)))

    return (loss, grad_x[None], *[G[n] for n in WEIGHT_ORDER], *[D_[n] for n in WEIGHT_ORDER],
            *[NM[n] for n in WEIGHT_ORDER], *[NV[n] for n in WEIGHT_ORDER])


def _head_sum(dg):
    return jnp.sum(dg.reshape(N_HEADS, HEAD_DIM), axis=0, keepdims=True)
```
